```python
import math
import jax, jax.numpy as jnp
from jax import lax
import numpy as np

D_MODEL = 1024
BATCH = 8
SEQ = 2048
DEPTH = 2

HEAD_DIM = 64
BLK = 128
A_GROUPS = ((128, 1), (512, 4), (2048, 16))
A_HEADS = 4
B_Q_HEADS = 8
B_KV_HEADS = 2
B_WINDOW = 128
C_HEADS = 4
N_BRANCH = 3
NUM_BUCKETS = 32
MAX_DISTANCE = max(w for w, _ in A_GROUPS)
D_FF = 4 * D_MODEL
CONV_WIDTH = 3
EPS = 1e-6

A_WIDTH = A_HEADS * HEAD_DIM
B_WIDTH = B_Q_HEADS * HEAD_DIM
C_WIDTH = C_HEADS * HEAD_DIM
B_GROUP = B_Q_HEADS // B_KV_HEADS
N_A_GROUP_HEADS = len(A_GROUPS) * A_HEADS
N_BIAS_HEADS = N_A_GROUP_HEADS + B_Q_HEADS
A_QKV_COLS = 3 * N_A_GROUP_HEADS * HEAD_DIM
B_Q_COLS = B_WIDTH
B_KV_COLS = 2 * B_KV_HEADS * HEAD_DIM
C_QKV_COLS = 3 * C_WIDTH
GATE_COLS = N_BRANCH * D_MODEL
OFF_B_Q = A_QKV_COLS
OFF_B_KV = OFF_B_Q + B_Q_COLS
OFF_C = OFF_B_KV + B_KV_COLS
OFF_GATE = OFF_C + C_QKV_COLS
IN_COLS = OFF_GATE + GATE_COLS
SCALE = HEAD_DIM ** -0.5

kernel_name = 'hybrid_dilated_swa_stickbreak_convffn'


def rms_norm(x, g):
    xf = x.astype(jnp.float32)
    y = xf * lax.rsqrt(jnp.mean(xf * xf, axis=-1, keepdims=True) + EPS)
    return (y * g.astype(jnp.float32)).astype(x.dtype)


def t5_bucket(dist):
    max_exact = NUM_BUCKETS // 2
    nf = jnp.maximum(dist, 1).astype(jnp.float32)
    large = max_exact + (jnp.log(nf / max_exact) / math.log(MAX_DISTANCE / max_exact)
                         * (NUM_BUCKETS - max_exact)).astype(jnp.int32)
    large = jnp.minimum(large, NUM_BUCKETS - 1)
    return jnp.where(dist < max_exact, dist, large)


def band_layout(n_blocks, max_dist):
    a = jnp.arange(BLK)[:, None]
    b = jnp.arange(2 * BLK)[None, :]
    dist = a + BLK - b
    in_band = (dist >= 0) & (dist <= max_dist)
    key_exists = (jnp.arange(n_blocks)[:, None] > 0) | (jnp.arange(2 * BLK)[None, :] >= BLK)
    mask = in_band[None] & key_exists[:, None, :]
    return jnp.maximum(dist, 0), mask


def band_bias(table, dist):
    return jnp.transpose(table[t5_bucket(dist)], (2, 0, 1)).astype(jnp.float32)


def banded_attention(q, k, v, bias, mask):
    n, hkv, g, L, dh = q.shape
    nb = L // BLK
    qb = q.reshape(n, hkv, g, nb, BLK, dh)

    def two_blocks(t):
        tb = t.reshape(n, hkv, nb, BLK, dh)
        prev = jnp.pad(tb, ((0, 0), (0, 0), (1, 0), (0, 0), (0, 0)))[:, :, :nb]
        return jnp.concatenate([prev, tb], axis=3)

    kw, vw = two_blocks(k), two_blocks(v)
    logits = jnp.einsum('nkgbqd,nkbsd->nkgbqs', qb, kw, preferred_element_type=jnp.float32) * SCALE
    logits = jnp.where(mask, logits + bias[:, :, None], -jnp.inf)
    m = jnp.max(logits, axis=-1)
    p = jnp.exp(logits - m[..., None])
    l = jnp.sum(p, axis=-1)
    num = jnp.einsum('nkgbqs,nkbsd->nkgbqd', p, vw.astype(jnp.float32))
    return num.reshape(n, hkv, g, L, dh), m.reshape(n, hkv, g, L), l.reshape(n, hkv, g, L)


def to_sub(t, d, lp):
    b, s, h, dh = t.shape
    L = s // d
    t = t.reshape(b, L, d, h, dh).transpose(0, 2, 3, 1, 4).reshape(b * d, h, L, dh)
    return jnp.pad(t, ((0, 0), (0, 0), (0, lp - L), (0, 0)))


def dilated_attention(q, k, v, table_a):
    b, s = q.shape[:2]
    nums, ms, ls = [], [], []
    for gi, (window, d) in enumerate(A_GROUPS):
        L = s // d
        lp = -(-L // BLK) * BLK
        dist, mask = band_layout(lp // BLK, window // d)
        bias = band_bias(table_a[:, gi], dist * d)[:, None]
        num, m, l = banded_attention(to_sub(q[:, :, gi], d, lp)[:, :, None],
                                     to_sub(k[:, :, gi], d, lp), to_sub(v[:, :, gi], d, lp), bias, mask)
        num = num[:, :, 0, :L].reshape(b, d, A_HEADS, L, HEAD_DIM).transpose(0, 3, 1, 2, 4)
        nums.append(num.reshape(b, s, A_HEADS, HEAD_DIM))
        ms.append(m[:, :, 0, :L].reshape(b, d, A_HEADS, L).transpose(0, 3, 1, 2).reshape(b, s, A_HEADS))
        ls.append(l[:, :, 0, :L].reshape(b, d, A_HEADS, L).transpose(0, 3, 1, 2).reshape(b, s, A_HEADS))
    nums = jnp.stack(nums, axis=2)
    ms = jnp.stack(ms, axis=2)
    ls = jnp.stack(ls, axis=2)
    c = jnp.exp(ms - jnp.max(ms, axis=2, keepdims=True))
    out = jnp.sum(c[..., None] * nums, axis=2) / jnp.sum(c * ls, axis=2)[..., None]
    return out.reshape(b, s, A_WIDTH).astype(q.dtype)


def sliding_window_gqa(q, k, v, sinks, table_b):
    b, s = q.shape[:2]
    dist, mask = band_layout(s // BLK, B_WINDOW - 1)
    bias = band_bias(table_b, dist).reshape(B_KV_HEADS, B_GROUP, BLK, 2 * BLK)
    qh = q.reshape(b, s, B_KV_HEADS, B_GROUP, HEAD_DIM).transpose(0, 2, 3, 1, 4)
    num, m, l = banded_attention(qh, k.transpose(0, 2, 1, 3), v.transpose(0, 2, 1, 3), bias, mask)
    sink = sinks.reshape(B_KV_HEADS, B_GROUP)[None, :, :, None].astype(jnp.float32)
    mx = jnp.maximum(m, sink)
    c = jnp.exp(m - mx)
    out = num * (c / (l * c + jnp.exp(sink - mx)))[..., None]
    return out.transpose(0, 3, 1, 2, 4).reshape(b, s, B_WIDTH).astype(q.dtype)


def stick_breaking_attention(q, k, v):
    b, s = q.shape[:2]
    qh, kh, vh = (t.transpose(0, 2, 1, 3) for t in (q, k, v))
    outs = []
    for i in range(s // BLK):
        lo, hi = i * BLK, (i + 1) * BLK
        z = jnp.einsum('bhqd,bhsd->bhqs', qh[:, :, lo:hi], kh[:, :, :hi],
                       preferred_element_type=jnp.float32) * SCALE
        before = jnp.arange(hi)[None, :] < jnp.arange(lo, hi)[:, None]
        log_keep = jnp.where(before, jax.nn.log_sigmoid(-z), 0.0)
        log_rest = lax.cumsum(log_keep, axis=3, reverse=True) - log_keep
        wts = jnp.where(before, jnp.exp(jax.nn.log_sigmoid(z) + log_rest), 0.0)
        outs.append(jnp.einsum('bhqs,bhsd->bhqd', wts, vh[:, :, :hi].astype(jnp.float32)))
    out = jnp.concatenate(outs, axis=2).transpose(0, 2, 1, 3)
    return out.reshape(b, s, C_WIDTH).astype(q.dtype)


def hybrid_mixer(h, rel_bias, w_in, b_gate, sinks, w_br_a, w_br_b, w_br_c, w_out):
    b, s, _ = h.shape
    proj = h @ w_in
    a_qkv, b_q, b_kv, c_qkv, gates = jnp.split(proj, [OFF_B_Q, OFF_B_KV, OFF_C, OFF_GATE], axis=-1)
    a_qkv = a_qkv.reshape(b, s, 3, len(A_GROUPS), A_HEADS, HEAD_DIM)
    table_a = rel_bias[:, :N_A_GROUP_HEADS].reshape(NUM_BUCKETS, len(A_GROUPS), A_HEADS)
    o_a = dilated_attention(a_qkv[:, :, 0], a_qkv[:, :, 1], a_qkv[:, :, 2], table_a)
    b_kv = b_kv.reshape(b, s, 2, B_KV_HEADS, HEAD_DIM)
    o_b = sliding_window_gqa(b_q.reshape(b, s, B_Q_HEADS, HEAD_DIM), b_kv[:, :, 0], b_kv[:, :, 1],
                             sinks, rel_bias[:, N_A_GROUP_HEADS:])
    c_qkv = c_qkv.reshape(b, s, 3, C_HEADS, HEAD_DIM)
    o_c = stick_breaking_attention(c_qkv[:, :, 0], c_qkv[:, :, 1], c_qkv[:, :, 2])
    g = jax.nn.sigmoid(gates.reshape(b, s, N_BRANCH, D_MODEL) + b_gate)
    merged = g[:, :, 0] * (o_a @ w_br_a) + g[:, :, 1] * (o_b @ w_br_b) + g[:, :, 2] * (o_c @ w_br_c)
    return merged @ w_out


def conv_ffn(h, w_up, conv_w, conv_b, w_down):
    u = h @ w_up
    u = lax.conv_general_dilated(u, conv_w[:, None, :], window_strides=(1,),
                                 padding=[(CONV_WIDTH - 1, 0)],
                                 dimension_numbers=('NWC', 'WIO', 'NWC'),
                                 feature_group_count=u.shape[-1]) + conv_b
    gate, val = jnp.split(u, 2, axis=-1)
    return (jax.nn.gelu(gate, approximate=True) * val) @ w_down


def _fwd_setup_inputs(seed: int = 0) -> dict:
    key = jax.random.key(seed)
    ks = jax.random.split(key, 17)

    def nrm(k, shape, scale):
        return jax.random.normal(k, shape, jnp.float32) * scale

    def gain(k):
        return 1.0 + nrm(k, (DEPTH, D_MODEL), 0.1)

    return {
        'x': nrm(ks[0], (BATCH, SEQ, D_MODEL), 1.0),
        'rel_bias': nrm(ks[1], (NUM_BUCKETS, N_BIAS_HEADS), 0.5),
        'attn_pre_norm': gain(ks[2]),
        'w_in': nrm(ks[3], (DEPTH, D_MODEL, IN_COLS), D_MODEL ** -0.5),
        'b_gate': nrm(ks[4], (DEPTH, N_BRANCH, D_MODEL), 0.1),
        'sinks': nrm(ks[5], (DEPTH, B_Q_HEADS), 0.5),
        'w_br_a': nrm(ks[6], (DEPTH, A_WIDTH, D_MODEL), A_WIDTH ** -0.5),
        'w_br_b': nrm(ks[7], (DEPTH, B_WIDTH, D_MODEL), B_WIDTH ** -0.5),
        'w_br_c': nrm(ks[8], (DEPTH, C_WIDTH, D_MODEL), C_WIDTH ** -0.5),
        'w_out': nrm(ks[9], (DEPTH, D_MODEL, D_MODEL), D_MODEL ** -0.5),
        'attn_post_norm': gain(ks[10]),
        'ffn_pre_norm': gain(ks[11]),
        'w_up': nrm(ks[12], (DEPTH, D_MODEL, 2 * D_FF), D_MODEL ** -0.5),
        'conv_w': nrm(ks[13], (DEPTH, CONV_WIDTH, 2 * D_FF), CONV_WIDTH ** -0.5),
        'conv_b': nrm(ks[14], (DEPTH, 2 * D_FF), 0.02),
        'w_down': nrm(ks[15], (DEPTH, D_FF, D_MODEL), D_FF ** -0.5),
        'ffn_post_norm': gain(ks[16]),
    }


def _fwd_reference(x, rel_bias, attn_pre_norm, w_in, b_gate, sinks, w_br_a, w_br_b, w_br_c, w_out,
              attn_post_norm, ffn_pre_norm, w_up, conv_w, conv_b, w_down, ffn_post_norm):
    for layer in range(DEPTH):
        h = rms_norm(x, attn_pre_norm[layer])
        h = hybrid_mixer(h, rel_bias, w_in[layer], b_gate[layer], sinks[layer],
                         w_br_a[layer], w_br_b[layer], w_br_c[layer], w_out[layer])
        x = x + rms_norm(h, attn_post_norm[layer])
        h = rms_norm(x, ffn_pre_norm[layer])
        h = conv_ffn(h, w_up[layer], conv_w[layer], conv_b[layer], w_down[layer])
        x = x + rms_norm(h, ffn_post_norm[layer])
    return x


import jax as _jax
import jax.numpy as _jnp

TWIN_FORMAT = 'train_step'
FWD_PARAMS = ['x', 'rel_bias', 'attn_pre_norm', 'w_in', 'b_gate', 'sinks', 'w_br_a', 'w_br_b', 'w_br_c', 'w_out', 'attn_post_norm', 'ffn_pre_norm', 'w_up', 'conv_w', 'conv_b', 'w_down', 'ffn_post_norm']
TWIN_WEIGHTS = ['rel_bias', 'attn_pre_norm', 'w_in', 'b_gate', 'sinks', 'w_br_a', 'w_br_b', 'w_br_c', 'w_out', 'attn_post_norm', 'ffn_pre_norm', 'w_up', 'conv_w', 'conv_b', 'w_down', 'ffn_post_norm']
TWIN_DIFF_INPUT = 'x'
TWIN_INPUTS = ['x', 'rel_bias', 'attn_pre_norm', 'w_in', 'b_gate', 'sinks', 'w_br_a', 'w_br_b', 'w_br_c', 'w_out', 'attn_post_norm', 'ffn_pre_norm', 'w_up', 'conv_w', 'conv_b', 'w_down', 'ffn_post_norm', 'loss_target', 'm_rel_bias', 'm_attn_pre_norm', 'm_w_in', 'm_b_gate', 'm_sinks', 'm_w_br_a', 'm_w_br_b', 'm_w_br_c', 'm_w_out', 'm_attn_post_norm', 'm_ffn_pre_norm', 'm_w_up', 'm_conv_w', 'm_conv_b', 'm_w_down', 'm_ffn_post_norm', 'v_rel_bias', 'v_attn_pre_norm', 'v_w_in', 'v_b_gate', 'v_sinks', 'v_w_br_a', 'v_w_br_b', 'v_w_br_c', 'v_w_out', 'v_attn_post_norm', 'v_ffn_pre_norm', 'v_w_up', 'v_conv_w', 'v_conv_b', 'v_w_down', 'v_ffn_post_norm']
TWIN_OUTPUTS = ['loss', 'grad_x', 'grad_rel_bias', 'grad_attn_pre_norm', 'grad_w_in', 'grad_b_gate', 'grad_sinks', 'grad_w_br_a', 'grad_w_br_b', 'grad_w_br_c', 'grad_w_out', 'grad_attn_post_norm', 'grad_ffn_pre_norm', 'grad_w_up', 'grad_conv_w', 'grad_conv_b', 'grad_w_down', 'grad_ffn_post_norm', 'delta_rel_bias', 'delta_attn_pre_norm', 'delta_w_in', 'delta_b_gate', 'delta_sinks', 'delta_w_br_a', 'delta_w_br_b', 'delta_w_br_c', 'delta_w_out', 'delta_attn_post_norm', 'delta_ffn_pre_norm', 'delta_w_up', 'delta_conv_w', 'delta_conv_b', 'delta_w_down', 'delta_ffn_post_norm', 'new_m_rel_bias', 'new_m_attn_pre_norm', 'new_m_w_in', 'new_m_b_gate', 'new_m_sinks', 'new_m_w_br_a', 'new_m_w_br_b', 'new_m_w_br_c', 'new_m_w_out', 'new_m_attn_post_norm', 'new_m_ffn_pre_norm', 'new_m_w_up', 'new_m_conv_w', 'new_m_conv_b', 'new_m_w_down', 'new_m_ffn_post_norm', 'new_v_rel_bias', 'new_v_attn_pre_norm', 'new_v_w_in', 'new_v_b_gate', 'new_v_sinks', 'new_v_w_br_a', 'new_v_w_br_b', 'new_v_w_br_c', 'new_v_w_out', 'new_v_attn_post_norm', 'new_v_ffn_pre_norm', 'new_v_w_up', 'new_v_conv_w', 'new_v_conv_b', 'new_v_w_down', 'new_v_ffn_post_norm']
TWIN_LEAF_KINDS = {'loss': 'loss', 'grad_x': 'grad_x', 'grad_rel_bias': 'grad_w', 'grad_attn_pre_norm': 'grad_w', 'grad_w_in': 'grad_w', 'grad_b_gate': 'grad_w', 'grad_sinks': 'grad_w', 'grad_w_br_a': 'grad_w', 'grad_w_br_b': 'grad_w', 'grad_w_br_c': 'grad_w', 'grad_w_out': 'grad_w', 'grad_attn_post_norm': 'grad_w', 'grad_ffn_pre_norm': 'grad_w', 'grad_w_up': 'grad_w', 'grad_conv_w': 'grad_w', 'grad_conv_b': 'grad_w', 'grad_w_down': 'grad_w', 'grad_ffn_post_norm': 'grad_w', 'delta_rel_bias': 'delta_w', 'delta_attn_pre_norm': 'delta_w', 'delta_w_in': 'delta_w', 'delta_b_gate': 'delta_w', 'delta_sinks': 'delta_w', 'delta_w_br_a': 'delta_w', 'delta_w_br_b': 'delta_w', 'delta_w_br_c': 'delta_w', 'delta_w_out': 'delta_w', 'delta_attn_post_norm': 'delta_w', 'delta_ffn_pre_norm': 'delta_w', 'delta_w_up': 'delta_w', 'delta_conv_w': 'delta_w', 'delta_conv_b': 'delta_w', 'delta_w_down': 'delta_w', 'delta_ffn_post_norm': 'delta_w', 'new_m_rel_bias': 'new_m', 'new_m_attn_pre_norm': 'new_m', 'new_m_w_in': 'new_m', 'new_m_b_gate': 'new_m', 'new_m_sinks': 'new_m', 'new_m_w_br_a': 'new_m', 'new_m_w_br_b': 'new_m', 'new_m_w_br_c': 'new_m', 'new_m_w_out': 'new_m', 'new_m_attn_post_norm': 'new_m', 'new_m_ffn_pre_norm': 'new_m', 'new_m_w_up': 'new_m', 'new_m_conv_w': 'new_m', 'new_m_conv_b': 'new_m', 'new_m_w_down': 'new_m', 'new_m_ffn_post_norm': 'new_m', 'new_v_rel_bias': 'new_v', 'new_v_attn_pre_norm': 'new_v', 'new_v_w_in': 'new_v', 'new_v_b_gate': 'new_v', 'new_v_sinks': 'new_v', 'new_v_w_br_a': 'new_v', 'new_v_w_br_b': 'new_v', 'new_v_w_br_c': 'new_v', 'new_v_w_out': 'new_v', 'new_v_attn_post_norm': 'new_v', 'new_v_ffn_pre_norm': 'new_v', 'new_v_w_up': 'new_v', 'new_v_conv_w': 'new_v', 'new_v_conv_b': 'new_v', 'new_v_w_down': 'new_v', 'new_v_ffn_post_norm': 'new_v'}


def _forward(args):
    return _fwd_reference(*[args[k] for k in FWD_PARAMS])


def _output_shape():
    out = _jax.eval_shape(lambda: _forward(_fwd_setup_inputs(0)))
    return out.shape, out.dtype

N_MICROBATCH = 1
ADAM_LR = 0.001
ADAM_B1 = 0.9
ADAM_B2 = 0.999
ADAM_EPS = 1e-08
ADAM_WD = 0.01
ADAM_STEP = 10
PER_EXAMPLE_BATCH_AXIS = {'x': 0, 'loss_target': 0}
SHARED_INPUTS = []
_WEIGHT_DTYPES = {'rel_bias': _jnp.float32, 'attn_pre_norm': _jnp.float32, 'w_in': _jnp.float32, 'b_gate': _jnp.float32, 'sinks': _jnp.float32, 'w_br_a': _jnp.float32, 'w_br_b': _jnp.float32, 'w_br_c': _jnp.float32, 'w_out': _jnp.float32, 'attn_post_norm': _jnp.float32, 'ffn_pre_norm': _jnp.float32, 'w_up': _jnp.float32, 'conv_w': _jnp.float32, 'conv_b': _jnp.float32, 'w_down': _jnp.float32, 'ffn_post_norm': _jnp.float32}
MOMENT_SCALE = {'rel_bias': 2.947129e-01, 'attn_pre_norm': 7.768819e-01, 'w_in': 2.706531e-01, 'b_gate': 1.481802e-01, 'sinks': 2.242074e-01, 'w_br_a': 1.660472e-01, 'w_br_b': 1.895409e-01, 'w_br_c': 6.210564e-01, 'w_out': 6.397578e-01, 'attn_post_norm': 1.614083e+01, 'ffn_pre_norm': 5.842037e-01, 'w_up': 2.096110e-01, 'conv_w': 2.231845e-01, 'conv_b': 3.246133e-01, 'w_down': 4.625616e-01, 'ffn_post_norm': 1.607547e+01}


def _to_microbatches(a, axis):
    t = _jnp.moveaxis(a, axis, 0)
    t = t.reshape((N_MICROBATCH, t.shape[0] // N_MICROBATCH) + t.shape[1:])
    return _jnp.moveaxis(t, 1, axis + 1)


def setup_inputs(seed: int = 0) -> dict:
    inp = _fwd_setup_inputs(seed)
    key = _jax.random.fold_in(_jax.random.key(seed), 7919)
    shape, _ = _output_shape()
    out = dict(inp)
    out["loss_target"] = _jax.random.normal(_jax.random.fold_in(key, 0), shape, _jnp.float32)
    for i, name in enumerate(TWIN_WEIGHTS):
        w = inp[name].astype(_jnp.float32)
        if MOMENT_SCALE is None:
            s = _jnp.sqrt(_jnp.mean(_jnp.square(w)) + 1e-30)
        else:
            s = MOMENT_SCALE[name]
        km, kv = _jax.random.split(_jax.random.fold_in(key, i + 1))
        out[name] = w
        out["m_" + name] = s * _jax.random.normal(km, w.shape, _jnp.float32)
        out["v_" + name] = (s * s) * _jax.random.uniform(kv, w.shape, _jnp.float32, 0.5, 1.5)
    if N_MICROBATCH > 1:
        for name, axis in PER_EXAMPLE_BATCH_AXIS.items():
            out[name] = _to_microbatches(out[name], axis)
    return {'x': out['x'], 'rel_bias': out['rel_bias'], 'attn_pre_norm': out['attn_pre_norm'], 'w_in': out['w_in'], 'b_gate': out['b_gate'], 'sinks': out['sinks'], 'w_br_a': out['w_br_a'], 'w_br_b': out['w_br_b'], 'w_br_c': out['w_br_c'], 'w_out': out['w_out'], 'attn_post_norm': out['attn_post_norm'], 'ffn_pre_norm': out['ffn_pre_norm'], 'w_up': out['w_up'], 'conv_w': out['conv_w'], 'conv_b': out['conv_b'], 'w_down': out['w_down'], 'ffn_post_norm': out['ffn_post_norm'], 'loss_target': out['loss_target'], 'm_rel_bias': out['m_rel_bias'], 'm_attn_pre_norm': out['m_attn_pre_norm'], 'm_w_in': out['m_w_in'], 'm_b_gate': out['m_b_gate'], 'm_sinks': out['m_sinks'], 'm_w_br_a': out['m_w_br_a'], 'm_w_br_b': out['m_w_br_b'], 'm_w_br_c': out['m_w_br_c'], 'm_w_out': out['m_w_out'], 'm_attn_post_norm': out['m_attn_post_norm'], 'm_ffn_pre_norm': out['m_ffn_pre_norm'], 'm_w_up': out['m_w_up'], 'm_conv_w': out['m_conv_w'], 'm_conv_b': out['m_conv_b'], 'm_w_down': out['m_w_down'], 'm_ffn_post_norm': out['m_ffn_post_norm'], 'v_rel_bias': out['v_rel_bias'], 'v_attn_pre_norm': out['v_attn_pre_norm'], 'v_w_in': out['v_w_in'], 'v_b_gate': out['v_b_gate'], 'v_sinks': out['v_sinks'], 'v_w_br_a': out['v_w_br_a'], 'v_w_br_b': out['v_w_br_b'], 'v_w_br_c': out['v_w_br_c'], 'v_w_out': out['v_w_out'], 'v_attn_post_norm': out['v_attn_post_norm'], 'v_ffn_pre_norm': out['v_ffn_pre_norm'], 'v_w_up': out['v_w_up'], 'v_conv_w': out['v_conv_w'], 'v_conv_b': out['v_conv_b'], 'v_w_down': out['v_w_down'], 'v_ffn_post_norm': out['v_ffn_post_norm']}


def _loss(weights, diff, rest, loss_target):
    with _jax.named_scope("forward"):
        args = {**rest, TWIN_DIFF_INPUT: diff, **{k: w.astype(_WEIGHT_DTYPES[k]) for k, w in weights.items()}}
        y = _forward(args)
    with _jax.named_scope("loss_head"):
        err = _jnp.square(y.astype(_jnp.float32) - loss_target)
        return 0.5 * _jnp.sum(_jnp.mean(err, axis=-1)) if err.ndim else 0.5 * err


def _adamw(w, g, m, v):
    m = ADAM_B1 * m + (1.0 - ADAM_B1) * g
    v = ADAM_B2 * v + (1.0 - ADAM_B2) * _jnp.square(g)
    m_hat = m / (1.0 - ADAM_B1 ** ADAM_STEP)
    v_hat = v / (1.0 - ADAM_B2 ** ADAM_STEP)
    delta = -ADAM_LR * (m_hat / (_jnp.sqrt(v_hat) + ADAM_EPS) + ADAM_WD * w)
    return delta, m, v


def reference(x, rel_bias, attn_pre_norm, w_in, b_gate, sinks, w_br_a, w_br_b, w_br_c, w_out, attn_post_norm, ffn_pre_norm, w_up, conv_w, conv_b, w_down, ffn_post_norm, loss_target, m_rel_bias, m_attn_pre_norm, m_w_in, m_b_gate, m_sinks, m_w_br_a, m_w_br_b, m_w_br_c, m_w_out, m_attn_post_norm, m_ffn_pre_norm, m_w_up, m_conv_w, m_conv_b, m_w_down, m_ffn_post_norm, v_rel_bias, v_attn_pre_norm, v_w_in, v_b_gate, v_sinks, v_w_br_a, v_w_br_b, v_w_br_c, v_w_out, v_attn_post_norm, v_ffn_pre_norm, v_w_up, v_conv_w, v_conv_b, v_w_down, v_ffn_post_norm):
    given = dict(x=x, rel_bias=rel_bias, attn_pre_norm=attn_pre_norm, w_in=w_in, b_gate=b_gate, sinks=sinks, w_br_a=w_br_a, w_br_b=w_br_b, w_br_c=w_br_c, w_out=w_out, attn_post_norm=attn_post_norm, ffn_pre_norm=ffn_pre_norm, w_up=w_up, conv_w=conv_w, conv_b=conv_b, w_down=w_down, ffn_post_norm=ffn_post_norm, loss_target=loss_target, m_rel_bias=m_rel_bias, m_attn_pre_norm=m_attn_pre_norm, m_w_in=m_w_in, m_b_gate=m_b_gate, m_sinks=m_sinks, m_w_br_a=m_w_br_a, m_w_br_b=m_w_br_b, m_w_br_c=m_w_br_c, m_w_out=m_w_out, m_attn_post_norm=m_attn_post_norm, m_ffn_pre_norm=m_ffn_pre_norm, m_w_up=m_w_up, m_conv_w=m_conv_w, m_conv_b=m_conv_b, m_w_down=m_w_down, m_ffn_post_norm=m_ffn_post_norm, v_rel_bias=v_rel_bias, v_attn_pre_norm=v_attn_pre_norm, v_w_in=v_w_in, v_b_gate=v_b_gate, v_sinks=v_sinks, v_w_br_a=v_w_br_a, v_w_br_b=v_w_br_b, v_w_br_c=v_w_br_c, v_w_out=v_w_out, v_attn_post_norm=v_attn_post_norm, v_ffn_pre_norm=v_ffn_pre_norm, v_w_up=v_w_up, v_conv_w=v_conv_w, v_conv_b=v_conv_b, v_w_down=v_w_down, v_ffn_post_norm=v_ffn_post_norm)
    weights = {n: given[n] for n in TWIN_WEIGHTS}
    shared = {n: given[n] for n in SHARED_INPUTS}
    per_example = {n: given[n] for n in ['x']}
    grad_fn = _jax.value_and_grad(_loss, argnums=(0, 1))

    def one_microbatch(ex, loss_target):
        ex = dict(ex)
        diff = ex.pop(TWIN_DIFF_INPUT)
        return grad_fn(weights, diff, {**shared, **ex}, loss_target)

    if N_MICROBATCH == 1:
        loss, (grad_w, grad_x) = one_microbatch(per_example, given["loss_target"])
    else:
        def body(carry, xs):
            loss_sum, grad_sum = carry
            l_k, (gw_k, gx_k) = one_microbatch(xs[0], xs[1])
            with _jax.named_scope("update"):
                return (loss_sum + l_k, _jax.tree.map(_jnp.add, grad_sum, gw_k)), gx_k

        init = (_jnp.zeros((), _jnp.float32), _jax.tree.map(_jnp.zeros_like, weights))
        (loss, grad_w), grad_x = _jax.lax.scan(body, init, (per_example, given["loss_target"]))
    with _jax.named_scope("update"):
        delta_w, new_m, new_v = {}, {}, {}
        for n in TWIN_WEIGHTS:
            delta_w[n], new_m[n], new_v[n] = _adamw(weights[n], grad_w[n], given["m_" + n], given["v_" + n])
    return (loss, grad_x, *[grad_w[n] for n in TWIN_WEIGHTS], *[delta_w[n] for n in TWIN_WEIGHTS],
            *[new_m[n] for n in TWIN_WEIGHTS], *[new_v[n] for n in TWIN_WEIGHTS])
```

```python
import functools
import math

import numpy as np
import jax
import jax.numpy as jnp
from jax import lax
from jax.experimental import pallas as pl
from jax.experimental.pallas import tpu as pltpu

F32 = jnp.float32
BF16 = jnp.bfloat16

S = 2048
D = 1024
DEPTH = 2
NDEV = 8
HD = 64
BLK = 128
NB = S // BLK
A_GROUPS = ((128, 1), (512, 4), (2048, 16))
NUM_BUCKETS = 32
MAX_DISTANCE = 2048
N_BIAS_HEADS = 20
D_FF = 4096
IN_COLS = 6912
QKV_COLS = 3840
QKV_SLABS = QKV_COLS // 128
GATE_COLS = 3072
EPS = 1e-6
SCALE = HD ** -0.5
NEG = -1e30
LANES = 128

ADAM_LR = 0.001
ADAM_B1 = 0.9
ADAM_B2 = 0.999
ADAM_EPS = 1e-08
ADAM_WD = 0.01
ADAM_STEP = 10

VMEM_LIMIT = 56 * 1024 * 1024
MESH = pl.DeviceIdType.MESH
ANY = pl.BlockSpec(memory_space=pl.ANY)
SMEM = pl.BlockSpec(memory_space=pltpu.SMEM)


def _cp(*sem):
    return pltpu.CompilerParams(dimension_semantics=sem if sem else None, vmem_limit_bytes=VMEM_LIMIT)


def _dot(a, b, ca, cb):
    return lax.dot_general(a, b, (((ca,), (cb,)), ((), ())), preferred_element_type=F32)


def _mm(a, b, *, grid, a_spec, b_spec, out_shape, out_spec, ca, cb, acc_shape, name,
        a_slab=False, b_slab=False, out_slab=False, alias_out=None):
    nk = grid[2]

    def body(*refs):
        if alias_out is not None:
            a_ref, b_ref, _, o_ref, acc_ref = refs
        else:
            a_ref, b_ref, o_ref, acc_ref = refs
        k = pl.program_id(2)

        @pl.when(k == 0)
        def _():
            acc_ref[...] = jnp.zeros(acc_shape, F32)

        def load(ref, slab):
            if slab:
                return jnp.concatenate([ref[s] for s in range(ref.shape[0])], axis=1).astype(BF16)
            return ref[...].astype(BF16)

        acc_ref[...] += _dot(load(a_ref, a_slab), load(b_ref, b_slab), ca, cb)

        @pl.when(k == nk - 1)
        def _():
            if out_slab:
                for s in range(o_ref.shape[0]):
                    o_ref[s] = acc_ref[:, s * LANES:(s + 1) * LANES].astype(o_ref.dtype)
            else:
                o_ref[...] = acc_ref[...].astype(o_ref.dtype)

    in_specs = [a_spec, b_spec]
    args = [a, b]
    aliases = {}
    if alias_out is not None:
        in_specs.append(ANY)
        args.append(alias_out)
        aliases = {2: 0}
    return pl.pallas_call(
        body, grid=grid, in_specs=in_specs, out_specs=out_spec, out_shape=out_shape,
        scratch_shapes=[pltpu.VMEM(acc_shape, F32)], input_output_aliases=aliases,
        compiler_params=_cp("parallel", "parallel", "arbitrary"), name=name)(*args)


def _mm_nn(a, b, out_dtype, tm, tn, tk, name):
    m, kk = a.shape
    n = b.shape[1]
    return _mm(a, b, grid=(m // tm, n // tn, kk // tk),
               a_spec=pl.BlockSpec((tm, tk), lambda i, j, k: (i, k)),
               b_spec=pl.BlockSpec((tk, tn), lambda i, j, k: (k, j)),
               out_shape=jax.ShapeDtypeStruct((m, n), out_dtype),
               out_spec=pl.BlockSpec((tm, tn), lambda i, j, k: (i, j)),
               ca=1, cb=0, acc_shape=(tm, tn), name=name)


def _mm_nt(a, b, out_dtype, tm, tn, tk, name):
    m, kk = a.shape
    n = b.shape[0]
    return _mm(a, b, grid=(m // tm, n // tn, kk // tk),
               a_spec=pl.BlockSpec((tm, tk), lambda i, j, k: (i, k)),
               b_spec=pl.BlockSpec((tn, tk), lambda i, j, k: (j, k)),
               out_shape=jax.ShapeDtypeStruct((m, n), out_dtype),
               out_spec=pl.BlockSpec((tm, tn), lambda i, j, k: (i, j)),
               ca=1, cb=1, acc_shape=(tm, tn), name=name)


def _mm_tn(a, b, out_dtype, tm, tn, tk, name):
    kk, m = a.shape
    n = b.shape[1]
    return _mm(a, b, grid=(m // tm, n // tn, kk // tk),
               a_spec=pl.BlockSpec((tk, tm), lambda i, j, k: (k, i)),
               b_spec=pl.BlockSpec((tk, tn), lambda i, j, k: (k, j)),
               out_shape=jax.ShapeDtypeStruct((m, n), out_dtype),
               out_spec=pl.BlockSpec((tm, tn), lambda i, j, k: (i, j)),
               ca=0, cb=0, acc_shape=(tm, tn), name=name)


ROW_TILE = 256


def _rms(x, g):
    r = lax.rsqrt(jnp.mean(x * x, axis=-1, keepdims=True) + EPS)
    return x * r * g


def _prenorm(x, g, name):
    def body(x_ref, g_ref, o_ref):
        o_ref[...] = _rms(x_ref[...], g_ref[...]).astype(BF16)

    return pl.pallas_call(
        body, grid=(S // ROW_TILE,),
        in_specs=[pl.BlockSpec((ROW_TILE, D), lambda i: (i, 0)), pl.BlockSpec((1, D), lambda i: (0, 0))],
        out_specs=pl.BlockSpec((ROW_TILE, D), lambda i: (i, 0)),
        out_shape=jax.ShapeDtypeStruct((S, D), BF16), compiler_params=_cp("parallel"), name=name)(x, g)


def _postnorm_res(x, f, g_post, g_next, name):
    def body(x_ref, f_ref, gp_ref, gn_ref, xo_ref, ho_ref):
        xn = x_ref[...] + _rms(f_ref[...], gp_ref[...])
        xo_ref[...] = xn
        ho_ref[...] = _rms(xn, gn_ref[...]).astype(BF16)

    row = pl.BlockSpec((ROW_TILE, D), lambda i: (i, 0))
    vec = pl.BlockSpec((1, D), lambda i: (0, 0))
    return pl.pallas_call(
        body, grid=(S // ROW_TILE,), in_specs=[row, row, vec, vec], out_specs=[row, row],
        out_shape=[jax.ShapeDtypeStruct((S, D), F32), jax.ShapeDtypeStruct((S, D), BF16)],
        compiler_params=_cp("parallel"), name=name)(x, f, g_post, g_next)


def _norm_bwd(f, g, dys, res, out_dtype, name):
    ndy = len(dys)
    has_res = res is not None

    def body(*refs):
        f_ref, g_ref = refs[0], refs[1]
        dy_refs = refs[2:2 + ndy]
        res_ref = refs[2 + ndy] if has_res else None
        o_ref, dg_ref = refs[-2], refs[-1]
        fv = f_ref[...]
        dy = dy_refs[0][...].astype(F32)
        for r in dy_refs[1:]:
            dy = dy + r[...].astype(F32)
        r = lax.rsqrt(jnp.mean(fv * fv, axis=-1, keepdims=True) + EPS)
        n = fv * r
        dn = dy * g_ref[...]
        df = r * (dn - n * jnp.mean(dn * n, axis=-1, keepdims=True))
        if has_res:
            df = df + res_ref[...]
        o_ref[...] = df.astype(out_dtype)

        @pl.when(pl.program_id(0) == 0)
        def _():
            dg_ref[...] = jnp.zeros((1, D), F32)

        dg_ref[...] += jnp.sum(dy * n, axis=0, keepdims=True)

    row = pl.BlockSpec((ROW_TILE, D), lambda i: (i, 0))
    vec = pl.BlockSpec((1, D), lambda i: (0, 0))
    in_specs = [row, vec] + [row] * ndy + ([row] if has_res else [])
    args = [f, g] + list(dys) + ([res] if has_res else [])
    return pl.pallas_call(
        body, grid=(S // ROW_TILE,), in_specs=in_specs, out_specs=[row, vec],
        out_shape=[jax.ShapeDtypeStruct((S, D), out_dtype), jax.ShapeDtypeStruct((1, D), F32)],
        compiler_params=_cp("arbitrary"), name=name)(*args)


def _loss_head(y, target, name):
    def body(y_ref, t_ref, dy_ref, l_ref):
        e = y_ref[...] - t_ref[...]
        dy_ref[...] = e * (1.0 / D)

        @pl.when(pl.program_id(0) == 0)
        def _():
            l_ref[...] = jnp.zeros((8, LANES), F32)

        l_ref[...] += jnp.sum(e * e) * (0.5 / D)

    row = pl.BlockSpec((ROW_TILE, D), lambda i: (i, 0))
    return pl.pallas_call(
        body, grid=(S // ROW_TILE,), in_specs=[row, row],
        out_specs=[row, pl.BlockSpec((8, LANES), lambda i: (0, 0))],
        out_shape=[jax.ShapeDtypeStruct((S, D), F32), jax.ShapeDtypeStruct((8, LANES), F32)],
        compiler_params=_cp("arbitrary"), name=name)(y, target)


def _bucket_tiles():
    a = np.arange(BLK)[:, None]
    b = np.arange(2 * BLK)[None, :]
    dist = a + BLK - b
    out = np.zeros((4, 2, BLK, 2 * BLK), np.int32)
    cfg = [(w // d, d) for w, d in A_GROUPS] + [(BLK - 1, 1)]
    for gi, (max_dist, d) in enumerate(cfg):
        band = (dist >= 0) & (dist <= max_dist)
        tok = np.maximum(dist, 0) * d
        nf = np.maximum(tok, 1).astype(np.float32)
        max_exact = NUM_BUCKETS // 2
        large = max_exact + (np.log(nf / np.float32(max_exact)) / np.float32(math.log(MAX_DISTANCE / max_exact))
                             * np.float32(NUM_BUCKETS - max_exact)).astype(np.int32)
        large = np.minimum(large, NUM_BUCKETS - 1)
        bkt = np.where(tok < max_exact, tok, large).astype(np.int32)
        full = np.where(band, bkt, -1)
        out[gi, 1] = full
        out[gi, 0] = np.where(b >= BLK, full, -1)
    return out


def _bias_tiles(rel_bias, buckets, name):
    def body(tab_ref, bkt_ref, o_ref):
        h = pl.program_id(0)
        bkt = bkt_ref[...]
        acc = jnp.zeros(bkt.shape, F32)
        for bb in range(NUM_BUCKETS):
            acc = jnp.where(bkt == bb, tab_ref[bb, h], acc)
        o_ref[...] = jnp.where(bkt < 0, NEG, acc)

    return pl.pallas_call(
        body, grid=(N_BIAS_HEADS,),
        in_specs=[SMEM, pl.BlockSpec((None, 2, BLK, 2 * BLK), lambda h: (jnp.minimum(h // 4, 3), 0, 0, 0))],
        out_specs=pl.BlockSpec((None, 2, BLK, 2 * BLK), lambda h: (h, 0, 0, 0)),
        out_shape=jax.ShapeDtypeStruct((N_BIAS_HEADS, 2, BLK, 2 * BLK), F32),
        compiler_params=_cp("arbitrary"), name=name)(rel_bias, buckets)


def _bias_grad(gs, buckets, name):
    ng = len(gs)

    def body(*refs):
        g_refs = refs[:ng]
        bkt_ref, o_ref = refs[ng], refs[ng + 1]
        h = pl.program_id(0)
        g = g_refs[0][...]
        for r in g_refs[1:]:
            g = g + r[...]
        bkt = bkt_ref[...]
        row = lax.broadcasted_iota(jnp.int32, (NUM_BUCKETS, LANES), 0)
        lane = lax.broadcasted_iota(jnp.int32, (NUM_BUCKETS, LANES), 1)

        @pl.when(h == 0)
        def _():
            o_ref[...] = jnp.zeros((NUM_BUCKETS, LANES), F32)

        acc = o_ref[...]
        for bb in range(NUM_BUCKETS):
            s = jnp.sum(jnp.where(bkt == bb, g, 0.0))
            acc = jnp.where((row == bb) & (lane == h), s, acc)
        o_ref[...] = acc

    g_spec = pl.BlockSpec((None, BLK, 2 * BLK), lambda h: (h, 0, 0))
    return pl.pallas_call(
        body, grid=(N_BIAS_HEADS,),
        in_specs=[g_spec] * ng + [pl.BlockSpec((None, None, BLK, 2 * BLK), lambda h: (jnp.minimum(h // 4, 3), 1, 0, 0))],
        out_specs=pl.BlockSpec((NUM_BUCKETS, LANES), lambda h: (0, 0)),
        out_shape=jax.ShapeDtypeStruct((NUM_BUCKETS, LANES), F32),
        compiler_params=_cp("arbitrary"), name=name)(*gs, buckets)


def _to_class_major(src_ref, dst_ref, d, scale=None, dtype=None):
    ln = S // d
    for r in range(d):
        v = src_ref[pl.ds(r, ln, stride=d), :] if d > 1 else src_ref[...]
        if scale is not None:
            v = v * scale
        dst_ref[pl.ds(r * ln, ln), :] = v.astype(dtype or dst_ref.dtype)


def _block_rows(b, d):
    nbc = NB // d
    i = b % nbc
    r = b // nbc
    has_prev = (i > 0).astype(jnp.int32)
    prev = pl.multiple_of(jnp.maximum(b - 1, 0) * BLK, BLK)
    nat = i * (BLK * d) + r
    return has_prev, prev, nat


def _lane_halves(v0, v1):
    lane = lax.broadcasted_iota(jnp.int32, (v0.shape[0], LANES), 1)
    return jnp.where(lane < HD, v0, v1)


def _band_fwd(proj, bias, *, d, q0, k0, v0, npairs, bias0, shared_kv, name):
    def body(q_ref, k_ref, v_ref, b_ref, num_ref, st_ref, qs, ks, vs):
        p = pl.program_id(0)
        _to_class_major(q_ref, qs, d, scale=SCALE)
        _to_class_major(k_ref, ks, d)
        _to_class_major(v_ref, vs, d)
        lane = lax.broadcasted_iota(jnp.int32, (BLK, LANES), 1)

        def blk(b, carry):
            has_prev, prev, nat = _block_rows(b, d)
            cur = pl.multiple_of(b * BLK, BLK)
            qb = qs[pl.ds(cur, BLK), :]
            k2 = jnp.concatenate([ks[pl.ds(prev, BLK), :], ks[pl.ds(cur, BLK), :]], axis=0)
            v2 = jnp.concatenate([vs[pl.ds(prev, BLK), :], vs[pl.ds(cur, BLK), :]], axis=0)
            nums, ms, ls = [], [], []
            for hh in range(2):
                qh = qb[:, hh * HD:(hh + 1) * HD]
                if shared_kv:
                    kh = jnp.where(p >= 2, k2[:, HD:], k2[:, :HD])
                    vh = jnp.where(p >= 2, v2[:, HD:], v2[:, :HD])
                else:
                    kh = k2[:, hh * HD:(hh + 1) * HD]
                    vh = v2[:, hh * HD:(hh + 1) * HD]
                z = _dot(qh, kh, 1, 1) + b_ref[hh, has_prev]
                m = jnp.max(z, axis=1, keepdims=True)
                e = jnp.exp(z - m)
                ls.append(jnp.sum(e, axis=1, keepdims=True))
                ms.append(m)
                nums.append(_dot(e.astype(BF16), vh, 1, 0))
            num_t = jnp.concatenate(nums, axis=1)
            st_t = jnp.where(lane < 32, ms[0], jnp.where(lane < 64, ls[0], jnp.where(lane < 96, ms[1], ls[1])))
            if d > 1:
                num_ref[pl.ds(nat, BLK, stride=d), :] = num_t
                st_ref[pl.ds(nat, BLK, stride=d), :] = st_t
            else:
                num_ref[pl.ds(cur, BLK), :] = num_t
                st_ref[pl.ds(cur, BLK), :] = st_t
            return carry

        lax.fori_loop(0, NB, blk, 0)

    slab = lambda off, per_pair: pl.BlockSpec((None, S, LANES), (lambda p: (off + p, 0, 0)) if per_pair else (lambda p: (off, 0, 0)))
    out = pl.BlockSpec((None, S, LANES), lambda p: (p, 0, 0))
    return pl.pallas_call(
        body, grid=(npairs,),
        in_specs=[slab(q0, True), slab(k0, not shared_kv), slab(v0, not shared_kv),
                  pl.BlockSpec((None, 2, 2, BLK, 2 * BLK), lambda p: (bias0 + p, 0, 0, 0, 0))],
        out_specs=[out, out],
        out_shape=[jax.ShapeDtypeStruct((npairs, S, LANES), F32)] * 2,
        scratch_shapes=[pltpu.VMEM((S, LANES), BF16)] * 3,
        compiler_params=_cp("arbitrary"), name=name)(proj, proj, proj, bias)


def _combine_a(nums, stats, name):
    rt = 512

    def body(n0, n1, n2, s0, s1, s2, o_ref, l_ref):
        n_refs, s_refs = (n0, n1, n2), (s0, s1, s2)
        outs, lses = [], []
        for hh in range(2):
            ms = [s[:, 64 * hh:64 * hh + 1] for s in s_refs]
            ls = [s[:, 64 * hh + 32:64 * hh + 33] for s in s_refs]
            mx = jnp.maximum(jnp.maximum(ms[0], ms[1]), ms[2])
            cs = [jnp.exp(m - mx) for m in ms]
            z = cs[0] * ls[0] + cs[1] * ls[1] + cs[2] * ls[2]
            acc = cs[0] * n_refs[0][:, hh * HD:(hh + 1) * HD]
            acc = acc + cs[1] * n_refs[1][:, hh * HD:(hh + 1) * HD]
            acc = acc + cs[2] * n_refs[2][:, hh * HD:(hh + 1) * HD]
            outs.append(acc / z)
            lses.append(mx + jnp.log(z))
        o_ref[...] = jnp.concatenate(outs, axis=1)
        l_ref[...] = _lane_halves(lses[0], lses[1])

    spec = pl.BlockSpec((None, rt, LANES), lambda p, i: (p, i, 0))
    return pl.pallas_call(
        body, grid=(2, S // rt), in_specs=[spec] * 6, out_specs=[spec, spec],
        out_shape=[jax.ShapeDtypeStruct((2, S, LANES), F32)] * 2,
        compiler_params=_cp("parallel", "parallel"), name=name)(*nums, *stats)


def _combine_b(num, stats, sinks, name):
    rt = 512

    def body(sink_ref, n_ref, s_ref, o_ref, l_ref):
        p = pl.program_id(0)
        outs, lses = [], []
        for hh in range(2):
            sink = sink_ref[0, 2 * p + hh]
            m = s_ref[:, 64 * hh:64 * hh + 1]
            l = s_ref[:, 64 * hh + 32:64 * hh + 33]
            mx = jnp.maximum(m, sink)
            c = jnp.exp(m - mx)
            z = l * c + jnp.exp(sink - mx)
            outs.append(n_ref[:, hh * HD:(hh + 1) * HD] * (c / z))
            lses.append(mx + jnp.log(z))
        o_ref[...] = jnp.concatenate(outs, axis=1)
        l_ref[...] = _lane_halves(lses[0], lses[1])

    spec = pl.BlockSpec((None, rt, LANES), lambda p, i: (p, i, 0))
    return pl.pallas_call(
        body, grid=(4, S // rt), in_specs=[SMEM, spec, spec], out_specs=[spec, spec],
        out_shape=[jax.ShapeDtypeStruct((4, S, LANES), F32)] * 2,
        compiler_params=_cp("parallel", "parallel"), name=name)(sinks, num, stats)


def _band_bwd(proj, bias, o, do, lse, sinks, *, d, q0, k0, v0, npairs, bias0, shared_kv, name):
    nkv = 1 if shared_kv else npairs

    def body(sink_ref, q_ref, k_ref, v_ref, b_ref, o_ref, do_ref, lse_ref,
             dq_ref, dk_ref, dv_ref, g_ref, ds_ref,
             qs, ks, vs, dos, lses, dls, dl_nat, dq_nat, dk_cm, dv_cm, kv_nat):
        p = pl.program_id(0)
        lane = lax.broadcasted_iota(jnp.int32, (S, LANES), 1)
        dov = do_ref[...]
        prod = dov * o_ref[...]
        dl0 = jnp.sum(jnp.where(lane < HD, prod, 0.0), axis=1, keepdims=True)
        dl1 = jnp.sum(jnp.where(lane >= HD, prod, 0.0), axis=1, keepdims=True)
        dl_nat[...] = jnp.where(lane < HD, dl0, dl1)
        if shared_kv:
            row8 = lax.broadcasted_iota(jnp.int32, (8, LANES), 0)
            lane8 = lax.broadcasted_iota(jnp.int32, (8, LANES), 1)
            t = jnp.zeros((8, LANES), F32)
            lv = lse_ref[...]
            for hh in range(2):
                sink = sink_ref[0, 2 * p + hh]
                ps = jnp.exp(sink - lv[:, 64 * hh:64 * hh + 1])
                dsink = -jnp.sum(ps * (dl0 if hh == 0 else dl1))
                t = jnp.where((row8 == 0) & (lane8 == hh), dsink, t)
            ds_ref[...] = t
        else:
            ds_ref[...] = jnp.zeros((8, LANES), F32)
        _to_class_major(q_ref, qs, d, scale=SCALE)
        _to_class_major(k_ref, ks, d)
        _to_class_major(v_ref, vs, d)
        _to_class_major(do_ref, dos, d)
        _to_class_major(lse_ref, lses, d)
        _to_class_major(dl_nat, dls, d)

        def zero_kv():
            dk_cm[...] = jnp.zeros((S, LANES), F32)
            dv_cm[...] = jnp.zeros((S, LANES), F32)

        if shared_kv:
            pl.when(p == 0)(zero_kv)
        else:
            zero_kv()

        g_ref[...] = jnp.zeros((2, BLK, 2 * BLK), F32)
        lane2 = lax.broadcasted_iota(jnp.int32, (2 * BLK, LANES), 1)

        def blk(b, carry):
            has_prev, prev, nat = _block_rows(b, d)
            cur = pl.multiple_of(b * BLK, BLK)
            qb = qs[pl.ds(cur, BLK), :]
            dob = dos[pl.ds(cur, BLK), :]
            lb = lses[pl.ds(cur, BLK), :]
            dlb = dls[pl.ds(cur, BLK), :]
            k2 = jnp.concatenate([ks[pl.ds(prev, BLK), :], ks[pl.ds(cur, BLK), :]], axis=0)
            v2 = jnp.concatenate([vs[pl.ds(prev, BLK), :], vs[pl.ds(cur, BLK), :]], axis=0)
            dqs, dks, dvs = [], [], []
            for hh in range(2):
                qh = qb[:, hh * HD:(hh + 1) * HD]
                doh = dob[:, hh * HD:(hh + 1) * HD]
                if shared_kv:
                    kh = jnp.where(p >= 2, k2[:, HD:], k2[:, :HD])
                    vh = jnp.where(p >= 2, v2[:, HD:], v2[:, :HD])
                else:
                    kh = k2[:, hh * HD:(hh + 1) * HD]
                    vh = v2[:, hh * HD:(hh + 1) * HD]
                z = _dot(qh, kh, 1, 1) + b_ref[hh, has_prev]
                pr = jnp.exp(z - lb[:, 64 * hh:64 * hh + 1])
                dp = _dot(doh, vh, 1, 1)
                dz = pr * (dp - dlb[:, 64 * hh:64 * hh + 1])
                g_ref[hh] += dz
                dzb = dz.astype(BF16)
                dqs.append(_dot(dzb, kh, 1, 0) * SCALE)
                dks.append(_dot(dzb, qh, 0, 0))
                dvs.append(_dot(pr.astype(BF16), doh, 0, 0))
            dq_t = jnp.concatenate(dqs, axis=1)
            if shared_kv:
                dk_t = jnp.concatenate([dks[0] + dks[1]] * 2, axis=1)
                dv_t = jnp.concatenate([dvs[0] + dvs[1]] * 2, axis=1)
                mine = (lane2 >= HD) == (p >= 2)
                dk_t = jnp.where(mine, dk_t, 0.0)
                dv_t = jnp.where(mine, dv_t, 0.0)
            else:
                dk_t = jnp.concatenate(dks, axis=1)
                dv_t = jnp.concatenate(dvs, axis=1)
            dk_cm[pl.ds(prev, BLK), :] += dk_t[:BLK]
            dk_cm[pl.ds(cur, BLK), :] += dk_t[BLK:]
            dv_cm[pl.ds(prev, BLK), :] += dv_t[:BLK]
            dv_cm[pl.ds(cur, BLK), :] += dv_t[BLK:]
            if d > 1:
                dq_nat[pl.ds(nat, BLK, stride=d), :] = dq_t
            else:
                dq_nat[pl.ds(cur, BLK), :] = dq_t
            return carry

        lax.fori_loop(0, NB, blk, 0)
        dq_ref[...] = dq_nat[...].astype(BF16)

        def from_class_major(src, dst_ref):
            if d == 1:
                dst_ref[...] = src[...].astype(BF16)
            else:
                ln = S // d
                for r in range(d):
                    kv_nat[pl.ds(r, ln, stride=d), :] = src[pl.ds(r * ln, ln), :]
                dst_ref[...] = kv_nat[...].astype(BF16)

        def write_kv():
            from_class_major(dk_cm, dk_ref)
            from_class_major(dv_cm, dv_ref)

        if shared_kv:
            pl.when(p == npairs - 1)(write_kv)
        else:
            write_kv()

    slab = lambda off, per_pair: pl.BlockSpec((None, S, LANES), (lambda p: (off + p, 0, 0)) if per_pair else (lambda p: (off, 0, 0)))
    pair = pl.BlockSpec((None, S, LANES), lambda p: (p, 0, 0))
    kv_out = pair if not shared_kv else pl.BlockSpec((None, S, LANES), lambda p: (0, 0, 0))
    return pl.pallas_call(
        body, grid=(npairs,),
        in_specs=[SMEM, slab(q0, True), slab(k0, not shared_kv), slab(v0, not shared_kv),
                  pl.BlockSpec((None, 2, 2, BLK, 2 * BLK), lambda p: (bias0 + p, 0, 0, 0, 0)),
                  pair, pair, pair],
        out_specs=[pair, kv_out, kv_out,
                   pl.BlockSpec((None, 2, BLK, 2 * BLK), lambda p: (p, 0, 0, 0)),
                   pl.BlockSpec((None, 8, LANES), lambda p: (p, 0, 0))],
        out_shape=[jax.ShapeDtypeStruct((npairs, S, LANES), BF16),
                   jax.ShapeDtypeStruct((nkv, S, LANES), BF16),
                   jax.ShapeDtypeStruct((nkv, S, LANES), BF16),
                   jax.ShapeDtypeStruct((npairs, 2, BLK, 2 * BLK), F32),
                   jax.ShapeDtypeStruct((npairs, 8, LANES), F32)],
        scratch_shapes=[pltpu.VMEM((S, LANES), BF16)] * 4 + [pltpu.VMEM((S, LANES), F32)] * 7,
        compiler_params=_cp("arbitrary"), name=name)(sinks, proj, proj, proj, bias, o, do, lse)


KC = 512
NSUB = KC // BLK


def _split2(x):
    hi = x.astype(BF16)
    lo = (x - hi.astype(F32)).astype(BF16)
    return hi, lo


def _sub_prefix(x, tri):
    st = jnp.concatenate([x[:, s * BLK:(s + 1) * BLK] for s in range(NSUB)], axis=0)
    hi, lo = _split2(st)
    r = _dot(hi, tri, 1, 0) + _dot(lo, tri, 1, 0)
    return [r[s * BLK:(s + 1) * BLK] for s in range(NSUB)]


def _log_sig_pair(z):
    sp = jnp.log1p(jnp.exp(-jnp.abs(z)))
    return jnp.minimum(z, 0.0) - sp, jnp.minimum(-z, 0.0) - sp


def _stick_fwd(proj, *, q0, k0, v0, name):
    def body(q_ref, k_ref, v_ref, o_ref, t_ref, qs, ks, vs):
        qs[...] = (q_ref[...] * SCALE).astype(BF16)
        ks[...] = k_ref[...].astype(BF16)
        vs[...] = v_ref[...].astype(BF16)
        jj = lax.broadcasted_iota(jnp.int32, (BLK, BLK), 0)
        ss = lax.broadcasted_iota(jnp.int32, (BLK, BLK), 1)
        tri = (jj > ss).astype(BF16)
        col = lax.broadcasted_iota(jnp.int32, (BLK, KC), 1)
        rowi = lax.broadcasted_iota(jnp.int32, (BLK, KC), 0)

        def qblock(i, carry0):
            t0 = pl.multiple_of(i * BLK, BLK)
            qb = qs[pl.ds(t0, BLK), :]
            nch = i // NSUB + 1
            outs, tots = [], []
            for hh in range(2):
                qh = qb[:, hh * HD:(hh + 1) * HD]

                def chunk(cc, st):
                    acc, run = st
                    c = nch - 1 - cc
                    s0 = pl.multiple_of(c * KC, KC)
                    kh = ks[pl.ds(s0, KC), hh * HD:(hh + 1) * HD]
                    vh = vs[pl.ds(s0, KC), hh * HD:(hh + 1) * HD]
                    z = _dot(qh, kh, 1, 1)
                    before = (s0 + col) < (t0 + rowi)
                    lb, lk = _log_sig_pair(z)
                    lk = jnp.where(before, lk, 0.0)
                    suf = _sub_prefix(lk, tri)
                    ws = []
                    for s in reversed(range(NSUB)):
                        lks = lk[:, s * BLK:(s + 1) * BLK]
                        rest = suf[s] + run
                        ws.append(jnp.exp(lb[:, s * BLK:(s + 1) * BLK] + rest))
                        run = run + jnp.sum(lks, axis=1, keepdims=True)
                    w = jnp.concatenate(ws[::-1], axis=1)
                    w = jnp.where(before, w, 0.0)
                    return acc + _dot(w.astype(BF16), vh, 1, 0), run

                acc, run = lax.fori_loop(0, nch, chunk, (jnp.zeros((BLK, HD), F32), jnp.zeros((BLK, 1), F32)))
                outs.append(acc)
                tots.append(run)
            o_ref[pl.ds(t0, BLK), :] = jnp.concatenate(outs, axis=1)
            t_ref[pl.ds(t0, BLK), :] = _lane_halves(tots[0], tots[1])
            return carry0

        lax.fori_loop(0, NB, qblock, 0)

    slab = lambda off: pl.BlockSpec((None, S, LANES), lambda p: (off + p, 0, 0))
    out = pl.BlockSpec((None, S, LANES), lambda p: (p, 0, 0))
    return pl.pallas_call(
        body, grid=(2,), in_specs=[slab(q0), slab(k0), slab(v0)], out_specs=[out, out],
        out_shape=[jax.ShapeDtypeStruct((2, S, LANES), F32)] * 2,
        scratch_shapes=[pltpu.VMEM((S, LANES), BF16)] * 3,
        compiler_params=_cp("arbitrary"), name=name)(proj, proj, proj)


def _stick_bwd(proj, do, tot, *, q0, k0, v0, name):
    def body(q_ref, k_ref, v_ref, do_ref, t_ref, dq_ref, dk_ref, dv_ref, qs, ks, vs, dos, dk_acc, dv_acc):
        qs[...] = (q_ref[...] * SCALE).astype(BF16)
        ks[...] = k_ref[...].astype(BF16)
        vs[...] = v_ref[...].astype(BF16)
        dos[...] = do_ref[...].astype(BF16)
        dk_acc[...] = jnp.zeros((2, S, HD), F32)
        dv_acc[...] = jnp.zeros((2, S, HD), F32)
        jj = lax.broadcasted_iota(jnp.int32, (BLK, BLK), 0)
        ss = lax.broadcasted_iota(jnp.int32, (BLK, BLK), 1)
        tri_inc = (jj <= ss).astype(BF16)
        tri_exc = (jj < ss).astype(BF16)
        col = lax.broadcasted_iota(jnp.int32, (BLK, KC), 1)
        rowi = lax.broadcasted_iota(jnp.int32, (BLK, KC), 0)

        def qblock(i, carry0):
            t0 = pl.multiple_of(i * BLK, BLK)
            qb = qs[pl.ds(t0, BLK), :]
            dob = dos[pl.ds(t0, BLK), :]
            tb = t_ref[pl.ds(t0, BLK), :]
            nch = i // NSUB + 1
            dqs = []
            for hh in range(2):
                qh = qb[:, hh * HD:(hh + 1) * HD]
                doh = dob[:, hh * HD:(hh + 1) * HD]
                tt = tb[:, 64 * hh:64 * hh + 1]

                def chunk(c, st):
                    dq, prun, erun = st
                    s0 = pl.multiple_of(c * KC, KC)
                    kh = ks[pl.ds(s0, KC), hh * HD:(hh + 1) * HD]
                    vh = vs[pl.ds(s0, KC), hh * HD:(hh + 1) * HD]
                    z = _dot(qh, kh, 1, 1)
                    before = (s0 + col) < (t0 + rowi)
                    lb, lk = _log_sig_pair(z)
                    lk = jnp.where(before, lk, 0.0)
                    pin = _sub_prefix(lk, tri_inc)
                    ws = []
                    for s in range(NSUB):
                        rest = tt - (pin[s] + prun)
                        ws.append(jnp.exp(lb[:, s * BLK:(s + 1) * BLK] + rest))
                        prun = prun + jnp.sum(lk[:, s * BLK:(s + 1) * BLK], axis=1, keepdims=True)
                    w = jnp.where(before, jnp.concatenate(ws, axis=1), 0.0)
                    dw = _dot(doh, vh, 1, 1)
                    e = w * dw
                    pex = _sub_prefix(e, tri_exc)
                    cs = []
                    for s in range(NSUB):
                        cs.append(pex[s] + erun)
                        erun = erun + jnp.sum(e[:, s * BLK:(s + 1) * BLK], axis=1, keepdims=True)
                    cex = jnp.concatenate(cs, axis=1)
                    sig = jnp.exp(lb)
                    dz = jnp.where(before, e * (1.0 - sig) - cex * sig, 0.0).astype(BF16)
                    dq = dq + _dot(dz, kh, 1, 0)
                    dk_acc[hh, pl.ds(s0, KC), :] += _dot(dz, qh, 0, 0)
                    dv_acc[hh, pl.ds(s0, KC), :] += _dot(w.astype(BF16), doh, 0, 0)
                    return dq, prun, erun

                init = (jnp.zeros((BLK, HD), F32), jnp.zeros((BLK, 1), F32), jnp.zeros((BLK, 1), F32))
                dq, _, _ = lax.fori_loop(0, nch, chunk, init)
                dqs.append(dq * SCALE)
            dq_ref[pl.ds(t0, BLK), :] = jnp.concatenate(dqs, axis=1).astype(BF16)
            return carry0

        lax.fori_loop(0, NB, qblock, 0)
        dk_ref[...] = jnp.concatenate([dk_acc[0], dk_acc[1]], axis=1).astype(BF16)
        dv_ref[...] = jnp.concatenate([dv_acc[0], dv_acc[1]], axis=1).astype(BF16)

    slab = lambda off: pl.BlockSpec((None, S, LANES), lambda p: (off + p, 0, 0))
    pair = pl.BlockSpec((None, S, LANES), lambda p: (p, 0, 0))
    return pl.pallas_call(
        body, grid=(2,), in_specs=[slab(q0), slab(k0), slab(v0), pair, pair], out_specs=[pair] * 3,
        out_shape=[jax.ShapeDtypeStruct((2, S, LANES), BF16)] * 3,
        scratch_shapes=[pltpu.VMEM((S, LANES), BF16)] * 4 + [pltpu.VMEM((2, S, HD), F32)] * 2,
        compiler_params=_cp("arbitrary"), name=name)(proj, proj, proj, do, tot)


def _cat_slabs(ref):
    return jnp.concatenate([ref[s] for s in range(ref.shape[0])], axis=1)


def _merge_fwd(o_a, o_b, o_c, gates, b_gate, wa, wb, wc, w_out, name):
    tm = ROW_TILE

    def body(oa_ref, ob_ref, oc_ref, g_ref, bg_ref, wa_ref, wb_ref, wc_ref, wo_ref, mg_ref, mo_ref):
        acc = jnp.zeros((tm, D), F32)
        for i, (o_ref, w_ref) in enumerate(((oa_ref, wa_ref), (ob_ref, wb_ref), (oc_ref, wc_ref))):
            pr = _dot(_cat_slabs(o_ref).astype(BF16), w_ref[...], 1, 0)
            sg = jax.nn.sigmoid(g_ref[:, i * D:(i + 1) * D] + bg_ref[i:i + 1, :])
            acc = acc + sg * pr
        mg = acc.astype(BF16)
        mg_ref[...] = mg
        mo_ref[...] = _dot(mg, wo_ref[...], 1, 0)

    slabs = lambda n: pl.BlockSpec((n, tm, LANES), lambda i: (0, i, 0))
    full = lambda r, c: pl.BlockSpec((r, c), lambda i: (0, 0))
    row = pl.BlockSpec((tm, D), lambda i: (i, 0))
    return pl.pallas_call(
        body, grid=(S // tm,),
        in_specs=[slabs(2), slabs(4), slabs(2), pl.BlockSpec((tm, GATE_COLS), lambda i: (i, 0)), full(3, D),
                  full(256, D), full(512, D), full(256, D), full(D, D)],
        out_specs=[row, row],
        out_shape=[jax.ShapeDtypeStruct((S, D), BF16), jax.ShapeDtypeStruct((S, D), F32)],
        compiler_params=_cp("parallel"), name=name)(o_a, o_b, o_c, gates, b_gate, wa, wb, wc, w_out)


def _merge_bwd(d_mo, o_a, o_b, o_c, gates, b_gate, wa, wb, wc, w_out, name):
    tm = ROW_TILE

    def body(dmo_ref, oa_ref, ob_ref, oc_ref, g_ref, bg_ref, wa_ref, wb_ref, wc_ref, wo_ref,
             doa_ref, dob_ref, doc_ref, dg_ref, dwa_ref, dwb_ref, dwc_ref, dbg_ref):
        @pl.when(pl.program_id(0) == 0)
        def _():
            dwa_ref[...] = jnp.zeros(dwa_ref.shape, F32)
            dwb_ref[...] = jnp.zeros(dwb_ref.shape, F32)
            dwc_ref[...] = jnp.zeros(dwc_ref.shape, F32)
            dbg_ref[...] = jnp.zeros(dbg_ref.shape, F32)

        dmg = _dot(dmo_ref[...], wo_ref[...], 1, 1)
        trip = ((oa_ref, wa_ref, doa_ref, dwa_ref), (ob_ref, wb_ref, dob_ref, dwb_ref), (oc_ref, wc_ref, doc_ref, dwc_ref))
        for i, (o_ref, w_ref, do_ref, dw_ref) in enumerate(trip):
            ob = _cat_slabs(o_ref).astype(BF16)
            pr = _dot(ob, w_ref[...], 1, 0)
            sg = jax.nn.sigmoid(g_ref[:, i * D:(i + 1) * D] + bg_ref[i:i + 1, :])
            dgate = dmg * pr * sg * (1.0 - sg)
            dg_ref[:, i * D:(i + 1) * D] = dgate.astype(BF16)
            dbg_ref[i:i + 1, :] += jnp.sum(dgate, axis=0, keepdims=True)
            dpr = (dmg * sg).astype(BF16)
            do = _dot(dpr, w_ref[...], 1, 1)
            for s in range(do_ref.shape[0]):
                do_ref[s] = do[:, s * LANES:(s + 1) * LANES]
            dw_ref[...] += _dot(ob, dpr, 0, 0)

    slabs = lambda n: pl.BlockSpec((n, tm, LANES), lambda i: (0, i, 0))
    full = lambda r, c: pl.BlockSpec((r, c), lambda i: (0, 0))
    row = pl.BlockSpec((tm, D), lambda i: (i, 0))
    return pl.pallas_call(
        body, grid=(S // tm,),
        in_specs=[row, slabs(2), slabs(4), slabs(2), pl.BlockSpec((tm, GATE_COLS), lambda i: (i, 0)), full(3, D),
                  full(256, D), full(512, D), full(256, D), full(D, D)],
        out_specs=[slabs(2), slabs(4), slabs(2), pl.BlockSpec((tm, GATE_COLS), lambda i: (i, 0)),
                   full(256, D), full(512, D), full(256, D), full(3, D)],
        out_shape=[jax.ShapeDtypeStruct((2, S, LANES), F32), jax.ShapeDtypeStruct((4, S, LANES), F32),
                   jax.ShapeDtypeStruct((2, S, LANES), F32), jax.ShapeDtypeStruct((S, GATE_COLS), BF16),
                   jax.ShapeDtypeStruct((256, D), F32), jax.ShapeDtypeStruct((512, D), F32),
                   jax.ShapeDtypeStruct((256, D), F32), jax.ShapeDtypeStruct((3, D), F32)],
        compiler_params=_cp("arbitrary"), name=name)(d_mo, o_a, o_b, o_c, gates, b_gate, wa, wb, wc, w_out)


FC = 256
GELU_K = math.sqrt(2.0 / math.pi)
GELU_C = 0.044715


def _shift_down(x, n):
    row = lax.broadcasted_iota(jnp.int32, x.shape, 0)
    return jnp.where(row < n, 0.0, pltpu.roll(x, n, 0))


def _shift_up(x, n):
    row = lax.broadcasted_iota(jnp.int32, x.shape, 0)
    return jnp.where(row >= x.shape[0] - n, 0.0, pltpu.roll(x, x.shape[0] - n, 0))


def _conv(u, w_ref, half, b):
    return (w_ref[0:1, half, :] * _shift_down(u, 2) + w_ref[1:2, half, :] * _shift_down(u, 1)
            + w_ref[2:3, half, :] * u + b)


def _ffn_act(u, conv_w, conv_b, name):
    def body(u_ref, w_ref, b_ref, a_ref):
        yg = _conv(u_ref[0], w_ref, 0, b_ref[0:1, :])
        yv = _conv(u_ref[1], w_ref, 1, b_ref[1:2, :])
        th = jnp.tanh(GELU_K * (yg + GELU_C * yg * yg * yg))
        a_ref[...] = (0.5 * yg * (1.0 + th) * yv).astype(BF16)

    return pl.pallas_call(
        body, grid=(D_FF // FC,),
        in_specs=[pl.BlockSpec((2, S, FC), lambda j: (0, 0, j)), pl.BlockSpec((3, 2, FC), lambda j: (0, 0, j)),
                  pl.BlockSpec((2, FC), lambda j: (0, j))],
        out_specs=pl.BlockSpec((S, FC), lambda j: (0, j)),
        out_shape=jax.ShapeDtypeStruct((S, D_FF), BF16),
        compiler_params=_cp("parallel"), name=name)(u, conv_w, conv_b)


def _ffn_act_bwd(u, d_a, conv_w, conv_b, name):
    def body(u_ref, da_ref, w_ref, b_ref, du_ref, dw_ref, db_ref):
        ug, uv = u_ref[0], u_ref[1]
        yg = _conv(ug, w_ref, 0, b_ref[0:1, :])
        yv = _conv(uv, w_ref, 1, b_ref[1:2, :])
        inner = GELU_K * (yg + GELU_C * yg * yg * yg)
        th = jnp.tanh(inner)
        gelu = 0.5 * yg * (1.0 + th)
        dgelu = 0.5 * (1.0 + th) + 0.5 * yg * (1.0 - th * th) * GELU_K * (1.0 + 3.0 * GELU_C * yg * yg)
        da = da_ref[...]
        for half, (uu, dy) in enumerate(((ug, da * yv * dgelu), (uv, da * gelu))):
            du = (w_ref[2:3, half, :] * dy + w_ref[1:2, half, :] * _shift_up(dy, 1)
                  + w_ref[0:1, half, :] * _shift_up(dy, 2))
            du_ref[half] = du.astype(BF16)
            dw_ref[0:1, half, :] = jnp.sum(dy * _shift_down(uu, 2), axis=0, keepdims=True)
            dw_ref[1:2, half, :] = jnp.sum(dy * _shift_down(uu, 1), axis=0, keepdims=True)
            dw_ref[2:3, half, :] = jnp.sum(dy * uu, axis=0, keepdims=True)
            db_ref[half:half + 1, :] = jnp.sum(dy, axis=0, keepdims=True)

    return pl.pallas_call(
        body, grid=(D_FF // FC,),
        in_specs=[pl.BlockSpec((2, S, FC), lambda j: (0, 0, j)), pl.BlockSpec((S, FC), lambda j: (0, j)),
                  pl.BlockSpec((3, 2, FC), lambda j: (0, 0, j)), pl.BlockSpec((2, FC), lambda j: (0, j))],
        out_specs=[pl.BlockSpec((2, S, FC), lambda j: (0, 0, j)), pl.BlockSpec((3, 2, FC), lambda j: (0, 0, j)),
                   pl.BlockSpec((2, FC), lambda j: (0, j))],
        out_shape=[jax.ShapeDtypeStruct((2, S, D_FF), BF16), jax.ShapeDtypeStruct((3, 2, D_FF), F32),
                   jax.ShapeDtypeStruct((2, D_FF), F32)],
        compiler_params=_cp("parallel"), name=name)(u, d_a, conv_w, conv_b)


def _layer_fwd(x, h1, w, bias, lname):
    n = lambda s: f"{lname}_{s}"
    tn = 768
    proj = _mm(h1, w["w_in"], grid=(S // 1024, QKV_COLS // tn, 1),
               a_spec=pl.BlockSpec((1024, D), lambda i, j, k: (i, 0)),
               b_spec=pl.BlockSpec((D, tn), lambda i, j, k: (0, j)),
               out_shape=jax.ShapeDtypeStruct((QKV_SLABS, S, LANES), F32),
               out_spec=pl.BlockSpec((tn // LANES, 1024, LANES), lambda i, j, k: (j, i, 0)),
               ca=1, cb=0, acc_shape=(1024, tn), out_slab=True, name=n("proj_qkv"))
    gates = _mm(h1, w["w_in"], grid=(S // 1024, GATE_COLS // tn, 1),
                a_spec=pl.BlockSpec((1024, D), lambda i, j, k: (i, 0)),
                b_spec=pl.BlockSpec((D, tn), lambda i, j, k: (0, j + QKV_COLS // tn)),
                out_shape=jax.ShapeDtypeStruct((S, GATE_COLS), F32),
                out_spec=pl.BlockSpec((1024, tn), lambda i, j, k: (i, j)),
                ca=1, cb=0, acc_shape=(1024, tn), name=n("proj_gate"))
    nums, stats = [], []
    for g, (_, d) in enumerate(A_GROUPS):
        nm, st = _band_fwd(proj, bias, d=d, q0=2 * g, k0=6 + 2 * g, v0=12 + 2 * g, npairs=2, bias0=2 * g,
                           shared_kv=False, name=n(f"attn_a{g}_fwd"))
        nums.append(nm)
        stats.append(st)
    o_a, lse_a = _combine_a(nums, stats, n("attn_a_combine"))
    nm_b, st_b = _band_fwd(proj, bias, d=1, q0=18, k0=22, v0=23, npairs=4, bias0=6, shared_kv=True, name=n("attn_b_fwd"))
    o_b, lse_b = _combine_b(nm_b, st_b, w["sinks"], n("attn_b_combine"))
    o_c, tot_c = _stick_fwd(proj, q0=24, k0=26, v0=28, name=n("attn_c_fwd"))
    merged, mo = _merge_fwd(o_a, o_b, o_c, gates, w["b_gate"], w["w_br_a"], w["w_br_b"], w["w_br_c"], w["w_out"], n("merge_fwd"))
    x2, h2 = _postnorm_res(x, mo, w["attn_post_norm"], w["ffn_pre_norm"], n("attn_post"))
    u = _mm(h2, w["w_up"], grid=(S // 1024, 2 * D_FF // 1024, 1),
            a_spec=pl.BlockSpec((1024, D), lambda i, j, k: (i, 0)),
            b_spec=pl.BlockSpec((D, 1024), lambda i, j, k: (0, j)),
            out_shape=jax.ShapeDtypeStruct((2, S, D_FF), F32),
            out_spec=pl.BlockSpec((None, 1024, 1024), lambda i, j, k: (j // 4, i, j % 4)),
            ca=1, cb=0, acc_shape=(1024, 1024), name=n("ffn_up"))
    a = _ffn_act(u, w["conv_w"], w["conv_b"], n("ffn_act"))
    fo = _mm_nn(a, w["w_down"], F32, 1024, 1024, 1024, n("ffn_down"))
    saved = dict(x=x, h1=h1, proj=proj, gates=gates, o_a=o_a, lse_a=lse_a, o_b=o_b, lse_b=lse_b, o_c=o_c, tot_c=tot_c,
                 merged=merged, mo=mo, x2=x2, h2=h2, u=u, a=a, fo=fo)
    return saved


def _layer_bwd(dx3, sv, w, bias, lname):
    n = lambda s: f"{lname}_{s}"
    g = {}
    d_fo, g["ffn_post_norm"] = _norm_bwd(sv["fo"], w["ffn_post_norm"], [dx3], None, BF16, n("ffn_post_bwd"))
    d_a = _mm_nt(d_fo, w["w_down"], F32, 1024, 1024, 1024, n("ffn_down_bwd_x"))
    g["w_down"] = _mm_tn(sv["a"], d_fo, BF16, 1024, 1024, 1024, n("ffn_down_bwd_w"))
    d_u, dcw, dcb = _ffn_act_bwd(sv["u"], d_a, w["conv_w"], w["conv_b"], n("ffn_act_bwd"))
    g["conv_w"] = dcw.reshape(3, 2 * D_FF)
    g["conv_b"] = dcb.reshape(1, 2 * D_FF)
    g["w_up"] = _mm(sv["h2"], d_u, grid=(1, 2 * D_FF // 1024, S // 1024),
                    a_spec=pl.BlockSpec((1024, D), lambda i, j, k: (k, 0)),
                    b_spec=pl.BlockSpec((None, 1024, 1024), lambda i, j, k: (j // 4, k, j % 4)),
                    out_shape=jax.ShapeDtypeStruct((D, 2 * D_FF), BF16),
                    out_spec=pl.BlockSpec((D, 1024), lambda i, j, k: (0, j)),
                    ca=0, cb=0, acc_shape=(D, 1024), name=n("ffn_up_bwd_w"))
    d_h2 = _mm(d_u, w["w_up"], grid=(S // 1024, 1, 2 * D_FF // 1024),
               a_spec=pl.BlockSpec((None, 1024, 1024), lambda i, j, k: (k // 4, i, k % 4)),
               b_spec=pl.BlockSpec((D, 1024), lambda i, j, k: (0, k)),
               out_shape=jax.ShapeDtypeStruct((S, D), F32),
               out_spec=pl.BlockSpec((1024, D), lambda i, j, k: (i, 0)),
               ca=1, cb=1, acc_shape=(1024, D), name=n("ffn_up_bwd_x"))
    dx2, g["ffn_pre_norm"] = _norm_bwd(sv["x2"], w["ffn_pre_norm"], [d_h2], dx3, F32, n("ffn_pre_bwd"))
    d_mo, g["attn_post_norm"] = _norm_bwd(sv["mo"], w["attn_post_norm"], [dx2], None, BF16, n("attn_post_bwd"))
    g["w_out"] = _mm_tn(sv["merged"], d_mo, BF16, 1024, 1024, 1024, n("out_bwd_w"))
    do_a, do_b, do_c, d_gates, dwa, dwb, dwc, g["b_gate"] = _merge_bwd(
        d_mo, sv["o_a"], sv["o_b"], sv["o_c"], sv["gates"], w["b_gate"], w["w_br_a"], w["w_br_b"], w["w_br_c"],
        w["w_out"], n("merge_bwd"))
    g["w_br_a"], g["w_br_b"], g["w_br_c"] = dwa, dwb, dwc
    proj = sv["proj"]
    dqa, dka, dva, gbias = [], [], [], []
    for gi, (_, d) in enumerate(A_GROUPS):
        dq, dk, dv, gg, _ = _band_bwd(proj, bias, sv["o_a"], do_a, sv["lse_a"], w["sinks"], d=d, q0=2 * gi, k0=6 + 2 * gi,
                                      v0=12 + 2 * gi, npairs=2, bias0=2 * gi, shared_kv=False, name=n(f"attn_a{gi}_bwd"))
        dqa.append(dq), dka.append(dk), dva.append(dv), gbias.append(gg)
    dqb, dkb, dvb, ggb, dsink = _band_bwd(proj, bias, sv["o_b"], do_b, sv["lse_b"], w["sinks"], d=1, q0=18, k0=22, v0=23,
                                          npairs=4, bias0=6, shared_kv=True, name=n("attn_b_bwd"))
    gbias.append(ggb)
    g["bias_g"] = jnp.concatenate(gbias, axis=0).reshape(N_BIAS_HEADS, BLK, 2 * BLK)
    g["sinks"] = dsink[:, 0, :2].reshape(1, 8)
    dqc, dkc, dvc = _stick_bwd(proj, do_c, sv["tot_c"], q0=24, k0=26, v0=28, name=n("attn_c_bwd"))
    dqkv = jnp.concatenate(dqa + dka + dva + [dqb, dkb, dvb, dqc, dkc, dvc], axis=0)
    ts = 6
    d_h1a = _mm(dqkv, w["w_in"], grid=(S // 1024, 1, QKV_SLABS // ts),
                a_spec=pl.BlockSpec((ts, 1024, LANES), lambda i, j, k: (k, i, 0)),
                b_spec=pl.BlockSpec((D, ts * LANES), lambda i, j, k: (0, k)),
                out_shape=jax.ShapeDtypeStruct((S, D), F32),
                out_spec=pl.BlockSpec((1024, D), lambda i, j, k: (i, 0)),
                ca=1, cb=1, acc_shape=(1024, D), a_slab=True, name=n("in_bwd_x_qkv"))
    d_h1b = _mm(d_gates, w["w_in"], grid=(S // 1024, 1, GATE_COLS // 768),
                a_spec=pl.BlockSpec((1024, 768), lambda i, j, k: (i, k)),
                b_spec=pl.BlockSpec((D, 768), lambda i, j, k: (0, k + QKV_COLS // 768)),
                out_shape=jax.ShapeDtypeStruct((S, D), F32),
                out_spec=pl.BlockSpec((1024, D), lambda i, j, k: (i, 0)),
                ca=1, cb=1, acc_shape=(1024, D), name=n("in_bwd_x_gate"))
    dw_in = _mm(sv["h1"], dqkv, grid=(1, QKV_SLABS // ts, S // 1024),
                a_spec=pl.BlockSpec((1024, D), lambda i, j, k: (k, 0)),
                b_spec=pl.BlockSpec((ts, 1024, LANES), lambda i, j, k: (j, k, 0)),
                out_shape=jax.ShapeDtypeStruct((D, IN_COLS), BF16),
                out_spec=pl.BlockSpec((D, ts * LANES), lambda i, j, k: (0, j)),
                ca=0, cb=0, acc_shape=(D, ts * LANES), b_slab=True, name=n("in_bwd_w_qkv"))
    g["w_in"] = _mm(sv["h1"], d_gates, grid=(1, GATE_COLS // 768, S // 1024),
                    a_spec=pl.BlockSpec((1024, D), lambda i, j, k: (k, 0)),
                    b_spec=pl.BlockSpec((1024, 768), lambda i, j, k: (k, j)),
                    out_shape=jax.ShapeDtypeStruct((D, IN_COLS), BF16),
                    out_spec=pl.BlockSpec((D, 768), lambda i, j, k: (0, j + QKV_COLS // 768)),
                    ca=0, cb=0, acc_shape=(D, 768), alias_out=dw_in, name=n("in_bwd_w_gate"))
    dx, g["attn_pre_norm"] = _norm_bwd(sv["x"], w["attn_pre_norm"], [d_h1a, d_h1b], dx2, F32, n("attn_pre_bwd"))
    return dx, g


def _local_step(x, target, ws, rel_bias):
    buckets = jnp.asarray(_bucket_tiles())
    bias = _bias_tiles(rel_bias, buckets, "bias_tiles").reshape(N_BIAS_HEADS // 2, 2, 2, BLK, 2 * BLK)
    saved = []
    h1 = _prenorm(x, ws[0]["attn_pre_norm"], "l0_attn_pre")
    for l in range(DEPTH):
        sv = _layer_fwd(x, h1, ws[l], bias, f"l{l}")
        saved.append(sv)
        g_next = ws[l + 1]["attn_pre_norm"] if l + 1 < DEPTH else ws[l]["attn_pre_norm"]
        x, h1 = _postnorm_res(sv["x2"], sv["fo"], ws[l]["ffn_post_norm"], g_next, f"l{l}_ffn_post")
    dy, loss_tile = _loss_head(x, target, "loss_head")
    grads = [None] * DEPTH
    for l in reversed(range(DEPTH)):
        dy, grads[l] = _layer_bwd(dy, saved[l], ws[l], bias, f"l{l}")
    g_rel = _bias_grad([grads[l]["bias_g"] for l in range(DEPTH)], buckets, "bias_grad")[:, :N_BIAS_HEADS]
    return loss_tile[0, 0], dy, grads, g_rel


def _coords():
    return lax.axis_index("x"), lax.axis_index("y"), lax.axis_index("c")


def _peer(rel):
    x, y, c = _coords()
    return (1 - x if rel & 4 else x, 1 - y if rel & 2 else y, 1 - c if rel & 1 else c)


def _exchange(srcs, dst_shapes, src_win, dst_win, name):
    nt = len(srcs)

    def body(*refs):
        src_refs, dst_refs = refs[:nt], refs[nt:2 * nt]
        send_sems, recv_sems, local_sems = refs[2 * nt:]
        x, y, c = _coords()
        me = 4 * x + 2 * y + c
        locals_ = []
        for t in range(nt):
            cp = pltpu.make_async_copy(src_win(t, src_refs[t], me), dst_win(t, dst_refs[t], me), local_sems.at[t])
            cp.start()
            locals_.append(cp)
        sends = []
        for rel in range(1, NDEV):
            px, py, pc = _peer(rel)
            q = 4 * px + 2 * py + pc
            for t in range(nt):
                cp = pltpu.make_async_remote_copy(
                    src_ref=src_win(t, src_refs[t], q), dst_ref=dst_win(t, dst_refs[t], me),
                    send_sem=send_sems.at[rel - 1, t], recv_sem=recv_sems.at[rel - 1, t],
                    device_id=(px, py, pc), device_id_type=MESH)
                cp.start()
                sends.append(cp)
        for rel in range(1, NDEV):
            px, py, pc = _peer(rel)
            q = 4 * px + 2 * py + pc
            for t in range(nt):
                pltpu.make_async_remote_copy(
                    src_ref=src_win(t, src_refs[t], me), dst_ref=dst_win(t, dst_refs[t], q),
                    send_sem=send_sems.at[rel - 1, t], recv_sem=recv_sems.at[rel - 1, t],
                    device_id=(px, py, pc), device_id_type=MESH).wait_recv()
        for cp in sends:
            cp.wait_send()
        for cp in locals_:
            cp.wait()

    return pl.pallas_call(
        body, in_specs=[ANY] * nt, out_specs=[ANY] * nt, out_shape=dst_shapes,
        scratch_shapes=[pltpu.SemaphoreType.DMA((NDEV - 1, nt)), pltpu.SemaphoreType.DMA((NDEV - 1, nt)),
                        pltpu.SemaphoreType.DMA((nt,))],
        name=name)(*srcs)


BIG = (("w_in", 1, 864), ("w_br_a", 1, 128), ("w_br_b", 1, 128), ("w_br_c", 1, 128), ("w_out", 0, 128),
       ("w_up", 1, 1024), ("w_down", 0, 512))


def _gather_weights(shards):
    shapes = []
    for (nm, ax, ext), s in zip(BIG, shards):
        if nm == "w_in":
            shapes.append(jax.ShapeDtypeStruct((NDEV,) + s.shape, BF16))
        else:
            full = list(s.shape)
            full[1 + ax] = ext * NDEV
            shapes.append(jax.ShapeDtypeStruct(tuple(full), BF16))

    def src_win(t, ref, q):
        return ref

    def dst_win(t, ref, k):
        nm, ax, ext = BIG[t]
        if nm == "w_in":
            return ref.at[k]
        off = pl.multiple_of(k * ext, ext)
        if ax == 0:
            return ref.at[:, pl.ds(off, ext), :]
        return ref.at[:, :, pl.ds(off, ext)]

    return _exchange(list(shards), shapes, src_win, dst_win, "gather_weights")


def _scatter_grads(full_grads, small):
    arrs = list(full_grads) + [small]
    shapes = []
    for (nm, ax, ext), gfull in zip(BIG, full_grads):
        if nm == "w_in":
            shp = (NDEV, DEPTH, D, ext)
        else:
            shp = list(gfull.shape)
            shp[1 + ax] = ext
            shp = (NDEV,) + tuple(shp)
        shapes.append(jax.ShapeDtypeStruct(shp, BF16))
    shapes.append(jax.ShapeDtypeStruct((NDEV,) + small.shape, F32))

    def src_win(t, ref, q):
        if t == len(BIG):
            return ref
        nm, ax, ext = BIG[t]
        if nm == "w_in":
            return ref.at[:, q]
        off = pl.multiple_of(q * ext, ext)
        if ax == 0:
            return ref.at[:, pl.ds(off, ext), :]
        return ref.at[:, :, pl.ds(off, ext)]

    def dst_win(t, ref, k):
        return ref.at[k]

    return _exchange(arrs, shapes, src_win, dst_win, "scatter_grads")


def _adamw_math(w, g, m, v):
    m2 = ADAM_B1 * m + (1.0 - ADAM_B1) * g
    v2 = ADAM_B2 * v + (1.0 - ADAM_B2) * (g * g)
    m_hat = m2 / (1.0 - ADAM_B1 ** ADAM_STEP)
    v_hat = v2 / (1.0 - ADAM_B2 ** ADAM_STEP)
    delta = -ADAM_LR * (m_hat / (jnp.sqrt(v_hat) + ADAM_EPS) + ADAM_WD * w)
    return delta, m2, v2


def _adamw(parts, w, m, v, rows, name):
    nl, nr, nc = w.shape

    def body(p_ref, w_ref, m_ref, v_ref, g_ref, d_ref, m2_ref, v2_ref):
        g = p_ref[0].astype(F32)
        for k in range(1, NDEV):
            g = g + p_ref[k].astype(F32)
        delta, m2, v2 = _adamw_math(w_ref[...], g, m_ref[...], v_ref[...])
        g_ref[...] = g
        d_ref[...] = delta
        m2_ref[...] = m2
        v2_ref[...] = v2

    blk = pl.BlockSpec((None, rows, nc), lambda l, i: (l, i, 0))
    pblk = pl.BlockSpec((NDEV, None, rows, nc), lambda l, i: (0, l, i, 0))
    return pl.pallas_call(
        body, grid=(nl, nr // rows), in_specs=[pblk, blk, blk, blk], out_specs=[blk] * 4,
        out_shape=[jax.ShapeDtypeStruct(w.shape, F32)] * 4,
        compiler_params=_cp("parallel", "parallel"), name=name)(parts, w, m, v)


SMALL_REPL = (("rel_bias", NUM_BUCKETS * N_BIAS_HEADS), ("attn_pre_norm", DEPTH * D), ("sinks", DEPTH * 8),
              ("attn_post_norm", DEPTH * D), ("ffn_pre_norm", DEPTH * D), ("conv_b", DEPTH * 2 * D_FF),
              ("ffn_post_norm", DEPTH * D))
SMALL_SHARD = (("b_gate", (DEPTH, 3, D), 128), ("conv_w", (DEPTH, 3, 2 * D_FF), 1024))


def _pack(vecs):
    flat = jnp.concatenate([v.reshape(-1).astype(F32) for v in vecs])
    n = flat.shape[0]
    rows = -(-n // (8 * LANES)) * 8
    return jnp.pad(flat, (0, rows * LANES - n)).reshape(rows, LANES)


def _unpack(packed, sizes):
    flat = packed.reshape(-1)
    out, off = [], 0
    for sz in sizes:
        out.append(flat[off:off + sz])
        off += sz
    return out


def _small_sum(parts, name):
    r = parts.shape[1]

    def body(p_ref, o_ref):
        g = p_ref[0]
        for k in range(1, NDEV):
            g = g + p_ref[k]
        o_ref[...] = g

    return pl.pallas_call(
        body, in_specs=[pl.BlockSpec(memory_space=pltpu.VMEM)], out_specs=pl.BlockSpec(memory_space=pltpu.VMEM),
        out_shape=jax.ShapeDtypeStruct((r, LANES), F32), name=name)(parts)


def _small_adamw(g, w, m, v, name):
    def body(g_ref, w_ref, m_ref, v_ref, d_ref, m2_ref, v2_ref):
        delta, m2, v2 = _adamw_math(w_ref[...], g_ref[...], m_ref[...], v_ref[...])
        d_ref[...] = delta
        m2_ref[...] = m2
        v2_ref[...] = v2

    vm = pl.BlockSpec(memory_space=pltpu.VMEM)
    return pl.pallas_call(
        body, in_specs=[vm] * 4, out_specs=[vm] * 3,
        out_shape=[jax.ShapeDtypeStruct(g.shape, F32)] * 3, name=name)(g, w, m, v)


def kernel(x, rel_bias, attn_pre_norm, w_in, b_gate, sinks, w_br_a, w_br_b, w_br_c, w_out, attn_post_norm, ffn_pre_norm, w_up, conv_w, conv_b, w_down, ffn_post_norm, loss_target, m_rel_bias, m_attn_pre_norm, m_w_in, m_b_gate, m_sinks, m_w_br_a, m_w_br_b, m_w_br_c, m_w_out, m_attn_post_norm, m_ffn_pre_norm, m_w_up, m_conv_w, m_conv_b, m_w_down, m_ffn_post_norm, v_rel_bias, v_attn_pre_norm, v_w_in, v_b_gate, v_sinks, v_w_br_a, v_w_br_b, v_w_br_c, v_w_out, v_attn_post_norm, v_ffn_pre_norm, v_w_up, v_conv_w, v_conv_b, v_w_down, v_ffn_post_norm):
    P = dict(rel_bias=rel_bias, attn_pre_norm=attn_pre_norm, w_in=w_in, b_gate=b_gate, sinks=sinks, w_br_a=w_br_a,
             w_br_b=w_br_b, w_br_c=w_br_c, w_out=w_out, attn_post_norm=attn_post_norm, ffn_pre_norm=ffn_pre_norm,
             w_up=w_up, conv_w=conv_w, conv_b=conv_b, w_down=w_down, ffn_post_norm=ffn_post_norm)
    M = dict(rel_bias=m_rel_bias, attn_pre_norm=m_attn_pre_norm, w_in=m_w_in, b_gate=m_b_gate, sinks=m_sinks,
             w_br_a=m_w_br_a, w_br_b=m_w_br_b, w_br_c=m_w_br_c, w_out=m_w_out, attn_post_norm=m_attn_post_norm,
             ffn_pre_norm=m_ffn_pre_norm, w_up=m_w_up, conv_w=m_conv_w, conv_b=m_conv_b, w_down=m_w_down,
             ffn_post_norm=m_ffn_post_norm)
    V = dict(rel_bias=v_rel_bias, attn_pre_norm=v_attn_pre_norm, w_in=v_w_in, b_gate=v_b_gate, sinks=v_sinks,
             w_br_a=v_w_br_a, w_br_b=v_w_br_b, w_br_c=v_w_br_c, w_out=v_w_out, attn_post_norm=v_attn_post_norm,
             ffn_pre_norm=v_ffn_pre_norm, w_up=v_w_up, conv_w=v_conv_w, conv_b=v_conv_b, w_down=v_w_down,
             ffn_post_norm=v_ffn_post_norm)
    xi, yi, ci = _coords()
    me = 4 * xi + 2 * yi + ci

    gathered = _gather_weights([P[nm].astype(BF16) for nm, _, _ in BIG])
    G = dict(zip([nm for nm, _, _ in BIG], gathered))
    w_in_full = jnp.transpose(G["w_in"], (1, 2, 0, 3)).reshape(DEPTH, D, IN_COLS)
    small_w = _pack([b_gate.reshape(-1), conv_w.reshape(-1)])
    (small_w_all,) = _exchange([small_w], [jax.ShapeDtypeStruct((NDEV,) + small_w.shape, F32)],
                               lambda t, ref, q: ref, lambda t, ref, k: ref.at[k], "gather_small_weights")
    nbg = DEPTH * 3 * 128
    ncw = DEPTH * 3 * 1024
    flat_all = small_w_all.reshape(NDEV, -1)
    b_gate_full = jnp.transpose(flat_all[:, :nbg].reshape(NDEV, DEPTH, 3, 128), (1, 2, 0, 3)).reshape(DEPTH, 3, D)
    conv_w_full = jnp.transpose(flat_all[:, nbg:nbg + ncw].reshape(NDEV, DEPTH, 3, 1024), (1, 2, 0, 3)).reshape(DEPTH, 3, 2 * D_FF)

    ws = []
    for l in range(DEPTH):
        ws.append(dict(
            w_in=w_in_full[l], w_br_a=G["w_br_a"][l], w_br_b=G["w_br_b"][l], w_br_c=G["w_br_c"][l], w_out=G["w_out"][l],
            w_up=G["w_up"][l], w_down=G["w_down"][l],
            b_gate=b_gate_full[l], conv_w=conv_w_full[l].reshape(3, 2, D_FF), conv_b=conv_b[l].reshape(2, D_FF),
            sinks=sinks[l].reshape(1, 8),
            attn_pre_norm=attn_pre_norm[l].reshape(1, D), attn_post_norm=attn_post_norm[l].reshape(1, D),
            ffn_pre_norm=ffn_pre_norm[l].reshape(1, D), ffn_post_norm=ffn_post_norm[l].reshape(1, D)))

    loss_local, grad_x, grads, g_rel = _local_step(x[0], loss_target[0], ws, rel_bias)
    loss = lax.psum(loss_local, ("x", "y", "c"))

    stack = lambda nm: jnp.stack([grads[l][nm] for l in range(DEPTH)], axis=0)
    full = []
    for nm, ax, ext in BIG:
        gfull = stack(nm).astype(BF16)
        if nm == "w_in":
            gfull = jnp.transpose(gfull.reshape(DEPTH, D, NDEV, ext), (0, 2, 1, 3))
        full.append(gfull)
    small_names = [nm for nm, _ in SMALL_REPL] + [nm for nm, _, _ in SMALL_SHARD]
    small_g = {"rel_bias": g_rel}
    for nm in small_names[1:]:
        small_g[nm] = stack(nm)
    small_packed = _pack([small_g[nm] for nm in small_names])
    *parts, small_parts = _scatter_grads(full, small_packed)
    small_tot = _small_sum(small_parts, "small_grad_sum")
    sizes = [sz for _, sz in SMALL_REPL] + [int(np.prod(shp)) for _, shp, _ in SMALL_SHARD]
    small_tot = dict(zip(small_names, _unpack(small_tot, sizes)))

    out_g, out_d, out_m, out_v = {}, {}, {}, {}
    for (nm, ax, ext), prt in zip(BIG, parts):
        w_l = P[nm]
        rows = {"w_in": 256, "w_up": 256, "w_down": 256}.get(nm, w_l.shape[1])
        out_g[nm], out_d[nm], out_m[nm], out_v[nm] = _adamw(prt, w_l, M[nm], V[nm], rows, f"adamw_{nm}")
    gsm = {}
    for nm, _ in SMALL_REPL:
        gsm[nm] = small_tot[nm].reshape(P[nm].shape)
    for nm, shp, ext in SMALL_SHARD:
        gsm[nm] = lax.dynamic_slice_in_dim(small_tot[nm].reshape(shp), me * ext, ext, axis=2)
    pk = lambda dct: _pack([dct[nm] for nm in small_names])
    d_s, m_s, v_s = _small_adamw(pk(gsm), pk(P), pk(M), pk(V), "adamw_small")
    szs = [int(np.prod(P[nm].shape)) for nm in small_names]
    for dst, packed in ((out_d, d_s), (out_m, m_s), (out_v, v_s)):
        for nm, piece in zip(small_names, _unpack(packed, szs)):
            dst[nm] = piece.reshape(P[nm].shape)
    for nm in small_names:
        out_g[nm] = gsm[nm]

    order = ["rel_bias", "attn_pre_norm", "w_in", "b_gate", "sinks", "w_br_a", "w_br_b", "w_br_c", "w_out",
             "attn_post_norm", "ffn_pre_norm", "w_up", "conv_w", "conv_b", "w_down", "ffn_post_norm"]
    return (loss, grad_x[None], *[out_g[k] for k in order], *[out_d[k] for k in order],
            *[out_m[k] for k in order], *[out_v[k] for k in order])
```

```python
import functools
import math

import numpy as np
import jax
import jax.numpy as jnp
from jax import lax
from jax.experimental import pallas as pl
from jax.experimental.pallas import tpu as pltpu

F32 = jnp.float32
BF16 = jnp.bfloat16

S = 2048
D = 1024
DEPTH = 2
NDEV = 8
HD = 64
BLK = 128
NB = S // BLK
A_GROUPS = ((128, 1), (512, 4), (2048, 16))
NUM_BUCKETS = 32
MAX_DISTANCE = 2048
N_BIAS_HEADS = 20
D_FF = 4096
IN_COLS = 6912
QKV_COLS = 3840
QKV_SLABS = QKV_COLS // 128
GATE_COLS = 3072
EPS = 1e-6
SCALE = HD ** -0.5
NEG = -1e30
LANES = 128

ADAM_LR = 0.001
ADAM_B1 = 0.9
ADAM_B2 = 0.999
ADAM_EPS = 1e-08
ADAM_WD = 0.01
ADAM_STEP = 10

VMEM_LIMIT = 56 * 1024 * 1024
MESH = pl.DeviceIdType.MESH
ANY = pl.BlockSpec(memory_space=pl.ANY)
SMEM = pl.BlockSpec(memory_space=pltpu.SMEM)


def _cp(*sem):
    return pltpu.CompilerParams(dimension_semantics=sem if sem else None, vmem_limit_bytes=VMEM_LIMIT)


def _dot(a, b, ca, cb):
    return lax.dot_general(a, b, (((ca,), (cb,)), ((), ())), preferred_element_type=F32)


def _mm(a, b, *, grid, a_spec, b_spec, out_shape, out_spec, ca, cb, acc_shape, name,
        a_slab=False, b_slab=False, out_slab=False, alias_out=None):
    nk = grid[2]

    def body(*refs):
        if alias_out is not None:
            a_ref, b_ref, _, o_ref, acc_ref = refs
        else:
            a_ref, b_ref, o_ref, acc_ref = refs
        k = pl.program_id(2)

        @pl.when(k == 0)
        def _():
            acc_ref[...] = jnp.zeros(acc_shape, F32)

        def load(ref, slab):
            if slab:
                return jnp.concatenate([ref[s] for s in range(ref.shape[0])], axis=1).astype(BF16)
            return ref[...].astype(BF16)

        acc_ref[...] += _dot(load(a_ref, a_slab), load(b_ref, b_slab), ca, cb)

        @pl.when(k == nk - 1)
        def _():
            if out_slab:
                for s in range(o_ref.shape[0]):
                    o_ref[s] = acc_ref[:, s * LANES:(s + 1) * LANES].astype(o_ref.dtype)
            else:
                o_ref[...] = acc_ref[...].astype(o_ref.dtype)

    in_specs = [a_spec, b_spec]
    args = [a, b]
    aliases = {}
    if alias_out is not None:
        in_specs.append(ANY)
        args.append(alias_out)
        aliases = {2: 0}
    return pl.pallas_call(
        body, grid=grid, in_specs=in_specs, out_specs=out_spec, out_shape=out_shape,
        scratch_shapes=[pltpu.VMEM(acc_shape, F32)], input_output_aliases=aliases,
        compiler_params=_cp("parallel", "parallel", "arbitrary"), name=name)(*args)


def _mm_nn(a, b, out_dtype, tm, tn, tk, name):
    m, kk = a.shape
    n = b.shape[1]
    return _mm(a, b, grid=(m // tm, n // tn, kk // tk),
               a_spec=pl.BlockSpec((tm, tk), lambda i, j, k: (i, k)),
               b_spec=pl.BlockSpec((tk, tn), lambda i, j, k: (k, j)),
               out_shape=jax.ShapeDtypeStruct((m, n), out_dtype),
               out_spec=pl.BlockSpec((tm, tn), lambda i, j, k: (i, j)),
               ca=1, cb=0, acc_shape=(tm, tn), name=name)


def _mm_nt(a, b, out_dtype, tm, tn, tk, name):
    m, kk = a.shape
    n = b.shape[0]
    return _mm(a, b, grid=(m // tm, n // tn, kk // tk),
               a_spec=pl.BlockSpec((tm, tk), lambda i, j, k: (i, k)),
               b_spec=pl.BlockSpec((tn, tk), lambda i, j, k: (j, k)),
               out_shape=jax.ShapeDtypeStruct((m, n), out_dtype),
               out_spec=pl.BlockSpec((tm, tn), lambda i, j, k: (i, j)),
               ca=1, cb=1, acc_shape=(tm, tn), name=name)


def _mm_tn(a, b, out_dtype, tm, tn, tk, name):
    kk, m = a.shape
    n = b.shape[1]
    return _mm(a, b, grid=(m // tm, n // tn, kk // tk),
               a_spec=pl.BlockSpec((tk, tm), lambda i, j, k: (k, i)),
               b_spec=pl.BlockSpec((tk, tn), lambda i, j, k: (k, j)),
               out_shape=jax.ShapeDtypeStruct((m, n), out_dtype),
               out_spec=pl.BlockSpec((tm, tn), lambda i, j, k: (i, j)),
               ca=0, cb=0, acc_shape=(tm, tn), name=name)


ROW_TILE = 256


def _rms(x, g):
    r = lax.rsqrt(jnp.mean(x * x, axis=-1, keepdims=True) + EPS)
    return x * r * g


def _prenorm(x, g, name):
    def body(x_ref, g_ref, o_ref):
        o_ref[...] = _rms(x_ref[...], g_ref[...]).astype(BF16)

    return pl.pallas_call(
        body, grid=(S // ROW_TILE,),
        in_specs=[pl.BlockSpec((ROW_TILE, D), lambda i: (i, 0)), pl.BlockSpec((1, D), lambda i: (0, 0))],
        out_specs=pl.BlockSpec((ROW_TILE, D), lambda i: (i, 0)),
        out_shape=jax.ShapeDtypeStruct((S, D), BF16), compiler_params=_cp("parallel"), name=name)(x, g)


def _postnorm_res(x, f, g_post, g_next, name):
    def body(x_ref, f_ref, gp_ref, gn_ref, xo_ref, ho_ref):
        xn = x_ref[...] + _rms(f_ref[...], gp_ref[...])
        xo_ref[...] = xn
        ho_ref[...] = _rms(xn, gn_ref[...]).astype(BF16)

    row = pl.BlockSpec((ROW_TILE, D), lambda i: (i, 0))
    vec = pl.BlockSpec((1, D), lambda i: (0, 0))
    return pl.pallas_call(
        body, grid=(S // ROW_TILE,), in_specs=[row, row, vec, vec], out_specs=[row, row],
        out_shape=[jax.ShapeDtypeStruct((S, D), F32), jax.ShapeDtypeStruct((S, D), BF16)],
        compiler_params=_cp("parallel"), name=name)(x, f, g_post, g_next)


def _norm_bwd(f, g, dys, res, out_dtype, name):
    ndy = len(dys)
    has_res = res is not None

    def body(*refs):
        f_ref, g_ref = refs[0], refs[1]
        dy_refs = refs[2:2 + ndy]
        res_ref = refs[2 + ndy] if has_res else None
        o_ref, dg_ref = refs[-2], refs[-1]
        fv = f_ref[...]
        dy = dy_refs[0][...].astype(F32)
        for r in dy_refs[1:]:
            dy = dy + r[...].astype(F32)
        r = lax.rsqrt(jnp.mean(fv * fv, axis=-1, keepdims=True) + EPS)
        n = fv * r
        dn = dy * g_ref[...]
        df = r * (dn - n * jnp.mean(dn * n, axis=-1, keepdims=True))
        if has_res:
            df = df + res_ref[...]
        o_ref[...] = df.astype(out_dtype)

        @pl.when(pl.program_id(0) == 0)
        def _():
            dg_ref[...] = jnp.zeros((1, D), F32)

        dg_ref[...] += jnp.sum(dy * n, axis=0, keepdims=True)

    row = pl.BlockSpec((ROW_TILE, D), lambda i: (i, 0))
    vec = pl.BlockSpec((1, D), lambda i: (0, 0))
    in_specs = [row, vec] + [row] * ndy + ([row] if has_res else [])
    args = [f, g] + list(dys) + ([res] if has_res else [])
    return pl.pallas_call(
        body, grid=(S // ROW_TILE,), in_specs=in_specs, out_specs=[row, vec],
        out_shape=[jax.ShapeDtypeStruct((S, D), out_dtype), jax.ShapeDtypeStruct((1, D), F32)],
        compiler_params=_cp("arbitrary"), name=name)(*args)


def _loss_head(y, target, name):
    def body(y_ref, t_ref, dy_ref, l_ref):
        e = y_ref[...] - t_ref[...]
        dy_ref[...] = e * (1.0 / D)

        @pl.when(pl.program_id(0) == 0)
        def _():
            l_ref[...] = jnp.zeros((8, LANES), F32)

        l_ref[...] += jnp.sum(e * e) * (0.5 / D)

    row = pl.BlockSpec((ROW_TILE, D), lambda i: (i, 0))
    return pl.pallas_call(
        body, grid=(S // ROW_TILE,), in_specs=[row, row],
        out_specs=[row, pl.BlockSpec((8, LANES), lambda i: (0, 0))],
        out_shape=[jax.ShapeDtypeStruct((S, D), F32), jax.ShapeDtypeStruct((8, LANES), F32)],
        compiler_params=_cp("arbitrary"), name=name)(y, target)


def _bucket_tiles():
    a = np.arange(BLK)[:, None]
    b = np.arange(2 * BLK)[None, :]
    dist = a + BLK - b
    out = np.zeros((4, 2, BLK, 2 * BLK), np.int32)
    cfg = [(w // d, d) for w, d in A_GROUPS] + [(BLK - 1, 1)]
    for gi, (max_dist, d) in enumerate(cfg):
        band = (dist >= 0) & (dist <= max_dist)
        tok = np.maximum(dist, 0) * d
        nf = np.maximum(tok, 1).astype(np.float32)
        max_exact = NUM_BUCKETS // 2
        large = max_exact + (np.log(nf / np.float32(max_exact)) / np.float32(math.log(MAX_DISTANCE / max_exact))
                             * np.float32(NUM_BUCKETS - max_exact)).astype(np.int32)
        large = np.minimum(large, NUM_BUCKETS - 1)
        bkt = np.where(tok < max_exact, tok, large).astype(np.int32)
        full = np.where(band, bkt, -1)
        out[gi, 1] = full
        out[gi, 0] = np.where(b >= BLK, full, -1)
    return out


def _bias_tiles(rel_bias, buckets, name):
    def body(tab_ref, bkt_ref, o_ref):
        h = pl.program_id(0)
        bkt = bkt_ref[...]
        acc = jnp.zeros(bkt.shape, F32)
        for bb in range(NUM_BUCKETS):
            acc = jnp.where(bkt == bb, tab_ref[bb, h], acc)
        o_ref[...] = jnp.where(bkt < 0, NEG, acc)

    return pl.pallas_call(
        body, grid=(N_BIAS_HEADS,),
        in_specs=[SMEM, pl.BlockSpec((None, 2, BLK, 2 * BLK), lambda h: (jnp.minimum(h // 4, 3), 0, 0, 0))],
        out_specs=pl.BlockSpec((None, 2, BLK, 2 * BLK), lambda h: (h, 0, 0, 0)),
        out_shape=jax.ShapeDtypeStruct((N_BIAS_HEADS, 2, BLK, 2 * BLK), F32),
        compiler_params=_cp("arbitrary"), name=name)(rel_bias, buckets)


def _bias_grad(gs, buckets, name):
    ng = len(gs)

    def body(*refs):
        g_refs = refs[:ng]
        bkt_ref, o_ref = refs[ng], refs[ng + 1]
        h = pl.program_id(0)
        g = g_refs[0][...]
        for r in g_refs[1:]:
            g = g + r[...]
        bkt = bkt_ref[...]
        row = lax.broadcasted_iota(jnp.int32, (NUM_BUCKETS, LANES), 0)
        lane = lax.broadcasted_iota(jnp.int32, (NUM_BUCKETS, LANES), 1)

        @pl.when(h == 0)
        def _():
            o_ref[...] = jnp.zeros((NUM_BUCKETS, LANES), F32)

        acc = o_ref[...]
        for bb in range(NUM_BUCKETS):
            s = jnp.sum(jnp.where(bkt == bb, g, 0.0))
            acc = jnp.where((row == bb) & (lane == h), s, acc)
        o_ref[...] = acc

    g_spec = pl.BlockSpec((None, BLK, 2 * BLK), lambda h: (h, 0, 0))
    return pl.pallas_call(
        body, grid=(N_BIAS_HEADS,),
        in_specs=[g_spec] * ng + [pl.BlockSpec((None, None, BLK, 2 * BLK), lambda h: (jnp.minimum(h // 4, 3), 1, 0, 0))],
        out_specs=pl.BlockSpec((NUM_BUCKETS, LANES), lambda h: (0, 0)),
        out_shape=jax.ShapeDtypeStruct((NUM_BUCKETS, LANES), F32),
        compiler_params=_cp("arbitrary"), name=name)(*gs, buckets)


def _to_class_major(src_ref, dst_ref, d, scale=None, dtype=None):
    ln = S // d
    for r in range(d):
        v = src_ref[pl.ds(r, ln, stride=d), :] if d > 1 else src_ref[...]
        if scale is not None:
            v = v * scale
        dst_ref[pl.ds(r * ln, ln), :] = v.astype(dtype or dst_ref.dtype)


def _block_rows(b, d):
    nbc = NB // d
    i = b % nbc
    r = b // nbc
    has_prev = (i > 0).astype(jnp.int32)
    prev = pl.multiple_of(jnp.maximum(b - 1, 0) * BLK, BLK)
    nat = i * (BLK * d) + r
    return has_prev, prev, nat


def _lane_halves(v0, v1):
    lane = lax.broadcasted_iota(jnp.int32, (v0.shape[0], LANES), 1)
    return jnp.where(lane < HD, v0, v1)


def _band_fwd(proj, bias, *, d, q0, k0, v0, npairs, bias0, shared_kv, name):
    def body(q_ref, k_ref, v_ref, b_ref, num_ref, st_ref, qs, ks, vs):
        p = pl.program_id(0)
        _to_class_major(q_ref, qs, d, scale=SCALE)
        _to_class_major(k_ref, ks, d)
        _to_class_major(v_ref, vs, d)
        lane = lax.broadcasted_iota(jnp.int32, (BLK, LANES), 1)

        def blk(b, carry):
            has_prev, prev, nat = _block_rows(b, d)
            cur = pl.multiple_of(b * BLK, BLK)
            qb = qs[pl.ds(cur, BLK), :]
            k2 = jnp.concatenate([ks[pl.ds(prev, BLK), :], ks[pl.ds(cur, BLK), :]], axis=0)
            v2 = jnp.concatenate([vs[pl.ds(prev, BLK), :], vs[pl.ds(cur, BLK), :]], axis=0)
            nums, ms, ls = [], [], []
            for hh in range(2):
                qh = qb[:, hh * HD:(hh + 1) * HD]
                if shared_kv:
                    kh = jnp.where(p >= 2, k2[:, HD:], k2[:, :HD])
                    vh = jnp.where(p >= 2, v2[:, HD:], v2[:, :HD])
                else:
                    kh = k2[:, hh * HD:(hh + 1) * HD]
                    vh = v2[:, hh * HD:(hh + 1) * HD]
                z = _dot(qh, kh, 1, 1) + b_ref[hh, has_prev]
                m = jnp.max(z, axis=1, keepdims=True)
                e = jnp.exp(z - m)
                ls.append(jnp.sum(e, axis=1, keepdims=True))
                ms.append(m)
                nums.append(_dot(e.astype(BF16), vh, 1, 0))
            num_t = jnp.concatenate(nums, axis=1)
            st_t = jnp.where(lane < 32, ms[0], jnp.where(lane < 64, ls[0], jnp.where(lane < 96, ms[1], ls[1])))
            if d > 1:
                num_ref[pl.ds(nat, BLK, stride=d), :] = num_t
                st_ref[pl.ds(nat, BLK, stride=d), :] = st_t
            else:
                num_ref[pl.ds(cur, BLK), :] = num_t
                st_ref[pl.ds(cur, BLK), :] = st_t
            return carry

        lax.fori_loop(0, NB, blk, 0)

    slab = lambda off, per_pair: pl.BlockSpec((None, S, LANES), (lambda p: (off + p, 0, 0)) if per_pair else (lambda p: (off, 0, 0)))
    out = pl.BlockSpec((None, S, LANES), lambda p: (p, 0, 0))
    return pl.pallas_call(
        body, grid=(npairs,),
        in_specs=[slab(q0, True), slab(k0, not shared_kv), slab(v0, not shared_kv),
                  pl.BlockSpec((None, 2, 2, BLK, 2 * BLK), lambda p: (bias0 + p, 0, 0, 0, 0))],
        out_specs=[out, out],
        out_shape=[jax.ShapeDtypeStruct((npairs, S, LANES), F32)] * 2,
        scratch_shapes=[pltpu.VMEM((S, LANES), BF16)] * 3,
        compiler_params=_cp("arbitrary"), name=name)(proj, proj, proj, bias)


def _combine_a(nums, stats, name):
    rt = 512

    def body(n0, n1, n2, s0, s1, s2, o_ref, l_ref):
        n_refs, s_refs = (n0, n1, n2), (s0, s1, s2)
        outs, lses = [], []
        for hh in range(2):
            ms = [s[:, 64 * hh:64 * hh + 1] for s in s_refs]
            ls = [s[:, 64 * hh + 32:64 * hh + 33] for s in s_refs]
            mx = jnp.maximum(jnp.maximum(ms[0], ms[1]), ms[2])
            cs = [jnp.exp(m - mx) for m in ms]
            z = cs[0] * ls[0] + cs[1] * ls[1] + cs[2] * ls[2]
            acc = cs[0] * n_refs[0][:, hh * HD:(hh + 1) * HD]
            acc = acc + cs[1] * n_refs[1][:, hh * HD:(hh + 1) * HD]
            acc = acc + cs[2] * n_refs[2][:, hh * HD:(hh + 1) * HD]
            outs.append(acc / z)
            lses.append(mx + jnp.log(z))
        o_ref[...] = jnp.concatenate(outs, axis=1)
        l_ref[...] = _lane_halves(lses[0], lses[1])

    spec = pl.BlockSpec((None, rt, LANES), lambda p, i: (p, i, 0))
    return pl.pallas_call(
        body, grid=(2, S // rt), in_specs=[spec] * 6, out_specs=[spec, spec],
        out_shape=[jax.ShapeDtypeStruct((2, S, LANES), F32)] * 2,
        compiler_params=_cp("parallel", "parallel"), name=name)(*nums, *stats)


def _combine_b(num, stats, sinks, name):
    rt = 512

    def body(sink_ref, n_ref, s_ref, o_ref, l_ref):
        p = pl.program_id(0)
        outs, lses = [], []
        for hh in range(2):
            sink = sink_ref[0, 2 * p + hh]
            m = s_ref[:, 64 * hh:64 * hh + 1]
            l = s_ref[:, 64 * hh + 32:64 * hh + 33]
            mx = jnp.maximum(m, sink)
            c = jnp.exp(m - mx)
            z = l * c + jnp.exp(sink - mx)
            outs.append(n_ref[:, hh * HD:(hh + 1) * HD] * (c / z))
            lses.append(mx + jnp.log(z))
        o_ref[...] = jnp.concatenate(outs, axis=1)
        l_ref[...] = _lane_halves(lses[0], lses[1])

    spec = pl.BlockSpec((None, rt, LANES), lambda p, i: (p, i, 0))
    return pl.pallas_call(
        body, grid=(4, S // rt), in_specs=[SMEM, spec, spec], out_specs=[spec, spec],
        out_shape=[jax.ShapeDtypeStruct((4, S, LANES), F32)] * 2,
        compiler_params=_cp("parallel", "parallel"), name=name)(sinks, num, stats)


def _band_bwd(proj, bias, o, do, lse, sinks, *, d, q0, k0, v0, npairs, bias0, shared_kv, name):
    nkv = 1 if shared_kv else npairs

    def body(sink_ref, q_ref, k_ref, v_ref, b_ref, o_ref, do_ref, lse_ref,
             dq_ref, dk_ref, dv_ref, g_ref, ds_ref,
             qs, ks, vs, dos, lses, dls, dl_nat, dq_nat, dk_cm, dv_cm, kv_nat):
        p = pl.program_id(0)
        lane = lax.broadcasted_iota(jnp.int32, (S, LANES), 1)
        dov = do_ref[...]
        prod = dov * o_ref[...]
        dl0 = jnp.sum(jnp.where(lane < HD, prod, 0.0), axis=1, keepdims=True)
        dl1 = jnp.sum(jnp.where(lane >= HD, prod, 0.0), axis=1, keepdims=True)
        dl_nat[...] = jnp.where(lane < HD, dl0, dl1)
        if shared_kv:
            row8 = lax.broadcasted_iota(jnp.int32, (8, LANES), 0)
            lane8 = lax.broadcasted_iota(jnp.int32, (8, LANES), 1)
            t = jnp.zeros((8, LANES), F32)
            lv = lse_ref[...]
            for hh in range(2):
                sink = sink_ref[0, 2 * p + hh]
                ps = jnp.exp(sink - lv[:, 64 * hh:64 * hh + 1])
                dsink = -jnp.sum(ps * (dl0 if hh == 0 else dl1))
                t = jnp.where((row8 == 0) & (lane8 == hh), dsink, t)
            ds_ref[...] = t
        else:
            ds_ref[...] = jnp.zeros((8, LANES), F32)
        _to_class_major(q_ref, qs, d, scale=SCALE)
        _to_class_major(k_ref, ks, d)
        _to_class_major(v_ref, vs, d)
        _to_class_major(do_ref, dos, d)
        _to_class_major(lse_ref, lses, d)
        _to_class_major(dl_nat, dls, d)

        def zero_kv():
            dk_cm[...] = jnp.zeros((S, LANES), F32)
            dv_cm[...] = jnp.zeros((S, LANES), F32)

        if shared_kv:
            pl.when(p == 0)(zero_kv)
        else:
            zero_kv()

        g_ref[...] = jnp.zeros((2, BLK, 2 * BLK), F32)
        lane2 = lax.broadcasted_iota(jnp.int32, (2 * BLK, LANES), 1)

        def blk(b, carry):
            has_prev, prev, nat = _block_rows(b, d)
            cur = pl.multiple_of(b * BLK, BLK)
            qb = qs[pl.ds(cur, BLK), :]
            dob = dos[pl.ds(cur, BLK), :]
            lb = lses[pl.ds(cur, BLK), :]
            dlb = dls[pl.ds(cur, BLK), :]
            k2 = jnp.concatenate([ks[pl.ds(prev, BLK), :], ks[pl.ds(cur, BLK), :]], axis=0)
            v2 = jnp.concatenate([vs[pl.ds(prev, BLK), :], vs[pl.ds(cur, BLK), :]], axis=0)
            dqs, dks, dvs = [], [], []
            for hh in range(2):
                qh = qb[:, hh * HD:(hh + 1) * HD]
                doh = dob[:, hh * HD:(hh + 1) * HD]
                if shared_kv:
                    kh = jnp.where(p >= 2, k2[:, HD:], k2[:, :HD])
                    vh = jnp.where(p >= 2, v2[:, HD:], v2[:, :HD])
                else:
                    kh = k2[:, hh * HD:(hh + 1) * HD]
                    vh = v2[:, hh * HD:(hh + 1) * HD]
                z = _dot(qh, kh, 1, 1) + b_ref[hh, has_prev]
                pr = jnp.exp(z - lb[:, 64 * hh:64 * hh + 1])
                dp = _dot(doh, vh, 1, 1)
                dz = pr * (dp - dlb[:, 64 * hh:64 * hh + 1])
                g_ref[hh] += dz
                dzb = dz.astype(BF16)
                dqs.append(_dot(dzb, kh, 1, 0) * SCALE)
                dks.append(_dot(dzb, qh, 0, 0))
                dvs.append(_dot(pr.astype(BF16), doh, 0, 0))
            dq_t = jnp.concatenate(dqs, axis=1)
            if shared_kv:
                dk_t = jnp.concatenate([dks[0] + dks[1]] * 2, axis=1)
                dv_t = jnp.concatenate([dvs[0] + dvs[1]] * 2, axis=1)
                mine = (lane2 >= HD) == (p >= 2)
                dk_t = jnp.where(mine, dk_t, 0.0)
                dv_t = jnp.where(mine, dv_t, 0.0)
            else:
                dk_t = jnp.concatenate(dks, axis=1)
                dv_t = jnp.concatenate(dvs, axis=1)
            dk_cm[pl.ds(prev, BLK), :] += dk_t[:BLK]
            dk_cm[pl.ds(cur, BLK), :] += dk_t[BLK:]
            dv_cm[pl.ds(prev, BLK), :] += dv_t[:BLK]
            dv_cm[pl.ds(cur, BLK), :] += dv_t[BLK:]
            if d > 1:
                dq_nat[pl.ds(nat, BLK, stride=d), :] = dq_t
            else:
                dq_nat[pl.ds(cur, BLK), :] = dq_t
            return carry

        lax.fori_loop(0, NB, blk, 0)
        dq_ref[...] = dq_nat[...].astype(BF16)

        def from_class_major(src, dst_ref):
            if d == 1:
                dst_ref[...] = src[...].astype(BF16)
            else:
                ln = S // d
                for r in range(d):
                    kv_nat[pl.ds(r, ln, stride=d), :] = src[pl.ds(r * ln, ln), :]
                dst_ref[...] = kv_nat[...].astype(BF16)

        def write_kv():
            from_class_major(dk_cm, dk_ref)
            from_class_major(dv_cm, dv_ref)

        if shared_kv:
            pl.when(p == npairs - 1)(write_kv)
        else:
            write_kv()

    slab = lambda off, per_pair: pl.BlockSpec((None, S, LANES), (lambda p: (off + p, 0, 0)) if per_pair else (lambda p: (off, 0, 0)))
    pair = pl.BlockSpec((None, S, LANES), lambda p: (p, 0, 0))
    kv_out = pair if not shared_kv else pl.BlockSpec((None, S, LANES), lambda p: (0, 0, 0))
    return pl.pallas_call(
        body, grid=(npairs,),
        in_specs=[SMEM, slab(q0, True), slab(k0, not shared_kv), slab(v0, not shared_kv),
                  pl.BlockSpec((None, 2, 2, BLK, 2 * BLK), lambda p: (bias0 + p, 0, 0, 0, 0)),
                  pair, pair, pair],
        out_specs=[pair, kv_out, kv_out,
                   pl.BlockSpec((None, 2, BLK, 2 * BLK), lambda p: (p, 0, 0, 0)),
                   pl.BlockSpec((None, 8, LANES), lambda p: (p, 0, 0))],
        out_shape=[jax.ShapeDtypeStruct((npairs, S, LANES), BF16),
                   jax.ShapeDtypeStruct((nkv, S, LANES), BF16),
                   jax.ShapeDtypeStruct((nkv, S, LANES), BF16),
                   jax.ShapeDtypeStruct((npairs, 2, BLK, 2 * BLK), F32),
                   jax.ShapeDtypeStruct((npairs, 8, LANES), F32)],
        scratch_shapes=[pltpu.VMEM((S, LANES), BF16)] * 4 + [pltpu.VMEM((S, LANES), F32)] * 7,
        compiler_params=_cp("arbitrary"), name=name)(sinks, proj, proj, proj, bias, o, do, lse)


KC = 512
NSUB = KC // BLK


def _split2(x):
    hi = x.astype(BF16)
    lo = (x - hi.astype(F32)).astype(BF16)
    return hi, lo


def _sub_prefix(x, tri):
    st = jnp.concatenate([x[:, s * BLK:(s + 1) * BLK] for s in range(NSUB)], axis=0)
    hi, lo = _split2(st)
    r = _dot(hi, tri, 1, 0) + _dot(lo, tri, 1, 0)
    return [r[s * BLK:(s + 1) * BLK] for s in range(NSUB)]


def _log_sig_pair(z):
    sp = jnp.log1p(jnp.exp(-jnp.abs(z)))
    return jnp.minimum(z, 0.0) - sp, jnp.minimum(-z, 0.0) - sp


def _stick_fwd(proj, *, q0, k0, v0, name):
    def body(q_ref, k_ref, v_ref, o_ref, t_ref, qs, ks, vs):
        qs[...] = (q_ref[...] * SCALE).astype(BF16)
        ks[...] = k_ref[...].astype(BF16)
        vs[...] = v_ref[...].astype(BF16)
        jj = lax.broadcasted_iota(jnp.int32, (BLK, BLK), 0)
        ss = lax.broadcasted_iota(jnp.int32, (BLK, BLK), 1)
        tri = (jj > ss).astype(BF16)
        col = lax.broadcasted_iota(jnp.int32, (BLK, KC), 1)
        rowi = lax.broadcasted_iota(jnp.int32, (BLK, KC), 0)

        def qblock(i, carry0):
            t0 = pl.multiple_of(i * BLK, BLK)
            qb = qs[pl.ds(t0, BLK), :]
            nch = i // NSUB + 1
            outs, tots = [], []
            for hh in range(2):
                qh = qb[:, hh * HD:(hh + 1) * HD]

                def chunk(cc, st):
                    acc, run = st
                    c = nch - 1 - cc
                    s0 = pl.multiple_of(c * KC, KC)
                    kh = ks[pl.ds(s0, KC), hh * HD:(hh + 1) * HD]
                    vh = vs[pl.ds(s0, KC), hh * HD:(hh + 1) * HD]
                    z = _dot(qh, kh, 1, 1)
                    before = (s0 + col) < (t0 + rowi)
                    lb, lk = _log_sig_pair(z)
                    lk = jnp.where(before, lk, 0.0)
                    suf = _sub_prefix(lk, tri)
                    ws = []
                    for s in reversed(range(NSUB)):
                        lks = lk[:, s * BLK:(s + 1) * BLK]
                        rest = suf[s] + run
                        ws.append(jnp.exp(lb[:, s * BLK:(s + 1) * BLK] + rest))
                        run = run + jnp.sum(lks, axis=1, keepdims=True)
                    w = jnp.concatenate(ws[::-1], axis=1)
                    w = jnp.where(before, w, 0.0)
                    return acc + _dot(w.astype(BF16), vh, 1, 0), run

                acc, run = lax.fori_loop(0, nch, chunk, (jnp.zeros((BLK, HD), F32), jnp.zeros((BLK, 1), F32)))
                outs.append(acc)
                tots.append(run)
            o_ref[pl.ds(t0, BLK), :] = jnp.concatenate(outs, axis=1)
            t_ref[pl.ds(t0, BLK), :] = _lane_halves(tots[0], tots[1])
            return carry0

        lax.fori_loop(0, NB, qblock, 0)

    slab = lambda off: pl.BlockSpec((None, S, LANES), lambda p: (off + p, 0, 0))
    out = pl.BlockSpec((None, S, LANES), lambda p: (p, 0, 0))
    return pl.pallas_call(
        body, grid=(2,), in_specs=[slab(q0), slab(k0), slab(v0)], out_specs=[out, out],
        out_shape=[jax.ShapeDtypeStruct((2, S, LANES), F32)] * 2,
        scratch_shapes=[pltpu.VMEM((S, LANES), BF16)] * 3,
        compiler_params=_cp("arbitrary"), name=name)(proj, proj, proj)


def _stick_bwd(proj, do, tot, *, q0, k0, v0, name):
    def body(q_ref, k_ref, v_ref, do_ref, t_ref, dq_ref, dk_ref, dv_ref, qs, ks, vs, dos, dk_acc, dv_acc):
        qs[...] = (q_ref[...] * SCALE).astype(BF16)
        ks[...] = k_ref[...].astype(BF16)
        vs[...] = v_ref[...].astype(BF16)
        dos[...] = do_ref[...].astype(BF16)
        dk_acc[...] = jnp.zeros((2, S, HD), F32)
        dv_acc[...] = jnp.zeros((2, S, HD), F32)
        jj = lax.broadcasted_iota(jnp.int32, (BLK, BLK), 0)
        ss = lax.broadcasted_iota(jnp.int32, (BLK, BLK), 1)
        tri_inc = (jj <= ss).astype(BF16)
        tri_exc = (jj < ss).astype(BF16)
        col = lax.broadcasted_iota(jnp.int32, (BLK, KC), 1)
        rowi = lax.broadcasted_iota(jnp.int32, (BLK, KC), 0)

        def qblock(i, carry0):
            t0 = pl.multiple_of(i * BLK, BLK)
            qb = qs[pl.ds(t0, BLK), :]
            dob = dos[pl.ds(t0, BLK), :]
            tb = t_ref[pl.ds(t0, BLK), :]
            nch = i // NSUB + 1
            dqs = []
            for hh in range(2):
                qh = qb[:, hh * HD:(hh + 1) * HD]
                doh = dob[:, hh * HD:(hh + 1) * HD]
                tt = tb[:, 64 * hh:64 * hh + 1]

                def chunk(c, st):
                    dq, prun, erun = st
                    s0 = pl.multiple_of(c * KC, KC)
                    kh = ks[pl.ds(s0, KC), hh * HD:(hh + 1) * HD]
                    vh = vs[pl.ds(s0, KC), hh * HD:(hh + 1) * HD]
                    z = _dot(qh, kh, 1, 1)
                    before = (s0 + col) < (t0 + rowi)
                    lb, lk = _log_sig_pair(z)
                    lk = jnp.where(before, lk, 0.0)
                    pin = _sub_prefix(lk, tri_inc)
                    ws = []
                    for s in range(NSUB):
                        rest = tt - (pin[s] + prun)
                        ws.append(jnp.exp(lb[:, s * BLK:(s + 1) * BLK] + rest))
                        prun = prun + jnp.sum(lk[:, s * BLK:(s + 1) * BLK], axis=1, keepdims=True)
                    w = jnp.where(before, jnp.concatenate(ws, axis=1), 0.0)
                    dw = _dot(doh, vh, 1, 1)
                    e = w * dw
                    pex = _sub_prefix(e, tri_exc)
                    cs = []
                    for s in range(NSUB):
                        cs.append(pex[s] + erun)
                        erun = erun + jnp.sum(e[:, s * BLK:(s + 1) * BLK], axis=1, keepdims=True)
                    cex = jnp.concatenate(cs, axis=1)
                    sig = jnp.exp(lb)
                    dz = jnp.where(before, e * (1.0 - sig) - cex * sig, 0.0).astype(BF16)
                    dq = dq + _dot(dz, kh, 1, 0)
                    dk_acc[hh, pl.ds(s0, KC), :] += _dot(dz, qh, 0, 0)
                    dv_acc[hh, pl.ds(s0, KC), :] += _dot(w.astype(BF16), doh, 0, 0)
                    return dq, prun, erun

                init = (jnp.zeros((BLK, HD), F32), jnp.zeros((BLK, 1), F32), jnp.zeros((BLK, 1), F32))
                dq, _, _ = lax.fori_loop(0, nch, chunk, init)
                dqs.append(dq * SCALE)
            dq_ref[pl.ds(t0, BLK), :] = jnp.concatenate(dqs, axis=1).astype(BF16)
            return carry0

        lax.fori_loop(0, NB, qblock, 0)
        dk_ref[...] = jnp.concatenate([dk_acc[0], dk_acc[1]], axis=1).astype(BF16)
        dv_ref[...] = jnp.concatenate([dv_acc[0], dv_acc[1]], axis=1).astype(BF16)

    slab = lambda off: pl.BlockSpec((None, S, LANES), lambda p: (off + p, 0, 0))
    pair = pl.BlockSpec((None, S, LANES), lambda p: (p, 0, 0))
    return pl.pallas_call(
        body, grid=(2,), in_specs=[slab(q0), slab(k0), slab(v0), pair, pair], out_specs=[pair] * 3,
        out_shape=[jax.ShapeDtypeStruct((2, S, LANES), BF16)] * 3,
        scratch_shapes=[pltpu.VMEM((S, LANES), BF16)] * 4 + [pltpu.VMEM((2, S, HD), F32)] * 2,
        compiler_params=_cp("arbitrary"), name=name)(proj, proj, proj, do, tot)


def _cat_slabs(ref):
    return jnp.concatenate([ref[s] for s in range(ref.shape[0])], axis=1)


def _merge_fwd(o_a, o_b, o_c, gates, b_gate, wa, wb, wc, w_out, name):
    tm = ROW_TILE

    def body(oa_ref, ob_ref, oc_ref, g_ref, bg_ref, wa_ref, wb_ref, wc_ref, wo_ref, mg_ref, mo_ref):
        acc = jnp.zeros((tm, D), F32)
        for i, (o_ref, w_ref) in enumerate(((oa_ref, wa_ref), (ob_ref, wb_ref), (oc_ref, wc_ref))):
            pr = _dot(_cat_slabs(o_ref).astype(BF16), w_ref[...], 1, 0)
            sg = jax.nn.sigmoid(g_ref[:, i * D:(i + 1) * D] + bg_ref[i:i + 1, :])
            acc = acc + sg * pr
        mg = acc.astype(BF16)
        mg_ref[...] = mg
        mo_ref[...] = _dot(mg, wo_ref[...], 1, 0)

    slabs = lambda n: pl.BlockSpec((n, tm, LANES), lambda i: (0, i, 0))
    full = lambda r, c: pl.BlockSpec((r, c), lambda i: (0, 0))
    row = pl.BlockSpec((tm, D), lambda i: (i, 0))
    return pl.pallas_call(
        body, grid=(S // tm,),
        in_specs=[slabs(2), slabs(4), slabs(2), pl.BlockSpec((tm, GATE_COLS), lambda i: (i, 0)), full(3, D),
                  full(256, D), full(512, D), full(256, D), full(D, D)],
        out_specs=[row, row],
        out_shape=[jax.ShapeDtypeStruct((S, D), BF16), jax.ShapeDtypeStruct((S, D), F32)],
        compiler_params=_cp("parallel"), name=name)(o_a, o_b, o_c, gates, b_gate, wa, wb, wc, w_out)


def _merge_bwd(d_mo, o_a, o_b, o_c, gates, b_gate, wa, wb, wc, w_out, name):
    tm = ROW_TILE

    def body(dmo_ref, oa_ref, ob_ref, oc_ref, g_ref, bg_ref, wa_ref, wb_ref, wc_ref, wo_ref,
             doa_ref, dob_ref, doc_ref, dg_ref, dwa_ref, dwb_ref, dwc_ref, dbg_ref):
        @pl.when(pl.program_id(0) == 0)
        def _():
            dwa_ref[...] = jnp.zeros(dwa_ref.shape, F32)
            dwb_ref[...] = jnp.zeros(dwb_ref.shape, F32)
            dwc_ref[...] = jnp.zeros(dwc_ref.shape, F32)
            dbg_ref[...] = jnp.zeros(dbg_ref.shape, F32)

        dmg = _dot(dmo_ref[...], wo_ref[...], 1, 1)
        trip = ((oa_ref, wa_ref, doa_ref, dwa_ref), (ob_ref, wb_ref, dob_ref, dwb_ref), (oc_ref, wc_ref, doc_ref, dwc_ref))
        for i, (o_ref, w_ref, do_ref, dw_ref) in enumerate(trip):
            ob = _cat_slabs(o_ref).astype(BF16)
            pr = _dot(ob, w_ref[...], 1, 0)
            sg = jax.nn.sigmoid(g_ref[:, i * D:(i + 1) * D] + bg_ref[i:i + 1, :])
            dgate = dmg * pr * sg * (1.0 - sg)
            dg_ref[:, i * D:(i + 1) * D] = dgate.astype(BF16)
            dbg_ref[i:i + 1, :] += jnp.sum(dgate, axis=0, keepdims=True)
            dpr = (dmg * sg).astype(BF16)
            do = _dot(dpr, w_ref[...], 1, 1)
            for s in range(do_ref.shape[0]):
                do_ref[s] = do[:, s * LANES:(s + 1) * LANES]
            dw_ref[...] += _dot(ob, dpr, 0, 0)

    slabs = lambda n: pl.BlockSpec((n, tm, LANES), lambda i: (0, i, 0))
    full = lambda r, c: pl.BlockSpec((r, c), lambda i: (0, 0))
    row = pl.BlockSpec((tm, D), lambda i: (i, 0))
    return pl.pallas_call(
        body, grid=(S // tm,),
        in_specs=[row, slabs(2), slabs(4), slabs(2), pl.BlockSpec((tm, GATE_COLS), lambda i: (i, 0)), full(3, D),
                  full(256, D), full(512, D), full(256, D), full(D, D)],
        out_specs=[slabs(2), slabs(4), slabs(2), pl.BlockSpec((tm, GATE_COLS), lambda i: (i, 0)),
                   full(256, D), full(512, D), full(256, D), full(3, D)],
        out_shape=[jax.ShapeDtypeStruct((2, S, LANES), F32), jax.ShapeDtypeStruct((4, S, LANES), F32),
                   jax.ShapeDtypeStruct((2, S, LANES), F32), jax.ShapeDtypeStruct((S, GATE_COLS), BF16),
                   jax.ShapeDtypeStruct((256, D), F32), jax.ShapeDtypeStruct((512, D), F32),
                   jax.ShapeDtypeStruct((256, D), F32), jax.ShapeDtypeStruct((3, D), F32)],
        compiler_params=_cp("arbitrary"), name=name)(d_mo, o_a, o_b, o_c, gates, b_gate, wa, wb, wc, w_out)


FC = 256
GELU_K = math.sqrt(2.0 / math.pi)
GELU_C = 0.044715


def _shift_down(x, n):
    row = lax.broadcasted_iota(jnp.int32, x.shape, 0)
    return jnp.where(row < n, 0.0, pltpu.roll(x, n, 0))


def _shift_up(x, n):
    row = lax.broadcasted_iota(jnp.int32, x.shape, 0)
    return jnp.where(row >= x.shape[0] - n, 0.0, pltpu.roll(x, x.shape[0] - n, 0))


def _conv(u, w_ref, half, b):
    return (w_ref[0:1, half, :] * _shift_down(u, 2) + w_ref[1:2, half, :] * _shift_down(u, 1)
            + w_ref[2:3, half, :] * u + b)


def _ffn_act(u, conv_w, conv_b, name):
    def body(u_ref, w_ref, b_ref, a_ref):
        yg = _conv(u_ref[0], w_ref, 0, b_ref[0:1, :])
        yv = _conv(u_ref[1], w_ref, 1, b_ref[1:2, :])
        th = jnp.tanh(GELU_K * (yg + GELU_C * yg * yg * yg))
        a_ref[...] = (0.5 * yg * (1.0 + th) * yv).astype(BF16)

    return pl.pallas_call(
        body, grid=(D_FF // FC,),
        in_specs=[pl.BlockSpec((2, S, FC), lambda j: (0, 0, j)), pl.BlockSpec((3, 2, FC), lambda j: (0, 0, j)),
                  pl.BlockSpec((2, FC), lambda j: (0, j))],
        out_specs=pl.BlockSpec((S, FC), lambda j: (0, j)),
        out_shape=jax.ShapeDtypeStruct((S, D_FF), BF16),
        compiler_params=_cp("parallel"), name=name)(u, conv_w, conv_b)


def _ffn_act_bwd(u, d_a, conv_w, conv_b, name):
    def body(u_ref, da_ref, w_ref, b_ref, du_ref, dw_ref, db_ref):
        ug, uv = u_ref[0], u_ref[1]
        yg = _conv(ug, w_ref, 0, b_ref[0:1, :])
        yv = _conv(uv, w_ref, 1, b_ref[1:2, :])
        inner = GELU_K * (yg + GELU_C * yg * yg * yg)
        th = jnp.tanh(inner)
        gelu = 0.5 * yg * (1.0 + th)
        dgelu = 0.5 * (1.0 + th) + 0.5 * yg * (1.0 - th * th) * GELU_K * (1.0 + 3.0 * GELU_C * yg * yg)
        da = da_ref[...]
        for half, (uu, dy) in enumerate(((ug, da * yv * dgelu), (uv, da * gelu))):
            du = (w_ref[2:3, half, :] * dy + w_ref[1:2, half, :] * _shift_up(dy, 1)
                  + w_ref[0:1, half, :] * _shift_up(dy, 2))
            du_ref[half] = du.astype(BF16)
            dw_ref[0:1, half, :] = jnp.sum(dy * _shift_down(uu, 2), axis=0, keepdims=True)
            dw_ref[1:2, half, :] = jnp.sum(dy * _shift_down(uu, 1), axis=0, keepdims=True)
            dw_ref[2:3, half, :] = jnp.sum(dy * uu, axis=0, keepdims=True)
            db_ref[half:half + 1, :] = jnp.sum(dy, axis=0, keepdims=True)

    return pl.pallas_call(
        body, grid=(D_FF // FC,),
        in_specs=[pl.BlockSpec((2, S, FC), lambda j: (0, 0, j)), pl.BlockSpec((S, FC), lambda j: (0, j)),
                  pl.BlockSpec((3, 2, FC), lambda j: (0, 0, j)), pl.BlockSpec((2, FC), lambda j: (0, j))],
        out_specs=[pl.BlockSpec((2, S, FC), lambda j: (0, 0, j)), pl.BlockSpec((3, 2, FC), lambda j: (0, 0, j)),
                   pl.BlockSpec((2, FC), lambda j: (0, j))],
        out_shape=[jax.ShapeDtypeStruct((2, S, D_FF), BF16), jax.ShapeDtypeStruct((3, 2, D_FF), F32),
                   jax.ShapeDtypeStruct((2, D_FF), F32)],
        compiler_params=_cp("parallel"), name=name)(u, d_a, conv_w, conv_b)


def _layer_fwd(x, h1, w, bias, lname):
    n = lambda s: f"{lname}_{s}"
    w.need("in", h1)
    tn = 768
    proj = _mm(h1, w["w_in"], grid=(S // 1024, QKV_COLS // tn, 1),
               a_spec=pl.BlockSpec((1024, D), lambda i, j, k: (i, 0)),
               b_spec=pl.BlockSpec((D, tn), lambda i, j, k: (0, j)),
               out_shape=jax.ShapeDtypeStruct((QKV_SLABS, S, LANES), F32),
               out_spec=pl.BlockSpec((tn // LANES, 1024, LANES), lambda i, j, k: (j, i, 0)),
               ca=1, cb=0, acc_shape=(1024, tn), out_slab=True, name=n("proj_qkv"))
    gates = _mm(h1, w["w_in"], grid=(S // 1024, GATE_COLS // tn, 1),
                a_spec=pl.BlockSpec((1024, D), lambda i, j, k: (i, 0)),
                b_spec=pl.BlockSpec((D, tn), lambda i, j, k: (0, j + QKV_COLS // tn)),
                out_shape=jax.ShapeDtypeStruct((S, GATE_COLS), F32),
                out_spec=pl.BlockSpec((1024, tn), lambda i, j, k: (i, j)),
                ca=1, cb=0, acc_shape=(1024, tn), name=n("proj_gate"))
    nums, stats = [], []
    for g, (_, d) in enumerate(A_GROUPS):
        nm, st = _band_fwd(proj, bias, d=d, q0=2 * g, k0=6 + 2 * g, v0=12 + 2 * g, npairs=2, bias0=2 * g,
                           shared_kv=False, name=n(f"attn_a{g}_fwd"))
        nums.append(nm)
        stats.append(st)
    o_a, lse_a = _combine_a(nums, stats, n("attn_a_combine"))
    nm_b, st_b = _band_fwd(proj, bias, d=1, q0=18, k0=22, v0=23, npairs=4, bias0=6, shared_kv=True, name=n("attn_b_fwd"))
    o_b, lse_b = _combine_b(nm_b, st_b, w["sinks"], n("attn_b_combine"))
    o_c, tot_c = _stick_fwd(proj, q0=24, k0=26, v0=28, name=n("attn_c_fwd"))
    w.need("mix", tot_c)
    merged, mo = _merge_fwd(o_a, o_b, o_c, gates, w["b_gate"], w["w_br_a"], w["w_br_b"], w["w_br_c"], w["w_out"], n("merge_fwd"))
    x2, h2 = _postnorm_res(x, mo, w["attn_post_norm"], w["ffn_pre_norm"], n("attn_post"))
    w.need("ffn", h2)
    u = _mm(h2, w["w_up"], grid=(S // 1024, 2 * D_FF // 1024, 1),
            a_spec=pl.BlockSpec((1024, D), lambda i, j, k: (i, 0)),
            b_spec=pl.BlockSpec((D, 1024), lambda i, j, k: (0, j)),
            out_shape=jax.ShapeDtypeStruct((2, S, D_FF), F32),
            out_spec=pl.BlockSpec((None, 1024, 1024), lambda i, j, k: (j // 4, i, j % 4)),
            ca=1, cb=0, acc_shape=(1024, 1024), name=n("ffn_up"))
    a = _ffn_act(u, w["conv_w"], w["conv_b"], n("ffn_act"))
    fo = _mm_nn(a, w["w_down"], F32, 1024, 1024, 1024, n("ffn_down"))
    saved = dict(x=x, h1=h1, proj=proj, gates=gates, o_a=o_a, lse_a=lse_a, o_b=o_b, lse_b=lse_b, o_c=o_c, tot_c=tot_c,
                 merged=merged, mo=mo, x2=x2, h2=h2, u=u, a=a, fo=fo)
    return saved


def _layer_bwd(dx3, sv, w, bias, lname, tok=None):
    n = lambda s: f"{lname}_{s}"
    g = {}
    gain = w["ffn_post_norm"] if tok is None else w["ffn_post_norm"] + tok
    d_fo, g["ffn_post_norm"] = _norm_bwd(sv["fo"], gain, [dx3], None, BF16, n("ffn_post_bwd"))
    d_a = _mm_nt(d_fo, w["w_down"], F32, 1024, 1024, 1024, n("ffn_down_bwd_x"))
    g["w_down"] = _mm_tn(sv["a"], d_fo, BF16, 1024, 1024, 1024, n("ffn_down_bwd_w"))
    d_u, dcw, dcb = _ffn_act_bwd(sv["u"], d_a, w["conv_w"], w["conv_b"], n("ffn_act_bwd"))
    g["conv_w"] = dcw.reshape(3, 2 * D_FF)
    g["conv_b"] = dcb.reshape(1, 2 * D_FF)
    g["w_up"] = _mm(sv["h2"], d_u, grid=(1, 2 * D_FF // 1024, S // 1024),
                    a_spec=pl.BlockSpec((1024, D), lambda i, j, k: (k, 0)),
                    b_spec=pl.BlockSpec((None, 1024, 1024), lambda i, j, k: (j // 4, k, j % 4)),
                    out_shape=jax.ShapeDtypeStruct((D, 2 * D_FF), BF16),
                    out_spec=pl.BlockSpec((D, 1024), lambda i, j, k: (0, j)),
                    ca=0, cb=0, acc_shape=(D, 1024), name=n("ffn_up_bwd_w"))
    d_h2 = _mm(d_u, w["w_up"], grid=(S // 1024, 1, 2 * D_FF // 1024),
               a_spec=pl.BlockSpec((None, 1024, 1024), lambda i, j, k: (k // 4, i, k % 4)),
               b_spec=pl.BlockSpec((D, 1024), lambda i, j, k: (0, k)),
               out_shape=jax.ShapeDtypeStruct((S, D), F32),
               out_spec=pl.BlockSpec((1024, D), lambda i, j, k: (i, 0)),
               ca=1, cb=1, acc_shape=(1024, D), name=n("ffn_up_bwd_x"))
    dx2, g["ffn_pre_norm"] = _norm_bwd(sv["x2"], w["ffn_pre_norm"], [d_h2], dx3, F32, n("ffn_pre_bwd"))
    d_mo, g["attn_post_norm"] = _norm_bwd(sv["mo"], w["attn_post_norm"], [dx2], None, BF16, n("attn_post_bwd"))
    g["w_out"] = _mm_tn(sv["merged"], d_mo, BF16, 1024, 1024, 1024, n("out_bwd_w"))
    do_a, do_b, do_c, d_gates, dwa, dwb, dwc, g["b_gate"] = _merge_bwd(
        d_mo, sv["o_a"], sv["o_b"], sv["o_c"], sv["gates"], w["b_gate"], w["w_br_a"], w["w_br_b"], w["w_br_c"],
        w["w_out"], n("merge_bwd"))
    g["w_br_a"], g["w_br_b"], g["w_br_c"] = dwa, dwb, dwc
    proj = sv["proj"]
    dqa, dka, dva, gbias = [], [], [], []
    for gi, (_, d) in enumerate(A_GROUPS):
        dq, dk, dv, gg, _ = _band_bwd(proj, bias, sv["o_a"], do_a, sv["lse_a"], w["sinks"], d=d, q0=2 * gi, k0=6 + 2 * gi,
                                      v0=12 + 2 * gi, npairs=2, bias0=2 * gi, shared_kv=False, name=n(f"attn_a{gi}_bwd"))
        dqa.append(dq), dka.append(dk), dva.append(dv), gbias.append(gg)
    dqb, dkb, dvb, ggb, dsink = _band_bwd(proj, bias, sv["o_b"], do_b, sv["lse_b"], w["sinks"], d=1, q0=18, k0=22, v0=23,
                                          npairs=4, bias0=6, shared_kv=True, name=n("attn_b_bwd"))
    gbias.append(ggb)
    g["bias_g"] = jnp.concatenate(gbias, axis=0).reshape(N_BIAS_HEADS, BLK, 2 * BLK)
    g["sinks"] = dsink[:, 0, :2].reshape(1, 8)
    dqc, dkc, dvc = _stick_bwd(proj, do_c, sv["tot_c"], q0=24, k0=26, v0=28, name=n("attn_c_bwd"))
    dqkv = jnp.concatenate(dqa + dka + dva + [dqb, dkb, dvb, dqc, dkc, dvc], axis=0)
    ts = 6
    d_h1a = _mm(dqkv, w["w_in"], grid=(S // 1024, 1, QKV_SLABS // ts),
                a_spec=pl.BlockSpec((ts, 1024, LANES), lambda i, j, k: (k, i, 0)),
                b_spec=pl.BlockSpec((D, ts * LANES), lambda i, j, k: (0, k)),
                out_shape=jax.ShapeDtypeStruct((S, D), F32),
                out_spec=pl.BlockSpec((1024, D), lambda i, j, k: (i, 0)),
                ca=1, cb=1, acc_shape=(1024, D), a_slab=True, name=n("in_bwd_x_qkv"))
    d_h1b = _mm(d_gates, w["w_in"], grid=(S // 1024, 1, GATE_COLS // 768),
                a_spec=pl.BlockSpec((1024, 768), lambda i, j, k: (i, k)),
                b_spec=pl.BlockSpec((D, 768), lambda i, j, k: (0, k + QKV_COLS // 768)),
                out_shape=jax.ShapeDtypeStruct((S, D), F32),
                out_spec=pl.BlockSpec((1024, D), lambda i, j, k: (i, 0)),
                ca=1, cb=1, acc_shape=(1024, D), name=n("in_bwd_x_gate"))
    dw_in = _mm(sv["h1"], dqkv, grid=(1, QKV_SLABS // ts, S // 1024),
                a_spec=pl.BlockSpec((1024, D), lambda i, j, k: (k, 0)),
                b_spec=pl.BlockSpec((ts, 1024, LANES), lambda i, j, k: (j, k, 0)),
                out_shape=jax.ShapeDtypeStruct((D, IN_COLS), BF16),
                out_spec=pl.BlockSpec((D, ts * LANES), lambda i, j, k: (0, j)),
                ca=0, cb=0, acc_shape=(D, ts * LANES), b_slab=True, name=n("in_bwd_w_qkv"))
    g["w_in"] = _mm(sv["h1"], d_gates, grid=(1, GATE_COLS // 768, S // 1024),
                    a_spec=pl.BlockSpec((1024, D), lambda i, j, k: (k, 0)),
                    b_spec=pl.BlockSpec((1024, 768), lambda i, j, k: (k, j)),
                    out_shape=jax.ShapeDtypeStruct((D, IN_COLS), BF16),
                    out_spec=pl.BlockSpec((D, 768), lambda i, j, k: (0, j + QKV_COLS // 768)),
                    ca=0, cb=0, acc_shape=(D, 768), alias_out=dw_in, name=n("in_bwd_w_gate"))
    dx, g["attn_pre_norm"] = _norm_bwd(sv["x"], w["attn_pre_norm"], [d_h1a, d_h1b], dx2, F32, n("attn_pre_bwd"))
    return dx, g


def _local_step(x, target, ws, rel_bias, tok=None, on_grads=None):
    buckets = jnp.asarray(_bucket_tiles())
    bias = _bias_tiles(rel_bias, buckets, "bias_tiles").reshape(N_BIAS_HEADS // 2, 2, 2, BLK, 2 * BLK)
    saved = []
    gain0 = ws[0]["attn_pre_norm"] if tok is None else ws[0]["attn_pre_norm"] + tok
    h1 = _prenorm(x, gain0, "l0_attn_pre")
    for l in range(DEPTH):
        sv = _layer_fwd(x, h1, ws[l], bias, f"l{l}")
        saved.append(sv)
        g_next = ws[l + 1]["attn_pre_norm"] if l + 1 < DEPTH else ws[l]["attn_pre_norm"]
        x, h1 = _postnorm_res(sv["x2"], sv["fo"], ws[l]["ffn_post_norm"], g_next, f"l{l}_ffn_post")
    dy, loss_tile = _loss_head(x, target, "loss_head")
    grads = [None] * DEPTH
    tok = None
    for l in reversed(range(DEPTH)):
        dy, grads[l] = _layer_bwd(dy, saved[l], ws[l], bias, f"l{l}", tok)
        if on_grads is not None:
            tok = on_grads(l, grads[l])
    g_rel = _bias_grad([grads[l]["bias_g"] for l in range(DEPTH)], buckets, "bias_grad")[:, :N_BIAS_HEADS]
    return loss_tile[0, 0], dy, grads, g_rel


def _coords():
    return lax.axis_index("x"), lax.axis_index("y"), lax.axis_index("c")


def _peer(rel):
    x, y, c = _coords()
    return (1 - x if rel & 4 else x, 1 - y if rel & 2 else y, 1 - c if rel & 1 else c)


def _exchange(srcs, dst_shapes, src_win, dst_win, name):
    nt = len(srcs)

    def body(*refs):
        src_refs, dst_refs = refs[:nt], refs[nt:2 * nt]
        send_sems, recv_sems, local_sems = refs[2 * nt:]
        x, y, c = _coords()
        me = 4 * x + 2 * y + c
        locals_ = []
        for t in range(nt):
            cp = pltpu.make_async_copy(src_win(t, src_refs[t], me), dst_win(t, dst_refs[t], me), local_sems.at[t])
            cp.start()
            locals_.append(cp)
        sends = []
        for rel in range(1, NDEV):
            px, py, pc = _peer(rel)
            q = 4 * px + 2 * py + pc
            for t in range(nt):
                cp = pltpu.make_async_remote_copy(
                    src_ref=src_win(t, src_refs[t], q), dst_ref=dst_win(t, dst_refs[t], me),
                    send_sem=send_sems.at[rel - 1, t], recv_sem=recv_sems.at[rel - 1, t],
                    device_id=(px, py, pc), device_id_type=MESH)
                cp.start()
                sends.append(cp)
        for rel in range(1, NDEV):
            px, py, pc = _peer(rel)
            q = 4 * px + 2 * py + pc
            for t in range(nt):
                pltpu.make_async_remote_copy(
                    src_ref=src_win(t, src_refs[t], me), dst_ref=dst_win(t, dst_refs[t], q),
                    send_sem=send_sems.at[rel - 1, t], recv_sem=recv_sems.at[rel - 1, t],
                    device_id=(px, py, pc), device_id_type=MESH).wait_recv()
        for cp in sends:
            cp.wait_send()
        for cp in locals_:
            cp.wait()

    return pl.pallas_call(
        body, in_specs=[ANY] * nt, out_specs=[ANY] * nt, out_shape=dst_shapes,
        scratch_shapes=[pltpu.SemaphoreType.DMA((NDEV - 1, nt)), pltpu.SemaphoreType.DMA((NDEV - 1, nt)),
                        pltpu.SemaphoreType.DMA((nt,))],
        name=name)(*srcs)


BIG = (("w_in", 1, 864), ("w_br_a", 1, 128), ("w_br_b", 1, 128), ("w_br_c", 1, 128), ("w_out", 0, 128),
       ("w_up", 1, 1024), ("w_down", 0, 512))


NBIG = len(BIG)
BIG_FULL = {"w_in": (D, IN_COLS), "w_br_a": (256, D), "w_br_b": (512, D), "w_br_c": (256, D), "w_out": (D, D),
            "w_up": (D, 2 * D_FF), "w_down": (D_FF, D)}
LAYER_GROUPS = (("in", (0,)), ("mix", (1, 2, 3, 4)), ("ffn", (5, 6)))

HBM_SPEC = pl.BlockSpec(memory_space=pltpu.HBM)
SEM_SPEC = pl.BlockSpec(memory_space=pltpu.SEMAPHORE)


def _hbm(a):
    return pltpu.with_memory_space_constraint(a, pltpu.HBM)


def _shard_window(t, ref, k):
    nm, ax, ext = BIG[t % NBIG]
    if nm == "w_in":
        return ref.at[k]
    off = pl.multiple_of(k * ext, ext)
    if ax == 0:
        return ref.at[pl.ds(off, ext), :]
    return ref.at[:, pl.ds(off, ext)]


def _whole(t, ref, k):
    return ref


def _slot(t, ref, k):
    return ref.at[k]


def _fill_own(srcs, land_shapes, src_win, dst_win, name):
    nt = len(srcs)

    def body(*refs):
        src_refs, dst_refs, sems = refs[:nt], refs[nt:2 * nt], refs[2 * nt]
        x, y, c = _coords()
        me = 4 * x + 2 * y + c
        cps = [pltpu.make_async_copy(src_win(t, src_refs[t], me), dst_win(t, dst_refs[t], me), sems.at[t]) for t in range(nt)]
        for cp in cps:
            cp.start()
        for cp in cps:
            cp.wait()

    return pl.pallas_call(
        body, in_specs=[ANY] * nt, out_specs=[ANY] * nt, out_shape=land_shapes,
        scratch_shapes=[pltpu.SemaphoreType.DMA((nt,))], name=name)(*srcs)


def _xchg_start(srcs, lands, groups, src_win, dst_win, name):
    nt, ng = len(srcs), len(groups)

    def body(*refs):
        src_refs, land_refs = refs[:nt], refs[nt:2 * nt]
        sems = refs[2 * nt:2 * nt + 2 * ng]
        token = refs[-1]
        x, y, c = _coords()
        me = 4 * x + 2 * y + c
        for gi, grp in enumerate(groups):
            for j, t in enumerate(grp):
                for rel in range(1, NDEV):
                    px, py, pc = _peer(rel)
                    q = 4 * px + 2 * py + pc
                    pltpu.make_async_remote_copy(
                        src_ref=src_win(t, src_refs[t], q), dst_ref=dst_win(t, land_refs[t], me),
                        send_sem=sems[2 * gi].at[(rel - 1) * len(grp) + j],
                        recv_sem=sems[2 * gi + 1].at[(rel - 1) * len(grp) + j],
                        device_id=(px, py, pc), device_id_type=MESH).start()
        token[...] = jnp.zeros((8, LANES), F32)

    out_shape = []
    for grp in groups:
        out_shape += [pltpu.SemaphoreType.DMA(((NDEV - 1) * len(grp),))] * 2
    out_shape += [pltpu.HBM(a.shape, a.dtype) for a in list(srcs) + list(lands)]
    out_shape.append(jax.ShapeDtypeStruct((8, LANES), F32))
    outs = pl.pallas_call(
        body, in_specs=[HBM_SPEC] * (2 * nt),
        out_specs=[SEM_SPEC] * (2 * ng) + [HBM_SPEC] * (2 * nt) + [pl.BlockSpec(memory_space=pltpu.VMEM)],
        out_shape=out_shape, input_output_aliases={i: 2 * ng + i for i in range(2 * nt)},
        compiler_params=pltpu.CompilerParams(has_side_effects=pltpu.SideEffectType.DATAFLOW_SIDE_EFFECTING),
        name=name)(*[_hbm(a) for a in list(srcs) + list(lands)])
    sems = [(outs[2 * gi], outs[2 * gi + 1]) for gi in range(ng)]
    return sems, list(outs[2 * ng:2 * ng + nt]), list(outs[2 * ng + nt:2 * ng + 2 * nt]), outs[-1]


def _xchg_wait(sems, srcs, lands, tids, after, src_win, dst_win, name):
    n = len(srcs)
    send_sem, recv_sem = sems

    def body(*refs):
        src_refs, land_refs = refs[:n], refs[n:2 * n]
        ssem, rsem = refs[2 * n], refs[2 * n + 1]
        for j, t in enumerate(tids):
            for rel in range(1, NDEV):
                px, py, pc = _peer(rel)
                q = 4 * px + 2 * py + pc
                cp = pltpu.make_async_remote_copy(
                    src_ref=src_win(t, src_refs[j], q), dst_ref=dst_win(t, land_refs[j], q),
                    send_sem=ssem.at[(rel - 1) * n + j], recv_sem=rsem.at[(rel - 1) * n + j],
                    device_id=(px, py, pc), device_id_type=MESH)
                cp.wait_send()
                cp.wait_recv()

    outs = pl.pallas_call(
        body, in_specs=[HBM_SPEC] * (2 * n) + [SEM_SPEC, SEM_SPEC, ANY], out_specs=[HBM_SPEC] * (2 * n),
        out_shape=[pltpu.HBM(a.shape, a.dtype) for a in list(srcs) + list(lands)],
        input_output_aliases={i: i for i in range(2 * n)},
        compiler_params=pltpu.CompilerParams(has_side_effects=pltpu.SideEffectType.DATAFLOW_SIDE_EFFECTING),
        name=name)(*srcs, *lands, send_sem, recv_sem, after)
    return list(outs[n:])


class _Weights:
    def __init__(self, ready, pending=None):
        self.ready = dict(ready)
        self.pending = dict(pending or {})

    def __getitem__(self, k):
        return self.ready[k]

    def need(self, group, after):
        fn = self.pending.pop(group, None)
        if fn is not None:
            self.ready.update(fn(after))


def _adamw_math(w, g, m, v):
    m2 = ADAM_B1 * m + (1.0 - ADAM_B1) * g
    v2 = ADAM_B2 * v + (1.0 - ADAM_B2) * (g * g)
    m_hat = m2 / (1.0 - ADAM_B1 ** ADAM_STEP)
    v_hat = v2 / (1.0 - ADAM_B2 ** ADAM_STEP)
    delta = -ADAM_LR * (m_hat / (jnp.sqrt(v_hat) + ADAM_EPS) + ADAM_WD * w)
    return delta, m2, v2


def _adamw(parts, w, m, v, layer, prev, rows, name):
    nl, nr, nc = w.shape

    def body(p_ref, w_ref, m_ref, v_ref, *rest):
        g_ref, d_ref, m2_ref, v2_ref = rest[-4:]
        g = p_ref[0].astype(F32)
        for k in range(1, NDEV):
            g = g + p_ref[k].astype(F32)
        delta, m2, v2 = _adamw_math(w_ref[...], g, m_ref[...], v_ref[...])
        g_ref[...] = g
        d_ref[...] = delta
        m2_ref[...] = m2
        v2_ref[...] = v2

    blk = pl.BlockSpec((None, rows, nc), lambda i: (layer, i, 0))
    pblk = pl.BlockSpec((NDEV, rows, nc), lambda i: (0, i, 0))
    extra = [] if prev is None else list(prev)
    return pl.pallas_call(
        body, grid=(nr // rows,), in_specs=[pblk, blk, blk, blk] + [ANY] * len(extra), out_specs=[blk] * 4,
        out_shape=[jax.ShapeDtypeStruct(w.shape, F32)] * 4,
        input_output_aliases={4 + k: k for k in range(len(extra))},
        compiler_params=_cp("parallel"), name=name)(parts, w, m, v, *extra)


SMALL_REPL = (("rel_bias", NUM_BUCKETS * N_BIAS_HEADS), ("attn_pre_norm", DEPTH * D), ("sinks", DEPTH * 8),
              ("attn_post_norm", DEPTH * D), ("ffn_pre_norm", DEPTH * D), ("conv_b", DEPTH * 2 * D_FF),
              ("ffn_post_norm", DEPTH * D))
SMALL_SHARD = (("b_gate", (DEPTH, 3, D), 128), ("conv_w", (DEPTH, 3, 2 * D_FF), 1024))


def _pack(vecs):
    flat = jnp.concatenate([v.reshape(-1).astype(F32) for v in vecs])
    n = flat.shape[0]
    rows = -(-n // (8 * LANES)) * 8
    return jnp.pad(flat, (0, rows * LANES - n)).reshape(rows, LANES)


def _unpack(packed, sizes):
    flat = packed.reshape(-1)
    out, off = [], 0
    for sz in sizes:
        out.append(flat[off:off + sz])
        off += sz
    return out


def _small_sum(parts, name):
    r = parts.shape[1]

    def body(p_ref, o_ref):
        g = p_ref[0]
        for k in range(1, NDEV):
            g = g + p_ref[k]
        o_ref[...] = g

    return pl.pallas_call(
        body, in_specs=[pl.BlockSpec(memory_space=pltpu.VMEM)], out_specs=pl.BlockSpec(memory_space=pltpu.VMEM),
        out_shape=jax.ShapeDtypeStruct((r, LANES), F32), name=name)(parts)


def _small_adamw(g, w, m, v, name):
    def body(g_ref, w_ref, m_ref, v_ref, d_ref, m2_ref, v2_ref):
        delta, m2, v2 = _adamw_math(w_ref[...], g_ref[...], m_ref[...], v_ref[...])
        d_ref[...] = delta
        m2_ref[...] = m2
        v2_ref[...] = v2

    vm = pl.BlockSpec(memory_space=pltpu.VMEM)
    return pl.pallas_call(
        body, in_specs=[vm] * 4, out_specs=[vm] * 3,
        out_shape=[jax.ShapeDtypeStruct(g.shape, F32)] * 3, name=name)(g, w, m, v)


def kernel(x, rel_bias, attn_pre_norm, w_in, b_gate, sinks, w_br_a, w_br_b, w_br_c, w_out, attn_post_norm, ffn_pre_norm, w_up, conv_w, conv_b, w_down, ffn_post_norm, loss_target, m_rel_bias, m_attn_pre_norm, m_w_in, m_b_gate, m_sinks, m_w_br_a, m_w_br_b, m_w_br_c, m_w_out, m_attn_post_norm, m_ffn_pre_norm, m_w_up, m_conv_w, m_conv_b, m_w_down, m_ffn_post_norm, v_rel_bias, v_attn_pre_norm, v_w_in, v_b_gate, v_sinks, v_w_br_a, v_w_br_b, v_w_br_c, v_w_out, v_attn_post_norm, v_ffn_pre_norm, v_w_up, v_conv_w, v_conv_b, v_w_down, v_ffn_post_norm):
    P = dict(rel_bias=rel_bias, attn_pre_norm=attn_pre_norm, w_in=w_in, b_gate=b_gate, sinks=sinks, w_br_a=w_br_a,
             w_br_b=w_br_b, w_br_c=w_br_c, w_out=w_out, attn_post_norm=attn_post_norm, ffn_pre_norm=ffn_pre_norm,
             w_up=w_up, conv_w=conv_w, conv_b=conv_b, w_down=w_down, ffn_post_norm=ffn_post_norm)
    M = dict(rel_bias=m_rel_bias, attn_pre_norm=m_attn_pre_norm, w_in=m_w_in, b_gate=m_b_gate, sinks=m_sinks,
             w_br_a=m_w_br_a, w_br_b=m_w_br_b, w_br_c=m_w_br_c, w_out=m_w_out, attn_post_norm=m_attn_post_norm,
             ffn_pre_norm=m_ffn_pre_norm, w_up=m_w_up, conv_w=m_conv_w, conv_b=m_conv_b, w_down=m_w_down,
             ffn_post_norm=m_ffn_post_norm)
    V = dict(rel_bias=v_rel_bias, attn_pre_norm=v_attn_pre_norm, w_in=v_w_in, b_gate=v_b_gate, sinks=v_sinks,
             w_br_a=v_w_br_a, w_br_b=v_w_br_b, w_br_c=v_w_br_c, w_out=v_w_out, attn_post_norm=v_attn_post_norm,
             ffn_pre_norm=v_ffn_pre_norm, w_up=v_w_up, conv_w=v_conv_w, conv_b=v_conv_b, w_down=v_w_down,
             ffn_post_norm=v_ffn_post_norm)
    xi, yi, ci = _coords()
    me = 4 * xi + 2 * yi + ci

    srcs = [P[nm][l].astype(BF16) for l in range(DEPTH) for nm, _, _ in BIG]
    land_shapes = [jax.ShapeDtypeStruct((NDEV, D, ext) if nm == "w_in" else BIG_FULL[nm], BF16)
                   for l in range(DEPTH) for nm, _, ext in BIG]
    lands = _fill_own(srcs, land_shapes, _whole, _shard_window, "gather_own")
    groups = [tuple(l * NBIG + t for t in tids) for l in range(DEPTH) for _, tids in LAYER_GROUPS]
    g_sems, g_srcs, g_lands, g_tok = _xchg_start(srcs, lands, groups, _whole, _shard_window, "gather_start")
    tok0 = g_tok[0:1, 0:1]

    def gather_waiter(gi, l, gname, tids):
        def wait(after):
            ids = [l * NBIG + t for t in tids]
            got = _xchg_wait(g_sems[gi], [g_srcs[i] for i in ids], [g_lands[i] for i in ids], ids, after,
                             _whole, _shard_window, f"gather_wait_l{l}_{gname}")
            out = {}
            for t, arr in zip(tids, got):
                nm = BIG[t][0]
                out[nm] = jnp.transpose(arr, (1, 0, 2)).reshape(D, IN_COLS) if nm == "w_in" else arr
            return out
        return wait

    pending = [{gname: gather_waiter(l * len(LAYER_GROUPS) + k, l, gname, tids)
                for k, (gname, tids) in enumerate(LAYER_GROUPS)} for l in range(DEPTH)]
    small_w = _pack([b_gate.reshape(-1), conv_w.reshape(-1)])
    (small_w_all,) = _exchange([small_w], [jax.ShapeDtypeStruct((NDEV,) + small_w.shape, F32)],
                               lambda t, ref, q: ref, lambda t, ref, k: ref.at[k], "gather_small_weights")
    nbg = DEPTH * 3 * 128
    ncw = DEPTH * 3 * 1024
    flat_all = small_w_all.reshape(NDEV, -1)
    b_gate_full = jnp.transpose(flat_all[:, :nbg].reshape(NDEV, DEPTH, 3, 128), (1, 2, 0, 3)).reshape(DEPTH, 3, D)
    conv_w_full = jnp.transpose(flat_all[:, nbg:nbg + ncw].reshape(NDEV, DEPTH, 3, 1024), (1, 2, 0, 3)).reshape(DEPTH, 3, 2 * D_FF)

    ws = []
    for l in range(DEPTH):
        ws.append(_Weights(dict(
            b_gate=b_gate_full[l], conv_w=conv_w_full[l].reshape(3, 2, D_FF), conv_b=conv_b[l].reshape(2, D_FF),
            sinks=sinks[l].reshape(1, 8),
            attn_pre_norm=attn_pre_norm[l].reshape(1, D), attn_post_norm=attn_post_norm[l].reshape(1, D),
            ffn_pre_norm=ffn_pre_norm[l].reshape(1, D), ffn_post_norm=ffn_post_norm[l].reshape(1, D)), pending[l]))

    rs = {}

    def start_scatter(l, grads_l):
        blocks = []
        for nm, ax, ext in BIG:
            gfull = grads_l[nm].astype(BF16)
            if nm == "w_in":
                gfull = jnp.transpose(gfull.reshape(D, NDEV, ext), (1, 0, 2))
            blocks.append(gfull)
        shapes = []
        for (nm, ax, ext), blk in zip(BIG, blocks):
            if nm == "w_in":
                shp = (NDEV, D, ext)
            else:
                shp = (NDEV, ext, blk.shape[1]) if ax == 0 else (NDEV, blk.shape[0], ext)
            shapes.append(jax.ShapeDtypeStruct(shp, BF16))
        own = _fill_own(blocks, shapes, _shard_window, _slot, f"scatter_own_l{l}")
        sems, s_thru, l_thru, tok = _xchg_start(blocks, own, [tuple(range(NBIG))], _shard_window, _slot, f"scatter_start_l{l}")
        rs[l] = (sems[0], s_thru, l_thru)
        return tok[0:1, 0:1]

    loss_local, grad_x, grads, g_rel = _local_step(x[0], loss_target[0], ws, rel_bias, tok0, start_scatter)
    loss = lax.psum(loss_local, ("x", "y", "c"))

    stack = lambda nm: jnp.stack([grads[l][nm] for l in range(DEPTH)], axis=0)
    small_names = [nm for nm, _ in SMALL_REPL] + [nm for nm, _, _ in SMALL_SHARD]
    small_g = {"rel_bias": g_rel}
    for nm in small_names[1:]:
        small_g[nm] = stack(nm)
    small_packed = _pack([small_g[nm] for nm in small_names])
    (small_parts,) = _exchange([small_packed], [jax.ShapeDtypeStruct((NDEV,) + small_packed.shape, F32)],
                               _whole, _slot, "gather_small_grads")
    small_tot = _small_sum(small_parts, "small_grad_sum")
    sizes = [sz for _, sz in SMALL_REPL] + [int(np.prod(shp)) for _, shp, _ in SMALL_SHARD]
    small_tot = dict(zip(small_names, _unpack(small_tot, sizes)))

    out_g, out_d, out_m, out_v = {}, {}, {}, {}
    prev = {nm: None for nm, _, _ in BIG}
    for l in reversed(range(DEPTH)):
        sems, s_thru, l_thru = rs[l]
        parts = _xchg_wait(sems, s_thru, l_thru, list(range(NBIG)), small_parts, _shard_window, _slot, f"scatter_wait_l{l}")
        for (nm, ax, ext), prt in zip(BIG, parts):
            rows = {"w_in": 256, "w_up": 256, "w_down": 256}.get(nm, P[nm].shape[1])
            prev[nm] = _adamw(prt, P[nm], M[nm], V[nm], l, prev[nm], rows, f"adamw_{nm}_l{l}")
    for nm, _, _ in BIG:
        out_g[nm], out_d[nm], out_m[nm], out_v[nm] = prev[nm]
    gsm = {}
    for nm, _ in SMALL_REPL:
        gsm[nm] = small_tot[nm].reshape(P[nm].shape)
    for nm, shp, ext in SMALL_SHARD:
        gsm[nm] = lax.dynamic_slice_in_dim(small_tot[nm].reshape(shp), me * ext, ext, axis=2)
    pk = lambda dct: _pack([dct[nm] for nm in small_names])
    d_s, m_s, v_s = _small_adamw(pk(gsm), pk(P), pk(M), pk(V), "adamw_small")
    szs = [int(np.prod(P[nm].shape)) for nm in small_names]
    for dst, packed in ((out_d, d_s), (out_m, m_s), (out_v, v_s)):
        for nm, piece in zip(small_names, _unpack(packed, szs)):
            dst[nm] = piece.reshape(P[nm].shape)
    for nm in small_names:
        out_g[nm] = gsm[nm]

    order = ["rel_bias", "attn_pre_norm", "w_in", "b_gate", "sinks", "w_br_a", "w_br_b", "w_br_c", "w_out",
             "attn_post_norm", "ffn_pre_norm", "w_up", "conv_w", "conv_b", "w_down", "ffn_post_norm"]
    return (loss, grad_x[None], *[out_g[k] for k in order], *[out_d[k] for k in order],
            *[out_m[k] for k in order], *[out_v[k] for k in order])
```

```python
import functools
import math

import numpy as np
import jax
import jax.numpy as jnp
from jax import lax
from jax.experimental import pallas as pl
from jax.experimental.pallas import tpu as pltpu

F32 = jnp.float32
BF16 = jnp.bfloat16

S = 2048
D = 1024
DEPTH = 2
NDEV = 8
HD = 64
BLK = 128
NB = S // BLK
A_GROUPS = ((128, 1), (512, 4), (2048, 16))
NUM_BUCKETS = 32
MAX_DISTANCE = 2048
N_BIAS_HEADS = 20
D_FF = 4096
IN_COLS = 6912
QKV_COLS = 3840
QKV_SLABS = QKV_COLS // 128
GATE_COLS = 3072
EPS = 1e-6
SCALE = HD ** -0.5
NEG = -1e30
LANES = 128

ADAM_LR = 0.001
ADAM_B1 = 0.9
ADAM_B2 = 0.999
ADAM_EPS = 1e-08
ADAM_WD = 0.01
ADAM_STEP = 10

VMEM_LIMIT = 56 * 1024 * 1024
MESH = pl.DeviceIdType.MESH
ANY = pl.BlockSpec(memory_space=pl.ANY)
SMEM = pl.BlockSpec(memory_space=pltpu.SMEM)


def _cp(*sem):
    return pltpu.CompilerParams(dimension_semantics=sem if sem else None, vmem_limit_bytes=VMEM_LIMIT)


def _dot(a, b, ca, cb):
    return lax.dot_general(a, b, (((ca,), (cb,)), ((), ())), preferred_element_type=F32)


def _mm(a, b, *, grid, a_spec, b_spec, out_shape, out_spec, ca, cb, acc_shape, name,
        a_slab=False, b_slab=False, out_slab=False, alias_out=None):
    nk = grid[2]

    def body(*refs):
        if alias_out is not None:
            a_ref, b_ref, _, o_ref, acc_ref = refs
        else:
            a_ref, b_ref, o_ref, acc_ref = refs
        k = pl.program_id(2)

        @pl.when(k == 0)
        def _():
            acc_ref[...] = jnp.zeros(acc_shape, F32)

        def load(ref, slab):
            if slab:
                return jnp.concatenate([ref[s] for s in range(ref.shape[0])], axis=1).astype(BF16)
            return ref[...].astype(BF16)

        acc_ref[...] += _dot(load(a_ref, a_slab), load(b_ref, b_slab), ca, cb)

        @pl.when(k == nk - 1)
        def _():
            if out_slab:
                for s in range(o_ref.shape[0]):
                    o_ref[s] = acc_ref[:, s * LANES:(s + 1) * LANES].astype(o_ref.dtype)
            else:
                o_ref[...] = acc_ref[...].astype(o_ref.dtype)

    in_specs = [a_spec, b_spec]
    args = [a, b]
    aliases = {}
    if alias_out is not None:
        in_specs.append(ANY)
        args.append(alias_out)
        aliases = {2: 0}
    return pl.pallas_call(
        body, grid=grid, in_specs=in_specs, out_specs=out_spec, out_shape=out_shape,
        scratch_shapes=[pltpu.VMEM(acc_shape, F32)], input_output_aliases=aliases,
        compiler_params=_cp("parallel", "parallel", "arbitrary"), name=name)(*args)


def _mm_nn(a, b, out_dtype, tm, tn, tk, name):
    m, kk = a.shape
    n = b.shape[1]
    return _mm(a, b, grid=(m // tm, n // tn, kk // tk),
               a_spec=pl.BlockSpec((tm, tk), lambda i, j, k: (i, k)),
               b_spec=pl.BlockSpec((tk, tn), lambda i, j, k: (k, j)),
               out_shape=jax.ShapeDtypeStruct((m, n), out_dtype),
               out_spec=pl.BlockSpec((tm, tn), lambda i, j, k: (i, j)),
               ca=1, cb=0, acc_shape=(tm, tn), name=name)


def _mm_nt(a, b, out_dtype, tm, tn, tk, name):
    m, kk = a.shape
    n = b.shape[0]
    return _mm(a, b, grid=(m // tm, n // tn, kk // tk),
               a_spec=pl.BlockSpec((tm, tk), lambda i, j, k: (i, k)),
               b_spec=pl.BlockSpec((tn, tk), lambda i, j, k: (j, k)),
               out_shape=jax.ShapeDtypeStruct((m, n), out_dtype),
               out_spec=pl.BlockSpec((tm, tn), lambda i, j, k: (i, j)),
               ca=1, cb=1, acc_shape=(tm, tn), name=name)


def _mm_tn(a, b, out_dtype, tm, tn, tk, name):
    kk, m = a.shape
    n = b.shape[1]
    return _mm(a, b, grid=(m // tm, n // tn, kk // tk),
               a_spec=pl.BlockSpec((tk, tm), lambda i, j, k: (k, i)),
               b_spec=pl.BlockSpec((tk, tn), lambda i, j, k: (k, j)),
               out_shape=jax.ShapeDtypeStruct((m, n), out_dtype),
               out_spec=pl.BlockSpec((tm, tn), lambda i, j, k: (i, j)),
               ca=0, cb=0, acc_shape=(tm, tn), name=name)


ROW_TILE = 256


def _rms(x, g):
    r = lax.rsqrt(jnp.mean(x * x, axis=-1, keepdims=True) + EPS)
    return x * r * g


def _prenorm(x, g, name):
    def body(x_ref, g_ref, o_ref):
        o_ref[...] = _rms(x_ref[...], g_ref[...]).astype(BF16)

    return pl.pallas_call(
        body, grid=(S // ROW_TILE,),
        in_specs=[pl.BlockSpec((ROW_TILE, D), lambda i: (i, 0)), pl.BlockSpec((1, D), lambda i: (0, 0))],
        out_specs=pl.BlockSpec((ROW_TILE, D), lambda i: (i, 0)),
        out_shape=jax.ShapeDtypeStruct((S, D), BF16), compiler_params=_cp("parallel"), name=name)(x, g)


def _postnorm_res(x, f, g_post, g_next, name):
    def body(x_ref, f_ref, gp_ref, gn_ref, xo_ref, ho_ref):
        xn = x_ref[...] + _rms(f_ref[...], gp_ref[...])
        xo_ref[...] = xn
        ho_ref[...] = _rms(xn, gn_ref[...]).astype(BF16)

    row = pl.BlockSpec((ROW_TILE, D), lambda i: (i, 0))
    vec = pl.BlockSpec((1, D), lambda i: (0, 0))
    return pl.pallas_call(
        body, grid=(S // ROW_TILE,), in_specs=[row, row, vec, vec], out_specs=[row, row],
        out_shape=[jax.ShapeDtypeStruct((S, D), F32), jax.ShapeDtypeStruct((S, D), BF16)],
        compiler_params=_cp("parallel"), name=name)(x, f, g_post, g_next)


def _norm_bwd(f, g, dys, res, out_dtype, name):
    ndy = len(dys)
    has_res = res is not None

    def body(*refs):
        f_ref, g_ref = refs[0], refs[1]
        dy_refs = refs[2:2 + ndy]
        res_ref = refs[2 + ndy] if has_res else None
        o_ref, dg_ref = refs[-2], refs[-1]
        fv = f_ref[...]
        dy = dy_refs[0][...].astype(F32)
        for r in dy_refs[1:]:
            dy = dy + r[...].astype(F32)
        r = lax.rsqrt(jnp.mean(fv * fv, axis=-1, keepdims=True) + EPS)
        n = fv * r
        dn = dy * g_ref[...]
        df = r * (dn - n * jnp.mean(dn * n, axis=-1, keepdims=True))
        if has_res:
            df = df + res_ref[...]
        o_ref[...] = df.astype(out_dtype)

        @pl.when(pl.program_id(0) == 0)
        def _():
            dg_ref[...] = jnp.zeros((1, D), F32)

        dg_ref[...] += jnp.sum(dy * n, axis=0, keepdims=True)

    row = pl.BlockSpec((ROW_TILE, D), lambda i: (i, 0))
    vec = pl.BlockSpec((1, D), lambda i: (0, 0))
    in_specs = [row, vec] + [row] * ndy + ([row] if has_res else [])
    args = [f, g] + list(dys) + ([res] if has_res else [])
    return pl.pallas_call(
        body, grid=(S // ROW_TILE,), in_specs=in_specs, out_specs=[row, vec],
        out_shape=[jax.ShapeDtypeStruct((S, D), out_dtype), jax.ShapeDtypeStruct((1, D), F32)],
        compiler_params=_cp("arbitrary"), name=name)(*args)


def _loss_head(y, target, name):
    def body(y_ref, t_ref, dy_ref, l_ref):
        e = y_ref[...] - t_ref[...]
        dy_ref[...] = e * (1.0 / D)

        @pl.when(pl.program_id(0) == 0)
        def _():
            l_ref[...] = jnp.zeros((8, LANES), F32)

        l_ref[...] += jnp.sum(e * e) * (0.5 / D)

    row = pl.BlockSpec((ROW_TILE, D), lambda i: (i, 0))
    return pl.pallas_call(
        body, grid=(S // ROW_TILE,), in_specs=[row, row],
        out_specs=[row, pl.BlockSpec((8, LANES), lambda i: (0, 0))],
        out_shape=[jax.ShapeDtypeStruct((S, D), F32), jax.ShapeDtypeStruct((8, LANES), F32)],
        compiler_params=_cp("arbitrary"), name=name)(y, target)


def _bucket_tiles():
    a = np.arange(BLK)[:, None]
    b = np.arange(2 * BLK)[None, :]
    dist = a + BLK - b
    out = np.zeros((4, 2, BLK, 2 * BLK), np.int32)
    cfg = [(w // d, d) for w, d in A_GROUPS] + [(BLK - 1, 1)]
    for gi, (max_dist, d) in enumerate(cfg):
        band = (dist >= 0) & (dist <= max_dist)
        tok = np.maximum(dist, 0) * d
        nf = np.maximum(tok, 1).astype(np.float32)
        max_exact = NUM_BUCKETS // 2
        large = max_exact + (np.log(nf / np.float32(max_exact)) / np.float32(math.log(MAX_DISTANCE / max_exact))
                             * np.float32(NUM_BUCKETS - max_exact)).astype(np.int32)
        large = np.minimum(large, NUM_BUCKETS - 1)
        bkt = np.where(tok < max_exact, tok, large).astype(np.int32)
        full = np.where(band, bkt, -1)
        out[gi, 1] = full
        out[gi, 0] = np.where(b >= BLK, full, -1)
    return out


def _bias_tiles(rel_bias, buckets, name):
    def body(tab_ref, bkt_ref, o_ref):
        h = pl.program_id(0)
        bkt = bkt_ref[...]
        acc = jnp.zeros(bkt.shape, F32)
        for bb in range(NUM_BUCKETS):
            acc = jnp.where(bkt == bb, tab_ref[bb, h], acc)
        o_ref[...] = jnp.where(bkt < 0, NEG, acc)

    return pl.pallas_call(
        body, grid=(N_BIAS_HEADS,),
        in_specs=[SMEM, pl.BlockSpec((None, 2, BLK, 2 * BLK), lambda h: (jnp.minimum(h // 4, 3), 0, 0, 0))],
        out_specs=pl.BlockSpec((None, 2, BLK, 2 * BLK), lambda h: (h, 0, 0, 0)),
        out_shape=jax.ShapeDtypeStruct((N_BIAS_HEADS, 2, BLK, 2 * BLK), F32),
        compiler_params=_cp("arbitrary"), name=name)(rel_bias, buckets)


def _bias_grad(gs, buckets, name):
    ng = len(gs)

    def body(*refs):
        g_refs = refs[:ng]
        bkt_ref, o_ref = refs[ng], refs[ng + 1]
        h = pl.program_id(0)
        g = g_refs[0][...]
        for r in g_refs[1:]:
            g = g + r[...]
        bkt = bkt_ref[...]
        row = lax.broadcasted_iota(jnp.int32, (NUM_BUCKETS, LANES), 0)
        lane = lax.broadcasted_iota(jnp.int32, (NUM_BUCKETS, LANES), 1)

        @pl.when(h == 0)
        def _():
            o_ref[...] = jnp.zeros((NUM_BUCKETS, LANES), F32)

        acc = o_ref[...]
        for bb in range(NUM_BUCKETS):
            s = jnp.sum(jnp.where(bkt == bb, g, 0.0))
            acc = jnp.where((row == bb) & (lane == h), s, acc)
        o_ref[...] = acc

    g_spec = pl.BlockSpec((None, BLK, 2 * BLK), lambda h: (h, 0, 0))
    return pl.pallas_call(
        body, grid=(N_BIAS_HEADS,),
        in_specs=[g_spec] * ng + [pl.BlockSpec((None, None, BLK, 2 * BLK), lambda h: (jnp.minimum(h // 4, 3), 1, 0, 0))],
        out_specs=pl.BlockSpec((NUM_BUCKETS, LANES), lambda h: (0, 0)),
        out_shape=jax.ShapeDtypeStruct((NUM_BUCKETS, LANES), F32),
        compiler_params=_cp("arbitrary"), name=name)(*gs, buckets)


def _to_class_major(src_ref, dst_ref, d, scale=None, dtype=None):
    ln = S // d
    for r in range(d):
        v = src_ref[pl.ds(r, ln, stride=d), :] if d > 1 else src_ref[...]
        if scale is not None:
            v = v * scale
        dst_ref[pl.ds(r * ln, ln), :] = v.astype(dtype or dst_ref.dtype)


def _block_rows(b, d):
    nbc = NB // d
    i = b % nbc
    r = b // nbc
    has_prev = (i > 0).astype(jnp.int32)
    prev = pl.multiple_of(jnp.maximum(b - 1, 0) * BLK, BLK)
    nat = i * (BLK * d) + r
    return has_prev, prev, nat


def _lane_halves(v0, v1):
    lane = lax.broadcasted_iota(jnp.int32, (v0.shape[0], LANES), 1)
    return jnp.where(lane < HD, v0, v1)


def _band_fwd(proj, bias, *, d, q0, k0, v0, npairs, bias0, shared_kv, name):
    def body(q_ref, k_ref, v_ref, b_ref, num_ref, st_ref, qs, ks, vs):
        p = pl.program_id(0)
        _to_class_major(q_ref, qs, d, scale=SCALE)
        _to_class_major(k_ref, ks, d)
        _to_class_major(v_ref, vs, d)
        lane = lax.broadcasted_iota(jnp.int32, (BLK, LANES), 1)

        def blk(b, carry):
            has_prev, prev, nat = _block_rows(b, d)
            cur = pl.multiple_of(b * BLK, BLK)
            qb = qs[pl.ds(cur, BLK), :]
            k2 = jnp.concatenate([ks[pl.ds(prev, BLK), :], ks[pl.ds(cur, BLK), :]], axis=0)
            v2 = jnp.concatenate([vs[pl.ds(prev, BLK), :], vs[pl.ds(cur, BLK), :]], axis=0)
            nums, ms, ls = [], [], []
            for hh in range(2):
                qh = qb[:, hh * HD:(hh + 1) * HD]
                if shared_kv:
                    kh = jnp.where(p >= 2, k2[:, HD:], k2[:, :HD])
                    vh = jnp.where(p >= 2, v2[:, HD:], v2[:, :HD])
                else:
                    kh = k2[:, hh * HD:(hh + 1) * HD]
                    vh = v2[:, hh * HD:(hh + 1) * HD]
                z = _dot(qh, kh, 1, 1) + b_ref[hh, has_prev]
                m = jnp.max(z, axis=1, keepdims=True)
                e = jnp.exp(z - m)
                ls.append(jnp.sum(e, axis=1, keepdims=True))
                ms.append(m)
                nums.append(_dot(e.astype(BF16), vh, 1, 0))
            num_t = jnp.concatenate(nums, axis=1)
            st_t = jnp.where(lane < 32, ms[0], jnp.where(lane < 64, ls[0], jnp.where(lane < 96, ms[1], ls[1])))
            if d > 1:
                num_ref[pl.ds(nat, BLK, stride=d), :] = num_t
                st_ref[pl.ds(nat, BLK, stride=d), :] = st_t
            else:
                num_ref[pl.ds(cur, BLK), :] = num_t
                st_ref[pl.ds(cur, BLK), :] = st_t
            return carry

        lax.fori_loop(0, NB, blk, 0)

    slab = lambda off, per_pair: pl.BlockSpec((None, S, LANES), (lambda p: (off + p, 0, 0)) if per_pair else (lambda p: (off, 0, 0)))
    out = pl.BlockSpec((None, S, LANES), lambda p: (p, 0, 0))
    return pl.pallas_call(
        body, grid=(npairs,),
        in_specs=[slab(q0, True), slab(k0, not shared_kv), slab(v0, not shared_kv),
                  pl.BlockSpec((None, 2, 2, BLK, 2 * BLK), lambda p: (bias0 + p, 0, 0, 0, 0))],
        out_specs=[out, out],
        out_shape=[jax.ShapeDtypeStruct((npairs, S, LANES), F32)] * 2,
        scratch_shapes=[pltpu.VMEM((S, LANES), BF16)] * 3,
        compiler_params=_cp("arbitrary"), name=name)(proj, proj, proj, bias)


def _combine_a(nums, stats, name):
    rt = 512

    def body(n0, n1, n2, s0, s1, s2, o_ref, l_ref):
        n_refs, s_refs = (n0, n1, n2), (s0, s1, s2)
        outs, lses = [], []
        for hh in range(2):
            ms = [s[:, 64 * hh:64 * hh + 1] for s in s_refs]
            ls = [s[:, 64 * hh + 32:64 * hh + 33] for s in s_refs]
            mx = jnp.maximum(jnp.maximum(ms[0], ms[1]), ms[2])
            cs = [jnp.exp(m - mx) for m in ms]
            z = cs[0] * ls[0] + cs[1] * ls[1] + cs[2] * ls[2]
            acc = cs[0] * n_refs[0][:, hh * HD:(hh + 1) * HD]
            acc = acc + cs[1] * n_refs[1][:, hh * HD:(hh + 1) * HD]
            acc = acc + cs[2] * n_refs[2][:, hh * HD:(hh + 1) * HD]
            outs.append(acc / z)
            lses.append(mx + jnp.log(z))
        o_ref[...] = jnp.concatenate(outs, axis=1)
        l_ref[...] = _lane_halves(lses[0], lses[1])

    spec = pl.BlockSpec((None, rt, LANES), lambda p, i: (p, i, 0))
    return pl.pallas_call(
        body, grid=(2, S // rt), in_specs=[spec] * 6, out_specs=[spec, spec],
        out_shape=[jax.ShapeDtypeStruct((2, S, LANES), F32)] * 2,
        compiler_params=_cp("parallel", "parallel"), name=name)(*nums, *stats)


def _combine_b(num, stats, sinks, name):
    rt = 512

    def body(sink_ref, n_ref, s_ref, o_ref, l_ref):
        p = pl.program_id(0)
        outs, lses = [], []
        for hh in range(2):
            sink = sink_ref[0, 2 * p + hh]
            m = s_ref[:, 64 * hh:64 * hh + 1]
            l = s_ref[:, 64 * hh + 32:64 * hh + 33]
            mx = jnp.maximum(m, sink)
            c = jnp.exp(m - mx)
            z = l * c + jnp.exp(sink - mx)
            outs.append(n_ref[:, hh * HD:(hh + 1) * HD] * (c / z))
            lses.append(mx + jnp.log(z))
        o_ref[...] = jnp.concatenate(outs, axis=1)
        l_ref[...] = _lane_halves(lses[0], lses[1])

    spec = pl.BlockSpec((None, rt, LANES), lambda p, i: (p, i, 0))
    return pl.pallas_call(
        body, grid=(4, S // rt), in_specs=[SMEM, spec, spec], out_specs=[spec, spec],
        out_shape=[jax.ShapeDtypeStruct((4, S, LANES), F32)] * 2,
        compiler_params=_cp("parallel", "parallel"), name=name)(sinks, num, stats)


def _band_bwd(proj, bias, o, do, lse, sinks, *, d, q0, k0, v0, npairs, bias0, shared_kv, name):
    nkv = 1 if shared_kv else npairs

    def body(sink_ref, q_ref, k_ref, v_ref, b_ref, o_ref, do_ref, lse_ref,
             dq_ref, dk_ref, dv_ref, g_ref, ds_ref,
             qs, ks, vs, dos, lses, dls, dl_nat, dq_nat, dk_cm, dv_cm, kv_nat):
        p = pl.program_id(0)
        lane = lax.broadcasted_iota(jnp.int32, (S, LANES), 1)
        dov = do_ref[...]
        prod = dov * o_ref[...]
        dl0 = jnp.sum(jnp.where(lane < HD, prod, 0.0), axis=1, keepdims=True)
        dl1 = jnp.sum(jnp.where(lane >= HD, prod, 0.0), axis=1, keepdims=True)
        dl_nat[...] = jnp.where(lane < HD, dl0, dl1)
        if shared_kv:
            row8 = lax.broadcasted_iota(jnp.int32, (8, LANES), 0)
            lane8 = lax.broadcasted_iota(jnp.int32, (8, LANES), 1)
            t = jnp.zeros((8, LANES), F32)
            lv = lse_ref[...]
            for hh in range(2):
                sink = sink_ref[0, 2 * p + hh]
                ps = jnp.exp(sink - lv[:, 64 * hh:64 * hh + 1])
                dsink = -jnp.sum(ps * (dl0 if hh == 0 else dl1))
                t = jnp.where((row8 == 0) & (lane8 == hh), dsink, t)
            ds_ref[...] = t
        else:
            ds_ref[...] = jnp.zeros((8, LANES), F32)
        _to_class_major(q_ref, qs, d, scale=SCALE)
        _to_class_major(k_ref, ks, d)
        _to_class_major(v_ref, vs, d)
        _to_class_major(do_ref, dos, d)
        _to_class_major(lse_ref, lses, d)
        _to_class_major(dl_nat, dls, d)

        def zero_kv():
            dk_cm[...] = jnp.zeros((S, LANES), F32)
            dv_cm[...] = jnp.zeros((S, LANES), F32)

        if shared_kv:
            pl.when(p == 0)(zero_kv)
        else:
            zero_kv()

        g_ref[...] = jnp.zeros((2, BLK, 2 * BLK), F32)
        lane2 = lax.broadcasted_iota(jnp.int32, (2 * BLK, LANES), 1)

        def blk(b, carry):
            has_prev, prev, nat = _block_rows(b, d)
            cur = pl.multiple_of(b * BLK, BLK)
            qb = qs[pl.ds(cur, BLK), :]
            dob = dos[pl.ds(cur, BLK), :]
            lb = lses[pl.ds(cur, BLK), :]
            dlb = dls[pl.ds(cur, BLK), :]
            k2 = jnp.concatenate([ks[pl.ds(prev, BLK), :], ks[pl.ds(cur, BLK), :]], axis=0)
            v2 = jnp.concatenate([vs[pl.ds(prev, BLK), :], vs[pl.ds(cur, BLK), :]], axis=0)
            dqs, dks, dvs = [], [], []
            for hh in range(2):
                qh = qb[:, hh * HD:(hh + 1) * HD]
                doh = dob[:, hh * HD:(hh + 1) * HD]
                if shared_kv:
                    kh = jnp.where(p >= 2, k2[:, HD:], k2[:, :HD])
                    vh = jnp.where(p >= 2, v2[:, HD:], v2[:, :HD])
                else:
                    kh = k2[:, hh * HD:(hh + 1) * HD]
                    vh = v2[:, hh * HD:(hh + 1) * HD]
                z = _dot(qh, kh, 1, 1) + b_ref[hh, has_prev]
                pr = jnp.exp(z - lb[:, 64 * hh:64 * hh + 1])
                dp = _dot(doh, vh, 1, 1)
                dz = pr * (dp - dlb[:, 64 * hh:64 * hh + 1])
                g_ref[hh] += dz
                dzb = dz.astype(BF16)
                dqs.append(_dot(dzb, kh, 1, 0) * SCALE)
                dks.append(_dot(dzb, qh, 0, 0))
                dvs.append(_dot(pr.astype(BF16), doh, 0, 0))
            dq_t = jnp.concatenate(dqs, axis=1)
            if shared_kv:
                dk_t = jnp.concatenate([dks[0] + dks[1]] * 2, axis=1)
                dv_t = jnp.concatenate([dvs[0] + dvs[1]] * 2, axis=1)
                mine = (lane2 >= HD) == (p >= 2)
                dk_t = jnp.where(mine, dk_t, 0.0)
                dv_t = jnp.where(mine, dv_t, 0.0)
            else:
                dk_t = jnp.concatenate(dks, axis=1)
                dv_t = jnp.concatenate(dvs, axis=1)
            dk_cm[pl.ds(prev, BLK), :] += dk_t[:BLK]
            dk_cm[pl.ds(cur, BLK), :] += dk_t[BLK:]
            dv_cm[pl.ds(prev, BLK), :] += dv_t[:BLK]
            dv_cm[pl.ds(cur, BLK), :] += dv_t[BLK:]
            if d > 1:
                dq_nat[pl.ds(nat, BLK, stride=d), :] = dq_t
            else:
                dq_nat[pl.ds(cur, BLK), :] = dq_t
            return carry

        lax.fori_loop(0, NB, blk, 0)
        dq_ref[...] = dq_nat[...].astype(BF16)

        def from_class_major(src, dst_ref):
            if d == 1:
                dst_ref[...] = src[...].astype(BF16)
            else:
                ln = S // d
                for r in range(d):
                    kv_nat[pl.ds(r, ln, stride=d), :] = src[pl.ds(r * ln, ln), :]
                dst_ref[...] = kv_nat[...].astype(BF16)

        def write_kv():
            from_class_major(dk_cm, dk_ref)
            from_class_major(dv_cm, dv_ref)

        if shared_kv:
            pl.when(p == npairs - 1)(write_kv)
        else:
            write_kv()

    slab = lambda off, per_pair: pl.BlockSpec((None, S, LANES), (lambda p: (off + p, 0, 0)) if per_pair else (lambda p: (off, 0, 0)))
    pair = pl.BlockSpec((None, S, LANES), lambda p: (p, 0, 0))
    kv_out = pair if not shared_kv else pl.BlockSpec((None, S, LANES), lambda p: (0, 0, 0))
    return pl.pallas_call(
        body, grid=(npairs,),
        in_specs=[SMEM, slab(q0, True), slab(k0, not shared_kv), slab(v0, not shared_kv),
                  pl.BlockSpec((None, 2, 2, BLK, 2 * BLK), lambda p: (bias0 + p, 0, 0, 0, 0)),
                  pair, pair, pair],
        out_specs=[pair, kv_out, kv_out,
                   pl.BlockSpec((None, 2, BLK, 2 * BLK), lambda p: (p, 0, 0, 0)),
                   pl.BlockSpec((None, 8, LANES), lambda p: (p, 0, 0))],
        out_shape=[jax.ShapeDtypeStruct((npairs, S, LANES), BF16),
                   jax.ShapeDtypeStruct((nkv, S, LANES), BF16),
                   jax.ShapeDtypeStruct((nkv, S, LANES), BF16),
                   jax.ShapeDtypeStruct((npairs, 2, BLK, 2 * BLK), F32),
                   jax.ShapeDtypeStruct((npairs, 8, LANES), F32)],
        scratch_shapes=[pltpu.VMEM((S, LANES), BF16)] * 4 + [pltpu.VMEM((S, LANES), F32)] * 7,
        compiler_params=_cp("arbitrary"), name=name)(sinks, proj, proj, proj, bias, o, do, lse)


KC = 512
NSUB = KC // BLK


def _split2(x):
    hi = x.astype(BF16)
    lo = (x - hi.astype(F32)).astype(BF16)
    return hi, lo


def _sub_prefix(x, tri):
    st = jnp.concatenate([x[:, s * BLK:(s + 1) * BLK] for s in range(NSUB)], axis=0)
    hi, lo = _split2(st)
    r = _dot(hi, tri, 1, 0) + _dot(lo, tri, 1, 0)
    return [r[s * BLK:(s + 1) * BLK] for s in range(NSUB)]


def _log_sig_pair(z):
    sp = jnp.log1p(jnp.exp(-jnp.abs(z)))
    return jnp.minimum(z, 0.0) - sp, jnp.minimum(-z, 0.0) - sp


def _stick_fwd(proj, *, q0, k0, v0, name):
    def body(q_ref, k_ref, v_ref, o_ref, t_ref, qs, ks, vs):
        qs[...] = (q_ref[...] * SCALE).astype(BF16)
        ks[...] = k_ref[...].astype(BF16)
        vs[...] = v_ref[...].astype(BF16)
        jj = lax.broadcasted_iota(jnp.int32, (BLK, BLK), 0)
        ss = lax.broadcasted_iota(jnp.int32, (BLK, BLK), 1)
        tri = (jj > ss).astype(BF16)
        col = lax.broadcasted_iota(jnp.int32, (BLK, KC), 1)
        rowi = lax.broadcasted_iota(jnp.int32, (BLK, KC), 0)

        def qblock(i, carry0):
            t0 = pl.multiple_of(i * BLK, BLK)
            qb = qs[pl.ds(t0, BLK), :]
            nch = i // NSUB + 1
            outs, tots = [], []
            for hh in range(2):
                qh = qb[:, hh * HD:(hh + 1) * HD]

                def chunk(cc, st):
                    acc, run = st
                    c = nch - 1 - cc
                    s0 = pl.multiple_of(c * KC, KC)
                    kh = ks[pl.ds(s0, KC), hh * HD:(hh + 1) * HD]
                    vh = vs[pl.ds(s0, KC), hh * HD:(hh + 1) * HD]
                    z = _dot(qh, kh, 1, 1)
                    before = (s0 + col) < (t0 + rowi)
                    lb, lk = _log_sig_pair(z)
                    lk = jnp.where(before, lk, 0.0)
                    suf = _sub_prefix(lk, tri)
                    ws = []
                    for s in reversed(range(NSUB)):
                        lks = lk[:, s * BLK:(s + 1) * BLK]
                        rest = suf[s] + run
                        ws.append(jnp.exp(lb[:, s * BLK:(s + 1) * BLK] + rest))
                        run = run + jnp.sum(lks, axis=1, keepdims=True)
                    w = jnp.concatenate(ws[::-1], axis=1)
                    w = jnp.where(before, w, 0.0)
                    return acc + _dot(w.astype(BF16), vh, 1, 0), run

                acc, run = lax.fori_loop(0, nch, chunk, (jnp.zeros((BLK, HD), F32), jnp.zeros((BLK, 1), F32)))
                outs.append(acc)
                tots.append(run)
            o_ref[pl.ds(t0, BLK), :] = jnp.concatenate(outs, axis=1)
            t_ref[pl.ds(t0, BLK), :] = _lane_halves(tots[0], tots[1])
            return carry0

        lax.fori_loop(0, NB, qblock, 0)

    slab = lambda off: pl.BlockSpec((None, S, LANES), lambda p: (off + p, 0, 0))
    out = pl.BlockSpec((None, S, LANES), lambda p: (p, 0, 0))
    return pl.pallas_call(
        body, grid=(2,), in_specs=[slab(q0), slab(k0), slab(v0)], out_specs=[out, out],
        out_shape=[jax.ShapeDtypeStruct((2, S, LANES), F32)] * 2,
        scratch_shapes=[pltpu.VMEM((S, LANES), BF16)] * 3,
        compiler_params=_cp("arbitrary"), name=name)(proj, proj, proj)


def _stick_bwd(proj, do, tot, *, q0, k0, v0, name):
    def body(q_ref, k_ref, v_ref, do_ref, t_ref, dq_ref, dk_ref, dv_ref, qs, ks, vs, dos, dk_acc, dv_acc):
        qs[...] = (q_ref[...] * SCALE).astype(BF16)
        ks[...] = k_ref[...].astype(BF16)
        vs[...] = v_ref[...].astype(BF16)
        dos[...] = do_ref[...].astype(BF16)
        dk_acc[...] = jnp.zeros((2, S, HD), F32)
        dv_acc[...] = jnp.zeros((2, S, HD), F32)
        jj = lax.broadcasted_iota(jnp.int32, (BLK, BLK), 0)
        ss = lax.broadcasted_iota(jnp.int32, (BLK, BLK), 1)
        tri_inc = (jj <= ss).astype(BF16)
        tri_exc = (jj < ss).astype(BF16)
        col = lax.broadcasted_iota(jnp.int32, (BLK, KC), 1)
        rowi = lax.broadcasted_iota(jnp.int32, (BLK, KC), 0)

        def qblock(i, carry0):
            t0 = pl.multiple_of(i * BLK, BLK)
            qb = qs[pl.ds(t0, BLK), :]
            dob = dos[pl.ds(t0, BLK), :]
            tb = t_ref[pl.ds(t0, BLK), :]
            nch = i // NSUB + 1
            dqs = []
            for hh in range(2):
                qh = qb[:, hh * HD:(hh + 1) * HD]
                doh = dob[:, hh * HD:(hh + 1) * HD]
                tt = tb[:, 64 * hh:64 * hh + 1]

                def chunk(c, st):
                    dq, prun, erun = st
                    s0 = pl.multiple_of(c * KC, KC)
                    kh = ks[pl.ds(s0, KC), hh * HD:(hh + 1) * HD]
                    vh = vs[pl.ds(s0, KC), hh * HD:(hh + 1) * HD]
                    z = _dot(qh, kh, 1, 1)
                    before = (s0 + col) < (t0 + rowi)
                    lb, lk = _log_sig_pair(z)
                    lk = jnp.where(before, lk, 0.0)
                    pin = _sub_prefix(lk, tri_inc)
                    ws = []
                    for s in range(NSUB):
                        rest = tt - (pin[s] + prun)
                        ws.append(jnp.exp(lb[:, s * BLK:(s + 1) * BLK] + rest))
                        prun = prun + jnp.sum(lk[:, s * BLK:(s + 1) * BLK], axis=1, keepdims=True)
                    w = jnp.where(before, jnp.concatenate(ws, axis=1), 0.0)
                    dw = _dot(doh, vh, 1, 1)
                    e = w * dw
                    pex = _sub_prefix(e, tri_exc)
                    cs = []
                    for s in range(NSUB):
                        cs.append(pex[s] + erun)
                        erun = erun + jnp.sum(e[:, s * BLK:(s + 1) * BLK], axis=1, keepdims=True)
                    cex = jnp.concatenate(cs, axis=1)
                    sig = jnp.exp(lb)
                    dz = jnp.where(before, e * (1.0 - sig) - cex * sig, 0.0).astype(BF16)
                    dq = dq + _dot(dz, kh, 1, 0)
                    dk_acc[hh, pl.ds(s0, KC), :] += _dot(dz, qh, 0, 0)
                    dv_acc[hh, pl.ds(s0, KC), :] += _dot(w.astype(BF16), doh, 0, 0)
                    return dq, prun, erun

                init = (jnp.zeros((BLK, HD), F32), jnp.zeros((BLK, 1), F32), jnp.zeros((BLK, 1), F32))
                dq, _, _ = lax.fori_loop(0, nch, chunk, init)
                dqs.append(dq * SCALE)
            dq_ref[pl.ds(t0, BLK), :] = jnp.concatenate(dqs, axis=1).astype(BF16)
            return carry0

        lax.fori_loop(0, NB, qblock, 0)
        dk_ref[...] = jnp.concatenate([dk_acc[0], dk_acc[1]], axis=1).astype(BF16)
        dv_ref[...] = jnp.concatenate([dv_acc[0], dv_acc[1]], axis=1).astype(BF16)

    slab = lambda off: pl.BlockSpec((None, S, LANES), lambda p: (off + p, 0, 0))
    pair = pl.BlockSpec((None, S, LANES), lambda p: (p, 0, 0))
    return pl.pallas_call(
        body, grid=(2,), in_specs=[slab(q0), slab(k0), slab(v0), pair, pair], out_specs=[pair] * 3,
        out_shape=[jax.ShapeDtypeStruct((2, S, LANES), BF16)] * 3,
        scratch_shapes=[pltpu.VMEM((S, LANES), BF16)] * 4 + [pltpu.VMEM((2, S, HD), F32)] * 2,
        compiler_params=_cp("arbitrary"), name=name)(proj, proj, proj, do, tot)


def _cat_slabs(ref):
    return jnp.concatenate([ref[s] for s in range(ref.shape[0])], axis=1)


def _merge_fwd(o_a, o_b, o_c, gates, b_gate, wa, wb, wc, w_out, name):
    tm = ROW_TILE

    def body(oa_ref, ob_ref, oc_ref, g_ref, bg_ref, wa_ref, wb_ref, wc_ref, wo_ref, mg_ref, mo_ref):
        acc = jnp.zeros((tm, D), F32)
        for i, (o_ref, w_ref) in enumerate(((oa_ref, wa_ref), (ob_ref, wb_ref), (oc_ref, wc_ref))):
            pr = _dot(_cat_slabs(o_ref).astype(BF16), w_ref[...], 1, 0)
            sg = jax.nn.sigmoid(g_ref[:, i * D:(i + 1) * D] + bg_ref[i:i + 1, :])
            acc = acc + sg * pr
        mg = acc.astype(BF16)
        mg_ref[...] = mg
        mo_ref[...] = _dot(mg, wo_ref[...], 1, 0)

    slabs = lambda n: pl.BlockSpec((n, tm, LANES), lambda i: (0, i, 0))
    full = lambda r, c: pl.BlockSpec((r, c), lambda i: (0, 0))
    row = pl.BlockSpec((tm, D), lambda i: (i, 0))
    return pl.pallas_call(
        body, grid=(S // tm,),
        in_specs=[slabs(2), slabs(4), slabs(2), pl.BlockSpec((tm, GATE_COLS), lambda i: (i, 0)), full(3, D),
                  full(256, D), full(512, D), full(256, D), full(D, D)],
        out_specs=[row, row],
        out_shape=[jax.ShapeDtypeStruct((S, D), BF16), jax.ShapeDtypeStruct((S, D), F32)],
        compiler_params=_cp("parallel"), name=name)(o_a, o_b, o_c, gates, b_gate, wa, wb, wc, w_out)


def _merge_bwd(d_mo, o_a, o_b, o_c, gates, b_gate, wa, wb, wc, w_out, name):
    tm = ROW_TILE

    def body(dmo_ref, oa_ref, ob_ref, oc_ref, g_ref, bg_ref, wa_ref, wb_ref, wc_ref, wo_ref,
             doa_ref, dob_ref, doc_ref, dg_ref, dwa_ref, dwb_ref, dwc_ref, dbg_ref):
        @pl.when(pl.program_id(0) == 0)
        def _():
            dwa_ref[...] = jnp.zeros(dwa_ref.shape, F32)
            dwb_ref[...] = jnp.zeros(dwb_ref.shape, F32)
            dwc_ref[...] = jnp.zeros(dwc_ref.shape, F32)
            dbg_ref[...] = jnp.zeros(dbg_ref.shape, F32)

        dmg = _dot(dmo_ref[...], wo_ref[...], 1, 1)
        trip = ((oa_ref, wa_ref, doa_ref, dwa_ref), (ob_ref, wb_ref, dob_ref, dwb_ref), (oc_ref, wc_ref, doc_ref, dwc_ref))
        for i, (o_ref, w_ref, do_ref, dw_ref) in enumerate(trip):
            ob = _cat_slabs(o_ref).astype(BF16)
            pr = _dot(ob, w_ref[...], 1, 0)
            sg = jax.nn.sigmoid(g_ref[:, i * D:(i + 1) * D] + bg_ref[i:i + 1, :])
            dgate = dmg * pr * sg * (1.0 - sg)
            dg_ref[:, i * D:(i + 1) * D] = dgate.astype(BF16)
            dbg_ref[i:i + 1, :] += jnp.sum(dgate, axis=0, keepdims=True)
            dpr = (dmg * sg).astype(BF16)
            do = _dot(dpr, w_ref[...], 1, 1)
            for s in range(do_ref.shape[0]):
                do_ref[s] = do[:, s * LANES:(s + 1) * LANES]
            dw_ref[...] += _dot(ob, dpr, 0, 0)

    slabs = lambda n: pl.BlockSpec((n, tm, LANES), lambda i: (0, i, 0))
    full = lambda r, c: pl.BlockSpec((r, c), lambda i: (0, 0))
    row = pl.BlockSpec((tm, D), lambda i: (i, 0))
    return pl.pallas_call(
        body, grid=(S // tm,),
        in_specs=[row, slabs(2), slabs(4), slabs(2), pl.BlockSpec((tm, GATE_COLS), lambda i: (i, 0)), full(3, D),
                  full(256, D), full(512, D), full(256, D), full(D, D)],
        out_specs=[slabs(2), slabs(4), slabs(2), pl.BlockSpec((tm, GATE_COLS), lambda i: (i, 0)),
                   full(256, D), full(512, D), full(256, D), full(3, D)],
        out_shape=[jax.ShapeDtypeStruct((2, S, LANES), F32), jax.ShapeDtypeStruct((4, S, LANES), F32),
                   jax.ShapeDtypeStruct((2, S, LANES), F32), jax.ShapeDtypeStruct((S, GATE_COLS), BF16),
                   jax.ShapeDtypeStruct((256, D), F32), jax.ShapeDtypeStruct((512, D), F32),
                   jax.ShapeDtypeStruct((256, D), F32), jax.ShapeDtypeStruct((3, D), F32)],
        compiler_params=_cp("arbitrary"), name=name)(d_mo, o_a, o_b, o_c, gates, b_gate, wa, wb, wc, w_out)


FC = 256
GELU_K = math.sqrt(2.0 / math.pi)
GELU_C = 0.044715


def _shift_down(x, n):
    row = lax.broadcasted_iota(jnp.int32, x.shape, 0)
    return jnp.where(row < n, 0.0, pltpu.roll(x, n, 0))


def _shift_up(x, n):
    row = lax.broadcasted_iota(jnp.int32, x.shape, 0)
    return jnp.where(row >= x.shape[0] - n, 0.0, pltpu.roll(x, x.shape[0] - n, 0))


def _conv(u, w_ref, half, b):
    return (w_ref[0:1, half, :] * _shift_down(u, 2) + w_ref[1:2, half, :] * _shift_down(u, 1)
            + w_ref[2:3, half, :] * u + b)


def _ffn_act(u, conv_w, conv_b, name):
    def body(u_ref, w_ref, b_ref, a_ref):
        yg = _conv(u_ref[0], w_ref, 0, b_ref[0:1, :])
        yv = _conv(u_ref[1], w_ref, 1, b_ref[1:2, :])
        th = jnp.tanh(GELU_K * (yg + GELU_C * yg * yg * yg))
        a_ref[...] = (0.5 * yg * (1.0 + th) * yv).astype(BF16)

    return pl.pallas_call(
        body, grid=(D_FF // FC,),
        in_specs=[pl.BlockSpec((2, S, FC), lambda j: (0, 0, j)), pl.BlockSpec((3, 2, FC), lambda j: (0, 0, j)),
                  pl.BlockSpec((2, FC), lambda j: (0, j))],
        out_specs=pl.BlockSpec((S, FC), lambda j: (0, j)),
        out_shape=jax.ShapeDtypeStruct((S, D_FF), BF16),
        compiler_params=_cp("parallel"), name=name)(u, conv_w, conv_b)


def _ffn_act_bwd(u, d_a, conv_w, conv_b, name):
    def body(u_ref, da_ref, w_ref, b_ref, du_ref, dw_ref, db_ref):
        ug, uv = u_ref[0], u_ref[1]
        yg = _conv(ug, w_ref, 0, b_ref[0:1, :])
        yv = _conv(uv, w_ref, 1, b_ref[1:2, :])
        inner = GELU_K * (yg + GELU_C * yg * yg * yg)
        th = jnp.tanh(inner)
        gelu = 0.5 * yg * (1.0 + th)
        dgelu = 0.5 * (1.0 + th) + 0.5 * yg * (1.0 - th * th) * GELU_K * (1.0 + 3.0 * GELU_C * yg * yg)
        da = da_ref[...]
        for half, (uu, dy) in enumerate(((ug, da * yv * dgelu), (uv, da * gelu))):
            du = (w_ref[2:3, half, :] * dy + w_ref[1:2, half, :] * _shift_up(dy, 1)
                  + w_ref[0:1, half, :] * _shift_up(dy, 2))
            du_ref[half] = du.astype(BF16)
            dw_ref[0:1, half, :] = jnp.sum(dy * _shift_down(uu, 2), axis=0, keepdims=True)
            dw_ref[1:2, half, :] = jnp.sum(dy * _shift_down(uu, 1), axis=0, keepdims=True)
            dw_ref[2:3, half, :] = jnp.sum(dy * uu, axis=0, keepdims=True)
            db_ref[half:half + 1, :] = jnp.sum(dy, axis=0, keepdims=True)

    return pl.pallas_call(
        body, grid=(D_FF // FC,),
        in_specs=[pl.BlockSpec((2, S, FC), lambda j: (0, 0, j)), pl.BlockSpec((S, FC), lambda j: (0, j)),
                  pl.BlockSpec((3, 2, FC), lambda j: (0, 0, j)), pl.BlockSpec((2, FC), lambda j: (0, j))],
        out_specs=[pl.BlockSpec((2, S, FC), lambda j: (0, 0, j)), pl.BlockSpec((3, 2, FC), lambda j: (0, 0, j)),
                   pl.BlockSpec((2, FC), lambda j: (0, j))],
        out_shape=[jax.ShapeDtypeStruct((2, S, D_FF), BF16), jax.ShapeDtypeStruct((3, 2, D_FF), F32),
                   jax.ShapeDtypeStruct((2, D_FF), F32)],
        compiler_params=_cp("parallel"), name=name)(u, d_a, conv_w, conv_b)


def _layer_fwd(x, h1, w, bias, lname):
    n = lambda s: f"{lname}_{s}"
    w.need("in", h1)
    tn = 768
    proj = _mm(h1, w["w_in"], grid=(S // 1024, QKV_COLS // tn, 1),
               a_spec=pl.BlockSpec((1024, D), lambda i, j, k: (i, 0)),
               b_spec=pl.BlockSpec((D, tn), lambda i, j, k: (0, j)),
               out_shape=jax.ShapeDtypeStruct((QKV_SLABS, S, LANES), F32),
               out_spec=pl.BlockSpec((tn // LANES, 1024, LANES), lambda i, j, k: (j, i, 0)),
               ca=1, cb=0, acc_shape=(1024, tn), out_slab=True, name=n("proj_qkv"))
    gates = _mm(h1, w["w_in"], grid=(S // 1024, GATE_COLS // tn, 1),
                a_spec=pl.BlockSpec((1024, D), lambda i, j, k: (i, 0)),
                b_spec=pl.BlockSpec((D, tn), lambda i, j, k: (0, j + QKV_COLS // tn)),
                out_shape=jax.ShapeDtypeStruct((S, GATE_COLS), F32),
                out_spec=pl.BlockSpec((1024, tn), lambda i, j, k: (i, j)),
                ca=1, cb=0, acc_shape=(1024, tn), name=n("proj_gate"))
    nums, stats = [], []
    for g, (_, d) in enumerate(A_GROUPS):
        nm, st = _band_fwd(proj, bias, d=d, q0=2 * g, k0=6 + 2 * g, v0=12 + 2 * g, npairs=2, bias0=2 * g,
                           shared_kv=False, name=n(f"attn_a{g}_fwd"))
        nums.append(nm)
        stats.append(st)
    o_a, lse_a = _combine_a(nums, stats, n("attn_a_combine"))
    nm_b, st_b = _band_fwd(proj, bias, d=1, q0=18, k0=22, v0=23, npairs=4, bias0=6, shared_kv=True, name=n("attn_b_fwd"))
    o_b, lse_b = _combine_b(nm_b, st_b, w["sinks"], n("attn_b_combine"))
    o_c, tot_c = _stick_fwd(proj, q0=24, k0=26, v0=28, name=n("attn_c_fwd"))
    w.need("mix", tot_c)
    merged, mo = _merge_fwd(o_a, o_b, o_c, gates, w["b_gate"], w["w_br_a"], w["w_br_b"], w["w_br_c"], w["w_out"], n("merge_fwd"))
    x2, h2 = _postnorm_res(x, mo, w["attn_post_norm"], w["ffn_pre_norm"], n("attn_post"))
    w.need("ffn", h2)
    u = _mm(h2, w["w_up"], grid=(S // 1024, 2 * D_FF // 1024, 1),
            a_spec=pl.BlockSpec((1024, D), lambda i, j, k: (i, 0)),
            b_spec=pl.BlockSpec((D, 1024), lambda i, j, k: (0, j)),
            out_shape=jax.ShapeDtypeStruct((2, S, D_FF), F32),
            out_spec=pl.BlockSpec((None, 1024, 1024), lambda i, j, k: (j // 4, i, j % 4)),
            ca=1, cb=0, acc_shape=(1024, 1024), name=n("ffn_up"))
    a = _ffn_act(u, w["conv_w"], w["conv_b"], n("ffn_act"))
    fo = _mm_nn(a, w["w_down"], F32, 1024, 1024, 1024, n("ffn_down"))
    saved = dict(x=x, h1=h1, proj=proj, gates=gates, o_a=o_a, lse_a=lse_a, o_b=o_b, lse_b=lse_b, o_c=o_c, tot_c=tot_c,
                 merged=merged, mo=mo, x2=x2, h2=h2, u=u, a=a, fo=fo)
    return saved


def _layer_bwd(dx3, sv, w, bias, lname, tok=None, on_part=None):
    n = lambda s: f"{lname}_{s}"
    g = {}

    def part(group, vec):
        t = on_part(group, g) if on_part is not None else None
        return vec if t is None else vec + t

    gain = w["ffn_post_norm"] if tok is None else w["ffn_post_norm"] + tok
    d_fo, g["ffn_post_norm"] = _norm_bwd(sv["fo"], gain, [dx3], None, BF16, n("ffn_post_bwd"))
    d_a = _mm_nt(d_fo, w["w_down"], F32, 1024, 1024, 1024, n("ffn_down_bwd_x"))
    g["w_down"] = _mm_tn(sv["a"], d_fo, BF16, 1024, 1024, 1024, n("ffn_down_bwd_w"))
    d_u, dcw, dcb = _ffn_act_bwd(sv["u"], d_a, w["conv_w"], w["conv_b"], n("ffn_act_bwd"))
    g["conv_w"] = dcw.reshape(3, 2 * D_FF)
    g["conv_b"] = dcb.reshape(1, 2 * D_FF)
    g["w_up"] = _mm(sv["h2"], d_u, grid=(1, 2 * D_FF // 1024, S // 1024),
                    a_spec=pl.BlockSpec((1024, D), lambda i, j, k: (k, 0)),
                    b_spec=pl.BlockSpec((None, 1024, 1024), lambda i, j, k: (j // 4, k, j % 4)),
                    out_shape=jax.ShapeDtypeStruct((D, 2 * D_FF), BF16),
                    out_spec=pl.BlockSpec((D, 1024), lambda i, j, k: (0, j)),
                    ca=0, cb=0, acc_shape=(D, 1024), name=n("ffn_up_bwd_w"))
    d_h2 = _mm(d_u, w["w_up"], grid=(S // 1024, 1, 2 * D_FF // 1024),
               a_spec=pl.BlockSpec((None, 1024, 1024), lambda i, j, k: (k // 4, i, k % 4)),
               b_spec=pl.BlockSpec((D, 1024), lambda i, j, k: (0, k)),
               out_shape=jax.ShapeDtypeStruct((S, D), F32),
               out_spec=pl.BlockSpec((1024, D), lambda i, j, k: (i, 0)),
               ca=1, cb=1, acc_shape=(1024, D), name=n("ffn_up_bwd_x"))
    dx2, g["ffn_pre_norm"] = _norm_bwd(sv["x2"], part("ffn", w["ffn_pre_norm"]), [d_h2], dx3, F32, n("ffn_pre_bwd"))
    d_mo, g["attn_post_norm"] = _norm_bwd(sv["mo"], w["attn_post_norm"], [dx2], None, BF16, n("attn_post_bwd"))
    g["w_out"] = _mm_tn(sv["merged"], d_mo, BF16, 1024, 1024, 1024, n("out_bwd_w"))
    do_a, do_b, do_c, d_gates, dwa, dwb, dwc, g["b_gate"] = _merge_bwd(
        d_mo, sv["o_a"], sv["o_b"], sv["o_c"], sv["gates"], w["b_gate"], w["w_br_a"], w["w_br_b"], w["w_br_c"],
        w["w_out"], n("merge_bwd"))
    g["w_br_a"], g["w_br_b"], g["w_br_c"] = dwa, dwb, dwc
    sinks = part("mix", w["sinks"])
    proj = sv["proj"]
    dqa, dka, dva, gbias = [], [], [], []
    for gi, (_, d) in enumerate(A_GROUPS):
        dq, dk, dv, gg, _ = _band_bwd(proj, bias, sv["o_a"], do_a, sv["lse_a"], sinks, d=d, q0=2 * gi, k0=6 + 2 * gi,
                                      v0=12 + 2 * gi, npairs=2, bias0=2 * gi, shared_kv=False, name=n(f"attn_a{gi}_bwd"))
        dqa.append(dq), dka.append(dk), dva.append(dv), gbias.append(gg)
    dqb, dkb, dvb, ggb, dsink = _band_bwd(proj, bias, sv["o_b"], do_b, sv["lse_b"], sinks, d=1, q0=18, k0=22, v0=23,
                                          npairs=4, bias0=6, shared_kv=True, name=n("attn_b_bwd"))
    gbias.append(ggb)
    g["bias_g"] = jnp.concatenate(gbias, axis=0).reshape(N_BIAS_HEADS, BLK, 2 * BLK)
    g["sinks"] = dsink[:, 0, :2].reshape(1, 8)
    dqc, dkc, dvc = _stick_bwd(proj, do_c, sv["tot_c"], q0=24, k0=26, v0=28, name=n("attn_c_bwd"))
    dqkv = jnp.concatenate(dqa + dka + dva + [dqb, dkb, dvb, dqc, dkc, dvc], axis=0)
    ts = 6
    d_h1a = _mm(dqkv, w["w_in"], grid=(S // 1024, 1, QKV_SLABS // ts),
                a_spec=pl.BlockSpec((ts, 1024, LANES), lambda i, j, k: (k, i, 0)),
                b_spec=pl.BlockSpec((D, ts * LANES), lambda i, j, k: (0, k)),
                out_shape=jax.ShapeDtypeStruct((S, D), F32),
                out_spec=pl.BlockSpec((1024, D), lambda i, j, k: (i, 0)),
                ca=1, cb=1, acc_shape=(1024, D), a_slab=True, name=n("in_bwd_x_qkv"))
    d_h1b = _mm(d_gates, w["w_in"], grid=(S // 1024, 1, GATE_COLS // 768),
                a_spec=pl.BlockSpec((1024, 768), lambda i, j, k: (i, k)),
                b_spec=pl.BlockSpec((D, 768), lambda i, j, k: (0, k + QKV_COLS // 768)),
                out_shape=jax.ShapeDtypeStruct((S, D), F32),
                out_spec=pl.BlockSpec((1024, D), lambda i, j, k: (i, 0)),
                ca=1, cb=1, acc_shape=(1024, D), name=n("in_bwd_x_gate"))
    dw_in = _mm(sv["h1"], dqkv, grid=(1, QKV_SLABS // ts, S // 1024),
                a_spec=pl.BlockSpec((1024, D), lambda i, j, k: (k, 0)),
                b_spec=pl.BlockSpec((ts, 1024, LANES), lambda i, j, k: (j, k, 0)),
                out_shape=jax.ShapeDtypeStruct((D, IN_COLS), BF16),
                out_spec=pl.BlockSpec((D, ts * LANES), lambda i, j, k: (0, j)),
                ca=0, cb=0, acc_shape=(D, ts * LANES), b_slab=True, name=n("in_bwd_w_qkv"))
    g["w_in"] = _mm(sv["h1"], d_gates, grid=(1, GATE_COLS // 768, S // 1024),
                    a_spec=pl.BlockSpec((1024, D), lambda i, j, k: (k, 0)),
                    b_spec=pl.BlockSpec((1024, 768), lambda i, j, k: (k, j)),
                    out_shape=jax.ShapeDtypeStruct((D, IN_COLS), BF16),
                    out_spec=pl.BlockSpec((D, 768), lambda i, j, k: (0, j + QKV_COLS // 768)),
                    ca=0, cb=0, acc_shape=(D, 768), alias_out=dw_in, name=n("in_bwd_w_gate"))
    dx, g["attn_pre_norm"] = _norm_bwd(sv["x"], w["attn_pre_norm"], [d_h1a, d_h1b], dx2, F32, n("attn_pre_bwd"))
    tok_in = on_part("in", g) if on_part is not None else None
    return dx, g, tok_in


def _local_step(x, target, ws, rel_bias, tok=None, on_grads=None):
    buckets = jnp.asarray(_bucket_tiles())
    bias = _bias_tiles(rel_bias, buckets, "bias_tiles").reshape(N_BIAS_HEADS // 2, 2, 2, BLK, 2 * BLK)
    saved = []
    gain0 = ws[0]["attn_pre_norm"] if tok is None else ws[0]["attn_pre_norm"] + tok
    h1 = _prenorm(x, gain0, "l0_attn_pre")
    for l in range(DEPTH):
        sv = _layer_fwd(x, h1, ws[l], bias, f"l{l}")
        saved.append(sv)
        g_next = ws[l + 1]["attn_pre_norm"] if l + 1 < DEPTH else ws[l]["attn_pre_norm"]
        x, h1 = _postnorm_res(sv["x2"], sv["fo"], ws[l]["ffn_post_norm"], g_next, f"l{l}_ffn_post")
    dy, loss_tile = _loss_head(x, target, "loss_head")
    grads = [None] * DEPTH
    tok = None
    for l in reversed(range(DEPTH)):
        on_part = None if on_grads is None else functools.partial(on_grads, l)
        dy, grads[l], tok = _layer_bwd(dy, saved[l], ws[l], bias, f"l{l}", tok, on_part)
    g_rel = _bias_grad([grads[l]["bias_g"] for l in range(DEPTH)], buckets, "bias_grad")[:, :N_BIAS_HEADS]
    return loss_tile[0, 0], dy, grads, g_rel


def _coords():
    return lax.axis_index("x"), lax.axis_index("y"), lax.axis_index("c")


def _peer(rel):
    x, y, c = _coords()
    return (1 - x if rel & 4 else x, 1 - y if rel & 2 else y, 1 - c if rel & 1 else c)


def _exchange(srcs, dst_shapes, src_win, dst_win, name):
    nt = len(srcs)

    def body(*refs):
        src_refs, dst_refs = refs[:nt], refs[nt:2 * nt]
        send_sems, recv_sems, local_sems = refs[2 * nt:]
        x, y, c = _coords()
        me = 4 * x + 2 * y + c
        locals_ = []
        for t in range(nt):
            cp = pltpu.make_async_copy(src_win(t, src_refs[t], me), dst_win(t, dst_refs[t], me), local_sems.at[t])
            cp.start()
            locals_.append(cp)
        sends = []
        for rel in range(1, NDEV):
            px, py, pc = _peer(rel)
            q = 4 * px + 2 * py + pc
            for t in range(nt):
                cp = pltpu.make_async_remote_copy(
                    src_ref=src_win(t, src_refs[t], q), dst_ref=dst_win(t, dst_refs[t], me),
                    send_sem=send_sems.at[rel - 1, t], recv_sem=recv_sems.at[rel - 1, t],
                    device_id=(px, py, pc), device_id_type=MESH)
                cp.start()
                sends.append(cp)
        for rel in range(1, NDEV):
            px, py, pc = _peer(rel)
            q = 4 * px + 2 * py + pc
            for t in range(nt):
                pltpu.make_async_remote_copy(
                    src_ref=src_win(t, src_refs[t], me), dst_ref=dst_win(t, dst_refs[t], q),
                    send_sem=send_sems.at[rel - 1, t], recv_sem=recv_sems.at[rel - 1, t],
                    device_id=(px, py, pc), device_id_type=MESH).wait_recv()
        for cp in sends:
            cp.wait_send()
        for cp in locals_:
            cp.wait()

    return pl.pallas_call(
        body, in_specs=[ANY] * nt, out_specs=[ANY] * nt, out_shape=dst_shapes,
        scratch_shapes=[pltpu.SemaphoreType.DMA((NDEV - 1, nt)), pltpu.SemaphoreType.DMA((NDEV - 1, nt)),
                        pltpu.SemaphoreType.DMA((nt,))],
        name=name)(*srcs)


BIG = (("w_in", 1, 864), ("w_br_a", 1, 128), ("w_br_b", 1, 128), ("w_br_c", 1, 128), ("w_out", 0, 128),
       ("w_up", 1, 1024), ("w_down", 0, 512))


NBIG = len(BIG)
BIG_FULL = {"w_in": (D, IN_COLS), "w_br_a": (256, D), "w_br_b": (512, D), "w_br_c": (256, D), "w_out": (D, D),
            "w_up": (D, 2 * D_FF), "w_down": (D_FF, D)}
LAYER_GROUPS = (("in", (0,)), ("mix", (1, 2, 3, 4)), ("ffn", (5, 6)))

HBM_SPEC = pl.BlockSpec(memory_space=pltpu.HBM)
SEM_SPEC = pl.BlockSpec(memory_space=pltpu.SEMAPHORE)


def _hbm(a):
    return pltpu.with_memory_space_constraint(a, pltpu.HBM)


def _shard_window(t, ref, k):
    nm, ax, ext = BIG[t % NBIG]
    if nm == "w_in":
        return ref.at[k]
    off = pl.multiple_of(k * ext, ext)
    if ax == 0:
        return ref.at[pl.ds(off, ext), :]
    return ref.at[:, pl.ds(off, ext)]


def _whole(t, ref, k):
    return ref


def _slot(t, ref, k):
    return ref.at[k]


def _own_block_spec(t, rows, me_of):
    nm, ax, ext = BIG[t % NBIG]
    r, c = BIG_FULL[nm]
    if nm == "w_in":
        return pl.BlockSpec((None, rows, ext), lambda i, m: (me_of(m), i, 0))
    if ax == 0:
        return pl.BlockSpec((rows, c), lambda i, m: (me_of(m) * (ext // rows) + i, 0))
    return pl.BlockSpec((rows, ext), lambda i, m: (i, me_of(m)))


def _cast_own(t, shard, me_arr, name):
    nm, ax, ext = BIG[t % NBIG]
    nr, nc = shard.shape
    rows = min(nr, 256)
    shape = (NDEV, D, ext) if nm == "w_in" else BIG_FULL[nm]

    def body(m_ref, s_ref, o_ref):
        o_ref[...] = s_ref[...].astype(BF16)

    return pl.pallas_call(
        body, grid_spec=pltpu.PrefetchScalarGridSpec(
            num_scalar_prefetch=1, grid=(nr // rows,),
            in_specs=[pl.BlockSpec((rows, nc), lambda i, m: (i, 0))],
            out_specs=_own_block_spec(t, rows, lambda m: m[0])),
        out_shape=jax.ShapeDtypeStruct(shape, BF16), compiler_params=_cp("arbitrary"), name=name)(me_arr, shard)


def _xchg_start(srcs, lands, groups, src_win, dst_win, after, name):
    ns = 0 if srcs is None else len(srcs)
    nt, ng = len(lands), len(groups)
    ins = ([] if srcs is None else list(srcs)) + list(lands)

    def body(*refs):
        src_refs, land_refs = refs[:ns], refs[ns:ns + nt]
        sems = refs[ns + nt + 1:ns + nt + 1 + 2 * ng]
        token = refs[-1]
        x, y, c = _coords()
        me = 4 * x + 2 * y + c
        for gi, grp in enumerate(groups):
            for j, t in enumerate(grp):
                for rel in range(1, NDEV):
                    px, py, pc = _peer(rel)
                    q = 4 * px + 2 * py + pc
                    src = dst_win(t, land_refs[t], me) if srcs is None else src_win(t, src_refs[t], q)
                    pltpu.make_async_remote_copy(
                        src_ref=src, dst_ref=dst_win(t, land_refs[t], me),
                        send_sem=sems[2 * gi].at[(rel - 1) * len(grp) + j],
                        recv_sem=sems[2 * gi + 1].at[(rel - 1) * len(grp) + j],
                        device_id=(px, py, pc), device_id_type=MESH).start()
        token[...] = jnp.zeros((8, LANES), F32)

    out_shape = []
    for grp in groups:
        out_shape += [pltpu.SemaphoreType.DMA(((NDEV - 1) * len(grp),))] * 2
    out_shape += [pltpu.HBM(a.shape, a.dtype) for a in ins]
    out_shape.append(jax.ShapeDtypeStruct((8, LANES), F32))
    outs = pl.pallas_call(
        body, in_specs=[HBM_SPEC] * len(ins) + [ANY],
        out_specs=[SEM_SPEC] * (2 * ng) + [HBM_SPEC] * len(ins) + [pl.BlockSpec(memory_space=pltpu.VMEM)],
        out_shape=out_shape, input_output_aliases={i: 2 * ng + i for i in range(len(ins))},
        compiler_params=pltpu.CompilerParams(has_side_effects=pltpu.SideEffectType.DATAFLOW_SIDE_EFFECTING),
        name=name)(*[_hbm(a) for a in ins], after)
    sems = [(outs[2 * gi], outs[2 * gi + 1]) for gi in range(ng)]
    thru = list(outs[2 * ng:2 * ng + len(ins)])
    return sems, (None if srcs is None else thru[:ns]), thru[ns:], outs[-1]


def _xchg_wait(sems, srcs, lands, tids, after, src_win, dst_win, name):
    ns = 0 if srcs is None else len(srcs)
    n = len(lands)
    send_sem, recv_sem = sems
    ins = ([] if srcs is None else list(srcs)) + list(lands)

    def body(*refs):
        src_refs, land_refs = refs[:ns], refs[ns:ns + n]
        ssem, rsem = refs[ns + n], refs[ns + n + 1]
        x, y, c = _coords()
        me = 4 * x + 2 * y + c
        for j, t in enumerate(tids):
            for rel in range(1, NDEV):
                px, py, pc = _peer(rel)
                q = 4 * px + 2 * py + pc
                src = dst_win(t, land_refs[j], me) if srcs is None else src_win(t, src_refs[j], q)
                cp = pltpu.make_async_remote_copy(
                    src_ref=src, dst_ref=dst_win(t, land_refs[j], q),
                    send_sem=ssem.at[(rel - 1) * n + j], recv_sem=rsem.at[(rel - 1) * n + j],
                    device_id=(px, py, pc), device_id_type=MESH)
                cp.wait_send()
                cp.wait_recv()

    outs = pl.pallas_call(
        body, in_specs=[HBM_SPEC] * len(ins) + [SEM_SPEC, SEM_SPEC, ANY], out_specs=[HBM_SPEC] * len(ins),
        out_shape=[pltpu.HBM(a.shape, a.dtype) for a in ins],
        input_output_aliases={i: i for i in range(len(ins))},
        compiler_params=pltpu.CompilerParams(has_side_effects=pltpu.SideEffectType.DATAFLOW_SIDE_EFFECTING),
        name=name)(*ins, send_sem, recv_sem, after)
    return (None if srcs is None else list(outs[:ns])), list(outs[ns:])


class _Weights:
    def __init__(self, ready, pending=None):
        self.ready = dict(ready)
        self.pending = dict(pending or {})

    def __getitem__(self, k):
        return self.ready[k]

    def need(self, group, after):
        fn = self.pending.pop(group, None)
        if fn is not None:
            self.ready.update(fn(after))


def _adamw_math(w, g, m, v):
    m2 = ADAM_B1 * m + (1.0 - ADAM_B1) * g
    v2 = ADAM_B2 * v + (1.0 - ADAM_B2) * (g * g)
    m_hat = m2 / (1.0 - ADAM_B1 ** ADAM_STEP)
    v_hat = v2 / (1.0 - ADAM_B2 ** ADAM_STEP)
    delta = -ADAM_LR * (m_hat / (jnp.sqrt(v_hat) + ADAM_EPS) + ADAM_WD * w)
    return delta, m2, v2


def _adamw(t, parts, own, me_arr, w, m, v, layer, prev, rows, name):
    nl, nr, nc = w.shape

    def body(me_ref, p_ref, own_ref, w_ref, m_ref, v_ref, *rest):
        g_ref, d_ref, m2_ref, v2_ref = rest[-4:]
        me = me_ref[0]
        g = None
        for k in range(NDEV):
            term = jnp.where(me == k, own_ref[...], p_ref[k]).astype(F32)
            g = term if g is None else g + term
        delta, m2, v2 = _adamw_math(w_ref[...], g, m_ref[...], v_ref[...])
        g_ref[...] = g
        d_ref[...] = delta
        m2_ref[...] = m2
        v2_ref[...] = v2

    blk = pl.BlockSpec((None, rows, nc), lambda i, mm: (layer, i, 0))
    pblk = pl.BlockSpec((NDEV, rows, nc), lambda i, mm: (0, i, 0))
    extra = [] if prev is None else list(prev)
    return pl.pallas_call(
        body, grid_spec=pltpu.PrefetchScalarGridSpec(
            num_scalar_prefetch=1, grid=(nr // rows,),
            in_specs=[pblk, _own_block_spec(t, rows, lambda mm: mm[0]), blk, blk, blk] + [ANY] * len(extra),
            out_specs=[blk] * 4),
        out_shape=[jax.ShapeDtypeStruct(w.shape, F32)] * 4,
        input_output_aliases={6 + k: k for k in range(len(extra))},
        compiler_params=_cp("arbitrary"), name=name)(me_arr, parts, own, w, m, v, *extra)


SMALL_REPL = (("rel_bias", NUM_BUCKETS * N_BIAS_HEADS), ("attn_pre_norm", DEPTH * D), ("sinks", DEPTH * 8),
              ("attn_post_norm", DEPTH * D), ("ffn_pre_norm", DEPTH * D), ("conv_b", DEPTH * 2 * D_FF),
              ("ffn_post_norm", DEPTH * D))
SMALL_SHARD = (("b_gate", (DEPTH, 3, D), 128), ("conv_w", (DEPTH, 3, 2 * D_FF), 1024))


def _pack(vecs):
    flat = jnp.concatenate([v.reshape(-1).astype(F32) for v in vecs])
    n = flat.shape[0]
    rows = -(-n // (8 * LANES)) * 8
    return jnp.pad(flat, (0, rows * LANES - n)).reshape(rows, LANES)


def _unpack(packed, sizes):
    flat = packed.reshape(-1)
    out, off = [], 0
    for sz in sizes:
        out.append(flat[off:off + sz])
        off += sz
    return out


def _small_sum(parts, name):
    r = parts.shape[1]

    def body(p_ref, o_ref):
        g = p_ref[0]
        for k in range(1, NDEV):
            g = g + p_ref[k]
        o_ref[...] = g

    return pl.pallas_call(
        body, in_specs=[pl.BlockSpec(memory_space=pltpu.VMEM)], out_specs=pl.BlockSpec(memory_space=pltpu.VMEM),
        out_shape=jax.ShapeDtypeStruct((r, LANES), F32), name=name)(parts)


def _small_adamw(g, w, m, v, name):
    def body(g_ref, w_ref, m_ref, v_ref, d_ref, m2_ref, v2_ref):
        delta, m2, v2 = _adamw_math(w_ref[...], g_ref[...], m_ref[...], v_ref[...])
        d_ref[...] = delta
        m2_ref[...] = m2
        v2_ref[...] = v2

    vm = pl.BlockSpec(memory_space=pltpu.VMEM)
    return pl.pallas_call(
        body, in_specs=[vm] * 4, out_specs=[vm] * 3,
        out_shape=[jax.ShapeDtypeStruct(g.shape, F32)] * 3, name=name)(g, w, m, v)


def kernel(x, rel_bias, attn_pre_norm, w_in, b_gate, sinks, w_br_a, w_br_b, w_br_c, w_out, attn_post_norm, ffn_pre_norm, w_up, conv_w, conv_b, w_down, ffn_post_norm, loss_target, m_rel_bias, m_attn_pre_norm, m_w_in, m_b_gate, m_sinks, m_w_br_a, m_w_br_b, m_w_br_c, m_w_out, m_attn_post_norm, m_ffn_pre_norm, m_w_up, m_conv_w, m_conv_b, m_w_down, m_ffn_post_norm, v_rel_bias, v_attn_pre_norm, v_w_in, v_b_gate, v_sinks, v_w_br_a, v_w_br_b, v_w_br_c, v_w_out, v_attn_post_norm, v_ffn_pre_norm, v_w_up, v_conv_w, v_conv_b, v_w_down, v_ffn_post_norm):
    P = dict(rel_bias=rel_bias, attn_pre_norm=attn_pre_norm, w_in=w_in, b_gate=b_gate, sinks=sinks, w_br_a=w_br_a,
             w_br_b=w_br_b, w_br_c=w_br_c, w_out=w_out, attn_post_norm=attn_post_norm, ffn_pre_norm=ffn_pre_norm,
             w_up=w_up, conv_w=conv_w, conv_b=conv_b, w_down=w_down, ffn_post_norm=ffn_post_norm)
    M = dict(rel_bias=m_rel_bias, attn_pre_norm=m_attn_pre_norm, w_in=m_w_in, b_gate=m_b_gate, sinks=m_sinks,
             w_br_a=m_w_br_a, w_br_b=m_w_br_b, w_br_c=m_w_br_c, w_out=m_w_out, attn_post_norm=m_attn_post_norm,
             ffn_pre_norm=m_ffn_pre_norm, w_up=m_w_up, conv_w=m_conv_w, conv_b=m_conv_b, w_down=m_w_down,
             ffn_post_norm=m_ffn_post_norm)
    V = dict(rel_bias=v_rel_bias, attn_pre_norm=v_attn_pre_norm, w_in=v_w_in, b_gate=v_b_gate, sinks=v_sinks,
             w_br_a=v_w_br_a, w_br_b=v_w_br_b, w_br_c=v_w_br_c, w_out=v_w_out, attn_post_norm=v_attn_post_norm,
             ffn_pre_norm=v_ffn_pre_norm, w_up=v_w_up, conv_w=v_conv_w, conv_b=v_conv_b, w_down=v_w_down,
             ffn_post_norm=v_ffn_post_norm)
    xi, yi, ci = _coords()
    me = 4 * xi + 2 * yi + ci

    me_arr = me.astype(jnp.int32).reshape(1)

    small_w = _pack([b_gate.reshape(-1), conv_w.reshape(-1)])
    (small_w_all,) = _exchange([small_w], [jax.ShapeDtypeStruct((NDEV,) + small_w.shape, F32)],
                               _whole, _slot, "gather_small_weights")

    lands = [_cast_own(l * NBIG + t, P[nm][l], me_arr, f"gather_own_l{l}_{nm}")
             for l in range(DEPTH) for t, (nm, _, _) in enumerate(BIG)]
    groups = [tuple(l * NBIG + t for t in tids) for l in range(DEPTH) for _, tids in LAYER_GROUPS]
    g_sems, _, g_lands, g_tok = _xchg_start(None, lands, groups, None, _shard_window, small_w_all, "gather_start")
    tok0 = g_tok[0:1, 0:1]

    def gather_waiter(gi, l, gname, tids):
        def wait(after):
            ids = [l * NBIG + t for t in tids]
            _, got = _xchg_wait(g_sems[gi], None, [g_lands[i] for i in ids], ids, after,
                                None, _shard_window, f"gather_wait_l{l}_{gname}")
            out = {}
            for t, arr in zip(tids, got):
                nm = BIG[t][0]
                out[nm] = jnp.transpose(arr, (1, 0, 2)).reshape(D, IN_COLS) if nm == "w_in" else arr
            return out
        return wait

    pending = [{gname: gather_waiter(l * len(LAYER_GROUPS) + k, l, gname, tids)
                for k, (gname, tids) in enumerate(LAYER_GROUPS)} for l in range(DEPTH)]
    nbg = DEPTH * 3 * 128
    ncw = DEPTH * 3 * 1024
    flat_all = small_w_all.reshape(NDEV, -1)
    b_gate_full = jnp.transpose(flat_all[:, :nbg].reshape(NDEV, DEPTH, 3, 128), (1, 2, 0, 3)).reshape(DEPTH, 3, D)
    conv_w_full = jnp.transpose(flat_all[:, nbg:nbg + ncw].reshape(NDEV, DEPTH, 3, 1024), (1, 2, 0, 3)).reshape(DEPTH, 3, 2 * D_FF)

    ws = []
    for l in range(DEPTH):
        ws.append(_Weights(dict(
            b_gate=b_gate_full[l], conv_w=conv_w_full[l].reshape(3, 2, D_FF), conv_b=conv_b[l].reshape(2, D_FF),
            sinks=sinks[l].reshape(1, 8),
            attn_pre_norm=attn_pre_norm[l].reshape(1, D), attn_post_norm=attn_post_norm[l].reshape(1, D),
            ffn_pre_norm=ffn_pre_norm[l].reshape(1, D), ffn_post_norm=ffn_post_norm[l].reshape(1, D)), pending[l]))

    rs = {}

    group_tids = dict(LAYER_GROUPS)

    def start_scatter(l, gname, grads_l):
        tids = group_tids[gname]
        blocks, lands_rs = [], []
        for t in tids:
            nm, ax, ext = BIG[t]
            gfull = grads_l[nm].astype(BF16)
            if nm == "w_in":
                gfull = jnp.transpose(gfull.reshape(D, NDEV, ext), (1, 0, 2))
                shp = (NDEV, D, ext)
            else:
                shp = (NDEV, ext, gfull.shape[1]) if ax == 0 else (NDEV, gfull.shape[0], ext)
            blocks.append(gfull)
            lands_rs.append(lax.empty(shp, BF16))
        local = list(range(len(tids)))
        win = lambda j, ref, k: _shard_window(tids[j], ref, k)
        sems, s_thru, l_thru, tok = _xchg_start(blocks, lands_rs, [tuple(local)], win, _slot, me_arr,
                                                f"scatter_start_l{l}_{gname}")
        rs[(l, gname)] = (sems[0], s_thru, l_thru, win, local)
        return tok[0:1, 0:1]

    loss_local, grad_x, grads, g_rel = _local_step(x[0], loss_target[0], ws, rel_bias, tok0, start_scatter)
    loss = lax.psum(loss_local, ("x", "y", "c"))

    stack = lambda nm: jnp.stack([grads[l][nm] for l in range(DEPTH)], axis=0)
    small_names = [nm for nm, _ in SMALL_REPL] + [nm for nm, _, _ in SMALL_SHARD]
    small_g = {"rel_bias": g_rel}
    for nm in small_names[1:]:
        small_g[nm] = stack(nm)
    small_packed = _pack([small_g[nm] for nm in small_names])
    (small_parts,) = _exchange([small_packed], [jax.ShapeDtypeStruct((NDEV,) + small_packed.shape, F32)],
                               _whole, _slot, "gather_small_grads")
    small_tot = _small_sum(small_parts, "small_grad_sum")
    sizes = [sz for _, sz in SMALL_REPL] + [int(np.prod(shp)) for _, shp, _ in SMALL_SHARD]
    small_tot = dict(zip(small_names, _unpack(small_tot, sizes)))

    out_g, out_d, out_m, out_v = {}, {}, {}, {}
    prev = {nm: None for nm, _, _ in BIG}
    for l in reversed(range(DEPTH)):
        for gname in ("ffn", "mix", "in"):
            sems, s_thru, l_thru, win, local = rs[(l, gname)]
            owns, parts = _xchg_wait(sems, s_thru, l_thru, local, small_parts, win, _slot, f"scatter_wait_l{l}_{gname}")
            for t, own, prt in zip(group_tids[gname], owns, parts):
                nm = BIG[t][0]
                rows = {"w_in": 256, "w_up": 256, "w_down": 256}.get(nm, P[nm].shape[1])
                prev[nm] = _adamw(t, prt, own, me_arr, P[nm], M[nm], V[nm], l, prev[nm], rows, f"adamw_{nm}_l{l}")
    for nm, _, _ in BIG:
        out_g[nm], out_d[nm], out_m[nm], out_v[nm] = prev[nm]
    gsm = {}
    for nm, _ in SMALL_REPL:
        gsm[nm] = small_tot[nm].reshape(P[nm].shape)
    for nm, shp, ext in SMALL_SHARD:
        gsm[nm] = lax.dynamic_slice_in_dim(small_tot[nm].reshape(shp), me * ext, ext, axis=2)
    pk = lambda dct: _pack([dct[nm] for nm in small_names])
    d_s, m_s, v_s = _small_adamw(pk(gsm), pk(P), pk(M), pk(V), "adamw_small")
    szs = [int(np.prod(P[nm].shape)) for nm in small_names]
    for dst, packed in ((out_d, d_s), (out_m, m_s), (out_v, v_s)):
        for nm, piece in zip(small_names, _unpack(packed, szs)):
            dst[nm] = piece.reshape(P[nm].shape)
    for nm in small_names:
        out_g[nm] = gsm[nm]

    order = ["rel_bias", "attn_pre_norm", "w_in", "b_gate", "sinks", "w_br_a", "w_br_b", "w_br_c", "w_out",
             "attn_post_norm", "ffn_pre_norm", "w_up", "conv_w", "conv_b", "w_down", "ffn_post_norm"]
    return (loss, grad_x[None], *[out_g[k] for k in order], *[out_d[k] for k in order],
            *[out_m[k] for k in order], *[out_v[k] for k in order])
```

```python
import functools
import math

import numpy as np
import jax
import jax.numpy as jnp
from jax import lax
from jax.experimental import pallas as pl
from jax.experimental.pallas import tpu as pltpu

F32 = jnp.float32
BF16 = jnp.bfloat16

S = 2048
D = 1024
DEPTH = 2
NDEV = 8
HD = 64
BLK = 128
NB = S // BLK
A_GROUPS = ((128, 1), (512, 4), (2048, 16))
NUM_BUCKETS = 32
MAX_DISTANCE = 2048
N_BIAS_HEADS = 20
D_FF = 4096
IN_COLS = 6912
QKV_COLS = 3840
QKV_SLABS = QKV_COLS // 128
GATE_COLS = 3072
EPS = 1e-6
SCALE = HD ** -0.5
NEG = -1e30
LANES = 128

ADAM_LR = 0.001
ADAM_B1 = 0.9
ADAM_B2 = 0.999
ADAM_EPS = 1e-08
ADAM_WD = 0.01
ADAM_STEP = 10

VMEM_LIMIT = 56 * 1024 * 1024
MESH = pl.DeviceIdType.MESH
ANY = pl.BlockSpec(memory_space=pl.ANY)
SMEM = pl.BlockSpec(memory_space=pltpu.SMEM)


def _cp(*sem):
    return pltpu.CompilerParams(dimension_semantics=sem if sem else None, vmem_limit_bytes=VMEM_LIMIT)


def _dot(a, b, ca, cb):
    return lax.dot_general(a, b, (((ca,), (cb,)), ((), ())), preferred_element_type=F32)


def _mm(a, b, *, grid, a_spec, b_spec, out_shape, out_spec, ca, cb, acc_shape, name,
        a_slab=False, b_slab=False, out_slab=False, alias_out=None):
    nk = grid[2]

    def body(*refs):
        if alias_out is not None:
            a_ref, b_ref, _, o_ref, acc_ref = refs
        else:
            a_ref, b_ref, o_ref, acc_ref = refs
        k = pl.program_id(2)

        @pl.when(k == 0)
        def _():
            acc_ref[...] = jnp.zeros(acc_shape, F32)

        def load(ref, slab):
            if slab:
                return jnp.concatenate([ref[s] for s in range(ref.shape[0])], axis=1).astype(BF16)
            return ref[...].astype(BF16)

        acc_ref[...] += _dot(load(a_ref, a_slab), load(b_ref, b_slab), ca, cb)

        @pl.when(k == nk - 1)
        def _():
            if out_slab:
                for s in range(o_ref.shape[0]):
                    o_ref[s] = acc_ref[:, s * LANES:(s + 1) * LANES].astype(o_ref.dtype)
            else:
                o_ref[...] = acc_ref[...].astype(o_ref.dtype)

    in_specs = [a_spec, b_spec]
    args = [a, b]
    aliases = {}
    if alias_out is not None:
        in_specs.append(ANY)
        args.append(alias_out)
        aliases = {2: 0}
    return pl.pallas_call(
        body, grid=grid, in_specs=in_specs, out_specs=out_spec, out_shape=out_shape,
        scratch_shapes=[pltpu.VMEM(acc_shape, F32)], input_output_aliases=aliases,
        compiler_params=_cp("parallel", "parallel", "arbitrary"), name=name)(*args)


def _mm_nn(a, b, out_dtype, tm, tn, tk, name):
    m, kk = a.shape
    n = b.shape[1]
    return _mm(a, b, grid=(m // tm, n // tn, kk // tk),
               a_spec=pl.BlockSpec((tm, tk), lambda i, j, k: (i, k)),
               b_spec=pl.BlockSpec((tk, tn), lambda i, j, k: (k, j)),
               out_shape=jax.ShapeDtypeStruct((m, n), out_dtype),
               out_spec=pl.BlockSpec((tm, tn), lambda i, j, k: (i, j)),
               ca=1, cb=0, acc_shape=(tm, tn), name=name)


def _mm_nt(a, b, out_dtype, tm, tn, tk, name):
    m, kk = a.shape
    n = b.shape[0]
    return _mm(a, b, grid=(m // tm, n // tn, kk // tk),
               a_spec=pl.BlockSpec((tm, tk), lambda i, j, k: (i, k)),
               b_spec=pl.BlockSpec((tn, tk), lambda i, j, k: (j, k)),
               out_shape=jax.ShapeDtypeStruct((m, n), out_dtype),
               out_spec=pl.BlockSpec((tm, tn), lambda i, j, k: (i, j)),
               ca=1, cb=1, acc_shape=(tm, tn), name=name)


def _mm_tn(a, b, out_dtype, tm, tn, tk, name):
    kk, m = a.shape
    n = b.shape[1]
    return _mm(a, b, grid=(m // tm, n // tn, kk // tk),
               a_spec=pl.BlockSpec((tk, tm), lambda i, j, k: (k, i)),
               b_spec=pl.BlockSpec((tk, tn), lambda i, j, k: (k, j)),
               out_shape=jax.ShapeDtypeStruct((m, n), out_dtype),
               out_spec=pl.BlockSpec((tm, tn), lambda i, j, k: (i, j)),
               ca=0, cb=0, acc_shape=(tm, tn), name=name)


ROW_TILE = 256


def _rms(x, g):
    r = lax.rsqrt(jnp.mean(x * x, axis=-1, keepdims=True) + EPS)
    return x * r * g


def _prenorm(x, g, name):
    def body(x_ref, g_ref, o_ref):
        o_ref[...] = _rms(x_ref[...], g_ref[...]).astype(BF16)

    return pl.pallas_call(
        body, grid=(S // ROW_TILE,),
        in_specs=[pl.BlockSpec((ROW_TILE, D), lambda i: (i, 0)), pl.BlockSpec((1, D), lambda i: (0, 0))],
        out_specs=pl.BlockSpec((ROW_TILE, D), lambda i: (i, 0)),
        out_shape=jax.ShapeDtypeStruct((S, D), BF16), compiler_params=_cp("parallel"), name=name)(x, g)


def _postnorm_res(x, f, g_post, g_next, name):
    def body(x_ref, f_ref, gp_ref, gn_ref, xo_ref, ho_ref):
        xn = x_ref[...] + _rms(f_ref[...], gp_ref[...])
        xo_ref[...] = xn
        ho_ref[...] = _rms(xn, gn_ref[...]).astype(BF16)

    row = pl.BlockSpec((ROW_TILE, D), lambda i: (i, 0))
    vec = pl.BlockSpec((1, D), lambda i: (0, 0))
    return pl.pallas_call(
        body, grid=(S // ROW_TILE,), in_specs=[row, row, vec, vec], out_specs=[row, row],
        out_shape=[jax.ShapeDtypeStruct((S, D), F32), jax.ShapeDtypeStruct((S, D), BF16)],
        compiler_params=_cp("parallel"), name=name)(x, f, g_post, g_next)


def _norm_bwd(f, g, dys, res, out_dtype, name):
    ndy = len(dys)
    has_res = res is not None

    def body(*refs):
        f_ref, g_ref = refs[0], refs[1]
        dy_refs = refs[2:2 + ndy]
        res_ref = refs[2 + ndy] if has_res else None
        o_ref, dg_ref = refs[-2], refs[-1]
        fv = f_ref[...]
        dy = dy_refs[0][...].astype(F32)
        for r in dy_refs[1:]:
            dy = dy + r[...].astype(F32)
        r = lax.rsqrt(jnp.mean(fv * fv, axis=-1, keepdims=True) + EPS)
        n = fv * r
        dn = dy * g_ref[...]
        df = r * (dn - n * jnp.mean(dn * n, axis=-1, keepdims=True))
        if has_res:
            df = df + res_ref[...]
        o_ref[...] = df.astype(out_dtype)

        @pl.when(pl.program_id(0) == 0)
        def _():
            dg_ref[...] = jnp.zeros((1, D), F32)

        dg_ref[...] += jnp.sum(dy * n, axis=0, keepdims=True)

    row = pl.BlockSpec((ROW_TILE, D), lambda i: (i, 0))
    vec = pl.BlockSpec((1, D), lambda i: (0, 0))
    in_specs = [row, vec] + [row] * ndy + ([row] if has_res else [])
    args = [f, g] + list(dys) + ([res] if has_res else [])
    return pl.pallas_call(
        body, grid=(S // ROW_TILE,), in_specs=in_specs, out_specs=[row, vec],
        out_shape=[jax.ShapeDtypeStruct((S, D), out_dtype), jax.ShapeDtypeStruct((1, D), F32)],
        compiler_params=_cp("arbitrary"), name=name)(*args)


def _loss_head(y, target, name):
    def body(y_ref, t_ref, dy_ref, l_ref):
        e = y_ref[...] - t_ref[...]
        dy_ref[...] = e * (1.0 / D)

        @pl.when(pl.program_id(0) == 0)
        def _():
            l_ref[...] = jnp.zeros((8, LANES), F32)

        l_ref[...] += jnp.sum(e * e) * (0.5 / D)

    row = pl.BlockSpec((ROW_TILE, D), lambda i: (i, 0))
    return pl.pallas_call(
        body, grid=(S // ROW_TILE,), in_specs=[row, row],
        out_specs=[row, pl.BlockSpec((8, LANES), lambda i: (0, 0))],
        out_shape=[jax.ShapeDtypeStruct((S, D), F32), jax.ShapeDtypeStruct((8, LANES), F32)],
        compiler_params=_cp("arbitrary"), name=name)(y, target)


def _bucket_tiles():
    a = np.arange(BLK)[:, None]
    b = np.arange(2 * BLK)[None, :]
    dist = a + BLK - b
    out = np.zeros((4, 2, BLK, 2 * BLK), np.int32)
    cfg = [(w // d, d) for w, d in A_GROUPS] + [(BLK - 1, 1)]
    for gi, (max_dist, d) in enumerate(cfg):
        band = (dist >= 0) & (dist <= max_dist)
        tok = np.maximum(dist, 0) * d
        nf = np.maximum(tok, 1).astype(np.float32)
        max_exact = NUM_BUCKETS // 2
        large = max_exact + (np.log(nf / np.float32(max_exact)) / np.float32(math.log(MAX_DISTANCE / max_exact))
                             * np.float32(NUM_BUCKETS - max_exact)).astype(np.int32)
        large = np.minimum(large, NUM_BUCKETS - 1)
        bkt = np.where(tok < max_exact, tok, large).astype(np.int32)
        full = np.where(band, bkt, -1)
        out[gi, 1] = full
        out[gi, 0] = np.where(b >= BLK, full, -1)
    return out


def _bias_tiles(rel_bias, buckets, name):
    def body(tab_ref, bkt_ref, o_ref):
        h = pl.program_id(0)
        bkt = bkt_ref[...]
        acc = jnp.zeros(bkt.shape, F32)
        for bb in range(NUM_BUCKETS):
            acc = jnp.where(bkt == bb, tab_ref[bb, h], acc)
        o_ref[...] = jnp.where(bkt < 0, NEG, acc)

    return pl.pallas_call(
        body, grid=(N_BIAS_HEADS,),
        in_specs=[SMEM, pl.BlockSpec((None, 2, BLK, 2 * BLK), lambda h: (jnp.minimum(h // 4, 3), 0, 0, 0))],
        out_specs=pl.BlockSpec((None, 2, BLK, 2 * BLK), lambda h: (h, 0, 0, 0)),
        out_shape=jax.ShapeDtypeStruct((N_BIAS_HEADS, 2, BLK, 2 * BLK), F32),
        compiler_params=_cp("arbitrary"), name=name)(rel_bias, buckets)


def _bias_grad(gs, buckets, name):
    ng = len(gs)

    def body(*refs):
        g_refs = refs[:ng]
        bkt_ref, o_ref = refs[ng], refs[ng + 1]
        h = pl.program_id(0)
        g = g_refs[0][...]
        for r in g_refs[1:]:
            g = g + r[...]
        bkt = bkt_ref[...]
        row = lax.broadcasted_iota(jnp.int32, (NUM_BUCKETS, LANES), 0)
        lane = lax.broadcasted_iota(jnp.int32, (NUM_BUCKETS, LANES), 1)

        @pl.when(h == 0)
        def _():
            o_ref[...] = jnp.zeros((NUM_BUCKETS, LANES), F32)

        acc = o_ref[...]
        for bb in range(NUM_BUCKETS):
            s = jnp.sum(jnp.where(bkt == bb, g, 0.0))
            acc = jnp.where((row == bb) & (lane == h), s, acc)
        o_ref[...] = acc

    g_spec = pl.BlockSpec((None, BLK, 2 * BLK), lambda h: (h, 0, 0))
    return pl.pallas_call(
        body, grid=(N_BIAS_HEADS,),
        in_specs=[g_spec] * ng + [pl.BlockSpec((None, None, BLK, 2 * BLK), lambda h: (jnp.minimum(h // 4, 3), 1, 0, 0))],
        out_specs=pl.BlockSpec((NUM_BUCKETS, LANES), lambda h: (0, 0)),
        out_shape=jax.ShapeDtypeStruct((NUM_BUCKETS, LANES), F32),
        compiler_params=_cp("arbitrary"), name=name)(*gs, buckets)


def _to_class_major(src_ref, dst_ref, d, scale=None, dtype=None):
    ln = S // d
    for r in range(d):
        v = src_ref[pl.ds(r, ln, stride=d), :] if d > 1 else src_ref[...]
        if scale is not None:
            v = v * scale
        dst_ref[pl.ds(r * ln, ln), :] = v.astype(dtype or dst_ref.dtype)


def _block_rows(b, d):
    nbc = NB // d
    i = b % nbc
    r = b // nbc
    has_prev = (i > 0).astype(jnp.int32)
    prev = pl.multiple_of(jnp.maximum(b - 1, 0) * BLK, BLK)
    nat = i * (BLK * d) + r
    return has_prev, prev, nat


def _lane_halves(v0, v1):
    lane = lax.broadcasted_iota(jnp.int32, (v0.shape[0], LANES), 1)
    return jnp.where(lane < HD, v0, v1)


def _band_fwd(proj, bias, *, d, q0, k0, v0, npairs, bias0, shared_kv, name):
    def body(q_ref, k_ref, v_ref, b_ref, num_ref, st_ref, qs, ks, vs):
        p = pl.program_id(0)
        _to_class_major(q_ref, qs, d, scale=SCALE)
        _to_class_major(k_ref, ks, d)
        _to_class_major(v_ref, vs, d)
        lane = lax.broadcasted_iota(jnp.int32, (BLK, LANES), 1)

        def blk(b, carry):
            has_prev, prev, nat = _block_rows(b, d)
            cur = pl.multiple_of(b * BLK, BLK)
            qb = qs[pl.ds(cur, BLK), :]
            k2 = jnp.concatenate([ks[pl.ds(prev, BLK), :], ks[pl.ds(cur, BLK), :]], axis=0)
            v2 = jnp.concatenate([vs[pl.ds(prev, BLK), :], vs[pl.ds(cur, BLK), :]], axis=0)
            nums, ms, ls = [], [], []
            for hh in range(2):
                qh = qb[:, hh * HD:(hh + 1) * HD]
                if shared_kv:
                    kh = jnp.where(p >= 2, k2[:, HD:], k2[:, :HD])
                    vh = jnp.where(p >= 2, v2[:, HD:], v2[:, :HD])
                else:
                    kh = k2[:, hh * HD:(hh + 1) * HD]
                    vh = v2[:, hh * HD:(hh + 1) * HD]
                z = _dot(qh, kh, 1, 1) + b_ref[hh, has_prev]
                m = jnp.max(z, axis=1, keepdims=True)
                e = jnp.exp(z - m)
                ls.append(jnp.sum(e, axis=1, keepdims=True))
                ms.append(m)
                nums.append(_dot(e.astype(BF16), vh, 1, 0))
            num_t = jnp.concatenate(nums, axis=1)
            st_t = jnp.where(lane < 32, ms[0], jnp.where(lane < 64, ls[0], jnp.where(lane < 96, ms[1], ls[1])))
            if d > 1:
                num_ref[pl.ds(nat, BLK, stride=d), :] = num_t
                st_ref[pl.ds(nat, BLK, stride=d), :] = st_t
            else:
                num_ref[pl.ds(cur, BLK), :] = num_t
                st_ref[pl.ds(cur, BLK), :] = st_t
            return carry

        lax.fori_loop(0, NB, blk, 0, unroll=4)

    slab = lambda off, per_pair: pl.BlockSpec((None, S, LANES), (lambda p: (off + p, 0, 0)) if per_pair else (lambda p: (off, 0, 0)))
    out = pl.BlockSpec((None, S, LANES), lambda p: (p, 0, 0))
    return pl.pallas_call(
        body, grid=(npairs,),
        in_specs=[slab(q0, True), slab(k0, not shared_kv), slab(v0, not shared_kv),
                  pl.BlockSpec((None, 2, 2, BLK, 2 * BLK), lambda p: (bias0 + p, 0, 0, 0, 0))],
        out_specs=[out, out],
        out_shape=[jax.ShapeDtypeStruct((npairs, S, LANES), F32)] * 2,
        scratch_shapes=[pltpu.VMEM((S, LANES), BF16)] * 3,
        compiler_params=_cp("arbitrary"), name=name)(proj, proj, proj, bias)


def _combine_a(nums, stats, name):
    rt = 512

    def body(n0, n1, n2, s0, s1, s2, o_ref, l_ref):
        n_refs, s_refs = (n0, n1, n2), (s0, s1, s2)
        outs, lses = [], []
        for hh in range(2):
            ms = [s[:, 64 * hh:64 * hh + 1] for s in s_refs]
            ls = [s[:, 64 * hh + 32:64 * hh + 33] for s in s_refs]
            mx = jnp.maximum(jnp.maximum(ms[0], ms[1]), ms[2])
            cs = [jnp.exp(m - mx) for m in ms]
            z = cs[0] * ls[0] + cs[1] * ls[1] + cs[2] * ls[2]
            acc = cs[0] * n_refs[0][:, hh * HD:(hh + 1) * HD]
            acc = acc + cs[1] * n_refs[1][:, hh * HD:(hh + 1) * HD]
            acc = acc + cs[2] * n_refs[2][:, hh * HD:(hh + 1) * HD]
            outs.append(acc / z)
            lses.append(mx + jnp.log(z))
        o_ref[...] = jnp.concatenate(outs, axis=1)
        l_ref[...] = _lane_halves(lses[0], lses[1])

    spec = pl.BlockSpec((None, rt, LANES), lambda p, i: (p, i, 0))
    return pl.pallas_call(
        body, grid=(2, S // rt), in_specs=[spec] * 6, out_specs=[spec, spec],
        out_shape=[jax.ShapeDtypeStruct((2, S, LANES), F32)] * 2,
        compiler_params=_cp("parallel", "parallel"), name=name)(*nums, *stats)


def _combine_b(num, stats, sinks, name):
    rt = 512

    def body(sink_ref, n_ref, s_ref, o_ref, l_ref):
        p = pl.program_id(0)
        outs, lses = [], []
        for hh in range(2):
            sink = sink_ref[0, 2 * p + hh]
            m = s_ref[:, 64 * hh:64 * hh + 1]
            l = s_ref[:, 64 * hh + 32:64 * hh + 33]
            mx = jnp.maximum(m, sink)
            c = jnp.exp(m - mx)
            z = l * c + jnp.exp(sink - mx)
            outs.append(n_ref[:, hh * HD:(hh + 1) * HD] * (c / z))
            lses.append(mx + jnp.log(z))
        o_ref[...] = jnp.concatenate(outs, axis=1)
        l_ref[...] = _lane_halves(lses[0], lses[1])

    spec = pl.BlockSpec((None, rt, LANES), lambda p, i: (p, i, 0))
    return pl.pallas_call(
        body, grid=(4, S // rt), in_specs=[SMEM, spec, spec], out_specs=[spec, spec],
        out_shape=[jax.ShapeDtypeStruct((4, S, LANES), F32)] * 2,
        compiler_params=_cp("parallel", "parallel"), name=name)(sinks, num, stats)


def _band_bwd(proj, bias, o, do, lse, sinks, *, d, q0, k0, v0, npairs, bias0, shared_kv, name):
    nkv = 1 if shared_kv else npairs

    def body(sink_ref, q_ref, k_ref, v_ref, b_ref, o_ref, do_ref, lse_ref,
             dq_ref, dk_ref, dv_ref, g_ref, ds_ref,
             qs, ks, vs, dos, lses, dls, dl_nat, dq_nat, dk_cm, dv_cm, kv_nat):
        p = pl.program_id(0)
        lane = lax.broadcasted_iota(jnp.int32, (S, LANES), 1)
        dov = do_ref[...]
        prod = dov * o_ref[...]
        dl0 = jnp.sum(jnp.where(lane < HD, prod, 0.0), axis=1, keepdims=True)
        dl1 = jnp.sum(jnp.where(lane >= HD, prod, 0.0), axis=1, keepdims=True)
        dl_nat[...] = jnp.where(lane < HD, dl0, dl1)
        if shared_kv:
            row8 = lax.broadcasted_iota(jnp.int32, (8, LANES), 0)
            lane8 = lax.broadcasted_iota(jnp.int32, (8, LANES), 1)
            t = jnp.zeros((8, LANES), F32)
            lv = lse_ref[...]
            for hh in range(2):
                sink = sink_ref[0, 2 * p + hh]
                ps = jnp.exp(sink - lv[:, 64 * hh:64 * hh + 1])
                dsink = -jnp.sum(ps * (dl0 if hh == 0 else dl1))
                t = jnp.where((row8 == 0) & (lane8 == hh), dsink, t)
            ds_ref[...] = t
        else:
            ds_ref[...] = jnp.zeros((8, LANES), F32)
        _to_class_major(q_ref, qs, d, scale=SCALE)
        _to_class_major(k_ref, ks, d)
        _to_class_major(v_ref, vs, d)
        _to_class_major(do_ref, dos, d)
        _to_class_major(lse_ref, lses, d)
        _to_class_major(dl_nat, dls, d)

        def zero_kv():
            dk_cm[...] = jnp.zeros((S, LANES), F32)
            dv_cm[...] = jnp.zeros((S, LANES), F32)

        if shared_kv:
            pl.when(p == 0)(zero_kv)
        else:
            zero_kv()

        g_ref[...] = jnp.zeros((2, BLK, 2 * BLK), F32)
        lane2 = lax.broadcasted_iota(jnp.int32, (2 * BLK, LANES), 1)

        def blk(b, carry):
            has_prev, prev, nat = _block_rows(b, d)
            cur = pl.multiple_of(b * BLK, BLK)
            qb = qs[pl.ds(cur, BLK), :]
            dob = dos[pl.ds(cur, BLK), :]
            lb = lses[pl.ds(cur, BLK), :]
            dlb = dls[pl.ds(cur, BLK), :]
            k2 = jnp.concatenate([ks[pl.ds(prev, BLK), :], ks[pl.ds(cur, BLK), :]], axis=0)
            v2 = jnp.concatenate([vs[pl.ds(prev, BLK), :], vs[pl.ds(cur, BLK), :]], axis=0)
            dqs, dks, dvs = [], [], []
            for hh in range(2):
                qh = qb[:, hh * HD:(hh + 1) * HD]
                doh = dob[:, hh * HD:(hh + 1) * HD]
                if shared_kv:
                    kh = jnp.where(p >= 2, k2[:, HD:], k2[:, :HD])
                    vh = jnp.where(p >= 2, v2[:, HD:], v2[:, :HD])
                else:
                    kh = k2[:, hh * HD:(hh + 1) * HD]
                    vh = v2[:, hh * HD:(hh + 1) * HD]
                z = _dot(qh, kh, 1, 1) + b_ref[hh, has_prev]
                pr = jnp.exp(z - lb[:, 64 * hh:64 * hh + 1])
                dp = _dot(doh, vh, 1, 1)
                dz = pr * (dp - dlb[:, 64 * hh:64 * hh + 1])
                g_ref[hh] += dz
                dzb = dz.astype(BF16)
                dqs.append(_dot(dzb, kh, 1, 0) * SCALE)
                dks.append(_dot(dzb, qh, 0, 0))
                dvs.append(_dot(pr.astype(BF16), doh, 0, 0))
            dq_t = jnp.concatenate(dqs, axis=1)
            if shared_kv:
                dk_t = jnp.concatenate([dks[0] + dks[1]] * 2, axis=1)
                dv_t = jnp.concatenate([dvs[0] + dvs[1]] * 2, axis=1)
                mine = (lane2 >= HD) == (p >= 2)
                dk_t = jnp.where(mine, dk_t, 0.0)
                dv_t = jnp.where(mine, dv_t, 0.0)
            else:
                dk_t = jnp.concatenate(dks, axis=1)
                dv_t = jnp.concatenate(dvs, axis=1)
            dk_cm[pl.ds(prev, BLK), :] += dk_t[:BLK]
            dk_cm[pl.ds(cur, BLK), :] += dk_t[BLK:]
            dv_cm[pl.ds(prev, BLK), :] += dv_t[:BLK]
            dv_cm[pl.ds(cur, BLK), :] += dv_t[BLK:]
            if d > 1:
                dq_nat[pl.ds(nat, BLK, stride=d), :] = dq_t
            else:
                dq_nat[pl.ds(cur, BLK), :] = dq_t
            return carry

        lax.fori_loop(0, NB, blk, 0, unroll=4)
        dq_ref[...] = dq_nat[...].astype(BF16)

        def from_class_major(src, dst_ref):
            if d == 1:
                dst_ref[...] = src[...].astype(BF16)
            else:
                ln = S // d
                for r in range(d):
                    kv_nat[pl.ds(r, ln, stride=d), :] = src[pl.ds(r * ln, ln), :]
                dst_ref[...] = kv_nat[...].astype(BF16)

        def write_kv():
            from_class_major(dk_cm, dk_ref)
            from_class_major(dv_cm, dv_ref)

        if shared_kv:
            pl.when(p == npairs - 1)(write_kv)
        else:
            write_kv()

    slab = lambda off, per_pair: pl.BlockSpec((None, S, LANES), (lambda p: (off + p, 0, 0)) if per_pair else (lambda p: (off, 0, 0)))
    pair = pl.BlockSpec((None, S, LANES), lambda p: (p, 0, 0))
    kv_out = pair if not shared_kv else pl.BlockSpec((None, S, LANES), lambda p: (0, 0, 0))
    return pl.pallas_call(
        body, grid=(npairs,),
        in_specs=[SMEM, slab(q0, True), slab(k0, not shared_kv), slab(v0, not shared_kv),
                  pl.BlockSpec((None, 2, 2, BLK, 2 * BLK), lambda p: (bias0 + p, 0, 0, 0, 0)),
                  pair, pair, pair],
        out_specs=[pair, kv_out, kv_out,
                   pl.BlockSpec((None, 2, BLK, 2 * BLK), lambda p: (p, 0, 0, 0)),
                   pl.BlockSpec((None, 8, LANES), lambda p: (p, 0, 0))],
        out_shape=[jax.ShapeDtypeStruct((npairs, S, LANES), BF16),
                   jax.ShapeDtypeStruct((nkv, S, LANES), BF16),
                   jax.ShapeDtypeStruct((nkv, S, LANES), BF16),
                   jax.ShapeDtypeStruct((npairs, 2, BLK, 2 * BLK), F32),
                   jax.ShapeDtypeStruct((npairs, 8, LANES), F32)],
        scratch_shapes=[pltpu.VMEM((S, LANES), BF16)] * 4 + [pltpu.VMEM((S, LANES), F32)] * 7,
        compiler_params=_cp("arbitrary"), name=name)(sinks, proj, proj, proj, bias, o, do, lse)


KC = 512
NSUB = KC // BLK


def _split2(x):
    hi = x.astype(BF16)
    lo = (x - hi.astype(F32)).astype(BF16)
    return hi, lo


def _tri_ones(cmp):
    jj = lax.broadcasted_iota(jnp.int32, (BLK, BLK), 0)
    ss = lax.broadcasted_iota(jnp.int32, (BLK, BLK), 1)
    return jnp.concatenate([cmp(jj, ss).astype(BF16), jnp.ones((BLK, BLK), BF16)], axis=1)


def _sub_sums(x, tri1):
    st = jnp.concatenate([x[:, s * BLK:(s + 1) * BLK] for s in range(NSUB)], axis=0)
    hi, lo = _split2(st)
    r = _dot(hi, tri1, 1, 0) + _dot(lo, tri1, 1, 0)
    return ([r[s * BLK:(s + 1) * BLK, :BLK] for s in range(NSUB)], [r[s * BLK:(s + 1) * BLK, BLK:] for s in range(NSUB)])


def _log_sig_pair(z):
    lb = jnp.minimum(z, 0.0) - jnp.log1p(jnp.exp(-jnp.abs(z)))
    return lb, lb - z


QGROUPS = NB // NSUB


def _stick_fwd(proj, *, q0, k0, v0, name):
    def body(q_ref, k_ref, v_ref, o_ref, t_ref, qs, ks, vs):
        qs[...] = (q_ref[...] * SCALE).astype(BF16)
        ks[...] = k_ref[...].astype(BF16)
        vs[...] = v_ref[...].astype(BF16)
        tri1 = _tri_ones(lambda j, s: j > s)
        col = lax.broadcasted_iota(jnp.int32, (BLK, KC), 1)
        rowi = lax.broadcasted_iota(jnp.int32, (BLK, KC), 0)

        for qg in range(QGROUPS):
            def qblock(ii, carry0, qg=qg):
                t0 = pl.multiple_of((qg * NSUB + ii) * BLK, BLK)
                qb = qs[pl.ds(t0, BLK), :]
                accs = [jnp.zeros((BLK, HD), F32)] * 2
                runs = [jnp.zeros((BLK, BLK), F32)] * 2
                for c in reversed(range(qg + 1)):
                    s0 = c * KC
                    diag = c == qg
                    before = (s0 + col) < (t0 + rowi) if diag else None
                    for hh in range(2):
                        kh = ks[s0:s0 + KC, hh * HD:(hh + 1) * HD]
                        vh = vs[s0:s0 + KC, hh * HD:(hh + 1) * HD]
                        lb, lk = _log_sig_pair(_dot(qb[:, hh * HD:(hh + 1) * HD], kh, 1, 1))
                        if diag:
                            lk = jnp.where(before, lk, 0.0)
                        suf, tot = _sub_sums(lk, tri1)
                        ws, run = [], runs[hh]
                        for s in reversed(range(NSUB)):
                            ws.append(jnp.exp(lb[:, s * BLK:(s + 1) * BLK] + suf[s] + run))
                            run = run + tot[s]
                        w = jnp.concatenate(ws[::-1], axis=1)
                        if diag:
                            w = jnp.where(before, w, 0.0)
                        accs[hh] = accs[hh] + _dot(w.astype(BF16), vh, 1, 0)
                        runs[hh] = run
                o_ref[pl.ds(t0, BLK), :] = jnp.concatenate(accs, axis=1)
                t_ref[pl.ds(t0, BLK), :] = _lane_halves(runs[0], runs[1])
                return carry0

            lax.fori_loop(0, NSUB, qblock, 0)

    slab = lambda off: pl.BlockSpec((None, S, LANES), lambda p: (off + p, 0, 0))
    out = pl.BlockSpec((None, S, LANES), lambda p: (p, 0, 0))
    return pl.pallas_call(
        body, grid=(2,), in_specs=[slab(q0), slab(k0), slab(v0)], out_specs=[out, out],
        out_shape=[jax.ShapeDtypeStruct((2, S, LANES), F32)] * 2,
        scratch_shapes=[pltpu.VMEM((S, LANES), BF16)] * 3,
        compiler_params=_cp("arbitrary"), name=name)(proj, proj, proj)


def _stick_bwd(proj, do, tot, *, q0, k0, v0, name):
    def body(q_ref, k_ref, v_ref, do_ref, t_ref, dq_ref, dk_ref, dv_ref, qs, ks, vs, dos, dk_acc, dv_acc):
        qs[...] = (q_ref[...] * SCALE).astype(BF16)
        ks[...] = k_ref[...].astype(BF16)
        vs[...] = v_ref[...].astype(BF16)
        dos[...] = do_ref[...].astype(BF16)
        dk_acc[...] = jnp.zeros((2, S, HD), F32)
        dv_acc[...] = jnp.zeros((2, S, HD), F32)
        tri_inc = _tri_ones(lambda j, s: j <= s)
        tri_exc = _tri_ones(lambda j, s: j < s)
        col = lax.broadcasted_iota(jnp.int32, (BLK, KC), 1)
        rowi = lax.broadcasted_iota(jnp.int32, (BLK, KC), 0)

        for qg in range(QGROUPS):
            def qblock(ii, carry0, qg=qg):
                t0 = pl.multiple_of((qg * NSUB + ii) * BLK, BLK)
                qb = qs[pl.ds(t0, BLK), :]
                dob = dos[pl.ds(t0, BLK), :]
                tb = t_ref[pl.ds(t0, BLK), :]
                dqs = [jnp.zeros((BLK, HD), F32)] * 2
                pruns = [jnp.zeros((BLK, BLK), F32)] * 2
                eruns = [jnp.zeros((BLK, BLK), F32)] * 2
                for c in range(qg + 1):
                    s0 = c * KC
                    diag = c == qg
                    before = (s0 + col) < (t0 + rowi) if diag else None
                    for hh in range(2):
                        qh = qb[:, hh * HD:(hh + 1) * HD]
                        doh = dob[:, hh * HD:(hh + 1) * HD]
                        tt = tb[:, 64 * hh:64 * hh + 1]
                        kh = ks[s0:s0 + KC, hh * HD:(hh + 1) * HD]
                        vh = vs[s0:s0 + KC, hh * HD:(hh + 1) * HD]
                        lb, lk = _log_sig_pair(_dot(qh, kh, 1, 1))
                        if diag:
                            lk = jnp.where(before, lk, 0.0)
                        pin, ptot = _sub_sums(lk, tri_inc)
                        ws, prun = [], pruns[hh]
                        for s in range(NSUB):
                            ws.append(jnp.exp(lb[:, s * BLK:(s + 1) * BLK] + (tt - (pin[s] + prun))))
                            prun = prun + ptot[s]
                        w = jnp.concatenate(ws, axis=1)
                        if diag:
                            w = jnp.where(before, w, 0.0)
                        e = w * _dot(doh, vh, 1, 1)
                        pex, etot = _sub_sums(e, tri_exc)
                        cs, erun = [], eruns[hh]
                        for s in range(NSUB):
                            cs.append(pex[s] + erun)
                            erun = erun + etot[s]
                        sig = jnp.exp(lb)
                        dz = e * (1.0 - sig) - jnp.concatenate(cs, axis=1) * sig
                        if diag:
                            dz = jnp.where(before, dz, 0.0)
                        dz = dz.astype(BF16)
                        dqs[hh] = dqs[hh] + _dot(dz, kh, 1, 0)
                        dk_acc[hh, s0:s0 + KC, :] += _dot(dz, qh, 0, 0)
                        dv_acc[hh, s0:s0 + KC, :] += _dot(w.astype(BF16), doh, 0, 0)
                        pruns[hh], eruns[hh] = prun, erun
                dq_ref[pl.ds(t0, BLK), :] = (jnp.concatenate(dqs, axis=1) * SCALE).astype(BF16)
                return carry0

            lax.fori_loop(0, NSUB, qblock, 0)
        dk_ref[...] = jnp.concatenate([dk_acc[0], dk_acc[1]], axis=1).astype(BF16)
        dv_ref[...] = jnp.concatenate([dv_acc[0], dv_acc[1]], axis=1).astype(BF16)

    slab = lambda off: pl.BlockSpec((None, S, LANES), lambda p: (off + p, 0, 0))
    pair = pl.BlockSpec((None, S, LANES), lambda p: (p, 0, 0))
    return pl.pallas_call(
        body, grid=(2,), in_specs=[slab(q0), slab(k0), slab(v0), pair, pair], out_specs=[pair] * 3,
        out_shape=[jax.ShapeDtypeStruct((2, S, LANES), BF16)] * 3,
        scratch_shapes=[pltpu.VMEM((S, LANES), BF16)] * 4 + [pltpu.VMEM((2, S, HD), F32)] * 2,
        compiler_params=_cp("arbitrary"), name=name)(proj, proj, proj, do, tot)


def _cat_slabs(ref):
    return jnp.concatenate([ref[s] for s in range(ref.shape[0])], axis=1)


def _merge_fwd(o_a, o_b, o_c, gates, b_gate, wa, wb, wc, w_out, name):
    tm = ROW_TILE

    def body(oa_ref, ob_ref, oc_ref, g_ref, bg_ref, wa_ref, wb_ref, wc_ref, wo_ref, mg_ref, mo_ref):
        acc = jnp.zeros((tm, D), F32)
        for i, (o_ref, w_ref) in enumerate(((oa_ref, wa_ref), (ob_ref, wb_ref), (oc_ref, wc_ref))):
            pr = _dot(_cat_slabs(o_ref).astype(BF16), w_ref[...], 1, 0)
            sg = jax.nn.sigmoid(g_ref[:, i * D:(i + 1) * D] + bg_ref[i:i + 1, :])
            acc = acc + sg * pr
        mg = acc.astype(BF16)
        mg_ref[...] = mg
        mo_ref[...] = _dot(mg, wo_ref[...], 1, 0)

    slabs = lambda n: pl.BlockSpec((n, tm, LANES), lambda i: (0, i, 0))
    full = lambda r, c: pl.BlockSpec((r, c), lambda i: (0, 0))
    row = pl.BlockSpec((tm, D), lambda i: (i, 0))
    return pl.pallas_call(
        body, grid=(S // tm,),
        in_specs=[slabs(2), slabs(4), slabs(2), pl.BlockSpec((tm, GATE_COLS), lambda i: (i, 0)), full(3, D),
                  full(256, D), full(512, D), full(256, D), full(D, D)],
        out_specs=[row, row],
        out_shape=[jax.ShapeDtypeStruct((S, D), BF16), jax.ShapeDtypeStruct((S, D), F32)],
        compiler_params=_cp("parallel"), name=name)(o_a, o_b, o_c, gates, b_gate, wa, wb, wc, w_out)


def _merge_bwd(d_mo, o_a, o_b, o_c, gates, b_gate, wa, wb, wc, w_out, name):
    tm = ROW_TILE

    def body(dmo_ref, oa_ref, ob_ref, oc_ref, g_ref, bg_ref, wa_ref, wb_ref, wc_ref, wo_ref,
             doa_ref, dob_ref, doc_ref, dg_ref, dwa_ref, dwb_ref, dwc_ref, dbg_ref):
        @pl.when(pl.program_id(0) == 0)
        def _():
            dwa_ref[...] = jnp.zeros(dwa_ref.shape, F32)
            dwb_ref[...] = jnp.zeros(dwb_ref.shape, F32)
            dwc_ref[...] = jnp.zeros(dwc_ref.shape, F32)
            dbg_ref[...] = jnp.zeros(dbg_ref.shape, F32)

        dmg = _dot(dmo_ref[...], wo_ref[...], 1, 1)
        trip = ((oa_ref, wa_ref, doa_ref, dwa_ref), (ob_ref, wb_ref, dob_ref, dwb_ref), (oc_ref, wc_ref, doc_ref, dwc_ref))
        for i, (o_ref, w_ref, do_ref, dw_ref) in enumerate(trip):
            ob = _cat_slabs(o_ref).astype(BF16)
            pr = _dot(ob, w_ref[...], 1, 0)
            sg = jax.nn.sigmoid(g_ref[:, i * D:(i + 1) * D] + bg_ref[i:i + 1, :])
            dgate = dmg * pr * sg * (1.0 - sg)
            dg_ref[:, i * D:(i + 1) * D] = dgate.astype(BF16)
            dbg_ref[i:i + 1, :] += jnp.sum(dgate, axis=0, keepdims=True)
            dpr = (dmg * sg).astype(BF16)
            do = _dot(dpr, w_ref[...], 1, 1)
            for s in range(do_ref.shape[0]):
                do_ref[s] = do[:, s * LANES:(s + 1) * LANES]
            dw_ref[...] += _dot(ob, dpr, 0, 0)

    slabs = lambda n: pl.BlockSpec((n, tm, LANES), lambda i: (0, i, 0))
    full = lambda r, c: pl.BlockSpec((r, c), lambda i: (0, 0))
    row = pl.BlockSpec((tm, D), lambda i: (i, 0))
    return pl.pallas_call(
        body, grid=(S // tm,),
        in_specs=[row, slabs(2), slabs(4), slabs(2), pl.BlockSpec((tm, GATE_COLS), lambda i: (i, 0)), full(3, D),
                  full(256, D), full(512, D), full(256, D), full(D, D)],
        out_specs=[slabs(2), slabs(4), slabs(2), pl.BlockSpec((tm, GATE_COLS), lambda i: (i, 0)),
                   full(256, D), full(512, D), full(256, D), full(3, D)],
        out_shape=[jax.ShapeDtypeStruct((2, S, LANES), F32), jax.ShapeDtypeStruct((4, S, LANES), F32),
                   jax.ShapeDtypeStruct((2, S, LANES), F32), jax.ShapeDtypeStruct((S, GATE_COLS), BF16),
                   jax.ShapeDtypeStruct((256, D), F32), jax.ShapeDtypeStruct((512, D), F32),
                   jax.ShapeDtypeStruct((256, D), F32), jax.ShapeDtypeStruct((3, D), F32)],
        compiler_params=_cp("arbitrary"), name=name)(d_mo, o_a, o_b, o_c, gates, b_gate, wa, wb, wc, w_out)


FC = 256
GELU_K = math.sqrt(2.0 / math.pi)
GELU_C = 0.044715


def _shift_down(x, n):
    row = lax.broadcasted_iota(jnp.int32, x.shape, 0)
    return jnp.where(row < n, 0.0, pltpu.roll(x, n, 0))


def _shift_up(x, n):
    row = lax.broadcasted_iota(jnp.int32, x.shape, 0)
    return jnp.where(row >= x.shape[0] - n, 0.0, pltpu.roll(x, x.shape[0] - n, 0))


def _conv(u, w_ref, half, b):
    return (w_ref[0:1, half, :] * _shift_down(u, 2) + w_ref[1:2, half, :] * _shift_down(u, 1)
            + w_ref[2:3, half, :] * u + b)


def _ffn_act(u, conv_w, conv_b, name):
    def body(u_ref, w_ref, b_ref, a_ref):
        yg = _conv(u_ref[0], w_ref, 0, b_ref[0:1, :])
        yv = _conv(u_ref[1], w_ref, 1, b_ref[1:2, :])
        th = jnp.tanh(GELU_K * (yg + GELU_C * yg * yg * yg))
        a_ref[...] = (0.5 * yg * (1.0 + th) * yv).astype(BF16)

    return pl.pallas_call(
        body, grid=(D_FF // FC,),
        in_specs=[pl.BlockSpec((2, S, FC), lambda j: (0, 0, j)), pl.BlockSpec((3, 2, FC), lambda j: (0, 0, j)),
                  pl.BlockSpec((2, FC), lambda j: (0, j))],
        out_specs=pl.BlockSpec((S, FC), lambda j: (0, j)),
        out_shape=jax.ShapeDtypeStruct((S, D_FF), BF16),
        compiler_params=_cp("parallel"), name=name)(u, conv_w, conv_b)


def _ffn_act_bwd(u, d_a, conv_w, conv_b, name):
    def body(u_ref, da_ref, w_ref, b_ref, du_ref, dw_ref, db_ref):
        ug, uv = u_ref[0], u_ref[1]
        yg = _conv(ug, w_ref, 0, b_ref[0:1, :])
        yv = _conv(uv, w_ref, 1, b_ref[1:2, :])
        inner = GELU_K * (yg + GELU_C * yg * yg * yg)
        th = jnp.tanh(inner)
        gelu = 0.5 * yg * (1.0 + th)
        dgelu = 0.5 * (1.0 + th) + 0.5 * yg * (1.0 - th * th) * GELU_K * (1.0 + 3.0 * GELU_C * yg * yg)
        da = da_ref[...]
        for half, (uu, dy) in enumerate(((ug, da * yv * dgelu), (uv, da * gelu))):
            du = (w_ref[2:3, half, :] * dy + w_ref[1:2, half, :] * _shift_up(dy, 1)
                  + w_ref[0:1, half, :] * _shift_up(dy, 2))
            du_ref[half] = du.astype(BF16)
            dw_ref[0:1, half, :] = jnp.sum(dy * _shift_down(uu, 2), axis=0, keepdims=True)
            dw_ref[1:2, half, :] = jnp.sum(dy * _shift_down(uu, 1), axis=0, keepdims=True)
            dw_ref[2:3, half, :] = jnp.sum(dy * uu, axis=0, keepdims=True)
            db_ref[half:half + 1, :] = jnp.sum(dy, axis=0, keepdims=True)

    return pl.pallas_call(
        body, grid=(D_FF // FC,),
        in_specs=[pl.BlockSpec((2, S, FC), lambda j: (0, 0, j)), pl.BlockSpec((S, FC), lambda j: (0, j)),
                  pl.BlockSpec((3, 2, FC), lambda j: (0, 0, j)), pl.BlockSpec((2, FC), lambda j: (0, j))],
        out_specs=[pl.BlockSpec((2, S, FC), lambda j: (0, 0, j)), pl.BlockSpec((3, 2, FC), lambda j: (0, 0, j)),
                   pl.BlockSpec((2, FC), lambda j: (0, j))],
        out_shape=[jax.ShapeDtypeStruct((2, S, D_FF), BF16), jax.ShapeDtypeStruct((3, 2, D_FF), F32),
                   jax.ShapeDtypeStruct((2, D_FF), F32)],
        compiler_params=_cp("parallel"), name=name)(u, d_a, conv_w, conv_b)


def _layer_fwd(x, h1, w, bias, lname):
    n = lambda s: f"{lname}_{s}"
    w.need("in", h1)
    tn = 768
    proj = _mm(h1, w["w_in"], grid=(S // 1024, QKV_COLS // tn, 1),
               a_spec=pl.BlockSpec((1024, D), lambda i, j, k: (i, 0)),
               b_spec=pl.BlockSpec((D, tn), lambda i, j, k: (0, j)),
               out_shape=jax.ShapeDtypeStruct((QKV_SLABS, S, LANES), F32),
               out_spec=pl.BlockSpec((tn // LANES, 1024, LANES), lambda i, j, k: (j, i, 0)),
               ca=1, cb=0, acc_shape=(1024, tn), out_slab=True, name=n("proj_qkv"))
    gates = _mm(h1, w["w_in"], grid=(S // 1024, GATE_COLS // tn, 1),
                a_spec=pl.BlockSpec((1024, D), lambda i, j, k: (i, 0)),
                b_spec=pl.BlockSpec((D, tn), lambda i, j, k: (0, j + QKV_COLS // tn)),
                out_shape=jax.ShapeDtypeStruct((S, GATE_COLS), F32),
                out_spec=pl.BlockSpec((1024, tn), lambda i, j, k: (i, j)),
                ca=1, cb=0, acc_shape=(1024, tn), name=n("proj_gate"))
    nums, stats = [], []
    for g, (_, d) in enumerate(A_GROUPS):
        nm, st = _band_fwd(proj, bias, d=d, q0=2 * g, k0=6 + 2 * g, v0=12 + 2 * g, npairs=2, bias0=2 * g,
                           shared_kv=False, name=n(f"attn_a{g}_fwd"))
        nums.append(nm)
        stats.append(st)
    o_a, lse_a = _combine_a(nums, stats, n("attn_a_combine"))
    nm_b, st_b = _band_fwd(proj, bias, d=1, q0=18, k0=22, v0=23, npairs=4, bias0=6, shared_kv=True, name=n("attn_b_fwd"))
    o_b, lse_b = _combine_b(nm_b, st_b, w["sinks"], n("attn_b_combine"))
    o_c, tot_c = _stick_fwd(proj, q0=24, k0=26, v0=28, name=n("attn_c_fwd"))
    w.need("mix", tot_c)
    merged, mo = _merge_fwd(o_a, o_b, o_c, gates, w["b_gate"], w["w_br_a"], w["w_br_b"], w["w_br_c"], w["w_out"], n("merge_fwd"))
    x2, h2 = _postnorm_res(x, mo, w["attn_post_norm"], w["ffn_pre_norm"], n("attn_post"))
    w.need("ffn", h2)
    u = _mm(h2, w["w_up"], grid=(S // 1024, 2 * D_FF // 1024, 1),
            a_spec=pl.BlockSpec((1024, D), lambda i, j, k: (i, 0)),
            b_spec=pl.BlockSpec((D, 1024), lambda i, j, k: (0, j)),
            out_shape=jax.ShapeDtypeStruct((2, S, D_FF), F32),
            out_spec=pl.BlockSpec((None, 1024, 1024), lambda i, j, k: (j // 4, i, j % 4)),
            ca=1, cb=0, acc_shape=(1024, 1024), name=n("ffn_up"))
    a = _ffn_act(u, w["conv_w"], w["conv_b"], n("ffn_act"))
    fo = _mm_nn(a, w["w_down"], F32, 1024, 1024, 1024, n("ffn_down"))
    saved = dict(x=x, h1=h1, proj=proj, gates=gates, o_a=o_a, lse_a=lse_a, o_b=o_b, lse_b=lse_b, o_c=o_c, tot_c=tot_c,
                 merged=merged, mo=mo, x2=x2, h2=h2, u=u, a=a, fo=fo)
    return saved


def _layer_bwd(dx3, sv, w, bias, lname, tok=None, on_part=None):
    n = lambda s: f"{lname}_{s}"
    g = {}

    def part(group, vec):
        t = on_part(group, g) if on_part is not None else None
        return vec if t is None else vec + t

    gain = w["ffn_post_norm"] if tok is None else w["ffn_post_norm"] + tok
    d_fo, g["ffn_post_norm"] = _norm_bwd(sv["fo"], gain, [dx3], None, BF16, n("ffn_post_bwd"))
    d_a = _mm_nt(d_fo, w["w_down"], F32, 1024, 1024, 1024, n("ffn_down_bwd_x"))
    g["w_down"] = _mm_tn(sv["a"], d_fo, BF16, 1024, 1024, 1024, n("ffn_down_bwd_w"))
    d_u, dcw, dcb = _ffn_act_bwd(sv["u"], d_a, w["conv_w"], w["conv_b"], n("ffn_act_bwd"))
    g["conv_w"] = dcw.reshape(3, 2 * D_FF)
    g["conv_b"] = dcb.reshape(1, 2 * D_FF)
    g["w_up"] = _mm(sv["h2"], d_u, grid=(1, 2 * D_FF // 1024, S // 1024),
                    a_spec=pl.BlockSpec((1024, D), lambda i, j, k: (k, 0)),
                    b_spec=pl.BlockSpec((None, 1024, 1024), lambda i, j, k: (j // 4, k, j % 4)),
                    out_shape=jax.ShapeDtypeStruct((D, 2 * D_FF), BF16),
                    out_spec=pl.BlockSpec((D, 1024), lambda i, j, k: (0, j)),
                    ca=0, cb=0, acc_shape=(D, 1024), name=n("ffn_up_bwd_w"))
    d_h2 = _mm(d_u, w["w_up"], grid=(S // 1024, 1, 2 * D_FF // 1024),
               a_spec=pl.BlockSpec((None, 1024, 1024), lambda i, j, k: (k // 4, i, k % 4)),
               b_spec=pl.BlockSpec((D, 1024), lambda i, j, k: (0, k)),
               out_shape=jax.ShapeDtypeStruct((S, D), F32),
               out_spec=pl.BlockSpec((1024, D), lambda i, j, k: (i, 0)),
               ca=1, cb=1, acc_shape=(1024, D), name=n("ffn_up_bwd_x"))
    dx2, g["ffn_pre_norm"] = _norm_bwd(sv["x2"], part("ffn", w["ffn_pre_norm"]), [d_h2], dx3, F32, n("ffn_pre_bwd"))
    d_mo, g["attn_post_norm"] = _norm_bwd(sv["mo"], w["attn_post_norm"], [dx2], None, BF16, n("attn_post_bwd"))
    g["w_out"] = _mm_tn(sv["merged"], d_mo, BF16, 1024, 1024, 1024, n("out_bwd_w"))
    do_a, do_b, do_c, d_gates, dwa, dwb, dwc, g["b_gate"] = _merge_bwd(
        d_mo, sv["o_a"], sv["o_b"], sv["o_c"], sv["gates"], w["b_gate"], w["w_br_a"], w["w_br_b"], w["w_br_c"],
        w["w_out"], n("merge_bwd"))
    g["w_br_a"], g["w_br_b"], g["w_br_c"] = dwa, dwb, dwc
    sinks = part("mix", w["sinks"])
    proj = sv["proj"]
    dqa, dka, dva, gbias = [], [], [], []
    for gi, (_, d) in enumerate(A_GROUPS):
        dq, dk, dv, gg, _ = _band_bwd(proj, bias, sv["o_a"], do_a, sv["lse_a"], sinks, d=d, q0=2 * gi, k0=6 + 2 * gi,
                                      v0=12 + 2 * gi, npairs=2, bias0=2 * gi, shared_kv=False, name=n(f"attn_a{gi}_bwd"))
        dqa.append(dq), dka.append(dk), dva.append(dv), gbias.append(gg)
    dqb, dkb, dvb, ggb, dsink = _band_bwd(proj, bias, sv["o_b"], do_b, sv["lse_b"], sinks, d=1, q0=18, k0=22, v0=23,
                                          npairs=4, bias0=6, shared_kv=True, name=n("attn_b_bwd"))
    gbias.append(ggb)
    g["bias_g"] = jnp.concatenate(gbias, axis=0).reshape(N_BIAS_HEADS, BLK, 2 * BLK)
    g["sinks"] = dsink[:, 0, :2].reshape(1, 8)
    dqc, dkc, dvc = _stick_bwd(proj, do_c, sv["tot_c"], q0=24, k0=26, v0=28, name=n("attn_c_bwd"))
    dqkv = jnp.concatenate(dqa + dka + dva + [dqb, dkb, dvb, dqc, dkc, dvc], axis=0)
    ts = 6
    d_h1a = _mm(dqkv, w["w_in"], grid=(S // 1024, 1, QKV_SLABS // ts),
                a_spec=pl.BlockSpec((ts, 1024, LANES), lambda i, j, k: (k, i, 0)),
                b_spec=pl.BlockSpec((D, ts * LANES), lambda i, j, k: (0, k)),
                out_shape=jax.ShapeDtypeStruct((S, D), F32),
                out_spec=pl.BlockSpec((1024, D), lambda i, j, k: (i, 0)),
                ca=1, cb=1, acc_shape=(1024, D), a_slab=True, name=n("in_bwd_x_qkv"))
    d_h1b = _mm(d_gates, w["w_in"], grid=(S // 1024, 1, GATE_COLS // 768),
                a_spec=pl.BlockSpec((1024, 768), lambda i, j, k: (i, k)),
                b_spec=pl.BlockSpec((D, 768), lambda i, j, k: (0, k + QKV_COLS // 768)),
                out_shape=jax.ShapeDtypeStruct((S, D), F32),
                out_spec=pl.BlockSpec((1024, D), lambda i, j, k: (i, 0)),
                ca=1, cb=1, acc_shape=(1024, D), name=n("in_bwd_x_gate"))
    dw_in = _mm(sv["h1"], dqkv, grid=(1, QKV_SLABS // ts, S // 1024),
                a_spec=pl.BlockSpec((1024, D), lambda i, j, k: (k, 0)),
                b_spec=pl.BlockSpec((ts, 1024, LANES), lambda i, j, k: (j, k, 0)),
                out_shape=jax.ShapeDtypeStruct((D, IN_COLS), BF16),
                out_spec=pl.BlockSpec((D, ts * LANES), lambda i, j, k: (0, j)),
                ca=0, cb=0, acc_shape=(D, ts * LANES), b_slab=True, name=n("in_bwd_w_qkv"))
    g["w_in"] = _mm(sv["h1"], d_gates, grid=(1, GATE_COLS // 768, S // 1024),
                    a_spec=pl.BlockSpec((1024, D), lambda i, j, k: (k, 0)),
                    b_spec=pl.BlockSpec((1024, 768), lambda i, j, k: (k, j)),
                    out_shape=jax.ShapeDtypeStruct((D, IN_COLS), BF16),
                    out_spec=pl.BlockSpec((D, 768), lambda i, j, k: (0, j + QKV_COLS // 768)),
                    ca=0, cb=0, acc_shape=(D, 768), alias_out=dw_in, name=n("in_bwd_w_gate"))
    dx, g["attn_pre_norm"] = _norm_bwd(sv["x"], w["attn_pre_norm"], [d_h1a, d_h1b], dx2, F32, n("attn_pre_bwd"))
    tok_in = on_part("in", g) if on_part is not None else None
    return dx, g, tok_in


def _local_step(x, target, ws, rel_bias, tok=None, on_grads=None):
    buckets = jnp.asarray(_bucket_tiles())
    bias = _bias_tiles(rel_bias, buckets, "bias_tiles").reshape(N_BIAS_HEADS // 2, 2, 2, BLK, 2 * BLK)
    saved = []
    gain0 = ws[0]["attn_pre_norm"] if tok is None else ws[0]["attn_pre_norm"] + tok
    h1 = _prenorm(x, gain0, "l0_attn_pre")
    for l in range(DEPTH):
        sv = _layer_fwd(x, h1, ws[l], bias, f"l{l}")
        saved.append(sv)
        g_next = ws[l + 1]["attn_pre_norm"] if l + 1 < DEPTH else ws[l]["attn_pre_norm"]
        x, h1 = _postnorm_res(sv["x2"], sv["fo"], ws[l]["ffn_post_norm"], g_next, f"l{l}_ffn_post")
    dy, loss_tile = _loss_head(x, target, "loss_head")
    grads = [None] * DEPTH
    tok = None
    for l in reversed(range(DEPTH)):
        on_part = None if on_grads is None else functools.partial(on_grads, l)
        dy, grads[l], tok = _layer_bwd(dy, saved[l], ws[l], bias, f"l{l}", tok, on_part)
    g_rel = _bias_grad([grads[l]["bias_g"] for l in range(DEPTH)], buckets, "bias_grad")[:, :N_BIAS_HEADS]
    return loss_tile[0, 0], dy, grads, g_rel


def _coords():
    return lax.axis_index("x"), lax.axis_index("y"), lax.axis_index("c")


def _peer(rel):
    x, y, c = _coords()
    return (1 - x if rel & 4 else x, 1 - y if rel & 2 else y, 1 - c if rel & 1 else c)


def _exchange(srcs, dst_shapes, src_win, dst_win, name):
    nt = len(srcs)

    def body(*refs):
        src_refs, dst_refs = refs[:nt], refs[nt:2 * nt]
        send_sems, recv_sems, local_sems = refs[2 * nt:]
        x, y, c = _coords()
        me = 4 * x + 2 * y + c
        locals_ = []
        for t in range(nt):
            cp = pltpu.make_async_copy(src_win(t, src_refs[t], me), dst_win(t, dst_refs[t], me), local_sems.at[t])
            cp.start()
            locals_.append(cp)
        sends = []
        for rel in range(1, NDEV):
            px, py, pc = _peer(rel)
            q = 4 * px + 2 * py + pc
            for t in range(nt):
                cp = pltpu.make_async_remote_copy(
                    src_ref=src_win(t, src_refs[t], q), dst_ref=dst_win(t, dst_refs[t], me),
                    send_sem=send_sems.at[rel - 1, t], recv_sem=recv_sems.at[rel - 1, t],
                    device_id=(px, py, pc), device_id_type=MESH)
                cp.start()
                sends.append(cp)
        for rel in range(1, NDEV):
            px, py, pc = _peer(rel)
            q = 4 * px + 2 * py + pc
            for t in range(nt):
                pltpu.make_async_remote_copy(
                    src_ref=src_win(t, src_refs[t], me), dst_ref=dst_win(t, dst_refs[t], q),
                    send_sem=send_sems.at[rel - 1, t], recv_sem=recv_sems.at[rel - 1, t],
                    device_id=(px, py, pc), device_id_type=MESH).wait_recv()
        for cp in sends:
            cp.wait_send()
        for cp in locals_:
            cp.wait()

    return pl.pallas_call(
        body, in_specs=[ANY] * nt, out_specs=[ANY] * nt, out_shape=dst_shapes,
        scratch_shapes=[pltpu.SemaphoreType.DMA((NDEV - 1, nt)), pltpu.SemaphoreType.DMA((NDEV - 1, nt)),
                        pltpu.SemaphoreType.DMA((nt,))],
        name=name)(*srcs)


BIG = (("w_in", 1, 864), ("w_br_a", 1, 128), ("w_br_b", 1, 128), ("w_br_c", 1, 128), ("w_out", 0, 128),
       ("w_up", 1, 1024), ("w_down", 0, 512))


NBIG = len(BIG)
BIG_FULL = {"w_in": (D, IN_COLS), "w_br_a": (256, D), "w_br_b": (512, D), "w_br_c": (256, D), "w_out": (D, D),
            "w_up": (D, 2 * D_FF), "w_down": (D_FF, D)}
LAYER_GROUPS = (("in", (0,)), ("mix", (1, 2, 3, 4)), ("ffn", (5, 6)))

HBM_SPEC = pl.BlockSpec(memory_space=pltpu.HBM)
SEM_SPEC = pl.BlockSpec(memory_space=pltpu.SEMAPHORE)


def _hbm(a):
    return pltpu.with_memory_space_constraint(a, pltpu.HBM)


def _shard_window(t, ref, k):
    nm, ax, ext = BIG[t % NBIG]
    if nm == "w_in":
        return ref.at[k]
    off = pl.multiple_of(k * ext, ext)
    if ax == 0:
        return ref.at[pl.ds(off, ext), :]
    return ref.at[:, pl.ds(off, ext)]


def _whole(t, ref, k):
    return ref


def _slot(t, ref, k):
    return ref.at[k]


def _own_block_spec(t, rows, me_of):
    nm, ax, ext = BIG[t % NBIG]
    r, c = BIG_FULL[nm]
    if nm == "w_in":
        return pl.BlockSpec((None, rows, ext), lambda i, m: (me_of(m), i, 0))
    if ax == 0:
        return pl.BlockSpec((rows, c), lambda i, m: (me_of(m) * (ext // rows) + i, 0))
    return pl.BlockSpec((rows, ext), lambda i, m: (i, me_of(m)))


def _cast_own(t, shard, me_arr, name):
    nm, ax, ext = BIG[t % NBIG]
    nr, nc = shard.shape
    rows = min(nr, 256)
    shape = (NDEV, D, ext) if nm == "w_in" else BIG_FULL[nm]

    def body(m_ref, s_ref, o_ref):
        o_ref[...] = s_ref[...].astype(BF16)

    return pl.pallas_call(
        body, grid_spec=pltpu.PrefetchScalarGridSpec(
            num_scalar_prefetch=1, grid=(nr // rows,),
            in_specs=[pl.BlockSpec((rows, nc), lambda i, m: (i, 0))],
            out_specs=_own_block_spec(t, rows, lambda m: m[0])),
        out_shape=jax.ShapeDtypeStruct(shape, BF16), compiler_params=_cp("arbitrary"), name=name)(me_arr, shard)


def _xchg_start(srcs, lands, groups, src_win, dst_win, after, name):
    ns = 0 if srcs is None else len(srcs)
    nt, ng = len(lands), len(groups)
    ins = ([] if srcs is None else list(srcs)) + list(lands)

    def body(*refs):
        src_refs, land_refs = refs[:ns], refs[ns:ns + nt]
        sems = refs[ns + nt + 1:ns + nt + 1 + 2 * ng]
        token = refs[-1]
        x, y, c = _coords()
        me = 4 * x + 2 * y + c
        for gi, grp in enumerate(groups):
            for j, t in enumerate(grp):
                for rel in range(1, NDEV):
                    px, py, pc = _peer(rel)
                    q = 4 * px + 2 * py + pc
                    src = dst_win(t, land_refs[t], me) if srcs is None else src_win(t, src_refs[t], q)
                    pltpu.make_async_remote_copy(
                        src_ref=src, dst_ref=dst_win(t, land_refs[t], me),
                        send_sem=sems[2 * gi].at[(rel - 1) * len(grp) + j],
                        recv_sem=sems[2 * gi + 1].at[(rel - 1) * len(grp) + j],
                        device_id=(px, py, pc), device_id_type=MESH).start()
        token[...] = jnp.zeros((8, LANES), F32)

    out_shape = []
    for grp in groups:
        out_shape += [pltpu.SemaphoreType.DMA(((NDEV - 1) * len(grp),))] * 2
    out_shape += [pltpu.HBM(a.shape, a.dtype) for a in ins]
    out_shape.append(jax.ShapeDtypeStruct((8, LANES), F32))
    outs = pl.pallas_call(
        body, in_specs=[HBM_SPEC] * len(ins) + [ANY],
        out_specs=[SEM_SPEC] * (2 * ng) + [HBM_SPEC] * len(ins) + [pl.BlockSpec(memory_space=pltpu.VMEM)],
        out_shape=out_shape, input_output_aliases={i: 2 * ng + i for i in range(len(ins))},
        compiler_params=pltpu.CompilerParams(has_side_effects=pltpu.SideEffectType.DATAFLOW_SIDE_EFFECTING),
        name=name)(*[_hbm(a) for a in ins], after)
    sems = [(outs[2 * gi], outs[2 * gi + 1]) for gi in range(ng)]
    thru = list(outs[2 * ng:2 * ng + len(ins)])
    return sems, (None if srcs is None else thru[:ns]), thru[ns:], outs[-1]


def _xchg_wait(sems, srcs, lands, tids, after, src_win, dst_win, name):
    ns = 0 if srcs is None else len(srcs)
    n = len(lands)
    send_sem, recv_sem = sems
    ins = ([] if srcs is None else list(srcs)) + list(lands)

    def body(*refs):
        src_refs, land_refs = refs[:ns], refs[ns:ns + n]
        ssem, rsem = refs[ns + n], refs[ns + n + 1]
        x, y, c = _coords()
        me = 4 * x + 2 * y + c
        for j, t in enumerate(tids):
            for rel in range(1, NDEV):
                px, py, pc = _peer(rel)
                q = 4 * px + 2 * py + pc
                src = dst_win(t, land_refs[j], me) if srcs is None else src_win(t, src_refs[j], q)
                cp = pltpu.make_async_remote_copy(
                    src_ref=src, dst_ref=dst_win(t, land_refs[j], q),
                    send_sem=ssem.at[(rel - 1) * n + j], recv_sem=rsem.at[(rel - 1) * n + j],
                    device_id=(px, py, pc), device_id_type=MESH)
                cp.wait_send()
                cp.wait_recv()

    outs = pl.pallas_call(
        body, in_specs=[HBM_SPEC] * len(ins) + [SEM_SPEC, SEM_SPEC, ANY], out_specs=[HBM_SPEC] * len(ins),
        out_shape=[pltpu.HBM(a.shape, a.dtype) for a in ins],
        input_output_aliases={i: i for i in range(len(ins))},
        compiler_params=pltpu.CompilerParams(has_side_effects=pltpu.SideEffectType.DATAFLOW_SIDE_EFFECTING),
        name=name)(*ins, send_sem, recv_sem, after)
    return (None if srcs is None else list(outs[:ns])), list(outs[ns:])


class _Weights:
    def __init__(self, ready, pending=None):
        self.ready = dict(ready)
        self.pending = dict(pending or {})

    def __getitem__(self, k):
        return self.ready[k]

    def need(self, group, after):
        fn = self.pending.pop(group, None)
        if fn is not None:
            self.ready.update(fn(after))


def _adamw_math(w, g, m, v):
    m2 = ADAM_B1 * m + (1.0 - ADAM_B1) * g
    v2 = ADAM_B2 * v + (1.0 - ADAM_B2) * (g * g)
    m_hat = m2 / (1.0 - ADAM_B1 ** ADAM_STEP)
    v_hat = v2 / (1.0 - ADAM_B2 ** ADAM_STEP)
    delta = -ADAM_LR * (m_hat / (jnp.sqrt(v_hat) + ADAM_EPS) + ADAM_WD * w)
    return delta, m2, v2


def _adamw(t, parts, own, me_arr, w, m, v, layer, prev, rows, name):
    nl, nr, nc = w.shape

    def body(me_ref, p_ref, own_ref, w_ref, m_ref, v_ref, *rest):
        g_ref, d_ref, m2_ref, v2_ref = rest[-4:]
        me = me_ref[0]
        g = None
        for k in range(NDEV):
            term = jnp.where(me == k, own_ref[...], p_ref[k]).astype(F32)
            g = term if g is None else g + term
        delta, m2, v2 = _adamw_math(w_ref[...], g, m_ref[...], v_ref[...])
        g_ref[...] = g
        d_ref[...] = delta
        m2_ref[...] = m2
        v2_ref[...] = v2

    blk = pl.BlockSpec((None, rows, nc), lambda i, mm: (layer, i, 0))
    pblk = pl.BlockSpec((NDEV, rows, nc), lambda i, mm: (0, i, 0))
    extra = [] if prev is None else list(prev)
    return pl.pallas_call(
        body, grid_spec=pltpu.PrefetchScalarGridSpec(
            num_scalar_prefetch=1, grid=(nr // rows,),
            in_specs=[pblk, _own_block_spec(t, rows, lambda mm: mm[0]), blk, blk, blk] + [ANY] * len(extra),
            out_specs=[blk] * 4),
        out_shape=[jax.ShapeDtypeStruct(w.shape, F32)] * 4,
        input_output_aliases={6 + k: k for k in range(len(extra))},
        compiler_params=_cp("arbitrary"), name=name)(me_arr, parts, own, w, m, v, *extra)


SMALL_REPL = (("rel_bias", NUM_BUCKETS * N_BIAS_HEADS), ("attn_pre_norm", DEPTH * D), ("sinks", DEPTH * 8),
              ("attn_post_norm", DEPTH * D), ("ffn_pre_norm", DEPTH * D), ("conv_b", DEPTH * 2 * D_FF),
              ("ffn_post_norm", DEPTH * D))
SMALL_SHARD = (("b_gate", (DEPTH, 3, D), 128), ("conv_w", (DEPTH, 3, 2 * D_FF), 1024))


def _pack(vecs):
    flat = jnp.concatenate([v.reshape(-1).astype(F32) for v in vecs])
    n = flat.shape[0]
    rows = -(-n // (8 * LANES)) * 8
    return jnp.pad(flat, (0, rows * LANES - n)).reshape(rows, LANES)


def _unpack(packed, sizes):
    flat = packed.reshape(-1)
    out, off = [], 0
    for sz in sizes:
        out.append(flat[off:off + sz])
        off += sz
    return out


def _small_sum(parts, name):
    r = parts.shape[1]

    def body(p_ref, o_ref):
        g = p_ref[0]
        for k in range(1, NDEV):
            g = g + p_ref[k]
        o_ref[...] = g

    return pl.pallas_call(
        body, in_specs=[pl.BlockSpec(memory_space=pltpu.VMEM)], out_specs=pl.BlockSpec(memory_space=pltpu.VMEM),
        out_shape=jax.ShapeDtypeStruct((r, LANES), F32), name=name)(parts)


def _small_adamw(g, w, m, v, name):
    def body(g_ref, w_ref, m_ref, v_ref, d_ref, m2_ref, v2_ref):
        delta, m2, v2 = _adamw_math(w_ref[...], g_ref[...], m_ref[...], v_ref[...])
        d_ref[...] = delta
        m2_ref[...] = m2
        v2_ref[...] = v2

    vm = pl.BlockSpec(memory_space=pltpu.VMEM)
    return pl.pallas_call(
        body, in_specs=[vm] * 4, out_specs=[vm] * 3,
        out_shape=[jax.ShapeDtypeStruct(g.shape, F32)] * 3, name=name)(g, w, m, v)


def kernel(x, rel_bias, attn_pre_norm, w_in, b_gate, sinks, w_br_a, w_br_b, w_br_c, w_out, attn_post_norm, ffn_pre_norm, w_up, conv_w, conv_b, w_down, ffn_post_norm, loss_target, m_rel_bias, m_attn_pre_norm, m_w_in, m_b_gate, m_sinks, m_w_br_a, m_w_br_b, m_w_br_c, m_w_out, m_attn_post_norm, m_ffn_pre_norm, m_w_up, m_conv_w, m_conv_b, m_w_down, m_ffn_post_norm, v_rel_bias, v_attn_pre_norm, v_w_in, v_b_gate, v_sinks, v_w_br_a, v_w_br_b, v_w_br_c, v_w_out, v_attn_post_norm, v_ffn_pre_norm, v_w_up, v_conv_w, v_conv_b, v_w_down, v_ffn_post_norm):
    P = dict(rel_bias=rel_bias, attn_pre_norm=attn_pre_norm, w_in=w_in, b_gate=b_gate, sinks=sinks, w_br_a=w_br_a,
             w_br_b=w_br_b, w_br_c=w_br_c, w_out=w_out, attn_post_norm=attn_post_norm, ffn_pre_norm=ffn_pre_norm,
             w_up=w_up, conv_w=conv_w, conv_b=conv_b, w_down=w_down, ffn_post_norm=ffn_post_norm)
    M = dict(rel_bias=m_rel_bias, attn_pre_norm=m_attn_pre_norm, w_in=m_w_in, b_gate=m_b_gate, sinks=m_sinks,
             w_br_a=m_w_br_a, w_br_b=m_w_br_b, w_br_c=m_w_br_c, w_out=m_w_out, attn_post_norm=m_attn_post_norm,
             ffn_pre_norm=m_ffn_pre_norm, w_up=m_w_up, conv_w=m_conv_w, conv_b=m_conv_b, w_down=m_w_down,
             ffn_post_norm=m_ffn_post_norm)
    V = dict(rel_bias=v_rel_bias, attn_pre_norm=v_attn_pre_norm, w_in=v_w_in, b_gate=v_b_gate, sinks=v_sinks,
             w_br_a=v_w_br_a, w_br_b=v_w_br_b, w_br_c=v_w_br_c, w_out=v_w_out, attn_post_norm=v_attn_post_norm,
             ffn_pre_norm=v_ffn_pre_norm, w_up=v_w_up, conv_w=v_conv_w, conv_b=v_conv_b, w_down=v_w_down,
             ffn_post_norm=v_ffn_post_norm)
    xi, yi, ci = _coords()
    me = 4 * xi + 2 * yi + ci

    me_arr = me.astype(jnp.int32).reshape(1)

    small_w = _pack([b_gate.reshape(-1), conv_w.reshape(-1)])
    (small_w_all,) = _exchange([small_w], [jax.ShapeDtypeStruct((NDEV,) + small_w.shape, F32)],
                               _whole, _slot, "gather_small_weights")

    lands = [_cast_own(l * NBIG + t, P[nm][l], me_arr, f"gather_own_l{l}_{nm}")
             for l in range(DEPTH) for t, (nm, _, _) in enumerate(BIG)]
    groups = [tuple(l * NBIG + t for t in tids) for l in range(DEPTH) for _, tids in LAYER_GROUPS]
    g_sems, _, g_lands, g_tok = _xchg_start(None, lands, groups, None, _shard_window, small_w_all, "gather_start")
    tok0 = g_tok[0:1, 0:1]

    def gather_waiter(gi, l, gname, tids):
        def wait(after):
            ids = [l * NBIG + t for t in tids]
            _, got = _xchg_wait(g_sems[gi], None, [g_lands[i] for i in ids], ids, after,
                                None, _shard_window, f"gather_wait_l{l}_{gname}")
            out = {}
            for t, arr in zip(tids, got):
                nm = BIG[t][0]
                out[nm] = jnp.transpose(arr, (1, 0, 2)).reshape(D, IN_COLS) if nm == "w_in" else arr
            return out
        return wait

    pending = [{gname: gather_waiter(l * len(LAYER_GROUPS) + k, l, gname, tids)
                for k, (gname, tids) in enumerate(LAYER_GROUPS)} for l in range(DEPTH)]
    nbg = DEPTH * 3 * 128
    ncw = DEPTH * 3 * 1024
    flat_all = small_w_all.reshape(NDEV, -1)
    b_gate_full = jnp.transpose(flat_all[:, :nbg].reshape(NDEV, DEPTH, 3, 128), (1, 2, 0, 3)).reshape(DEPTH, 3, D)
    conv_w_full = jnp.transpose(flat_all[:, nbg:nbg + ncw].reshape(NDEV, DEPTH, 3, 1024), (1, 2, 0, 3)).reshape(DEPTH, 3, 2 * D_FF)

    ws = []
    for l in range(DEPTH):
        ws.append(_Weights(dict(
            b_gate=b_gate_full[l], conv_w=conv_w_full[l].reshape(3, 2, D_FF), conv_b=conv_b[l].reshape(2, D_FF),
            sinks=sinks[l].reshape(1, 8),
            attn_pre_norm=attn_pre_norm[l].reshape(1, D), attn_post_norm=attn_post_norm[l].reshape(1, D),
            ffn_pre_norm=ffn_pre_norm[l].reshape(1, D), ffn_post_norm=ffn_post_norm[l].reshape(1, D)), pending[l]))

    rs = {}

    group_tids = dict(LAYER_GROUPS)

    def start_scatter(l, gname, grads_l):
        tids = group_tids[gname]
        blocks, lands_rs = [], []
        for t in tids:
            nm, ax, ext = BIG[t]
            gfull = grads_l[nm].astype(BF16)
            if nm == "w_in":
                gfull = jnp.transpose(gfull.reshape(D, NDEV, ext), (1, 0, 2))
                shp = (NDEV, D, ext)
            else:
                shp = (NDEV, ext, gfull.shape[1]) if ax == 0 else (NDEV, gfull.shape[0], ext)
            blocks.append(gfull)
            lands_rs.append(lax.empty(shp, BF16))
        local = list(range(len(tids)))
        win = lambda j, ref, k: _shard_window(tids[j], ref, k)
        sems, s_thru, l_thru, tok = _xchg_start(blocks, lands_rs, [tuple(local)], win, _slot, me_arr,
                                                f"scatter_start_l{l}_{gname}")
        rs[(l, gname)] = (sems[0], s_thru, l_thru, win, local)
        return tok[0:1, 0:1]

    loss_local, grad_x, grads, g_rel = _local_step(x[0], loss_target[0], ws, rel_bias, tok0, start_scatter)
    loss = lax.psum(loss_local, ("x", "y", "c"))

    stack = lambda nm: jnp.stack([grads[l][nm] for l in range(DEPTH)], axis=0)
    small_names = [nm for nm, _ in SMALL_REPL] + [nm for nm, _, _ in SMALL_SHARD]
    small_g = {"rel_bias": g_rel}
    for nm in small_names[1:]:
        small_g[nm] = stack(nm)
    small_packed = _pack([small_g[nm] for nm in small_names])
    (small_parts,) = _exchange([small_packed], [jax.ShapeDtypeStruct((NDEV,) + small_packed.shape, F32)],
                               _whole, _slot, "gather_small_grads")
    small_tot = _small_sum(small_parts, "small_grad_sum")
    sizes = [sz for _, sz in SMALL_REPL] + [int(np.prod(shp)) for _, shp, _ in SMALL_SHARD]
    small_tot = dict(zip(small_names, _unpack(small_tot, sizes)))

    out_g, out_d, out_m, out_v = {}, {}, {}, {}
    prev = {nm: None for nm, _, _ in BIG}
    for l in reversed(range(DEPTH)):
        for gname in ("ffn", "mix", "in"):
            sems, s_thru, l_thru, win, local = rs[(l, gname)]
            owns, parts = _xchg_wait(sems, s_thru, l_thru, local, small_parts, win, _slot, f"scatter_wait_l{l}_{gname}")
            for t, own, prt in zip(group_tids[gname], owns, parts):
                nm = BIG[t][0]
                rows = {"w_in": 256, "w_up": 256, "w_down": 256}.get(nm, P[nm].shape[1])
                prev[nm] = _adamw(t, prt, own, me_arr, P[nm], M[nm], V[nm], l, prev[nm], rows, f"adamw_{nm}_l{l}")
    for nm, _, _ in BIG:
        out_g[nm], out_d[nm], out_m[nm], out_v[nm] = prev[nm]
    gsm = {}
    for nm, _ in SMALL_REPL:
        gsm[nm] = small_tot[nm].reshape(P[nm].shape)
    for nm, shp, ext in SMALL_SHARD:
        gsm[nm] = lax.dynamic_slice_in_dim(small_tot[nm].reshape(shp), me * ext, ext, axis=2)
    pk = lambda dct: _pack([dct[nm] for nm in small_names])
    d_s, m_s, v_s = _small_adamw(pk(gsm), pk(P), pk(M), pk(V), "adamw_small")
    szs = [int(np.prod(P[nm].shape)) for nm in small_names]
    for dst, packed in ((out_d, d_s), (out_m, m_s), (out_v, v_s)):
        for nm, piece in zip(small_names, _unpack(packed, szs)):
            dst[nm] = piece.reshape(P[nm].shape)
    for nm in small_names:
        out_g[nm] = gsm[nm]

    order = ["rel_bias", "attn_pre_norm", "w_in", "b_gate", "sinks", "w_br_a", "w_br_b", "w_br_c", "w_out",
             "attn_post_norm", "ffn_pre_norm", "w_up", "conv_w", "conv_b", "w_down", "ffn_post_norm"]
    return (loss, grad_x[None], *[out_g[k] for k in order], *[out_d[k] for k in order],
            *[out_m[k] for k in order], *[out_v[k] for k in order])
```

```python
import functools
import math

import numpy as np
import jax
import jax.numpy as jnp
from jax import lax
from jax.experimental import pallas as pl
from jax.experimental.pallas import tpu as pltpu

F32 = jnp.float32
BF16 = jnp.bfloat16

S = 2048
D = 1024
DEPTH = 2
NDEV = 8
HD = 64
BLK = 128
NB = S // BLK
A_GROUPS = ((128, 1), (512, 4), (2048, 16))
NUM_BUCKETS = 32
MAX_DISTANCE = 2048
N_BIAS_HEADS = 20
D_FF = 4096
IN_COLS = 6912
QKV_COLS = 3840
QKV_SLABS = QKV_COLS // 128
GATE_COLS = 3072
EPS = 1e-6
SCALE = HD ** -0.5
NEG = -1e30
LANES = 128

ADAM_LR = 0.001
ADAM_B1 = 0.9
ADAM_B2 = 0.999
ADAM_EPS = 1e-08
ADAM_WD = 0.01
ADAM_STEP = 10

VMEM_LIMIT = 56 * 1024 * 1024
MESH = pl.DeviceIdType.MESH
ANY = pl.BlockSpec(memory_space=pl.ANY)
SMEM = pl.BlockSpec(memory_space=pltpu.SMEM)


def _cp(*sem):
    return pltpu.CompilerParams(dimension_semantics=sem if sem else None, vmem_limit_bytes=VMEM_LIMIT)


def _dot(a, b, ca, cb):
    return lax.dot_general(a, b, (((ca,), (cb,)), ((), ())), preferred_element_type=F32)


def _mm(a, b, *, grid, a_spec, b_spec, out_shape, out_spec, ca, cb, acc_shape, name,
        a_slab=False, b_slab=False, out_slab=False, alias_out=None):
    nk = grid[2]

    def body(*refs):
        if alias_out is not None:
            a_ref, b_ref, _, o_ref, acc_ref = refs
        else:
            a_ref, b_ref, o_ref, acc_ref = refs
        k = pl.program_id(2)

        @pl.when(k == 0)
        def _():
            acc_ref[...] = jnp.zeros(acc_shape, F32)

        def load(ref, slab):
            if slab:
                return jnp.concatenate([ref[s] for s in range(ref.shape[0])], axis=1).astype(BF16)
            return ref[...].astype(BF16)

        acc_ref[...] += _dot(load(a_ref, a_slab), load(b_ref, b_slab), ca, cb)

        @pl.when(k == nk - 1)
        def _():
            if out_slab:
                for s in range(o_ref.shape[0]):
                    o_ref[s] = acc_ref[:, s * LANES:(s + 1) * LANES].astype(o_ref.dtype)
            else:
                o_ref[...] = acc_ref[...].astype(o_ref.dtype)

    in_specs = [a_spec, b_spec]
    args = [a, b]
    aliases = {}
    if alias_out is not None:
        in_specs.append(ANY)
        args.append(alias_out)
        aliases = {2: 0}
    return pl.pallas_call(
        body, grid=grid, in_specs=in_specs, out_specs=out_spec, out_shape=out_shape,
        scratch_shapes=[pltpu.VMEM(acc_shape, F32)], input_output_aliases=aliases,
        compiler_params=_cp("parallel", "parallel", "arbitrary"), name=name)(*args)


def _mm_nn(a, b, out_dtype, tm, tn, tk, name):
    m, kk = a.shape
    n = b.shape[1]
    return _mm(a, b, grid=(m // tm, n // tn, kk // tk),
               a_spec=pl.BlockSpec((tm, tk), lambda i, j, k: (i, k)),
               b_spec=pl.BlockSpec((tk, tn), lambda i, j, k: (k, j)),
               out_shape=jax.ShapeDtypeStruct((m, n), out_dtype),
               out_spec=pl.BlockSpec((tm, tn), lambda i, j, k: (i, j)),
               ca=1, cb=0, acc_shape=(tm, tn), name=name)


def _mm_nt(a, b, out_dtype, tm, tn, tk, name):
    m, kk = a.shape
    n = b.shape[0]
    return _mm(a, b, grid=(m // tm, n // tn, kk // tk),
               a_spec=pl.BlockSpec((tm, tk), lambda i, j, k: (i, k)),
               b_spec=pl.BlockSpec((tn, tk), lambda i, j, k: (j, k)),
               out_shape=jax.ShapeDtypeStruct((m, n), out_dtype),
               out_spec=pl.BlockSpec((tm, tn), lambda i, j, k: (i, j)),
               ca=1, cb=1, acc_shape=(tm, tn), name=name)


def _mm_tn(a, b, out_dtype, tm, tn, tk, name):
    kk, m = a.shape
    n = b.shape[1]
    return _mm(a, b, grid=(m // tm, n // tn, kk // tk),
               a_spec=pl.BlockSpec((tk, tm), lambda i, j, k: (k, i)),
               b_spec=pl.BlockSpec((tk, tn), lambda i, j, k: (k, j)),
               out_shape=jax.ShapeDtypeStruct((m, n), out_dtype),
               out_spec=pl.BlockSpec((tm, tn), lambda i, j, k: (i, j)),
               ca=0, cb=0, acc_shape=(tm, tn), name=name)


ROW_TILE = 256


def _rms(x, g):
    r = lax.rsqrt(jnp.mean(x * x, axis=-1, keepdims=True) + EPS)
    return x * r * g


def _prenorm(x, g, name):
    def body(x_ref, g_ref, o_ref):
        o_ref[...] = _rms(x_ref[...], g_ref[...]).astype(BF16)

    return pl.pallas_call(
        body, grid=(S // ROW_TILE,),
        in_specs=[pl.BlockSpec((ROW_TILE, D), lambda i: (i, 0)), pl.BlockSpec((1, D), lambda i: (0, 0))],
        out_specs=pl.BlockSpec((ROW_TILE, D), lambda i: (i, 0)),
        out_shape=jax.ShapeDtypeStruct((S, D), BF16), compiler_params=_cp("parallel"), name=name)(x, g)


def _postnorm_res(x, f, g_post, g_next, name):
    def body(x_ref, f_ref, gp_ref, gn_ref, xo_ref, ho_ref):
        xn = x_ref[...] + _rms(f_ref[...], gp_ref[...])
        xo_ref[...] = xn
        ho_ref[...] = _rms(xn, gn_ref[...]).astype(BF16)

    row = pl.BlockSpec((ROW_TILE, D), lambda i: (i, 0))
    vec = pl.BlockSpec((1, D), lambda i: (0, 0))
    return pl.pallas_call(
        body, grid=(S // ROW_TILE,), in_specs=[row, row, vec, vec], out_specs=[row, row],
        out_shape=[jax.ShapeDtypeStruct((S, D), F32), jax.ShapeDtypeStruct((S, D), BF16)],
        compiler_params=_cp("parallel"), name=name)(x, f, g_post, g_next)


def _norm_bwd(f, g, dys, res, out_dtype, name):
    ndy = len(dys)
    has_res = res is not None

    def body(*refs):
        f_ref, g_ref = refs[0], refs[1]
        dy_refs = refs[2:2 + ndy]
        res_ref = refs[2 + ndy] if has_res else None
        o_ref, dg_ref = refs[-2], refs[-1]
        fv = f_ref[...]
        dy = dy_refs[0][...].astype(F32)
        for r in dy_refs[1:]:
            dy = dy + r[...].astype(F32)
        r = lax.rsqrt(jnp.mean(fv * fv, axis=-1, keepdims=True) + EPS)
        n = fv * r
        dn = dy * g_ref[...]
        df = r * (dn - n * jnp.mean(dn * n, axis=-1, keepdims=True))
        if has_res:
            df = df + res_ref[...]
        o_ref[...] = df.astype(out_dtype)

        @pl.when(pl.program_id(0) == 0)
        def _():
            dg_ref[...] = jnp.zeros((1, D), F32)

        dg_ref[...] += jnp.sum(dy * n, axis=0, keepdims=True)

    row = pl.BlockSpec((ROW_TILE, D), lambda i: (i, 0))
    vec = pl.BlockSpec((1, D), lambda i: (0, 0))
    in_specs = [row, vec] + [row] * ndy + ([row] if has_res else [])
    args = [f, g] + list(dys) + ([res] if has_res else [])
    return pl.pallas_call(
        body, grid=(S // ROW_TILE,), in_specs=in_specs, out_specs=[row, vec],
        out_shape=[jax.ShapeDtypeStruct((S, D), out_dtype), jax.ShapeDtypeStruct((1, D), F32)],
        compiler_params=_cp("arbitrary"), name=name)(*args)


def _loss_head(y, target, name):
    def body(y_ref, t_ref, dy_ref, l_ref):
        e = y_ref[...] - t_ref[...]
        dy_ref[...] = e * (1.0 / D)

        @pl.when(pl.program_id(0) == 0)
        def _():
            l_ref[...] = jnp.zeros((8, LANES), F32)

        l_ref[...] += jnp.sum(e * e) * (0.5 / D)

    row = pl.BlockSpec((ROW_TILE, D), lambda i: (i, 0))
    return pl.pallas_call(
        body, grid=(S // ROW_TILE,), in_specs=[row, row],
        out_specs=[row, pl.BlockSpec((8, LANES), lambda i: (0, 0))],
        out_shape=[jax.ShapeDtypeStruct((S, D), F32), jax.ShapeDtypeStruct((8, LANES), F32)],
        compiler_params=_cp("arbitrary"), name=name)(y, target)


def _bucket_tiles():
    a = np.arange(BLK)[:, None]
    b = np.arange(2 * BLK)[None, :]
    dist = a + BLK - b
    out = np.zeros((4, 2, BLK, 2 * BLK), np.int32)
    cfg = [(w // d, d) for w, d in A_GROUPS] + [(BLK - 1, 1)]
    for gi, (max_dist, d) in enumerate(cfg):
        band = (dist >= 0) & (dist <= max_dist)
        tok = np.maximum(dist, 0) * d
        nf = np.maximum(tok, 1).astype(np.float32)
        max_exact = NUM_BUCKETS // 2
        large = max_exact + (np.log(nf / np.float32(max_exact)) / np.float32(math.log(MAX_DISTANCE / max_exact))
                             * np.float32(NUM_BUCKETS - max_exact)).astype(np.int32)
        large = np.minimum(large, NUM_BUCKETS - 1)
        bkt = np.where(tok < max_exact, tok, large).astype(np.int32)
        full = np.where(band, bkt, -1)
        out[gi, 1] = full
        out[gi, 0] = np.where(b >= BLK, full, -1)
    return out


def _bias_tiles(rel_bias, buckets, name):
    def body(tab_ref, bkt_ref, o_ref):
        h = pl.program_id(0)
        bkt = bkt_ref[...]
        acc = jnp.zeros(bkt.shape, F32)
        for bb in range(NUM_BUCKETS):
            acc = jnp.where(bkt == bb, tab_ref[bb, h], acc)
        o_ref[...] = jnp.where(bkt < 0, NEG, acc)

    return pl.pallas_call(
        body, grid=(N_BIAS_HEADS,),
        in_specs=[SMEM, pl.BlockSpec((None, 2, BLK, 2 * BLK), lambda h: (jnp.minimum(h // 4, 3), 0, 0, 0))],
        out_specs=pl.BlockSpec((None, 2, BLK, 2 * BLK), lambda h: (h, 0, 0, 0)),
        out_shape=jax.ShapeDtypeStruct((N_BIAS_HEADS, 2, BLK, 2 * BLK), F32),
        compiler_params=_cp("arbitrary"), name=name)(rel_bias, buckets)


def _bias_grad(gs, buckets, name):
    ng = len(gs)

    def body(*refs):
        g_refs = refs[:ng]
        bkt_ref, o_ref = refs[ng], refs[ng + 1]
        h = pl.program_id(0)
        g = g_refs[0][...]
        for r in g_refs[1:]:
            g = g + r[...]
        bkt = bkt_ref[...]
        row = lax.broadcasted_iota(jnp.int32, (NUM_BUCKETS, LANES), 0)
        lane = lax.broadcasted_iota(jnp.int32, (NUM_BUCKETS, LANES), 1)

        @pl.when(h == 0)
        def _():
            o_ref[...] = jnp.zeros((NUM_BUCKETS, LANES), F32)

        acc = o_ref[...]
        for bb in range(NUM_BUCKETS):
            s = jnp.sum(jnp.where(bkt == bb, g, 0.0))
            acc = jnp.where((row == bb) & (lane == h), s, acc)
        o_ref[...] = acc

    g_spec = pl.BlockSpec((None, BLK, 2 * BLK), lambda h: (h, 0, 0))
    return pl.pallas_call(
        body, grid=(N_BIAS_HEADS,),
        in_specs=[g_spec] * ng + [pl.BlockSpec((None, None, BLK, 2 * BLK), lambda h: (jnp.minimum(h // 4, 3), 1, 0, 0))],
        out_specs=pl.BlockSpec((NUM_BUCKETS, LANES), lambda h: (0, 0)),
        out_shape=jax.ShapeDtypeStruct((NUM_BUCKETS, LANES), F32),
        compiler_params=_cp("arbitrary"), name=name)(*gs, buckets)


def _to_class_major(src_ref, dst_ref, d, scale=None, dtype=None):
    ln = S // d
    for r in range(d):
        v = src_ref[pl.ds(r, ln, stride=d), :] if d > 1 else src_ref[...]
        if scale is not None:
            v = v * scale
        dst_ref[pl.ds(r * ln, ln), :] = v.astype(dtype or dst_ref.dtype)


def _block_rows(b, d):
    nbc = NB // d
    i = b % nbc
    r = b // nbc
    has_prev = (i > 0).astype(jnp.int32)
    prev = pl.multiple_of(jnp.maximum(b - 1, 0) * BLK, BLK)
    nat = i * (BLK * d) + r
    return has_prev, prev, nat


def _lane_halves(v0, v1):
    lane = lax.broadcasted_iota(jnp.int32, (v0.shape[0], LANES), 1)
    return jnp.where(lane < HD, v0, v1)


def _band_fwd(proj, bias, *, d, q0, k0, v0, npairs, bias0, shared_kv, name):
    def body(q_ref, k_ref, v_ref, b_ref, num_ref, st_ref, qs, ks, vs):
        p = pl.program_id(0)
        _to_class_major(q_ref, qs, d, scale=SCALE)
        _to_class_major(k_ref, ks, d)
        _to_class_major(v_ref, vs, d)
        lane = lax.broadcasted_iota(jnp.int32, (BLK, LANES), 1)

        def blk(b, carry):
            has_prev, prev, nat = _block_rows(b, d)
            cur = pl.multiple_of(b * BLK, BLK)
            qb = qs[pl.ds(cur, BLK), :]
            k2 = jnp.concatenate([ks[pl.ds(prev, BLK), :], ks[pl.ds(cur, BLK), :]], axis=0)
            v2 = jnp.concatenate([vs[pl.ds(prev, BLK), :], vs[pl.ds(cur, BLK), :]], axis=0)
            nums, ms, ls = [], [], []
            for hh in range(2):
                qh = qb[:, hh * HD:(hh + 1) * HD]
                if shared_kv:
                    kh = jnp.where(p >= 2, k2[:, HD:], k2[:, :HD])
                    vh = jnp.where(p >= 2, v2[:, HD:], v2[:, :HD])
                else:
                    kh = k2[:, hh * HD:(hh + 1) * HD]
                    vh = v2[:, hh * HD:(hh + 1) * HD]
                z = _dot(qh, kh, 1, 1) + b_ref[hh, has_prev]
                m = jnp.max(z, axis=1, keepdims=True)
                e = jnp.exp(z - m)
                ls.append(jnp.sum(e, axis=1, keepdims=True))
                ms.append(m)
                nums.append(_dot(e.astype(BF16), vh, 1, 0))
            num_t = jnp.concatenate(nums, axis=1)
            st_t = jnp.where(lane < 32, ms[0], jnp.where(lane < 64, ls[0], jnp.where(lane < 96, ms[1], ls[1])))
            if d > 1:
                num_ref[pl.ds(nat, BLK, stride=d), :] = num_t
                st_ref[pl.ds(nat, BLK, stride=d), :] = st_t
            else:
                num_ref[pl.ds(cur, BLK), :] = num_t
                st_ref[pl.ds(cur, BLK), :] = st_t
            return carry

        lax.fori_loop(0, NB, blk, 0, unroll=4)

    slab = lambda off, per_pair: pl.BlockSpec((None, S, LANES), (lambda p: (off + p, 0, 0)) if per_pair else (lambda p: (off, 0, 0)))
    out = pl.BlockSpec((None, S, LANES), lambda p: (p, 0, 0))
    return pl.pallas_call(
        body, grid=(npairs,),
        in_specs=[slab(q0, True), slab(k0, not shared_kv), slab(v0, not shared_kv),
                  pl.BlockSpec((None, 2, 2, BLK, 2 * BLK), lambda p: (bias0 + p, 0, 0, 0, 0))],
        out_specs=[out, out],
        out_shape=[jax.ShapeDtypeStruct((npairs, S, LANES), F32)] * 2,
        scratch_shapes=[pltpu.VMEM((S, LANES), BF16)] * 3,
        compiler_params=_cp("arbitrary"), name=name)(proj, proj, proj, bias)


def _combine_a(nums, stats, name):
    rt = 512

    def body(n0, n1, n2, s0, s1, s2, o_ref, l_ref):
        n_refs, s_refs = (n0, n1, n2), (s0, s1, s2)
        outs, lses = [], []
        for hh in range(2):
            ms = [s[:, 64 * hh:64 * hh + 1] for s in s_refs]
            ls = [s[:, 64 * hh + 32:64 * hh + 33] for s in s_refs]
            mx = jnp.maximum(jnp.maximum(ms[0], ms[1]), ms[2])
            cs = [jnp.exp(m - mx) for m in ms]
            z = cs[0] * ls[0] + cs[1] * ls[1] + cs[2] * ls[2]
            acc = cs[0] * n_refs[0][:, hh * HD:(hh + 1) * HD]
            acc = acc + cs[1] * n_refs[1][:, hh * HD:(hh + 1) * HD]
            acc = acc + cs[2] * n_refs[2][:, hh * HD:(hh + 1) * HD]
            outs.append(acc / z)
            lses.append(mx + jnp.log(z))
        o_ref[...] = jnp.concatenate(outs, axis=1)
        l_ref[...] = _lane_halves(lses[0], lses[1])

    spec = pl.BlockSpec((None, rt, LANES), lambda p, i: (p, i, 0))
    return pl.pallas_call(
        body, grid=(2, S // rt), in_specs=[spec] * 6, out_specs=[spec, spec],
        out_shape=[jax.ShapeDtypeStruct((2, S, LANES), F32)] * 2,
        compiler_params=_cp("parallel", "parallel"), name=name)(*nums, *stats)


def _combine_b(num, stats, sinks, name):
    rt = 512

    def body(sink_ref, n_ref, s_ref, o_ref, l_ref):
        p = pl.program_id(0)
        outs, lses = [], []
        for hh in range(2):
            sink = sink_ref[0, 2 * p + hh]
            m = s_ref[:, 64 * hh:64 * hh + 1]
            l = s_ref[:, 64 * hh + 32:64 * hh + 33]
            mx = jnp.maximum(m, sink)
            c = jnp.exp(m - mx)
            z = l * c + jnp.exp(sink - mx)
            outs.append(n_ref[:, hh * HD:(hh + 1) * HD] * (c / z))
            lses.append(mx + jnp.log(z))
        o_ref[...] = jnp.concatenate(outs, axis=1)
        l_ref[...] = _lane_halves(lses[0], lses[1])

    spec = pl.BlockSpec((None, rt, LANES), lambda p, i: (p, i, 0))
    return pl.pallas_call(
        body, grid=(4, S // rt), in_specs=[SMEM, spec, spec], out_specs=[spec, spec],
        out_shape=[jax.ShapeDtypeStruct((4, S, LANES), F32)] * 2,
        compiler_params=_cp("parallel", "parallel"), name=name)(sinks, num, stats)


def _band_bwd(proj, bias, o, do, lse, sinks, *, d, q0, k0, v0, npairs, bias0, shared_kv, name):
    nkv = 1 if shared_kv else npairs

    def body(sink_ref, q_ref, k_ref, v_ref, b_ref, o_ref, do_ref, lse_ref,
             dq_ref, dk_ref, dv_ref, g_ref, ds_ref,
             qs, ks, vs, dos, lses, dls, dl_nat, dq_nat, dk_cm, dv_cm, kv_nat):
        p = pl.program_id(0)
        lane = lax.broadcasted_iota(jnp.int32, (S, LANES), 1)
        dov = do_ref[...]
        prod = dov * o_ref[...]
        dl0 = jnp.sum(jnp.where(lane < HD, prod, 0.0), axis=1, keepdims=True)
        dl1 = jnp.sum(jnp.where(lane >= HD, prod, 0.0), axis=1, keepdims=True)
        dl_nat[...] = jnp.where(lane < HD, dl0, dl1)
        if shared_kv:
            row8 = lax.broadcasted_iota(jnp.int32, (8, LANES), 0)
            lane8 = lax.broadcasted_iota(jnp.int32, (8, LANES), 1)
            t = jnp.zeros((8, LANES), F32)
            lv = lse_ref[...]
            for hh in range(2):
                sink = sink_ref[0, 2 * p + hh]
                ps = jnp.exp(sink - lv[:, 64 * hh:64 * hh + 1])
                dsink = -jnp.sum(ps * (dl0 if hh == 0 else dl1))
                t = jnp.where((row8 == 0) & (lane8 == hh), dsink, t)
            ds_ref[...] = t
        else:
            ds_ref[...] = jnp.zeros((8, LANES), F32)
        _to_class_major(q_ref, qs, d, scale=SCALE)
        _to_class_major(k_ref, ks, d)
        _to_class_major(v_ref, vs, d)
        _to_class_major(do_ref, dos, d)
        _to_class_major(lse_ref, lses, d)
        _to_class_major(dl_nat, dls, d)

        def zero_kv():
            dk_cm[...] = jnp.zeros((S, LANES), F32)
            dv_cm[...] = jnp.zeros((S, LANES), F32)

        if shared_kv:
            pl.when(p == 0)(zero_kv)
        else:
            zero_kv()

        g_ref[...] = jnp.zeros((2, BLK, 2 * BLK), F32)
        lane2 = lax.broadcasted_iota(jnp.int32, (2 * BLK, LANES), 1)

        def blk(b, carry):
            has_prev, prev, nat = _block_rows(b, d)
            cur = pl.multiple_of(b * BLK, BLK)
            qb = qs[pl.ds(cur, BLK), :]
            dob = dos[pl.ds(cur, BLK), :]
            lb = lses[pl.ds(cur, BLK), :]
            dlb = dls[pl.ds(cur, BLK), :]
            k2 = jnp.concatenate([ks[pl.ds(prev, BLK), :], ks[pl.ds(cur, BLK), :]], axis=0)
            v2 = jnp.concatenate([vs[pl.ds(prev, BLK), :], vs[pl.ds(cur, BLK), :]], axis=0)
            dqs, dks, dvs = [], [], []
            for hh in range(2):
                qh = qb[:, hh * HD:(hh + 1) * HD]
                doh = dob[:, hh * HD:(hh + 1) * HD]
                if shared_kv:
                    kh = jnp.where(p >= 2, k2[:, HD:], k2[:, :HD])
                    vh = jnp.where(p >= 2, v2[:, HD:], v2[:, :HD])
                else:
                    kh = k2[:, hh * HD:(hh + 1) * HD]
                    vh = v2[:, hh * HD:(hh + 1) * HD]
                z = _dot(qh, kh, 1, 1) + b_ref[hh, has_prev]
                pr = jnp.exp(z - lb[:, 64 * hh:64 * hh + 1])
                dp = _dot(doh, vh, 1, 1)
                dz = pr * (dp - dlb[:, 64 * hh:64 * hh + 1])
                g_ref[hh] += dz
                dzb = dz.astype(BF16)
                dqs.append(_dot(dzb, kh, 1, 0) * SCALE)
                dks.append(_dot(dzb, qh, 0, 0))
                dvs.append(_dot(pr.astype(BF16), doh, 0, 0))
            dq_t = jnp.concatenate(dqs, axis=1)
            if shared_kv:
                dk_t = jnp.concatenate([dks[0] + dks[1]] * 2, axis=1)
                dv_t = jnp.concatenate([dvs[0] + dvs[1]] * 2, axis=1)
                mine = (lane2 >= HD) == (p >= 2)
                dk_t = jnp.where(mine, dk_t, 0.0)
                dv_t = jnp.where(mine, dv_t, 0.0)
            else:
                dk_t = jnp.concatenate(dks, axis=1)
                dv_t = jnp.concatenate(dvs, axis=1)
            dk_cm[pl.ds(prev, BLK), :] += dk_t[:BLK]
            dk_cm[pl.ds(cur, BLK), :] += dk_t[BLK:]
            dv_cm[pl.ds(prev, BLK), :] += dv_t[:BLK]
            dv_cm[pl.ds(cur, BLK), :] += dv_t[BLK:]
            if d > 1:
                dq_nat[pl.ds(nat, BLK, stride=d), :] = dq_t
            else:
                dq_nat[pl.ds(cur, BLK), :] = dq_t
            return carry

        lax.fori_loop(0, NB, blk, 0, unroll=4)
        dq_ref[...] = dq_nat[...].astype(BF16)

        def from_class_major(src, dst_ref):
            if d == 1:
                dst_ref[...] = src[...].astype(BF16)
            else:
                ln = S // d
                for r in range(d):
                    kv_nat[pl.ds(r, ln, stride=d), :] = src[pl.ds(r * ln, ln), :]
                dst_ref[...] = kv_nat[...].astype(BF16)

        def write_kv():
            from_class_major(dk_cm, dk_ref)
            from_class_major(dv_cm, dv_ref)

        if shared_kv:
            pl.when(p == npairs - 1)(write_kv)
        else:
            write_kv()

    slab = lambda off, per_pair: pl.BlockSpec((None, S, LANES), (lambda p: (off + p, 0, 0)) if per_pair else (lambda p: (off, 0, 0)))
    pair = pl.BlockSpec((None, S, LANES), lambda p: (p, 0, 0))
    kv_out = pair if not shared_kv else pl.BlockSpec((None, S, LANES), lambda p: (0, 0, 0))
    return pl.pallas_call(
        body, grid=(npairs,),
        in_specs=[SMEM, slab(q0, True), slab(k0, not shared_kv), slab(v0, not shared_kv),
                  pl.BlockSpec((None, 2, 2, BLK, 2 * BLK), lambda p: (bias0 + p, 0, 0, 0, 0)),
                  pair, pair, pair],
        out_specs=[pair, kv_out, kv_out,
                   pl.BlockSpec((None, 2, BLK, 2 * BLK), lambda p: (p, 0, 0, 0)),
                   pl.BlockSpec((None, 8, LANES), lambda p: (p, 0, 0))],
        out_shape=[jax.ShapeDtypeStruct((npairs, S, LANES), BF16),
                   jax.ShapeDtypeStruct((nkv, S, LANES), BF16),
                   jax.ShapeDtypeStruct((nkv, S, LANES), BF16),
                   jax.ShapeDtypeStruct((npairs, 2, BLK, 2 * BLK), F32),
                   jax.ShapeDtypeStruct((npairs, 8, LANES), F32)],
        scratch_shapes=[pltpu.VMEM((S, LANES), BF16)] * 4 + [pltpu.VMEM((S, LANES), F32)] * 7,
        compiler_params=_cp("arbitrary"), name=name)(sinks, proj, proj, proj, bias, o, do, lse)


KC = 512
NSUB = KC // BLK


def _split2(x):
    hi = x.astype(BF16)
    lo = (x - hi.astype(F32)).astype(BF16)
    return hi, lo


def _tri_ones(cmp):
    jj = lax.broadcasted_iota(jnp.int32, (2 * BLK, BLK), 0) % BLK
    ss = lax.broadcasted_iota(jnp.int32, (2 * BLK, BLK), 1)
    return jnp.concatenate([cmp(jj, ss).astype(BF16), jnp.ones((2 * BLK, BLK), BF16)], axis=1)


def _sub_sums(x, tri1):
    st = jnp.concatenate([x[:, s * BLK:(s + 1) * BLK] for s in range(NSUB)], axis=0)
    hi, lo = _split2(st)
    r = _dot(jnp.concatenate([hi, lo], axis=1), tri1, 1, 0)
    return ([r[s * BLK:(s + 1) * BLK, :BLK] for s in range(NSUB)], [r[s * BLK:(s + 1) * BLK, BLK:] for s in range(NSUB)])


def _log_sig_pair(z):
    lb = jnp.minimum(z, 0.0) - jnp.log1p(jnp.exp(-jnp.abs(z)))
    return lb, lb - z


QGROUPS = NB // NSUB


def _stick_fwd(proj, *, q0, k0, v0, name):
    def body(q_ref, k_ref, v_ref, o_ref, t_ref, qs, ks, vs):
        qs[...] = (q_ref[...] * SCALE).astype(BF16)
        ks[...] = k_ref[...].astype(BF16)
        vs[...] = v_ref[...].astype(BF16)
        tri1 = _tri_ones(lambda j, s: j > s)
        col = lax.broadcasted_iota(jnp.int32, (BLK, KC), 1)
        rowi = lax.broadcasted_iota(jnp.int32, (BLK, KC), 0)

        for qg in range(QGROUPS):
            def qblock(ii, carry0, qg=qg):
                t0 = pl.multiple_of((qg * NSUB + ii) * BLK, BLK)
                qb = qs[pl.ds(t0, BLK), :]
                accs = [jnp.zeros((BLK, HD), F32)] * 2
                runs = [jnp.zeros((BLK, BLK), F32)] * 2
                for c in reversed(range(qg + 1)):
                    s0 = c * KC
                    diag = c == qg
                    before = (s0 + col) < (t0 + rowi) if diag else None
                    for hh in range(2):
                        kh = ks[s0:s0 + KC, hh * HD:(hh + 1) * HD]
                        vh = vs[s0:s0 + KC, hh * HD:(hh + 1) * HD]
                        lb, lk = _log_sig_pair(_dot(qb[:, hh * HD:(hh + 1) * HD], kh, 1, 1))
                        if diag:
                            lk = jnp.where(before, lk, 0.0)
                        suf, tot = _sub_sums(lk, tri1)
                        ws, run = [], runs[hh]
                        for s in reversed(range(NSUB)):
                            ws.append(jnp.exp(lb[:, s * BLK:(s + 1) * BLK] + suf[s] + run))
                            run = run + tot[s]
                        w = jnp.concatenate(ws[::-1], axis=1)
                        if diag:
                            w = jnp.where(before, w, 0.0)
                        accs[hh] = accs[hh] + _dot(w.astype(BF16), vh, 1, 0)
                        runs[hh] = run
                o_ref[pl.ds(t0, BLK), :] = jnp.concatenate(accs, axis=1)
                t_ref[pl.ds(t0, BLK), :] = _lane_halves(runs[0], runs[1])
                return carry0

            lax.fori_loop(0, NSUB, qblock, 0)

    slab = lambda off: pl.BlockSpec((None, S, LANES), lambda p: (off + p, 0, 0))
    out = pl.BlockSpec((None, S, LANES), lambda p: (p, 0, 0))
    return pl.pallas_call(
        body, grid=(2,), in_specs=[slab(q0), slab(k0), slab(v0)], out_specs=[out, out],
        out_shape=[jax.ShapeDtypeStruct((2, S, LANES), F32)] * 2,
        scratch_shapes=[pltpu.VMEM((S, LANES), BF16)] * 3,
        compiler_params=_cp("arbitrary"), name=name)(proj, proj, proj)


def _stick_bwd(proj, do, tot, *, q0, k0, v0, name):
    def body(q_ref, k_ref, v_ref, do_ref, t_ref, dq_ref, dk_ref, dv_ref, qs, ks, vs, dos, dk_acc, dv_acc):
        qs[...] = (q_ref[...] * SCALE).astype(BF16)
        ks[...] = k_ref[...].astype(BF16)
        vs[...] = v_ref[...].astype(BF16)
        dos[...] = do_ref[...].astype(BF16)
        dk_acc[...] = jnp.zeros((2, S, HD), F32)
        dv_acc[...] = jnp.zeros((2, S, HD), F32)
        tri_inc = _tri_ones(lambda j, s: j <= s)
        tri_exc = _tri_ones(lambda j, s: j < s)
        col = lax.broadcasted_iota(jnp.int32, (BLK, KC), 1)
        rowi = lax.broadcasted_iota(jnp.int32, (BLK, KC), 0)

        for qg in range(QGROUPS):
            def qblock(ii, carry0, qg=qg):
                t0 = pl.multiple_of((qg * NSUB + ii) * BLK, BLK)
                qb = qs[pl.ds(t0, BLK), :]
                dob = dos[pl.ds(t0, BLK), :]
                tb = t_ref[pl.ds(t0, BLK), :]
                dqs = [jnp.zeros((BLK, HD), F32)] * 2
                pruns = [jnp.zeros((BLK, BLK), F32)] * 2
                eruns = [jnp.zeros((BLK, BLK), F32)] * 2
                for c in range(qg + 1):
                    s0 = c * KC
                    diag = c == qg
                    before = (s0 + col) < (t0 + rowi) if diag else None
                    for hh in range(2):
                        qh = qb[:, hh * HD:(hh + 1) * HD]
                        doh = dob[:, hh * HD:(hh + 1) * HD]
                        tt = tb[:, 64 * hh:64 * hh + 1]
                        kh = ks[s0:s0 + KC, hh * HD:(hh + 1) * HD]
                        vh = vs[s0:s0 + KC, hh * HD:(hh + 1) * HD]
                        lb, lk = _log_sig_pair(_dot(qh, kh, 1, 1))
                        if diag:
                            lk = jnp.where(before, lk, 0.0)
                        pin, ptot = _sub_sums(lk, tri_inc)
                        ws, prun = [], pruns[hh]
                        for s in range(NSUB):
                            ws.append(jnp.exp(lb[:, s * BLK:(s + 1) * BLK] + (tt - (pin[s] + prun))))
                            prun = prun + ptot[s]
                        w = jnp.concatenate(ws, axis=1)
                        if diag:
                            w = jnp.where(before, w, 0.0)
                        e = w * _dot(doh, vh, 1, 1)
                        pex, etot = _sub_sums(e, tri_exc)
                        cs, erun = [], eruns[hh]
                        for s in range(NSUB):
                            cs.append(pex[s] + erun)
                            erun = erun + etot[s]
                        sig = jnp.exp(lb)
                        dz = e * (1.0 - sig) - jnp.concatenate(cs, axis=1) * sig
                        if diag:
                            dz = jnp.where(before, dz, 0.0)
                        dz = dz.astype(BF16)
                        dqs[hh] = dqs[hh] + _dot(dz, kh, 1, 0)
                        dk_acc[hh, s0:s0 + KC, :] += _dot(dz, qh, 0, 0)
                        dv_acc[hh, s0:s0 + KC, :] += _dot(w.astype(BF16), doh, 0, 0)
                        pruns[hh], eruns[hh] = prun, erun
                dq_ref[pl.ds(t0, BLK), :] = (jnp.concatenate(dqs, axis=1) * SCALE).astype(BF16)
                return carry0

            lax.fori_loop(0, NSUB, qblock, 0)
        dk_ref[...] = jnp.concatenate([dk_acc[0], dk_acc[1]], axis=1).astype(BF16)
        dv_ref[...] = jnp.concatenate([dv_acc[0], dv_acc[1]], axis=1).astype(BF16)

    slab = lambda off: pl.BlockSpec((None, S, LANES), lambda p: (off + p, 0, 0))
    pair = pl.BlockSpec((None, S, LANES), lambda p: (p, 0, 0))
    return pl.pallas_call(
        body, grid=(2,), in_specs=[slab(q0), slab(k0), slab(v0), pair, pair], out_specs=[pair] * 3,
        out_shape=[jax.ShapeDtypeStruct((2, S, LANES), BF16)] * 3,
        scratch_shapes=[pltpu.VMEM((S, LANES), BF16)] * 4 + [pltpu.VMEM((2, S, HD), F32)] * 2,
        compiler_params=_cp("arbitrary"), name=name)(proj, proj, proj, do, tot)


def _cat_slabs(ref):
    return jnp.concatenate([ref[s] for s in range(ref.shape[0])], axis=1)


def _merge_fwd(o_a, o_b, o_c, gates, b_gate, wa, wb, wc, w_out, name):
    tm = ROW_TILE

    def body(oa_ref, ob_ref, oc_ref, g_ref, bg_ref, wa_ref, wb_ref, wc_ref, wo_ref, mg_ref, mo_ref):
        acc = jnp.zeros((tm, D), F32)
        for i, (o_ref, w_ref) in enumerate(((oa_ref, wa_ref), (ob_ref, wb_ref), (oc_ref, wc_ref))):
            pr = _dot(_cat_slabs(o_ref).astype(BF16), w_ref[...], 1, 0)
            sg = jax.nn.sigmoid(g_ref[:, i * D:(i + 1) * D] + bg_ref[i:i + 1, :])
            acc = acc + sg * pr
        mg = acc.astype(BF16)
        mg_ref[...] = mg
        mo_ref[...] = _dot(mg, wo_ref[...], 1, 0)

    slabs = lambda n: pl.BlockSpec((n, tm, LANES), lambda i: (0, i, 0))
    full = lambda r, c: pl.BlockSpec((r, c), lambda i: (0, 0))
    row = pl.BlockSpec((tm, D), lambda i: (i, 0))
    return pl.pallas_call(
        body, grid=(S // tm,),
        in_specs=[slabs(2), slabs(4), slabs(2), pl.BlockSpec((tm, GATE_COLS), lambda i: (i, 0)), full(3, D),
                  full(256, D), full(512, D), full(256, D), full(D, D)],
        out_specs=[row, row],
        out_shape=[jax.ShapeDtypeStruct((S, D), BF16), jax.ShapeDtypeStruct((S, D), F32)],
        compiler_params=_cp("parallel"), name=name)(o_a, o_b, o_c, gates, b_gate, wa, wb, wc, w_out)


def _merge_bwd(d_mo, o_a, o_b, o_c, gates, b_gate, wa, wb, wc, w_out, name):
    tm = ROW_TILE

    def body(dmo_ref, oa_ref, ob_ref, oc_ref, g_ref, bg_ref, wa_ref, wb_ref, wc_ref, wo_ref,
             doa_ref, dob_ref, doc_ref, dg_ref, dwa_ref, dwb_ref, dwc_ref, dbg_ref):
        @pl.when(pl.program_id(0) == 0)
        def _():
            dwa_ref[...] = jnp.zeros(dwa_ref.shape, F32)
            dwb_ref[...] = jnp.zeros(dwb_ref.shape, F32)
            dwc_ref[...] = jnp.zeros(dwc_ref.shape, F32)
            dbg_ref[...] = jnp.zeros(dbg_ref.shape, F32)

        dmg = _dot(dmo_ref[...], wo_ref[...], 1, 1)
        trip = ((oa_ref, wa_ref, doa_ref, dwa_ref), (ob_ref, wb_ref, dob_ref, dwb_ref), (oc_ref, wc_ref, doc_ref, dwc_ref))
        for i, (o_ref, w_ref, do_ref, dw_ref) in enumerate(trip):
            ob = _cat_slabs(o_ref).astype(BF16)
            pr = _dot(ob, w_ref[...], 1, 0)
            sg = jax.nn.sigmoid(g_ref[:, i * D:(i + 1) * D] + bg_ref[i:i + 1, :])
            dgate = dmg * pr * sg * (1.0 - sg)
            dg_ref[:, i * D:(i + 1) * D] = dgate.astype(BF16)
            dbg_ref[i:i + 1, :] += jnp.sum(dgate, axis=0, keepdims=True)
            dpr = (dmg * sg).astype(BF16)
            do = _dot(dpr, w_ref[...], 1, 1)
            for s in range(do_ref.shape[0]):
                do_ref[s] = do[:, s * LANES:(s + 1) * LANES]
            dw_ref[...] += _dot(ob, dpr, 0, 0)

    slabs = lambda n: pl.BlockSpec((n, tm, LANES), lambda i: (0, i, 0))
    full = lambda r, c: pl.BlockSpec((r, c), lambda i: (0, 0))
    row = pl.BlockSpec((tm, D), lambda i: (i, 0))
    return pl.pallas_call(
        body, grid=(S // tm,),
        in_specs=[row, slabs(2), slabs(4), slabs(2), pl.BlockSpec((tm, GATE_COLS), lambda i: (i, 0)), full(3, D),
                  full(256, D), full(512, D), full(256, D), full(D, D)],
        out_specs=[slabs(2), slabs(4), slabs(2), pl.BlockSpec((tm, GATE_COLS), lambda i: (i, 0)),
                   full(256, D), full(512, D), full(256, D), full(3, D)],
        out_shape=[jax.ShapeDtypeStruct((2, S, LANES), F32), jax.ShapeDtypeStruct((4, S, LANES), F32),
                   jax.ShapeDtypeStruct((2, S, LANES), F32), jax.ShapeDtypeStruct((S, GATE_COLS), BF16),
                   jax.ShapeDtypeStruct((256, D), F32), jax.ShapeDtypeStruct((512, D), F32),
                   jax.ShapeDtypeStruct((256, D), F32), jax.ShapeDtypeStruct((3, D), F32)],
        compiler_params=_cp("arbitrary"), name=name)(d_mo, o_a, o_b, o_c, gates, b_gate, wa, wb, wc, w_out)


FC = 256
GELU_K = math.sqrt(2.0 / math.pi)
GELU_C = 0.044715


RC = 64
NRC = S // RC


def _down(prev, cur, n):
    row = lax.broadcasted_iota(jnp.int32, cur.shape, 0)
    return jnp.where(row < n, pltpu.roll(prev, n, 0), pltpu.roll(cur, n, 0))


def _up(cur, nxt, n):
    row = lax.broadcasted_iota(jnp.int32, cur.shape, 0)
    return jnp.where(row >= RC - n, pltpu.roll(nxt, RC - n, 0), pltpu.roll(cur, RC - n, 0))


def _conv_chunk(load, j, w_ref, b_ref, half):
    cur = load(j)
    prev = jnp.where(j > 0, load(jnp.maximum(j - 1, 0)), 0.0)
    d1 = _down(prev, cur, 1)
    d2 = _down(prev, cur, 2)
    y = w_ref[0:1, half, :] * d2 + w_ref[1:2, half, :] * d1 + w_ref[2:3, half, :] * cur + b_ref[half:half + 1, :]
    return y, cur, d1, d2


def _chunk(j):
    return pl.ds(pl.multiple_of(j * RC, RC), RC)


def _fold8(x):
    return jnp.sum(x.reshape(RC // 8, 8, x.shape[-1]), axis=0)


def _ffn_act(u, conv_w, conv_b, name):
    def body(u_ref, w_ref, b_ref, a_ref):
        def step(j, carry):
            yg = _conv_chunk(lambda k: u_ref[0, _chunk(k), :], j, w_ref, b_ref, 0)[0]
            yv = _conv_chunk(lambda k: u_ref[1, _chunk(k), :], j, w_ref, b_ref, 1)[0]
            th = jnp.tanh(GELU_K * (yg + GELU_C * yg * yg * yg))
            a_ref[_chunk(j), :] = (0.5 * yg * (1.0 + th) * yv).astype(BF16)
            return carry

        lax.fori_loop(0, NRC, step, 0)

    return pl.pallas_call(
        body, grid=(D_FF // FC,),
        in_specs=[pl.BlockSpec((2, S, FC), lambda j: (0, 0, j)), pl.BlockSpec((3, 2, FC), lambda j: (0, 0, j)),
                  pl.BlockSpec((2, FC), lambda j: (0, j))],
        out_specs=pl.BlockSpec((S, FC), lambda j: (0, j)),
        out_shape=jax.ShapeDtypeStruct((S, D_FF), BF16),
        compiler_params=_cp("parallel"), name=name)(u, conv_w, conv_b)


def _ffn_act_bwd(u, d_a, conv_w, conv_b, name):
    def body(u_ref, da_ref, w_ref, b_ref, du_ref, dw_ref, db_ref, dy_s):
        def first(j, acc):
            yg, ug, ug1, ug2 = _conv_chunk(lambda k: u_ref[0, _chunk(k), :], j, w_ref, b_ref, 0)
            yv, uv, uv1, uv2 = _conv_chunk(lambda k: u_ref[1, _chunk(k), :], j, w_ref, b_ref, 1)
            th = jnp.tanh(GELU_K * (yg + GELU_C * yg * yg * yg))
            gelu = 0.5 * yg * (1.0 + th)
            dgelu = 0.5 * (1.0 + th) + 0.5 * yg * (1.0 - th * th) * GELU_K * (1.0 + 3.0 * GELU_C * yg * yg)
            da = da_ref[_chunk(j), :]
            dyg = da * yv * dgelu
            dyv = da * gelu
            dy_s[0, _chunk(j), :] = dyg
            dy_s[1, _chunk(j), :] = dyv
            new = (_fold8(dyg * ug2), _fold8(dyg * ug1), _fold8(dyg * ug), _fold8(dyg),
                   _fold8(dyv * uv2), _fold8(dyv * uv1), _fold8(dyv * uv), _fold8(dyv))
            return tuple(a + n for a, n in zip(acc, new))

        acc = lax.fori_loop(0, NRC, first, tuple(jnp.zeros((8, FC), F32) for _ in range(8)))
        for half in range(2):
            for k in range(3):
                dw_ref[k:k + 1, half, :] = jnp.sum(acc[4 * half + k], axis=0, keepdims=True)
            db_ref[half:half + 1, :] = jnp.sum(acc[4 * half + 3], axis=0, keepdims=True)

        def second(j, carry):
            for half in range(2):
                cur = dy_s[half, _chunk(j), :]
                nxt = jnp.where(j < NRC - 1, dy_s[half, _chunk(jnp.minimum(j + 1, NRC - 1)), :], 0.0)
                du = (w_ref[2:3, half, :] * cur + w_ref[1:2, half, :] * _up(cur, nxt, 1)
                      + w_ref[0:1, half, :] * _up(cur, nxt, 2))
                du_ref[half, _chunk(j), :] = du.astype(BF16)
            return carry

        lax.fori_loop(0, NRC, second, 0)

    return pl.pallas_call(
        body, grid=(D_FF // FC,),
        in_specs=[pl.BlockSpec((2, S, FC), lambda j: (0, 0, j)), pl.BlockSpec((S, FC), lambda j: (0, j)),
                  pl.BlockSpec((3, 2, FC), lambda j: (0, 0, j)), pl.BlockSpec((2, FC), lambda j: (0, j))],
        out_specs=[pl.BlockSpec((2, S, FC), lambda j: (0, 0, j)), pl.BlockSpec((3, 2, FC), lambda j: (0, 0, j)),
                   pl.BlockSpec((2, FC), lambda j: (0, j))],
        out_shape=[jax.ShapeDtypeStruct((2, S, D_FF), BF16), jax.ShapeDtypeStruct((3, 2, D_FF), F32),
                   jax.ShapeDtypeStruct((2, D_FF), F32)],
        scratch_shapes=[pltpu.VMEM((2, S, FC), F32)],
        compiler_params=_cp("parallel"), name=name)(u, d_a, conv_w, conv_b)


def _layer_fwd(x, h1, w, bias, lname):
    n = lambda s: f"{lname}_{s}"
    w.need("in", h1)
    tn = 768
    proj = _mm(h1, w["w_in"], grid=(S // 1024, QKV_COLS // tn, 1),
               a_spec=pl.BlockSpec((1024, D), lambda i, j, k: (i, 0)),
               b_spec=pl.BlockSpec((D, tn), lambda i, j, k: (0, j)),
               out_shape=jax.ShapeDtypeStruct((QKV_SLABS, S, LANES), F32),
               out_spec=pl.BlockSpec((tn // LANES, 1024, LANES), lambda i, j, k: (j, i, 0)),
               ca=1, cb=0, acc_shape=(1024, tn), out_slab=True, name=n("proj_qkv"))
    gates = _mm(h1, w["w_in"], grid=(S // 1024, GATE_COLS // tn, 1),
                a_spec=pl.BlockSpec((1024, D), lambda i, j, k: (i, 0)),
                b_spec=pl.BlockSpec((D, tn), lambda i, j, k: (0, j + QKV_COLS // tn)),
                out_shape=jax.ShapeDtypeStruct((S, GATE_COLS), F32),
                out_spec=pl.BlockSpec((1024, tn), lambda i, j, k: (i, j)),
                ca=1, cb=0, acc_shape=(1024, tn), name=n("proj_gate"))
    nums, stats = [], []
    for g, (_, d) in enumerate(A_GROUPS):
        nm, st = _band_fwd(proj, bias, d=d, q0=2 * g, k0=6 + 2 * g, v0=12 + 2 * g, npairs=2, bias0=2 * g,
                           shared_kv=False, name=n(f"attn_a{g}_fwd"))
        nums.append(nm)
        stats.append(st)
    o_a, lse_a = _combine_a(nums, stats, n("attn_a_combine"))
    nm_b, st_b = _band_fwd(proj, bias, d=1, q0=18, k0=22, v0=23, npairs=4, bias0=6, shared_kv=True, name=n("attn_b_fwd"))
    o_b, lse_b = _combine_b(nm_b, st_b, w["sinks"], n("attn_b_combine"))
    o_c, tot_c = _stick_fwd(proj, q0=24, k0=26, v0=28, name=n("attn_c_fwd"))
    w.need("mix", tot_c)
    merged, mo = _merge_fwd(o_a, o_b, o_c, gates, w["b_gate"], w["w_br_a"], w["w_br_b"], w["w_br_c"], w["w_out"], n("merge_fwd"))
    x2, h2 = _postnorm_res(x, mo, w["attn_post_norm"], w["ffn_pre_norm"], n("attn_post"))
    w.need("ffn", h2)
    u = _mm(h2, w["w_up"], grid=(S // 1024, 2 * D_FF // 1024, 1),
            a_spec=pl.BlockSpec((1024, D), lambda i, j, k: (i, 0)),
            b_spec=pl.BlockSpec((D, 1024), lambda i, j, k: (0, j)),
            out_shape=jax.ShapeDtypeStruct((2, S, D_FF), F32),
            out_spec=pl.BlockSpec((None, 1024, 1024), lambda i, j, k: (j // 4, i, j % 4)),
            ca=1, cb=0, acc_shape=(1024, 1024), name=n("ffn_up"))
    a = _ffn_act(u, w["conv_w"], w["conv_b"], n("ffn_act"))
    fo = _mm_nn(a, w["w_down"], F32, 1024, 1024, 1024, n("ffn_down"))
    saved = dict(x=x, h1=h1, proj=proj, gates=gates, o_a=o_a, lse_a=lse_a, o_b=o_b, lse_b=lse_b, o_c=o_c, tot_c=tot_c,
                 merged=merged, mo=mo, x2=x2, h2=h2, u=u, a=a, fo=fo)
    return saved


def _layer_bwd(dx3, sv, w, bias, lname, tok=None, on_part=None):
    n = lambda s: f"{lname}_{s}"
    g = {}

    def part(group, vec):
        t = on_part(group, g) if on_part is not None else None
        return vec if t is None else vec + t

    gain = w["ffn_post_norm"] if tok is None else w["ffn_post_norm"] + tok
    d_fo, g["ffn_post_norm"] = _norm_bwd(sv["fo"], gain, [dx3], None, BF16, n("ffn_post_bwd"))
    d_a = _mm_nt(d_fo, w["w_down"], F32, 1024, 1024, 1024, n("ffn_down_bwd_x"))
    g["w_down"] = _mm_tn(sv["a"], d_fo, BF16, 1024, 1024, 1024, n("ffn_down_bwd_w"))
    d_u, dcw, dcb = _ffn_act_bwd(sv["u"], d_a, w["conv_w"], w["conv_b"], n("ffn_act_bwd"))
    g["conv_w"] = dcw.reshape(3, 2 * D_FF)
    g["conv_b"] = dcb.reshape(1, 2 * D_FF)
    g["w_up"] = _mm(sv["h2"], d_u, grid=(1, 2 * D_FF // 1024, S // 1024),
                    a_spec=pl.BlockSpec((1024, D), lambda i, j, k: (k, 0)),
                    b_spec=pl.BlockSpec((None, 1024, 1024), lambda i, j, k: (j // 4, k, j % 4)),
                    out_shape=jax.ShapeDtypeStruct((D, 2 * D_FF), BF16),
                    out_spec=pl.BlockSpec((D, 1024), lambda i, j, k: (0, j)),
                    ca=0, cb=0, acc_shape=(D, 1024), name=n("ffn_up_bwd_w"))
    d_h2 = _mm(d_u, w["w_up"], grid=(S // 1024, 1, 2 * D_FF // 1024),
               a_spec=pl.BlockSpec((None, 1024, 1024), lambda i, j, k: (k // 4, i, k % 4)),
               b_spec=pl.BlockSpec((D, 1024), lambda i, j, k: (0, k)),
               out_shape=jax.ShapeDtypeStruct((S, D), F32),
               out_spec=pl.BlockSpec((1024, D), lambda i, j, k: (i, 0)),
               ca=1, cb=1, acc_shape=(1024, D), name=n("ffn_up_bwd_x"))
    dx2, g["ffn_pre_norm"] = _norm_bwd(sv["x2"], part("ffn", w["ffn_pre_norm"]), [d_h2], dx3, F32, n("ffn_pre_bwd"))
    d_mo, g["attn_post_norm"] = _norm_bwd(sv["mo"], w["attn_post_norm"], [dx2], None, BF16, n("attn_post_bwd"))
    g["w_out"] = _mm_tn(sv["merged"], d_mo, BF16, 1024, 1024, 1024, n("out_bwd_w"))
    do_a, do_b, do_c, d_gates, dwa, dwb, dwc, g["b_gate"] = _merge_bwd(
        d_mo, sv["o_a"], sv["o_b"], sv["o_c"], sv["gates"], w["b_gate"], w["w_br_a"], w["w_br_b"], w["w_br_c"],
        w["w_out"], n("merge_bwd"))
    g["w_br_a"], g["w_br_b"], g["w_br_c"] = dwa, dwb, dwc
    sinks = part("mix", w["sinks"])
    proj = sv["proj"]
    dqa, dka, dva, gbias = [], [], [], []
    for gi, (_, d) in enumerate(A_GROUPS):
        dq, dk, dv, gg, _ = _band_bwd(proj, bias, sv["o_a"], do_a, sv["lse_a"], sinks, d=d, q0=2 * gi, k0=6 + 2 * gi,
                                      v0=12 + 2 * gi, npairs=2, bias0=2 * gi, shared_kv=False, name=n(f"attn_a{gi}_bwd"))
        dqa.append(dq), dka.append(dk), dva.append(dv), gbias.append(gg)
    dqb, dkb, dvb, ggb, dsink = _band_bwd(proj, bias, sv["o_b"], do_b, sv["lse_b"], sinks, d=1, q0=18, k0=22, v0=23,
                                          npairs=4, bias0=6, shared_kv=True, name=n("attn_b_bwd"))
    gbias.append(ggb)
    g["bias_g"] = jnp.concatenate(gbias, axis=0).reshape(N_BIAS_HEADS, BLK, 2 * BLK)
    g["sinks"] = dsink[:, 0, :2].reshape(1, 8)
    dqc, dkc, dvc = _stick_bwd(proj, do_c, sv["tot_c"], q0=24, k0=26, v0=28, name=n("attn_c_bwd"))
    dqkv = jnp.concatenate(dqa + dka + dva + [dqb, dkb, dvb, dqc, dkc, dvc], axis=0)
    ts = 6
    d_h1a = _mm(dqkv, w["w_in"], grid=(S // 1024, 1, QKV_SLABS // ts),
                a_spec=pl.BlockSpec((ts, 1024, LANES), lambda i, j, k: (k, i, 0)),
                b_spec=pl.BlockSpec((D, ts * LANES), lambda i, j, k: (0, k)),
                out_shape=jax.ShapeDtypeStruct((S, D), F32),
                out_spec=pl.BlockSpec((1024, D), lambda i, j, k: (i, 0)),
                ca=1, cb=1, acc_shape=(1024, D), a_slab=True, name=n("in_bwd_x_qkv"))
    d_h1b = _mm(d_gates, w["w_in"], grid=(S // 1024, 1, GATE_COLS // 768),
                a_spec=pl.BlockSpec((1024, 768), lambda i, j, k: (i, k)),
                b_spec=pl.BlockSpec((D, 768), lambda i, j, k: (0, k + QKV_COLS // 768)),
                out_shape=jax.ShapeDtypeStruct((S, D), F32),
                out_spec=pl.BlockSpec((1024, D), lambda i, j, k: (i, 0)),
                ca=1, cb=1, acc_shape=(1024, D), name=n("in_bwd_x_gate"))
    dw_in = _mm(sv["h1"], dqkv, grid=(1, QKV_SLABS // ts, S // 1024),
                a_spec=pl.BlockSpec((1024, D), lambda i, j, k: (k, 0)),
                b_spec=pl.BlockSpec((ts, 1024, LANES), lambda i, j, k: (j, k, 0)),
                out_shape=jax.ShapeDtypeStruct((D, IN_COLS), BF16),
                out_spec=pl.BlockSpec((D, ts * LANES), lambda i, j, k: (0, j)),
                ca=0, cb=0, acc_shape=(D, ts * LANES), b_slab=True, name=n("in_bwd_w_qkv"))
    g["w_in"] = _mm(sv["h1"], d_gates, grid=(1, GATE_COLS // 768, S // 1024),
                    a_spec=pl.BlockSpec((1024, D), lambda i, j, k: (k, 0)),
                    b_spec=pl.BlockSpec((1024, 768), lambda i, j, k: (k, j)),
                    out_shape=jax.ShapeDtypeStruct((D, IN_COLS), BF16),
                    out_spec=pl.BlockSpec((D, 768), lambda i, j, k: (0, j + QKV_COLS // 768)),
                    ca=0, cb=0, acc_shape=(D, 768), alias_out=dw_in, name=n("in_bwd_w_gate"))
    dx, g["attn_pre_norm"] = _norm_bwd(sv["x"], w["attn_pre_norm"], [d_h1a, d_h1b], dx2, F32, n("attn_pre_bwd"))
    tok_in = on_part("in", g) if on_part is not None else None
    return dx, g, tok_in


def _local_step(x, target, ws, rel_bias, tok=None, on_grads=None):
    buckets = jnp.asarray(_bucket_tiles())
    bias = _bias_tiles(rel_bias, buckets, "bias_tiles").reshape(N_BIAS_HEADS // 2, 2, 2, BLK, 2 * BLK)
    saved = []
    gain0 = ws[0]["attn_pre_norm"] if tok is None else ws[0]["attn_pre_norm"] + tok
    h1 = _prenorm(x, gain0, "l0_attn_pre")
    for l in range(DEPTH):
        sv = _layer_fwd(x, h1, ws[l], bias, f"l{l}")
        saved.append(sv)
        g_next = ws[l + 1]["attn_pre_norm"] if l + 1 < DEPTH else ws[l]["attn_pre_norm"]
        x, h1 = _postnorm_res(sv["x2"], sv["fo"], ws[l]["ffn_post_norm"], g_next, f"l{l}_ffn_post")
    dy, loss_tile = _loss_head(x, target, "loss_head")
    grads = [None] * DEPTH
    tok = None
    for l in reversed(range(DEPTH)):
        on_part = None if on_grads is None else functools.partial(on_grads, l)
        dy, grads[l], tok = _layer_bwd(dy, saved[l], ws[l], bias, f"l{l}", tok, on_part)
    g_rel = _bias_grad([grads[l]["bias_g"] for l in range(DEPTH)], buckets, "bias_grad")[:, :N_BIAS_HEADS]
    return loss_tile[0, 0], dy, grads, g_rel


def _coords():
    return lax.axis_index("x"), lax.axis_index("y"), lax.axis_index("c")


def _peer(rel):
    x, y, c = _coords()
    return (1 - x if rel & 4 else x, 1 - y if rel & 2 else y, 1 - c if rel & 1 else c)


def _exchange(srcs, dst_shapes, src_win, dst_win, name):
    nt = len(srcs)

    def body(*refs):
        src_refs, dst_refs = refs[:nt], refs[nt:2 * nt]
        send_sems, recv_sems, local_sems = refs[2 * nt:]
        x, y, c = _coords()
        me = 4 * x + 2 * y + c
        locals_ = []
        for t in range(nt):
            cp = pltpu.make_async_copy(src_win(t, src_refs[t], me), dst_win(t, dst_refs[t], me), local_sems.at[t])
            cp.start()
            locals_.append(cp)
        sends = []
        for rel in range(1, NDEV):
            px, py, pc = _peer(rel)
            q = 4 * px + 2 * py + pc
            for t in range(nt):
                cp = pltpu.make_async_remote_copy(
                    src_ref=src_win(t, src_refs[t], q), dst_ref=dst_win(t, dst_refs[t], me),
                    send_sem=send_sems.at[rel - 1, t], recv_sem=recv_sems.at[rel - 1, t],
                    device_id=(px, py, pc), device_id_type=MESH)
                cp.start()
                sends.append(cp)
        for rel in range(1, NDEV):
            px, py, pc = _peer(rel)
            q = 4 * px + 2 * py + pc
            for t in range(nt):
                pltpu.make_async_remote_copy(
                    src_ref=src_win(t, src_refs[t], me), dst_ref=dst_win(t, dst_refs[t], q),
                    send_sem=send_sems.at[rel - 1, t], recv_sem=recv_sems.at[rel - 1, t],
                    device_id=(px, py, pc), device_id_type=MESH).wait_recv()
        for cp in sends:
            cp.wait_send()
        for cp in locals_:
            cp.wait()

    return pl.pallas_call(
        body, in_specs=[ANY] * nt, out_specs=[ANY] * nt, out_shape=dst_shapes,
        scratch_shapes=[pltpu.SemaphoreType.DMA((NDEV - 1, nt)), pltpu.SemaphoreType.DMA((NDEV - 1, nt)),
                        pltpu.SemaphoreType.DMA((nt,))],
        name=name)(*srcs)


BIG = (("w_in", 1, 864), ("w_br_a", 1, 128), ("w_br_b", 1, 128), ("w_br_c", 1, 128), ("w_out", 0, 128),
       ("w_up", 1, 1024), ("w_down", 0, 512))


NBIG = len(BIG)
BIG_FULL = {"w_in": (D, IN_COLS), "w_br_a": (256, D), "w_br_b": (512, D), "w_br_c": (256, D), "w_out": (D, D),
            "w_up": (D, 2 * D_FF), "w_down": (D_FF, D)}
LAYER_GROUPS = (("in", (0,)), ("mix", (1, 2, 3, 4)), ("ffn", (5, 6)))

HBM_SPEC = pl.BlockSpec(memory_space=pltpu.HBM)
SEM_SPEC = pl.BlockSpec(memory_space=pltpu.SEMAPHORE)


def _hbm(a):
    return pltpu.with_memory_space_constraint(a, pltpu.HBM)


def _shard_window(t, ref, k):
    nm, ax, ext = BIG[t % NBIG]
    if nm == "w_in":
        return ref.at[k]
    off = pl.multiple_of(k * ext, ext)
    if ax == 0:
        return ref.at[pl.ds(off, ext), :]
    return ref.at[:, pl.ds(off, ext)]


def _whole(t, ref, k):
    return ref


def _slot(t, ref, k):
    return ref.at[k]


def _own_block_spec(t, rows, me_of):
    nm, ax, ext = BIG[t % NBIG]
    r, c = BIG_FULL[nm]
    if nm == "w_in":
        return pl.BlockSpec((None, rows, ext), lambda i, m: (me_of(m), i, 0))
    if ax == 0:
        return pl.BlockSpec((rows, c), lambda i, m: (me_of(m) * (ext // rows) + i, 0))
    return pl.BlockSpec((rows, ext), lambda i, m: (i, me_of(m)))


def _cast_own(t, shard, me_arr, name):
    nm, ax, ext = BIG[t % NBIG]
    nr, nc = shard.shape
    rows = min(nr, 256)
    shape = (NDEV, D, ext) if nm == "w_in" else BIG_FULL[nm]

    def body(m_ref, s_ref, o_ref):
        o_ref[...] = s_ref[...].astype(BF16)

    return pl.pallas_call(
        body, grid_spec=pltpu.PrefetchScalarGridSpec(
            num_scalar_prefetch=1, grid=(nr // rows,),
            in_specs=[pl.BlockSpec((rows, nc), lambda i, m: (i, 0))],
            out_specs=_own_block_spec(t, rows, lambda m: m[0])),
        out_shape=jax.ShapeDtypeStruct(shape, BF16), compiler_params=_cp("arbitrary"), name=name)(me_arr, shard)


def _xchg_start(srcs, lands, groups, src_win, dst_win, after, name):
    ns = 0 if srcs is None else len(srcs)
    nt, ng = len(lands), len(groups)
    ins = ([] if srcs is None else list(srcs)) + list(lands)

    def body(*refs):
        src_refs, land_refs = refs[:ns], refs[ns:ns + nt]
        sems = refs[ns + nt + 1:ns + nt + 1 + 2 * ng]
        token = refs[-1]
        x, y, c = _coords()
        me = 4 * x + 2 * y + c
        for gi, grp in enumerate(groups):
            for j, t in enumerate(grp):
                for rel in range(1, NDEV):
                    px, py, pc = _peer(rel)
                    q = 4 * px + 2 * py + pc
                    src = dst_win(t, land_refs[t], me) if srcs is None else src_win(t, src_refs[t], q)
                    pltpu.make_async_remote_copy(
                        src_ref=src, dst_ref=dst_win(t, land_refs[t], me),
                        send_sem=sems[2 * gi].at[(rel - 1) * len(grp) + j],
                        recv_sem=sems[2 * gi + 1].at[(rel - 1) * len(grp) + j],
                        device_id=(px, py, pc), device_id_type=MESH).start()
        token[...] = jnp.zeros((8, LANES), F32)

    out_shape = []
    for grp in groups:
        out_shape += [pltpu.SemaphoreType.DMA(((NDEV - 1) * len(grp),))] * 2
    out_shape += [pltpu.HBM(a.shape, a.dtype) for a in ins]
    out_shape.append(jax.ShapeDtypeStruct((8, LANES), F32))
    outs = pl.pallas_call(
        body, in_specs=[HBM_SPEC] * len(ins) + [ANY],
        out_specs=[SEM_SPEC] * (2 * ng) + [HBM_SPEC] * len(ins) + [pl.BlockSpec(memory_space=pltpu.VMEM)],
        out_shape=out_shape, input_output_aliases={i: 2 * ng + i for i in range(len(ins))},
        compiler_params=pltpu.CompilerParams(has_side_effects=pltpu.SideEffectType.DATAFLOW_SIDE_EFFECTING),
        name=name)(*[_hbm(a) for a in ins], after)
    sems = [(outs[2 * gi], outs[2 * gi + 1]) for gi in range(ng)]
    thru = list(outs[2 * ng:2 * ng + len(ins)])
    return sems, (None if srcs is None else thru[:ns]), thru[ns:], outs[-1]


def _xchg_wait(sems, srcs, lands, tids, after, src_win, dst_win, name):
    ns = 0 if srcs is None else len(srcs)
    n = len(lands)
    send_sem, recv_sem = sems
    ins = ([] if srcs is None else list(srcs)) + list(lands)

    def body(*refs):
        src_refs, land_refs = refs[:ns], refs[ns:ns + n]
        ssem, rsem = refs[ns + n], refs[ns + n + 1]
        x, y, c = _coords()
        me = 4 * x + 2 * y + c
        for j, t in enumerate(tids):
            for rel in range(1, NDEV):
                px, py, pc = _peer(rel)
                q = 4 * px + 2 * py + pc
                src = dst_win(t, land_refs[j], me) if srcs is None else src_win(t, src_refs[j], q)
                cp = pltpu.make_async_remote_copy(
                    src_ref=src, dst_ref=dst_win(t, land_refs[j], q),
                    send_sem=ssem.at[(rel - 1) * n + j], recv_sem=rsem.at[(rel - 1) * n + j],
                    device_id=(px, py, pc), device_id_type=MESH)
                cp.wait_send()
                cp.wait_recv()

    outs = pl.pallas_call(
        body, in_specs=[HBM_SPEC] * len(ins) + [SEM_SPEC, SEM_SPEC, ANY], out_specs=[HBM_SPEC] * len(ins),
        out_shape=[pltpu.HBM(a.shape, a.dtype) for a in ins],
        input_output_aliases={i: i for i in range(len(ins))},
        compiler_params=pltpu.CompilerParams(has_side_effects=pltpu.SideEffectType.DATAFLOW_SIDE_EFFECTING),
        name=name)(*ins, send_sem, recv_sem, after)
    return (None if srcs is None else list(outs[:ns])), list(outs[ns:])


class _Weights:
    def __init__(self, ready, pending=None):
        self.ready = dict(ready)
        self.pending = dict(pending or {})

    def __getitem__(self, k):
        return self.ready[k]

    def need(self, group, after):
        fn = self.pending.pop(group, None)
        if fn is not None:
            self.ready.update(fn(after))


def _adamw_math(w, g, m, v):
    m2 = ADAM_B1 * m + (1.0 - ADAM_B1) * g
    v2 = ADAM_B2 * v + (1.0 - ADAM_B2) * (g * g)
    m_hat = m2 / (1.0 - ADAM_B1 ** ADAM_STEP)
    v_hat = v2 / (1.0 - ADAM_B2 ** ADAM_STEP)
    delta = -ADAM_LR * (m_hat / (jnp.sqrt(v_hat) + ADAM_EPS) + ADAM_WD * w)
    return delta, m2, v2


def _adamw(t, parts, own, me_arr, w, m, v, layer, prev, rows, name):
    nl, nr, nc = w.shape

    def body(me_ref, p_ref, own_ref, w_ref, m_ref, v_ref, *rest):
        g_ref, d_ref, m2_ref, v2_ref = rest[-4:]
        me = me_ref[0]
        g = None
        for k in range(NDEV):
            term = jnp.where(me == k, own_ref[...], p_ref[k]).astype(F32)
            g = term if g is None else g + term
        delta, m2, v2 = _adamw_math(w_ref[...], g, m_ref[...], v_ref[...])
        g_ref[...] = g
        d_ref[...] = delta
        m2_ref[...] = m2
        v2_ref[...] = v2

    blk = pl.BlockSpec((None, rows, nc), lambda i, mm: (layer, i, 0))
    pblk = pl.BlockSpec((NDEV, rows, nc), lambda i, mm: (0, i, 0))
    extra = [] if prev is None else list(prev)
    return pl.pallas_call(
        body, grid_spec=pltpu.PrefetchScalarGridSpec(
            num_scalar_prefetch=1, grid=(nr // rows,),
            in_specs=[pblk, _own_block_spec(t, rows, lambda mm: mm[0]), blk, blk, blk] + [ANY] * len(extra),
            out_specs=[blk] * 4),
        out_shape=[jax.ShapeDtypeStruct(w.shape, F32)] * 4,
        input_output_aliases={6 + k: k for k in range(len(extra))},
        compiler_params=_cp("arbitrary"), name=name)(me_arr, parts, own, w, m, v, *extra)


SMALL_REPL = (("rel_bias", NUM_BUCKETS * N_BIAS_HEADS), ("attn_pre_norm", DEPTH * D), ("sinks", DEPTH * 8),
              ("attn_post_norm", DEPTH * D), ("ffn_pre_norm", DEPTH * D), ("conv_b", DEPTH * 2 * D_FF),
              ("ffn_post_norm", DEPTH * D))
SMALL_SHARD = (("b_gate", (DEPTH, 3, D), 128), ("conv_w", (DEPTH, 3, 2 * D_FF), 1024))


def _pack(vecs):
    flat = jnp.concatenate([v.reshape(-1).astype(F32) for v in vecs])
    n = flat.shape[0]
    rows = -(-n // (8 * LANES)) * 8
    return jnp.pad(flat, (0, rows * LANES - n)).reshape(rows, LANES)


def _unpack(packed, sizes):
    flat = packed.reshape(-1)
    out, off = [], 0
    for sz in sizes:
        out.append(flat[off:off + sz])
        off += sz
    return out


def _small_sum(parts, name):
    r = parts.shape[1]

    def body(p_ref, o_ref):
        g = p_ref[0]
        for k in range(1, NDEV):
            g = g + p_ref[k]
        o_ref[...] = g

    return pl.pallas_call(
        body, in_specs=[pl.BlockSpec(memory_space=pltpu.VMEM)], out_specs=pl.BlockSpec(memory_space=pltpu.VMEM),
        out_shape=jax.ShapeDtypeStruct((r, LANES), F32), name=name)(parts)


def _small_adamw(g, w, m, v, name):
    def body(g_ref, w_ref, m_ref, v_ref, d_ref, m2_ref, v2_ref):
        delta, m2, v2 = _adamw_math(w_ref[...], g_ref[...], m_ref[...], v_ref[...])
        d_ref[...] = delta
        m2_ref[...] = m2
        v2_ref[...] = v2

    vm = pl.BlockSpec(memory_space=pltpu.VMEM)
    return pl.pallas_call(
        body, in_specs=[vm] * 4, out_specs=[vm] * 3,
        out_shape=[jax.ShapeDtypeStruct(g.shape, F32)] * 3, name=name)(g, w, m, v)


def kernel(x, rel_bias, attn_pre_norm, w_in, b_gate, sinks, w_br_a, w_br_b, w_br_c, w_out, attn_post_norm, ffn_pre_norm, w_up, conv_w, conv_b, w_down, ffn_post_norm, loss_target, m_rel_bias, m_attn_pre_norm, m_w_in, m_b_gate, m_sinks, m_w_br_a, m_w_br_b, m_w_br_c, m_w_out, m_attn_post_norm, m_ffn_pre_norm, m_w_up, m_conv_w, m_conv_b, m_w_down, m_ffn_post_norm, v_rel_bias, v_attn_pre_norm, v_w_in, v_b_gate, v_sinks, v_w_br_a, v_w_br_b, v_w_br_c, v_w_out, v_attn_post_norm, v_ffn_pre_norm, v_w_up, v_conv_w, v_conv_b, v_w_down, v_ffn_post_norm):
    P = dict(rel_bias=rel_bias, attn_pre_norm=attn_pre_norm, w_in=w_in, b_gate=b_gate, sinks=sinks, w_br_a=w_br_a,
             w_br_b=w_br_b, w_br_c=w_br_c, w_out=w_out, attn_post_norm=attn_post_norm, ffn_pre_norm=ffn_pre_norm,
             w_up=w_up, conv_w=conv_w, conv_b=conv_b, w_down=w_down, ffn_post_norm=ffn_post_norm)
    M = dict(rel_bias=m_rel_bias, attn_pre_norm=m_attn_pre_norm, w_in=m_w_in, b_gate=m_b_gate, sinks=m_sinks,
             w_br_a=m_w_br_a, w_br_b=m_w_br_b, w_br_c=m_w_br_c, w_out=m_w_out, attn_post_norm=m_attn_post_norm,
             ffn_pre_norm=m_ffn_pre_norm, w_up=m_w_up, conv_w=m_conv_w, conv_b=m_conv_b, w_down=m_w_down,
             ffn_post_norm=m_ffn_post_norm)
    V = dict(rel_bias=v_rel_bias, attn_pre_norm=v_attn_pre_norm, w_in=v_w_in, b_gate=v_b_gate, sinks=v_sinks,
             w_br_a=v_w_br_a, w_br_b=v_w_br_b, w_br_c=v_w_br_c, w_out=v_w_out, attn_post_norm=v_attn_post_norm,
             ffn_pre_norm=v_ffn_pre_norm, w_up=v_w_up, conv_w=v_conv_w, conv_b=v_conv_b, w_down=v_w_down,
             ffn_post_norm=v_ffn_post_norm)
    xi, yi, ci = _coords()
    me = 4 * xi + 2 * yi + ci

    me_arr = me.astype(jnp.int32).reshape(1)

    small_w = _pack([b_gate.reshape(-1), conv_w.reshape(-1)])
    (small_w_all,) = _exchange([small_w], [jax.ShapeDtypeStruct((NDEV,) + small_w.shape, F32)],
                               _whole, _slot, "gather_small_weights")

    lands = [_cast_own(l * NBIG + t, P[nm][l], me_arr, f"gather_own_l{l}_{nm}")
             for l in range(DEPTH) for t, (nm, _, _) in enumerate(BIG)]
    groups = [tuple(l * NBIG + t for t in tids) for l in range(DEPTH) for _, tids in LAYER_GROUPS]
    g_sems, _, g_lands, g_tok = _xchg_start(None, lands, groups, None, _shard_window, small_w_all, "gather_start")
    tok0 = g_tok[0:1, 0:1]

    def gather_waiter(gi, l, gname, tids):
        def wait(after):
            ids = [l * NBIG + t for t in tids]
            _, got = _xchg_wait(g_sems[gi], None, [g_lands[i] for i in ids], ids, after,
                                None, _shard_window, f"gather_wait_l{l}_{gname}")
            out = {}
            for t, arr in zip(tids, got):
                nm = BIG[t][0]
                out[nm] = jnp.transpose(arr, (1, 0, 2)).reshape(D, IN_COLS) if nm == "w_in" else arr
            return out
        return wait

    pending = [{gname: gather_waiter(l * len(LAYER_GROUPS) + k, l, gname, tids)
                for k, (gname, tids) in enumerate(LAYER_GROUPS)} for l in range(DEPTH)]
    nbg = DEPTH * 3 * 128
    ncw = DEPTH * 3 * 1024
    flat_all = small_w_all.reshape(NDEV, -1)
    b_gate_full = jnp.transpose(flat_all[:, :nbg].reshape(NDEV, DEPTH, 3, 128), (1, 2, 0, 3)).reshape(DEPTH, 3, D)
    conv_w_full = jnp.transpose(flat_all[:, nbg:nbg + ncw].reshape(NDEV, DEPTH, 3, 1024), (1, 2, 0, 3)).reshape(DEPTH, 3, 2 * D_FF)

    ws = []
    for l in range(DEPTH):
        ws.append(_Weights(dict(
            b_gate=b_gate_full[l], conv_w=conv_w_full[l].reshape(3, 2, D_FF), conv_b=conv_b[l].reshape(2, D_FF),
            sinks=sinks[l].reshape(1, 8),
            attn_pre_norm=attn_pre_norm[l].reshape(1, D), attn_post_norm=attn_post_norm[l].reshape(1, D),
            ffn_pre_norm=ffn_pre_norm[l].reshape(1, D), ffn_post_norm=ffn_post_norm[l].reshape(1, D)), pending[l]))

    rs = {}

    group_tids = dict(LAYER_GROUPS)

    def start_scatter(l, gname, grads_l):
        tids = group_tids[gname]
        blocks, lands_rs = [], []
        for t in tids:
            nm, ax, ext = BIG[t]
            gfull = grads_l[nm].astype(BF16)
            if nm == "w_in":
                gfull = jnp.transpose(gfull.reshape(D, NDEV, ext), (1, 0, 2))
                shp = (NDEV, D, ext)
            else:
                shp = (NDEV, ext, gfull.shape[1]) if ax == 0 else (NDEV, gfull.shape[0], ext)
            blocks.append(gfull)
            lands_rs.append(lax.empty(shp, BF16))
        local = list(range(len(tids)))
        win = lambda j, ref, k: _shard_window(tids[j], ref, k)
        sems, s_thru, l_thru, tok = _xchg_start(blocks, lands_rs, [tuple(local)], win, _slot, me_arr,
                                                f"scatter_start_l{l}_{gname}")
        rs[(l, gname)] = (sems[0], s_thru, l_thru, win, local)
        return tok[0:1, 0:1]

    loss_local, grad_x, grads, g_rel = _local_step(x[0], loss_target[0], ws, rel_bias, tok0, start_scatter)
    loss = lax.psum(loss_local, ("x", "y", "c"))

    stack = lambda nm: jnp.stack([grads[l][nm] for l in range(DEPTH)], axis=0)
    small_names = [nm for nm, _ in SMALL_REPL] + [nm for nm, _, _ in SMALL_SHARD]
    small_g = {"rel_bias": g_rel}
    for nm in small_names[1:]:
        small_g[nm] = stack(nm)
    small_packed = _pack([small_g[nm] for nm in small_names])
    (small_parts,) = _exchange([small_packed], [jax.ShapeDtypeStruct((NDEV,) + small_packed.shape, F32)],
                               _whole, _slot, "gather_small_grads")
    small_tot = _small_sum(small_parts, "small_grad_sum")
    sizes = [sz for _, sz in SMALL_REPL] + [int(np.prod(shp)) for _, shp, _ in SMALL_SHARD]
    small_tot = dict(zip(small_names, _unpack(small_tot, sizes)))

    out_g, out_d, out_m, out_v = {}, {}, {}, {}
    prev = {nm: None for nm, _, _ in BIG}
    for l in reversed(range(DEPTH)):
        for gname in ("ffn", "mix", "in"):
            sems, s_thru, l_thru, win, local = rs[(l, gname)]
            owns, parts = _xchg_wait(sems, s_thru, l_thru, local, small_parts, win, _slot, f"scatter_wait_l{l}_{gname}")
            for t, own, prt in zip(group_tids[gname], owns, parts):
                nm = BIG[t][0]
                rows = {"w_in": 256, "w_up": 256, "w_down": 256}.get(nm, P[nm].shape[1])
                prev[nm] = _adamw(t, prt, own, me_arr, P[nm], M[nm], V[nm], l, prev[nm], rows, f"adamw_{nm}_l{l}")
    for nm, _, _ in BIG:
        out_g[nm], out_d[nm], out_m[nm], out_v[nm] = prev[nm]
    gsm = {}
    for nm, _ in SMALL_REPL:
        gsm[nm] = small_tot[nm].reshape(P[nm].shape)
    for nm, shp, ext in SMALL_SHARD:
        gsm[nm] = lax.dynamic_slice_in_dim(small_tot[nm].reshape(shp), me * ext, ext, axis=2)
    pk = lambda dct: _pack([dct[nm] for nm in small_names])
    d_s, m_s, v_s = _small_adamw(pk(gsm), pk(P), pk(M), pk(V), "adamw_small")
    szs = [int(np.prod(P[nm].shape)) for nm in small_names]
    for dst, packed in ((out_d, d_s), (out_m, m_s), (out_v, v_s)):
        for nm, piece in zip(small_names, _unpack(packed, szs)):
            dst[nm] = piece.reshape(P[nm].shape)
    for nm in small_names:
        out_g[nm] = gsm[nm]

    order = ["rel_bias", "attn_pre_norm", "w_in", "b_gate", "sinks", "w_br_a", "w_br_b", "w_br_c", "w_out",
             "attn_post_norm", "ffn_pre_norm", "w_up", "conv_w", "conv_b", "w_down", "ffn_post_norm"]
    return (loss, grad_x[None], *[out_g[k] for k in order], *[out_d[k] for k in order],
            *[out_m[k] for k in order], *[out_v[k] for k in order])
```

```python
import functools
import math

import numpy as np
import jax
import jax.numpy as jnp
from jax import lax
from jax.experimental import pallas as pl
from jax.experimental.pallas import tpu as pltpu

F32 = jnp.float32
BF16 = jnp.bfloat16

S = 2048
D = 1024
DEPTH = 2
NDEV = 8
HD = 64
BLK = 128
NB = S // BLK
A_GROUPS = ((128, 1), (512, 4), (2048, 16))
NUM_BUCKETS = 32
MAX_DISTANCE = 2048
N_BIAS_HEADS = 20
D_FF = 4096
IN_COLS = 6912
QKV_COLS = 3840
QKV_SLABS = QKV_COLS // 128
GATE_COLS = 3072
EPS = 1e-6
SCALE = HD ** -0.5
NEG = -1e30
LANES = 128

ADAM_LR = 0.001
ADAM_B1 = 0.9
ADAM_B2 = 0.999
ADAM_EPS = 1e-08
ADAM_WD = 0.01
ADAM_STEP = 10

VMEM_LIMIT = 56 * 1024 * 1024
MESH = pl.DeviceIdType.MESH
ANY = pl.BlockSpec(memory_space=pl.ANY)
SMEM = pl.BlockSpec(memory_space=pltpu.SMEM)


def _cp(*sem):
    return pltpu.CompilerParams(dimension_semantics=sem if sem else None, vmem_limit_bytes=VMEM_LIMIT)


def _dot(a, b, ca, cb):
    return lax.dot_general(a, b, (((ca,), (cb,)), ((), ())), preferred_element_type=F32)


def _mm(a, b, *, grid, a_spec, b_spec, out_shape, out_spec, ca, cb, acc_shape, name,
        a_slab=False, b_slab=False, out_slab=False, alias_out=None):
    nk = grid[2]

    def body(*refs):
        if alias_out is not None:
            a_ref, b_ref, _, o_ref, acc_ref = refs
        else:
            a_ref, b_ref, o_ref, acc_ref = refs
        k = pl.program_id(2)

        def load(ref, slab):
            if slab:
                return jnp.concatenate([ref[s] for s in range(ref.shape[0])], axis=1).astype(BF16)
            return ref[...].astype(BF16)

        def write(val):
            if out_slab:
                for s in range(o_ref.shape[0]):
                    o_ref[s] = val[:, s * LANES:(s + 1) * LANES].astype(o_ref.dtype)
            else:
                o_ref[...] = val.astype(o_ref.dtype)

        d = _dot(load(a_ref, a_slab), load(b_ref, b_slab), ca, cb)
        if nk == 1:
            write(d)
        elif direct:
            @pl.when(k == 0)
            def _():
                o_ref[...] = d

            @pl.when(k > 0)
            def _():
                o_ref[...] += d
        else:
            @pl.when(k == 0)
            def _():
                acc_ref[...] = d

            if nk > 2:
                @pl.when((k > 0) & (k < nk - 1))
                def _():
                    acc_ref[...] += d

            @pl.when(k == nk - 1)
            def _():
                write(acc_ref[...] + d)

    direct = (not out_slab) and out_shape.dtype == F32
    if nk == 1 or direct:
        acc_shape = (8, LANES)
    in_specs = [a_spec, b_spec]
    args = [a, b]
    aliases = {}
    if alias_out is not None:
        in_specs.append(ANY)
        args.append(alias_out)
        aliases = {2: 0}
    return pl.pallas_call(
        body, grid=grid, in_specs=in_specs, out_specs=out_spec, out_shape=out_shape,
        scratch_shapes=[pltpu.VMEM(acc_shape, F32)], input_output_aliases=aliases,
        compiler_params=_cp("parallel", "parallel", "arbitrary"), name=name)(*args)


def _mm_nn(a, b, out_dtype, tm, tn, tk, name):
    m, kk = a.shape
    n = b.shape[1]
    return _mm(a, b, grid=(m // tm, n // tn, kk // tk),
               a_spec=pl.BlockSpec((tm, tk), lambda i, j, k: (i, k)),
               b_spec=pl.BlockSpec((tk, tn), lambda i, j, k: (k, j)),
               out_shape=jax.ShapeDtypeStruct((m, n), out_dtype),
               out_spec=pl.BlockSpec((tm, tn), lambda i, j, k: (i, j)),
               ca=1, cb=0, acc_shape=(tm, tn), name=name)


def _mm_nt(a, b, out_dtype, tm, tn, tk, name):
    m, kk = a.shape
    n = b.shape[0]
    return _mm(a, b, grid=(m // tm, n // tn, kk // tk),
               a_spec=pl.BlockSpec((tm, tk), lambda i, j, k: (i, k)),
               b_spec=pl.BlockSpec((tn, tk), lambda i, j, k: (j, k)),
               out_shape=jax.ShapeDtypeStruct((m, n), out_dtype),
               out_spec=pl.BlockSpec((tm, tn), lambda i, j, k: (i, j)),
               ca=1, cb=1, acc_shape=(tm, tn), name=name)


def _mm_tn(a, b, out_dtype, tm, tn, tk, name):
    kk, m = a.shape
    n = b.shape[1]
    return _mm(a, b, grid=(m // tm, n // tn, kk // tk),
               a_spec=pl.BlockSpec((tk, tm), lambda i, j, k: (k, i)),
               b_spec=pl.BlockSpec((tk, tn), lambda i, j, k: (k, j)),
               out_shape=jax.ShapeDtypeStruct((m, n), out_dtype),
               out_spec=pl.BlockSpec((tm, tn), lambda i, j, k: (i, j)),
               ca=0, cb=0, acc_shape=(tm, tn), name=name)


ROW_TILE = 256


def _rms(x, g):
    r = lax.rsqrt(jnp.mean(x * x, axis=-1, keepdims=True) + EPS)
    return x * r * g


def _prenorm(x, g, name):
    def body(x_ref, g_ref, o_ref):
        o_ref[...] = _rms(x_ref[...], g_ref[...]).astype(BF16)

    return pl.pallas_call(
        body, grid=(S // ROW_TILE,),
        in_specs=[pl.BlockSpec((ROW_TILE, D), lambda i: (i, 0)), pl.BlockSpec((1, D), lambda i: (0, 0))],
        out_specs=pl.BlockSpec((ROW_TILE, D), lambda i: (i, 0)),
        out_shape=jax.ShapeDtypeStruct((S, D), BF16), compiler_params=_cp("parallel"), name=name)(x, g)


def _postnorm_res(x, f, g_post, g_next, name):
    def body(x_ref, f_ref, gp_ref, gn_ref, xo_ref, ho_ref):
        xn = x_ref[...] + _rms(f_ref[...], gp_ref[...])
        xo_ref[...] = xn
        ho_ref[...] = _rms(xn, gn_ref[...]).astype(BF16)

    row = pl.BlockSpec((ROW_TILE, D), lambda i: (i, 0))
    vec = pl.BlockSpec((1, D), lambda i: (0, 0))
    return pl.pallas_call(
        body, grid=(S // ROW_TILE,), in_specs=[row, row, vec, vec], out_specs=[row, row],
        out_shape=[jax.ShapeDtypeStruct((S, D), F32), jax.ShapeDtypeStruct((S, D), BF16)],
        compiler_params=_cp("parallel"), name=name)(x, f, g_post, g_next)


def _norm_bwd(f, g, dys, res, out_dtype, name):
    ndy = len(dys)
    has_res = res is not None

    def body(*refs):
        f_ref, g_ref = refs[0], refs[1]
        dy_refs = refs[2:2 + ndy]
        res_ref = refs[2 + ndy] if has_res else None
        o_ref, dg_ref = refs[-2], refs[-1]
        fv = f_ref[...]
        dy = dy_refs[0][...].astype(F32)
        for r in dy_refs[1:]:
            dy = dy + r[...].astype(F32)
        r = lax.rsqrt(jnp.mean(fv * fv, axis=-1, keepdims=True) + EPS)
        n = fv * r
        dn = dy * g_ref[...]
        df = r * (dn - n * jnp.mean(dn * n, axis=-1, keepdims=True))
        if has_res:
            df = df + res_ref[...]
        o_ref[...] = df.astype(out_dtype)

        @pl.when(pl.program_id(0) == 0)
        def _():
            dg_ref[...] = jnp.zeros((1, D), F32)

        dg_ref[...] += jnp.sum(dy * n, axis=0, keepdims=True)

    row = pl.BlockSpec((ROW_TILE, D), lambda i: (i, 0))
    vec = pl.BlockSpec((1, D), lambda i: (0, 0))
    in_specs = [row, vec] + [row] * ndy + ([row] if has_res else [])
    args = [f, g] + list(dys) + ([res] if has_res else [])
    return pl.pallas_call(
        body, grid=(S // ROW_TILE,), in_specs=in_specs, out_specs=[row, vec],
        out_shape=[jax.ShapeDtypeStruct((S, D), out_dtype), jax.ShapeDtypeStruct((1, D), F32)],
        compiler_params=_cp("arbitrary"), name=name)(*args)


def _loss_head(y, target, name):
    def body(y_ref, t_ref, dy_ref, l_ref):
        e = y_ref[...] - t_ref[...]
        dy_ref[...] = e * (1.0 / D)

        @pl.when(pl.program_id(0) == 0)
        def _():
            l_ref[...] = jnp.zeros((8, LANES), F32)

        l_ref[...] += jnp.sum(e * e) * (0.5 / D)

    row = pl.BlockSpec((ROW_TILE, D), lambda i: (i, 0))
    return pl.pallas_call(
        body, grid=(S // ROW_TILE,), in_specs=[row, row],
        out_specs=[row, pl.BlockSpec((8, LANES), lambda i: (0, 0))],
        out_shape=[jax.ShapeDtypeStruct((S, D), F32), jax.ShapeDtypeStruct((8, LANES), F32)],
        compiler_params=_cp("arbitrary"), name=name)(y, target)


def _bucket_tiles():
    a = np.arange(BLK)[:, None]
    b = np.arange(2 * BLK)[None, :]
    dist = a + BLK - b
    out = np.zeros((4, 2, BLK, 2 * BLK), np.int32)
    cfg = [(w // d, d) for w, d in A_GROUPS] + [(BLK - 1, 1)]
    for gi, (max_dist, d) in enumerate(cfg):
        band = (dist >= 0) & (dist <= max_dist)
        tok = np.maximum(dist, 0) * d
        nf = np.maximum(tok, 1).astype(np.float32)
        max_exact = NUM_BUCKETS // 2
        large = max_exact + (np.log(nf / np.float32(max_exact)) / np.float32(math.log(MAX_DISTANCE / max_exact))
                             * np.float32(NUM_BUCKETS - max_exact)).astype(np.int32)
        large = np.minimum(large, NUM_BUCKETS - 1)
        bkt = np.where(tok < max_exact, tok, large).astype(np.int32)
        full = np.where(band, bkt, -1)
        out[gi, 1] = full
        out[gi, 0] = np.where(b >= BLK, full, -1)
    return out


def _bias_tiles(rel_bias, buckets, name):
    def body(tab_ref, bkt_ref, o_ref):
        h = pl.program_id(0)
        bkt = bkt_ref[...]
        acc = jnp.zeros(bkt.shape, F32)
        for bb in range(NUM_BUCKETS):
            acc = jnp.where(bkt == bb, tab_ref[bb, h], acc)
        o_ref[...] = jnp.where(bkt < 0, NEG, acc)

    return pl.pallas_call(
        body, grid=(N_BIAS_HEADS,),
        in_specs=[SMEM, pl.BlockSpec((None, 2, BLK, 2 * BLK), lambda h: (jnp.minimum(h // 4, 3), 0, 0, 0))],
        out_specs=pl.BlockSpec((None, 2, BLK, 2 * BLK), lambda h: (h, 0, 0, 0)),
        out_shape=jax.ShapeDtypeStruct((N_BIAS_HEADS, 2, BLK, 2 * BLK), F32),
        compiler_params=_cp("arbitrary"), name=name)(rel_bias, buckets)


def _bias_grad(gs, buckets, name):
    ng = len(gs)

    def body(*refs):
        g_refs = refs[:ng]
        bkt_ref, o_ref = refs[ng], refs[ng + 1]
        h = pl.program_id(0)
        g = g_refs[0][...]
        for r in g_refs[1:]:
            g = g + r[...]
        bkt = bkt_ref[...]
        row = lax.broadcasted_iota(jnp.int32, (NUM_BUCKETS, LANES), 0)
        lane = lax.broadcasted_iota(jnp.int32, (NUM_BUCKETS, LANES), 1)

        @pl.when(h == 0)
        def _():
            o_ref[...] = jnp.zeros((NUM_BUCKETS, LANES), F32)

        acc = o_ref[...]
        for bb in range(NUM_BUCKETS):
            s = jnp.sum(jnp.where(bkt == bb, g, 0.0))
            acc = jnp.where((row == bb) & (lane == h), s, acc)
        o_ref[...] = acc

    g_spec = pl.BlockSpec((None, BLK, 2 * BLK), lambda h: (h, 0, 0))
    return pl.pallas_call(
        body, grid=(N_BIAS_HEADS,),
        in_specs=[g_spec] * ng + [pl.BlockSpec((None, None, BLK, 2 * BLK), lambda h: (jnp.minimum(h // 4, 3), 1, 0, 0))],
        out_specs=pl.BlockSpec((NUM_BUCKETS, LANES), lambda h: (0, 0)),
        out_shape=jax.ShapeDtypeStruct((NUM_BUCKETS, LANES), F32),
        compiler_params=_cp("arbitrary"), name=name)(*gs, buckets)


def _to_class_major(src_ref, dst_ref, d, scale=None, dtype=None):
    ln = S // d
    for r in range(d):
        v = src_ref[pl.ds(r, ln, stride=d), :] if d > 1 else src_ref[...]
        if scale is not None:
            v = v * scale
        dst_ref[pl.ds(r * ln, ln), :] = v.astype(dtype or dst_ref.dtype)


def _block_rows(b, d):
    nbc = NB // d
    i = b % nbc
    r = b // nbc
    has_prev = (i > 0).astype(jnp.int32)
    prev = pl.multiple_of(jnp.maximum(b - 1, 0) * BLK, BLK)
    nat = i * (BLK * d) + r
    return has_prev, prev, nat


def _lane_halves(v0, v1):
    lane = lax.broadcasted_iota(jnp.int32, (v0.shape[0], LANES), 1)
    return jnp.where(lane < HD, v0, v1)


def _band_fwd(proj, bias, *, d, q0, k0, v0, npairs, bias0, shared_kv, name):
    def body(q_ref, k_ref, v_ref, b_ref, num_ref, st_ref, qs, ks, vs):
        p = pl.program_id(0)
        _to_class_major(q_ref, qs, d, scale=SCALE)
        _to_class_major(k_ref, ks, d)
        _to_class_major(v_ref, vs, d)
        lane = lax.broadcasted_iota(jnp.int32, (BLK, LANES), 1)

        def blk(b, carry):
            has_prev, prev, nat = _block_rows(b, d)
            cur = pl.multiple_of(b * BLK, BLK)
            qb = qs[pl.ds(cur, BLK), :]
            k2 = jnp.concatenate([ks[pl.ds(prev, BLK), :], ks[pl.ds(cur, BLK), :]], axis=0)
            v2 = jnp.concatenate([vs[pl.ds(prev, BLK), :], vs[pl.ds(cur, BLK), :]], axis=0)
            nums, ms, ls = [], [], []
            for hh in range(2):
                qh = qb[:, hh * HD:(hh + 1) * HD]
                if shared_kv:
                    kh = jnp.where(p >= 2, k2[:, HD:], k2[:, :HD])
                    vh = jnp.where(p >= 2, v2[:, HD:], v2[:, :HD])
                else:
                    kh = k2[:, hh * HD:(hh + 1) * HD]
                    vh = v2[:, hh * HD:(hh + 1) * HD]
                z = _dot(qh, kh, 1, 1) + b_ref[hh, has_prev]
                m = jnp.max(z, axis=1, keepdims=True)
                e = jnp.exp(z - m)
                ls.append(jnp.sum(e, axis=1, keepdims=True))
                ms.append(m)
                nums.append(_dot(e.astype(BF16), vh, 1, 0))
            num_t = jnp.concatenate(nums, axis=1)
            st_t = jnp.where(lane < 32, ms[0], jnp.where(lane < 64, ls[0], jnp.where(lane < 96, ms[1], ls[1])))
            if d > 1:
                num_ref[pl.ds(nat, BLK, stride=d), :] = num_t
                st_ref[pl.ds(nat, BLK, stride=d), :] = st_t
            else:
                num_ref[pl.ds(cur, BLK), :] = num_t
                st_ref[pl.ds(cur, BLK), :] = st_t
            return carry

        lax.fori_loop(0, NB, blk, 0, unroll=4)

    slab = lambda off, per_pair: pl.BlockSpec((None, S, LANES), (lambda p: (off + p, 0, 0)) if per_pair else (lambda p: (off, 0, 0)))
    out = pl.BlockSpec((None, S, LANES), lambda p: (p, 0, 0))
    return pl.pallas_call(
        body, grid=(npairs,),
        in_specs=[slab(q0, True), slab(k0, not shared_kv), slab(v0, not shared_kv),
                  pl.BlockSpec((None, 2, 2, BLK, 2 * BLK), lambda p: (bias0 + p, 0, 0, 0, 0))],
        out_specs=[out, out],
        out_shape=[jax.ShapeDtypeStruct((npairs, S, LANES), F32)] * 2,
        scratch_shapes=[pltpu.VMEM((S, LANES), BF16)] * 3,
        compiler_params=_cp("arbitrary"), name=name)(proj, proj, proj, bias)


def _combine_a(nums, stats, name):
    rt = 512

    def body(n0, n1, n2, s0, s1, s2, o_ref, l_ref):
        n_refs, s_refs = (n0, n1, n2), (s0, s1, s2)
        outs, lses = [], []
        for hh in range(2):
            ms = [s[:, 64 * hh:64 * hh + 1] for s in s_refs]
            ls = [s[:, 64 * hh + 32:64 * hh + 33] for s in s_refs]
            mx = jnp.maximum(jnp.maximum(ms[0], ms[1]), ms[2])
            cs = [jnp.exp(m - mx) for m in ms]
            z = cs[0] * ls[0] + cs[1] * ls[1] + cs[2] * ls[2]
            acc = cs[0] * n_refs[0][:, hh * HD:(hh + 1) * HD]
            acc = acc + cs[1] * n_refs[1][:, hh * HD:(hh + 1) * HD]
            acc = acc + cs[2] * n_refs[2][:, hh * HD:(hh + 1) * HD]
            outs.append(acc / z)
            lses.append(mx + jnp.log(z))
        o_ref[...] = jnp.concatenate(outs, axis=1)
        l_ref[...] = _lane_halves(lses[0], lses[1])

    spec = pl.BlockSpec((None, rt, LANES), lambda p, i: (p, i, 0))
    return pl.pallas_call(
        body, grid=(2, S // rt), in_specs=[spec] * 6, out_specs=[spec, spec],
        out_shape=[jax.ShapeDtypeStruct((2, S, LANES), F32)] * 2,
        compiler_params=_cp("parallel", "parallel"), name=name)(*nums, *stats)


def _combine_b(num, stats, sinks, name):
    rt = 512

    def body(sink_ref, n_ref, s_ref, o_ref, l_ref):
        p = pl.program_id(0)
        outs, lses = [], []
        for hh in range(2):
            sink = sink_ref[0, 2 * p + hh]
            m = s_ref[:, 64 * hh:64 * hh + 1]
            l = s_ref[:, 64 * hh + 32:64 * hh + 33]
            mx = jnp.maximum(m, sink)
            c = jnp.exp(m - mx)
            z = l * c + jnp.exp(sink - mx)
            outs.append(n_ref[:, hh * HD:(hh + 1) * HD] * (c / z))
            lses.append(mx + jnp.log(z))
        o_ref[...] = jnp.concatenate(outs, axis=1)
        l_ref[...] = _lane_halves(lses[0], lses[1])

    spec = pl.BlockSpec((None, rt, LANES), lambda p, i: (p, i, 0))
    return pl.pallas_call(
        body, grid=(4, S // rt), in_specs=[SMEM, spec, spec], out_specs=[spec, spec],
        out_shape=[jax.ShapeDtypeStruct((4, S, LANES), F32)] * 2,
        compiler_params=_cp("parallel", "parallel"), name=name)(sinks, num, stats)


def _band_bwd(proj, bias, o, do, lse, sinks, *, d, q0, k0, v0, npairs, bias0, shared_kv, name):
    nkv = 1 if shared_kv else npairs

    def body(sink_ref, q_ref, k_ref, v_ref, b_ref, o_ref, do_ref, lse_ref,
             dq_ref, dk_ref, dv_ref, g_ref, ds_ref,
             qs, ks, vs, dos, lses, dls, dl_nat, dq_nat, dk_cm, dv_cm, kv_nat):
        p = pl.program_id(0)
        lane = lax.broadcasted_iota(jnp.int32, (S, LANES), 1)
        dov = do_ref[...]
        prod = dov * o_ref[...]
        dl0 = jnp.sum(jnp.where(lane < HD, prod, 0.0), axis=1, keepdims=True)
        dl1 = jnp.sum(jnp.where(lane >= HD, prod, 0.0), axis=1, keepdims=True)
        dl_nat[...] = jnp.where(lane < HD, dl0, dl1)
        if shared_kv:
            row8 = lax.broadcasted_iota(jnp.int32, (8, LANES), 0)
            lane8 = lax.broadcasted_iota(jnp.int32, (8, LANES), 1)
            t = jnp.zeros((8, LANES), F32)
            lv = lse_ref[...]
            for hh in range(2):
                sink = sink_ref[0, 2 * p + hh]
                ps = jnp.exp(sink - lv[:, 64 * hh:64 * hh + 1])
                dsink = -jnp.sum(ps * (dl0 if hh == 0 else dl1))
                t = jnp.where((row8 == 0) & (lane8 == hh), dsink, t)
            ds_ref[...] = t
        else:
            ds_ref[...] = jnp.zeros((8, LANES), F32)
        _to_class_major(q_ref, qs, d, scale=SCALE)
        _to_class_major(k_ref, ks, d)
        _to_class_major(v_ref, vs, d)
        _to_class_major(do_ref, dos, d)
        _to_class_major(lse_ref, lses, d)
        _to_class_major(dl_nat, dls, d)

        def zero_kv():
            dk_cm[...] = jnp.zeros((S, LANES), F32)
            dv_cm[...] = jnp.zeros((S, LANES), F32)

        if shared_kv:
            pl.when(p == 0)(zero_kv)
        else:
            zero_kv()

        g_ref[...] = jnp.zeros((2, BLK, 2 * BLK), F32)
        lane2 = lax.broadcasted_iota(jnp.int32, (2 * BLK, LANES), 1)

        def blk(b, carry):
            has_prev, prev, nat = _block_rows(b, d)
            cur = pl.multiple_of(b * BLK, BLK)
            qb = qs[pl.ds(cur, BLK), :]
            dob = dos[pl.ds(cur, BLK), :]
            lb = lses[pl.ds(cur, BLK), :]
            dlb = dls[pl.ds(cur, BLK), :]
            k2 = jnp.concatenate([ks[pl.ds(prev, BLK), :], ks[pl.ds(cur, BLK), :]], axis=0)
            v2 = jnp.concatenate([vs[pl.ds(prev, BLK), :], vs[pl.ds(cur, BLK), :]], axis=0)
            dqs, dks, dvs = [], [], []
            for hh in range(2):
                qh = qb[:, hh * HD:(hh + 1) * HD]
                doh = dob[:, hh * HD:(hh + 1) * HD]
                if shared_kv:
                    kh = jnp.where(p >= 2, k2[:, HD:], k2[:, :HD])
                    vh = jnp.where(p >= 2, v2[:, HD:], v2[:, :HD])
                else:
                    kh = k2[:, hh * HD:(hh + 1) * HD]
                    vh = v2[:, hh * HD:(hh + 1) * HD]
                z = _dot(qh, kh, 1, 1) + b_ref[hh, has_prev]
                pr = jnp.exp(z - lb[:, 64 * hh:64 * hh + 1])
                dp = _dot(doh, vh, 1, 1)
                dz = pr * (dp - dlb[:, 64 * hh:64 * hh + 1])
                g_ref[hh] += dz
                dzb = dz.astype(BF16)
                dqs.append(_dot(dzb, kh, 1, 0) * SCALE)
                dks.append(_dot(dzb, qh, 0, 0))
                dvs.append(_dot(pr.astype(BF16), doh, 0, 0))
            dq_t = jnp.concatenate(dqs, axis=1)
            if shared_kv:
                dk_t = jnp.concatenate([dks[0] + dks[1]] * 2, axis=1)
                dv_t = jnp.concatenate([dvs[0] + dvs[1]] * 2, axis=1)
                mine = (lane2 >= HD) == (p >= 2)
                dk_t = jnp.where(mine, dk_t, 0.0)
                dv_t = jnp.where(mine, dv_t, 0.0)
            else:
                dk_t = jnp.concatenate(dks, axis=1)
                dv_t = jnp.concatenate(dvs, axis=1)
            dk_cm[pl.ds(prev, BLK), :] += dk_t[:BLK]
            dk_cm[pl.ds(cur, BLK), :] += dk_t[BLK:]
            dv_cm[pl.ds(prev, BLK), :] += dv_t[:BLK]
            dv_cm[pl.ds(cur, BLK), :] += dv_t[BLK:]
            if d > 1:
                dq_nat[pl.ds(nat, BLK, stride=d), :] = dq_t
            else:
                dq_nat[pl.ds(cur, BLK), :] = dq_t
            return carry

        lax.fori_loop(0, NB, blk, 0, unroll=4)
        dq_ref[...] = dq_nat[...].astype(BF16)

        def from_class_major(src, dst_ref):
            if d == 1:
                dst_ref[...] = src[...].astype(BF16)
            else:
                ln = S // d
                for r in range(d):
                    kv_nat[pl.ds(r, ln, stride=d), :] = src[pl.ds(r * ln, ln), :]
                dst_ref[...] = kv_nat[...].astype(BF16)

        def write_kv():
            from_class_major(dk_cm, dk_ref)
            from_class_major(dv_cm, dv_ref)

        if shared_kv:
            pl.when(p == npairs - 1)(write_kv)
        else:
            write_kv()

    slab = lambda off, per_pair: pl.BlockSpec((None, S, LANES), (lambda p: (off + p, 0, 0)) if per_pair else (lambda p: (off, 0, 0)))
    pair = pl.BlockSpec((None, S, LANES), lambda p: (p, 0, 0))
    kv_out = pair if not shared_kv else pl.BlockSpec((None, S, LANES), lambda p: (0, 0, 0))
    return pl.pallas_call(
        body, grid=(npairs,),
        in_specs=[SMEM, slab(q0, True), slab(k0, not shared_kv), slab(v0, not shared_kv),
                  pl.BlockSpec((None, 2, 2, BLK, 2 * BLK), lambda p: (bias0 + p, 0, 0, 0, 0)),
                  pair, pair, pair],
        out_specs=[pair, kv_out, kv_out,
                   pl.BlockSpec((None, 2, BLK, 2 * BLK), lambda p: (p, 0, 0, 0)),
                   pl.BlockSpec((None, 8, LANES), lambda p: (p, 0, 0))],
        out_shape=[jax.ShapeDtypeStruct((npairs, S, LANES), BF16),
                   jax.ShapeDtypeStruct((nkv, S, LANES), BF16),
                   jax.ShapeDtypeStruct((nkv, S, LANES), BF16),
                   jax.ShapeDtypeStruct((npairs, 2, BLK, 2 * BLK), F32),
                   jax.ShapeDtypeStruct((npairs, 8, LANES), F32)],
        scratch_shapes=[pltpu.VMEM((S, LANES), BF16)] * 4 + [pltpu.VMEM((S, LANES), F32)] * 7,
        compiler_params=_cp("arbitrary"), name=name)(sinks, proj, proj, proj, bias, o, do, lse)


KC = 512
NSUB = KC // BLK


def _split2(x):
    hi = x.astype(BF16)
    lo = (x - hi.astype(F32)).astype(BF16)
    return hi, lo


def _tri_ones(cmp):
    jj = lax.broadcasted_iota(jnp.int32, (2 * BLK, BLK), 0) % BLK
    ss = lax.broadcasted_iota(jnp.int32, (2 * BLK, BLK), 1)
    return jnp.concatenate([cmp(jj, ss).astype(BF16), jnp.ones((2 * BLK, BLK), BF16)], axis=1)


def _sub_sums(x, tri1):
    st = jnp.concatenate([x[:, s * BLK:(s + 1) * BLK] for s in range(NSUB)], axis=0)
    hi, lo = _split2(st)
    r = _dot(jnp.concatenate([hi, lo], axis=1), tri1, 1, 0)
    return ([r[s * BLK:(s + 1) * BLK, :BLK] for s in range(NSUB)], [r[s * BLK:(s + 1) * BLK, BLK:] for s in range(NSUB)])


def _log_sig_pair(z):
    lb = jnp.minimum(z, 0.0) - jnp.log1p(jnp.exp(-jnp.abs(z)))
    return lb, lb - z


QGROUPS = NB // NSUB


def _stick_fwd(proj, *, q0, k0, v0, name):
    def body(q_ref, k_ref, v_ref, o_ref, t_ref, qs, ks, vs):
        qs[...] = (q_ref[...] * SCALE).astype(BF16)
        ks[...] = k_ref[...].astype(BF16)
        vs[...] = v_ref[...].astype(BF16)
        tri1 = _tri_ones(lambda j, s: j > s)
        col = lax.broadcasted_iota(jnp.int32, (BLK, KC), 1)
        rowi = lax.broadcasted_iota(jnp.int32, (BLK, KC), 0)

        for qg in range(QGROUPS):
            def qblock(ii, carry0, qg=qg):
                t0 = pl.multiple_of((qg * NSUB + ii) * BLK, BLK)
                qb = qs[pl.ds(t0, BLK), :]
                accs = [jnp.zeros((BLK, HD), F32)] * 2
                runs = [jnp.zeros((BLK, BLK), F32)] * 2
                for c in reversed(range(qg + 1)):
                    s0 = c * KC
                    diag = c == qg
                    before = (s0 + col) < (t0 + rowi) if diag else None
                    for hh in range(2):
                        kh = ks[s0:s0 + KC, hh * HD:(hh + 1) * HD]
                        vh = vs[s0:s0 + KC, hh * HD:(hh + 1) * HD]
                        lb, lk = _log_sig_pair(_dot(qb[:, hh * HD:(hh + 1) * HD], kh, 1, 1))
                        if diag:
                            lk = jnp.where(before, lk, 0.0)
                        suf, tot = _sub_sums(lk, tri1)
                        ws, run = [], runs[hh]
                        for s in reversed(range(NSUB)):
                            ws.append(jnp.exp(lb[:, s * BLK:(s + 1) * BLK] + suf[s] + run))
                            run = run + tot[s]
                        w = jnp.concatenate(ws[::-1], axis=1)
                        if diag:
                            w = jnp.where(before, w, 0.0)
                        accs[hh] = accs[hh] + _dot(w.astype(BF16), vh, 1, 0)
                        runs[hh] = run
                o_ref[pl.ds(t0, BLK), :] = jnp.concatenate(accs, axis=1)
                t_ref[pl.ds(t0, BLK), :] = _lane_halves(runs[0], runs[1])
                return carry0

            lax.fori_loop(0, NSUB, qblock, 0)

    slab = lambda off: pl.BlockSpec((None, S, LANES), lambda p: (off + p, 0, 0))
    out = pl.BlockSpec((None, S, LANES), lambda p: (p, 0, 0))
    return pl.pallas_call(
        body, grid=(2,), in_specs=[slab(q0), slab(k0), slab(v0)], out_specs=[out, out],
        out_shape=[jax.ShapeDtypeStruct((2, S, LANES), F32)] * 2,
        scratch_shapes=[pltpu.VMEM((S, LANES), BF16)] * 3,
        compiler_params=_cp("arbitrary"), name=name)(proj, proj, proj)


def _stick_bwd(proj, do, tot, *, q0, k0, v0, name):
    def body(q_ref, k_ref, v_ref, do_ref, t_ref, dq_ref, dk_ref, dv_ref, qs, ks, vs, dos, dk_acc, dv_acc):
        qs[...] = (q_ref[...] * SCALE).astype(BF16)
        ks[...] = k_ref[...].astype(BF16)
        vs[...] = v_ref[...].astype(BF16)
        dos[...] = do_ref[...].astype(BF16)
        dk_acc[...] = jnp.zeros((2, S, HD), F32)
        dv_acc[...] = jnp.zeros((2, S, HD), F32)
        tri_inc = _tri_ones(lambda j, s: j <= s)
        tri_exc = _tri_ones(lambda j, s: j < s)
        col = lax.broadcasted_iota(jnp.int32, (BLK, KC), 1)
        rowi = lax.broadcasted_iota(jnp.int32, (BLK, KC), 0)

        for qg in range(QGROUPS):
            def qblock(ii, carry0, qg=qg):
                t0 = pl.multiple_of((qg * NSUB + ii) * BLK, BLK)
                qb = qs[pl.ds(t0, BLK), :]
                dob = dos[pl.ds(t0, BLK), :]
                tb = t_ref[pl.ds(t0, BLK), :]
                dqs = [jnp.zeros((BLK, HD), F32)] * 2
                pruns = [jnp.zeros((BLK, BLK), F32)] * 2
                eruns = [jnp.zeros((BLK, BLK), F32)] * 2
                for c in range(qg + 1):
                    s0 = c * KC
                    diag = c == qg
                    before = (s0 + col) < (t0 + rowi) if diag else None
                    for hh in range(2):
                        qh = qb[:, hh * HD:(hh + 1) * HD]
                        doh = dob[:, hh * HD:(hh + 1) * HD]
                        tt = tb[:, 64 * hh:64 * hh + 1]
                        kh = ks[s0:s0 + KC, hh * HD:(hh + 1) * HD]
                        vh = vs[s0:s0 + KC, hh * HD:(hh + 1) * HD]
                        lb, lk = _log_sig_pair(_dot(qh, kh, 1, 1))
                        if diag:
                            lk = jnp.where(before, lk, 0.0)
                        pin, ptot = _sub_sums(lk, tri_inc)
                        ws, prun = [], pruns[hh]
                        for s in range(NSUB):
                            ws.append(jnp.exp(lb[:, s * BLK:(s + 1) * BLK] + (tt - (pin[s] + prun))))
                            prun = prun + ptot[s]
                        w = jnp.concatenate(ws, axis=1)
                        if diag:
                            w = jnp.where(before, w, 0.0)
                        e = w * _dot(doh, vh, 1, 1)
                        pex, etot = _sub_sums(e, tri_exc)
                        cs, erun = [], eruns[hh]
                        for s in range(NSUB):
                            cs.append(pex[s] + erun)
                            erun = erun + etot[s]
                        sig = jnp.exp(lb)
                        dz = e * (1.0 - sig) - jnp.concatenate(cs, axis=1) * sig
                        if diag:
                            dz = jnp.where(before, dz, 0.0)
                        dz = dz.astype(BF16)
                        dqs[hh] = dqs[hh] + _dot(dz, kh, 1, 0)
                        dk_acc[hh, s0:s0 + KC, :] += _dot(dz, qh, 0, 0)
                        dv_acc[hh, s0:s0 + KC, :] += _dot(w.astype(BF16), doh, 0, 0)
                        pruns[hh], eruns[hh] = prun, erun
                dq_ref[pl.ds(t0, BLK), :] = (jnp.concatenate(dqs, axis=1) * SCALE).astype(BF16)
                return carry0

            lax.fori_loop(0, NSUB, qblock, 0)
        dk_ref[...] = jnp.concatenate([dk_acc[0], dk_acc[1]], axis=1).astype(BF16)
        dv_ref[...] = jnp.concatenate([dv_acc[0], dv_acc[1]], axis=1).astype(BF16)

    slab = lambda off: pl.BlockSpec((None, S, LANES), lambda p: (off + p, 0, 0))
    pair = pl.BlockSpec((None, S, LANES), lambda p: (p, 0, 0))
    return pl.pallas_call(
        body, grid=(2,), in_specs=[slab(q0), slab(k0), slab(v0), pair, pair], out_specs=[pair] * 3,
        out_shape=[jax.ShapeDtypeStruct((2, S, LANES), BF16)] * 3,
        scratch_shapes=[pltpu.VMEM((S, LANES), BF16)] * 4 + [pltpu.VMEM((2, S, HD), F32)] * 2,
        compiler_params=_cp("arbitrary"), name=name)(proj, proj, proj, do, tot)


def _cat_slabs(ref):
    return jnp.concatenate([ref[s] for s in range(ref.shape[0])], axis=1)


def _merge_fwd(o_a, o_b, o_c, gates, b_gate, wa, wb, wc, w_out, name):
    tm = ROW_TILE

    def body(oa_ref, ob_ref, oc_ref, g_ref, bg_ref, wa_ref, wb_ref, wc_ref, wo_ref, mg_ref, mo_ref):
        acc = jnp.zeros((tm, D), F32)
        for i, (o_ref, w_ref) in enumerate(((oa_ref, wa_ref), (ob_ref, wb_ref), (oc_ref, wc_ref))):
            pr = _dot(_cat_slabs(o_ref).astype(BF16), w_ref[...], 1, 0)
            sg = jax.nn.sigmoid(g_ref[:, i * D:(i + 1) * D] + bg_ref[i:i + 1, :])
            acc = acc + sg * pr
        mg = acc.astype(BF16)
        mg_ref[...] = mg
        mo_ref[...] = _dot(mg, wo_ref[...], 1, 0)

    slabs = lambda n: pl.BlockSpec((n, tm, LANES), lambda i: (0, i, 0))
    full = lambda r, c: pl.BlockSpec((r, c), lambda i: (0, 0))
    row = pl.BlockSpec((tm, D), lambda i: (i, 0))
    return pl.pallas_call(
        body, grid=(S // tm,),
        in_specs=[slabs(2), slabs(4), slabs(2), pl.BlockSpec((tm, GATE_COLS), lambda i: (i, 0)), full(3, D),
                  full(256, D), full(512, D), full(256, D), full(D, D)],
        out_specs=[row, row],
        out_shape=[jax.ShapeDtypeStruct((S, D), BF16), jax.ShapeDtypeStruct((S, D), F32)],
        compiler_params=_cp("parallel"), name=name)(o_a, o_b, o_c, gates, b_gate, wa, wb, wc, w_out)


def _merge_bwd(d_mo, o_a, o_b, o_c, gates, b_gate, wa, wb, wc, w_out, name):
    tm = ROW_TILE

    def body(dmo_ref, oa_ref, ob_ref, oc_ref, g_ref, bg_ref, wa_ref, wb_ref, wc_ref, wo_ref,
             doa_ref, dob_ref, doc_ref, dg_ref, dwa_ref, dwb_ref, dwc_ref, dbg_ref):
        @pl.when(pl.program_id(0) == 0)
        def _():
            dwa_ref[...] = jnp.zeros(dwa_ref.shape, F32)
            dwb_ref[...] = jnp.zeros(dwb_ref.shape, F32)
            dwc_ref[...] = jnp.zeros(dwc_ref.shape, F32)
            dbg_ref[...] = jnp.zeros(dbg_ref.shape, F32)

        dmg = _dot(dmo_ref[...], wo_ref[...], 1, 1)
        trip = ((oa_ref, wa_ref, doa_ref, dwa_ref), (ob_ref, wb_ref, dob_ref, dwb_ref), (oc_ref, wc_ref, doc_ref, dwc_ref))
        for i, (o_ref, w_ref, do_ref, dw_ref) in enumerate(trip):
            ob = _cat_slabs(o_ref).astype(BF16)
            pr = _dot(ob, w_ref[...], 1, 0)
            sg = jax.nn.sigmoid(g_ref[:, i * D:(i + 1) * D] + bg_ref[i:i + 1, :])
            dgate = dmg * pr * sg * (1.0 - sg)
            dg_ref[:, i * D:(i + 1) * D] = dgate.astype(BF16)
            dbg_ref[i:i + 1, :] += jnp.sum(dgate, axis=0, keepdims=True)
            dpr = (dmg * sg).astype(BF16)
            do = _dot(dpr, w_ref[...], 1, 1)
            for s in range(do_ref.shape[0]):
                do_ref[s] = do[:, s * LANES:(s + 1) * LANES]
            dw_ref[...] += _dot(ob, dpr, 0, 0)

    slabs = lambda n: pl.BlockSpec((n, tm, LANES), lambda i: (0, i, 0))
    full = lambda r, c: pl.BlockSpec((r, c), lambda i: (0, 0))
    row = pl.BlockSpec((tm, D), lambda i: (i, 0))
    return pl.pallas_call(
        body, grid=(S // tm,),
        in_specs=[row, slabs(2), slabs(4), slabs(2), pl.BlockSpec((tm, GATE_COLS), lambda i: (i, 0)), full(3, D),
                  full(256, D), full(512, D), full(256, D), full(D, D)],
        out_specs=[slabs(2), slabs(4), slabs(2), pl.BlockSpec((tm, GATE_COLS), lambda i: (i, 0)),
                   full(256, D), full(512, D), full(256, D), full(3, D)],
        out_shape=[jax.ShapeDtypeStruct((2, S, LANES), F32), jax.ShapeDtypeStruct((4, S, LANES), F32),
                   jax.ShapeDtypeStruct((2, S, LANES), F32), jax.ShapeDtypeStruct((S, GATE_COLS), BF16),
                   jax.ShapeDtypeStruct((256, D), F32), jax.ShapeDtypeStruct((512, D), F32),
                   jax.ShapeDtypeStruct((256, D), F32), jax.ShapeDtypeStruct((3, D), F32)],
        compiler_params=_cp("arbitrary"), name=name)(d_mo, o_a, o_b, o_c, gates, b_gate, wa, wb, wc, w_out)


FC = 256
GELU_K = math.sqrt(2.0 / math.pi)
GELU_C = 0.044715


RC = 64
NRC = S // RC


def _down(prev, cur, n):
    row = lax.broadcasted_iota(jnp.int32, cur.shape, 0)
    return jnp.where(row < n, pltpu.roll(prev, n, 0), pltpu.roll(cur, n, 0))


def _up(cur, nxt, n):
    row = lax.broadcasted_iota(jnp.int32, cur.shape, 0)
    return jnp.where(row >= RC - n, pltpu.roll(nxt, RC - n, 0), pltpu.roll(cur, RC - n, 0))


def _conv_chunk(load, j, w_ref, b_ref, half):
    cur = load(j)
    prev = jnp.where(j > 0, load(jnp.maximum(j - 1, 0)), 0.0)
    d1 = _down(prev, cur, 1)
    d2 = _down(prev, cur, 2)
    y = w_ref[0:1, half, :] * d2 + w_ref[1:2, half, :] * d1 + w_ref[2:3, half, :] * cur + b_ref[half:half + 1, :]
    return y, cur, d1, d2


def _chunk(j):
    return pl.ds(pl.multiple_of(j * RC, RC), RC)


def _fold8(x):
    return jnp.sum(x.reshape(RC // 8, 8, x.shape[-1]), axis=0)


def _ffn_act(u, conv_w, conv_b, name):
    def body(u_ref, w_ref, b_ref, a_ref):
        def step(j, carry):
            yg = _conv_chunk(lambda k: u_ref[0, _chunk(k), :], j, w_ref, b_ref, 0)[0]
            yv = _conv_chunk(lambda k: u_ref[1, _chunk(k), :], j, w_ref, b_ref, 1)[0]
            th = jnp.tanh(GELU_K * (yg + GELU_C * yg * yg * yg))
            a_ref[_chunk(j), :] = (0.5 * yg * (1.0 + th) * yv).astype(BF16)
            return carry

        lax.fori_loop(0, NRC, step, 0)

    return pl.pallas_call(
        body, grid=(D_FF // FC,),
        in_specs=[pl.BlockSpec((2, S, FC), lambda j: (0, 0, j)), pl.BlockSpec((3, 2, FC), lambda j: (0, 0, j)),
                  pl.BlockSpec((2, FC), lambda j: (0, j))],
        out_specs=pl.BlockSpec((S, FC), lambda j: (0, j)),
        out_shape=jax.ShapeDtypeStruct((S, D_FF), BF16),
        compiler_params=_cp("parallel"), name=name)(u, conv_w, conv_b)


def _ffn_act_bwd(u, d_a, conv_w, conv_b, name):
    def body(u_ref, da_ref, w_ref, b_ref, du_ref, dw_ref, db_ref, dy_s):
        def first(j, acc):
            yg, ug, ug1, ug2 = _conv_chunk(lambda k: u_ref[0, _chunk(k), :], j, w_ref, b_ref, 0)
            yv, uv, uv1, uv2 = _conv_chunk(lambda k: u_ref[1, _chunk(k), :], j, w_ref, b_ref, 1)
            th = jnp.tanh(GELU_K * (yg + GELU_C * yg * yg * yg))
            gelu = 0.5 * yg * (1.0 + th)
            dgelu = 0.5 * (1.0 + th) + 0.5 * yg * (1.0 - th * th) * GELU_K * (1.0 + 3.0 * GELU_C * yg * yg)
            da = da_ref[_chunk(j), :]
            dyg = da * yv * dgelu
            dyv = da * gelu
            dy_s[0, _chunk(j), :] = dyg
            dy_s[1, _chunk(j), :] = dyv
            new = (_fold8(dyg * ug2), _fold8(dyg * ug1), _fold8(dyg * ug), _fold8(dyg),
                   _fold8(dyv * uv2), _fold8(dyv * uv1), _fold8(dyv * uv), _fold8(dyv))
            return tuple(a + n for a, n in zip(acc, new))

        acc = lax.fori_loop(0, NRC, first, tuple(jnp.zeros((8, FC), F32) for _ in range(8)))
        for half in range(2):
            for k in range(3):
                dw_ref[k:k + 1, half, :] = jnp.sum(acc[4 * half + k], axis=0, keepdims=True)
            db_ref[half:half + 1, :] = jnp.sum(acc[4 * half + 3], axis=0, keepdims=True)

        def second(j, carry):
            for half in range(2):
                cur = dy_s[half, _chunk(j), :]
                nxt = jnp.where(j < NRC - 1, dy_s[half, _chunk(jnp.minimum(j + 1, NRC - 1)), :], 0.0)
                du = (w_ref[2:3, half, :] * cur + w_ref[1:2, half, :] * _up(cur, nxt, 1)
                      + w_ref[0:1, half, :] * _up(cur, nxt, 2))
                du_ref[half, _chunk(j), :] = du.astype(BF16)
            return carry

        lax.fori_loop(0, NRC, second, 0)

    return pl.pallas_call(
        body, grid=(D_FF // FC,),
        in_specs=[pl.BlockSpec((2, S, FC), lambda j: (0, 0, j)), pl.BlockSpec((S, FC), lambda j: (0, j)),
                  pl.BlockSpec((3, 2, FC), lambda j: (0, 0, j)), pl.BlockSpec((2, FC), lambda j: (0, j))],
        out_specs=[pl.BlockSpec((2, S, FC), lambda j: (0, 0, j)), pl.BlockSpec((3, 2, FC), lambda j: (0, 0, j)),
                   pl.BlockSpec((2, FC), lambda j: (0, j))],
        out_shape=[jax.ShapeDtypeStruct((2, S, D_FF), BF16), jax.ShapeDtypeStruct((3, 2, D_FF), F32),
                   jax.ShapeDtypeStruct((2, D_FF), F32)],
        scratch_shapes=[pltpu.VMEM((2, S, FC), F32)],
        compiler_params=_cp("parallel"), name=name)(u, d_a, conv_w, conv_b)


def _layer_fwd(x, h1, w, bias, lname):
    n = lambda s: f"{lname}_{s}"
    w.need("in", h1)
    tn = 768
    proj = _mm(h1, w["w_in"], grid=(S // 1024, QKV_COLS // tn, 1),
               a_spec=pl.BlockSpec((1024, D), lambda i, j, k: (i, 0)),
               b_spec=pl.BlockSpec((D, tn), lambda i, j, k: (0, j)),
               out_shape=jax.ShapeDtypeStruct((QKV_SLABS, S, LANES), F32),
               out_spec=pl.BlockSpec((tn // LANES, 1024, LANES), lambda i, j, k: (j, i, 0)),
               ca=1, cb=0, acc_shape=(1024, tn), out_slab=True, name=n("proj_qkv"))
    gates = _mm(h1, w["w_in"], grid=(S // 1024, GATE_COLS // tn, 1),
                a_spec=pl.BlockSpec((1024, D), lambda i, j, k: (i, 0)),
                b_spec=pl.BlockSpec((D, tn), lambda i, j, k: (0, j + QKV_COLS // tn)),
                out_shape=jax.ShapeDtypeStruct((S, GATE_COLS), F32),
                out_spec=pl.BlockSpec((1024, tn), lambda i, j, k: (i, j)),
                ca=1, cb=0, acc_shape=(1024, tn), name=n("proj_gate"))
    nums, stats = [], []
    for g, (_, d) in enumerate(A_GROUPS):
        nm, st = _band_fwd(proj, bias, d=d, q0=2 * g, k0=6 + 2 * g, v0=12 + 2 * g, npairs=2, bias0=2 * g,
                           shared_kv=False, name=n(f"attn_a{g}_fwd"))
        nums.append(nm)
        stats.append(st)
    o_a, lse_a = _combine_a(nums, stats, n("attn_a_combine"))
    nm_b, st_b = _band_fwd(proj, bias, d=1, q0=18, k0=22, v0=23, npairs=4, bias0=6, shared_kv=True, name=n("attn_b_fwd"))
    o_b, lse_b = _combine_b(nm_b, st_b, w["sinks"], n("attn_b_combine"))
    o_c, tot_c = _stick_fwd(proj, q0=24, k0=26, v0=28, name=n("attn_c_fwd"))
    w.need("mix", tot_c)
    merged, mo = _merge_fwd(o_a, o_b, o_c, gates, w["b_gate"], w["w_br_a"], w["w_br_b"], w["w_br_c"], w["w_out"], n("merge_fwd"))
    x2, h2 = _postnorm_res(x, mo, w["attn_post_norm"], w["ffn_pre_norm"], n("attn_post"))
    w.need("ffn", h2)
    u = _mm(h2, w["w_up"], grid=(S // 1024, 2 * D_FF // 1024, 1),
            a_spec=pl.BlockSpec((1024, D), lambda i, j, k: (i, 0)),
            b_spec=pl.BlockSpec((D, 1024), lambda i, j, k: (0, j)),
            out_shape=jax.ShapeDtypeStruct((2, S, D_FF), F32),
            out_spec=pl.BlockSpec((None, 1024, 1024), lambda i, j, k: (j // 4, i, j % 4)),
            ca=1, cb=0, acc_shape=(1024, 1024), name=n("ffn_up"))
    a = _ffn_act(u, w["conv_w"], w["conv_b"], n("ffn_act"))
    fo = _mm_nn(a, w["w_down"], F32, 1024, 1024, 1024, n("ffn_down"))
    saved = dict(x=x, h1=h1, proj=proj, gates=gates, o_a=o_a, lse_a=lse_a, o_b=o_b, lse_b=lse_b, o_c=o_c, tot_c=tot_c,
                 merged=merged, mo=mo, x2=x2, h2=h2, u=u, a=a, fo=fo)
    return saved


def _layer_bwd(dx3, sv, w, bias, lname, tok=None, on_part=None):
    n = lambda s: f"{lname}_{s}"
    g = {}

    def part(group, vec):
        t = on_part(group, g) if on_part is not None else None
        return vec if t is None else vec + t

    gain = w["ffn_post_norm"] if tok is None else w["ffn_post_norm"] + tok
    d_fo, g["ffn_post_norm"] = _norm_bwd(sv["fo"], gain, [dx3], None, BF16, n("ffn_post_bwd"))
    d_a = _mm_nt(d_fo, w["w_down"], F32, 1024, 1024, 1024, n("ffn_down_bwd_x"))
    g["w_down"] = _mm_tn(sv["a"], d_fo, BF16, 1024, 1024, 1024, n("ffn_down_bwd_w"))
    d_u, dcw, dcb = _ffn_act_bwd(sv["u"], d_a, w["conv_w"], w["conv_b"], n("ffn_act_bwd"))
    g["conv_w"] = dcw.reshape(3, 2 * D_FF)
    g["conv_b"] = dcb.reshape(1, 2 * D_FF)
    g["w_up"] = _mm(sv["h2"], d_u, grid=(1, 2 * D_FF // 1024, S // 1024),
                    a_spec=pl.BlockSpec((1024, D), lambda i, j, k: (k, 0)),
                    b_spec=pl.BlockSpec((None, 1024, 1024), lambda i, j, k: (j // 4, k, j % 4)),
                    out_shape=jax.ShapeDtypeStruct((D, 2 * D_FF), BF16),
                    out_spec=pl.BlockSpec((D, 1024), lambda i, j, k: (0, j)),
                    ca=0, cb=0, acc_shape=(D, 1024), name=n("ffn_up_bwd_w"))
    d_h2 = _mm(d_u, w["w_up"], grid=(S // 1024, 1, 2 * D_FF // 1024),
               a_spec=pl.BlockSpec((None, 1024, 1024), lambda i, j, k: (k // 4, i, k % 4)),
               b_spec=pl.BlockSpec((D, 1024), lambda i, j, k: (0, k)),
               out_shape=jax.ShapeDtypeStruct((S, D), F32),
               out_spec=pl.BlockSpec((1024, D), lambda i, j, k: (i, 0)),
               ca=1, cb=1, acc_shape=(1024, D), name=n("ffn_up_bwd_x"))
    dx2, g["ffn_pre_norm"] = _norm_bwd(sv["x2"], part("ffn", w["ffn_pre_norm"]), [d_h2], dx3, F32, n("ffn_pre_bwd"))
    d_mo, g["attn_post_norm"] = _norm_bwd(sv["mo"], w["attn_post_norm"], [dx2], None, BF16, n("attn_post_bwd"))
    g["w_out"] = _mm_tn(sv["merged"], d_mo, BF16, 1024, 1024, 1024, n("out_bwd_w"))
    do_a, do_b, do_c, d_gates, dwa, dwb, dwc, g["b_gate"] = _merge_bwd(
        d_mo, sv["o_a"], sv["o_b"], sv["o_c"], sv["gates"], w["b_gate"], w["w_br_a"], w["w_br_b"], w["w_br_c"],
        w["w_out"], n("merge_bwd"))
    g["w_br_a"], g["w_br_b"], g["w_br_c"] = dwa, dwb, dwc
    sinks = part("mix", w["sinks"])
    proj = sv["proj"]
    dqa, dka, dva, gbias = [], [], [], []
    for gi, (_, d) in enumerate(A_GROUPS):
        dq, dk, dv, gg, _ = _band_bwd(proj, bias, sv["o_a"], do_a, sv["lse_a"], sinks, d=d, q0=2 * gi, k0=6 + 2 * gi,
                                      v0=12 + 2 * gi, npairs=2, bias0=2 * gi, shared_kv=False, name=n(f"attn_a{gi}_bwd"))
        dqa.append(dq), dka.append(dk), dva.append(dv), gbias.append(gg)
    dqb, dkb, dvb, ggb, dsink = _band_bwd(proj, bias, sv["o_b"], do_b, sv["lse_b"], sinks, d=1, q0=18, k0=22, v0=23,
                                          npairs=4, bias0=6, shared_kv=True, name=n("attn_b_bwd"))
    gbias.append(ggb)
    g["bias_g"] = jnp.concatenate(gbias, axis=0).reshape(N_BIAS_HEADS, BLK, 2 * BLK)
    g["sinks"] = dsink[:, 0, :2].reshape(1, 8)
    dqc, dkc, dvc = _stick_bwd(proj, do_c, sv["tot_c"], q0=24, k0=26, v0=28, name=n("attn_c_bwd"))
    dqkv = jnp.concatenate(dqa + dka + dva + [dqb, dkb, dvb, dqc, dkc, dvc], axis=0)
    ts = 6
    d_h1a = _mm(dqkv, w["w_in"], grid=(S // 1024, 1, QKV_SLABS // ts),
                a_spec=pl.BlockSpec((ts, 1024, LANES), lambda i, j, k: (k, i, 0)),
                b_spec=pl.BlockSpec((D, ts * LANES), lambda i, j, k: (0, k)),
                out_shape=jax.ShapeDtypeStruct((S, D), F32),
                out_spec=pl.BlockSpec((1024, D), lambda i, j, k: (i, 0)),
                ca=1, cb=1, acc_shape=(1024, D), a_slab=True, name=n("in_bwd_x_qkv"))
    d_h1b = _mm(d_gates, w["w_in"], grid=(S // 1024, 1, GATE_COLS // 768),
                a_spec=pl.BlockSpec((1024, 768), lambda i, j, k: (i, k)),
                b_spec=pl.BlockSpec((D, 768), lambda i, j, k: (0, k + QKV_COLS // 768)),
                out_shape=jax.ShapeDtypeStruct((S, D), F32),
                out_spec=pl.BlockSpec((1024, D), lambda i, j, k: (i, 0)),
                ca=1, cb=1, acc_shape=(1024, D), name=n("in_bwd_x_gate"))
    dw_in = _mm(sv["h1"], dqkv, grid=(1, QKV_SLABS // ts, S // 1024),
                a_spec=pl.BlockSpec((1024, D), lambda i, j, k: (k, 0)),
                b_spec=pl.BlockSpec((ts, 1024, LANES), lambda i, j, k: (j, k, 0)),
                out_shape=jax.ShapeDtypeStruct((D, IN_COLS), BF16),
                out_spec=pl.BlockSpec((D, ts * LANES), lambda i, j, k: (0, j)),
                ca=0, cb=0, acc_shape=(D, ts * LANES), b_slab=True, name=n("in_bwd_w_qkv"))
    g["w_in"] = _mm(sv["h1"], d_gates, grid=(1, GATE_COLS // 768, S // 1024),
                    a_spec=pl.BlockSpec((1024, D), lambda i, j, k: (k, 0)),
                    b_spec=pl.BlockSpec((1024, 768), lambda i, j, k: (k, j)),
                    out_shape=jax.ShapeDtypeStruct((D, IN_COLS), BF16),
                    out_spec=pl.BlockSpec((D, 768), lambda i, j, k: (0, j + QKV_COLS // 768)),
                    ca=0, cb=0, acc_shape=(D, 768), alias_out=dw_in, name=n("in_bwd_w_gate"))
    dx, g["attn_pre_norm"] = _norm_bwd(sv["x"], w["attn_pre_norm"], [d_h1a, d_h1b], dx2, F32, n("attn_pre_bwd"))
    tok_in = on_part("in", g) if on_part is not None else None
    return dx, g, tok_in


def _local_step(x, target, ws, rel_bias, tok=None, on_grads=None):
    buckets = jnp.asarray(_bucket_tiles())
    bias = _bias_tiles(rel_bias, buckets, "bias_tiles").reshape(N_BIAS_HEADS // 2, 2, 2, BLK, 2 * BLK)
    saved = []
    gain0 = ws[0]["attn_pre_norm"] if tok is None else ws[0]["attn_pre_norm"] + tok
    h1 = _prenorm(x, gain0, "l0_attn_pre")
    for l in range(DEPTH):
        sv = _layer_fwd(x, h1, ws[l], bias, f"l{l}")
        saved.append(sv)
        g_next = ws[l + 1]["attn_pre_norm"] if l + 1 < DEPTH else ws[l]["attn_pre_norm"]
        x, h1 = _postnorm_res(sv["x2"], sv["fo"], ws[l]["ffn_post_norm"], g_next, f"l{l}_ffn_post")
    dy, loss_tile = _loss_head(x, target, "loss_head")
    grads = [None] * DEPTH
    tok = None
    for l in reversed(range(DEPTH)):
        on_part = None if on_grads is None else functools.partial(on_grads, l)
        dy, grads[l], tok = _layer_bwd(dy, saved[l], ws[l], bias, f"l{l}", tok, on_part)
    g_rel = _bias_grad([grads[l]["bias_g"] for l in range(DEPTH)], buckets, "bias_grad")[:, :N_BIAS_HEADS]
    return loss_tile[0, 0], dy, grads, g_rel


def _coords():
    return lax.axis_index("x"), lax.axis_index("y"), lax.axis_index("c")


def _peer(rel):
    x, y, c = _coords()
    return (1 - x if rel & 4 else x, 1 - y if rel & 2 else y, 1 - c if rel & 1 else c)


def _exchange(srcs, dst_shapes, src_win, dst_win, name):
    nt = len(srcs)

    def body(*refs):
        src_refs, dst_refs = refs[:nt], refs[nt:2 * nt]
        send_sems, recv_sems, local_sems = refs[2 * nt:]
        x, y, c = _coords()
        me = 4 * x + 2 * y + c
        locals_ = []
        for t in range(nt):
            cp = pltpu.make_async_copy(src_win(t, src_refs[t], me), dst_win(t, dst_refs[t], me), local_sems.at[t])
            cp.start()
            locals_.append(cp)
        sends = []
        for rel in range(1, NDEV):
            px, py, pc = _peer(rel)
            q = 4 * px + 2 * py + pc
            for t in range(nt):
                cp = pltpu.make_async_remote_copy(
                    src_ref=src_win(t, src_refs[t], q), dst_ref=dst_win(t, dst_refs[t], me),
                    send_sem=send_sems.at[rel - 1, t], recv_sem=recv_sems.at[rel - 1, t],
                    device_id=(px, py, pc), device_id_type=MESH)
                cp.start()
                sends.append(cp)
        for rel in range(1, NDEV):
            px, py, pc = _peer(rel)
            q = 4 * px + 2 * py + pc
            for t in range(nt):
                pltpu.make_async_remote_copy(
                    src_ref=src_win(t, src_refs[t], me), dst_ref=dst_win(t, dst_refs[t], q),
                    send_sem=send_sems.at[rel - 1, t], recv_sem=recv_sems.at[rel - 1, t],
                    device_id=(px, py, pc), device_id_type=MESH).wait_recv()
        for cp in sends:
            cp.wait_send()
        for cp in locals_:
            cp.wait()

    return pl.pallas_call(
        body, in_specs=[ANY] * nt, out_specs=[ANY] * nt, out_shape=dst_shapes,
        scratch_shapes=[pltpu.SemaphoreType.DMA((NDEV - 1, nt)), pltpu.SemaphoreType.DMA((NDEV - 1, nt)),
                        pltpu.SemaphoreType.DMA((nt,))],
        name=name)(*srcs)


BIG = (("w_in", 1, 864), ("w_br_a", 1, 128), ("w_br_b", 1, 128), ("w_br_c", 1, 128), ("w_out", 0, 128),
       ("w_up", 1, 1024), ("w_down", 0, 512))


NBIG = len(BIG)
BIG_FULL = {"w_in": (D, IN_COLS), "w_br_a": (256, D), "w_br_b": (512, D), "w_br_c": (256, D), "w_out": (D, D),
            "w_up": (D, 2 * D_FF), "w_down": (D_FF, D)}
LAYER_GROUPS = (("in", (0,)), ("mix", (1, 2, 3, 4)), ("ffn", (5, 6)))

HBM_SPEC = pl.BlockSpec(memory_space=pltpu.HBM)
SEM_SPEC = pl.BlockSpec(memory_space=pltpu.SEMAPHORE)


def _hbm(a):
    return pltpu.with_memory_space_constraint(a, pltpu.HBM)


def _shard_window(t, ref, k):
    nm, ax, ext = BIG[t % NBIG]
    if nm == "w_in":
        return ref.at[k]
    off = pl.multiple_of(k * ext, ext)
    if ax == 0:
        return ref.at[pl.ds(off, ext), :]
    return ref.at[:, pl.ds(off, ext)]


def _whole(t, ref, k):
    return ref


def _slot(t, ref, k):
    return ref.at[k]


def _own_block_spec(t, rows, me_of):
    nm, ax, ext = BIG[t % NBIG]
    r, c = BIG_FULL[nm]
    if nm == "w_in":
        return pl.BlockSpec((None, rows, ext), lambda i, m: (me_of(m), i, 0))
    if ax == 0:
        return pl.BlockSpec((rows, c), lambda i, m: (me_of(m) * (ext // rows) + i, 0))
    return pl.BlockSpec((rows, ext), lambda i, m: (i, me_of(m)))


def _cast_own(t, shard, me_arr, name):
    nm, ax, ext = BIG[t % NBIG]
    nr, nc = shard.shape
    rows = min(nr, 256)
    shape = (NDEV, D, ext) if nm == "w_in" else BIG_FULL[nm]

    def body(m_ref, s_ref, o_ref):
        o_ref[...] = s_ref[...].astype(BF16)

    return pl.pallas_call(
        body, grid_spec=pltpu.PrefetchScalarGridSpec(
            num_scalar_prefetch=1, grid=(nr // rows,),
            in_specs=[pl.BlockSpec((rows, nc), lambda i, m: (i, 0))],
            out_specs=_own_block_spec(t, rows, lambda m: m[0])),
        out_shape=jax.ShapeDtypeStruct(shape, BF16), compiler_params=_cp("arbitrary"), name=name)(me_arr, shard)


def _xchg_start(srcs, lands, groups, src_win, dst_win, after, name):
    ns = 0 if srcs is None else len(srcs)
    nt, ng = len(lands), len(groups)
    ins = ([] if srcs is None else list(srcs)) + list(lands)

    def body(*refs):
        src_refs, land_refs = refs[:ns], refs[ns:ns + nt]
        sems = refs[ns + nt + 1:ns + nt + 1 + 2 * ng]
        token = refs[-1]
        x, y, c = _coords()
        me = 4 * x + 2 * y + c
        for gi, grp in enumerate(groups):
            for j, t in enumerate(grp):
                for rel in range(1, NDEV):
                    px, py, pc = _peer(rel)
                    q = 4 * px + 2 * py + pc
                    src = dst_win(t, land_refs[t], me) if srcs is None else src_win(t, src_refs[t], q)
                    pltpu.make_async_remote_copy(
                        src_ref=src, dst_ref=dst_win(t, land_refs[t], me),
                        send_sem=sems[2 * gi].at[(rel - 1) * len(grp) + j],
                        recv_sem=sems[2 * gi + 1].at[(rel - 1) * len(grp) + j],
                        device_id=(px, py, pc), device_id_type=MESH).start()
        token[...] = jnp.zeros((8, LANES), F32)

    out_shape = []
    for grp in groups:
        out_shape += [pltpu.SemaphoreType.DMA(((NDEV - 1) * len(grp),))] * 2
    out_shape += [pltpu.HBM(a.shape, a.dtype) for a in ins]
    out_shape.append(jax.ShapeDtypeStruct((8, LANES), F32))
    outs = pl.pallas_call(
        body, in_specs=[HBM_SPEC] * len(ins) + [ANY],
        out_specs=[SEM_SPEC] * (2 * ng) + [HBM_SPEC] * len(ins) + [pl.BlockSpec(memory_space=pltpu.VMEM)],
        out_shape=out_shape, input_output_aliases={i: 2 * ng + i for i in range(len(ins))},
        compiler_params=pltpu.CompilerParams(has_side_effects=pltpu.SideEffectType.DATAFLOW_SIDE_EFFECTING),
        name=name)(*[_hbm(a) for a in ins], after)
    sems = [(outs[2 * gi], outs[2 * gi + 1]) for gi in range(ng)]
    thru = list(outs[2 * ng:2 * ng + len(ins)])
    return sems, (None if srcs is None else thru[:ns]), thru[ns:], outs[-1]


def _xchg_wait(sems, srcs, lands, tids, after, src_win, dst_win, name):
    ns = 0 if srcs is None else len(srcs)
    n = len(lands)
    send_sem, recv_sem = sems
    ins = ([] if srcs is None else list(srcs)) + list(lands)

    def body(*refs):
        src_refs, land_refs = refs[:ns], refs[ns:ns + n]
        ssem, rsem = refs[ns + n], refs[ns + n + 1]
        x, y, c = _coords()
        me = 4 * x + 2 * y + c
        for j, t in enumerate(tids):
            for rel in range(1, NDEV):
                px, py, pc = _peer(rel)
                q = 4 * px + 2 * py + pc
                src = dst_win(t, land_refs[j], me) if srcs is None else src_win(t, src_refs[j], q)
                cp = pltpu.make_async_remote_copy(
                    src_ref=src, dst_ref=dst_win(t, land_refs[j], q),
                    send_sem=ssem.at[(rel - 1) * n + j], recv_sem=rsem.at[(rel - 1) * n + j],
                    device_id=(px, py, pc), device_id_type=MESH)
                cp.wait_send()
                cp.wait_recv()

    outs = pl.pallas_call(
        body, in_specs=[HBM_SPEC] * len(ins) + [SEM_SPEC, SEM_SPEC, ANY], out_specs=[HBM_SPEC] * len(ins),
        out_shape=[pltpu.HBM(a.shape, a.dtype) for a in ins],
        input_output_aliases={i: i for i in range(len(ins))},
        compiler_params=pltpu.CompilerParams(has_side_effects=pltpu.SideEffectType.DATAFLOW_SIDE_EFFECTING),
        name=name)(*ins, send_sem, recv_sem, after)
    return (None if srcs is None else list(outs[:ns])), list(outs[ns:])


class _Weights:
    def __init__(self, ready, pending=None):
        self.ready = dict(ready)
        self.pending = dict(pending or {})

    def __getitem__(self, k):
        return self.ready[k]

    def need(self, group, after):
        fn = self.pending.pop(group, None)
        if fn is not None:
            self.ready.update(fn(after))


def _adamw_math(w, g, m, v):
    m2 = ADAM_B1 * m + (1.0 - ADAM_B1) * g
    v2 = ADAM_B2 * v + (1.0 - ADAM_B2) * (g * g)
    m_hat = m2 / (1.0 - ADAM_B1 ** ADAM_STEP)
    v_hat = v2 / (1.0 - ADAM_B2 ** ADAM_STEP)
    delta = -ADAM_LR * (m_hat / (jnp.sqrt(v_hat) + ADAM_EPS) + ADAM_WD * w)
    return delta, m2, v2


def _adamw(t, parts, own, me_arr, w, m, v, layer, prev, rows, name):
    nl, nr, nc = w.shape

    def body(me_ref, p_ref, own_ref, w_ref, m_ref, v_ref, *rest):
        g_ref, d_ref, m2_ref, v2_ref = rest[-4:]
        me = me_ref[0]
        g = None
        for k in range(NDEV):
            term = jnp.where(me == k, own_ref[...], p_ref[k]).astype(F32)
            g = term if g is None else g + term
        delta, m2, v2 = _adamw_math(w_ref[...], g, m_ref[...], v_ref[...])
        g_ref[...] = g
        d_ref[...] = delta
        m2_ref[...] = m2
        v2_ref[...] = v2

    blk = pl.BlockSpec((None, rows, nc), lambda i, mm: (layer, i, 0))
    pblk = pl.BlockSpec((NDEV, rows, nc), lambda i, mm: (0, i, 0))
    extra = [] if prev is None else list(prev)
    return pl.pallas_call(
        body, grid_spec=pltpu.PrefetchScalarGridSpec(
            num_scalar_prefetch=1, grid=(nr // rows,),
            in_specs=[pblk, _own_block_spec(t, rows, lambda mm: mm[0]), blk, blk, blk] + [ANY] * len(extra),
            out_specs=[blk] * 4),
        out_shape=[jax.ShapeDtypeStruct(w.shape, F32)] * 4,
        input_output_aliases={6 + k: k for k in range(len(extra))},
        compiler_params=_cp("arbitrary"), name=name)(me_arr, parts, own, w, m, v, *extra)


SMALL_REPL = (("rel_bias", NUM_BUCKETS * N_BIAS_HEADS), ("attn_pre_norm", DEPTH * D), ("sinks", DEPTH * 8),
              ("attn_post_norm", DEPTH * D), ("ffn_pre_norm", DEPTH * D), ("conv_b", DEPTH * 2 * D_FF),
              ("ffn_post_norm", DEPTH * D))
SMALL_SHARD = (("b_gate", (DEPTH, 3, D), 128), ("conv_w", (DEPTH, 3, 2 * D_FF), 1024))


def _pack(vecs):
    flat = jnp.concatenate([v.reshape(-1).astype(F32) for v in vecs])
    n = flat.shape[0]
    rows = -(-n // (8 * LANES)) * 8
    return jnp.pad(flat, (0, rows * LANES - n)).reshape(rows, LANES)


def _unpack(packed, sizes):
    flat = packed.reshape(-1)
    out, off = [], 0
    for sz in sizes:
        out.append(flat[off:off + sz])
        off += sz
    return out


ROWPACK = (("rel_bias", 32, 32, (NUM_BUCKETS, N_BIAS_HEADS)), ("sinks", 8, 8, (DEPTH, 8)),
           ("attn_pre_norm", 16, 16, (DEPTH, D)), ("attn_post_norm", 16, 16, (DEPTH, D)),
           ("ffn_pre_norm", 16, 16, (DEPTH, D)), ("ffn_post_norm", 16, 16, (DEPTH, D)),
           ("conv_b", 128, 128, (DEPTH, 2 * D_FF)), ("b_gate", 48, 8, (DEPTH, 3, 128)),
           ("conv_w", 384, 48, (DEPTH, 3, 1024)))
ROWS_FULL = sum(r for _, r, _, _ in ROWPACK)
ROWS_OWN = sum(r for _, _, r, _ in ROWPACK)


def _as_rows(a, rows):
    a = a.astype(F32)
    if a.shape[-1] < LANES:
        a = jnp.pad(a.reshape(-1, a.shape[-1]), ((0, 0), (0, LANES - a.shape[-1])))
    a = a.reshape(-1, LANES)
    return jnp.pad(a, ((0, rows - a.shape[0]), (0, 0)))


def _rowpack(arrs, own):
    return jnp.concatenate([_as_rows(arrs[nm], ro if own else rf) for nm, rf, ro, _ in ROWPACK], axis=0)


def _small_update(parts, w, m, v, me_arr, name):
    nsm = len(ROWPACK)

    def body(me_ref, p_ref, w_ref, m_ref, v_ref, *rest):
        outs = rest[:4 * nsm]
        gfull, g_s, d_s, m_s, v_s = rest[4 * nsm:]
        me = me_ref[0]
        g = p_ref[0]
        for k in range(1, NDEV):
            g = g + p_ref[k]
        gfull[...] = g
        of, oo = 0, 0
        for nm, rf, ro, _ in ROWPACK:
            if nm == "b_gate":
                g_s[oo:oo + ro, :] = jnp.zeros((ro, LANES), F32)
                for r in range(DEPTH * 3):
                    g_s[oo + r:oo + r + 1, :] = gfull[pl.ds(of + r * NDEV + me, 1), :]
            elif nm == "conv_w":
                for r in range(DEPTH * 3):
                    g_s[oo + r * 8:oo + r * 8 + 8, :] = gfull[pl.ds(pl.multiple_of(of + r * 64 + me * 8, 8), 8), :]
            else:
                g_s[oo:oo + ro, :] = gfull[of:of + rf, :]
            of, oo = of + rf, oo + ro
        delta, m2, v2 = _adamw_math(w_ref[...], g_s[...], m_ref[...], v_ref[...])
        d_s[...] = delta
        m_s[...] = m2
        v_s[...] = v2
        for kind, src in enumerate((g_s, d_s, m_s, v_s)):
            oo = 0
            for idx, (nm, rf, ro, shp) in enumerate(ROWPACK):
                o_ref = outs[kind * nsm + idx]
                if nm in ("rel_bias", "sinks"):
                    o_ref[...] = src[oo:oo + shp[0], 0:shp[1]]
                elif nm == "b_gate":
                    for l in range(DEPTH):
                        o_ref[l] = src[oo + 3 * l:oo + 3 * l + 3, :]
                elif nm == "conv_w":
                    for l in range(DEPTH):
                        for k in range(8):
                            o_ref[l, :, k * LANES:(k + 1) * LANES] = src[pl.ds(oo + 24 * l + k, 3, stride=8), :]
                else:
                    per = shp[1] // LANES
                    for k in range(per):
                        o_ref[:, k * LANES:(k + 1) * LANES] = src[pl.ds(oo + k, DEPTH, stride=per), :]
                oo += ro

    vm = pl.BlockSpec(memory_space=pltpu.VMEM)
    shapes = [jax.ShapeDtypeStruct(shp, F32) for _ in range(4) for _, _, _, shp in ROWPACK]
    outs = pl.pallas_call(
        body, in_specs=[SMEM, vm, vm, vm, vm], out_specs=[vm] * (4 * nsm), out_shape=shapes,
        scratch_shapes=[pltpu.VMEM((ROWS_FULL, LANES), F32)] + [pltpu.VMEM((ROWS_OWN, LANES), F32)] * 4,
        name=name)(me_arr, parts, w, m, v)
    names = [nm for nm, _, _, _ in ROWPACK]
    return [dict(zip(names, outs[kind * nsm:(kind + 1) * nsm])) for kind in range(4)]


def kernel(x, rel_bias, attn_pre_norm, w_in, b_gate, sinks, w_br_a, w_br_b, w_br_c, w_out, attn_post_norm, ffn_pre_norm, w_up, conv_w, conv_b, w_down, ffn_post_norm, loss_target, m_rel_bias, m_attn_pre_norm, m_w_in, m_b_gate, m_sinks, m_w_br_a, m_w_br_b, m_w_br_c, m_w_out, m_attn_post_norm, m_ffn_pre_norm, m_w_up, m_conv_w, m_conv_b, m_w_down, m_ffn_post_norm, v_rel_bias, v_attn_pre_norm, v_w_in, v_b_gate, v_sinks, v_w_br_a, v_w_br_b, v_w_br_c, v_w_out, v_attn_post_norm, v_ffn_pre_norm, v_w_up, v_conv_w, v_conv_b, v_w_down, v_ffn_post_norm):
    P = dict(rel_bias=rel_bias, attn_pre_norm=attn_pre_norm, w_in=w_in, b_gate=b_gate, sinks=sinks, w_br_a=w_br_a,
             w_br_b=w_br_b, w_br_c=w_br_c, w_out=w_out, attn_post_norm=attn_post_norm, ffn_pre_norm=ffn_pre_norm,
             w_up=w_up, conv_w=conv_w, conv_b=conv_b, w_down=w_down, ffn_post_norm=ffn_post_norm)
    M = dict(rel_bias=m_rel_bias, attn_pre_norm=m_attn_pre_norm, w_in=m_w_in, b_gate=m_b_gate, sinks=m_sinks,
             w_br_a=m_w_br_a, w_br_b=m_w_br_b, w_br_c=m_w_br_c, w_out=m_w_out, attn_post_norm=m_attn_post_norm,
             ffn_pre_norm=m_ffn_pre_norm, w_up=m_w_up, conv_w=m_conv_w, conv_b=m_conv_b, w_down=m_w_down,
             ffn_post_norm=m_ffn_post_norm)
    V = dict(rel_bias=v_rel_bias, attn_pre_norm=v_attn_pre_norm, w_in=v_w_in, b_gate=v_b_gate, sinks=v_sinks,
             w_br_a=v_w_br_a, w_br_b=v_w_br_b, w_br_c=v_w_br_c, w_out=v_w_out, attn_post_norm=v_attn_post_norm,
             ffn_pre_norm=v_ffn_pre_norm, w_up=v_w_up, conv_w=v_conv_w, conv_b=v_conv_b, w_down=v_w_down,
             ffn_post_norm=v_ffn_post_norm)
    xi, yi, ci = _coords()
    me = 4 * xi + 2 * yi + ci

    me_arr = me.astype(jnp.int32).reshape(1)

    small_w = _pack([b_gate.reshape(-1), conv_w.reshape(-1)])
    (small_w_all,) = _exchange([small_w], [jax.ShapeDtypeStruct((NDEV,) + small_w.shape, F32)],
                               _whole, _slot, "gather_small_weights")

    lands = [_cast_own(l * NBIG + t, P[nm][l], me_arr, f"gather_own_l{l}_{nm}")
             for l in range(DEPTH) for t, (nm, _, _) in enumerate(BIG)]
    groups = [tuple(l * NBIG + t for t in tids) for l in range(DEPTH) for _, tids in LAYER_GROUPS]
    g_sems, _, g_lands, g_tok = _xchg_start(None, lands, groups, None, _shard_window, small_w_all, "gather_start")
    tok0 = g_tok[0:1, 0:1]

    def gather_waiter(gi, l, gname, tids):
        def wait(after):
            ids = [l * NBIG + t for t in tids]
            _, got = _xchg_wait(g_sems[gi], None, [g_lands[i] for i in ids], ids, after,
                                None, _shard_window, f"gather_wait_l{l}_{gname}")
            out = {}
            for t, arr in zip(tids, got):
                nm = BIG[t][0]
                out[nm] = jnp.transpose(arr, (1, 0, 2)).reshape(D, IN_COLS) if nm == "w_in" else arr
            return out
        return wait

    pending = [{gname: gather_waiter(l * len(LAYER_GROUPS) + k, l, gname, tids)
                for k, (gname, tids) in enumerate(LAYER_GROUPS)} for l in range(DEPTH)]
    nbg = DEPTH * 3 * 128
    ncw = DEPTH * 3 * 1024
    flat_all = small_w_all.reshape(NDEV, -1)
    b_gate_full = jnp.transpose(flat_all[:, :nbg].reshape(NDEV, DEPTH, 3, 128), (1, 2, 0, 3)).reshape(DEPTH, 3, D)
    conv_w_full = jnp.transpose(flat_all[:, nbg:nbg + ncw].reshape(NDEV, DEPTH, 3, 1024), (1, 2, 0, 3)).reshape(DEPTH, 3, 2 * D_FF)

    ws = []
    for l in range(DEPTH):
        ws.append(_Weights(dict(
            b_gate=b_gate_full[l], conv_w=conv_w_full[l].reshape(3, 2, D_FF), conv_b=conv_b[l].reshape(2, D_FF),
            sinks=sinks[l].reshape(1, 8),
            attn_pre_norm=attn_pre_norm[l].reshape(1, D), attn_post_norm=attn_post_norm[l].reshape(1, D),
            ffn_pre_norm=ffn_pre_norm[l].reshape(1, D), ffn_post_norm=ffn_post_norm[l].reshape(1, D)), pending[l]))

    rs = {}

    group_tids = dict(LAYER_GROUPS)

    def start_scatter(l, gname, grads_l):
        tids = group_tids[gname]
        blocks, lands_rs = [], []
        for t in tids:
            nm, ax, ext = BIG[t]
            gfull = grads_l[nm].astype(BF16)
            if nm == "w_in":
                gfull = jnp.transpose(gfull.reshape(D, NDEV, ext), (1, 0, 2))
                shp = (NDEV, D, ext)
            else:
                shp = (NDEV, ext, gfull.shape[1]) if ax == 0 else (NDEV, gfull.shape[0], ext)
            blocks.append(gfull)
            lands_rs.append(lax.empty(shp, BF16))
        local = list(range(len(tids)))
        win = lambda j, ref, k: _shard_window(tids[j], ref, k)
        sems, s_thru, l_thru, tok = _xchg_start(blocks, lands_rs, [tuple(local)], win, _slot, me_arr,
                                                f"scatter_start_l{l}_{gname}")
        rs[(l, gname)] = (sems[0], s_thru, l_thru, win, local)
        return tok[0:1, 0:1]

    loss_local, grad_x, grads, g_rel = _local_step(x[0], loss_target[0], ws, rel_bias, tok0, start_scatter)
    loss = lax.psum(loss_local, ("x", "y", "c"))

    stack = lambda nm: jnp.stack([grads[l][nm] for l in range(DEPTH)], axis=0)
    small_names = [nm for nm, _ in SMALL_REPL] + [nm for nm, _, _ in SMALL_SHARD]
    small_g = {"rel_bias": g_rel}
    for nm in small_names[1:]:
        small_g[nm] = stack(nm)
    small_packed = _rowpack(small_g, own=False)
    (small_parts,) = _exchange([small_packed], [jax.ShapeDtypeStruct((NDEV,) + small_packed.shape, F32)],
                               _whole, _slot, "gather_small_grads")

    out_g, out_d, out_m, out_v = {}, {}, {}, {}
    prev = {nm: None for nm, _, _ in BIG}
    for l in reversed(range(DEPTH)):
        for gname in ("ffn", "mix", "in"):
            sems, s_thru, l_thru, win, local = rs[(l, gname)]
            owns, parts = _xchg_wait(sems, s_thru, l_thru, local, small_parts, win, _slot, f"scatter_wait_l{l}_{gname}")
            for t, own, prt in zip(group_tids[gname], owns, parts):
                nm = BIG[t][0]
                rows = {"w_in": 256, "w_up": 256, "w_down": 256}.get(nm, P[nm].shape[1])
                prev[nm] = _adamw(t, prt, own, me_arr, P[nm], M[nm], V[nm], l, prev[nm], rows, f"adamw_{nm}_l{l}")
    for nm, _, _ in BIG:
        out_g[nm], out_d[nm], out_m[nm], out_v[nm] = prev[nm]
    sm_g, sm_d, sm_m, sm_v = _small_update(small_parts, _rowpack(P, True), _rowpack(M, True), _rowpack(V, True),
                                           me_arr, "small_update")
    for dst, src in ((out_g, sm_g), (out_d, sm_d), (out_m, sm_m), (out_v, sm_v)):
        dst.update(src)

    order = ["rel_bias", "attn_pre_norm", "w_in", "b_gate", "sinks", "w_br_a", "w_br_b", "w_br_c", "w_out",
             "attn_post_norm", "ffn_pre_norm", "w_up", "conv_w", "conv_b", "w_down", "ffn_post_norm"]
    return (loss, grad_x[None], *[out_g[k] for k in order], *[out_d[k] for k in order],
            *[out_m[k] for k in order], *[out_v[k] for k in order])
```

```python
import functools
import math

import numpy as np
import jax
import jax.numpy as jnp
from jax import lax
from jax.experimental import pallas as pl
from jax.experimental.pallas import tpu as pltpu

F32 = jnp.float32
BF16 = jnp.bfloat16

S = 2048
D = 1024
DEPTH = 2
NDEV = 8
HD = 64
BLK = 128
NB = S // BLK
A_GROUPS = ((128, 1), (512, 4), (2048, 16))
NUM_BUCKETS = 32
MAX_DISTANCE = 2048
N_BIAS_HEADS = 20
D_FF = 4096
IN_COLS = 6912
QKV_COLS = 3840
QKV_SLABS = QKV_COLS // 128
GATE_COLS = 3072
EPS = 1e-6
SCALE = HD ** -0.5
NEG = -1e30
LANES = 128

ADAM_LR = 0.001
ADAM_B1 = 0.9
ADAM_B2 = 0.999
ADAM_EPS = 1e-08
ADAM_WD = 0.01
ADAM_STEP = 10

VMEM_LIMIT = 56 * 1024 * 1024
MESH = pl.DeviceIdType.MESH
ANY = pl.BlockSpec(memory_space=pl.ANY)
SMEM = pl.BlockSpec(memory_space=pltpu.SMEM)


def _cp(*sem):
    return pltpu.CompilerParams(dimension_semantics=sem if sem else None, vmem_limit_bytes=VMEM_LIMIT)


def _dot(a, b, ca, cb):
    return lax.dot_general(a, b, (((ca,), (cb,)), ((), ())), preferred_element_type=F32)


def _mm(a, b, *, grid, a_spec, b_spec, out_shape, out_spec, ca, cb, acc_shape, name,
        a_slab=False, b_slab=False, out_slab=False, alias_out=None):
    nk = grid[2]

    def body(*refs):
        if alias_out is not None:
            a_ref, b_ref, _, o_ref, acc_ref = refs
        else:
            a_ref, b_ref, o_ref, acc_ref = refs
        k = pl.program_id(2)

        def load(ref, slab):
            if slab:
                return jnp.concatenate([ref[s] for s in range(ref.shape[0])], axis=1).astype(BF16)
            return ref[...].astype(BF16)

        def write(val):
            if out_slab:
                for s in range(o_ref.shape[0]):
                    o_ref[s] = val[:, s * LANES:(s + 1) * LANES].astype(o_ref.dtype)
            else:
                o_ref[...] = val.astype(o_ref.dtype)

        d = _dot(load(a_ref, a_slab), load(b_ref, b_slab), ca, cb)
        if nk == 1:
            write(d)
        elif direct:
            @pl.when(k == 0)
            def _():
                o_ref[...] = d

            @pl.when(k > 0)
            def _():
                o_ref[...] += d
        else:
            @pl.when(k == 0)
            def _():
                acc_ref[...] = d

            if nk > 2:
                @pl.when((k > 0) & (k < nk - 1))
                def _():
                    acc_ref[...] += d

            @pl.when(k == nk - 1)
            def _():
                write(acc_ref[...] + d)

    direct = (not out_slab) and out_shape.dtype == F32
    if nk == 1 or direct:
        acc_shape = (8, LANES)
    in_specs = [a_spec, b_spec]
    args = [a, b]
    aliases = {}
    if alias_out is not None:
        in_specs.append(ANY)
        args.append(alias_out)
        aliases = {2: 0}
    return pl.pallas_call(
        body, grid=grid, in_specs=in_specs, out_specs=out_spec, out_shape=out_shape,
        scratch_shapes=[pltpu.VMEM(acc_shape, F32)], input_output_aliases=aliases,
        compiler_params=_cp("parallel", "parallel", "arbitrary"), name=name)(*args)


def _mm_nn(a, b, out_dtype, tm, tn, tk, name):
    m, kk = a.shape
    n = b.shape[1]
    return _mm(a, b, grid=(m // tm, n // tn, kk // tk),
               a_spec=pl.BlockSpec((tm, tk), lambda i, j, k: (i, k)),
               b_spec=pl.BlockSpec((tk, tn), lambda i, j, k: (k, j)),
               out_shape=jax.ShapeDtypeStruct((m, n), out_dtype),
               out_spec=pl.BlockSpec((tm, tn), lambda i, j, k: (i, j)),
               ca=1, cb=0, acc_shape=(tm, tn), name=name)


def _mm_nt(a, b, out_dtype, tm, tn, tk, name):
    m, kk = a.shape
    n = b.shape[0]
    return _mm(a, b, grid=(m // tm, n // tn, kk // tk),
               a_spec=pl.BlockSpec((tm, tk), lambda i, j, k: (i, k)),
               b_spec=pl.BlockSpec((tn, tk), lambda i, j, k: (j, k)),
               out_shape=jax.ShapeDtypeStruct((m, n), out_dtype),
               out_spec=pl.BlockSpec((tm, tn), lambda i, j, k: (i, j)),
               ca=1, cb=1, acc_shape=(tm, tn), name=name)


def _mm_tn(a, b, out_dtype, tm, tn, tk, name):
    kk, m = a.shape
    n = b.shape[1]
    return _mm(a, b, grid=(m // tm, n // tn, kk // tk),
               a_spec=pl.BlockSpec((tk, tm), lambda i, j, k: (k, i)),
               b_spec=pl.BlockSpec((tk, tn), lambda i, j, k: (k, j)),
               out_shape=jax.ShapeDtypeStruct((m, n), out_dtype),
               out_spec=pl.BlockSpec((tm, tn), lambda i, j, k: (i, j)),
               ca=0, cb=0, acc_shape=(tm, tn), name=name)


ROW_TILE = 256


def _rms(x, g):
    r = lax.rsqrt(jnp.mean(x * x, axis=-1, keepdims=True) + EPS)
    return x * r * g


def _prenorm(x, g, name):
    def body(x_ref, g_ref, o_ref):
        o_ref[...] = _rms(x_ref[...], g_ref[...]).astype(BF16)

    return pl.pallas_call(
        body, grid=(S // ROW_TILE,),
        in_specs=[pl.BlockSpec((ROW_TILE, D), lambda i: (i, 0)), pl.BlockSpec((1, D), lambda i: (0, 0))],
        out_specs=pl.BlockSpec((ROW_TILE, D), lambda i: (i, 0)),
        out_shape=jax.ShapeDtypeStruct((S, D), BF16), compiler_params=_cp("parallel"), name=name)(x, g)


def _postnorm_res(x, f, g_post, g_next, name):
    def body(x_ref, f_ref, gp_ref, gn_ref, xo_ref, ho_ref):
        xn = x_ref[...] + _rms(f_ref[...], gp_ref[...])
        xo_ref[...] = xn
        ho_ref[...] = _rms(xn, gn_ref[...]).astype(BF16)

    row = pl.BlockSpec((ROW_TILE, D), lambda i: (i, 0))
    vec = pl.BlockSpec((1, D), lambda i: (0, 0))
    return pl.pallas_call(
        body, grid=(S // ROW_TILE,), in_specs=[row, row, vec, vec], out_specs=[row, row],
        out_shape=[jax.ShapeDtypeStruct((S, D), F32), jax.ShapeDtypeStruct((S, D), BF16)],
        compiler_params=_cp("parallel"), name=name)(x, f, g_post, g_next)


def _norm_bwd(f, g, dys, res, out_dtype, name):
    ndy = len(dys)
    has_res = res is not None

    def body(*refs):
        f_ref, g_ref = refs[0], refs[1]
        dy_refs = refs[2:2 + ndy]
        res_ref = refs[2 + ndy] if has_res else None
        o_ref, dg_ref = refs[-2], refs[-1]
        fv = f_ref[...]
        dy = dy_refs[0][...].astype(F32)
        for r in dy_refs[1:]:
            dy = dy + r[...].astype(F32)
        r = lax.rsqrt(jnp.mean(fv * fv, axis=-1, keepdims=True) + EPS)
        n = fv * r
        dn = dy * g_ref[...]
        df = r * (dn - n * jnp.mean(dn * n, axis=-1, keepdims=True))
        if has_res:
            df = df + res_ref[...]
        o_ref[...] = df.astype(out_dtype)

        @pl.when(pl.program_id(0) == 0)
        def _():
            dg_ref[...] = jnp.zeros((1, D), F32)

        dg_ref[...] += jnp.sum(dy * n, axis=0, keepdims=True)

    row = pl.BlockSpec((ROW_TILE, D), lambda i: (i, 0))
    vec = pl.BlockSpec((1, D), lambda i: (0, 0))
    in_specs = [row, vec] + [row] * ndy + ([row] if has_res else [])
    args = [f, g] + list(dys) + ([res] if has_res else [])
    return pl.pallas_call(
        body, grid=(S // ROW_TILE,), in_specs=in_specs, out_specs=[row, vec],
        out_shape=[jax.ShapeDtypeStruct((S, D), out_dtype), jax.ShapeDtypeStruct((1, D), F32)],
        compiler_params=_cp("arbitrary"), name=name)(*args)


def _loss_head(y, target, name):
    def body(y_ref, t_ref, dy_ref, l_ref):
        e = y_ref[...] - t_ref[...]
        dy_ref[...] = e * (1.0 / D)

        @pl.when(pl.program_id(0) == 0)
        def _():
            l_ref[...] = jnp.zeros((8, LANES), F32)

        l_ref[...] += jnp.sum(e * e) * (0.5 / D)

    row = pl.BlockSpec((ROW_TILE, D), lambda i: (i, 0))
    return pl.pallas_call(
        body, grid=(S // ROW_TILE,), in_specs=[row, row],
        out_specs=[row, pl.BlockSpec((8, LANES), lambda i: (0, 0))],
        out_shape=[jax.ShapeDtypeStruct((S, D), F32), jax.ShapeDtypeStruct((8, LANES), F32)],
        compiler_params=_cp("arbitrary"), name=name)(y, target)


def _bucket_tiles():
    a = np.arange(BLK)[:, None]
    b = np.arange(2 * BLK)[None, :]
    dist = a + BLK - b
    out = np.zeros((4, 2, BLK, 2 * BLK), np.int32)
    cfg = [(w // d, d) for w, d in A_GROUPS] + [(BLK - 1, 1)]
    for gi, (max_dist, d) in enumerate(cfg):
        band = (dist >= 0) & (dist <= max_dist)
        tok = np.maximum(dist, 0) * d
        nf = np.maximum(tok, 1).astype(np.float32)
        max_exact = NUM_BUCKETS // 2
        large = max_exact + (np.log(nf / np.float32(max_exact)) / np.float32(math.log(MAX_DISTANCE / max_exact))
                             * np.float32(NUM_BUCKETS - max_exact)).astype(np.int32)
        large = np.minimum(large, NUM_BUCKETS - 1)
        bkt = np.where(tok < max_exact, tok, large).astype(np.int32)
        full = np.where(band, bkt, -1)
        out[gi, 1] = full
        out[gi, 0] = np.where(b >= BLK, full, -1)
    return out


def _bias_tiles(rel_bias, buckets, name):
    def body(tab_ref, bkt_ref, o_ref):
        h = pl.program_id(0)
        bkt = bkt_ref[...]
        acc = jnp.zeros(bkt.shape, F32)
        for bb in range(NUM_BUCKETS):
            acc = jnp.where(bkt == bb, tab_ref[bb, h], acc)
        o_ref[...] = jnp.where(bkt < 0, NEG, acc)

    return pl.pallas_call(
        body, grid=(N_BIAS_HEADS,),
        in_specs=[SMEM, pl.BlockSpec((None, 2, BLK, 2 * BLK), lambda h: (jnp.minimum(h // 4, 3), 0, 0, 0))],
        out_specs=pl.BlockSpec((None, 2, BLK, 2 * BLK), lambda h: (h, 0, 0, 0)),
        out_shape=jax.ShapeDtypeStruct((N_BIAS_HEADS, 2, BLK, 2 * BLK), F32),
        compiler_params=_cp("arbitrary"), name=name)(rel_bias, buckets)


def _bias_grad(gs, buckets, name):
    ng = len(gs)

    def body(*refs):
        g_refs = refs[:ng]
        bkt_ref, o_ref = refs[ng], refs[ng + 1]
        h = pl.program_id(0)
        g = g_refs[0][...]
        for r in g_refs[1:]:
            g = g + r[...]
        bkt = bkt_ref[...]
        row = lax.broadcasted_iota(jnp.int32, (NUM_BUCKETS, LANES), 0)
        lane = lax.broadcasted_iota(jnp.int32, (NUM_BUCKETS, LANES), 1)

        @pl.when(h == 0)
        def _():
            o_ref[...] = jnp.zeros((NUM_BUCKETS, LANES), F32)

        acc = o_ref[...]
        for bb in range(NUM_BUCKETS):
            s = jnp.sum(jnp.where(bkt == bb, g, 0.0))
            acc = jnp.where((row == bb) & (lane == h), s, acc)
        o_ref[...] = acc

    g_spec = pl.BlockSpec((None, BLK, 2 * BLK), lambda h: (h, 0, 0))
    return pl.pallas_call(
        body, grid=(N_BIAS_HEADS,),
        in_specs=[g_spec] * ng + [pl.BlockSpec((None, None, BLK, 2 * BLK), lambda h: (jnp.minimum(h // 4, 3), 1, 0, 0))],
        out_specs=pl.BlockSpec((NUM_BUCKETS, LANES), lambda h: (0, 0)),
        out_shape=jax.ShapeDtypeStruct((NUM_BUCKETS, LANES), F32),
        compiler_params=_cp("arbitrary"), name=name)(*gs, buckets)


def _to_class_major(src_ref, dst_ref, d, scale=None, dtype=None):
    ln = S // d
    for r in range(d):
        v = src_ref[pl.ds(r, ln, stride=d), :] if d > 1 else src_ref[...]
        if scale is not None:
            v = v * scale
        dst_ref[pl.ds(r * ln, ln), :] = v.astype(dtype or dst_ref.dtype)


def _block_rows(b, d):
    nbc = NB // d
    i = b % nbc
    r = b // nbc
    has_prev = (i > 0).astype(jnp.int32)
    prev = pl.multiple_of(jnp.maximum(b - 1, 0) * BLK, BLK)
    nat = i * (BLK * d) + r
    return has_prev, prev, nat


def _lane_halves(v0, v1):
    lane = lax.broadcasted_iota(jnp.int32, (v0.shape[0], LANES), 1)
    return jnp.where(lane < HD, v0, v1)


def _band_fwd(proj, bias, sinks, *, d, q0, k0, v0, npairs, bias0, shared_kv, name):
    def body(sink_ref, q_ref, k_ref, v_ref, b_ref, num_ref, st_ref, qs, ks, vs):
        p = pl.program_id(0)
        _to_class_major(q_ref, qs, d, scale=SCALE)
        _to_class_major(k_ref, ks, d)
        _to_class_major(v_ref, vs, d)
        lane = lax.broadcasted_iota(jnp.int32, (BLK, LANES), 1)

        def blk(b, carry):
            has_prev, prev, nat = _block_rows(b, d)
            cur = pl.multiple_of(b * BLK, BLK)
            qb = qs[pl.ds(cur, BLK), :]
            k2 = jnp.concatenate([ks[pl.ds(prev, BLK), :], ks[pl.ds(cur, BLK), :]], axis=0)
            v2 = jnp.concatenate([vs[pl.ds(prev, BLK), :], vs[pl.ds(cur, BLK), :]], axis=0)
            nums, ms, ls = [], [], []
            for hh in range(2):
                qh = qb[:, hh * HD:(hh + 1) * HD]
                if shared_kv:
                    kh = jnp.where(p >= 2, k2[:, HD:], k2[:, :HD])
                    vh = jnp.where(p >= 2, v2[:, HD:], v2[:, :HD])
                else:
                    kh = k2[:, hh * HD:(hh + 1) * HD]
                    vh = v2[:, hh * HD:(hh + 1) * HD]
                z = _dot(qh, kh, 1, 1) + b_ref[hh, has_prev]
                m = jnp.max(z, axis=1, keepdims=True)
                e = jnp.exp(z - m)
                l = jnp.sum(e, axis=1, keepdims=True)
                num = _dot(e.astype(BF16), vh, 1, 0)
                if shared_kv:
                    sink = sink_ref[0, 2 * p + hh]
                    mx = jnp.maximum(m, sink)
                    c = jnp.exp(m - mx)
                    zden = l * c + jnp.exp(sink - mx)
                    num = num * (c / zden)
                    m = mx + jnp.log(zden)
                ls.append(l)
                ms.append(m)
                nums.append(num)
            num_t = jnp.concatenate(nums, axis=1)
            if shared_kv:
                st_t = jnp.where(lane < HD, ms[0], ms[1])
            else:
                st_t = jnp.where(lane < 32, ms[0], jnp.where(lane < 64, ls[0], jnp.where(lane < 96, ms[1], ls[1])))
            if d > 1:
                num_ref[pl.ds(nat, BLK, stride=d), :] = num_t
                st_ref[pl.ds(nat, BLK, stride=d), :] = st_t
            else:
                num_ref[pl.ds(cur, BLK), :] = num_t
                st_ref[pl.ds(cur, BLK), :] = st_t
            return carry

        lax.fori_loop(0, NB, blk, 0, unroll=8)

    slab = lambda off, per_pair: pl.BlockSpec((None, S, LANES), (lambda p: (off + p, 0, 0)) if per_pair else (lambda p: (off, 0, 0)))
    out = pl.BlockSpec((None, S, LANES), lambda p: (p, 0, 0))
    return pl.pallas_call(
        body, grid=(npairs,),
        in_specs=[SMEM, slab(q0, True), slab(k0, not shared_kv), slab(v0, not shared_kv),
                  pl.BlockSpec((None, 2, 2, BLK, 2 * BLK), lambda p: (bias0 + p, 0, 0, 0, 0))],
        out_specs=[out, out],
        out_shape=[jax.ShapeDtypeStruct((npairs, S, LANES), F32)] * 2,
        scratch_shapes=[pltpu.VMEM((S, LANES), BF16)] * 3,
        compiler_params=_cp("arbitrary"), name=name)(sinks, proj, proj, proj, bias)


def _combine_a(nums, stats, name):
    rt = 512

    def body(n0, n1, n2, s0, s1, s2, o_ref, l_ref):
        n_refs, s_refs = (n0, n1, n2), (s0, s1, s2)
        outs, lses = [], []
        for hh in range(2):
            ms = [s[:, 64 * hh:64 * hh + 1] for s in s_refs]
            ls = [s[:, 64 * hh + 32:64 * hh + 33] for s in s_refs]
            mx = jnp.maximum(jnp.maximum(ms[0], ms[1]), ms[2])
            cs = [jnp.exp(m - mx) for m in ms]
            z = cs[0] * ls[0] + cs[1] * ls[1] + cs[2] * ls[2]
            acc = cs[0] * n_refs[0][:, hh * HD:(hh + 1) * HD]
            acc = acc + cs[1] * n_refs[1][:, hh * HD:(hh + 1) * HD]
            acc = acc + cs[2] * n_refs[2][:, hh * HD:(hh + 1) * HD]
            outs.append(acc / z)
            lses.append(mx + jnp.log(z))
        o_ref[...] = jnp.concatenate(outs, axis=1)
        l_ref[...] = _lane_halves(lses[0], lses[1])

    spec = pl.BlockSpec((None, rt, LANES), lambda p, i: (p, i, 0))
    return pl.pallas_call(
        body, grid=(2, S // rt), in_specs=[spec] * 6, out_specs=[spec, spec],
        out_shape=[jax.ShapeDtypeStruct((2, S, LANES), F32)] * 2,
        compiler_params=_cp("parallel", "parallel"), name=name)(*nums, *stats)


def _band_bwd(proj, bias, o, do, lse, sinks, *, d, q0, k0, v0, npairs, bias0, shared_kv, name):
    nkv = 1 if shared_kv else npairs

    def body(sink_ref, q_ref, k_ref, v_ref, b_ref, o_ref, do_ref, lse_ref,
             dq_ref, dk_ref, dv_ref, g_ref, ds_ref,
             qs, ks, vs, dos, lses, dls, dl_nat, dq_nat, dk_cm, dv_cm, kv_nat):
        p = pl.program_id(0)
        lane = lax.broadcasted_iota(jnp.int32, (S, LANES), 1)
        dov = do_ref[...]
        prod = dov * o_ref[...]
        dl0 = jnp.sum(jnp.where(lane < HD, prod, 0.0), axis=1, keepdims=True)
        dl1 = jnp.sum(jnp.where(lane >= HD, prod, 0.0), axis=1, keepdims=True)
        dl_nat[...] = jnp.where(lane < HD, dl0, dl1)
        if shared_kv:
            row8 = lax.broadcasted_iota(jnp.int32, (8, LANES), 0)
            lane8 = lax.broadcasted_iota(jnp.int32, (8, LANES), 1)
            t = jnp.zeros((8, LANES), F32)
            lv = lse_ref[...]
            for hh in range(2):
                sink = sink_ref[0, 2 * p + hh]
                ps = jnp.exp(sink - lv[:, 64 * hh:64 * hh + 1])
                dsink = -jnp.sum(ps * (dl0 if hh == 0 else dl1))
                t = jnp.where((row8 == 0) & (lane8 == hh), dsink, t)
            ds_ref[...] = t
        else:
            ds_ref[...] = jnp.zeros((8, LANES), F32)
        _to_class_major(q_ref, qs, d, scale=SCALE)
        _to_class_major(k_ref, ks, d)
        _to_class_major(v_ref, vs, d)
        _to_class_major(do_ref, dos, d)
        _to_class_major(lse_ref, lses, d)
        _to_class_major(dl_nat, dls, d)

        def zero_kv():
            dk_cm[...] = jnp.zeros((S, LANES), F32)
            dv_cm[...] = jnp.zeros((S, LANES), F32)

        if shared_kv:
            pl.when(p == 0)(zero_kv)
        else:
            zero_kv()

        g_ref[...] = jnp.zeros((2, BLK, 2 * BLK), F32)
        lane2 = lax.broadcasted_iota(jnp.int32, (2 * BLK, LANES), 1)

        def blk(b, carry):
            has_prev, prev, nat = _block_rows(b, d)
            cur = pl.multiple_of(b * BLK, BLK)
            qb = qs[pl.ds(cur, BLK), :]
            dob = dos[pl.ds(cur, BLK), :]
            lb = lses[pl.ds(cur, BLK), :]
            dlb = dls[pl.ds(cur, BLK), :]
            k2 = jnp.concatenate([ks[pl.ds(prev, BLK), :], ks[pl.ds(cur, BLK), :]], axis=0)
            v2 = jnp.concatenate([vs[pl.ds(prev, BLK), :], vs[pl.ds(cur, BLK), :]], axis=0)
            dqs, dks, dvs = [], [], []
            for hh in range(2):
                qh = qb[:, hh * HD:(hh + 1) * HD]
                doh = dob[:, hh * HD:(hh + 1) * HD]
                if shared_kv:
                    kh = jnp.where(p >= 2, k2[:, HD:], k2[:, :HD])
                    vh = jnp.where(p >= 2, v2[:, HD:], v2[:, :HD])
                else:
                    kh = k2[:, hh * HD:(hh + 1) * HD]
                    vh = v2[:, hh * HD:(hh + 1) * HD]
                z = _dot(qh, kh, 1, 1) + b_ref[hh, has_prev]
                pr = jnp.exp(z - lb[:, 64 * hh:64 * hh + 1])
                dp = _dot(doh, vh, 1, 1)
                dz = pr * (dp - dlb[:, 64 * hh:64 * hh + 1])
                g_ref[hh] += dz
                dzb = dz.astype(BF16)
                dqs.append(_dot(dzb, kh, 1, 0) * SCALE)
                dks.append(_dot(dzb, qh, 0, 0))
                dvs.append(_dot(pr.astype(BF16), doh, 0, 0))
            dq_t = jnp.concatenate(dqs, axis=1)
            if shared_kv:
                dk_t = jnp.concatenate([dks[0] + dks[1]] * 2, axis=1)
                dv_t = jnp.concatenate([dvs[0] + dvs[1]] * 2, axis=1)
                mine = (lane2 >= HD) == (p >= 2)
                dk_t = jnp.where(mine, dk_t, 0.0)
                dv_t = jnp.where(mine, dv_t, 0.0)
            else:
                dk_t = jnp.concatenate(dks, axis=1)
                dv_t = jnp.concatenate(dvs, axis=1)
            dk_cm[pl.ds(prev, BLK), :] += dk_t[:BLK]
            dk_cm[pl.ds(cur, BLK), :] += dk_t[BLK:]
            dv_cm[pl.ds(prev, BLK), :] += dv_t[:BLK]
            dv_cm[pl.ds(cur, BLK), :] += dv_t[BLK:]
            if d > 1:
                dq_nat[pl.ds(nat, BLK, stride=d), :] = dq_t
            else:
                dq_nat[pl.ds(cur, BLK), :] = dq_t
            return carry

        lax.fori_loop(0, NB, blk, 0, unroll=8)
        dq_ref[...] = dq_nat[...].astype(BF16)

        def from_class_major(src, dst_ref):
            if d == 1:
                dst_ref[...] = src[...].astype(BF16)
            else:
                ln = S // d
                for r in range(d):
                    kv_nat[pl.ds(r, ln, stride=d), :] = src[pl.ds(r * ln, ln), :]
                dst_ref[...] = kv_nat[...].astype(BF16)

        def write_kv():
            from_class_major(dk_cm, dk_ref)
            from_class_major(dv_cm, dv_ref)

        if shared_kv:
            pl.when(p == npairs - 1)(write_kv)
        else:
            write_kv()

    slab = lambda off, per_pair: pl.BlockSpec((None, S, LANES), (lambda p: (off + p, 0, 0)) if per_pair else (lambda p: (off, 0, 0)))
    pair = pl.BlockSpec((None, S, LANES), lambda p: (p, 0, 0))
    kv_out = pair if not shared_kv else pl.BlockSpec((None, S, LANES), lambda p: (0, 0, 0))
    return pl.pallas_call(
        body, grid=(npairs,),
        in_specs=[SMEM, slab(q0, True), slab(k0, not shared_kv), slab(v0, not shared_kv),
                  pl.BlockSpec((None, 2, 2, BLK, 2 * BLK), lambda p: (bias0 + p, 0, 0, 0, 0)),
                  pair, pair, pair],
        out_specs=[pair, kv_out, kv_out,
                   pl.BlockSpec((None, 2, BLK, 2 * BLK), lambda p: (p, 0, 0, 0)),
                   pl.BlockSpec((None, 8, LANES), lambda p: (p, 0, 0))],
        out_shape=[jax.ShapeDtypeStruct((npairs, S, LANES), BF16),
                   jax.ShapeDtypeStruct((nkv, S, LANES), BF16),
                   jax.ShapeDtypeStruct((nkv, S, LANES), BF16),
                   jax.ShapeDtypeStruct((npairs, 2, BLK, 2 * BLK), F32),
                   jax.ShapeDtypeStruct((npairs, 8, LANES), F32)],
        scratch_shapes=[pltpu.VMEM((S, LANES), BF16)] * 4 + [pltpu.VMEM((S, LANES), F32)] * 7,
        compiler_params=_cp("arbitrary"), name=name)(sinks, proj, proj, proj, bias, o, do, lse)


KC = 512
NSUB = KC // BLK


def _split2(x):
    hi = x.astype(BF16)
    lo = (x - hi.astype(F32)).astype(BF16)
    return hi, lo


def _tri_ones(cmp):
    jj = lax.broadcasted_iota(jnp.int32, (2 * BLK, BLK), 0) % BLK
    ss = lax.broadcasted_iota(jnp.int32, (2 * BLK, BLK), 1)
    return jnp.concatenate([cmp(jj, ss).astype(BF16), jnp.ones((2 * BLK, BLK), BF16)], axis=1)


def _sub_sums(x, tri1):
    st = jnp.concatenate([x[:, s * BLK:(s + 1) * BLK] for s in range(NSUB)], axis=0)
    hi, lo = _split2(st)
    r = _dot(jnp.concatenate([hi, lo], axis=1), tri1, 1, 0)
    return ([r[s * BLK:(s + 1) * BLK, :BLK] for s in range(NSUB)], [r[s * BLK:(s + 1) * BLK, BLK:] for s in range(NSUB)])


def _log_sig_pair(z):
    lb = jnp.minimum(z, 0.0) - jnp.log1p(jnp.exp(-jnp.abs(z)))
    return lb, lb - z


QGROUPS = NB // NSUB


def _stick_fwd(proj, *, q0, k0, v0, name):
    def body(q_ref, k_ref, v_ref, o_ref, t_ref, qs, ks, vs):
        qs[...] = (q_ref[...] * SCALE).astype(BF16)
        ks[...] = k_ref[...].astype(BF16)
        vs[...] = v_ref[...].astype(BF16)
        tri1 = _tri_ones(lambda j, s: j > s)
        col = lax.broadcasted_iota(jnp.int32, (BLK, KC), 1)
        rowi = lax.broadcasted_iota(jnp.int32, (BLK, KC), 0)

        for qg in range(QGROUPS):
            def qblock(ii, carry0, qg=qg):
                t0 = pl.multiple_of((qg * NSUB + ii) * BLK, BLK)
                qb = qs[pl.ds(t0, BLK), :]
                accs = [jnp.zeros((BLK, HD), F32)] * 2
                runs = [jnp.zeros((BLK, BLK), F32)] * 2
                for c in reversed(range(qg + 1)):
                    s0 = c * KC
                    diag = c == qg
                    before = (s0 + col) < (t0 + rowi) if diag else None
                    for hh in range(2):
                        kh = ks[s0:s0 + KC, hh * HD:(hh + 1) * HD]
                        vh = vs[s0:s0 + KC, hh * HD:(hh + 1) * HD]
                        lb, lk = _log_sig_pair(_dot(qb[:, hh * HD:(hh + 1) * HD], kh, 1, 1))
                        if diag:
                            lk = jnp.where(before, lk, 0.0)
                        suf, tot = _sub_sums(lk, tri1)
                        ws, run = [], runs[hh]
                        for s in reversed(range(NSUB)):
                            ws.append(jnp.exp(lb[:, s * BLK:(s + 1) * BLK] + suf[s] + run))
                            run = run + tot[s]
                        w = jnp.concatenate(ws[::-1], axis=1)
                        if diag:
                            w = jnp.where(before, w, 0.0)
                        accs[hh] = accs[hh] + _dot(w.astype(BF16), vh, 1, 0)
                        runs[hh] = run
                o_ref[pl.ds(t0, BLK), :] = jnp.concatenate(accs, axis=1)
                t_ref[pl.ds(t0, BLK), :] = _lane_halves(runs[0], runs[1])
                return carry0

            lax.fori_loop(0, NSUB, qblock, 0)

    slab = lambda off: pl.BlockSpec((None, S, LANES), lambda p: (off + p, 0, 0))
    out = pl.BlockSpec((None, S, LANES), lambda p: (p, 0, 0))
    return pl.pallas_call(
        body, grid=(2,), in_specs=[slab(q0), slab(k0), slab(v0)], out_specs=[out, out],
        out_shape=[jax.ShapeDtypeStruct((2, S, LANES), F32)] * 2,
        scratch_shapes=[pltpu.VMEM((S, LANES), BF16)] * 3,
        compiler_params=_cp("arbitrary"), name=name)(proj, proj, proj)


def _stick_bwd(proj, do, tot, *, q0, k0, v0, name):
    def body(q_ref, k_ref, v_ref, do_ref, t_ref, dq_ref, dk_ref, dv_ref, qs, ks, vs, dos, dk_acc, dv_acc):
        qs[...] = (q_ref[...] * SCALE).astype(BF16)
        ks[...] = k_ref[...].astype(BF16)
        vs[...] = v_ref[...].astype(BF16)
        dos[...] = do_ref[...].astype(BF16)
        dk_acc[...] = jnp.zeros((2, S, HD), F32)
        dv_acc[...] = jnp.zeros((2, S, HD), F32)
        tri_inc = _tri_ones(lambda j, s: j <= s)
        tri_exc = _tri_ones(lambda j, s: j < s)
        col = lax.broadcasted_iota(jnp.int32, (BLK, KC), 1)
        rowi = lax.broadcasted_iota(jnp.int32, (BLK, KC), 0)

        for qg in range(QGROUPS):
            def qblock(ii, carry0, qg=qg):
                t0 = pl.multiple_of((qg * NSUB + ii) * BLK, BLK)
                qb = qs[pl.ds(t0, BLK), :]
                dob = dos[pl.ds(t0, BLK), :]
                tb = t_ref[pl.ds(t0, BLK), :]
                dqs = [jnp.zeros((BLK, HD), F32)] * 2
                pruns = [jnp.zeros((BLK, BLK), F32)] * 2
                eruns = [jnp.zeros((BLK, BLK), F32)] * 2
                for c in range(qg + 1):
                    s0 = c * KC
                    diag = c == qg
                    before = (s0 + col) < (t0 + rowi) if diag else None
                    for hh in range(2):
                        qh = qb[:, hh * HD:(hh + 1) * HD]
                        doh = dob[:, hh * HD:(hh + 1) * HD]
                        tt = tb[:, 64 * hh:64 * hh + 1]
                        kh = ks[s0:s0 + KC, hh * HD:(hh + 1) * HD]
                        vh = vs[s0:s0 + KC, hh * HD:(hh + 1) * HD]
                        lb, lk = _log_sig_pair(_dot(qh, kh, 1, 1))
                        if diag:
                            lk = jnp.where(before, lk, 0.0)
                        pin, ptot = _sub_sums(lk, tri_inc)
                        ws, prun = [], pruns[hh]
                        for s in range(NSUB):
                            ws.append(jnp.exp(lb[:, s * BLK:(s + 1) * BLK] + (tt - (pin[s] + prun))))
                            prun = prun + ptot[s]
                        w = jnp.concatenate(ws, axis=1)
                        if diag:
                            w = jnp.where(before, w, 0.0)
                        e = w * _dot(doh, vh, 1, 1)
                        pex, etot = _sub_sums(e, tri_exc)
                        cs, erun = [], eruns[hh]
                        for s in range(NSUB):
                            cs.append(pex[s] + erun)
                            erun = erun + etot[s]
                        sig = jnp.exp(lb)
                        dz = e * (1.0 - sig) - jnp.concatenate(cs, axis=1) * sig
                        if diag:
                            dz = jnp.where(before, dz, 0.0)
                        dz = dz.astype(BF16)
                        dqs[hh] = dqs[hh] + _dot(dz, kh, 1, 0)
                        dk_acc[hh, s0:s0 + KC, :] += _dot(dz, qh, 0, 0)
                        dv_acc[hh, s0:s0 + KC, :] += _dot(w.astype(BF16), doh, 0, 0)
                        pruns[hh], eruns[hh] = prun, erun
                dq_ref[pl.ds(t0, BLK), :] = (jnp.concatenate(dqs, axis=1) * SCALE).astype(BF16)
                return carry0

            lax.fori_loop(0, NSUB, qblock, 0)
        dk_ref[...] = jnp.concatenate([dk_acc[0], dk_acc[1]], axis=1).astype(BF16)
        dv_ref[...] = jnp.concatenate([dv_acc[0], dv_acc[1]], axis=1).astype(BF16)

    slab = lambda off: pl.BlockSpec((None, S, LANES), lambda p: (off + p, 0, 0))
    pair = pl.BlockSpec((None, S, LANES), lambda p: (p, 0, 0))
    return pl.pallas_call(
        body, grid=(2,), in_specs=[slab(q0), slab(k0), slab(v0), pair, pair], out_specs=[pair] * 3,
        out_shape=[jax.ShapeDtypeStruct((2, S, LANES), BF16)] * 3,
        scratch_shapes=[pltpu.VMEM((S, LANES), BF16)] * 4 + [pltpu.VMEM((2, S, HD), F32)] * 2,
        compiler_params=_cp("arbitrary"), name=name)(proj, proj, proj, do, tot)


def _cat_slabs(ref):
    return jnp.concatenate([ref[s] for s in range(ref.shape[0])], axis=1)


def _merge_fwd(o_a, o_b, o_c, gates, b_gate, wa, wb, wc, w_out, name):
    tm = ROW_TILE

    def body(oa_ref, ob_ref, oc_ref, g_ref, bg_ref, wa_ref, wb_ref, wc_ref, wo_ref, mg_ref, mo_ref):
        acc = jnp.zeros((tm, D), F32)
        for i, (o_ref, w_ref) in enumerate(((oa_ref, wa_ref), (ob_ref, wb_ref), (oc_ref, wc_ref))):
            pr = _dot(_cat_slabs(o_ref).astype(BF16), w_ref[...], 1, 0)
            sg = jax.nn.sigmoid(g_ref[:, i * D:(i + 1) * D] + bg_ref[i:i + 1, :])
            acc = acc + sg * pr
        mg = acc.astype(BF16)
        mg_ref[...] = mg
        mo_ref[...] = _dot(mg, wo_ref[...], 1, 0)

    slabs = lambda n: pl.BlockSpec((n, tm, LANES), lambda i: (0, i, 0))
    full = lambda r, c: pl.BlockSpec((r, c), lambda i: (0, 0))
    row = pl.BlockSpec((tm, D), lambda i: (i, 0))
    return pl.pallas_call(
        body, grid=(S // tm,),
        in_specs=[slabs(2), slabs(4), slabs(2), pl.BlockSpec((tm, GATE_COLS), lambda i: (i, 0)), full(3, D),
                  full(256, D), full(512, D), full(256, D), full(D, D)],
        out_specs=[row, row],
        out_shape=[jax.ShapeDtypeStruct((S, D), BF16), jax.ShapeDtypeStruct((S, D), F32)],
        compiler_params=_cp("parallel"), name=name)(o_a, o_b, o_c, gates, b_gate, wa, wb, wc, w_out)


def _merge_bwd(d_mo, o_a, o_b, o_c, gates, b_gate, wa, wb, wc, w_out, name):
    tm = ROW_TILE

    def body(dmo_ref, oa_ref, ob_ref, oc_ref, g_ref, bg_ref, wa_ref, wb_ref, wc_ref, wo_ref,
             doa_ref, dob_ref, doc_ref, dg_ref, dwa_ref, dwb_ref, dwc_ref, dbg_ref):
        @pl.when(pl.program_id(0) == 0)
        def _():
            dwa_ref[...] = jnp.zeros(dwa_ref.shape, F32)
            dwb_ref[...] = jnp.zeros(dwb_ref.shape, F32)
            dwc_ref[...] = jnp.zeros(dwc_ref.shape, F32)
            dbg_ref[...] = jnp.zeros(dbg_ref.shape, F32)

        dmg = _dot(dmo_ref[...], wo_ref[...], 1, 1)
        trip = ((oa_ref, wa_ref, doa_ref, dwa_ref), (ob_ref, wb_ref, dob_ref, dwb_ref), (oc_ref, wc_ref, doc_ref, dwc_ref))
        for i, (o_ref, w_ref, do_ref, dw_ref) in enumerate(trip):
            ob = _cat_slabs(o_ref).astype(BF16)
            pr = _dot(ob, w_ref[...], 1, 0)
            sg = jax.nn.sigmoid(g_ref[:, i * D:(i + 1) * D] + bg_ref[i:i + 1, :])
            dgate = dmg * pr * sg * (1.0 - sg)
            dg_ref[:, i * D:(i + 1) * D] = dgate.astype(BF16)
            dbg_ref[i:i + 1, :] += jnp.sum(dgate, axis=0, keepdims=True)
            dpr = (dmg * sg).astype(BF16)
            do = _dot(dpr, w_ref[...], 1, 1)
            for s in range(do_ref.shape[0]):
                do_ref[s] = do[:, s * LANES:(s + 1) * LANES]
            dw_ref[...] += _dot(ob, dpr, 0, 0)

    slabs = lambda n: pl.BlockSpec((n, tm, LANES), lambda i: (0, i, 0))
    full = lambda r, c: pl.BlockSpec((r, c), lambda i: (0, 0))
    row = pl.BlockSpec((tm, D), lambda i: (i, 0))
    return pl.pallas_call(
        body, grid=(S // tm,),
        in_specs=[row, slabs(2), slabs(4), slabs(2), pl.BlockSpec((tm, GATE_COLS), lambda i: (i, 0)), full(3, D),
                  full(256, D), full(512, D), full(256, D), full(D, D)],
        out_specs=[slabs(2), slabs(4), slabs(2), pl.BlockSpec((tm, GATE_COLS), lambda i: (i, 0)),
                   full(256, D), full(512, D), full(256, D), full(3, D)],
        out_shape=[jax.ShapeDtypeStruct((2, S, LANES), F32), jax.ShapeDtypeStruct((4, S, LANES), F32),
                   jax.ShapeDtypeStruct((2, S, LANES), F32), jax.ShapeDtypeStruct((S, GATE_COLS), BF16),
                   jax.ShapeDtypeStruct((256, D), F32), jax.ShapeDtypeStruct((512, D), F32),
                   jax.ShapeDtypeStruct((256, D), F32), jax.ShapeDtypeStruct((3, D), F32)],
        compiler_params=_cp("arbitrary"), name=name)(d_mo, o_a, o_b, o_c, gates, b_gate, wa, wb, wc, w_out)


FC = 256
GELU_K = math.sqrt(2.0 / math.pi)
GELU_C = 0.044715


RC = 64
NRC = S // RC


def _down(tail, cur, n):
    row = lax.broadcasted_iota(jnp.int32, tail.shape, 0)
    rolled = pltpu.roll(cur, n, 0)
    first = jnp.where(row < n, pltpu.roll(tail, n, 0), rolled[0:8])
    return jnp.concatenate([first, rolled[8:]], axis=0)


def _up(cur, head, n):
    row = lax.broadcasted_iota(jnp.int32, head.shape, 0)
    rolled = pltpu.roll(cur, RC - n, 0)
    last = jnp.where(row >= 8 - n, pltpu.roll(head, 8 - n, 0), rolled[RC - 8:])
    return jnp.concatenate([rolled[:RC - 8], last], axis=0)


def _conv_chunk(load, j, w_ref, b_ref, half):
    r0 = pl.multiple_of(j * RC, RC)
    cur = load(r0, RC)
    tail = jnp.where(j > 0, load(pl.multiple_of(jnp.maximum(r0 - 8, 0), 8), 8), 0.0)
    d1 = _down(tail, cur, 1)
    d2 = _down(tail, cur, 2)
    y = w_ref[0:1, half, :] * d2 + w_ref[1:2, half, :] * d1 + w_ref[2:3, half, :] * cur + b_ref[half:half + 1, :]
    return y, cur, d1, d2


def _chunk(j):
    return pl.ds(pl.multiple_of(j * RC, RC), RC)


def _fold8(x):
    return jnp.sum(x.reshape(RC // 8, 8, x.shape[-1]), axis=0)


def _ffn_act(u, conv_w, conv_b, name):
    def body(u_ref, w_ref, b_ref, a_ref):
        def step(j, carry):
            yg = _conv_chunk(lambda r, n: u_ref[0, pl.ds(r, n), :], j, w_ref, b_ref, 0)[0]
            yv = _conv_chunk(lambda r, n: u_ref[1, pl.ds(r, n), :], j, w_ref, b_ref, 1)[0]
            th = jnp.tanh(GELU_K * (yg + GELU_C * yg * yg * yg))
            a_ref[_chunk(j), :] = (0.5 * yg * (1.0 + th) * yv).astype(BF16)
            return carry

        lax.fori_loop(0, NRC, step, 0)

    return pl.pallas_call(
        body, grid=(D_FF // FC,),
        in_specs=[pl.BlockSpec((2, S, FC), lambda j: (0, 0, j)), pl.BlockSpec((3, 2, FC), lambda j: (0, 0, j)),
                  pl.BlockSpec((2, FC), lambda j: (0, j))],
        out_specs=pl.BlockSpec((S, FC), lambda j: (0, j)),
        out_shape=jax.ShapeDtypeStruct((S, D_FF), BF16),
        compiler_params=_cp("parallel"), name=name)(u, conv_w, conv_b)


def _ffn_act_bwd(u, d_a, conv_w, conv_b, name):
    def body(u_ref, da_ref, w_ref, b_ref, du_ref, dw_ref, db_ref, dy_s):
        def first(j, acc):
            yg, ug, ug1, ug2 = _conv_chunk(lambda r, n: u_ref[0, pl.ds(r, n), :], j, w_ref, b_ref, 0)
            yv, uv, uv1, uv2 = _conv_chunk(lambda r, n: u_ref[1, pl.ds(r, n), :], j, w_ref, b_ref, 1)
            th = jnp.tanh(GELU_K * (yg + GELU_C * yg * yg * yg))
            gelu = 0.5 * yg * (1.0 + th)
            dgelu = 0.5 * (1.0 + th) + 0.5 * yg * (1.0 - th * th) * GELU_K * (1.0 + 3.0 * GELU_C * yg * yg)
            da = da_ref[_chunk(j), :]
            dyg = da * yv * dgelu
            dyv = da * gelu
            dy_s[0, _chunk(j), :] = dyg
            dy_s[1, _chunk(j), :] = dyv
            new = (_fold8(dyg * ug2), _fold8(dyg * ug1), _fold8(dyg * ug), _fold8(dyg),
                   _fold8(dyv * uv2), _fold8(dyv * uv1), _fold8(dyv * uv), _fold8(dyv))
            return tuple(a + n for a, n in zip(acc, new))

        acc = lax.fori_loop(0, NRC, first, tuple(jnp.zeros((8, FC), F32) for _ in range(8)))
        for half in range(2):
            for k in range(3):
                dw_ref[k:k + 1, half, :] = jnp.sum(acc[4 * half + k], axis=0, keepdims=True)
            db_ref[half:half + 1, :] = jnp.sum(acc[4 * half + 3], axis=0, keepdims=True)

        def second(j, carry):
            for half in range(2):
                cur = dy_s[half, _chunk(j), :]
                h0 = pl.multiple_of(jnp.minimum((j + 1) * RC, S - 8), 8)
                head = jnp.where(j < NRC - 1, dy_s[half, pl.ds(h0, 8), :], 0.0)
                du = (w_ref[2:3, half, :] * cur + w_ref[1:2, half, :] * _up(cur, head, 1)
                      + w_ref[0:1, half, :] * _up(cur, head, 2))
                du_ref[half, _chunk(j), :] = du.astype(BF16)
            return carry

        lax.fori_loop(0, NRC, second, 0)

    return pl.pallas_call(
        body, grid=(D_FF // FC,),
        in_specs=[pl.BlockSpec((2, S, FC), lambda j: (0, 0, j)), pl.BlockSpec((S, FC), lambda j: (0, j)),
                  pl.BlockSpec((3, 2, FC), lambda j: (0, 0, j)), pl.BlockSpec((2, FC), lambda j: (0, j))],
        out_specs=[pl.BlockSpec((2, S, FC), lambda j: (0, 0, j)), pl.BlockSpec((3, 2, FC), lambda j: (0, 0, j)),
                   pl.BlockSpec((2, FC), lambda j: (0, j))],
        out_shape=[jax.ShapeDtypeStruct((2, S, D_FF), BF16), jax.ShapeDtypeStruct((3, 2, D_FF), F32),
                   jax.ShapeDtypeStruct((2, D_FF), F32)],
        scratch_shapes=[pltpu.VMEM((2, S, FC), F32)],
        compiler_params=_cp("parallel"), name=name)(u, d_a, conv_w, conv_b)


def _layer_fwd(x, h1, w, bias, lname):
    n = lambda s: f"{lname}_{s}"
    w.need("in", h1)
    tn = 768
    proj = _mm(h1, w["w_in"], grid=(S // 1024, QKV_COLS // tn, 1),
               a_spec=pl.BlockSpec((1024, D), lambda i, j, k: (i, 0)),
               b_spec=pl.BlockSpec((D, tn), lambda i, j, k: (0, j)),
               out_shape=jax.ShapeDtypeStruct((QKV_SLABS, S, LANES), F32),
               out_spec=pl.BlockSpec((tn // LANES, 1024, LANES), lambda i, j, k: (j, i, 0)),
               ca=1, cb=0, acc_shape=(1024, tn), out_slab=True, name=n("proj_qkv"))
    gates = _mm(h1, w["w_in"], grid=(S // 1024, GATE_COLS // tn, 1),
                a_spec=pl.BlockSpec((1024, D), lambda i, j, k: (i, 0)),
                b_spec=pl.BlockSpec((D, tn), lambda i, j, k: (0, j + QKV_COLS // tn)),
                out_shape=jax.ShapeDtypeStruct((S, GATE_COLS), F32),
                out_spec=pl.BlockSpec((1024, tn), lambda i, j, k: (i, j)),
                ca=1, cb=0, acc_shape=(1024, tn), name=n("proj_gate"))
    nums, stats = [], []
    for g, (_, d) in enumerate(A_GROUPS):
        nm, st = _band_fwd(proj, bias, w["sinks"], d=d, q0=2 * g, k0=6 + 2 * g, v0=12 + 2 * g, npairs=2, bias0=2 * g,
                           shared_kv=False, name=n(f"attn_a{g}_fwd"))
        nums.append(nm)
        stats.append(st)
    o_a, lse_a = _combine_a(nums, stats, n("attn_a_combine"))
    o_b, lse_b = _band_fwd(proj, bias, w["sinks"], d=1, q0=18, k0=22, v0=23, npairs=4, bias0=6, shared_kv=True,
                           name=n("attn_b_fwd"))
    o_c, tot_c = _stick_fwd(proj, q0=24, k0=26, v0=28, name=n("attn_c_fwd"))
    w.need("mix", tot_c)
    merged, mo = _merge_fwd(o_a, o_b, o_c, gates, w["b_gate"], w["w_br_a"], w["w_br_b"], w["w_br_c"], w["w_out"], n("merge_fwd"))
    x2, h2 = _postnorm_res(x, mo, w["attn_post_norm"], w["ffn_pre_norm"], n("attn_post"))
    w.need("ffn", h2)
    u = _mm(h2, w["w_up"], grid=(S // 1024, 2 * D_FF // 1024, 1),
            a_spec=pl.BlockSpec((1024, D), lambda i, j, k: (i, 0)),
            b_spec=pl.BlockSpec((D, 1024), lambda i, j, k: (0, j)),
            out_shape=jax.ShapeDtypeStruct((2, S, D_FF), F32),
            out_spec=pl.BlockSpec((None, 1024, 1024), lambda i, j, k: (j // 4, i, j % 4)),
            ca=1, cb=0, acc_shape=(1024, 1024), name=n("ffn_up"))
    a = _ffn_act(u, w["conv_w"], w["conv_b"], n("ffn_act"))
    fo = _mm_nn(a, w["w_down"], F32, 1024, 1024, 2048, n("ffn_down"))
    saved = dict(x=x, h1=h1, proj=proj, gates=gates, o_a=o_a, lse_a=lse_a, o_b=o_b, lse_b=lse_b, o_c=o_c, tot_c=tot_c,
                 merged=merged, mo=mo, x2=x2, h2=h2, u=u, a=a, fo=fo)
    return saved


def _layer_bwd(dx3, sv, w, bias, lname, tok=None, on_part=None):
    n = lambda s: f"{lname}_{s}"
    g = {}

    def part(group, vec):
        t = on_part(group, g) if on_part is not None else None
        return vec if t is None else vec + t

    gain = w["ffn_post_norm"] if tok is None else w["ffn_post_norm"] + tok
    d_fo, g["ffn_post_norm"] = _norm_bwd(sv["fo"], gain, [dx3], None, BF16, n("ffn_post_bwd"))
    d_a = _mm_nt(d_fo, w["w_down"], F32, 1024, 1024, 1024, n("ffn_down_bwd_x"))
    g["w_down"] = _mm_tn(sv["a"], d_fo, BF16, 1024, 1024, S, n("ffn_down_bwd_w"))
    d_u, dcw, dcb = _ffn_act_bwd(sv["u"], d_a, w["conv_w"], w["conv_b"], n("ffn_act_bwd"))
    g["conv_w"] = dcw.reshape(3, 2 * D_FF)
    g["conv_b"] = dcb.reshape(1, 2 * D_FF)
    g["w_up"] = _mm(sv["h2"], d_u, grid=(1, 2 * D_FF // 1024, 1),
                    a_spec=pl.BlockSpec((S, D), lambda i, j, k: (k, 0)),
                    b_spec=pl.BlockSpec((None, S, 1024), lambda i, j, k: (j // 4, k, j % 4)),
                    out_shape=jax.ShapeDtypeStruct((D, 2 * D_FF), BF16),
                    out_spec=pl.BlockSpec((D, 1024), lambda i, j, k: (0, j)),
                    ca=0, cb=0, acc_shape=(D, 1024), name=n("ffn_up_bwd_w"))
    d_h2 = _mm(d_u, w["w_up"], grid=(S // 1024, 1, 2 * D_FF // 2048),
               a_spec=pl.BlockSpec((None, 1024, 2048), lambda i, j, k: (k // 2, i, k % 2)),
               b_spec=pl.BlockSpec((D, 2048), lambda i, j, k: (0, k)),
               out_shape=jax.ShapeDtypeStruct((S, D), F32),
               out_spec=pl.BlockSpec((1024, D), lambda i, j, k: (i, 0)),
               ca=1, cb=1, acc_shape=(1024, D), name=n("ffn_up_bwd_x"))
    dx2, g["ffn_pre_norm"] = _norm_bwd(sv["x2"], part("ffn", w["ffn_pre_norm"]), [d_h2], dx3, F32, n("ffn_pre_bwd"))
    d_mo, g["attn_post_norm"] = _norm_bwd(sv["mo"], w["attn_post_norm"], [dx2], None, BF16, n("attn_post_bwd"))
    g["w_out"] = _mm_tn(sv["merged"], d_mo, BF16, 1024, 1024, S, n("out_bwd_w"))
    do_a, do_b, do_c, d_gates, dwa, dwb, dwc, g["b_gate"] = _merge_bwd(
        d_mo, sv["o_a"], sv["o_b"], sv["o_c"], sv["gates"], w["b_gate"], w["w_br_a"], w["w_br_b"], w["w_br_c"],
        w["w_out"], n("merge_bwd"))
    g["w_br_a"], g["w_br_b"], g["w_br_c"] = dwa, dwb, dwc
    sinks = part("mix", w["sinks"])
    proj = sv["proj"]
    dqa, dka, dva, gbias = [], [], [], []
    for gi, (_, d) in enumerate(A_GROUPS):
        dq, dk, dv, gg, _ = _band_bwd(proj, bias, sv["o_a"], do_a, sv["lse_a"], sinks, d=d, q0=2 * gi, k0=6 + 2 * gi,
                                      v0=12 + 2 * gi, npairs=2, bias0=2 * gi, shared_kv=False, name=n(f"attn_a{gi}_bwd"))
        dqa.append(dq), dka.append(dk), dva.append(dv), gbias.append(gg)
    dqb, dkb, dvb, ggb, dsink = _band_bwd(proj, bias, sv["o_b"], do_b, sv["lse_b"], sinks, d=1, q0=18, k0=22, v0=23,
                                          npairs=4, bias0=6, shared_kv=True, name=n("attn_b_bwd"))
    gbias.append(ggb)
    g["bias_g"] = jnp.concatenate(gbias, axis=0).reshape(N_BIAS_HEADS, BLK, 2 * BLK)
    g["sinks"] = dsink[:, 0, :2].reshape(1, 8)
    dqc, dkc, dvc = _stick_bwd(proj, do_c, sv["tot_c"], q0=24, k0=26, v0=28, name=n("attn_c_bwd"))
    dqkv = jnp.concatenate(dqa + dka + dva + [dqb, dkb, dvb, dqc, dkc, dvc], axis=0)
    ts = 6
    tsx = 15
    d_h1a = _mm(dqkv, w["w_in"], grid=(S // 1024, 1, QKV_SLABS // tsx),
                a_spec=pl.BlockSpec((tsx, 1024, LANES), lambda i, j, k: (k, i, 0)),
                b_spec=pl.BlockSpec((D, tsx * LANES), lambda i, j, k: (0, k)),
                out_shape=jax.ShapeDtypeStruct((S, D), F32),
                out_spec=pl.BlockSpec((1024, D), lambda i, j, k: (i, 0)),
                ca=1, cb=1, acc_shape=(1024, D), a_slab=True, name=n("in_bwd_x_qkv"))
    d_h1b = _mm(d_gates, w["w_in"], grid=(S // 1024, 1, GATE_COLS // 768),
                a_spec=pl.BlockSpec((1024, 768), lambda i, j, k: (i, k)),
                b_spec=pl.BlockSpec((D, 768), lambda i, j, k: (0, k + QKV_COLS // 768)),
                out_shape=jax.ShapeDtypeStruct((S, D), F32),
                out_spec=pl.BlockSpec((1024, D), lambda i, j, k: (i, 0)),
                ca=1, cb=1, acc_shape=(1024, D), name=n("in_bwd_x_gate"))
    dw_in = _mm(sv["h1"], dqkv, grid=(1, QKV_SLABS // ts, 1),
                a_spec=pl.BlockSpec((S, D), lambda i, j, k: (k, 0)),
                b_spec=pl.BlockSpec((ts, S, LANES), lambda i, j, k: (j, k, 0)),
                out_shape=jax.ShapeDtypeStruct((D, IN_COLS), BF16),
                out_spec=pl.BlockSpec((D, ts * LANES), lambda i, j, k: (0, j)),
                ca=0, cb=0, acc_shape=(D, ts * LANES), b_slab=True, name=n("in_bwd_w_qkv"))
    g["w_in"] = _mm(sv["h1"], d_gates, grid=(1, GATE_COLS // 768, 1),
                    a_spec=pl.BlockSpec((S, D), lambda i, j, k: (k, 0)),
                    b_spec=pl.BlockSpec((S, 768), lambda i, j, k: (k, j)),
                    out_shape=jax.ShapeDtypeStruct((D, IN_COLS), BF16),
                    out_spec=pl.BlockSpec((D, 768), lambda i, j, k: (0, j + QKV_COLS // 768)),
                    ca=0, cb=0, acc_shape=(D, 768), alias_out=dw_in, name=n("in_bwd_w_gate"))
    dx, g["attn_pre_norm"] = _norm_bwd(sv["x"], w["attn_pre_norm"], [d_h1a, d_h1b], dx2, F32, n("attn_pre_bwd"))
    tok_in = on_part("in", g) if on_part is not None else None
    return dx, g, tok_in


def _local_step(x, target, ws, rel_bias, tok=None, on_grads=None):
    buckets = jnp.asarray(_bucket_tiles())
    bias = _bias_tiles(rel_bias, buckets, "bias_tiles").reshape(N_BIAS_HEADS // 2, 2, 2, BLK, 2 * BLK)
    saved = []
    gain0 = ws[0]["attn_pre_norm"] if tok is None else ws[0]["attn_pre_norm"] + tok
    h1 = _prenorm(x, gain0, "l0_attn_pre")
    for l in range(DEPTH):
        sv = _layer_fwd(x, h1, ws[l], bias, f"l{l}")
        saved.append(sv)
        g_next = ws[l + 1]["attn_pre_norm"] if l + 1 < DEPTH else ws[l]["attn_pre_norm"]
        x, h1 = _postnorm_res(sv["x2"], sv["fo"], ws[l]["ffn_post_norm"], g_next, f"l{l}_ffn_post")
    dy, loss_tile = _loss_head(x, target, "loss_head")
    grads = [None] * DEPTH
    tok = None
    for l in reversed(range(DEPTH)):
        on_part = None if on_grads is None else functools.partial(on_grads, l)
        dy, grads[l], tok = _layer_bwd(dy, saved[l], ws[l], bias, f"l{l}", tok, on_part)
    g_rel = _bias_grad([grads[l]["bias_g"] for l in range(DEPTH)], buckets, "bias_grad")[:, :N_BIAS_HEADS]
    return loss_tile[0, 0], dy, grads, g_rel


def _coords():
    return lax.axis_index("x"), lax.axis_index("y"), lax.axis_index("c")


def _peer(rel):
    x, y, c = _coords()
    return (1 - x if rel & 4 else x, 1 - y if rel & 2 else y, 1 - c if rel & 1 else c)


def _exchange(srcs, dst_shapes, src_win, dst_win, name):
    nt = len(srcs)

    def body(*refs):
        src_refs, dst_refs = refs[:nt], refs[nt:2 * nt]
        send_sems, recv_sems, local_sems = refs[2 * nt:]
        x, y, c = _coords()
        me = 4 * x + 2 * y + c
        locals_ = []
        for t in range(nt):
            cp = pltpu.make_async_copy(src_win(t, src_refs[t], me), dst_win(t, dst_refs[t], me), local_sems.at[t])
            cp.start()
            locals_.append(cp)
        sends = []
        for rel in range(1, NDEV):
            px, py, pc = _peer(rel)
            q = 4 * px + 2 * py + pc
            for t in range(nt):
                cp = pltpu.make_async_remote_copy(
                    src_ref=src_win(t, src_refs[t], q), dst_ref=dst_win(t, dst_refs[t], me),
                    send_sem=send_sems.at[rel - 1, t], recv_sem=recv_sems.at[rel - 1, t],
                    device_id=(px, py, pc), device_id_type=MESH)
                cp.start()
                sends.append(cp)
        for rel in range(1, NDEV):
            px, py, pc = _peer(rel)
            q = 4 * px + 2 * py + pc
            for t in range(nt):
                pltpu.make_async_remote_copy(
                    src_ref=src_win(t, src_refs[t], me), dst_ref=dst_win(t, dst_refs[t], q),
                    send_sem=send_sems.at[rel - 1, t], recv_sem=recv_sems.at[rel - 1, t],
                    device_id=(px, py, pc), device_id_type=MESH).wait_recv()
        for cp in sends:
            cp.wait_send()
        for cp in locals_:
            cp.wait()

    return pl.pallas_call(
        body, in_specs=[ANY] * nt, out_specs=[ANY] * nt, out_shape=dst_shapes,
        scratch_shapes=[pltpu.SemaphoreType.DMA((NDEV - 1, nt)), pltpu.SemaphoreType.DMA((NDEV - 1, nt)),
                        pltpu.SemaphoreType.DMA((nt,))],
        name=name)(*srcs)


BIG = (("w_in", 1, 864), ("w_br_a", 1, 128), ("w_br_b", 1, 128), ("w_br_c", 1, 128), ("w_out", 0, 128),
       ("w_up", 1, 1024), ("w_down", 0, 512))


NBIG = len(BIG)
BIG_FULL = {"w_in": (D, IN_COLS), "w_br_a": (256, D), "w_br_b": (512, D), "w_br_c": (256, D), "w_out": (D, D),
            "w_up": (D, 2 * D_FF), "w_down": (D_FF, D)}
LAYER_GROUPS = (("in", (0,)), ("mix", (1, 2, 3, 4)), ("ffn", (5, 6)))

HBM_SPEC = pl.BlockSpec(memory_space=pltpu.HBM)
SEM_SPEC = pl.BlockSpec(memory_space=pltpu.SEMAPHORE)


def _hbm(a):
    return pltpu.with_memory_space_constraint(a, pltpu.HBM)


def _shard_window(t, ref, k):
    nm, ax, ext = BIG[t % NBIG]
    if nm == "w_in":
        return ref.at[k]
    off = pl.multiple_of(k * ext, ext)
    if ax == 0:
        return ref.at[pl.ds(off, ext), :]
    return ref.at[:, pl.ds(off, ext)]


def _whole(t, ref, k):
    return ref


def _slot(t, ref, k):
    return ref.at[k]


def _own_block_spec(t, rows, me_of):
    nm, ax, ext = BIG[t % NBIG]
    r, c = BIG_FULL[nm]
    if nm == "w_in":
        return pl.BlockSpec((None, rows, ext), lambda i, m: (me_of(m), i, 0))
    if ax == 0:
        return pl.BlockSpec((rows, c), lambda i, m: (me_of(m) * (ext // rows) + i, 0))
    return pl.BlockSpec((rows, ext), lambda i, m: (i, me_of(m)))


def _cast_own(t, shard, me_arr, name):
    nm, ax, ext = BIG[t % NBIG]
    nr, nc = shard.shape
    rows = min(nr, 256)
    shape = (NDEV, D, ext) if nm == "w_in" else BIG_FULL[nm]

    def body(m_ref, s_ref, o_ref):
        o_ref[...] = s_ref[...].astype(BF16)

    return pl.pallas_call(
        body, grid_spec=pltpu.PrefetchScalarGridSpec(
            num_scalar_prefetch=1, grid=(nr // rows,),
            in_specs=[pl.BlockSpec((rows, nc), lambda i, m: (i, 0))],
            out_specs=_own_block_spec(t, rows, lambda m: m[0])),
        out_shape=jax.ShapeDtypeStruct(shape, BF16), compiler_params=_cp("arbitrary"), name=name)(me_arr, shard)


def _xchg_start(srcs, lands, groups, src_win, dst_win, after, name):
    ns = 0 if srcs is None else len(srcs)
    nt, ng = len(lands), len(groups)
    ins = ([] if srcs is None else list(srcs)) + list(lands)

    def body(*refs):
        src_refs, land_refs = refs[:ns], refs[ns:ns + nt]
        sems = refs[ns + nt + 1:ns + nt + 1 + 2 * ng]
        token = refs[-1]
        x, y, c = _coords()
        me = 4 * x + 2 * y + c
        for gi, grp in enumerate(groups):
            for j, t in enumerate(grp):
                for rel in range(1, NDEV):
                    px, py, pc = _peer(rel)
                    q = 4 * px + 2 * py + pc
                    src = dst_win(t, land_refs[t], me) if srcs is None else src_win(t, src_refs[t], q)
                    pltpu.make_async_remote_copy(
                        src_ref=src, dst_ref=dst_win(t, land_refs[t], me),
                        send_sem=sems[2 * gi].at[(rel - 1) * len(grp) + j],
                        recv_sem=sems[2 * gi + 1].at[(rel - 1) * len(grp) + j],
                        device_id=(px, py, pc), device_id_type=MESH).start()
        token[...] = jnp.zeros((8, LANES), F32)

    out_shape = []
    for grp in groups:
        out_shape += [pltpu.SemaphoreType.DMA(((NDEV - 1) * len(grp),))] * 2
    out_shape += [pltpu.HBM(a.shape, a.dtype) for a in ins]
    out_shape.append(jax.ShapeDtypeStruct((8, LANES), F32))
    outs = pl.pallas_call(
        body, in_specs=[HBM_SPEC] * len(ins) + [ANY],
        out_specs=[SEM_SPEC] * (2 * ng) + [HBM_SPEC] * len(ins) + [pl.BlockSpec(memory_space=pltpu.VMEM)],
        out_shape=out_shape, input_output_aliases={i: 2 * ng + i for i in range(len(ins))},
        compiler_params=pltpu.CompilerParams(has_side_effects=pltpu.SideEffectType.DATAFLOW_SIDE_EFFECTING),
        name=name)(*[_hbm(a) for a in ins], after)
    sems = [(outs[2 * gi], outs[2 * gi + 1]) for gi in range(ng)]
    thru = list(outs[2 * ng:2 * ng + len(ins)])
    return sems, (None if srcs is None else thru[:ns]), thru[ns:], outs[-1]


def _xchg_wait(sems, srcs, lands, tids, after, src_win, dst_win, name):
    ns = 0 if srcs is None else len(srcs)
    n = len(lands)
    send_sem, recv_sem = sems
    ins = ([] if srcs is None else list(srcs)) + list(lands)

    def body(*refs):
        src_refs, land_refs = refs[:ns], refs[ns:ns + n]
        ssem, rsem = refs[ns + n], refs[ns + n + 1]
        x, y, c = _coords()
        me = 4 * x + 2 * y + c
        for j, t in enumerate(tids):
            for rel in range(1, NDEV):
                px, py, pc = _peer(rel)
                q = 4 * px + 2 * py + pc
                src = dst_win(t, land_refs[j], me) if srcs is None else src_win(t, src_refs[j], q)
                cp = pltpu.make_async_remote_copy(
                    src_ref=src, dst_ref=dst_win(t, land_refs[j], q),
                    send_sem=ssem.at[(rel - 1) * n + j], recv_sem=rsem.at[(rel - 1) * n + j],
                    device_id=(px, py, pc), device_id_type=MESH)
                cp.wait_send()
                cp.wait_recv()

    outs = pl.pallas_call(
        body, in_specs=[HBM_SPEC] * len(ins) + [SEM_SPEC, SEM_SPEC, ANY], out_specs=[HBM_SPEC] * len(ins),
        out_shape=[pltpu.HBM(a.shape, a.dtype) for a in ins],
        input_output_aliases={i: i for i in range(len(ins))},
        compiler_params=pltpu.CompilerParams(has_side_effects=pltpu.SideEffectType.DATAFLOW_SIDE_EFFECTING),
        name=name)(*ins, send_sem, recv_sem, after)
    return (None if srcs is None else list(outs[:ns])), list(outs[ns:])


class _Weights:
    def __init__(self, ready, pending=None):
        self.ready = dict(ready)
        self.pending = dict(pending or {})

    def __getitem__(self, k):
        return self.ready[k]

    def need(self, group, after):
        fn = self.pending.pop(group, None)
        if fn is not None:
            self.ready.update(fn(after))


def _adamw_math(w, g, m, v):
    m2 = ADAM_B1 * m + (1.0 - ADAM_B1) * g
    v2 = ADAM_B2 * v + (1.0 - ADAM_B2) * (g * g)
    m_hat = m2 / (1.0 - ADAM_B1 ** ADAM_STEP)
    v_hat = v2 / (1.0 - ADAM_B2 ** ADAM_STEP)
    delta = -ADAM_LR * (m_hat / (jnp.sqrt(v_hat) + ADAM_EPS) + ADAM_WD * w)
    return delta, m2, v2


def _adamw(t, parts, own, me_arr, w, m, v, layer, prev, rows, name):
    nl, nr, nc = w.shape

    def body(me_ref, p_ref, own_ref, w_ref, m_ref, v_ref, *rest):
        g_ref, d_ref, m2_ref, v2_ref = rest[-4:]
        me = me_ref[0]
        g = None
        for k in range(NDEV):
            term = jnp.where(me == k, own_ref[...], p_ref[k]).astype(F32)
            g = term if g is None else g + term
        delta, m2, v2 = _adamw_math(w_ref[...], g, m_ref[...], v_ref[...])
        g_ref[...] = g
        d_ref[...] = delta
        m2_ref[...] = m2
        v2_ref[...] = v2

    blk = pl.BlockSpec((None, rows, nc), lambda i, mm: (layer, i, 0))
    pblk = pl.BlockSpec((NDEV, rows, nc), lambda i, mm: (0, i, 0))
    extra = [] if prev is None else list(prev)
    return pl.pallas_call(
        body, grid_spec=pltpu.PrefetchScalarGridSpec(
            num_scalar_prefetch=1, grid=(nr // rows,),
            in_specs=[pblk, _own_block_spec(t, rows, lambda mm: mm[0]), blk, blk, blk] + [ANY] * len(extra),
            out_specs=[blk] * 4),
        out_shape=[jax.ShapeDtypeStruct(w.shape, F32)] * 4,
        input_output_aliases={6 + k: k for k in range(len(extra))},
        compiler_params=_cp("arbitrary"), name=name)(me_arr, parts, own, w, m, v, *extra)


SMALL_REPL = (("rel_bias", NUM_BUCKETS * N_BIAS_HEADS), ("attn_pre_norm", DEPTH * D), ("sinks", DEPTH * 8),
              ("attn_post_norm", DEPTH * D), ("ffn_pre_norm", DEPTH * D), ("conv_b", DEPTH * 2 * D_FF),
              ("ffn_post_norm", DEPTH * D))
SMALL_SHARD = (("b_gate", (DEPTH, 3, D), 128), ("conv_w", (DEPTH, 3, 2 * D_FF), 1024))


def _pack(vecs):
    flat = jnp.concatenate([v.reshape(-1).astype(F32) for v in vecs])
    n = flat.shape[0]
    rows = -(-n // (8 * LANES)) * 8
    return jnp.pad(flat, (0, rows * LANES - n)).reshape(rows, LANES)


def _unpack(packed, sizes):
    flat = packed.reshape(-1)
    out, off = [], 0
    for sz in sizes:
        out.append(flat[off:off + sz])
        off += sz
    return out


ROWPACK = (("rel_bias", 32, 32, (NUM_BUCKETS, N_BIAS_HEADS)), ("sinks", 8, 8, (DEPTH, 8)),
           ("attn_pre_norm", 16, 16, (DEPTH, D)), ("attn_post_norm", 16, 16, (DEPTH, D)),
           ("ffn_pre_norm", 16, 16, (DEPTH, D)), ("ffn_post_norm", 16, 16, (DEPTH, D)),
           ("conv_b", 128, 128, (DEPTH, 2 * D_FF)), ("b_gate", 48, 8, (DEPTH, 3, 128)),
           ("conv_w", 384, 48, (DEPTH, 3, 1024)))
ROWS_FULL = sum(r for _, r, _, _ in ROWPACK)
ROWS_OWN = sum(r for _, _, r, _ in ROWPACK)


def _as_rows(a, rows):
    a = a.astype(F32)
    if a.shape[-1] < LANES:
        a = jnp.pad(a.reshape(-1, a.shape[-1]), ((0, 0), (0, LANES - a.shape[-1])))
    a = a.reshape(-1, LANES)
    return jnp.pad(a, ((0, rows - a.shape[0]), (0, 0)))


def _rowpack(arrs, own):
    return jnp.concatenate([_as_rows(arrs[nm], ro if own else rf) for nm, rf, ro, _ in ROWPACK], axis=0)


def _small_update(parts, w, m, v, me_arr, name):
    nsm = len(ROWPACK)

    def body(me_ref, p_ref, w_ref, m_ref, v_ref, *rest):
        outs = rest[:4 * nsm]
        gfull, g_s, d_s, m_s, v_s = rest[4 * nsm:]
        me = me_ref[0]
        g = p_ref[0]
        for k in range(1, NDEV):
            g = g + p_ref[k]
        gfull[...] = g
        of, oo = 0, 0
        for nm, rf, ro, _ in ROWPACK:
            if nm == "b_gate":
                g_s[oo:oo + ro, :] = jnp.zeros((ro, LANES), F32)
                for r in range(DEPTH * 3):
                    g_s[oo + r:oo + r + 1, :] = gfull[pl.ds(of + r * NDEV + me, 1), :]
            elif nm == "conv_w":
                for r in range(DEPTH * 3):
                    g_s[oo + r * 8:oo + r * 8 + 8, :] = gfull[pl.ds(pl.multiple_of(of + r * 64 + me * 8, 8), 8), :]
            else:
                g_s[oo:oo + ro, :] = gfull[of:of + rf, :]
            of, oo = of + rf, oo + ro
        delta, m2, v2 = _adamw_math(w_ref[...], g_s[...], m_ref[...], v_ref[...])
        d_s[...] = delta
        m_s[...] = m2
        v_s[...] = v2
        for kind, src in enumerate((g_s, d_s, m_s, v_s)):
            oo = 0
            for idx, (nm, rf, ro, shp) in enumerate(ROWPACK):
                o_ref = outs[kind * nsm + idx]
                if nm in ("rel_bias", "sinks"):
                    o_ref[...] = src[oo:oo + shp[0], 0:shp[1]]
                elif nm == "b_gate":
                    for l in range(DEPTH):
                        o_ref[l] = src[oo + 3 * l:oo + 3 * l + 3, :]
                elif nm == "conv_w":
                    for l in range(DEPTH):
                        for k in range(8):
                            o_ref[l, :, k * LANES:(k + 1) * LANES] = src[pl.ds(oo + 24 * l + k, 3, stride=8), :]
                else:
                    per = shp[1] // LANES
                    for k in range(per):
                        o_ref[:, k * LANES:(k + 1) * LANES] = src[pl.ds(oo + k, DEPTH, stride=per), :]
                oo += ro

    vm = pl.BlockSpec(memory_space=pltpu.VMEM)
    shapes = [jax.ShapeDtypeStruct(shp, F32) for _ in range(4) for _, _, _, shp in ROWPACK]
    outs = pl.pallas_call(
        body, in_specs=[SMEM, vm, vm, vm, vm], out_specs=[vm] * (4 * nsm), out_shape=shapes,
        scratch_shapes=[pltpu.VMEM((ROWS_FULL, LANES), F32)] + [pltpu.VMEM((ROWS_OWN, LANES), F32)] * 4,
        name=name)(me_arr, parts, w, m, v)
    names = [nm for nm, _, _, _ in ROWPACK]
    return [dict(zip(names, outs[kind * nsm:(kind + 1) * nsm])) for kind in range(4)]


def kernel(x, rel_bias, attn_pre_norm, w_in, b_gate, sinks, w_br_a, w_br_b, w_br_c, w_out, attn_post_norm, ffn_pre_norm, w_up, conv_w, conv_b, w_down, ffn_post_norm, loss_target, m_rel_bias, m_attn_pre_norm, m_w_in, m_b_gate, m_sinks, m_w_br_a, m_w_br_b, m_w_br_c, m_w_out, m_attn_post_norm, m_ffn_pre_norm, m_w_up, m_conv_w, m_conv_b, m_w_down, m_ffn_post_norm, v_rel_bias, v_attn_pre_norm, v_w_in, v_b_gate, v_sinks, v_w_br_a, v_w_br_b, v_w_br_c, v_w_out, v_attn_post_norm, v_ffn_pre_norm, v_w_up, v_conv_w, v_conv_b, v_w_down, v_ffn_post_norm):
    P = dict(rel_bias=rel_bias, attn_pre_norm=attn_pre_norm, w_in=w_in, b_gate=b_gate, sinks=sinks, w_br_a=w_br_a,
             w_br_b=w_br_b, w_br_c=w_br_c, w_out=w_out, attn_post_norm=attn_post_norm, ffn_pre_norm=ffn_pre_norm,
             w_up=w_up, conv_w=conv_w, conv_b=conv_b, w_down=w_down, ffn_post_norm=ffn_post_norm)
    M = dict(rel_bias=m_rel_bias, attn_pre_norm=m_attn_pre_norm, w_in=m_w_in, b_gate=m_b_gate, sinks=m_sinks,
             w_br_a=m_w_br_a, w_br_b=m_w_br_b, w_br_c=m_w_br_c, w_out=m_w_out, attn_post_norm=m_attn_post_norm,
             ffn_pre_norm=m_ffn_pre_norm, w_up=m_w_up, conv_w=m_conv_w, conv_b=m_conv_b, w_down=m_w_down,
             ffn_post_norm=m_ffn_post_norm)
    V = dict(rel_bias=v_rel_bias, attn_pre_norm=v_attn_pre_norm, w_in=v_w_in, b_gate=v_b_gate, sinks=v_sinks,
             w_br_a=v_w_br_a, w_br_b=v_w_br_b, w_br_c=v_w_br_c, w_out=v_w_out, attn_post_norm=v_attn_post_norm,
             ffn_pre_norm=v_ffn_pre_norm, w_up=v_w_up, conv_w=v_conv_w, conv_b=v_conv_b, w_down=v_w_down,
             ffn_post_norm=v_ffn_post_norm)
    xi, yi, ci = _coords()
    me = 4 * xi + 2 * yi + ci

    me_arr = me.astype(jnp.int32).reshape(1)

    small_w = _pack([b_gate.reshape(-1), conv_w.reshape(-1)])
    (small_w_all,) = _exchange([small_w], [jax.ShapeDtypeStruct((NDEV,) + small_w.shape, F32)],
                               _whole, _slot, "gather_small_weights")

    lands = [_cast_own(l * NBIG + t, P[nm][l], me_arr, f"gather_own_l{l}_{nm}")
             for l in range(DEPTH) for t, (nm, _, _) in enumerate(BIG)]
    groups = [tuple(l * NBIG + t for t in tids) for l in range(DEPTH) for _, tids in LAYER_GROUPS]
    g_sems, _, g_lands, g_tok = _xchg_start(None, lands, groups, None, _shard_window, small_w_all, "gather_start")
    tok0 = g_tok[0:1, 0:1]

    def gather_waiter(gi, l, gname, tids):
        def wait(after):
            ids = [l * NBIG + t for t in tids]
            _, got = _xchg_wait(g_sems[gi], None, [g_lands[i] for i in ids], ids, after,
                                None, _shard_window, f"gather_wait_l{l}_{gname}")
            out = {}
            for t, arr in zip(tids, got):
                nm = BIG[t][0]
                out[nm] = jnp.transpose(arr, (1, 0, 2)).reshape(D, IN_COLS) if nm == "w_in" else arr
            return out
        return wait

    pending = [{gname: gather_waiter(l * len(LAYER_GROUPS) + k, l, gname, tids)
                for k, (gname, tids) in enumerate(LAYER_GROUPS)} for l in range(DEPTH)]
    nbg = DEPTH * 3 * 128
    ncw = DEPTH * 3 * 1024
    flat_all = small_w_all.reshape(NDEV, -1)
    b_gate_full = jnp.transpose(flat_all[:, :nbg].reshape(NDEV, DEPTH, 3, 128), (1, 2, 0, 3)).reshape(DEPTH, 3, D)
    conv_w_full = jnp.transpose(flat_all[:, nbg:nbg + ncw].reshape(NDEV, DEPTH, 3, 1024), (1, 2, 0, 3)).reshape(DEPTH, 3, 2 * D_FF)

    ws = []
    for l in range(DEPTH):
        ws.append(_Weights(dict(
            b_gate=b_gate_full[l], conv_w=conv_w_full[l].reshape(3, 2, D_FF), conv_b=conv_b[l].reshape(2, D_FF),
            sinks=sinks[l].reshape(1, 8),
            attn_pre_norm=attn_pre_norm[l].reshape(1, D), attn_post_norm=attn_post_norm[l].reshape(1, D),
            ffn_pre_norm=ffn_pre_norm[l].reshape(1, D), ffn_post_norm=ffn_post_norm[l].reshape(1, D)), pending[l]))

    rs = {}

    group_tids = dict(LAYER_GROUPS)

    def start_scatter(l, gname, grads_l):
        tids = group_tids[gname]
        blocks, lands_rs = [], []
        for t in tids:
            nm, ax, ext = BIG[t]
            gfull = grads_l[nm].astype(BF16)
            if nm == "w_in":
                gfull = jnp.transpose(gfull.reshape(D, NDEV, ext), (1, 0, 2))
                shp = (NDEV, D, ext)
            else:
                shp = (NDEV, ext, gfull.shape[1]) if ax == 0 else (NDEV, gfull.shape[0], ext)
            blocks.append(gfull)
            lands_rs.append(lax.empty(shp, BF16))
        local = list(range(len(tids)))
        win = lambda j, ref, k: _shard_window(tids[j], ref, k)
        sems, s_thru, l_thru, tok = _xchg_start(blocks, lands_rs, [tuple(local)], win, _slot, me_arr,
                                                f"scatter_start_l{l}_{gname}")
        rs[(l, gname)] = (sems[0], s_thru, l_thru, win, local)
        return tok[0:1, 0:1]

    loss_local, grad_x, grads, g_rel = _local_step(x[0], loss_target[0], ws, rel_bias, tok0, start_scatter)
    loss = lax.psum(loss_local, ("x", "y", "c"))

    stack = lambda nm: jnp.stack([grads[l][nm] for l in range(DEPTH)], axis=0)
    small_names = [nm for nm, _ in SMALL_REPL] + [nm for nm, _, _ in SMALL_SHARD]
    small_g = {"rel_bias": g_rel}
    for nm in small_names[1:]:
        small_g[nm] = stack(nm)
    small_packed = _rowpack(small_g, own=False)
    (small_parts,) = _exchange([small_packed], [jax.ShapeDtypeStruct((NDEV,) + small_packed.shape, F32)],
                               _whole, _slot, "gather_small_grads")

    out_g, out_d, out_m, out_v = {}, {}, {}, {}
    prev = {nm: None for nm, _, _ in BIG}
    for l in reversed(range(DEPTH)):
        for gname in ("ffn", "mix", "in"):
            sems, s_thru, l_thru, win, local = rs[(l, gname)]
            owns, parts = _xchg_wait(sems, s_thru, l_thru, local, small_parts, win, _slot, f"scatter_wait_l{l}_{gname}")
            for t, own, prt in zip(group_tids[gname], owns, parts):
                nm = BIG[t][0]
                rows = {"w_in": 256, "w_up": 256, "w_down": 256}.get(nm, P[nm].shape[1])
                prev[nm] = _adamw(t, prt, own, me_arr, P[nm], M[nm], V[nm], l, prev[nm], rows, f"adamw_{nm}_l{l}")
    for nm, _, _ in BIG:
        out_g[nm], out_d[nm], out_m[nm], out_v[nm] = prev[nm]
    sm_g, sm_d, sm_m, sm_v = _small_update(small_parts, _rowpack(P, True), _rowpack(M, True), _rowpack(V, True),
                                           me_arr, "small_update")
    for dst, src in ((out_g, sm_g), (out_d, sm_d), (out_m, sm_m), (out_v, sm_v)):
        dst.update(src)

    order = ["rel_bias", "attn_pre_norm", "w_in", "b_gate", "sinks", "w_br_a", "w_br_b", "w_br_c", "w_out",
             "attn_post_norm", "ffn_pre_norm", "w_up", "conv_w", "conv_b", "w_down", "ffn_post_norm"]
    return (loss, grad_x[None], *[out_g[k] for k in order], *[out_d[k] for k in order],
            *[out_m[k] for k in order], *[out_v[k] for k in order])
```

```python
import functools
import math

import numpy as np
import jax
import jax.numpy as jnp
from jax import lax
from jax.experimental import pallas as pl
from jax.experimental.pallas import tpu as pltpu

F32 = jnp.float32
BF16 = jnp.bfloat16

S = 2048
D = 1024
DEPTH = 2
NDEV = 8
HD = 64
BLK = 128
NB = S // BLK
A_GROUPS = ((128, 1), (512, 4), (2048, 16))
NUM_BUCKETS = 32
MAX_DISTANCE = 2048
N_BIAS_HEADS = 20
D_FF = 4096
IN_COLS = 6912
QKV_COLS = 3840
QKV_SLABS = QKV_COLS // 128
GATE_COLS = 3072
EPS = 1e-6
SCALE = HD ** -0.5
NEG = -1e30
LANES = 128

ADAM_LR = 0.001
ADAM_B1 = 0.9
ADAM_B2 = 0.999
ADAM_EPS = 1e-08
ADAM_WD = 0.01
ADAM_STEP = 10

VMEM_LIMIT = 56 * 1024 * 1024
MESH = pl.DeviceIdType.MESH
ANY = pl.BlockSpec(memory_space=pl.ANY)
SMEM = pl.BlockSpec(memory_space=pltpu.SMEM)


def _cp(*sem):
    return pltpu.CompilerParams(dimension_semantics=sem if sem else None, vmem_limit_bytes=VMEM_LIMIT)


def _dot(a, b, ca, cb):
    return lax.dot_general(a, b, (((ca,), (cb,)), ((), ())), preferred_element_type=F32)


def _mm(a, b, *, grid, a_spec, b_spec, out_shape, out_spec, ca, cb, acc_shape, name,
        a_slab=False, b_slab=False, out_slab=False, alias_out=None, after=None):
    nk = grid[2]

    def body(*refs):
        a_ref, b_ref = refs[0], refs[1]
        o_ref, acc_ref = refs[-2], refs[-1]
        k = pl.program_id(2)

        def load(ref, slab):
            if slab:
                return jnp.concatenate([ref[s] for s in range(ref.shape[0])], axis=1).astype(BF16)
            return ref[...].astype(BF16)

        def write(val):
            if out_slab:
                for s in range(o_ref.shape[0]):
                    o_ref[s] = val[:, s * LANES:(s + 1) * LANES].astype(o_ref.dtype)
            else:
                o_ref[...] = val.astype(o_ref.dtype)

        d = _dot(load(a_ref, a_slab), load(b_ref, b_slab), ca, cb)
        if nk == 1:
            write(d)
        elif direct:
            @pl.when(k == 0)
            def _():
                o_ref[...] = d

            @pl.when(k > 0)
            def _():
                o_ref[...] += d
        else:
            @pl.when(k == 0)
            def _():
                acc_ref[...] = d

            if nk > 2:
                @pl.when((k > 0) & (k < nk - 1))
                def _():
                    acc_ref[...] += d

            @pl.when(k == nk - 1)
            def _():
                write(acc_ref[...] + d)

    direct = (not out_slab) and out_shape.dtype == F32
    if nk == 1 or direct:
        acc_shape = (8, LANES)
    in_specs = [a_spec, b_spec]
    args = [a, b]
    aliases = {}
    if alias_out is not None:
        in_specs.append(ANY)
        args.append(alias_out)
        aliases = {2: 0}
    if after is not None:
        in_specs.append(ANY)
        args.append(after)
    return pl.pallas_call(
        body, grid=grid, in_specs=in_specs, out_specs=out_spec, out_shape=out_shape,
        scratch_shapes=[pltpu.VMEM(acc_shape, F32)], input_output_aliases=aliases,
        compiler_params=_cp("parallel", "parallel", "arbitrary"), name=name)(*args)


def _mm_nn(a, b, out_dtype, tm, tn, tk, name):
    m, kk = a.shape
    n = b.shape[1]
    return _mm(a, b, grid=(m // tm, n // tn, kk // tk),
               a_spec=pl.BlockSpec((tm, tk), lambda i, j, k: (i, k)),
               b_spec=pl.BlockSpec((tk, tn), lambda i, j, k: (k, j)),
               out_shape=jax.ShapeDtypeStruct((m, n), out_dtype),
               out_spec=pl.BlockSpec((tm, tn), lambda i, j, k: (i, j)),
               ca=1, cb=0, acc_shape=(tm, tn), name=name)


def _mm_nt(a, b, out_dtype, tm, tn, tk, name):
    m, kk = a.shape
    n = b.shape[0]
    return _mm(a, b, grid=(m // tm, n // tn, kk // tk),
               a_spec=pl.BlockSpec((tm, tk), lambda i, j, k: (i, k)),
               b_spec=pl.BlockSpec((tn, tk), lambda i, j, k: (j, k)),
               out_shape=jax.ShapeDtypeStruct((m, n), out_dtype),
               out_spec=pl.BlockSpec((tm, tn), lambda i, j, k: (i, j)),
               ca=1, cb=1, acc_shape=(tm, tn), name=name)


def _mm_tn(a, b, out_dtype, tm, tn, tk, name):
    kk, m = a.shape
    n = b.shape[1]
    return _mm(a, b, grid=(m // tm, n // tn, kk // tk),
               a_spec=pl.BlockSpec((tk, tm), lambda i, j, k: (k, i)),
               b_spec=pl.BlockSpec((tk, tn), lambda i, j, k: (k, j)),
               out_shape=jax.ShapeDtypeStruct((m, n), out_dtype),
               out_spec=pl.BlockSpec((tm, tn), lambda i, j, k: (i, j)),
               ca=0, cb=0, acc_shape=(tm, tn), name=name)


ROW_TILE = 256


def _rms(x, g):
    r = lax.rsqrt(jnp.mean(x * x, axis=-1, keepdims=True) + EPS)
    return x * r * g


def _prenorm(x, g, name):
    def body(x_ref, g_ref, o_ref):
        o_ref[...] = _rms(x_ref[...], g_ref[...]).astype(BF16)

    return pl.pallas_call(
        body, grid=(S // ROW_TILE,),
        in_specs=[pl.BlockSpec((ROW_TILE, D), lambda i: (i, 0)), pl.BlockSpec((1, D), lambda i: (0, 0))],
        out_specs=pl.BlockSpec((ROW_TILE, D), lambda i: (i, 0)),
        out_shape=jax.ShapeDtypeStruct((S, D), BF16), compiler_params=_cp("parallel"), name=name)(x, g)


def _postnorm_res(x, f, g_post, g_next, name):
    def body(x_ref, f_ref, gp_ref, gn_ref, xo_ref, ho_ref):
        xn = x_ref[...] + _rms(f_ref[...], gp_ref[...])
        xo_ref[...] = xn
        ho_ref[...] = _rms(xn, gn_ref[...]).astype(BF16)

    row = pl.BlockSpec((ROW_TILE, D), lambda i: (i, 0))
    vec = pl.BlockSpec((1, D), lambda i: (0, 0))
    return pl.pallas_call(
        body, grid=(S // ROW_TILE,), in_specs=[row, row, vec, vec], out_specs=[row, row],
        out_shape=[jax.ShapeDtypeStruct((S, D), F32), jax.ShapeDtypeStruct((S, D), BF16)],
        compiler_params=_cp("parallel"), name=name)(x, f, g_post, g_next)


def _norm_bwd(f, g, dys, res, out_dtype, name):
    ndy = len(dys)
    has_res = res is not None

    def body(*refs):
        f_ref, g_ref = refs[0], refs[1]
        dy_refs = refs[2:2 + ndy]
        res_ref = refs[2 + ndy] if has_res else None
        o_ref, dg_ref = refs[-2], refs[-1]
        fv = f_ref[...]
        dy = dy_refs[0][...].astype(F32)
        for r in dy_refs[1:]:
            dy = dy + r[...].astype(F32)
        r = lax.rsqrt(jnp.mean(fv * fv, axis=-1, keepdims=True) + EPS)
        n = fv * r
        dn = dy * g_ref[...]
        df = r * (dn - n * jnp.mean(dn * n, axis=-1, keepdims=True))
        if has_res:
            df = df + res_ref[...]
        o_ref[...] = df.astype(out_dtype)

        @pl.when(pl.program_id(0) == 0)
        def _():
            dg_ref[...] = jnp.zeros((1, D), F32)

        dg_ref[...] += jnp.sum(dy * n, axis=0, keepdims=True)

    row = pl.BlockSpec((ROW_TILE, D), lambda i: (i, 0))
    vec = pl.BlockSpec((1, D), lambda i: (0, 0))
    in_specs = [row, vec] + [row] * ndy + ([row] if has_res else [])
    args = [f, g] + list(dys) + ([res] if has_res else [])
    return pl.pallas_call(
        body, grid=(S // ROW_TILE,), in_specs=in_specs, out_specs=[row, vec],
        out_shape=[jax.ShapeDtypeStruct((S, D), out_dtype), jax.ShapeDtypeStruct((1, D), F32)],
        compiler_params=_cp("arbitrary"), name=name)(*args)


def _loss_head(y, target, name):
    def body(y_ref, t_ref, dy_ref, l_ref):
        e = y_ref[...] - t_ref[...]
        dy_ref[...] = e * (1.0 / D)

        @pl.when(pl.program_id(0) == 0)
        def _():
            l_ref[...] = jnp.zeros((8, LANES), F32)

        l_ref[...] += jnp.sum(e * e) * (0.5 / D)

    row = pl.BlockSpec((ROW_TILE, D), lambda i: (i, 0))
    return pl.pallas_call(
        body, grid=(S // ROW_TILE,), in_specs=[row, row],
        out_specs=[row, pl.BlockSpec((8, LANES), lambda i: (0, 0))],
        out_shape=[jax.ShapeDtypeStruct((S, D), F32), jax.ShapeDtypeStruct((8, LANES), F32)],
        compiler_params=_cp("arbitrary"), name=name)(y, target)


def _bucket_tiles():
    a = np.arange(BLK)[:, None]
    b = np.arange(2 * BLK)[None, :]
    dist = a + BLK - b
    out = np.zeros((4, 2, BLK, 2 * BLK), np.int32)
    cfg = [(w // d, d) for w, d in A_GROUPS] + [(BLK - 1, 1)]
    for gi, (max_dist, d) in enumerate(cfg):
        band = (dist >= 0) & (dist <= max_dist)
        tok = np.maximum(dist, 0) * d
        nf = np.maximum(tok, 1).astype(np.float32)
        max_exact = NUM_BUCKETS // 2
        large = max_exact + (np.log(nf / np.float32(max_exact)) / np.float32(math.log(MAX_DISTANCE / max_exact))
                             * np.float32(NUM_BUCKETS - max_exact)).astype(np.int32)
        large = np.minimum(large, NUM_BUCKETS - 1)
        bkt = np.where(tok < max_exact, tok, large).astype(np.int32)
        full = np.where(band, bkt, -1)
        out[gi, 1] = full
        out[gi, 0] = np.where(b >= BLK, full, -1)
    return out


def _bias_tiles(rel_bias, buckets, name):
    def body(tab_ref, bkt_ref, o_ref):
        h = pl.program_id(0)
        bkt = bkt_ref[...]
        acc = jnp.zeros(bkt.shape, F32)
        for bb in range(NUM_BUCKETS):
            acc = jnp.where(bkt == bb, tab_ref[bb, h], acc)
        o_ref[...] = jnp.where(bkt < 0, NEG, acc)

    return pl.pallas_call(
        body, grid=(N_BIAS_HEADS,),
        in_specs=[SMEM, pl.BlockSpec((None, 2, BLK, 2 * BLK), lambda h: (jnp.minimum(h // 4, 3), 0, 0, 0))],
        out_specs=pl.BlockSpec((None, 2, BLK, 2 * BLK), lambda h: (h, 0, 0, 0)),
        out_shape=jax.ShapeDtypeStruct((N_BIAS_HEADS, 2, BLK, 2 * BLK), F32),
        compiler_params=_cp("arbitrary"), name=name)(rel_bias, buckets)


def _bias_grad(gs, buckets, name):
    ng = len(gs)

    def body(*refs):
        g_refs = refs[:ng]
        bkt_ref, o_ref = refs[ng], refs[ng + 1]
        h = pl.program_id(0)
        g = g_refs[0][...]
        for r in g_refs[1:]:
            g = g + r[...]
        bkt = bkt_ref[...]
        row = lax.broadcasted_iota(jnp.int32, (NUM_BUCKETS, LANES), 0)
        lane = lax.broadcasted_iota(jnp.int32, (NUM_BUCKETS, LANES), 1)

        @pl.when(h == 0)
        def _():
            o_ref[...] = jnp.zeros((NUM_BUCKETS, LANES), F32)

        acc = o_ref[...]
        for bb in range(NUM_BUCKETS):
            s = jnp.sum(jnp.where(bkt == bb, g, 0.0))
            acc = jnp.where((row == bb) & (lane == h), s, acc)
        o_ref[...] = acc

    g_spec = pl.BlockSpec((None, BLK, 2 * BLK), lambda h: (h, 0, 0))
    return pl.pallas_call(
        body, grid=(N_BIAS_HEADS,),
        in_specs=[g_spec] * ng + [pl.BlockSpec((None, None, BLK, 2 * BLK), lambda h: (jnp.minimum(h // 4, 3), 1, 0, 0))],
        out_specs=pl.BlockSpec((NUM_BUCKETS, LANES), lambda h: (0, 0)),
        out_shape=jax.ShapeDtypeStruct((NUM_BUCKETS, LANES), F32),
        compiler_params=_cp("arbitrary"), name=name)(*gs, buckets)


def _to_class_major(src_ref, dst_ref, d, scale=None, dtype=None):
    ln = S // d
    for r in range(d):
        v = src_ref[pl.ds(r, ln, stride=d), :] if d > 1 else src_ref[...]
        if scale is not None:
            v = v * scale
        dst_ref[pl.ds(r * ln, ln), :] = v.astype(dtype or dst_ref.dtype)


def _block_rows(b, d):
    nbc = NB // d
    i = b % nbc
    r = b // nbc
    has_prev = (i > 0).astype(jnp.int32)
    prev = pl.multiple_of(jnp.maximum(b - 1, 0) * BLK, BLK)
    nat = i * (BLK * d) + r
    return has_prev, prev, nat


def _lane_halves(v0, v1):
    lane = lax.broadcasted_iota(jnp.int32, (v0.shape[0], LANES), 1)
    return jnp.where(lane < HD, v0, v1)


def _band_fwd(proj, bias, sinks, *, d, q0, k0, v0, npairs, bias0, shared_kv, name):
    def body(sink_ref, q_ref, k_ref, v_ref, b_ref, num_ref, st_ref, qs, ks, vs):
        p = pl.program_id(0)
        _to_class_major(q_ref, qs, d, scale=SCALE)
        _to_class_major(k_ref, ks, d)
        _to_class_major(v_ref, vs, d)
        lane = lax.broadcasted_iota(jnp.int32, (BLK, LANES), 1)

        def blk(b, carry):
            has_prev, prev, nat = _block_rows(b, d)
            cur = pl.multiple_of(b * BLK, BLK)
            qb = qs[pl.ds(cur, BLK), :]
            k2 = jnp.concatenate([ks[pl.ds(prev, BLK), :], ks[pl.ds(cur, BLK), :]], axis=0)
            v2 = jnp.concatenate([vs[pl.ds(prev, BLK), :], vs[pl.ds(cur, BLK), :]], axis=0)
            nums, ms, ls = [], [], []
            for hh in range(2):
                qh = qb[:, hh * HD:(hh + 1) * HD]
                if shared_kv:
                    kh = jnp.where(p >= 2, k2[:, HD:], k2[:, :HD])
                    vh = jnp.where(p >= 2, v2[:, HD:], v2[:, :HD])
                else:
                    kh = k2[:, hh * HD:(hh + 1) * HD]
                    vh = v2[:, hh * HD:(hh + 1) * HD]
                z = _dot(qh, kh, 1, 1) + b_ref[hh, has_prev]
                m = jnp.max(z, axis=1, keepdims=True)
                e = jnp.exp(z - m)
                l = jnp.sum(e, axis=1, keepdims=True)
                num = _dot(e.astype(BF16), vh, 1, 0)
                if shared_kv:
                    sink = sink_ref[0, 2 * p + hh]
                    mx = jnp.maximum(m, sink)
                    c = jnp.exp(m - mx)
                    zden = l * c + jnp.exp(sink - mx)
                    num = num * (c / zden)
                    m = mx + jnp.log(zden)
                ls.append(l)
                ms.append(m)
                nums.append(num)
            num_t = jnp.concatenate(nums, axis=1)
            if shared_kv:
                st_t = jnp.where(lane < HD, ms[0], ms[1])
            else:
                st_t = jnp.where(lane < 32, ms[0], jnp.where(lane < 64, ls[0], jnp.where(lane < 96, ms[1], ls[1])))
            if d > 1:
                num_ref[pl.ds(nat, BLK, stride=d), :] = num_t
                st_ref[pl.ds(nat, BLK, stride=d), :] = st_t
            else:
                num_ref[pl.ds(cur, BLK), :] = num_t
                st_ref[pl.ds(cur, BLK), :] = st_t
            return carry

        lax.fori_loop(0, NB, blk, 0, unroll=8)

    slab = lambda off, per_pair: pl.BlockSpec((None, S, LANES), (lambda p: (off + p, 0, 0)) if per_pair else (lambda p: (off, 0, 0)))
    out = pl.BlockSpec((None, S, LANES), lambda p: (p, 0, 0))
    return pl.pallas_call(
        body, grid=(npairs,),
        in_specs=[SMEM, slab(q0, True), slab(k0, not shared_kv), slab(v0, not shared_kv),
                  pl.BlockSpec((None, 2, 2, BLK, 2 * BLK), lambda p: (bias0 + p, 0, 0, 0, 0))],
        out_specs=[out, out],
        out_shape=[jax.ShapeDtypeStruct((npairs, S, LANES), F32)] * 2,
        scratch_shapes=[pltpu.VMEM((S, LANES), BF16)] * 3,
        compiler_params=_cp("arbitrary"), name=name)(sinks, proj, proj, proj, bias)


def _combine_a(nums, stats, name):
    rt = 512

    def body(n0, n1, n2, s0, s1, s2, o_ref, l_ref):
        n_refs, s_refs = (n0, n1, n2), (s0, s1, s2)
        outs, lses = [], []
        for hh in range(2):
            ms = [s[:, 64 * hh:64 * hh + 1] for s in s_refs]
            ls = [s[:, 64 * hh + 32:64 * hh + 33] for s in s_refs]
            mx = jnp.maximum(jnp.maximum(ms[0], ms[1]), ms[2])
            cs = [jnp.exp(m - mx) for m in ms]
            z = cs[0] * ls[0] + cs[1] * ls[1] + cs[2] * ls[2]
            acc = cs[0] * n_refs[0][:, hh * HD:(hh + 1) * HD]
            acc = acc + cs[1] * n_refs[1][:, hh * HD:(hh + 1) * HD]
            acc = acc + cs[2] * n_refs[2][:, hh * HD:(hh + 1) * HD]
            outs.append(acc / z)
            lses.append(mx + jnp.log(z))
        o_ref[...] = jnp.concatenate(outs, axis=1)
        l_ref[...] = _lane_halves(lses[0], lses[1])

    spec = pl.BlockSpec((None, rt, LANES), lambda p, i: (p, i, 0))
    return pl.pallas_call(
        body, grid=(2, S // rt), in_specs=[spec] * 6, out_specs=[spec, spec],
        out_shape=[jax.ShapeDtypeStruct((2, S, LANES), F32)] * 2,
        compiler_params=_cp("parallel", "parallel"), name=name)(*nums, *stats)


def _band_bwd(proj, bias, o, do, lse, sinks, *, d, q0, k0, v0, npairs, bias0, shared_kv, name):
    nkv = 1 if shared_kv else npairs

    def body(sink_ref, q_ref, k_ref, v_ref, b_ref, o_ref, do_ref, lse_ref,
             dq_ref, dk_ref, dv_ref, g_ref, ds_ref,
             qs, ks, vs, dos, lses, dls, dl_nat, dq_nat, dk_cm, dv_cm, kv_nat):
        p = pl.program_id(0)
        lane = lax.broadcasted_iota(jnp.int32, (S, LANES), 1)
        dov = do_ref[...]
        prod = dov * o_ref[...]
        dl0 = jnp.sum(jnp.where(lane < HD, prod, 0.0), axis=1, keepdims=True)
        dl1 = jnp.sum(jnp.where(lane >= HD, prod, 0.0), axis=1, keepdims=True)
        dl_nat[...] = jnp.where(lane < HD, dl0, dl1)
        if shared_kv:
            row8 = lax.broadcasted_iota(jnp.int32, (8, LANES), 0)
            lane8 = lax.broadcasted_iota(jnp.int32, (8, LANES), 1)
            t = jnp.zeros((8, LANES), F32)
            lv = lse_ref[...]
            for hh in range(2):
                sink = sink_ref[0, 2 * p + hh]
                ps = jnp.exp(sink - lv[:, 64 * hh:64 * hh + 1])
                dsink = -jnp.sum(ps * (dl0 if hh == 0 else dl1))
                t = jnp.where((row8 == 0) & (lane8 == hh), dsink, t)
            ds_ref[...] = t
        else:
            ds_ref[...] = jnp.zeros((8, LANES), F32)
        _to_class_major(q_ref, qs, d, scale=SCALE)
        _to_class_major(k_ref, ks, d)
        _to_class_major(v_ref, vs, d)
        _to_class_major(do_ref, dos, d)
        _to_class_major(lse_ref, lses, d)
        _to_class_major(dl_nat, dls, d)

        def zero_kv():
            dk_cm[...] = jnp.zeros((S, LANES), F32)
            dv_cm[...] = jnp.zeros((S, LANES), F32)

        if shared_kv:
            pl.when(p == 0)(zero_kv)
        else:
            zero_kv()

        g_ref[...] = jnp.zeros((2, BLK, 2 * BLK), F32)
        lane2 = lax.broadcasted_iota(jnp.int32, (2 * BLK, LANES), 1)

        def blk(b, carry):
            has_prev, prev, nat = _block_rows(b, d)
            cur = pl.multiple_of(b * BLK, BLK)
            qb = qs[pl.ds(cur, BLK), :]
            dob = dos[pl.ds(cur, BLK), :]
            lb = lses[pl.ds(cur, BLK), :]
            dlb = dls[pl.ds(cur, BLK), :]
            k2 = jnp.concatenate([ks[pl.ds(prev, BLK), :], ks[pl.ds(cur, BLK), :]], axis=0)
            v2 = jnp.concatenate([vs[pl.ds(prev, BLK), :], vs[pl.ds(cur, BLK), :]], axis=0)
            dqs, dks, dvs = [], [], []
            for hh in range(2):
                qh = qb[:, hh * HD:(hh + 1) * HD]
                doh = dob[:, hh * HD:(hh + 1) * HD]
                if shared_kv:
                    kh = jnp.where(p >= 2, k2[:, HD:], k2[:, :HD])
                    vh = jnp.where(p >= 2, v2[:, HD:], v2[:, :HD])
                else:
                    kh = k2[:, hh * HD:(hh + 1) * HD]
                    vh = v2[:, hh * HD:(hh + 1) * HD]
                z = _dot(qh, kh, 1, 1) + b_ref[hh, has_prev]
                pr = jnp.exp(z - lb[:, 64 * hh:64 * hh + 1])
                dp = _dot(doh, vh, 1, 1)
                dz = pr * (dp - dlb[:, 64 * hh:64 * hh + 1])
                g_ref[hh] += dz
                dzb = dz.astype(BF16)
                dqs.append(_dot(dzb, kh, 1, 0) * SCALE)
                dks.append(_dot(dzb, qh, 0, 0))
                dvs.append(_dot(pr.astype(BF16), doh, 0, 0))
            dq_t = jnp.concatenate(dqs, axis=1)
            if shared_kv:
                dk_t = jnp.concatenate([dks[0] + dks[1]] * 2, axis=1)
                dv_t = jnp.concatenate([dvs[0] + dvs[1]] * 2, axis=1)
                mine = (lane2 >= HD) == (p >= 2)
                dk_t = jnp.where(mine, dk_t, 0.0)
                dv_t = jnp.where(mine, dv_t, 0.0)
            else:
                dk_t = jnp.concatenate(dks, axis=1)
                dv_t = jnp.concatenate(dvs, axis=1)
            dk_cm[pl.ds(prev, BLK), :] += dk_t[:BLK]
            dk_cm[pl.ds(cur, BLK), :] += dk_t[BLK:]
            dv_cm[pl.ds(prev, BLK), :] += dv_t[:BLK]
            dv_cm[pl.ds(cur, BLK), :] += dv_t[BLK:]
            if d > 1:
                dq_nat[pl.ds(nat, BLK, stride=d), :] = dq_t
            else:
                dq_nat[pl.ds(cur, BLK), :] = dq_t
            return carry

        lax.fori_loop(0, NB, blk, 0, unroll=8)
        dq_ref[...] = dq_nat[...].astype(BF16)

        def from_class_major(src, dst_ref):
            if d == 1:
                dst_ref[...] = src[...].astype(BF16)
            else:
                ln = S // d
                for r in range(d):
                    kv_nat[pl.ds(r, ln, stride=d), :] = src[pl.ds(r * ln, ln), :]
                dst_ref[...] = kv_nat[...].astype(BF16)

        def write_kv():
            from_class_major(dk_cm, dk_ref)
            from_class_major(dv_cm, dv_ref)

        if shared_kv:
            pl.when(p == npairs - 1)(write_kv)
        else:
            write_kv()

    slab = lambda off, per_pair: pl.BlockSpec((None, S, LANES), (lambda p: (off + p, 0, 0)) if per_pair else (lambda p: (off, 0, 0)))
    pair = pl.BlockSpec((None, S, LANES), lambda p: (p, 0, 0))
    kv_out = pair if not shared_kv else pl.BlockSpec((None, S, LANES), lambda p: (0, 0, 0))
    return pl.pallas_call(
        body, grid=(npairs,),
        in_specs=[SMEM, slab(q0, True), slab(k0, not shared_kv), slab(v0, not shared_kv),
                  pl.BlockSpec((None, 2, 2, BLK, 2 * BLK), lambda p: (bias0 + p, 0, 0, 0, 0)),
                  pair, pair, pair],
        out_specs=[pair, kv_out, kv_out,
                   pl.BlockSpec((None, 2, BLK, 2 * BLK), lambda p: (p, 0, 0, 0)),
                   pl.BlockSpec((None, 8, LANES), lambda p: (p, 0, 0))],
        out_shape=[jax.ShapeDtypeStruct((npairs, S, LANES), BF16),
                   jax.ShapeDtypeStruct((nkv, S, LANES), BF16),
                   jax.ShapeDtypeStruct((nkv, S, LANES), BF16),
                   jax.ShapeDtypeStruct((npairs, 2, BLK, 2 * BLK), F32),
                   jax.ShapeDtypeStruct((npairs, 8, LANES), F32)],
        scratch_shapes=[pltpu.VMEM((S, LANES), BF16)] * 4 + [pltpu.VMEM((S, LANES), F32)] * 7,
        compiler_params=_cp("arbitrary"), name=name)(sinks, proj, proj, proj, bias, o, do, lse)


KC = 512
NSUB = KC // BLK


def _split2(x):
    hi = x.astype(BF16)
    lo = (x - hi.astype(F32)).astype(BF16)
    return hi, lo


def _tri_ones(cmp):
    jj = lax.broadcasted_iota(jnp.int32, (2 * BLK, BLK), 0) % BLK
    ss = lax.broadcasted_iota(jnp.int32, (2 * BLK, BLK), 1)
    return jnp.concatenate([cmp(jj, ss).astype(BF16), jnp.ones((2 * BLK, BLK), BF16)], axis=1)


def _sub_sums(x, tri1):
    st = jnp.concatenate([x[:, s * BLK:(s + 1) * BLK] for s in range(NSUB)], axis=0)
    hi, lo = _split2(st)
    r = _dot(jnp.concatenate([hi, lo], axis=1), tri1, 1, 0)
    return ([r[s * BLK:(s + 1) * BLK, :BLK] for s in range(NSUB)], [r[s * BLK:(s + 1) * BLK, BLK:] for s in range(NSUB)])


def _log_sig_pair(z):
    lb = jnp.minimum(z, 0.0) - jnp.log1p(jnp.exp(-jnp.abs(z)))
    return lb, lb - z


QGROUPS = NB // NSUB


def _stick_fwd(proj, *, q0, k0, v0, name):
    def body(q_ref, k_ref, v_ref, o_ref, t_ref, qs, ks, vs):
        qs[...] = (q_ref[...] * SCALE).astype(BF16)
        ks[...] = k_ref[...].astype(BF16)
        vs[...] = v_ref[...].astype(BF16)
        tri1 = _tri_ones(lambda j, s: j > s)
        col = lax.broadcasted_iota(jnp.int32, (BLK, KC), 1)
        rowi = lax.broadcasted_iota(jnp.int32, (BLK, KC), 0)

        for qg in range(QGROUPS):
            def qblock(ii, carry0, qg=qg):
                t0 = pl.multiple_of((qg * NSUB + ii) * BLK, BLK)
                qb = qs[pl.ds(t0, BLK), :]
                accs = [jnp.zeros((BLK, HD), F32)] * 2
                runs = [jnp.zeros((BLK, BLK), F32)] * 2
                for c in reversed(range(qg + 1)):
                    s0 = c * KC
                    diag = c == qg
                    before = (s0 + col) < (t0 + rowi) if diag else None
                    for hh in range(2):
                        kh = ks[s0:s0 + KC, hh * HD:(hh + 1) * HD]
                        vh = vs[s0:s0 + KC, hh * HD:(hh + 1) * HD]
                        lb, lk = _log_sig_pair(_dot(qb[:, hh * HD:(hh + 1) * HD], kh, 1, 1))
                        if diag:
                            lk = jnp.where(before, lk, 0.0)
                        suf, tot = _sub_sums(lk, tri1)
                        ws, run = [], runs[hh]
                        for s in reversed(range(NSUB)):
                            ws.append(jnp.exp(lb[:, s * BLK:(s + 1) * BLK] + suf[s] + run))
                            run = run + tot[s]
                        w = jnp.concatenate(ws[::-1], axis=1)
                        if diag:
                            w = jnp.where(before, w, 0.0)
                        accs[hh] = accs[hh] + _dot(w.astype(BF16), vh, 1, 0)
                        runs[hh] = run
                o_ref[pl.ds(t0, BLK), :] = jnp.concatenate(accs, axis=1)
                t_ref[pl.ds(t0, BLK), :] = _lane_halves(runs[0], runs[1])
                return carry0

            lax.fori_loop(0, NSUB, qblock, 0)

    slab = lambda off: pl.BlockSpec((None, S, LANES), lambda p: (off + p, 0, 0))
    out = pl.BlockSpec((None, S, LANES), lambda p: (p, 0, 0))
    return pl.pallas_call(
        body, grid=(2,), in_specs=[slab(q0), slab(k0), slab(v0)], out_specs=[out, out],
        out_shape=[jax.ShapeDtypeStruct((2, S, LANES), F32)] * 2,
        scratch_shapes=[pltpu.VMEM((S, LANES), BF16)] * 3,
        compiler_params=_cp("arbitrary"), name=name)(proj, proj, proj)


def _stick_bwd(proj, do, tot, *, q0, k0, v0, name):
    def body(q_ref, k_ref, v_ref, do_ref, t_ref, dq_ref, dk_ref, dv_ref, qs, ks, vs, dos, dk_acc, dv_acc):
        qs[...] = (q_ref[...] * SCALE).astype(BF16)
        ks[...] = k_ref[...].astype(BF16)
        vs[...] = v_ref[...].astype(BF16)
        dos[...] = do_ref[...].astype(BF16)
        dk_acc[...] = jnp.zeros((2, S, HD), F32)
        dv_acc[...] = jnp.zeros((2, S, HD), F32)
        tri_inc = _tri_ones(lambda j, s: j <= s)
        tri_exc = _tri_ones(lambda j, s: j < s)
        col = lax.broadcasted_iota(jnp.int32, (BLK, KC), 1)
        rowi = lax.broadcasted_iota(jnp.int32, (BLK, KC), 0)

        for qg in range(QGROUPS):
            def qblock(ii, carry0, qg=qg):
                t0 = pl.multiple_of((qg * NSUB + ii) * BLK, BLK)
                qb = qs[pl.ds(t0, BLK), :]
                dob = dos[pl.ds(t0, BLK), :]
                tb = t_ref[pl.ds(t0, BLK), :]
                dqs = [jnp.zeros((BLK, HD), F32)] * 2
                pruns = [jnp.zeros((BLK, BLK), F32)] * 2
                eruns = [jnp.zeros((BLK, BLK), F32)] * 2
                for c in range(qg + 1):
                    s0 = c * KC
                    diag = c == qg
                    before = (s0 + col) < (t0 + rowi) if diag else None
                    for hh in range(2):
                        qh = qb[:, hh * HD:(hh + 1) * HD]
                        doh = dob[:, hh * HD:(hh + 1) * HD]
                        tt = tb[:, 64 * hh:64 * hh + 1]
                        kh = ks[s0:s0 + KC, hh * HD:(hh + 1) * HD]
                        vh = vs[s0:s0 + KC, hh * HD:(hh + 1) * HD]
                        lb, lk = _log_sig_pair(_dot(qh, kh, 1, 1))
                        if diag:
                            lk = jnp.where(before, lk, 0.0)
                        pin, ptot = _sub_sums(lk, tri_inc)
                        ws, prun = [], pruns[hh]
                        for s in range(NSUB):
                            ws.append(jnp.exp(lb[:, s * BLK:(s + 1) * BLK] + (tt - (pin[s] + prun))))
                            prun = prun + ptot[s]
                        w = jnp.concatenate(ws, axis=1)
                        if diag:
                            w = jnp.where(before, w, 0.0)
                        e = w * _dot(doh, vh, 1, 1)
                        pex, etot = _sub_sums(e, tri_exc)
                        cs, erun = [], eruns[hh]
                        for s in range(NSUB):
                            cs.append(pex[s] + erun)
                            erun = erun + etot[s]
                        sig = jnp.exp(lb)
                        dz = e * (1.0 - sig) - jnp.concatenate(cs, axis=1) * sig
                        if diag:
                            dz = jnp.where(before, dz, 0.0)
                        dz = dz.astype(BF16)
                        dqs[hh] = dqs[hh] + _dot(dz, kh, 1, 0)
                        dk_acc[hh, s0:s0 + KC, :] += _dot(dz, qh, 0, 0)
                        dv_acc[hh, s0:s0 + KC, :] += _dot(w.astype(BF16), doh, 0, 0)
                        pruns[hh], eruns[hh] = prun, erun
                dq_ref[pl.ds(t0, BLK), :] = (jnp.concatenate(dqs, axis=1) * SCALE).astype(BF16)
                return carry0

            lax.fori_loop(0, NSUB, qblock, 0)
        dk_ref[...] = jnp.concatenate([dk_acc[0], dk_acc[1]], axis=1).astype(BF16)
        dv_ref[...] = jnp.concatenate([dv_acc[0], dv_acc[1]], axis=1).astype(BF16)

    slab = lambda off: pl.BlockSpec((None, S, LANES), lambda p: (off + p, 0, 0))
    pair = pl.BlockSpec((None, S, LANES), lambda p: (p, 0, 0))
    return pl.pallas_call(
        body, grid=(2,), in_specs=[slab(q0), slab(k0), slab(v0), pair, pair], out_specs=[pair] * 3,
        out_shape=[jax.ShapeDtypeStruct((2, S, LANES), BF16)] * 3,
        scratch_shapes=[pltpu.VMEM((S, LANES), BF16)] * 4 + [pltpu.VMEM((2, S, HD), F32)] * 2,
        compiler_params=_cp("arbitrary"), name=name)(proj, proj, proj, do, tot)


def _cat_slabs(ref):
    return jnp.concatenate([ref[s] for s in range(ref.shape[0])], axis=1)


def _merge_fwd(o_a, o_b, o_c, gates, b_gate, wa, wb, wc, w_out, name):
    tm = ROW_TILE

    def body(oa_ref, ob_ref, oc_ref, g_ref, bg_ref, wa_ref, wb_ref, wc_ref, wo_ref, mg_ref, mo_ref):
        acc = jnp.zeros((tm, D), F32)
        for i, (o_ref, w_ref) in enumerate(((oa_ref, wa_ref), (ob_ref, wb_ref), (oc_ref, wc_ref))):
            pr = _dot(_cat_slabs(o_ref).astype(BF16), w_ref[...], 1, 0)
            sg = jax.nn.sigmoid(g_ref[:, i * D:(i + 1) * D] + bg_ref[i:i + 1, :])
            acc = acc + sg * pr
        mg = acc.astype(BF16)
        mg_ref[...] = mg
        mo_ref[...] = _dot(mg, wo_ref[...], 1, 0)

    slabs = lambda n: pl.BlockSpec((n, tm, LANES), lambda i: (0, i, 0))
    full = lambda r, c: pl.BlockSpec((r, c), lambda i: (0, 0))
    row = pl.BlockSpec((tm, D), lambda i: (i, 0))
    return pl.pallas_call(
        body, grid=(S // tm,),
        in_specs=[slabs(2), slabs(4), slabs(2), pl.BlockSpec((tm, GATE_COLS), lambda i: (i, 0)), full(3, D),
                  full(256, D), full(512, D), full(256, D), full(D, D)],
        out_specs=[row, row],
        out_shape=[jax.ShapeDtypeStruct((S, D), BF16), jax.ShapeDtypeStruct((S, D), F32)],
        compiler_params=_cp("parallel"), name=name)(o_a, o_b, o_c, gates, b_gate, wa, wb, wc, w_out)


def _merge_bwd(d_mo, o_a, o_b, o_c, gates, b_gate, wa, wb, wc, w_out, name):
    tm = ROW_TILE

    def body(dmo_ref, oa_ref, ob_ref, oc_ref, g_ref, bg_ref, wa_ref, wb_ref, wc_ref, wo_ref,
             doa_ref, dob_ref, doc_ref, dg_ref, dwa_ref, dwb_ref, dwc_ref, dbg_ref):
        @pl.when(pl.program_id(0) == 0)
        def _():
            dwa_ref[...] = jnp.zeros(dwa_ref.shape, F32)
            dwb_ref[...] = jnp.zeros(dwb_ref.shape, F32)
            dwc_ref[...] = jnp.zeros(dwc_ref.shape, F32)
            dbg_ref[...] = jnp.zeros(dbg_ref.shape, F32)

        dmg = _dot(dmo_ref[...], wo_ref[...], 1, 1)
        trip = ((oa_ref, wa_ref, doa_ref, dwa_ref), (ob_ref, wb_ref, dob_ref, dwb_ref), (oc_ref, wc_ref, doc_ref, dwc_ref))
        for i, (o_ref, w_ref, do_ref, dw_ref) in enumerate(trip):
            ob = _cat_slabs(o_ref).astype(BF16)
            pr = _dot(ob, w_ref[...], 1, 0)
            sg = jax.nn.sigmoid(g_ref[:, i * D:(i + 1) * D] + bg_ref[i:i + 1, :])
            dgate = dmg * pr * sg * (1.0 - sg)
            dg_ref[:, i * D:(i + 1) * D] = dgate.astype(BF16)
            dbg_ref[i:i + 1, :] += jnp.sum(dgate, axis=0, keepdims=True)
            dpr = (dmg * sg).astype(BF16)
            do = _dot(dpr, w_ref[...], 1, 1)
            for s in range(do_ref.shape[0]):
                do_ref[s] = do[:, s * LANES:(s + 1) * LANES]
            dw_ref[...] += _dot(ob, dpr, 0, 0)

    slabs = lambda n: pl.BlockSpec((n, tm, LANES), lambda i: (0, i, 0))
    full = lambda r, c: pl.BlockSpec((r, c), lambda i: (0, 0))
    row = pl.BlockSpec((tm, D), lambda i: (i, 0))
    return pl.pallas_call(
        body, grid=(S // tm,),
        in_specs=[row, slabs(2), slabs(4), slabs(2), pl.BlockSpec((tm, GATE_COLS), lambda i: (i, 0)), full(3, D),
                  full(256, D), full(512, D), full(256, D), full(D, D)],
        out_specs=[slabs(2), slabs(4), slabs(2), pl.BlockSpec((tm, GATE_COLS), lambda i: (i, 0)),
                   full(256, D), full(512, D), full(256, D), full(3, D)],
        out_shape=[jax.ShapeDtypeStruct((2, S, LANES), F32), jax.ShapeDtypeStruct((4, S, LANES), F32),
                   jax.ShapeDtypeStruct((2, S, LANES), F32), jax.ShapeDtypeStruct((S, GATE_COLS), BF16),
                   jax.ShapeDtypeStruct((256, D), F32), jax.ShapeDtypeStruct((512, D), F32),
                   jax.ShapeDtypeStruct((256, D), F32), jax.ShapeDtypeStruct((3, D), F32)],
        compiler_params=_cp("arbitrary"), name=name)(d_mo, o_a, o_b, o_c, gates, b_gate, wa, wb, wc, w_out)


FC = 256
GELU_K = math.sqrt(2.0 / math.pi)
GELU_C = 0.044715


RC = 64
NRC = S // RC


def _down(tail, cur, n):
    row = lax.broadcasted_iota(jnp.int32, tail.shape, 0)
    rolled = pltpu.roll(cur, n, 0)
    first = jnp.where(row < n, pltpu.roll(tail, n, 0), rolled[0:8])
    return jnp.concatenate([first, rolled[8:]], axis=0)


def _up(cur, head, n):
    row = lax.broadcasted_iota(jnp.int32, head.shape, 0)
    rolled = pltpu.roll(cur, RC - n, 0)
    last = jnp.where(row >= 8 - n, pltpu.roll(head, 8 - n, 0), rolled[RC - 8:])
    return jnp.concatenate([rolled[:RC - 8], last], axis=0)


def _conv_chunk(load, j, w_ref, b_ref, half):
    r0 = pl.multiple_of(j * RC, RC)
    cur = load(r0, RC)
    tail = jnp.where(j > 0, load(pl.multiple_of(jnp.maximum(r0 - 8, 0), 8), 8), 0.0)
    d1 = _down(tail, cur, 1)
    d2 = _down(tail, cur, 2)
    y = w_ref[0:1, half, :] * d2 + w_ref[1:2, half, :] * d1 + w_ref[2:3, half, :] * cur + b_ref[half:half + 1, :]
    return y, cur, d1, d2


def _chunk(j):
    return pl.ds(pl.multiple_of(j * RC, RC), RC)


def _fold8(x):
    return jnp.sum(x.reshape(RC // 8, 8, x.shape[-1]), axis=0)


def _ffn_act(u, conv_w, conv_b, name):
    def body(u_ref, w_ref, b_ref, a_ref):
        def step(j, carry):
            yg = _conv_chunk(lambda r, n: u_ref[0, pl.ds(r, n), :], j, w_ref, b_ref, 0)[0]
            yv = _conv_chunk(lambda r, n: u_ref[1, pl.ds(r, n), :], j, w_ref, b_ref, 1)[0]
            th = jnp.tanh(GELU_K * (yg + GELU_C * yg * yg * yg))
            a_ref[_chunk(j), :] = (0.5 * yg * (1.0 + th) * yv).astype(BF16)
            return carry

        lax.fori_loop(0, NRC, step, 0)

    return pl.pallas_call(
        body, grid=(D_FF // FC,),
        in_specs=[pl.BlockSpec((2, S, FC), lambda j: (0, 0, j)), pl.BlockSpec((3, 2, FC), lambda j: (0, 0, j)),
                  pl.BlockSpec((2, FC), lambda j: (0, j))],
        out_specs=pl.BlockSpec((S, FC), lambda j: (0, j)),
        out_shape=jax.ShapeDtypeStruct((S, D_FF), BF16),
        compiler_params=_cp("parallel"), name=name)(u, conv_w, conv_b)


def _ffn_act_bwd(u, d_a, conv_w, conv_b, name):
    def body(u_ref, da_ref, w_ref, b_ref, du_ref, dw_ref, db_ref, dy_s):
        def first(j, acc):
            yg, ug, ug1, ug2 = _conv_chunk(lambda r, n: u_ref[0, pl.ds(r, n), :], j, w_ref, b_ref, 0)
            yv, uv, uv1, uv2 = _conv_chunk(lambda r, n: u_ref[1, pl.ds(r, n), :], j, w_ref, b_ref, 1)
            th = jnp.tanh(GELU_K * (yg + GELU_C * yg * yg * yg))
            gelu = 0.5 * yg * (1.0 + th)
            dgelu = 0.5 * (1.0 + th) + 0.5 * yg * (1.0 - th * th) * GELU_K * (1.0 + 3.0 * GELU_C * yg * yg)
            da = da_ref[_chunk(j), :]
            dyg = da * yv * dgelu
            dyv = da * gelu
            dy_s[0, _chunk(j), :] = dyg
            dy_s[1, _chunk(j), :] = dyv
            new = (_fold8(dyg * ug2), _fold8(dyg * ug1), _fold8(dyg * ug), _fold8(dyg),
                   _fold8(dyv * uv2), _fold8(dyv * uv1), _fold8(dyv * uv), _fold8(dyv))
            return tuple(a + n for a, n in zip(acc, new))

        acc = lax.fori_loop(0, NRC, first, tuple(jnp.zeros((8, FC), F32) for _ in range(8)))
        for half in range(2):
            for k in range(3):
                dw_ref[k:k + 1, half, :] = jnp.sum(acc[4 * half + k], axis=0, keepdims=True)
            db_ref[half:half + 1, :] = jnp.sum(acc[4 * half + 3], axis=0, keepdims=True)

        def second(j, carry):
            for half in range(2):
                cur = dy_s[half, _chunk(j), :]
                h0 = pl.multiple_of(jnp.minimum((j + 1) * RC, S - 8), 8)
                head = jnp.where(j < NRC - 1, dy_s[half, pl.ds(h0, 8), :], 0.0)
                du = (w_ref[2:3, half, :] * cur + w_ref[1:2, half, :] * _up(cur, head, 1)
                      + w_ref[0:1, half, :] * _up(cur, head, 2))
                du_ref[half, _chunk(j), :] = du.astype(BF16)
            return carry

        lax.fori_loop(0, NRC, second, 0)

    return pl.pallas_call(
        body, grid=(D_FF // FC,),
        in_specs=[pl.BlockSpec((2, S, FC), lambda j: (0, 0, j)), pl.BlockSpec((S, FC), lambda j: (0, j)),
                  pl.BlockSpec((3, 2, FC), lambda j: (0, 0, j)), pl.BlockSpec((2, FC), lambda j: (0, j))],
        out_specs=[pl.BlockSpec((2, S, FC), lambda j: (0, 0, j)), pl.BlockSpec((3, 2, FC), lambda j: (0, 0, j)),
                   pl.BlockSpec((2, FC), lambda j: (0, j))],
        out_shape=[jax.ShapeDtypeStruct((2, S, D_FF), BF16), jax.ShapeDtypeStruct((3, 2, D_FF), F32),
                   jax.ShapeDtypeStruct((2, D_FF), F32)],
        scratch_shapes=[pltpu.VMEM((2, S, FC), F32)],
        compiler_params=_cp("parallel"), name=name)(u, d_a, conv_w, conv_b)


def _layer_fwd(x, h1, w, bias, lname):
    n = lambda s: f"{lname}_{s}"
    w.need("in", h1)
    tn = 768
    proj = _mm(h1, w["w_in"], grid=(S // 1024, QKV_COLS // tn, 1),
               a_spec=pl.BlockSpec((1024, D), lambda i, j, k: (i, 0)),
               b_spec=pl.BlockSpec((D, tn), lambda i, j, k: (0, j)),
               out_shape=jax.ShapeDtypeStruct((QKV_SLABS, S, LANES), F32),
               out_spec=pl.BlockSpec((tn // LANES, 1024, LANES), lambda i, j, k: (j, i, 0)),
               ca=1, cb=0, acc_shape=(1024, tn), out_slab=True, name=n("proj_qkv"))
    gates = _mm(h1, w["w_in"], grid=(S // 1024, GATE_COLS // tn, 1),
                a_spec=pl.BlockSpec((1024, D), lambda i, j, k: (i, 0)),
                b_spec=pl.BlockSpec((D, tn), lambda i, j, k: (0, j + QKV_COLS // tn)),
                out_shape=jax.ShapeDtypeStruct((S, GATE_COLS), F32),
                out_spec=pl.BlockSpec((1024, tn), lambda i, j, k: (i, j)),
                ca=1, cb=0, acc_shape=(1024, tn), name=n("proj_gate"))
    nums, stats = [], []
    for g, (_, d) in enumerate(A_GROUPS):
        nm, st = _band_fwd(proj, bias, w["sinks"], d=d, q0=2 * g, k0=6 + 2 * g, v0=12 + 2 * g, npairs=2, bias0=2 * g,
                           shared_kv=False, name=n(f"attn_a{g}_fwd"))
        nums.append(nm)
        stats.append(st)
    o_a, lse_a = _combine_a(nums, stats, n("attn_a_combine"))
    o_b, lse_b = _band_fwd(proj, bias, w["sinks"], d=1, q0=18, k0=22, v0=23, npairs=4, bias0=6, shared_kv=True,
                           name=n("attn_b_fwd"))
    o_c, tot_c = _stick_fwd(proj, q0=24, k0=26, v0=28, name=n("attn_c_fwd"))
    w.need("mix", tot_c)
    merged, mo = _merge_fwd(o_a, o_b, o_c, gates, w["b_gate"], w["w_br_a"], w["w_br_b"], w["w_br_c"], w["w_out"], n("merge_fwd"))
    x2, h2 = _postnorm_res(x, mo, w["attn_post_norm"], w["ffn_pre_norm"], n("attn_post"))
    w.need("ffn", h2)
    u = _mm(h2, w["w_up"], grid=(S // 1024, 2 * D_FF // 1024, 1),
            a_spec=pl.BlockSpec((1024, D), lambda i, j, k: (i, 0)),
            b_spec=pl.BlockSpec((D, 1024), lambda i, j, k: (0, j)),
            out_shape=jax.ShapeDtypeStruct((2, S, D_FF), F32),
            out_spec=pl.BlockSpec((None, 1024, 1024), lambda i, j, k: (j // 4, i, j % 4)),
            ca=1, cb=0, acc_shape=(1024, 1024), name=n("ffn_up"))
    a = _ffn_act(u, w["conv_w"], w["conv_b"], n("ffn_act"))
    fo = _mm_nn(a, w["w_down"], F32, 1024, 1024, 2048, n("ffn_down"))
    saved = dict(x=x, h1=h1, proj=proj, gates=gates, o_a=o_a, lse_a=lse_a, o_b=o_b, lse_b=lse_b, o_c=o_c, tot_c=tot_c,
                 merged=merged, mo=mo, x2=x2, h2=h2, u=u, a=a, fo=fo)
    return saved


def _layer_bwd(dx3, sv, w, bias, lname, tok=None, on_part=None):
    n = lambda s: f"{lname}_{s}"
    g = {}

    def part(group, vec):
        t = on_part(group, g) if on_part is not None else None
        return vec if t is None else vec + t

    gain = w["ffn_post_norm"] if tok is None else w["ffn_post_norm"] + tok
    d_fo, g["ffn_post_norm"] = _norm_bwd(sv["fo"], gain, [dx3], None, BF16, n("ffn_post_bwd"))
    d_a = _mm_nt(d_fo, w["w_down"], F32, 1024, 1024, 1024, n("ffn_down_bwd_x"))
    g["w_down"] = _mm_tn(sv["a"], d_fo, BF16, 1024, 1024, S, n("ffn_down_bwd_w"))
    d_u, dcw, dcb = _ffn_act_bwd(sv["u"], d_a, w["conv_w"], w["conv_b"], n("ffn_act_bwd"))
    g["conv_w"] = dcw.reshape(3, 2 * D_FF)
    g["conv_b"] = dcb.reshape(1, 2 * D_FF)
    g["w_up"] = _mm(sv["h2"], d_u, grid=(1, 2 * D_FF // 1024, 1),
                    a_spec=pl.BlockSpec((S, D), lambda i, j, k: (k, 0)),
                    b_spec=pl.BlockSpec((None, S, 1024), lambda i, j, k: (j // 4, k, j % 4)),
                    out_shape=jax.ShapeDtypeStruct((D, 2 * D_FF), BF16),
                    out_spec=pl.BlockSpec((D, 1024), lambda i, j, k: (0, j)),
                    ca=0, cb=0, acc_shape=(D, 1024), name=n("ffn_up_bwd_w"))
    tok_ffn = on_part("ffn", g) if on_part is not None else None
    d_h2 = _mm(d_u, w["w_up"], grid=(S // 1024, 1, 2 * D_FF // 2048),
               a_spec=pl.BlockSpec((None, 1024, 2048), lambda i, j, k: (k // 2, i, k % 2)),
               b_spec=pl.BlockSpec((D, 2048), lambda i, j, k: (0, k)),
               out_shape=jax.ShapeDtypeStruct((S, D), F32),
               out_spec=pl.BlockSpec((1024, D), lambda i, j, k: (i, 0)),
               ca=1, cb=1, acc_shape=(1024, D), after=tok_ffn, name=n("ffn_up_bwd_x"))
    dx2, g["ffn_pre_norm"] = _norm_bwd(sv["x2"], w["ffn_pre_norm"], [d_h2], dx3, F32, n("ffn_pre_bwd"))
    d_mo, g["attn_post_norm"] = _norm_bwd(sv["mo"], w["attn_post_norm"], [dx2], None, BF16, n("attn_post_bwd"))
    g["w_out"] = _mm_tn(sv["merged"], d_mo, BF16, 1024, 1024, S, n("out_bwd_w"))
    do_a, do_b, do_c, d_gates, dwa, dwb, dwc, g["b_gate"] = _merge_bwd(
        d_mo, sv["o_a"], sv["o_b"], sv["o_c"], sv["gates"], w["b_gate"], w["w_br_a"], w["w_br_b"], w["w_br_c"],
        w["w_out"], n("merge_bwd"))
    g["w_br_a"], g["w_br_b"], g["w_br_c"] = dwa, dwb, dwc
    sinks = part("mix", w["sinks"])
    proj = sv["proj"]
    dqa, dka, dva, gbias = [], [], [], []
    for gi, (_, d) in enumerate(A_GROUPS):
        dq, dk, dv, gg, _ = _band_bwd(proj, bias, sv["o_a"], do_a, sv["lse_a"], sinks, d=d, q0=2 * gi, k0=6 + 2 * gi,
                                      v0=12 + 2 * gi, npairs=2, bias0=2 * gi, shared_kv=False, name=n(f"attn_a{gi}_bwd"))
        dqa.append(dq), dka.append(dk), dva.append(dv), gbias.append(gg)
    dqb, dkb, dvb, ggb, dsink = _band_bwd(proj, bias, sv["o_b"], do_b, sv["lse_b"], sinks, d=1, q0=18, k0=22, v0=23,
                                          npairs=4, bias0=6, shared_kv=True, name=n("attn_b_bwd"))
    gbias.append(ggb)
    g["bias_g"] = jnp.concatenate(gbias, axis=0).reshape(N_BIAS_HEADS, BLK, 2 * BLK)
    g["sinks"] = dsink[:, 0, :2].reshape(1, 8)
    dqc, dkc, dvc = _stick_bwd(proj, do_c, sv["tot_c"], q0=24, k0=26, v0=28, name=n("attn_c_bwd"))
    dqkv = jnp.concatenate(dqa + dka + dva + [dqb, dkb, dvb, dqc, dkc, dvc], axis=0)
    ts = 6
    tsx = 15
    dw_in = _mm(sv["h1"], dqkv, grid=(1, QKV_SLABS // ts, 1),
                a_spec=pl.BlockSpec((S, D), lambda i, j, k: (k, 0)),
                b_spec=pl.BlockSpec((ts, S, LANES), lambda i, j, k: (j, k, 0)),
                out_shape=jax.ShapeDtypeStruct((D, IN_COLS), BF16),
                out_spec=pl.BlockSpec((D, ts * LANES), lambda i, j, k: (0, j)),
                ca=0, cb=0, acc_shape=(D, ts * LANES), b_slab=True, name=n("in_bwd_w_qkv"))
    g["w_in"] = _mm(sv["h1"], d_gates, grid=(1, GATE_COLS // 768, 1),
                    a_spec=pl.BlockSpec((S, D), lambda i, j, k: (k, 0)),
                    b_spec=pl.BlockSpec((S, 768), lambda i, j, k: (k, j)),
                    out_shape=jax.ShapeDtypeStruct((D, IN_COLS), BF16),
                    out_spec=pl.BlockSpec((D, 768), lambda i, j, k: (0, j + QKV_COLS // 768)),
                    ca=0, cb=0, acc_shape=(D, 768), alias_out=dw_in, name=n("in_bwd_w_gate"))
    tok_in = on_part("in", g) if on_part is not None else None
    d_h1a = _mm(dqkv, w["w_in"], grid=(S // 1024, 1, QKV_SLABS // tsx),
                a_spec=pl.BlockSpec((tsx, 1024, LANES), lambda i, j, k: (k, i, 0)),
                b_spec=pl.BlockSpec((D, tsx * LANES), lambda i, j, k: (0, k)),
                out_shape=jax.ShapeDtypeStruct((S, D), F32),
                out_spec=pl.BlockSpec((1024, D), lambda i, j, k: (i, 0)),
                ca=1, cb=1, acc_shape=(1024, D), a_slab=True, after=tok_in, name=n("in_bwd_x_qkv"))
    d_h1b = _mm(d_gates, w["w_in"], grid=(S // 1024, 1, GATE_COLS // 768),
                a_spec=pl.BlockSpec((1024, 768), lambda i, j, k: (i, k)),
                b_spec=pl.BlockSpec((D, 768), lambda i, j, k: (0, k + QKV_COLS // 768)),
                out_shape=jax.ShapeDtypeStruct((S, D), F32),
                out_spec=pl.BlockSpec((1024, D), lambda i, j, k: (i, 0)),
                ca=1, cb=1, acc_shape=(1024, D), after=tok_in, name=n("in_bwd_x_gate"))
    dx, g["attn_pre_norm"] = _norm_bwd(sv["x"], w["attn_pre_norm"], [d_h1a, d_h1b], dx2, F32, n("attn_pre_bwd"))
    return dx, g, tok_in


def _local_step(x, target, ws, rel_bias, tok=None, on_grads=None):
    buckets = jnp.asarray(_bucket_tiles())
    bias = _bias_tiles(rel_bias, buckets, "bias_tiles").reshape(N_BIAS_HEADS // 2, 2, 2, BLK, 2 * BLK)
    saved = []
    gain0 = ws[0]["attn_pre_norm"] if tok is None else ws[0]["attn_pre_norm"] + tok
    h1 = _prenorm(x, gain0, "l0_attn_pre")
    for l in range(DEPTH):
        sv = _layer_fwd(x, h1, ws[l], bias, f"l{l}")
        saved.append(sv)
        g_next = ws[l + 1]["attn_pre_norm"] if l + 1 < DEPTH else ws[l]["attn_pre_norm"]
        x, h1 = _postnorm_res(sv["x2"], sv["fo"], ws[l]["ffn_post_norm"], g_next, f"l{l}_ffn_post")
    dy, loss_tile = _loss_head(x, target, "loss_head")
    grads = [None] * DEPTH
    tok = None
    for l in reversed(range(DEPTH)):
        on_part = None if on_grads is None else functools.partial(on_grads, l)
        dy, grads[l], tok = _layer_bwd(dy, saved[l], ws[l], bias, f"l{l}", tok, on_part)
    g_rel = _bias_grad([grads[l]["bias_g"] for l in range(DEPTH)], buckets, "bias_grad")[:, :N_BIAS_HEADS]
    return loss_tile[0, 0], dy, grads, g_rel


def _coords():
    return lax.axis_index("x"), lax.axis_index("y"), lax.axis_index("c")


def _peer(rel):
    x, y, c = _coords()
    return (1 - x if rel & 4 else x, 1 - y if rel & 2 else y, 1 - c if rel & 1 else c)


def _exchange(srcs, dst_shapes, src_win, dst_win, name):
    nt = len(srcs)

    def body(*refs):
        src_refs, dst_refs = refs[:nt], refs[nt:2 * nt]
        send_sems, recv_sems, local_sems = refs[2 * nt:]
        x, y, c = _coords()
        me = 4 * x + 2 * y + c
        locals_ = []
        for t in range(nt):
            cp = pltpu.make_async_copy(src_win(t, src_refs[t], me), dst_win(t, dst_refs[t], me), local_sems.at[t])
            cp.start()
            locals_.append(cp)
        sends = []
        for rel in range(1, NDEV):
            px, py, pc = _peer(rel)
            q = 4 * px + 2 * py + pc
            for t in range(nt):
                cp = pltpu.make_async_remote_copy(
                    src_ref=src_win(t, src_refs[t], q), dst_ref=dst_win(t, dst_refs[t], me),
                    send_sem=send_sems.at[rel - 1, t], recv_sem=recv_sems.at[rel - 1, t],
                    device_id=(px, py, pc), device_id_type=MESH)
                cp.start()
                sends.append(cp)
        for rel in range(1, NDEV):
            px, py, pc = _peer(rel)
            q = 4 * px + 2 * py + pc
            for t in range(nt):
                pltpu.make_async_remote_copy(
                    src_ref=src_win(t, src_refs[t], me), dst_ref=dst_win(t, dst_refs[t], q),
                    send_sem=send_sems.at[rel - 1, t], recv_sem=recv_sems.at[rel - 1, t],
                    device_id=(px, py, pc), device_id_type=MESH).wait_recv()
        for cp in sends:
            cp.wait_send()
        for cp in locals_:
            cp.wait()

    return pl.pallas_call(
        body, in_specs=[ANY] * nt, out_specs=[ANY] * nt, out_shape=dst_shapes,
        scratch_shapes=[pltpu.SemaphoreType.DMA((NDEV - 1, nt)), pltpu.SemaphoreType.DMA((NDEV - 1, nt)),
                        pltpu.SemaphoreType.DMA((nt,))],
        name=name)(*srcs)


BIG = (("w_in", 1, 864), ("w_br_a", 1, 128), ("w_br_b", 1, 128), ("w_br_c", 1, 128), ("w_out", 0, 128),
       ("w_up", 1, 1024), ("w_down", 0, 512))


NBIG = len(BIG)
BIG_FULL = {"w_in": (D, IN_COLS), "w_br_a": (256, D), "w_br_b": (512, D), "w_br_c": (256, D), "w_out": (D, D),
            "w_up": (D, 2 * D_FF), "w_down": (D_FF, D)}
LAYER_GROUPS = (("in", (0,)), ("mix", (1, 2, 3, 4)), ("ffn", (5, 6)))

HBM_SPEC = pl.BlockSpec(memory_space=pltpu.HBM)
SEM_SPEC = pl.BlockSpec(memory_space=pltpu.SEMAPHORE)


def _hbm(a):
    return pltpu.with_memory_space_constraint(a, pltpu.HBM)


def _shard_window(t, ref, k):
    nm, ax, ext = BIG[t % NBIG]
    if nm == "w_in":
        return ref.at[k]
    off = pl.multiple_of(k * ext, ext)
    if ax == 0:
        return ref.at[pl.ds(off, ext), :]
    return ref.at[:, pl.ds(off, ext)]


def _whole(t, ref, k):
    return ref


def _slot(t, ref, k):
    return ref.at[k]


def _own_block_spec(t, rows, me_of):
    nm, ax, ext = BIG[t % NBIG]
    r, c = BIG_FULL[nm]
    if nm == "w_in":
        return pl.BlockSpec((None, rows, ext), lambda i, m: (me_of(m), i, 0))
    if ax == 0:
        return pl.BlockSpec((rows, c), lambda i, m: (me_of(m) * (ext // rows) + i, 0))
    return pl.BlockSpec((rows, ext), lambda i, m: (i, me_of(m)))


def _cast_own(t, shard, me_arr, name):
    nm, ax, ext = BIG[t % NBIG]
    nr, nc = shard.shape
    rows = min(nr, 256)
    shape = (NDEV, D, ext) if nm == "w_in" else BIG_FULL[nm]

    def body(m_ref, s_ref, o_ref):
        o_ref[...] = s_ref[...].astype(BF16)

    return pl.pallas_call(
        body, grid_spec=pltpu.PrefetchScalarGridSpec(
            num_scalar_prefetch=1, grid=(nr // rows,),
            in_specs=[pl.BlockSpec((rows, nc), lambda i, m: (i, 0))],
            out_specs=_own_block_spec(t, rows, lambda m: m[0])),
        out_shape=jax.ShapeDtypeStruct(shape, BF16), compiler_params=_cp("arbitrary"), name=name)(me_arr, shard)


def _xchg_start(srcs, lands, groups, src_win, dst_win, after, name):
    ns = 0 if srcs is None else len(srcs)
    nt, ng = len(lands), len(groups)
    ins = ([] if srcs is None else list(srcs)) + list(lands)

    def body(*refs):
        src_refs, land_refs = refs[:ns], refs[ns:ns + nt]
        sems = refs[ns + nt + 1:ns + nt + 1 + 2 * ng]
        token = refs[-1]
        x, y, c = _coords()
        me = 4 * x + 2 * y + c
        for gi, grp in enumerate(groups):
            for j, t in enumerate(grp):
                for rel in range(1, NDEV):
                    px, py, pc = _peer(rel)
                    q = 4 * px + 2 * py + pc
                    src = dst_win(t, land_refs[t], me) if srcs is None else src_win(t, src_refs[t], q)
                    pltpu.make_async_remote_copy(
                        src_ref=src, dst_ref=dst_win(t, land_refs[t], me),
                        send_sem=sems[2 * gi].at[(rel - 1) * len(grp) + j],
                        recv_sem=sems[2 * gi + 1].at[(rel - 1) * len(grp) + j],
                        device_id=(px, py, pc), device_id_type=MESH).start()
        token[...] = jnp.zeros((8, LANES), F32)

    out_shape = []
    for grp in groups:
        out_shape += [pltpu.SemaphoreType.DMA(((NDEV - 1) * len(grp),))] * 2
    out_shape += [pltpu.HBM(a.shape, a.dtype) for a in ins]
    out_shape.append(jax.ShapeDtypeStruct((8, LANES), F32))
    outs = pl.pallas_call(
        body, in_specs=[HBM_SPEC] * len(ins) + [ANY],
        out_specs=[SEM_SPEC] * (2 * ng) + [HBM_SPEC] * len(ins) + [pl.BlockSpec(memory_space=pltpu.VMEM)],
        out_shape=out_shape, input_output_aliases={i: 2 * ng + i for i in range(len(ins))},
        compiler_params=pltpu.CompilerParams(has_side_effects=pltpu.SideEffectType.DATAFLOW_SIDE_EFFECTING),
        name=name)(*[_hbm(a) for a in ins], after)
    sems = [(outs[2 * gi], outs[2 * gi + 1]) for gi in range(ng)]
    thru = list(outs[2 * ng:2 * ng + len(ins)])
    return sems, (None if srcs is None else thru[:ns]), thru[ns:], outs[-1]


def _xchg_wait(sems, srcs, lands, tids, after, src_win, dst_win, name):
    ns = 0 if srcs is None else len(srcs)
    n = len(lands)
    send_sem, recv_sem = sems
    ins = ([] if srcs is None else list(srcs)) + list(lands)

    def body(*refs):
        src_refs, land_refs = refs[:ns], refs[ns:ns + n]
        ssem, rsem = refs[ns + n], refs[ns + n + 1]
        x, y, c = _coords()
        me = 4 * x + 2 * y + c
        for j, t in enumerate(tids):
            for rel in range(1, NDEV):
                px, py, pc = _peer(rel)
                q = 4 * px + 2 * py + pc
                src = dst_win(t, land_refs[j], me) if srcs is None else src_win(t, src_refs[j], q)
                cp = pltpu.make_async_remote_copy(
                    src_ref=src, dst_ref=dst_win(t, land_refs[j], q),
                    send_sem=ssem.at[(rel - 1) * n + j], recv_sem=rsem.at[(rel - 1) * n + j],
                    device_id=(px, py, pc), device_id_type=MESH)
                cp.wait_send()
                cp.wait_recv()

    outs = pl.pallas_call(
        body, in_specs=[HBM_SPEC] * len(ins) + [SEM_SPEC, SEM_SPEC, ANY], out_specs=[HBM_SPEC] * len(ins),
        out_shape=[pltpu.HBM(a.shape, a.dtype) for a in ins],
        input_output_aliases={i: i for i in range(len(ins))},
        compiler_params=pltpu.CompilerParams(has_side_effects=pltpu.SideEffectType.DATAFLOW_SIDE_EFFECTING),
        name=name)(*ins, send_sem, recv_sem, after)
    return (None if srcs is None else list(outs[:ns])), list(outs[ns:])


class _Weights:
    def __init__(self, ready, pending=None):
        self.ready = dict(ready)
        self.pending = dict(pending or {})

    def __getitem__(self, k):
        return self.ready[k]

    def need(self, group, after):
        fn = self.pending.pop(group, None)
        if fn is not None:
            self.ready.update(fn(after))


def _adamw_math(w, g, m, v):
    m2 = ADAM_B1 * m + (1.0 - ADAM_B1) * g
    v2 = ADAM_B2 * v + (1.0 - ADAM_B2) * (g * g)
    m_hat = m2 / (1.0 - ADAM_B1 ** ADAM_STEP)
    v_hat = v2 / (1.0 - ADAM_B2 ** ADAM_STEP)
    delta = -ADAM_LR * (m_hat / (jnp.sqrt(v_hat) + ADAM_EPS) + ADAM_WD * w)
    return delta, m2, v2


def _adamw(t, parts, own, me_arr, w, m, v, layer, prev, rows, name):
    nl, nr, nc = w.shape

    def body(me_ref, p_ref, own_ref, w_ref, m_ref, v_ref, *rest):
        g_ref, d_ref, m2_ref, v2_ref = rest[-4:]
        me = me_ref[0]
        g = None
        for k in range(NDEV):
            term = jnp.where(me == k, own_ref[...], p_ref[k]).astype(F32)
            g = term if g is None else g + term
        delta, m2, v2 = _adamw_math(w_ref[...], g, m_ref[...], v_ref[...])
        g_ref[...] = g
        d_ref[...] = delta
        m2_ref[...] = m2
        v2_ref[...] = v2

    blk = pl.BlockSpec((None, rows, nc), lambda i, mm: (layer, i, 0))
    pblk = pl.BlockSpec((NDEV, rows, nc), lambda i, mm: (0, i, 0))
    extra = [] if prev is None else list(prev)
    return pl.pallas_call(
        body, grid_spec=pltpu.PrefetchScalarGridSpec(
            num_scalar_prefetch=1, grid=(nr // rows,),
            in_specs=[pblk, _own_block_spec(t, rows, lambda mm: mm[0]), blk, blk, blk] + [ANY] * len(extra),
            out_specs=[blk] * 4),
        out_shape=[jax.ShapeDtypeStruct(w.shape, F32)] * 4,
        input_output_aliases={6 + k: k for k in range(len(extra))},
        compiler_params=_cp("arbitrary"), name=name)(me_arr, parts, own, w, m, v, *extra)


SMALL_REPL = (("rel_bias", NUM_BUCKETS * N_BIAS_HEADS), ("attn_pre_norm", DEPTH * D), ("sinks", DEPTH * 8),
              ("attn_post_norm", DEPTH * D), ("ffn_pre_norm", DEPTH * D), ("conv_b", DEPTH * 2 * D_FF),
              ("ffn_post_norm", DEPTH * D))
SMALL_SHARD = (("b_gate", (DEPTH, 3, D), 128), ("conv_w", (DEPTH, 3, 2 * D_FF), 1024))


def _pack(vecs):
    flat = jnp.concatenate([v.reshape(-1).astype(F32) for v in vecs])
    n = flat.shape[0]
    rows = -(-n // (8 * LANES)) * 8
    return jnp.pad(flat, (0, rows * LANES - n)).reshape(rows, LANES)


def _unpack(packed, sizes):
    flat = packed.reshape(-1)
    out, off = [], 0
    for sz in sizes:
        out.append(flat[off:off + sz])
        off += sz
    return out


ROWPACK = (("rel_bias", 32, 32, (NUM_BUCKETS, N_BIAS_HEADS)), ("sinks", 8, 8, (DEPTH, 8)),
           ("attn_pre_norm", 16, 16, (DEPTH, D)), ("attn_post_norm", 16, 16, (DEPTH, D)),
           ("ffn_pre_norm", 16, 16, (DEPTH, D)), ("ffn_post_norm", 16, 16, (DEPTH, D)),
           ("conv_b", 128, 128, (DEPTH, 2 * D_FF)), ("b_gate", 48, 8, (DEPTH, 3, 128)),
           ("conv_w", 384, 48, (DEPTH, 3, 1024)))
ROWS_FULL = sum(r for _, r, _, _ in ROWPACK)
ROWS_OWN = sum(r for _, _, r, _ in ROWPACK)


def _as_rows(a, rows):
    a = a.astype(F32)
    if a.shape[-1] < LANES:
        a = jnp.pad(a.reshape(-1, a.shape[-1]), ((0, 0), (0, LANES - a.shape[-1])))
    a = a.reshape(-1, LANES)
    return jnp.pad(a, ((0, rows - a.shape[0]), (0, 0)))


def _rowpack(arrs, own):
    return jnp.concatenate([_as_rows(arrs[nm], ro if own else rf) for nm, rf, ro, _ in ROWPACK], axis=0)


def _small_update(parts, w, m, v, me_arr, name):
    nsm = len(ROWPACK)

    def body(me_ref, p_ref, w_ref, m_ref, v_ref, *rest):
        outs = rest[:4 * nsm]
        gfull, g_s, d_s, m_s, v_s = rest[4 * nsm:]
        me = me_ref[0]
        g = p_ref[0]
        for k in range(1, NDEV):
            g = g + p_ref[k]
        gfull[...] = g
        of, oo = 0, 0
        for nm, rf, ro, _ in ROWPACK:
            if nm == "b_gate":
                g_s[oo:oo + ro, :] = jnp.zeros((ro, LANES), F32)
                for r in range(DEPTH * 3):
                    g_s[oo + r:oo + r + 1, :] = gfull[pl.ds(of + r * NDEV + me, 1), :]
            elif nm == "conv_w":
                for r in range(DEPTH * 3):
                    g_s[oo + r * 8:oo + r * 8 + 8, :] = gfull[pl.ds(pl.multiple_of(of + r * 64 + me * 8, 8), 8), :]
            else:
                g_s[oo:oo + ro, :] = gfull[of:of + rf, :]
            of, oo = of + rf, oo + ro
        delta, m2, v2 = _adamw_math(w_ref[...], g_s[...], m_ref[...], v_ref[...])
        d_s[...] = delta
        m_s[...] = m2
        v_s[...] = v2
        for kind, src in enumerate((g_s, d_s, m_s, v_s)):
            oo = 0
            for idx, (nm, rf, ro, shp) in enumerate(ROWPACK):
                o_ref = outs[kind * nsm + idx]
                if nm in ("rel_bias", "sinks"):
                    o_ref[...] = src[oo:oo + shp[0], 0:shp[1]]
                elif nm == "b_gate":
                    for l in range(DEPTH):
                        o_ref[l] = src[oo + 3 * l:oo + 3 * l + 3, :]
                elif nm == "conv_w":
                    for l in range(DEPTH):
                        for k in range(8):
                            o_ref[l, :, k * LANES:(k + 1) * LANES] = src[pl.ds(oo + 24 * l + k, 3, stride=8), :]
                else:
                    per = shp[1] // LANES
                    for k in range(per):
                        o_ref[:, k * LANES:(k + 1) * LANES] = src[pl.ds(oo + k, DEPTH, stride=per), :]
                oo += ro

    vm = pl.BlockSpec(memory_space=pltpu.VMEM)
    shapes = [jax.ShapeDtypeStruct(shp, F32) for _ in range(4) for _, _, _, shp in ROWPACK]
    outs = pl.pallas_call(
        body, in_specs=[SMEM, vm, vm, vm, vm], out_specs=[vm] * (4 * nsm), out_shape=shapes,
        scratch_shapes=[pltpu.VMEM((ROWS_FULL, LANES), F32)] + [pltpu.VMEM((ROWS_OWN, LANES), F32)] * 4,
        name=name)(me_arr, parts, w, m, v)
    names = [nm for nm, _, _, _ in ROWPACK]
    return [dict(zip(names, outs[kind * nsm:(kind + 1) * nsm])) for kind in range(4)]


def kernel(x, rel_bias, attn_pre_norm, w_in, b_gate, sinks, w_br_a, w_br_b, w_br_c, w_out, attn_post_norm, ffn_pre_norm, w_up, conv_w, conv_b, w_down, ffn_post_norm, loss_target, m_rel_bias, m_attn_pre_norm, m_w_in, m_b_gate, m_sinks, m_w_br_a, m_w_br_b, m_w_br_c, m_w_out, m_attn_post_norm, m_ffn_pre_norm, m_w_up, m_conv_w, m_conv_b, m_w_down, m_ffn_post_norm, v_rel_bias, v_attn_pre_norm, v_w_in, v_b_gate, v_sinks, v_w_br_a, v_w_br_b, v_w_br_c, v_w_out, v_attn_post_norm, v_ffn_pre_norm, v_w_up, v_conv_w, v_conv_b, v_w_down, v_ffn_post_norm):
    P = dict(rel_bias=rel_bias, attn_pre_norm=attn_pre_norm, w_in=w_in, b_gate=b_gate, sinks=sinks, w_br_a=w_br_a,
             w_br_b=w_br_b, w_br_c=w_br_c, w_out=w_out, attn_post_norm=attn_post_norm, ffn_pre_norm=ffn_pre_norm,
             w_up=w_up, conv_w=conv_w, conv_b=conv_b, w_down=w_down, ffn_post_norm=ffn_post_norm)
    M = dict(rel_bias=m_rel_bias, attn_pre_norm=m_attn_pre_norm, w_in=m_w_in, b_gate=m_b_gate, sinks=m_sinks,
             w_br_a=m_w_br_a, w_br_b=m_w_br_b, w_br_c=m_w_br_c, w_out=m_w_out, attn_post_norm=m_attn_post_norm,
             ffn_pre_norm=m_ffn_pre_norm, w_up=m_w_up, conv_w=m_conv_w, conv_b=m_conv_b, w_down=m_w_down,
             ffn_post_norm=m_ffn_post_norm)
    V = dict(rel_bias=v_rel_bias, attn_pre_norm=v_attn_pre_norm, w_in=v_w_in, b_gate=v_b_gate, sinks=v_sinks,
             w_br_a=v_w_br_a, w_br_b=v_w_br_b, w_br_c=v_w_br_c, w_out=v_w_out, attn_post_norm=v_attn_post_norm,
             ffn_pre_norm=v_ffn_pre_norm, w_up=v_w_up, conv_w=v_conv_w, conv_b=v_conv_b, w_down=v_w_down,
             ffn_post_norm=v_ffn_post_norm)
    xi, yi, ci = _coords()
    me = 4 * xi + 2 * yi + ci

    me_arr = me.astype(jnp.int32).reshape(1)

    small_w = _pack([b_gate.reshape(-1), conv_w.reshape(-1)])
    (small_w_all,) = _exchange([small_w], [jax.ShapeDtypeStruct((NDEV,) + small_w.shape, F32)],
                               _whole, _slot, "gather_small_weights")

    lands = [_cast_own(l * NBIG + t, P[nm][l], me_arr, f"gather_own_l{l}_{nm}")
             for l in range(DEPTH) for t, (nm, _, _) in enumerate(BIG)]
    groups = [tuple(l * NBIG + t for t in tids) for l in range(DEPTH) for _, tids in LAYER_GROUPS]
    g_sems, _, g_lands, g_tok = _xchg_start(None, lands, groups, None, _shard_window, small_w_all, "gather_start")
    tok0 = g_tok[0:1, 0:1]

    def gather_waiter(gi, l, gname, tids):
        def wait(after):
            ids = [l * NBIG + t for t in tids]
            _, got = _xchg_wait(g_sems[gi], None, [g_lands[i] for i in ids], ids, after,
                                None, _shard_window, f"gather_wait_l{l}_{gname}")
            out = {}
            for t, arr in zip(tids, got):
                nm = BIG[t][0]
                out[nm] = jnp.transpose(arr, (1, 0, 2)).reshape(D, IN_COLS) if nm == "w_in" else arr
            return out
        return wait

    pending = [{gname: gather_waiter(l * len(LAYER_GROUPS) + k, l, gname, tids)
                for k, (gname, tids) in enumerate(LAYER_GROUPS)} for l in range(DEPTH)]
    nbg = DEPTH * 3 * 128
    ncw = DEPTH * 3 * 1024
    flat_all = small_w_all.reshape(NDEV, -1)
    b_gate_full = jnp.transpose(flat_all[:, :nbg].reshape(NDEV, DEPTH, 3, 128), (1, 2, 0, 3)).reshape(DEPTH, 3, D)
    conv_w_full = jnp.transpose(flat_all[:, nbg:nbg + ncw].reshape(NDEV, DEPTH, 3, 1024), (1, 2, 0, 3)).reshape(DEPTH, 3, 2 * D_FF)

    ws = []
    for l in range(DEPTH):
        ws.append(_Weights(dict(
            b_gate=b_gate_full[l], conv_w=conv_w_full[l].reshape(3, 2, D_FF), conv_b=conv_b[l].reshape(2, D_FF),
            sinks=sinks[l].reshape(1, 8),
            attn_pre_norm=attn_pre_norm[l].reshape(1, D), attn_post_norm=attn_post_norm[l].reshape(1, D),
            ffn_pre_norm=ffn_pre_norm[l].reshape(1, D), ffn_post_norm=ffn_post_norm[l].reshape(1, D)), pending[l]))

    rs = {}

    group_tids = dict(LAYER_GROUPS)

    def start_scatter(l, gname, grads_l):
        tids = group_tids[gname]
        blocks, lands_rs = [], []
        for t in tids:
            nm, ax, ext = BIG[t]
            gfull = grads_l[nm].astype(BF16)
            if nm == "w_in":
                gfull = jnp.transpose(gfull.reshape(D, NDEV, ext), (1, 0, 2))
                shp = (NDEV, D, ext)
            else:
                shp = (NDEV, ext, gfull.shape[1]) if ax == 0 else (NDEV, gfull.shape[0], ext)
            blocks.append(gfull)
            lands_rs.append(lax.empty(shp, BF16))
        local = list(range(len(tids)))
        win = lambda j, ref, k: _shard_window(tids[j], ref, k)
        sems, s_thru, l_thru, tok = _xchg_start(blocks, lands_rs, [tuple(local)], win, _slot, me_arr,
                                                f"scatter_start_l{l}_{gname}")
        rs[(l, gname)] = (sems[0], s_thru, l_thru, win, local)
        return tok[0:1, 0:1]

    loss_local, grad_x, grads, g_rel = _local_step(x[0], loss_target[0], ws, rel_bias, tok0, start_scatter)
    loss = lax.psum(loss_local, ("x", "y", "c"))

    stack = lambda nm: jnp.stack([grads[l][nm] for l in range(DEPTH)], axis=0)
    small_names = [nm for nm, _ in SMALL_REPL] + [nm for nm, _, _ in SMALL_SHARD]
    small_g = {"rel_bias": g_rel}
    for nm in small_names[1:]:
        small_g[nm] = stack(nm)
    small_packed = _rowpack(small_g, own=False)
    (small_parts,) = _exchange([small_packed], [jax.ShapeDtypeStruct((NDEV,) + small_packed.shape, F32)],
                               _whole, _slot, "gather_small_grads")

    out_g, out_d, out_m, out_v = {}, {}, {}, {}
    prev = {nm: None for nm, _, _ in BIG}
    for l in reversed(range(DEPTH)):
        for gname in ("ffn", "mix", "in"):
            sems, s_thru, l_thru, win, local = rs[(l, gname)]
            owns, parts = _xchg_wait(sems, s_thru, l_thru, local, small_parts, win, _slot, f"scatter_wait_l{l}_{gname}")
            for t, own, prt in zip(group_tids[gname], owns, parts):
                nm = BIG[t][0]
                rows = {"w_in": 256, "w_up": 256, "w_down": 256}.get(nm, P[nm].shape[1])
                prev[nm] = _adamw(t, prt, own, me_arr, P[nm], M[nm], V[nm], l, prev[nm], rows, f"adamw_{nm}_l{l}")
    for nm, _, _ in BIG:
        out_g[nm], out_d[nm], out_m[nm], out_v[nm] = prev[nm]
    sm_g, sm_d, sm_m, sm_v = _small_update(small_parts, _rowpack(P, True), _rowpack(M, True), _rowpack(V, True),
                                           me_arr, "small_update")
    for dst, src in ((out_g, sm_g), (out_d, sm_d), (out_m, sm_m), (out_v, sm_v)):
        dst.update(src)

    order = ["rel_bias", "attn_pre_norm", "w_in", "b_gate", "sinks", "w_br_a", "w_br_b", "w_br_c", "w_out",
             "attn_post_norm", "ffn_pre_norm", "w_up", "conv_w", "conv_b", "w_down", "ffn_post_norm"]
    return (loss, grad_x[None], *[out_g[k] for k in order], *[out_d[k] for k in order],
            *[out_m[k] for k in order], *[out_v[k] for k in order])
```

```python
import functools
import math

import numpy as np
import jax
import jax.numpy as jnp
from jax import lax
from jax.experimental import pallas as pl
from jax.experimental.pallas import tpu as pltpu

F32 = jnp.float32
BF16 = jnp.bfloat16

S = 2048
D = 1024
DEPTH = 2
NDEV = 8
HD = 64
BLK = 128
NB = S // BLK
A_GROUPS = ((128, 1), (512, 4), (2048, 16))
NUM_BUCKETS = 32
MAX_DISTANCE = 2048
N_BIAS_HEADS = 20
D_FF = 4096
IN_COLS = 6912
QKV_COLS = 3840
QKV_SLABS = QKV_COLS // 128
GATE_COLS = 3072
EPS = 1e-6
SCALE = HD ** -0.5
NEG = -1e30
LANES = 128

ADAM_LR = 0.001
ADAM_B1 = 0.9
ADAM_B2 = 0.999
ADAM_EPS = 1e-08
ADAM_WD = 0.01
ADAM_STEP = 10

VMEM_LIMIT = 56 * 1024 * 1024
MESH = pl.DeviceIdType.MESH
ANY = pl.BlockSpec(memory_space=pl.ANY)
SMEM = pl.BlockSpec(memory_space=pltpu.SMEM)


def _cp(*sem):
    return pltpu.CompilerParams(dimension_semantics=sem if sem else None, vmem_limit_bytes=VMEM_LIMIT)


def _dot(a, b, ca, cb):
    return lax.dot_general(a, b, (((ca,), (cb,)), ((), ())), preferred_element_type=F32)


def _mm(a, b, *, grid, a_spec, b_spec, out_shape, out_spec, ca, cb, acc_shape, name,
        a_slab=False, b_slab=False, out_slab=False, alias_out=None, after=None):
    nk = grid[2]

    def body(*refs):
        a_ref, b_ref = refs[0], refs[1]
        o_ref, acc_ref = refs[-2], refs[-1]
        k = pl.program_id(2)

        def load(ref, slab):
            if slab:
                return jnp.concatenate([ref[s] for s in range(ref.shape[0])], axis=1).astype(BF16)
            return ref[...].astype(BF16)

        def write(val):
            if out_slab:
                for s in range(o_ref.shape[0]):
                    o_ref[s] = val[:, s * LANES:(s + 1) * LANES].astype(o_ref.dtype)
            else:
                o_ref[...] = val.astype(o_ref.dtype)

        d = _dot(load(a_ref, a_slab), load(b_ref, b_slab), ca, cb)
        if nk == 1:
            write(d)
        elif direct:
            @pl.when(k == 0)
            def _():
                o_ref[...] = d

            @pl.when(k > 0)
            def _():
                o_ref[...] += d
        else:
            @pl.when(k == 0)
            def _():
                acc_ref[...] = d

            if nk > 2:
                @pl.when((k > 0) & (k < nk - 1))
                def _():
                    acc_ref[...] += d

            @pl.when(k == nk - 1)
            def _():
                write(acc_ref[...] + d)

    direct = (not out_slab) and out_shape.dtype == F32
    if nk == 1 or direct:
        acc_shape = (8, LANES)
    in_specs = [a_spec, b_spec]
    args = [a, b]
    aliases = {}
    if alias_out is not None:
        in_specs.append(ANY)
        args.append(alias_out)
        aliases = {2: 0}
    if after is not None:
        in_specs.append(ANY)
        args.append(after)
    return pl.pallas_call(
        body, grid=grid, in_specs=in_specs, out_specs=out_spec, out_shape=out_shape,
        scratch_shapes=[pltpu.VMEM(acc_shape, F32)], input_output_aliases=aliases,
        compiler_params=_cp("parallel", "parallel", "arbitrary"), name=name)(*args)


def _mm_nn(a, b, out_dtype, tm, tn, tk, name):
    m, kk = a.shape
    n = b.shape[1]
    return _mm(a, b, grid=(m // tm, n // tn, kk // tk),
               a_spec=pl.BlockSpec((tm, tk), lambda i, j, k: (i, k)),
               b_spec=pl.BlockSpec((tk, tn), lambda i, j, k: (k, j)),
               out_shape=jax.ShapeDtypeStruct((m, n), out_dtype),
               out_spec=pl.BlockSpec((tm, tn), lambda i, j, k: (i, j)),
               ca=1, cb=0, acc_shape=(tm, tn), name=name)


def _mm_nt(a, b, out_dtype, tm, tn, tk, name):
    m, kk = a.shape
    n = b.shape[0]
    return _mm(a, b, grid=(m // tm, n // tn, kk // tk),
               a_spec=pl.BlockSpec((tm, tk), lambda i, j, k: (i, k)),
               b_spec=pl.BlockSpec((tn, tk), lambda i, j, k: (j, k)),
               out_shape=jax.ShapeDtypeStruct((m, n), out_dtype),
               out_spec=pl.BlockSpec((tm, tn), lambda i, j, k: (i, j)),
               ca=1, cb=1, acc_shape=(tm, tn), name=name)


def _mm_tn(a, b, out_dtype, tm, tn, tk, name):
    kk, m = a.shape
    n = b.shape[1]
    return _mm(a, b, grid=(m // tm, n // tn, kk // tk),
               a_spec=pl.BlockSpec((tk, tm), lambda i, j, k: (k, i)),
               b_spec=pl.BlockSpec((tk, tn), lambda i, j, k: (k, j)),
               out_shape=jax.ShapeDtypeStruct((m, n), out_dtype),
               out_spec=pl.BlockSpec((tm, tn), lambda i, j, k: (i, j)),
               ca=0, cb=0, acc_shape=(tm, tn), name=name)


ROW_TILE = 256


def _rms(x, g):
    r = lax.rsqrt(jnp.mean(x * x, axis=-1, keepdims=True) + EPS)
    return x * r * g


def _prenorm(x, g, name):
    def body(x_ref, g_ref, o_ref):
        o_ref[...] = _rms(x_ref[...], g_ref[...]).astype(BF16)

    return pl.pallas_call(
        body, grid=(S // ROW_TILE,),
        in_specs=[pl.BlockSpec((ROW_TILE, D), lambda i: (i, 0)), pl.BlockSpec((1, D), lambda i: (0, 0))],
        out_specs=pl.BlockSpec((ROW_TILE, D), lambda i: (i, 0)),
        out_shape=jax.ShapeDtypeStruct((S, D), BF16), compiler_params=_cp("parallel"), name=name)(x, g)


def _postnorm_res(x, f, g_post, g_next, name):
    def body(x_ref, f_ref, gp_ref, gn_ref, xo_ref, ho_ref):
        xn = x_ref[...] + _rms(f_ref[...], gp_ref[...])
        xo_ref[...] = xn
        ho_ref[...] = _rms(xn, gn_ref[...]).astype(BF16)

    row = pl.BlockSpec((ROW_TILE, D), lambda i: (i, 0))
    vec = pl.BlockSpec((1, D), lambda i: (0, 0))
    return pl.pallas_call(
        body, grid=(S // ROW_TILE,), in_specs=[row, row, vec, vec], out_specs=[row, row],
        out_shape=[jax.ShapeDtypeStruct((S, D), F32), jax.ShapeDtypeStruct((S, D), BF16)],
        compiler_params=_cp("parallel"), name=name)(x, f, g_post, g_next)


def _norm_bwd(f, g, dys, res, out_dtype, name):
    ndy = len(dys)
    has_res = res is not None

    def body(*refs):
        f_ref, g_ref = refs[0], refs[1]
        dy_refs = refs[2:2 + ndy]
        res_ref = refs[2 + ndy] if has_res else None
        o_ref, dg_ref = refs[-2], refs[-1]
        fv = f_ref[...]
        dy = dy_refs[0][...].astype(F32)
        for r in dy_refs[1:]:
            dy = dy + r[...].astype(F32)
        r = lax.rsqrt(jnp.mean(fv * fv, axis=-1, keepdims=True) + EPS)
        n = fv * r
        dn = dy * g_ref[...]
        df = r * (dn - n * jnp.mean(dn * n, axis=-1, keepdims=True))
        if has_res:
            df = df + res_ref[...]
        o_ref[...] = df.astype(out_dtype)

        @pl.when(pl.program_id(0) == 0)
        def _():
            dg_ref[...] = jnp.zeros((1, D), F32)

        dg_ref[...] += jnp.sum(dy * n, axis=0, keepdims=True)

    row = pl.BlockSpec((ROW_TILE, D), lambda i: (i, 0))
    vec = pl.BlockSpec((1, D), lambda i: (0, 0))
    in_specs = [row, vec] + [row] * ndy + ([row] if has_res else [])
    args = [f, g] + list(dys) + ([res] if has_res else [])
    return pl.pallas_call(
        body, grid=(S // ROW_TILE,), in_specs=in_specs, out_specs=[row, vec],
        out_shape=[jax.ShapeDtypeStruct((S, D), out_dtype), jax.ShapeDtypeStruct((1, D), F32)],
        compiler_params=_cp("arbitrary"), name=name)(*args)


def _loss_head(y, target, name):
    def body(y_ref, t_ref, dy_ref, l_ref):
        e = y_ref[...] - t_ref[...]
        dy_ref[...] = e * (1.0 / D)

        @pl.when(pl.program_id(0) == 0)
        def _():
            l_ref[...] = jnp.zeros((8, LANES), F32)

        l_ref[...] += jnp.sum(e * e) * (0.5 / D)

    row = pl.BlockSpec((ROW_TILE, D), lambda i: (i, 0))
    return pl.pallas_call(
        body, grid=(S // ROW_TILE,), in_specs=[row, row],
        out_specs=[row, pl.BlockSpec((8, LANES), lambda i: (0, 0))],
        out_shape=[jax.ShapeDtypeStruct((S, D), F32), jax.ShapeDtypeStruct((8, LANES), F32)],
        compiler_params=_cp("arbitrary"), name=name)(y, target)


def _bucket_tiles():
    a = np.arange(BLK)[:, None]
    b = np.arange(2 * BLK)[None, :]
    dist = a + BLK - b
    out = np.zeros((4, 2, BLK, 2 * BLK), np.int32)
    cfg = [(w // d, d) for w, d in A_GROUPS] + [(BLK - 1, 1)]
    for gi, (max_dist, d) in enumerate(cfg):
        band = (dist >= 0) & (dist <= max_dist)
        tok = np.maximum(dist, 0) * d
        nf = np.maximum(tok, 1).astype(np.float32)
        max_exact = NUM_BUCKETS // 2
        large = max_exact + (np.log(nf / np.float32(max_exact)) / np.float32(math.log(MAX_DISTANCE / max_exact))
                             * np.float32(NUM_BUCKETS - max_exact)).astype(np.int32)
        large = np.minimum(large, NUM_BUCKETS - 1)
        bkt = np.where(tok < max_exact, tok, large).astype(np.int32)
        full = np.where(band, bkt, -1)
        out[gi, 1] = full
        out[gi, 0] = np.where(b >= BLK, full, -1)
    return out


def _bias_tiles(rel_bias, buckets, name):
    def body(tab_ref, bkt_ref, o_ref):
        h = pl.program_id(0)
        bkt = bkt_ref[...]
        acc = jnp.zeros(bkt.shape, F32)
        for bb in range(NUM_BUCKETS):
            acc = jnp.where(bkt == bb, tab_ref[bb, h], acc)
        o_ref[...] = jnp.where(bkt < 0, NEG, acc)

    return pl.pallas_call(
        body, grid=(N_BIAS_HEADS,),
        in_specs=[SMEM, pl.BlockSpec((None, 2, BLK, 2 * BLK), lambda h: (jnp.minimum(h // 4, 3), 0, 0, 0))],
        out_specs=pl.BlockSpec((None, 2, BLK, 2 * BLK), lambda h: (h, 0, 0, 0)),
        out_shape=jax.ShapeDtypeStruct((N_BIAS_HEADS, 2, BLK, 2 * BLK), F32),
        compiler_params=_cp("arbitrary"), name=name)(rel_bias, buckets)


def _bias_grad(gs, buckets, name):
    ng = len(gs)

    def body(*refs):
        g_refs = refs[:ng]
        bkt_ref, o_ref = refs[ng], refs[ng + 1]
        h = pl.program_id(0)
        g = g_refs[0][...]
        for r in g_refs[1:]:
            g = g + r[...]
        bkt = bkt_ref[...]
        row = lax.broadcasted_iota(jnp.int32, (NUM_BUCKETS, LANES), 0)
        lane = lax.broadcasted_iota(jnp.int32, (NUM_BUCKETS, LANES), 1)

        @pl.when(h == 0)
        def _():
            o_ref[...] = jnp.zeros((NUM_BUCKETS, LANES), F32)

        acc = o_ref[...]
        for bb in range(NUM_BUCKETS):
            s = jnp.sum(jnp.where(bkt == bb, g, 0.0))
            acc = jnp.where((row == bb) & (lane == h), s, acc)
        o_ref[...] = acc

    g_spec = pl.BlockSpec((None, BLK, 2 * BLK), lambda h: (h, 0, 0))
    return pl.pallas_call(
        body, grid=(N_BIAS_HEADS,),
        in_specs=[g_spec] * ng + [pl.BlockSpec((None, None, BLK, 2 * BLK), lambda h: (jnp.minimum(h // 4, 3), 1, 0, 0))],
        out_specs=pl.BlockSpec((NUM_BUCKETS, LANES), lambda h: (0, 0)),
        out_shape=jax.ShapeDtypeStruct((NUM_BUCKETS, LANES), F32),
        compiler_params=_cp("arbitrary"), name=name)(*gs, buckets)


def _to_class_major(src_ref, dst_refs, d, fn=None):
    ln = S // d
    for r in range(d):
        v = src_ref[pl.ds(r, ln, stride=d), :] if d > 1 else src_ref[...]
        outs = fn(v) if fn is not None else (v,) * len(dst_refs)
        for dst, o in zip(dst_refs, outs):
            dst[pl.ds(r * ln, ln), :] = o.astype(dst.dtype)


def _head_masks(rows):
    lane = lax.broadcasted_iota(jnp.int32, (rows, LANES), 1)
    return lane < HD, lane >= HD


def _split_heads(v):
    m0, m1 = _head_masks(v.shape[0])
    return jnp.where(m0, v, 0.0), jnp.where(m1, v, 0.0)


def _dup_head(v, hi):
    m0, _ = _head_masks(v.shape[0])
    r = pltpu.roll(v, HD, 1)
    return jnp.where(m0, jnp.where(hi, r, v), jnp.where(hi, v, r))


def _block_rows(b, d):
    nbc = NB // d
    i = b % nbc
    r = b // nbc
    has_prev = (i > 0).astype(jnp.int32)
    prev = pl.multiple_of(jnp.maximum(b - 1, 0) * BLK, BLK)
    nat = i * (BLK * d) + r
    return has_prev, prev, nat


def _lane_halves(v0, v1):
    lane = lax.broadcasted_iota(jnp.int32, (v0.shape[0], LANES), 1)
    return jnp.where(lane < HD, v0, v1)


def _band_fwd(proj, bias, sinks, *, d, q0, k0, v0, npairs, bias0, shared_kv, name):
    def body(sink_ref, q_ref, k_ref, v_ref, b_ref, num_ref, st_ref, qz0, qz1, ks, vs):
        p = pl.program_id(0)
        kv = (lambda v: (_dup_head(v, p >= 2),)) if shared_kv else None
        _to_class_major(q_ref, (qz0, qz1), d, lambda v: _split_heads(v * SCALE))
        _to_class_major(k_ref, (ks,), d, kv)
        _to_class_major(v_ref, (vs,), d, kv)
        lane = lax.broadcasted_iota(jnp.int32, (BLK, LANES), 1)

        def blk(b, carry):
            has_prev, prev, nat = _block_rows(b, d)
            cur = pl.multiple_of(b * BLK, BLK)
            k2 = jnp.concatenate([ks[pl.ds(prev, BLK), :], ks[pl.ds(cur, BLK), :]], axis=0)
            v2 = jnp.concatenate([vs[pl.ds(prev, BLK), :], vs[pl.ds(cur, BLK), :]], axis=0)
            nums, ms, ls = [], [], []
            for hh, qz in enumerate((qz0, qz1)):
                z = _dot(qz[pl.ds(cur, BLK), :], k2, 1, 1) + b_ref[hh, has_prev]
                m = jnp.max(z, axis=1, keepdims=True)
                e = jnp.exp(z - m)
                l = jnp.sum(e, axis=1, keepdims=True)
                num = _dot(e.astype(BF16), v2, 1, 0)
                if shared_kv:
                    sink = sink_ref[0, 2 * p + hh]
                    mx = jnp.maximum(m, sink)
                    c = jnp.exp(m - mx)
                    zden = l * c + jnp.exp(sink - mx)
                    num = num * (c / zden)
                    m = mx + jnp.log(zden)
                ls.append(l)
                ms.append(m)
                nums.append(num)
            num_t = jnp.where(lane < HD, nums[0], nums[1])
            if shared_kv:
                st_t = jnp.where(lane < HD, ms[0], ms[1])
            else:
                st_t = jnp.where(lane < 32, ms[0], jnp.where(lane < 64, ls[0], jnp.where(lane < 96, ms[1], ls[1])))
            if d > 1:
                num_ref[pl.ds(nat, BLK, stride=d), :] = num_t
                st_ref[pl.ds(nat, BLK, stride=d), :] = st_t
            else:
                num_ref[pl.ds(cur, BLK), :] = num_t
                st_ref[pl.ds(cur, BLK), :] = st_t
            return carry

        lax.fori_loop(0, NB, blk, 0, unroll=8)

    slab = lambda off, per_pair: pl.BlockSpec((None, S, LANES), (lambda p: (off + p, 0, 0)) if per_pair else (lambda p: (off, 0, 0)))
    out = pl.BlockSpec((None, S, LANES), lambda p: (p, 0, 0))
    return pl.pallas_call(
        body, grid=(npairs,),
        in_specs=[SMEM, slab(q0, True), slab(k0, not shared_kv), slab(v0, not shared_kv),
                  pl.BlockSpec((None, 2, 2, BLK, 2 * BLK), lambda p: (bias0 + p, 0, 0, 0, 0))],
        out_specs=[out, out],
        out_shape=[jax.ShapeDtypeStruct((npairs, S, LANES), F32)] * 2,
        scratch_shapes=[pltpu.VMEM((S, LANES), BF16)] * 4,
        compiler_params=_cp("arbitrary"), name=name)(sinks, proj, proj, proj, bias)


def _combine_a(nums, stats, name):
    rt = 512

    def body(n0, n1, n2, s0, s1, s2, o_ref, l_ref):
        n_refs, s_refs = (n0, n1, n2), (s0, s1, s2)
        outs, lses = [], []
        for hh in range(2):
            ms = [s[:, 64 * hh:64 * hh + 1] for s in s_refs]
            ls = [s[:, 64 * hh + 32:64 * hh + 33] for s in s_refs]
            mx = jnp.maximum(jnp.maximum(ms[0], ms[1]), ms[2])
            cs = [jnp.exp(m - mx) for m in ms]
            z = cs[0] * ls[0] + cs[1] * ls[1] + cs[2] * ls[2]
            acc = cs[0] * n_refs[0][:, hh * HD:(hh + 1) * HD]
            acc = acc + cs[1] * n_refs[1][:, hh * HD:(hh + 1) * HD]
            acc = acc + cs[2] * n_refs[2][:, hh * HD:(hh + 1) * HD]
            outs.append(acc / z)
            lses.append(mx + jnp.log(z))
        o_ref[...] = jnp.concatenate(outs, axis=1)
        l_ref[...] = _lane_halves(lses[0], lses[1])

    spec = pl.BlockSpec((None, rt, LANES), lambda p, i: (p, i, 0))
    return pl.pallas_call(
        body, grid=(2, S // rt), in_specs=[spec] * 6, out_specs=[spec, spec],
        out_shape=[jax.ShapeDtypeStruct((2, S, LANES), F32)] * 2,
        compiler_params=_cp("parallel", "parallel"), name=name)(*nums, *stats)


def _band_bwd(proj, bias, o, do, lse, sinks, *, d, q0, k0, v0, npairs, bias0, shared_kv, name):
    nkv = 1 if shared_kv else npairs

    def body(sink_ref, q_ref, k_ref, v_ref, b_ref, o_ref, do_ref, lse_ref,
             dq_ref, dk_ref, dv_ref, g_ref, ds_ref,
             qz0, qz1, ks, vs, doz0, doz1, ls0, ls1, dls0, dls1, stage, dq_nat, dk_cm, dv_cm, kv_nat, dk_acc, dv_acc):
        p = pl.program_id(0)
        m0, m1 = _head_masks(S)
        prod = do_ref[...] * o_ref[...]
        dl0 = jnp.sum(jnp.where(m0, prod, 0.0), axis=1, keepdims=True)
        dl1 = jnp.sum(jnp.where(m1, prod, 0.0), axis=1, keepdims=True)
        if shared_kv:
            row8 = lax.broadcasted_iota(jnp.int32, (8, LANES), 0)
            lane8 = lax.broadcasted_iota(jnp.int32, (8, LANES), 1)
            t = jnp.zeros((8, LANES), F32)
            lv = lse_ref[...]
            for hh in range(2):
                sink = sink_ref[0, 2 * p + hh]
                ps = jnp.exp(sink - lv[:, 64 * hh:64 * hh + 1])
                dsink = -jnp.sum(ps * (dl0 if hh == 0 else dl1))
                t = jnp.where((row8 == 0) & (lane8 == hh), dsink, t)
            ds_ref[...] = t
        else:
            ds_ref[...] = jnp.zeros((8, LANES), F32)
        kv = (lambda v: (_dup_head(v, p >= 2),)) if shared_kv else None
        _to_class_major(q_ref, (qz0, qz1), d, lambda v: _split_heads(v * SCALE))
        _to_class_major(k_ref, (ks,), d, kv)
        _to_class_major(v_ref, (vs,), d, kv)
        _to_class_major(do_ref, (doz0, doz1), d, _split_heads)
        def spread(v):
            a0, a1 = _head_masks(v.shape[0])
            r = pltpu.roll(v, HD, 1)
            return jnp.where(a0, v, r), jnp.where(a1, v, r)

        _to_class_major(lse_ref, (ls0, ls1), d, spread)
        stage[...] = jnp.where(m0, dl0, dl1)
        _to_class_major(stage, (dls0, dls1), d, spread)

        dk_cm[...] = jnp.zeros((S, LANES), F32)
        dv_cm[...] = jnp.zeros((S, LANES), F32)
        g_ref[...] = jnp.zeros((2, BLK, 2 * BLK), F32)
        lane = lax.broadcasted_iota(jnp.int32, (BLK, LANES), 1)

        def blk(b, carry):
            has_prev, prev, nat = _block_rows(b, d)
            cur = pl.multiple_of(b * BLK, BLK)
            k2 = jnp.concatenate([ks[pl.ds(prev, BLK), :], ks[pl.ds(cur, BLK), :]], axis=0)
            v2 = jnp.concatenate([vs[pl.ds(prev, BLK), :], vs[pl.ds(cur, BLK), :]], axis=0)
            dqs, dks, dvs = [], [], []
            for hh, (qz, doz, lsr, dlr) in enumerate(((qz0, doz0, ls0, dls0), (qz1, doz1, ls1, dls1))):
                qb = qz[pl.ds(cur, BLK), :]
                dob = doz[pl.ds(cur, BLK), :]
                lb = lsr[pl.ds(cur, BLK), :]
                dlb = dlr[pl.ds(cur, BLK), :]
                z = _dot(qb, k2, 1, 1) + b_ref[hh, has_prev]
                pr = jnp.exp(z - jnp.concatenate([lb, lb], axis=1))
                dp = _dot(dob, v2, 1, 1)
                dz = pr * (dp - jnp.concatenate([dlb, dlb], axis=1))
                g_ref[hh] += dz
                dzb = dz.astype(BF16)
                dqs.append(_dot(dzb, k2, 1, 0))
                dks.append(_dot(dzb, qb, 0, 0))
                dvs.append(_dot(pr.astype(BF16), dob, 0, 0))
            dq_t = jnp.where(lane < HD, dqs[0], dqs[1]) * SCALE
            dk_t = dks[0] + dks[1]
            dv_t = dvs[0] + dvs[1]
            dk_cm[pl.ds(prev, BLK), :] += dk_t[:BLK]
            dk_cm[pl.ds(cur, BLK), :] += dk_t[BLK:]
            dv_cm[pl.ds(prev, BLK), :] += dv_t[:BLK]
            dv_cm[pl.ds(cur, BLK), :] += dv_t[BLK:]
            if d > 1:
                dq_nat[pl.ds(nat, BLK, stride=d), :] = dq_t
            else:
                dq_nat[pl.ds(cur, BLK), :] = dq_t
            return carry

        lax.fori_loop(0, NB, blk, 0, unroll=8)
        dq_ref[...] = dq_nat[...].astype(BF16)

        def from_class_major(src, dst_ref):
            if d == 1:
                dst_ref[...] = src[...].astype(BF16)
            else:
                ln = S // d
                for r in range(d):
                    kv_nat[pl.ds(r, ln, stride=d), :] = src[pl.ds(r * ln, ln), :]
                dst_ref[...] = kv_nat[...].astype(BF16)

        if not shared_kv:
            from_class_major(dk_cm, dk_ref)
            from_class_major(dv_cm, dv_ref)
        else:
            @pl.when(p == 0)
            def _():
                dk_acc[...] = jnp.zeros((S, LANES), F32)
                dv_acc[...] = jnp.zeros((S, LANES), F32)

            mine = m1 == (p >= 2)
            for cm, acc in ((dk_cm, dk_acc), (dv_cm, dv_acc)):
                val = cm[...]
                acc[...] += jnp.where(mine, val + pltpu.roll(val, HD, 1), 0.0)

            @pl.when(p == npairs - 1)
            def _():
                from_class_major(dk_acc, dk_ref)
                from_class_major(dv_acc, dv_ref)

    slab = lambda off, per_pair: pl.BlockSpec((None, S, LANES), (lambda p: (off + p, 0, 0)) if per_pair else (lambda p: (off, 0, 0)))
    pair = pl.BlockSpec((None, S, LANES), lambda p: (p, 0, 0))
    kv_out = pair if not shared_kv else pl.BlockSpec((None, S, LANES), lambda p: (0, 0, 0))
    return pl.pallas_call(
        body, grid=(npairs,),
        in_specs=[SMEM, slab(q0, True), slab(k0, not shared_kv), slab(v0, not shared_kv),
                  pl.BlockSpec((None, 2, 2, BLK, 2 * BLK), lambda p: (bias0 + p, 0, 0, 0, 0)),
                  pair, pair, pair],
        out_specs=[pair, kv_out, kv_out,
                   pl.BlockSpec((None, 2, BLK, 2 * BLK), lambda p: (p, 0, 0, 0)),
                   pl.BlockSpec((None, 8, LANES), lambda p: (p, 0, 0))],
        out_shape=[jax.ShapeDtypeStruct((npairs, S, LANES), BF16),
                   jax.ShapeDtypeStruct((nkv, S, LANES), BF16),
                   jax.ShapeDtypeStruct((nkv, S, LANES), BF16),
                   jax.ShapeDtypeStruct((npairs, 2, BLK, 2 * BLK), F32),
                   jax.ShapeDtypeStruct((npairs, 8, LANES), F32)],
        scratch_shapes=[pltpu.VMEM((S, LANES), BF16)] * 6 + [pltpu.VMEM((S, LANES), F32)] * 11,
        compiler_params=_cp("arbitrary"), name=name)(sinks, proj, proj, proj, bias, o, do, lse)


KC = 512
NSUB = KC // BLK


def _split2(x):
    hi = x.astype(BF16)
    lo = (x - hi.astype(F32)).astype(BF16)
    return hi, lo


def _tri_ones(cmp):
    jj = lax.broadcasted_iota(jnp.int32, (2 * BLK, BLK), 0) % BLK
    ss = lax.broadcasted_iota(jnp.int32, (2 * BLK, BLK), 1)
    return jnp.concatenate([cmp(jj, ss).astype(BF16), jnp.ones((2 * BLK, BLK), BF16)], axis=1)


def _sub_sums(x, tri1):
    st = jnp.concatenate([x[:, s * BLK:(s + 1) * BLK] for s in range(NSUB)], axis=0)
    hi, lo = _split2(st)
    r = _dot(jnp.concatenate([hi, lo], axis=1), tri1, 1, 0)
    return ([r[s * BLK:(s + 1) * BLK, :BLK] for s in range(NSUB)], [r[s * BLK:(s + 1) * BLK, BLK:] for s in range(NSUB)])


def _log_sig_pair(z):
    lb = jnp.minimum(z, 0.0) - jnp.log1p(jnp.exp(-jnp.abs(z)))
    return lb, lb - z


QGROUPS = NB // NSUB


def _stick_fwd(proj, *, q0, k0, v0, name):
    def body(q_ref, k_ref, v_ref, o_ref, t_ref, qs, ks, vs):
        qs[...] = (q_ref[...] * SCALE).astype(BF16)
        ks[...] = k_ref[...].astype(BF16)
        vs[...] = v_ref[...].astype(BF16)
        tri1 = _tri_ones(lambda j, s: j > s)
        col = lax.broadcasted_iota(jnp.int32, (BLK, KC), 1)
        rowi = lax.broadcasted_iota(jnp.int32, (BLK, KC), 0)

        for qg in range(QGROUPS):
            def qblock(ii, carry0, qg=qg):
                t0 = pl.multiple_of((qg * NSUB + ii) * BLK, BLK)
                qb = qs[pl.ds(t0, BLK), :]
                accs = [jnp.zeros((BLK, HD), F32)] * 2
                runs = [jnp.zeros((BLK, BLK), F32)] * 2
                for c in reversed(range(qg + 1)):
                    s0 = c * KC
                    diag = c == qg
                    before = (s0 + col) < (t0 + rowi) if diag else None
                    for hh in range(2):
                        kh = ks[s0:s0 + KC, hh * HD:(hh + 1) * HD]
                        vh = vs[s0:s0 + KC, hh * HD:(hh + 1) * HD]
                        lb, lk = _log_sig_pair(_dot(qb[:, hh * HD:(hh + 1) * HD], kh, 1, 1))
                        if diag:
                            lk = jnp.where(before, lk, 0.0)
                        suf, tot = _sub_sums(lk, tri1)
                        ws, run = [], runs[hh]
                        for s in reversed(range(NSUB)):
                            ws.append(jnp.exp(lb[:, s * BLK:(s + 1) * BLK] + suf[s] + run))
                            run = run + tot[s]
                        w = jnp.concatenate(ws[::-1], axis=1)
                        if diag:
                            w = jnp.where(before, w, 0.0)
                        accs[hh] = accs[hh] + _dot(w.astype(BF16), vh, 1, 0)
                        runs[hh] = run
                o_ref[pl.ds(t0, BLK), :] = jnp.concatenate(accs, axis=1)
                t_ref[pl.ds(t0, BLK), :] = _lane_halves(runs[0], runs[1])
                return carry0

            lax.fori_loop(0, NSUB, qblock, 0)

    slab = lambda off: pl.BlockSpec((None, S, LANES), lambda p: (off + p, 0, 0))
    out = pl.BlockSpec((None, S, LANES), lambda p: (p, 0, 0))
    return pl.pallas_call(
        body, grid=(2,), in_specs=[slab(q0), slab(k0), slab(v0)], out_specs=[out, out],
        out_shape=[jax.ShapeDtypeStruct((2, S, LANES), F32)] * 2,
        scratch_shapes=[pltpu.VMEM((S, LANES), BF16)] * 3,
        compiler_params=_cp("arbitrary"), name=name)(proj, proj, proj)


def _stick_bwd(proj, do, tot, *, q0, k0, v0, name):
    def body(q_ref, k_ref, v_ref, do_ref, t_ref, dq_ref, dk_ref, dv_ref, qs, ks, vs, dos, dk_acc, dv_acc):
        qs[...] = (q_ref[...] * SCALE).astype(BF16)
        ks[...] = k_ref[...].astype(BF16)
        vs[...] = v_ref[...].astype(BF16)
        dos[...] = do_ref[...].astype(BF16)
        dk_acc[...] = jnp.zeros((2, S, HD), F32)
        dv_acc[...] = jnp.zeros((2, S, HD), F32)
        tri_inc = _tri_ones(lambda j, s: j <= s)
        tri_exc = _tri_ones(lambda j, s: j < s)
        col = lax.broadcasted_iota(jnp.int32, (BLK, KC), 1)
        rowi = lax.broadcasted_iota(jnp.int32, (BLK, KC), 0)

        for qg in range(QGROUPS):
            def qblock(ii, carry0, qg=qg):
                t0 = pl.multiple_of((qg * NSUB + ii) * BLK, BLK)
                qb = qs[pl.ds(t0, BLK), :]
                dob = dos[pl.ds(t0, BLK), :]
                tb = t_ref[pl.ds(t0, BLK), :]
                dqs = [jnp.zeros((BLK, HD), F32)] * 2
                pruns = [jnp.zeros((BLK, BLK), F32)] * 2
                eruns = [jnp.zeros((BLK, BLK), F32)] * 2
                for c in range(qg + 1):
                    s0 = c * KC
                    diag = c == qg
                    before = (s0 + col) < (t0 + rowi) if diag else None
                    for hh in range(2):
                        qh = qb[:, hh * HD:(hh + 1) * HD]
                        doh = dob[:, hh * HD:(hh + 1) * HD]
                        tt = tb[:, 64 * hh:64 * hh + 1]
                        kh = ks[s0:s0 + KC, hh * HD:(hh + 1) * HD]
                        vh = vs[s0:s0 + KC, hh * HD:(hh + 1) * HD]
                        lb, lk = _log_sig_pair(_dot(qh, kh, 1, 1))
                        if diag:
                            lk = jnp.where(before, lk, 0.0)
                        pin, ptot = _sub_sums(lk, tri_inc)
                        ws, prun = [], pruns[hh]
                        for s in range(NSUB):
                            ws.append(jnp.exp(lb[:, s * BLK:(s + 1) * BLK] + (tt - (pin[s] + prun))))
                            prun = prun + ptot[s]
                        w = jnp.concatenate(ws, axis=1)
                        if diag:
                            w = jnp.where(before, w, 0.0)
                        e = w * _dot(doh, vh, 1, 1)
                        pex, etot = _sub_sums(e, tri_exc)
                        cs, erun = [], eruns[hh]
                        for s in range(NSUB):
                            cs.append(pex[s] + erun)
                            erun = erun + etot[s]
                        sig = jnp.exp(lb)
                        dz = e * (1.0 - sig) - jnp.concatenate(cs, axis=1) * sig
                        if diag:
                            dz = jnp.where(before, dz, 0.0)
                        dz = dz.astype(BF16)
                        dqs[hh] = dqs[hh] + _dot(dz, kh, 1, 0)
                        dk_acc[hh, s0:s0 + KC, :] += _dot(dz, qh, 0, 0)
                        dv_acc[hh, s0:s0 + KC, :] += _dot(w.astype(BF16), doh, 0, 0)
                        pruns[hh], eruns[hh] = prun, erun
                dq_ref[pl.ds(t0, BLK), :] = (jnp.concatenate(dqs, axis=1) * SCALE).astype(BF16)
                return carry0

            lax.fori_loop(0, NSUB, qblock, 0)
        dk_ref[...] = jnp.concatenate([dk_acc[0], dk_acc[1]], axis=1).astype(BF16)
        dv_ref[...] = jnp.concatenate([dv_acc[0], dv_acc[1]], axis=1).astype(BF16)

    slab = lambda off: pl.BlockSpec((None, S, LANES), lambda p: (off + p, 0, 0))
    pair = pl.BlockSpec((None, S, LANES), lambda p: (p, 0, 0))
    return pl.pallas_call(
        body, grid=(2,), in_specs=[slab(q0), slab(k0), slab(v0), pair, pair], out_specs=[pair] * 3,
        out_shape=[jax.ShapeDtypeStruct((2, S, LANES), BF16)] * 3,
        scratch_shapes=[pltpu.VMEM((S, LANES), BF16)] * 4 + [pltpu.VMEM((2, S, HD), F32)] * 2,
        compiler_params=_cp("arbitrary"), name=name)(proj, proj, proj, do, tot)


def _cat_slabs(ref):
    return jnp.concatenate([ref[s] for s in range(ref.shape[0])], axis=1)


def _merge_fwd(o_a, o_b, o_c, gates, b_gate, wa, wb, wc, w_out, name):
    tm = ROW_TILE

    def body(oa_ref, ob_ref, oc_ref, g_ref, bg_ref, wa_ref, wb_ref, wc_ref, wo_ref, mg_ref, mo_ref):
        acc = jnp.zeros((tm, D), F32)
        for i, (o_ref, w_ref) in enumerate(((oa_ref, wa_ref), (ob_ref, wb_ref), (oc_ref, wc_ref))):
            pr = _dot(_cat_slabs(o_ref).astype(BF16), w_ref[...], 1, 0)
            sg = jax.nn.sigmoid(g_ref[:, i * D:(i + 1) * D] + bg_ref[i:i + 1, :])
            acc = acc + sg * pr
        mg = acc.astype(BF16)
        mg_ref[...] = mg
        mo_ref[...] = _dot(mg, wo_ref[...], 1, 0)

    slabs = lambda n: pl.BlockSpec((n, tm, LANES), lambda i: (0, i, 0))
    full = lambda r, c: pl.BlockSpec((r, c), lambda i: (0, 0))
    row = pl.BlockSpec((tm, D), lambda i: (i, 0))
    return pl.pallas_call(
        body, grid=(S // tm,),
        in_specs=[slabs(2), slabs(4), slabs(2), pl.BlockSpec((tm, GATE_COLS), lambda i: (i, 0)), full(3, D),
                  full(256, D), full(512, D), full(256, D), full(D, D)],
        out_specs=[row, row],
        out_shape=[jax.ShapeDtypeStruct((S, D), BF16), jax.ShapeDtypeStruct((S, D), F32)],
        compiler_params=_cp("parallel"), name=name)(o_a, o_b, o_c, gates, b_gate, wa, wb, wc, w_out)


def _merge_bwd(d_mo, o_a, o_b, o_c, gates, b_gate, wa, wb, wc, w_out, name):
    tm = ROW_TILE

    def body(dmo_ref, oa_ref, ob_ref, oc_ref, g_ref, bg_ref, wa_ref, wb_ref, wc_ref, wo_ref,
             doa_ref, dob_ref, doc_ref, dg_ref, dwa_ref, dwb_ref, dwc_ref, dbg_ref):
        @pl.when(pl.program_id(0) == 0)
        def _():
            dwa_ref[...] = jnp.zeros(dwa_ref.shape, F32)
            dwb_ref[...] = jnp.zeros(dwb_ref.shape, F32)
            dwc_ref[...] = jnp.zeros(dwc_ref.shape, F32)
            dbg_ref[...] = jnp.zeros(dbg_ref.shape, F32)

        dmg = _dot(dmo_ref[...], wo_ref[...], 1, 1)
        trip = ((oa_ref, wa_ref, doa_ref, dwa_ref), (ob_ref, wb_ref, dob_ref, dwb_ref), (oc_ref, wc_ref, doc_ref, dwc_ref))
        for i, (o_ref, w_ref, do_ref, dw_ref) in enumerate(trip):
            ob = _cat_slabs(o_ref).astype(BF16)
            pr = _dot(ob, w_ref[...], 1, 0)
            sg = jax.nn.sigmoid(g_ref[:, i * D:(i + 1) * D] + bg_ref[i:i + 1, :])
            dgate = dmg * pr * sg * (1.0 - sg)
            dg_ref[:, i * D:(i + 1) * D] = dgate.astype(BF16)
            dbg_ref[i:i + 1, :] += jnp.sum(dgate, axis=0, keepdims=True)
            dpr = (dmg * sg).astype(BF16)
            do = _dot(dpr, w_ref[...], 1, 1)
            for s in range(do_ref.shape[0]):
                do_ref[s] = do[:, s * LANES:(s + 1) * LANES]
            dw_ref[...] += _dot(ob, dpr, 0, 0)

    slabs = lambda n: pl.BlockSpec((n, tm, LANES), lambda i: (0, i, 0))
    full = lambda r, c: pl.BlockSpec((r, c), lambda i: (0, 0))
    row = pl.BlockSpec((tm, D), lambda i: (i, 0))
    return pl.pallas_call(
        body, grid=(S // tm,),
        in_specs=[row, slabs(2), slabs(4), slabs(2), pl.BlockSpec((tm, GATE_COLS), lambda i: (i, 0)), full(3, D),
                  full(256, D), full(512, D), full(256, D), full(D, D)],
        out_specs=[slabs(2), slabs(4), slabs(2), pl.BlockSpec((tm, GATE_COLS), lambda i: (i, 0)),
                   full(256, D), full(512, D), full(256, D), full(3, D)],
        out_shape=[jax.ShapeDtypeStruct((2, S, LANES), F32), jax.ShapeDtypeStruct((4, S, LANES), F32),
                   jax.ShapeDtypeStruct((2, S, LANES), F32), jax.ShapeDtypeStruct((S, GATE_COLS), BF16),
                   jax.ShapeDtypeStruct((256, D), F32), jax.ShapeDtypeStruct((512, D), F32),
                   jax.ShapeDtypeStruct((256, D), F32), jax.ShapeDtypeStruct((3, D), F32)],
        compiler_params=_cp("arbitrary"), name=name)(d_mo, o_a, o_b, o_c, gates, b_gate, wa, wb, wc, w_out)


FC = 256
GELU_K = math.sqrt(2.0 / math.pi)
GELU_C = 0.044715


RC = 64
NRC = S // RC


def _down(tail, cur, n):
    row = lax.broadcasted_iota(jnp.int32, tail.shape, 0)
    rolled = pltpu.roll(cur, n, 0)
    first = jnp.where(row < n, pltpu.roll(tail, n, 0), rolled[0:8])
    return jnp.concatenate([first, rolled[8:]], axis=0)


def _up(cur, head, n):
    row = lax.broadcasted_iota(jnp.int32, head.shape, 0)
    rolled = pltpu.roll(cur, RC - n, 0)
    last = jnp.where(row >= 8 - n, pltpu.roll(head, 8 - n, 0), rolled[RC - 8:])
    return jnp.concatenate([rolled[:RC - 8], last], axis=0)


def _conv_chunk(load, j, w_ref, b_ref, half):
    r0 = pl.multiple_of(j * RC, RC)
    cur = load(r0, RC)
    tail = jnp.where(j > 0, load(pl.multiple_of(jnp.maximum(r0 - 8, 0), 8), 8), 0.0)
    d1 = _down(tail, cur, 1)
    d2 = _down(tail, cur, 2)
    y = w_ref[0:1, half, :] * d2 + w_ref[1:2, half, :] * d1 + w_ref[2:3, half, :] * cur + b_ref[half:half + 1, :]
    return y, cur, d1, d2


def _chunk(j):
    return pl.ds(pl.multiple_of(j * RC, RC), RC)


def _fold8(x):
    return jnp.sum(x.reshape(RC // 8, 8, x.shape[-1]), axis=0)


def _ffn_act(u, conv_w, conv_b, name):
    def body(u_ref, w_ref, b_ref, a_ref):
        def step(j, carry):
            yg = _conv_chunk(lambda r, n: u_ref[0, pl.ds(r, n), :], j, w_ref, b_ref, 0)[0]
            yv = _conv_chunk(lambda r, n: u_ref[1, pl.ds(r, n), :], j, w_ref, b_ref, 1)[0]
            th = jnp.tanh(GELU_K * (yg + GELU_C * yg * yg * yg))
            a_ref[_chunk(j), :] = (0.5 * yg * (1.0 + th) * yv).astype(BF16)
            return carry

        lax.fori_loop(0, NRC, step, 0)

    return pl.pallas_call(
        body, grid=(D_FF // FC,),
        in_specs=[pl.BlockSpec((2, S, FC), lambda j: (0, 0, j)), pl.BlockSpec((3, 2, FC), lambda j: (0, 0, j)),
                  pl.BlockSpec((2, FC), lambda j: (0, j))],
        out_specs=pl.BlockSpec((S, FC), lambda j: (0, j)),
        out_shape=jax.ShapeDtypeStruct((S, D_FF), BF16),
        compiler_params=_cp("parallel"), name=name)(u, conv_w, conv_b)


def _ffn_act_bwd(u, d_a, conv_w, conv_b, name):
    def body(u_ref, da_ref, w_ref, b_ref, du_ref, dw_ref, db_ref, dy_s):
        def first(j, acc):
            yg, ug, ug1, ug2 = _conv_chunk(lambda r, n: u_ref[0, pl.ds(r, n), :], j, w_ref, b_ref, 0)
            yv, uv, uv1, uv2 = _conv_chunk(lambda r, n: u_ref[1, pl.ds(r, n), :], j, w_ref, b_ref, 1)
            th = jnp.tanh(GELU_K * (yg + GELU_C * yg * yg * yg))
            gelu = 0.5 * yg * (1.0 + th)
            dgelu = 0.5 * (1.0 + th) + 0.5 * yg * (1.0 - th * th) * GELU_K * (1.0 + 3.0 * GELU_C * yg * yg)
            da = da_ref[_chunk(j), :]
            dyg = da * yv * dgelu
            dyv = da * gelu
            dy_s[0, _chunk(j), :] = dyg
            dy_s[1, _chunk(j), :] = dyv
            new = (_fold8(dyg * ug2), _fold8(dyg * ug1), _fold8(dyg * ug), _fold8(dyg),
                   _fold8(dyv * uv2), _fold8(dyv * uv1), _fold8(dyv * uv), _fold8(dyv))
            return tuple(a + n for a, n in zip(acc, new))

        acc = lax.fori_loop(0, NRC, first, tuple(jnp.zeros((8, FC), F32) for _ in range(8)))
        for half in range(2):
            for k in range(3):
                dw_ref[k:k + 1, half, :] = jnp.sum(acc[4 * half + k], axis=0, keepdims=True)
            db_ref[half:half + 1, :] = jnp.sum(acc[4 * half + 3], axis=0, keepdims=True)

        def second(j, carry):
            for half in range(2):
                cur = dy_s[half, _chunk(j), :]
                h0 = pl.multiple_of(jnp.minimum((j + 1) * RC, S - 8), 8)
                head = jnp.where(j < NRC - 1, dy_s[half, pl.ds(h0, 8), :], 0.0)
                du = (w_ref[2:3, half, :] * cur + w_ref[1:2, half, :] * _up(cur, head, 1)
                      + w_ref[0:1, half, :] * _up(cur, head, 2))
                du_ref[half, _chunk(j), :] = du.astype(BF16)
            return carry

        lax.fori_loop(0, NRC, second, 0)

    return pl.pallas_call(
        body, grid=(D_FF // FC,),
        in_specs=[pl.BlockSpec((2, S, FC), lambda j: (0, 0, j)), pl.BlockSpec((S, FC), lambda j: (0, j)),
                  pl.BlockSpec((3, 2, FC), lambda j: (0, 0, j)), pl.BlockSpec((2, FC), lambda j: (0, j))],
        out_specs=[pl.BlockSpec((2, S, FC), lambda j: (0, 0, j)), pl.BlockSpec((3, 2, FC), lambda j: (0, 0, j)),
                   pl.BlockSpec((2, FC), lambda j: (0, j))],
        out_shape=[jax.ShapeDtypeStruct((2, S, D_FF), BF16), jax.ShapeDtypeStruct((3, 2, D_FF), F32),
                   jax.ShapeDtypeStruct((2, D_FF), F32)],
        scratch_shapes=[pltpu.VMEM((2, S, FC), F32)],
        compiler_params=_cp("parallel"), name=name)(u, d_a, conv_w, conv_b)


def _layer_fwd(x, h1, w, bias, lname):
    n = lambda s: f"{lname}_{s}"
    w.need("in", h1)
    tn = 768
    proj = _mm(h1, w["w_in"], grid=(S // 1024, QKV_COLS // tn, 1),
               a_spec=pl.BlockSpec((1024, D), lambda i, j, k: (i, 0)),
               b_spec=pl.BlockSpec((D, tn), lambda i, j, k: (0, j)),
               out_shape=jax.ShapeDtypeStruct((QKV_SLABS, S, LANES), F32),
               out_spec=pl.BlockSpec((tn // LANES, 1024, LANES), lambda i, j, k: (j, i, 0)),
               ca=1, cb=0, acc_shape=(1024, tn), out_slab=True, name=n("proj_qkv"))
    gates = _mm(h1, w["w_in"], grid=(S // 1024, GATE_COLS // tn, 1),
                a_spec=pl.BlockSpec((1024, D), lambda i, j, k: (i, 0)),
                b_spec=pl.BlockSpec((D, tn), lambda i, j, k: (0, j + QKV_COLS // tn)),
                out_shape=jax.ShapeDtypeStruct((S, GATE_COLS), F32),
                out_spec=pl.BlockSpec((1024, tn), lambda i, j, k: (i, j)),
                ca=1, cb=0, acc_shape=(1024, tn), name=n("proj_gate"))
    nums, stats = [], []
    for g, (_, d) in enumerate(A_GROUPS):
        nm, st = _band_fwd(proj, bias, w["sinks"], d=d, q0=2 * g, k0=6 + 2 * g, v0=12 + 2 * g, npairs=2, bias0=2 * g,
                           shared_kv=False, name=n(f"attn_a{g}_fwd"))
        nums.append(nm)
        stats.append(st)
    o_a, lse_a = _combine_a(nums, stats, n("attn_a_combine"))
    o_b, lse_b = _band_fwd(proj, bias, w["sinks"], d=1, q0=18, k0=22, v0=23, npairs=4, bias0=6, shared_kv=True,
                           name=n("attn_b_fwd"))
    o_c, tot_c = _stick_fwd(proj, q0=24, k0=26, v0=28, name=n("attn_c_fwd"))
    w.need("mix", tot_c)
    merged, mo = _merge_fwd(o_a, o_b, o_c, gates, w["b_gate"], w["w_br_a"], w["w_br_b"], w["w_br_c"], w["w_out"], n("merge_fwd"))
    x2, h2 = _postnorm_res(x, mo, w["attn_post_norm"], w["ffn_pre_norm"], n("attn_post"))
    w.need("ffn", h2)
    u = _mm(h2, w["w_up"], grid=(S // 1024, 2 * D_FF // 1024, 1),
            a_spec=pl.BlockSpec((1024, D), lambda i, j, k: (i, 0)),
            b_spec=pl.BlockSpec((D, 1024), lambda i, j, k: (0, j)),
            out_shape=jax.ShapeDtypeStruct((2, S, D_FF), F32),
            out_spec=pl.BlockSpec((None, 1024, 1024), lambda i, j, k: (j // 4, i, j % 4)),
            ca=1, cb=0, acc_shape=(1024, 1024), name=n("ffn_up"))
    a = _ffn_act(u, w["conv_w"], w["conv_b"], n("ffn_act"))
    fo = _mm_nn(a, w["w_down"], F32, 1024, 1024, 2048, n("ffn_down"))
    saved = dict(x=x, h1=h1, proj=proj, gates=gates, o_a=o_a, lse_a=lse_a, o_b=o_b, lse_b=lse_b, o_c=o_c, tot_c=tot_c,
                 merged=merged, mo=mo, x2=x2, h2=h2, u=u, a=a, fo=fo)
    return saved


def _layer_bwd(dx3, sv, w, bias, lname, tok=None, on_part=None):
    n = lambda s: f"{lname}_{s}"
    g = {}

    def part(group, vec):
        t = on_part(group, g) if on_part is not None else None
        return vec if t is None else vec + t

    gain = w["ffn_post_norm"] if tok is None else w["ffn_post_norm"] + tok
    d_fo, g["ffn_post_norm"] = _norm_bwd(sv["fo"], gain, [dx3], None, BF16, n("ffn_post_bwd"))
    d_a = _mm_nt(d_fo, w["w_down"], F32, 1024, 1024, 1024, n("ffn_down_bwd_x"))
    g["w_down"] = _mm_tn(sv["a"], d_fo, BF16, 1024, 1024, S, n("ffn_down_bwd_w"))
    d_u, dcw, dcb = _ffn_act_bwd(sv["u"], d_a, w["conv_w"], w["conv_b"], n("ffn_act_bwd"))
    g["conv_w"] = dcw.reshape(3, 2 * D_FF)
    g["conv_b"] = dcb.reshape(1, 2 * D_FF)
    g["w_up"] = _mm(sv["h2"], d_u, grid=(1, 2 * D_FF // 1024, 1),
                    a_spec=pl.BlockSpec((S, D), lambda i, j, k: (k, 0)),
                    b_spec=pl.BlockSpec((None, S, 1024), lambda i, j, k: (j // 4, k, j % 4)),
                    out_shape=jax.ShapeDtypeStruct((D, 2 * D_FF), BF16),
                    out_spec=pl.BlockSpec((D, 1024), lambda i, j, k: (0, j)),
                    ca=0, cb=0, acc_shape=(D, 1024), name=n("ffn_up_bwd_w"))
    tok_ffn = on_part("ffn", g) if on_part is not None else None
    d_h2 = _mm(d_u, w["w_up"], grid=(S // 1024, 1, 2 * D_FF // 2048),
               a_spec=pl.BlockSpec((None, 1024, 2048), lambda i, j, k: (k // 2, i, k % 2)),
               b_spec=pl.BlockSpec((D, 2048), lambda i, j, k: (0, k)),
               out_shape=jax.ShapeDtypeStruct((S, D), F32),
               out_spec=pl.BlockSpec((1024, D), lambda i, j, k: (i, 0)),
               ca=1, cb=1, acc_shape=(1024, D), after=tok_ffn, name=n("ffn_up_bwd_x"))
    dx2, g["ffn_pre_norm"] = _norm_bwd(sv["x2"], w["ffn_pre_norm"], [d_h2], dx3, F32, n("ffn_pre_bwd"))
    d_mo, g["attn_post_norm"] = _norm_bwd(sv["mo"], w["attn_post_norm"], [dx2], None, BF16, n("attn_post_bwd"))
    g["w_out"] = _mm_tn(sv["merged"], d_mo, BF16, 1024, 1024, S, n("out_bwd_w"))
    do_a, do_b, do_c, d_gates, dwa, dwb, dwc, g["b_gate"] = _merge_bwd(
        d_mo, sv["o_a"], sv["o_b"], sv["o_c"], sv["gates"], w["b_gate"], w["w_br_a"], w["w_br_b"], w["w_br_c"],
        w["w_out"], n("merge_bwd"))
    g["w_br_a"], g["w_br_b"], g["w_br_c"] = dwa, dwb, dwc
    sinks = part("mix", w["sinks"])
    proj = sv["proj"]
    dqa, dka, dva, gbias = [], [], [], []
    for gi, (_, d) in enumerate(A_GROUPS):
        dq, dk, dv, gg, _ = _band_bwd(proj, bias, sv["o_a"], do_a, sv["lse_a"], sinks, d=d, q0=2 * gi, k0=6 + 2 * gi,
                                      v0=12 + 2 * gi, npairs=2, bias0=2 * gi, shared_kv=False, name=n(f"attn_a{gi}_bwd"))
        dqa.append(dq), dka.append(dk), dva.append(dv), gbias.append(gg)
    dqb, dkb, dvb, ggb, dsink = _band_bwd(proj, bias, sv["o_b"], do_b, sv["lse_b"], sinks, d=1, q0=18, k0=22, v0=23,
                                          npairs=4, bias0=6, shared_kv=True, name=n("attn_b_bwd"))
    gbias.append(ggb)
    g["bias_g"] = jnp.concatenate(gbias, axis=0).reshape(N_BIAS_HEADS, BLK, 2 * BLK)
    g["sinks"] = dsink[:, 0, :2].reshape(1, 8)
    dqc, dkc, dvc = _stick_bwd(proj, do_c, sv["tot_c"], q0=24, k0=26, v0=28, name=n("attn_c_bwd"))
    dqkv = jnp.concatenate(dqa + dka + dva + [dqb, dkb, dvb, dqc, dkc, dvc], axis=0)
    ts = 6
    tsx = 15
    dw_in = _mm(sv["h1"], dqkv, grid=(1, QKV_SLABS // ts, 1),
                a_spec=pl.BlockSpec((S, D), lambda i, j, k: (k, 0)),
                b_spec=pl.BlockSpec((ts, S, LANES), lambda i, j, k: (j, k, 0)),
                out_shape=jax.ShapeDtypeStruct((D, IN_COLS), BF16),
                out_spec=pl.BlockSpec((D, ts * LANES), lambda i, j, k: (0, j)),
                ca=0, cb=0, acc_shape=(D, ts * LANES), b_slab=True, name=n("in_bwd_w_qkv"))
    g["w_in"] = _mm(sv["h1"], d_gates, grid=(1, GATE_COLS // 768, 1),
                    a_spec=pl.BlockSpec((S, D), lambda i, j, k: (k, 0)),
                    b_spec=pl.BlockSpec((S, 768), lambda i, j, k: (k, j)),
                    out_shape=jax.ShapeDtypeStruct((D, IN_COLS), BF16),
                    out_spec=pl.BlockSpec((D, 768), lambda i, j, k: (0, j + QKV_COLS // 768)),
                    ca=0, cb=0, acc_shape=(D, 768), alias_out=dw_in, name=n("in_bwd_w_gate"))
    tok_in = on_part("in", g) if on_part is not None else None
    d_h1a = _mm(dqkv, w["w_in"], grid=(S // 1024, 1, QKV_SLABS // tsx),
                a_spec=pl.BlockSpec((tsx, 1024, LANES), lambda i, j, k: (k, i, 0)),
                b_spec=pl.BlockSpec((D, tsx * LANES), lambda i, j, k: (0, k)),
                out_shape=jax.ShapeDtypeStruct((S, D), F32),
                out_spec=pl.BlockSpec((1024, D), lambda i, j, k: (i, 0)),
                ca=1, cb=1, acc_shape=(1024, D), a_slab=True, after=tok_in, name=n("in_bwd_x_qkv"))
    d_h1b = _mm(d_gates, w["w_in"], grid=(S // 1024, 1, GATE_COLS // 768),
                a_spec=pl.BlockSpec((1024, 768), lambda i, j, k: (i, k)),
                b_spec=pl.BlockSpec((D, 768), lambda i, j, k: (0, k + QKV_COLS // 768)),
                out_shape=jax.ShapeDtypeStruct((S, D), F32),
                out_spec=pl.BlockSpec((1024, D), lambda i, j, k: (i, 0)),
                ca=1, cb=1, acc_shape=(1024, D), after=tok_in, name=n("in_bwd_x_gate"))
    dx, g["attn_pre_norm"] = _norm_bwd(sv["x"], w["attn_pre_norm"], [d_h1a, d_h1b], dx2, F32, n("attn_pre_bwd"))
    return dx, g, tok_in


def _local_step(x, target, ws, rel_bias, tok=None, on_grads=None):
    buckets = jnp.asarray(_bucket_tiles())
    bias = _bias_tiles(rel_bias, buckets, "bias_tiles").reshape(N_BIAS_HEADS // 2, 2, 2, BLK, 2 * BLK)
    saved = []
    gain0 = ws[0]["attn_pre_norm"] if tok is None else ws[0]["attn_pre_norm"] + tok
    h1 = _prenorm(x, gain0, "l0_attn_pre")
    for l in range(DEPTH):
        sv = _layer_fwd(x, h1, ws[l], bias, f"l{l}")
        saved.append(sv)
        g_next = ws[l + 1]["attn_pre_norm"] if l + 1 < DEPTH else ws[l]["attn_pre_norm"]
        x, h1 = _postnorm_res(sv["x2"], sv["fo"], ws[l]["ffn_post_norm"], g_next, f"l{l}_ffn_post")
    dy, loss_tile = _loss_head(x, target, "loss_head")
    grads = [None] * DEPTH
    tok = None
    for l in reversed(range(DEPTH)):
        on_part = None if on_grads is None else functools.partial(on_grads, l)
        dy, grads[l], tok = _layer_bwd(dy, saved[l], ws[l], bias, f"l{l}", tok, on_part)
    g_rel = _bias_grad([grads[l]["bias_g"] for l in range(DEPTH)], buckets, "bias_grad")[:, :N_BIAS_HEADS]
    return loss_tile[0, 0], dy, grads, g_rel


def _coords():
    return lax.axis_index("x"), lax.axis_index("y"), lax.axis_index("c")


def _peer(rel):
    x, y, c = _coords()
    return (1 - x if rel & 4 else x, 1 - y if rel & 2 else y, 1 - c if rel & 1 else c)


def _exchange(srcs, dst_shapes, src_win, dst_win, name, after=None):
    nt = len(srcs)
    extra = [] if after is None else [after]

    def body(*refs):
        src_refs, dst_refs = refs[:nt], refs[nt + len(extra):2 * nt + len(extra)]
        send_sems, recv_sems, local_sems = refs[2 * nt + len(extra):]
        x, y, c = _coords()
        me = 4 * x + 2 * y + c
        locals_ = []
        for t in range(nt):
            cp = pltpu.make_async_copy(src_win(t, src_refs[t], me), dst_win(t, dst_refs[t], me), local_sems.at[t])
            cp.start()
            locals_.append(cp)
        sends = []
        for rel in range(1, NDEV):
            px, py, pc = _peer(rel)
            q = 4 * px + 2 * py + pc
            for t in range(nt):
                cp = pltpu.make_async_remote_copy(
                    src_ref=src_win(t, src_refs[t], q), dst_ref=dst_win(t, dst_refs[t], me),
                    send_sem=send_sems.at[rel - 1, t], recv_sem=recv_sems.at[rel - 1, t],
                    device_id=(px, py, pc), device_id_type=MESH)
                cp.start()
                sends.append(cp)
        for rel in range(1, NDEV):
            px, py, pc = _peer(rel)
            q = 4 * px + 2 * py + pc
            for t in range(nt):
                pltpu.make_async_remote_copy(
                    src_ref=src_win(t, src_refs[t], me), dst_ref=dst_win(t, dst_refs[t], q),
                    send_sem=send_sems.at[rel - 1, t], recv_sem=recv_sems.at[rel - 1, t],
                    device_id=(px, py, pc), device_id_type=MESH).wait_recv()
        for cp in sends:
            cp.wait_send()
        for cp in locals_:
            cp.wait()

    return pl.pallas_call(
        body, in_specs=[ANY] * (nt + len(extra)), out_specs=[ANY] * nt, out_shape=dst_shapes,
        scratch_shapes=[pltpu.SemaphoreType.DMA((NDEV - 1, nt)), pltpu.SemaphoreType.DMA((NDEV - 1, nt)),
                        pltpu.SemaphoreType.DMA((nt,))],
        name=name)(*srcs, *extra)


BIG = (("w_in", 1, 864), ("w_br_a", 1, 128), ("w_br_b", 1, 128), ("w_br_c", 1, 128), ("w_out", 0, 128),
       ("w_up", 1, 1024), ("w_down", 0, 512))


NBIG = len(BIG)
BIG_FULL = {"w_in": (D, IN_COLS), "w_br_a": (256, D), "w_br_b": (512, D), "w_br_c": (256, D), "w_out": (D, D),
            "w_up": (D, 2 * D_FF), "w_down": (D_FF, D)}
LAYER_GROUPS = (("in", (0,)), ("mix", (1, 2, 3, 4)), ("ffn", (5, 6)))

HBM_SPEC = pl.BlockSpec(memory_space=pltpu.HBM)
SEM_SPEC = pl.BlockSpec(memory_space=pltpu.SEMAPHORE)


def _hbm(a):
    return pltpu.with_memory_space_constraint(a, pltpu.HBM)


def _shard_window(t, ref, k):
    nm, ax, ext = BIG[t % NBIG]
    if nm == "w_in":
        return ref.at[k]
    off = pl.multiple_of(k * ext, ext)
    if ax == 0:
        return ref.at[pl.ds(off, ext), :]
    return ref.at[:, pl.ds(off, ext)]


def _whole(t, ref, k):
    return ref


def _slot(t, ref, k):
    return ref.at[k]


def _own_block_spec(t, rows, me_of):
    nm, ax, ext = BIG[t % NBIG]
    r, c = BIG_FULL[nm]
    if nm == "w_in":
        return pl.BlockSpec((None, rows, ext), lambda i, m: (me_of(m), i, 0))
    if ax == 0:
        return pl.BlockSpec((rows, c), lambda i, m: (me_of(m) * (ext // rows) + i, 0))
    return pl.BlockSpec((rows, ext), lambda i, m: (i, me_of(m)))


def _cast_own(t, shard, me_arr, name):
    nm, ax, ext = BIG[t % NBIG]
    nr, nc = shard.shape
    rows = min(nr, 256)
    shape = (NDEV, D, ext) if nm == "w_in" else BIG_FULL[nm]

    def body(m_ref, s_ref, o_ref):
        o_ref[...] = s_ref[...].astype(BF16)

    return pl.pallas_call(
        body, grid_spec=pltpu.PrefetchScalarGridSpec(
            num_scalar_prefetch=1, grid=(nr // rows,),
            in_specs=[pl.BlockSpec((rows, nc), lambda i, m: (i, 0))],
            out_specs=_own_block_spec(t, rows, lambda m: m[0])),
        out_shape=jax.ShapeDtypeStruct(shape, BF16), compiler_params=_cp("arbitrary"), name=name)(me_arr, shard)


def _xchg_start(srcs, lands, groups, src_win, dst_win, after, name):
    ns = 0 if srcs is None else len(srcs)
    nt, ng = len(lands), len(groups)
    ins = ([] if srcs is None else list(srcs)) + list(lands)

    def body(*refs):
        src_refs, land_refs = refs[:ns], refs[ns:ns + nt]
        sems = refs[ns + nt + 1:ns + nt + 1 + 2 * ng]
        token = refs[-1]
        x, y, c = _coords()
        me = 4 * x + 2 * y + c
        for gi, grp in enumerate(groups):
            for j, t in enumerate(grp):
                for rel in range(1, NDEV):
                    px, py, pc = _peer(rel)
                    q = 4 * px + 2 * py + pc
                    src = dst_win(t, land_refs[t], me) if srcs is None else src_win(t, src_refs[t], q)
                    pltpu.make_async_remote_copy(
                        src_ref=src, dst_ref=dst_win(t, land_refs[t], me),
                        send_sem=sems[2 * gi].at[(rel - 1) * len(grp) + j],
                        recv_sem=sems[2 * gi + 1].at[(rel - 1) * len(grp) + j],
                        device_id=(px, py, pc), device_id_type=MESH).start()
        token[...] = jnp.zeros((8, LANES), F32)

    out_shape = []
    for grp in groups:
        out_shape += [pltpu.SemaphoreType.DMA(((NDEV - 1) * len(grp),))] * 2
    out_shape += [pltpu.HBM(a.shape, a.dtype) for a in ins]
    out_shape.append(jax.ShapeDtypeStruct((8, LANES), F32))
    outs = pl.pallas_call(
        body, in_specs=[HBM_SPEC] * len(ins) + [ANY],
        out_specs=[SEM_SPEC] * (2 * ng) + [HBM_SPEC] * len(ins) + [pl.BlockSpec(memory_space=pltpu.VMEM)],
        out_shape=out_shape, input_output_aliases={i: 2 * ng + i for i in range(len(ins))},
        compiler_params=pltpu.CompilerParams(has_side_effects=pltpu.SideEffectType.DATAFLOW_SIDE_EFFECTING),
        name=name)(*[_hbm(a) for a in ins], after)
    sems = [(outs[2 * gi], outs[2 * gi + 1]) for gi in range(ng)]
    thru = list(outs[2 * ng:2 * ng + len(ins)])
    return sems, (None if srcs is None else thru[:ns]), thru[ns:], outs[-1]


def _xchg_wait(sems, srcs, lands, tids, after, src_win, dst_win, name):
    ns = 0 if srcs is None else len(srcs)
    n = len(lands)
    send_sem, recv_sem = sems
    ins = ([] if srcs is None else list(srcs)) + list(lands)

    def body(*refs):
        src_refs, land_refs = refs[:ns], refs[ns:ns + n]
        ssem, rsem = refs[ns + n], refs[ns + n + 1]
        x, y, c = _coords()
        me = 4 * x + 2 * y + c
        for j, t in enumerate(tids):
            for rel in range(1, NDEV):
                px, py, pc = _peer(rel)
                q = 4 * px + 2 * py + pc
                src = dst_win(t, land_refs[j], me) if srcs is None else src_win(t, src_refs[j], q)
                cp = pltpu.make_async_remote_copy(
                    src_ref=src, dst_ref=dst_win(t, land_refs[j], q),
                    send_sem=ssem.at[(rel - 1) * n + j], recv_sem=rsem.at[(rel - 1) * n + j],
                    device_id=(px, py, pc), device_id_type=MESH)
                cp.wait_send()
                cp.wait_recv()

    outs = pl.pallas_call(
        body, in_specs=[HBM_SPEC] * len(ins) + [SEM_SPEC, SEM_SPEC, ANY], out_specs=[HBM_SPEC] * len(ins),
        out_shape=[pltpu.HBM(a.shape, a.dtype) for a in ins],
        input_output_aliases={i: i for i in range(len(ins))},
        compiler_params=pltpu.CompilerParams(has_side_effects=pltpu.SideEffectType.DATAFLOW_SIDE_EFFECTING),
        name=name)(*ins, send_sem, recv_sem, after)
    return (None if srcs is None else list(outs[:ns])), list(outs[ns:])


class _Weights:
    def __init__(self, ready, pending=None):
        self.ready = dict(ready)
        self.pending = dict(pending or {})

    def __getitem__(self, k):
        return self.ready[k]

    def need(self, group, after):
        fn = self.pending.pop(group, None)
        if fn is not None:
            self.ready.update(fn(after))


def _adamw_math(w, g, m, v):
    m2 = ADAM_B1 * m + (1.0 - ADAM_B1) * g
    v2 = ADAM_B2 * v + (1.0 - ADAM_B2) * (g * g)
    m_hat = m2 / (1.0 - ADAM_B1 ** ADAM_STEP)
    v_hat = v2 / (1.0 - ADAM_B2 ** ADAM_STEP)
    delta = -ADAM_LR * (m_hat / (jnp.sqrt(v_hat) + ADAM_EPS) + ADAM_WD * w)
    return delta, m2, v2


def _adamw(t, parts, own, me_arr, w, m, v, layer, prev, rows, name):
    nl, nr, nc = w.shape

    def body(me_ref, p_ref, own_ref, w_ref, m_ref, v_ref, *rest):
        g_ref, d_ref, m2_ref, v2_ref = rest[-4:]
        me = me_ref[0]
        g = None
        for k in range(NDEV):
            term = jnp.where(me == k, own_ref[...], p_ref[k]).astype(F32)
            g = term if g is None else g + term
        delta, m2, v2 = _adamw_math(w_ref[...], g, m_ref[...], v_ref[...])
        g_ref[...] = g
        d_ref[...] = delta
        m2_ref[...] = m2
        v2_ref[...] = v2

    blk = pl.BlockSpec((None, rows, nc), lambda i, mm: (layer, i, 0))
    pblk = pl.BlockSpec((NDEV, rows, nc), lambda i, mm: (0, i, 0))
    extra = [] if prev is None else list(prev)
    return pl.pallas_call(
        body, grid_spec=pltpu.PrefetchScalarGridSpec(
            num_scalar_prefetch=1, grid=(nr // rows,),
            in_specs=[pblk, _own_block_spec(t, rows, lambda mm: mm[0]), blk, blk, blk] + [ANY] * len(extra),
            out_specs=[blk] * 4),
        out_shape=[jax.ShapeDtypeStruct(w.shape, F32)] * 4,
        input_output_aliases={6 + k: k for k in range(len(extra))},
        compiler_params=_cp("arbitrary"), name=name)(me_arr, parts, own, w, m, v, *extra)


SMALL_REPL = (("rel_bias", NUM_BUCKETS * N_BIAS_HEADS), ("attn_pre_norm", DEPTH * D), ("sinks", DEPTH * 8),
              ("attn_post_norm", DEPTH * D), ("ffn_pre_norm", DEPTH * D), ("conv_b", DEPTH * 2 * D_FF),
              ("ffn_post_norm", DEPTH * D))
SMALL_SHARD = (("b_gate", (DEPTH, 3, D), 128), ("conv_w", (DEPTH, 3, 2 * D_FF), 1024))


def _pack(vecs):
    flat = jnp.concatenate([v.reshape(-1).astype(F32) for v in vecs])
    n = flat.shape[0]
    rows = -(-n // (8 * LANES)) * 8
    return jnp.pad(flat, (0, rows * LANES - n)).reshape(rows, LANES)


def _unpack(packed, sizes):
    flat = packed.reshape(-1)
    out, off = [], 0
    for sz in sizes:
        out.append(flat[off:off + sz])
        off += sz
    return out


ROWPACK = (("rel_bias", 32, 32, (NUM_BUCKETS, N_BIAS_HEADS)), ("sinks", 8, 8, (DEPTH, 8)),
           ("attn_pre_norm", 16, 16, (DEPTH, D)), ("attn_post_norm", 16, 16, (DEPTH, D)),
           ("ffn_pre_norm", 16, 16, (DEPTH, D)), ("ffn_post_norm", 16, 16, (DEPTH, D)),
           ("conv_b", 128, 128, (DEPTH, 2 * D_FF)), ("b_gate", 48, 8, (DEPTH, 3, 128)),
           ("conv_w", 384, 48, (DEPTH, 3, 1024)))
ROWS_FULL = sum(r for _, r, _, _ in ROWPACK)
ROWS_OWN = sum(r for _, _, r, _ in ROWPACK)


def _as_rows(a, rows):
    a = a.astype(F32)
    if a.shape[-1] < LANES:
        a = jnp.pad(a.reshape(-1, a.shape[-1]), ((0, 0), (0, LANES - a.shape[-1])))
    a = a.reshape(-1, LANES)
    return jnp.pad(a, ((0, rows - a.shape[0]), (0, 0)))


def _rowpack(arrs, own):
    return jnp.concatenate([_as_rows(arrs[nm], ro if own else rf) for nm, rf, ro, _ in ROWPACK], axis=0)


def _small_update(parts, w, m, v, me_arr, name):
    nsm = len(ROWPACK)

    def body(me_ref, p_ref, w_ref, m_ref, v_ref, *rest):
        outs = rest[:4 * nsm]
        gfull, g_s, d_s, m_s, v_s = rest[4 * nsm:]
        me = me_ref[0]
        g = p_ref[0]
        for k in range(1, NDEV):
            g = g + p_ref[k]
        gfull[...] = g
        of, oo = 0, 0
        for nm, rf, ro, _ in ROWPACK:
            if nm == "b_gate":
                g_s[oo:oo + ro, :] = jnp.zeros((ro, LANES), F32)
                for r in range(DEPTH * 3):
                    g_s[oo + r:oo + r + 1, :] = gfull[pl.ds(of + r * NDEV + me, 1), :]
            elif nm == "conv_w":
                for r in range(DEPTH * 3):
                    g_s[oo + r * 8:oo + r * 8 + 8, :] = gfull[pl.ds(pl.multiple_of(of + r * 64 + me * 8, 8), 8), :]
            else:
                g_s[oo:oo + ro, :] = gfull[of:of + rf, :]
            of, oo = of + rf, oo + ro
        delta, m2, v2 = _adamw_math(w_ref[...], g_s[...], m_ref[...], v_ref[...])
        d_s[...] = delta
        m_s[...] = m2
        v_s[...] = v2
        for kind, src in enumerate((g_s, d_s, m_s, v_s)):
            oo = 0
            for idx, (nm, rf, ro, shp) in enumerate(ROWPACK):
                o_ref = outs[kind * nsm + idx]
                if nm in ("rel_bias", "sinks"):
                    o_ref[...] = src[oo:oo + shp[0], 0:shp[1]]
                elif nm == "b_gate":
                    for l in range(DEPTH):
                        o_ref[l] = src[oo + 3 * l:oo + 3 * l + 3, :]
                elif nm == "conv_w":
                    for l in range(DEPTH):
                        for k in range(8):
                            o_ref[l, :, k * LANES:(k + 1) * LANES] = src[pl.ds(oo + 24 * l + k, 3, stride=8), :]
                else:
                    per = shp[1] // LANES
                    for k in range(per):
                        o_ref[:, k * LANES:(k + 1) * LANES] = src[pl.ds(oo + k, DEPTH, stride=per), :]
                oo += ro

    vm = pl.BlockSpec(memory_space=pltpu.VMEM)
    shapes = [jax.ShapeDtypeStruct(shp, F32) for _ in range(4) for _, _, _, shp in ROWPACK]
    outs = pl.pallas_call(
        body, in_specs=[SMEM, vm, vm, vm, vm], out_specs=[vm] * (4 * nsm), out_shape=shapes,
        scratch_shapes=[pltpu.VMEM((ROWS_FULL, LANES), F32)] + [pltpu.VMEM((ROWS_OWN, LANES), F32)] * 4,
        name=name)(me_arr, parts, w, m, v)
    names = [nm for nm, _, _, _ in ROWPACK]
    return [dict(zip(names, outs[kind * nsm:(kind + 1) * nsm])) for kind in range(4)]


def kernel(x, rel_bias, attn_pre_norm, w_in, b_gate, sinks, w_br_a, w_br_b, w_br_c, w_out, attn_post_norm, ffn_pre_norm, w_up, conv_w, conv_b, w_down, ffn_post_norm, loss_target, m_rel_bias, m_attn_pre_norm, m_w_in, m_b_gate, m_sinks, m_w_br_a, m_w_br_b, m_w_br_c, m_w_out, m_attn_post_norm, m_ffn_pre_norm, m_w_up, m_conv_w, m_conv_b, m_w_down, m_ffn_post_norm, v_rel_bias, v_attn_pre_norm, v_w_in, v_b_gate, v_sinks, v_w_br_a, v_w_br_b, v_w_br_c, v_w_out, v_attn_post_norm, v_ffn_pre_norm, v_w_up, v_conv_w, v_conv_b, v_w_down, v_ffn_post_norm):
    P = dict(rel_bias=rel_bias, attn_pre_norm=attn_pre_norm, w_in=w_in, b_gate=b_gate, sinks=sinks, w_br_a=w_br_a,
             w_br_b=w_br_b, w_br_c=w_br_c, w_out=w_out, attn_post_norm=attn_post_norm, ffn_pre_norm=ffn_pre_norm,
             w_up=w_up, conv_w=conv_w, conv_b=conv_b, w_down=w_down, ffn_post_norm=ffn_post_norm)
    M = dict(rel_bias=m_rel_bias, attn_pre_norm=m_attn_pre_norm, w_in=m_w_in, b_gate=m_b_gate, sinks=m_sinks,
             w_br_a=m_w_br_a, w_br_b=m_w_br_b, w_br_c=m_w_br_c, w_out=m_w_out, attn_post_norm=m_attn_post_norm,
             ffn_pre_norm=m_ffn_pre_norm, w_up=m_w_up, conv_w=m_conv_w, conv_b=m_conv_b, w_down=m_w_down,
             ffn_post_norm=m_ffn_post_norm)
    V = dict(rel_bias=v_rel_bias, attn_pre_norm=v_attn_pre_norm, w_in=v_w_in, b_gate=v_b_gate, sinks=v_sinks,
             w_br_a=v_w_br_a, w_br_b=v_w_br_b, w_br_c=v_w_br_c, w_out=v_w_out, attn_post_norm=v_attn_post_norm,
             ffn_pre_norm=v_ffn_pre_norm, w_up=v_w_up, conv_w=v_conv_w, conv_b=v_conv_b, w_down=v_w_down,
             ffn_post_norm=v_ffn_post_norm)
    xi, yi, ci = _coords()
    me = 4 * xi + 2 * yi + ci

    me_arr = me.astype(jnp.int32).reshape(1)

    small_w = _pack([b_gate.reshape(-1), conv_w.reshape(-1)])
    (small_w_all,) = _exchange([small_w], [jax.ShapeDtypeStruct((NDEV,) + small_w.shape, F32)],
                               _whole, _slot, "gather_small_weights")

    lands = [_cast_own(l * NBIG + t, P[nm][l], me_arr, f"gather_own_l{l}_{nm}")
             for l in range(DEPTH) for t, (nm, _, _) in enumerate(BIG)]
    groups = [tuple(l * NBIG + t for t in tids) for l in range(DEPTH) for _, tids in LAYER_GROUPS]
    g_sems, _, g_lands, g_tok = _xchg_start(None, lands, groups, None, _shard_window, small_w_all, "gather_start")
    tok0 = g_tok[0:1, 0:1]

    def gather_waiter(gi, l, gname, tids):
        def wait(after):
            ids = [l * NBIG + t for t in tids]
            _, got = _xchg_wait(g_sems[gi], None, [g_lands[i] for i in ids], ids, after,
                                None, _shard_window, f"gather_wait_l{l}_{gname}")
            out = {}
            for t, arr in zip(tids, got):
                nm = BIG[t][0]
                out[nm] = jnp.transpose(arr, (1, 0, 2)).reshape(D, IN_COLS) if nm == "w_in" else arr
            return out
        return wait

    pending = [{gname: gather_waiter(l * len(LAYER_GROUPS) + k, l, gname, tids)
                for k, (gname, tids) in enumerate(LAYER_GROUPS)} for l in range(DEPTH)]
    nbg = DEPTH * 3 * 128
    ncw = DEPTH * 3 * 1024
    flat_all = small_w_all.reshape(NDEV, -1)
    b_gate_full = jnp.transpose(flat_all[:, :nbg].reshape(NDEV, DEPTH, 3, 128), (1, 2, 0, 3)).reshape(DEPTH, 3, D)
    conv_w_full = jnp.transpose(flat_all[:, nbg:nbg + ncw].reshape(NDEV, DEPTH, 3, 1024), (1, 2, 0, 3)).reshape(DEPTH, 3, 2 * D_FF)

    ws = []
    for l in range(DEPTH):
        ws.append(_Weights(dict(
            b_gate=b_gate_full[l], conv_w=conv_w_full[l].reshape(3, 2, D_FF), conv_b=conv_b[l].reshape(2, D_FF),
            sinks=sinks[l].reshape(1, 8),
            attn_pre_norm=attn_pre_norm[l].reshape(1, D), attn_post_norm=attn_post_norm[l].reshape(1, D),
            ffn_pre_norm=ffn_pre_norm[l].reshape(1, D), ffn_post_norm=ffn_post_norm[l].reshape(1, D)), pending[l]))

    rs = {}

    group_tids = dict(LAYER_GROUPS)

    def start_scatter(l, gname, grads_l):
        tids = group_tids[gname]
        blocks, lands_rs = [], []
        for t in tids:
            nm, ax, ext = BIG[t]
            gfull = grads_l[nm].astype(BF16)
            if nm == "w_in":
                gfull = jnp.transpose(gfull.reshape(D, NDEV, ext), (1, 0, 2))
                shp = (NDEV, D, ext)
            else:
                shp = (NDEV, ext, gfull.shape[1]) if ax == 0 else (NDEV, gfull.shape[0], ext)
            blocks.append(gfull)
            lands_rs.append(lax.empty(shp, BF16))
        local = list(range(len(tids)))
        win = lambda j, ref, k: _shard_window(tids[j], ref, k)
        sems, s_thru, l_thru, tok = _xchg_start(blocks, lands_rs, [tuple(local)], win, _slot, me_arr,
                                                f"scatter_start_l{l}_{gname}")
        rs[(l, gname)] = (sems[0], s_thru, l_thru, win, local)
        return tok[0:1, 0:1]

    loss_local, grad_x, grads, g_rel = _local_step(x[0], loss_target[0], ws, rel_bias, tok0, start_scatter)
    loss = lax.psum(loss_local, ("x", "y", "c"))

    stack = lambda nm: jnp.stack([grads[l][nm] for l in range(DEPTH)], axis=0)
    small_names = [nm for nm, _ in SMALL_REPL] + [nm for nm, _, _ in SMALL_SHARD]
    small_g = {"rel_bias": g_rel}
    for nm in small_names[1:]:
        small_g[nm] = stack(nm)
    small_packed = _rowpack(small_g, own=False)

    out_g, out_d, out_m, out_v = {}, {}, {}, {}
    prev = {nm: None for nm, _, _ in BIG}
    todo = [(l, gname) for l in reversed(range(DEPTH)) for gname in ("ffn", "mix", "in")]
    after, small_parts = grad_x, None
    for l, gname in todo:
        if (l, gname) == todo[-1]:
            (small_parts,) = _exchange([small_packed], [jax.ShapeDtypeStruct((NDEV,) + small_packed.shape, F32)],
                                       _whole, _slot, "gather_small_grads", after=after)
            after = small_parts
        sems, s_thru, l_thru, win, local = rs[(l, gname)]
        owns, parts = _xchg_wait(sems, s_thru, l_thru, local, after, win, _slot, f"scatter_wait_l{l}_{gname}")
        for t, own, prt in zip(group_tids[gname], owns, parts):
            nm = BIG[t][0]
            rows = {"w_in": 256, "w_up": 256, "w_down": 256}.get(nm, P[nm].shape[1])
            prev[nm] = _adamw(t, prt, own, me_arr, P[nm], M[nm], V[nm], l, prev[nm], rows, f"adamw_{nm}_l{l}")
            after = prev[nm][1]
    for nm, _, _ in BIG:
        out_g[nm], out_d[nm], out_m[nm], out_v[nm] = prev[nm]
    sm_g, sm_d, sm_m, sm_v = _small_update(small_parts, _rowpack(P, True), _rowpack(M, True), _rowpack(V, True),
                                           me_arr, "small_update")
    for dst, src in ((out_g, sm_g), (out_d, sm_d), (out_m, sm_m), (out_v, sm_v)):
        dst.update(src)

    order = ["rel_bias", "attn_pre_norm", "w_in", "b_gate", "sinks", "w_br_a", "w_br_b", "w_br_c", "w_out",
             "attn_post_norm", "ffn_pre_norm", "w_up", "conv_w", "conv_b", "w_down", "ffn_post_norm"]
    return (loss, grad_x[None], *[out_g[k] for k in order], *[out_d[k] for k in order],
            *[out_m[k] for k in order], *[out_v[k] for k in order])
```

```python
import functools
import math

import numpy as np
import jax
import jax.numpy as jnp
from jax import lax
from jax.experimental import pallas as pl
from jax.experimental.pallas import tpu as pltpu

F32 = jnp.float32
BF16 = jnp.bfloat16

S = 2048
D = 1024
DEPTH = 2
NDEV = 8
HD = 64
BLK = 128
NB = S // BLK
A_GROUPS = ((128, 1), (512, 4), (2048, 16))
NUM_BUCKETS = 32
MAX_DISTANCE = 2048
N_BIAS_HEADS = 20
D_FF = 4096
IN_COLS = 6912
QKV_COLS = 3840
QKV_SLABS = QKV_COLS // 128
GATE_COLS = 3072
EPS = 1e-6
SCALE = HD ** -0.5
NEG = -1e30
LANES = 128

ADAM_LR = 0.001
ADAM_B1 = 0.9
ADAM_B2 = 0.999
ADAM_EPS = 1e-08
ADAM_WD = 0.01
ADAM_STEP = 10

VMEM_LIMIT = 56 * 1024 * 1024
MESH = pl.DeviceIdType.MESH
ANY = pl.BlockSpec(memory_space=pl.ANY)
SMEM = pl.BlockSpec(memory_space=pltpu.SMEM)


def _cp(*sem):
    return pltpu.CompilerParams(dimension_semantics=sem if sem else None, vmem_limit_bytes=VMEM_LIMIT)


def _dot(a, b, ca, cb):
    return lax.dot_general(a, b, (((ca,), (cb,)), ((), ())), preferred_element_type=F32)


def _mm(a, b, *, grid, a_spec, b_spec, out_shape, out_spec, ca, cb, acc_shape, name,
        a_slab=False, b_slab=False, out_slab=False, alias_out=None, after=None):
    nk = grid[2]

    def body(*refs):
        a_ref, b_ref = refs[0], refs[1]
        o_ref, acc_ref = refs[-2], refs[-1]
        k = pl.program_id(2)

        def load(ref, slab):
            if slab:
                return jnp.concatenate([ref[s] for s in range(ref.shape[0])], axis=1).astype(BF16)
            return ref[...].astype(BF16)

        def write(val):
            if out_slab:
                for s in range(o_ref.shape[0]):
                    o_ref[s] = val[:, s * LANES:(s + 1) * LANES].astype(o_ref.dtype)
            else:
                o_ref[...] = val.astype(o_ref.dtype)

        d = _dot(load(a_ref, a_slab), load(b_ref, b_slab), ca, cb)
        if nk == 1:
            write(d)
        elif direct:
            @pl.when(k == 0)
            def _():
                o_ref[...] = d

            @pl.when(k > 0)
            def _():
                o_ref[...] += d
        else:
            @pl.when(k == 0)
            def _():
                acc_ref[...] = d

            if nk > 2:
                @pl.when((k > 0) & (k < nk - 1))
                def _():
                    acc_ref[...] += d

            @pl.when(k == nk - 1)
            def _():
                write(acc_ref[...] + d)

    direct = (not out_slab) and out_shape.dtype == F32
    if nk == 1 or direct:
        acc_shape = (8, LANES)
    in_specs = [a_spec, b_spec]
    args = [a, b]
    aliases = {}
    if alias_out is not None:
        in_specs.append(ANY)
        args.append(alias_out)
        aliases = {2: 0}
    if after is not None:
        in_specs.append(ANY)
        args.append(after)
    return pl.pallas_call(
        body, grid=grid, in_specs=in_specs, out_specs=out_spec, out_shape=out_shape,
        scratch_shapes=[pltpu.VMEM(acc_shape, F32)], input_output_aliases=aliases,
        compiler_params=_cp("parallel", "parallel", "arbitrary"), name=name)(*args)


def _mm_nn(a, b, out_dtype, tm, tn, tk, name):
    m, kk = a.shape
    n = b.shape[1]
    return _mm(a, b, grid=(m // tm, n // tn, kk // tk),
               a_spec=pl.BlockSpec((tm, tk), lambda i, j, k: (i, k)),
               b_spec=pl.BlockSpec((tk, tn), lambda i, j, k: (k, j)),
               out_shape=jax.ShapeDtypeStruct((m, n), out_dtype),
               out_spec=pl.BlockSpec((tm, tn), lambda i, j, k: (i, j)),
               ca=1, cb=0, acc_shape=(tm, tn), name=name)


def _mm_nt(a, b, out_dtype, tm, tn, tk, name):
    m, kk = a.shape
    n = b.shape[0]
    return _mm(a, b, grid=(m // tm, n // tn, kk // tk),
               a_spec=pl.BlockSpec((tm, tk), lambda i, j, k: (i, k)),
               b_spec=pl.BlockSpec((tn, tk), lambda i, j, k: (j, k)),
               out_shape=jax.ShapeDtypeStruct((m, n), out_dtype),
               out_spec=pl.BlockSpec((tm, tn), lambda i, j, k: (i, j)),
               ca=1, cb=1, acc_shape=(tm, tn), name=name)


def _mm_tn(a, b, out_dtype, tm, tn, tk, name):
    kk, m = a.shape
    n = b.shape[1]
    return _mm(a, b, grid=(m // tm, n // tn, kk // tk),
               a_spec=pl.BlockSpec((tk, tm), lambda i, j, k: (k, i)),
               b_spec=pl.BlockSpec((tk, tn), lambda i, j, k: (k, j)),
               out_shape=jax.ShapeDtypeStruct((m, n), out_dtype),
               out_spec=pl.BlockSpec((tm, tn), lambda i, j, k: (i, j)),
               ca=0, cb=0, acc_shape=(tm, tn), name=name)


ROW_TILE = 256


def _rms(x, g):
    r = lax.rsqrt(jnp.mean(x * x, axis=-1, keepdims=True) + EPS)
    return x * r * g


def _prenorm(x, g, name):
    def body(x_ref, g_ref, o_ref):
        o_ref[...] = _rms(x_ref[...], g_ref[...]).astype(BF16)

    return pl.pallas_call(
        body, grid=(S // ROW_TILE,),
        in_specs=[pl.BlockSpec((ROW_TILE, D), lambda i: (i, 0)), pl.BlockSpec((1, D), lambda i: (0, 0))],
        out_specs=pl.BlockSpec((ROW_TILE, D), lambda i: (i, 0)),
        out_shape=jax.ShapeDtypeStruct((S, D), BF16), compiler_params=_cp("parallel"), name=name)(x, g)


def _postnorm_res(x, f, g_post, g_next, name):
    def body(x_ref, f_ref, gp_ref, gn_ref, xo_ref, ho_ref):
        xn = x_ref[...] + _rms(f_ref[...], gp_ref[...])
        xo_ref[...] = xn
        ho_ref[...] = _rms(xn, gn_ref[...]).astype(BF16)

    row = pl.BlockSpec((ROW_TILE, D), lambda i: (i, 0))
    vec = pl.BlockSpec((1, D), lambda i: (0, 0))
    return pl.pallas_call(
        body, grid=(S // ROW_TILE,), in_specs=[row, row, vec, vec], out_specs=[row, row],
        out_shape=[jax.ShapeDtypeStruct((S, D), F32), jax.ShapeDtypeStruct((S, D), BF16)],
        compiler_params=_cp("parallel"), name=name)(x, f, g_post, g_next)


def _norm_bwd(f, g, dys, res, out_dtype, name):
    ndy = len(dys)
    has_res = res is not None

    def body(*refs):
        f_ref, g_ref = refs[0], refs[1]
        dy_refs = refs[2:2 + ndy]
        res_ref = refs[2 + ndy] if has_res else None
        o_ref, dg_ref = refs[-2], refs[-1]
        fv = f_ref[...]
        dy = dy_refs[0][...].astype(F32)
        for r in dy_refs[1:]:
            dy = dy + r[...].astype(F32)
        r = lax.rsqrt(jnp.mean(fv * fv, axis=-1, keepdims=True) + EPS)
        n = fv * r
        dn = dy * g_ref[...]
        df = r * (dn - n * jnp.mean(dn * n, axis=-1, keepdims=True))
        if has_res:
            df = df + res_ref[...]
        o_ref[...] = df.astype(out_dtype)

        @pl.when(pl.program_id(0) == 0)
        def _():
            dg_ref[...] = jnp.zeros((1, D), F32)

        dg_ref[...] += jnp.sum(dy * n, axis=0, keepdims=True)

    row = pl.BlockSpec((ROW_TILE, D), lambda i: (i, 0))
    vec = pl.BlockSpec((1, D), lambda i: (0, 0))
    in_specs = [row, vec] + [row] * ndy + ([row] if has_res else [])
    args = [f, g] + list(dys) + ([res] if has_res else [])
    return pl.pallas_call(
        body, grid=(S // ROW_TILE,), in_specs=in_specs, out_specs=[row, vec],
        out_shape=[jax.ShapeDtypeStruct((S, D), out_dtype), jax.ShapeDtypeStruct((1, D), F32)],
        compiler_params=_cp("arbitrary"), name=name)(*args)


def _loss_head(y, target, name):
    def body(y_ref, t_ref, dy_ref, l_ref):
        e = y_ref[...] - t_ref[...]
        dy_ref[...] = e * (1.0 / D)

        @pl.when(pl.program_id(0) == 0)
        def _():
            l_ref[...] = jnp.zeros((8, LANES), F32)

        l_ref[...] += jnp.sum(e * e) * (0.5 / D)

    row = pl.BlockSpec((ROW_TILE, D), lambda i: (i, 0))
    return pl.pallas_call(
        body, grid=(S // ROW_TILE,), in_specs=[row, row],
        out_specs=[row, pl.BlockSpec((8, LANES), lambda i: (0, 0))],
        out_shape=[jax.ShapeDtypeStruct((S, D), F32), jax.ShapeDtypeStruct((8, LANES), F32)],
        compiler_params=_cp("arbitrary"), name=name)(y, target)


def _bucket_tiles():
    a = np.arange(BLK)[:, None]
    b = np.arange(2 * BLK)[None, :]
    dist = a + BLK - b
    out = np.zeros((4, 2, BLK, 2 * BLK), np.int32)
    cfg = [(w // d, d) for w, d in A_GROUPS] + [(BLK - 1, 1)]
    for gi, (max_dist, d) in enumerate(cfg):
        band = (dist >= 0) & (dist <= max_dist)
        tok = np.maximum(dist, 0) * d
        nf = np.maximum(tok, 1).astype(np.float32)
        max_exact = NUM_BUCKETS // 2
        large = max_exact + (np.log(nf / np.float32(max_exact)) / np.float32(math.log(MAX_DISTANCE / max_exact))
                             * np.float32(NUM_BUCKETS - max_exact)).astype(np.int32)
        large = np.minimum(large, NUM_BUCKETS - 1)
        bkt = np.where(tok < max_exact, tok, large).astype(np.int32)
        full = np.where(band, bkt, -1)
        out[gi, 1] = full
        out[gi, 0] = np.where(b >= BLK, full, -1)
    return out


def _bias_tiles(rel_bias, buckets, name):
    def body(tab_ref, bkt_ref, o_ref):
        h = pl.program_id(0)
        bkt = bkt_ref[...]
        acc = jnp.zeros(bkt.shape, F32)
        for bb in range(NUM_BUCKETS):
            acc = jnp.where(bkt == bb, tab_ref[bb, h], acc)
        o_ref[...] = jnp.where(bkt < 0, NEG, acc)

    return pl.pallas_call(
        body, grid=(N_BIAS_HEADS,),
        in_specs=[SMEM, pl.BlockSpec((None, 2, BLK, 2 * BLK), lambda h: (jnp.minimum(h // 4, 3), 0, 0, 0))],
        out_specs=pl.BlockSpec((None, 2, BLK, 2 * BLK), lambda h: (h, 0, 0, 0)),
        out_shape=jax.ShapeDtypeStruct((N_BIAS_HEADS, 2, BLK, 2 * BLK), F32),
        compiler_params=_cp("arbitrary"), name=name)(rel_bias, buckets)


def _bias_grad(gs, buckets, name):
    ng = len(gs)

    def body(*refs):
        g_refs = refs[:ng]
        bkt_ref, o_ref = refs[ng], refs[ng + 1]
        h = pl.program_id(0)
        g = g_refs[0][...]
        for r in g_refs[1:]:
            g = g + r[...]
        bkt = bkt_ref[...]
        row = lax.broadcasted_iota(jnp.int32, (NUM_BUCKETS, LANES), 0)
        lane = lax.broadcasted_iota(jnp.int32, (NUM_BUCKETS, LANES), 1)

        @pl.when(h == 0)
        def _():
            o_ref[...] = jnp.zeros((NUM_BUCKETS, LANES), F32)

        acc = o_ref[...]
        for bb in range(NUM_BUCKETS):
            s = jnp.sum(jnp.where(bkt == bb, g, 0.0))
            acc = jnp.where((row == bb) & (lane == h), s, acc)
        o_ref[...] = acc

    g_spec = pl.BlockSpec((None, BLK, 2 * BLK), lambda h: (h, 0, 0))
    return pl.pallas_call(
        body, grid=(N_BIAS_HEADS,),
        in_specs=[g_spec] * ng + [pl.BlockSpec((None, None, BLK, 2 * BLK), lambda h: (jnp.minimum(h // 4, 3), 1, 0, 0))],
        out_specs=pl.BlockSpec((NUM_BUCKETS, LANES), lambda h: (0, 0)),
        out_shape=jax.ShapeDtypeStruct((NUM_BUCKETS, LANES), F32),
        compiler_params=_cp("arbitrary"), name=name)(*gs, buckets)


def _to_class_major(src_ref, dst_refs, d, fn=None):
    ln = S // d
    for r in range(d):
        v = src_ref[pl.ds(r, ln, stride=d), :] if d > 1 else src_ref[...]
        outs = fn(v) if fn is not None else (v,) * len(dst_refs)
        for dst, o in zip(dst_refs, outs):
            dst[pl.ds(r * ln, ln), :] = o.astype(dst.dtype)


def _head_masks(rows):
    lane = lax.broadcasted_iota(jnp.int32, (rows, LANES), 1)
    return lane < HD, lane >= HD


def _split_heads(v):
    m0, m1 = _head_masks(v.shape[0])
    return jnp.where(m0, v, 0.0), jnp.where(m1, v, 0.0)


def _dup_head(v, hi):
    m0, _ = _head_masks(v.shape[0])
    r = pltpu.roll(v, HD, 1)
    return jnp.where(m0, jnp.where(hi, r, v), jnp.where(hi, v, r))


def _block_rows(b, d):
    nbc = NB // d
    i = b % nbc
    r = b // nbc
    has_prev = (i > 0).astype(jnp.int32)
    prev = pl.multiple_of(jnp.maximum(b - 1, 0) * BLK, BLK)
    nat = i * (BLK * d) + r
    return has_prev, prev, nat


def _lane_halves(v0, v1):
    lane = lax.broadcasted_iota(jnp.int32, (v0.shape[0], LANES), 1)
    return jnp.where(lane < HD, v0, v1)


def _band_fwd(proj, bias, sinks, *, d, q0, k0, v0, npairs, bias0, shared_kv, name):
    def body(sink_ref, q_ref, k_ref, v_ref, b_ref, num_ref, st_ref, qz0, qz1, ks, vs):
        p = pl.program_id(0)
        kv = (lambda v: (_dup_head(v, p >= 2),)) if shared_kv else None
        _to_class_major(q_ref, (qz0, qz1), d, lambda v: _split_heads(v * SCALE))
        _to_class_major(k_ref, (ks,), d, kv)
        _to_class_major(v_ref, (vs,), d, kv)
        lane = lax.broadcasted_iota(jnp.int32, (BLK, LANES), 1)

        def blk(b, carry):
            has_prev, prev, nat = _block_rows(b, d)
            cur = pl.multiple_of(b * BLK, BLK)
            k2 = jnp.concatenate([ks[pl.ds(prev, BLK), :], ks[pl.ds(cur, BLK), :]], axis=0)
            v2 = jnp.concatenate([vs[pl.ds(prev, BLK), :], vs[pl.ds(cur, BLK), :]], axis=0)
            nums, ms, ls = [], [], []
            for hh, qz in enumerate((qz0, qz1)):
                z = _dot(qz[pl.ds(cur, BLK), :], k2, 1, 1) + b_ref[hh, has_prev]
                m = jnp.max(z, axis=1, keepdims=True)
                e = jnp.exp(z - m)
                l = jnp.sum(e, axis=1, keepdims=True)
                num = _dot(e.astype(BF16), v2, 1, 0)
                if shared_kv:
                    sink = sink_ref[0, 2 * p + hh]
                    mx = jnp.maximum(m, sink)
                    c = jnp.exp(m - mx)
                    zden = l * c + jnp.exp(sink - mx)
                    num = num * (c / zden)
                    m = mx + jnp.log(zden)
                ls.append(l)
                ms.append(m)
                nums.append(num)
            num_t = jnp.where(lane < HD, nums[0], nums[1])
            if shared_kv:
                st_t = jnp.where(lane < HD, ms[0], ms[1])
            else:
                st_t = jnp.where(lane < 32, ms[0], jnp.where(lane < 64, ls[0], jnp.where(lane < 96, ms[1], ls[1])))
            if d > 1:
                num_ref[pl.ds(nat, BLK, stride=d), :] = num_t
                st_ref[pl.ds(nat, BLK, stride=d), :] = st_t
            else:
                num_ref[pl.ds(cur, BLK), :] = num_t
                st_ref[pl.ds(cur, BLK), :] = st_t
            return carry

        lax.fori_loop(0, NB, blk, 0, unroll=8)

    slab = lambda off, per_pair: pl.BlockSpec((None, S, LANES), (lambda p: (off + p, 0, 0)) if per_pair else (lambda p: (off, 0, 0)))
    out = pl.BlockSpec((None, S, LANES), lambda p: (p, 0, 0))
    return pl.pallas_call(
        body, grid=(npairs,),
        in_specs=[SMEM, slab(q0, True), slab(k0, not shared_kv), slab(v0, not shared_kv),
                  pl.BlockSpec((None, 2, 2, BLK, 2 * BLK), lambda p: (bias0 + p, 0, 0, 0, 0))],
        out_specs=[out, out],
        out_shape=[jax.ShapeDtypeStruct((npairs, S, LANES), F32)] * 2,
        scratch_shapes=[pltpu.VMEM((S, LANES), BF16)] * 4,
        compiler_params=_cp("arbitrary"), name=name)(sinks, proj, proj, proj, bias)


def _combine_a(nums, stats, name):
    rt = 512

    def body(n0, n1, n2, s0, s1, s2, o_ref, l_ref):
        n_refs, s_refs = (n0, n1, n2), (s0, s1, s2)
        outs, lses = [], []
        for hh in range(2):
            ms = [s[:, 64 * hh:64 * hh + 1] for s in s_refs]
            ls = [s[:, 64 * hh + 32:64 * hh + 33] for s in s_refs]
            mx = jnp.maximum(jnp.maximum(ms[0], ms[1]), ms[2])
            cs = [jnp.exp(m - mx) for m in ms]
            z = cs[0] * ls[0] + cs[1] * ls[1] + cs[2] * ls[2]
            acc = cs[0] * n_refs[0][:, hh * HD:(hh + 1) * HD]
            acc = acc + cs[1] * n_refs[1][:, hh * HD:(hh + 1) * HD]
            acc = acc + cs[2] * n_refs[2][:, hh * HD:(hh + 1) * HD]
            outs.append(acc / z)
            lses.append(mx + jnp.log(z))
        o_ref[...] = jnp.concatenate(outs, axis=1)
        l_ref[...] = _lane_halves(lses[0], lses[1])

    spec = pl.BlockSpec((None, rt, LANES), lambda p, i: (p, i, 0))
    return pl.pallas_call(
        body, grid=(2, S // rt), in_specs=[spec] * 6, out_specs=[spec, spec],
        out_shape=[jax.ShapeDtypeStruct((2, S, LANES), F32)] * 2,
        compiler_params=_cp("parallel", "parallel"), name=name)(*nums, *stats)


def _band_bwd(proj, bias, o, do, lse, sinks, *, d, q0, k0, v0, npairs, bias0, shared_kv, name):
    nkv = 1 if shared_kv else npairs

    def body(sink_ref, q_ref, k_ref, v_ref, b_ref, o_ref, do_ref, lse_ref,
             dq_ref, dk_ref, dv_ref, g_ref, ds_ref,
             qz0, qz1, ks, vs, doz0, doz1, ls0, ls1, dls0, dls1, stage, dq_nat, dk_cm, dv_cm, kv_nat, dk_acc, dv_acc):
        p = pl.program_id(0)
        m0, m1 = _head_masks(S)
        prod = do_ref[...] * o_ref[...]
        dl0 = jnp.sum(jnp.where(m0, prod, 0.0), axis=1, keepdims=True)
        dl1 = jnp.sum(jnp.where(m1, prod, 0.0), axis=1, keepdims=True)
        if shared_kv:
            row8 = lax.broadcasted_iota(jnp.int32, (8, LANES), 0)
            lane8 = lax.broadcasted_iota(jnp.int32, (8, LANES), 1)
            t = jnp.zeros((8, LANES), F32)
            lv = lse_ref[...]
            for hh in range(2):
                sink = sink_ref[0, 2 * p + hh]
                ps = jnp.exp(sink - lv[:, 64 * hh:64 * hh + 1])
                dsink = -jnp.sum(ps * (dl0 if hh == 0 else dl1))
                t = jnp.where((row8 == 0) & (lane8 == hh), dsink, t)
            ds_ref[...] = t
        else:
            ds_ref[...] = jnp.zeros((8, LANES), F32)
        kv = (lambda v: (_dup_head(v, p >= 2),)) if shared_kv else None
        _to_class_major(q_ref, (qz0, qz1), d, lambda v: _split_heads(v * SCALE))
        _to_class_major(k_ref, (ks,), d, kv)
        _to_class_major(v_ref, (vs,), d, kv)
        _to_class_major(do_ref, (doz0, doz1), d, _split_heads)
        def spread(v):
            a0, a1 = _head_masks(v.shape[0])
            r = pltpu.roll(v, HD, 1)
            return jnp.where(a0, v, r), jnp.where(a1, v, r)

        _to_class_major(lse_ref, (ls0, ls1), d, spread)
        stage[...] = jnp.where(m0, dl0, dl1)
        _to_class_major(stage, (dls0, dls1), d, spread)

        dk_cm[...] = jnp.zeros((S, LANES), F32)
        dv_cm[...] = jnp.zeros((S, LANES), F32)
        g_ref[...] = jnp.zeros((2, BLK, 2 * BLK), F32)
        lane = lax.broadcasted_iota(jnp.int32, (BLK, LANES), 1)

        def blk(b, carry):
            has_prev, prev, nat = _block_rows(b, d)
            cur = pl.multiple_of(b * BLK, BLK)
            k2 = jnp.concatenate([ks[pl.ds(prev, BLK), :], ks[pl.ds(cur, BLK), :]], axis=0)
            v2 = jnp.concatenate([vs[pl.ds(prev, BLK), :], vs[pl.ds(cur, BLK), :]], axis=0)
            dqs, dks, dvs = [], [], []
            for hh, (qz, doz, lsr, dlr) in enumerate(((qz0, doz0, ls0, dls0), (qz1, doz1, ls1, dls1))):
                qb = qz[pl.ds(cur, BLK), :]
                dob = doz[pl.ds(cur, BLK), :]
                lb = lsr[pl.ds(cur, BLK), :]
                dlb = dlr[pl.ds(cur, BLK), :]
                z = _dot(qb, k2, 1, 1) + b_ref[hh, has_prev]
                pr = jnp.exp(z - jnp.concatenate([lb, lb], axis=1))
                dp = _dot(dob, v2, 1, 1)
                dz = pr * (dp - jnp.concatenate([dlb, dlb], axis=1))
                g_ref[hh] += dz
                dzb = dz.astype(BF16)
                dqs.append(_dot(dzb, k2, 1, 0))
                dks.append(_dot(dzb, qb, 0, 0))
                dvs.append(_dot(pr.astype(BF16), dob, 0, 0))
            dq_t = jnp.where(lane < HD, dqs[0], dqs[1]) * SCALE
            dk_t = dks[0] + dks[1]
            dv_t = dvs[0] + dvs[1]
            dk_cm[pl.ds(prev, BLK), :] += dk_t[:BLK]
            dk_cm[pl.ds(cur, BLK), :] += dk_t[BLK:]
            dv_cm[pl.ds(prev, BLK), :] += dv_t[:BLK]
            dv_cm[pl.ds(cur, BLK), :] += dv_t[BLK:]
            if d > 1:
                dq_nat[pl.ds(nat, BLK, stride=d), :] = dq_t
            else:
                dq_nat[pl.ds(cur, BLK), :] = dq_t
            return carry

        lax.fori_loop(0, NB, blk, 0, unroll=8)
        dq_ref[...] = dq_nat[...].astype(BF16)

        def from_class_major(src, dst_ref):
            if d == 1:
                dst_ref[...] = src[...].astype(BF16)
            else:
                ln = S // d
                for r in range(d):
                    kv_nat[pl.ds(r, ln, stride=d), :] = src[pl.ds(r * ln, ln), :]
                dst_ref[...] = kv_nat[...].astype(BF16)

        if not shared_kv:
            from_class_major(dk_cm, dk_ref)
            from_class_major(dv_cm, dv_ref)
        else:
            @pl.when(p == 0)
            def _():
                dk_acc[...] = jnp.zeros((S, LANES), F32)
                dv_acc[...] = jnp.zeros((S, LANES), F32)

            mine = m1 == (p >= 2)
            for cm, acc in ((dk_cm, dk_acc), (dv_cm, dv_acc)):
                val = cm[...]
                acc[...] += jnp.where(mine, val + pltpu.roll(val, HD, 1), 0.0)

            @pl.when(p == npairs - 1)
            def _():
                from_class_major(dk_acc, dk_ref)
                from_class_major(dv_acc, dv_ref)

    slab = lambda off, per_pair: pl.BlockSpec((None, S, LANES), (lambda p: (off + p, 0, 0)) if per_pair else (lambda p: (off, 0, 0)))
    pair = pl.BlockSpec((None, S, LANES), lambda p: (p, 0, 0))
    kv_out = pair if not shared_kv else pl.BlockSpec((None, S, LANES), lambda p: (0, 0, 0))
    return pl.pallas_call(
        body, grid=(npairs,),
        in_specs=[SMEM, slab(q0, True), slab(k0, not shared_kv), slab(v0, not shared_kv),
                  pl.BlockSpec((None, 2, 2, BLK, 2 * BLK), lambda p: (bias0 + p, 0, 0, 0, 0)),
                  pair, pair, pair],
        out_specs=[pair, kv_out, kv_out,
                   pl.BlockSpec((None, 2, BLK, 2 * BLK), lambda p: (p, 0, 0, 0)),
                   pl.BlockSpec((None, 8, LANES), lambda p: (p, 0, 0))],
        out_shape=[jax.ShapeDtypeStruct((npairs, S, LANES), BF16),
                   jax.ShapeDtypeStruct((nkv, S, LANES), BF16),
                   jax.ShapeDtypeStruct((nkv, S, LANES), BF16),
                   jax.ShapeDtypeStruct((npairs, 2, BLK, 2 * BLK), F32),
                   jax.ShapeDtypeStruct((npairs, 8, LANES), F32)],
        scratch_shapes=[pltpu.VMEM((S, LANES), BF16)] * 6 + [pltpu.VMEM((S, LANES), F32)] * 11,
        compiler_params=_cp("arbitrary"), name=name)(sinks, proj, proj, proj, bias, o, do, lse)


KC = 512
NSUB = KC // BLK


def _split2(x):
    hi = x.astype(BF16)
    lo = (x - hi.astype(F32)).astype(BF16)
    return hi, lo


def _tri_ones(cmp):
    jj = lax.broadcasted_iota(jnp.int32, (2 * BLK, BLK), 0) % BLK
    ss = lax.broadcasted_iota(jnp.int32, (2 * BLK, BLK), 1)
    return jnp.concatenate([cmp(jj, ss).astype(BF16), jnp.ones((2 * BLK, BLK), BF16)], axis=1)


def _sub_sums(x, tri1):
    st = jnp.concatenate([x[:, s * BLK:(s + 1) * BLK] for s in range(NSUB)], axis=0)
    hi, lo = _split2(st)
    r = _dot(jnp.concatenate([hi, lo], axis=1), tri1, 1, 0)
    return ([r[s * BLK:(s + 1) * BLK, :BLK] for s in range(NSUB)], [r[s * BLK:(s + 1) * BLK, BLK:] for s in range(NSUB)])


def _log_sig_pair(z):
    lb = jnp.minimum(z, 0.0) - jnp.log1p(jnp.exp(-jnp.abs(z)))
    return lb, lb - z


QGROUPS = NB // NSUB


def _stick_fwd(proj, *, q0, k0, v0, name):
    def body(q_ref, k_ref, v_ref, o_ref, t_ref, qs, ks, vs):
        qs[...] = (q_ref[...] * SCALE).astype(BF16)
        ks[...] = k_ref[...].astype(BF16)
        vs[...] = v_ref[...].astype(BF16)
        tri1 = _tri_ones(lambda j, s: j > s)
        col = lax.broadcasted_iota(jnp.int32, (BLK, KC), 1)
        rowi = lax.broadcasted_iota(jnp.int32, (BLK, KC), 0)

        for qg in range(QGROUPS):
            def qblock(ii, carry0, qg=qg):
                t0 = pl.multiple_of((qg * NSUB + ii) * BLK, BLK)
                qb = qs[pl.ds(t0, BLK), :]
                accs = [jnp.zeros((BLK, HD), F32)] * 2
                runs = [jnp.zeros((BLK, BLK), F32)] * 2
                for c in reversed(range(qg + 1)):
                    s0 = c * KC
                    diag = c == qg
                    before = (s0 + col) < (t0 + rowi) if diag else None
                    for hh in range(2):
                        kh = ks[s0:s0 + KC, hh * HD:(hh + 1) * HD]
                        vh = vs[s0:s0 + KC, hh * HD:(hh + 1) * HD]
                        lb, lk = _log_sig_pair(_dot(qb[:, hh * HD:(hh + 1) * HD], kh, 1, 1))
                        if diag:
                            lk = jnp.where(before, lk, 0.0)
                        suf, tot = _sub_sums(lk, tri1)
                        ws, run = [], runs[hh]
                        for s in reversed(range(NSUB)):
                            ws.append(jnp.exp(lb[:, s * BLK:(s + 1) * BLK] + suf[s] + run))
                            run = run + tot[s]
                        w = jnp.concatenate(ws[::-1], axis=1)
                        if diag:
                            w = jnp.where(before, w, 0.0)
                        accs[hh] = accs[hh] + _dot(w.astype(BF16), vh, 1, 0)
                        runs[hh] = run
                o_ref[pl.ds(t0, BLK), :] = jnp.concatenate(accs, axis=1)
                t_ref[pl.ds(t0, BLK), :] = _lane_halves(runs[0], runs[1])
                return carry0

            lax.fori_loop(0, NSUB, qblock, 0)

    slab = lambda off: pl.BlockSpec((None, S, LANES), lambda p: (off + p, 0, 0))
    out = pl.BlockSpec((None, S, LANES), lambda p: (p, 0, 0))
    return pl.pallas_call(
        body, grid=(2,), in_specs=[slab(q0), slab(k0), slab(v0)], out_specs=[out, out],
        out_shape=[jax.ShapeDtypeStruct((2, S, LANES), F32)] * 2,
        scratch_shapes=[pltpu.VMEM((S, LANES), BF16)] * 3,
        compiler_params=_cp("arbitrary"), name=name)(proj, proj, proj)


def _stick_bwd(proj, do, tot, *, q0, k0, v0, name):
    def body(q_ref, k_ref, v_ref, do_ref, t_ref, dq_ref, dk_ref, dv_ref, qs, ks, vs, dos, dk_acc, dv_acc):
        qs[...] = (q_ref[...] * SCALE).astype(BF16)
        ks[...] = k_ref[...].astype(BF16)
        vs[...] = v_ref[...].astype(BF16)
        dos[...] = do_ref[...].astype(BF16)
        dk_acc[...] = jnp.zeros((2, S, HD), F32)
        dv_acc[...] = jnp.zeros((2, S, HD), F32)
        tri_inc = _tri_ones(lambda j, s: j <= s)
        tri_exc = _tri_ones(lambda j, s: j < s)
        col = lax.broadcasted_iota(jnp.int32, (BLK, KC), 1)
        rowi = lax.broadcasted_iota(jnp.int32, (BLK, KC), 0)

        for qg in range(QGROUPS):
            def qblock(ii, carry0, qg=qg):
                t0 = pl.multiple_of((qg * NSUB + ii) * BLK, BLK)
                qb = qs[pl.ds(t0, BLK), :]
                dob = dos[pl.ds(t0, BLK), :]
                tb = t_ref[pl.ds(t0, BLK), :]
                dqs = [jnp.zeros((BLK, HD), F32)] * 2
                pruns = [jnp.zeros((BLK, BLK), F32)] * 2
                eruns = [jnp.zeros((BLK, BLK), F32)] * 2
                for c in range(qg + 1):
                    s0 = c * KC
                    diag = c == qg
                    before = (s0 + col) < (t0 + rowi) if diag else None
                    for hh in range(2):
                        qh = qb[:, hh * HD:(hh + 1) * HD]
                        doh = dob[:, hh * HD:(hh + 1) * HD]
                        tt = tb[:, 64 * hh:64 * hh + 1]
                        kh = ks[s0:s0 + KC, hh * HD:(hh + 1) * HD]
                        vh = vs[s0:s0 + KC, hh * HD:(hh + 1) * HD]
                        lb, lk = _log_sig_pair(_dot(qh, kh, 1, 1))
                        if diag:
                            lk = jnp.where(before, lk, 0.0)
                        pin, ptot = _sub_sums(lk, tri_inc)
                        ws, prun = [], pruns[hh]
                        for s in range(NSUB):
                            ws.append(jnp.exp(lb[:, s * BLK:(s + 1) * BLK] + (tt - (pin[s] + prun))))
                            prun = prun + ptot[s]
                        w = jnp.concatenate(ws, axis=1)
                        if diag:
                            w = jnp.where(before, w, 0.0)
                        e = w * _dot(doh, vh, 1, 1)
                        pex, etot = _sub_sums(e, tri_exc)
                        cs, erun = [], eruns[hh]
                        for s in range(NSUB):
                            cs.append(pex[s] + erun)
                            erun = erun + etot[s]
                        sig = jnp.exp(lb)
                        dz = e * (1.0 - sig) - jnp.concatenate(cs, axis=1) * sig
                        if diag:
                            dz = jnp.where(before, dz, 0.0)
                        dz = dz.astype(BF16)
                        dqs[hh] = dqs[hh] + _dot(dz, kh, 1, 0)
                        dk_acc[hh, s0:s0 + KC, :] += _dot(dz, qh, 0, 0)
                        dv_acc[hh, s0:s0 + KC, :] += _dot(w.astype(BF16), doh, 0, 0)
                        pruns[hh], eruns[hh] = prun, erun
                dq_ref[pl.ds(t0, BLK), :] = (jnp.concatenate(dqs, axis=1) * SCALE).astype(BF16)
                return carry0

            lax.fori_loop(0, NSUB, qblock, 0)
        dk_ref[...] = jnp.concatenate([dk_acc[0], dk_acc[1]], axis=1).astype(BF16)
        dv_ref[...] = jnp.concatenate([dv_acc[0], dv_acc[1]], axis=1).astype(BF16)

    slab = lambda off: pl.BlockSpec((None, S, LANES), lambda p: (off + p, 0, 0))
    pair = pl.BlockSpec((None, S, LANES), lambda p: (p, 0, 0))
    return pl.pallas_call(
        body, grid=(2,), in_specs=[slab(q0), slab(k0), slab(v0), pair, pair], out_specs=[pair] * 3,
        out_shape=[jax.ShapeDtypeStruct((2, S, LANES), BF16)] * 3,
        scratch_shapes=[pltpu.VMEM((S, LANES), BF16)] * 4 + [pltpu.VMEM((2, S, HD), F32)] * 2,
        compiler_params=_cp("arbitrary"), name=name)(proj, proj, proj, do, tot)


def _cat_slabs(ref):
    return jnp.concatenate([ref[s] for s in range(ref.shape[0])], axis=1)


def _merge_fwd(o_a, o_b, o_c, gates, b_gate, wa, wb, wc, w_out, name):
    tm = ROW_TILE

    def body(oa_ref, ob_ref, oc_ref, g_ref, bg_ref, wa_ref, wb_ref, wc_ref, wo_ref, mg_ref, mo_ref):
        acc = jnp.zeros((tm, D), F32)
        for i, (o_ref, w_ref) in enumerate(((oa_ref, wa_ref), (ob_ref, wb_ref), (oc_ref, wc_ref))):
            pr = _dot(_cat_slabs(o_ref).astype(BF16), w_ref[...], 1, 0)
            sg = jax.nn.sigmoid(g_ref[:, i * D:(i + 1) * D] + bg_ref[i:i + 1, :])
            acc = acc + sg * pr
        mg = acc.astype(BF16)
        mg_ref[...] = mg
        mo_ref[...] = _dot(mg, wo_ref[...], 1, 0)

    slabs = lambda n: pl.BlockSpec((n, tm, LANES), lambda i: (0, i, 0))
    full = lambda r, c: pl.BlockSpec((r, c), lambda i: (0, 0))
    row = pl.BlockSpec((tm, D), lambda i: (i, 0))
    return pl.pallas_call(
        body, grid=(S // tm,),
        in_specs=[slabs(2), slabs(4), slabs(2), pl.BlockSpec((tm, GATE_COLS), lambda i: (i, 0)), full(3, D),
                  full(256, D), full(512, D), full(256, D), full(D, D)],
        out_specs=[row, row],
        out_shape=[jax.ShapeDtypeStruct((S, D), BF16), jax.ShapeDtypeStruct((S, D), F32)],
        compiler_params=_cp("parallel"), name=name)(o_a, o_b, o_c, gates, b_gate, wa, wb, wc, w_out)


def _merge_bwd(d_mo, o_a, o_b, o_c, gates, b_gate, wa, wb, wc, w_out, name):
    tm = ROW_TILE

    def body(dmo_ref, oa_ref, ob_ref, oc_ref, g_ref, bg_ref, wa_ref, wb_ref, wc_ref, wo_ref,
             doa_ref, dob_ref, doc_ref, dg_ref, dwa_ref, dwb_ref, dwc_ref, dbg_ref):
        @pl.when(pl.program_id(0) == 0)
        def _():
            dwa_ref[...] = jnp.zeros(dwa_ref.shape, F32)
            dwb_ref[...] = jnp.zeros(dwb_ref.shape, F32)
            dwc_ref[...] = jnp.zeros(dwc_ref.shape, F32)
            dbg_ref[...] = jnp.zeros(dbg_ref.shape, F32)

        dmg = _dot(dmo_ref[...], wo_ref[...], 1, 1)
        trip = ((oa_ref, wa_ref, doa_ref, dwa_ref), (ob_ref, wb_ref, dob_ref, dwb_ref), (oc_ref, wc_ref, doc_ref, dwc_ref))
        for i, (o_ref, w_ref, do_ref, dw_ref) in enumerate(trip):
            ob = _cat_slabs(o_ref).astype(BF16)
            pr = _dot(ob, w_ref[...], 1, 0)
            sg = jax.nn.sigmoid(g_ref[:, i * D:(i + 1) * D] + bg_ref[i:i + 1, :])
            dgate = dmg * pr * sg * (1.0 - sg)
            dg_ref[:, i * D:(i + 1) * D] = dgate.astype(BF16)
            dbg_ref[i:i + 1, :] += jnp.sum(dgate, axis=0, keepdims=True)
            dpr = (dmg * sg).astype(BF16)
            do = _dot(dpr, w_ref[...], 1, 1)
            for s in range(do_ref.shape[0]):
                do_ref[s] = do[:, s * LANES:(s + 1) * LANES]
            dw_ref[...] += _dot(ob, dpr, 0, 0)

    slabs = lambda n: pl.BlockSpec((n, tm, LANES), lambda i: (0, i, 0))
    full = lambda r, c: pl.BlockSpec((r, c), lambda i: (0, 0))
    row = pl.BlockSpec((tm, D), lambda i: (i, 0))
    return pl.pallas_call(
        body, grid=(S // tm,),
        in_specs=[row, slabs(2), slabs(4), slabs(2), pl.BlockSpec((tm, GATE_COLS), lambda i: (i, 0)), full(3, D),
                  full(256, D), full(512, D), full(256, D), full(D, D)],
        out_specs=[slabs(2), slabs(4), slabs(2), pl.BlockSpec((tm, GATE_COLS), lambda i: (i, 0)),
                   full(256, D), full(512, D), full(256, D), full(3, D)],
        out_shape=[jax.ShapeDtypeStruct((2, S, LANES), F32), jax.ShapeDtypeStruct((4, S, LANES), F32),
                   jax.ShapeDtypeStruct((2, S, LANES), F32), jax.ShapeDtypeStruct((S, GATE_COLS), BF16),
                   jax.ShapeDtypeStruct((256, D), F32), jax.ShapeDtypeStruct((512, D), F32),
                   jax.ShapeDtypeStruct((256, D), F32), jax.ShapeDtypeStruct((3, D), F32)],
        compiler_params=_cp("arbitrary"), name=name)(d_mo, o_a, o_b, o_c, gates, b_gate, wa, wb, wc, w_out)


FC = 256
GELU_K = math.sqrt(2.0 / math.pi)
GELU_C = 0.044715


RC = 64
NRC = S // RC


def _down(tail, cur, n):
    row = lax.broadcasted_iota(jnp.int32, tail.shape, 0)
    rolled = pltpu.roll(cur, n, 0)
    first = jnp.where(row < n, pltpu.roll(tail, n, 0), rolled[0:8])
    return jnp.concatenate([first, rolled[8:]], axis=0)


def _up(cur, head, n):
    row = lax.broadcasted_iota(jnp.int32, head.shape, 0)
    rolled = pltpu.roll(cur, RC - n, 0)
    last = jnp.where(row >= 8 - n, pltpu.roll(head, 8 - n, 0), rolled[RC - 8:])
    return jnp.concatenate([rolled[:RC - 8], last], axis=0)


def _conv_chunk(load, j, w_ref, b_ref, half):
    r0 = pl.multiple_of(j * RC, RC)
    cur = load(r0, RC)
    tail = jnp.where(j > 0, load(pl.multiple_of(jnp.maximum(r0 - 8, 0), 8), 8), 0.0)
    d1 = _down(tail, cur, 1)
    d2 = _down(tail, cur, 2)
    y = w_ref[0:1, half, :] * d2 + w_ref[1:2, half, :] * d1 + w_ref[2:3, half, :] * cur + b_ref[half:half + 1, :]
    return y, cur, d1, d2


def _chunk(j):
    return pl.ds(pl.multiple_of(j * RC, RC), RC)


def _fold8(x):
    return jnp.sum(x.reshape(RC // 8, 8, x.shape[-1]), axis=0)


def _ffn_act(u, conv_w, conv_b, name):
    def body(u_ref, w_ref, b_ref, a_ref):
        def step(j, carry):
            yg = _conv_chunk(lambda r, n: u_ref[0, pl.ds(r, n), :], j, w_ref, b_ref, 0)[0]
            yv = _conv_chunk(lambda r, n: u_ref[1, pl.ds(r, n), :], j, w_ref, b_ref, 1)[0]
            th = jnp.tanh(GELU_K * (yg + GELU_C * yg * yg * yg))
            a_ref[_chunk(j), :] = (0.5 * yg * (1.0 + th) * yv).astype(BF16)
            return carry

        lax.fori_loop(0, NRC, step, 0)

    return pl.pallas_call(
        body, grid=(D_FF // FC,),
        in_specs=[pl.BlockSpec((2, S, FC), lambda j: (0, 0, j)), pl.BlockSpec((3, 2, FC), lambda j: (0, 0, j)),
                  pl.BlockSpec((2, FC), lambda j: (0, j))],
        out_specs=pl.BlockSpec((S, FC), lambda j: (0, j)),
        out_shape=jax.ShapeDtypeStruct((S, D_FF), BF16),
        compiler_params=_cp("parallel"), name=name)(u, conv_w, conv_b)


def _ffn_act_bwd(u, d_a, conv_w, conv_b, name):
    def body(u_ref, da_ref, w_ref, b_ref, du_ref, dw_ref, db_ref, dy_s):
        def first(j, acc):
            yg, ug, ug1, ug2 = _conv_chunk(lambda r, n: u_ref[0, pl.ds(r, n), :], j, w_ref, b_ref, 0)
            yv, uv, uv1, uv2 = _conv_chunk(lambda r, n: u_ref[1, pl.ds(r, n), :], j, w_ref, b_ref, 1)
            th = jnp.tanh(GELU_K * (yg + GELU_C * yg * yg * yg))
            gelu = 0.5 * yg * (1.0 + th)
            dgelu = 0.5 * (1.0 + th) + 0.5 * yg * (1.0 - th * th) * GELU_K * (1.0 + 3.0 * GELU_C * yg * yg)
            da = da_ref[_chunk(j), :]
            dyg = da * yv * dgelu
            dyv = da * gelu
            dy_s[0, _chunk(j), :] = dyg
            dy_s[1, _chunk(j), :] = dyv
            new = (_fold8(dyg * ug2), _fold8(dyg * ug1), _fold8(dyg * ug), _fold8(dyg),
                   _fold8(dyv * uv2), _fold8(dyv * uv1), _fold8(dyv * uv), _fold8(dyv))
            return tuple(a + n for a, n in zip(acc, new))

        acc = lax.fori_loop(0, NRC, first, tuple(jnp.zeros((8, FC), F32) for _ in range(8)))
        for half in range(2):
            for k in range(3):
                dw_ref[k:k + 1, half, :] = jnp.sum(acc[4 * half + k], axis=0, keepdims=True)
            db_ref[half:half + 1, :] = jnp.sum(acc[4 * half + 3], axis=0, keepdims=True)

        def second(j, carry):
            for half in range(2):
                cur = dy_s[half, _chunk(j), :]
                h0 = pl.multiple_of(jnp.minimum((j + 1) * RC, S - 8), 8)
                head = jnp.where(j < NRC - 1, dy_s[half, pl.ds(h0, 8), :], 0.0)
                du = (w_ref[2:3, half, :] * cur + w_ref[1:2, half, :] * _up(cur, head, 1)
                      + w_ref[0:1, half, :] * _up(cur, head, 2))
                du_ref[half, _chunk(j), :] = du.astype(BF16)
            return carry

        lax.fori_loop(0, NRC, second, 0)

    return pl.pallas_call(
        body, grid=(D_FF // FC,),
        in_specs=[pl.BlockSpec((2, S, FC), lambda j: (0, 0, j)), pl.BlockSpec((S, FC), lambda j: (0, j)),
                  pl.BlockSpec((3, 2, FC), lambda j: (0, 0, j)), pl.BlockSpec((2, FC), lambda j: (0, j))],
        out_specs=[pl.BlockSpec((2, S, FC), lambda j: (0, 0, j)), pl.BlockSpec((3, 2, FC), lambda j: (0, 0, j)),
                   pl.BlockSpec((2, FC), lambda j: (0, j))],
        out_shape=[jax.ShapeDtypeStruct((2, S, D_FF), BF16), jax.ShapeDtypeStruct((3, 2, D_FF), F32),
                   jax.ShapeDtypeStruct((2, D_FF), F32)],
        scratch_shapes=[pltpu.VMEM((2, S, FC), F32)],
        compiler_params=_cp("parallel"), name=name)(u, d_a, conv_w, conv_b)


def _layer_fwd(x, h1, w, bias, lname):
    n = lambda s: f"{lname}_{s}"
    w.need("in", h1)
    tn = 768
    proj = _mm(h1, w["w_in"], grid=(S // 1024, QKV_COLS // tn, 1),
               a_spec=pl.BlockSpec((1024, D), lambda i, j, k: (i, 0)),
               b_spec=pl.BlockSpec((D, tn), lambda i, j, k: (0, j)),
               out_shape=jax.ShapeDtypeStruct((QKV_SLABS, S, LANES), F32),
               out_spec=pl.BlockSpec((tn // LANES, 1024, LANES), lambda i, j, k: (j, i, 0)),
               ca=1, cb=0, acc_shape=(1024, tn), out_slab=True, name=n("proj_qkv"))
    gates = _mm(h1, w["w_in"], grid=(S // 1024, GATE_COLS // tn, 1),
                a_spec=pl.BlockSpec((1024, D), lambda i, j, k: (i, 0)),
                b_spec=pl.BlockSpec((D, tn), lambda i, j, k: (0, j + QKV_COLS // tn)),
                out_shape=jax.ShapeDtypeStruct((S, GATE_COLS), F32),
                out_spec=pl.BlockSpec((1024, tn), lambda i, j, k: (i, j)),
                ca=1, cb=0, acc_shape=(1024, tn), name=n("proj_gate"))
    nums, stats = [], []
    for g, (_, d) in enumerate(A_GROUPS):
        nm, st = _band_fwd(proj, bias, w["sinks"], d=d, q0=2 * g, k0=6 + 2 * g, v0=12 + 2 * g, npairs=2, bias0=2 * g,
                           shared_kv=False, name=n(f"attn_a{g}_fwd"))
        nums.append(nm)
        stats.append(st)
    o_a, lse_a = _combine_a(nums, stats, n("attn_a_combine"))
    o_b, lse_b = _band_fwd(proj, bias, w["sinks"], d=1, q0=18, k0=22, v0=23, npairs=4, bias0=6, shared_kv=True,
                           name=n("attn_b_fwd"))
    o_c, tot_c = _stick_fwd(proj, q0=24, k0=26, v0=28, name=n("attn_c_fwd"))
    w.need("mix", tot_c)
    merged, mo = _merge_fwd(o_a, o_b, o_c, gates, w["b_gate"], w["w_br_a"], w["w_br_b"], w["w_br_c"], w["w_out"], n("merge_fwd"))
    x2, h2 = _postnorm_res(x, mo, w["attn_post_norm"], w["ffn_pre_norm"], n("attn_post"))
    w.need("ffn", h2)
    u = _mm(h2, w["w_up"], grid=(S // 1024, 2 * D_FF // 1024, 1),
            a_spec=pl.BlockSpec((1024, D), lambda i, j, k: (i, 0)),
            b_spec=pl.BlockSpec((D, 1024), lambda i, j, k: (0, j)),
            out_shape=jax.ShapeDtypeStruct((2, S, D_FF), F32),
            out_spec=pl.BlockSpec((None, 1024, 1024), lambda i, j, k: (j // 4, i, j % 4)),
            ca=1, cb=0, acc_shape=(1024, 1024), name=n("ffn_up"))
    a = _ffn_act(u, w["conv_w"], w["conv_b"], n("ffn_act"))
    fo = _mm_nn(a, w["w_down"], F32, 1024, 1024, 2048, n("ffn_down"))
    saved = dict(x=x, h1=h1, proj=proj, gates=gates, o_a=o_a, lse_a=lse_a, o_b=o_b, lse_b=lse_b, o_c=o_c, tot_c=tot_c,
                 merged=merged, mo=mo, x2=x2, h2=h2, u=u, a=a, fo=fo)
    return saved


def _layer_bwd(dx3, sv, w, bias, lname, tok=None, on_part=None):
    n = lambda s: f"{lname}_{s}"
    g = {}

    def part(group, vec):
        t = on_part(group, g) if on_part is not None else None
        return vec if t is None else vec + t

    gain = w["ffn_post_norm"] if tok is None else w["ffn_post_norm"] + tok
    d_fo, g["ffn_post_norm"] = _norm_bwd(sv["fo"], gain, [dx3], None, BF16, n("ffn_post_bwd"))
    d_a = _mm_nt(d_fo, w["w_down"], F32, 1024, 1024, 1024, n("ffn_down_bwd_x"))
    g["w_down"] = _mm_tn(sv["a"], d_fo, BF16, 1024, 1024, S, n("ffn_down_bwd_w"))
    d_u, dcw, dcb = _ffn_act_bwd(sv["u"], d_a, w["conv_w"], w["conv_b"], n("ffn_act_bwd"))
    g["conv_w"] = dcw.reshape(3, 2 * D_FF)
    g["conv_b"] = dcb.reshape(1, 2 * D_FF)
    g["w_up"] = _mm(sv["h2"], d_u, grid=(1, 2 * D_FF // 1024, 1),
                    a_spec=pl.BlockSpec((S, D), lambda i, j, k: (k, 0)),
                    b_spec=pl.BlockSpec((None, S, 1024), lambda i, j, k: (j // 4, k, j % 4)),
                    out_shape=jax.ShapeDtypeStruct((D, 2 * D_FF), BF16),
                    out_spec=pl.BlockSpec((D, 1024), lambda i, j, k: (0, j)),
                    ca=0, cb=0, acc_shape=(D, 1024), name=n("ffn_up_bwd_w"))
    tok_ffn = on_part("ffn", g) if on_part is not None else None
    d_h2 = _mm(d_u, w["w_up"], grid=(S // 1024, 1, 2 * D_FF // 2048),
               a_spec=pl.BlockSpec((None, 1024, 2048), lambda i, j, k: (k // 2, i, k % 2)),
               b_spec=pl.BlockSpec((D, 2048), lambda i, j, k: (0, k)),
               out_shape=jax.ShapeDtypeStruct((S, D), F32),
               out_spec=pl.BlockSpec((1024, D), lambda i, j, k: (i, 0)),
               ca=1, cb=1, acc_shape=(1024, D), after=tok_ffn, name=n("ffn_up_bwd_x"))
    dx2, g["ffn_pre_norm"] = _norm_bwd(sv["x2"], w["ffn_pre_norm"], [d_h2], dx3, F32, n("ffn_pre_bwd"))
    d_mo, g["attn_post_norm"] = _norm_bwd(sv["mo"], w["attn_post_norm"], [dx2], None, BF16, n("attn_post_bwd"))
    g["w_out"] = _mm_tn(sv["merged"], d_mo, BF16, 1024, 1024, S, n("out_bwd_w"))
    do_a, do_b, do_c, d_gates, dwa, dwb, dwc, g["b_gate"] = _merge_bwd(
        d_mo, sv["o_a"], sv["o_b"], sv["o_c"], sv["gates"], w["b_gate"], w["w_br_a"], w["w_br_b"], w["w_br_c"],
        w["w_out"], n("merge_bwd"))
    g["w_br_a"], g["w_br_b"], g["w_br_c"] = dwa, dwb, dwc
    sinks = part("mix", w["sinks"])
    proj = sv["proj"]
    dqa, dka, dva, gbias = [], [], [], []
    for gi, (_, d) in enumerate(A_GROUPS):
        dq, dk, dv, gg, _ = _band_bwd(proj, bias, sv["o_a"], do_a, sv["lse_a"], sinks, d=d, q0=2 * gi, k0=6 + 2 * gi,
                                      v0=12 + 2 * gi, npairs=2, bias0=2 * gi, shared_kv=False, name=n(f"attn_a{gi}_bwd"))
        dqa.append(dq), dka.append(dk), dva.append(dv), gbias.append(gg)
    dqb, dkb, dvb, ggb, dsink = _band_bwd(proj, bias, sv["o_b"], do_b, sv["lse_b"], sinks, d=1, q0=18, k0=22, v0=23,
                                          npairs=4, bias0=6, shared_kv=True, name=n("attn_b_bwd"))
    gbias.append(ggb)
    g["bias_g"] = jnp.concatenate(gbias, axis=0).reshape(N_BIAS_HEADS, BLK, 2 * BLK)
    g["sinks"] = dsink[:, 0, :2].reshape(1, 8)
    dqc, dkc, dvc = _stick_bwd(proj, do_c, sv["tot_c"], q0=24, k0=26, v0=28, name=n("attn_c_bwd"))
    dqkv = jnp.concatenate(dqa + dka + dva + [dqb, dkb, dvb, dqc, dkc, dvc], axis=0)
    ts = 6
    tsx = 15
    dw_in = _mm(sv["h1"], dqkv, grid=(1, QKV_SLABS // ts, 1),
                a_spec=pl.BlockSpec((S, D), lambda i, j, k: (k, 0)),
                b_spec=pl.BlockSpec((ts, S, LANES), lambda i, j, k: (j, k, 0)),
                out_shape=jax.ShapeDtypeStruct((D, IN_COLS), BF16),
                out_spec=pl.BlockSpec((D, ts * LANES), lambda i, j, k: (0, j)),
                ca=0, cb=0, acc_shape=(D, ts * LANES), b_slab=True, name=n("in_bwd_w_qkv"))
    g["w_in"] = _mm(sv["h1"], d_gates, grid=(1, GATE_COLS // 768, 1),
                    a_spec=pl.BlockSpec((S, D), lambda i, j, k: (k, 0)),
                    b_spec=pl.BlockSpec((S, 768), lambda i, j, k: (k, j)),
                    out_shape=jax.ShapeDtypeStruct((D, IN_COLS), BF16),
                    out_spec=pl.BlockSpec((D, 768), lambda i, j, k: (0, j + QKV_COLS // 768)),
                    ca=0, cb=0, acc_shape=(D, 768), alias_out=dw_in, name=n("in_bwd_w_gate"))
    tok_in = on_part("in", g) if on_part is not None else None
    d_h1a = _mm(dqkv, w["w_in"], grid=(S // 1024, 1, QKV_SLABS // tsx),
                a_spec=pl.BlockSpec((tsx, 1024, LANES), lambda i, j, k: (k, i, 0)),
                b_spec=pl.BlockSpec((D, tsx * LANES), lambda i, j, k: (0, k)),
                out_shape=jax.ShapeDtypeStruct((S, D), F32),
                out_spec=pl.BlockSpec((1024, D), lambda i, j, k: (i, 0)),
                ca=1, cb=1, acc_shape=(1024, D), a_slab=True, after=tok_in, name=n("in_bwd_x_qkv"))
    d_h1b = _mm(d_gates, w["w_in"], grid=(S // 1024, 1, GATE_COLS // 768),
                a_spec=pl.BlockSpec((1024, 768), lambda i, j, k: (i, k)),
                b_spec=pl.BlockSpec((D, 768), lambda i, j, k: (0, k + QKV_COLS // 768)),
                out_shape=jax.ShapeDtypeStruct((S, D), F32),
                out_spec=pl.BlockSpec((1024, D), lambda i, j, k: (i, 0)),
                ca=1, cb=1, acc_shape=(1024, D), after=tok_in, name=n("in_bwd_x_gate"))
    dx, g["attn_pre_norm"] = _norm_bwd(sv["x"], w["attn_pre_norm"], [d_h1a, d_h1b], dx2, F32, n("attn_pre_bwd"))
    return dx, g, tok_in


def _local_step(x, target, ws, rel_bias, tok=None, on_grads=None):
    buckets = jnp.asarray(_bucket_tiles())
    bias = _bias_tiles(rel_bias, buckets, "bias_tiles").reshape(N_BIAS_HEADS // 2, 2, 2, BLK, 2 * BLK)
    saved = []
    gain0 = ws[0]["attn_pre_norm"] if tok is None else ws[0]["attn_pre_norm"] + tok
    h1 = _prenorm(x, gain0, "l0_attn_pre")
    for l in range(DEPTH):
        sv = _layer_fwd(x, h1, ws[l], bias, f"l{l}")
        saved.append(sv)
        g_next = ws[l + 1]["attn_pre_norm"] if l + 1 < DEPTH else ws[l]["attn_pre_norm"]
        x, h1 = _postnorm_res(sv["x2"], sv["fo"], ws[l]["ffn_post_norm"], g_next, f"l{l}_ffn_post")
    dy, loss_tile = _loss_head(x, target, "loss_head")
    grads = [None] * DEPTH
    tok = None
    for l in reversed(range(DEPTH)):
        on_part = None if on_grads is None else functools.partial(on_grads, l)
        dy, grads[l], tok = _layer_bwd(dy, saved[l], ws[l], bias, f"l{l}", tok, on_part)
    g_rel = _bias_grad([grads[l]["bias_g"] for l in range(DEPTH)], buckets, "bias_grad")[:, :N_BIAS_HEADS]
    return loss_tile[0, 0], dy, grads, g_rel


def _coords():
    return lax.axis_index("x"), lax.axis_index("y"), lax.axis_index("c")


def _peer(rel):
    x, y, c = _coords()
    return (1 - x if rel & 4 else x, 1 - y if rel & 2 else y, 1 - c if rel & 1 else c)


def _exchange(srcs, dst_shapes, src_win, dst_win, name, after=None):
    nt = len(srcs)
    extra = [] if after is None else [after]

    def body(*refs):
        src_refs, dst_refs = refs[:nt], refs[nt + len(extra):2 * nt + len(extra)]
        send_sems, recv_sems, local_sems = refs[2 * nt + len(extra):]
        x, y, c = _coords()
        me = 4 * x + 2 * y + c
        locals_ = []
        for t in range(nt):
            cp = pltpu.make_async_copy(src_win(t, src_refs[t], me), dst_win(t, dst_refs[t], me), local_sems.at[t])
            cp.start()
            locals_.append(cp)
        sends = []
        for rel in range(1, NDEV):
            px, py, pc = _peer(rel)
            q = 4 * px + 2 * py + pc
            for t in range(nt):
                cp = pltpu.make_async_remote_copy(
                    src_ref=src_win(t, src_refs[t], q), dst_ref=dst_win(t, dst_refs[t], me),
                    send_sem=send_sems.at[rel - 1, t], recv_sem=recv_sems.at[rel - 1, t],
                    device_id=(px, py, pc), device_id_type=MESH)
                cp.start()
                sends.append(cp)
        for rel in range(1, NDEV):
            px, py, pc = _peer(rel)
            q = 4 * px + 2 * py + pc
            for t in range(nt):
                pltpu.make_async_remote_copy(
                    src_ref=src_win(t, src_refs[t], me), dst_ref=dst_win(t, dst_refs[t], q),
                    send_sem=send_sems.at[rel - 1, t], recv_sem=recv_sems.at[rel - 1, t],
                    device_id=(px, py, pc), device_id_type=MESH).wait_recv()
        for cp in sends:
            cp.wait_send()
        for cp in locals_:
            cp.wait()

    return pl.pallas_call(
        body, in_specs=[ANY] * (nt + len(extra)), out_specs=[ANY] * nt, out_shape=dst_shapes,
        scratch_shapes=[pltpu.SemaphoreType.DMA((NDEV - 1, nt)), pltpu.SemaphoreType.DMA((NDEV - 1, nt)),
                        pltpu.SemaphoreType.DMA((nt,))],
        name=name)(*srcs, *extra)


BIG = (("w_in", 1, 864), ("w_br_a", 1, 128), ("w_br_b", 1, 128), ("w_br_c", 1, 128), ("w_out", 0, 128),
       ("w_up", 1, 1024), ("w_down", 0, 512))


NBIG = len(BIG)
BIG_FULL = {"w_in": (D, IN_COLS), "w_br_a": (256, D), "w_br_b": (512, D), "w_br_c": (256, D), "w_out": (D, D),
            "w_up": (D, 2 * D_FF), "w_down": (D_FF, D)}
LAYER_GROUPS = (("in", (0,)), ("mix", (1, 2, 3, 4)), ("ffn", (5, 6)))

HBM_SPEC = pl.BlockSpec(memory_space=pltpu.HBM)
SEM_SPEC = pl.BlockSpec(memory_space=pltpu.SEMAPHORE)


def _hbm(a):
    return pltpu.with_memory_space_constraint(a, pltpu.HBM)


def _shard_window(t, ref, k):
    nm, ax, ext = BIG[t % NBIG]
    if nm == "w_in":
        return ref.at[k]
    off = pl.multiple_of(k * ext, ext)
    if ax == 0:
        return ref.at[pl.ds(off, ext), :]
    return ref.at[:, pl.ds(off, ext)]


def _whole(t, ref, k):
    return ref


def _slot(t, ref, k):
    return ref.at[k]


def _own_block_spec(t, rows, me_of):
    nm, ax, ext = BIG[t % NBIG]
    r, c = BIG_FULL[nm]
    if nm == "w_in":
        return pl.BlockSpec((None, rows, ext), lambda i, m: (me_of(m), i, 0))
    if ax == 0:
        return pl.BlockSpec((rows, c), lambda i, m: (me_of(m) * (ext // rows) + i, 0))
    return pl.BlockSpec((rows, ext), lambda i, m: (i, me_of(m)))


def _cast_own(t, shard, me_arr, name):
    nm, ax, ext = BIG[t % NBIG]
    nr, nc = shard.shape
    rows = min(nr, 256)
    shape = (NDEV, D, ext) if nm == "w_in" else BIG_FULL[nm]

    def body(m_ref, s_ref, o_ref):
        o_ref[...] = s_ref[...].astype(BF16)

    return pl.pallas_call(
        body, grid_spec=pltpu.PrefetchScalarGridSpec(
            num_scalar_prefetch=1, grid=(nr // rows,),
            in_specs=[pl.BlockSpec((rows, nc), lambda i, m: (i, 0))],
            out_specs=_own_block_spec(t, rows, lambda m: m[0])),
        out_shape=jax.ShapeDtypeStruct(shape, BF16), compiler_params=_cp("arbitrary"), name=name)(me_arr, shard)


ALL_RELS = tuple(range(1, NDEV))
NEAR_RELS = (1, 2, 4, 6)
FAR_RELS = (2, 4, 6)


def _xchg_start(srcs, lands, groups, src_win, dst_win, after, name, rels=ALL_RELS):
    ns = 0 if srcs is None else len(srcs)
    nt, ng = len(lands), len(groups)
    ins = ([] if srcs is None else list(srcs)) + list(lands)

    def body(*refs):
        src_refs, land_refs = refs[:ns], refs[ns:ns + nt]
        sems = refs[ns + nt + 1:ns + nt + 1 + 2 * ng]
        token = refs[-1]
        x, y, c = _coords()
        me = 4 * x + 2 * y + c
        for gi, grp in enumerate(groups):
            for j, t in enumerate(grp):
                for ri, rel in enumerate(rels):
                    px, py, pc = _peer(rel)
                    q = 4 * px + 2 * py + pc
                    src = dst_win(t, land_refs[t], me) if srcs is None else src_win(t, src_refs[t], q)
                    pltpu.make_async_remote_copy(
                        src_ref=src, dst_ref=dst_win(t, land_refs[t], me),
                        send_sem=sems[2 * gi].at[ri * len(grp) + j],
                        recv_sem=sems[2 * gi + 1].at[ri * len(grp) + j],
                        device_id=(px, py, pc), device_id_type=MESH).start()
        token[...] = jnp.zeros((8, LANES), F32)

    out_shape = []
    for grp in groups:
        out_shape += [pltpu.SemaphoreType.DMA((len(rels) * len(grp),))] * 2
    out_shape += [pltpu.HBM(a.shape, a.dtype) for a in ins]
    out_shape.append(jax.ShapeDtypeStruct((8, LANES), F32))
    outs = pl.pallas_call(
        body, in_specs=[HBM_SPEC] * len(ins) + [ANY],
        out_specs=[SEM_SPEC] * (2 * ng) + [HBM_SPEC] * len(ins) + [pl.BlockSpec(memory_space=pltpu.VMEM)],
        out_shape=out_shape, input_output_aliases={i: 2 * ng + i for i in range(len(ins))},
        compiler_params=pltpu.CompilerParams(has_side_effects=pltpu.SideEffectType.DATAFLOW_SIDE_EFFECTING),
        name=name)(*[_hbm(a) for a in ins], after)
    sems = [(outs[2 * gi], outs[2 * gi + 1]) for gi in range(ng)]
    thru = list(outs[2 * ng:2 * ng + len(ins)])
    return sems, (None if srcs is None else thru[:ns]), thru[ns:], outs[-1]


def _xchg_wait(sems, srcs, lands, tids, after, src_win, dst_win, name, rels=ALL_RELS):
    ns = 0 if srcs is None else len(srcs)
    n = len(lands)
    send_sem, recv_sem = sems
    ins = ([] if srcs is None else list(srcs)) + list(lands)

    def body(*refs):
        src_refs, land_refs = refs[:ns], refs[ns:ns + n]
        ssem, rsem = refs[ns + n], refs[ns + n + 1]
        x, y, c = _coords()
        me = 4 * x + 2 * y + c
        for j, t in enumerate(tids):
            for ri, rel in enumerate(rels):
                px, py, pc = _peer(rel)
                q = 4 * px + 2 * py + pc
                src = dst_win(t, land_refs[j], me) if srcs is None else src_win(t, src_refs[j], q)
                cp = pltpu.make_async_remote_copy(
                    src_ref=src, dst_ref=dst_win(t, land_refs[j], q),
                    send_sem=ssem.at[ri * n + j], recv_sem=rsem.at[ri * n + j],
                    device_id=(px, py, pc), device_id_type=MESH)
                cp.wait_send()
                cp.wait_recv()

    outs = pl.pallas_call(
        body, in_specs=[HBM_SPEC] * len(ins) + [SEM_SPEC, SEM_SPEC, ANY], out_specs=[HBM_SPEC] * len(ins),
        out_shape=[pltpu.HBM(a.shape, a.dtype) for a in ins],
        input_output_aliases={i: i for i in range(len(ins))},
        compiler_params=pltpu.CompilerParams(has_side_effects=pltpu.SideEffectType.DATAFLOW_SIDE_EFFECTING),
        name=name)(*ins, send_sem, recv_sem, after)
    return (None if srcs is None else list(outs[:ns])), list(outs[ns:])


def _gather_forward(sems_in, lands, groups, tids, after, dst_win, name):
    nt, ng = len(lands), len(groups)

    def body(*refs):
        land_refs = refs[:nt]
        in_sems = refs[nt:nt + 2 * ng]
        out_sems = refs[nt + 2 * ng + 1:nt + 4 * ng + 1]
        token = refs[-1]
        x, y, c = _coords()
        me = 4 * x + 2 * y + c
        sib = (x, y, 1 - c)
        for gi, grp in enumerate(groups):
            n = len(grp)
            for j, pos in enumerate(grp):
                t = tids[pos]
                for ri, rel in enumerate(NEAR_RELS):
                    px, py, pc = _peer(rel)
                    q = 4 * px + 2 * py + pc
                    cp = pltpu.make_async_remote_copy(
                        src_ref=dst_win(t, land_refs[pos], me), dst_ref=dst_win(t, land_refs[pos], q),
                        send_sem=in_sems[2 * gi].at[ri * n + j], recv_sem=in_sems[2 * gi + 1].at[ri * n + j],
                        device_id=(px, py, pc), device_id_type=MESH)
                    cp.wait_send()
                    cp.wait_recv()
            for j, pos in enumerate(grp):
                t = tids[pos]
                for fi, rel in enumerate(FAR_RELS):
                    px, py, pc = _peer(rel)
                    q = 4 * px + 2 * py + pc
                    win = dst_win(t, land_refs[pos], q)
                    pltpu.make_async_remote_copy(
                        src_ref=win, dst_ref=win,
                        send_sem=out_sems[2 * gi].at[fi * n + j], recv_sem=out_sems[2 * gi + 1].at[fi * n + j],
                        device_id=sib, device_id_type=MESH).start()
        token[...] = jnp.zeros((8, LANES), F32)

    out_shape = []
    for grp in groups:
        out_shape += [pltpu.SemaphoreType.DMA((len(FAR_RELS) * len(grp),))] * 2
    out_shape += [pltpu.HBM(a.shape, a.dtype) for a in lands]
    out_shape.append(jax.ShapeDtypeStruct((8, LANES), F32))
    flat_sems = [s for pair in sems_in for s in pair]
    outs = pl.pallas_call(
        body, in_specs=[HBM_SPEC] * nt + [SEM_SPEC] * (2 * ng) + [ANY],
        out_specs=[SEM_SPEC] * (2 * ng) + [HBM_SPEC] * nt + [pl.BlockSpec(memory_space=pltpu.VMEM)],
        out_shape=out_shape, input_output_aliases={i: 2 * ng + i for i in range(nt)},
        compiler_params=pltpu.CompilerParams(has_side_effects=pltpu.SideEffectType.DATAFLOW_SIDE_EFFECTING),
        name=name)(*[_hbm(a) for a in lands], *flat_sems, after)
    sems = [(outs[2 * gi], outs[2 * gi + 1]) for gi in range(ng)]
    return sems, list(outs[2 * ng:2 * ng + nt]), outs[-1]


class _Weights:
    def __init__(self, ready, pending=None):
        self.ready = dict(ready)
        self.pending = dict(pending or {})

    def __getitem__(self, k):
        return self.ready[k]

    def need(self, group, after):
        fn = self.pending.pop(group, None)
        if fn is not None:
            self.ready.update(fn(after))


def _adamw_math(w, g, m, v):
    m2 = ADAM_B1 * m + (1.0 - ADAM_B1) * g
    v2 = ADAM_B2 * v + (1.0 - ADAM_B2) * (g * g)
    m_hat = m2 / (1.0 - ADAM_B1 ** ADAM_STEP)
    v_hat = v2 / (1.0 - ADAM_B2 ** ADAM_STEP)
    delta = -ADAM_LR * (m_hat / (jnp.sqrt(v_hat) + ADAM_EPS) + ADAM_WD * w)
    return delta, m2, v2


def _adamw(t, parts, own, me_arr, w, m, v, layer, prev, rows, name):
    nl, nr, nc = w.shape

    def body(me_ref, p_ref, own_ref, w_ref, m_ref, v_ref, *rest):
        g_ref, d_ref, m2_ref, v2_ref = rest[-4:]
        me = me_ref[0]
        g = None
        for k in range(NDEV):
            term = jnp.where(me == k, own_ref[...], p_ref[k]).astype(F32)
            g = term if g is None else g + term
        delta, m2, v2 = _adamw_math(w_ref[...], g, m_ref[...], v_ref[...])
        g_ref[...] = g
        d_ref[...] = delta
        m2_ref[...] = m2
        v2_ref[...] = v2

    blk = pl.BlockSpec((None, rows, nc), lambda i, mm: (layer, i, 0))
    pblk = pl.BlockSpec((NDEV, rows, nc), lambda i, mm: (0, i, 0))
    extra = [] if prev is None else list(prev)
    return pl.pallas_call(
        body, grid_spec=pltpu.PrefetchScalarGridSpec(
            num_scalar_prefetch=1, grid=(nr // rows,),
            in_specs=[pblk, _own_block_spec(t, rows, lambda mm: mm[0]), blk, blk, blk] + [ANY] * len(extra),
            out_specs=[blk] * 4),
        out_shape=[jax.ShapeDtypeStruct(w.shape, F32)] * 4,
        input_output_aliases={6 + k: k for k in range(len(extra))},
        compiler_params=_cp("arbitrary"), name=name)(me_arr, parts, own, w, m, v, *extra)


SMALL_REPL = (("rel_bias", NUM_BUCKETS * N_BIAS_HEADS), ("attn_pre_norm", DEPTH * D), ("sinks", DEPTH * 8),
              ("attn_post_norm", DEPTH * D), ("ffn_pre_norm", DEPTH * D), ("conv_b", DEPTH * 2 * D_FF),
              ("ffn_post_norm", DEPTH * D))
SMALL_SHARD = (("b_gate", (DEPTH, 3, D), 128), ("conv_w", (DEPTH, 3, 2 * D_FF), 1024))


def _pack(vecs):
    flat = jnp.concatenate([v.reshape(-1).astype(F32) for v in vecs])
    n = flat.shape[0]
    rows = -(-n // (8 * LANES)) * 8
    return jnp.pad(flat, (0, rows * LANES - n)).reshape(rows, LANES)


def _unpack(packed, sizes):
    flat = packed.reshape(-1)
    out, off = [], 0
    for sz in sizes:
        out.append(flat[off:off + sz])
        off += sz
    return out


ROWPACK = (("rel_bias", 32, 32, (NUM_BUCKETS, N_BIAS_HEADS)), ("sinks", 8, 8, (DEPTH, 8)),
           ("attn_pre_norm", 16, 16, (DEPTH, D)), ("attn_post_norm", 16, 16, (DEPTH, D)),
           ("ffn_pre_norm", 16, 16, (DEPTH, D)), ("ffn_post_norm", 16, 16, (DEPTH, D)),
           ("conv_b", 128, 128, (DEPTH, 2 * D_FF)), ("b_gate", 48, 8, (DEPTH, 3, 128)),
           ("conv_w", 384, 48, (DEPTH, 3, 1024)))
ROWS_FULL = sum(r for _, r, _, _ in ROWPACK)
ROWS_OWN = sum(r for _, _, r, _ in ROWPACK)


def _as_rows(a, rows):
    a = a.astype(F32)
    if a.shape[-1] < LANES:
        a = jnp.pad(a.reshape(-1, a.shape[-1]), ((0, 0), (0, LANES - a.shape[-1])))
    a = a.reshape(-1, LANES)
    return jnp.pad(a, ((0, rows - a.shape[0]), (0, 0)))


def _rowpack(arrs, own):
    return jnp.concatenate([_as_rows(arrs[nm], ro if own else rf) for nm, rf, ro, _ in ROWPACK], axis=0)


def _small_update(parts, w, m, v, me_arr, name):
    nsm = len(ROWPACK)

    def body(me_ref, p_ref, w_ref, m_ref, v_ref, *rest):
        outs = rest[:4 * nsm]
        gfull, g_s, d_s, m_s, v_s = rest[4 * nsm:]
        me = me_ref[0]
        g = p_ref[0]
        for k in range(1, NDEV):
            g = g + p_ref[k]
        gfull[...] = g
        of, oo = 0, 0
        for nm, rf, ro, _ in ROWPACK:
            if nm == "b_gate":
                g_s[oo:oo + ro, :] = jnp.zeros((ro, LANES), F32)
                for r in range(DEPTH * 3):
                    g_s[oo + r:oo + r + 1, :] = gfull[pl.ds(of + r * NDEV + me, 1), :]
            elif nm == "conv_w":
                for r in range(DEPTH * 3):
                    g_s[oo + r * 8:oo + r * 8 + 8, :] = gfull[pl.ds(pl.multiple_of(of + r * 64 + me * 8, 8), 8), :]
            else:
                g_s[oo:oo + ro, :] = gfull[of:of + rf, :]
            of, oo = of + rf, oo + ro
        delta, m2, v2 = _adamw_math(w_ref[...], g_s[...], m_ref[...], v_ref[...])
        d_s[...] = delta
        m_s[...] = m2
        v_s[...] = v2
        for kind, src in enumerate((g_s, d_s, m_s, v_s)):
            oo = 0
            for idx, (nm, rf, ro, shp) in enumerate(ROWPACK):
                o_ref = outs[kind * nsm + idx]
                if nm in ("rel_bias", "sinks"):
                    o_ref[...] = src[oo:oo + shp[0], 0:shp[1]]
                elif nm == "b_gate":
                    for l in range(DEPTH):
                        o_ref[l] = src[oo + 3 * l:oo + 3 * l + 3, :]
                elif nm == "conv_w":
                    for l in range(DEPTH):
                        for k in range(8):
                            o_ref[l, :, k * LANES:(k + 1) * LANES] = src[pl.ds(oo + 24 * l + k, 3, stride=8), :]
                else:
                    per = shp[1] // LANES
                    for k in range(per):
                        o_ref[:, k * LANES:(k + 1) * LANES] = src[pl.ds(oo + k, DEPTH, stride=per), :]
                oo += ro

    vm = pl.BlockSpec(memory_space=pltpu.VMEM)
    shapes = [jax.ShapeDtypeStruct(shp, F32) for _ in range(4) for _, _, _, shp in ROWPACK]
    outs = pl.pallas_call(
        body, in_specs=[SMEM, vm, vm, vm, vm], out_specs=[vm] * (4 * nsm), out_shape=shapes,
        scratch_shapes=[pltpu.VMEM((ROWS_FULL, LANES), F32)] + [pltpu.VMEM((ROWS_OWN, LANES), F32)] * 4,
        name=name)(me_arr, parts, w, m, v)
    names = [nm for nm, _, _, _ in ROWPACK]
    return [dict(zip(names, outs[kind * nsm:(kind + 1) * nsm])) for kind in range(4)]


def kernel(x, rel_bias, attn_pre_norm, w_in, b_gate, sinks, w_br_a, w_br_b, w_br_c, w_out, attn_post_norm, ffn_pre_norm, w_up, conv_w, conv_b, w_down, ffn_post_norm, loss_target, m_rel_bias, m_attn_pre_norm, m_w_in, m_b_gate, m_sinks, m_w_br_a, m_w_br_b, m_w_br_c, m_w_out, m_attn_post_norm, m_ffn_pre_norm, m_w_up, m_conv_w, m_conv_b, m_w_down, m_ffn_post_norm, v_rel_bias, v_attn_pre_norm, v_w_in, v_b_gate, v_sinks, v_w_br_a, v_w_br_b, v_w_br_c, v_w_out, v_attn_post_norm, v_ffn_pre_norm, v_w_up, v_conv_w, v_conv_b, v_w_down, v_ffn_post_norm):
    P = dict(rel_bias=rel_bias, attn_pre_norm=attn_pre_norm, w_in=w_in, b_gate=b_gate, sinks=sinks, w_br_a=w_br_a,
             w_br_b=w_br_b, w_br_c=w_br_c, w_out=w_out, attn_post_norm=attn_post_norm, ffn_pre_norm=ffn_pre_norm,
             w_up=w_up, conv_w=conv_w, conv_b=conv_b, w_down=w_down, ffn_post_norm=ffn_post_norm)
    M = dict(rel_bias=m_rel_bias, attn_pre_norm=m_attn_pre_norm, w_in=m_w_in, b_gate=m_b_gate, sinks=m_sinks,
             w_br_a=m_w_br_a, w_br_b=m_w_br_b, w_br_c=m_w_br_c, w_out=m_w_out, attn_post_norm=m_attn_post_norm,
             ffn_pre_norm=m_ffn_pre_norm, w_up=m_w_up, conv_w=m_conv_w, conv_b=m_conv_b, w_down=m_w_down,
             ffn_post_norm=m_ffn_post_norm)
    V = dict(rel_bias=v_rel_bias, attn_pre_norm=v_attn_pre_norm, w_in=v_w_in, b_gate=v_b_gate, sinks=v_sinks,
             w_br_a=v_w_br_a, w_br_b=v_w_br_b, w_br_c=v_w_br_c, w_out=v_w_out, attn_post_norm=v_attn_post_norm,
             ffn_pre_norm=v_ffn_pre_norm, w_up=v_w_up, conv_w=v_conv_w, conv_b=v_conv_b, w_down=v_w_down,
             ffn_post_norm=v_ffn_post_norm)
    xi, yi, ci = _coords()
    me = 4 * xi + 2 * yi + ci

    me_arr = me.astype(jnp.int32).reshape(1)

    small_w = _pack([b_gate.reshape(-1), conv_w.reshape(-1)])
    (small_w_all,) = _exchange([small_w], [jax.ShapeDtypeStruct((NDEV,) + small_w.shape, F32)],
                               _whole, _slot, "gather_small_weights")

    lands = [_cast_own(l * NBIG + t, P[nm][l], me_arr, f"gather_own_l{l}_{nm}")
             for l in range(DEPTH) for t, (nm, _, _) in enumerate(BIG)]
    groups = [tuple(l * NBIG + t for t in tids) for l in range(DEPTH) for _, tids in LAYER_GROUPS]
    g_sems, _, g_lands, g_tok = _xchg_start(None, lands, groups, None, _shard_window, small_w_all, "gather_start",
                                            rels=NEAR_RELS)
    tok0 = g_tok[0:1, 0:1]
    lands_now = list(g_lands)
    fwd_sems = {}
    fwd_plan = {0: (0,), 1: (1, 2), 3: (3, 4, 5)}

    def gather_waiter(gi, l, gname, tids):
        def wait(after):
            if gi in fwd_plan:
                gis = fwd_plan[gi]
                flat = [i for g2 in gis for i in groups[g2]]
                where = {tid: k for k, tid in enumerate(flat)}
                fs, new_lands, ftok = _gather_forward(
                    [g_sems[g2] for g2 in gis], [lands_now[i] for i in flat],
                    [[where[i] for i in groups[g2]] for g2 in gis], flat, after, _shard_window, f"gather_forward_{gi}")
                for g2, s in zip(gis, fs):
                    fwd_sems[g2] = s
                for i, a in zip(flat, new_lands):
                    lands_now[i] = a
                after = ftok
            ids = [l * NBIG + t for t in tids]
            _, got = _xchg_wait(fwd_sems[gi], None, [lands_now[i] for i in ids], ids, after,
                                None, _shard_window, f"gather_wait_l{l}_{gname}", rels=FAR_RELS)
            out = {}
            for t, arr in zip(tids, got):
                nm = BIG[t][0]
                out[nm] = jnp.transpose(arr, (1, 0, 2)).reshape(D, IN_COLS) if nm == "w_in" else arr
            return out
        return wait

    pending = [{gname: gather_waiter(l * len(LAYER_GROUPS) + k, l, gname, tids)
                for k, (gname, tids) in enumerate(LAYER_GROUPS)} for l in range(DEPTH)]
    nbg = DEPTH * 3 * 128
    ncw = DEPTH * 3 * 1024
    flat_all = small_w_all.reshape(NDEV, -1)
    b_gate_full = jnp.transpose(flat_all[:, :nbg].reshape(NDEV, DEPTH, 3, 128), (1, 2, 0, 3)).reshape(DEPTH, 3, D)
    conv_w_full = jnp.transpose(flat_all[:, nbg:nbg + ncw].reshape(NDEV, DEPTH, 3, 1024), (1, 2, 0, 3)).reshape(DEPTH, 3, 2 * D_FF)

    ws = []
    for l in range(DEPTH):
        ws.append(_Weights(dict(
            b_gate=b_gate_full[l], conv_w=conv_w_full[l].reshape(3, 2, D_FF), conv_b=conv_b[l].reshape(2, D_FF),
            sinks=sinks[l].reshape(1, 8),
            attn_pre_norm=attn_pre_norm[l].reshape(1, D), attn_post_norm=attn_post_norm[l].reshape(1, D),
            ffn_pre_norm=ffn_pre_norm[l].reshape(1, D), ffn_post_norm=ffn_post_norm[l].reshape(1, D)), pending[l]))

    rs = {}

    group_tids = dict(LAYER_GROUPS)

    def start_scatter(l, gname, grads_l):
        tids = group_tids[gname]
        blocks, lands_rs = [], []
        for t in tids:
            nm, ax, ext = BIG[t]
            gfull = grads_l[nm].astype(BF16)
            if nm == "w_in":
                gfull = jnp.transpose(gfull.reshape(D, NDEV, ext), (1, 0, 2))
                shp = (NDEV, D, ext)
            else:
                shp = (NDEV, ext, gfull.shape[1]) if ax == 0 else (NDEV, gfull.shape[0], ext)
            blocks.append(gfull)
            lands_rs.append(lax.empty(shp, BF16))
        local = list(range(len(tids)))
        win = lambda j, ref, k: _shard_window(tids[j], ref, k)
        sems, s_thru, l_thru, tok = _xchg_start(blocks, lands_rs, [tuple(local)], win, _slot, me_arr,
                                                f"scatter_start_l{l}_{gname}")
        rs[(l, gname)] = (sems[0], s_thru, l_thru, win, local)
        return tok[0:1, 0:1]

    loss_local, grad_x, grads, g_rel = _local_step(x[0], loss_target[0], ws, rel_bias, tok0, start_scatter)
    loss = lax.psum(loss_local, ("x", "y", "c"))

    stack = lambda nm: jnp.stack([grads[l][nm] for l in range(DEPTH)], axis=0)
    small_names = [nm for nm, _ in SMALL_REPL] + [nm for nm, _, _ in SMALL_SHARD]
    small_g = {"rel_bias": g_rel}
    for nm in small_names[1:]:
        small_g[nm] = stack(nm)
    small_packed = _rowpack(small_g, own=False)

    out_g, out_d, out_m, out_v = {}, {}, {}, {}
    prev = {nm: None for nm, _, _ in BIG}
    todo = [(l, gname) for l in reversed(range(DEPTH)) for gname in ("ffn", "mix", "in")]
    after, small_parts = grad_x, None
    for l, gname in todo:
        if (l, gname) == todo[-1]:
            (small_parts,) = _exchange([small_packed], [jax.ShapeDtypeStruct((NDEV,) + small_packed.shape, F32)],
                                       _whole, _slot, "gather_small_grads", after=after)
            after = small_parts
        sems, s_thru, l_thru, win, local = rs[(l, gname)]
        owns, parts = _xchg_wait(sems, s_thru, l_thru, local, after, win, _slot, f"scatter_wait_l{l}_{gname}")
        for t, own, prt in zip(group_tids[gname], owns, parts):
            nm = BIG[t][0]
            rows = {"w_in": 256, "w_up": 256, "w_down": 256}.get(nm, P[nm].shape[1])
            prev[nm] = _adamw(t, prt, own, me_arr, P[nm], M[nm], V[nm], l, prev[nm], rows, f"adamw_{nm}_l{l}")
            after = prev[nm][1]
    for nm, _, _ in BIG:
        out_g[nm], out_d[nm], out_m[nm], out_v[nm] = prev[nm]
    sm_g, sm_d, sm_m, sm_v = _small_update(small_parts, _rowpack(P, True), _rowpack(M, True), _rowpack(V, True),
                                           me_arr, "small_update")
    for dst, src in ((out_g, sm_g), (out_d, sm_d), (out_m, sm_m), (out_v, sm_v)):
        dst.update(src)

    order = ["rel_bias", "attn_pre_norm", "w_in", "b_gate", "sinks", "w_br_a", "w_br_b", "w_br_c", "w_out",
             "attn_post_norm", "ffn_pre_norm", "w_up", "conv_w", "conv_b", "w_down", "ffn_post_norm"]
    return (loss, grad_x[None], *[out_g[k] for k in order], *[out_d[k] for k in order],
            *[out_m[k] for k in order], *[out_v[k] for k in order])
```

```python
import functools
import math

import numpy as np
import jax
import jax.numpy as jnp
from jax import lax
from jax.experimental import pallas as pl
from jax.experimental.pallas import tpu as pltpu

F32 = jnp.float32
BF16 = jnp.bfloat16

S = 2048
D = 1024
DEPTH = 2
NDEV = 8
HD = 64
BLK = 128
NB = S // BLK
A_GROUPS = ((128, 1), (512, 4), (2048, 16))
NUM_BUCKETS = 32
MAX_DISTANCE = 2048
N_BIAS_HEADS = 20
D_FF = 4096
IN_COLS = 6912
QKV_COLS = 3840
QKV_SLABS = QKV_COLS // 128
GATE_COLS = 3072
EPS = 1e-6
SCALE = HD ** -0.5
NEG = -1e30
LANES = 128

ADAM_LR = 0.001
ADAM_B1 = 0.9
ADAM_B2 = 0.999
ADAM_EPS = 1e-08
ADAM_WD = 0.01
ADAM_STEP = 10

VMEM_LIMIT = 56 * 1024 * 1024
MESH = pl.DeviceIdType.MESH
ANY = pl.BlockSpec(memory_space=pl.ANY)
SMEM = pl.BlockSpec(memory_space=pltpu.SMEM)


def _cp(*sem):
    return pltpu.CompilerParams(dimension_semantics=sem if sem else None, vmem_limit_bytes=VMEM_LIMIT)


def _dot(a, b, ca, cb):
    return lax.dot_general(a, b, (((ca,), (cb,)), ((), ())), preferred_element_type=F32)


def _mm(a, b, *, grid, a_spec, b_spec, out_shape, out_spec, ca, cb, acc_shape, name,
        a_slab=False, b_slab=False, out_slab=False, alias_out=None, after=None):
    nk = grid[2]

    def body(*refs):
        a_ref, b_ref = refs[0], refs[1]
        o_ref, acc_ref = refs[-2], refs[-1]
        k = pl.program_id(2)

        def load(ref, slab):
            if slab:
                return jnp.concatenate([ref[s] for s in range(ref.shape[0])], axis=1).astype(BF16)
            return ref[...].astype(BF16)

        def write(val):
            if out_slab:
                for s in range(o_ref.shape[0]):
                    o_ref[s] = val[:, s * LANES:(s + 1) * LANES].astype(o_ref.dtype)
            else:
                o_ref[...] = val.astype(o_ref.dtype)

        d = _dot(load(a_ref, a_slab), load(b_ref, b_slab), ca, cb)
        if nk == 1:
            write(d)
        elif direct:
            @pl.when(k == 0)
            def _():
                o_ref[...] = d

            @pl.when(k > 0)
            def _():
                o_ref[...] += d
        else:
            @pl.when(k == 0)
            def _():
                acc_ref[...] = d

            if nk > 2:
                @pl.when((k > 0) & (k < nk - 1))
                def _():
                    acc_ref[...] += d

            @pl.when(k == nk - 1)
            def _():
                write(acc_ref[...] + d)

    direct = (not out_slab) and out_shape.dtype == F32
    if nk == 1 or direct:
        acc_shape = (8, LANES)
    in_specs = [a_spec, b_spec]
    args = [a, b]
    aliases = {}
    if alias_out is not None:
        in_specs.append(ANY)
        args.append(alias_out)
        aliases = {2: 0}
    if after is not None:
        in_specs.append(ANY)
        args.append(after)
    return pl.pallas_call(
        body, grid=grid, in_specs=in_specs, out_specs=out_spec, out_shape=out_shape,
        scratch_shapes=[pltpu.VMEM(acc_shape, F32)], input_output_aliases=aliases,
        compiler_params=_cp("parallel", "parallel", "arbitrary"), name=name)(*args)


def _mm_nn(a, b, out_dtype, tm, tn, tk, name):
    m, kk = a.shape
    n = b.shape[1]
    return _mm(a, b, grid=(m // tm, n // tn, kk // tk),
               a_spec=pl.BlockSpec((tm, tk), lambda i, j, k: (i, k)),
               b_spec=pl.BlockSpec((tk, tn), lambda i, j, k: (k, j)),
               out_shape=jax.ShapeDtypeStruct((m, n), out_dtype),
               out_spec=pl.BlockSpec((tm, tn), lambda i, j, k: (i, j)),
               ca=1, cb=0, acc_shape=(tm, tn), name=name)


def _mm_nt(a, b, out_dtype, tm, tn, tk, name):
    m, kk = a.shape
    n = b.shape[0]
    return _mm(a, b, grid=(m // tm, n // tn, kk // tk),
               a_spec=pl.BlockSpec((tm, tk), lambda i, j, k: (i, k)),
               b_spec=pl.BlockSpec((tn, tk), lambda i, j, k: (j, k)),
               out_shape=jax.ShapeDtypeStruct((m, n), out_dtype),
               out_spec=pl.BlockSpec((tm, tn), lambda i, j, k: (i, j)),
               ca=1, cb=1, acc_shape=(tm, tn), name=name)


def _mm_tn(a, b, out_dtype, tm, tn, tk, name):
    kk, m = a.shape
    n = b.shape[1]
    return _mm(a, b, grid=(m // tm, n // tn, kk // tk),
               a_spec=pl.BlockSpec((tk, tm), lambda i, j, k: (k, i)),
               b_spec=pl.BlockSpec((tk, tn), lambda i, j, k: (k, j)),
               out_shape=jax.ShapeDtypeStruct((m, n), out_dtype),
               out_spec=pl.BlockSpec((tm, tn), lambda i, j, k: (i, j)),
               ca=0, cb=0, acc_shape=(tm, tn), name=name)


ROW_TILE = 256


def _rms(x, g):
    r = lax.rsqrt(jnp.mean(x * x, axis=-1, keepdims=True) + EPS)
    return x * r * g


def _prenorm(x, g, name):
    def body(x_ref, g_ref, o_ref):
        o_ref[...] = _rms(x_ref[...], g_ref[...]).astype(BF16)

    return pl.pallas_call(
        body, grid=(S // ROW_TILE,),
        in_specs=[pl.BlockSpec((ROW_TILE, D), lambda i: (i, 0)), pl.BlockSpec((1, D), lambda i: (0, 0))],
        out_specs=pl.BlockSpec((ROW_TILE, D), lambda i: (i, 0)),
        out_shape=jax.ShapeDtypeStruct((S, D), BF16), compiler_params=_cp("parallel"), name=name)(x, g)


def _postnorm_res(x, f, g_post, g_next, name):
    def body(x_ref, f_ref, gp_ref, gn_ref, xo_ref, ho_ref):
        xn = x_ref[...] + _rms(f_ref[...], gp_ref[...])
        xo_ref[...] = xn
        ho_ref[...] = _rms(xn, gn_ref[...]).astype(BF16)

    row = pl.BlockSpec((ROW_TILE, D), lambda i: (i, 0))
    vec = pl.BlockSpec((1, D), lambda i: (0, 0))
    return pl.pallas_call(
        body, grid=(S // ROW_TILE,), in_specs=[row, row, vec, vec], out_specs=[row, row],
        out_shape=[jax.ShapeDtypeStruct((S, D), F32), jax.ShapeDtypeStruct((S, D), BF16)],
        compiler_params=_cp("parallel"), name=name)(x, f, g_post, g_next)


def _norm_bwd(f, g, dys, res, out_dtype, name):
    ndy = len(dys)
    has_res = res is not None

    def body(*refs):
        f_ref, g_ref = refs[0], refs[1]
        dy_refs = refs[2:2 + ndy]
        res_ref = refs[2 + ndy] if has_res else None
        o_ref, dg_ref = refs[-2], refs[-1]
        fv = f_ref[...]
        dy = dy_refs[0][...].astype(F32)
        for r in dy_refs[1:]:
            dy = dy + r[...].astype(F32)
        r = lax.rsqrt(jnp.mean(fv * fv, axis=-1, keepdims=True) + EPS)
        n = fv * r
        dn = dy * g_ref[...]
        df = r * (dn - n * jnp.mean(dn * n, axis=-1, keepdims=True))
        if has_res:
            df = df + res_ref[...]
        o_ref[...] = df.astype(out_dtype)

        @pl.when(pl.program_id(0) == 0)
        def _():
            dg_ref[...] = jnp.zeros((1, D), F32)

        dg_ref[...] += jnp.sum(dy * n, axis=0, keepdims=True)

    row = pl.BlockSpec((ROW_TILE, D), lambda i: (i, 0))
    vec = pl.BlockSpec((1, D), lambda i: (0, 0))
    in_specs = [row, vec] + [row] * ndy + ([row] if has_res else [])
    args = [f, g] + list(dys) + ([res] if has_res else [])
    return pl.pallas_call(
        body, grid=(S // ROW_TILE,), in_specs=in_specs, out_specs=[row, vec],
        out_shape=[jax.ShapeDtypeStruct((S, D), out_dtype), jax.ShapeDtypeStruct((1, D), F32)],
        compiler_params=_cp("arbitrary"), name=name)(*args)


def _loss_head(y, target, name):
    def body(y_ref, t_ref, dy_ref, l_ref):
        e = y_ref[...] - t_ref[...]
        dy_ref[...] = e * (1.0 / D)

        @pl.when(pl.program_id(0) == 0)
        def _():
            l_ref[...] = jnp.zeros((8, LANES), F32)

        l_ref[...] += jnp.sum(e * e) * (0.5 / D)

    row = pl.BlockSpec((ROW_TILE, D), lambda i: (i, 0))
    return pl.pallas_call(
        body, grid=(S // ROW_TILE,), in_specs=[row, row],
        out_specs=[row, pl.BlockSpec((8, LANES), lambda i: (0, 0))],
        out_shape=[jax.ShapeDtypeStruct((S, D), F32), jax.ShapeDtypeStruct((8, LANES), F32)],
        compiler_params=_cp("arbitrary"), name=name)(y, target)


def _bucket_tiles():
    a = np.arange(BLK)[:, None]
    b = np.arange(2 * BLK)[None, :]
    dist = a + BLK - b
    out = np.zeros((4, 2, BLK, 2 * BLK), np.int32)
    cfg = [(w // d, d) for w, d in A_GROUPS] + [(BLK - 1, 1)]
    for gi, (max_dist, d) in enumerate(cfg):
        band = (dist >= 0) & (dist <= max_dist)
        tok = np.maximum(dist, 0) * d
        nf = np.maximum(tok, 1).astype(np.float32)
        max_exact = NUM_BUCKETS // 2
        large = max_exact + (np.log(nf / np.float32(max_exact)) / np.float32(math.log(MAX_DISTANCE / max_exact))
                             * np.float32(NUM_BUCKETS - max_exact)).astype(np.int32)
        large = np.minimum(large, NUM_BUCKETS - 1)
        bkt = np.where(tok < max_exact, tok, large).astype(np.int32)
        full = np.where(band, bkt, -1)
        out[gi, 1] = full
        out[gi, 0] = np.where(b >= BLK, full, -1)
    return out


def _bias_tiles(rel_bias, buckets, name):
    def body(tab_ref, bkt_ref, o_ref):
        h = pl.program_id(0)
        bkt = bkt_ref[...]
        acc = jnp.zeros(bkt.shape, F32)
        for bb in range(NUM_BUCKETS):
            acc = jnp.where(bkt == bb, tab_ref[bb, h], acc)
        o_ref[...] = jnp.where(bkt < 0, NEG, acc)

    return pl.pallas_call(
        body, grid=(N_BIAS_HEADS,),
        in_specs=[SMEM, pl.BlockSpec((None, 2, BLK, 2 * BLK), lambda h: (jnp.minimum(h // 4, 3), 0, 0, 0))],
        out_specs=pl.BlockSpec((None, 2, BLK, 2 * BLK), lambda h: (h, 0, 0, 0)),
        out_shape=jax.ShapeDtypeStruct((N_BIAS_HEADS, 2, BLK, 2 * BLK), F32),
        compiler_params=_cp("arbitrary"), name=name)(rel_bias, buckets)


def _bias_grad(gs, buckets, name):
    ng = len(gs)

    def body(*refs):
        g_refs = refs[:ng]
        bkt_ref, o_ref = refs[ng], refs[ng + 1]
        h = pl.program_id(0)
        g = g_refs[0][...]
        for r in g_refs[1:]:
            g = g + r[...]
        bkt = bkt_ref[...]
        row = lax.broadcasted_iota(jnp.int32, (NUM_BUCKETS, LANES), 0)
        lane = lax.broadcasted_iota(jnp.int32, (NUM_BUCKETS, LANES), 1)

        @pl.when(h == 0)
        def _():
            o_ref[...] = jnp.zeros((NUM_BUCKETS, LANES), F32)

        acc = o_ref[...]
        for bb in range(NUM_BUCKETS):
            s = jnp.sum(jnp.where(bkt == bb, g, 0.0))
            acc = jnp.where((row == bb) & (lane == h), s, acc)
        o_ref[...] = acc

    g_spec = pl.BlockSpec((None, BLK, 2 * BLK), lambda h: (h, 0, 0))
    return pl.pallas_call(
        body, grid=(N_BIAS_HEADS,),
        in_specs=[g_spec] * ng + [pl.BlockSpec((None, None, BLK, 2 * BLK), lambda h: (jnp.minimum(h // 4, 3), 1, 0, 0))],
        out_specs=pl.BlockSpec((NUM_BUCKETS, LANES), lambda h: (0, 0)),
        out_shape=jax.ShapeDtypeStruct((NUM_BUCKETS, LANES), F32),
        compiler_params=_cp("arbitrary"), name=name)(*gs, buckets)


def _to_class_major(src_ref, dst_refs, d, fn=None):
    ln = S // d
    for r in range(d):
        v = src_ref[pl.ds(r, ln, stride=d), :] if d > 1 else src_ref[...]
        outs = fn(v) if fn is not None else (v,) * len(dst_refs)
        for dst, o in zip(dst_refs, outs):
            dst[pl.ds(r * ln, ln), :] = o.astype(dst.dtype)


def _head_masks(rows):
    lane = lax.broadcasted_iota(jnp.int32, (rows, LANES), 1)
    return lane < HD, lane >= HD


def _split_heads(v):
    m0, m1 = _head_masks(v.shape[0])
    return jnp.where(m0, v, 0.0), jnp.where(m1, v, 0.0)


def _dup_head(v, hi):
    m0, _ = _head_masks(v.shape[0])
    r = pltpu.roll(v, HD, 1)
    return jnp.where(m0, jnp.where(hi, r, v), jnp.where(hi, v, r))


def _block_rows(b, d):
    nbc = NB // d
    i = b % nbc
    r = b // nbc
    has_prev = (i > 0).astype(jnp.int32)
    prev = pl.multiple_of(jnp.maximum(b - 1, 0) * BLK, BLK)
    nat = i * (BLK * d) + r
    return has_prev, prev, nat


def _lane_halves(v0, v1):
    lane = lax.broadcasted_iota(jnp.int32, (v0.shape[0], LANES), 1)
    return jnp.where(lane < HD, v0, v1)


def _band_fwd(proj, bias, sinks, *, d, q0, k0, v0, npairs, bias0, shared_kv, name):
    def body(sink_ref, q_ref, k_ref, v_ref, b_ref, num_ref, st_ref, qz0, qz1, ks, vs):
        p = pl.program_id(0)
        kv = (lambda v: (_dup_head(v, p >= 2),)) if shared_kv else None
        _to_class_major(q_ref, (qz0, qz1), d, lambda v: _split_heads(v * SCALE))
        _to_class_major(k_ref, (ks,), d, kv)
        _to_class_major(v_ref, (vs,), d, kv)
        lane = lax.broadcasted_iota(jnp.int32, (BLK, LANES), 1)

        def blk(b, carry):
            has_prev, prev, nat = _block_rows(b, d)
            cur = pl.multiple_of(b * BLK, BLK)
            k2 = jnp.concatenate([ks[pl.ds(prev, BLK), :], ks[pl.ds(cur, BLK), :]], axis=0)
            v2 = jnp.concatenate([vs[pl.ds(prev, BLK), :], vs[pl.ds(cur, BLK), :]], axis=0)
            nums, ms, ls = [], [], []
            for hh, qz in enumerate((qz0, qz1)):
                z = _dot(qz[pl.ds(cur, BLK), :], k2, 1, 1) + b_ref[hh, has_prev]
                m = jnp.max(z, axis=1, keepdims=True)
                e = jnp.exp(z - m)
                l = jnp.sum(e, axis=1, keepdims=True)
                num = _dot(e.astype(BF16), v2, 1, 0)
                if shared_kv:
                    sink = sink_ref[0, 2 * p + hh]
                    mx = jnp.maximum(m, sink)
                    c = jnp.exp(m - mx)
                    zden = l * c + jnp.exp(sink - mx)
                    num = num * (c / zden)
                    m = mx + jnp.log(zden)
                ls.append(l)
                ms.append(m)
                nums.append(num)
            num_t = jnp.where(lane < HD, nums[0], nums[1])
            if shared_kv:
                st_t = jnp.where(lane < HD, ms[0], ms[1])
            else:
                st_t = jnp.where(lane < 32, ms[0], jnp.where(lane < 64, ls[0], jnp.where(lane < 96, ms[1], ls[1])))
            if d > 1:
                num_ref[pl.ds(nat, BLK, stride=d), :] = num_t
                st_ref[pl.ds(nat, BLK, stride=d), :] = st_t
            else:
                num_ref[pl.ds(cur, BLK), :] = num_t
                st_ref[pl.ds(cur, BLK), :] = st_t
            return carry

        lax.fori_loop(0, NB, blk, 0, unroll=8)

    slab = lambda off, per_pair: pl.BlockSpec((None, S, LANES), (lambda p: (off + p, 0, 0)) if per_pair else (lambda p: (off, 0, 0)))
    out = pl.BlockSpec((None, S, LANES), lambda p: (p, 0, 0))
    return pl.pallas_call(
        body, grid=(npairs,),
        in_specs=[SMEM, slab(q0, True), slab(k0, not shared_kv), slab(v0, not shared_kv),
                  pl.BlockSpec((None, 2, 2, BLK, 2 * BLK), lambda p: (bias0 + p, 0, 0, 0, 0))],
        out_specs=[out, out],
        out_shape=[jax.ShapeDtypeStruct((npairs, S, LANES), F32)] * 2,
        scratch_shapes=[pltpu.VMEM((S, LANES), BF16)] * 4,
        compiler_params=_cp("arbitrary"), name=name)(sinks, proj, proj, proj, bias)


def _combine_a(nums, stats, name):
    rt = 512

    def body(n0, n1, n2, s0, s1, s2, o_ref, l_ref):
        n_refs, s_refs = (n0, n1, n2), (s0, s1, s2)
        outs, lses = [], []
        for hh in range(2):
            ms = [s[:, 64 * hh:64 * hh + 1] for s in s_refs]
            ls = [s[:, 64 * hh + 32:64 * hh + 33] for s in s_refs]
            mx = jnp.maximum(jnp.maximum(ms[0], ms[1]), ms[2])
            cs = [jnp.exp(m - mx) for m in ms]
            z = cs[0] * ls[0] + cs[1] * ls[1] + cs[2] * ls[2]
            acc = cs[0] * n_refs[0][:, hh * HD:(hh + 1) * HD]
            acc = acc + cs[1] * n_refs[1][:, hh * HD:(hh + 1) * HD]
            acc = acc + cs[2] * n_refs[2][:, hh * HD:(hh + 1) * HD]
            outs.append(acc / z)
            lses.append(mx + jnp.log(z))
        o_ref[...] = jnp.concatenate(outs, axis=1)
        l_ref[...] = _lane_halves(lses[0], lses[1])

    spec = pl.BlockSpec((None, rt, LANES), lambda p, i: (p, i, 0))
    return pl.pallas_call(
        body, grid=(2, S // rt), in_specs=[spec] * 6, out_specs=[spec, spec],
        out_shape=[jax.ShapeDtypeStruct((2, S, LANES), F32)] * 2,
        compiler_params=_cp("parallel", "parallel"), name=name)(*nums, *stats)


def _band_bwd(proj, bias, o, do, lse, sinks, *, d, q0, k0, v0, npairs, bias0, shared_kv, name):
    nkv = 1 if shared_kv else npairs

    def body(sink_ref, q_ref, k_ref, v_ref, b_ref, o_ref, do_ref, lse_ref,
             dq_ref, dk_ref, dv_ref, g_ref, ds_ref,
             qz0, qz1, ks, vs, doz0, doz1, ls0, ls1, dls0, dls1, stage, dq_nat, dk_cm, dv_cm, kv_nat, dk_acc, dv_acc):
        p = pl.program_id(0)
        m0, m1 = _head_masks(S)
        prod = do_ref[...] * o_ref[...]
        dl0 = jnp.sum(jnp.where(m0, prod, 0.0), axis=1, keepdims=True)
        dl1 = jnp.sum(jnp.where(m1, prod, 0.0), axis=1, keepdims=True)
        if shared_kv:
            row8 = lax.broadcasted_iota(jnp.int32, (8, LANES), 0)
            lane8 = lax.broadcasted_iota(jnp.int32, (8, LANES), 1)
            t = jnp.zeros((8, LANES), F32)
            lv = lse_ref[...]
            for hh in range(2):
                sink = sink_ref[0, 2 * p + hh]
                ps = jnp.exp(sink - lv[:, 64 * hh:64 * hh + 1])
                dsink = -jnp.sum(ps * (dl0 if hh == 0 else dl1))
                t = jnp.where((row8 == 0) & (lane8 == hh), dsink, t)
            ds_ref[...] = t
        else:
            ds_ref[...] = jnp.zeros((8, LANES), F32)
        kv = (lambda v: (_dup_head(v, p >= 2),)) if shared_kv else None
        _to_class_major(q_ref, (qz0, qz1), d, lambda v: _split_heads(v * SCALE))
        _to_class_major(k_ref, (ks,), d, kv)
        _to_class_major(v_ref, (vs,), d, kv)
        _to_class_major(do_ref, (doz0, doz1), d, _split_heads)
        def spread(v):
            a0, a1 = _head_masks(v.shape[0])
            r = pltpu.roll(v, HD, 1)
            return jnp.where(a0, v, r), jnp.where(a1, v, r)

        _to_class_major(lse_ref, (ls0, ls1), d, spread)
        stage[...] = jnp.where(m0, dl0, dl1)
        _to_class_major(stage, (dls0, dls1), d, spread)

        dk_cm[...] = jnp.zeros((S, LANES), F32)
        dv_cm[...] = jnp.zeros((S, LANES), F32)
        g_ref[...] = jnp.zeros((2, BLK, 2 * BLK), F32)
        lane = lax.broadcasted_iota(jnp.int32, (BLK, LANES), 1)

        def blk(b, carry):
            has_prev, prev, nat = _block_rows(b, d)
            cur = pl.multiple_of(b * BLK, BLK)
            k2 = jnp.concatenate([ks[pl.ds(prev, BLK), :], ks[pl.ds(cur, BLK), :]], axis=0)
            v2 = jnp.concatenate([vs[pl.ds(prev, BLK), :], vs[pl.ds(cur, BLK), :]], axis=0)
            dqs, dks, dvs = [], [], []
            for hh, (qz, doz, lsr, dlr) in enumerate(((qz0, doz0, ls0, dls0), (qz1, doz1, ls1, dls1))):
                qb = qz[pl.ds(cur, BLK), :]
                dob = doz[pl.ds(cur, BLK), :]
                lb = lsr[pl.ds(cur, BLK), :]
                dlb = dlr[pl.ds(cur, BLK), :]
                z = _dot(qb, k2, 1, 1) + b_ref[hh, has_prev]
                pr = jnp.exp(z - jnp.concatenate([lb, lb], axis=1))
                dp = _dot(dob, v2, 1, 1)
                dz = pr * (dp - jnp.concatenate([dlb, dlb], axis=1))
                g_ref[hh] += dz
                dzb = dz.astype(BF16)
                dqs.append(_dot(dzb, k2, 1, 0))
                dks.append(_dot(dzb, qb, 0, 0))
                dvs.append(_dot(pr.astype(BF16), dob, 0, 0))
            dq_t = jnp.where(lane < HD, dqs[0], dqs[1]) * SCALE
            dk_t = dks[0] + dks[1]
            dv_t = dvs[0] + dvs[1]
            dk_cm[pl.ds(prev, BLK), :] += dk_t[:BLK]
            dk_cm[pl.ds(cur, BLK), :] += dk_t[BLK:]
            dv_cm[pl.ds(prev, BLK), :] += dv_t[:BLK]
            dv_cm[pl.ds(cur, BLK), :] += dv_t[BLK:]
            if d > 1:
                dq_nat[pl.ds(nat, BLK, stride=d), :] = dq_t
            else:
                dq_nat[pl.ds(cur, BLK), :] = dq_t
            return carry

        lax.fori_loop(0, NB, blk, 0, unroll=8)
        dq_ref[...] = dq_nat[...].astype(BF16)

        def from_class_major(src, dst_ref):
            if d == 1:
                dst_ref[...] = src[...].astype(BF16)
            else:
                ln = S // d
                for r in range(d):
                    kv_nat[pl.ds(r, ln, stride=d), :] = src[pl.ds(r * ln, ln), :]
                dst_ref[...] = kv_nat[...].astype(BF16)

        if not shared_kv:
            from_class_major(dk_cm, dk_ref)
            from_class_major(dv_cm, dv_ref)
        else:
            @pl.when(p == 0)
            def _():
                dk_acc[...] = jnp.zeros((S, LANES), F32)
                dv_acc[...] = jnp.zeros((S, LANES), F32)

            mine = m1 == (p >= 2)
            for cm, acc in ((dk_cm, dk_acc), (dv_cm, dv_acc)):
                val = cm[...]
                acc[...] += jnp.where(mine, val + pltpu.roll(val, HD, 1), 0.0)

            @pl.when(p == npairs - 1)
            def _():
                from_class_major(dk_acc, dk_ref)
                from_class_major(dv_acc, dv_ref)

    slab = lambda off, per_pair: pl.BlockSpec((None, S, LANES), (lambda p: (off + p, 0, 0)) if per_pair else (lambda p: (off, 0, 0)))
    pair = pl.BlockSpec((None, S, LANES), lambda p: (p, 0, 0))
    kv_out = pair if not shared_kv else pl.BlockSpec((None, S, LANES), lambda p: (0, 0, 0))
    return pl.pallas_call(
        body, grid=(npairs,),
        in_specs=[SMEM, slab(q0, True), slab(k0, not shared_kv), slab(v0, not shared_kv),
                  pl.BlockSpec((None, 2, 2, BLK, 2 * BLK), lambda p: (bias0 + p, 0, 0, 0, 0)),
                  pair, pair, pair],
        out_specs=[pair, kv_out, kv_out,
                   pl.BlockSpec((None, 2, BLK, 2 * BLK), lambda p: (p, 0, 0, 0)),
                   pl.BlockSpec((None, 8, LANES), lambda p: (p, 0, 0))],
        out_shape=[jax.ShapeDtypeStruct((npairs, S, LANES), BF16),
                   jax.ShapeDtypeStruct((nkv, S, LANES), BF16),
                   jax.ShapeDtypeStruct((nkv, S, LANES), BF16),
                   jax.ShapeDtypeStruct((npairs, 2, BLK, 2 * BLK), F32),
                   jax.ShapeDtypeStruct((npairs, 8, LANES), F32)],
        scratch_shapes=[pltpu.VMEM((S, LANES), BF16)] * 6 + [pltpu.VMEM((S, LANES), F32)] * 11,
        compiler_params=_cp("arbitrary"), name=name)(sinks, proj, proj, proj, bias, o, do, lse)


KC = 512
NSUB = KC // BLK
QB = 512
QPG = KC // QB


def _split2(x):
    hi = x.astype(BF16)
    lo = (x - hi.astype(F32)).astype(BF16)
    return hi, lo


def _tri_ones(cmp):
    jj = lax.broadcasted_iota(jnp.int32, (2 * BLK, BLK), 0) % BLK
    ss = lax.broadcasted_iota(jnp.int32, (2 * BLK, BLK), 1)
    return jnp.concatenate([cmp(jj, ss).astype(BF16), jnp.ones((2 * BLK, BLK), BF16)], axis=1)


def _sub_sums(x, tri1):
    n = x.shape[0]
    st = jnp.concatenate([x[:, s * BLK:(s + 1) * BLK] for s in range(NSUB)], axis=0)
    hi, lo = _split2(st)
    r = _dot(jnp.concatenate([hi, lo], axis=1), tri1, 1, 0)
    return ([r[s * n:(s + 1) * n, :BLK] for s in range(NSUB)], [r[s * n:(s + 1) * n, BLK:] for s in range(NSUB)])


def _log_sig_pair(z):
    lb = jnp.minimum(z, 0.0) - jnp.log1p(jnp.exp(-jnp.abs(z)))
    return lb, lb - z


QGROUPS = NB // NSUB


def _stick_fwd(proj, *, q0, k0, v0, name):
    def body(q_ref, k_ref, v_ref, o_ref, t_ref, qs, ks, vs):
        qs[...] = (q_ref[...] * SCALE).astype(BF16)
        ks[...] = k_ref[...].astype(BF16)
        vs[...] = v_ref[...].astype(BF16)
        tri1 = _tri_ones(lambda j, s: j > s)
        col = lax.broadcasted_iota(jnp.int32, (QB, KC), 1)
        rowi = lax.broadcasted_iota(jnp.int32, (QB, KC), 0)

        for qg in range(QGROUPS):
            def qblock(ii, carry0, qg=qg):
                t0 = pl.multiple_of((qg * QPG + ii) * QB, QB)
                qb = qs[pl.ds(t0, QB), :]
                accs = [jnp.zeros((QB, HD), F32)] * 2
                runs = [jnp.zeros((QB, BLK), F32)] * 2
                for c in reversed(range(qg + 1)):
                    s0 = c * KC
                    diag = c == qg
                    before = (s0 + col) < (t0 + rowi) if diag else None
                    for hh in range(2):
                        kh = ks[s0:s0 + KC, hh * HD:(hh + 1) * HD]
                        vh = vs[s0:s0 + KC, hh * HD:(hh + 1) * HD]
                        lb, lk = _log_sig_pair(_dot(qb[:, hh * HD:(hh + 1) * HD], kh, 1, 1))
                        if diag:
                            lk = jnp.where(before, lk, 0.0)
                        suf, tot = _sub_sums(lk, tri1)
                        ws, run = [], runs[hh]
                        for s in reversed(range(NSUB)):
                            ws.append(jnp.exp(lb[:, s * BLK:(s + 1) * BLK] + suf[s] + run))
                            run = run + tot[s]
                        w = jnp.concatenate(ws[::-1], axis=1)
                        if diag:
                            w = jnp.where(before, w, 0.0)
                        accs[hh] = accs[hh] + _dot(w.astype(BF16), vh, 1, 0)
                        runs[hh] = run
                o_ref[pl.ds(t0, QB), :] = jnp.concatenate(accs, axis=1)
                t_ref[pl.ds(t0, QB), :] = _lane_halves(runs[0], runs[1])
                return carry0

            lax.fori_loop(0, QPG, qblock, 0)

    slab = lambda off: pl.BlockSpec((None, S, LANES), lambda p: (off + p, 0, 0))
    out = pl.BlockSpec((None, S, LANES), lambda p: (p, 0, 0))
    return pl.pallas_call(
        body, grid=(2,), in_specs=[slab(q0), slab(k0), slab(v0)], out_specs=[out, out],
        out_shape=[jax.ShapeDtypeStruct((2, S, LANES), F32)] * 2,
        scratch_shapes=[pltpu.VMEM((S, LANES), BF16)] * 3,
        compiler_params=_cp("arbitrary"), name=name)(proj, proj, proj)


def _stick_bwd(proj, do, tot, *, q0, k0, v0, name):
    def body(q_ref, k_ref, v_ref, do_ref, t_ref, dq_ref, dk_ref, dv_ref, qs, ks, vs, dos, dk_acc, dv_acc):
        qs[...] = (q_ref[...] * SCALE).astype(BF16)
        ks[...] = k_ref[...].astype(BF16)
        vs[...] = v_ref[...].astype(BF16)
        dos[...] = do_ref[...].astype(BF16)
        dk_acc[...] = jnp.zeros((2, S, HD), F32)
        dv_acc[...] = jnp.zeros((2, S, HD), F32)
        tri_inc = _tri_ones(lambda j, s: j <= s)
        tri_exc = _tri_ones(lambda j, s: j < s)
        col = lax.broadcasted_iota(jnp.int32, (QB, KC), 1)
        rowi = lax.broadcasted_iota(jnp.int32, (QB, KC), 0)

        for qg in range(QGROUPS):
            def qblock(ii, carry0, qg=qg):
                t0 = pl.multiple_of((qg * QPG + ii) * QB, QB)
                qb = qs[pl.ds(t0, QB), :]
                dob = dos[pl.ds(t0, QB), :]
                tb = t_ref[pl.ds(t0, QB), :]
                dqs = [jnp.zeros((QB, HD), F32)] * 2
                pruns = [jnp.zeros((QB, BLK), F32)] * 2
                eruns = [jnp.zeros((QB, BLK), F32)] * 2
                for c in range(qg + 1):
                    s0 = c * KC
                    diag = c == qg
                    before = (s0 + col) < (t0 + rowi) if diag else None
                    for hh in range(2):
                        qh = qb[:, hh * HD:(hh + 1) * HD]
                        doh = dob[:, hh * HD:(hh + 1) * HD]
                        tt = tb[:, 64 * hh:64 * hh + 1]
                        kh = ks[s0:s0 + KC, hh * HD:(hh + 1) * HD]
                        vh = vs[s0:s0 + KC, hh * HD:(hh + 1) * HD]
                        lb, lk = _log_sig_pair(_dot(qh, kh, 1, 1))
                        if diag:
                            lk = jnp.where(before, lk, 0.0)
                        pin, ptot = _sub_sums(lk, tri_inc)
                        ws, prun = [], pruns[hh]
                        for s in range(NSUB):
                            ws.append(jnp.exp(lb[:, s * BLK:(s + 1) * BLK] + (tt - (pin[s] + prun))))
                            prun = prun + ptot[s]
                        w = jnp.concatenate(ws, axis=1)
                        if diag:
                            w = jnp.where(before, w, 0.0)
                        e = w * _dot(doh, vh, 1, 1)
                        pex, etot = _sub_sums(e, tri_exc)
                        cs, erun = [], eruns[hh]
                        for s in range(NSUB):
                            cs.append(pex[s] + erun)
                            erun = erun + etot[s]
                        sig = jnp.exp(lb)
                        dz = e * (1.0 - sig) - jnp.concatenate(cs, axis=1) * sig
                        if diag:
                            dz = jnp.where(before, dz, 0.0)
                        dz = dz.astype(BF16)
                        dqs[hh] = dqs[hh] + _dot(dz, kh, 1, 0)
                        dk_acc[hh, s0:s0 + KC, :] += _dot(dz, qh, 0, 0)
                        dv_acc[hh, s0:s0 + KC, :] += _dot(w.astype(BF16), doh, 0, 0)
                        pruns[hh], eruns[hh] = prun, erun
                dq_ref[pl.ds(t0, QB), :] = (jnp.concatenate(dqs, axis=1) * SCALE).astype(BF16)
                return carry0

            lax.fori_loop(0, QPG, qblock, 0)
        dk_ref[...] = jnp.concatenate([dk_acc[0], dk_acc[1]], axis=1).astype(BF16)
        dv_ref[...] = jnp.concatenate([dv_acc[0], dv_acc[1]], axis=1).astype(BF16)

    slab = lambda off: pl.BlockSpec((None, S, LANES), lambda p: (off + p, 0, 0))
    pair = pl.BlockSpec((None, S, LANES), lambda p: (p, 0, 0))
    return pl.pallas_call(
        body, grid=(2,), in_specs=[slab(q0), slab(k0), slab(v0), pair, pair], out_specs=[pair] * 3,
        out_shape=[jax.ShapeDtypeStruct((2, S, LANES), BF16)] * 3,
        scratch_shapes=[pltpu.VMEM((S, LANES), BF16)] * 4 + [pltpu.VMEM((2, S, HD), F32)] * 2,
        compiler_params=_cp("arbitrary"), name=name)(proj, proj, proj, do, tot)


def _cat_slabs(ref):
    return jnp.concatenate([ref[s] for s in range(ref.shape[0])], axis=1)


def _merge_fwd(o_a, o_b, o_c, gates, b_gate, wa, wb, wc, w_out, name):
    tm = ROW_TILE

    def body(oa_ref, ob_ref, oc_ref, g_ref, bg_ref, wa_ref, wb_ref, wc_ref, wo_ref, mg_ref, mo_ref):
        acc = jnp.zeros((tm, D), F32)
        for i, (o_ref, w_ref) in enumerate(((oa_ref, wa_ref), (ob_ref, wb_ref), (oc_ref, wc_ref))):
            pr = _dot(_cat_slabs(o_ref).astype(BF16), w_ref[...], 1, 0)
            sg = jax.nn.sigmoid(g_ref[:, i * D:(i + 1) * D] + bg_ref[i:i + 1, :])
            acc = acc + sg * pr
        mg = acc.astype(BF16)
        mg_ref[...] = mg
        mo_ref[...] = _dot(mg, wo_ref[...], 1, 0)

    slabs = lambda n: pl.BlockSpec((n, tm, LANES), lambda i: (0, i, 0))
    full = lambda r, c: pl.BlockSpec((r, c), lambda i: (0, 0))
    row = pl.BlockSpec((tm, D), lambda i: (i, 0))
    return pl.pallas_call(
        body, grid=(S // tm,),
        in_specs=[slabs(2), slabs(4), slabs(2), pl.BlockSpec((tm, GATE_COLS), lambda i: (i, 0)), full(3, D),
                  full(256, D), full(512, D), full(256, D), full(D, D)],
        out_specs=[row, row],
        out_shape=[jax.ShapeDtypeStruct((S, D), BF16), jax.ShapeDtypeStruct((S, D), F32)],
        compiler_params=_cp("parallel"), name=name)(o_a, o_b, o_c, gates, b_gate, wa, wb, wc, w_out)


def _merge_bwd(d_mo, o_a, o_b, o_c, gates, b_gate, wa, wb, wc, w_out, name):
    tm = ROW_TILE

    def body(dmo_ref, oa_ref, ob_ref, oc_ref, g_ref, bg_ref, wa_ref, wb_ref, wc_ref, wo_ref,
             doa_ref, dob_ref, doc_ref, dg_ref, dwa_ref, dwb_ref, dwc_ref, dbg_ref):
        @pl.when(pl.program_id(0) == 0)
        def _():
            dwa_ref[...] = jnp.zeros(dwa_ref.shape, F32)
            dwb_ref[...] = jnp.zeros(dwb_ref.shape, F32)
            dwc_ref[...] = jnp.zeros(dwc_ref.shape, F32)
            dbg_ref[...] = jnp.zeros(dbg_ref.shape, F32)

        dmg = _dot(dmo_ref[...], wo_ref[...], 1, 1)
        trip = ((oa_ref, wa_ref, doa_ref, dwa_ref), (ob_ref, wb_ref, dob_ref, dwb_ref), (oc_ref, wc_ref, doc_ref, dwc_ref))
        for i, (o_ref, w_ref, do_ref, dw_ref) in enumerate(trip):
            ob = _cat_slabs(o_ref).astype(BF16)
            pr = _dot(ob, w_ref[...], 1, 0)
            sg = jax.nn.sigmoid(g_ref[:, i * D:(i + 1) * D] + bg_ref[i:i + 1, :])
            dgate = dmg * pr * sg * (1.0 - sg)
            dg_ref[:, i * D:(i + 1) * D] = dgate.astype(BF16)
            dbg_ref[i:i + 1, :] += jnp.sum(dgate, axis=0, keepdims=True)
            dpr = (dmg * sg).astype(BF16)
            do = _dot(dpr, w_ref[...], 1, 1)
            for s in range(do_ref.shape[0]):
                do_ref[s] = do[:, s * LANES:(s + 1) * LANES]
            dw_ref[...] += _dot(ob, dpr, 0, 0)

    slabs = lambda n: pl.BlockSpec((n, tm, LANES), lambda i: (0, i, 0))
    full = lambda r, c: pl.BlockSpec((r, c), lambda i: (0, 0))
    row = pl.BlockSpec((tm, D), lambda i: (i, 0))
    return pl.pallas_call(
        body, grid=(S // tm,),
        in_specs=[row, slabs(2), slabs(4), slabs(2), pl.BlockSpec((tm, GATE_COLS), lambda i: (i, 0)), full(3, D),
                  full(256, D), full(512, D), full(256, D), full(D, D)],
        out_specs=[slabs(2), slabs(4), slabs(2), pl.BlockSpec((tm, GATE_COLS), lambda i: (i, 0)),
                   full(256, D), full(512, D), full(256, D), full(3, D)],
        out_shape=[jax.ShapeDtypeStruct((2, S, LANES), F32), jax.ShapeDtypeStruct((4, S, LANES), F32),
                   jax.ShapeDtypeStruct((2, S, LANES), F32), jax.ShapeDtypeStruct((S, GATE_COLS), BF16),
                   jax.ShapeDtypeStruct((256, D), F32), jax.ShapeDtypeStruct((512, D), F32),
                   jax.ShapeDtypeStruct((256, D), F32), jax.ShapeDtypeStruct((3, D), F32)],
        compiler_params=_cp("arbitrary"), name=name)(d_mo, o_a, o_b, o_c, gates, b_gate, wa, wb, wc, w_out)


FC = 256
GELU_K = math.sqrt(2.0 / math.pi)
GELU_C = 0.044715


RC = 64
NRC = S // RC


def _down(tail, cur, n):
    row = lax.broadcasted_iota(jnp.int32, tail.shape, 0)
    rolled = pltpu.roll(cur, n, 0)
    first = jnp.where(row < n, pltpu.roll(tail, n, 0), rolled[0:8])
    return jnp.concatenate([first, rolled[8:]], axis=0)


def _up(cur, head, n):
    row = lax.broadcasted_iota(jnp.int32, head.shape, 0)
    rolled = pltpu.roll(cur, RC - n, 0)
    last = jnp.where(row >= 8 - n, pltpu.roll(head, 8 - n, 0), rolled[RC - 8:])
    return jnp.concatenate([rolled[:RC - 8], last], axis=0)


def _conv_chunk(load, j, w_ref, b_ref, half):
    r0 = pl.multiple_of(j * RC, RC)
    cur = load(r0, RC)
    tail = jnp.where(j > 0, load(pl.multiple_of(jnp.maximum(r0 - 8, 0), 8), 8), 0.0)
    d1 = _down(tail, cur, 1)
    d2 = _down(tail, cur, 2)
    y = w_ref[0:1, half, :] * d2 + w_ref[1:2, half, :] * d1 + w_ref[2:3, half, :] * cur + b_ref[half:half + 1, :]
    return y, cur, d1, d2


def _chunk(j):
    return pl.ds(pl.multiple_of(j * RC, RC), RC)


def _fold8(x):
    return jnp.sum(x.reshape(RC // 8, 8, x.shape[-1]), axis=0)


def _ffn_act(u, conv_w, conv_b, name):
    def body(u_ref, w_ref, b_ref, a_ref):
        def step(j, carry):
            yg = _conv_chunk(lambda r, n: u_ref[0, pl.ds(r, n), :], j, w_ref, b_ref, 0)[0]
            yv = _conv_chunk(lambda r, n: u_ref[1, pl.ds(r, n), :], j, w_ref, b_ref, 1)[0]
            th = jnp.tanh(GELU_K * (yg + GELU_C * yg * yg * yg))
            a_ref[_chunk(j), :] = (0.5 * yg * (1.0 + th) * yv).astype(BF16)
            return carry

        lax.fori_loop(0, NRC, step, 0)

    return pl.pallas_call(
        body, grid=(D_FF // FC,),
        in_specs=[pl.BlockSpec((2, S, FC), lambda j: (0, 0, j)), pl.BlockSpec((3, 2, FC), lambda j: (0, 0, j)),
                  pl.BlockSpec((2, FC), lambda j: (0, j))],
        out_specs=pl.BlockSpec((S, FC), lambda j: (0, j)),
        out_shape=jax.ShapeDtypeStruct((S, D_FF), BF16),
        compiler_params=_cp("parallel"), name=name)(u, conv_w, conv_b)


def _ffn_act_bwd(u, d_a, conv_w, conv_b, name):
    def body(u_ref, da_ref, w_ref, b_ref, du_ref, dw_ref, db_ref, dy_s):
        def first(j, acc):
            yg, ug, ug1, ug2 = _conv_chunk(lambda r, n: u_ref[0, pl.ds(r, n), :], j, w_ref, b_ref, 0)
            yv, uv, uv1, uv2 = _conv_chunk(lambda r, n: u_ref[1, pl.ds(r, n), :], j, w_ref, b_ref, 1)
            th = jnp.tanh(GELU_K * (yg + GELU_C * yg * yg * yg))
            gelu = 0.5 * yg * (1.0 + th)
            dgelu = 0.5 * (1.0 + th) + 0.5 * yg * (1.0 - th * th) * GELU_K * (1.0 + 3.0 * GELU_C * yg * yg)
            da = da_ref[_chunk(j), :]
            dyg = da * yv * dgelu
            dyv = da * gelu
            dy_s[0, _chunk(j), :] = dyg
            dy_s[1, _chunk(j), :] = dyv
            new = (_fold8(dyg * ug2), _fold8(dyg * ug1), _fold8(dyg * ug), _fold8(dyg),
                   _fold8(dyv * uv2), _fold8(dyv * uv1), _fold8(dyv * uv), _fold8(dyv))
            return tuple(a + n for a, n in zip(acc, new))

        acc = lax.fori_loop(0, NRC, first, tuple(jnp.zeros((8, FC), F32) for _ in range(8)))
        for half in range(2):
            for k in range(3):
                dw_ref[k:k + 1, half, :] = jnp.sum(acc[4 * half + k], axis=0, keepdims=True)
            db_ref[half:half + 1, :] = jnp.sum(acc[4 * half + 3], axis=0, keepdims=True)

        def second(j, carry):
            for half in range(2):
                cur = dy_s[half, _chunk(j), :]
                h0 = pl.multiple_of(jnp.minimum((j + 1) * RC, S - 8), 8)
                head = jnp.where(j < NRC - 1, dy_s[half, pl.ds(h0, 8), :], 0.0)
                du = (w_ref[2:3, half, :] * cur + w_ref[1:2, half, :] * _up(cur, head, 1)
                      + w_ref[0:1, half, :] * _up(cur, head, 2))
                du_ref[half, _chunk(j), :] = du.astype(BF16)
            return carry

        lax.fori_loop(0, NRC, second, 0)

    return pl.pallas_call(
        body, grid=(D_FF // FC,),
        in_specs=[pl.BlockSpec((2, S, FC), lambda j: (0, 0, j)), pl.BlockSpec((S, FC), lambda j: (0, j)),
                  pl.BlockSpec((3, 2, FC), lambda j: (0, 0, j)), pl.BlockSpec((2, FC), lambda j: (0, j))],
        out_specs=[pl.BlockSpec((2, S, FC), lambda j: (0, 0, j)), pl.BlockSpec((3, 2, FC), lambda j: (0, 0, j)),
                   pl.BlockSpec((2, FC), lambda j: (0, j))],
        out_shape=[jax.ShapeDtypeStruct((2, S, D_FF), BF16), jax.ShapeDtypeStruct((3, 2, D_FF), F32),
                   jax.ShapeDtypeStruct((2, D_FF), F32)],
        scratch_shapes=[pltpu.VMEM((2, S, FC), F32)],
        compiler_params=_cp("parallel"), name=name)(u, d_a, conv_w, conv_b)


def _layer_fwd(x, h1, w, bias, lname):
    n = lambda s: f"{lname}_{s}"
    w.need("in", h1)
    tn = 768
    proj = _mm(h1, w["w_in"], grid=(S // 1024, QKV_COLS // tn, 1),
               a_spec=pl.BlockSpec((1024, D), lambda i, j, k: (i, 0)),
               b_spec=pl.BlockSpec((D, tn), lambda i, j, k: (0, j)),
               out_shape=jax.ShapeDtypeStruct((QKV_SLABS, S, LANES), F32),
               out_spec=pl.BlockSpec((tn // LANES, 1024, LANES), lambda i, j, k: (j, i, 0)),
               ca=1, cb=0, acc_shape=(1024, tn), out_slab=True, name=n("proj_qkv"))
    gates = _mm(h1, w["w_in"], grid=(S // 1024, GATE_COLS // tn, 1),
                a_spec=pl.BlockSpec((1024, D), lambda i, j, k: (i, 0)),
                b_spec=pl.BlockSpec((D, tn), lambda i, j, k: (0, j + QKV_COLS // tn)),
                out_shape=jax.ShapeDtypeStruct((S, GATE_COLS), F32),
                out_spec=pl.BlockSpec((1024, tn), lambda i, j, k: (i, j)),
                ca=1, cb=0, acc_shape=(1024, tn), name=n("proj_gate"))
    nums, stats = [], []
    for g, (_, d) in enumerate(A_GROUPS):
        nm, st = _band_fwd(proj, bias, w["sinks"], d=d, q0=2 * g, k0=6 + 2 * g, v0=12 + 2 * g, npairs=2, bias0=2 * g,
                           shared_kv=False, name=n(f"attn_a{g}_fwd"))
        nums.append(nm)
        stats.append(st)
    o_a, lse_a = _combine_a(nums, stats, n("attn_a_combine"))
    o_b, lse_b = _band_fwd(proj, bias, w["sinks"], d=1, q0=18, k0=22, v0=23, npairs=4, bias0=6, shared_kv=True,
                           name=n("attn_b_fwd"))
    o_c, tot_c = _stick_fwd(proj, q0=24, k0=26, v0=28, name=n("attn_c_fwd"))
    w.need("mix", tot_c)
    merged, mo = _merge_fwd(o_a, o_b, o_c, gates, w["b_gate"], w["w_br_a"], w["w_br_b"], w["w_br_c"], w["w_out"], n("merge_fwd"))
    x2, h2 = _postnorm_res(x, mo, w["attn_post_norm"], w["ffn_pre_norm"], n("attn_post"))
    w.need("ffn", h2)
    u = _mm(h2, w["w_up"], grid=(S // 1024, 2 * D_FF // 1024, 1),
            a_spec=pl.BlockSpec((1024, D), lambda i, j, k: (i, 0)),
            b_spec=pl.BlockSpec((D, 1024), lambda i, j, k: (0, j)),
            out_shape=jax.ShapeDtypeStruct((2, S, D_FF), F32),
            out_spec=pl.BlockSpec((None, 1024, 1024), lambda i, j, k: (j // 4, i, j % 4)),
            ca=1, cb=0, acc_shape=(1024, 1024), name=n("ffn_up"))
    a = _ffn_act(u, w["conv_w"], w["conv_b"], n("ffn_act"))
    fo = _mm_nn(a, w["w_down"], F32, 1024, 1024, 2048, n("ffn_down"))
    saved = dict(x=x, h1=h1, proj=proj, gates=gates, o_a=o_a, lse_a=lse_a, o_b=o_b, lse_b=lse_b, o_c=o_c, tot_c=tot_c,
                 merged=merged, mo=mo, x2=x2, h2=h2, u=u, a=a, fo=fo)
    return saved


def _layer_bwd(dx3, sv, w, bias, lname, tok=None, on_part=None):
    n = lambda s: f"{lname}_{s}"
    g = {}

    def part(group, vec):
        t = on_part(group, g) if on_part is not None else None
        return vec if t is None else vec + t

    gain = w["ffn_post_norm"] if tok is None else w["ffn_post_norm"] + tok
    d_fo, g["ffn_post_norm"] = _norm_bwd(sv["fo"], gain, [dx3], None, BF16, n("ffn_post_bwd"))
    d_a = _mm_nt(d_fo, w["w_down"], F32, 1024, 1024, 1024, n("ffn_down_bwd_x"))
    g["w_down"] = _mm_tn(sv["a"], d_fo, BF16, 1024, 1024, S, n("ffn_down_bwd_w"))
    d_u, dcw, dcb = _ffn_act_bwd(sv["u"], d_a, w["conv_w"], w["conv_b"], n("ffn_act_bwd"))
    g["conv_w"] = dcw.reshape(3, 2 * D_FF)
    g["conv_b"] = dcb.reshape(1, 2 * D_FF)
    g["w_up"] = _mm(sv["h2"], d_u, grid=(1, 2 * D_FF // 1024, 1),
                    a_spec=pl.BlockSpec((S, D), lambda i, j, k: (k, 0)),
                    b_spec=pl.BlockSpec((None, S, 1024), lambda i, j, k: (j // 4, k, j % 4)),
                    out_shape=jax.ShapeDtypeStruct((D, 2 * D_FF), BF16),
                    out_spec=pl.BlockSpec((D, 1024), lambda i, j, k: (0, j)),
                    ca=0, cb=0, acc_shape=(D, 1024), name=n("ffn_up_bwd_w"))
    tok_ffn = on_part("ffn", g) if on_part is not None else None
    d_h2 = _mm(d_u, w["w_up"], grid=(S // 1024, 1, 2 * D_FF // 2048),
               a_spec=pl.BlockSpec((None, 1024, 2048), lambda i, j, k: (k // 2, i, k % 2)),
               b_spec=pl.BlockSpec((D, 2048), lambda i, j, k: (0, k)),
               out_shape=jax.ShapeDtypeStruct((S, D), F32),
               out_spec=pl.BlockSpec((1024, D), lambda i, j, k: (i, 0)),
               ca=1, cb=1, acc_shape=(1024, D), after=tok_ffn, name=n("ffn_up_bwd_x"))
    dx2, g["ffn_pre_norm"] = _norm_bwd(sv["x2"], w["ffn_pre_norm"], [d_h2], dx3, F32, n("ffn_pre_bwd"))
    d_mo, g["attn_post_norm"] = _norm_bwd(sv["mo"], w["attn_post_norm"], [dx2], None, BF16, n("attn_post_bwd"))
    g["w_out"] = _mm_tn(sv["merged"], d_mo, BF16, 1024, 1024, S, n("out_bwd_w"))
    do_a, do_b, do_c, d_gates, dwa, dwb, dwc, g["b_gate"] = _merge_bwd(
        d_mo, sv["o_a"], sv["o_b"], sv["o_c"], sv["gates"], w["b_gate"], w["w_br_a"], w["w_br_b"], w["w_br_c"],
        w["w_out"], n("merge_bwd"))
    g["w_br_a"], g["w_br_b"], g["w_br_c"] = dwa, dwb, dwc
    sinks = part("mix", w["sinks"])
    proj = sv["proj"]
    dqa, dka, dva, gbias = [], [], [], []
    for gi, (_, d) in enumerate(A_GROUPS):
        dq, dk, dv, gg, _ = _band_bwd(proj, bias, sv["o_a"], do_a, sv["lse_a"], sinks, d=d, q0=2 * gi, k0=6 + 2 * gi,
                                      v0=12 + 2 * gi, npairs=2, bias0=2 * gi, shared_kv=False, name=n(f"attn_a{gi}_bwd"))
        dqa.append(dq), dka.append(dk), dva.append(dv), gbias.append(gg)
    dqb, dkb, dvb, ggb, dsink = _band_bwd(proj, bias, sv["o_b"], do_b, sv["lse_b"], sinks, d=1, q0=18, k0=22, v0=23,
                                          npairs=4, bias0=6, shared_kv=True, name=n("attn_b_bwd"))
    gbias.append(ggb)
    g["bias_g"] = jnp.concatenate(gbias, axis=0).reshape(N_BIAS_HEADS, BLK, 2 * BLK)
    g["sinks"] = dsink[:, 0, :2].reshape(1, 8)
    dqc, dkc, dvc = _stick_bwd(proj, do_c, sv["tot_c"], q0=24, k0=26, v0=28, name=n("attn_c_bwd"))
    dqkv = jnp.concatenate(dqa + dka + dva + [dqb, dkb, dvb, dqc, dkc, dvc], axis=0)
    ts = 6
    tsx = 15
    dw_in = _mm(sv["h1"], dqkv, grid=(1, QKV_SLABS // ts, 1),
                a_spec=pl.BlockSpec((S, D), lambda i, j, k: (k, 0)),
                b_spec=pl.BlockSpec((ts, S, LANES), lambda i, j, k: (j, k, 0)),
                out_shape=jax.ShapeDtypeStruct((D, IN_COLS), BF16),
                out_spec=pl.BlockSpec((D, ts * LANES), lambda i, j, k: (0, j)),
                ca=0, cb=0, acc_shape=(D, ts * LANES), b_slab=True, name=n("in_bwd_w_qkv"))
    g["w_in"] = _mm(sv["h1"], d_gates, grid=(1, GATE_COLS // 768, 1),
                    a_spec=pl.BlockSpec((S, D), lambda i, j, k: (k, 0)),
                    b_spec=pl.BlockSpec((S, 768), lambda i, j, k: (k, j)),
                    out_shape=jax.ShapeDtypeStruct((D, IN_COLS), BF16),
                    out_spec=pl.BlockSpec((D, 768), lambda i, j, k: (0, j + QKV_COLS // 768)),
                    ca=0, cb=0, acc_shape=(D, 768), alias_out=dw_in, name=n("in_bwd_w_gate"))
    tok_in = on_part("in", g) if on_part is not None else None
    d_h1a = _mm(dqkv, w["w_in"], grid=(S // 1024, 1, QKV_SLABS // tsx),
                a_spec=pl.BlockSpec((tsx, 1024, LANES), lambda i, j, k: (k, i, 0)),
                b_spec=pl.BlockSpec((D, tsx * LANES), lambda i, j, k: (0, k)),
                out_shape=jax.ShapeDtypeStruct((S, D), F32),
                out_spec=pl.BlockSpec((1024, D), lambda i, j, k: (i, 0)),
                ca=1, cb=1, acc_shape=(1024, D), a_slab=True, after=tok_in, name=n("in_bwd_x_qkv"))
    d_h1b = _mm(d_gates, w["w_in"], grid=(S // 1024, 1, GATE_COLS // 768),
                a_spec=pl.BlockSpec((1024, 768), lambda i, j, k: (i, k)),
                b_spec=pl.BlockSpec((D, 768), lambda i, j, k: (0, k + QKV_COLS // 768)),
                out_shape=jax.ShapeDtypeStruct((S, D), F32),
                out_spec=pl.BlockSpec((1024, D), lambda i, j, k: (i, 0)),
                ca=1, cb=1, acc_shape=(1024, D), after=tok_in, name=n("in_bwd_x_gate"))
    dx, g["attn_pre_norm"] = _norm_bwd(sv["x"], w["attn_pre_norm"], [d_h1a, d_h1b], dx2, F32, n("attn_pre_bwd"))
    return dx, g, tok_in


def _local_step(x, target, ws, rel_bias, tok=None, on_grads=None):
    buckets = jnp.asarray(_bucket_tiles())
    bias = _bias_tiles(rel_bias, buckets, "bias_tiles").reshape(N_BIAS_HEADS // 2, 2, 2, BLK, 2 * BLK)
    saved = []
    gain0 = ws[0]["attn_pre_norm"] if tok is None else ws[0]["attn_pre_norm"] + tok
    h1 = _prenorm(x, gain0, "l0_attn_pre")
    for l in range(DEPTH):
        sv = _layer_fwd(x, h1, ws[l], bias, f"l{l}")
        saved.append(sv)
        g_next = ws[l + 1]["attn_pre_norm"] if l + 1 < DEPTH else ws[l]["attn_pre_norm"]
        x, h1 = _postnorm_res(sv["x2"], sv["fo"], ws[l]["ffn_post_norm"], g_next, f"l{l}_ffn_post")
    dy, loss_tile = _loss_head(x, target, "loss_head")
    grads = [None] * DEPTH
    tok = None
    for l in reversed(range(DEPTH)):
        on_part = None if on_grads is None else functools.partial(on_grads, l)
        dy, grads[l], tok = _layer_bwd(dy, saved[l], ws[l], bias, f"l{l}", tok, on_part)
    g_rel = _bias_grad([grads[l]["bias_g"] for l in range(DEPTH)], buckets, "bias_grad")[:, :N_BIAS_HEADS]
    return loss_tile[0, 0], dy, grads, g_rel


def _coords():
    return lax.axis_index("x"), lax.axis_index("y"), lax.axis_index("c")


def _peer(rel):
    x, y, c = _coords()
    return (1 - x if rel & 4 else x, 1 - y if rel & 2 else y, 1 - c if rel & 1 else c)


def _exchange(srcs, dst_shapes, src_win, dst_win, name, after=None):
    nt = len(srcs)
    extra = [] if after is None else [after]

    def body(*refs):
        src_refs, dst_refs = refs[:nt], refs[nt + len(extra):2 * nt + len(extra)]
        send_sems, recv_sems, local_sems = refs[2 * nt + len(extra):]
        x, y, c = _coords()
        me = 4 * x + 2 * y + c
        locals_ = []
        for t in range(nt):
            cp = pltpu.make_async_copy(src_win(t, src_refs[t], me), dst_win(t, dst_refs[t], me), local_sems.at[t])
            cp.start()
            locals_.append(cp)
        sends = []
        for rel in range(1, NDEV):
            px, py, pc = _peer(rel)
            q = 4 * px + 2 * py + pc
            for t in range(nt):
                cp = pltpu.make_async_remote_copy(
                    src_ref=src_win(t, src_refs[t], q), dst_ref=dst_win(t, dst_refs[t], me),
                    send_sem=send_sems.at[rel - 1, t], recv_sem=recv_sems.at[rel - 1, t],
                    device_id=(px, py, pc), device_id_type=MESH)
                cp.start()
                sends.append(cp)
        for rel in range(1, NDEV):
            px, py, pc = _peer(rel)
            q = 4 * px + 2 * py + pc
            for t in range(nt):
                pltpu.make_async_remote_copy(
                    src_ref=src_win(t, src_refs[t], me), dst_ref=dst_win(t, dst_refs[t], q),
                    send_sem=send_sems.at[rel - 1, t], recv_sem=recv_sems.at[rel - 1, t],
                    device_id=(px, py, pc), device_id_type=MESH).wait_recv()
        for cp in sends:
            cp.wait_send()
        for cp in locals_:
            cp.wait()

    return pl.pallas_call(
        body, in_specs=[ANY] * (nt + len(extra)), out_specs=[ANY] * nt, out_shape=dst_shapes,
        scratch_shapes=[pltpu.SemaphoreType.DMA((NDEV - 1, nt)), pltpu.SemaphoreType.DMA((NDEV - 1, nt)),
                        pltpu.SemaphoreType.DMA((nt,))],
        name=name)(*srcs, *extra)


BIG = (("w_in", 1, 864), ("w_br_a", 1, 128), ("w_br_b", 1, 128), ("w_br_c", 1, 128), ("w_out", 0, 128),
       ("w_up", 1, 1024), ("w_down", 0, 512))


NBIG = len(BIG)
BIG_FULL = {"w_in": (D, IN_COLS), "w_br_a": (256, D), "w_br_b": (512, D), "w_br_c": (256, D), "w_out": (D, D),
            "w_up": (D, 2 * D_FF), "w_down": (D_FF, D)}
LAYER_GROUPS = (("in", (0,)), ("mix", (1, 2, 3, 4)), ("ffn", (5, 6)))

HBM_SPEC = pl.BlockSpec(memory_space=pltpu.HBM)
SEM_SPEC = pl.BlockSpec(memory_space=pltpu.SEMAPHORE)


def _hbm(a):
    return pltpu.with_memory_space_constraint(a, pltpu.HBM)


def _shard_window(t, ref, k):
    nm, ax, ext = BIG[t % NBIG]
    if nm == "w_in":
        return ref.at[k]
    off = pl.multiple_of(k * ext, ext)
    if ax == 0:
        return ref.at[pl.ds(off, ext), :]
    return ref.at[:, pl.ds(off, ext)]


def _whole(t, ref, k):
    return ref


def _slot(t, ref, k):
    return ref.at[k]


def _own_block_spec(t, rows, me_of):
    nm, ax, ext = BIG[t % NBIG]
    r, c = BIG_FULL[nm]
    if nm == "w_in":
        return pl.BlockSpec((None, rows, ext), lambda i, m: (me_of(m), i, 0))
    if ax == 0:
        return pl.BlockSpec((rows, c), lambda i, m: (me_of(m) * (ext // rows) + i, 0))
    return pl.BlockSpec((rows, ext), lambda i, m: (i, me_of(m)))


def _cast_own(t, shard, me_arr, name):
    nm, ax, ext = BIG[t % NBIG]
    nr, nc = shard.shape
    rows = min(nr, 256)
    shape = (NDEV, D, ext) if nm == "w_in" else BIG_FULL[nm]

    def body(m_ref, s_ref, o_ref):
        o_ref[...] = s_ref[...].astype(BF16)

    return pl.pallas_call(
        body, grid_spec=pltpu.PrefetchScalarGridSpec(
            num_scalar_prefetch=1, grid=(nr // rows,),
            in_specs=[pl.BlockSpec((rows, nc), lambda i, m: (i, 0))],
            out_specs=_own_block_spec(t, rows, lambda m: m[0])),
        out_shape=jax.ShapeDtypeStruct(shape, BF16), compiler_params=_cp("arbitrary"), name=name)(me_arr, shard)


ALL_RELS = tuple(range(1, NDEV))
NEAR_RELS = (1, 2, 4, 6)
FAR_RELS = (2, 4, 6)


def _xchg_start(srcs, lands, groups, src_win, dst_win, after, name, rels=ALL_RELS, tids=None):
    ns = 0 if srcs is None else len(srcs)
    nt, ng = len(lands), len(groups)
    ins = ([] if srcs is None else list(srcs)) + list(lands)

    def body(*refs):
        src_refs, land_refs = refs[:ns], refs[ns:ns + nt]
        sems = refs[ns + nt + 1:ns + nt + 1 + 2 * ng]
        token = refs[-1]
        x, y, c = _coords()
        me = 4 * x + 2 * y + c
        for gi, grp in enumerate(groups):
            for j, t in enumerate(grp):
                tid = t if tids is None else tids[t]
                for ri, rel in enumerate(rels):
                    px, py, pc = _peer(rel)
                    q = 4 * px + 2 * py + pc
                    src = dst_win(tid, land_refs[t], me) if srcs is None else src_win(tid, src_refs[t], q)
                    pltpu.make_async_remote_copy(
                        src_ref=src, dst_ref=dst_win(tid, land_refs[t], me),
                        send_sem=sems[2 * gi].at[ri * len(grp) + j],
                        recv_sem=sems[2 * gi + 1].at[ri * len(grp) + j],
                        device_id=(px, py, pc), device_id_type=MESH).start()
        token[...] = jnp.zeros((8, LANES), F32)

    out_shape = []
    for grp in groups:
        out_shape += [pltpu.SemaphoreType.DMA((len(rels) * len(grp),))] * 2
    out_shape += [pltpu.HBM(a.shape, a.dtype) for a in ins]
    out_shape.append(jax.ShapeDtypeStruct((8, LANES), F32))
    outs = pl.pallas_call(
        body, in_specs=[HBM_SPEC] * len(ins) + [ANY],
        out_specs=[SEM_SPEC] * (2 * ng) + [HBM_SPEC] * len(ins) + [pl.BlockSpec(memory_space=pltpu.VMEM)],
        out_shape=out_shape, input_output_aliases={i: 2 * ng + i for i in range(len(ins))},
        compiler_params=pltpu.CompilerParams(has_side_effects=pltpu.SideEffectType.DATAFLOW_SIDE_EFFECTING),
        name=name)(*[_hbm(a) for a in ins], after)
    sems = [(outs[2 * gi], outs[2 * gi + 1]) for gi in range(ng)]
    thru = list(outs[2 * ng:2 * ng + len(ins)])
    return sems, (None if srcs is None else thru[:ns]), thru[ns:], outs[-1]


def _xchg_wait(sems, srcs, lands, tids, after, src_win, dst_win, name, rels=ALL_RELS):
    ns = 0 if srcs is None else len(srcs)
    n = len(lands)
    send_sem, recv_sem = sems
    ins = ([] if srcs is None else list(srcs)) + list(lands)

    def body(*refs):
        src_refs, land_refs = refs[:ns], refs[ns:ns + n]
        ssem, rsem = refs[ns + n], refs[ns + n + 1]
        x, y, c = _coords()
        me = 4 * x + 2 * y + c
        for j, t in enumerate(tids):
            for ri, rel in enumerate(rels):
                px, py, pc = _peer(rel)
                q = 4 * px + 2 * py + pc
                src = dst_win(t, land_refs[j], me) if srcs is None else src_win(t, src_refs[j], q)
                cp = pltpu.make_async_remote_copy(
                    src_ref=src, dst_ref=dst_win(t, land_refs[j], q),
                    send_sem=ssem.at[ri * n + j], recv_sem=rsem.at[ri * n + j],
                    device_id=(px, py, pc), device_id_type=MESH)
                cp.wait_send()
                cp.wait_recv()

    outs = pl.pallas_call(
        body, in_specs=[HBM_SPEC] * len(ins) + [SEM_SPEC, SEM_SPEC, ANY], out_specs=[HBM_SPEC] * len(ins),
        out_shape=[pltpu.HBM(a.shape, a.dtype) for a in ins],
        input_output_aliases={i: i for i in range(len(ins))},
        compiler_params=pltpu.CompilerParams(has_side_effects=pltpu.SideEffectType.DATAFLOW_SIDE_EFFECTING),
        name=name)(*ins, send_sem, recv_sem, after)
    return (None if srcs is None else list(outs[:ns])), list(outs[ns:])


def _gather_forward(sems_in, lands, groups, tids, after, dst_win, name):
    nt, ng = len(lands), len(groups)

    def body(*refs):
        land_refs = refs[:nt]
        in_sems = refs[nt:nt + 2 * ng]
        out_sems = refs[nt + 2 * ng + 1:nt + 4 * ng + 1]
        token = refs[-1]
        x, y, c = _coords()
        me = 4 * x + 2 * y + c
        sib = (x, y, 1 - c)
        for gi, grp in enumerate(groups):
            n = len(grp)
            for j, pos in enumerate(grp):
                t = tids[pos]
                for ri, rel in enumerate(NEAR_RELS):
                    px, py, pc = _peer(rel)
                    q = 4 * px + 2 * py + pc
                    cp = pltpu.make_async_remote_copy(
                        src_ref=dst_win(t, land_refs[pos], me), dst_ref=dst_win(t, land_refs[pos], q),
                        send_sem=in_sems[2 * gi].at[ri * n + j], recv_sem=in_sems[2 * gi + 1].at[ri * n + j],
                        device_id=(px, py, pc), device_id_type=MESH)
                    cp.wait_send()
                    cp.wait_recv()
            for j, pos in enumerate(grp):
                t = tids[pos]
                for fi, rel in enumerate(FAR_RELS):
                    px, py, pc = _peer(rel)
                    q = 4 * px + 2 * py + pc
                    win = dst_win(t, land_refs[pos], q)
                    pltpu.make_async_remote_copy(
                        src_ref=win, dst_ref=win,
                        send_sem=out_sems[2 * gi].at[fi * n + j], recv_sem=out_sems[2 * gi + 1].at[fi * n + j],
                        device_id=sib, device_id_type=MESH).start()
        token[...] = jnp.zeros((8, LANES), F32)

    out_shape = []
    for grp in groups:
        out_shape += [pltpu.SemaphoreType.DMA((len(FAR_RELS) * len(grp),))] * 2
    out_shape += [pltpu.HBM(a.shape, a.dtype) for a in lands]
    out_shape.append(jax.ShapeDtypeStruct((8, LANES), F32))
    flat_sems = [s for pair in sems_in for s in pair]
    outs = pl.pallas_call(
        body, in_specs=[HBM_SPEC] * nt + [SEM_SPEC] * (2 * ng) + [ANY],
        out_specs=[SEM_SPEC] * (2 * ng) + [HBM_SPEC] * nt + [pl.BlockSpec(memory_space=pltpu.VMEM)],
        out_shape=out_shape, input_output_aliases={i: 2 * ng + i for i in range(nt)},
        compiler_params=pltpu.CompilerParams(has_side_effects=pltpu.SideEffectType.DATAFLOW_SIDE_EFFECTING),
        name=name)(*[_hbm(a) for a in lands], *flat_sems, after)
    sems = [(outs[2 * gi], outs[2 * gi + 1]) for gi in range(ng)]
    return sems, list(outs[2 * ng:2 * ng + nt]), outs[-1]


class _Weights:
    def __init__(self, ready, pending=None):
        self.ready = dict(ready)
        self.pending = dict(pending or {})

    def __getitem__(self, k):
        return self.ready[k]

    def need(self, group, after):
        fn = self.pending.pop(group, None)
        if fn is not None:
            self.ready.update(fn(after))


def _adamw_math(w, g, m, v):
    m2 = ADAM_B1 * m + (1.0 - ADAM_B1) * g
    v2 = ADAM_B2 * v + (1.0 - ADAM_B2) * (g * g)
    m_hat = m2 / (1.0 - ADAM_B1 ** ADAM_STEP)
    v_hat = v2 / (1.0 - ADAM_B2 ** ADAM_STEP)
    delta = -ADAM_LR * (m_hat / (jnp.sqrt(v_hat) + ADAM_EPS) + ADAM_WD * w)
    return delta, m2, v2


def _adamw(t, parts, own, me_arr, w, m, v, layer, prev, rows, name):
    nl, nr, nc = w.shape

    def body(me_ref, p_ref, own_ref, w_ref, m_ref, v_ref, *rest):
        g_ref, d_ref, m2_ref, v2_ref = rest[-4:]
        me = me_ref[0]
        g = None
        for k in range(NDEV):
            term = jnp.where(me == k, own_ref[...], p_ref[k]).astype(F32)
            g = term if g is None else g + term
        delta, m2, v2 = _adamw_math(w_ref[...], g, m_ref[...], v_ref[...])
        g_ref[...] = g
        d_ref[...] = delta
        m2_ref[...] = m2
        v2_ref[...] = v2

    blk = pl.BlockSpec((None, rows, nc), lambda i, mm: (layer, i, 0))
    pblk = pl.BlockSpec((NDEV, rows, nc), lambda i, mm: (0, i, 0))
    extra = [] if prev is None else list(prev)
    return pl.pallas_call(
        body, grid_spec=pltpu.PrefetchScalarGridSpec(
            num_scalar_prefetch=1, grid=(nr // rows,),
            in_specs=[pblk, _own_block_spec(t, rows, lambda mm: mm[0]), blk, blk, blk] + [ANY] * len(extra),
            out_specs=[blk] * 4),
        out_shape=[jax.ShapeDtypeStruct(w.shape, F32)] * 4,
        input_output_aliases={6 + k: k for k in range(len(extra))},
        compiler_params=_cp("arbitrary"), name=name)(me_arr, parts, own, w, m, v, *extra)


SMALL_REPL = (("rel_bias", NUM_BUCKETS * N_BIAS_HEADS), ("attn_pre_norm", DEPTH * D), ("sinks", DEPTH * 8),
              ("attn_post_norm", DEPTH * D), ("ffn_pre_norm", DEPTH * D), ("conv_b", DEPTH * 2 * D_FF),
              ("ffn_post_norm", DEPTH * D))
SMALL_SHARD = (("b_gate", (DEPTH, 3, D), 128), ("conv_w", (DEPTH, 3, 2 * D_FF), 1024))


def _pack(vecs):
    flat = jnp.concatenate([v.reshape(-1).astype(F32) for v in vecs])
    n = flat.shape[0]
    rows = -(-n // (8 * LANES)) * 8
    return jnp.pad(flat, (0, rows * LANES - n)).reshape(rows, LANES)


def _unpack(packed, sizes):
    flat = packed.reshape(-1)
    out, off = [], 0
    for sz in sizes:
        out.append(flat[off:off + sz])
        off += sz
    return out


ROWPACK = (("rel_bias", 32, 32, (NUM_BUCKETS, N_BIAS_HEADS)), ("sinks", 8, 8, (DEPTH, 8)),
           ("attn_pre_norm", 16, 16, (DEPTH, D)), ("attn_post_norm", 16, 16, (DEPTH, D)),
           ("ffn_pre_norm", 16, 16, (DEPTH, D)), ("ffn_post_norm", 16, 16, (DEPTH, D)),
           ("conv_b", 128, 128, (DEPTH, 2 * D_FF)), ("b_gate", 48, 8, (DEPTH, 3, 128)),
           ("conv_w", 384, 48, (DEPTH, 3, 1024)))
ROWS_FULL = sum(r for _, r, _, _ in ROWPACK)
ROWS_OWN = sum(r for _, _, r, _ in ROWPACK)


def _as_rows(a, rows):
    a = a.astype(F32)
    if a.shape[-1] < LANES:
        a = jnp.pad(a.reshape(-1, a.shape[-1]), ((0, 0), (0, LANES - a.shape[-1])))
    a = a.reshape(-1, LANES)
    return jnp.pad(a, ((0, rows - a.shape[0]), (0, 0)))


def _rowpack(arrs, own):
    return jnp.concatenate([_as_rows(arrs[nm], ro if own else rf) for nm, rf, ro, _ in ROWPACK], axis=0)


def _small_update(parts, w, m, v, me_arr, name):
    nsm = len(ROWPACK)

    def body(me_ref, p_ref, w_ref, m_ref, v_ref, *rest):
        outs = rest[:4 * nsm]
        gfull, g_s, d_s, m_s, v_s = rest[4 * nsm:]
        me = me_ref[0]
        g = p_ref[0]
        for k in range(1, NDEV):
            g = g + p_ref[k]
        gfull[...] = g
        of, oo = 0, 0
        for nm, rf, ro, _ in ROWPACK:
            if nm == "b_gate":
                g_s[oo:oo + ro, :] = jnp.zeros((ro, LANES), F32)
                for r in range(DEPTH * 3):
                    g_s[oo + r:oo + r + 1, :] = gfull[pl.ds(of + r * NDEV + me, 1), :]
            elif nm == "conv_w":
                for r in range(DEPTH * 3):
                    g_s[oo + r * 8:oo + r * 8 + 8, :] = gfull[pl.ds(pl.multiple_of(of + r * 64 + me * 8, 8), 8), :]
            else:
                g_s[oo:oo + ro, :] = gfull[of:of + rf, :]
            of, oo = of + rf, oo + ro
        delta, m2, v2 = _adamw_math(w_ref[...], g_s[...], m_ref[...], v_ref[...])
        d_s[...] = delta
        m_s[...] = m2
        v_s[...] = v2
        for kind, src in enumerate((g_s, d_s, m_s, v_s)):
            oo = 0
            for idx, (nm, rf, ro, shp) in enumerate(ROWPACK):
                o_ref = outs[kind * nsm + idx]
                if nm in ("rel_bias", "sinks"):
                    o_ref[...] = src[oo:oo + shp[0], 0:shp[1]]
                elif nm == "b_gate":
                    for l in range(DEPTH):
                        o_ref[l] = src[oo + 3 * l:oo + 3 * l + 3, :]
                elif nm == "conv_w":
                    for l in range(DEPTH):
                        for k in range(8):
                            o_ref[l, :, k * LANES:(k + 1) * LANES] = src[pl.ds(oo + 24 * l + k, 3, stride=8), :]
                else:
                    per = shp[1] // LANES
                    for k in range(per):
                        o_ref[:, k * LANES:(k + 1) * LANES] = src[pl.ds(oo + k, DEPTH, stride=per), :]
                oo += ro

    vm = pl.BlockSpec(memory_space=pltpu.VMEM)
    shapes = [jax.ShapeDtypeStruct(shp, F32) for _ in range(4) for _, _, _, shp in ROWPACK]
    outs = pl.pallas_call(
        body, in_specs=[SMEM, vm, vm, vm, vm], out_specs=[vm] * (4 * nsm), out_shape=shapes,
        scratch_shapes=[pltpu.VMEM((ROWS_FULL, LANES), F32)] + [pltpu.VMEM((ROWS_OWN, LANES), F32)] * 4,
        name=name)(me_arr, parts, w, m, v)
    names = [nm for nm, _, _, _ in ROWPACK]
    return [dict(zip(names, outs[kind * nsm:(kind + 1) * nsm])) for kind in range(4)]


def kernel(x, rel_bias, attn_pre_norm, w_in, b_gate, sinks, w_br_a, w_br_b, w_br_c, w_out, attn_post_norm, ffn_pre_norm, w_up, conv_w, conv_b, w_down, ffn_post_norm, loss_target, m_rel_bias, m_attn_pre_norm, m_w_in, m_b_gate, m_sinks, m_w_br_a, m_w_br_b, m_w_br_c, m_w_out, m_attn_post_norm, m_ffn_pre_norm, m_w_up, m_conv_w, m_conv_b, m_w_down, m_ffn_post_norm, v_rel_bias, v_attn_pre_norm, v_w_in, v_b_gate, v_sinks, v_w_br_a, v_w_br_b, v_w_br_c, v_w_out, v_attn_post_norm, v_ffn_pre_norm, v_w_up, v_conv_w, v_conv_b, v_w_down, v_ffn_post_norm):
    P = dict(rel_bias=rel_bias, attn_pre_norm=attn_pre_norm, w_in=w_in, b_gate=b_gate, sinks=sinks, w_br_a=w_br_a,
             w_br_b=w_br_b, w_br_c=w_br_c, w_out=w_out, attn_post_norm=attn_post_norm, ffn_pre_norm=ffn_pre_norm,
             w_up=w_up, conv_w=conv_w, conv_b=conv_b, w_down=w_down, ffn_post_norm=ffn_post_norm)
    M = dict(rel_bias=m_rel_bias, attn_pre_norm=m_attn_pre_norm, w_in=m_w_in, b_gate=m_b_gate, sinks=m_sinks,
             w_br_a=m_w_br_a, w_br_b=m_w_br_b, w_br_c=m_w_br_c, w_out=m_w_out, attn_post_norm=m_attn_post_norm,
             ffn_pre_norm=m_ffn_pre_norm, w_up=m_w_up, conv_w=m_conv_w, conv_b=m_conv_b, w_down=m_w_down,
             ffn_post_norm=m_ffn_post_norm)
    V = dict(rel_bias=v_rel_bias, attn_pre_norm=v_attn_pre_norm, w_in=v_w_in, b_gate=v_b_gate, sinks=v_sinks,
             w_br_a=v_w_br_a, w_br_b=v_w_br_b, w_br_c=v_w_br_c, w_out=v_w_out, attn_post_norm=v_attn_post_norm,
             ffn_pre_norm=v_ffn_pre_norm, w_up=v_w_up, conv_w=v_conv_w, conv_b=v_conv_b, w_down=v_w_down,
             ffn_post_norm=v_ffn_post_norm)
    xi, yi, ci = _coords()
    me = 4 * xi + 2 * yi + ci

    me_arr = me.astype(jnp.int32).reshape(1)

    small_w = _pack([b_gate.reshape(-1), conv_w.reshape(-1)])
    (small_w_all,) = _exchange([small_w], [jax.ShapeDtypeStruct((NDEV,) + small_w.shape, F32)],
                               _whole, _slot, "gather_small_weights")

    groups = [tuple(l * NBIG + t for t in tids) for l in range(DEPTH) for _, tids in LAYER_GROUPS]
    cast = lambda i, m=me_arr: _cast_own(i, P[BIG[i % NBIG][0]][i // NBIG], m, f"gather_own_l{i // NBIG}_{BIG[i % NBIG][0]}")
    first = list(groups[0])
    rest = [i for grp in groups[1:] for i in grp]
    sems0, _, lands0, tok_first = _xchg_start(None, [cast(i) for i in first], [tuple(range(len(first)))], None,
                                              _shard_window, small_w_all, "gather_start_first", rels=NEAR_RELS, tids=first)
    where_rest = {tid: k for k, tid in enumerate(rest)}
    me_rest = me_arr + tok_first[0, 0:1].astype(jnp.int32)
    sems1, _, lands1, g_tok = _xchg_start(None, [cast(i, me_rest) for i in rest],
                                          [tuple(where_rest[i] for i in grp) for grp in groups[1:]], None,
                                          _shard_window, lands0[0], "gather_start_rest", rels=NEAR_RELS, tids=rest)
    g_sems = list(sems0) + list(sems1)
    tok0 = g_tok[0:1, 0:1]
    lands_now = [None] * (DEPTH * NBIG)
    for i, a in zip(first + rest, list(lands0) + list(lands1)):
        lands_now[i] = a
    fwd_sems = {}
    fwd_plan = {0: (0,), 1: (1, 2), 3: (3, 4, 5)}

    def gather_waiter(gi, l, gname, tids):
        def wait(after):
            if gi in fwd_plan:
                gis = fwd_plan[gi]
                flat = [i for g2 in gis for i in groups[g2]]
                where = {tid: k for k, tid in enumerate(flat)}
                fs, new_lands, ftok = _gather_forward(
                    [g_sems[g2] for g2 in gis], [lands_now[i] for i in flat],
                    [[where[i] for i in groups[g2]] for g2 in gis], flat, after, _shard_window, f"gather_forward_{gi}")
                for g2, s in zip(gis, fs):
                    fwd_sems[g2] = s
                for i, a in zip(flat, new_lands):
                    lands_now[i] = a
                after = ftok
            ids = [l * NBIG + t for t in tids]
            _, got = _xchg_wait(fwd_sems[gi], None, [lands_now[i] for i in ids], ids, after,
                                None, _shard_window, f"gather_wait_l{l}_{gname}", rels=FAR_RELS)
            out = {}
            for t, arr in zip(tids, got):
                nm = BIG[t][0]
                out[nm] = jnp.transpose(arr, (1, 0, 2)).reshape(D, IN_COLS) if nm == "w_in" else arr
            return out
        return wait

    pending = [{gname: gather_waiter(l * len(LAYER_GROUPS) + k, l, gname, tids)
                for k, (gname, tids) in enumerate(LAYER_GROUPS)} for l in range(DEPTH)]
    nbg = DEPTH * 3 * 128
    ncw = DEPTH * 3 * 1024
    flat_all = small_w_all.reshape(NDEV, -1)
    b_gate_full = jnp.transpose(flat_all[:, :nbg].reshape(NDEV, DEPTH, 3, 128), (1, 2, 0, 3)).reshape(DEPTH, 3, D)
    conv_w_full = jnp.transpose(flat_all[:, nbg:nbg + ncw].reshape(NDEV, DEPTH, 3, 1024), (1, 2, 0, 3)).reshape(DEPTH, 3, 2 * D_FF)

    ws = []
    for l in range(DEPTH):
        ws.append(_Weights(dict(
            b_gate=b_gate_full[l], conv_w=conv_w_full[l].reshape(3, 2, D_FF), conv_b=conv_b[l].reshape(2, D_FF),
            sinks=sinks[l].reshape(1, 8),
            attn_pre_norm=attn_pre_norm[l].reshape(1, D), attn_post_norm=attn_post_norm[l].reshape(1, D),
            ffn_pre_norm=ffn_pre_norm[l].reshape(1, D), ffn_post_norm=ffn_post_norm[l].reshape(1, D)), pending[l]))

    rs = {}

    group_tids = dict(LAYER_GROUPS)

    def start_scatter(l, gname, grads_l):
        tids = group_tids[gname]
        blocks, lands_rs = [], []
        for t in tids:
            nm, ax, ext = BIG[t]
            gfull = grads_l[nm].astype(BF16)
            if nm == "w_in":
                gfull = jnp.transpose(gfull.reshape(D, NDEV, ext), (1, 0, 2))
                shp = (NDEV, D, ext)
            else:
                shp = (NDEV, ext, gfull.shape[1]) if ax == 0 else (NDEV, gfull.shape[0], ext)
            blocks.append(gfull)
            lands_rs.append(lax.empty(shp, BF16))
        local = list(range(len(tids)))
        win = lambda j, ref, k: _shard_window(tids[j], ref, k)
        sems, s_thru, l_thru, tok = _xchg_start(blocks, lands_rs, [tuple(local)], win, _slot, me_arr,
                                                f"scatter_start_l{l}_{gname}")
        rs[(l, gname)] = (sems[0], s_thru, l_thru, win, local)
        return tok[0:1, 0:1]

    loss_local, grad_x, grads, g_rel = _local_step(x[0], loss_target[0], ws, rel_bias, tok0, start_scatter)
    loss = lax.psum(loss_local, ("x", "y", "c"))

    stack = lambda nm: jnp.stack([grads[l][nm] for l in range(DEPTH)], axis=0)
    small_names = [nm for nm, _ in SMALL_REPL] + [nm for nm, _, _ in SMALL_SHARD]
    small_g = {"rel_bias": g_rel}
    for nm in small_names[1:]:
        small_g[nm] = stack(nm)
    small_packed = _rowpack(small_g, own=False)

    out_g, out_d, out_m, out_v = {}, {}, {}, {}
    prev = {nm: None for nm, _, _ in BIG}
    todo = [(l, gname) for l in reversed(range(DEPTH)) for gname in ("ffn", "mix", "in")]
    after, small_parts = grad_x, None
    for l, gname in todo:
        if (l, gname) == todo[-1]:
            (small_parts,) = _exchange([small_packed], [jax.ShapeDtypeStruct((NDEV,) + small_packed.shape, F32)],
                                       _whole, _slot, "gather_small_grads", after=after)
            after = small_parts
        sems, s_thru, l_thru, win, local = rs[(l, gname)]
        owns, parts = _xchg_wait(sems, s_thru, l_thru, local, after, win, _slot, f"scatter_wait_l{l}_{gname}")
        for t, own, prt in zip(group_tids[gname], owns, parts):
            nm = BIG[t][0]
            rows = {"w_in": 256, "w_up": 256, "w_down": 256}.get(nm, P[nm].shape[1])
            prev[nm] = _adamw(t, prt, own, me_arr, P[nm], M[nm], V[nm], l, prev[nm], rows, f"adamw_{nm}_l{l}")
            after = prev[nm][1]
    for nm, _, _ in BIG:
        out_g[nm], out_d[nm], out_m[nm], out_v[nm] = prev[nm]
    sm_g, sm_d, sm_m, sm_v = _small_update(small_parts, _rowpack(P, True), _rowpack(M, True), _rowpack(V, True),
                                           me_arr, "small_update")
    for dst, src in ((out_g, sm_g), (out_d, sm_d), (out_m, sm_m), (out_v, sm_v)):
        dst.update(src)

    order = ["rel_bias", "attn_pre_norm", "w_in", "b_gate", "sinks", "w_br_a", "w_br_b", "w_br_c", "w_out",
             "attn_post_norm", "ffn_pre_norm", "w_up", "conv_w", "conv_b", "w_down", "ffn_post_norm"]
    return (loss, grad_x[None], *[out_g[k] for k in order], *[out_d[k] for k in order],
            *[out_m[k] for k in order], *[out_v[k] for k in order])
```

```python
import functools
import math

import numpy as np
import jax
import jax.numpy as jnp
from jax import lax
from jax.experimental import pallas as pl
from jax.experimental.pallas import tpu as pltpu

F32 = jnp.float32
BF16 = jnp.bfloat16

S = 2048
D = 1024
DEPTH = 2
NDEV = 8
HD = 64
BLK = 128
NB = S // BLK
A_GROUPS = ((128, 1), (512, 4), (2048, 16))
NUM_BUCKETS = 32
MAX_DISTANCE = 2048
N_BIAS_HEADS = 20
D_FF = 4096
IN_COLS = 6912
QKV_COLS = 3840
QKV_SLABS = QKV_COLS // 128
GATE_COLS = 3072
EPS = 1e-6
SCALE = HD ** -0.5
NEG = -1e30
LANES = 128

ADAM_LR = 0.001
ADAM_B1 = 0.9
ADAM_B2 = 0.999
ADAM_EPS = 1e-08
ADAM_WD = 0.01
ADAM_STEP = 10

VMEM_LIMIT = 56 * 1024 * 1024
MESH = pl.DeviceIdType.MESH
ANY = pl.BlockSpec(memory_space=pl.ANY)
SMEM = pl.BlockSpec(memory_space=pltpu.SMEM)


def _cp(*sem):
    return pltpu.CompilerParams(dimension_semantics=sem if sem else None, vmem_limit_bytes=VMEM_LIMIT)


def _dot(a, b, ca, cb):
    return lax.dot_general(a, b, (((ca,), (cb,)), ((), ())), preferred_element_type=F32)


def _mm(a, b, *, grid, a_spec, b_spec, out_shape, out_spec, ca, cb, acc_shape, name,
        a_slab=False, b_slab=False, out_slab=False, alias_out=None, after=None):
    nk = grid[2]

    def body(*refs):
        a_ref, b_ref = refs[0], refs[1]
        o_ref, acc_ref = refs[-2], refs[-1]
        k = pl.program_id(2)

        def load(ref, slab):
            if slab:
                return jnp.concatenate([ref[s] for s in range(ref.shape[0])], axis=1).astype(BF16)
            return ref[...].astype(BF16)

        def write(val):
            if out_slab:
                for s in range(o_ref.shape[0]):
                    o_ref[s] = val[:, s * LANES:(s + 1) * LANES].astype(o_ref.dtype)
            else:
                o_ref[...] = val.astype(o_ref.dtype)

        d = _dot(load(a_ref, a_slab), load(b_ref, b_slab), ca, cb)
        if nk == 1:
            write(d)
        elif direct:
            @pl.when(k == 0)
            def _():
                o_ref[...] = d

            @pl.when(k > 0)
            def _():
                o_ref[...] += d
        else:
            @pl.when(k == 0)
            def _():
                acc_ref[...] = d

            if nk > 2:
                @pl.when((k > 0) & (k < nk - 1))
                def _():
                    acc_ref[...] += d

            @pl.when(k == nk - 1)
            def _():
                write(acc_ref[...] + d)

    direct = (not out_slab) and out_shape.dtype == F32
    if nk == 1 or direct:
        acc_shape = (8, LANES)
    in_specs = [a_spec, b_spec]
    args = [a, b]
    aliases = {}
    if alias_out is not None:
        in_specs.append(ANY)
        args.append(alias_out)
        aliases = {2: 0}
    if after is not None:
        in_specs.append(ANY)
        args.append(after)
    return pl.pallas_call(
        body, grid=grid, in_specs=in_specs, out_specs=out_spec, out_shape=out_shape,
        scratch_shapes=[pltpu.VMEM(acc_shape, F32)], input_output_aliases=aliases,
        compiler_params=_cp("parallel", "parallel", "arbitrary"), name=name)(*args)


def _mm_nn(a, b, out_dtype, tm, tn, tk, name):
    m, kk = a.shape
    n = b.shape[1]
    return _mm(a, b, grid=(m // tm, n // tn, kk // tk),
               a_spec=pl.BlockSpec((tm, tk), lambda i, j, k: (i, k)),
               b_spec=pl.BlockSpec((tk, tn), lambda i, j, k: (k, j)),
               out_shape=jax.ShapeDtypeStruct((m, n), out_dtype),
               out_spec=pl.BlockSpec((tm, tn), lambda i, j, k: (i, j)),
               ca=1, cb=0, acc_shape=(tm, tn), name=name)


def _mm_nt(a, b, out_dtype, tm, tn, tk, name):
    m, kk = a.shape
    n = b.shape[0]
    return _mm(a, b, grid=(m // tm, n // tn, kk // tk),
               a_spec=pl.BlockSpec((tm, tk), lambda i, j, k: (i, k)),
               b_spec=pl.BlockSpec((tn, tk), lambda i, j, k: (j, k)),
               out_shape=jax.ShapeDtypeStruct((m, n), out_dtype),
               out_spec=pl.BlockSpec((tm, tn), lambda i, j, k: (i, j)),
               ca=1, cb=1, acc_shape=(tm, tn), name=name)


def _mm_tn(a, b, out_dtype, tm, tn, tk, name):
    kk, m = a.shape
    n = b.shape[1]
    return _mm(a, b, grid=(m // tm, n // tn, kk // tk),
               a_spec=pl.BlockSpec((tk, tm), lambda i, j, k: (k, i)),
               b_spec=pl.BlockSpec((tk, tn), lambda i, j, k: (k, j)),
               out_shape=jax.ShapeDtypeStruct((m, n), out_dtype),
               out_spec=pl.BlockSpec((tm, tn), lambda i, j, k: (i, j)),
               ca=0, cb=0, acc_shape=(tm, tn), name=name)


ROW_TILE = 256


def _rms(x, g):
    r = lax.rsqrt(jnp.mean(x * x, axis=-1, keepdims=True) + EPS)
    return x * r * g


def _prenorm(x, g, name):
    def body(x_ref, g_ref, o_ref):
        o_ref[...] = _rms(x_ref[...], g_ref[...]).astype(BF16)

    return pl.pallas_call(
        body, grid=(S // ROW_TILE,),
        in_specs=[pl.BlockSpec((ROW_TILE, D), lambda i: (i, 0)), pl.BlockSpec((1, D), lambda i: (0, 0))],
        out_specs=pl.BlockSpec((ROW_TILE, D), lambda i: (i, 0)),
        out_shape=jax.ShapeDtypeStruct((S, D), BF16), compiler_params=_cp("parallel"), name=name)(x, g)


def _postnorm_res(x, f, g_post, g_next, name):
    def body(x_ref, f_ref, gp_ref, gn_ref, xo_ref, ho_ref):
        xn = x_ref[...] + _rms(f_ref[...], gp_ref[...])
        xo_ref[...] = xn
        ho_ref[...] = _rms(xn, gn_ref[...]).astype(BF16)

    row = pl.BlockSpec((ROW_TILE, D), lambda i: (i, 0))
    vec = pl.BlockSpec((1, D), lambda i: (0, 0))
    return pl.pallas_call(
        body, grid=(S // ROW_TILE,), in_specs=[row, row, vec, vec], out_specs=[row, row],
        out_shape=[jax.ShapeDtypeStruct((S, D), F32), jax.ShapeDtypeStruct((S, D), BF16)],
        compiler_params=_cp("parallel"), name=name)(x, f, g_post, g_next)


def _norm_bwd(f, g, dys, res, out_dtype, name):
    ndy = len(dys)
    has_res = res is not None

    def body(*refs):
        f_ref, g_ref = refs[0], refs[1]
        dy_refs = refs[2:2 + ndy]
        res_ref = refs[2 + ndy] if has_res else None
        o_ref, dg_ref = refs[-2], refs[-1]
        fv = f_ref[...]
        dy = dy_refs[0][...].astype(F32)
        for r in dy_refs[1:]:
            dy = dy + r[...].astype(F32)
        r = lax.rsqrt(jnp.mean(fv * fv, axis=-1, keepdims=True) + EPS)
        n = fv * r
        dn = dy * g_ref[...]
        df = r * (dn - n * jnp.mean(dn * n, axis=-1, keepdims=True))
        if has_res:
            df = df + res_ref[...]
        o_ref[...] = df.astype(out_dtype)

        @pl.when(pl.program_id(0) == 0)
        def _():
            dg_ref[...] = jnp.zeros((1, D), F32)

        dg_ref[...] += jnp.sum(dy * n, axis=0, keepdims=True)

    row = pl.BlockSpec((ROW_TILE, D), lambda i: (i, 0))
    vec = pl.BlockSpec((1, D), lambda i: (0, 0))
    in_specs = [row, vec] + [row] * ndy + ([row] if has_res else [])
    args = [f, g] + list(dys) + ([res] if has_res else [])
    return pl.pallas_call(
        body, grid=(S // ROW_TILE,), in_specs=in_specs, out_specs=[row, vec],
        out_shape=[jax.ShapeDtypeStruct((S, D), out_dtype), jax.ShapeDtypeStruct((1, D), F32)],
        compiler_params=_cp("arbitrary"), name=name)(*args)


def _loss_head(y, target, name):
    def body(y_ref, t_ref, dy_ref, l_ref):
        e = y_ref[...] - t_ref[...]
        dy_ref[...] = e * (1.0 / D)

        @pl.when(pl.program_id(0) == 0)
        def _():
            l_ref[...] = jnp.zeros((8, LANES), F32)

        l_ref[...] += jnp.sum(e * e) * (0.5 / D)

    row = pl.BlockSpec((ROW_TILE, D), lambda i: (i, 0))
    return pl.pallas_call(
        body, grid=(S // ROW_TILE,), in_specs=[row, row],
        out_specs=[row, pl.BlockSpec((8, LANES), lambda i: (0, 0))],
        out_shape=[jax.ShapeDtypeStruct((S, D), F32), jax.ShapeDtypeStruct((8, LANES), F32)],
        compiler_params=_cp("arbitrary"), name=name)(y, target)


def _bucket_tiles():
    a = np.arange(BLK)[:, None]
    b = np.arange(2 * BLK)[None, :]
    dist = a + BLK - b
    out = np.zeros((4, 2, BLK, 2 * BLK), np.int32)
    cfg = [(w // d, d) for w, d in A_GROUPS] + [(BLK - 1, 1)]
    for gi, (max_dist, d) in enumerate(cfg):
        band = (dist >= 0) & (dist <= max_dist)
        tok = np.maximum(dist, 0) * d
        nf = np.maximum(tok, 1).astype(np.float32)
        max_exact = NUM_BUCKETS // 2
        large = max_exact + (np.log(nf / np.float32(max_exact)) / np.float32(math.log(MAX_DISTANCE / max_exact))
                             * np.float32(NUM_BUCKETS - max_exact)).astype(np.int32)
        large = np.minimum(large, NUM_BUCKETS - 1)
        bkt = np.where(tok < max_exact, tok, large).astype(np.int32)
        full = np.where(band, bkt, -1)
        out[gi, 1] = full
        out[gi, 0] = np.where(b >= BLK, full, -1)
    return out


def _bias_tiles(rel_bias, buckets, name):
    def body(tab_ref, bkt_ref, o_ref):
        h = pl.program_id(0)
        bkt = bkt_ref[...]
        acc = jnp.zeros(bkt.shape, F32)
        for bb in range(NUM_BUCKETS):
            acc = jnp.where(bkt == bb, tab_ref[bb, h], acc)
        o_ref[...] = jnp.where(bkt < 0, NEG, acc)

    return pl.pallas_call(
        body, grid=(N_BIAS_HEADS,),
        in_specs=[SMEM, pl.BlockSpec((None, 2, BLK, 2 * BLK), lambda h: (jnp.minimum(h // 4, 3), 0, 0, 0))],
        out_specs=pl.BlockSpec((None, 2, BLK, 2 * BLK), lambda h: (h, 0, 0, 0)),
        out_shape=jax.ShapeDtypeStruct((N_BIAS_HEADS, 2, BLK, 2 * BLK), F32),
        compiler_params=_cp("arbitrary"), name=name)(rel_bias, buckets)


def _bias_grad(gs, buckets, name):
    ng = len(gs)

    def body(*refs):
        g_refs = refs[:ng]
        bkt_ref, o_ref = refs[ng], refs[ng + 1]
        h = pl.program_id(0)
        g = g_refs[0][...]
        for r in g_refs[1:]:
            g = g + r[...]
        bkt = bkt_ref[...]
        row = lax.broadcasted_iota(jnp.int32, (NUM_BUCKETS, LANES), 0)
        lane = lax.broadcasted_iota(jnp.int32, (NUM_BUCKETS, LANES), 1)

        @pl.when(h == 0)
        def _():
            o_ref[...] = jnp.zeros((NUM_BUCKETS, LANES), F32)

        acc = o_ref[...]
        for bb in range(NUM_BUCKETS):
            s = jnp.sum(jnp.where(bkt == bb, g, 0.0))
            acc = jnp.where((row == bb) & (lane == h), s, acc)
        o_ref[...] = acc

    g_spec = pl.BlockSpec((None, BLK, 2 * BLK), lambda h: (h, 0, 0))
    return pl.pallas_call(
        body, grid=(N_BIAS_HEADS,),
        in_specs=[g_spec] * ng + [pl.BlockSpec((None, None, BLK, 2 * BLK), lambda h: (jnp.minimum(h // 4, 3), 1, 0, 0))],
        out_specs=pl.BlockSpec((NUM_BUCKETS, LANES), lambda h: (0, 0)),
        out_shape=jax.ShapeDtypeStruct((NUM_BUCKETS, LANES), F32),
        compiler_params=_cp("arbitrary"), name=name)(*gs, buckets)


def _to_class_major(src_ref, dst_refs, d, fn=None):
    ln = S // d
    for r in range(d):
        v = src_ref[pl.ds(r, ln, stride=d), :] if d > 1 else src_ref[...]
        outs = fn(v) if fn is not None else (v,) * len(dst_refs)
        for dst, o in zip(dst_refs, outs):
            dst[pl.ds(r * ln, ln), :] = o.astype(dst.dtype)


def _head_masks(rows):
    lane = lax.broadcasted_iota(jnp.int32, (rows, LANES), 1)
    return lane < HD, lane >= HD


def _split_heads(v):
    m0, m1 = _head_masks(v.shape[0])
    return jnp.where(m0, v, 0.0), jnp.where(m1, v, 0.0)


def _dup_head(v, hi):
    m0, _ = _head_masks(v.shape[0])
    r = pltpu.roll(v, HD, 1)
    return jnp.where(m0, jnp.where(hi, r, v), jnp.where(hi, v, r))


def _block_rows(b, d):
    nbc = NB // d
    i = b % nbc
    r = b // nbc
    has_prev = (i > 0).astype(jnp.int32)
    prev = pl.multiple_of(jnp.maximum(b - 1, 0) * BLK, BLK)
    nat = i * (BLK * d) + r
    return has_prev, prev, nat


def _lane_halves(v0, v1):
    lane = lax.broadcasted_iota(jnp.int32, (v0.shape[0], LANES), 1)
    return jnp.where(lane < HD, v0, v1)


def _band_fwd(proj, bias, sinks, *, d, q0, k0, v0, npairs, bias0, shared_kv, name):
    def body(sink_ref, q_ref, k_ref, v_ref, b_ref, num_ref, st_ref, qz0, qz1, ks, vs):
        p = pl.program_id(0)
        kv = (lambda v: (_dup_head(v, p >= 2),)) if shared_kv else None
        _to_class_major(q_ref, (qz0, qz1), d, lambda v: _split_heads(v * SCALE))
        _to_class_major(k_ref, (ks,), d, kv)
        _to_class_major(v_ref, (vs,), d, kv)
        lane = lax.broadcasted_iota(jnp.int32, (BLK, LANES), 1)

        def blk(b, carry):
            has_prev, prev, nat = _block_rows(b, d)
            cur = pl.multiple_of(b * BLK, BLK)
            k2 = jnp.concatenate([ks[pl.ds(prev, BLK), :], ks[pl.ds(cur, BLK), :]], axis=0)
            v2 = jnp.concatenate([vs[pl.ds(prev, BLK), :], vs[pl.ds(cur, BLK), :]], axis=0)
            nums, ms, ls = [], [], []
            for hh, qz in enumerate((qz0, qz1)):
                z = _dot(qz[pl.ds(cur, BLK), :], k2, 1, 1) + b_ref[hh, has_prev]
                m = jnp.max(z, axis=1, keepdims=True)
                e = jnp.exp(z - m)
                l = jnp.sum(e, axis=1, keepdims=True)
                num = _dot(e.astype(BF16), v2, 1, 0)
                if shared_kv:
                    sink = sink_ref[0, 2 * p + hh]
                    mx = jnp.maximum(m, sink)
                    c = jnp.exp(m - mx)
                    zden = l * c + jnp.exp(sink - mx)
                    num = num * (c / zden)
                    m = mx + jnp.log(zden)
                ls.append(l)
                ms.append(m)
                nums.append(num)
            num_t = jnp.where(lane < HD, nums[0], nums[1])
            if shared_kv:
                st_t = jnp.where(lane < HD, ms[0], ms[1])
            else:
                st_t = jnp.where(lane < 32, ms[0], jnp.where(lane < 64, ls[0], jnp.where(lane < 96, ms[1], ls[1])))
            if d > 1:
                num_ref[pl.ds(nat, BLK, stride=d), :] = num_t
                st_ref[pl.ds(nat, BLK, stride=d), :] = st_t
            else:
                num_ref[pl.ds(cur, BLK), :] = num_t
                st_ref[pl.ds(cur, BLK), :] = st_t
            return carry

        lax.fori_loop(0, NB, blk, 0, unroll=8)

    slab = lambda off, per_pair: pl.BlockSpec((None, S, LANES), (lambda p: (off + p, 0, 0)) if per_pair else (lambda p: (off, 0, 0)))
    out = pl.BlockSpec((None, S, LANES), lambda p: (p, 0, 0))
    return pl.pallas_call(
        body, grid=(npairs,),
        in_specs=[SMEM, slab(q0, True), slab(k0, not shared_kv), slab(v0, not shared_kv),
                  pl.BlockSpec((None, 2, 2, BLK, 2 * BLK), lambda p: (bias0 + p, 0, 0, 0, 0))],
        out_specs=[out, out],
        out_shape=[jax.ShapeDtypeStruct((npairs, S, LANES), F32)] * 2,
        scratch_shapes=[pltpu.VMEM((S, LANES), BF16)] * 4,
        compiler_params=_cp("arbitrary"), name=name)(sinks, proj, proj, proj, bias)


def _combine_a(nums, stats, name):
    rt = 512

    def body(n0, n1, n2, s0, s1, s2, o_ref, l_ref):
        n_refs, s_refs = (n0, n1, n2), (s0, s1, s2)
        outs, lses = [], []
        for hh in range(2):
            ms = [s[:, 64 * hh:64 * hh + 1] for s in s_refs]
            ls = [s[:, 64 * hh + 32:64 * hh + 33] for s in s_refs]
            mx = jnp.maximum(jnp.maximum(ms[0], ms[1]), ms[2])
            cs = [jnp.exp(m - mx) for m in ms]
            z = cs[0] * ls[0] + cs[1] * ls[1] + cs[2] * ls[2]
            acc = cs[0] * n_refs[0][:, hh * HD:(hh + 1) * HD]
            acc = acc + cs[1] * n_refs[1][:, hh * HD:(hh + 1) * HD]
            acc = acc + cs[2] * n_refs[2][:, hh * HD:(hh + 1) * HD]
            outs.append(acc / z)
            lses.append(mx + jnp.log(z))
        o_ref[...] = jnp.concatenate(outs, axis=1)
        l_ref[...] = _lane_halves(lses[0], lses[1])

    spec = pl.BlockSpec((None, rt, LANES), lambda p, i: (p, i, 0))
    return pl.pallas_call(
        body, grid=(2, S // rt), in_specs=[spec] * 6, out_specs=[spec, spec],
        out_shape=[jax.ShapeDtypeStruct((2, S, LANES), F32)] * 2,
        compiler_params=_cp("parallel", "parallel"), name=name)(*nums, *stats)


def _band_bwd(proj, bias, o, do, lse, sinks, *, d, q0, k0, v0, npairs, bias0, shared_kv, name):
    nkv = 1 if shared_kv else npairs

    def body(sink_ref, q_ref, k_ref, v_ref, b_ref, o_ref, do_ref, lse_ref,
             dq_ref, dk_ref, dv_ref, g_ref, ds_ref,
             qz0, qz1, ks, vs, doz0, doz1, ls0, ls1, dls0, dls1, stage, dq_nat, dk_cm, dv_cm, kv_nat, dk_acc, dv_acc):
        p = pl.program_id(0)
        m0, m1 = _head_masks(S)
        prod = do_ref[...] * o_ref[...]
        dl0 = jnp.sum(jnp.where(m0, prod, 0.0), axis=1, keepdims=True)
        dl1 = jnp.sum(jnp.where(m1, prod, 0.0), axis=1, keepdims=True)
        if shared_kv:
            row8 = lax.broadcasted_iota(jnp.int32, (8, LANES), 0)
            lane8 = lax.broadcasted_iota(jnp.int32, (8, LANES), 1)
            t = jnp.zeros((8, LANES), F32)
            lv = lse_ref[...]
            for hh in range(2):
                sink = sink_ref[0, 2 * p + hh]
                ps = jnp.exp(sink - lv[:, 64 * hh:64 * hh + 1])
                dsink = -jnp.sum(ps * (dl0 if hh == 0 else dl1))
                t = jnp.where((row8 == 0) & (lane8 == hh), dsink, t)
            ds_ref[...] = t
        else:
            ds_ref[...] = jnp.zeros((8, LANES), F32)
        kv = (lambda v: (_dup_head(v, p >= 2),)) if shared_kv else None
        _to_class_major(q_ref, (qz0, qz1), d, lambda v: _split_heads(v * SCALE))
        _to_class_major(k_ref, (ks,), d, kv)
        _to_class_major(v_ref, (vs,), d, kv)
        _to_class_major(do_ref, (doz0, doz1), d, _split_heads)
        def spread(v):
            a0, a1 = _head_masks(v.shape[0])
            r = pltpu.roll(v, HD, 1)
            return jnp.where(a0, v, r), jnp.where(a1, v, r)

        _to_class_major(lse_ref, (ls0, ls1), d, spread)
        stage[...] = jnp.where(m0, dl0, dl1)
        _to_class_major(stage, (dls0, dls1), d, spread)

        dk_cm[...] = jnp.zeros((S, LANES), F32)
        dv_cm[...] = jnp.zeros((S, LANES), F32)
        g_ref[...] = jnp.zeros((2, BLK, 2 * BLK), F32)
        lane = lax.broadcasted_iota(jnp.int32, (BLK, LANES), 1)

        def blk(b, carry):
            has_prev, prev, nat = _block_rows(b, d)
            cur = pl.multiple_of(b * BLK, BLK)
            k2 = jnp.concatenate([ks[pl.ds(prev, BLK), :], ks[pl.ds(cur, BLK), :]], axis=0)
            v2 = jnp.concatenate([vs[pl.ds(prev, BLK), :], vs[pl.ds(cur, BLK), :]], axis=0)
            dqs, dks, dvs = [], [], []
            for hh, (qz, doz, lsr, dlr) in enumerate(((qz0, doz0, ls0, dls0), (qz1, doz1, ls1, dls1))):
                qb = qz[pl.ds(cur, BLK), :]
                dob = doz[pl.ds(cur, BLK), :]
                lb = lsr[pl.ds(cur, BLK), :]
                dlb = dlr[pl.ds(cur, BLK), :]
                z = _dot(qb, k2, 1, 1) + b_ref[hh, has_prev]
                pr = jnp.exp(z - jnp.concatenate([lb, lb], axis=1))
                dp = _dot(dob, v2, 1, 1)
                dz = pr * (dp - jnp.concatenate([dlb, dlb], axis=1))
                g_ref[hh] += dz
                dzb = dz.astype(BF16)
                dqs.append(_dot(dzb, k2, 1, 0))
                dks.append(_dot(dzb, qb, 0, 0))
                dvs.append(_dot(pr.astype(BF16), dob, 0, 0))
            dq_t = jnp.where(lane < HD, dqs[0], dqs[1]) * SCALE
            dk_t = dks[0] + dks[1]
            dv_t = dvs[0] + dvs[1]
            dk_cm[pl.ds(prev, BLK), :] += dk_t[:BLK]
            dk_cm[pl.ds(cur, BLK), :] += dk_t[BLK:]
            dv_cm[pl.ds(prev, BLK), :] += dv_t[:BLK]
            dv_cm[pl.ds(cur, BLK), :] += dv_t[BLK:]
            if d > 1:
                dq_nat[pl.ds(nat, BLK, stride=d), :] = dq_t
            else:
                dq_nat[pl.ds(cur, BLK), :] = dq_t
            return carry

        lax.fori_loop(0, NB, blk, 0, unroll=8)
        dq_ref[...] = dq_nat[...].astype(BF16)

        def from_class_major(src, dst_ref):
            if d == 1:
                dst_ref[...] = src[...].astype(BF16)
            else:
                ln = S // d
                for r in range(d):
                    kv_nat[pl.ds(r, ln, stride=d), :] = src[pl.ds(r * ln, ln), :]
                dst_ref[...] = kv_nat[...].astype(BF16)

        if not shared_kv:
            from_class_major(dk_cm, dk_ref)
            from_class_major(dv_cm, dv_ref)
        else:
            @pl.when(p == 0)
            def _():
                dk_acc[...] = jnp.zeros((S, LANES), F32)
                dv_acc[...] = jnp.zeros((S, LANES), F32)

            mine = m1 == (p >= 2)
            for cm, acc in ((dk_cm, dk_acc), (dv_cm, dv_acc)):
                val = cm[...]
                acc[...] += jnp.where(mine, val + pltpu.roll(val, HD, 1), 0.0)

            @pl.when(p == npairs - 1)
            def _():
                from_class_major(dk_acc, dk_ref)
                from_class_major(dv_acc, dv_ref)

    slab = lambda off, per_pair: pl.BlockSpec((None, S, LANES), (lambda p: (off + p, 0, 0)) if per_pair else (lambda p: (off, 0, 0)))
    pair = pl.BlockSpec((None, S, LANES), lambda p: (p, 0, 0))
    kv_out = pair if not shared_kv else pl.BlockSpec((None, S, LANES), lambda p: (0, 0, 0))
    return pl.pallas_call(
        body, grid=(npairs,),
        in_specs=[SMEM, slab(q0, True), slab(k0, not shared_kv), slab(v0, not shared_kv),
                  pl.BlockSpec((None, 2, 2, BLK, 2 * BLK), lambda p: (bias0 + p, 0, 0, 0, 0)),
                  pair, pair, pair],
        out_specs=[pair, kv_out, kv_out,
                   pl.BlockSpec((None, 2, BLK, 2 * BLK), lambda p: (p, 0, 0, 0)),
                   pl.BlockSpec((None, 8, LANES), lambda p: (p, 0, 0))],
        out_shape=[jax.ShapeDtypeStruct((npairs, S, LANES), BF16),
                   jax.ShapeDtypeStruct((nkv, S, LANES), BF16),
                   jax.ShapeDtypeStruct((nkv, S, LANES), BF16),
                   jax.ShapeDtypeStruct((npairs, 2, BLK, 2 * BLK), F32),
                   jax.ShapeDtypeStruct((npairs, 8, LANES), F32)],
        scratch_shapes=[pltpu.VMEM((S, LANES), BF16)] * 6 + [pltpu.VMEM((S, LANES), F32)] * 11,
        compiler_params=_cp("arbitrary"), name=name)(sinks, proj, proj, proj, bias, o, do, lse)


KC = 512
NSUB = KC // BLK
QB = 512
QPG = KC // QB


def _split2(x):
    hi = x.astype(BF16)
    lo = (x - hi.astype(F32)).astype(BF16)
    return hi, lo


def _tri_ones(cmp):
    jj = lax.broadcasted_iota(jnp.int32, (2 * BLK, BLK), 0) % BLK
    ss = lax.broadcasted_iota(jnp.int32, (2 * BLK, BLK), 1)
    return jnp.concatenate([cmp(jj, ss).astype(BF16), jnp.ones((2 * BLK, BLK), BF16)], axis=1)


def _sub_sums(x, tri1):
    n = x.shape[0]
    st = jnp.concatenate([x[:, s * BLK:(s + 1) * BLK] for s in range(NSUB)], axis=0)
    hi, lo = _split2(st)
    r = _dot(jnp.concatenate([hi, lo], axis=1), tri1, 1, 0)
    return ([r[s * n:(s + 1) * n, :BLK] for s in range(NSUB)], [r[s * n:(s + 1) * n, BLK:] for s in range(NSUB)])


def _log_sig_pair(z):
    lb = jnp.minimum(z, 0.0) - jnp.log1p(jnp.exp(-jnp.abs(z)))
    return lb, lb - z


QGROUPS = NB // NSUB


def _stick_fwd(proj, *, q0, k0, v0, name):
    def body(q_ref, k_ref, v_ref, o_ref, t_ref, qs, ks, vs):
        qs[...] = (q_ref[...] * SCALE).astype(BF16)
        ks[...] = k_ref[...].astype(BF16)
        vs[...] = v_ref[...].astype(BF16)
        tri1 = _tri_ones(lambda j, s: j > s)
        col = lax.broadcasted_iota(jnp.int32, (QB, KC), 1)
        rowi = lax.broadcasted_iota(jnp.int32, (QB, KC), 0)

        for qg in range(QGROUPS):
            def qblock(ii, carry0, qg=qg):
                t0 = pl.multiple_of((qg * QPG + ii) * QB, QB)
                qb = qs[pl.ds(t0, QB), :]
                accs = [jnp.zeros((QB, HD), F32)] * 2
                runs = [jnp.zeros((QB, BLK), F32)] * 2
                for c in reversed(range(qg + 1)):
                    s0 = c * KC
                    diag = c == qg
                    before = (s0 + col) < (t0 + rowi) if diag else None
                    for hh in range(2):
                        kh = ks[s0:s0 + KC, hh * HD:(hh + 1) * HD]
                        vh = vs[s0:s0 + KC, hh * HD:(hh + 1) * HD]
                        lb, lk = _log_sig_pair(_dot(qb[:, hh * HD:(hh + 1) * HD], kh, 1, 1))
                        if diag:
                            lk = jnp.where(before, lk, 0.0)
                        suf, tot = _sub_sums(lk, tri1)
                        ws, run = [], runs[hh]
                        for s in reversed(range(NSUB)):
                            ws.append(jnp.exp(lb[:, s * BLK:(s + 1) * BLK] + suf[s] + run))
                            run = run + tot[s]
                        w = jnp.concatenate(ws[::-1], axis=1)
                        if diag:
                            w = jnp.where(before, w, 0.0)
                        accs[hh] = accs[hh] + _dot(w.astype(BF16), vh, 1, 0)
                        runs[hh] = run
                o_ref[pl.ds(t0, QB), :] = jnp.concatenate(accs, axis=1)
                t_ref[pl.ds(t0, QB), :] = _lane_halves(runs[0], runs[1])
                return carry0

            lax.fori_loop(0, QPG, qblock, 0)

    slab = lambda off: pl.BlockSpec((None, S, LANES), lambda p: (off + p, 0, 0))
    out = pl.BlockSpec((None, S, LANES), lambda p: (p, 0, 0))
    return pl.pallas_call(
        body, grid=(2,), in_specs=[slab(q0), slab(k0), slab(v0)], out_specs=[out, out],
        out_shape=[jax.ShapeDtypeStruct((2, S, LANES), F32)] * 2,
        scratch_shapes=[pltpu.VMEM((S, LANES), BF16)] * 3,
        compiler_params=_cp("arbitrary"), name=name)(proj, proj, proj)


def _stick_bwd(proj, do, tot, *, q0, k0, v0, name):
    def body(q_ref, k_ref, v_ref, do_ref, t_ref, dq_ref, dk_ref, dv_ref, qs, ks, vs, dos, dk_acc, dv_acc):
        qs[...] = (q_ref[...] * SCALE).astype(BF16)
        ks[...] = k_ref[...].astype(BF16)
        vs[...] = v_ref[...].astype(BF16)
        dos[...] = do_ref[...].astype(BF16)
        dk_acc[...] = jnp.zeros((2, S, HD), F32)
        dv_acc[...] = jnp.zeros((2, S, HD), F32)
        tri_inc = _tri_ones(lambda j, s: j <= s)
        tri_exc = _tri_ones(lambda j, s: j < s)
        col = lax.broadcasted_iota(jnp.int32, (QB, KC), 1)
        rowi = lax.broadcasted_iota(jnp.int32, (QB, KC), 0)

        for qg in range(QGROUPS):
            def qblock(ii, carry0, qg=qg):
                t0 = pl.multiple_of((qg * QPG + ii) * QB, QB)
                qb = qs[pl.ds(t0, QB), :]
                dob = dos[pl.ds(t0, QB), :]
                tb = t_ref[pl.ds(t0, QB), :]
                dqs = [jnp.zeros((QB, HD), F32)] * 2
                pruns = [jnp.zeros((QB, BLK), F32)] * 2
                eruns = [jnp.zeros((QB, BLK), F32)] * 2
                for c in range(qg + 1):
                    s0 = c * KC
                    diag = c == qg
                    before = (s0 + col) < (t0 + rowi) if diag else None
                    for hh in range(2):
                        qh = qb[:, hh * HD:(hh + 1) * HD]
                        doh = dob[:, hh * HD:(hh + 1) * HD]
                        tt = tb[:, 64 * hh:64 * hh + 1]
                        kh = ks[s0:s0 + KC, hh * HD:(hh + 1) * HD]
                        vh = vs[s0:s0 + KC, hh * HD:(hh + 1) * HD]
                        lb, lk = _log_sig_pair(_dot(qh, kh, 1, 1))
                        if diag:
                            lk = jnp.where(before, lk, 0.0)
                        pin, ptot = _sub_sums(lk, tri_inc)
                        ws, prun = [], pruns[hh]
                        for s in range(NSUB):
                            ws.append(jnp.exp(lb[:, s * BLK:(s + 1) * BLK] + (tt - (pin[s] + prun))))
                            prun = prun + ptot[s]
                        w = jnp.concatenate(ws, axis=1)
                        if diag:
                            w = jnp.where(before, w, 0.0)
                        e = w * _dot(doh, vh, 1, 1)
                        pex, etot = _sub_sums(e, tri_exc)
                        cs, erun = [], eruns[hh]
                        for s in range(NSUB):
                            cs.append(pex[s] + erun)
                            erun = erun + etot[s]
                        sig = jnp.exp(lb)
                        dz = e * (1.0 - sig) - jnp.concatenate(cs, axis=1) * sig
                        if diag:
                            dz = jnp.where(before, dz, 0.0)
                        dz = dz.astype(BF16)
                        dqs[hh] = dqs[hh] + _dot(dz, kh, 1, 0)
                        dk_acc[hh, s0:s0 + KC, :] += _dot(dz, qh, 0, 0)
                        dv_acc[hh, s0:s0 + KC, :] += _dot(w.astype(BF16), doh, 0, 0)
                        pruns[hh], eruns[hh] = prun, erun
                dq_ref[pl.ds(t0, QB), :] = (jnp.concatenate(dqs, axis=1) * SCALE).astype(BF16)
                return carry0

            lax.fori_loop(0, QPG, qblock, 0)
        dk_ref[...] = jnp.concatenate([dk_acc[0], dk_acc[1]], axis=1).astype(BF16)
        dv_ref[...] = jnp.concatenate([dv_acc[0], dv_acc[1]], axis=1).astype(BF16)

    slab = lambda off: pl.BlockSpec((None, S, LANES), lambda p: (off + p, 0, 0))
    pair = pl.BlockSpec((None, S, LANES), lambda p: (p, 0, 0))
    return pl.pallas_call(
        body, grid=(2,), in_specs=[slab(q0), slab(k0), slab(v0), pair, pair], out_specs=[pair] * 3,
        out_shape=[jax.ShapeDtypeStruct((2, S, LANES), BF16)] * 3,
        scratch_shapes=[pltpu.VMEM((S, LANES), BF16)] * 4 + [pltpu.VMEM((2, S, HD), F32)] * 2,
        compiler_params=_cp("arbitrary"), name=name)(proj, proj, proj, do, tot)


def _cat_slabs(ref):
    return jnp.concatenate([ref[s] for s in range(ref.shape[0])], axis=1)


def _merge_fwd(o_a, o_b, o_c, gates, b_gate, wa, wb, wc, w_out, name):
    tm = ROW_TILE

    def body(oa_ref, ob_ref, oc_ref, g_ref, bg_ref, wa_ref, wb_ref, wc_ref, wo_ref, mg_ref, mo_ref):
        acc = jnp.zeros((tm, D), F32)
        for i, (o_ref, w_ref) in enumerate(((oa_ref, wa_ref), (ob_ref, wb_ref), (oc_ref, wc_ref))):
            pr = _dot(_cat_slabs(o_ref).astype(BF16), w_ref[...], 1, 0)
            sg = jax.nn.sigmoid(g_ref[:, i * D:(i + 1) * D] + bg_ref[i:i + 1, :])
            acc = acc + sg * pr
        mg = acc.astype(BF16)
        mg_ref[...] = mg
        mo_ref[...] = _dot(mg, wo_ref[...], 1, 0)

    slabs = lambda n: pl.BlockSpec((n, tm, LANES), lambda i: (0, i, 0))
    full = lambda r, c: pl.BlockSpec((r, c), lambda i: (0, 0))
    row = pl.BlockSpec((tm, D), lambda i: (i, 0))
    return pl.pallas_call(
        body, grid=(S // tm,),
        in_specs=[slabs(2), slabs(4), slabs(2), pl.BlockSpec((tm, GATE_COLS), lambda i: (i, 0)), full(3, D),
                  full(256, D), full(512, D), full(256, D), full(D, D)],
        out_specs=[row, row],
        out_shape=[jax.ShapeDtypeStruct((S, D), BF16), jax.ShapeDtypeStruct((S, D), F32)],
        compiler_params=_cp("parallel"), name=name)(o_a, o_b, o_c, gates, b_gate, wa, wb, wc, w_out)


def _merge_bwd(d_mo, o_a, o_b, o_c, gates, b_gate, wa, wb, wc, w_out, name):
    tm = ROW_TILE

    def body(dmo_ref, oa_ref, ob_ref, oc_ref, g_ref, bg_ref, wa_ref, wb_ref, wc_ref, wo_ref,
             doa_ref, dob_ref, doc_ref, dg_ref, dwa_ref, dwb_ref, dwc_ref, dbg_ref):
        @pl.when(pl.program_id(0) == 0)
        def _():
            dwa_ref[...] = jnp.zeros(dwa_ref.shape, F32)
            dwb_ref[...] = jnp.zeros(dwb_ref.shape, F32)
            dwc_ref[...] = jnp.zeros(dwc_ref.shape, F32)
            dbg_ref[...] = jnp.zeros(dbg_ref.shape, F32)

        dmg = _dot(dmo_ref[...], wo_ref[...], 1, 1)
        trip = ((oa_ref, wa_ref, doa_ref, dwa_ref), (ob_ref, wb_ref, dob_ref, dwb_ref), (oc_ref, wc_ref, doc_ref, dwc_ref))
        for i, (o_ref, w_ref, do_ref, dw_ref) in enumerate(trip):
            ob = _cat_slabs(o_ref).astype(BF16)
            pr = _dot(ob, w_ref[...], 1, 0)
            sg = jax.nn.sigmoid(g_ref[:, i * D:(i + 1) * D] + bg_ref[i:i + 1, :])
            dgate = dmg * pr * sg * (1.0 - sg)
            dg_ref[:, i * D:(i + 1) * D] = dgate.astype(BF16)
            dbg_ref[i:i + 1, :] += jnp.sum(dgate, axis=0, keepdims=True)
            dpr = (dmg * sg).astype(BF16)
            do = _dot(dpr, w_ref[...], 1, 1)
            for s in range(do_ref.shape[0]):
                do_ref[s] = do[:, s * LANES:(s + 1) * LANES]
            dw_ref[...] += _dot(ob, dpr, 0, 0)

    slabs = lambda n: pl.BlockSpec((n, tm, LANES), lambda i: (0, i, 0))
    full = lambda r, c: pl.BlockSpec((r, c), lambda i: (0, 0))
    row = pl.BlockSpec((tm, D), lambda i: (i, 0))
    return pl.pallas_call(
        body, grid=(S // tm,),
        in_specs=[row, slabs(2), slabs(4), slabs(2), pl.BlockSpec((tm, GATE_COLS), lambda i: (i, 0)), full(3, D),
                  full(256, D), full(512, D), full(256, D), full(D, D)],
        out_specs=[slabs(2), slabs(4), slabs(2), pl.BlockSpec((tm, GATE_COLS), lambda i: (i, 0)),
                   full(256, D), full(512, D), full(256, D), full(3, D)],
        out_shape=[jax.ShapeDtypeStruct((2, S, LANES), F32), jax.ShapeDtypeStruct((4, S, LANES), F32),
                   jax.ShapeDtypeStruct((2, S, LANES), F32), jax.ShapeDtypeStruct((S, GATE_COLS), BF16),
                   jax.ShapeDtypeStruct((256, D), F32), jax.ShapeDtypeStruct((512, D), F32),
                   jax.ShapeDtypeStruct((256, D), F32), jax.ShapeDtypeStruct((3, D), F32)],
        compiler_params=_cp("arbitrary"), name=name)(d_mo, o_a, o_b, o_c, gates, b_gate, wa, wb, wc, w_out)


FC = 256
GELU_K = math.sqrt(2.0 / math.pi)
GELU_C = 0.044715


RC = 64
NRC = S // RC


def _down(tail, cur, n):
    row = lax.broadcasted_iota(jnp.int32, tail.shape, 0)
    rolled = pltpu.roll(cur, n, 0)
    first = jnp.where(row < n, pltpu.roll(tail, n, 0), rolled[0:8])
    return jnp.concatenate([first, rolled[8:]], axis=0)


def _up(cur, head, n):
    row = lax.broadcasted_iota(jnp.int32, head.shape, 0)
    rolled = pltpu.roll(cur, RC - n, 0)
    last = jnp.where(row >= 8 - n, pltpu.roll(head, 8 - n, 0), rolled[RC - 8:])
    return jnp.concatenate([rolled[:RC - 8], last], axis=0)


def _conv_chunk(load, j, w_ref, b_ref, half):
    r0 = pl.multiple_of(j * RC, RC)
    cur = load(r0, RC)
    tail = jnp.where(j > 0, load(pl.multiple_of(jnp.maximum(r0 - 8, 0), 8), 8), 0.0)
    d1 = _down(tail, cur, 1)
    d2 = _down(tail, cur, 2)
    y = w_ref[0:1, half, :] * d2 + w_ref[1:2, half, :] * d1 + w_ref[2:3, half, :] * cur + b_ref[half:half + 1, :]
    return y, cur, d1, d2


def _chunk(j):
    return pl.ds(pl.multiple_of(j * RC, RC), RC)


def _fold8(x):
    return jnp.sum(x.reshape(RC // 8, 8, x.shape[-1]), axis=0)


def _ffn_act(u, conv_w, conv_b, name):
    def body(u_ref, w_ref, b_ref, a_ref):
        def step(j, carry):
            yg = _conv_chunk(lambda r, n: u_ref[0, pl.ds(r, n), :], j, w_ref, b_ref, 0)[0]
            yv = _conv_chunk(lambda r, n: u_ref[1, pl.ds(r, n), :], j, w_ref, b_ref, 1)[0]
            th = jnp.tanh(GELU_K * (yg + GELU_C * yg * yg * yg))
            a_ref[_chunk(j), :] = (0.5 * yg * (1.0 + th) * yv).astype(BF16)
            return carry

        lax.fori_loop(0, NRC, step, 0)

    return pl.pallas_call(
        body, grid=(D_FF // FC,),
        in_specs=[pl.BlockSpec((2, S, FC), lambda j: (0, 0, j)), pl.BlockSpec((3, 2, FC), lambda j: (0, 0, j)),
                  pl.BlockSpec((2, FC), lambda j: (0, j))],
        out_specs=pl.BlockSpec((S, FC), lambda j: (0, j)),
        out_shape=jax.ShapeDtypeStruct((S, D_FF), BF16),
        compiler_params=_cp("parallel"), name=name)(u, conv_w, conv_b)


def _ffn_act_bwd(u, d_a, conv_w, conv_b, name):
    def body(u_ref, da_ref, w_ref, b_ref, du_ref, dw_ref, db_ref, dy_s):
        def first(j, acc):
            yg, ug, ug1, ug2 = _conv_chunk(lambda r, n: u_ref[0, pl.ds(r, n), :], j, w_ref, b_ref, 0)
            yv, uv, uv1, uv2 = _conv_chunk(lambda r, n: u_ref[1, pl.ds(r, n), :], j, w_ref, b_ref, 1)
            th = jnp.tanh(GELU_K * (yg + GELU_C * yg * yg * yg))
            gelu = 0.5 * yg * (1.0 + th)
            dgelu = 0.5 * (1.0 + th) + 0.5 * yg * (1.0 - th * th) * GELU_K * (1.0 + 3.0 * GELU_C * yg * yg)
            da = da_ref[_chunk(j), :]
            dyg = da * yv * dgelu
            dyv = da * gelu
            dy_s[0, _chunk(j), :] = dyg
            dy_s[1, _chunk(j), :] = dyv
            new = (_fold8(dyg * ug2), _fold8(dyg * ug1), _fold8(dyg * ug), _fold8(dyg),
                   _fold8(dyv * uv2), _fold8(dyv * uv1), _fold8(dyv * uv), _fold8(dyv))
            return tuple(a + n for a, n in zip(acc, new))

        acc = lax.fori_loop(0, NRC, first, tuple(jnp.zeros((8, FC), F32) for _ in range(8)))
        for half in range(2):
            for k in range(3):
                dw_ref[k:k + 1, half, :] = jnp.sum(acc[4 * half + k], axis=0, keepdims=True)
            db_ref[half:half + 1, :] = jnp.sum(acc[4 * half + 3], axis=0, keepdims=True)

        def second(j, carry):
            for half in range(2):
                cur = dy_s[half, _chunk(j), :]
                h0 = pl.multiple_of(jnp.minimum((j + 1) * RC, S - 8), 8)
                head = jnp.where(j < NRC - 1, dy_s[half, pl.ds(h0, 8), :], 0.0)
                du = (w_ref[2:3, half, :] * cur + w_ref[1:2, half, :] * _up(cur, head, 1)
                      + w_ref[0:1, half, :] * _up(cur, head, 2))
                du_ref[half, _chunk(j), :] = du.astype(BF16)
            return carry

        lax.fori_loop(0, NRC, second, 0)

    return pl.pallas_call(
        body, grid=(D_FF // FC,),
        in_specs=[pl.BlockSpec((2, S, FC), lambda j: (0, 0, j)), pl.BlockSpec((S, FC), lambda j: (0, j)),
                  pl.BlockSpec((3, 2, FC), lambda j: (0, 0, j)), pl.BlockSpec((2, FC), lambda j: (0, j))],
        out_specs=[pl.BlockSpec((2, S, FC), lambda j: (0, 0, j)), pl.BlockSpec((3, 2, FC), lambda j: (0, 0, j)),
                   pl.BlockSpec((2, FC), lambda j: (0, j))],
        out_shape=[jax.ShapeDtypeStruct((2, S, D_FF), BF16), jax.ShapeDtypeStruct((3, 2, D_FF), F32),
                   jax.ShapeDtypeStruct((2, D_FF), F32)],
        scratch_shapes=[pltpu.VMEM((2, S, FC), F32)],
        compiler_params=_cp("parallel"), name=name)(u, d_a, conv_w, conv_b)


def _layer_fwd(x, h1, w, bias, lname):
    n = lambda s: f"{lname}_{s}"
    w.need("in", h1)
    tn = 768
    proj = _mm(h1, w["w_in"], grid=(S // 1024, QKV_COLS // tn, 1),
               a_spec=pl.BlockSpec((1024, D), lambda i, j, k: (i, 0)),
               b_spec=pl.BlockSpec((tn, D), lambda i, j, k: (j, 0)),
               out_shape=jax.ShapeDtypeStruct((QKV_SLABS, S, LANES), F32),
               out_spec=pl.BlockSpec((tn // LANES, 1024, LANES), lambda i, j, k: (j, i, 0)),
               ca=1, cb=1, acc_shape=(1024, tn), out_slab=True, name=n("proj_qkv"))
    gates = _mm(h1, w["w_in"], grid=(S // 1024, GATE_COLS // tn, 1),
                a_spec=pl.BlockSpec((1024, D), lambda i, j, k: (i, 0)),
                b_spec=pl.BlockSpec((tn, D), lambda i, j, k: (j + QKV_COLS // tn, 0)),
                out_shape=jax.ShapeDtypeStruct((S, GATE_COLS), F32),
                out_spec=pl.BlockSpec((1024, tn), lambda i, j, k: (i, j)),
                ca=1, cb=1, acc_shape=(1024, tn), name=n("proj_gate"))
    nums, stats = [], []
    for g, (_, d) in enumerate(A_GROUPS):
        nm, st = _band_fwd(proj, bias, w["sinks"], d=d, q0=2 * g, k0=6 + 2 * g, v0=12 + 2 * g, npairs=2, bias0=2 * g,
                           shared_kv=False, name=n(f"attn_a{g}_fwd"))
        nums.append(nm)
        stats.append(st)
    o_a, lse_a = _combine_a(nums, stats, n("attn_a_combine"))
    o_b, lse_b = _band_fwd(proj, bias, w["sinks"], d=1, q0=18, k0=22, v0=23, npairs=4, bias0=6, shared_kv=True,
                           name=n("attn_b_fwd"))
    o_c, tot_c = _stick_fwd(proj, q0=24, k0=26, v0=28, name=n("attn_c_fwd"))
    w.need("mix", tot_c)
    merged, mo = _merge_fwd(o_a, o_b, o_c, gates, w["b_gate"], w["w_br_a"], w["w_br_b"], w["w_br_c"], w["w_out"], n("merge_fwd"))
    x2, h2 = _postnorm_res(x, mo, w["attn_post_norm"], w["ffn_pre_norm"], n("attn_post"))
    w.need("ffn", h2)
    u = _mm(h2, w["w_up"], grid=(S // 1024, 2 * D_FF // 1024, 1),
            a_spec=pl.BlockSpec((1024, D), lambda i, j, k: (i, 0)),
            b_spec=pl.BlockSpec((D, 1024), lambda i, j, k: (0, j)),
            out_shape=jax.ShapeDtypeStruct((2, S, D_FF), F32),
            out_spec=pl.BlockSpec((None, 1024, 1024), lambda i, j, k: (j // 4, i, j % 4)),
            ca=1, cb=0, acc_shape=(1024, 1024), name=n("ffn_up"))
    a = _ffn_act(u, w["conv_w"], w["conv_b"], n("ffn_act"))
    fo = _mm_nn(a, w["w_down"], F32, 1024, 1024, 2048, n("ffn_down"))
    saved = dict(x=x, h1=h1, proj=proj, gates=gates, o_a=o_a, lse_a=lse_a, o_b=o_b, lse_b=lse_b, o_c=o_c, tot_c=tot_c,
                 merged=merged, mo=mo, x2=x2, h2=h2, u=u, a=a, fo=fo)
    return saved


def _layer_bwd(dx3, sv, w, bias, lname, tok=None, on_part=None):
    n = lambda s: f"{lname}_{s}"
    g = {}

    def part(group, vec):
        t = on_part(group, g) if on_part is not None else None
        return vec if t is None else vec + t

    gain = w["ffn_post_norm"] if tok is None else w["ffn_post_norm"] + tok
    d_fo, g["ffn_post_norm"] = _norm_bwd(sv["fo"], gain, [dx3], None, BF16, n("ffn_post_bwd"))
    d_a = _mm_nt(d_fo, w["w_down"], F32, 1024, 1024, 1024, n("ffn_down_bwd_x"))
    g["w_down"] = _mm_tn(sv["a"], d_fo, BF16, 1024, 1024, S, n("ffn_down_bwd_w"))
    d_u, dcw, dcb = _ffn_act_bwd(sv["u"], d_a, w["conv_w"], w["conv_b"], n("ffn_act_bwd"))
    g["conv_w"] = dcw.reshape(3, 2 * D_FF)
    g["conv_b"] = dcb.reshape(1, 2 * D_FF)
    g["w_up"] = _mm(sv["h2"], d_u, grid=(1, 2 * D_FF // 1024, 1),
                    a_spec=pl.BlockSpec((S, D), lambda i, j, k: (k, 0)),
                    b_spec=pl.BlockSpec((None, S, 1024), lambda i, j, k: (j // 4, k, j % 4)),
                    out_shape=jax.ShapeDtypeStruct((D, 2 * D_FF), BF16),
                    out_spec=pl.BlockSpec((D, 1024), lambda i, j, k: (0, j)),
                    ca=0, cb=0, acc_shape=(D, 1024), name=n("ffn_up_bwd_w"))
    tok_ffn = on_part("ffn", g) if on_part is not None else None
    d_h2 = _mm(d_u, w["w_up"], grid=(S // 1024, 1, 2 * D_FF // 2048),
               a_spec=pl.BlockSpec((None, 1024, 2048), lambda i, j, k: (k // 2, i, k % 2)),
               b_spec=pl.BlockSpec((D, 2048), lambda i, j, k: (0, k)),
               out_shape=jax.ShapeDtypeStruct((S, D), F32),
               out_spec=pl.BlockSpec((1024, D), lambda i, j, k: (i, 0)),
               ca=1, cb=1, acc_shape=(1024, D), after=tok_ffn, name=n("ffn_up_bwd_x"))
    dx2, g["ffn_pre_norm"] = _norm_bwd(sv["x2"], w["ffn_pre_norm"], [d_h2], dx3, F32, n("ffn_pre_bwd"))
    d_mo, g["attn_post_norm"] = _norm_bwd(sv["mo"], w["attn_post_norm"], [dx2], None, BF16, n("attn_post_bwd"))
    g["w_out"] = _mm_tn(sv["merged"], d_mo, BF16, 1024, 1024, S, n("out_bwd_w"))
    do_a, do_b, do_c, d_gates, dwa, dwb, dwc, g["b_gate"] = _merge_bwd(
        d_mo, sv["o_a"], sv["o_b"], sv["o_c"], sv["gates"], w["b_gate"], w["w_br_a"], w["w_br_b"], w["w_br_c"],
        w["w_out"], n("merge_bwd"))
    g["w_br_a"], g["w_br_b"], g["w_br_c"] = dwa, dwb, dwc
    sinks = part("mix", w["sinks"])
    proj = sv["proj"]
    dqa, dka, dva, gbias = [], [], [], []
    for gi, (_, d) in enumerate(A_GROUPS):
        dq, dk, dv, gg, _ = _band_bwd(proj, bias, sv["o_a"], do_a, sv["lse_a"], sinks, d=d, q0=2 * gi, k0=6 + 2 * gi,
                                      v0=12 + 2 * gi, npairs=2, bias0=2 * gi, shared_kv=False, name=n(f"attn_a{gi}_bwd"))
        dqa.append(dq), dka.append(dk), dva.append(dv), gbias.append(gg)
    dqb, dkb, dvb, ggb, dsink = _band_bwd(proj, bias, sv["o_b"], do_b, sv["lse_b"], sinks, d=1, q0=18, k0=22, v0=23,
                                          npairs=4, bias0=6, shared_kv=True, name=n("attn_b_bwd"))
    gbias.append(ggb)
    g["bias_g"] = jnp.concatenate(gbias, axis=0).reshape(N_BIAS_HEADS, BLK, 2 * BLK)
    g["sinks"] = dsink[:, 0, :2].reshape(1, 8)
    dqc, dkc, dvc = _stick_bwd(proj, do_c, sv["tot_c"], q0=24, k0=26, v0=28, name=n("attn_c_bwd"))
    dqkv = jnp.concatenate(dqa + dka + dva + [dqb, dkb, dvb, dqc, dkc, dvc], axis=0)
    ts = 6
    tsx = 15
    dw_in = _mm(dqkv, sv["h1"], grid=(QKV_SLABS // ts, 1, 1),
                a_spec=pl.BlockSpec((ts, S, LANES), lambda i, j, k: (i, k, 0)),
                b_spec=pl.BlockSpec((S, D), lambda i, j, k: (k, 0)),
                out_shape=jax.ShapeDtypeStruct((IN_COLS, D), BF16),
                out_spec=pl.BlockSpec((ts * LANES, D), lambda i, j, k: (i, 0)),
                ca=0, cb=0, acc_shape=(ts * LANES, D), a_slab=True, name=n("in_bwd_w_qkv"))
    g["w_in"] = _mm(d_gates, sv["h1"], grid=(GATE_COLS // 768, 1, 1),
                    a_spec=pl.BlockSpec((S, 768), lambda i, j, k: (k, i)),
                    b_spec=pl.BlockSpec((S, D), lambda i, j, k: (k, 0)),
                    out_shape=jax.ShapeDtypeStruct((IN_COLS, D), BF16),
                    out_spec=pl.BlockSpec((768, D), lambda i, j, k: (i + QKV_COLS // 768, 0)),
                    ca=0, cb=0, acc_shape=(768, D), alias_out=dw_in, name=n("in_bwd_w_gate"))
    tok_in = on_part("in", g) if on_part is not None else None
    d_h1a = _mm(dqkv, w["w_in"], grid=(S // 1024, 1, QKV_SLABS // tsx),
                a_spec=pl.BlockSpec((tsx, 1024, LANES), lambda i, j, k: (k, i, 0)),
                b_spec=pl.BlockSpec((tsx * LANES, D), lambda i, j, k: (k, 0)),
                out_shape=jax.ShapeDtypeStruct((S, D), F32),
                out_spec=pl.BlockSpec((1024, D), lambda i, j, k: (i, 0)),
                ca=1, cb=0, acc_shape=(1024, D), a_slab=True, after=tok_in, name=n("in_bwd_x_qkv"))
    d_h1b = _mm(d_gates, w["w_in"], grid=(S // 1024, 1, GATE_COLS // 768),
                a_spec=pl.BlockSpec((1024, 768), lambda i, j, k: (i, k)),
                b_spec=pl.BlockSpec((768, D), lambda i, j, k: (k + QKV_COLS // 768, 0)),
                out_shape=jax.ShapeDtypeStruct((S, D), F32),
                out_spec=pl.BlockSpec((1024, D), lambda i, j, k: (i, 0)),
                ca=1, cb=0, acc_shape=(1024, D), after=tok_in, name=n("in_bwd_x_gate"))
    dx, g["attn_pre_norm"] = _norm_bwd(sv["x"], w["attn_pre_norm"], [d_h1a, d_h1b], dx2, F32, n("attn_pre_bwd"))
    return dx, g, tok_in


def _local_step(x, target, ws, rel_bias, tok=None, on_grads=None):
    buckets = jnp.asarray(_bucket_tiles())
    bias = _bias_tiles(rel_bias, buckets, "bias_tiles").reshape(N_BIAS_HEADS // 2, 2, 2, BLK, 2 * BLK)
    saved = []
    gain0 = ws[0]["attn_pre_norm"] if tok is None else ws[0]["attn_pre_norm"] + tok
    h1 = _prenorm(x, gain0, "l0_attn_pre")
    for l in range(DEPTH):
        sv = _layer_fwd(x, h1, ws[l], bias, f"l{l}")
        saved.append(sv)
        g_next = ws[l + 1]["attn_pre_norm"] if l + 1 < DEPTH else ws[l]["attn_pre_norm"]
        x, h1 = _postnorm_res(sv["x2"], sv["fo"], ws[l]["ffn_post_norm"], g_next, f"l{l}_ffn_post")
    dy, loss_tile = _loss_head(x, target, "loss_head")
    grads = [None] * DEPTH
    tok = None
    for l in reversed(range(DEPTH)):
        on_part = None if on_grads is None else functools.partial(on_grads, l)
        dy, grads[l], tok = _layer_bwd(dy, saved[l], ws[l], bias, f"l{l}", tok, on_part)
    g_rel = _bias_grad([grads[l]["bias_g"] for l in range(DEPTH)], buckets, "bias_grad")[:, :N_BIAS_HEADS]
    return loss_tile[0, 0], dy, grads, g_rel


def _coords():
    return lax.axis_index("x"), lax.axis_index("y"), lax.axis_index("c")


def _peer(rel):
    x, y, c = _coords()
    return (1 - x if rel & 4 else x, 1 - y if rel & 2 else y, 1 - c if rel & 1 else c)


def _exchange(srcs, dst_shapes, src_win, dst_win, name, after=None):
    nt = len(srcs)
    extra = [] if after is None else [after]

    def body(*refs):
        src_refs, dst_refs = refs[:nt], refs[nt + len(extra):2 * nt + len(extra)]
        send_sems, recv_sems, local_sems = refs[2 * nt + len(extra):]
        x, y, c = _coords()
        me = 4 * x + 2 * y + c
        locals_ = []
        for t in range(nt):
            cp = pltpu.make_async_copy(src_win(t, src_refs[t], me), dst_win(t, dst_refs[t], me), local_sems.at[t])
            cp.start()
            locals_.append(cp)
        sends = []
        for rel in range(1, NDEV):
            px, py, pc = _peer(rel)
            q = 4 * px + 2 * py + pc
            for t in range(nt):
                cp = pltpu.make_async_remote_copy(
                    src_ref=src_win(t, src_refs[t], q), dst_ref=dst_win(t, dst_refs[t], me),
                    send_sem=send_sems.at[rel - 1, t], recv_sem=recv_sems.at[rel - 1, t],
                    device_id=(px, py, pc), device_id_type=MESH)
                cp.start()
                sends.append(cp)
        for rel in range(1, NDEV):
            px, py, pc = _peer(rel)
            q = 4 * px + 2 * py + pc
            for t in range(nt):
                pltpu.make_async_remote_copy(
                    src_ref=src_win(t, src_refs[t], me), dst_ref=dst_win(t, dst_refs[t], q),
                    send_sem=send_sems.at[rel - 1, t], recv_sem=recv_sems.at[rel - 1, t],
                    device_id=(px, py, pc), device_id_type=MESH).wait_recv()
        for cp in sends:
            cp.wait_send()
        for cp in locals_:
            cp.wait()

    return pl.pallas_call(
        body, in_specs=[ANY] * (nt + len(extra)), out_specs=[ANY] * nt, out_shape=dst_shapes,
        scratch_shapes=[pltpu.SemaphoreType.DMA((NDEV - 1, nt)), pltpu.SemaphoreType.DMA((NDEV - 1, nt)),
                        pltpu.SemaphoreType.DMA((nt,))],
        name=name)(*srcs, *extra)


BIG = (("w_in", 0, 864), ("w_br_a", 1, 128), ("w_br_b", 1, 128), ("w_br_c", 1, 128), ("w_out", 0, 128),
       ("w_up", 1, 1024), ("w_down", 0, 512))


NBIG = len(BIG)
BIG_FULL = {"w_in": (IN_COLS, D), "w_br_a": (256, D), "w_br_b": (512, D), "w_br_c": (256, D), "w_out": (D, D),
            "w_up": (D, 2 * D_FF), "w_down": (D_FF, D)}
SHARD_ROWS = {"w_in": 288, "w_up": 256, "w_down": 256}
LAYER_GROUPS = (("in", (0,)), ("mix", (1, 2, 3, 4)), ("ffn", (5, 6)))

HBM_SPEC = pl.BlockSpec(memory_space=pltpu.HBM)
SEM_SPEC = pl.BlockSpec(memory_space=pltpu.SEMAPHORE)


def _hbm(a):
    return pltpu.with_memory_space_constraint(a, pltpu.HBM)


def _shard_window(t, ref, k):
    nm, ax, ext = BIG[t % NBIG]
    off = pl.multiple_of(k * ext, ext)
    if ax == 0:
        return ref.at[pl.ds(off, ext), :]
    return ref.at[:, pl.ds(off, ext)]


def _whole(t, ref, k):
    return ref


def _slot(t, ref, k):
    return ref.at[k]


def _own_block_spec(t, rows, me_of):
    nm, ax, ext = BIG[t % NBIG]
    r, c = BIG_FULL[nm]
    if ax == 0:
        return pl.BlockSpec((rows, c), lambda i, m: (me_of(m) * (ext // rows) + i, 0))
    return pl.BlockSpec((rows, ext), lambda i, m: (i, me_of(m)))


def _cast_own(t, shard, me_arr, name):
    nm, ax, ext = BIG[t % NBIG]
    nr, nc = shard.shape
    rows = SHARD_ROWS.get(nm, nr)
    shape = BIG_FULL[nm]

    def body(m_ref, s_ref, o_ref):
        o_ref[...] = s_ref[...].astype(BF16)

    return pl.pallas_call(
        body, grid_spec=pltpu.PrefetchScalarGridSpec(
            num_scalar_prefetch=1, grid=(nr // rows,),
            in_specs=[pl.BlockSpec((rows, nc), lambda i, m: (i, 0))],
            out_specs=_own_block_spec(t, rows, lambda m: m[0])),
        out_shape=jax.ShapeDtypeStruct(shape, BF16), compiler_params=_cp("arbitrary"), name=name)(me_arr, shard)


ALL_RELS = tuple(range(1, NDEV))
NEAR_RELS = (1, 2, 4, 6)
FAR_RELS = (2, 4, 6)


def _xchg_start(srcs, lands, groups, src_win, dst_win, after, name, rels=ALL_RELS, tids=None):
    ns = 0 if srcs is None else len(srcs)
    nt, ng = len(lands), len(groups)
    ins = ([] if srcs is None else list(srcs)) + list(lands)

    def body(*refs):
        src_refs, land_refs = refs[:ns], refs[ns:ns + nt]
        sems = refs[ns + nt + 1:ns + nt + 1 + 2 * ng]
        token = refs[-1]
        x, y, c = _coords()
        me = 4 * x + 2 * y + c
        for gi, grp in enumerate(groups):
            for j, t in enumerate(grp):
                tid = t if tids is None else tids[t]
                for ri, rel in enumerate(rels):
                    px, py, pc = _peer(rel)
                    q = 4 * px + 2 * py + pc
                    src = dst_win(tid, land_refs[t], me) if srcs is None else src_win(tid, src_refs[t], q)
                    pltpu.make_async_remote_copy(
                        src_ref=src, dst_ref=dst_win(tid, land_refs[t], me),
                        send_sem=sems[2 * gi].at[ri * len(grp) + j],
                        recv_sem=sems[2 * gi + 1].at[ri * len(grp) + j],
                        device_id=(px, py, pc), device_id_type=MESH).start()
        token[...] = jnp.zeros((8, LANES), F32)

    out_shape = []
    for grp in groups:
        out_shape += [pltpu.SemaphoreType.DMA((len(rels) * len(grp),))] * 2
    out_shape += [pltpu.HBM(a.shape, a.dtype) for a in ins]
    out_shape.append(jax.ShapeDtypeStruct((8, LANES), F32))
    outs = pl.pallas_call(
        body, in_specs=[HBM_SPEC] * len(ins) + [ANY],
        out_specs=[SEM_SPEC] * (2 * ng) + [HBM_SPEC] * len(ins) + [pl.BlockSpec(memory_space=pltpu.VMEM)],
        out_shape=out_shape, input_output_aliases={i: 2 * ng + i for i in range(len(ins))},
        compiler_params=pltpu.CompilerParams(has_side_effects=pltpu.SideEffectType.DATAFLOW_SIDE_EFFECTING),
        name=name)(*[_hbm(a) for a in ins], after)
    sems = [(outs[2 * gi], outs[2 * gi + 1]) for gi in range(ng)]
    thru = list(outs[2 * ng:2 * ng + len(ins)])
    return sems, (None if srcs is None else thru[:ns]), thru[ns:], outs[-1]


def _xchg_wait(sems, srcs, lands, tids, after, src_win, dst_win, name, rels=ALL_RELS):
    ns = 0 if srcs is None else len(srcs)
    n = len(lands)
    send_sem, recv_sem = sems
    ins = ([] if srcs is None else list(srcs)) + list(lands)

    def body(*refs):
        src_refs, land_refs = refs[:ns], refs[ns:ns + n]
        ssem, rsem = refs[ns + n], refs[ns + n + 1]
        x, y, c = _coords()
        me = 4 * x + 2 * y + c
        for j, t in enumerate(tids):
            for ri, rel in enumerate(rels):
                px, py, pc = _peer(rel)
                q = 4 * px + 2 * py + pc
                src = dst_win(t, land_refs[j], me) if srcs is None else src_win(t, src_refs[j], q)
                cp = pltpu.make_async_remote_copy(
                    src_ref=src, dst_ref=dst_win(t, land_refs[j], q),
                    send_sem=ssem.at[ri * n + j], recv_sem=rsem.at[ri * n + j],
                    device_id=(px, py, pc), device_id_type=MESH)
                cp.wait_send()
                cp.wait_recv()

    outs = pl.pallas_call(
        body, in_specs=[HBM_SPEC] * len(ins) + [SEM_SPEC, SEM_SPEC, ANY], out_specs=[HBM_SPEC] * len(ins),
        out_shape=[pltpu.HBM(a.shape, a.dtype) for a in ins],
        input_output_aliases={i: i for i in range(len(ins))},
        compiler_params=pltpu.CompilerParams(has_side_effects=pltpu.SideEffectType.DATAFLOW_SIDE_EFFECTING),
        name=name)(*ins, send_sem, recv_sem, after)
    return (None if srcs is None else list(outs[:ns])), list(outs[ns:])


def _gather_forward(sems_in, lands, groups, tids, after, dst_win, name):
    nt, ng = len(lands), len(groups)

    def body(*refs):
        land_refs = refs[:nt]
        in_sems = refs[nt:nt + 2 * ng]
        out_sems = refs[nt + 2 * ng + 1:nt + 4 * ng + 1]
        token = refs[-1]
        x, y, c = _coords()
        me = 4 * x + 2 * y + c
        sib = (x, y, 1 - c)
        for gi, grp in enumerate(groups):
            n = len(grp)
            for j, pos in enumerate(grp):
                t = tids[pos]
                for ri, rel in enumerate(NEAR_RELS):
                    px, py, pc = _peer(rel)
                    q = 4 * px + 2 * py + pc
                    cp = pltpu.make_async_remote_copy(
                        src_ref=dst_win(t, land_refs[pos], me), dst_ref=dst_win(t, land_refs[pos], q),
                        send_sem=in_sems[2 * gi].at[ri * n + j], recv_sem=in_sems[2 * gi + 1].at[ri * n + j],
                        device_id=(px, py, pc), device_id_type=MESH)
                    cp.wait_send()
                    cp.wait_recv()
            for j, pos in enumerate(grp):
                t = tids[pos]
                for fi, rel in enumerate(FAR_RELS):
                    px, py, pc = _peer(rel)
                    q = 4 * px + 2 * py + pc
                    win = dst_win(t, land_refs[pos], q)
                    pltpu.make_async_remote_copy(
                        src_ref=win, dst_ref=win,
                        send_sem=out_sems[2 * gi].at[fi * n + j], recv_sem=out_sems[2 * gi + 1].at[fi * n + j],
                        device_id=sib, device_id_type=MESH).start()
        token[...] = jnp.zeros((8, LANES), F32)

    out_shape = []
    for grp in groups:
        out_shape += [pltpu.SemaphoreType.DMA((len(FAR_RELS) * len(grp),))] * 2
    out_shape += [pltpu.HBM(a.shape, a.dtype) for a in lands]
    out_shape.append(jax.ShapeDtypeStruct((8, LANES), F32))
    flat_sems = [s for pair in sems_in for s in pair]
    outs = pl.pallas_call(
        body, in_specs=[HBM_SPEC] * nt + [SEM_SPEC] * (2 * ng) + [ANY],
        out_specs=[SEM_SPEC] * (2 * ng) + [HBM_SPEC] * nt + [pl.BlockSpec(memory_space=pltpu.VMEM)],
        out_shape=out_shape, input_output_aliases={i: 2 * ng + i for i in range(nt)},
        compiler_params=pltpu.CompilerParams(has_side_effects=pltpu.SideEffectType.DATAFLOW_SIDE_EFFECTING),
        name=name)(*[_hbm(a) for a in lands], *flat_sems, after)
    sems = [(outs[2 * gi], outs[2 * gi + 1]) for gi in range(ng)]
    return sems, list(outs[2 * ng:2 * ng + nt]), outs[-1]


class _Weights:
    def __init__(self, ready, pending=None):
        self.ready = dict(ready)
        self.pending = dict(pending or {})

    def __getitem__(self, k):
        return self.ready[k]

    def need(self, group, after):
        fn = self.pending.pop(group, None)
        if fn is not None:
            self.ready.update(fn(after))


def _adamw_math(w, g, m, v):
    m2 = ADAM_B1 * m + (1.0 - ADAM_B1) * g
    v2 = ADAM_B2 * v + (1.0 - ADAM_B2) * (g * g)
    m_hat = m2 / (1.0 - ADAM_B1 ** ADAM_STEP)
    v_hat = v2 / (1.0 - ADAM_B2 ** ADAM_STEP)
    delta = -ADAM_LR * (m_hat / (jnp.sqrt(v_hat) + ADAM_EPS) + ADAM_WD * w)
    return delta, m2, v2


def _adamw(t, parts, own, me_arr, w, m, v, layer, prev, rows, name):
    nl, nr, nc = w.shape

    def body(me_ref, p_ref, own_ref, w_ref, m_ref, v_ref, *rest):
        g_ref, d_ref, m2_ref, v2_ref = rest[-4:]
        me = me_ref[0]
        g = None
        for k in range(NDEV):
            term = jnp.where(me == k, own_ref[...], p_ref[k]).astype(F32)
            g = term if g is None else g + term
        delta, m2, v2 = _adamw_math(w_ref[...], g, m_ref[...], v_ref[...])
        g_ref[...] = g
        d_ref[...] = delta
        m2_ref[...] = m2
        v2_ref[...] = v2

    blk = pl.BlockSpec((None, rows, nc), lambda i, mm: (layer, i, 0))
    pblk = pl.BlockSpec((NDEV, rows, nc), lambda i, mm: (0, i, 0))
    extra = [] if prev is None else list(prev)
    return pl.pallas_call(
        body, grid_spec=pltpu.PrefetchScalarGridSpec(
            num_scalar_prefetch=1, grid=(nr // rows,),
            in_specs=[pblk, _own_block_spec(t, rows, lambda mm: mm[0]), blk, blk, blk] + [ANY] * len(extra),
            out_specs=[blk] * 4),
        out_shape=[jax.ShapeDtypeStruct(w.shape, F32)] * 4,
        input_output_aliases={6 + k: k for k in range(len(extra))},
        compiler_params=_cp("arbitrary"), name=name)(me_arr, parts, own, w, m, v, *extra)


SMALL_REPL = (("rel_bias", NUM_BUCKETS * N_BIAS_HEADS), ("attn_pre_norm", DEPTH * D), ("sinks", DEPTH * 8),
              ("attn_post_norm", DEPTH * D), ("ffn_pre_norm", DEPTH * D), ("conv_b", DEPTH * 2 * D_FF),
              ("ffn_post_norm", DEPTH * D))
SMALL_SHARD = (("b_gate", (DEPTH, 3, D), 128), ("conv_w", (DEPTH, 3, 2 * D_FF), 1024))


def _pack(vecs):
    flat = jnp.concatenate([v.reshape(-1).astype(F32) for v in vecs])
    n = flat.shape[0]
    rows = -(-n // (8 * LANES)) * 8
    return jnp.pad(flat, (0, rows * LANES - n)).reshape(rows, LANES)


def _unpack(packed, sizes):
    flat = packed.reshape(-1)
    out, off = [], 0
    for sz in sizes:
        out.append(flat[off:off + sz])
        off += sz
    return out


ROWPACK = (("rel_bias", 32, 32, (NUM_BUCKETS, N_BIAS_HEADS)), ("sinks", 8, 8, (DEPTH, 8)),
           ("attn_pre_norm", 16, 16, (DEPTH, D)), ("attn_post_norm", 16, 16, (DEPTH, D)),
           ("ffn_pre_norm", 16, 16, (DEPTH, D)), ("ffn_post_norm", 16, 16, (DEPTH, D)),
           ("conv_b", 128, 128, (DEPTH, 2 * D_FF)), ("b_gate", 48, 8, (DEPTH, 3, 128)),
           ("conv_w", 384, 48, (DEPTH, 3, 1024)))
ROWS_FULL = sum(r for _, r, _, _ in ROWPACK)
ROWS_OWN = sum(r for _, _, r, _ in ROWPACK)


def _as_rows(a, rows):
    a = a.astype(F32)
    if a.shape[-1] < LANES:
        a = jnp.pad(a.reshape(-1, a.shape[-1]), ((0, 0), (0, LANES - a.shape[-1])))
    a = a.reshape(-1, LANES)
    return jnp.pad(a, ((0, rows - a.shape[0]), (0, 0)))


def _rowpack(arrs, own):
    return jnp.concatenate([_as_rows(arrs[nm], ro if own else rf) for nm, rf, ro, _ in ROWPACK], axis=0)


def _small_update(parts, w, m, v, me_arr, name):
    nsm = len(ROWPACK)

    def body(me_ref, p_ref, w_ref, m_ref, v_ref, *rest):
        outs = rest[:4 * nsm]
        gfull, g_s, d_s, m_s, v_s = rest[4 * nsm:]
        me = me_ref[0]
        g = p_ref[0]
        for k in range(1, NDEV):
            g = g + p_ref[k]
        gfull[...] = g
        of, oo = 0, 0
        for nm, rf, ro, _ in ROWPACK:
            if nm == "b_gate":
                g_s[oo:oo + ro, :] = jnp.zeros((ro, LANES), F32)
                for r in range(DEPTH * 3):
                    g_s[oo + r:oo + r + 1, :] = gfull[pl.ds(of + r * NDEV + me, 1), :]
            elif nm == "conv_w":
                for r in range(DEPTH * 3):
                    g_s[oo + r * 8:oo + r * 8 + 8, :] = gfull[pl.ds(pl.multiple_of(of + r * 64 + me * 8, 8), 8), :]
            else:
                g_s[oo:oo + ro, :] = gfull[of:of + rf, :]
            of, oo = of + rf, oo + ro
        delta, m2, v2 = _adamw_math(w_ref[...], g_s[...], m_ref[...], v_ref[...])
        d_s[...] = delta
        m_s[...] = m2
        v_s[...] = v2
        for kind, src in enumerate((g_s, d_s, m_s, v_s)):
            oo = 0
            for idx, (nm, rf, ro, shp) in enumerate(ROWPACK):
                o_ref = outs[kind * nsm + idx]
                if nm in ("rel_bias", "sinks"):
                    o_ref[...] = src[oo:oo + shp[0], 0:shp[1]]
                elif nm == "b_gate":
                    for l in range(DEPTH):
                        o_ref[l] = src[oo + 3 * l:oo + 3 * l + 3, :]
                elif nm == "conv_w":
                    for l in range(DEPTH):
                        for k in range(8):
                            o_ref[l, :, k * LANES:(k + 1) * LANES] = src[pl.ds(oo + 24 * l + k, 3, stride=8), :]
                else:
                    per = shp[1] // LANES
                    for k in range(per):
                        o_ref[:, k * LANES:(k + 1) * LANES] = src[pl.ds(oo + k, DEPTH, stride=per), :]
                oo += ro

    vm = pl.BlockSpec(memory_space=pltpu.VMEM)
    shapes = [jax.ShapeDtypeStruct(shp, F32) for _ in range(4) for _, _, _, shp in ROWPACK]
    outs = pl.pallas_call(
        body, in_specs=[SMEM, vm, vm, vm, vm], out_specs=[vm] * (4 * nsm), out_shape=shapes,
        scratch_shapes=[pltpu.VMEM((ROWS_FULL, LANES), F32)] + [pltpu.VMEM((ROWS_OWN, LANES), F32)] * 4,
        name=name)(me_arr, parts, w, m, v)
    names = [nm for nm, _, _, _ in ROWPACK]
    return [dict(zip(names, outs[kind * nsm:(kind + 1) * nsm])) for kind in range(4)]


def kernel(x, rel_bias, attn_pre_norm, w_in, b_gate, sinks, w_br_a, w_br_b, w_br_c, w_out, attn_post_norm, ffn_pre_norm, w_up, conv_w, conv_b, w_down, ffn_post_norm, loss_target, m_rel_bias, m_attn_pre_norm, m_w_in, m_b_gate, m_sinks, m_w_br_a, m_w_br_b, m_w_br_c, m_w_out, m_attn_post_norm, m_ffn_pre_norm, m_w_up, m_conv_w, m_conv_b, m_w_down, m_ffn_post_norm, v_rel_bias, v_attn_pre_norm, v_w_in, v_b_gate, v_sinks, v_w_br_a, v_w_br_b, v_w_br_c, v_w_out, v_attn_post_norm, v_ffn_pre_norm, v_w_up, v_conv_w, v_conv_b, v_w_down, v_ffn_post_norm):
    P = dict(rel_bias=rel_bias, attn_pre_norm=attn_pre_norm, w_in=w_in, b_gate=b_gate, sinks=sinks, w_br_a=w_br_a,
             w_br_b=w_br_b, w_br_c=w_br_c, w_out=w_out, attn_post_norm=attn_post_norm, ffn_pre_norm=ffn_pre_norm,
             w_up=w_up, conv_w=conv_w, conv_b=conv_b, w_down=w_down, ffn_post_norm=ffn_post_norm)
    M = dict(rel_bias=m_rel_bias, attn_pre_norm=m_attn_pre_norm, w_in=m_w_in, b_gate=m_b_gate, sinks=m_sinks,
             w_br_a=m_w_br_a, w_br_b=m_w_br_b, w_br_c=m_w_br_c, w_out=m_w_out, attn_post_norm=m_attn_post_norm,
             ffn_pre_norm=m_ffn_pre_norm, w_up=m_w_up, conv_w=m_conv_w, conv_b=m_conv_b, w_down=m_w_down,
             ffn_post_norm=m_ffn_post_norm)
    V = dict(rel_bias=v_rel_bias, attn_pre_norm=v_attn_pre_norm, w_in=v_w_in, b_gate=v_b_gate, sinks=v_sinks,
             w_br_a=v_w_br_a, w_br_b=v_w_br_b, w_br_c=v_w_br_c, w_out=v_w_out, attn_post_norm=v_attn_post_norm,
             ffn_pre_norm=v_ffn_pre_norm, w_up=v_w_up, conv_w=v_conv_w, conv_b=v_conv_b, w_down=v_w_down,
             ffn_post_norm=v_ffn_post_norm)
    tr = lambda a: jnp.swapaxes(a, 1, 2)
    PB = {nm: (tr(P[nm]) if nm == "w_in" else P[nm]) for nm, _, _ in BIG}
    MB = {nm: (tr(M[nm]) if nm == "w_in" else M[nm]) for nm, _, _ in BIG}
    VB = {nm: (tr(V[nm]) if nm == "w_in" else V[nm]) for nm, _, _ in BIG}
    xi, yi, ci = _coords()
    me = 4 * xi + 2 * yi + ci

    me_arr = me.astype(jnp.int32).reshape(1)

    small_w = _pack([b_gate.reshape(-1), conv_w.reshape(-1)])
    (small_w_all,) = _exchange([small_w], [jax.ShapeDtypeStruct((NDEV,) + small_w.shape, F32)],
                               _whole, _slot, "gather_small_weights")

    groups = [tuple(l * NBIG + t for t in tids) for l in range(DEPTH) for _, tids in LAYER_GROUPS]
    cast = lambda i, m=me_arr: _cast_own(i, PB[BIG[i % NBIG][0]][i // NBIG], m, f"gather_own_l{i // NBIG}_{BIG[i % NBIG][0]}")
    first = list(groups[0])
    rest = [i for grp in groups[1:] for i in grp]
    sems0, _, lands0, tok_first = _xchg_start(None, [cast(i) for i in first], [tuple(range(len(first)))], None,
                                              _shard_window, small_w_all, "gather_start_first", rels=NEAR_RELS, tids=first)
    where_rest = {tid: k for k, tid in enumerate(rest)}
    me_rest = me_arr + tok_first[0, 0:1].astype(jnp.int32)
    sems1, _, lands1, g_tok = _xchg_start(None, [cast(i, me_rest) for i in rest],
                                          [tuple(where_rest[i] for i in grp) for grp in groups[1:]], None,
                                          _shard_window, lands0[0], "gather_start_rest", rels=NEAR_RELS, tids=rest)
    g_sems = list(sems0) + list(sems1)
    tok0 = g_tok[0:1, 0:1]
    lands_now = [None] * (DEPTH * NBIG)
    for i, a in zip(first + rest, list(lands0) + list(lands1)):
        lands_now[i] = a
    fwd_sems = {}
    fwd_plan = {0: (0,), 1: (1, 2), 3: (3, 4, 5)}

    def gather_waiter(gi, l, gname, tids):
        def wait(after):
            if gi in fwd_plan:
                gis = fwd_plan[gi]
                flat = [i for g2 in gis for i in groups[g2]]
                where = {tid: k for k, tid in enumerate(flat)}
                fs, new_lands, ftok = _gather_forward(
                    [g_sems[g2] for g2 in gis], [lands_now[i] for i in flat],
                    [[where[i] for i in groups[g2]] for g2 in gis], flat, after, _shard_window, f"gather_forward_{gi}")
                for g2, s in zip(gis, fs):
                    fwd_sems[g2] = s
                for i, a in zip(flat, new_lands):
                    lands_now[i] = a
                after = ftok
            ids = [l * NBIG + t for t in tids]
            _, got = _xchg_wait(fwd_sems[gi], None, [lands_now[i] for i in ids], ids, after,
                                None, _shard_window, f"gather_wait_l{l}_{gname}", rels=FAR_RELS)
            out = {}
            for t, arr in zip(tids, got):
                nm = BIG[t][0]
                out[nm] = arr
            return out
        return wait

    pending = [{gname: gather_waiter(l * len(LAYER_GROUPS) + k, l, gname, tids)
                for k, (gname, tids) in enumerate(LAYER_GROUPS)} for l in range(DEPTH)]
    nbg = DEPTH * 3 * 128
    ncw = DEPTH * 3 * 1024
    flat_all = small_w_all.reshape(NDEV, -1)
    b_gate_full = jnp.transpose(flat_all[:, :nbg].reshape(NDEV, DEPTH, 3, 128), (1, 2, 0, 3)).reshape(DEPTH, 3, D)
    conv_w_full = jnp.transpose(flat_all[:, nbg:nbg + ncw].reshape(NDEV, DEPTH, 3, 1024), (1, 2, 0, 3)).reshape(DEPTH, 3, 2 * D_FF)

    ws = []
    for l in range(DEPTH):
        ws.append(_Weights(dict(
            b_gate=b_gate_full[l], conv_w=conv_w_full[l].reshape(3, 2, D_FF), conv_b=conv_b[l].reshape(2, D_FF),
            sinks=sinks[l].reshape(1, 8),
            attn_pre_norm=attn_pre_norm[l].reshape(1, D), attn_post_norm=attn_post_norm[l].reshape(1, D),
            ffn_pre_norm=ffn_pre_norm[l].reshape(1, D), ffn_post_norm=ffn_post_norm[l].reshape(1, D)), pending[l]))

    rs = {}

    group_tids = dict(LAYER_GROUPS)

    def start_scatter(l, gname, grads_l):
        tids = group_tids[gname]
        blocks, lands_rs = [], []
        for t in tids:
            nm, ax, ext = BIG[t]
            gfull = grads_l[nm].astype(BF16)
            shp = (NDEV, ext, gfull.shape[1]) if ax == 0 else (NDEV, gfull.shape[0], ext)
            blocks.append(gfull)
            lands_rs.append(lax.empty(shp, BF16))
        local = list(range(len(tids)))
        win = lambda j, ref, k: _shard_window(tids[j], ref, k)
        sems, s_thru, l_thru, tok = _xchg_start(blocks, lands_rs, [tuple(local)], win, _slot, me_arr,
                                                f"scatter_start_l{l}_{gname}")
        rs[(l, gname)] = (sems[0], s_thru, l_thru, win, local)
        return tok[0:1, 0:1]

    loss_local, grad_x, grads, g_rel = _local_step(x[0], loss_target[0], ws, rel_bias, tok0, start_scatter)
    loss = lax.psum(loss_local, ("x", "y", "c"))

    stack = lambda nm: jnp.stack([grads[l][nm] for l in range(DEPTH)], axis=0)
    small_names = [nm for nm, _ in SMALL_REPL] + [nm for nm, _, _ in SMALL_SHARD]
    small_g = {"rel_bias": g_rel}
    for nm in small_names[1:]:
        small_g[nm] = stack(nm)
    small_packed = _rowpack(small_g, own=False)

    out_g, out_d, out_m, out_v = {}, {}, {}, {}
    prev = {nm: None for nm, _, _ in BIG}
    todo = [(l, gname) for l in reversed(range(DEPTH)) for gname in ("ffn", "mix", "in")]
    after, small_parts = grad_x, None
    for l, gname in todo:
        if (l, gname) == todo[-1]:
            (small_parts,) = _exchange([small_packed], [jax.ShapeDtypeStruct((NDEV,) + small_packed.shape, F32)],
                                       _whole, _slot, "gather_small_grads", after=after)
            after = small_parts
        sems, s_thru, l_thru, win, local = rs[(l, gname)]
        owns, parts = _xchg_wait(sems, s_thru, l_thru, local, after, win, _slot, f"scatter_wait_l{l}_{gname}")
        for t, own, prt in zip(group_tids[gname], owns, parts):
            nm = BIG[t][0]
            rows = SHARD_ROWS.get(nm, PB[nm].shape[1])
            prev[nm] = _adamw(t, prt, own, me_arr, PB[nm], MB[nm], VB[nm], l, prev[nm], rows, f"adamw_{nm}_l{l}")
            after = prev[nm][1]
    for nm, _, _ in BIG:
        out_g[nm], out_d[nm], out_m[nm], out_v[nm] = [tr(a) if nm == "w_in" else a for a in prev[nm]]
    sm_g, sm_d, sm_m, sm_v = _small_update(small_parts, _rowpack(P, True), _rowpack(M, True), _rowpack(V, True),
                                           me_arr, "small_update")
    for dst, src in ((out_g, sm_g), (out_d, sm_d), (out_m, sm_m), (out_v, sm_v)):
        dst.update(src)

    order = ["rel_bias", "attn_pre_norm", "w_in", "b_gate", "sinks", "w_br_a", "w_br_b", "w_br_c", "w_out",
             "attn_post_norm", "ffn_pre_norm", "w_up", "conv_w", "conv_b", "w_down", "ffn_post_norm"]
    return (loss, grad_x[None], *[out_g[k] for k in order], *[out_d[k] for k in order],
            *[out_m[k] for k in order], *[out_v[k] for k in order])
```

```python
import functools
import math

import numpy as np
import jax
import jax.numpy as jnp
from jax import lax
from jax.experimental import pallas as pl
from jax.experimental.pallas import tpu as pltpu

F32 = jnp.float32
BF16 = jnp.bfloat16

S = 2048
D = 1024
DEPTH = 2
NDEV = 8
HD = 64
BLK = 128
NB = S // BLK
A_GROUPS = ((128, 1), (512, 4), (2048, 16))
NUM_BUCKETS = 32
MAX_DISTANCE = 2048
N_BIAS_HEADS = 20
D_FF = 4096
IN_COLS = 6912
QKV_COLS = 3840
QKV_SLABS = QKV_COLS // 128
GATE_COLS = 3072
EPS = 1e-6
SCALE = HD ** -0.5
NEG = -1e30
LANES = 128

ADAM_LR = 0.001
ADAM_B1 = 0.9
ADAM_B2 = 0.999
ADAM_EPS = 1e-08
ADAM_WD = 0.01
ADAM_STEP = 10

VMEM_LIMIT = 56 * 1024 * 1024
MESH = pl.DeviceIdType.MESH
ANY = pl.BlockSpec(memory_space=pl.ANY)
SMEM = pl.BlockSpec(memory_space=pltpu.SMEM)


def _cp(*sem):
    return pltpu.CompilerParams(dimension_semantics=sem if sem else None, vmem_limit_bytes=VMEM_LIMIT)


def _dot(a, b, ca, cb):
    return lax.dot_general(a, b, (((ca,), (cb,)), ((), ())), preferred_element_type=F32)


def _mm(a, b, *, grid, a_spec, b_spec, out_shape, out_spec, ca, cb, acc_shape, name,
        a_slab=False, b_slab=False, out_slab=False, alias_out=None, after=None):
    nk = grid[2]

    def body(*refs):
        a_ref, b_ref = refs[0], refs[1]
        o_ref, acc_ref = refs[-2], refs[-1]
        k = pl.program_id(2)

        def load(ref, slab):
            if slab:
                return jnp.concatenate([ref[s] for s in range(ref.shape[0])], axis=1).astype(BF16)
            return ref[...].astype(BF16)

        def write(val):
            if out_slab:
                for s in range(o_ref.shape[0]):
                    o_ref[s] = val[:, s * LANES:(s + 1) * LANES].astype(o_ref.dtype)
            else:
                o_ref[...] = val.astype(o_ref.dtype)

        d = _dot(load(a_ref, a_slab), load(b_ref, b_slab), ca, cb)
        if nk == 1:
            write(d)
        elif direct:
            @pl.when(k == 0)
            def _():
                o_ref[...] = d

            @pl.when(k > 0)
            def _():
                o_ref[...] += d
        else:
            @pl.when(k == 0)
            def _():
                acc_ref[...] = d

            if nk > 2:
                @pl.when((k > 0) & (k < nk - 1))
                def _():
                    acc_ref[...] += d

            @pl.when(k == nk - 1)
            def _():
                write(acc_ref[...] + d)

    direct = (not out_slab) and out_shape.dtype == F32
    if nk == 1 or direct:
        acc_shape = (8, LANES)
    in_specs = [a_spec, b_spec]
    args = [a, b]
    aliases = {}
    if alias_out is not None:
        in_specs.append(ANY)
        args.append(alias_out)
        aliases = {2: 0}
    if after is not None:
        in_specs.append(ANY)
        args.append(after)
    return pl.pallas_call(
        body, grid=grid, in_specs=in_specs, out_specs=out_spec, out_shape=out_shape,
        scratch_shapes=[pltpu.VMEM(acc_shape, F32)], input_output_aliases=aliases,
        compiler_params=_cp("parallel", "parallel", "arbitrary"), name=name)(*args)


def _mm_nn(a, b, out_dtype, tm, tn, tk, name):
    m, kk = a.shape
    n = b.shape[1]
    return _mm(a, b, grid=(m // tm, n // tn, kk // tk),
               a_spec=pl.BlockSpec((tm, tk), lambda i, j, k: (i, k)),
               b_spec=pl.BlockSpec((tk, tn), lambda i, j, k: (k, j)),
               out_shape=jax.ShapeDtypeStruct((m, n), out_dtype),
               out_spec=pl.BlockSpec((tm, tn), lambda i, j, k: (i, j)),
               ca=1, cb=0, acc_shape=(tm, tn), name=name)


def _mm_nt(a, b, out_dtype, tm, tn, tk, name):
    m, kk = a.shape
    n = b.shape[0]
    return _mm(a, b, grid=(m // tm, n // tn, kk // tk),
               a_spec=pl.BlockSpec((tm, tk), lambda i, j, k: (i, k)),
               b_spec=pl.BlockSpec((tn, tk), lambda i, j, k: (j, k)),
               out_shape=jax.ShapeDtypeStruct((m, n), out_dtype),
               out_spec=pl.BlockSpec((tm, tn), lambda i, j, k: (i, j)),
               ca=1, cb=1, acc_shape=(tm, tn), name=name)


def _mm_tn(a, b, out_dtype, tm, tn, tk, name):
    kk, m = a.shape
    n = b.shape[1]
    return _mm(a, b, grid=(m // tm, n // tn, kk // tk),
               a_spec=pl.BlockSpec((tk, tm), lambda i, j, k: (k, i)),
               b_spec=pl.BlockSpec((tk, tn), lambda i, j, k: (k, j)),
               out_shape=jax.ShapeDtypeStruct((m, n), out_dtype),
               out_spec=pl.BlockSpec((tm, tn), lambda i, j, k: (i, j)),
               ca=0, cb=0, acc_shape=(tm, tn), name=name)


ROW_TILE = 256


def _rms(x, g):
    r = lax.rsqrt(jnp.mean(x * x, axis=-1, keepdims=True) + EPS)
    return x * r * g


def _prenorm(x, g, name):
    def body(x_ref, g_ref, o_ref):
        o_ref[...] = _rms(x_ref[...], g_ref[...]).astype(BF16)

    return pl.pallas_call(
        body, grid=(S // ROW_TILE,),
        in_specs=[pl.BlockSpec((ROW_TILE, D), lambda i: (i, 0)), pl.BlockSpec((1, D), lambda i: (0, 0))],
        out_specs=pl.BlockSpec((ROW_TILE, D), lambda i: (i, 0)),
        out_shape=jax.ShapeDtypeStruct((S, D), BF16), compiler_params=_cp("parallel"), name=name)(x, g)


def _postnorm_res(x, f, g_post, g_next, name):
    def body(x_ref, f_ref, gp_ref, gn_ref, xo_ref, ho_ref):
        xn = x_ref[...] + _rms(f_ref[...], gp_ref[...])
        xo_ref[...] = xn
        ho_ref[...] = _rms(xn, gn_ref[...]).astype(BF16)

    row = pl.BlockSpec((ROW_TILE, D), lambda i: (i, 0))
    vec = pl.BlockSpec((1, D), lambda i: (0, 0))
    return pl.pallas_call(
        body, grid=(S // ROW_TILE,), in_specs=[row, row, vec, vec], out_specs=[row, row],
        out_shape=[jax.ShapeDtypeStruct((S, D), F32), jax.ShapeDtypeStruct((S, D), BF16)],
        compiler_params=_cp("parallel"), name=name)(x, f, g_post, g_next)


def _norm_bwd(f, g, dys, res, out_dtype, name):
    ndy = len(dys)
    has_res = res is not None

    def body(*refs):
        f_ref, g_ref = refs[0], refs[1]
        dy_refs = refs[2:2 + ndy]
        res_ref = refs[2 + ndy] if has_res else None
        o_ref, dg_ref = refs[-2], refs[-1]
        fv = f_ref[...]
        dy = dy_refs[0][...].astype(F32)
        for r in dy_refs[1:]:
            dy = dy + r[...].astype(F32)
        r = lax.rsqrt(jnp.mean(fv * fv, axis=-1, keepdims=True) + EPS)
        n = fv * r
        dn = dy * g_ref[...]
        df = r * (dn - n * jnp.mean(dn * n, axis=-1, keepdims=True))
        if has_res:
            df = df + res_ref[...]
        o_ref[...] = df.astype(out_dtype)

        @pl.when(pl.program_id(0) == 0)
        def _():
            dg_ref[...] = jnp.zeros((1, D), F32)

        dg_ref[...] += jnp.sum(dy * n, axis=0, keepdims=True)

    row = pl.BlockSpec((ROW_TILE, D), lambda i: (i, 0))
    vec = pl.BlockSpec((1, D), lambda i: (0, 0))
    in_specs = [row, vec] + [row] * ndy + ([row] if has_res else [])
    args = [f, g] + list(dys) + ([res] if has_res else [])
    return pl.pallas_call(
        body, grid=(S // ROW_TILE,), in_specs=in_specs, out_specs=[row, vec],
        out_shape=[jax.ShapeDtypeStruct((S, D), out_dtype), jax.ShapeDtypeStruct((1, D), F32)],
        compiler_params=_cp("arbitrary"), name=name)(*args)


def _loss_head(y, target, name):
    def body(y_ref, t_ref, dy_ref, l_ref):
        e = y_ref[...] - t_ref[...]
        dy_ref[...] = e * (1.0 / D)

        @pl.when(pl.program_id(0) == 0)
        def _():
            l_ref[...] = jnp.zeros((8, LANES), F32)

        l_ref[...] += jnp.sum(e * e) * (0.5 / D)

    row = pl.BlockSpec((ROW_TILE, D), lambda i: (i, 0))
    return pl.pallas_call(
        body, grid=(S // ROW_TILE,), in_specs=[row, row],
        out_specs=[row, pl.BlockSpec((8, LANES), lambda i: (0, 0))],
        out_shape=[jax.ShapeDtypeStruct((S, D), F32), jax.ShapeDtypeStruct((8, LANES), F32)],
        compiler_params=_cp("arbitrary"), name=name)(y, target)


def _bucket_tiles():
    a = np.arange(BLK)[:, None]
    b = np.arange(2 * BLK)[None, :]
    dist = a + BLK - b
    out = np.zeros((4, 2, BLK, 2 * BLK), np.int32)
    cfg = [(w // d, d) for w, d in A_GROUPS] + [(BLK - 1, 1)]
    for gi, (max_dist, d) in enumerate(cfg):
        band = (dist >= 0) & (dist <= max_dist)
        tok = np.maximum(dist, 0) * d
        nf = np.maximum(tok, 1).astype(np.float32)
        max_exact = NUM_BUCKETS // 2
        large = max_exact + (np.log(nf / np.float32(max_exact)) / np.float32(math.log(MAX_DISTANCE / max_exact))
                             * np.float32(NUM_BUCKETS - max_exact)).astype(np.int32)
        large = np.minimum(large, NUM_BUCKETS - 1)
        bkt = np.where(tok < max_exact, tok, large).astype(np.int32)
        full = np.where(band, bkt, -1)
        out[gi, 1] = full
        out[gi, 0] = np.where(b >= BLK, full, -1)
    return out


def _bias_tiles(rel_bias, buckets, name):
    def body(tab_ref, bkt_ref, o_ref):
        h = pl.program_id(0)
        bkt = bkt_ref[...]
        acc = jnp.zeros(bkt.shape, F32)
        for bb in range(NUM_BUCKETS):
            acc = jnp.where(bkt == bb, tab_ref[bb, h], acc)
        o_ref[...] = jnp.where(bkt < 0, NEG, acc)

    return pl.pallas_call(
        body, grid=(N_BIAS_HEADS,),
        in_specs=[SMEM, pl.BlockSpec((None, 2, BLK, 2 * BLK), lambda h: (jnp.minimum(h // 4, 3), 0, 0, 0))],
        out_specs=pl.BlockSpec((None, 2, BLK, 2 * BLK), lambda h: (h, 0, 0, 0)),
        out_shape=jax.ShapeDtypeStruct((N_BIAS_HEADS, 2, BLK, 2 * BLK), F32),
        compiler_params=_cp("arbitrary"), name=name)(rel_bias, buckets)


def _bias_grad(gs, buckets, name):
    ng = len(gs)

    def body(*refs):
        g_refs = refs[:ng]
        bkt_ref, o_ref = refs[ng], refs[ng + 1]
        h = pl.program_id(0)
        g = g_refs[0][...]
        for r in g_refs[1:]:
            g = g + r[...]
        bkt = bkt_ref[...]
        row = lax.broadcasted_iota(jnp.int32, (NUM_BUCKETS, LANES), 0)
        lane = lax.broadcasted_iota(jnp.int32, (NUM_BUCKETS, LANES), 1)

        @pl.when(h == 0)
        def _():
            o_ref[...] = jnp.zeros((NUM_BUCKETS, LANES), F32)

        acc = o_ref[...]
        for bb in range(NUM_BUCKETS):
            s = jnp.sum(jnp.where(bkt == bb, g, 0.0))
            acc = jnp.where((row == bb) & (lane == h), s, acc)
        o_ref[...] = acc

    g_spec = pl.BlockSpec((None, BLK, 2 * BLK), lambda h: (h, 0, 0))
    return pl.pallas_call(
        body, grid=(N_BIAS_HEADS,),
        in_specs=[g_spec] * ng + [pl.BlockSpec((None, None, BLK, 2 * BLK), lambda h: (jnp.minimum(h // 4, 3), 1, 0, 0))],
        out_specs=pl.BlockSpec((NUM_BUCKETS, LANES), lambda h: (0, 0)),
        out_shape=jax.ShapeDtypeStruct((NUM_BUCKETS, LANES), F32),
        compiler_params=_cp("arbitrary"), name=name)(*gs, buckets)


def _to_class_major(src_ref, dst_refs, d, fn=None):
    ln = S // d
    for r in range(d):
        v = src_ref[pl.ds(r, ln, stride=d), :] if d > 1 else src_ref[...]
        outs = fn(v) if fn is not None else (v,) * len(dst_refs)
        for dst, o in zip(dst_refs, outs):
            dst[pl.ds(r * ln, ln), :] = o.astype(dst.dtype)


def _head_masks(rows):
    lane = lax.broadcasted_iota(jnp.int32, (rows, LANES), 1)
    return lane < HD, lane >= HD


def _split_heads(v):
    m0, m1 = _head_masks(v.shape[0])
    return jnp.where(m0, v, 0.0), jnp.where(m1, v, 0.0)


def _dup_head(v, hi):
    m0, _ = _head_masks(v.shape[0])
    r = pltpu.roll(v, HD, 1)
    return jnp.where(m0, jnp.where(hi, r, v), jnp.where(hi, v, r))


def _block_rows(b, d):
    nbc = NB // d
    i = b % nbc
    r = b // nbc
    has_prev = (i > 0).astype(jnp.int32)
    prev = pl.multiple_of(jnp.maximum(b - 1, 0) * BLK, BLK)
    nat = i * (BLK * d) + r
    return has_prev, prev, nat


def _lane_halves(v0, v1):
    lane = lax.broadcasted_iota(jnp.int32, (v0.shape[0], LANES), 1)
    return jnp.where(lane < HD, v0, v1)


def _band_fwd(proj, bias, sinks, *, d, q0, k0, v0, npairs, bias0, shared_kv, name):
    def body(sink_ref, q_ref, k_ref, v_ref, b_ref, num_ref, st_ref, qz0, qz1, ks, vs):
        p = pl.program_id(0)
        kv = (lambda v: (_dup_head(v, p >= 2),)) if shared_kv else None
        _to_class_major(q_ref, (qz0, qz1), d, lambda v: _split_heads(v * SCALE))
        _to_class_major(k_ref, (ks,), d, kv)
        _to_class_major(v_ref, (vs,), d, kv)
        lane = lax.broadcasted_iota(jnp.int32, (BLK, LANES), 1)

        def blk(b, carry):
            has_prev, prev, nat = _block_rows(b, d)
            cur = pl.multiple_of(b * BLK, BLK)
            k2 = jnp.concatenate([ks[pl.ds(prev, BLK), :], ks[pl.ds(cur, BLK), :]], axis=0)
            v2 = jnp.concatenate([vs[pl.ds(prev, BLK), :], vs[pl.ds(cur, BLK), :]], axis=0)
            nums, ms, ls = [], [], []
            for hh, qz in enumerate((qz0, qz1)):
                z = _dot(qz[pl.ds(cur, BLK), :], k2, 1, 1) + b_ref[hh, has_prev]
                m = jnp.max(z, axis=1, keepdims=True)
                e = jnp.exp(z - m)
                l = jnp.sum(e, axis=1, keepdims=True)
                num = _dot(e.astype(BF16), v2, 1, 0)
                if shared_kv:
                    sink = sink_ref[0, 2 * p + hh]
                    mx = jnp.maximum(m, sink)
                    c = jnp.exp(m - mx)
                    zden = l * c + jnp.exp(sink - mx)
                    num = num * (c / zden)
                    m = mx + jnp.log(zden)
                ls.append(l)
                ms.append(m)
                nums.append(num)
            num_t = jnp.where(lane < HD, nums[0], nums[1])
            if shared_kv:
                st_t = jnp.where(lane < HD, ms[0], ms[1])
            else:
                st_t = jnp.where(lane < 32, ms[0], jnp.where(lane < 64, ls[0], jnp.where(lane < 96, ms[1], ls[1])))
            if d > 1:
                num_ref[pl.ds(nat, BLK, stride=d), :] = num_t
                st_ref[pl.ds(nat, BLK, stride=d), :] = st_t
            else:
                num_ref[pl.ds(cur, BLK), :] = num_t
                st_ref[pl.ds(cur, BLK), :] = st_t
            return carry

        lax.fori_loop(0, NB, blk, 0, unroll=8)

    slab = lambda off, per_pair: pl.BlockSpec((None, S, LANES), (lambda p: (off + p, 0, 0)) if per_pair else (lambda p: (off, 0, 0)))
    out = pl.BlockSpec((None, S, LANES), lambda p: (p, 0, 0))
    return pl.pallas_call(
        body, grid=(npairs,),
        in_specs=[SMEM, slab(q0, True), slab(k0, not shared_kv), slab(v0, not shared_kv),
                  pl.BlockSpec((None, 2, 2, BLK, 2 * BLK), lambda p: (bias0 + p, 0, 0, 0, 0))],
        out_specs=[out, out],
        out_shape=[jax.ShapeDtypeStruct((npairs, S, LANES), F32)] * 2,
        scratch_shapes=[pltpu.VMEM((S, LANES), BF16)] * 4,
        compiler_params=_cp("arbitrary"), name=name)(sinks, proj, proj, proj, bias)


def _combine_a(nums, stats, name):
    rt = 512

    def body(n0, n1, n2, s0, s1, s2, o_ref, l_ref):
        n_refs, s_refs = (n0, n1, n2), (s0, s1, s2)
        outs, lses = [], []
        for hh in range(2):
            ms = [s[:, 64 * hh:64 * hh + 1] for s in s_refs]
            ls = [s[:, 64 * hh + 32:64 * hh + 33] for s in s_refs]
            mx = jnp.maximum(jnp.maximum(ms[0], ms[1]), ms[2])
            cs = [jnp.exp(m - mx) for m in ms]
            z = cs[0] * ls[0] + cs[1] * ls[1] + cs[2] * ls[2]
            acc = cs[0] * n_refs[0][:, hh * HD:(hh + 1) * HD]
            acc = acc + cs[1] * n_refs[1][:, hh * HD:(hh + 1) * HD]
            acc = acc + cs[2] * n_refs[2][:, hh * HD:(hh + 1) * HD]
            outs.append(acc / z)
            lses.append(mx + jnp.log(z))
        o_ref[...] = jnp.concatenate(outs, axis=1)
        l_ref[...] = _lane_halves(lses[0], lses[1])

    spec = pl.BlockSpec((None, rt, LANES), lambda p, i: (p, i, 0))
    return pl.pallas_call(
        body, grid=(2, S // rt), in_specs=[spec] * 6, out_specs=[spec, spec],
        out_shape=[jax.ShapeDtypeStruct((2, S, LANES), F32)] * 2,
        compiler_params=_cp("parallel", "parallel"), name=name)(*nums, *stats)


def _band_bwd(proj, bias, o, do, lse, sinks, *, d, q0, k0, v0, npairs, bias0, shared_kv, name):
    nkv = 1 if shared_kv else npairs

    def body(sink_ref, q_ref, k_ref, v_ref, b_ref, o_ref, do_ref, lse_ref,
             dq_ref, dk_ref, dv_ref, g_ref, ds_ref,
             qz0, qz1, ks, vs, doz0, doz1, ls0, ls1, dls0, dls1, stage, dq_nat, dk_cm, dv_cm, kv_nat, dk_acc, dv_acc):
        p = pl.program_id(0)
        m0, m1 = _head_masks(S)
        prod = do_ref[...] * o_ref[...]
        dl0 = jnp.sum(jnp.where(m0, prod, 0.0), axis=1, keepdims=True)
        dl1 = jnp.sum(jnp.where(m1, prod, 0.0), axis=1, keepdims=True)
        if shared_kv:
            row8 = lax.broadcasted_iota(jnp.int32, (8, LANES), 0)
            lane8 = lax.broadcasted_iota(jnp.int32, (8, LANES), 1)
            t = jnp.zeros((8, LANES), F32)
            lv = lse_ref[...]
            for hh in range(2):
                sink = sink_ref[0, 2 * p + hh]
                ps = jnp.exp(sink - lv[:, 64 * hh:64 * hh + 1])
                dsink = -jnp.sum(ps * (dl0 if hh == 0 else dl1))
                t = jnp.where((row8 == 0) & (lane8 == hh), dsink, t)
            ds_ref[...] = t
        else:
            ds_ref[...] = jnp.zeros((8, LANES), F32)
        kv = (lambda v: (_dup_head(v, p >= 2),)) if shared_kv else None
        _to_class_major(q_ref, (qz0, qz1), d, lambda v: _split_heads(v * SCALE))
        _to_class_major(k_ref, (ks,), d, kv)
        _to_class_major(v_ref, (vs,), d, kv)
        _to_class_major(do_ref, (doz0, doz1), d, _split_heads)
        def spread(v):
            a0, a1 = _head_masks(v.shape[0])
            r = pltpu.roll(v, HD, 1)
            return jnp.where(a0, v, r), jnp.where(a1, v, r)

        _to_class_major(lse_ref, (ls0, ls1), d, spread)
        stage[...] = jnp.where(m0, dl0, dl1)
        _to_class_major(stage, (dls0, dls1), d, spread)

        dk_cm[...] = jnp.zeros((S, LANES), F32)
        dv_cm[...] = jnp.zeros((S, LANES), F32)
        g_ref[...] = jnp.zeros((2, BLK, 2 * BLK), F32)
        lane = lax.broadcasted_iota(jnp.int32, (BLK, LANES), 1)

        def blk(b, carry):
            has_prev, prev, nat = _block_rows(b, d)
            cur = pl.multiple_of(b * BLK, BLK)
            k2 = jnp.concatenate([ks[pl.ds(prev, BLK), :], ks[pl.ds(cur, BLK), :]], axis=0)
            v2 = jnp.concatenate([vs[pl.ds(prev, BLK), :], vs[pl.ds(cur, BLK), :]], axis=0)
            dqs, dks, dvs = [], [], []
            for hh, (qz, doz, lsr, dlr) in enumerate(((qz0, doz0, ls0, dls0), (qz1, doz1, ls1, dls1))):
                qb = qz[pl.ds(cur, BLK), :]
                dob = doz[pl.ds(cur, BLK), :]
                lb = lsr[pl.ds(cur, BLK), :]
                dlb = dlr[pl.ds(cur, BLK), :]
                z = _dot(qb, k2, 1, 1) + b_ref[hh, has_prev]
                pr = jnp.exp(z - jnp.concatenate([lb, lb], axis=1))
                dp = _dot(dob, v2, 1, 1)
                dz = pr * (dp - jnp.concatenate([dlb, dlb], axis=1))
                g_ref[hh] += dz
                dzb = dz.astype(BF16)
                dqs.append(_dot(dzb, k2, 1, 0))
                dks.append(_dot(dzb, qb, 0, 0))
                dvs.append(_dot(pr.astype(BF16), dob, 0, 0))
            dq_t = jnp.where(lane < HD, dqs[0], dqs[1]) * SCALE
            dk_t = dks[0] + dks[1]
            dv_t = dvs[0] + dvs[1]
            dk_cm[pl.ds(prev, BLK), :] += dk_t[:BLK]
            dk_cm[pl.ds(cur, BLK), :] += dk_t[BLK:]
            dv_cm[pl.ds(prev, BLK), :] += dv_t[:BLK]
            dv_cm[pl.ds(cur, BLK), :] += dv_t[BLK:]
            if d > 1:
                dq_nat[pl.ds(nat, BLK, stride=d), :] = dq_t
            else:
                dq_nat[pl.ds(cur, BLK), :] = dq_t
            return carry

        lax.fori_loop(0, NB, blk, 0, unroll=8)
        dq_ref[...] = dq_nat[...].astype(BF16)

        def from_class_major(src, dst_ref):
            if d == 1:
                dst_ref[...] = src[...].astype(BF16)
            else:
                ln = S // d
                for r in range(d):
                    kv_nat[pl.ds(r, ln, stride=d), :] = src[pl.ds(r * ln, ln), :]
                dst_ref[...] = kv_nat[...].astype(BF16)

        if not shared_kv:
            from_class_major(dk_cm, dk_ref)
            from_class_major(dv_cm, dv_ref)
        else:
            @pl.when(p == 0)
            def _():
                dk_acc[...] = jnp.zeros((S, LANES), F32)
                dv_acc[...] = jnp.zeros((S, LANES), F32)

            mine = m1 == (p >= 2)
            for cm, acc in ((dk_cm, dk_acc), (dv_cm, dv_acc)):
                val = cm[...]
                acc[...] += jnp.where(mine, val + pltpu.roll(val, HD, 1), 0.0)

            @pl.when(p == npairs - 1)
            def _():
                from_class_major(dk_acc, dk_ref)
                from_class_major(dv_acc, dv_ref)

    slab = lambda off, per_pair: pl.BlockSpec((None, S, LANES), (lambda p: (off + p, 0, 0)) if per_pair else (lambda p: (off, 0, 0)))
    pair = pl.BlockSpec((None, S, LANES), lambda p: (p, 0, 0))
    kv_out = pair if not shared_kv else pl.BlockSpec((None, S, LANES), lambda p: (0, 0, 0))
    return pl.pallas_call(
        body, grid=(npairs,),
        in_specs=[SMEM, slab(q0, True), slab(k0, not shared_kv), slab(v0, not shared_kv),
                  pl.BlockSpec((None, 2, 2, BLK, 2 * BLK), lambda p: (bias0 + p, 0, 0, 0, 0)),
                  pair, pair, pair],
        out_specs=[pair, kv_out, kv_out,
                   pl.BlockSpec((None, 2, BLK, 2 * BLK), lambda p: (p, 0, 0, 0)),
                   pl.BlockSpec((None, 8, LANES), lambda p: (p, 0, 0))],
        out_shape=[jax.ShapeDtypeStruct((npairs, S, LANES), BF16),
                   jax.ShapeDtypeStruct((nkv, S, LANES), BF16),
                   jax.ShapeDtypeStruct((nkv, S, LANES), BF16),
                   jax.ShapeDtypeStruct((npairs, 2, BLK, 2 * BLK), F32),
                   jax.ShapeDtypeStruct((npairs, 8, LANES), F32)],
        scratch_shapes=[pltpu.VMEM((S, LANES), BF16)] * 6 + [pltpu.VMEM((S, LANES), F32)] * 11,
        compiler_params=_cp("arbitrary"), name=name)(sinks, proj, proj, proj, bias, o, do, lse)


KC = 512
NSUB = KC // BLK
QB = 512
QPG = KC // QB


def _split2(x):
    hi = x.astype(BF16)
    lo = (x - hi.astype(F32)).astype(BF16)
    return hi, lo


def _tri_ones(cmp):
    jj = lax.broadcasted_iota(jnp.int32, (2 * BLK, BLK), 0) % BLK
    ss = lax.broadcasted_iota(jnp.int32, (2 * BLK, BLK), 1)
    return jnp.concatenate([cmp(jj, ss).astype(BF16), jnp.ones((2 * BLK, BLK), BF16)], axis=1)


def _sub_sums(x, tri1):
    n = x.shape[0]
    st = jnp.concatenate([x[:, s * BLK:(s + 1) * BLK] for s in range(NSUB)], axis=0)
    hi, lo = _split2(st)
    r = _dot(jnp.concatenate([hi, lo], axis=1), tri1, 1, 0)
    return ([r[s * n:(s + 1) * n, :BLK] for s in range(NSUB)], [r[s * n:(s + 1) * n, BLK:] for s in range(NSUB)])


def _log_sig_pair(z):
    lb = jnp.minimum(z, 0.0) - jnp.log1p(jnp.exp(-jnp.abs(z)))
    return lb, lb - z


QGROUPS = NB // NSUB


def _stick_fwd(proj, *, q0, k0, v0, name):
    def body(q_ref, k_ref, v_ref, o_ref, t_ref, qs, ks, vs):
        qs[...] = (q_ref[...] * SCALE).astype(BF16)
        ks[...] = k_ref[...].astype(BF16)
        vs[...] = v_ref[...].astype(BF16)
        tri1 = _tri_ones(lambda j, s: j > s)
        col = lax.broadcasted_iota(jnp.int32, (QB, KC), 1)
        rowi = lax.broadcasted_iota(jnp.int32, (QB, KC), 0)

        for qg in range(QGROUPS):
            def qblock(ii, carry0, qg=qg):
                t0 = pl.multiple_of((qg * QPG + ii) * QB, QB)
                qb = qs[pl.ds(t0, QB), :]
                accs = [jnp.zeros((QB, HD), F32)] * 2
                runs = [jnp.zeros((QB, BLK), F32)] * 2
                for c in reversed(range(qg + 1)):
                    s0 = c * KC
                    diag = c == qg
                    before = (s0 + col) < (t0 + rowi) if diag else None
                    for hh in range(2):
                        kh = ks[s0:s0 + KC, hh * HD:(hh + 1) * HD]
                        vh = vs[s0:s0 + KC, hh * HD:(hh + 1) * HD]
                        lb, lk = _log_sig_pair(_dot(qb[:, hh * HD:(hh + 1) * HD], kh, 1, 1))
                        if diag:
                            lk = jnp.where(before, lk, 0.0)
                        suf, tot = _sub_sums(lk, tri1)
                        ws, run = [], runs[hh]
                        for s in reversed(range(NSUB)):
                            ws.append(jnp.exp(lb[:, s * BLK:(s + 1) * BLK] + suf[s] + run))
                            run = run + tot[s]
                        w = jnp.concatenate(ws[::-1], axis=1)
                        if diag:
                            w = jnp.where(before, w, 0.0)
                        accs[hh] = accs[hh] + _dot(w.astype(BF16), vh, 1, 0)
                        runs[hh] = run
                o_ref[pl.ds(t0, QB), :] = jnp.concatenate(accs, axis=1)
                t_ref[pl.ds(t0, QB), :] = _lane_halves(runs[0], runs[1])
                return carry0

            lax.fori_loop(0, QPG, qblock, 0)

    slab = lambda off: pl.BlockSpec((None, S, LANES), lambda p: (off + p, 0, 0))
    out = pl.BlockSpec((None, S, LANES), lambda p: (p, 0, 0))
    return pl.pallas_call(
        body, grid=(2,), in_specs=[slab(q0), slab(k0), slab(v0)], out_specs=[out, out],
        out_shape=[jax.ShapeDtypeStruct((2, S, LANES), F32)] * 2,
        scratch_shapes=[pltpu.VMEM((S, LANES), BF16)] * 3,
        compiler_params=_cp("arbitrary"), name=name)(proj, proj, proj)


def _stick_bwd(proj, do, tot, *, q0, k0, v0, name):
    def body(q_ref, k_ref, v_ref, do_ref, t_ref, dq_ref, dk_ref, dv_ref, qs, ks, vs, dos, dk_acc, dv_acc):
        qs[...] = (q_ref[...] * SCALE).astype(BF16)
        ks[...] = k_ref[...].astype(BF16)
        vs[...] = v_ref[...].astype(BF16)
        dos[...] = do_ref[...].astype(BF16)
        dk_acc[...] = jnp.zeros((2, S, HD), F32)
        dv_acc[...] = jnp.zeros((2, S, HD), F32)
        tri_inc = _tri_ones(lambda j, s: j <= s)
        tri_exc = _tri_ones(lambda j, s: j < s)
        col = lax.broadcasted_iota(jnp.int32, (QB, KC), 1)
        rowi = lax.broadcasted_iota(jnp.int32, (QB, KC), 0)

        for qg in range(QGROUPS):
            def qblock(ii, carry0, qg=qg):
                t0 = pl.multiple_of((qg * QPG + ii) * QB, QB)
                qb = qs[pl.ds(t0, QB), :]
                dob = dos[pl.ds(t0, QB), :]
                tb = t_ref[pl.ds(t0, QB), :]
                dqs = [jnp.zeros((QB, HD), F32)] * 2
                pruns = [jnp.zeros((QB, BLK), F32)] * 2
                eruns = [jnp.zeros((QB, BLK), F32)] * 2
                for c in range(qg + 1):
                    s0 = c * KC
                    diag = c == qg
                    before = (s0 + col) < (t0 + rowi) if diag else None
                    for hh in range(2):
                        qh = qb[:, hh * HD:(hh + 1) * HD]
                        doh = dob[:, hh * HD:(hh + 1) * HD]
                        tt = tb[:, 64 * hh:64 * hh + 1]
                        kh = ks[s0:s0 + KC, hh * HD:(hh + 1) * HD]
                        vh = vs[s0:s0 + KC, hh * HD:(hh + 1) * HD]
                        lb, lk = _log_sig_pair(_dot(qh, kh, 1, 1))
                        if diag:
                            lk = jnp.where(before, lk, 0.0)
                        pin, ptot = _sub_sums(lk, tri_inc)
                        ws, prun = [], pruns[hh]
                        for s in range(NSUB):
                            ws.append(jnp.exp(lb[:, s * BLK:(s + 1) * BLK] + (tt - (pin[s] + prun))))
                            prun = prun + ptot[s]
                        w = jnp.concatenate(ws, axis=1)
                        if diag:
                            w = jnp.where(before, w, 0.0)
                        e = w * _dot(doh, vh, 1, 1)
                        pex, etot = _sub_sums(e, tri_exc)
                        cs, erun = [], eruns[hh]
                        for s in range(NSUB):
                            cs.append(pex[s] + erun)
                            erun = erun + etot[s]
                        sig = jnp.exp(lb)
                        dz = e * (1.0 - sig) - jnp.concatenate(cs, axis=1) * sig
                        if diag:
                            dz = jnp.where(before, dz, 0.0)
                        dz = dz.astype(BF16)
                        dqs[hh] = dqs[hh] + _dot(dz, kh, 1, 0)
                        dk_acc[hh, s0:s0 + KC, :] += _dot(dz, qh, 0, 0)
                        dv_acc[hh, s0:s0 + KC, :] += _dot(w.astype(BF16), doh, 0, 0)
                        pruns[hh], eruns[hh] = prun, erun
                dq_ref[pl.ds(t0, QB), :] = (jnp.concatenate(dqs, axis=1) * SCALE).astype(BF16)
                return carry0

            lax.fori_loop(0, QPG, qblock, 0)
        dk_ref[...] = jnp.concatenate([dk_acc[0], dk_acc[1]], axis=1).astype(BF16)
        dv_ref[...] = jnp.concatenate([dv_acc[0], dv_acc[1]], axis=1).astype(BF16)

    slab = lambda off: pl.BlockSpec((None, S, LANES), lambda p: (off + p, 0, 0))
    pair = pl.BlockSpec((None, S, LANES), lambda p: (p, 0, 0))
    return pl.pallas_call(
        body, grid=(2,), in_specs=[slab(q0), slab(k0), slab(v0), pair, pair], out_specs=[pair] * 3,
        out_shape=[jax.ShapeDtypeStruct((2, S, LANES), BF16)] * 3,
        scratch_shapes=[pltpu.VMEM((S, LANES), BF16)] * 4 + [pltpu.VMEM((2, S, HD), F32)] * 2,
        compiler_params=_cp("arbitrary"), name=name)(proj, proj, proj, do, tot)


def _cat_slabs(ref):
    return jnp.concatenate([ref[s] for s in range(ref.shape[0])], axis=1)


def _merge_fwd(o_a, o_b, o_c, gates, b_gate, wa, wb, wc, w_out, name):
    tm = ROW_TILE

    def body(oa_ref, ob_ref, oc_ref, g_ref, bg_ref, wa_ref, wb_ref, wc_ref, wo_ref, mg_ref, mo_ref):
        acc = jnp.zeros((tm, D), F32)
        for i, (o_ref, w_ref) in enumerate(((oa_ref, wa_ref), (ob_ref, wb_ref), (oc_ref, wc_ref))):
            pr = _dot(_cat_slabs(o_ref).astype(BF16), w_ref[...], 1, 0)
            sg = jax.nn.sigmoid(g_ref[:, i * D:(i + 1) * D] + bg_ref[i:i + 1, :])
            acc = acc + sg * pr
        mg = acc.astype(BF16)
        mg_ref[...] = mg
        mo_ref[...] = _dot(mg, wo_ref[...], 1, 0)

    slabs = lambda n: pl.BlockSpec((n, tm, LANES), lambda i: (0, i, 0))
    full = lambda r, c: pl.BlockSpec((r, c), lambda i: (0, 0))
    row = pl.BlockSpec((tm, D), lambda i: (i, 0))
    return pl.pallas_call(
        body, grid=(S // tm,),
        in_specs=[slabs(2), slabs(4), slabs(2), pl.BlockSpec((tm, GATE_COLS), lambda i: (i, 0)), full(3, D),
                  full(256, D), full(512, D), full(256, D), full(D, D)],
        out_specs=[row, row],
        out_shape=[jax.ShapeDtypeStruct((S, D), BF16), jax.ShapeDtypeStruct((S, D), F32)],
        compiler_params=_cp("parallel"), name=name)(o_a, o_b, o_c, gates, b_gate, wa, wb, wc, w_out)


def _merge_bwd(d_mo, o_a, o_b, o_c, gates, b_gate, wa, wb, wc, w_out, name):
    tm = ROW_TILE

    def body(dmo_ref, oa_ref, ob_ref, oc_ref, g_ref, bg_ref, wa_ref, wb_ref, wc_ref, wo_ref,
             doa_ref, dob_ref, doc_ref, dg_ref, dwa_ref, dwb_ref, dwc_ref, dbg_ref):
        @pl.when(pl.program_id(0) == 0)
        def _():
            dwa_ref[...] = jnp.zeros(dwa_ref.shape, F32)
            dwb_ref[...] = jnp.zeros(dwb_ref.shape, F32)
            dwc_ref[...] = jnp.zeros(dwc_ref.shape, F32)
            dbg_ref[...] = jnp.zeros(dbg_ref.shape, F32)

        dmg = _dot(dmo_ref[...], wo_ref[...], 1, 1)
        trip = ((oa_ref, wa_ref, doa_ref, dwa_ref), (ob_ref, wb_ref, dob_ref, dwb_ref), (oc_ref, wc_ref, doc_ref, dwc_ref))
        for i, (o_ref, w_ref, do_ref, dw_ref) in enumerate(trip):
            ob = _cat_slabs(o_ref).astype(BF16)
            pr = _dot(ob, w_ref[...], 1, 0)
            sg = jax.nn.sigmoid(g_ref[:, i * D:(i + 1) * D] + bg_ref[i:i + 1, :])
            dgate = dmg * pr * sg * (1.0 - sg)
            dg_ref[:, i * D:(i + 1) * D] = dgate.astype(BF16)
            dbg_ref[i:i + 1, :] += jnp.sum(dgate, axis=0, keepdims=True)
            dpr = (dmg * sg).astype(BF16)
            do = _dot(dpr, w_ref[...], 1, 1)
            for s in range(do_ref.shape[0]):
                do_ref[s] = do[:, s * LANES:(s + 1) * LANES]
            dw_ref[...] += _dot(ob, dpr, 0, 0)

    slabs = lambda n: pl.BlockSpec((n, tm, LANES), lambda i: (0, i, 0))
    full = lambda r, c: pl.BlockSpec((r, c), lambda i: (0, 0))
    row = pl.BlockSpec((tm, D), lambda i: (i, 0))
    return pl.pallas_call(
        body, grid=(S // tm,),
        in_specs=[row, slabs(2), slabs(4), slabs(2), pl.BlockSpec((tm, GATE_COLS), lambda i: (i, 0)), full(3, D),
                  full(256, D), full(512, D), full(256, D), full(D, D)],
        out_specs=[slabs(2), slabs(4), slabs(2), pl.BlockSpec((tm, GATE_COLS), lambda i: (i, 0)),
                   full(256, D), full(512, D), full(256, D), full(3, D)],
        out_shape=[jax.ShapeDtypeStruct((2, S, LANES), F32), jax.ShapeDtypeStruct((4, S, LANES), F32),
                   jax.ShapeDtypeStruct((2, S, LANES), F32), jax.ShapeDtypeStruct((S, GATE_COLS), BF16),
                   jax.ShapeDtypeStruct((256, D), F32), jax.ShapeDtypeStruct((512, D), F32),
                   jax.ShapeDtypeStruct((256, D), F32), jax.ShapeDtypeStruct((3, D), F32)],
        compiler_params=_cp("arbitrary"), name=name)(d_mo, o_a, o_b, o_c, gates, b_gate, wa, wb, wc, w_out)


FC = 256
GELU_K = math.sqrt(2.0 / math.pi)
GELU_C = 0.044715


RC = 64
NRC = S // RC


def _down(tail, cur, n):
    row = lax.broadcasted_iota(jnp.int32, tail.shape, 0)
    rolled = pltpu.roll(cur, n, 0)
    first = jnp.where(row < n, pltpu.roll(tail, n, 0), rolled[0:8])
    return jnp.concatenate([first, rolled[8:]], axis=0)


def _up(cur, head, n):
    row = lax.broadcasted_iota(jnp.int32, head.shape, 0)
    rolled = pltpu.roll(cur, RC - n, 0)
    last = jnp.where(row >= 8 - n, pltpu.roll(head, 8 - n, 0), rolled[RC - 8:])
    return jnp.concatenate([rolled[:RC - 8], last], axis=0)


def _conv_chunk(load, j, w_ref, b_ref, half):
    r0 = pl.multiple_of(j * RC, RC)
    cur = load(r0, RC).astype(F32)
    tail = load(pl.multiple_of(jnp.maximum(r0 - 16, 0), 16), 16).astype(F32)[8:16]
    tail = jnp.where(j > 0, tail, 0.0)
    d1 = _down(tail, cur, 1)
    d2 = _down(tail, cur, 2)
    y = w_ref[0:1, half, :] * d2 + w_ref[1:2, half, :] * d1 + w_ref[2:3, half, :] * cur + b_ref[half:half + 1, :]
    return y, cur, d1, d2


def _chunk(j):
    return pl.ds(pl.multiple_of(j * RC, RC), RC)


def _fold8(x):
    return jnp.sum(x.reshape(RC // 8, 8, x.shape[-1]), axis=0)


def _ffn_act(u, conv_w, conv_b, name):
    def body(u_ref, w_ref, b_ref, a_ref):
        def step(j, carry):
            yg = _conv_chunk(lambda r, n: u_ref[0, pl.ds(r, n), :], j, w_ref, b_ref, 0)[0]
            yv = _conv_chunk(lambda r, n: u_ref[1, pl.ds(r, n), :], j, w_ref, b_ref, 1)[0]
            th = jnp.tanh(GELU_K * (yg + GELU_C * yg * yg * yg))
            a_ref[_chunk(j), :] = (0.5 * yg * (1.0 + th) * yv).astype(BF16)
            return carry

        lax.fori_loop(0, NRC, step, 0)

    return pl.pallas_call(
        body, grid=(D_FF // FC,),
        in_specs=[pl.BlockSpec((2, S, FC), lambda j: (0, 0, j)), pl.BlockSpec((3, 2, FC), lambda j: (0, 0, j)),
                  pl.BlockSpec((2, FC), lambda j: (0, j))],
        out_specs=pl.BlockSpec((S, FC), lambda j: (0, j)),
        out_shape=jax.ShapeDtypeStruct((S, D_FF), BF16),
        compiler_params=_cp("parallel"), name=name)(u, conv_w, conv_b)


def _ffn_act_bwd(u, d_a, conv_w, conv_b, name):
    def body(u_ref, da_ref, w_ref, b_ref, du_ref, dw_ref, db_ref, dy_s):
        def first(j, acc):
            yg, ug, ug1, ug2 = _conv_chunk(lambda r, n: u_ref[0, pl.ds(r, n), :], j, w_ref, b_ref, 0)
            yv, uv, uv1, uv2 = _conv_chunk(lambda r, n: u_ref[1, pl.ds(r, n), :], j, w_ref, b_ref, 1)
            th = jnp.tanh(GELU_K * (yg + GELU_C * yg * yg * yg))
            gelu = 0.5 * yg * (1.0 + th)
            dgelu = 0.5 * (1.0 + th) + 0.5 * yg * (1.0 - th * th) * GELU_K * (1.0 + 3.0 * GELU_C * yg * yg)
            da = da_ref[_chunk(j), :].astype(F32)
            dyg = da * yv * dgelu
            dyv = da * gelu
            dy_s[0, _chunk(j), :] = dyg
            dy_s[1, _chunk(j), :] = dyv
            new = (_fold8(dyg * ug2), _fold8(dyg * ug1), _fold8(dyg * ug), _fold8(dyg),
                   _fold8(dyv * uv2), _fold8(dyv * uv1), _fold8(dyv * uv), _fold8(dyv))
            return tuple(a + n for a, n in zip(acc, new))

        acc = lax.fori_loop(0, NRC, first, tuple(jnp.zeros((8, FC), F32) for _ in range(8)))
        for half in range(2):
            for k in range(3):
                dw_ref[k:k + 1, half, :] = jnp.sum(acc[4 * half + k], axis=0, keepdims=True)
            db_ref[half:half + 1, :] = jnp.sum(acc[4 * half + 3], axis=0, keepdims=True)

        def second(j, carry):
            for half in range(2):
                cur = dy_s[half, _chunk(j), :]
                h0 = pl.multiple_of(jnp.minimum((j + 1) * RC, S - 8), 8)
                head = jnp.where(j < NRC - 1, dy_s[half, pl.ds(h0, 8), :], 0.0)
                du = (w_ref[2:3, half, :] * cur + w_ref[1:2, half, :] * _up(cur, head, 1)
                      + w_ref[0:1, half, :] * _up(cur, head, 2))
                du_ref[half, _chunk(j), :] = du.astype(BF16)
            return carry

        lax.fori_loop(0, NRC, second, 0)

    return pl.pallas_call(
        body, grid=(D_FF // FC,),
        in_specs=[pl.BlockSpec((2, S, FC), lambda j: (0, 0, j)), pl.BlockSpec((S, FC), lambda j: (0, j)),
                  pl.BlockSpec((3, 2, FC), lambda j: (0, 0, j)), pl.BlockSpec((2, FC), lambda j: (0, j))],
        out_specs=[pl.BlockSpec((2, S, FC), lambda j: (0, 0, j)), pl.BlockSpec((3, 2, FC), lambda j: (0, 0, j)),
                   pl.BlockSpec((2, FC), lambda j: (0, j))],
        out_shape=[jax.ShapeDtypeStruct((2, S, D_FF), BF16), jax.ShapeDtypeStruct((3, 2, D_FF), F32),
                   jax.ShapeDtypeStruct((2, D_FF), F32)],
        scratch_shapes=[pltpu.VMEM((2, S, FC), F32)],
        compiler_params=_cp("parallel"), name=name)(u, d_a, conv_w, conv_b)


def _layer_fwd(x, h1, w, bias, lname):
    n = lambda s: f"{lname}_{s}"
    w.need("in", h1)
    tn = 768
    proj = _mm(h1, w["w_in"], grid=(S // 1024, QKV_COLS // tn, 1),
               a_spec=pl.BlockSpec((1024, D), lambda i, j, k: (i, 0)),
               b_spec=pl.BlockSpec((tn, D), lambda i, j, k: (j, 0)),
               out_shape=jax.ShapeDtypeStruct((QKV_SLABS, S, LANES), F32),
               out_spec=pl.BlockSpec((tn // LANES, 1024, LANES), lambda i, j, k: (j, i, 0)),
               ca=1, cb=1, acc_shape=(1024, tn), out_slab=True, name=n("proj_qkv"))
    gates = _mm(h1, w["w_in"], grid=(S // 1024, GATE_COLS // tn, 1),
                a_spec=pl.BlockSpec((1024, D), lambda i, j, k: (i, 0)),
                b_spec=pl.BlockSpec((tn, D), lambda i, j, k: (j + QKV_COLS // tn, 0)),
                out_shape=jax.ShapeDtypeStruct((S, GATE_COLS), F32),
                out_spec=pl.BlockSpec((1024, tn), lambda i, j, k: (i, j)),
                ca=1, cb=1, acc_shape=(1024, tn), name=n("proj_gate"))
    nums, stats = [], []
    for g, (_, d) in enumerate(A_GROUPS):
        nm, st = _band_fwd(proj, bias, w["sinks"], d=d, q0=2 * g, k0=6 + 2 * g, v0=12 + 2 * g, npairs=2, bias0=2 * g,
                           shared_kv=False, name=n(f"attn_a{g}_fwd"))
        nums.append(nm)
        stats.append(st)
    o_a, lse_a = _combine_a(nums, stats, n("attn_a_combine"))
    o_b, lse_b = _band_fwd(proj, bias, w["sinks"], d=1, q0=18, k0=22, v0=23, npairs=4, bias0=6, shared_kv=True,
                           name=n("attn_b_fwd"))
    o_c, tot_c = _stick_fwd(proj, q0=24, k0=26, v0=28, name=n("attn_c_fwd"))
    w.need("mix", tot_c)
    merged, mo = _merge_fwd(o_a, o_b, o_c, gates, w["b_gate"], w["w_br_a"], w["w_br_b"], w["w_br_c"], w["w_out"], n("merge_fwd"))
    x2, h2 = _postnorm_res(x, mo, w["attn_post_norm"], w["ffn_pre_norm"], n("attn_post"))
    w.need("ffn", h2)
    u = _mm(h2, w["w_up"], grid=(S // 1024, 2 * D_FF // 1024, 1),
            a_spec=pl.BlockSpec((1024, D), lambda i, j, k: (i, 0)),
            b_spec=pl.BlockSpec((D, 1024), lambda i, j, k: (0, j)),
            out_shape=jax.ShapeDtypeStruct((2, S, D_FF), BF16),
            out_spec=pl.BlockSpec((None, 1024, 1024), lambda i, j, k: (j // 4, i, j % 4)),
            ca=1, cb=0, acc_shape=(1024, 1024), name=n("ffn_up"))
    a = _ffn_act(u, w["conv_w"], w["conv_b"], n("ffn_act"))
    fo = _mm_nn(a, w["w_down"], F32, 1024, 1024, 2048, n("ffn_down"))
    saved = dict(x=x, h1=h1, proj=proj, gates=gates, o_a=o_a, lse_a=lse_a, o_b=o_b, lse_b=lse_b, o_c=o_c, tot_c=tot_c,
                 merged=merged, mo=mo, x2=x2, h2=h2, u=u, a=a, fo=fo)
    return saved


def _layer_bwd(dx3, sv, w, bias, lname, tok=None, on_part=None):
    n = lambda s: f"{lname}_{s}"
    g = {}

    def part(group, vec):
        t = on_part(group, g) if on_part is not None else None
        return vec if t is None else vec + t

    gain = w["ffn_post_norm"] if tok is None else w["ffn_post_norm"] + tok
    d_fo, g["ffn_post_norm"] = _norm_bwd(sv["fo"], gain, [dx3], None, BF16, n("ffn_post_bwd"))
    d_a = _mm_nt(d_fo, w["w_down"], BF16, 1024, 1024, 1024, n("ffn_down_bwd_x"))
    g["w_down"] = _mm_tn(sv["a"], d_fo, BF16, 1024, 1024, S, n("ffn_down_bwd_w"))
    d_u, dcw, dcb = _ffn_act_bwd(sv["u"], d_a, w["conv_w"], w["conv_b"], n("ffn_act_bwd"))
    g["conv_w"] = dcw.reshape(3, 2 * D_FF)
    g["conv_b"] = dcb.reshape(1, 2 * D_FF)
    g["w_up"] = _mm(sv["h2"], d_u, grid=(1, 2 * D_FF // 1024, 1),
                    a_spec=pl.BlockSpec((S, D), lambda i, j, k: (k, 0)),
                    b_spec=pl.BlockSpec((None, S, 1024), lambda i, j, k: (j // 4, k, j % 4)),
                    out_shape=jax.ShapeDtypeStruct((D, 2 * D_FF), BF16),
                    out_spec=pl.BlockSpec((D, 1024), lambda i, j, k: (0, j)),
                    ca=0, cb=0, acc_shape=(D, 1024), name=n("ffn_up_bwd_w"))
    tok_ffn = on_part("ffn", g) if on_part is not None else None
    d_h2 = _mm(d_u, w["w_up"], grid=(S // 1024, 1, 2 * D_FF // 2048),
               a_spec=pl.BlockSpec((None, 1024, 2048), lambda i, j, k: (k // 2, i, k % 2)),
               b_spec=pl.BlockSpec((D, 2048), lambda i, j, k: (0, k)),
               out_shape=jax.ShapeDtypeStruct((S, D), F32),
               out_spec=pl.BlockSpec((1024, D), lambda i, j, k: (i, 0)),
               ca=1, cb=1, acc_shape=(1024, D), after=tok_ffn, name=n("ffn_up_bwd_x"))
    dx2, g["ffn_pre_norm"] = _norm_bwd(sv["x2"], w["ffn_pre_norm"], [d_h2], dx3, F32, n("ffn_pre_bwd"))
    d_mo, g["attn_post_norm"] = _norm_bwd(sv["mo"], w["attn_post_norm"], [dx2], None, BF16, n("attn_post_bwd"))
    g["w_out"] = _mm_tn(sv["merged"], d_mo, BF16, 1024, 1024, S, n("out_bwd_w"))
    do_a, do_b, do_c, d_gates, dwa, dwb, dwc, g["b_gate"] = _merge_bwd(
        d_mo, sv["o_a"], sv["o_b"], sv["o_c"], sv["gates"], w["b_gate"], w["w_br_a"], w["w_br_b"], w["w_br_c"],
        w["w_out"], n("merge_bwd"))
    g["w_br_a"], g["w_br_b"], g["w_br_c"] = dwa, dwb, dwc
    sinks = part("mix", w["sinks"])
    proj = sv["proj"]
    dqa, dka, dva, gbias = [], [], [], []
    for gi, (_, d) in enumerate(A_GROUPS):
        dq, dk, dv, gg, _ = _band_bwd(proj, bias, sv["o_a"], do_a, sv["lse_a"], sinks, d=d, q0=2 * gi, k0=6 + 2 * gi,
                                      v0=12 + 2 * gi, npairs=2, bias0=2 * gi, shared_kv=False, name=n(f"attn_a{gi}_bwd"))
        dqa.append(dq), dka.append(dk), dva.append(dv), gbias.append(gg)
    dqb, dkb, dvb, ggb, dsink = _band_bwd(proj, bias, sv["o_b"], do_b, sv["lse_b"], sinks, d=1, q0=18, k0=22, v0=23,
                                          npairs=4, bias0=6, shared_kv=True, name=n("attn_b_bwd"))
    gbias.append(ggb)
    g["bias_g"] = jnp.concatenate(gbias, axis=0).reshape(N_BIAS_HEADS, BLK, 2 * BLK)
    g["sinks"] = dsink[:, 0, :2].reshape(1, 8)
    dqc, dkc, dvc = _stick_bwd(proj, do_c, sv["tot_c"], q0=24, k0=26, v0=28, name=n("attn_c_bwd"))
    dqkv = jnp.concatenate(dqa + dka + dva + [dqb, dkb, dvb, dqc, dkc, dvc], axis=0)
    ts = 6
    tsx = 15
    dw_in = _mm(dqkv, sv["h1"], grid=(QKV_SLABS // ts, 1, 1),
                a_spec=pl.BlockSpec((ts, S, LANES), lambda i, j, k: (i, k, 0)),
                b_spec=pl.BlockSpec((S, D), lambda i, j, k: (k, 0)),
                out_shape=jax.ShapeDtypeStruct((IN_COLS, D), BF16),
                out_spec=pl.BlockSpec((ts * LANES, D), lambda i, j, k: (i, 0)),
                ca=0, cb=0, acc_shape=(ts * LANES, D), a_slab=True, name=n("in_bwd_w_qkv"))
    g["w_in"] = _mm(d_gates, sv["h1"], grid=(GATE_COLS // 768, 1, 1),
                    a_spec=pl.BlockSpec((S, 768), lambda i, j, k: (k, i)),
                    b_spec=pl.BlockSpec((S, D), lambda i, j, k: (k, 0)),
                    out_shape=jax.ShapeDtypeStruct((IN_COLS, D), BF16),
                    out_spec=pl.BlockSpec((768, D), lambda i, j, k: (i + QKV_COLS // 768, 0)),
                    ca=0, cb=0, acc_shape=(768, D), alias_out=dw_in, name=n("in_bwd_w_gate"))
    tok_in = on_part("in", g) if on_part is not None else None
    d_h1a = _mm(dqkv, w["w_in"], grid=(S // 1024, 1, QKV_SLABS // tsx),
                a_spec=pl.BlockSpec((tsx, 1024, LANES), lambda i, j, k: (k, i, 0)),
                b_spec=pl.BlockSpec((tsx * LANES, D), lambda i, j, k: (k, 0)),
                out_shape=jax.ShapeDtypeStruct((S, D), F32),
                out_spec=pl.BlockSpec((1024, D), lambda i, j, k: (i, 0)),
                ca=1, cb=0, acc_shape=(1024, D), a_slab=True, after=tok_in, name=n("in_bwd_x_qkv"))
    d_h1b = _mm(d_gates, w["w_in"], grid=(S // 1024, 1, GATE_COLS // 768),
                a_spec=pl.BlockSpec((1024, 768), lambda i, j, k: (i, k)),
                b_spec=pl.BlockSpec((768, D), lambda i, j, k: (k + QKV_COLS // 768, 0)),
                out_shape=jax.ShapeDtypeStruct((S, D), F32),
                out_spec=pl.BlockSpec((1024, D), lambda i, j, k: (i, 0)),
                ca=1, cb=0, acc_shape=(1024, D), after=tok_in, name=n("in_bwd_x_gate"))
    dx, g["attn_pre_norm"] = _norm_bwd(sv["x"], w["attn_pre_norm"], [d_h1a, d_h1b], dx2, F32, n("attn_pre_bwd"))
    return dx, g, tok_in


def _local_step(x, target, ws, rel_bias, tok=None, on_grads=None):
    buckets = jnp.asarray(_bucket_tiles())
    bias = _bias_tiles(rel_bias, buckets, "bias_tiles").reshape(N_BIAS_HEADS // 2, 2, 2, BLK, 2 * BLK)
    saved = []
    gain0 = ws[0]["attn_pre_norm"] if tok is None else ws[0]["attn_pre_norm"] + tok
    h1 = _prenorm(x, gain0, "l0_attn_pre")
    for l in range(DEPTH):
        sv = _layer_fwd(x, h1, ws[l], bias, f"l{l}")
        saved.append(sv)
        g_next = ws[l + 1]["attn_pre_norm"] if l + 1 < DEPTH else ws[l]["attn_pre_norm"]
        x, h1 = _postnorm_res(sv["x2"], sv["fo"], ws[l]["ffn_post_norm"], g_next, f"l{l}_ffn_post")
    dy, loss_tile = _loss_head(x, target, "loss_head")
    grads = [None] * DEPTH
    tok = None
    for l in reversed(range(DEPTH)):
        on_part = None if on_grads is None else functools.partial(on_grads, l)
        dy, grads[l], tok = _layer_bwd(dy, saved[l], ws[l], bias, f"l{l}", tok, on_part)
    g_rel = _bias_grad([grads[l]["bias_g"] for l in range(DEPTH)], buckets, "bias_grad")[:, :N_BIAS_HEADS]
    return loss_tile[0, 0], dy, grads, g_rel


def _coords():
    return lax.axis_index("x"), lax.axis_index("y"), lax.axis_index("c")


def _peer(rel):
    x, y, c = _coords()
    return (1 - x if rel & 4 else x, 1 - y if rel & 2 else y, 1 - c if rel & 1 else c)


def _exchange(srcs, dst_shapes, src_win, dst_win, name, after=None):
    nt = len(srcs)
    extra = [] if after is None else [after]

    def body(*refs):
        src_refs, dst_refs = refs[:nt], refs[nt + len(extra):2 * nt + len(extra)]
        send_sems, recv_sems, local_sems = refs[2 * nt + len(extra):]
        x, y, c = _coords()
        me = 4 * x + 2 * y + c
        locals_ = []
        for t in range(nt):
            cp = pltpu.make_async_copy(src_win(t, src_refs[t], me), dst_win(t, dst_refs[t], me), local_sems.at[t])
            cp.start()
            locals_.append(cp)
        sends = []
        for rel in range(1, NDEV):
            px, py, pc = _peer(rel)
            q = 4 * px + 2 * py + pc
            for t in range(nt):
                cp = pltpu.make_async_remote_copy(
                    src_ref=src_win(t, src_refs[t], q), dst_ref=dst_win(t, dst_refs[t], me),
                    send_sem=send_sems.at[rel - 1, t], recv_sem=recv_sems.at[rel - 1, t],
                    device_id=(px, py, pc), device_id_type=MESH)
                cp.start()
                sends.append(cp)
        for rel in range(1, NDEV):
            px, py, pc = _peer(rel)
            q = 4 * px + 2 * py + pc
            for t in range(nt):
                pltpu.make_async_remote_copy(
                    src_ref=src_win(t, src_refs[t], me), dst_ref=dst_win(t, dst_refs[t], q),
                    send_sem=send_sems.at[rel - 1, t], recv_sem=recv_sems.at[rel - 1, t],
                    device_id=(px, py, pc), device_id_type=MESH).wait_recv()
        for cp in sends:
            cp.wait_send()
        for cp in locals_:
            cp.wait()

    return pl.pallas_call(
        body, in_specs=[ANY] * (nt + len(extra)), out_specs=[ANY] * nt, out_shape=dst_shapes,
        scratch_shapes=[pltpu.SemaphoreType.DMA((NDEV - 1, nt)), pltpu.SemaphoreType.DMA((NDEV - 1, nt)),
                        pltpu.SemaphoreType.DMA((nt,))],
        name=name)(*srcs, *extra)


BIG = (("w_in", 0, 864), ("w_br_a", 1, 128), ("w_br_b", 1, 128), ("w_br_c", 1, 128), ("w_out", 0, 128),
       ("w_up", 1, 1024), ("w_down", 0, 512))


NBIG = len(BIG)
BIG_FULL = {"w_in": (IN_COLS, D), "w_br_a": (256, D), "w_br_b": (512, D), "w_br_c": (256, D), "w_out": (D, D),
            "w_up": (D, 2 * D_FF), "w_down": (D_FF, D)}
SHARD_ROWS = {"w_in": 288, "w_up": 256, "w_down": 256}
LAYER_GROUPS = (("in", (0,)), ("mix", (1, 2, 3, 4)), ("ffn", (5, 6)))

HBM_SPEC = pl.BlockSpec(memory_space=pltpu.HBM)
SEM_SPEC = pl.BlockSpec(memory_space=pltpu.SEMAPHORE)


def _hbm(a):
    return pltpu.with_memory_space_constraint(a, pltpu.HBM)


def _shard_window(t, ref, k):
    nm, ax, ext = BIG[t % NBIG]
    off = pl.multiple_of(k * ext, ext)
    if ax == 0:
        return ref.at[pl.ds(off, ext), :]
    return ref.at[:, pl.ds(off, ext)]


def _whole(t, ref, k):
    return ref


def _slot(t, ref, k):
    return ref.at[k]


def _own_block_spec(t, rows, me_of):
    nm, ax, ext = BIG[t % NBIG]
    r, c = BIG_FULL[nm]
    if ax == 0:
        return pl.BlockSpec((rows, c), lambda i, m: (me_of(m) * (ext // rows) + i, 0))
    return pl.BlockSpec((rows, ext), lambda i, m: (i, me_of(m)))


def _cast_own(t, shard, me_arr, name):
    nm, ax, ext = BIG[t % NBIG]
    nr, nc = shard.shape
    rows = SHARD_ROWS.get(nm, nr)
    shape = BIG_FULL[nm]

    def body(m_ref, s_ref, o_ref):
        o_ref[...] = s_ref[...].astype(BF16)

    return pl.pallas_call(
        body, grid_spec=pltpu.PrefetchScalarGridSpec(
            num_scalar_prefetch=1, grid=(nr // rows,),
            in_specs=[pl.BlockSpec((rows, nc), lambda i, m: (i, 0))],
            out_specs=_own_block_spec(t, rows, lambda m: m[0])),
        out_shape=jax.ShapeDtypeStruct(shape, BF16), compiler_params=_cp("arbitrary"), name=name)(me_arr, shard)


ALL_RELS = tuple(range(1, NDEV))
NEAR_RELS = (1, 2, 4, 6)
FAR_RELS = (2, 4, 6)


def _xchg_start(srcs, lands, groups, src_win, dst_win, after, name, rels=ALL_RELS, tids=None):
    ns = 0 if srcs is None else len(srcs)
    nt, ng = len(lands), len(groups)
    ins = ([] if srcs is None else list(srcs)) + list(lands)

    def body(*refs):
        src_refs, land_refs = refs[:ns], refs[ns:ns + nt]
        sems = refs[ns + nt + 1:ns + nt + 1 + 2 * ng]
        token = refs[-1]
        x, y, c = _coords()
        me = 4 * x + 2 * y + c
        for gi, grp in enumerate(groups):
            for j, t in enumerate(grp):
                tid = t if tids is None else tids[t]
                for ri, rel in enumerate(rels):
                    px, py, pc = _peer(rel)
                    q = 4 * px + 2 * py + pc
                    src = dst_win(tid, land_refs[t], me) if srcs is None else src_win(tid, src_refs[t], q)
                    pltpu.make_async_remote_copy(
                        src_ref=src, dst_ref=dst_win(tid, land_refs[t], me),
                        send_sem=sems[2 * gi].at[ri * len(grp) + j],
                        recv_sem=sems[2 * gi + 1].at[ri * len(grp) + j],
                        device_id=(px, py, pc), device_id_type=MESH).start()
        token[...] = jnp.zeros((8, LANES), F32)

    out_shape = []
    for grp in groups:
        out_shape += [pltpu.SemaphoreType.DMA((len(rels) * len(grp),))] * 2
    out_shape += [pltpu.HBM(a.shape, a.dtype) for a in ins]
    out_shape.append(jax.ShapeDtypeStruct((8, LANES), F32))
    outs = pl.pallas_call(
        body, in_specs=[HBM_SPEC] * len(ins) + [ANY],
        out_specs=[SEM_SPEC] * (2 * ng) + [HBM_SPEC] * len(ins) + [pl.BlockSpec(memory_space=pltpu.VMEM)],
        out_shape=out_shape, input_output_aliases={i: 2 * ng + i for i in range(len(ins))},
        compiler_params=pltpu.CompilerParams(has_side_effects=pltpu.SideEffectType.DATAFLOW_SIDE_EFFECTING),
        name=name)(*[_hbm(a) for a in ins], after)
    sems = [(outs[2 * gi], outs[2 * gi + 1]) for gi in range(ng)]
    thru = list(outs[2 * ng:2 * ng + len(ins)])
    return sems, (None if srcs is None else thru[:ns]), thru[ns:], outs[-1]


def _xchg_wait(sems, srcs, lands, tids, after, src_win, dst_win, name, rels=ALL_RELS):
    ns = 0 if srcs is None else len(srcs)
    n = len(lands)
    send_sem, recv_sem = sems
    ins = ([] if srcs is None else list(srcs)) + list(lands)

    def body(*refs):
        src_refs, land_refs = refs[:ns], refs[ns:ns + n]
        ssem, rsem = refs[ns + n], refs[ns + n + 1]
        x, y, c = _coords()
        me = 4 * x + 2 * y + c
        for j, t in enumerate(tids):
            for ri, rel in enumerate(rels):
                px, py, pc = _peer(rel)
                q = 4 * px + 2 * py + pc
                src = dst_win(t, land_refs[j], me) if srcs is None else src_win(t, src_refs[j], q)
                cp = pltpu.make_async_remote_copy(
                    src_ref=src, dst_ref=dst_win(t, land_refs[j], q),
                    send_sem=ssem.at[ri * n + j], recv_sem=rsem.at[ri * n + j],
                    device_id=(px, py, pc), device_id_type=MESH)
                cp.wait_send()
                cp.wait_recv()

    outs = pl.pallas_call(
        body, in_specs=[HBM_SPEC] * len(ins) + [SEM_SPEC, SEM_SPEC, ANY], out_specs=[HBM_SPEC] * len(ins),
        out_shape=[pltpu.HBM(a.shape, a.dtype) for a in ins],
        input_output_aliases={i: i for i in range(len(ins))},
        compiler_params=pltpu.CompilerParams(has_side_effects=pltpu.SideEffectType.DATAFLOW_SIDE_EFFECTING),
        name=name)(*ins, send_sem, recv_sem, after)
    return (None if srcs is None else list(outs[:ns])), list(outs[ns:])


def _gather_forward(sems_in, lands, groups, tids, after, dst_win, name):
    nt, ng = len(lands), len(groups)

    def body(*refs):
        land_refs = refs[:nt]
        in_sems = refs[nt:nt + 2 * ng]
        out_sems = refs[nt + 2 * ng + 1:nt + 4 * ng + 1]
        token = refs[-1]
        x, y, c = _coords()
        me = 4 * x + 2 * y + c
        sib = (x, y, 1 - c)
        for gi, grp in enumerate(groups):
            n = len(grp)
            for j, pos in enumerate(grp):
                t = tids[pos]
                for ri, rel in enumerate(NEAR_RELS):
                    px, py, pc = _peer(rel)
                    q = 4 * px + 2 * py + pc
                    cp = pltpu.make_async_remote_copy(
                        src_ref=dst_win(t, land_refs[pos], me), dst_ref=dst_win(t, land_refs[pos], q),
                        send_sem=in_sems[2 * gi].at[ri * n + j], recv_sem=in_sems[2 * gi + 1].at[ri * n + j],
                        device_id=(px, py, pc), device_id_type=MESH)
                    cp.wait_send()
                    cp.wait_recv()
            for j, pos in enumerate(grp):
                t = tids[pos]
                for fi, rel in enumerate(FAR_RELS):
                    px, py, pc = _peer(rel)
                    q = 4 * px + 2 * py + pc
                    win = dst_win(t, land_refs[pos], q)
                    pltpu.make_async_remote_copy(
                        src_ref=win, dst_ref=win,
                        send_sem=out_sems[2 * gi].at[fi * n + j], recv_sem=out_sems[2 * gi + 1].at[fi * n + j],
                        device_id=sib, device_id_type=MESH).start()
        token[...] = jnp.zeros((8, LANES), F32)

    out_shape = []
    for grp in groups:
        out_shape += [pltpu.SemaphoreType.DMA((len(FAR_RELS) * len(grp),))] * 2
    out_shape += [pltpu.HBM(a.shape, a.dtype) for a in lands]
    out_shape.append(jax.ShapeDtypeStruct((8, LANES), F32))
    flat_sems = [s for pair in sems_in for s in pair]
    outs = pl.pallas_call(
        body, in_specs=[HBM_SPEC] * nt + [SEM_SPEC] * (2 * ng) + [ANY],
        out_specs=[SEM_SPEC] * (2 * ng) + [HBM_SPEC] * nt + [pl.BlockSpec(memory_space=pltpu.VMEM)],
        out_shape=out_shape, input_output_aliases={i: 2 * ng + i for i in range(nt)},
        compiler_params=pltpu.CompilerParams(has_side_effects=pltpu.SideEffectType.DATAFLOW_SIDE_EFFECTING),
        name=name)(*[_hbm(a) for a in lands], *flat_sems, after)
    sems = [(outs[2 * gi], outs[2 * gi + 1]) for gi in range(ng)]
    return sems, list(outs[2 * ng:2 * ng + nt]), outs[-1]


class _Weights:
    def __init__(self, ready, pending=None):
        self.ready = dict(ready)
        self.pending = dict(pending or {})

    def __getitem__(self, k):
        return self.ready[k]

    def need(self, group, after):
        fn = self.pending.pop(group, None)
        if fn is not None:
            self.ready.update(fn(after))


def _adamw_math(w, g, m, v):
    m2 = ADAM_B1 * m + (1.0 - ADAM_B1) * g
    v2 = ADAM_B2 * v + (1.0 - ADAM_B2) * (g * g)
    m_hat = m2 / (1.0 - ADAM_B1 ** ADAM_STEP)
    v_hat = v2 / (1.0 - ADAM_B2 ** ADAM_STEP)
    delta = -ADAM_LR * (m_hat / (jnp.sqrt(v_hat) + ADAM_EPS) + ADAM_WD * w)
    return delta, m2, v2


def _adamw(t, parts, own, me_arr, w, m, v, layer, prev, rows, name):
    nl, nr, nc = w.shape

    def body(me_ref, p_ref, own_ref, w_ref, m_ref, v_ref, *rest):
        g_ref, d_ref, m2_ref, v2_ref = rest[-4:]
        me = me_ref[0]
        g = None
        for k in range(NDEV):
            term = jnp.where(me == k, own_ref[...], p_ref[k]).astype(F32)
            g = term if g is None else g + term
        delta, m2, v2 = _adamw_math(w_ref[...], g, m_ref[...], v_ref[...])
        g_ref[...] = g
        d_ref[...] = delta
        m2_ref[...] = m2
        v2_ref[...] = v2

    blk = pl.BlockSpec((None, rows, nc), lambda i, mm: (layer, i, 0))
    pblk = pl.BlockSpec((NDEV, rows, nc), lambda i, mm: (0, i, 0))
    extra = [] if prev is None else list(prev)
    return pl.pallas_call(
        body, grid_spec=pltpu.PrefetchScalarGridSpec(
            num_scalar_prefetch=1, grid=(nr // rows,),
            in_specs=[pblk, _own_block_spec(t, rows, lambda mm: mm[0]), blk, blk, blk] + [ANY] * len(extra),
            out_specs=[blk] * 4),
        out_shape=[jax.ShapeDtypeStruct(w.shape, F32)] * 4,
        input_output_aliases={6 + k: k for k in range(len(extra))},
        compiler_params=_cp("arbitrary"), name=name)(me_arr, parts, own, w, m, v, *extra)


SMALL_REPL = (("rel_bias", NUM_BUCKETS * N_BIAS_HEADS), ("attn_pre_norm", DEPTH * D), ("sinks", DEPTH * 8),
              ("attn_post_norm", DEPTH * D), ("ffn_pre_norm", DEPTH * D), ("conv_b", DEPTH * 2 * D_FF),
              ("ffn_post_norm", DEPTH * D))
SMALL_SHARD = (("b_gate", (DEPTH, 3, D), 128), ("conv_w", (DEPTH, 3, 2 * D_FF), 1024))


def _pack(vecs):
    flat = jnp.concatenate([v.reshape(-1).astype(F32) for v in vecs])
    n = flat.shape[0]
    rows = -(-n // (8 * LANES)) * 8
    return jnp.pad(flat, (0, rows * LANES - n)).reshape(rows, LANES)


def _unpack(packed, sizes):
    flat = packed.reshape(-1)
    out, off = [], 0
    for sz in sizes:
        out.append(flat[off:off + sz])
        off += sz
    return out


ROWPACK = (("rel_bias", 32, 32, (NUM_BUCKETS, N_BIAS_HEADS)), ("sinks", 8, 8, (DEPTH, 8)),
           ("attn_pre_norm", 16, 16, (DEPTH, D)), ("attn_post_norm", 16, 16, (DEPTH, D)),
           ("ffn_pre_norm", 16, 16, (DEPTH, D)), ("ffn_post_norm", 16, 16, (DEPTH, D)),
           ("conv_b", 128, 128, (DEPTH, 2 * D_FF)), ("b_gate", 48, 8, (DEPTH, 3, 128)),
           ("conv_w", 384, 48, (DEPTH, 3, 1024)))
ROWS_FULL = sum(r for _, r, _, _ in ROWPACK)
ROWS_OWN = sum(r for _, _, r, _ in ROWPACK)


def _as_rows(a, rows):
    a = a.astype(F32)
    if a.shape[-1] < LANES:
        a = jnp.pad(a.reshape(-1, a.shape[-1]), ((0, 0), (0, LANES - a.shape[-1])))
    a = a.reshape(-1, LANES)
    return jnp.pad(a, ((0, rows - a.shape[0]), (0, 0)))


def _rowpack(arrs, own):
    return jnp.concatenate([_as_rows(arrs[nm], ro if own else rf) for nm, rf, ro, _ in ROWPACK], axis=0)


def _small_update(parts, w, m, v, me_arr, name):
    nsm = len(ROWPACK)

    def body(me_ref, p_ref, w_ref, m_ref, v_ref, *rest):
        outs = rest[:4 * nsm]
        gfull, g_s, d_s, m_s, v_s = rest[4 * nsm:]
        me = me_ref[0]
        g = p_ref[0]
        for k in range(1, NDEV):
            g = g + p_ref[k]
        gfull[...] = g
        of, oo = 0, 0
        for nm, rf, ro, _ in ROWPACK:
            if nm == "b_gate":
                g_s[oo:oo + ro, :] = jnp.zeros((ro, LANES), F32)
                for r in range(DEPTH * 3):
                    g_s[oo + r:oo + r + 1, :] = gfull[pl.ds(of + r * NDEV + me, 1), :]
            elif nm == "conv_w":
                for r in range(DEPTH * 3):
                    g_s[oo + r * 8:oo + r * 8 + 8, :] = gfull[pl.ds(pl.multiple_of(of + r * 64 + me * 8, 8), 8), :]
            else:
                g_s[oo:oo + ro, :] = gfull[of:of + rf, :]
            of, oo = of + rf, oo + ro
        delta, m2, v2 = _adamw_math(w_ref[...], g_s[...], m_ref[...], v_ref[...])
        d_s[...] = delta
        m_s[...] = m2
        v_s[...] = v2
        for kind, src in enumerate((g_s, d_s, m_s, v_s)):
            oo = 0
            for idx, (nm, rf, ro, shp) in enumerate(ROWPACK):
                o_ref = outs[kind * nsm + idx]
                if nm in ("rel_bias", "sinks"):
                    o_ref[...] = src[oo:oo + shp[0], 0:shp[1]]
                elif nm == "b_gate":
                    for l in range(DEPTH):
                        o_ref[l] = src[oo + 3 * l:oo + 3 * l + 3, :]
                elif nm == "conv_w":
                    for l in range(DEPTH):
                        for k in range(8):
                            o_ref[l, :, k * LANES:(k + 1) * LANES] = src[pl.ds(oo + 24 * l + k, 3, stride=8), :]
                else:
                    per = shp[1] // LANES
                    for k in range(per):
                        o_ref[:, k * LANES:(k + 1) * LANES] = src[pl.ds(oo + k, DEPTH, stride=per), :]
                oo += ro

    vm = pl.BlockSpec(memory_space=pltpu.VMEM)
    shapes = [jax.ShapeDtypeStruct(shp, F32) for _ in range(4) for _, _, _, shp in ROWPACK]
    outs = pl.pallas_call(
        body, in_specs=[SMEM, vm, vm, vm, vm], out_specs=[vm] * (4 * nsm), out_shape=shapes,
        scratch_shapes=[pltpu.VMEM((ROWS_FULL, LANES), F32)] + [pltpu.VMEM((ROWS_OWN, LANES), F32)] * 4,
        name=name)(me_arr, parts, w, m, v)
    names = [nm for nm, _, _, _ in ROWPACK]
    return [dict(zip(names, outs[kind * nsm:(kind + 1) * nsm])) for kind in range(4)]


def kernel(x, rel_bias, attn_pre_norm, w_in, b_gate, sinks, w_br_a, w_br_b, w_br_c, w_out, attn_post_norm, ffn_pre_norm, w_up, conv_w, conv_b, w_down, ffn_post_norm, loss_target, m_rel_bias, m_attn_pre_norm, m_w_in, m_b_gate, m_sinks, m_w_br_a, m_w_br_b, m_w_br_c, m_w_out, m_attn_post_norm, m_ffn_pre_norm, m_w_up, m_conv_w, m_conv_b, m_w_down, m_ffn_post_norm, v_rel_bias, v_attn_pre_norm, v_w_in, v_b_gate, v_sinks, v_w_br_a, v_w_br_b, v_w_br_c, v_w_out, v_attn_post_norm, v_ffn_pre_norm, v_w_up, v_conv_w, v_conv_b, v_w_down, v_ffn_post_norm):
    P = dict(rel_bias=rel_bias, attn_pre_norm=attn_pre_norm, w_in=w_in, b_gate=b_gate, sinks=sinks, w_br_a=w_br_a,
             w_br_b=w_br_b, w_br_c=w_br_c, w_out=w_out, attn_post_norm=attn_post_norm, ffn_pre_norm=ffn_pre_norm,
             w_up=w_up, conv_w=conv_w, conv_b=conv_b, w_down=w_down, ffn_post_norm=ffn_post_norm)
    M = dict(rel_bias=m_rel_bias, attn_pre_norm=m_attn_pre_norm, w_in=m_w_in, b_gate=m_b_gate, sinks=m_sinks,
             w_br_a=m_w_br_a, w_br_b=m_w_br_b, w_br_c=m_w_br_c, w_out=m_w_out, attn_post_norm=m_attn_post_norm,
             ffn_pre_norm=m_ffn_pre_norm, w_up=m_w_up, conv_w=m_conv_w, conv_b=m_conv_b, w_down=m_w_down,
             ffn_post_norm=m_ffn_post_norm)
    V = dict(rel_bias=v_rel_bias, attn_pre_norm=v_attn_pre_norm, w_in=v_w_in, b_gate=v_b_gate, sinks=v_sinks,
             w_br_a=v_w_br_a, w_br_b=v_w_br_b, w_br_c=v_w_br_c, w_out=v_w_out, attn_post_norm=v_attn_post_norm,
             ffn_pre_norm=v_ffn_pre_norm, w_up=v_w_up, conv_w=v_conv_w, conv_b=v_conv_b, w_down=v_w_down,
             ffn_post_norm=v_ffn_post_norm)
    tr = lambda a: jnp.swapaxes(a, 1, 2)
    PB = {nm: (tr(P[nm]) if nm == "w_in" else P[nm]) for nm, _, _ in BIG}
    MB = {nm: (tr(M[nm]) if nm == "w_in" else M[nm]) for nm, _, _ in BIG}
    VB = {nm: (tr(V[nm]) if nm == "w_in" else V[nm]) for nm, _, _ in BIG}
    xi, yi, ci = _coords()
    me = 4 * xi + 2 * yi + ci

    me_arr = me.astype(jnp.int32).reshape(1)

    small_w = _pack([b_gate.reshape(-1), conv_w.reshape(-1)])
    (small_w_all,) = _exchange([small_w], [jax.ShapeDtypeStruct((NDEV,) + small_w.shape, F32)],
                               _whole, _slot, "gather_small_weights")

    groups = [tuple(l * NBIG + t for t in tids) for l in range(DEPTH) for _, tids in LAYER_GROUPS]
    cast = lambda i, m=me_arr: _cast_own(i, PB[BIG[i % NBIG][0]][i // NBIG], m, f"gather_own_l{i // NBIG}_{BIG[i % NBIG][0]}")
    first = list(groups[0])
    rest = [i for grp in groups[1:] for i in grp]
    sems0, _, lands0, tok_first = _xchg_start(None, [cast(i) for i in first], [tuple(range(len(first)))], None,
                                              _shard_window, small_w_all, "gather_start_first", rels=NEAR_RELS, tids=first)
    where_rest = {tid: k for k, tid in enumerate(rest)}
    me_rest = me_arr + tok_first[0, 0:1].astype(jnp.int32)
    sems1, _, lands1, g_tok = _xchg_start(None, [cast(i, me_rest) for i in rest],
                                          [tuple(where_rest[i] for i in grp) for grp in groups[1:]], None,
                                          _shard_window, lands0[0], "gather_start_rest", rels=NEAR_RELS, tids=rest)
    g_sems = list(sems0) + list(sems1)
    tok0 = g_tok[0:1, 0:1]
    lands_now = [None] * (DEPTH * NBIG)
    for i, a in zip(first + rest, list(lands0) + list(lands1)):
        lands_now[i] = a
    fwd_sems = {}
    fwd_plan = {0: (0,), 1: (1,), 2: (2,), 3: (3, 4, 5)}

    def gather_waiter(gi, l, gname, tids):
        def wait(after):
            if gi in fwd_plan:
                gis = fwd_plan[gi]
                flat = [i for g2 in gis for i in groups[g2]]
                where = {tid: k for k, tid in enumerate(flat)}
                fs, new_lands, ftok = _gather_forward(
                    [g_sems[g2] for g2 in gis], [lands_now[i] for i in flat],
                    [[where[i] for i in groups[g2]] for g2 in gis], flat, after, _shard_window, f"gather_forward_{gi}")
                for g2, s in zip(gis, fs):
                    fwd_sems[g2] = s
                for i, a in zip(flat, new_lands):
                    lands_now[i] = a
                after = ftok
            ids = [l * NBIG + t for t in tids]
            _, got = _xchg_wait(fwd_sems[gi], None, [lands_now[i] for i in ids], ids, after,
                                None, _shard_window, f"gather_wait_l{l}_{gname}", rels=FAR_RELS)
            out = {}
            for t, arr in zip(tids, got):
                nm = BIG[t][0]
                out[nm] = arr
            return out
        return wait

    pending = [{gname: gather_waiter(l * len(LAYER_GROUPS) + k, l, gname, tids)
                for k, (gname, tids) in enumerate(LAYER_GROUPS)} for l in range(DEPTH)]
    nbg = DEPTH * 3 * 128
    ncw = DEPTH * 3 * 1024
    flat_all = small_w_all.reshape(NDEV, -1)
    b_gate_full = jnp.transpose(flat_all[:, :nbg].reshape(NDEV, DEPTH, 3, 128), (1, 2, 0, 3)).reshape(DEPTH, 3, D)
    conv_w_full = jnp.transpose(flat_all[:, nbg:nbg + ncw].reshape(NDEV, DEPTH, 3, 1024), (1, 2, 0, 3)).reshape(DEPTH, 3, 2 * D_FF)

    ws = []
    for l in range(DEPTH):
        ws.append(_Weights(dict(
            b_gate=b_gate_full[l], conv_w=conv_w_full[l].reshape(3, 2, D_FF), conv_b=conv_b[l].reshape(2, D_FF),
            sinks=sinks[l].reshape(1, 8),
            attn_pre_norm=attn_pre_norm[l].reshape(1, D), attn_post_norm=attn_post_norm[l].reshape(1, D),
            ffn_pre_norm=ffn_pre_norm[l].reshape(1, D), ffn_post_norm=ffn_post_norm[l].reshape(1, D)), pending[l]))

    rs = {}

    group_tids = dict(LAYER_GROUPS)

    def start_scatter(l, gname, grads_l):
        tids = group_tids[gname]
        blocks, lands_rs = [], []
        for t in tids:
            nm, ax, ext = BIG[t]
            gfull = grads_l[nm].astype(BF16)
            shp = (NDEV, ext, gfull.shape[1]) if ax == 0 else (NDEV, gfull.shape[0], ext)
            blocks.append(gfull)
            lands_rs.append(lax.empty(shp, BF16))
        local = list(range(len(tids)))
        win = lambda j, ref, k: _shard_window(tids[j], ref, k)
        sems, s_thru, l_thru, tok = _xchg_start(blocks, lands_rs, [tuple(local)], win, _slot, me_arr,
                                                f"scatter_start_l{l}_{gname}")
        rs[(l, gname)] = (sems[0], s_thru, l_thru, win, local)
        return tok[0:1, 0:1]

    loss_local, grad_x, grads, g_rel = _local_step(x[0], loss_target[0], ws, rel_bias, tok0, start_scatter)
    loss = lax.psum(loss_local, ("x", "y", "c"))

    stack = lambda nm: jnp.stack([grads[l][nm] for l in range(DEPTH)], axis=0)
    small_names = [nm for nm, _ in SMALL_REPL] + [nm for nm, _, _ in SMALL_SHARD]
    small_g = {"rel_bias": g_rel}
    for nm in small_names[1:]:
        small_g[nm] = stack(nm)
    small_packed = _rowpack(small_g, own=False)

    out_g, out_d, out_m, out_v = {}, {}, {}, {}
    prev = {nm: None for nm, _, _ in BIG}
    todo = [(l, gname) for l in reversed(range(DEPTH)) for gname in ("ffn", "mix", "in")]
    after, small_parts = grad_x, None
    for l, gname in todo:
        if (l, gname) == todo[-1]:
            (small_parts,) = _exchange([small_packed], [jax.ShapeDtypeStruct((NDEV,) + small_packed.shape, F32)],
                                       _whole, _slot, "gather_small_grads", after=after)
            after = small_parts
        sems, s_thru, l_thru, win, local = rs[(l, gname)]
        owns, parts = _xchg_wait(sems, s_thru, l_thru, local, after, win, _slot, f"scatter_wait_l{l}_{gname}")
        for t, own, prt in zip(group_tids[gname], owns, parts):
            nm = BIG[t][0]
            rows = SHARD_ROWS.get(nm, PB[nm].shape[1])
            prev[nm] = _adamw(t, prt, own, me_arr, PB[nm], MB[nm], VB[nm], l, prev[nm], rows, f"adamw_{nm}_l{l}")
            after = prev[nm][1]
    for nm, _, _ in BIG:
        out_g[nm], out_d[nm], out_m[nm], out_v[nm] = [tr(a) if nm == "w_in" else a for a in prev[nm]]
    sm_g, sm_d, sm_m, sm_v = _small_update(small_parts, _rowpack(P, True), _rowpack(M, True), _rowpack(V, True),
                                           me_arr, "small_update")
    for dst, src in ((out_g, sm_g), (out_d, sm_d), (out_m, sm_m), (out_v, sm_v)):
        dst.update(src)

    order = ["rel_bias", "attn_pre_norm", "w_in", "b_gate", "sinks", "w_br_a", "w_br_b", "w_br_c", "w_out",
             "attn_post_norm", "ffn_pre_norm", "w_up", "conv_w", "conv_b", "w_down", "ffn_post_norm"]
    return (loss, grad_x[None], *[out_g[k] for k in order], *[out_d[k] for k in order],
            *[out_m[k] for k in order], *[out_v[k] for k in order])
```

```python
import functools
import math

import numpy as np
import jax
import jax.numpy as jnp
from jax import lax
from jax.experimental import pallas as pl
from jax.experimental.pallas import tpu as pltpu

F32 = jnp.float32
BF16 = jnp.bfloat16

S = 2048
D = 1024
DEPTH = 2
NDEV = 8
HD = 64
BLK = 128
NB = S // BLK
A_GROUPS = ((128, 1), (512, 4), (2048, 16))
NUM_BUCKETS = 32
MAX_DISTANCE = 2048
N_BIAS_HEADS = 20
D_FF = 4096
IN_COLS = 6912
QKV_COLS = 3840
QKV_SLABS = QKV_COLS // 128
GATE_COLS = 3072
EPS = 1e-6
SCALE = HD ** -0.5
NEG = -1e30
LANES = 128

ADAM_LR = 0.001
ADAM_B1 = 0.9
ADAM_B2 = 0.999
ADAM_EPS = 1e-08
ADAM_WD = 0.01
ADAM_STEP = 10

VMEM_LIMIT = 56 * 1024 * 1024
MESH = pl.DeviceIdType.MESH
ANY = pl.BlockSpec(memory_space=pl.ANY)
SMEM = pl.BlockSpec(memory_space=pltpu.SMEM)


def _cp(*sem):
    return pltpu.CompilerParams(dimension_semantics=sem if sem else None, vmem_limit_bytes=VMEM_LIMIT)


def _dot(a, b, ca, cb):
    return lax.dot_general(a, b, (((ca,), (cb,)), ((), ())), preferred_element_type=F32)


def _mm(a, b, *, grid, a_spec, b_spec, out_shape, out_spec, ca, cb, acc_shape, name,
        a_slab=False, b_slab=False, out_slab=False, alias_out=None, after=None):
    nk = grid[2]

    def body(*refs):
        a_ref, b_ref = refs[0], refs[1]
        o_ref, acc_ref = refs[-2], refs[-1]
        k = pl.program_id(2)

        def load(ref, slab):
            if slab:
                return jnp.concatenate([ref[s] for s in range(ref.shape[0])], axis=1).astype(BF16)
            return ref[...].astype(BF16)

        def write(val):
            if out_slab:
                for s in range(o_ref.shape[0]):
                    o_ref[s] = val[:, s * LANES:(s + 1) * LANES].astype(o_ref.dtype)
            else:
                o_ref[...] = val.astype(o_ref.dtype)

        d = _dot(load(a_ref, a_slab), load(b_ref, b_slab), ca, cb)
        if nk == 1:
            write(d)
        elif direct:
            @pl.when(k == 0)
            def _():
                o_ref[...] = d

            @pl.when(k > 0)
            def _():
                o_ref[...] += d
        else:
            @pl.when(k == 0)
            def _():
                acc_ref[...] = d

            if nk > 2:
                @pl.when((k > 0) & (k < nk - 1))
                def _():
                    acc_ref[...] += d

            @pl.when(k == nk - 1)
            def _():
                write(acc_ref[...] + d)

    direct = (not out_slab) and out_shape.dtype == F32
    if nk == 1 or direct:
        acc_shape = (8, LANES)
    in_specs = [a_spec, b_spec]
    args = [a, b]
    aliases = {}
    if alias_out is not None:
        in_specs.append(ANY)
        args.append(alias_out)
        aliases = {2: 0}
    if after is not None:
        in_specs.append(ANY)
        args.append(after)
    return pl.pallas_call(
        body, grid=grid, in_specs=in_specs, out_specs=out_spec, out_shape=out_shape,
        scratch_shapes=[pltpu.VMEM(acc_shape, F32)], input_output_aliases=aliases,
        compiler_params=_cp("parallel", "parallel", "arbitrary"), name=name)(*args)


def _mm_nn(a, b, out_dtype, tm, tn, tk, name):
    m, kk = a.shape
    n = b.shape[1]
    return _mm(a, b, grid=(m // tm, n // tn, kk // tk),
               a_spec=pl.BlockSpec((tm, tk), lambda i, j, k: (i, k)),
               b_spec=pl.BlockSpec((tk, tn), lambda i, j, k: (k, j)),
               out_shape=jax.ShapeDtypeStruct((m, n), out_dtype),
               out_spec=pl.BlockSpec((tm, tn), lambda i, j, k: (i, j)),
               ca=1, cb=0, acc_shape=(tm, tn), name=name)


def _mm_nt(a, b, out_dtype, tm, tn, tk, name):
    m, kk = a.shape
    n = b.shape[0]
    return _mm(a, b, grid=(m // tm, n // tn, kk // tk),
               a_spec=pl.BlockSpec((tm, tk), lambda i, j, k: (i, k)),
               b_spec=pl.BlockSpec((tn, tk), lambda i, j, k: (j, k)),
               out_shape=jax.ShapeDtypeStruct((m, n), out_dtype),
               out_spec=pl.BlockSpec((tm, tn), lambda i, j, k: (i, j)),
               ca=1, cb=1, acc_shape=(tm, tn), name=name)


def _mm_tn(a, b, out_dtype, tm, tn, tk, name):
    kk, m = a.shape
    n = b.shape[1]
    return _mm(a, b, grid=(m // tm, n // tn, kk // tk),
               a_spec=pl.BlockSpec((tk, tm), lambda i, j, k: (k, i)),
               b_spec=pl.BlockSpec((tk, tn), lambda i, j, k: (k, j)),
               out_shape=jax.ShapeDtypeStruct((m, n), out_dtype),
               out_spec=pl.BlockSpec((tm, tn), lambda i, j, k: (i, j)),
               ca=0, cb=0, acc_shape=(tm, tn), name=name)


ROW_TILE = 256


def _rms(x, g):
    r = lax.rsqrt(jnp.mean(x * x, axis=-1, keepdims=True) + EPS)
    return x * r * g


def _prenorm(x, g, name):
    def body(x_ref, g_ref, o_ref):
        o_ref[...] = _rms(x_ref[...], g_ref[...]).astype(BF16)

    return pl.pallas_call(
        body, grid=(S // ROW_TILE,),
        in_specs=[pl.BlockSpec((ROW_TILE, D), lambda i: (i, 0)), pl.BlockSpec((1, D), lambda i: (0, 0))],
        out_specs=pl.BlockSpec((ROW_TILE, D), lambda i: (i, 0)),
        out_shape=jax.ShapeDtypeStruct((S, D), BF16), compiler_params=_cp("parallel"), name=name)(x, g)


def _postnorm_res(x, f, g_post, g_next, name):
    def body(x_ref, f_ref, gp_ref, gn_ref, xo_ref, ho_ref):
        xn = x_ref[...] + _rms(f_ref[...], gp_ref[...])
        xo_ref[...] = xn
        ho_ref[...] = _rms(xn, gn_ref[...]).astype(BF16)

    row = pl.BlockSpec((ROW_TILE, D), lambda i: (i, 0))
    vec = pl.BlockSpec((1, D), lambda i: (0, 0))
    return pl.pallas_call(
        body, grid=(S // ROW_TILE,), in_specs=[row, row, vec, vec], out_specs=[row, row],
        out_shape=[jax.ShapeDtypeStruct((S, D), F32), jax.ShapeDtypeStruct((S, D), BF16)],
        compiler_params=_cp("parallel"), name=name)(x, f, g_post, g_next)


def _norm_bwd(f, g, dys, res, out_dtype, name):
    ndy = len(dys)
    has_res = res is not None

    def body(*refs):
        f_ref, g_ref = refs[0], refs[1]
        dy_refs = refs[2:2 + ndy]
        res_ref = refs[2 + ndy] if has_res else None
        o_ref, dg_ref = refs[-2], refs[-1]
        fv = f_ref[...]
        dy = dy_refs[0][...].astype(F32)
        for r in dy_refs[1:]:
            dy = dy + r[...].astype(F32)
        r = lax.rsqrt(jnp.mean(fv * fv, axis=-1, keepdims=True) + EPS)
        n = fv * r
        dn = dy * g_ref[...]
        df = r * (dn - n * jnp.mean(dn * n, axis=-1, keepdims=True))
        if has_res:
            df = df + res_ref[...]
        o_ref[...] = df.astype(out_dtype)

        @pl.when(pl.program_id(0) == 0)
        def _():
            dg_ref[...] = jnp.zeros((1, D), F32)

        dg_ref[...] += jnp.sum(dy * n, axis=0, keepdims=True)

    row = pl.BlockSpec((ROW_TILE, D), lambda i: (i, 0))
    vec = pl.BlockSpec((1, D), lambda i: (0, 0))
    in_specs = [row, vec] + [row] * ndy + ([row] if has_res else [])
    args = [f, g] + list(dys) + ([res] if has_res else [])
    return pl.pallas_call(
        body, grid=(S // ROW_TILE,), in_specs=in_specs, out_specs=[row, vec],
        out_shape=[jax.ShapeDtypeStruct((S, D), out_dtype), jax.ShapeDtypeStruct((1, D), F32)],
        compiler_params=_cp("arbitrary"), name=name)(*args)


def _loss_head(y, target, name):
    def body(y_ref, t_ref, dy_ref, l_ref):
        e = y_ref[...] - t_ref[...]
        dy_ref[...] = e * (1.0 / D)

        @pl.when(pl.program_id(0) == 0)
        def _():
            l_ref[...] = jnp.zeros((8, LANES), F32)

        l_ref[...] += jnp.sum(e * e) * (0.5 / D)

    row = pl.BlockSpec((ROW_TILE, D), lambda i: (i, 0))
    return pl.pallas_call(
        body, grid=(S // ROW_TILE,), in_specs=[row, row],
        out_specs=[row, pl.BlockSpec((8, LANES), lambda i: (0, 0))],
        out_shape=[jax.ShapeDtypeStruct((S, D), F32), jax.ShapeDtypeStruct((8, LANES), F32)],
        compiler_params=_cp("arbitrary"), name=name)(y, target)


def _bucket_tiles():
    a = np.arange(BLK)[:, None]
    b = np.arange(2 * BLK)[None, :]
    dist = a + BLK - b
    out = np.zeros((4, 2, BLK, 2 * BLK), np.int32)
    cfg = [(w // d, d) for w, d in A_GROUPS] + [(BLK - 1, 1)]
    for gi, (max_dist, d) in enumerate(cfg):
        band = (dist >= 0) & (dist <= max_dist)
        tok = np.maximum(dist, 0) * d
        nf = np.maximum(tok, 1).astype(np.float32)
        max_exact = NUM_BUCKETS // 2
        large = max_exact + (np.log(nf / np.float32(max_exact)) / np.float32(math.log(MAX_DISTANCE / max_exact))
                             * np.float32(NUM_BUCKETS - max_exact)).astype(np.int32)
        large = np.minimum(large, NUM_BUCKETS - 1)
        bkt = np.where(tok < max_exact, tok, large).astype(np.int32)
        full = np.where(band, bkt, -1)
        out[gi, 1] = full
        out[gi, 0] = np.where(b >= BLK, full, -1)
    return out


def _bias_tiles(rel_bias, buckets, name):
    def body(tab_ref, bkt_ref, o_ref):
        h = pl.program_id(0)
        bkt = bkt_ref[...]
        acc = jnp.zeros(bkt.shape, F32)
        for bb in range(NUM_BUCKETS):
            acc = jnp.where(bkt == bb, tab_ref[bb, h], acc)
        o_ref[...] = jnp.where(bkt < 0, NEG, acc)

    return pl.pallas_call(
        body, grid=(N_BIAS_HEADS,),
        in_specs=[SMEM, pl.BlockSpec((None, 2, BLK, 2 * BLK), lambda h: (jnp.minimum(h // 4, 3), 0, 0, 0))],
        out_specs=pl.BlockSpec((None, 2, BLK, 2 * BLK), lambda h: (h, 0, 0, 0)),
        out_shape=jax.ShapeDtypeStruct((N_BIAS_HEADS, 2, BLK, 2 * BLK), F32),
        compiler_params=_cp("arbitrary"), name=name)(rel_bias, buckets)


def _bias_grad(gs, buckets, name):
    ng = len(gs)

    def body(*refs):
        g_refs = refs[:ng]
        bkt_ref, o_ref = refs[ng], refs[ng + 1]
        h = pl.program_id(0)
        g = g_refs[0][...]
        for r in g_refs[1:]:
            g = g + r[...]
        bkt = bkt_ref[...]
        row = lax.broadcasted_iota(jnp.int32, (NUM_BUCKETS, LANES), 0)
        lane = lax.broadcasted_iota(jnp.int32, (NUM_BUCKETS, LANES), 1)

        @pl.when(h == 0)
        def _():
            o_ref[...] = jnp.zeros((NUM_BUCKETS, LANES), F32)

        acc = o_ref[...]
        for bb in range(NUM_BUCKETS):
            s = jnp.sum(jnp.where(bkt == bb, g, 0.0))
            acc = jnp.where((row == bb) & (lane == h), s, acc)
        o_ref[...] = acc

    g_spec = pl.BlockSpec((None, BLK, 2 * BLK), lambda h: (h, 0, 0))
    return pl.pallas_call(
        body, grid=(N_BIAS_HEADS,),
        in_specs=[g_spec] * ng + [pl.BlockSpec((None, None, BLK, 2 * BLK), lambda h: (jnp.minimum(h // 4, 3), 1, 0, 0))],
        out_specs=pl.BlockSpec((NUM_BUCKETS, LANES), lambda h: (0, 0)),
        out_shape=jax.ShapeDtypeStruct((NUM_BUCKETS, LANES), F32),
        compiler_params=_cp("arbitrary"), name=name)(*gs, buckets)


def _to_class_major(src_ref, dst_refs, d, fn=None):
    ln = S // d
    for r in range(d):
        v = src_ref[pl.ds(r, ln, stride=d), :] if d > 1 else src_ref[...]
        outs = fn(v) if fn is not None else (v,) * len(dst_refs)
        for dst, o in zip(dst_refs, outs):
            dst[pl.ds(r * ln, ln), :] = o.astype(dst.dtype)


def _head_masks(rows):
    lane = lax.broadcasted_iota(jnp.int32, (rows, LANES), 1)
    return lane < HD, lane >= HD


def _split_heads(v):
    m0, m1 = _head_masks(v.shape[0])
    return jnp.where(m0, v, 0.0), jnp.where(m1, v, 0.0)


def _dup_head(v, hi):
    m0, _ = _head_masks(v.shape[0])
    r = pltpu.roll(v, HD, 1)
    return jnp.where(m0, jnp.where(hi, r, v), jnp.where(hi, v, r))


def _block_rows(b, d):
    nbc = NB // d
    i = b % nbc
    r = b // nbc
    has_prev = (i > 0).astype(jnp.int32)
    prev = pl.multiple_of(jnp.maximum(b - 1, 0) * BLK, BLK)
    nat = i * (BLK * d) + r
    return has_prev, prev, nat


def _lane_halves(v0, v1):
    lane = lax.broadcasted_iota(jnp.int32, (v0.shape[0], LANES), 1)
    return jnp.where(lane < HD, v0, v1)


def _band_fwd(proj, bias, sinks, *, d, q0, k0, v0, npairs, bias0, shared_kv, name):
    def body(sink_ref, q_ref, k_ref, v_ref, b_ref, num_ref, st_ref, qz0, qz1, ks, vs):
        p = pl.program_id(0)
        kv = (lambda v: (_dup_head(v, p >= 2),)) if shared_kv else None
        _to_class_major(q_ref, (qz0, qz1), d, lambda v: _split_heads(v * SCALE))
        _to_class_major(k_ref, (ks,), d, kv)
        _to_class_major(v_ref, (vs,), d, kv)
        lane = lax.broadcasted_iota(jnp.int32, (BLK, LANES), 1)

        def blk(b, carry):
            has_prev, prev, nat = _block_rows(b, d)
            cur = pl.multiple_of(b * BLK, BLK)
            k2 = jnp.concatenate([ks[pl.ds(prev, BLK), :], ks[pl.ds(cur, BLK), :]], axis=0)
            v2 = jnp.concatenate([vs[pl.ds(prev, BLK), :], vs[pl.ds(cur, BLK), :]], axis=0)
            nums, ms, ls = [], [], []
            for hh, qz in enumerate((qz0, qz1)):
                z = _dot(qz[pl.ds(cur, BLK), :], k2, 1, 1) + b_ref[hh, has_prev]
                m = jnp.max(z, axis=1, keepdims=True)
                e = jnp.exp(z - m)
                l = jnp.sum(e, axis=1, keepdims=True)
                num = _dot(e.astype(BF16), v2, 1, 0)
                if shared_kv:
                    sink = sink_ref[0, 2 * p + hh]
                    mx = jnp.maximum(m, sink)
                    c = jnp.exp(m - mx)
                    zden = l * c + jnp.exp(sink - mx)
                    num = num * (c / zden)
                    m = mx + jnp.log(zden)
                ls.append(l)
                ms.append(m)
                nums.append(num)
            num_t = jnp.where(lane < HD, nums[0], nums[1])
            if shared_kv:
                st_t = jnp.where(lane < HD, ms[0], ms[1])
            else:
                st_t = jnp.where(lane < 32, ms[0], jnp.where(lane < 64, ls[0], jnp.where(lane < 96, ms[1], ls[1])))
            if d > 1:
                num_ref[pl.ds(nat, BLK, stride=d), :] = num_t
                st_ref[pl.ds(nat, BLK, stride=d), :] = st_t
            else:
                num_ref[pl.ds(cur, BLK), :] = num_t
                st_ref[pl.ds(cur, BLK), :] = st_t
            return carry

        lax.fori_loop(0, NB, blk, 0, unroll=8)

    slab = lambda off, per_pair: pl.BlockSpec((None, S, LANES), (lambda p: (off + p, 0, 0)) if per_pair else (lambda p: (off, 0, 0)))
    out = pl.BlockSpec((None, S, LANES), lambda p: (p, 0, 0))
    return pl.pallas_call(
        body, grid=(npairs,),
        in_specs=[SMEM, slab(q0, True), slab(k0, not shared_kv), slab(v0, not shared_kv),
                  pl.BlockSpec((None, 2, 2, BLK, 2 * BLK), lambda p: (bias0 + p, 0, 0, 0, 0))],
        out_specs=[out, out],
        out_shape=[jax.ShapeDtypeStruct((npairs, S, LANES), F32)] * 2,
        scratch_shapes=[pltpu.VMEM((S, LANES), BF16)] * 4,
        compiler_params=_cp("arbitrary"), name=name)(sinks, proj, proj, proj, bias)


def _combine_a(nums, stats, name):
    rt = 512

    def body(n0, n1, n2, s0, s1, s2, o_ref, l_ref):
        n_refs, s_refs = (n0, n1, n2), (s0, s1, s2)
        outs, lses = [], []
        for hh in range(2):
            ms = [s[:, 64 * hh:64 * hh + 1] for s in s_refs]
            ls = [s[:, 64 * hh + 32:64 * hh + 33] for s in s_refs]
            mx = jnp.maximum(jnp.maximum(ms[0], ms[1]), ms[2])
            cs = [jnp.exp(m - mx) for m in ms]
            z = cs[0] * ls[0] + cs[1] * ls[1] + cs[2] * ls[2]
            acc = cs[0] * n_refs[0][:, hh * HD:(hh + 1) * HD]
            acc = acc + cs[1] * n_refs[1][:, hh * HD:(hh + 1) * HD]
            acc = acc + cs[2] * n_refs[2][:, hh * HD:(hh + 1) * HD]
            outs.append(acc / z)
            lses.append(mx + jnp.log(z))
        o_ref[...] = jnp.concatenate(outs, axis=1)
        l_ref[...] = _lane_halves(lses[0], lses[1])

    spec = pl.BlockSpec((None, rt, LANES), lambda p, i: (p, i, 0))
    return pl.pallas_call(
        body, grid=(2, S // rt), in_specs=[spec] * 6, out_specs=[spec, spec],
        out_shape=[jax.ShapeDtypeStruct((2, S, LANES), F32)] * 2,
        compiler_params=_cp("parallel", "parallel"), name=name)(*nums, *stats)


def _band_bwd(proj, bias, o, do, lse, sinks, *, d, q0, k0, v0, npairs, bias0, shared_kv, name):
    nkv = 1 if shared_kv else npairs

    def body(sink_ref, q_ref, k_ref, v_ref, b_ref, o_ref, do_ref, lse_ref,
             dq_ref, dk_ref, dv_ref, g_ref, ds_ref,
             qz0, qz1, ks, vs, doz0, doz1, ls0, ls1, dls0, dls1, stage, dq_nat, dk_cm, dv_cm, kv_nat, dk_acc, dv_acc):
        p = pl.program_id(0)
        m0, m1 = _head_masks(S)
        prod = do_ref[...] * o_ref[...]
        dl0 = jnp.sum(jnp.where(m0, prod, 0.0), axis=1, keepdims=True)
        dl1 = jnp.sum(jnp.where(m1, prod, 0.0), axis=1, keepdims=True)
        if shared_kv:
            row8 = lax.broadcasted_iota(jnp.int32, (8, LANES), 0)
            lane8 = lax.broadcasted_iota(jnp.int32, (8, LANES), 1)
            t = jnp.zeros((8, LANES), F32)
            lv = lse_ref[...]
            for hh in range(2):
                sink = sink_ref[0, 2 * p + hh]
                ps = jnp.exp(sink - lv[:, 64 * hh:64 * hh + 1])
                dsink = -jnp.sum(ps * (dl0 if hh == 0 else dl1))
                t = jnp.where((row8 == 0) & (lane8 == hh), dsink, t)
            ds_ref[...] = t
        else:
            ds_ref[...] = jnp.zeros((8, LANES), F32)
        kv = (lambda v: (_dup_head(v, p >= 2),)) if shared_kv else None
        _to_class_major(q_ref, (qz0, qz1), d, lambda v: _split_heads(v * SCALE))
        _to_class_major(k_ref, (ks,), d, kv)
        _to_class_major(v_ref, (vs,), d, kv)
        _to_class_major(do_ref, (doz0, doz1), d, _split_heads)
        def spread(v):
            a0, a1 = _head_masks(v.shape[0])
            r = pltpu.roll(v, HD, 1)
            return jnp.where(a0, v, r), jnp.where(a1, v, r)

        _to_class_major(lse_ref, (ls0, ls1), d, spread)
        stage[...] = jnp.where(m0, dl0, dl1)
        _to_class_major(stage, (dls0, dls1), d, spread)

        dk_cm[...] = jnp.zeros((S, LANES), F32)
        dv_cm[...] = jnp.zeros((S, LANES), F32)
        g_ref[...] = jnp.zeros((2, BLK, 2 * BLK), F32)
        lane = lax.broadcasted_iota(jnp.int32, (BLK, LANES), 1)

        def blk(b, carry):
            has_prev, prev, nat = _block_rows(b, d)
            cur = pl.multiple_of(b * BLK, BLK)
            k2 = jnp.concatenate([ks[pl.ds(prev, BLK), :], ks[pl.ds(cur, BLK), :]], axis=0)
            v2 = jnp.concatenate([vs[pl.ds(prev, BLK), :], vs[pl.ds(cur, BLK), :]], axis=0)
            dqs, dks, dvs = [], [], []
            for hh, (qz, doz, lsr, dlr) in enumerate(((qz0, doz0, ls0, dls0), (qz1, doz1, ls1, dls1))):
                qb = qz[pl.ds(cur, BLK), :]
                dob = doz[pl.ds(cur, BLK), :]
                lb = lsr[pl.ds(cur, BLK), :]
                dlb = dlr[pl.ds(cur, BLK), :]
                z = _dot(qb, k2, 1, 1) + b_ref[hh, has_prev]
                pr = jnp.exp(z - jnp.concatenate([lb, lb], axis=1))
                dp = _dot(dob, v2, 1, 1)
                dz = pr * (dp - jnp.concatenate([dlb, dlb], axis=1))
                g_ref[hh] += dz
                dzb = dz.astype(BF16)
                dqs.append(_dot(dzb, k2, 1, 0))
                dks.append(_dot(dzb, qb, 0, 0))
                dvs.append(_dot(pr.astype(BF16), dob, 0, 0))
            dq_t = jnp.where(lane < HD, dqs[0], dqs[1]) * SCALE
            dk_t = dks[0] + dks[1]
            dv_t = dvs[0] + dvs[1]
            dk_cm[pl.ds(prev, BLK), :] += dk_t[:BLK]
            dk_cm[pl.ds(cur, BLK), :] += dk_t[BLK:]
            dv_cm[pl.ds(prev, BLK), :] += dv_t[:BLK]
            dv_cm[pl.ds(cur, BLK), :] += dv_t[BLK:]
            if d > 1:
                dq_nat[pl.ds(nat, BLK, stride=d), :] = dq_t
            else:
                dq_nat[pl.ds(cur, BLK), :] = dq_t
            return carry

        lax.fori_loop(0, NB, blk, 0, unroll=8)
        dq_ref[...] = dq_nat[...].astype(BF16)

        def from_class_major(src, dst_ref):
            if d == 1:
                dst_ref[...] = src[...].astype(BF16)
            else:
                ln = S // d
                for r in range(d):
                    kv_nat[pl.ds(r, ln, stride=d), :] = src[pl.ds(r * ln, ln), :]
                dst_ref[...] = kv_nat[...].astype(BF16)

        if not shared_kv:
            from_class_major(dk_cm, dk_ref)
            from_class_major(dv_cm, dv_ref)
        else:
            @pl.when(p == 0)
            def _():
                dk_acc[...] = jnp.zeros((S, LANES), F32)
                dv_acc[...] = jnp.zeros((S, LANES), F32)

            mine = m1 == (p >= 2)
            for cm, acc in ((dk_cm, dk_acc), (dv_cm, dv_acc)):
                val = cm[...]
                acc[...] += jnp.where(mine, val + pltpu.roll(val, HD, 1), 0.0)

            @pl.when(p == npairs - 1)
            def _():
                from_class_major(dk_acc, dk_ref)
                from_class_major(dv_acc, dv_ref)

    slab = lambda off, per_pair: pl.BlockSpec((None, S, LANES), (lambda p: (off + p, 0, 0)) if per_pair else (lambda p: (off, 0, 0)))
    pair = pl.BlockSpec((None, S, LANES), lambda p: (p, 0, 0))
    kv_out = pair if not shared_kv else pl.BlockSpec((None, S, LANES), lambda p: (0, 0, 0))
    return pl.pallas_call(
        body, grid=(npairs,),
        in_specs=[SMEM, slab(q0, True), slab(k0, not shared_kv), slab(v0, not shared_kv),
                  pl.BlockSpec((None, 2, 2, BLK, 2 * BLK), lambda p: (bias0 + p, 0, 0, 0, 0)),
                  pair, pair, pair],
        out_specs=[pair, kv_out, kv_out,
                   pl.BlockSpec((None, 2, BLK, 2 * BLK), lambda p: (p, 0, 0, 0)),
                   pl.BlockSpec((None, 8, LANES), lambda p: (p, 0, 0))],
        out_shape=[jax.ShapeDtypeStruct((npairs, S, LANES), BF16),
                   jax.ShapeDtypeStruct((nkv, S, LANES), BF16),
                   jax.ShapeDtypeStruct((nkv, S, LANES), BF16),
                   jax.ShapeDtypeStruct((npairs, 2, BLK, 2 * BLK), F32),
                   jax.ShapeDtypeStruct((npairs, 8, LANES), F32)],
        scratch_shapes=[pltpu.VMEM((S, LANES), BF16)] * 6 + [pltpu.VMEM((S, LANES), F32)] * 11,
        compiler_params=_cp("arbitrary"), name=name)(sinks, proj, proj, proj, bias, o, do, lse)


KC = 512
NSUB = KC // BLK
QB = 512
QPG = KC // QB


def _split2(x):
    hi = x.astype(BF16)
    lo = (x - hi.astype(F32)).astype(BF16)
    return hi, lo


def _tri_ones(cmp):
    jj = lax.broadcasted_iota(jnp.int32, (2 * BLK, BLK), 0) % BLK
    ss = lax.broadcasted_iota(jnp.int32, (2 * BLK, BLK), 1)
    return jnp.concatenate([cmp(jj, ss).astype(BF16), jnp.ones((2 * BLK, BLK), BF16)], axis=1)


def _sub_sums(x, tri1):
    n = x.shape[0]
    st = jnp.concatenate([x[:, s * BLK:(s + 1) * BLK] for s in range(NSUB)], axis=0)
    hi, lo = _split2(st)
    r = _dot(jnp.concatenate([hi, lo], axis=1), tri1, 1, 0)
    return ([r[s * n:(s + 1) * n, :BLK] for s in range(NSUB)], [r[s * n:(s + 1) * n, BLK:] for s in range(NSUB)])


def _log_sig_pair(z):
    lb = jnp.minimum(z, 0.0) - jnp.log1p(jnp.exp(-jnp.abs(z)))
    return lb, lb - z


QGROUPS = NB // NSUB


def _stick_fwd(proj, *, q0, k0, v0, name):
    def body(q_ref, k_ref, v_ref, o_ref, t_ref, qs, ks, vs):
        qs[...] = (q_ref[...] * SCALE).astype(BF16)
        ks[...] = k_ref[...].astype(BF16)
        vs[...] = v_ref[...].astype(BF16)
        tri1 = _tri_ones(lambda j, s: j > s)
        col = lax.broadcasted_iota(jnp.int32, (QB, KC), 1)
        rowi = lax.broadcasted_iota(jnp.int32, (QB, KC), 0)

        for qg in range(QGROUPS):
            def qblock(ii, carry0, qg=qg):
                t0 = pl.multiple_of((qg * QPG + ii) * QB, QB)
                qb = qs[pl.ds(t0, QB), :]
                accs = [jnp.zeros((QB, HD), F32)] * 2
                runs = [jnp.zeros((QB, BLK), F32)] * 2
                for c in reversed(range(qg + 1)):
                    s0 = c * KC
                    diag = c == qg
                    before = (s0 + col) < (t0 + rowi) if diag else None
                    for hh in range(2):
                        kh = ks[s0:s0 + KC, hh * HD:(hh + 1) * HD]
                        vh = vs[s0:s0 + KC, hh * HD:(hh + 1) * HD]
                        lb, lk = _log_sig_pair(_dot(qb[:, hh * HD:(hh + 1) * HD], kh, 1, 1))
                        if diag:
                            lk = jnp.where(before, lk, 0.0)
                        suf, tot = _sub_sums(lk, tri1)
                        ws, run = [], runs[hh]
                        for s in reversed(range(NSUB)):
                            ws.append(jnp.exp(lb[:, s * BLK:(s + 1) * BLK] + suf[s] + run))
                            run = run + tot[s]
                        w = jnp.concatenate(ws[::-1], axis=1)
                        if diag:
                            w = jnp.where(before, w, 0.0)
                        accs[hh] = accs[hh] + _dot(w.astype(BF16), vh, 1, 0)
                        runs[hh] = run
                o_ref[pl.ds(t0, QB), :] = jnp.concatenate(accs, axis=1)
                t_ref[pl.ds(t0, QB), :] = _lane_halves(runs[0], runs[1])
                return carry0

            lax.fori_loop(0, QPG, qblock, 0)

    slab = lambda off: pl.BlockSpec((None, S, LANES), lambda p: (off + p, 0, 0))
    out = pl.BlockSpec((None, S, LANES), lambda p: (p, 0, 0))
    return pl.pallas_call(
        body, grid=(2,), in_specs=[slab(q0), slab(k0), slab(v0)], out_specs=[out, out],
        out_shape=[jax.ShapeDtypeStruct((2, S, LANES), F32)] * 2,
        scratch_shapes=[pltpu.VMEM((S, LANES), BF16)] * 3,
        compiler_params=_cp("arbitrary"), name=name)(proj, proj, proj)


def _stick_bwd(proj, do, tot, *, q0, k0, v0, name):
    def body(q_ref, k_ref, v_ref, do_ref, t_ref, dq_ref, dk_ref, dv_ref, qs, ks, vs, dos, dk_acc, dv_acc):
        qs[...] = (q_ref[...] * SCALE).astype(BF16)
        ks[...] = k_ref[...].astype(BF16)
        vs[...] = v_ref[...].astype(BF16)
        dos[...] = do_ref[...].astype(BF16)
        dk_acc[...] = jnp.zeros((2, S, HD), F32)
        dv_acc[...] = jnp.zeros((2, S, HD), F32)
        tri_inc = _tri_ones(lambda j, s: j <= s)
        tri_exc = _tri_ones(lambda j, s: j < s)
        col = lax.broadcasted_iota(jnp.int32, (QB, KC), 1)
        rowi = lax.broadcasted_iota(jnp.int32, (QB, KC), 0)

        for qg in range(QGROUPS):
            def qblock(ii, carry0, qg=qg):
                t0 = pl.multiple_of((qg * QPG + ii) * QB, QB)
                qb = qs[pl.ds(t0, QB), :]
                dob = dos[pl.ds(t0, QB), :]
                tb = t_ref[pl.ds(t0, QB), :]
                dqs = [jnp.zeros((QB, HD), F32)] * 2
                pruns = [jnp.zeros((QB, BLK), F32)] * 2
                eruns = [jnp.zeros((QB, BLK), F32)] * 2
                for c in range(qg + 1):
                    s0 = c * KC
                    diag = c == qg
                    before = (s0 + col) < (t0 + rowi) if diag else None
                    for hh in range(2):
                        qh = qb[:, hh * HD:(hh + 1) * HD]
                        doh = dob[:, hh * HD:(hh + 1) * HD]
                        tt = tb[:, 64 * hh:64 * hh + 1]
                        kh = ks[s0:s0 + KC, hh * HD:(hh + 1) * HD]
                        vh = vs[s0:s0 + KC, hh * HD:(hh + 1) * HD]
                        lb, lk = _log_sig_pair(_dot(qh, kh, 1, 1))
                        if diag:
                            lk = jnp.where(before, lk, 0.0)
                        pin, ptot = _sub_sums(lk, tri_inc)
                        ws, prun = [], pruns[hh]
                        for s in range(NSUB):
                            ws.append(jnp.exp(lb[:, s * BLK:(s + 1) * BLK] + (tt - (pin[s] + prun))))
                            prun = prun + ptot[s]
                        w = jnp.concatenate(ws, axis=1)
                        if diag:
                            w = jnp.where(before, w, 0.0)
                        e = w * _dot(doh, vh, 1, 1)
                        pex, etot = _sub_sums(e, tri_exc)
                        cs, erun = [], eruns[hh]
                        for s in range(NSUB):
                            cs.append(pex[s] + erun)
                            erun = erun + etot[s]
                        sig = jnp.exp(lb)
                        dz = e * (1.0 - sig) - jnp.concatenate(cs, axis=1) * sig
                        if diag:
                            dz = jnp.where(before, dz, 0.0)
                        dz = dz.astype(BF16)
                        dqs[hh] = dqs[hh] + _dot(dz, kh, 1, 0)
                        dk_acc[hh, s0:s0 + KC, :] += _dot(dz, qh, 0, 0)
                        dv_acc[hh, s0:s0 + KC, :] += _dot(w.astype(BF16), doh, 0, 0)
                        pruns[hh], eruns[hh] = prun, erun
                dq_ref[pl.ds(t0, QB), :] = (jnp.concatenate(dqs, axis=1) * SCALE).astype(BF16)
                return carry0

            lax.fori_loop(0, QPG, qblock, 0)
        dk_ref[...] = jnp.concatenate([dk_acc[0], dk_acc[1]], axis=1).astype(BF16)
        dv_ref[...] = jnp.concatenate([dv_acc[0], dv_acc[1]], axis=1).astype(BF16)

    slab = lambda off: pl.BlockSpec((None, S, LANES), lambda p: (off + p, 0, 0))
    pair = pl.BlockSpec((None, S, LANES), lambda p: (p, 0, 0))
    return pl.pallas_call(
        body, grid=(2,), in_specs=[slab(q0), slab(k0), slab(v0), pair, pair], out_specs=[pair] * 3,
        out_shape=[jax.ShapeDtypeStruct((2, S, LANES), BF16)] * 3,
        scratch_shapes=[pltpu.VMEM((S, LANES), BF16)] * 4 + [pltpu.VMEM((2, S, HD), F32)] * 2,
        compiler_params=_cp("arbitrary"), name=name)(proj, proj, proj, do, tot)


def _cat_slabs(ref):
    return jnp.concatenate([ref[s] for s in range(ref.shape[0])], axis=1)


def _merge_fwd(o_a, o_b, o_c, gates, b_gate, wa, wb, wc, w_out, name):
    tm = ROW_TILE

    def body(oa_ref, ob_ref, oc_ref, g_ref, bg_ref, wa_ref, wb_ref, wc_ref, wo_ref, mg_ref, mo_ref):
        acc = jnp.zeros((tm, D), F32)
        for i, (o_ref, w_ref) in enumerate(((oa_ref, wa_ref), (ob_ref, wb_ref), (oc_ref, wc_ref))):
            pr = _dot(_cat_slabs(o_ref).astype(BF16), w_ref[...], 1, 0)
            sg = jax.nn.sigmoid(g_ref[:, i * D:(i + 1) * D] + bg_ref[i:i + 1, :])
            acc = acc + sg * pr
        mg = acc.astype(BF16)
        mg_ref[...] = mg
        mo_ref[...] = _dot(mg, wo_ref[...], 1, 0)

    slabs = lambda n: pl.BlockSpec((n, tm, LANES), lambda i: (0, i, 0))
    full = lambda r, c: pl.BlockSpec((r, c), lambda i: (0, 0))
    row = pl.BlockSpec((tm, D), lambda i: (i, 0))
    return pl.pallas_call(
        body, grid=(S // tm,),
        in_specs=[slabs(2), slabs(4), slabs(2), pl.BlockSpec((tm, GATE_COLS), lambda i: (i, 0)), full(3, D),
                  full(256, D), full(512, D), full(256, D), full(D, D)],
        out_specs=[row, row],
        out_shape=[jax.ShapeDtypeStruct((S, D), BF16), jax.ShapeDtypeStruct((S, D), F32)],
        compiler_params=_cp("parallel"), name=name)(o_a, o_b, o_c, gates, b_gate, wa, wb, wc, w_out)


def _merge_bwd(d_mo, o_a, o_b, o_c, gates, b_gate, wa, wb, wc, w_out, name):
    tm = ROW_TILE

    def body(dmo_ref, oa_ref, ob_ref, oc_ref, g_ref, bg_ref, wa_ref, wb_ref, wc_ref, wo_ref,
             doa_ref, dob_ref, doc_ref, dg_ref, dwa_ref, dwb_ref, dwc_ref, dbg_ref):
        @pl.when(pl.program_id(0) == 0)
        def _():
            dwa_ref[...] = jnp.zeros(dwa_ref.shape, F32)
            dwb_ref[...] = jnp.zeros(dwb_ref.shape, F32)
            dwc_ref[...] = jnp.zeros(dwc_ref.shape, F32)
            dbg_ref[...] = jnp.zeros(dbg_ref.shape, F32)

        dmg = _dot(dmo_ref[...], wo_ref[...], 1, 1)
        trip = ((oa_ref, wa_ref, doa_ref, dwa_ref), (ob_ref, wb_ref, dob_ref, dwb_ref), (oc_ref, wc_ref, doc_ref, dwc_ref))
        for i, (o_ref, w_ref, do_ref, dw_ref) in enumerate(trip):
            ob = _cat_slabs(o_ref).astype(BF16)
            pr = _dot(ob, w_ref[...], 1, 0)
            sg = jax.nn.sigmoid(g_ref[:, i * D:(i + 1) * D] + bg_ref[i:i + 1, :])
            dgate = dmg * pr * sg * (1.0 - sg)
            dg_ref[:, i * D:(i + 1) * D] = dgate.astype(BF16)
            dbg_ref[i:i + 1, :] += jnp.sum(dgate, axis=0, keepdims=True)
            dpr = (dmg * sg).astype(BF16)
            do = _dot(dpr, w_ref[...], 1, 1)
            for s in range(do_ref.shape[0]):
                do_ref[s] = do[:, s * LANES:(s + 1) * LANES]
            dw_ref[...] += _dot(ob, dpr, 0, 0)

    slabs = lambda n: pl.BlockSpec((n, tm, LANES), lambda i: (0, i, 0))
    full = lambda r, c: pl.BlockSpec((r, c), lambda i: (0, 0))
    row = pl.BlockSpec((tm, D), lambda i: (i, 0))
    return pl.pallas_call(
        body, grid=(S // tm,),
        in_specs=[row, slabs(2), slabs(4), slabs(2), pl.BlockSpec((tm, GATE_COLS), lambda i: (i, 0)), full(3, D),
                  full(256, D), full(512, D), full(256, D), full(D, D)],
        out_specs=[slabs(2), slabs(4), slabs(2), pl.BlockSpec((tm, GATE_COLS), lambda i: (i, 0)),
                   full(256, D), full(512, D), full(256, D), full(3, D)],
        out_shape=[jax.ShapeDtypeStruct((2, S, LANES), F32), jax.ShapeDtypeStruct((4, S, LANES), F32),
                   jax.ShapeDtypeStruct((2, S, LANES), F32), jax.ShapeDtypeStruct((S, GATE_COLS), BF16),
                   jax.ShapeDtypeStruct((256, D), F32), jax.ShapeDtypeStruct((512, D), F32),
                   jax.ShapeDtypeStruct((256, D), F32), jax.ShapeDtypeStruct((3, D), F32)],
        compiler_params=_cp("arbitrary"), name=name)(d_mo, o_a, o_b, o_c, gates, b_gate, wa, wb, wc, w_out)


FC = 256
GELU_K = math.sqrt(2.0 / math.pi)
GELU_C = 0.044715


RC = 64
NRC = S // RC


def _down(tail, cur, n):
    row = lax.broadcasted_iota(jnp.int32, tail.shape, 0)
    rolled = pltpu.roll(cur, n, 0)
    first = jnp.where(row < n, pltpu.roll(tail, n, 0), rolled[0:8])
    return jnp.concatenate([first, rolled[8:]], axis=0)


def _up(cur, head, n):
    row = lax.broadcasted_iota(jnp.int32, head.shape, 0)
    rolled = pltpu.roll(cur, RC - n, 0)
    last = jnp.where(row >= 8 - n, pltpu.roll(head, 8 - n, 0), rolled[RC - 8:])
    return jnp.concatenate([rolled[:RC - 8], last], axis=0)


def _conv_chunk(load, j, w_ref, b_ref, half):
    r0 = pl.multiple_of(j * RC, RC)
    cur = load(r0, RC).astype(F32)
    tail = load(pl.multiple_of(jnp.maximum(r0 - 16, 0), 16), 16).astype(F32)[8:16]
    tail = jnp.where(j > 0, tail, 0.0)
    d1 = _down(tail, cur, 1)
    d2 = _down(tail, cur, 2)
    y = w_ref[0:1, half, :] * d2 + w_ref[1:2, half, :] * d1 + w_ref[2:3, half, :] * cur + b_ref[half:half + 1, :]
    return y, cur, d1, d2


def _chunk(j):
    return pl.ds(pl.multiple_of(j * RC, RC), RC)


def _fold8(x):
    return jnp.sum(x.reshape(RC // 8, 8, x.shape[-1]), axis=0)


def _ffn_act(u, conv_w, conv_b, name):
    def body(u_ref, w_ref, b_ref, a_ref):
        def step(j, carry):
            yg = _conv_chunk(lambda r, n: u_ref[0, pl.ds(r, n), :], j, w_ref, b_ref, 0)[0]
            yv = _conv_chunk(lambda r, n: u_ref[1, pl.ds(r, n), :], j, w_ref, b_ref, 1)[0]
            th = jnp.tanh(GELU_K * (yg + GELU_C * yg * yg * yg))
            a_ref[_chunk(j), :] = (0.5 * yg * (1.0 + th) * yv).astype(BF16)
            return carry

        lax.fori_loop(0, NRC, step, 0)

    return pl.pallas_call(
        body, grid=(D_FF // FC,),
        in_specs=[pl.BlockSpec((2, S, FC), lambda j: (0, 0, j)), pl.BlockSpec((3, 2, FC), lambda j: (0, 0, j)),
                  pl.BlockSpec((2, FC), lambda j: (0, j))],
        out_specs=pl.BlockSpec((S, FC), lambda j: (0, j)),
        out_shape=jax.ShapeDtypeStruct((S, D_FF), BF16),
        compiler_params=_cp("parallel"), name=name)(u, conv_w, conv_b)


def _ffn_act_bwd(u, d_a, conv_w, conv_b, name):
    def body(u_ref, da_ref, w_ref, b_ref, du_ref, dw_ref, db_ref, dy_s):
        def first(j, acc):
            yg, ug, ug1, ug2 = _conv_chunk(lambda r, n: u_ref[0, pl.ds(r, n), :], j, w_ref, b_ref, 0)
            yv, uv, uv1, uv2 = _conv_chunk(lambda r, n: u_ref[1, pl.ds(r, n), :], j, w_ref, b_ref, 1)
            th = jnp.tanh(GELU_K * (yg + GELU_C * yg * yg * yg))
            gelu = 0.5 * yg * (1.0 + th)
            dgelu = 0.5 * (1.0 + th) + 0.5 * yg * (1.0 - th * th) * GELU_K * (1.0 + 3.0 * GELU_C * yg * yg)
            da = da_ref[_chunk(j), :].astype(F32)
            dyg = da * yv * dgelu
            dyv = da * gelu
            dy_s[0, _chunk(j), :] = dyg
            dy_s[1, _chunk(j), :] = dyv
            new = (_fold8(dyg * ug2), _fold8(dyg * ug1), _fold8(dyg * ug), _fold8(dyg),
                   _fold8(dyv * uv2), _fold8(dyv * uv1), _fold8(dyv * uv), _fold8(dyv))
            return tuple(a + n for a, n in zip(acc, new))

        acc = lax.fori_loop(0, NRC, first, tuple(jnp.zeros((8, FC), F32) for _ in range(8)))
        for half in range(2):
            for k in range(3):
                dw_ref[k:k + 1, half, :] = jnp.sum(acc[4 * half + k], axis=0, keepdims=True)
            db_ref[half:half + 1, :] = jnp.sum(acc[4 * half + 3], axis=0, keepdims=True)

        def second(j, carry):
            for half in range(2):
                cur = dy_s[half, _chunk(j), :]
                h0 = pl.multiple_of(jnp.minimum((j + 1) * RC, S - 8), 8)
                head = jnp.where(j < NRC - 1, dy_s[half, pl.ds(h0, 8), :], 0.0)
                du = (w_ref[2:3, half, :] * cur + w_ref[1:2, half, :] * _up(cur, head, 1)
                      + w_ref[0:1, half, :] * _up(cur, head, 2))
                du_ref[half, _chunk(j), :] = du.astype(BF16)
            return carry

        lax.fori_loop(0, NRC, second, 0)

    return pl.pallas_call(
        body, grid=(D_FF // FC,),
        in_specs=[pl.BlockSpec((2, S, FC), lambda j: (0, 0, j)), pl.BlockSpec((S, FC), lambda j: (0, j)),
                  pl.BlockSpec((3, 2, FC), lambda j: (0, 0, j)), pl.BlockSpec((2, FC), lambda j: (0, j))],
        out_specs=[pl.BlockSpec((2, S, FC), lambda j: (0, 0, j)), pl.BlockSpec((3, 2, FC), lambda j: (0, 0, j)),
                   pl.BlockSpec((2, FC), lambda j: (0, j))],
        out_shape=[jax.ShapeDtypeStruct((2, S, D_FF), BF16), jax.ShapeDtypeStruct((3, 2, D_FF), F32),
                   jax.ShapeDtypeStruct((2, D_FF), F32)],
        scratch_shapes=[pltpu.VMEM((2, S, FC), F32)],
        compiler_params=_cp("parallel"), name=name)(u, d_a, conv_w, conv_b)


def _layer_fwd(x, h1, w, bias, lname):
    n = lambda s: f"{lname}_{s}"
    w.need("in", h1)
    tn = 768
    proj = _mm(h1, w["w_in"], grid=(S // 1024, QKV_COLS // tn, 1),
               a_spec=pl.BlockSpec((1024, D), lambda i, j, k: (i, 0)),
               b_spec=pl.BlockSpec((tn, D), lambda i, j, k: (j, 0)),
               out_shape=jax.ShapeDtypeStruct((QKV_SLABS, S, LANES), F32),
               out_spec=pl.BlockSpec((tn // LANES, 1024, LANES), lambda i, j, k: (j, i, 0)),
               ca=1, cb=1, acc_shape=(1024, tn), out_slab=True, name=n("proj_qkv"))
    gates = _mm(h1, w["w_in"], grid=(S // 1024, GATE_COLS // tn, 1),
                a_spec=pl.BlockSpec((1024, D), lambda i, j, k: (i, 0)),
                b_spec=pl.BlockSpec((tn, D), lambda i, j, k: (j + QKV_COLS // tn, 0)),
                out_shape=jax.ShapeDtypeStruct((S, GATE_COLS), BF16),
                out_spec=pl.BlockSpec((1024, tn), lambda i, j, k: (i, j)),
                ca=1, cb=1, acc_shape=(1024, tn), name=n("proj_gate"))
    nums, stats = [], []
    for g, (_, d) in enumerate(A_GROUPS):
        nm, st = _band_fwd(proj, bias, w["sinks"], d=d, q0=2 * g, k0=6 + 2 * g, v0=12 + 2 * g, npairs=2, bias0=2 * g,
                           shared_kv=False, name=n(f"attn_a{g}_fwd"))
        nums.append(nm)
        stats.append(st)
    o_a, lse_a = _combine_a(nums, stats, n("attn_a_combine"))
    o_b, lse_b = _band_fwd(proj, bias, w["sinks"], d=1, q0=18, k0=22, v0=23, npairs=4, bias0=6, shared_kv=True,
                           name=n("attn_b_fwd"))
    o_c, tot_c = _stick_fwd(proj, q0=24, k0=26, v0=28, name=n("attn_c_fwd"))
    w.need("mix", tot_c)
    merged, mo = _merge_fwd(o_a, o_b, o_c, gates, w["b_gate"], w["w_br_a"], w["w_br_b"], w["w_br_c"], w["w_out"], n("merge_fwd"))
    x2, h2 = _postnorm_res(x, mo, w["attn_post_norm"], w["ffn_pre_norm"], n("attn_post"))
    w.need("ffn", h2)
    u = _mm(h2, w["w_up"], grid=(S // 1024, 2 * D_FF // 1024, 1),
            a_spec=pl.BlockSpec((1024, D), lambda i, j, k: (i, 0)),
            b_spec=pl.BlockSpec((D, 1024), lambda i, j, k: (0, j)),
            out_shape=jax.ShapeDtypeStruct((2, S, D_FF), BF16),
            out_spec=pl.BlockSpec((None, 1024, 1024), lambda i, j, k: (j // 4, i, j % 4)),
            ca=1, cb=0, acc_shape=(1024, 1024), name=n("ffn_up"))
    a = _ffn_act(u, w["conv_w"], w["conv_b"], n("ffn_act"))
    fo = _mm_nn(a, w["w_down"], F32, 1024, 1024, 2048, n("ffn_down"))
    saved = dict(x=x, h1=h1, proj=proj, gates=gates, o_a=o_a, lse_a=lse_a, o_b=o_b, lse_b=lse_b, o_c=o_c, tot_c=tot_c,
                 merged=merged, mo=mo, x2=x2, h2=h2, u=u, a=a, fo=fo)
    return saved


def _layer_bwd(dx3, sv, w, bias, lname, tok=None, on_part=None):
    n = lambda s: f"{lname}_{s}"
    g = {}

    def part(group, vec):
        t = on_part(group, g) if on_part is not None else None
        return vec if t is None else vec + t

    gain = w["ffn_post_norm"] if tok is None else w["ffn_post_norm"] + tok
    d_fo, g["ffn_post_norm"] = _norm_bwd(sv["fo"], gain, [dx3], None, BF16, n("ffn_post_bwd"))
    d_a = _mm_nt(d_fo, w["w_down"], BF16, 1024, 1024, 1024, n("ffn_down_bwd_x"))
    g["w_down"] = _mm_tn(sv["a"], d_fo, BF16, 1024, 1024, S, n("ffn_down_bwd_w"))
    d_u, dcw, dcb = _ffn_act_bwd(sv["u"], d_a, w["conv_w"], w["conv_b"], n("ffn_act_bwd"))
    g["conv_w"] = dcw.reshape(3, 2 * D_FF)
    g["conv_b"] = dcb.reshape(1, 2 * D_FF)
    g["w_up"] = _mm(sv["h2"], d_u, grid=(1, 2 * D_FF // 1024, 1),
                    a_spec=pl.BlockSpec((S, D), lambda i, j, k: (k, 0)),
                    b_spec=pl.BlockSpec((None, S, 1024), lambda i, j, k: (j // 4, k, j % 4)),
                    out_shape=jax.ShapeDtypeStruct((D, 2 * D_FF), BF16),
                    out_spec=pl.BlockSpec((D, 1024), lambda i, j, k: (0, j)),
                    ca=0, cb=0, acc_shape=(D, 1024), name=n("ffn_up_bwd_w"))
    tok_ffn = on_part("ffn", g) if on_part is not None else None
    d_h2 = _mm(d_u, w["w_up"], grid=(S // 1024, 1, 2),
               a_spec=pl.BlockSpec((None, 1024, D_FF), lambda i, j, k: (k, i, 0)),
               b_spec=pl.BlockSpec((D, D_FF), lambda i, j, k: (0, k)),
               out_shape=jax.ShapeDtypeStruct((S, D), F32),
               out_spec=pl.BlockSpec((1024, D), lambda i, j, k: (i, 0)),
               ca=1, cb=1, acc_shape=(1024, D), after=tok_ffn, name=n("ffn_up_bwd_x"))
    dx2, g["ffn_pre_norm"] = _norm_bwd(sv["x2"], w["ffn_pre_norm"], [d_h2], dx3, F32, n("ffn_pre_bwd"))
    d_mo, g["attn_post_norm"] = _norm_bwd(sv["mo"], w["attn_post_norm"], [dx2], None, BF16, n("attn_post_bwd"))
    g["w_out"] = _mm_tn(sv["merged"], d_mo, BF16, 1024, 1024, S, n("out_bwd_w"))
    do_a, do_b, do_c, d_gates, dwa, dwb, dwc, g["b_gate"] = _merge_bwd(
        d_mo, sv["o_a"], sv["o_b"], sv["o_c"], sv["gates"], w["b_gate"], w["w_br_a"], w["w_br_b"], w["w_br_c"],
        w["w_out"], n("merge_bwd"))
    g["w_br_a"], g["w_br_b"], g["w_br_c"] = dwa, dwb, dwc
    sinks = part("mix", w["sinks"])
    proj = sv["proj"]
    dqa, dka, dva, gbias = [], [], [], []
    for gi, (_, d) in enumerate(A_GROUPS):
        dq, dk, dv, gg, _ = _band_bwd(proj, bias, sv["o_a"], do_a, sv["lse_a"], sinks, d=d, q0=2 * gi, k0=6 + 2 * gi,
                                      v0=12 + 2 * gi, npairs=2, bias0=2 * gi, shared_kv=False, name=n(f"attn_a{gi}_bwd"))
        dqa.append(dq), dka.append(dk), dva.append(dv), gbias.append(gg)
    dqb, dkb, dvb, ggb, dsink = _band_bwd(proj, bias, sv["o_b"], do_b, sv["lse_b"], sinks, d=1, q0=18, k0=22, v0=23,
                                          npairs=4, bias0=6, shared_kv=True, name=n("attn_b_bwd"))
    gbias.append(ggb)
    g["bias_g"] = jnp.concatenate(gbias, axis=0).reshape(N_BIAS_HEADS, BLK, 2 * BLK)
    g["sinks"] = dsink[:, 0, :2].reshape(1, 8)
    dqc, dkc, dvc = _stick_bwd(proj, do_c, sv["tot_c"], q0=24, k0=26, v0=28, name=n("attn_c_bwd"))
    dqkv = jnp.concatenate(dqa + dka + dva + [dqb, dkb, dvb, dqc, dkc, dvc], axis=0)
    ts = 6
    tsx = QKV_SLABS
    dw_in = _mm(dqkv, sv["h1"], grid=(QKV_SLABS // ts, 1, 1),
                a_spec=pl.BlockSpec((ts, S, LANES), lambda i, j, k: (i, k, 0)),
                b_spec=pl.BlockSpec((S, D), lambda i, j, k: (k, 0)),
                out_shape=jax.ShapeDtypeStruct((IN_COLS, D), BF16),
                out_spec=pl.BlockSpec((ts * LANES, D), lambda i, j, k: (i, 0)),
                ca=0, cb=0, acc_shape=(ts * LANES, D), a_slab=True, name=n("in_bwd_w_qkv"))
    g["w_in"] = _mm(d_gates, sv["h1"], grid=(GATE_COLS // 768, 1, 1),
                    a_spec=pl.BlockSpec((S, 768), lambda i, j, k: (k, i)),
                    b_spec=pl.BlockSpec((S, D), lambda i, j, k: (k, 0)),
                    out_shape=jax.ShapeDtypeStruct((IN_COLS, D), BF16),
                    out_spec=pl.BlockSpec((768, D), lambda i, j, k: (i + QKV_COLS // 768, 0)),
                    ca=0, cb=0, acc_shape=(768, D), alias_out=dw_in, name=n("in_bwd_w_gate"))
    tok_in = on_part("in", g) if on_part is not None else None
    d_h1a = _mm(dqkv, w["w_in"], grid=(S // 1024, 1, QKV_SLABS // tsx),
                a_spec=pl.BlockSpec((tsx, 1024, LANES), lambda i, j, k: (k, i, 0)),
                b_spec=pl.BlockSpec((tsx * LANES, D), lambda i, j, k: (k, 0)),
                out_shape=jax.ShapeDtypeStruct((S, D), F32),
                out_spec=pl.BlockSpec((1024, D), lambda i, j, k: (i, 0)),
                ca=1, cb=0, acc_shape=(1024, D), a_slab=True, after=tok_in, name=n("in_bwd_x_qkv"))
    d_h1b = _mm(d_gates, w["w_in"], grid=(S // 1024, 1, GATE_COLS // 768),
                a_spec=pl.BlockSpec((1024, 768), lambda i, j, k: (i, k)),
                b_spec=pl.BlockSpec((768, D), lambda i, j, k: (k + QKV_COLS // 768, 0)),
                out_shape=jax.ShapeDtypeStruct((S, D), F32),
                out_spec=pl.BlockSpec((1024, D), lambda i, j, k: (i, 0)),
                ca=1, cb=0, acc_shape=(1024, D), after=tok_in, name=n("in_bwd_x_gate"))
    dx, g["attn_pre_norm"] = _norm_bwd(sv["x"], w["attn_pre_norm"], [d_h1a, d_h1b], dx2, F32, n("attn_pre_bwd"))
    return dx, g, tok_in


def _local_step(x, target, ws, rel_bias, tok=None, on_grads=None):
    buckets = jnp.asarray(_bucket_tiles())
    bias = _bias_tiles(rel_bias, buckets, "bias_tiles").reshape(N_BIAS_HEADS // 2, 2, 2, BLK, 2 * BLK)
    saved = []
    gain0 = ws[0]["attn_pre_norm"] if tok is None else ws[0]["attn_pre_norm"] + tok
    h1 = _prenorm(x, gain0, "l0_attn_pre")
    for l in range(DEPTH):
        sv = _layer_fwd(x, h1, ws[l], bias, f"l{l}")
        saved.append(sv)
        g_next = ws[l + 1]["attn_pre_norm"] if l + 1 < DEPTH else ws[l]["attn_pre_norm"]
        x, h1 = _postnorm_res(sv["x2"], sv["fo"], ws[l]["ffn_post_norm"], g_next, f"l{l}_ffn_post")
    dy, loss_tile = _loss_head(x, target, "loss_head")
    grads = [None] * DEPTH
    tok = None
    for l in reversed(range(DEPTH)):
        on_part = None if on_grads is None else functools.partial(on_grads, l)
        dy, grads[l], tok = _layer_bwd(dy, saved[l], ws[l], bias, f"l{l}", tok, on_part)
    g_rel = _bias_grad([grads[l]["bias_g"] for l in range(DEPTH)], buckets, "bias_grad")[:, :N_BIAS_HEADS]
    return loss_tile[0, 0], dy, grads, g_rel


def _coords():
    return lax.axis_index("x"), lax.axis_index("y"), lax.axis_index("c")


def _peer(rel):
    x, y, c = _coords()
    return (1 - x if rel & 4 else x, 1 - y if rel & 2 else y, 1 - c if rel & 1 else c)


def _exchange(srcs, dst_shapes, src_win, dst_win, name, after=None):
    nt = len(srcs)
    extra = [] if after is None else [after]

    def body(*refs):
        src_refs, dst_refs = refs[:nt], refs[nt + len(extra):2 * nt + len(extra)]
        send_sems, recv_sems, local_sems = refs[2 * nt + len(extra):]
        x, y, c = _coords()
        me = 4 * x + 2 * y + c
        locals_ = []
        for t in range(nt):
            cp = pltpu.make_async_copy(src_win(t, src_refs[t], me), dst_win(t, dst_refs[t], me), local_sems.at[t])
            cp.start()
            locals_.append(cp)
        sends = []
        for rel in range(1, NDEV):
            px, py, pc = _peer(rel)
            q = 4 * px + 2 * py + pc
            for t in range(nt):
                cp = pltpu.make_async_remote_copy(
                    src_ref=src_win(t, src_refs[t], q), dst_ref=dst_win(t, dst_refs[t], me),
                    send_sem=send_sems.at[rel - 1, t], recv_sem=recv_sems.at[rel - 1, t],
                    device_id=(px, py, pc), device_id_type=MESH)
                cp.start()
                sends.append(cp)
        for rel in range(1, NDEV):
            px, py, pc = _peer(rel)
            q = 4 * px + 2 * py + pc
            for t in range(nt):
                pltpu.make_async_remote_copy(
                    src_ref=src_win(t, src_refs[t], me), dst_ref=dst_win(t, dst_refs[t], q),
                    send_sem=send_sems.at[rel - 1, t], recv_sem=recv_sems.at[rel - 1, t],
                    device_id=(px, py, pc), device_id_type=MESH).wait_recv()
        for cp in sends:
            cp.wait_send()
        for cp in locals_:
            cp.wait()

    return pl.pallas_call(
        body, in_specs=[ANY] * (nt + len(extra)), out_specs=[ANY] * nt, out_shape=dst_shapes,
        scratch_shapes=[pltpu.SemaphoreType.DMA((NDEV - 1, nt)), pltpu.SemaphoreType.DMA((NDEV - 1, nt)),
                        pltpu.SemaphoreType.DMA((nt,))],
        name=name)(*srcs, *extra)


BIG = (("w_in", 0, 864), ("w_br_a", 1, 128), ("w_br_b", 1, 128), ("w_br_c", 1, 128), ("w_out", 0, 128),
       ("w_up", 1, 1024), ("w_down", 0, 512))


NBIG = len(BIG)
BIG_FULL = {"w_in": (IN_COLS, D), "w_br_a": (256, D), "w_br_b": (512, D), "w_br_c": (256, D), "w_out": (D, D),
            "w_up": (D, 2 * D_FF), "w_down": (D_FF, D)}
SHARD_ROWS = {"w_in": 288, "w_up": 256, "w_down": 256}
LAYER_GROUPS = (("in", (0,)), ("mix", (1, 2, 3, 4)), ("ffn", (5, 6)))

HBM_SPEC = pl.BlockSpec(memory_space=pltpu.HBM)
SEM_SPEC = pl.BlockSpec(memory_space=pltpu.SEMAPHORE)


def _hbm(a):
    return pltpu.with_memory_space_constraint(a, pltpu.HBM)


def _shard_window(t, ref, k):
    nm, ax, ext = BIG[t % NBIG]
    off = pl.multiple_of(k * ext, ext)
    if ax == 0:
        return ref.at[pl.ds(off, ext), :]
    return ref.at[:, pl.ds(off, ext)]


def _whole(t, ref, k):
    return ref


def _slot(t, ref, k):
    return ref.at[k]


def _own_block_spec(t, rows, me_of):
    nm, ax, ext = BIG[t % NBIG]
    r, c = BIG_FULL[nm]
    if ax == 0:
        return pl.BlockSpec((rows, c), lambda i, m: (me_of(m) * (ext // rows) + i, 0))
    return pl.BlockSpec((rows, ext), lambda i, m: (i, me_of(m)))


def _cast_own(t, shards, me_arr, name):
    nm, ax, ext = BIG[t % NBIG]
    layer = t // NBIG
    _, nr, nc = shards.shape
    rows = SHARD_ROWS.get(nm, nr)
    shape = BIG_FULL[nm]

    def body(m_ref, s_ref, o_ref):
        o_ref[...] = s_ref[...].astype(BF16)

    return pl.pallas_call(
        body, grid_spec=pltpu.PrefetchScalarGridSpec(
            num_scalar_prefetch=1, grid=(nr // rows,),
            in_specs=[pl.BlockSpec((None, rows, nc), lambda i, m: (layer, i, 0))],
            out_specs=_own_block_spec(t, rows, lambda m: m[0])),
        out_shape=jax.ShapeDtypeStruct(shape, BF16), compiler_params=_cp("arbitrary"), name=name)(me_arr, shards)


ALL_RELS = tuple(range(1, NDEV))
NEAR_RELS = (1, 2, 4, 6)
FAR_RELS = (2, 4, 6)


def _xchg_start(srcs, lands, groups, src_win, dst_win, after, name, rels=ALL_RELS, tids=None):
    ns = 0 if srcs is None else len(srcs)
    nt, ng = len(lands), len(groups)
    ins = ([] if srcs is None else list(srcs)) + list(lands)

    def body(*refs):
        src_refs, land_refs = refs[:ns], refs[ns:ns + nt]
        sems = refs[ns + nt + 1:ns + nt + 1 + 2 * ng]
        token = refs[-1]
        x, y, c = _coords()
        me = 4 * x + 2 * y + c
        for gi, grp in enumerate(groups):
            for j, t in enumerate(grp):
                tid = t if tids is None else tids[t]
                for ri, rel in enumerate(rels):
                    px, py, pc = _peer(rel)
                    q = 4 * px + 2 * py + pc
                    src = dst_win(tid, land_refs[t], me) if srcs is None else src_win(tid, src_refs[t], q)
                    pltpu.make_async_remote_copy(
                        src_ref=src, dst_ref=dst_win(tid, land_refs[t], me),
                        send_sem=sems[2 * gi].at[ri * len(grp) + j],
                        recv_sem=sems[2 * gi + 1].at[ri * len(grp) + j],
                        device_id=(px, py, pc), device_id_type=MESH).start()
        token[...] = jnp.zeros((8, LANES), F32)

    out_shape = []
    for grp in groups:
        out_shape += [pltpu.SemaphoreType.DMA((len(rels) * len(grp),))] * 2
    out_shape += [pltpu.HBM(a.shape, a.dtype) for a in ins]
    out_shape.append(jax.ShapeDtypeStruct((8, LANES), F32))
    outs = pl.pallas_call(
        body, in_specs=[HBM_SPEC] * len(ins) + [ANY],
        out_specs=[SEM_SPEC] * (2 * ng) + [HBM_SPEC] * len(ins) + [pl.BlockSpec(memory_space=pltpu.VMEM)],
        out_shape=out_shape, input_output_aliases={i: 2 * ng + i for i in range(len(ins))},
        compiler_params=pltpu.CompilerParams(has_side_effects=pltpu.SideEffectType.DATAFLOW_SIDE_EFFECTING),
        name=name)(*[_hbm(a) for a in ins], after)
    sems = [(outs[2 * gi], outs[2 * gi + 1]) for gi in range(ng)]
    thru = list(outs[2 * ng:2 * ng + len(ins)])
    return sems, (None if srcs is None else thru[:ns]), thru[ns:], outs[-1]


def _xchg_wait(sems, srcs, lands, tids, after, src_win, dst_win, name, rels=ALL_RELS):
    ns = 0 if srcs is None else len(srcs)
    n = len(lands)
    send_sem, recv_sem = sems
    ins = ([] if srcs is None else list(srcs)) + list(lands)

    def body(*refs):
        src_refs, land_refs = refs[:ns], refs[ns:ns + n]
        ssem, rsem = refs[ns + n], refs[ns + n + 1]
        x, y, c = _coords()
        me = 4 * x + 2 * y + c
        for j, t in enumerate(tids):
            for ri, rel in enumerate(rels):
                px, py, pc = _peer(rel)
                q = 4 * px + 2 * py + pc
                src = dst_win(t, land_refs[j], me) if srcs is None else src_win(t, src_refs[j], q)
                cp = pltpu.make_async_remote_copy(
                    src_ref=src, dst_ref=dst_win(t, land_refs[j], q),
                    send_sem=ssem.at[ri * n + j], recv_sem=rsem.at[ri * n + j],
                    device_id=(px, py, pc), device_id_type=MESH)
                cp.wait_send()
                cp.wait_recv()

    outs = pl.pallas_call(
        body, in_specs=[HBM_SPEC] * len(ins) + [SEM_SPEC, SEM_SPEC, ANY], out_specs=[HBM_SPEC] * len(ins),
        out_shape=[pltpu.HBM(a.shape, a.dtype) for a in ins],
        input_output_aliases={i: i for i in range(len(ins))},
        compiler_params=pltpu.CompilerParams(has_side_effects=pltpu.SideEffectType.DATAFLOW_SIDE_EFFECTING),
        name=name)(*ins, send_sem, recv_sem, after)
    return (None if srcs is None else list(outs[:ns])), list(outs[ns:])


def _gather_forward(sems_in, lands, groups, tids, after, dst_win, name):
    nt, ng = len(lands), len(groups)

    def body(*refs):
        land_refs = refs[:nt]
        in_sems = refs[nt:nt + 2 * ng]
        out_sems = refs[nt + 2 * ng + 1:nt + 4 * ng + 1]
        token = refs[-1]
        x, y, c = _coords()
        me = 4 * x + 2 * y + c
        sib = (x, y, 1 - c)
        for gi, grp in enumerate(groups):
            n = len(grp)
            for j, pos in enumerate(grp):
                t = tids[pos]
                for ri, rel in enumerate(NEAR_RELS):
                    px, py, pc = _peer(rel)
                    q = 4 * px + 2 * py + pc
                    cp = pltpu.make_async_remote_copy(
                        src_ref=dst_win(t, land_refs[pos], me), dst_ref=dst_win(t, land_refs[pos], q),
                        send_sem=in_sems[2 * gi].at[ri * n + j], recv_sem=in_sems[2 * gi + 1].at[ri * n + j],
                        device_id=(px, py, pc), device_id_type=MESH)
                    cp.wait_send()
                    cp.wait_recv()
            for j, pos in enumerate(grp):
                t = tids[pos]
                for fi, rel in enumerate(FAR_RELS):
                    px, py, pc = _peer(rel)
                    q = 4 * px + 2 * py + pc
                    win = dst_win(t, land_refs[pos], q)
                    pltpu.make_async_remote_copy(
                        src_ref=win, dst_ref=win,
                        send_sem=out_sems[2 * gi].at[fi * n + j], recv_sem=out_sems[2 * gi + 1].at[fi * n + j],
                        device_id=sib, device_id_type=MESH).start()
        token[...] = jnp.zeros((8, LANES), F32)

    out_shape = []
    for grp in groups:
        out_shape += [pltpu.SemaphoreType.DMA((len(FAR_RELS) * len(grp),))] * 2
    out_shape += [pltpu.HBM(a.shape, a.dtype) for a in lands]
    out_shape.append(jax.ShapeDtypeStruct((8, LANES), F32))
    flat_sems = [s for pair in sems_in for s in pair]
    outs = pl.pallas_call(
        body, in_specs=[HBM_SPEC] * nt + [SEM_SPEC] * (2 * ng) + [ANY],
        out_specs=[SEM_SPEC] * (2 * ng) + [HBM_SPEC] * nt + [pl.BlockSpec(memory_space=pltpu.VMEM)],
        out_shape=out_shape, input_output_aliases={i: 2 * ng + i for i in range(nt)},
        compiler_params=pltpu.CompilerParams(has_side_effects=pltpu.SideEffectType.DATAFLOW_SIDE_EFFECTING),
        name=name)(*[_hbm(a) for a in lands], *flat_sems, after)
    sems = [(outs[2 * gi], outs[2 * gi + 1]) for gi in range(ng)]
    return sems, list(outs[2 * ng:2 * ng + nt]), outs[-1]


class _Weights:
    def __init__(self, ready, pending=None):
        self.ready = dict(ready)
        self.pending = dict(pending or {})

    def __getitem__(self, k):
        return self.ready[k]

    def need(self, group, after):
        fn = self.pending.pop(group, None)
        if fn is not None:
            self.ready.update(fn(after))


def _adamw_math(w, g, m, v):
    m2 = ADAM_B1 * m + (1.0 - ADAM_B1) * g
    v2 = ADAM_B2 * v + (1.0 - ADAM_B2) * (g * g)
    m_hat = m2 / (1.0 - ADAM_B1 ** ADAM_STEP)
    v_hat = v2 / (1.0 - ADAM_B2 ** ADAM_STEP)
    delta = -ADAM_LR * (m_hat / (jnp.sqrt(v_hat) + ADAM_EPS) + ADAM_WD * w)
    return delta, m2, v2


def _adamw(t, parts, own, me_arr, w, m, v, layer, prev, rows, name):
    nl, nr, nc = w.shape

    def body(me_ref, p_ref, own_ref, w_ref, m_ref, v_ref, *rest):
        g_ref, d_ref, m2_ref, v2_ref = rest[-4:]
        me = me_ref[0]
        g = None
        for k in range(NDEV):
            term = jnp.where(me == k, own_ref[...], p_ref[k]).astype(F32)
            g = term if g is None else g + term
        delta, m2, v2 = _adamw_math(w_ref[...], g, m_ref[...], v_ref[...])
        g_ref[...] = g
        d_ref[...] = delta
        m2_ref[...] = m2
        v2_ref[...] = v2

    blk = pl.BlockSpec((None, rows, nc), lambda i, mm: (layer, i, 0))
    pblk = pl.BlockSpec((NDEV, rows, nc), lambda i, mm: (0, i, 0))
    extra = [] if prev is None else list(prev)
    return pl.pallas_call(
        body, grid_spec=pltpu.PrefetchScalarGridSpec(
            num_scalar_prefetch=1, grid=(nr // rows,),
            in_specs=[pblk, _own_block_spec(t, rows, lambda mm: mm[0]), blk, blk, blk] + [ANY] * len(extra),
            out_specs=[blk] * 4),
        out_shape=[jax.ShapeDtypeStruct(w.shape, F32)] * 4,
        input_output_aliases={6 + k: k for k in range(len(extra))},
        compiler_params=_cp("arbitrary"), name=name)(me_arr, parts, own, w, m, v, *extra)


SMALL_REPL = (("rel_bias", NUM_BUCKETS * N_BIAS_HEADS), ("attn_pre_norm", DEPTH * D), ("sinks", DEPTH * 8),
              ("attn_post_norm", DEPTH * D), ("ffn_pre_norm", DEPTH * D), ("conv_b", DEPTH * 2 * D_FF),
              ("ffn_post_norm", DEPTH * D))
SMALL_SHARD = (("b_gate", (DEPTH, 3, D), 128), ("conv_w", (DEPTH, 3, 2 * D_FF), 1024))


def _pack(vecs):
    flat = jnp.concatenate([v.reshape(-1).astype(F32) for v in vecs])
    n = flat.shape[0]
    rows = -(-n // (8 * LANES)) * 8
    return jnp.pad(flat, (0, rows * LANES - n)).reshape(rows, LANES)


def _unpack(packed, sizes):
    flat = packed.reshape(-1)
    out, off = [], 0
    for sz in sizes:
        out.append(flat[off:off + sz])
        off += sz
    return out


ROWPACK = (("rel_bias", 32, 32, (NUM_BUCKETS, N_BIAS_HEADS)), ("sinks", 8, 8, (DEPTH, 8)),
           ("attn_pre_norm", 16, 16, (DEPTH, D)), ("attn_post_norm", 16, 16, (DEPTH, D)),
           ("ffn_pre_norm", 16, 16, (DEPTH, D)), ("ffn_post_norm", 16, 16, (DEPTH, D)),
           ("conv_b", 128, 128, (DEPTH, 2 * D_FF)), ("b_gate", 48, 8, (DEPTH, 3, 128)),
           ("conv_w", 384, 48, (DEPTH, 3, 1024)))
ROWS_FULL = sum(r for _, r, _, _ in ROWPACK)
ROWS_OWN = sum(r for _, _, r, _ in ROWPACK)


def _as_rows(a, rows):
    a = a.astype(F32)
    if a.shape[-1] < LANES:
        a = jnp.pad(a.reshape(-1, a.shape[-1]), ((0, 0), (0, LANES - a.shape[-1])))
    a = a.reshape(-1, LANES)
    return jnp.pad(a, ((0, rows - a.shape[0]), (0, 0)))


def _rowpack(arrs, own):
    return jnp.concatenate([_as_rows(arrs[nm], ro if own else rf) for nm, rf, ro, _ in ROWPACK], axis=0)


def _small_update(parts, w, m, v, me_arr, name):
    nsm = len(ROWPACK)

    def body(me_ref, p_ref, w_ref, m_ref, v_ref, *rest):
        outs = rest[:4 * nsm]
        gfull, g_s, d_s, m_s, v_s = rest[4 * nsm:]
        me = me_ref[0]
        g = p_ref[0]
        for k in range(1, NDEV):
            g = g + p_ref[k]
        gfull[...] = g
        of, oo = 0, 0
        for nm, rf, ro, _ in ROWPACK:
            if nm == "b_gate":
                g_s[oo:oo + ro, :] = jnp.zeros((ro, LANES), F32)
                for r in range(DEPTH * 3):
                    g_s[oo + r:oo + r + 1, :] = gfull[pl.ds(of + r * NDEV + me, 1), :]
            elif nm == "conv_w":
                for r in range(DEPTH * 3):
                    g_s[oo + r * 8:oo + r * 8 + 8, :] = gfull[pl.ds(pl.multiple_of(of + r * 64 + me * 8, 8), 8), :]
            else:
                g_s[oo:oo + ro, :] = gfull[of:of + rf, :]
            of, oo = of + rf, oo + ro
        delta, m2, v2 = _adamw_math(w_ref[...], g_s[...], m_ref[...], v_ref[...])
        d_s[...] = delta
        m_s[...] = m2
        v_s[...] = v2
        for kind, src in enumerate((g_s, d_s, m_s, v_s)):
            oo = 0
            for idx, (nm, rf, ro, shp) in enumerate(ROWPACK):
                o_ref = outs[kind * nsm + idx]
                if nm in ("rel_bias", "sinks"):
                    o_ref[...] = src[oo:oo + shp[0], 0:shp[1]]
                elif nm == "b_gate":
                    for l in range(DEPTH):
                        o_ref[l] = src[oo + 3 * l:oo + 3 * l + 3, :]
                elif nm == "conv_w":
                    for l in range(DEPTH):
                        for k in range(8):
                            o_ref[l, :, k * LANES:(k + 1) * LANES] = src[pl.ds(oo + 24 * l + k, 3, stride=8), :]
                else:
                    per = shp[1] // LANES
                    for k in range(per):
                        o_ref[:, k * LANES:(k + 1) * LANES] = src[pl.ds(oo + k, DEPTH, stride=per), :]
                oo += ro

    vm = pl.BlockSpec(memory_space=pltpu.VMEM)
    shapes = [jax.ShapeDtypeStruct(shp, F32) for _ in range(4) for _, _, _, shp in ROWPACK]
    outs = pl.pallas_call(
        body, in_specs=[SMEM, vm, vm, vm, vm], out_specs=[vm] * (4 * nsm), out_shape=shapes,
        scratch_shapes=[pltpu.VMEM((ROWS_FULL, LANES), F32)] + [pltpu.VMEM((ROWS_OWN, LANES), F32)] * 4,
        name=name)(me_arr, parts, w, m, v)
    names = [nm for nm, _, _, _ in ROWPACK]
    return [dict(zip(names, outs[kind * nsm:(kind + 1) * nsm])) for kind in range(4)]


def kernel(x, rel_bias, attn_pre_norm, w_in, b_gate, sinks, w_br_a, w_br_b, w_br_c, w_out, attn_post_norm, ffn_pre_norm, w_up, conv_w, conv_b, w_down, ffn_post_norm, loss_target, m_rel_bias, m_attn_pre_norm, m_w_in, m_b_gate, m_sinks, m_w_br_a, m_w_br_b, m_w_br_c, m_w_out, m_attn_post_norm, m_ffn_pre_norm, m_w_up, m_conv_w, m_conv_b, m_w_down, m_ffn_post_norm, v_rel_bias, v_attn_pre_norm, v_w_in, v_b_gate, v_sinks, v_w_br_a, v_w_br_b, v_w_br_c, v_w_out, v_attn_post_norm, v_ffn_pre_norm, v_w_up, v_conv_w, v_conv_b, v_w_down, v_ffn_post_norm):
    P = dict(rel_bias=rel_bias, attn_pre_norm=attn_pre_norm, w_in=w_in, b_gate=b_gate, sinks=sinks, w_br_a=w_br_a,
             w_br_b=w_br_b, w_br_c=w_br_c, w_out=w_out, attn_post_norm=attn_post_norm, ffn_pre_norm=ffn_pre_norm,
             w_up=w_up, conv_w=conv_w, conv_b=conv_b, w_down=w_down, ffn_post_norm=ffn_post_norm)
    M = dict(rel_bias=m_rel_bias, attn_pre_norm=m_attn_pre_norm, w_in=m_w_in, b_gate=m_b_gate, sinks=m_sinks,
             w_br_a=m_w_br_a, w_br_b=m_w_br_b, w_br_c=m_w_br_c, w_out=m_w_out, attn_post_norm=m_attn_post_norm,
             ffn_pre_norm=m_ffn_pre_norm, w_up=m_w_up, conv_w=m_conv_w, conv_b=m_conv_b, w_down=m_w_down,
             ffn_post_norm=m_ffn_post_norm)
    V = dict(rel_bias=v_rel_bias, attn_pre_norm=v_attn_pre_norm, w_in=v_w_in, b_gate=v_b_gate, sinks=v_sinks,
             w_br_a=v_w_br_a, w_br_b=v_w_br_b, w_br_c=v_w_br_c, w_out=v_w_out, attn_post_norm=v_attn_post_norm,
             ffn_pre_norm=v_ffn_pre_norm, w_up=v_w_up, conv_w=v_conv_w, conv_b=v_conv_b, w_down=v_w_down,
             ffn_post_norm=v_ffn_post_norm)
    tr = lambda a: jnp.swapaxes(a, 1, 2)
    PB = {nm: (tr(P[nm]) if nm == "w_in" else P[nm]) for nm, _, _ in BIG}
    MB = {nm: (tr(M[nm]) if nm == "w_in" else M[nm]) for nm, _, _ in BIG}
    VB = {nm: (tr(V[nm]) if nm == "w_in" else V[nm]) for nm, _, _ in BIG}
    xi, yi, ci = _coords()
    me = 4 * xi + 2 * yi + ci

    me_arr = me.astype(jnp.int32).reshape(1)

    small_w = _pack([b_gate.reshape(-1), conv_w.reshape(-1)])
    (small_w_all,) = _exchange([small_w], [jax.ShapeDtypeStruct((NDEV,) + small_w.shape, F32)],
                               _whole, _slot, "gather_small_weights")

    groups = [tuple(l * NBIG + t for t in tids) for l in range(DEPTH) for _, tids in LAYER_GROUPS]
    cast = lambda i, m=me_arr: _cast_own(i, PB[BIG[i % NBIG][0]], m, f"gather_own_l{i // NBIG}_{BIG[i % NBIG][0]}")
    first = list(groups[0])
    rest = [i for grp in groups[1:] for i in grp]
    sems0, _, lands0, tok_first = _xchg_start(None, [cast(i) for i in first], [tuple(range(len(first)))], None,
                                              _shard_window, small_w_all, "gather_start_first", rels=NEAR_RELS, tids=first)
    where_rest = {tid: k for k, tid in enumerate(rest)}
    me_rest = me_arr + tok_first[0, 0:1].astype(jnp.int32)
    sems1, _, lands1, g_tok = _xchg_start(None, [cast(i, me_rest) for i in rest],
                                          [tuple(where_rest[i] for i in grp) for grp in groups[1:]], None,
                                          _shard_window, lands0[0], "gather_start_rest", rels=NEAR_RELS, tids=rest)
    g_sems = list(sems0) + list(sems1)
    tok0 = g_tok[0:1, 0:1]
    lands_now = [None] * (DEPTH * NBIG)
    for i, a in zip(first + rest, list(lands0) + list(lands1)):
        lands_now[i] = a
    fwd_sems = {}
    fwd_plan = {0: (0,), 1: (1,), 2: (2,), 3: (3, 4, 5)}

    def gather_waiter(gi, l, gname, tids):
        def wait(after):
            if gi in fwd_plan:
                gis = fwd_plan[gi]
                flat = [i for g2 in gis for i in groups[g2]]
                where = {tid: k for k, tid in enumerate(flat)}
                fs, new_lands, ftok = _gather_forward(
                    [g_sems[g2] for g2 in gis], [lands_now[i] for i in flat],
                    [[where[i] for i in groups[g2]] for g2 in gis], flat, after, _shard_window, f"gather_forward_{gi}")
                for g2, s in zip(gis, fs):
                    fwd_sems[g2] = s
                for i, a in zip(flat, new_lands):
                    lands_now[i] = a
                after = ftok
            ids = [l * NBIG + t for t in tids]
            _, got = _xchg_wait(fwd_sems[gi], None, [lands_now[i] for i in ids], ids, after,
                                None, _shard_window, f"gather_wait_l{l}_{gname}", rels=FAR_RELS)
            out = {}
            for t, arr in zip(tids, got):
                nm = BIG[t][0]
                out[nm] = arr
            return out
        return wait

    pending = [{gname: gather_waiter(l * len(LAYER_GROUPS) + k, l, gname, tids)
                for k, (gname, tids) in enumerate(LAYER_GROUPS)} for l in range(DEPTH)]
    nbg = DEPTH * 3 * 128
    ncw = DEPTH * 3 * 1024
    flat_all = small_w_all.reshape(NDEV, -1)
    b_gate_full = jnp.transpose(flat_all[:, :nbg].reshape(NDEV, DEPTH, 3, 128), (1, 2, 0, 3)).reshape(DEPTH, 3, D)
    conv_w_full = jnp.transpose(flat_all[:, nbg:nbg + ncw].reshape(NDEV, DEPTH, 3, 1024), (1, 2, 0, 3)).reshape(DEPTH, 3, 2 * D_FF)

    ws = []
    for l in range(DEPTH):
        ws.append(_Weights(dict(
            b_gate=b_gate_full[l], conv_w=conv_w_full[l].reshape(3, 2, D_FF), conv_b=conv_b[l].reshape(2, D_FF),
            sinks=sinks[l].reshape(1, 8),
            attn_pre_norm=attn_pre_norm[l].reshape(1, D), attn_post_norm=attn_post_norm[l].reshape(1, D),
            ffn_pre_norm=ffn_pre_norm[l].reshape(1, D), ffn_post_norm=ffn_post_norm[l].reshape(1, D)), pending[l]))

    rs = {}

    group_tids = dict(LAYER_GROUPS)

    def start_scatter(l, gname, grads_l):
        tids = group_tids[gname]
        blocks, lands_rs = [], []
        for t in tids:
            nm, ax, ext = BIG[t]
            gfull = grads_l[nm].astype(BF16)
            shp = (NDEV, ext, gfull.shape[1]) if ax == 0 else (NDEV, gfull.shape[0], ext)
            blocks.append(gfull)
            lands_rs.append(lax.empty(shp, BF16))
        local = list(range(len(tids)))
        win = lambda j, ref, k: _shard_window(tids[j], ref, k)
        sems, s_thru, l_thru, tok = _xchg_start(blocks, lands_rs, [tuple(local)], win, _slot, me_arr,
                                                f"scatter_start_l{l}_{gname}")
        rs[(l, gname)] = (sems[0], s_thru, l_thru, win, local)
        return tok[0:1, 0:1]

    loss_local, grad_x, grads, g_rel = _local_step(x[0], loss_target[0], ws, rel_bias, tok0, start_scatter)
    loss = lax.psum(loss_local, ("x", "y", "c"))

    stack = lambda nm: jnp.stack([grads[l][nm] for l in range(DEPTH)], axis=0)
    small_names = [nm for nm, _ in SMALL_REPL] + [nm for nm, _, _ in SMALL_SHARD]
    small_g = {"rel_bias": g_rel}
    for nm in small_names[1:]:
        small_g[nm] = stack(nm)
    small_packed = _rowpack(small_g, own=False)

    out_g, out_d, out_m, out_v = {}, {}, {}, {}
    prev = {nm: None for nm, _, _ in BIG}
    todo = [(l, gname) for l in reversed(range(DEPTH)) for gname in ("ffn", "mix", "in")]
    after, small_parts = grad_x, None
    for l, gname in todo:
        if (l, gname) == todo[-1]:
            (small_parts,) = _exchange([small_packed], [jax.ShapeDtypeStruct((NDEV,) + small_packed.shape, F32)],
                                       _whole, _slot, "gather_small_grads", after=after)
            after = small_parts
        sems, s_thru, l_thru, win, local = rs[(l, gname)]
        owns, parts = _xchg_wait(sems, s_thru, l_thru, local, after, win, _slot, f"scatter_wait_l{l}_{gname}")
        for t, own, prt in zip(group_tids[gname], owns, parts):
            nm = BIG[t][0]
            rows = SHARD_ROWS.get(nm, PB[nm].shape[1])
            prev[nm] = _adamw(t, prt, own, me_arr, PB[nm], MB[nm], VB[nm], l, prev[nm], rows, f"adamw_{nm}_l{l}")
            after = prev[nm][1]
    for nm, _, _ in BIG:
        out_g[nm], out_d[nm], out_m[nm], out_v[nm] = [tr(a) if nm == "w_in" else a for a in prev[nm]]
    sm_g, sm_d, sm_m, sm_v = _small_update(small_parts, _rowpack(P, True), _rowpack(M, True), _rowpack(V, True),
                                           me_arr, "small_update")
    for dst, src in ((out_g, sm_g), (out_d, sm_d), (out_m, sm_m), (out_v, sm_v)):
        dst.update(src)

    order = ["rel_bias", "attn_pre_norm", "w_in", "b_gate", "sinks", "w_br_a", "w_br_b", "w_br_c", "w_out",
             "attn_post_norm", "ffn_pre_norm", "w_up", "conv_w", "conv_b", "w_down", "ffn_post_norm"]
    return (loss, grad_x[None], *[out_g[k] for k in order], *[out_d[k] for k in order],
            *[out_m[k] for k in order], *[out_v[k] for k in order])
```

```python
import functools
import math

import numpy as np
import jax
import jax.numpy as jnp
from jax import lax
from jax.experimental import pallas as pl
from jax.experimental.pallas import tpu as pltpu

F32 = jnp.float32
BF16 = jnp.bfloat16

S = 2048
D = 1024
DEPTH = 2
NDEV = 8
HD = 64
BLK = 128
NB = S // BLK
A_GROUPS = ((128, 1), (512, 4), (2048, 16))
NUM_BUCKETS = 32
MAX_DISTANCE = 2048
N_BIAS_HEADS = 20
D_FF = 4096
IN_COLS = 6912
QKV_COLS = 3840
QKV_SLABS = QKV_COLS // 128
GATE_COLS = 3072
EPS = 1e-6
SCALE = HD ** -0.5
NEG = -1e30
LANES = 128

ADAM_LR = 0.001
ADAM_B1 = 0.9
ADAM_B2 = 0.999
ADAM_EPS = 1e-08
ADAM_WD = 0.01
ADAM_STEP = 10

VMEM_LIMIT = 56 * 1024 * 1024
MESH = pl.DeviceIdType.MESH
ANY = pl.BlockSpec(memory_space=pl.ANY)
SMEM = pl.BlockSpec(memory_space=pltpu.SMEM)


def _cp(*sem):
    return pltpu.CompilerParams(dimension_semantics=sem if sem else None, vmem_limit_bytes=VMEM_LIMIT)


def _dot(a, b, ca, cb):
    return lax.dot_general(a, b, (((ca,), (cb,)), ((), ())), preferred_element_type=F32)


def _mm(a, b, *, grid, a_spec, b_spec, out_shape, out_spec, ca, cb, acc_shape, name,
        a_slab=False, b_slab=False, out_slab=False, alias_out=None, after=None):
    nk = grid[2]

    def body(*refs):
        a_ref, b_ref = refs[0], refs[1]
        o_ref, acc_ref = refs[-2], refs[-1]
        k = pl.program_id(2)

        def load(ref, slab):
            if slab:
                return jnp.concatenate([ref[s] for s in range(ref.shape[0])], axis=1).astype(BF16)
            return ref[...].astype(BF16)

        def write(val):
            if out_slab:
                for s in range(o_ref.shape[0]):
                    o_ref[s] = val[:, s * LANES:(s + 1) * LANES].astype(o_ref.dtype)
            else:
                o_ref[...] = val.astype(o_ref.dtype)

        d = _dot(load(a_ref, a_slab), load(b_ref, b_slab), ca, cb)
        if nk == 1:
            write(d)
        elif direct:
            @pl.when(k == 0)
            def _():
                o_ref[...] = d

            @pl.when(k > 0)
            def _():
                o_ref[...] += d
        else:
            @pl.when(k == 0)
            def _():
                acc_ref[...] = d

            if nk > 2:
                @pl.when((k > 0) & (k < nk - 1))
                def _():
                    acc_ref[...] += d

            @pl.when(k == nk - 1)
            def _():
                write(acc_ref[...] + d)

    direct = (not out_slab) and out_shape.dtype == F32
    if nk == 1 or direct:
        acc_shape = (8, LANES)
    in_specs = [a_spec, b_spec]
    args = [a, b]
    aliases = {}
    if alias_out is not None:
        in_specs.append(ANY)
        args.append(alias_out)
        aliases = {2: 0}
    if after is not None:
        in_specs.append(ANY)
        args.append(after)
    return pl.pallas_call(
        body, grid=grid, in_specs=in_specs, out_specs=out_spec, out_shape=out_shape,
        scratch_shapes=[pltpu.VMEM(acc_shape, F32)], input_output_aliases=aliases,
        compiler_params=_cp("parallel", "parallel", "arbitrary"), name=name)(*args)


def _mm_nn(a, b, out_dtype, tm, tn, tk, name):
    m, kk = a.shape
    n = b.shape[1]
    return _mm(a, b, grid=(m // tm, n // tn, kk // tk),
               a_spec=pl.BlockSpec((tm, tk), lambda i, j, k: (i, k)),
               b_spec=pl.BlockSpec((tk, tn), lambda i, j, k: (k, j)),
               out_shape=jax.ShapeDtypeStruct((m, n), out_dtype),
               out_spec=pl.BlockSpec((tm, tn), lambda i, j, k: (i, j)),
               ca=1, cb=0, acc_shape=(tm, tn), name=name)


def _mm_nt(a, b, out_dtype, tm, tn, tk, name):
    m, kk = a.shape
    n = b.shape[0]
    return _mm(a, b, grid=(m // tm, n // tn, kk // tk),
               a_spec=pl.BlockSpec((tm, tk), lambda i, j, k: (i, k)),
               b_spec=pl.BlockSpec((tn, tk), lambda i, j, k: (j, k)),
               out_shape=jax.ShapeDtypeStruct((m, n), out_dtype),
               out_spec=pl.BlockSpec((tm, tn), lambda i, j, k: (i, j)),
               ca=1, cb=1, acc_shape=(tm, tn), name=name)


def _mm_tn(a, b, out_dtype, tm, tn, tk, name):
    kk, m = a.shape
    n = b.shape[1]
    return _mm(a, b, grid=(m // tm, n // tn, kk // tk),
               a_spec=pl.BlockSpec((tk, tm), lambda i, j, k: (k, i)),
               b_spec=pl.BlockSpec((tk, tn), lambda i, j, k: (k, j)),
               out_shape=jax.ShapeDtypeStruct((m, n), out_dtype),
               out_spec=pl.BlockSpec((tm, tn), lambda i, j, k: (i, j)),
               ca=0, cb=0, acc_shape=(tm, tn), name=name)


ROW_TILE = 256


def _rms(x, g):
    r = lax.rsqrt(jnp.mean(x * x, axis=-1, keepdims=True) + EPS)
    return x * r * g


def _prenorm(x, g, name):
    def body(x_ref, g_ref, o_ref):
        o_ref[...] = _rms(x_ref[...], g_ref[...]).astype(BF16)

    return pl.pallas_call(
        body, grid=(S // ROW_TILE,),
        in_specs=[pl.BlockSpec((ROW_TILE, D), lambda i: (i, 0)), pl.BlockSpec((1, D), lambda i: (0, 0))],
        out_specs=pl.BlockSpec((ROW_TILE, D), lambda i: (i, 0)),
        out_shape=jax.ShapeDtypeStruct((S, D), BF16), compiler_params=_cp("parallel"), name=name)(x, g)


def _postnorm_res(x, f, g_post, g_next, name):
    def body(x_ref, f_ref, gp_ref, gn_ref, xo_ref, ho_ref):
        xn = x_ref[...] + _rms(f_ref[...], gp_ref[...])
        xo_ref[...] = xn
        ho_ref[...] = _rms(xn, gn_ref[...]).astype(BF16)

    row = pl.BlockSpec((ROW_TILE, D), lambda i: (i, 0))
    vec = pl.BlockSpec((1, D), lambda i: (0, 0))
    return pl.pallas_call(
        body, grid=(S // ROW_TILE,), in_specs=[row, row, vec, vec], out_specs=[row, row],
        out_shape=[jax.ShapeDtypeStruct((S, D), F32), jax.ShapeDtypeStruct((S, D), BF16)],
        compiler_params=_cp("parallel"), name=name)(x, f, g_post, g_next)


def _norm_bwd(f, g, dys, res, out_dtype, name):
    ndy = len(dys)
    has_res = res is not None

    def body(*refs):
        f_ref, g_ref = refs[0], refs[1]
        dy_refs = refs[2:2 + ndy]
        res_ref = refs[2 + ndy] if has_res else None
        o_ref, dg_ref = refs[-2], refs[-1]
        fv = f_ref[...]
        dy = dy_refs[0][...].astype(F32)
        for r in dy_refs[1:]:
            dy = dy + r[...].astype(F32)
        r = lax.rsqrt(jnp.mean(fv * fv, axis=-1, keepdims=True) + EPS)
        n = fv * r
        dn = dy * g_ref[...]
        df = r * (dn - n * jnp.mean(dn * n, axis=-1, keepdims=True))
        if has_res:
            df = df + res_ref[...]
        o_ref[...] = df.astype(out_dtype)

        @pl.when(pl.program_id(0) == 0)
        def _():
            dg_ref[...] = jnp.zeros((1, D), F32)

        dg_ref[...] += jnp.sum(dy * n, axis=0, keepdims=True)

    row = pl.BlockSpec((ROW_TILE, D), lambda i: (i, 0))
    vec = pl.BlockSpec((1, D), lambda i: (0, 0))
    in_specs = [row, vec] + [row] * ndy + ([row] if has_res else [])
    args = [f, g] + list(dys) + ([res] if has_res else [])
    return pl.pallas_call(
        body, grid=(S // ROW_TILE,), in_specs=in_specs, out_specs=[row, vec],
        out_shape=[jax.ShapeDtypeStruct((S, D), out_dtype), jax.ShapeDtypeStruct((1, D), F32)],
        compiler_params=_cp("arbitrary"), name=name)(*args)


def _loss_head(y, target, name):
    def body(y_ref, t_ref, dy_ref, l_ref):
        e = y_ref[...] - t_ref[...]
        dy_ref[...] = e * (1.0 / D)

        @pl.when(pl.program_id(0) == 0)
        def _():
            l_ref[...] = jnp.zeros((8, LANES), F32)

        l_ref[...] += jnp.sum(e * e) * (0.5 / D)

    row = pl.BlockSpec((ROW_TILE, D), lambda i: (i, 0))
    return pl.pallas_call(
        body, grid=(S // ROW_TILE,), in_specs=[row, row],
        out_specs=[row, pl.BlockSpec((8, LANES), lambda i: (0, 0))],
        out_shape=[jax.ShapeDtypeStruct((S, D), F32), jax.ShapeDtypeStruct((8, LANES), F32)],
        compiler_params=_cp("arbitrary"), name=name)(y, target)


def _bucket_tiles():
    a = np.arange(BLK)[:, None]
    b = np.arange(2 * BLK)[None, :]
    dist = a + BLK - b
    out = np.zeros((4, 2, BLK, 2 * BLK), np.int32)
    cfg = [(w // d, d) for w, d in A_GROUPS] + [(BLK - 1, 1)]
    for gi, (max_dist, d) in enumerate(cfg):
        band = (dist >= 0) & (dist <= max_dist)
        tok = np.maximum(dist, 0) * d
        nf = np.maximum(tok, 1).astype(np.float32)
        max_exact = NUM_BUCKETS // 2
        large = max_exact + (np.log(nf / np.float32(max_exact)) / np.float32(math.log(MAX_DISTANCE / max_exact))
                             * np.float32(NUM_BUCKETS - max_exact)).astype(np.int32)
        large = np.minimum(large, NUM_BUCKETS - 1)
        bkt = np.where(tok < max_exact, tok, large).astype(np.int32)
        full = np.where(band, bkt, -1)
        out[gi, 1] = full
        out[gi, 0] = np.where(b >= BLK, full, -1)
    return out


def _bias_tiles(rel_bias, buckets, name):
    def body(tab_ref, bkt_ref, o_ref):
        h = pl.program_id(0)
        bkt = bkt_ref[...]
        acc = jnp.zeros(bkt.shape, F32)
        for bb in range(NUM_BUCKETS):
            acc = jnp.where(bkt == bb, tab_ref[bb, h], acc)
        o_ref[...] = jnp.where(bkt < 0, NEG, acc)

    return pl.pallas_call(
        body, grid=(N_BIAS_HEADS,),
        in_specs=[SMEM, pl.BlockSpec((None, 2, BLK, 2 * BLK), lambda h: (jnp.minimum(h // 4, 3), 0, 0, 0))],
        out_specs=pl.BlockSpec((None, 2, BLK, 2 * BLK), lambda h: (h, 0, 0, 0)),
        out_shape=jax.ShapeDtypeStruct((N_BIAS_HEADS, 2, BLK, 2 * BLK), F32),
        compiler_params=_cp("arbitrary"), name=name)(rel_bias, buckets)


def _bias_grad(gs, buckets, name):
    ng = len(gs)

    def body(*refs):
        g_refs = refs[:ng]
        bkt_ref, o_ref = refs[ng], refs[ng + 1]
        h = pl.program_id(0)
        g = g_refs[0][...]
        for r in g_refs[1:]:
            g = g + r[...]
        bkt = bkt_ref[...]
        row = lax.broadcasted_iota(jnp.int32, (NUM_BUCKETS, LANES), 0)
        lane = lax.broadcasted_iota(jnp.int32, (NUM_BUCKETS, LANES), 1)

        @pl.when(h == 0)
        def _():
            o_ref[...] = jnp.zeros((NUM_BUCKETS, LANES), F32)

        acc = o_ref[...]
        for bb in range(NUM_BUCKETS):
            s = jnp.sum(jnp.where(bkt == bb, g, 0.0))
            acc = jnp.where((row == bb) & (lane == h), s, acc)
        o_ref[...] = acc

    g_spec = pl.BlockSpec((None, BLK, 2 * BLK), lambda h: (h, 0, 0))
    return pl.pallas_call(
        body, grid=(N_BIAS_HEADS,),
        in_specs=[g_spec] * ng + [pl.BlockSpec((None, None, BLK, 2 * BLK), lambda h: (jnp.minimum(h // 4, 3), 1, 0, 0))],
        out_specs=pl.BlockSpec((NUM_BUCKETS, LANES), lambda h: (0, 0)),
        out_shape=jax.ShapeDtypeStruct((NUM_BUCKETS, LANES), F32),
        compiler_params=_cp("arbitrary"), name=name)(*gs, buckets)


def _to_class_major(src_ref, dst_refs, d, fn=None):
    ln = S // d
    for r in range(d):
        v = src_ref[pl.ds(r, ln, stride=d), :] if d > 1 else src_ref[...]
        outs = fn(v) if fn is not None else (v,) * len(dst_refs)
        for dst, o in zip(dst_refs, outs):
            dst[pl.ds(r * ln, ln), :] = o.astype(dst.dtype)


def _head_masks(rows):
    lane = lax.broadcasted_iota(jnp.int32, (rows, LANES), 1)
    return lane < HD, lane >= HD


def _split_heads(v):
    m0, m1 = _head_masks(v.shape[0])
    return jnp.where(m0, v, 0.0), jnp.where(m1, v, 0.0)


def _dup_head(v, hi):
    m0, _ = _head_masks(v.shape[0])
    r = pltpu.roll(v, HD, 1)
    return jnp.where(m0, jnp.where(hi, r, v), jnp.where(hi, v, r))


def _block_rows(b, d):
    nbc = NB // d
    i = b % nbc
    r = b // nbc
    has_prev = (i > 0).astype(jnp.int32)
    prev = pl.multiple_of(jnp.maximum(b - 1, 0) * BLK, BLK)
    nat = i * (BLK * d) + r
    return has_prev, prev, nat


def _lane_halves(v0, v1):
    lane = lax.broadcasted_iota(jnp.int32, (v0.shape[0], LANES), 1)
    return jnp.where(lane < HD, v0, v1)


def _band_fwd(proj, bias, sinks, *, d, q0, k0, v0, npairs, bias0, shared_kv, name):
    def body(sink_ref, q_ref, k_ref, v_ref, b_ref, num_ref, st_ref, qz0, qz1, ks, vs):
        p = pl.program_id(0)
        kv = (lambda v: (_dup_head(v, p >= 2),)) if shared_kv else None
        _to_class_major(q_ref, (qz0, qz1), d, lambda v: _split_heads(v * SCALE))
        _to_class_major(k_ref, (ks,), d, kv)
        _to_class_major(v_ref, (vs,), d, kv)
        lane = lax.broadcasted_iota(jnp.int32, (BLK, LANES), 1)

        def blk(b, carry):
            has_prev, prev, nat = _block_rows(b, d)
            cur = pl.multiple_of(b * BLK, BLK)
            k2 = jnp.concatenate([ks[pl.ds(prev, BLK), :], ks[pl.ds(cur, BLK), :]], axis=0)
            v2 = jnp.concatenate([vs[pl.ds(prev, BLK), :], vs[pl.ds(cur, BLK), :]], axis=0)
            nums, ms, ls = [], [], []
            for hh, qz in enumerate((qz0, qz1)):
                z = _dot(qz[pl.ds(cur, BLK), :], k2, 1, 1) + b_ref[hh, has_prev]
                m = jnp.max(z, axis=1, keepdims=True)
                e = jnp.exp(z - m)
                l = jnp.sum(e, axis=1, keepdims=True)
                num = _dot(e.astype(BF16), v2, 1, 0)
                if shared_kv:
                    sink = sink_ref[0, 2 * p + hh]
                    mx = jnp.maximum(m, sink)
                    c = jnp.exp(m - mx)
                    zden = l * c + jnp.exp(sink - mx)
                    num = num * (c / zden)
                    m = mx + jnp.log(zden)
                ls.append(l)
                ms.append(m)
                nums.append(num)
            num_t = jnp.where(lane < HD, nums[0], nums[1])
            if shared_kv:
                st_t = jnp.where(lane < HD, ms[0], ms[1])
            else:
                st_t = jnp.where(lane < 32, ms[0], jnp.where(lane < 64, ls[0], jnp.where(lane < 96, ms[1], ls[1])))
            if d > 1:
                num_ref[pl.ds(nat, BLK, stride=d), :] = num_t
                st_ref[pl.ds(nat, BLK, stride=d), :] = st_t
            else:
                num_ref[pl.ds(cur, BLK), :] = num_t
                st_ref[pl.ds(cur, BLK), :] = st_t
            return carry

        lax.fori_loop(0, NB, blk, 0, unroll=8)

    slab = lambda off, per_pair: pl.BlockSpec((None, S, LANES), (lambda p: (off + p, 0, 0)) if per_pair else (lambda p: (off, 0, 0)))
    out = pl.BlockSpec((None, S, LANES), lambda p: (p, 0, 0))
    return pl.pallas_call(
        body, grid=(npairs,),
        in_specs=[SMEM, slab(q0, True), slab(k0, not shared_kv), slab(v0, not shared_kv),
                  pl.BlockSpec((None, 2, 2, BLK, 2 * BLK), lambda p: (bias0 + p, 0, 0, 0, 0))],
        out_specs=[out, out],
        out_shape=[jax.ShapeDtypeStruct((npairs, S, LANES), F32)] * 2,
        scratch_shapes=[pltpu.VMEM((S, LANES), BF16)] * 4,
        compiler_params=_cp("arbitrary"), name=name)(sinks, proj, proj, proj, bias)


def _combine_a(nums, stats, name):
    rt = 512

    def body(n0, n1, n2, s0, s1, s2, o_ref, l_ref):
        n_refs, s_refs = (n0, n1, n2), (s0, s1, s2)
        outs, lses = [], []
        for hh in range(2):
            ms = [s[:, 64 * hh:64 * hh + 1] for s in s_refs]
            ls = [s[:, 64 * hh + 32:64 * hh + 33] for s in s_refs]
            mx = jnp.maximum(jnp.maximum(ms[0], ms[1]), ms[2])
            cs = [jnp.exp(m - mx) for m in ms]
            z = cs[0] * ls[0] + cs[1] * ls[1] + cs[2] * ls[2]
            acc = cs[0] * n_refs[0][:, hh * HD:(hh + 1) * HD]
            acc = acc + cs[1] * n_refs[1][:, hh * HD:(hh + 1) * HD]
            acc = acc + cs[2] * n_refs[2][:, hh * HD:(hh + 1) * HD]
            outs.append(acc / z)
            lses.append(mx + jnp.log(z))
        o_ref[...] = jnp.concatenate(outs, axis=1)
        l_ref[...] = _lane_halves(lses[0], lses[1])

    spec = pl.BlockSpec((None, rt, LANES), lambda p, i: (p, i, 0))
    return pl.pallas_call(
        body, grid=(2, S // rt), in_specs=[spec] * 6, out_specs=[spec, spec],
        out_shape=[jax.ShapeDtypeStruct((2, S, LANES), F32)] * 2,
        compiler_params=_cp("parallel", "parallel"), name=name)(*nums, *stats)


def _band_bwd(proj, bias, o, do, lse, sinks, *, d, q0, k0, v0, npairs, bias0, shared_kv, name):
    nkv = 1 if shared_kv else npairs

    def body(sink_ref, q_ref, k_ref, v_ref, b_ref, o_ref, do_ref, lse_ref,
             dq_ref, dk_ref, dv_ref, g_ref, ds_ref,
             qz0, qz1, ks, vs, doz0, doz1, ls0, ls1, dls0, dls1, stage, dq_nat, dk_cm, dv_cm, kv_nat, dk_acc, dv_acc):
        p = pl.program_id(0)
        m0, m1 = _head_masks(S)
        prod = do_ref[...] * o_ref[...]
        dl0 = jnp.sum(jnp.where(m0, prod, 0.0), axis=1, keepdims=True)
        dl1 = jnp.sum(jnp.where(m1, prod, 0.0), axis=1, keepdims=True)
        if shared_kv:
            row8 = lax.broadcasted_iota(jnp.int32, (8, LANES), 0)
            lane8 = lax.broadcasted_iota(jnp.int32, (8, LANES), 1)
            t = jnp.zeros((8, LANES), F32)
            lv = lse_ref[...]
            for hh in range(2):
                sink = sink_ref[0, 2 * p + hh]
                ps = jnp.exp(sink - lv[:, 64 * hh:64 * hh + 1])
                dsink = -jnp.sum(ps * (dl0 if hh == 0 else dl1))
                t = jnp.where((row8 == 0) & (lane8 == hh), dsink, t)
            ds_ref[...] = t
        else:
            ds_ref[...] = jnp.zeros((8, LANES), F32)
        kv = (lambda v: (_dup_head(v, p >= 2),)) if shared_kv else None
        _to_class_major(q_ref, (qz0, qz1), d, lambda v: _split_heads(v * SCALE))
        _to_class_major(k_ref, (ks,), d, kv)
        _to_class_major(v_ref, (vs,), d, kv)
        _to_class_major(do_ref, (doz0, doz1), d, _split_heads)
        def spread(v):
            a0, a1 = _head_masks(v.shape[0])
            r = pltpu.roll(v, HD, 1)
            return jnp.where(a0, v, r), jnp.where(a1, v, r)

        _to_class_major(lse_ref, (ls0, ls1), d, spread)
        stage[...] = jnp.where(m0, dl0, dl1)
        _to_class_major(stage, (dls0, dls1), d, spread)

        dk_cm[...] = jnp.zeros((S, LANES), F32)
        dv_cm[...] = jnp.zeros((S, LANES), F32)
        g_ref[...] = jnp.zeros((2, BLK, 2 * BLK), F32)
        lane = lax.broadcasted_iota(jnp.int32, (BLK, LANES), 1)

        def blk(b, carry):
            has_prev, prev, nat = _block_rows(b, d)
            cur = pl.multiple_of(b * BLK, BLK)
            k2 = jnp.concatenate([ks[pl.ds(prev, BLK), :], ks[pl.ds(cur, BLK), :]], axis=0)
            v2 = jnp.concatenate([vs[pl.ds(prev, BLK), :], vs[pl.ds(cur, BLK), :]], axis=0)
            dqs, dks, dvs = [], [], []
            for hh, (qz, doz, lsr, dlr) in enumerate(((qz0, doz0, ls0, dls0), (qz1, doz1, ls1, dls1))):
                qb = qz[pl.ds(cur, BLK), :]
                dob = doz[pl.ds(cur, BLK), :]
                lb = lsr[pl.ds(cur, BLK), :]
                dlb = dlr[pl.ds(cur, BLK), :]
                z = _dot(qb, k2, 1, 1) + b_ref[hh, has_prev]
                pr = jnp.exp(z - jnp.concatenate([lb, lb], axis=1))
                dp = _dot(dob, v2, 1, 1)
                dz = pr * (dp - jnp.concatenate([dlb, dlb], axis=1))
                g_ref[hh] += dz
                dzb = dz.astype(BF16)
                dqs.append(_dot(dzb, k2, 1, 0))
                dks.append(_dot(dzb, qb, 0, 0))
                dvs.append(_dot(pr.astype(BF16), dob, 0, 0))
            dq_t = jnp.where(lane < HD, dqs[0], dqs[1]) * SCALE
            dk_t = dks[0] + dks[1]
            dv_t = dvs[0] + dvs[1]
            dk_cm[pl.ds(prev, BLK), :] += dk_t[:BLK]
            dk_cm[pl.ds(cur, BLK), :] += dk_t[BLK:]
            dv_cm[pl.ds(prev, BLK), :] += dv_t[:BLK]
            dv_cm[pl.ds(cur, BLK), :] += dv_t[BLK:]
            if d > 1:
                dq_nat[pl.ds(nat, BLK, stride=d), :] = dq_t
            else:
                dq_nat[pl.ds(cur, BLK), :] = dq_t
            return carry

        lax.fori_loop(0, NB, blk, 0, unroll=8)
        dq_ref[...] = dq_nat[...].astype(BF16)

        def from_class_major(src, dst_ref):
            if d == 1:
                dst_ref[...] = src[...].astype(BF16)
            else:
                ln = S // d
                for r in range(d):
                    kv_nat[pl.ds(r, ln, stride=d), :] = src[pl.ds(r * ln, ln), :]
                dst_ref[...] = kv_nat[...].astype(BF16)

        if not shared_kv:
            from_class_major(dk_cm, dk_ref)
            from_class_major(dv_cm, dv_ref)
        else:
            @pl.when(p == 0)
            def _():
                dk_acc[...] = jnp.zeros((S, LANES), F32)
                dv_acc[...] = jnp.zeros((S, LANES), F32)

            mine = m1 == (p >= 2)
            for cm, acc in ((dk_cm, dk_acc), (dv_cm, dv_acc)):
                val = cm[...]
                acc[...] += jnp.where(mine, val + pltpu.roll(val, HD, 1), 0.0)

            @pl.when(p == npairs - 1)
            def _():
                from_class_major(dk_acc, dk_ref)
                from_class_major(dv_acc, dv_ref)

    slab = lambda off, per_pair: pl.BlockSpec((None, S, LANES), (lambda p: (off + p, 0, 0)) if per_pair else (lambda p: (off, 0, 0)))
    pair = pl.BlockSpec((None, S, LANES), lambda p: (p, 0, 0))
    kv_out = pair if not shared_kv else pl.BlockSpec((None, S, LANES), lambda p: (0, 0, 0))
    return pl.pallas_call(
        body, grid=(npairs,),
        in_specs=[SMEM, slab(q0, True), slab(k0, not shared_kv), slab(v0, not shared_kv),
                  pl.BlockSpec((None, 2, 2, BLK, 2 * BLK), lambda p: (bias0 + p, 0, 0, 0, 0)),
                  pair, pair, pair],
        out_specs=[pair, kv_out, kv_out,
                   pl.BlockSpec((None, 2, BLK, 2 * BLK), lambda p: (p, 0, 0, 0)),
                   pl.BlockSpec((None, 8, LANES), lambda p: (p, 0, 0))],
        out_shape=[jax.ShapeDtypeStruct((npairs, S, LANES), BF16),
                   jax.ShapeDtypeStruct((nkv, S, LANES), BF16),
                   jax.ShapeDtypeStruct((nkv, S, LANES), BF16),
                   jax.ShapeDtypeStruct((npairs, 2, BLK, 2 * BLK), F32),
                   jax.ShapeDtypeStruct((npairs, 8, LANES), F32)],
        scratch_shapes=[pltpu.VMEM((S, LANES), BF16)] * 6 + [pltpu.VMEM((S, LANES), F32)] * 11,
        compiler_params=_cp("arbitrary"), name=name)(sinks, proj, proj, proj, bias, o, do, lse)


KC = 512
NSUB = KC // BLK
QB = 512
QPG = KC // QB


def _split2(x):
    hi = x.astype(BF16)
    lo = (x - hi.astype(F32)).astype(BF16)
    return hi, lo


def _tri_ones(cmp):
    jj = lax.broadcasted_iota(jnp.int32, (2 * BLK, BLK), 0) % BLK
    ss = lax.broadcasted_iota(jnp.int32, (2 * BLK, BLK), 1)
    return jnp.concatenate([cmp(jj, ss).astype(BF16), jnp.ones((2 * BLK, BLK), BF16)], axis=1)


def _sub_sums(x, tri1):
    n = x.shape[0]
    st = jnp.concatenate([x[:, s * BLK:(s + 1) * BLK] for s in range(NSUB)], axis=0)
    hi, lo = _split2(st)
    r = _dot(jnp.concatenate([hi, lo], axis=1), tri1, 1, 0)
    return ([r[s * n:(s + 1) * n, :BLK] for s in range(NSUB)], [r[s * n:(s + 1) * n, BLK:] for s in range(NSUB)])


def _log_sig_pair(z):
    lb = jnp.minimum(z, 0.0) - jnp.log1p(jnp.exp(-jnp.abs(z)))
    return lb, lb - z


QGROUPS = NB // NSUB


def _stick_fwd(proj, *, q0, k0, v0, name):
    def body(q_ref, k_ref, v_ref, o_ref, t_ref, qs, ks, vs):
        qs[...] = (q_ref[...] * SCALE).astype(BF16)
        ks[...] = k_ref[...].astype(BF16)
        vs[...] = v_ref[...].astype(BF16)
        tri1 = _tri_ones(lambda j, s: j > s)
        col = lax.broadcasted_iota(jnp.int32, (QB, KC), 1)
        rowi = lax.broadcasted_iota(jnp.int32, (QB, KC), 0)

        for qg in range(QGROUPS):
            def qblock(ii, carry0, qg=qg):
                t0 = pl.multiple_of((qg * QPG + ii) * QB, QB)
                qb = qs[pl.ds(t0, QB), :]
                accs = [jnp.zeros((QB, HD), F32)] * 2
                runs = [jnp.zeros((QB, BLK), F32)] * 2
                for c in reversed(range(qg + 1)):
                    s0 = c * KC
                    diag = c == qg
                    before = (s0 + col) < (t0 + rowi) if diag else None
                    for hh in range(2):
                        kh = ks[s0:s0 + KC, hh * HD:(hh + 1) * HD]
                        vh = vs[s0:s0 + KC, hh * HD:(hh + 1) * HD]
                        lb, lk = _log_sig_pair(_dot(qb[:, hh * HD:(hh + 1) * HD], kh, 1, 1))
                        if diag:
                            lk = jnp.where(before, lk, 0.0)
                        suf, tot = _sub_sums(lk, tri1)
                        ws, run = [], runs[hh]
                        for s in reversed(range(NSUB)):
                            ws.append(jnp.exp(lb[:, s * BLK:(s + 1) * BLK] + suf[s] + run))
                            run = run + tot[s]
                        w = jnp.concatenate(ws[::-1], axis=1)
                        if diag:
                            w = jnp.where(before, w, 0.0)
                        accs[hh] = accs[hh] + _dot(w.astype(BF16), vh, 1, 0)
                        runs[hh] = run
                o_ref[pl.ds(t0, QB), :] = jnp.concatenate(accs, axis=1)
                t_ref[pl.ds(t0, QB), :] = _lane_halves(runs[0], runs[1])
                return carry0

            lax.fori_loop(0, QPG, qblock, 0)

    slab = lambda off: pl.BlockSpec((None, S, LANES), lambda p: (off + p, 0, 0))
    out = pl.BlockSpec((None, S, LANES), lambda p: (p, 0, 0))
    return pl.pallas_call(
        body, grid=(2,), in_specs=[slab(q0), slab(k0), slab(v0)], out_specs=[out, out],
        out_shape=[jax.ShapeDtypeStruct((2, S, LANES), F32)] * 2,
        scratch_shapes=[pltpu.VMEM((S, LANES), BF16)] * 3,
        compiler_params=_cp("arbitrary"), name=name)(proj, proj, proj)


def _stick_bwd(proj, do, tot, *, q0, k0, v0, name):
    def body(q_ref, k_ref, v_ref, do_ref, t_ref, dq_ref, dk_ref, dv_ref, qs, ks, vs, dos, dk_acc, dv_acc):
        qs[...] = (q_ref[...] * SCALE).astype(BF16)
        ks[...] = k_ref[...].astype(BF16)
        vs[...] = v_ref[...].astype(BF16)
        dos[...] = do_ref[...].astype(BF16)
        dk_acc[...] = jnp.zeros((2, S, HD), F32)
        dv_acc[...] = jnp.zeros((2, S, HD), F32)
        tri_inc = _tri_ones(lambda j, s: j <= s)
        tri_exc = _tri_ones(lambda j, s: j < s)
        col = lax.broadcasted_iota(jnp.int32, (QB, KC), 1)
        rowi = lax.broadcasted_iota(jnp.int32, (QB, KC), 0)

        for qg in range(QGROUPS):
            def qblock(ii, carry0, qg=qg):
                t0 = pl.multiple_of((qg * QPG + ii) * QB, QB)
                qb = qs[pl.ds(t0, QB), :]
                dob = dos[pl.ds(t0, QB), :]
                tb = t_ref[pl.ds(t0, QB), :]
                dqs = [jnp.zeros((QB, HD), F32)] * 2
                pruns = [jnp.zeros((QB, BLK), F32)] * 2
                eruns = [jnp.zeros((QB, BLK), F32)] * 2
                for c in range(qg + 1):
                    s0 = c * KC
                    diag = c == qg
                    before = (s0 + col) < (t0 + rowi) if diag else None
                    for hh in range(2):
                        qh = qb[:, hh * HD:(hh + 1) * HD]
                        doh = dob[:, hh * HD:(hh + 1) * HD]
                        tt = tb[:, 64 * hh:64 * hh + 1]
                        kh = ks[s0:s0 + KC, hh * HD:(hh + 1) * HD]
                        vh = vs[s0:s0 + KC, hh * HD:(hh + 1) * HD]
                        lb, lk = _log_sig_pair(_dot(qh, kh, 1, 1))
                        if diag:
                            lk = jnp.where(before, lk, 0.0)
                        pin, ptot = _sub_sums(lk, tri_inc)
                        ws, prun = [], pruns[hh]
                        for s in range(NSUB):
                            ws.append(jnp.exp(lb[:, s * BLK:(s + 1) * BLK] + (tt - (pin[s] + prun))))
                            prun = prun + ptot[s]
                        w = jnp.concatenate(ws, axis=1)
                        if diag:
                            w = jnp.where(before, w, 0.0)
                        e = w * _dot(doh, vh, 1, 1)
                        pex, etot = _sub_sums(e, tri_exc)
                        cs, erun = [], eruns[hh]
                        for s in range(NSUB):
                            cs.append(pex[s] + erun)
                            erun = erun + etot[s]
                        sig = jnp.exp(lb)
                        dz = e * (1.0 - sig) - jnp.concatenate(cs, axis=1) * sig
                        if diag:
                            dz = jnp.where(before, dz, 0.0)
                        dz = dz.astype(BF16)
                        dqs[hh] = dqs[hh] + _dot(dz, kh, 1, 0)
                        dk_acc[hh, s0:s0 + KC, :] += _dot(dz, qh, 0, 0)
                        dv_acc[hh, s0:s0 + KC, :] += _dot(w.astype(BF16), doh, 0, 0)
                        pruns[hh], eruns[hh] = prun, erun
                dq_ref[pl.ds(t0, QB), :] = (jnp.concatenate(dqs, axis=1) * SCALE).astype(BF16)
                return carry0

            lax.fori_loop(0, QPG, qblock, 0)
        dk_ref[...] = jnp.concatenate([dk_acc[0], dk_acc[1]], axis=1).astype(BF16)
        dv_ref[...] = jnp.concatenate([dv_acc[0], dv_acc[1]], axis=1).astype(BF16)

    slab = lambda off: pl.BlockSpec((None, S, LANES), lambda p: (off + p, 0, 0))
    pair = pl.BlockSpec((None, S, LANES), lambda p: (p, 0, 0))
    return pl.pallas_call(
        body, grid=(2,), in_specs=[slab(q0), slab(k0), slab(v0), pair, pair], out_specs=[pair] * 3,
        out_shape=[jax.ShapeDtypeStruct((2, S, LANES), BF16)] * 3,
        scratch_shapes=[pltpu.VMEM((S, LANES), BF16)] * 4 + [pltpu.VMEM((2, S, HD), F32)] * 2,
        compiler_params=_cp("arbitrary"), name=name)(proj, proj, proj, do, tot)


def _cat_slabs(ref):
    return jnp.concatenate([ref[s] for s in range(ref.shape[0])], axis=1)


def _merge_fwd(o_a, o_b, o_c, gates, b_gate, wa, wb, wc, w_out, name):
    tm = ROW_TILE

    def body(oa_ref, ob_ref, oc_ref, g_ref, bg_ref, wa_ref, wb_ref, wc_ref, wo_ref, mg_ref, mo_ref):
        acc = jnp.zeros((tm, D), F32)
        for i, (o_ref, w_ref) in enumerate(((oa_ref, wa_ref), (ob_ref, wb_ref), (oc_ref, wc_ref))):
            pr = _dot(_cat_slabs(o_ref).astype(BF16), w_ref[...], 1, 0)
            sg = jax.nn.sigmoid(g_ref[:, i * D:(i + 1) * D] + bg_ref[i:i + 1, :])
            acc = acc + sg * pr
        mg = acc.astype(BF16)
        mg_ref[...] = mg
        mo_ref[...] = _dot(mg, wo_ref[...], 1, 0)

    slabs = lambda n: pl.BlockSpec((n, tm, LANES), lambda i: (0, i, 0))
    full = lambda r, c: pl.BlockSpec((r, c), lambda i: (0, 0))
    row = pl.BlockSpec((tm, D), lambda i: (i, 0))
    return pl.pallas_call(
        body, grid=(S // tm,),
        in_specs=[slabs(2), slabs(4), slabs(2), pl.BlockSpec((tm, GATE_COLS), lambda i: (i, 0)), full(3, D),
                  full(256, D), full(512, D), full(256, D), full(D, D)],
        out_specs=[row, row],
        out_shape=[jax.ShapeDtypeStruct((S, D), BF16), jax.ShapeDtypeStruct((S, D), F32)],
        compiler_params=_cp("parallel"), name=name)(o_a, o_b, o_c, gates, b_gate, wa, wb, wc, w_out)


def _merge_bwd(d_mo, o_a, o_b, o_c, gates, b_gate, wa, wb, wc, w_out, name):
    tm = ROW_TILE

    def body(dmo_ref, oa_ref, ob_ref, oc_ref, g_ref, bg_ref, wa_ref, wb_ref, wc_ref, wo_ref,
             doa_ref, dob_ref, doc_ref, dg_ref, dwa_ref, dwb_ref, dwc_ref, dbg_ref):
        @pl.when(pl.program_id(0) == 0)
        def _():
            dwa_ref[...] = jnp.zeros(dwa_ref.shape, F32)
            dwb_ref[...] = jnp.zeros(dwb_ref.shape, F32)
            dwc_ref[...] = jnp.zeros(dwc_ref.shape, F32)
            dbg_ref[...] = jnp.zeros(dbg_ref.shape, F32)

        dmg = _dot(dmo_ref[...], wo_ref[...], 1, 1)
        trip = ((oa_ref, wa_ref, doa_ref, dwa_ref), (ob_ref, wb_ref, dob_ref, dwb_ref), (oc_ref, wc_ref, doc_ref, dwc_ref))
        for i, (o_ref, w_ref, do_ref, dw_ref) in enumerate(trip):
            ob = _cat_slabs(o_ref).astype(BF16)
            pr = _dot(ob, w_ref[...], 1, 0)
            sg = jax.nn.sigmoid(g_ref[:, i * D:(i + 1) * D] + bg_ref[i:i + 1, :])
            dgate = dmg * pr * sg * (1.0 - sg)
            dg_ref[:, i * D:(i + 1) * D] = dgate.astype(BF16)
            dbg_ref[i:i + 1, :] += jnp.sum(dgate, axis=0, keepdims=True)
            dpr = (dmg * sg).astype(BF16)
            do = _dot(dpr, w_ref[...], 1, 1)
            for s in range(do_ref.shape[0]):
                do_ref[s] = do[:, s * LANES:(s + 1) * LANES]
            dw_ref[...] += _dot(ob, dpr, 0, 0)

    slabs = lambda n: pl.BlockSpec((n, tm, LANES), lambda i: (0, i, 0))
    full = lambda r, c: pl.BlockSpec((r, c), lambda i: (0, 0))
    row = pl.BlockSpec((tm, D), lambda i: (i, 0))
    return pl.pallas_call(
        body, grid=(S // tm,),
        in_specs=[row, slabs(2), slabs(4), slabs(2), pl.BlockSpec((tm, GATE_COLS), lambda i: (i, 0)), full(3, D),
                  full(256, D), full(512, D), full(256, D), full(D, D)],
        out_specs=[slabs(2), slabs(4), slabs(2), pl.BlockSpec((tm, GATE_COLS), lambda i: (i, 0)),
                   full(256, D), full(512, D), full(256, D), full(3, D)],
        out_shape=[jax.ShapeDtypeStruct((2, S, LANES), F32), jax.ShapeDtypeStruct((4, S, LANES), F32),
                   jax.ShapeDtypeStruct((2, S, LANES), F32), jax.ShapeDtypeStruct((S, GATE_COLS), BF16),
                   jax.ShapeDtypeStruct((256, D), F32), jax.ShapeDtypeStruct((512, D), F32),
                   jax.ShapeDtypeStruct((256, D), F32), jax.ShapeDtypeStruct((3, D), F32)],
        compiler_params=_cp("arbitrary"), name=name)(d_mo, o_a, o_b, o_c, gates, b_gate, wa, wb, wc, w_out)


FC = 256
GELU_K = math.sqrt(2.0 / math.pi)
GELU_C = 0.044715


RC = 64
NRC = S // RC


def _down(tail, cur, n):
    row = lax.broadcasted_iota(jnp.int32, tail.shape, 0)
    rolled = pltpu.roll(cur, n, 0)
    first = jnp.where(row < n, pltpu.roll(tail, n, 0), rolled[0:8])
    return jnp.concatenate([first, rolled[8:]], axis=0)


def _up(cur, head, n):
    row = lax.broadcasted_iota(jnp.int32, head.shape, 0)
    rolled = pltpu.roll(cur, RC - n, 0)
    last = jnp.where(row >= 8 - n, pltpu.roll(head, 8 - n, 0), rolled[RC - 8:])
    return jnp.concatenate([rolled[:RC - 8], last], axis=0)


def _conv_chunk(load, j, w_ref, b_ref, half):
    r0 = pl.multiple_of(j * RC, RC)
    cur = load(r0, RC).astype(F32)
    tail = load(pl.multiple_of(jnp.maximum(r0 - 16, 0), 16), 16).astype(F32)[8:16]
    tail = jnp.where(j > 0, tail, 0.0)
    d1 = _down(tail, cur, 1)
    d2 = _down(tail, cur, 2)
    y = w_ref[0:1, half, :] * d2 + w_ref[1:2, half, :] * d1 + w_ref[2:3, half, :] * cur + b_ref[half:half + 1, :]
    return y, cur, d1, d2


def _chunk(j):
    return pl.ds(pl.multiple_of(j * RC, RC), RC)


def _fold8(x):
    return jnp.sum(x.reshape(RC // 8, 8, x.shape[-1]), axis=0)


def _ffn_act(u, conv_w, conv_b, name):
    def body(u_ref, w_ref, b_ref, a_ref):
        def step(j, carry):
            yg = _conv_chunk(lambda r, n: u_ref[0, pl.ds(r, n), :], j, w_ref, b_ref, 0)[0]
            yv = _conv_chunk(lambda r, n: u_ref[1, pl.ds(r, n), :], j, w_ref, b_ref, 1)[0]
            th = jnp.tanh(GELU_K * (yg + GELU_C * yg * yg * yg))
            a_ref[_chunk(j), :] = (0.5 * yg * (1.0 + th) * yv).astype(BF16)
            return carry

        lax.fori_loop(0, NRC, step, 0)

    return pl.pallas_call(
        body, grid=(D_FF // FC,),
        in_specs=[pl.BlockSpec((2, S, FC), lambda j: (0, 0, j)), pl.BlockSpec((3, 2, FC), lambda j: (0, 0, j)),
                  pl.BlockSpec((2, FC), lambda j: (0, j))],
        out_specs=pl.BlockSpec((S, FC), lambda j: (0, j)),
        out_shape=jax.ShapeDtypeStruct((S, D_FF), BF16),
        compiler_params=_cp("parallel"), name=name)(u, conv_w, conv_b)


def _ffn_act_bwd(u, d_a, conv_w, conv_b, name):
    def body(u_ref, da_ref, w_ref, b_ref, du_ref, dw_ref, db_ref, dy_s):
        def first(j, acc):
            yg, ug, ug1, ug2 = _conv_chunk(lambda r, n: u_ref[0, pl.ds(r, n), :], j, w_ref, b_ref, 0)
            yv, uv, uv1, uv2 = _conv_chunk(lambda r, n: u_ref[1, pl.ds(r, n), :], j, w_ref, b_ref, 1)
            th = jnp.tanh(GELU_K * (yg + GELU_C * yg * yg * yg))
            gelu = 0.5 * yg * (1.0 + th)
            dgelu = 0.5 * (1.0 + th) + 0.5 * yg * (1.0 - th * th) * GELU_K * (1.0 + 3.0 * GELU_C * yg * yg)
            da = da_ref[_chunk(j), :].astype(F32)
            dyg = da * yv * dgelu
            dyv = da * gelu
            dy_s[0, _chunk(j), :] = dyg
            dy_s[1, _chunk(j), :] = dyv
            new = (_fold8(dyg * ug2), _fold8(dyg * ug1), _fold8(dyg * ug), _fold8(dyg),
                   _fold8(dyv * uv2), _fold8(dyv * uv1), _fold8(dyv * uv), _fold8(dyv))
            return tuple(a + n for a, n in zip(acc, new))

        acc = lax.fori_loop(0, NRC, first, tuple(jnp.zeros((8, FC), F32) for _ in range(8)))
        for half in range(2):
            for k in range(3):
                dw_ref[k:k + 1, half, :] = jnp.sum(acc[4 * half + k], axis=0, keepdims=True)
            db_ref[half:half + 1, :] = jnp.sum(acc[4 * half + 3], axis=0, keepdims=True)

        def second(j, carry):
            for half in range(2):
                cur = dy_s[half, _chunk(j), :]
                h0 = pl.multiple_of(jnp.minimum((j + 1) * RC, S - 8), 8)
                head = jnp.where(j < NRC - 1, dy_s[half, pl.ds(h0, 8), :], 0.0)
                du = (w_ref[2:3, half, :] * cur + w_ref[1:2, half, :] * _up(cur, head, 1)
                      + w_ref[0:1, half, :] * _up(cur, head, 2))
                du_ref[half, _chunk(j), :] = du.astype(BF16)
            return carry

        lax.fori_loop(0, NRC, second, 0)

    return pl.pallas_call(
        body, grid=(D_FF // FC,),
        in_specs=[pl.BlockSpec((2, S, FC), lambda j: (0, 0, j)), pl.BlockSpec((S, FC), lambda j: (0, j)),
                  pl.BlockSpec((3, 2, FC), lambda j: (0, 0, j)), pl.BlockSpec((2, FC), lambda j: (0, j))],
        out_specs=[pl.BlockSpec((2, S, FC), lambda j: (0, 0, j)), pl.BlockSpec((3, 2, FC), lambda j: (0, 0, j)),
                   pl.BlockSpec((2, FC), lambda j: (0, j))],
        out_shape=[jax.ShapeDtypeStruct((2, S, D_FF), BF16), jax.ShapeDtypeStruct((3, 2, D_FF), F32),
                   jax.ShapeDtypeStruct((2, D_FF), F32)],
        scratch_shapes=[pltpu.VMEM((2, S, FC), F32)],
        compiler_params=_cp("parallel"), name=name)(u, d_a, conv_w, conv_b)


def _layer_fwd(x, h1, w, bias, lname):
    n = lambda s: f"{lname}_{s}"
    w.need("in", h1)
    tn = 768
    proj = _mm(h1, w["w_in"], grid=(S // 1024, QKV_COLS // tn, 1),
               a_spec=pl.BlockSpec((1024, D), lambda i, j, k: (i, 0)),
               b_spec=pl.BlockSpec((tn, D), lambda i, j, k: (j, 0)),
               out_shape=jax.ShapeDtypeStruct((QKV_SLABS, S, LANES), F32),
               out_spec=pl.BlockSpec((tn // LANES, 1024, LANES), lambda i, j, k: (j, i, 0)),
               ca=1, cb=1, acc_shape=(1024, tn), out_slab=True, name=n("proj_qkv"))
    gates = _mm(h1, w["w_in"], grid=(S // 1024, GATE_COLS // tn, 1),
                a_spec=pl.BlockSpec((1024, D), lambda i, j, k: (i, 0)),
                b_spec=pl.BlockSpec((tn, D), lambda i, j, k: (j + QKV_COLS // tn, 0)),
                out_shape=jax.ShapeDtypeStruct((S, GATE_COLS), BF16),
                out_spec=pl.BlockSpec((1024, tn), lambda i, j, k: (i, j)),
                ca=1, cb=1, acc_shape=(1024, tn), name=n("proj_gate"))
    nums, stats = [], []
    for g, (_, d) in enumerate(A_GROUPS):
        nm, st = _band_fwd(proj, bias, w["sinks"], d=d, q0=2 * g, k0=6 + 2 * g, v0=12 + 2 * g, npairs=2, bias0=2 * g,
                           shared_kv=False, name=n(f"attn_a{g}_fwd"))
        nums.append(nm)
        stats.append(st)
    o_a, lse_a = _combine_a(nums, stats, n("attn_a_combine"))
    o_b, lse_b = _band_fwd(proj, bias, w["sinks"], d=1, q0=18, k0=22, v0=23, npairs=4, bias0=6, shared_kv=True,
                           name=n("attn_b_fwd"))
    o_c, tot_c = _stick_fwd(proj, q0=24, k0=26, v0=28, name=n("attn_c_fwd"))
    w.need("mix", tot_c)
    merged, mo = _merge_fwd(o_a, o_b, o_c, gates, w["b_gate"], w["w_br_a"], w["w_br_b"], w["w_br_c"], w["w_out"], n("merge_fwd"))
    x2, h2 = _postnorm_res(x, mo, w["attn_post_norm"], w["ffn_pre_norm"], n("attn_post"))
    w.need("ffn", h2)
    u = _mm(h2, w["w_up"], grid=(S // 1024, 2 * D_FF // 1024, 1),
            a_spec=pl.BlockSpec((1024, D), lambda i, j, k: (i, 0)),
            b_spec=pl.BlockSpec((D, 1024), lambda i, j, k: (0, j)),
            out_shape=jax.ShapeDtypeStruct((2, S, D_FF), BF16),
            out_spec=pl.BlockSpec((None, 1024, 1024), lambda i, j, k: (j // 4, i, j % 4)),
            ca=1, cb=0, acc_shape=(1024, 1024), name=n("ffn_up"))
    a = _ffn_act(u, w["conv_w"], w["conv_b"], n("ffn_act"))
    fo = _mm_nn(a, w["w_down"], F32, 1024, 1024, 2048, n("ffn_down"))
    saved = dict(x=x, h1=h1, proj=proj, gates=gates, o_a=o_a, lse_a=lse_a, o_b=o_b, lse_b=lse_b, o_c=o_c, tot_c=tot_c,
                 merged=merged, mo=mo, x2=x2, h2=h2, u=u, a=a, fo=fo)
    return saved


def _layer_bwd(dx3, sv, w, bias, lname, tok=None, on_part=None):
    n = lambda s: f"{lname}_{s}"
    g = {}

    def part(group, vec):
        t = on_part(group, g) if on_part is not None else None
        return vec if t is None else vec + t

    gain = w["ffn_post_norm"] if tok is None else w["ffn_post_norm"] + tok
    d_fo, g["ffn_post_norm"] = _norm_bwd(sv["fo"], gain, [dx3], None, BF16, n("ffn_post_bwd"))
    d_a = _mm_nt(d_fo, w["w_down"], BF16, 1024, 1024, 1024, n("ffn_down_bwd_x"))
    g["w_down"] = _mm_tn(sv["a"], d_fo, BF16, 1024, 1024, S, n("ffn_down_bwd_w"))
    d_u, dcw, dcb = _ffn_act_bwd(sv["u"], d_a, w["conv_w"], w["conv_b"], n("ffn_act_bwd"))
    g["conv_w"] = dcw.reshape(3, 2 * D_FF)
    g["conv_b"] = dcb.reshape(1, 2 * D_FF)
    g["w_up"] = _mm(sv["h2"], d_u, grid=(1, 2 * D_FF // 1024, 1),
                    a_spec=pl.BlockSpec((S, D), lambda i, j, k: (k, 0)),
                    b_spec=pl.BlockSpec((None, S, 1024), lambda i, j, k: (j // 4, k, j % 4)),
                    out_shape=jax.ShapeDtypeStruct((D, 2 * D_FF), BF16),
                    out_spec=pl.BlockSpec((D, 1024), lambda i, j, k: (0, j)),
                    ca=0, cb=0, acc_shape=(D, 1024), name=n("ffn_up_bwd_w"))
    tok_ffn = on_part("ffn", g) if on_part is not None else None
    d_h2 = _mm(d_u, w["w_up"], grid=(S // 1024, 1, 2),
               a_spec=pl.BlockSpec((None, 1024, D_FF), lambda i, j, k: (k, i, 0)),
               b_spec=pl.BlockSpec((D, D_FF), lambda i, j, k: (0, k)),
               out_shape=jax.ShapeDtypeStruct((S, D), F32),
               out_spec=pl.BlockSpec((1024, D), lambda i, j, k: (i, 0)),
               ca=1, cb=1, acc_shape=(1024, D), after=tok_ffn, name=n("ffn_up_bwd_x"))
    dx2, g["ffn_pre_norm"] = _norm_bwd(sv["x2"], w["ffn_pre_norm"], [d_h2], dx3, F32, n("ffn_pre_bwd"))
    d_mo, g["attn_post_norm"] = _norm_bwd(sv["mo"], w["attn_post_norm"], [dx2], None, BF16, n("attn_post_bwd"))
    g["w_out"] = _mm_tn(sv["merged"], d_mo, BF16, 1024, 1024, S, n("out_bwd_w"))
    do_a, do_b, do_c, d_gates, dwa, dwb, dwc, g["b_gate"] = _merge_bwd(
        d_mo, sv["o_a"], sv["o_b"], sv["o_c"], sv["gates"], w["b_gate"], w["w_br_a"], w["w_br_b"], w["w_br_c"],
        w["w_out"], n("merge_bwd"))
    g["w_br_a"], g["w_br_b"], g["w_br_c"] = dwa, dwb, dwc
    sinks = part("mix", w["sinks"])
    proj = sv["proj"]
    dqa, dka, dva, gbias = [], [], [], []
    for gi, (_, d) in enumerate(A_GROUPS):
        dq, dk, dv, gg, _ = _band_bwd(proj, bias, sv["o_a"], do_a, sv["lse_a"], sinks, d=d, q0=2 * gi, k0=6 + 2 * gi,
                                      v0=12 + 2 * gi, npairs=2, bias0=2 * gi, shared_kv=False, name=n(f"attn_a{gi}_bwd"))
        dqa.append(dq), dka.append(dk), dva.append(dv), gbias.append(gg)
    dqb, dkb, dvb, ggb, dsink = _band_bwd(proj, bias, sv["o_b"], do_b, sv["lse_b"], sinks, d=1, q0=18, k0=22, v0=23,
                                          npairs=4, bias0=6, shared_kv=True, name=n("attn_b_bwd"))
    gbias.append(ggb)
    g["bias_g"] = jnp.concatenate(gbias, axis=0).reshape(N_BIAS_HEADS, BLK, 2 * BLK)
    g["sinks"] = dsink[:, 0, :2].reshape(1, 8)
    dqc, dkc, dvc = _stick_bwd(proj, do_c, sv["tot_c"], q0=24, k0=26, v0=28, name=n("attn_c_bwd"))
    dqkv = jnp.concatenate(dqa + dka + dva + [dqb, dkb, dvb, dqc, dkc, dvc], axis=0)
    ts = 6
    tsx = QKV_SLABS
    dw_in = _mm(dqkv, sv["h1"], grid=(QKV_SLABS // ts, 1, 1),
                a_spec=pl.BlockSpec((ts, S, LANES), lambda i, j, k: (i, k, 0)),
                b_spec=pl.BlockSpec((S, D), lambda i, j, k: (k, 0)),
                out_shape=jax.ShapeDtypeStruct((IN_COLS, D), BF16),
                out_spec=pl.BlockSpec((ts * LANES, D), lambda i, j, k: (i, 0)),
                ca=0, cb=0, acc_shape=(ts * LANES, D), a_slab=True, name=n("in_bwd_w_qkv"))
    g["w_in"] = _mm(d_gates, sv["h1"], grid=(GATE_COLS // 768, 1, 1),
                    a_spec=pl.BlockSpec((S, 768), lambda i, j, k: (k, i)),
                    b_spec=pl.BlockSpec((S, D), lambda i, j, k: (k, 0)),
                    out_shape=jax.ShapeDtypeStruct((IN_COLS, D), BF16),
                    out_spec=pl.BlockSpec((768, D), lambda i, j, k: (i + QKV_COLS // 768, 0)),
                    ca=0, cb=0, acc_shape=(768, D), alias_out=dw_in, name=n("in_bwd_w_gate"))
    tok_in = on_part("in", g) if on_part is not None else None
    d_h1a = _mm(dqkv, w["w_in"], grid=(S // 1024, 1, QKV_SLABS // tsx),
                a_spec=pl.BlockSpec((tsx, 1024, LANES), lambda i, j, k: (k, i, 0)),
                b_spec=pl.BlockSpec((tsx * LANES, D), lambda i, j, k: (k, 0)),
                out_shape=jax.ShapeDtypeStruct((S, D), F32),
                out_spec=pl.BlockSpec((1024, D), lambda i, j, k: (i, 0)),
                ca=1, cb=0, acc_shape=(1024, D), a_slab=True, after=tok_in, name=n("in_bwd_x_qkv"))
    d_h1b = _mm(d_gates, w["w_in"], grid=(S // 1024, 1, GATE_COLS // 768),
                a_spec=pl.BlockSpec((1024, 768), lambda i, j, k: (i, k)),
                b_spec=pl.BlockSpec((768, D), lambda i, j, k: (k + QKV_COLS // 768, 0)),
                out_shape=jax.ShapeDtypeStruct((S, D), F32),
                out_spec=pl.BlockSpec((1024, D), lambda i, j, k: (i, 0)),
                ca=1, cb=0, acc_shape=(1024, D), after=tok_in, name=n("in_bwd_x_gate"))
    dx, g["attn_pre_norm"] = _norm_bwd(sv["x"], w["attn_pre_norm"], [d_h1a, d_h1b], dx2, F32, n("attn_pre_bwd"))
    return dx, g, tok_in


def _local_step(x, target, ws, rel_bias, tok=None, on_grads=None):
    buckets = jnp.asarray(_bucket_tiles())
    bias = _bias_tiles(rel_bias, buckets, "bias_tiles").reshape(N_BIAS_HEADS // 2, 2, 2, BLK, 2 * BLK)
    saved = []
    gain0 = ws[0]["attn_pre_norm"] if tok is None else ws[0]["attn_pre_norm"] + tok
    h1 = _prenorm(x, gain0, "l0_attn_pre")
    for l in range(DEPTH):
        sv = _layer_fwd(x, h1, ws[l], bias, f"l{l}")
        saved.append(sv)
        g_next = ws[l + 1]["attn_pre_norm"] if l + 1 < DEPTH else ws[l]["attn_pre_norm"]
        x, h1 = _postnorm_res(sv["x2"], sv["fo"], ws[l]["ffn_post_norm"], g_next, f"l{l}_ffn_post")
    dy, loss_tile = _loss_head(x, target, "loss_head")
    grads = [None] * DEPTH
    tok = None
    for l in reversed(range(DEPTH)):
        on_part = None if on_grads is None else functools.partial(on_grads, l)
        dy, grads[l], tok = _layer_bwd(dy, saved[l], ws[l], bias, f"l{l}", tok, on_part)
    g_rel = _bias_grad([grads[l]["bias_g"] for l in range(DEPTH)], buckets, "bias_grad")[:, :N_BIAS_HEADS]
    return loss_tile[0, 0], dy, grads, g_rel


def _coords():
    return lax.axis_index("x"), lax.axis_index("y"), lax.axis_index("c")


def _peer(rel):
    x, y, c = _coords()
    return (1 - x if rel & 4 else x, 1 - y if rel & 2 else y, 1 - c if rel & 1 else c)


def _exchange(srcs, dst_shapes, src_win, dst_win, name, after=None):
    nt = len(srcs)
    extra = [] if after is None else [after]

    def body(*refs):
        src_refs, dst_refs = refs[:nt], refs[nt + len(extra):2 * nt + len(extra)]
        send_sems, recv_sems, local_sems = refs[2 * nt + len(extra):]
        x, y, c = _coords()
        me = 4 * x + 2 * y + c
        locals_ = []
        for t in range(nt):
            cp = pltpu.make_async_copy(src_win(t, src_refs[t], me), dst_win(t, dst_refs[t], me), local_sems.at[t])
            cp.start()
            locals_.append(cp)
        sends = []
        for rel in range(1, NDEV):
            px, py, pc = _peer(rel)
            q = 4 * px + 2 * py + pc
            for t in range(nt):
                cp = pltpu.make_async_remote_copy(
                    src_ref=src_win(t, src_refs[t], q), dst_ref=dst_win(t, dst_refs[t], me),
                    send_sem=send_sems.at[rel - 1, t], recv_sem=recv_sems.at[rel - 1, t],
                    device_id=(px, py, pc), device_id_type=MESH)
                cp.start()
                sends.append(cp)
        for rel in range(1, NDEV):
            px, py, pc = _peer(rel)
            q = 4 * px + 2 * py + pc
            for t in range(nt):
                pltpu.make_async_remote_copy(
                    src_ref=src_win(t, src_refs[t], me), dst_ref=dst_win(t, dst_refs[t], q),
                    send_sem=send_sems.at[rel - 1, t], recv_sem=recv_sems.at[rel - 1, t],
                    device_id=(px, py, pc), device_id_type=MESH).wait_recv()
        for cp in sends:
            cp.wait_send()
        for cp in locals_:
            cp.wait()

    return pl.pallas_call(
        body, in_specs=[ANY] * (nt + len(extra)), out_specs=[ANY] * nt, out_shape=dst_shapes,
        scratch_shapes=[pltpu.SemaphoreType.DMA((NDEV - 1, nt)), pltpu.SemaphoreType.DMA((NDEV - 1, nt)),
                        pltpu.SemaphoreType.DMA((nt,))],
        name=name)(*srcs, *extra)


BIG = (("w_in", 0, 864), ("w_br_a", 1, 128), ("w_br_b", 1, 128), ("w_br_c", 1, 128), ("w_out", 0, 128),
       ("w_up", 1, 1024), ("w_down", 0, 512))


NBIG = len(BIG)
BIG_FULL = {"w_in": (IN_COLS, D), "w_br_a": (256, D), "w_br_b": (512, D), "w_br_c": (256, D), "w_out": (D, D),
            "w_up": (D, 2 * D_FF), "w_down": (D_FF, D)}
SHARD_ROWS = {"w_in": 288, "w_up": 256, "w_down": 256}
LAYER_GROUPS = (("in", (0,)), ("mix", (1, 2, 3, 4)), ("ffn", (5, 6)))

HBM_SPEC = pl.BlockSpec(memory_space=pltpu.HBM)
SEM_SPEC = pl.BlockSpec(memory_space=pltpu.SEMAPHORE)


def _hbm(a):
    return pltpu.with_memory_space_constraint(a, pltpu.HBM)


def _shard_window(t, ref, k):
    nm, ax, ext = BIG[t % NBIG]
    off = pl.multiple_of(k * ext, ext)
    if ax == 0:
        return ref.at[pl.ds(off, ext), :]
    return ref.at[:, pl.ds(off, ext)]


def _whole(t, ref, k):
    return ref


def _slot(t, ref, k):
    return ref.at[k]


def _own_block_spec(t, rows, me_of):
    nm, ax, ext = BIG[t % NBIG]
    r, c = BIG_FULL[nm]
    if ax == 0:
        return pl.BlockSpec((rows, c), lambda i, m: (me_of(m) * (ext // rows) + i, 0))
    return pl.BlockSpec((rows, ext), lambda i, m: (i, me_of(m)))


def _cast_own(t, shards, me_arr, name):
    nm, ax, ext = BIG[t % NBIG]
    layer = t // NBIG
    _, nr, nc = shards.shape
    rows = SHARD_ROWS.get(nm, nr)
    shape = BIG_FULL[nm]

    def body(m_ref, s_ref, o_ref):
        o_ref[...] = s_ref[...].astype(BF16)

    return pl.pallas_call(
        body, grid_spec=pltpu.PrefetchScalarGridSpec(
            num_scalar_prefetch=1, grid=(nr // rows,),
            in_specs=[pl.BlockSpec((None, rows, nc), lambda i, m: (layer, i, 0))],
            out_specs=_own_block_spec(t, rows, lambda m: m[0])),
        out_shape=jax.ShapeDtypeStruct(shape, BF16), compiler_params=_cp("arbitrary"), name=name)(me_arr, shards)


ALL_RELS = tuple(range(1, NDEV))
NEAR_RELS = (1, 2, 4, 6)
FAR_RELS = (2, 4, 6)


def _xchg_start(srcs, lands, groups, src_win, dst_win, after, name, rels=ALL_RELS, tids=None):
    ns = 0 if srcs is None else len(srcs)
    nt, ng = len(lands), len(groups)
    ins = ([] if srcs is None else list(srcs)) + list(lands)

    def body(*refs):
        src_refs, land_refs = refs[:ns], refs[ns:ns + nt]
        sems = refs[ns + nt + 1:ns + nt + 1 + 2 * ng]
        token = refs[-1]
        x, y, c = _coords()
        me = 4 * x + 2 * y + c
        for gi, grp in enumerate(groups):
            for j, t in enumerate(grp):
                tid = t if tids is None else tids[t]
                for ri, rel in enumerate(rels):
                    px, py, pc = _peer(rel)
                    q = 4 * px + 2 * py + pc
                    src = dst_win(tid, land_refs[t], me) if srcs is None else src_win(tid, src_refs[t], q)
                    pltpu.make_async_remote_copy(
                        src_ref=src, dst_ref=dst_win(tid, land_refs[t], me),
                        send_sem=sems[2 * gi].at[ri * len(grp) + j],
                        recv_sem=sems[2 * gi + 1].at[ri * len(grp) + j],
                        device_id=(px, py, pc), device_id_type=MESH).start()
        token[...] = jnp.zeros((8, LANES), F32)

    out_shape = []
    for grp in groups:
        out_shape += [pltpu.SemaphoreType.DMA((len(rels) * len(grp),))] * 2
    out_shape += [pltpu.HBM(a.shape, a.dtype) for a in ins]
    out_shape.append(jax.ShapeDtypeStruct((8, LANES), F32))
    outs = pl.pallas_call(
        body, in_specs=[HBM_SPEC] * len(ins) + [ANY],
        out_specs=[SEM_SPEC] * (2 * ng) + [HBM_SPEC] * len(ins) + [pl.BlockSpec(memory_space=pltpu.VMEM)],
        out_shape=out_shape, input_output_aliases={i: 2 * ng + i for i in range(len(ins))},
        compiler_params=pltpu.CompilerParams(has_side_effects=pltpu.SideEffectType.DATAFLOW_SIDE_EFFECTING),
        name=name)(*[_hbm(a) for a in ins], after)
    sems = [(outs[2 * gi], outs[2 * gi + 1]) for gi in range(ng)]
    thru = list(outs[2 * ng:2 * ng + len(ins)])
    return sems, (None if srcs is None else thru[:ns]), thru[ns:], outs[-1]


def _xchg_wait(sems, srcs, lands, tids, after, src_win, dst_win, name, rels=ALL_RELS):
    ns = 0 if srcs is None else len(srcs)
    n = len(lands)
    send_sem, recv_sem = sems
    ins = ([] if srcs is None else list(srcs)) + list(lands)

    def body(*refs):
        src_refs, land_refs = refs[:ns], refs[ns:ns + n]
        ssem, rsem = refs[ns + n], refs[ns + n + 1]
        x, y, c = _coords()
        me = 4 * x + 2 * y + c
        for j, t in enumerate(tids):
            for ri, rel in enumerate(rels):
                px, py, pc = _peer(rel)
                q = 4 * px + 2 * py + pc
                src = dst_win(t, land_refs[j], me) if srcs is None else src_win(t, src_refs[j], q)
                cp = pltpu.make_async_remote_copy(
                    src_ref=src, dst_ref=dst_win(t, land_refs[j], q),
                    send_sem=ssem.at[ri * n + j], recv_sem=rsem.at[ri * n + j],
                    device_id=(px, py, pc), device_id_type=MESH)
                cp.wait_send()
                cp.wait_recv()

    outs = pl.pallas_call(
        body, in_specs=[HBM_SPEC] * len(ins) + [SEM_SPEC, SEM_SPEC, ANY], out_specs=[HBM_SPEC] * len(ins),
        out_shape=[pltpu.HBM(a.shape, a.dtype) for a in ins],
        input_output_aliases={i: i for i in range(len(ins))},
        compiler_params=pltpu.CompilerParams(has_side_effects=pltpu.SideEffectType.DATAFLOW_SIDE_EFFECTING),
        name=name)(*ins, send_sem, recv_sem, after)
    return (None if srcs is None else list(outs[:ns])), list(outs[ns:])


def _gather_forward(sems_in, lands, groups, tids, after, dst_win, name):
    nt, ng = len(lands), len(groups)

    def body(*refs):
        land_refs = refs[:nt]
        in_sems = refs[nt:nt + 2 * ng]
        out_sems = refs[nt + 2 * ng + 1:nt + 4 * ng + 1]
        token = refs[-1]
        x, y, c = _coords()
        me = 4 * x + 2 * y + c
        sib = (x, y, 1 - c)
        for gi, grp in enumerate(groups):
            n = len(grp)
            for j, pos in enumerate(grp):
                t = tids[pos]
                for ri, rel in enumerate(NEAR_RELS):
                    px, py, pc = _peer(rel)
                    q = 4 * px + 2 * py + pc
                    cp = pltpu.make_async_remote_copy(
                        src_ref=dst_win(t, land_refs[pos], me), dst_ref=dst_win(t, land_refs[pos], q),
                        send_sem=in_sems[2 * gi].at[ri * n + j], recv_sem=in_sems[2 * gi + 1].at[ri * n + j],
                        device_id=(px, py, pc), device_id_type=MESH)
                    cp.wait_send()
                    cp.wait_recv()
            for j, pos in enumerate(grp):
                t = tids[pos]
                for fi, rel in enumerate(FAR_RELS):
                    px, py, pc = _peer(rel)
                    q = 4 * px + 2 * py + pc
                    win = dst_win(t, land_refs[pos], q)
                    pltpu.make_async_remote_copy(
                        src_ref=win, dst_ref=win,
                        send_sem=out_sems[2 * gi].at[fi * n + j], recv_sem=out_sems[2 * gi + 1].at[fi * n + j],
                        device_id=sib, device_id_type=MESH).start()
        token[...] = jnp.zeros((8, LANES), F32)

    out_shape = []
    for grp in groups:
        out_shape += [pltpu.SemaphoreType.DMA((len(FAR_RELS) * len(grp),))] * 2
    out_shape += [pltpu.HBM(a.shape, a.dtype) for a in lands]
    out_shape.append(jax.ShapeDtypeStruct((8, LANES), F32))
    flat_sems = [s for pair in sems_in for s in pair]
    outs = pl.pallas_call(
        body, in_specs=[HBM_SPEC] * nt + [SEM_SPEC] * (2 * ng) + [ANY],
        out_specs=[SEM_SPEC] * (2 * ng) + [HBM_SPEC] * nt + [pl.BlockSpec(memory_space=pltpu.VMEM)],
        out_shape=out_shape, input_output_aliases={i: 2 * ng + i for i in range(nt)},
        compiler_params=pltpu.CompilerParams(has_side_effects=pltpu.SideEffectType.DATAFLOW_SIDE_EFFECTING),
        name=name)(*[_hbm(a) for a in lands], *flat_sems, after)
    sems = [(outs[2 * gi], outs[2 * gi + 1]) for gi in range(ng)]
    return sems, list(outs[2 * ng:2 * ng + nt]), outs[-1]


class _Weights:
    def __init__(self, ready, pending=None):
        self.ready = dict(ready)
        self.pending = dict(pending or {})

    def __getitem__(self, k):
        return self.ready[k]

    def need(self, group, after):
        fn = self.pending.pop(group, None)
        if fn is not None:
            self.ready.update(fn(after))


def _adamw_math(w, g, m, v):
    m2 = ADAM_B1 * m + (1.0 - ADAM_B1) * g
    v2 = ADAM_B2 * v + (1.0 - ADAM_B2) * (g * g)
    m_hat = m2 / (1.0 - ADAM_B1 ** ADAM_STEP)
    v_hat = v2 / (1.0 - ADAM_B2 ** ADAM_STEP)
    delta = -ADAM_LR * (m_hat / (jnp.sqrt(v_hat) + ADAM_EPS) + ADAM_WD * w)
    return delta, m2, v2


def _adamw(t, parts, own, me_arr, w, m, v, layer, prev, rows, name):
    nl, nr, nc = w.shape

    def body(me_ref, p_ref, own_ref, w_ref, m_ref, v_ref, *rest):
        g_ref, d_ref, m2_ref, v2_ref = rest[-4:]
        me = me_ref[0]
        g = None
        for k in range(NDEV):
            term = jnp.where(me == k, own_ref[...], p_ref[k]).astype(F32)
            g = term if g is None else g + term
        delta, m2, v2 = _adamw_math(w_ref[...], g, m_ref[...], v_ref[...])
        g_ref[...] = g
        d_ref[...] = delta
        m2_ref[...] = m2
        v2_ref[...] = v2

    blk = pl.BlockSpec((None, rows, nc), lambda i, mm: (layer, i, 0))
    pblk = pl.BlockSpec((NDEV, rows, nc), lambda i, mm: (0, i, 0))
    extra = [] if prev is None else list(prev)
    return pl.pallas_call(
        body, grid_spec=pltpu.PrefetchScalarGridSpec(
            num_scalar_prefetch=1, grid=(nr // rows,),
            in_specs=[pblk, _own_block_spec(t, rows, lambda mm: mm[0]), blk, blk, blk] + [ANY] * len(extra),
            out_specs=[blk] * 4),
        out_shape=[jax.ShapeDtypeStruct(w.shape, F32)] * 4,
        input_output_aliases={6 + k: k for k in range(len(extra))},
        compiler_params=_cp("arbitrary"), name=name)(me_arr, parts, own, w, m, v, *extra)


def _pack(vecs):
    flat = jnp.concatenate([v.reshape(-1).astype(F32) for v in vecs])
    n = flat.shape[0]
    rows = -(-n // (8 * LANES)) * 8
    return jnp.pad(flat, (0, rows * LANES - n)).reshape(rows, LANES)


ROWPACK = (("rel_bias", 32, 32, (NUM_BUCKETS, N_BIAS_HEADS)), ("sinks", 8, 8, (DEPTH, 8)),
           ("attn_pre_norm", 16, 16, (DEPTH, D)), ("attn_post_norm", 16, 16, (DEPTH, D)),
           ("ffn_pre_norm", 16, 16, (DEPTH, D)), ("ffn_post_norm", 16, 16, (DEPTH, D)),
           ("conv_b", 128, 128, (DEPTH, 2 * D_FF)), ("b_gate", 48, 8, (DEPTH, 3, 128)),
           ("conv_w", 384, 48, (DEPTH, 3, 1024)))
ROWS_OWN = sum(r for _, _, r, _ in ROWPACK)
N_REPL = 7
ROWS_REPL = sum(r for _, _, r, _ in ROWPACK[:N_REPL])
ROWS_SHARD = ROWS_OWN - ROWS_REPL


def _as_rows(a, rows):
    a = a.astype(F32)
    if a.shape[-1] < LANES:
        a = jnp.pad(a.reshape(-1, a.shape[-1]), ((0, 0), (0, LANES - a.shape[-1])))
    a = a.reshape(-1, LANES)
    return jnp.pad(a, ((0, rows - a.shape[0]), (0, 0)))


def _rowpack(arrs, entries=ROWPACK):
    return jnp.concatenate([_as_rows(arrs[nm], ro) for nm, _, ro, _ in entries], axis=0)


def _shard_rows(g):
    bg = jnp.transpose(g["b_gate"].astype(F32).reshape(DEPTH * 3, NDEV, LANES), (1, 0, 2))
    bg = jnp.pad(bg, ((0, 0), (0, 8 - DEPTH * 3), (0, 0)))
    cw = jnp.transpose(g["conv_w"].astype(F32).reshape(DEPTH * 3, NDEV, 8, LANES), (1, 0, 2, 3))
    return jnp.concatenate([bg, cw.reshape(NDEV, DEPTH * 3 * 8, LANES)], axis=1)


def _small_update(parts_repl, parts_shard, w, m, v, name):
    nsm = len(ROWPACK)

    def body(pr_ref, ps_ref, w_ref, m_ref, v_ref, *rest):
        outs = rest[:4 * nsm]
        g_s, d_s, m_s, v_s = rest[4 * nsm:]
        gr, gs = pr_ref[0], ps_ref[0]
        for k in range(1, NDEV):
            gr = gr + pr_ref[k]
            gs = gs + ps_ref[k]
        g_s[0:ROWS_REPL, :] = gr
        g_s[ROWS_REPL:ROWS_OWN, :] = gs
        delta, m2, v2 = _adamw_math(w_ref[...], g_s[...], m_ref[...], v_ref[...])
        d_s[...] = delta
        m_s[...] = m2
        v_s[...] = v2
        for kind, src in enumerate((g_s, d_s, m_s, v_s)):
            oo = 0
            for idx, (nm, rf, ro, shp) in enumerate(ROWPACK):
                o_ref = outs[kind * nsm + idx]
                if nm in ("rel_bias", "sinks"):
                    o_ref[...] = src[oo:oo + shp[0], 0:shp[1]]
                elif nm == "b_gate":
                    for l in range(DEPTH):
                        o_ref[l] = src[oo + 3 * l:oo + 3 * l + 3, :]
                elif nm == "conv_w":
                    for l in range(DEPTH):
                        for k in range(8):
                            o_ref[l, :, k * LANES:(k + 1) * LANES] = src[pl.ds(oo + 24 * l + k, 3, stride=8), :]
                else:
                    per = shp[1] // LANES
                    for k in range(per):
                        o_ref[:, k * LANES:(k + 1) * LANES] = src[pl.ds(oo + k, DEPTH, stride=per), :]
                oo += ro

    vm = pl.BlockSpec(memory_space=pltpu.VMEM)
    shapes = [jax.ShapeDtypeStruct(shp, F32) for _ in range(4) for _, _, _, shp in ROWPACK]
    outs = pl.pallas_call(
        body, in_specs=[vm] * 5, out_specs=[vm] * (4 * nsm), out_shape=shapes,
        scratch_shapes=[pltpu.VMEM((ROWS_OWN, LANES), F32)] * 4,
        name=name)(parts_repl, parts_shard, w, m, v)
    names = [nm for nm, _, _, _ in ROWPACK]
    return [dict(zip(names, outs[kind * nsm:(kind + 1) * nsm])) for kind in range(4)]


def kernel(x, rel_bias, attn_pre_norm, w_in, b_gate, sinks, w_br_a, w_br_b, w_br_c, w_out, attn_post_norm, ffn_pre_norm, w_up, conv_w, conv_b, w_down, ffn_post_norm, loss_target, m_rel_bias, m_attn_pre_norm, m_w_in, m_b_gate, m_sinks, m_w_br_a, m_w_br_b, m_w_br_c, m_w_out, m_attn_post_norm, m_ffn_pre_norm, m_w_up, m_conv_w, m_conv_b, m_w_down, m_ffn_post_norm, v_rel_bias, v_attn_pre_norm, v_w_in, v_b_gate, v_sinks, v_w_br_a, v_w_br_b, v_w_br_c, v_w_out, v_attn_post_norm, v_ffn_pre_norm, v_w_up, v_conv_w, v_conv_b, v_w_down, v_ffn_post_norm):
    P = dict(rel_bias=rel_bias, attn_pre_norm=attn_pre_norm, w_in=w_in, b_gate=b_gate, sinks=sinks, w_br_a=w_br_a,
             w_br_b=w_br_b, w_br_c=w_br_c, w_out=w_out, attn_post_norm=attn_post_norm, ffn_pre_norm=ffn_pre_norm,
             w_up=w_up, conv_w=conv_w, conv_b=conv_b, w_down=w_down, ffn_post_norm=ffn_post_norm)
    M = dict(rel_bias=m_rel_bias, attn_pre_norm=m_attn_pre_norm, w_in=m_w_in, b_gate=m_b_gate, sinks=m_sinks,
             w_br_a=m_w_br_a, w_br_b=m_w_br_b, w_br_c=m_w_br_c, w_out=m_w_out, attn_post_norm=m_attn_post_norm,
             ffn_pre_norm=m_ffn_pre_norm, w_up=m_w_up, conv_w=m_conv_w, conv_b=m_conv_b, w_down=m_w_down,
             ffn_post_norm=m_ffn_post_norm)
    V = dict(rel_bias=v_rel_bias, attn_pre_norm=v_attn_pre_norm, w_in=v_w_in, b_gate=v_b_gate, sinks=v_sinks,
             w_br_a=v_w_br_a, w_br_b=v_w_br_b, w_br_c=v_w_br_c, w_out=v_w_out, attn_post_norm=v_attn_post_norm,
             ffn_pre_norm=v_ffn_pre_norm, w_up=v_w_up, conv_w=v_conv_w, conv_b=v_conv_b, w_down=v_w_down,
             ffn_post_norm=v_ffn_post_norm)
    tr = lambda a: jnp.swapaxes(a, 1, 2)
    PB = {nm: (tr(P[nm]) if nm == "w_in" else P[nm]) for nm, _, _ in BIG}
    MB = {nm: (tr(M[nm]) if nm == "w_in" else M[nm]) for nm, _, _ in BIG}
    VB = {nm: (tr(V[nm]) if nm == "w_in" else V[nm]) for nm, _, _ in BIG}
    xi, yi, ci = _coords()
    me = 4 * xi + 2 * yi + ci

    me_arr = me.astype(jnp.int32).reshape(1)

    small_w = _pack([b_gate.reshape(-1), conv_w.reshape(-1)])
    (small_w_all,) = _exchange([small_w], [jax.ShapeDtypeStruct((NDEV,) + small_w.shape, F32)],
                               _whole, _slot, "gather_small_weights")

    groups = [tuple(l * NBIG + t for t in tids) for l in range(DEPTH) for _, tids in LAYER_GROUPS]
    cast = lambda i, m=me_arr: _cast_own(i, PB[BIG[i % NBIG][0]], m, f"gather_own_l{i // NBIG}_{BIG[i % NBIG][0]}")
    first = list(groups[0])
    rest = [i for grp in groups[1:] for i in grp]
    sems0, _, lands0, tok_first = _xchg_start(None, [cast(i) for i in first], [tuple(range(len(first)))], None,
                                              _shard_window, small_w_all, "gather_start_first", rels=NEAR_RELS, tids=first)
    where_rest = {tid: k for k, tid in enumerate(rest)}
    me_rest = me_arr + tok_first[0, 0:1].astype(jnp.int32)
    sems1, _, lands1, g_tok = _xchg_start(None, [cast(i, me_rest) for i in rest],
                                          [tuple(where_rest[i] for i in grp) for grp in groups[1:]], None,
                                          _shard_window, lands0[0], "gather_start_rest", rels=NEAR_RELS, tids=rest)
    g_sems = list(sems0) + list(sems1)
    tok0 = g_tok[0:1, 0:1]
    lands_now = [None] * (DEPTH * NBIG)
    for i, a in zip(first + rest, list(lands0) + list(lands1)):
        lands_now[i] = a
    fwd_sems = {}
    fwd_plan = {0: (0,), 1: (1,), 2: (2,), 3: (3, 4, 5)}

    def gather_waiter(gi, l, gname, tids):
        def wait(after):
            if gi in fwd_plan:
                gis = fwd_plan[gi]
                flat = [i for g2 in gis for i in groups[g2]]
                where = {tid: k for k, tid in enumerate(flat)}
                fs, new_lands, ftok = _gather_forward(
                    [g_sems[g2] for g2 in gis], [lands_now[i] for i in flat],
                    [[where[i] for i in groups[g2]] for g2 in gis], flat, after, _shard_window, f"gather_forward_{gi}")
                for g2, s in zip(gis, fs):
                    fwd_sems[g2] = s
                for i, a in zip(flat, new_lands):
                    lands_now[i] = a
                after = ftok
            ids = [l * NBIG + t for t in tids]
            _, got = _xchg_wait(fwd_sems[gi], None, [lands_now[i] for i in ids], ids, after,
                                None, _shard_window, f"gather_wait_l{l}_{gname}", rels=FAR_RELS)
            out = {}
            for t, arr in zip(tids, got):
                nm = BIG[t][0]
                out[nm] = arr
            return out
        return wait

    pending = [{gname: gather_waiter(l * len(LAYER_GROUPS) + k, l, gname, tids)
                for k, (gname, tids) in enumerate(LAYER_GROUPS)} for l in range(DEPTH)]
    nbg = DEPTH * 3 * 128
    ncw = DEPTH * 3 * 1024
    flat_all = small_w_all.reshape(NDEV, -1)
    b_gate_full = jnp.transpose(flat_all[:, :nbg].reshape(NDEV, DEPTH, 3, 128), (1, 2, 0, 3)).reshape(DEPTH, 3, D)
    conv_w_full = jnp.transpose(flat_all[:, nbg:nbg + ncw].reshape(NDEV, DEPTH, 3, 1024), (1, 2, 0, 3)).reshape(DEPTH, 3, 2 * D_FF)

    ws = []
    for l in range(DEPTH):
        ws.append(_Weights(dict(
            b_gate=b_gate_full[l], conv_w=conv_w_full[l].reshape(3, 2, D_FF), conv_b=conv_b[l].reshape(2, D_FF),
            sinks=sinks[l].reshape(1, 8),
            attn_pre_norm=attn_pre_norm[l].reshape(1, D), attn_post_norm=attn_post_norm[l].reshape(1, D),
            ffn_pre_norm=ffn_pre_norm[l].reshape(1, D), ffn_post_norm=ffn_post_norm[l].reshape(1, D)), pending[l]))

    rs = {}

    group_tids = dict(LAYER_GROUPS)

    def start_scatter(l, gname, grads_l):
        tids = group_tids[gname]
        blocks, lands_rs = [], []
        for t in tids:
            nm, ax, ext = BIG[t]
            gfull = grads_l[nm].astype(BF16)
            shp = (NDEV, ext, gfull.shape[1]) if ax == 0 else (NDEV, gfull.shape[0], ext)
            blocks.append(gfull)
            lands_rs.append(lax.empty(shp, BF16))
        local = list(range(len(tids)))
        win = lambda j, ref, k: _shard_window(tids[j], ref, k)
        sems, s_thru, l_thru, tok = _xchg_start(blocks, lands_rs, [tuple(local)], win, _slot, me_arr,
                                                f"scatter_start_l{l}_{gname}")
        rs[(l, gname)] = (sems[0], s_thru, l_thru, win, local)
        return tok[0:1, 0:1]

    loss_local, grad_x, grads, g_rel = _local_step(x[0], loss_target[0], ws, rel_bias, tok0, start_scatter)
    loss = lax.psum(loss_local, ("x", "y", "c"))

    stack = lambda nm: jnp.stack([grads[l][nm] for l in range(DEPTH)], axis=0)
    small_g = {nm: (g_rel if nm == "rel_bias" else stack(nm)) for nm, _, _, _ in ROWPACK}
    small_repl = _rowpack(small_g, ROWPACK[:N_REPL])
    small_shard = _shard_rows(small_g)

    out_g, out_d, out_m, out_v = {}, {}, {}, {}
    prev = {nm: None for nm, _, _ in BIG}
    todo = [(l, gname) for l in reversed(range(DEPTH)) for gname in ("ffn", "mix", "in")]
    after, small_parts = grad_x, None
    for l, gname in todo:
        if (l, gname) == todo[-1]:
            small_parts = _exchange(
                [small_repl, small_shard],
                [jax.ShapeDtypeStruct((NDEV, ROWS_REPL, LANES), F32), jax.ShapeDtypeStruct((NDEV, ROWS_SHARD, LANES), F32)],
                lambda t, ref, q: ref if t == 0 else ref.at[q], _slot, "exchange_small_grads", after=after)
            after = small_parts[0]
        sems, s_thru, l_thru, win, local = rs[(l, gname)]
        owns, parts = _xchg_wait(sems, s_thru, l_thru, local, after, win, _slot, f"scatter_wait_l{l}_{gname}")
        for t, own, prt in zip(group_tids[gname], owns, parts):
            nm = BIG[t][0]
            rows = SHARD_ROWS.get(nm, PB[nm].shape[1])
            prev[nm] = _adamw(t, prt, own, me_arr, PB[nm], MB[nm], VB[nm], l, prev[nm], rows, f"adamw_{nm}_l{l}")
            after = prev[nm][1]
    for nm, _, _ in BIG:
        out_g[nm], out_d[nm], out_m[nm], out_v[nm] = [tr(a) if nm == "w_in" else a for a in prev[nm]]
    sm_g, sm_d, sm_m, sm_v = _small_update(small_parts[0], small_parts[1], _rowpack(P), _rowpack(M), _rowpack(V),
                                           "small_update")
    for dst, src in ((out_g, sm_g), (out_d, sm_d), (out_m, sm_m), (out_v, sm_v)):
        dst.update(src)

    order = ["rel_bias", "attn_pre_norm", "w_in", "b_gate", "sinks", "w_br_a", "w_br_b", "w_br_c", "w_out",
             "attn_post_norm", "ffn_pre_norm", "w_up", "conv_w", "conv_b", "w_down", "ffn_post_norm"]
    return (loss, grad_x[None], *[out_g[k] for k in order], *[out_d[k] for k in order],
            *[out_m[k] for k in order], *[out_v[k] for k in order])
```

```python
import functools
import math

import numpy as np
import jax
import jax.numpy as jnp
from jax import lax
from jax.experimental import pallas as pl
from jax.experimental.pallas import tpu as pltpu

F32 = jnp.float32
BF16 = jnp.bfloat16

S = 2048
D = 1024
DEPTH = 2
NDEV = 8
HD = 64
BLK = 128
NB = S // BLK
A_GROUPS = ((128, 1), (512, 4), (2048, 16))
NUM_BUCKETS = 32
MAX_DISTANCE = 2048
N_BIAS_HEADS = 20
D_FF = 4096
IN_COLS = 6912
QKV_COLS = 3840
QKV_SLABS = QKV_COLS // 128
GATE_COLS = 3072
EPS = 1e-6
SCALE = HD ** -0.5
NEG = -1e30
LANES = 128

ADAM_LR = 0.001
ADAM_B1 = 0.9
ADAM_B2 = 0.999
ADAM_EPS = 1e-08
ADAM_WD = 0.01
ADAM_STEP = 10

VMEM_LIMIT = 56 * 1024 * 1024
MESH = pl.DeviceIdType.MESH
ANY = pl.BlockSpec(memory_space=pl.ANY)
SMEM = pl.BlockSpec(memory_space=pltpu.SMEM)


def _cp(*sem):
    return pltpu.CompilerParams(dimension_semantics=sem if sem else None, vmem_limit_bytes=VMEM_LIMIT)


def _dot(a, b, ca, cb):
    return lax.dot_general(a, b, (((ca,), (cb,)), ((), ())), preferred_element_type=F32)


def _mm(a, b, *, grid, a_spec, b_spec, out_shape, out_spec, ca, cb, acc_shape, name,
        a_slab=False, b_slab=False, out_slab=False, alias_out=None, after=None):
    nk = grid[2]

    def body(*refs):
        a_ref, b_ref = refs[0], refs[1]
        o_ref, acc_ref = refs[-2], refs[-1]
        k = pl.program_id(2)

        def load(ref, slab):
            if slab:
                return jnp.concatenate([ref[s] for s in range(ref.shape[0])], axis=1).astype(BF16)
            return ref[...].astype(BF16)

        def write(val):
            if out_slab:
                for s in range(o_ref.shape[0]):
                    o_ref[s] = val[:, s * LANES:(s + 1) * LANES].astype(o_ref.dtype)
            else:
                o_ref[...] = val.astype(o_ref.dtype)

        d = _dot(load(a_ref, a_slab), load(b_ref, b_slab), ca, cb)
        if nk == 1:
            write(d)
        elif direct:
            @pl.when(k == 0)
            def _():
                o_ref[...] = d

            @pl.when(k > 0)
            def _():
                o_ref[...] += d
        else:
            @pl.when(k == 0)
            def _():
                acc_ref[...] = d

            if nk > 2:
                @pl.when((k > 0) & (k < nk - 1))
                def _():
                    acc_ref[...] += d

            @pl.when(k == nk - 1)
            def _():
                write(acc_ref[...] + d)

    direct = (not out_slab) and out_shape.dtype == F32
    if nk == 1 or direct:
        acc_shape = (8, LANES)
    in_specs = [a_spec, b_spec]
    args = [a, b]
    aliases = {}
    if alias_out is not None:
        in_specs.append(ANY)
        args.append(alias_out)
        aliases = {2: 0}
    if after is not None:
        in_specs.append(ANY)
        args.append(after)
    return pl.pallas_call(
        body, grid=grid, in_specs=in_specs, out_specs=out_spec, out_shape=out_shape,
        scratch_shapes=[pltpu.VMEM(acc_shape, F32)], input_output_aliases=aliases,
        compiler_params=_cp("parallel", "parallel", "arbitrary"), name=name)(*args)


def _mm_nn(a, b, out_dtype, tm, tn, tk, name):
    m, kk = a.shape
    n = b.shape[1]
    return _mm(a, b, grid=(m // tm, n // tn, kk // tk),
               a_spec=pl.BlockSpec((tm, tk), lambda i, j, k: (i, k)),
               b_spec=pl.BlockSpec((tk, tn), lambda i, j, k: (k, j)),
               out_shape=jax.ShapeDtypeStruct((m, n), out_dtype),
               out_spec=pl.BlockSpec((tm, tn), lambda i, j, k: (i, j)),
               ca=1, cb=0, acc_shape=(tm, tn), name=name)


def _mm_nt(a, b, out_dtype, tm, tn, tk, name):
    m, kk = a.shape
    n = b.shape[0]
    return _mm(a, b, grid=(m // tm, n // tn, kk // tk),
               a_spec=pl.BlockSpec((tm, tk), lambda i, j, k: (i, k)),
               b_spec=pl.BlockSpec((tn, tk), lambda i, j, k: (j, k)),
               out_shape=jax.ShapeDtypeStruct((m, n), out_dtype),
               out_spec=pl.BlockSpec((tm, tn), lambda i, j, k: (i, j)),
               ca=1, cb=1, acc_shape=(tm, tn), name=name)


def _mm_tn(a, b, out_dtype, tm, tn, tk, name):
    kk, m = a.shape
    n = b.shape[1]
    return _mm(a, b, grid=(m // tm, n // tn, kk // tk),
               a_spec=pl.BlockSpec((tk, tm), lambda i, j, k: (k, i)),
               b_spec=pl.BlockSpec((tk, tn), lambda i, j, k: (k, j)),
               out_shape=jax.ShapeDtypeStruct((m, n), out_dtype),
               out_spec=pl.BlockSpec((tm, tn), lambda i, j, k: (i, j)),
               ca=0, cb=0, acc_shape=(tm, tn), name=name)


ROW_TILE = 256


def _rms(x, g):
    r = lax.rsqrt(jnp.mean(x * x, axis=-1, keepdims=True) + EPS)
    return x * r * g


def _prenorm(x, g, name):
    def body(x_ref, g_ref, o_ref):
        o_ref[...] = _rms(x_ref[...], g_ref[...]).astype(BF16)

    return pl.pallas_call(
        body, grid=(S // ROW_TILE,),
        in_specs=[pl.BlockSpec((ROW_TILE, D), lambda i: (i, 0)), pl.BlockSpec((1, D), lambda i: (0, 0))],
        out_specs=pl.BlockSpec((ROW_TILE, D), lambda i: (i, 0)),
        out_shape=jax.ShapeDtypeStruct((S, D), BF16), compiler_params=_cp("parallel"), name=name)(x, g)


def _postnorm_res(x, f, g_post, g_next, name):
    def body(x_ref, f_ref, gp_ref, gn_ref, xo_ref, ho_ref):
        xn = x_ref[...] + _rms(f_ref[...], gp_ref[...])
        xo_ref[...] = xn
        ho_ref[...] = _rms(xn, gn_ref[...]).astype(BF16)

    row = pl.BlockSpec((ROW_TILE, D), lambda i: (i, 0))
    vec = pl.BlockSpec((1, D), lambda i: (0, 0))
    return pl.pallas_call(
        body, grid=(S // ROW_TILE,), in_specs=[row, row, vec, vec], out_specs=[row, row],
        out_shape=[jax.ShapeDtypeStruct((S, D), F32), jax.ShapeDtypeStruct((S, D), BF16)],
        compiler_params=_cp("parallel"), name=name)(x, f, g_post, g_next)


def _norm_bwd(f, g, dys, res, out_dtype, name):
    ndy = len(dys)
    has_res = res is not None

    def body(*refs):
        f_ref, g_ref = refs[0], refs[1]
        dy_refs = refs[2:2 + ndy]
        res_ref = refs[2 + ndy] if has_res else None
        o_ref, dg_ref = refs[-2], refs[-1]
        fv = f_ref[...]
        dy = dy_refs[0][...].astype(F32)
        for r in dy_refs[1:]:
            dy = dy + r[...].astype(F32)
        r = lax.rsqrt(jnp.mean(fv * fv, axis=-1, keepdims=True) + EPS)
        n = fv * r
        dn = dy * g_ref[...]
        df = r * (dn - n * jnp.mean(dn * n, axis=-1, keepdims=True))
        if has_res:
            df = df + res_ref[...]
        o_ref[...] = df.astype(out_dtype)

        @pl.when(pl.program_id(0) == 0)
        def _():
            dg_ref[...] = jnp.zeros((1, D), F32)

        dg_ref[...] += jnp.sum(dy * n, axis=0, keepdims=True)

    row = pl.BlockSpec((ROW_TILE, D), lambda i: (i, 0))
    vec = pl.BlockSpec((1, D), lambda i: (0, 0))
    in_specs = [row, vec] + [row] * ndy + ([row] if has_res else [])
    args = [f, g] + list(dys) + ([res] if has_res else [])
    return pl.pallas_call(
        body, grid=(S // ROW_TILE,), in_specs=in_specs, out_specs=[row, vec],
        out_shape=[jax.ShapeDtypeStruct((S, D), out_dtype), jax.ShapeDtypeStruct((1, D), F32)],
        compiler_params=_cp("arbitrary"), name=name)(*args)


def _loss_head(y, target, name):
    def body(y_ref, t_ref, dy_ref, l_ref):
        e = y_ref[...] - t_ref[...]
        dy_ref[...] = e * (1.0 / D)

        @pl.when(pl.program_id(0) == 0)
        def _():
            l_ref[...] = jnp.zeros((8, LANES), F32)

        l_ref[...] += jnp.sum(e * e) * (0.5 / D)

    row = pl.BlockSpec((ROW_TILE, D), lambda i: (i, 0))
    return pl.pallas_call(
        body, grid=(S // ROW_TILE,), in_specs=[row, row],
        out_specs=[row, pl.BlockSpec((8, LANES), lambda i: (0, 0))],
        out_shape=[jax.ShapeDtypeStruct((S, D), F32), jax.ShapeDtypeStruct((8, LANES), F32)],
        compiler_params=_cp("arbitrary"), name=name)(y, target)


def _bucket_tiles():
    a = np.arange(BLK)[:, None]
    b = np.arange(2 * BLK)[None, :]
    dist = a + BLK - b
    out = np.zeros((4, 2, BLK, 2 * BLK), np.int32)
    cfg = [(w // d, d) for w, d in A_GROUPS] + [(BLK - 1, 1)]
    for gi, (max_dist, d) in enumerate(cfg):
        band = (dist >= 0) & (dist <= max_dist)
        tok = np.maximum(dist, 0) * d
        nf = np.maximum(tok, 1).astype(np.float32)
        max_exact = NUM_BUCKETS // 2
        large = max_exact + (np.log(nf / np.float32(max_exact)) / np.float32(math.log(MAX_DISTANCE / max_exact))
                             * np.float32(NUM_BUCKETS - max_exact)).astype(np.int32)
        large = np.minimum(large, NUM_BUCKETS - 1)
        bkt = np.where(tok < max_exact, tok, large).astype(np.int32)
        full = np.where(band, bkt, -1)
        out[gi, 1] = full
        out[gi, 0] = np.where(b >= BLK, full, -1)
    return out


def _bias_tiles(rel_bias, buckets, name):
    def body(tab_ref, bkt_ref, o_ref):
        h = pl.program_id(0)
        bkt = bkt_ref[...]
        acc = jnp.zeros(bkt.shape, F32)
        for bb in range(NUM_BUCKETS):
            acc = jnp.where(bkt == bb, tab_ref[bb, h], acc)
        o_ref[...] = jnp.where(bkt < 0, NEG, acc)

    return pl.pallas_call(
        body, grid=(N_BIAS_HEADS,),
        in_specs=[SMEM, pl.BlockSpec((None, 2, BLK, 2 * BLK), lambda h: (jnp.minimum(h // 4, 3), 0, 0, 0))],
        out_specs=pl.BlockSpec((None, 2, BLK, 2 * BLK), lambda h: (h, 0, 0, 0)),
        out_shape=jax.ShapeDtypeStruct((N_BIAS_HEADS, 2, BLK, 2 * BLK), F32),
        compiler_params=_cp("arbitrary"), name=name)(rel_bias, buckets)


def _bias_grad(gs, buckets, name):
    ng = len(gs)

    def body(*refs):
        g_refs = refs[:ng]
        bkt_ref, o_ref = refs[ng], refs[ng + 1]
        h = pl.program_id(0)
        g = g_refs[0][...]
        for r in g_refs[1:]:
            g = g + r[...]
        bkt = bkt_ref[...]
        row = lax.broadcasted_iota(jnp.int32, (NUM_BUCKETS, LANES), 0)
        lane = lax.broadcasted_iota(jnp.int32, (NUM_BUCKETS, LANES), 1)

        @pl.when(h == 0)
        def _():
            o_ref[...] = jnp.zeros((NUM_BUCKETS, LANES), F32)

        acc = o_ref[...]
        for bb in range(NUM_BUCKETS):
            s = jnp.sum(jnp.where(bkt == bb, g, 0.0))
            acc = jnp.where((row == bb) & (lane == h), s, acc)
        o_ref[...] = acc

    g_spec = pl.BlockSpec((None, BLK, 2 * BLK), lambda h: (h, 0, 0))
    return pl.pallas_call(
        body, grid=(N_BIAS_HEADS,),
        in_specs=[g_spec] * ng + [pl.BlockSpec((None, None, BLK, 2 * BLK), lambda h: (jnp.minimum(h // 4, 3), 1, 0, 0))],
        out_specs=pl.BlockSpec((NUM_BUCKETS, LANES), lambda h: (0, 0)),
        out_shape=jax.ShapeDtypeStruct((NUM_BUCKETS, LANES), F32),
        compiler_params=_cp("arbitrary"), name=name)(*gs, buckets)


def _to_class_major(src_ref, dst_refs, d, fn=None):
    ln = S // d
    for r in range(d):
        v = src_ref[pl.ds(r, ln, stride=d), :] if d > 1 else src_ref[...]
        outs = fn(v) if fn is not None else (v,) * len(dst_refs)
        for dst, o in zip(dst_refs, outs):
            dst[pl.ds(r * ln, ln), :] = o.astype(dst.dtype)


def _head_masks(rows):
    lane = lax.broadcasted_iota(jnp.int32, (rows, LANES), 1)
    return lane < HD, lane >= HD


def _split_heads(v):
    m0, m1 = _head_masks(v.shape[0])
    return jnp.where(m0, v, 0.0), jnp.where(m1, v, 0.0)


def _dup_head(v, hi):
    m0, _ = _head_masks(v.shape[0])
    r = pltpu.roll(v, HD, 1)
    return jnp.where(m0, jnp.where(hi, r, v), jnp.where(hi, v, r))


def _block_rows(b, d):
    nbc = NB // d
    i = b % nbc
    r = b // nbc
    has_prev = (i > 0).astype(jnp.int32)
    prev = pl.multiple_of(jnp.maximum(b - 1, 0) * BLK, BLK)
    nat = i * (BLK * d) + r
    return has_prev, prev, nat


def _lane_halves(v0, v1):
    lane = lax.broadcasted_iota(jnp.int32, (v0.shape[0], LANES), 1)
    return jnp.where(lane < HD, v0, v1)


def _band_fwd(proj, bias, sinks, *, d, q0, k0, v0, npairs, bias0, shared_kv, name):
    def body(sink_ref, q_ref, k_ref, v_ref, b_ref, num_ref, st_ref, qz0, qz1, ks, vs):
        p = pl.program_id(0)
        kv = (lambda v: (_dup_head(v, p >= 2),)) if shared_kv else None
        _to_class_major(q_ref, (qz0, qz1), d, lambda v: _split_heads(v * SCALE))
        _to_class_major(k_ref, (ks,), d, kv)
        _to_class_major(v_ref, (vs,), d, kv)
        lane = lax.broadcasted_iota(jnp.int32, (BLK, LANES), 1)

        def blk(b, carry):
            has_prev, prev, nat = _block_rows(b, d)
            cur = pl.multiple_of(b * BLK, BLK)
            k2 = jnp.concatenate([ks[pl.ds(prev, BLK), :], ks[pl.ds(cur, BLK), :]], axis=0)
            v2 = jnp.concatenate([vs[pl.ds(prev, BLK), :], vs[pl.ds(cur, BLK), :]], axis=0)
            nums, ms, ls = [], [], []
            for hh, qz in enumerate((qz0, qz1)):
                z = _dot(qz[pl.ds(cur, BLK), :], k2, 1, 1) + b_ref[hh, has_prev]
                m = jnp.max(z, axis=1, keepdims=True)
                e = jnp.exp(z - m)
                l = jnp.sum(e, axis=1, keepdims=True)
                num = _dot(e.astype(BF16), v2, 1, 0)
                if shared_kv:
                    sink = sink_ref[0, 2 * p + hh]
                    mx = jnp.maximum(m, sink)
                    c = jnp.exp(m - mx)
                    zden = l * c + jnp.exp(sink - mx)
                    num = num * (c / zden)
                    m = mx + jnp.log(zden)
                ls.append(l)
                ms.append(m)
                nums.append(num)
            num_t = jnp.where(lane < HD, nums[0], nums[1])
            if shared_kv:
                st_t = jnp.where(lane < HD, ms[0], ms[1])
            else:
                st_t = jnp.where(lane < 32, ms[0], jnp.where(lane < 64, ls[0], jnp.where(lane < 96, ms[1], ls[1])))
            if d > 1:
                num_ref[pl.ds(nat, BLK, stride=d), :] = num_t
                st_ref[pl.ds(nat, BLK, stride=d), :] = st_t
            else:
                num_ref[pl.ds(cur, BLK), :] = num_t
                st_ref[pl.ds(cur, BLK), :] = st_t
            return carry

        lax.fori_loop(0, NB, blk, 0, unroll=8)

    slab = lambda off, per_pair: pl.BlockSpec((None, S, LANES), (lambda p: (off + p, 0, 0)) if per_pair else (lambda p: (off, 0, 0)))
    out = pl.BlockSpec((None, S, LANES), lambda p: (p, 0, 0))
    return pl.pallas_call(
        body, grid=(npairs,),
        in_specs=[SMEM, slab(q0, True), slab(k0, not shared_kv), slab(v0, not shared_kv),
                  pl.BlockSpec((None, 2, 2, BLK, 2 * BLK), lambda p: (bias0 + p, 0, 0, 0, 0))],
        out_specs=[out, out],
        out_shape=[jax.ShapeDtypeStruct((npairs, S, LANES), F32)] * 2,
        scratch_shapes=[pltpu.VMEM((S, LANES), BF16)] * 4,
        compiler_params=_cp("arbitrary"), name=name)(sinks, proj, proj, proj, bias)


def _combine_a(nums, stats, name):
    rt = 512

    def body(n0, n1, n2, s0, s1, s2, o_ref, l_ref):
        n_refs, s_refs = (n0, n1, n2), (s0, s1, s2)
        outs, lses = [], []
        for hh in range(2):
            ms = [s[:, 64 * hh:64 * hh + 1] for s in s_refs]
            ls = [s[:, 64 * hh + 32:64 * hh + 33] for s in s_refs]
            mx = jnp.maximum(jnp.maximum(ms[0], ms[1]), ms[2])
            cs = [jnp.exp(m - mx) for m in ms]
            z = cs[0] * ls[0] + cs[1] * ls[1] + cs[2] * ls[2]
            acc = cs[0] * n_refs[0][:, hh * HD:(hh + 1) * HD]
            acc = acc + cs[1] * n_refs[1][:, hh * HD:(hh + 1) * HD]
            acc = acc + cs[2] * n_refs[2][:, hh * HD:(hh + 1) * HD]
            outs.append(acc / z)
            lses.append(mx + jnp.log(z))
        o_ref[...] = jnp.concatenate(outs, axis=1)
        l_ref[...] = _lane_halves(lses[0], lses[1])

    spec = pl.BlockSpec((None, rt, LANES), lambda p, i: (p, i, 0))
    return pl.pallas_call(
        body, grid=(2, S // rt), in_specs=[spec] * 6, out_specs=[spec, spec],
        out_shape=[jax.ShapeDtypeStruct((2, S, LANES), F32)] * 2,
        compiler_params=_cp("parallel", "parallel"), name=name)(*nums, *stats)


def _band_bwd(proj, bias, o, do, lse, sinks, *, d, q0, k0, v0, npairs, bias0, shared_kv, name):
    nkv = 1 if shared_kv else npairs

    def body(sink_ref, q_ref, k_ref, v_ref, b_ref, o_ref, do_ref, lse_ref,
             dq_ref, dk_ref, dv_ref, g_ref, ds_ref,
             qz0, qz1, ks, vs, doz0, doz1, ls0, ls1, dls0, dls1, stage, dq_nat, dk_cm, dv_cm, kv_nat, dk_acc, dv_acc):
        p = pl.program_id(0)
        m0, m1 = _head_masks(S)
        prod = do_ref[...] * o_ref[...]
        dl0 = jnp.sum(jnp.where(m0, prod, 0.0), axis=1, keepdims=True)
        dl1 = jnp.sum(jnp.where(m1, prod, 0.0), axis=1, keepdims=True)
        if shared_kv:
            row8 = lax.broadcasted_iota(jnp.int32, (8, LANES), 0)
            lane8 = lax.broadcasted_iota(jnp.int32, (8, LANES), 1)
            t = jnp.zeros((8, LANES), F32)
            lv = lse_ref[...]
            for hh in range(2):
                sink = sink_ref[0, 2 * p + hh]
                ps = jnp.exp(sink - lv[:, 64 * hh:64 * hh + 1])
                dsink = -jnp.sum(ps * (dl0 if hh == 0 else dl1))
                t = jnp.where((row8 == 0) & (lane8 == hh), dsink, t)
            ds_ref[...] = t
        else:
            ds_ref[...] = jnp.zeros((8, LANES), F32)
        kv = (lambda v: (_dup_head(v, p >= 2),)) if shared_kv else None
        _to_class_major(q_ref, (qz0, qz1), d, lambda v: _split_heads(v * SCALE))
        _to_class_major(k_ref, (ks,), d, kv)
        _to_class_major(v_ref, (vs,), d, kv)
        _to_class_major(do_ref, (doz0, doz1), d, _split_heads)
        def spread(v):
            a0, a1 = _head_masks(v.shape[0])
            r = pltpu.roll(v, HD, 1)
            return jnp.where(a0, v, r), jnp.where(a1, v, r)

        _to_class_major(lse_ref, (ls0, ls1), d, spread)
        stage[...] = jnp.where(m0, dl0, dl1)
        _to_class_major(stage, (dls0, dls1), d, spread)

        dk_cm[...] = jnp.zeros((S, LANES), F32)
        dv_cm[...] = jnp.zeros((S, LANES), F32)
        g_ref[...] = jnp.zeros((2, BLK, 2 * BLK), F32)
        lane = lax.broadcasted_iota(jnp.int32, (BLK, LANES), 1)

        def blk(b, carry):
            has_prev, prev, nat = _block_rows(b, d)
            cur = pl.multiple_of(b * BLK, BLK)
            k2 = jnp.concatenate([ks[pl.ds(prev, BLK), :], ks[pl.ds(cur, BLK), :]], axis=0)
            v2 = jnp.concatenate([vs[pl.ds(prev, BLK), :], vs[pl.ds(cur, BLK), :]], axis=0)
            dqs, dks, dvs = [], [], []
            for hh, (qz, doz, lsr, dlr) in enumerate(((qz0, doz0, ls0, dls0), (qz1, doz1, ls1, dls1))):
                qb = qz[pl.ds(cur, BLK), :]
                dob = doz[pl.ds(cur, BLK), :]
                lb = lsr[pl.ds(cur, BLK), :]
                dlb = dlr[pl.ds(cur, BLK), :]
                z = _dot(qb, k2, 1, 1) + b_ref[hh, has_prev]
                pr = jnp.exp(z - jnp.concatenate([lb, lb], axis=1))
                dp = _dot(dob, v2, 1, 1)
                dz = pr * (dp - jnp.concatenate([dlb, dlb], axis=1))
                g_ref[hh] += dz
                dzb = dz.astype(BF16)
                dqs.append(_dot(dzb, k2, 1, 0))
                dks.append(_dot(dzb, qb, 0, 0))
                dvs.append(_dot(pr.astype(BF16), dob, 0, 0))
            dq_t = jnp.where(lane < HD, dqs[0], dqs[1]) * SCALE
            dk_t = dks[0] + dks[1]
            dv_t = dvs[0] + dvs[1]
            dk_cm[pl.ds(prev, BLK), :] += dk_t[:BLK]
            dk_cm[pl.ds(cur, BLK), :] += dk_t[BLK:]
            dv_cm[pl.ds(prev, BLK), :] += dv_t[:BLK]
            dv_cm[pl.ds(cur, BLK), :] += dv_t[BLK:]
            if d > 1:
                dq_nat[pl.ds(nat, BLK, stride=d), :] = dq_t
            else:
                dq_nat[pl.ds(cur, BLK), :] = dq_t
            return carry

        lax.fori_loop(0, NB, blk, 0, unroll=8)
        dq_ref[...] = dq_nat[...].astype(BF16)

        def from_class_major(src, dst_ref):
            if d == 1:
                dst_ref[...] = src[...].astype(BF16)
            else:
                ln = S // d
                for r in range(d):
                    kv_nat[pl.ds(r, ln, stride=d), :] = src[pl.ds(r * ln, ln), :]
                dst_ref[...] = kv_nat[...].astype(BF16)

        if not shared_kv:
            from_class_major(dk_cm, dk_ref)
            from_class_major(dv_cm, dv_ref)
        else:
            @pl.when(p == 0)
            def _():
                dk_acc[...] = jnp.zeros((S, LANES), F32)
                dv_acc[...] = jnp.zeros((S, LANES), F32)

            mine = m1 == (p >= 2)
            for cm, acc in ((dk_cm, dk_acc), (dv_cm, dv_acc)):
                val = cm[...]
                acc[...] += jnp.where(mine, val + pltpu.roll(val, HD, 1), 0.0)

            @pl.when(p == npairs - 1)
            def _():
                from_class_major(dk_acc, dk_ref)
                from_class_major(dv_acc, dv_ref)

    slab = lambda off, per_pair: pl.BlockSpec((None, S, LANES), (lambda p: (off + p, 0, 0)) if per_pair else (lambda p: (off, 0, 0)))
    pair = pl.BlockSpec((None, S, LANES), lambda p: (p, 0, 0))
    kv_out = pair if not shared_kv else pl.BlockSpec((None, S, LANES), lambda p: (0, 0, 0))
    return pl.pallas_call(
        body, grid=(npairs,),
        in_specs=[SMEM, slab(q0, True), slab(k0, not shared_kv), slab(v0, not shared_kv),
                  pl.BlockSpec((None, 2, 2, BLK, 2 * BLK), lambda p: (bias0 + p, 0, 0, 0, 0)),
                  pair, pair, pair],
        out_specs=[pair, kv_out, kv_out,
                   pl.BlockSpec((None, 2, BLK, 2 * BLK), lambda p: (p, 0, 0, 0)),
                   pl.BlockSpec((None, 8, LANES), lambda p: (p, 0, 0))],
        out_shape=[jax.ShapeDtypeStruct((npairs, S, LANES), BF16),
                   jax.ShapeDtypeStruct((nkv, S, LANES), BF16),
                   jax.ShapeDtypeStruct((nkv, S, LANES), BF16),
                   jax.ShapeDtypeStruct((npairs, 2, BLK, 2 * BLK), F32),
                   jax.ShapeDtypeStruct((npairs, 8, LANES), F32)],
        scratch_shapes=[pltpu.VMEM((S, LANES), BF16)] * 6 + [pltpu.VMEM((S, LANES), F32)] * 11,
        compiler_params=_cp("arbitrary"), name=name)(sinks, proj, proj, proj, bias, o, do, lse)


KC = 512
NSUB = KC // BLK
QB = 512
QPG = KC // QB


def _split2(x):
    hi = x.astype(BF16)
    lo = (x - hi.astype(F32)).astype(BF16)
    return hi, lo


def _tri_ones(cmp):
    jj = lax.broadcasted_iota(jnp.int32, (2 * BLK, BLK), 0) % BLK
    ss = lax.broadcasted_iota(jnp.int32, (2 * BLK, BLK), 1)
    return jnp.concatenate([cmp(jj, ss).astype(BF16), jnp.ones((2 * BLK, BLK), BF16)], axis=1)


def _sub_sums(x, tri1):
    n = x.shape[0]
    st = jnp.concatenate([x[:, s * BLK:(s + 1) * BLK] for s in range(NSUB)], axis=0)
    hi, lo = _split2(st)
    r = _dot(jnp.concatenate([hi, lo], axis=1), tri1, 1, 0)
    return ([r[s * n:(s + 1) * n, :BLK] for s in range(NSUB)], [r[s * n:(s + 1) * n, BLK:] for s in range(NSUB)])


def _log_sig_pair(z):
    lb = jnp.minimum(z, 0.0) - jnp.log1p(jnp.exp(-jnp.abs(z)))
    return lb, lb - z


QGROUPS = NB // NSUB


def _stick_fwd(proj, *, q0, k0, v0, name):
    def body(q_ref, k_ref, v_ref, o_ref, t_ref, qs, ks, vs):
        qs[...] = (q_ref[...] * SCALE).astype(BF16)
        ks[...] = k_ref[...].astype(BF16)
        vs[...] = v_ref[...].astype(BF16)
        tri1 = _tri_ones(lambda j, s: j > s)
        col = lax.broadcasted_iota(jnp.int32, (QB, KC), 1)
        rowi = lax.broadcasted_iota(jnp.int32, (QB, KC), 0)

        for qg in range(QGROUPS):
            def qblock(ii, carry0, qg=qg):
                t0 = pl.multiple_of((qg * QPG + ii) * QB, QB)
                qb = qs[pl.ds(t0, QB), :]
                accs = [jnp.zeros((QB, HD), F32)] * 2
                runs = [jnp.zeros((QB, BLK), F32)] * 2
                for c in reversed(range(qg + 1)):
                    s0 = c * KC
                    diag = c == qg
                    before = (s0 + col) < (t0 + rowi) if diag else None
                    for hh in range(2):
                        kh = ks[s0:s0 + KC, hh * HD:(hh + 1) * HD]
                        vh = vs[s0:s0 + KC, hh * HD:(hh + 1) * HD]
                        lb, lk = _log_sig_pair(_dot(qb[:, hh * HD:(hh + 1) * HD], kh, 1, 1))
                        if diag:
                            lk = jnp.where(before, lk, 0.0)
                        suf, tot = _sub_sums(lk, tri1)
                        ws, run = [], runs[hh]
                        for s in reversed(range(NSUB)):
                            ws.append(jnp.exp(lb[:, s * BLK:(s + 1) * BLK] + suf[s] + run))
                            run = run + tot[s]
                        w = jnp.concatenate(ws[::-1], axis=1)
                        if diag:
                            w = jnp.where(before, w, 0.0)
                        accs[hh] = accs[hh] + _dot(w.astype(BF16), vh, 1, 0)
                        runs[hh] = run
                o_ref[pl.ds(t0, QB), :] = jnp.concatenate(accs, axis=1)
                t_ref[pl.ds(t0, QB), :] = _lane_halves(runs[0], runs[1])
                return carry0

            lax.fori_loop(0, QPG, qblock, 0)

    slab = lambda off: pl.BlockSpec((None, S, LANES), lambda p: (off + p, 0, 0))
    out = pl.BlockSpec((None, S, LANES), lambda p: (p, 0, 0))
    return pl.pallas_call(
        body, grid=(2,), in_specs=[slab(q0), slab(k0), slab(v0)], out_specs=[out, out],
        out_shape=[jax.ShapeDtypeStruct((2, S, LANES), F32)] * 2,
        scratch_shapes=[pltpu.VMEM((S, LANES), BF16)] * 3,
        compiler_params=_cp("arbitrary"), name=name)(proj, proj, proj)


def _stick_bwd(proj, do, tot, *, q0, k0, v0, name):
    def body(q_ref, k_ref, v_ref, do_ref, t_ref, dq_ref, dk_ref, dv_ref, qs, ks, vs, dos, dk_acc, dv_acc):
        qs[...] = (q_ref[...] * SCALE).astype(BF16)
        ks[...] = k_ref[...].astype(BF16)
        vs[...] = v_ref[...].astype(BF16)
        dos[...] = do_ref[...].astype(BF16)
        dk_acc[...] = jnp.zeros((2, S, HD), F32)
        dv_acc[...] = jnp.zeros((2, S, HD), F32)
        tri_inc = _tri_ones(lambda j, s: j <= s)
        tri_exc = _tri_ones(lambda j, s: j < s)
        col = lax.broadcasted_iota(jnp.int32, (QB, KC), 1)
        rowi = lax.broadcasted_iota(jnp.int32, (QB, KC), 0)

        for qg in range(QGROUPS):
            def qblock(ii, carry0, qg=qg):
                t0 = pl.multiple_of((qg * QPG + ii) * QB, QB)
                qb = qs[pl.ds(t0, QB), :]
                dob = dos[pl.ds(t0, QB), :]
                tb = t_ref[pl.ds(t0, QB), :]
                dqs = [jnp.zeros((QB, HD), F32)] * 2
                pruns = [jnp.zeros((QB, BLK), F32)] * 2
                eruns = [jnp.zeros((QB, BLK), F32)] * 2
                for c in range(qg + 1):
                    s0 = c * KC
                    diag = c == qg
                    before = (s0 + col) < (t0 + rowi) if diag else None
                    for hh in range(2):
                        qh = qb[:, hh * HD:(hh + 1) * HD]
                        doh = dob[:, hh * HD:(hh + 1) * HD]
                        tt = tb[:, 64 * hh:64 * hh + 1]
                        kh = ks[s0:s0 + KC, hh * HD:(hh + 1) * HD]
                        vh = vs[s0:s0 + KC, hh * HD:(hh + 1) * HD]
                        lb, lk = _log_sig_pair(_dot(qh, kh, 1, 1))
                        if diag:
                            lk = jnp.where(before, lk, 0.0)
                        pin, ptot = _sub_sums(lk, tri_inc)
                        ws, prun = [], pruns[hh]
                        for s in range(NSUB):
                            ws.append(jnp.exp(lb[:, s * BLK:(s + 1) * BLK] + (tt - (pin[s] + prun))))
                            prun = prun + ptot[s]
                        w = jnp.concatenate(ws, axis=1)
                        if diag:
                            w = jnp.where(before, w, 0.0)
                        e = w * _dot(doh, vh, 1, 1)
                        pex, etot = _sub_sums(e, tri_exc)
                        cs, erun = [], eruns[hh]
                        for s in range(NSUB):
                            cs.append(pex[s] + erun)
                            erun = erun + etot[s]
                        sig = jnp.exp(lb)
                        dz = e * (1.0 - sig) - jnp.concatenate(cs, axis=1) * sig
                        if diag:
                            dz = jnp.where(before, dz, 0.0)
                        dz = dz.astype(BF16)
                        dqs[hh] = dqs[hh] + _dot(dz, kh, 1, 0)
                        dk_acc[hh, s0:s0 + KC, :] += _dot(dz, qh, 0, 0)
                        dv_acc[hh, s0:s0 + KC, :] += _dot(w.astype(BF16), doh, 0, 0)
                        pruns[hh], eruns[hh] = prun, erun
                dq_ref[pl.ds(t0, QB), :] = (jnp.concatenate(dqs, axis=1) * SCALE).astype(BF16)
                return carry0

            lax.fori_loop(0, QPG, qblock, 0)
        dk_ref[...] = jnp.concatenate([dk_acc[0], dk_acc[1]], axis=1).astype(BF16)
        dv_ref[...] = jnp.concatenate([dv_acc[0], dv_acc[1]], axis=1).astype(BF16)

    slab = lambda off: pl.BlockSpec((None, S, LANES), lambda p: (off + p, 0, 0))
    pair = pl.BlockSpec((None, S, LANES), lambda p: (p, 0, 0))
    return pl.pallas_call(
        body, grid=(2,), in_specs=[slab(q0), slab(k0), slab(v0), pair, pair], out_specs=[pair] * 3,
        out_shape=[jax.ShapeDtypeStruct((2, S, LANES), BF16)] * 3,
        scratch_shapes=[pltpu.VMEM((S, LANES), BF16)] * 4 + [pltpu.VMEM((2, S, HD), F32)] * 2,
        compiler_params=_cp("arbitrary"), name=name)(proj, proj, proj, do, tot)


def _cat_slabs(ref):
    return jnp.concatenate([ref[s] for s in range(ref.shape[0])], axis=1)


def _merge_fwd(o_a, o_b, o_c, gates, b_gate, wa, wb, wc, w_out, name):
    tm = ROW_TILE

    def body(oa_ref, ob_ref, oc_ref, g_ref, bg_ref, wa_ref, wb_ref, wc_ref, wo_ref, mg_ref, mo_ref):
        acc = jnp.zeros((tm, D), F32)
        for i, (o_ref, w_ref) in enumerate(((oa_ref, wa_ref), (ob_ref, wb_ref), (oc_ref, wc_ref))):
            pr = _dot(_cat_slabs(o_ref).astype(BF16), w_ref[...], 1, 0)
            sg = jax.nn.sigmoid(g_ref[:, i * D:(i + 1) * D] + bg_ref[i:i + 1, :])
            acc = acc + sg * pr
        mg = acc.astype(BF16)
        mg_ref[...] = mg
        mo_ref[...] = _dot(mg, wo_ref[...], 1, 0)

    slabs = lambda n: pl.BlockSpec((n, tm, LANES), lambda i: (0, i, 0))
    full = lambda r, c: pl.BlockSpec((r, c), lambda i: (0, 0))
    row = pl.BlockSpec((tm, D), lambda i: (i, 0))
    return pl.pallas_call(
        body, grid=(S // tm,),
        in_specs=[slabs(2), slabs(4), slabs(2), pl.BlockSpec((tm, GATE_COLS), lambda i: (i, 0)), full(3, D),
                  full(256, D), full(512, D), full(256, D), full(D, D)],
        out_specs=[row, row],
        out_shape=[jax.ShapeDtypeStruct((S, D), BF16), jax.ShapeDtypeStruct((S, D), F32)],
        compiler_params=_cp("parallel"), name=name)(o_a, o_b, o_c, gates, b_gate, wa, wb, wc, w_out)


def _merge_bwd(d_mo, o_a, o_b, o_c, gates, b_gate, wa, wb, wc, w_out, name):
    tm = ROW_TILE

    def body(dmo_ref, oa_ref, ob_ref, oc_ref, g_ref, bg_ref, wa_ref, wb_ref, wc_ref, wo_ref,
             doa_ref, dob_ref, doc_ref, dg_ref, dwa_ref, dwb_ref, dwc_ref, dbg_ref):
        @pl.when(pl.program_id(0) == 0)
        def _():
            dwa_ref[...] = jnp.zeros(dwa_ref.shape, F32)
            dwb_ref[...] = jnp.zeros(dwb_ref.shape, F32)
            dwc_ref[...] = jnp.zeros(dwc_ref.shape, F32)
            dbg_ref[...] = jnp.zeros(dbg_ref.shape, F32)

        dmg = _dot(dmo_ref[...], wo_ref[...], 1, 1)
        trip = ((oa_ref, wa_ref, doa_ref, dwa_ref), (ob_ref, wb_ref, dob_ref, dwb_ref), (oc_ref, wc_ref, doc_ref, dwc_ref))
        for i, (o_ref, w_ref, do_ref, dw_ref) in enumerate(trip):
            ob = _cat_slabs(o_ref).astype(BF16)
            pr = _dot(ob, w_ref[...], 1, 0)
            sg = jax.nn.sigmoid(g_ref[:, i * D:(i + 1) * D] + bg_ref[i:i + 1, :])
            dgate = dmg * pr * sg * (1.0 - sg)
            dg_ref[:, i * D:(i + 1) * D] = dgate.astype(BF16)
            dbg_ref[i:i + 1, :] += jnp.sum(dgate, axis=0, keepdims=True)
            dpr = (dmg * sg).astype(BF16)
            do = _dot(dpr, w_ref[...], 1, 1)
            for s in range(do_ref.shape[0]):
                do_ref[s] = do[:, s * LANES:(s + 1) * LANES]
            dw_ref[...] += _dot(ob, dpr, 0, 0)

    slabs = lambda n: pl.BlockSpec((n, tm, LANES), lambda i: (0, i, 0))
    full = lambda r, c: pl.BlockSpec((r, c), lambda i: (0, 0))
    row = pl.BlockSpec((tm, D), lambda i: (i, 0))
    return pl.pallas_call(
        body, grid=(S // tm,),
        in_specs=[row, slabs(2), slabs(4), slabs(2), pl.BlockSpec((tm, GATE_COLS), lambda i: (i, 0)), full(3, D),
                  full(256, D), full(512, D), full(256, D), full(D, D)],
        out_specs=[slabs(2), slabs(4), slabs(2), pl.BlockSpec((tm, GATE_COLS), lambda i: (i, 0)),
                   full(256, D), full(512, D), full(256, D), full(3, D)],
        out_shape=[jax.ShapeDtypeStruct((2, S, LANES), F32), jax.ShapeDtypeStruct((4, S, LANES), F32),
                   jax.ShapeDtypeStruct((2, S, LANES), F32), jax.ShapeDtypeStruct((S, GATE_COLS), BF16),
                   jax.ShapeDtypeStruct((256, D), F32), jax.ShapeDtypeStruct((512, D), F32),
                   jax.ShapeDtypeStruct((256, D), F32), jax.ShapeDtypeStruct((3, D), F32)],
        compiler_params=_cp("arbitrary"), name=name)(d_mo, o_a, o_b, o_c, gates, b_gate, wa, wb, wc, w_out)


FC = 256
GELU_K = math.sqrt(2.0 / math.pi)
GELU_C = 0.044715


RC = 64
NRC = S // RC


def _down(tail, cur, n):
    row = lax.broadcasted_iota(jnp.int32, tail.shape, 0)
    rolled = pltpu.roll(cur, n, 0)
    first = jnp.where(row < n, pltpu.roll(tail, n, 0), rolled[0:8])
    return jnp.concatenate([first, rolled[8:]], axis=0)


def _up(cur, head, n):
    row = lax.broadcasted_iota(jnp.int32, head.shape, 0)
    rolled = pltpu.roll(cur, RC - n, 0)
    last = jnp.where(row >= 8 - n, pltpu.roll(head, 8 - n, 0), rolled[RC - 8:])
    return jnp.concatenate([rolled[:RC - 8], last], axis=0)


def _conv_chunk(load, j, w_ref, b_ref, half):
    r0 = pl.multiple_of(j * RC, RC)
    cur = load(r0, RC).astype(F32)
    tail = load(pl.multiple_of(jnp.maximum(r0 - 16, 0), 16), 16).astype(F32)[8:16]
    tail = jnp.where(j > 0, tail, 0.0)
    d1 = _down(tail, cur, 1)
    d2 = _down(tail, cur, 2)
    y = w_ref[0:1, half, :] * d2 + w_ref[1:2, half, :] * d1 + w_ref[2:3, half, :] * cur + b_ref[half:half + 1, :]
    return y, cur, d1, d2


def _chunk(j):
    return pl.ds(pl.multiple_of(j * RC, RC), RC)


def _fold8(x):
    return jnp.sum(x.reshape(RC // 8, 8, x.shape[-1]), axis=0)


def _ffn_act(u, conv_w, conv_b, name):
    def body(u_ref, w_ref, b_ref, a_ref, y_ref):
        def step(j, carry):
            yg = _conv_chunk(lambda r, n: u_ref[0, pl.ds(r, n), :], j, w_ref, b_ref, 0)[0]
            yv = _conv_chunk(lambda r, n: u_ref[1, pl.ds(r, n), :], j, w_ref, b_ref, 1)[0]
            th = jnp.tanh(GELU_K * (yg + GELU_C * yg * yg * yg))
            a_ref[_chunk(j), :] = (0.5 * yg * (1.0 + th) * yv).astype(BF16)
            y_ref[0, _chunk(j), :] = yg.astype(BF16)
            y_ref[1, _chunk(j), :] = yv.astype(BF16)
            return carry

        lax.fori_loop(0, NRC, step, 0)

    return pl.pallas_call(
        body, grid=(D_FF // FC,),
        in_specs=[pl.BlockSpec((2, S, FC), lambda j: (0, 0, j)), pl.BlockSpec((3, 2, FC), lambda j: (0, 0, j)),
                  pl.BlockSpec((2, FC), lambda j: (0, j))],
        out_specs=[pl.BlockSpec((S, FC), lambda j: (0, j)), pl.BlockSpec((2, S, FC), lambda j: (0, 0, j))],
        out_shape=[jax.ShapeDtypeStruct((S, D_FF), BF16), jax.ShapeDtypeStruct((2, S, D_FF), BF16)],
        compiler_params=_cp("parallel"), name=name)(u, conv_w, conv_b)


def _ffn_act_bwd(u, y, d_a, conv_w, name):
    def body(u_ref, y_ref, da_ref, w_ref, du_ref, dw_ref, db_ref, dy_s):
        def first(j, acc):
            yg = y_ref[0, _chunk(j), :].astype(F32)
            yv = y_ref[1, _chunk(j), :].astype(F32)
            th = jnp.tanh(GELU_K * (yg + GELU_C * yg * yg * yg))
            gelu = 0.5 * yg * (1.0 + th)
            dgelu = 0.5 * (1.0 + th) + 0.5 * yg * (1.0 - th * th) * GELU_K * (1.0 + 3.0 * GELU_C * yg * yg)
            da = da_ref[_chunk(j), :].astype(F32)
            dyg = da * yv * dgelu
            dyv = da * gelu
            dy_s[0, _chunk(j), :] = dyg
            dy_s[1, _chunk(j), :] = dyv
            return acc[0] + _fold8(dyg), acc[1] + _fold8(dyv)

        zero = jnp.zeros((8, FC), F32)
        accb = lax.fori_loop(0, NRC, first, (zero, zero))
        for half in range(2):
            db_ref[half:half + 1, :] = jnp.sum(accb[half], axis=0, keepdims=True)

        def second(j, acc):
            new = []
            for half in range(2):
                cur = dy_s[half, _chunk(j), :]
                h0 = pl.multiple_of(jnp.minimum((j + 1) * RC, S - 8), 8)
                head = jnp.where(j < NRC - 1, dy_s[half, pl.ds(h0, 8), :], 0.0)
                up1 = _up(cur, head, 1)
                up2 = _up(cur, head, 2)
                du = w_ref[2:3, half, :] * cur + w_ref[1:2, half, :] * up1 + w_ref[0:1, half, :] * up2
                du_ref[half, _chunk(j), :] = du.astype(BF16)
                uu = u_ref[half, _chunk(j), :].astype(F32)
                new += [_fold8(up2 * uu), _fold8(up1 * uu), _fold8(cur * uu)]
            return tuple(a + n for a, n in zip(acc, new))

        accw = lax.fori_loop(0, NRC, second, tuple(zero for _ in range(6)))
        for half in range(2):
            for k in range(3):
                dw_ref[k:k + 1, half, :] = jnp.sum(accw[3 * half + k], axis=0, keepdims=True)

    return pl.pallas_call(
        body, grid=(D_FF // FC,),
        in_specs=[pl.BlockSpec((2, S, FC), lambda j: (0, 0, j)), pl.BlockSpec((2, S, FC), lambda j: (0, 0, j)),
                  pl.BlockSpec((S, FC), lambda j: (0, j)), pl.BlockSpec((3, 2, FC), lambda j: (0, 0, j))],
        out_specs=[pl.BlockSpec((2, S, FC), lambda j: (0, 0, j)), pl.BlockSpec((3, 2, FC), lambda j: (0, 0, j)),
                   pl.BlockSpec((2, FC), lambda j: (0, j))],
        out_shape=[jax.ShapeDtypeStruct((2, S, D_FF), BF16), jax.ShapeDtypeStruct((3, 2, D_FF), F32),
                   jax.ShapeDtypeStruct((2, D_FF), F32)],
        scratch_shapes=[pltpu.VMEM((2, S, FC), F32)],
        compiler_params=_cp("parallel"), name=name)(u, y, d_a, conv_w)


def _layer_fwd(x, h1, w, bias, lname):
    n = lambda s: f"{lname}_{s}"
    w.need("in", h1)
    tn = 768
    proj = _mm(h1, w["w_in"], grid=(S // 1024, QKV_COLS // tn, 1),
               a_spec=pl.BlockSpec((1024, D), lambda i, j, k: (i, 0)),
               b_spec=pl.BlockSpec((tn, D), lambda i, j, k: (j, 0)),
               out_shape=jax.ShapeDtypeStruct((QKV_SLABS, S, LANES), F32),
               out_spec=pl.BlockSpec((tn // LANES, 1024, LANES), lambda i, j, k: (j, i, 0)),
               ca=1, cb=1, acc_shape=(1024, tn), out_slab=True, name=n("proj_qkv"))
    gates = _mm(h1, w["w_in"], grid=(S // 1024, GATE_COLS // tn, 1),
                a_spec=pl.BlockSpec((1024, D), lambda i, j, k: (i, 0)),
                b_spec=pl.BlockSpec((tn, D), lambda i, j, k: (j + QKV_COLS // tn, 0)),
                out_shape=jax.ShapeDtypeStruct((S, GATE_COLS), BF16),
                out_spec=pl.BlockSpec((1024, tn), lambda i, j, k: (i, j)),
                ca=1, cb=1, acc_shape=(1024, tn), name=n("proj_gate"))
    nums, stats = [], []
    for g, (_, d) in enumerate(A_GROUPS):
        nm, st = _band_fwd(proj, bias, w["sinks"], d=d, q0=2 * g, k0=6 + 2 * g, v0=12 + 2 * g, npairs=2, bias0=2 * g,
                           shared_kv=False, name=n(f"attn_a{g}_fwd"))
        nums.append(nm)
        stats.append(st)
    o_a, lse_a = _combine_a(nums, stats, n("attn_a_combine"))
    o_b, lse_b = _band_fwd(proj, bias, w["sinks"], d=1, q0=18, k0=22, v0=23, npairs=4, bias0=6, shared_kv=True,
                           name=n("attn_b_fwd"))
    o_c, tot_c = _stick_fwd(proj, q0=24, k0=26, v0=28, name=n("attn_c_fwd"))
    w.need("mix", tot_c)
    merged, mo = _merge_fwd(o_a, o_b, o_c, gates, w["b_gate"], w["w_br_a"], w["w_br_b"], w["w_br_c"], w["w_out"], n("merge_fwd"))
    x2, h2 = _postnorm_res(x, mo, w["attn_post_norm"], w["ffn_pre_norm"], n("attn_post"))
    w.need("ffn", h2)
    u = _mm(h2, w["w_up"], grid=(S // 1024, 2 * D_FF // 1024, 1),
            a_spec=pl.BlockSpec((1024, D), lambda i, j, k: (i, 0)),
            b_spec=pl.BlockSpec((D, 1024), lambda i, j, k: (0, j)),
            out_shape=jax.ShapeDtypeStruct((2, S, D_FF), BF16),
            out_spec=pl.BlockSpec((None, 1024, 1024), lambda i, j, k: (j // 4, i, j % 4)),
            ca=1, cb=0, acc_shape=(1024, 1024), name=n("ffn_up"))
    a, y = _ffn_act(u, w["conv_w"], w["conv_b"], n("ffn_act"))
    fo = _mm_nn(a, w["w_down"], F32, 1024, 1024, 2048, n("ffn_down"))
    saved = dict(x=x, h1=h1, proj=proj, gates=gates, o_a=o_a, lse_a=lse_a, o_b=o_b, lse_b=lse_b, o_c=o_c, tot_c=tot_c,
                 merged=merged, mo=mo, x2=x2, h2=h2, u=u, y=y, a=a, fo=fo)
    return saved


def _layer_bwd(dx3, sv, w, bias, lname, tok=None, on_part=None):
    n = lambda s: f"{lname}_{s}"
    g = {}

    def part(group, vec):
        t = on_part(group, g) if on_part is not None else None
        return vec if t is None else vec + t

    gain = w["ffn_post_norm"] if tok is None else w["ffn_post_norm"] + tok
    d_fo, g["ffn_post_norm"] = _norm_bwd(sv["fo"], gain, [dx3], None, BF16, n("ffn_post_bwd"))
    d_a = _mm_nt(d_fo, w["w_down"], BF16, 1024, 1024, 1024, n("ffn_down_bwd_x"))
    g["w_down"] = _mm_tn(sv["a"], d_fo, BF16, 1024, 1024, S, n("ffn_down_bwd_w"))
    d_u, dcw, dcb = _ffn_act_bwd(sv["u"], sv["y"], d_a, w["conv_w"], n("ffn_act_bwd"))
    g["conv_w"] = dcw.reshape(3, 2 * D_FF)
    g["conv_b"] = dcb.reshape(1, 2 * D_FF)
    g["w_up"] = _mm(sv["h2"], d_u, grid=(1, 2 * D_FF // 1024, 1),
                    a_spec=pl.BlockSpec((S, D), lambda i, j, k: (k, 0)),
                    b_spec=pl.BlockSpec((None, S, 1024), lambda i, j, k: (j // 4, k, j % 4)),
                    out_shape=jax.ShapeDtypeStruct((D, 2 * D_FF), BF16),
                    out_spec=pl.BlockSpec((D, 1024), lambda i, j, k: (0, j)),
                    ca=0, cb=0, acc_shape=(D, 1024), name=n("ffn_up_bwd_w"))
    tok_ffn = on_part("ffn", g) if on_part is not None else None
    d_h2 = _mm(d_u, w["w_up"], grid=(S // 1024, 1, 2),
               a_spec=pl.BlockSpec((None, 1024, D_FF), lambda i, j, k: (k, i, 0)),
               b_spec=pl.BlockSpec((D, D_FF), lambda i, j, k: (0, k)),
               out_shape=jax.ShapeDtypeStruct((S, D), F32),
               out_spec=pl.BlockSpec((1024, D), lambda i, j, k: (i, 0)),
               ca=1, cb=1, acc_shape=(1024, D), after=tok_ffn, name=n("ffn_up_bwd_x"))
    dx2, g["ffn_pre_norm"] = _norm_bwd(sv["x2"], w["ffn_pre_norm"], [d_h2], dx3, F32, n("ffn_pre_bwd"))
    d_mo, g["attn_post_norm"] = _norm_bwd(sv["mo"], w["attn_post_norm"], [dx2], None, BF16, n("attn_post_bwd"))
    g["w_out"] = _mm_tn(sv["merged"], d_mo, BF16, 1024, 1024, S, n("out_bwd_w"))
    do_a, do_b, do_c, d_gates, dwa, dwb, dwc, g["b_gate"] = _merge_bwd(
        d_mo, sv["o_a"], sv["o_b"], sv["o_c"], sv["gates"], w["b_gate"], w["w_br_a"], w["w_br_b"], w["w_br_c"],
        w["w_out"], n("merge_bwd"))
    g["w_br_a"], g["w_br_b"], g["w_br_c"] = dwa, dwb, dwc
    sinks = part("mix", w["sinks"])
    proj = sv["proj"]
    dqa, dka, dva, gbias = [], [], [], []
    for gi, (_, d) in enumerate(A_GROUPS):
        dq, dk, dv, gg, _ = _band_bwd(proj, bias, sv["o_a"], do_a, sv["lse_a"], sinks, d=d, q0=2 * gi, k0=6 + 2 * gi,
                                      v0=12 + 2 * gi, npairs=2, bias0=2 * gi, shared_kv=False, name=n(f"attn_a{gi}_bwd"))
        dqa.append(dq), dka.append(dk), dva.append(dv), gbias.append(gg)
    dqb, dkb, dvb, ggb, dsink = _band_bwd(proj, bias, sv["o_b"], do_b, sv["lse_b"], sinks, d=1, q0=18, k0=22, v0=23,
                                          npairs=4, bias0=6, shared_kv=True, name=n("attn_b_bwd"))
    gbias.append(ggb)
    g["bias_g"] = jnp.concatenate(gbias, axis=0).reshape(N_BIAS_HEADS, BLK, 2 * BLK)
    g["sinks"] = dsink[:, 0, :2].reshape(1, 8)
    dqc, dkc, dvc = _stick_bwd(proj, do_c, sv["tot_c"], q0=24, k0=26, v0=28, name=n("attn_c_bwd"))
    dqkv = jnp.concatenate(dqa + dka + dva + [dqb, dkb, dvb, dqc, dkc, dvc], axis=0)
    ts = 6
    tsx = QKV_SLABS
    dw_in = _mm(dqkv, sv["h1"], grid=(QKV_SLABS // ts, 1, 1),
                a_spec=pl.BlockSpec((ts, S, LANES), lambda i, j, k: (i, k, 0)),
                b_spec=pl.BlockSpec((S, D), lambda i, j, k: (k, 0)),
                out_shape=jax.ShapeDtypeStruct((IN_COLS, D), BF16),
                out_spec=pl.BlockSpec((ts * LANES, D), lambda i, j, k: (i, 0)),
                ca=0, cb=0, acc_shape=(ts * LANES, D), a_slab=True, name=n("in_bwd_w_qkv"))
    g["w_in"] = _mm(d_gates, sv["h1"], grid=(GATE_COLS // 768, 1, 1),
                    a_spec=pl.BlockSpec((S, 768), lambda i, j, k: (k, i)),
                    b_spec=pl.BlockSpec((S, D), lambda i, j, k: (k, 0)),
                    out_shape=jax.ShapeDtypeStruct((IN_COLS, D), BF16),
                    out_spec=pl.BlockSpec((768, D), lambda i, j, k: (i + QKV_COLS // 768, 0)),
                    ca=0, cb=0, acc_shape=(768, D), alias_out=dw_in, name=n("in_bwd_w_gate"))
    tok_in = on_part("in", g) if on_part is not None else None
    d_h1a = _mm(dqkv, w["w_in"], grid=(S // 1024, 1, QKV_SLABS // tsx),
                a_spec=pl.BlockSpec((tsx, 1024, LANES), lambda i, j, k: (k, i, 0)),
                b_spec=pl.BlockSpec((tsx * LANES, D), lambda i, j, k: (k, 0)),
                out_shape=jax.ShapeDtypeStruct((S, D), F32),
                out_spec=pl.BlockSpec((1024, D), lambda i, j, k: (i, 0)),
                ca=1, cb=0, acc_shape=(1024, D), a_slab=True, after=tok_in, name=n("in_bwd_x_qkv"))
    d_h1b = _mm(d_gates, w["w_in"], grid=(S // 1024, 1, GATE_COLS // 768),
                a_spec=pl.BlockSpec((1024, 768), lambda i, j, k: (i, k)),
                b_spec=pl.BlockSpec((768, D), lambda i, j, k: (k + QKV_COLS // 768, 0)),
                out_shape=jax.ShapeDtypeStruct((S, D), F32),
                out_spec=pl.BlockSpec((1024, D), lambda i, j, k: (i, 0)),
                ca=1, cb=0, acc_shape=(1024, D), after=tok_in, name=n("in_bwd_x_gate"))
    dx, g["attn_pre_norm"] = _norm_bwd(sv["x"], w["attn_pre_norm"], [d_h1a, d_h1b], dx2, F32, n("attn_pre_bwd"))
    return dx, g, tok_in


def _local_step(x, target, ws, rel_bias, tok=None, on_grads=None):
    buckets = jnp.asarray(_bucket_tiles())
    bias = _bias_tiles(rel_bias, buckets, "bias_tiles").reshape(N_BIAS_HEADS // 2, 2, 2, BLK, 2 * BLK)
    saved = []
    gain0 = ws[0]["attn_pre_norm"] if tok is None else ws[0]["attn_pre_norm"] + tok
    h1 = _prenorm(x, gain0, "l0_attn_pre")
    for l in range(DEPTH):
        sv = _layer_fwd(x, h1, ws[l], bias, f"l{l}")
        saved.append(sv)
        g_next = ws[l + 1]["attn_pre_norm"] if l + 1 < DEPTH else ws[l]["attn_pre_norm"]
        x, h1 = _postnorm_res(sv["x2"], sv["fo"], ws[l]["ffn_post_norm"], g_next, f"l{l}_ffn_post")
    dy, loss_tile = _loss_head(x, target, "loss_head")
    grads = [None] * DEPTH
    tok = None
    for l in reversed(range(DEPTH)):
        on_part = None if on_grads is None else functools.partial(on_grads, l)
        dy, grads[l], tok = _layer_bwd(dy, saved[l], ws[l], bias, f"l{l}", tok, on_part)
    g_rel = _bias_grad([grads[l]["bias_g"] for l in range(DEPTH)], buckets, "bias_grad")[:, :N_BIAS_HEADS]
    return loss_tile[0, 0], dy, grads, g_rel


def _coords():
    return lax.axis_index("x"), lax.axis_index("y"), lax.axis_index("c")


def _peer(rel):
    x, y, c = _coords()
    return (1 - x if rel & 4 else x, 1 - y if rel & 2 else y, 1 - c if rel & 1 else c)


def _exchange(srcs, dst_shapes, src_win, dst_win, name, after=None):
    nt = len(srcs)
    extra = [] if after is None else [after]

    def body(*refs):
        src_refs, dst_refs = refs[:nt], refs[nt + len(extra):2 * nt + len(extra)]
        send_sems, recv_sems, local_sems = refs[2 * nt + len(extra):]
        x, y, c = _coords()
        me = 4 * x + 2 * y + c
        locals_ = []
        for t in range(nt):
            cp = pltpu.make_async_copy(src_win(t, src_refs[t], me), dst_win(t, dst_refs[t], me), local_sems.at[t])
            cp.start()
            locals_.append(cp)
        sends = []
        for rel in range(1, NDEV):
            px, py, pc = _peer(rel)
            q = 4 * px + 2 * py + pc
            for t in range(nt):
                cp = pltpu.make_async_remote_copy(
                    src_ref=src_win(t, src_refs[t], q), dst_ref=dst_win(t, dst_refs[t], me),
                    send_sem=send_sems.at[rel - 1, t], recv_sem=recv_sems.at[rel - 1, t],
                    device_id=(px, py, pc), device_id_type=MESH)
                cp.start()
                sends.append(cp)
        for rel in range(1, NDEV):
            px, py, pc = _peer(rel)
            q = 4 * px + 2 * py + pc
            for t in range(nt):
                pltpu.make_async_remote_copy(
                    src_ref=src_win(t, src_refs[t], me), dst_ref=dst_win(t, dst_refs[t], q),
                    send_sem=send_sems.at[rel - 1, t], recv_sem=recv_sems.at[rel - 1, t],
                    device_id=(px, py, pc), device_id_type=MESH).wait_recv()
        for cp in sends:
            cp.wait_send()
        for cp in locals_:
            cp.wait()

    return pl.pallas_call(
        body, in_specs=[ANY] * (nt + len(extra)), out_specs=[ANY] * nt, out_shape=dst_shapes,
        scratch_shapes=[pltpu.SemaphoreType.DMA((NDEV - 1, nt)), pltpu.SemaphoreType.DMA((NDEV - 1, nt)),
                        pltpu.SemaphoreType.DMA((nt,))],
        name=name)(*srcs, *extra)


BIG = (("w_in", 0, 864), ("w_br_a", 1, 128), ("w_br_b", 1, 128), ("w_br_c", 1, 128), ("w_out", 0, 128),
       ("w_up", 1, 1024), ("w_down", 0, 512))


NBIG = len(BIG)
BIG_FULL = {"w_in": (IN_COLS, D), "w_br_a": (256, D), "w_br_b": (512, D), "w_br_c": (256, D), "w_out": (D, D),
            "w_up": (D, 2 * D_FF), "w_down": (D_FF, D)}
SHARD_ROWS = {"w_in": 288, "w_up": 256, "w_down": 256}
LAYER_GROUPS = (("in", (0,)), ("mix", (1, 2, 3, 4)), ("ffn", (5, 6)))

HBM_SPEC = pl.BlockSpec(memory_space=pltpu.HBM)
SEM_SPEC = pl.BlockSpec(memory_space=pltpu.SEMAPHORE)


def _hbm(a):
    return pltpu.with_memory_space_constraint(a, pltpu.HBM)


def _shard_window(t, ref, k):
    nm, ax, ext = BIG[t % NBIG]
    off = pl.multiple_of(k * ext, ext)
    if ax == 0:
        return ref.at[pl.ds(off, ext), :]
    return ref.at[:, pl.ds(off, ext)]


def _whole(t, ref, k):
    return ref


def _slot(t, ref, k):
    return ref.at[k]


def _own_block_spec(t, rows, me_of):
    nm, ax, ext = BIG[t % NBIG]
    r, c = BIG_FULL[nm]
    if ax == 0:
        return pl.BlockSpec((rows, c), lambda i, m: (me_of(m) * (ext // rows) + i, 0))
    return pl.BlockSpec((rows, ext), lambda i, m: (i, me_of(m)))


def _cast_own(t, shards, me_arr, name):
    nm, ax, ext = BIG[t % NBIG]
    layer = t // NBIG
    _, nr, nc = shards.shape
    rows = SHARD_ROWS.get(nm, nr)
    shape = BIG_FULL[nm]

    def body(m_ref, s_ref, o_ref):
        o_ref[...] = s_ref[...].astype(BF16)

    return pl.pallas_call(
        body, grid_spec=pltpu.PrefetchScalarGridSpec(
            num_scalar_prefetch=1, grid=(nr // rows,),
            in_specs=[pl.BlockSpec((None, rows, nc), lambda i, m: (layer, i, 0))],
            out_specs=_own_block_spec(t, rows, lambda m: m[0])),
        out_shape=jax.ShapeDtypeStruct(shape, BF16), compiler_params=_cp("arbitrary"), name=name)(me_arr, shards)


ALL_RELS = tuple(range(1, NDEV))
NEAR_RELS = (1, 2, 4, 6)
FAR_RELS = (2, 4, 6)


def _xchg_start(srcs, lands, groups, src_win, dst_win, after, name, rels=ALL_RELS, tids=None):
    ns = 0 if srcs is None else len(srcs)
    nt, ng = len(lands), len(groups)
    ins = ([] if srcs is None else list(srcs)) + list(lands)

    def body(*refs):
        src_refs, land_refs = refs[:ns], refs[ns:ns + nt]
        sems = refs[ns + nt + 1:ns + nt + 1 + 2 * ng]
        token = refs[-1]
        x, y, c = _coords()
        me = 4 * x + 2 * y + c
        for gi, grp in enumerate(groups):
            for j, t in enumerate(grp):
                tid = t if tids is None else tids[t]
                for ri, rel in enumerate(rels):
                    px, py, pc = _peer(rel)
                    q = 4 * px + 2 * py + pc
                    src = dst_win(tid, land_refs[t], me) if srcs is None else src_win(tid, src_refs[t], q)
                    pltpu.make_async_remote_copy(
                        src_ref=src, dst_ref=dst_win(tid, land_refs[t], me),
                        send_sem=sems[2 * gi].at[ri * len(grp) + j],
                        recv_sem=sems[2 * gi + 1].at[ri * len(grp) + j],
                        device_id=(px, py, pc), device_id_type=MESH).start()
        token[...] = jnp.zeros((8, LANES), F32)

    out_shape = []
    for grp in groups:
        out_shape += [pltpu.SemaphoreType.DMA((len(rels) * len(grp),))] * 2
    out_shape += [pltpu.HBM(a.shape, a.dtype) for a in ins]
    out_shape.append(jax.ShapeDtypeStruct((8, LANES), F32))
    outs = pl.pallas_call(
        body, in_specs=[HBM_SPEC] * len(ins) + [ANY],
        out_specs=[SEM_SPEC] * (2 * ng) + [HBM_SPEC] * len(ins) + [pl.BlockSpec(memory_space=pltpu.VMEM)],
        out_shape=out_shape, input_output_aliases={i: 2 * ng + i for i in range(len(ins))},
        compiler_params=pltpu.CompilerParams(has_side_effects=pltpu.SideEffectType.DATAFLOW_SIDE_EFFECTING),
        name=name)(*[_hbm(a) for a in ins], after)
    sems = [(outs[2 * gi], outs[2 * gi + 1]) for gi in range(ng)]
    thru = list(outs[2 * ng:2 * ng + len(ins)])
    return sems, (None if srcs is None else thru[:ns]), thru[ns:], outs[-1]


def _xchg_wait(sems, srcs, lands, tids, after, src_win, dst_win, name, rels=ALL_RELS):
    ns = 0 if srcs is None else len(srcs)
    n = len(lands)
    send_sem, recv_sem = sems
    ins = ([] if srcs is None else list(srcs)) + list(lands)

    def body(*refs):
        src_refs, land_refs = refs[:ns], refs[ns:ns + n]
        ssem, rsem = refs[ns + n], refs[ns + n + 1]
        x, y, c = _coords()
        me = 4 * x + 2 * y + c
        for j, t in enumerate(tids):
            for ri, rel in enumerate(rels):
                px, py, pc = _peer(rel)
                q = 4 * px + 2 * py + pc
                src = dst_win(t, land_refs[j], me) if srcs is None else src_win(t, src_refs[j], q)
                cp = pltpu.make_async_remote_copy(
                    src_ref=src, dst_ref=dst_win(t, land_refs[j], q),
                    send_sem=ssem.at[ri * n + j], recv_sem=rsem.at[ri * n + j],
                    device_id=(px, py, pc), device_id_type=MESH)
                cp.wait_send()
                cp.wait_recv()

    outs = pl.pallas_call(
        body, in_specs=[HBM_SPEC] * len(ins) + [SEM_SPEC, SEM_SPEC, ANY], out_specs=[HBM_SPEC] * len(ins),
        out_shape=[pltpu.HBM(a.shape, a.dtype) for a in ins],
        input_output_aliases={i: i for i in range(len(ins))},
        compiler_params=pltpu.CompilerParams(has_side_effects=pltpu.SideEffectType.DATAFLOW_SIDE_EFFECTING),
        name=name)(*ins, send_sem, recv_sem, after)
    return (None if srcs is None else list(outs[:ns])), list(outs[ns:])


def _gather_forward(sems_in, lands, groups, tids, after, dst_win, name):
    nt, ng = len(lands), len(groups)

    def body(*refs):
        land_refs = refs[:nt]
        in_sems = refs[nt:nt + 2 * ng]
        out_sems = refs[nt + 2 * ng + 1:nt + 4 * ng + 1]
        token = refs[-1]
        x, y, c = _coords()
        me = 4 * x + 2 * y + c
        sib = (x, y, 1 - c)
        for gi, grp in enumerate(groups):
            n = len(grp)
            for j, pos in enumerate(grp):
                t = tids[pos]
                for ri, rel in enumerate(NEAR_RELS):
                    px, py, pc = _peer(rel)
                    q = 4 * px + 2 * py + pc
                    cp = pltpu.make_async_remote_copy(
                        src_ref=dst_win(t, land_refs[pos], me), dst_ref=dst_win(t, land_refs[pos], q),
                        send_sem=in_sems[2 * gi].at[ri * n + j], recv_sem=in_sems[2 * gi + 1].at[ri * n + j],
                        device_id=(px, py, pc), device_id_type=MESH)
                    cp.wait_send()
                    cp.wait_recv()
            for j, pos in enumerate(grp):
                t = tids[pos]
                for fi, rel in enumerate(FAR_RELS):
                    px, py, pc = _peer(rel)
                    q = 4 * px + 2 * py + pc
                    win = dst_win(t, land_refs[pos], q)
                    pltpu.make_async_remote_copy(
                        src_ref=win, dst_ref=win,
                        send_sem=out_sems[2 * gi].at[fi * n + j], recv_sem=out_sems[2 * gi + 1].at[fi * n + j],
                        device_id=sib, device_id_type=MESH).start()
        token[...] = jnp.zeros((8, LANES), F32)

    out_shape = []
    for grp in groups:
        out_shape += [pltpu.SemaphoreType.DMA((len(FAR_RELS) * len(grp),))] * 2
    out_shape += [pltpu.HBM(a.shape, a.dtype) for a in lands]
    out_shape.append(jax.ShapeDtypeStruct((8, LANES), F32))
    flat_sems = [s for pair in sems_in for s in pair]
    outs = pl.pallas_call(
        body, in_specs=[HBM_SPEC] * nt + [SEM_SPEC] * (2 * ng) + [ANY],
        out_specs=[SEM_SPEC] * (2 * ng) + [HBM_SPEC] * nt + [pl.BlockSpec(memory_space=pltpu.VMEM)],
        out_shape=out_shape, input_output_aliases={i: 2 * ng + i for i in range(nt)},
        compiler_params=pltpu.CompilerParams(has_side_effects=pltpu.SideEffectType.DATAFLOW_SIDE_EFFECTING),
        name=name)(*[_hbm(a) for a in lands], *flat_sems, after)
    sems = [(outs[2 * gi], outs[2 * gi + 1]) for gi in range(ng)]
    return sems, list(outs[2 * ng:2 * ng + nt]), outs[-1]


class _Weights:
    def __init__(self, ready, pending=None):
        self.ready = dict(ready)
        self.pending = dict(pending or {})

    def __getitem__(self, k):
        return self.ready[k]

    def need(self, group, after):
        fn = self.pending.pop(group, None)
        if fn is not None:
            self.ready.update(fn(after))


def _adamw_math(w, g, m, v):
    m2 = ADAM_B1 * m + (1.0 - ADAM_B1) * g
    v2 = ADAM_B2 * v + (1.0 - ADAM_B2) * (g * g)
    m_hat = m2 / (1.0 - ADAM_B1 ** ADAM_STEP)
    v_hat = v2 / (1.0 - ADAM_B2 ** ADAM_STEP)
    delta = -ADAM_LR * (m_hat / (jnp.sqrt(v_hat) + ADAM_EPS) + ADAM_WD * w)
    return delta, m2, v2


def _adamw(t, parts, own, me_arr, w, m, v, layer, prev, rows, name):
    nl, nr, nc = w.shape

    def body(me_ref, p_ref, own_ref, w_ref, m_ref, v_ref, *rest):
        g_ref, d_ref, m2_ref, v2_ref = rest[-4:]
        me = me_ref[0]
        g = None
        for k in range(NDEV):
            term = jnp.where(me == k, own_ref[...], p_ref[k]).astype(F32)
            g = term if g is None else g + term
        delta, m2, v2 = _adamw_math(w_ref[...], g, m_ref[...], v_ref[...])
        g_ref[...] = g
        d_ref[...] = delta
        m2_ref[...] = m2
        v2_ref[...] = v2

    blk = pl.BlockSpec((None, rows, nc), lambda i, mm: (layer, i, 0))
    pblk = pl.BlockSpec((NDEV, rows, nc), lambda i, mm: (0, i, 0))
    extra = [] if prev is None else list(prev)
    return pl.pallas_call(
        body, grid_spec=pltpu.PrefetchScalarGridSpec(
            num_scalar_prefetch=1, grid=(nr // rows,),
            in_specs=[pblk, _own_block_spec(t, rows, lambda mm: mm[0]), blk, blk, blk] + [ANY] * len(extra),
            out_specs=[blk] * 4),
        out_shape=[jax.ShapeDtypeStruct(w.shape, F32)] * 4,
        input_output_aliases={6 + k: k for k in range(len(extra))},
        compiler_params=_cp("arbitrary"), name=name)(me_arr, parts, own, w, m, v, *extra)


def _pack(vecs):
    flat = jnp.concatenate([v.reshape(-1).astype(F32) for v in vecs])
    n = flat.shape[0]
    rows = -(-n // (8 * LANES)) * 8
    return jnp.pad(flat, (0, rows * LANES - n)).reshape(rows, LANES)


ROWPACK = (("rel_bias", 32, 32, (NUM_BUCKETS, N_BIAS_HEADS)), ("sinks", 8, 8, (DEPTH, 8)),
           ("attn_pre_norm", 16, 16, (DEPTH, D)), ("attn_post_norm", 16, 16, (DEPTH, D)),
           ("ffn_pre_norm", 16, 16, (DEPTH, D)), ("ffn_post_norm", 16, 16, (DEPTH, D)),
           ("conv_b", 128, 128, (DEPTH, 2 * D_FF)), ("b_gate", 48, 8, (DEPTH, 3, 128)),
           ("conv_w", 384, 48, (DEPTH, 3, 1024)))
ROWS_OWN = sum(r for _, _, r, _ in ROWPACK)
N_REPL = 7
ROWS_REPL = sum(r for _, _, r, _ in ROWPACK[:N_REPL])
ROWS_SHARD = ROWS_OWN - ROWS_REPL


def _as_rows(a, rows):
    a = a.astype(F32)
    if a.shape[-1] < LANES:
        a = jnp.pad(a.reshape(-1, a.shape[-1]), ((0, 0), (0, LANES - a.shape[-1])))
    a = a.reshape(-1, LANES)
    return jnp.pad(a, ((0, rows - a.shape[0]), (0, 0)))


def _rowpack(arrs, entries=ROWPACK):
    return jnp.concatenate([_as_rows(arrs[nm], ro) for nm, _, ro, _ in entries], axis=0)


def _shard_rows(g):
    bg = jnp.transpose(g["b_gate"].astype(F32).reshape(DEPTH * 3, NDEV, LANES), (1, 0, 2))
    bg = jnp.pad(bg, ((0, 0), (0, 8 - DEPTH * 3), (0, 0)))
    cw = jnp.transpose(g["conv_w"].astype(F32).reshape(DEPTH * 3, NDEV, 8, LANES), (1, 0, 2, 3))
    return jnp.concatenate([bg, cw.reshape(NDEV, DEPTH * 3 * 8, LANES)], axis=1)


def _small_update(parts_repl, parts_shard, w, m, v, name):
    nsm = len(ROWPACK)

    def body(pr_ref, ps_ref, w_ref, m_ref, v_ref, *rest):
        outs = rest[:4 * nsm]
        g_s, d_s, m_s, v_s = rest[4 * nsm:]
        gr, gs = pr_ref[0], ps_ref[0]
        for k in range(1, NDEV):
            gr = gr + pr_ref[k]
            gs = gs + ps_ref[k]
        g_s[0:ROWS_REPL, :] = gr
        g_s[ROWS_REPL:ROWS_OWN, :] = gs
        delta, m2, v2 = _adamw_math(w_ref[...], g_s[...], m_ref[...], v_ref[...])
        d_s[...] = delta
        m_s[...] = m2
        v_s[...] = v2
        for kind, src in enumerate((g_s, d_s, m_s, v_s)):
            oo = 0
            for idx, (nm, rf, ro, shp) in enumerate(ROWPACK):
                o_ref = outs[kind * nsm + idx]
                if nm in ("rel_bias", "sinks"):
                    o_ref[...] = src[oo:oo + shp[0], 0:shp[1]]
                elif nm == "b_gate":
                    for l in range(DEPTH):
                        o_ref[l] = src[oo + 3 * l:oo + 3 * l + 3, :]
                elif nm == "conv_w":
                    for l in range(DEPTH):
                        for k in range(8):
                            o_ref[l, :, k * LANES:(k + 1) * LANES] = src[pl.ds(oo + 24 * l + k, 3, stride=8), :]
                else:
                    per = shp[1] // LANES
                    for k in range(per):
                        o_ref[:, k * LANES:(k + 1) * LANES] = src[pl.ds(oo + k, DEPTH, stride=per), :]
                oo += ro

    vm = pl.BlockSpec(memory_space=pltpu.VMEM)
    shapes = [jax.ShapeDtypeStruct(shp, F32) for _ in range(4) for _, _, _, shp in ROWPACK]
    outs = pl.pallas_call(
        body, in_specs=[vm] * 5, out_specs=[vm] * (4 * nsm), out_shape=shapes,
        scratch_shapes=[pltpu.VMEM((ROWS_OWN, LANES), F32)] * 4,
        name=name)(parts_repl, parts_shard, w, m, v)
    names = [nm for nm, _, _, _ in ROWPACK]
    return [dict(zip(names, outs[kind * nsm:(kind + 1) * nsm])) for kind in range(4)]


def kernel(x, rel_bias, attn_pre_norm, w_in, b_gate, sinks, w_br_a, w_br_b, w_br_c, w_out, attn_post_norm, ffn_pre_norm, w_up, conv_w, conv_b, w_down, ffn_post_norm, loss_target, m_rel_bias, m_attn_pre_norm, m_w_in, m_b_gate, m_sinks, m_w_br_a, m_w_br_b, m_w_br_c, m_w_out, m_attn_post_norm, m_ffn_pre_norm, m_w_up, m_conv_w, m_conv_b, m_w_down, m_ffn_post_norm, v_rel_bias, v_attn_pre_norm, v_w_in, v_b_gate, v_sinks, v_w_br_a, v_w_br_b, v_w_br_c, v_w_out, v_attn_post_norm, v_ffn_pre_norm, v_w_up, v_conv_w, v_conv_b, v_w_down, v_ffn_post_norm):
    P = dict(rel_bias=rel_bias, attn_pre_norm=attn_pre_norm, w_in=w_in, b_gate=b_gate, sinks=sinks, w_br_a=w_br_a,
             w_br_b=w_br_b, w_br_c=w_br_c, w_out=w_out, attn_post_norm=attn_post_norm, ffn_pre_norm=ffn_pre_norm,
             w_up=w_up, conv_w=conv_w, conv_b=conv_b, w_down=w_down, ffn_post_norm=ffn_post_norm)
    M = dict(rel_bias=m_rel_bias, attn_pre_norm=m_attn_pre_norm, w_in=m_w_in, b_gate=m_b_gate, sinks=m_sinks,
             w_br_a=m_w_br_a, w_br_b=m_w_br_b, w_br_c=m_w_br_c, w_out=m_w_out, attn_post_norm=m_attn_post_norm,
             ffn_pre_norm=m_ffn_pre_norm, w_up=m_w_up, conv_w=m_conv_w, conv_b=m_conv_b, w_down=m_w_down,
             ffn_post_norm=m_ffn_post_norm)
    V = dict(rel_bias=v_rel_bias, attn_pre_norm=v_attn_pre_norm, w_in=v_w_in, b_gate=v_b_gate, sinks=v_sinks,
             w_br_a=v_w_br_a, w_br_b=v_w_br_b, w_br_c=v_w_br_c, w_out=v_w_out, attn_post_norm=v_attn_post_norm,
             ffn_pre_norm=v_ffn_pre_norm, w_up=v_w_up, conv_w=v_conv_w, conv_b=v_conv_b, w_down=v_w_down,
             ffn_post_norm=v_ffn_post_norm)
    tr = lambda a: jnp.swapaxes(a, 1, 2)
    PB = {nm: (tr(P[nm]) if nm == "w_in" else P[nm]) for nm, _, _ in BIG}
    MB = {nm: (tr(M[nm]) if nm == "w_in" else M[nm]) for nm, _, _ in BIG}
    VB = {nm: (tr(V[nm]) if nm == "w_in" else V[nm]) for nm, _, _ in BIG}
    xi, yi, ci = _coords()
    me = 4 * xi + 2 * yi + ci

    me_arr = me.astype(jnp.int32).reshape(1)

    small_w = _pack([b_gate.reshape(-1), conv_w.reshape(-1)])
    (small_w_all,) = _exchange([small_w], [jax.ShapeDtypeStruct((NDEV,) + small_w.shape, F32)],
                               _whole, _slot, "gather_small_weights")

    groups = [tuple(l * NBIG + t for t in tids) for l in range(DEPTH) for _, tids in LAYER_GROUPS]
    cast = lambda i, m=me_arr: _cast_own(i, PB[BIG[i % NBIG][0]], m, f"gather_own_l{i // NBIG}_{BIG[i % NBIG][0]}")
    first = list(groups[0])
    rest = [i for grp in groups[1:] for i in grp]
    sems0, _, lands0, tok_first = _xchg_start(None, [cast(i) for i in first], [tuple(range(len(first)))], None,
                                              _shard_window, small_w_all, "gather_start_first", rels=NEAR_RELS, tids=first)
    where_rest = {tid: k for k, tid in enumerate(rest)}
    me_rest = me_arr + tok_first[0, 0:1].astype(jnp.int32)
    sems1, _, lands1, g_tok = _xchg_start(None, [cast(i, me_rest) for i in rest],
                                          [tuple(where_rest[i] for i in grp) for grp in groups[1:]], None,
                                          _shard_window, lands0[0], "gather_start_rest", rels=NEAR_RELS, tids=rest)
    g_sems = list(sems0) + list(sems1)
    tok0 = g_tok[0:1, 0:1]
    lands_now = [None] * (DEPTH * NBIG)
    for i, a in zip(first + rest, list(lands0) + list(lands1)):
        lands_now[i] = a
    fwd_sems = {}
    fwd_plan = {0: (0,), 1: (1,), 2: (2,), 3: (3, 4, 5)}

    def gather_waiter(gi, l, gname, tids):
        def wait(after):
            if gi in fwd_plan:
                gis = fwd_plan[gi]
                flat = [i for g2 in gis for i in groups[g2]]
                where = {tid: k for k, tid in enumerate(flat)}
                fs, new_lands, ftok = _gather_forward(
                    [g_sems[g2] for g2 in gis], [lands_now[i] for i in flat],
                    [[where[i] for i in groups[g2]] for g2 in gis], flat, after, _shard_window, f"gather_forward_{gi}")
                for g2, s in zip(gis, fs):
                    fwd_sems[g2] = s
                for i, a in zip(flat, new_lands):
                    lands_now[i] = a
                after = ftok
            ids = [l * NBIG + t for t in tids]
            _, got = _xchg_wait(fwd_sems[gi], None, [lands_now[i] for i in ids], ids, after,
                                None, _shard_window, f"gather_wait_l{l}_{gname}", rels=FAR_RELS)
            out = {}
            for t, arr in zip(tids, got):
                nm = BIG[t][0]
                out[nm] = arr
            return out
        return wait

    pending = [{gname: gather_waiter(l * len(LAYER_GROUPS) + k, l, gname, tids)
                for k, (gname, tids) in enumerate(LAYER_GROUPS)} for l in range(DEPTH)]
    nbg = DEPTH * 3 * 128
    ncw = DEPTH * 3 * 1024
    flat_all = small_w_all.reshape(NDEV, -1)
    b_gate_full = jnp.transpose(flat_all[:, :nbg].reshape(NDEV, DEPTH, 3, 128), (1, 2, 0, 3)).reshape(DEPTH, 3, D)
    conv_w_full = jnp.transpose(flat_all[:, nbg:nbg + ncw].reshape(NDEV, DEPTH, 3, 1024), (1, 2, 0, 3)).reshape(DEPTH, 3, 2 * D_FF)

    ws = []
    for l in range(DEPTH):
        ws.append(_Weights(dict(
            b_gate=b_gate_full[l], conv_w=conv_w_full[l].reshape(3, 2, D_FF), conv_b=conv_b[l].reshape(2, D_FF),
            sinks=sinks[l].reshape(1, 8),
            attn_pre_norm=attn_pre_norm[l].reshape(1, D), attn_post_norm=attn_post_norm[l].reshape(1, D),
            ffn_pre_norm=ffn_pre_norm[l].reshape(1, D), ffn_post_norm=ffn_post_norm[l].reshape(1, D)), pending[l]))

    rs = {}

    group_tids = dict(LAYER_GROUPS)

    def start_scatter(l, gname, grads_l):
        tids = group_tids[gname]
        blocks, lands_rs = [], []
        for t in tids:
            nm, ax, ext = BIG[t]
            gfull = grads_l[nm].astype(BF16)
            shp = (NDEV, ext, gfull.shape[1]) if ax == 0 else (NDEV, gfull.shape[0], ext)
            blocks.append(gfull)
            lands_rs.append(lax.empty(shp, BF16))
        local = list(range(len(tids)))
        win = lambda j, ref, k: _shard_window(tids[j], ref, k)
        sems, s_thru, l_thru, tok = _xchg_start(blocks, lands_rs, [tuple(local)], win, _slot, me_arr,
                                                f"scatter_start_l{l}_{gname}")
        rs[(l, gname)] = (sems[0], s_thru, l_thru, win, local)
        return tok[0:1, 0:1]

    loss_local, grad_x, grads, g_rel = _local_step(x[0], loss_target[0], ws, rel_bias, tok0, start_scatter)
    loss = lax.psum(loss_local, ("x", "y", "c"))

    stack = lambda nm: jnp.stack([grads[l][nm] for l in range(DEPTH)], axis=0)
    small_g = {nm: (g_rel if nm == "rel_bias" else stack(nm)) for nm, _, _, _ in ROWPACK}
    small_repl = _rowpack(small_g, ROWPACK[:N_REPL])
    small_shard = _shard_rows(small_g)

    out_g, out_d, out_m, out_v = {}, {}, {}, {}
    prev = {nm: None for nm, _, _ in BIG}
    todo = [(l, gname) for l in reversed(range(DEPTH)) for gname in ("ffn", "mix", "in")]
    after, small_parts = grad_x, None
    for l, gname in todo:
        if (l, gname) == todo[-1]:
            small_parts = _exchange(
                [small_repl, small_shard],
                [jax.ShapeDtypeStruct((NDEV, ROWS_REPL, LANES), F32), jax.ShapeDtypeStruct((NDEV, ROWS_SHARD, LANES), F32)],
                lambda t, ref, q: ref if t == 0 else ref.at[q], _slot, "exchange_small_grads", after=after)
            after = small_parts[0]
        sems, s_thru, l_thru, win, local = rs[(l, gname)]
        owns, parts = _xchg_wait(sems, s_thru, l_thru, local, after, win, _slot, f"scatter_wait_l{l}_{gname}")
        for t, own, prt in zip(group_tids[gname], owns, parts):
            nm = BIG[t][0]
            rows = SHARD_ROWS.get(nm, PB[nm].shape[1])
            prev[nm] = _adamw(t, prt, own, me_arr, PB[nm], MB[nm], VB[nm], l, prev[nm], rows, f"adamw_{nm}_l{l}")
            after = prev[nm][1]
    for nm, _, _ in BIG:
        out_g[nm], out_d[nm], out_m[nm], out_v[nm] = [tr(a) if nm == "w_in" else a for a in prev[nm]]
    sm_g, sm_d, sm_m, sm_v = _small_update(small_parts[0], small_parts[1], _rowpack(P), _rowpack(M), _rowpack(V),
                                           "small_update")
    for dst, src in ((out_g, sm_g), (out_d, sm_d), (out_m, sm_m), (out_v, sm_v)):
        dst.update(src)

    order = ["rel_bias", "attn_pre_norm", "w_in", "b_gate", "sinks", "w_br_a", "w_br_b", "w_br_c", "w_out",
             "attn_post_norm", "ffn_pre_norm", "w_up", "conv_w", "conv_b", "w_down", "ffn_post_norm"]
    return (loss, grad_x[None], *[out_g[k] for k in order], *[out_d[k] for k in order],
            *[out_m[k] for k in order], *[out_v[k] for k in order])
```

```python
import functools
import math

import numpy as np
import jax
import jax.numpy as jnp
from jax import lax
from jax.experimental import pallas as pl
from jax.experimental.pallas import tpu as pltpu

F32 = jnp.float32
BF16 = jnp.bfloat16

S = 2048
D = 1024
DEPTH = 2
NDEV = 8
HD = 64
BLK = 128
NB = S // BLK
A_GROUPS = ((128, 1), (512, 4), (2048, 16))
NUM_BUCKETS = 32
MAX_DISTANCE = 2048
N_BIAS_HEADS = 20
D_FF = 4096
IN_COLS = 6912
QKV_COLS = 3840
QKV_SLABS = QKV_COLS // 128
GATE_COLS = 3072
EPS = 1e-6
SCALE = HD ** -0.5
NEG = -1e30
LANES = 128

ADAM_LR = 0.001
ADAM_B1 = 0.9
ADAM_B2 = 0.999
ADAM_EPS = 1e-08
ADAM_WD = 0.01
ADAM_STEP = 10

VMEM_LIMIT = 56 * 1024 * 1024
MESH = pl.DeviceIdType.MESH
ANY = pl.BlockSpec(memory_space=pl.ANY)
SMEM = pl.BlockSpec(memory_space=pltpu.SMEM)


def _cp(*sem):
    return pltpu.CompilerParams(dimension_semantics=sem if sem else None, vmem_limit_bytes=VMEM_LIMIT)


def _dot(a, b, ca, cb):
    return lax.dot_general(a, b, (((ca,), (cb,)), ((), ())), preferred_element_type=F32)


def _mm(a, b, *, grid, a_spec, b_spec, out_shape, out_spec, ca, cb, acc_shape, name,
        a_slab=False, b_slab=False, out_slab=False, alias_out=None, after=None):
    nk = grid[2]

    def body(*refs):
        a_ref, b_ref = refs[0], refs[1]
        o_ref, acc_ref = refs[-2], refs[-1]
        k = pl.program_id(2)

        def load(ref, slab):
            if slab:
                return jnp.concatenate([ref[s] for s in range(ref.shape[0])], axis=1).astype(BF16)
            return ref[...].astype(BF16)

        def write(val):
            if out_slab:
                for s in range(o_ref.shape[0]):
                    o_ref[s] = val[:, s * LANES:(s + 1) * LANES].astype(o_ref.dtype)
            else:
                o_ref[...] = val.astype(o_ref.dtype)

        d = _dot(load(a_ref, a_slab), load(b_ref, b_slab), ca, cb)
        if nk == 1:
            write(d)
        elif direct:
            @pl.when(k == 0)
            def _():
                o_ref[...] = d

            @pl.when(k > 0)
            def _():
                o_ref[...] += d
        else:
            @pl.when(k == 0)
            def _():
                acc_ref[...] = d

            if nk > 2:
                @pl.when((k > 0) & (k < nk - 1))
                def _():
                    acc_ref[...] += d

            @pl.when(k == nk - 1)
            def _():
                write(acc_ref[...] + d)

    direct = (not out_slab) and out_shape.dtype == F32
    if nk == 1 or direct:
        acc_shape = (8, LANES)
    in_specs = [a_spec, b_spec]
    args = [a, b]
    aliases = {}
    if alias_out is not None:
        in_specs.append(ANY)
        args.append(alias_out)
        aliases = {2: 0}
    if after is not None:
        in_specs.append(ANY)
        args.append(after)
    return pl.pallas_call(
        body, grid=grid, in_specs=in_specs, out_specs=out_spec, out_shape=out_shape,
        scratch_shapes=[pltpu.VMEM(acc_shape, F32)], input_output_aliases=aliases,
        compiler_params=_cp("parallel", "parallel", "arbitrary"), name=name)(*args)


def _mm_nn(a, b, out_dtype, tm, tn, tk, name):
    m, kk = a.shape
    n = b.shape[1]
    return _mm(a, b, grid=(m // tm, n // tn, kk // tk),
               a_spec=pl.BlockSpec((tm, tk), lambda i, j, k: (i, k)),
               b_spec=pl.BlockSpec((tk, tn), lambda i, j, k: (k, j)),
               out_shape=jax.ShapeDtypeStruct((m, n), out_dtype),
               out_spec=pl.BlockSpec((tm, tn), lambda i, j, k: (i, j)),
               ca=1, cb=0, acc_shape=(tm, tn), name=name)


def _mm_nt(a, b, out_dtype, tm, tn, tk, name):
    m, kk = a.shape
    n = b.shape[0]
    return _mm(a, b, grid=(m // tm, n // tn, kk // tk),
               a_spec=pl.BlockSpec((tm, tk), lambda i, j, k: (i, k)),
               b_spec=pl.BlockSpec((tn, tk), lambda i, j, k: (j, k)),
               out_shape=jax.ShapeDtypeStruct((m, n), out_dtype),
               out_spec=pl.BlockSpec((tm, tn), lambda i, j, k: (i, j)),
               ca=1, cb=1, acc_shape=(tm, tn), name=name)


def _mm_tn(a, b, out_dtype, tm, tn, tk, name):
    kk, m = a.shape
    n = b.shape[1]
    return _mm(a, b, grid=(m // tm, n // tn, kk // tk),
               a_spec=pl.BlockSpec((tk, tm), lambda i, j, k: (k, i)),
               b_spec=pl.BlockSpec((tk, tn), lambda i, j, k: (k, j)),
               out_shape=jax.ShapeDtypeStruct((m, n), out_dtype),
               out_spec=pl.BlockSpec((tm, tn), lambda i, j, k: (i, j)),
               ca=0, cb=0, acc_shape=(tm, tn), name=name)


ROW_TILE = 256


def _rms(x, g):
    r = lax.rsqrt(jnp.mean(x * x, axis=-1, keepdims=True) + EPS)
    return x * r * g


def _prenorm(x, g, name):
    def body(x_ref, g_ref, o_ref):
        o_ref[...] = _rms(x_ref[...], g_ref[...]).astype(BF16)

    return pl.pallas_call(
        body, grid=(S // ROW_TILE,),
        in_specs=[pl.BlockSpec((ROW_TILE, D), lambda i: (i, 0)), pl.BlockSpec((1, D), lambda i: (0, 0))],
        out_specs=pl.BlockSpec((ROW_TILE, D), lambda i: (i, 0)),
        out_shape=jax.ShapeDtypeStruct((S, D), BF16), compiler_params=_cp("parallel"), name=name)(x, g)


def _postnorm_res(x, f, g_post, g_next, name):
    def body(x_ref, f_ref, gp_ref, gn_ref, xo_ref, ho_ref):
        xn = x_ref[...] + _rms(f_ref[...], gp_ref[...])
        xo_ref[...] = xn
        ho_ref[...] = _rms(xn, gn_ref[...]).astype(BF16)

    row = pl.BlockSpec((ROW_TILE, D), lambda i: (i, 0))
    vec = pl.BlockSpec((1, D), lambda i: (0, 0))
    return pl.pallas_call(
        body, grid=(S // ROW_TILE,), in_specs=[row, row, vec, vec], out_specs=[row, row],
        out_shape=[jax.ShapeDtypeStruct((S, D), F32), jax.ShapeDtypeStruct((S, D), BF16)],
        compiler_params=_cp("parallel"), name=name)(x, f, g_post, g_next)


def _norm_bwd(f, g, dys, res, out_dtype, name):
    ndy = len(dys)
    has_res = res is not None

    def body(*refs):
        f_ref, g_ref = refs[0], refs[1]
        dy_refs = refs[2:2 + ndy]
        res_ref = refs[2 + ndy] if has_res else None
        o_ref, dg_ref = refs[-2], refs[-1]
        fv = f_ref[...]
        dy = dy_refs[0][...].astype(F32)
        for r in dy_refs[1:]:
            dy = dy + r[...].astype(F32)
        r = lax.rsqrt(jnp.mean(fv * fv, axis=-1, keepdims=True) + EPS)
        n = fv * r
        dn = dy * g_ref[...]
        df = r * (dn - n * jnp.mean(dn * n, axis=-1, keepdims=True))
        if has_res:
            df = df + res_ref[...]
        o_ref[...] = df.astype(out_dtype)

        @pl.when(pl.program_id(0) == 0)
        def _():
            dg_ref[...] = jnp.zeros((1, D), F32)

        dg_ref[...] += jnp.sum(dy * n, axis=0, keepdims=True)

    row = pl.BlockSpec((ROW_TILE, D), lambda i: (i, 0))
    vec = pl.BlockSpec((1, D), lambda i: (0, 0))
    in_specs = [row, vec] + [row] * ndy + ([row] if has_res else [])
    args = [f, g] + list(dys) + ([res] if has_res else [])
    return pl.pallas_call(
        body, grid=(S // ROW_TILE,), in_specs=in_specs, out_specs=[row, vec],
        out_shape=[jax.ShapeDtypeStruct((S, D), out_dtype), jax.ShapeDtypeStruct((1, D), F32)],
        compiler_params=_cp("arbitrary"), name=name)(*args)


def _rms_bwd_rows(fv, g, dy):
    r = lax.rsqrt(jnp.mean(fv * fv, axis=-1, keepdims=True) + EPS)
    n = fv * r
    dn = dy * g
    return r * (dn - n * jnp.mean(dn * n, axis=-1, keepdims=True)), dy * n


def _norm_bwd_chain(f1, g1, dys, res, f2, g2, name):
    ndy = len(dys)

    def body(*refs):
        f1_ref, g1_ref = refs[0], refs[1]
        dy_refs = refs[2:2 + ndy]
        res_ref, f2_ref, g2_ref = refs[2 + ndy:5 + ndy]
        o1_ref, o2_ref, dg1_ref, dg2_ref = refs[-4:]
        dy = dy_refs[0][...].astype(F32)
        for r in dy_refs[1:]:
            dy = dy + r[...].astype(F32)
        df1, c1 = _rms_bwd_rows(f1_ref[...], g1_ref[...], dy)
        out1 = df1 + res_ref[...]
        o1_ref[...] = out1
        df2, c2 = _rms_bwd_rows(f2_ref[...], g2_ref[...], out1)
        o2_ref[...] = df2.astype(BF16)

        @pl.when(pl.program_id(0) == 0)
        def _():
            dg1_ref[...] = jnp.zeros((1, D), F32)
            dg2_ref[...] = jnp.zeros((1, D), F32)

        dg1_ref[...] += jnp.sum(c1, axis=0, keepdims=True)
        dg2_ref[...] += jnp.sum(c2, axis=0, keepdims=True)

    row = pl.BlockSpec((ROW_TILE, D), lambda i: (i, 0))
    vec = pl.BlockSpec((1, D), lambda i: (0, 0))
    return pl.pallas_call(
        body, grid=(S // ROW_TILE,), in_specs=[row, vec] + [row] * ndy + [row, row, vec],
        out_specs=[row, row, vec, vec],
        out_shape=[jax.ShapeDtypeStruct((S, D), F32), jax.ShapeDtypeStruct((S, D), BF16),
                   jax.ShapeDtypeStruct((1, D), F32), jax.ShapeDtypeStruct((1, D), F32)],
        compiler_params=_cp("arbitrary"), name=name)(f1, g1, *dys, res, f2, g2)


def _loss_head(y, target, name):
    def body(y_ref, t_ref, dy_ref, l_ref):
        e = y_ref[...] - t_ref[...]
        dy_ref[...] = e * (1.0 / D)

        @pl.when(pl.program_id(0) == 0)
        def _():
            l_ref[...] = jnp.zeros((8, LANES), F32)

        l_ref[...] += jnp.sum(e * e) * (0.5 / D)

    row = pl.BlockSpec((ROW_TILE, D), lambda i: (i, 0))
    return pl.pallas_call(
        body, grid=(S // ROW_TILE,), in_specs=[row, row],
        out_specs=[row, pl.BlockSpec((8, LANES), lambda i: (0, 0))],
        out_shape=[jax.ShapeDtypeStruct((S, D), F32), jax.ShapeDtypeStruct((8, LANES), F32)],
        compiler_params=_cp("arbitrary"), name=name)(y, target)


def _bucket_tiles():
    a = np.arange(BLK)[:, None]
    b = np.arange(2 * BLK)[None, :]
    dist = a + BLK - b
    out = np.zeros((4, 2, BLK, 2 * BLK), np.int32)
    cfg = [(w // d, d) for w, d in A_GROUPS] + [(BLK - 1, 1)]
    for gi, (max_dist, d) in enumerate(cfg):
        band = (dist >= 0) & (dist <= max_dist)
        tok = np.maximum(dist, 0) * d
        nf = np.maximum(tok, 1).astype(np.float32)
        max_exact = NUM_BUCKETS // 2
        large = max_exact + (np.log(nf / np.float32(max_exact)) / np.float32(math.log(MAX_DISTANCE / max_exact))
                             * np.float32(NUM_BUCKETS - max_exact)).astype(np.int32)
        large = np.minimum(large, NUM_BUCKETS - 1)
        bkt = np.where(tok < max_exact, tok, large).astype(np.int32)
        full = np.where(band, bkt, -1)
        out[gi, 1] = full
        out[gi, 0] = np.where(b >= BLK, full, -1)
    return out


def _bias_tiles(rel_bias, buckets, name):
    def body(tab_ref, bkt_ref, o_ref):
        h = pl.program_id(0)
        bkt = bkt_ref[...]
        acc = jnp.zeros(bkt.shape, F32)
        for bb in range(NUM_BUCKETS):
            acc = jnp.where(bkt == bb, tab_ref[bb, h], acc)
        o_ref[...] = jnp.where(bkt < 0, NEG, acc)

    return pl.pallas_call(
        body, grid=(N_BIAS_HEADS,),
        in_specs=[SMEM, pl.BlockSpec((None, 2, BLK, 2 * BLK), lambda h: (jnp.minimum(h // 4, 3), 0, 0, 0))],
        out_specs=pl.BlockSpec((None, 2, BLK, 2 * BLK), lambda h: (h, 0, 0, 0)),
        out_shape=jax.ShapeDtypeStruct((N_BIAS_HEADS, 2, BLK, 2 * BLK), F32),
        compiler_params=_cp("arbitrary"), name=name)(rel_bias, buckets)


def _bias_grad(gs, buckets, name):
    ng = len(gs)

    def body(*refs):
        g_refs = refs[:ng]
        bkt_ref, o_ref = refs[ng], refs[ng + 1]
        h = pl.program_id(0)
        g = g_refs[0][...]
        for r in g_refs[1:]:
            g = g + r[...]
        bkt = bkt_ref[...]
        row = lax.broadcasted_iota(jnp.int32, (NUM_BUCKETS, LANES), 0)
        lane = lax.broadcasted_iota(jnp.int32, (NUM_BUCKETS, LANES), 1)

        @pl.when(h == 0)
        def _():
            o_ref[...] = jnp.zeros((NUM_BUCKETS, LANES), F32)

        acc = o_ref[...]
        for bb in range(NUM_BUCKETS):
            s = jnp.sum(jnp.where(bkt == bb, g, 0.0))
            acc = jnp.where((row == bb) & (lane == h), s, acc)
        o_ref[...] = acc

    g_spec = pl.BlockSpec((None, BLK, 2 * BLK), lambda h: (h, 0, 0))
    return pl.pallas_call(
        body, grid=(N_BIAS_HEADS,),
        in_specs=[g_spec] * ng + [pl.BlockSpec((None, None, BLK, 2 * BLK), lambda h: (jnp.minimum(h // 4, 3), 1, 0, 0))],
        out_specs=pl.BlockSpec((NUM_BUCKETS, LANES), lambda h: (0, 0)),
        out_shape=jax.ShapeDtypeStruct((NUM_BUCKETS, LANES), F32),
        compiler_params=_cp("arbitrary"), name=name)(*gs, buckets)


def _to_class_major(src_ref, dst_refs, d, fn=None):
    ln = S // d
    for r in range(d):
        v = src_ref[pl.ds(r, ln, stride=d), :] if d > 1 else src_ref[...]
        outs = fn(v) if fn is not None else (v,) * len(dst_refs)
        for dst, o in zip(dst_refs, outs):
            dst[pl.ds(r * ln, ln), :] = o.astype(dst.dtype)


def _head_masks(rows):
    lane = lax.broadcasted_iota(jnp.int32, (rows, LANES), 1)
    return lane < HD, lane >= HD


def _split_heads(v):
    m0, m1 = _head_masks(v.shape[0])
    return jnp.where(m0, v, 0.0), jnp.where(m1, v, 0.0)


def _dup_head(v, hi):
    m0, _ = _head_masks(v.shape[0])
    r = pltpu.roll(v, HD, 1)
    return jnp.where(m0, jnp.where(hi, r, v), jnp.where(hi, v, r))


def _block_rows(b, d):
    nbc = NB // d
    i = b % nbc
    r = b // nbc
    has_prev = (i > 0).astype(jnp.int32)
    prev = pl.multiple_of(jnp.maximum(b - 1, 0) * BLK, BLK)
    nat = i * (BLK * d) + r
    return has_prev, prev, nat


def _lane_halves(v0, v1):
    lane = lax.broadcasted_iota(jnp.int32, (v0.shape[0], LANES), 1)
    return jnp.where(lane < HD, v0, v1)


def _band_fwd(proj, bias, sinks, *, d, q0, k0, v0, npairs, bias0, shared_kv, name):
    def body(sink_ref, q_ref, k_ref, v_ref, b_ref, num_ref, st_ref, qz0, qz1, ks, vs):
        p = pl.program_id(0)
        kv = (lambda v: (_dup_head(v, p >= 2),)) if shared_kv else None
        _to_class_major(q_ref, (qz0, qz1), d, lambda v: _split_heads(v * SCALE))
        _to_class_major(k_ref, (ks,), d, kv)
        _to_class_major(v_ref, (vs,), d, kv)
        lane = lax.broadcasted_iota(jnp.int32, (BLK, LANES), 1)

        def blk(b, carry):
            has_prev, prev, nat = _block_rows(b, d)
            cur = pl.multiple_of(b * BLK, BLK)
            k2 = jnp.concatenate([ks[pl.ds(prev, BLK), :], ks[pl.ds(cur, BLK), :]], axis=0)
            v2 = jnp.concatenate([vs[pl.ds(prev, BLK), :], vs[pl.ds(cur, BLK), :]], axis=0)
            nums, ms, ls = [], [], []
            for hh, qz in enumerate((qz0, qz1)):
                z = _dot(qz[pl.ds(cur, BLK), :], k2, 1, 1) + b_ref[hh, has_prev]
                m = jnp.max(z, axis=1, keepdims=True)
                e = jnp.exp(z - m)
                l = jnp.sum(e, axis=1, keepdims=True)
                num = _dot(e.astype(BF16), v2, 1, 0)
                if shared_kv:
                    sink = sink_ref[0, 2 * p + hh]
                    mx = jnp.maximum(m, sink)
                    c = jnp.exp(m - mx)
                    zden = l * c + jnp.exp(sink - mx)
                    num = num * (c / zden)
                    m = mx + jnp.log(zden)
                ls.append(l)
                ms.append(m)
                nums.append(num)
            num_t = jnp.where(lane < HD, nums[0], nums[1])
            if shared_kv:
                st_t = jnp.where(lane < HD, ms[0], ms[1])
            else:
                st_t = jnp.where(lane < 32, ms[0], jnp.where(lane < 64, ls[0], jnp.where(lane < 96, ms[1], ls[1])))
            if d > 1:
                num_ref[pl.ds(nat, BLK, stride=d), :] = num_t
                st_ref[pl.ds(nat, BLK, stride=d), :] = st_t
            else:
                num_ref[pl.ds(cur, BLK), :] = num_t
                st_ref[pl.ds(cur, BLK), :] = st_t
            return carry

        lax.fori_loop(0, NB, blk, 0, unroll=8)

    slab = lambda off, per_pair: pl.BlockSpec((None, S, LANES), (lambda p: (off + p, 0, 0)) if per_pair else (lambda p: (off, 0, 0)))
    out = pl.BlockSpec((None, S, LANES), lambda p: (p, 0, 0))
    return pl.pallas_call(
        body, grid=(npairs,),
        in_specs=[SMEM, slab(q0, True), slab(k0, not shared_kv), slab(v0, not shared_kv),
                  pl.BlockSpec((None, 2, 2, BLK, 2 * BLK), lambda p: (bias0 + p, 0, 0, 0, 0))],
        out_specs=[out, out],
        out_shape=[jax.ShapeDtypeStruct((npairs, S, LANES), F32)] * 2,
        scratch_shapes=[pltpu.VMEM((S, LANES), BF16)] * 4,
        compiler_params=_cp("arbitrary"), name=name)(sinks, proj, proj, proj, bias)


def _combine_a(nums, stats, name):
    rt = 512

    def body(n0, n1, n2, s0, s1, s2, o_ref, l_ref):
        n_refs, s_refs = (n0, n1, n2), (s0, s1, s2)
        outs, lses = [], []
        for hh in range(2):
            ms = [s[:, 64 * hh:64 * hh + 1] for s in s_refs]
            ls = [s[:, 64 * hh + 32:64 * hh + 33] for s in s_refs]
            mx = jnp.maximum(jnp.maximum(ms[0], ms[1]), ms[2])
            cs = [jnp.exp(m - mx) for m in ms]
            z = cs[0] * ls[0] + cs[1] * ls[1] + cs[2] * ls[2]
            acc = cs[0] * n_refs[0][:, hh * HD:(hh + 1) * HD]
            acc = acc + cs[1] * n_refs[1][:, hh * HD:(hh + 1) * HD]
            acc = acc + cs[2] * n_refs[2][:, hh * HD:(hh + 1) * HD]
            outs.append(acc / z)
            lses.append(mx + jnp.log(z))
        o_ref[...] = jnp.concatenate(outs, axis=1)
        l_ref[...] = _lane_halves(lses[0], lses[1])

    spec = pl.BlockSpec((None, rt, LANES), lambda p, i: (p, i, 0))
    return pl.pallas_call(
        body, grid=(2, S // rt), in_specs=[spec] * 6, out_specs=[spec, spec],
        out_shape=[jax.ShapeDtypeStruct((2, S, LANES), F32)] * 2,
        compiler_params=_cp("parallel", "parallel"), name=name)(*nums, *stats)


def _band_bwd(proj, bias, o, do, lse, sinks, *, d, q0, k0, v0, npairs, bias0, shared_kv, name):
    nkv = 1 if shared_kv else npairs

    def body(sink_ref, q_ref, k_ref, v_ref, b_ref, o_ref, do_ref, lse_ref,
             dq_ref, dk_ref, dv_ref, g_ref, ds_ref,
             qz0, qz1, ks, vs, doz0, doz1, ls0, ls1, dls0, dls1, stage, dq_nat, dk_cm, dv_cm, kv_nat, dk_acc, dv_acc):
        p = pl.program_id(0)
        m0, m1 = _head_masks(S)
        prod = do_ref[...] * o_ref[...]
        dl0 = jnp.sum(jnp.where(m0, prod, 0.0), axis=1, keepdims=True)
        dl1 = jnp.sum(jnp.where(m1, prod, 0.0), axis=1, keepdims=True)
        if shared_kv:
            row8 = lax.broadcasted_iota(jnp.int32, (8, LANES), 0)
            lane8 = lax.broadcasted_iota(jnp.int32, (8, LANES), 1)
            t = jnp.zeros((8, LANES), F32)
            lv = lse_ref[...]
            for hh in range(2):
                sink = sink_ref[0, 2 * p + hh]
                ps = jnp.exp(sink - lv[:, 64 * hh:64 * hh + 1])
                dsink = -jnp.sum(ps * (dl0 if hh == 0 else dl1))
                t = jnp.where((row8 == 0) & (lane8 == hh), dsink, t)
            ds_ref[...] = t
        else:
            ds_ref[...] = jnp.zeros((8, LANES), F32)
        kv = (lambda v: (_dup_head(v, p >= 2),)) if shared_kv else None
        _to_class_major(q_ref, (qz0, qz1), d, lambda v: _split_heads(v * SCALE))
        _to_class_major(k_ref, (ks,), d, kv)
        _to_class_major(v_ref, (vs,), d, kv)
        _to_class_major(do_ref, (doz0, doz1), d, _split_heads)
        def spread(v):
            a0, a1 = _head_masks(v.shape[0])
            r = pltpu.roll(v, HD, 1)
            return jnp.where(a0, v, r), jnp.where(a1, v, r)

        _to_class_major(lse_ref, (ls0, ls1), d, spread)
        stage[...] = jnp.where(m0, dl0, dl1)
        _to_class_major(stage, (dls0, dls1), d, spread)

        dk_cm[...] = jnp.zeros((S, LANES), F32)
        dv_cm[...] = jnp.zeros((S, LANES), F32)
        g_ref[...] = jnp.zeros((2, BLK, 2 * BLK), F32)
        lane = lax.broadcasted_iota(jnp.int32, (BLK, LANES), 1)

        def blk(b, carry):
            has_prev, prev, nat = _block_rows(b, d)
            cur = pl.multiple_of(b * BLK, BLK)
            k2 = jnp.concatenate([ks[pl.ds(prev, BLK), :], ks[pl.ds(cur, BLK), :]], axis=0)
            v2 = jnp.concatenate([vs[pl.ds(prev, BLK), :], vs[pl.ds(cur, BLK), :]], axis=0)
            dqs, dks, dvs = [], [], []
            for hh, (qz, doz, lsr, dlr) in enumerate(((qz0, doz0, ls0, dls0), (qz1, doz1, ls1, dls1))):
                qb = qz[pl.ds(cur, BLK), :]
                dob = doz[pl.ds(cur, BLK), :]
                lb = lsr[pl.ds(cur, BLK), :]
                dlb = dlr[pl.ds(cur, BLK), :]
                z = _dot(qb, k2, 1, 1) + b_ref[hh, has_prev]
                pr = jnp.exp(z - jnp.concatenate([lb, lb], axis=1))
                dp = _dot(dob, v2, 1, 1)
                dz = pr * (dp - jnp.concatenate([dlb, dlb], axis=1))
                g_ref[hh] += dz
                dzb = dz.astype(BF16)
                dqs.append(_dot(dzb, k2, 1, 0))
                dks.append(_dot(dzb, qb, 0, 0))
                dvs.append(_dot(pr.astype(BF16), dob, 0, 0))
            dq_t = jnp.where(lane < HD, dqs[0], dqs[1]) * SCALE
            dk_t = dks[0] + dks[1]
            dv_t = dvs[0] + dvs[1]
            dk_cm[pl.ds(prev, BLK), :] += dk_t[:BLK]
            dk_cm[pl.ds(cur, BLK), :] += dk_t[BLK:]
            dv_cm[pl.ds(prev, BLK), :] += dv_t[:BLK]
            dv_cm[pl.ds(cur, BLK), :] += dv_t[BLK:]
            if d > 1:
                dq_nat[pl.ds(nat, BLK, stride=d), :] = dq_t
            else:
                dq_nat[pl.ds(cur, BLK), :] = dq_t
            return carry

        lax.fori_loop(0, NB, blk, 0, unroll=8)
        dq_ref[...] = dq_nat[...].astype(BF16)

        def from_class_major(src, dst_ref):
            if d == 1:
                dst_ref[...] = src[...].astype(BF16)
            else:
                ln = S // d
                for r in range(d):
                    kv_nat[pl.ds(r, ln, stride=d), :] = src[pl.ds(r * ln, ln), :]
                dst_ref[...] = kv_nat[...].astype(BF16)

        if not shared_kv:
            from_class_major(dk_cm, dk_ref)
            from_class_major(dv_cm, dv_ref)
        else:
            @pl.when(p == 0)
            def _():
                dk_acc[...] = jnp.zeros((S, LANES), F32)
                dv_acc[...] = jnp.zeros((S, LANES), F32)

            mine = m1 == (p >= 2)
            for cm, acc in ((dk_cm, dk_acc), (dv_cm, dv_acc)):
                val = cm[...]
                acc[...] += jnp.where(mine, val + pltpu.roll(val, HD, 1), 0.0)

            @pl.when(p == npairs - 1)
            def _():
                from_class_major(dk_acc, dk_ref)
                from_class_major(dv_acc, dv_ref)

    slab = lambda off, per_pair: pl.BlockSpec((None, S, LANES), (lambda p: (off + p, 0, 0)) if per_pair else (lambda p: (off, 0, 0)))
    pair = pl.BlockSpec((None, S, LANES), lambda p: (p, 0, 0))
    kv_out = pair if not shared_kv else pl.BlockSpec((None, S, LANES), lambda p: (0, 0, 0))
    return pl.pallas_call(
        body, grid=(npairs,),
        in_specs=[SMEM, slab(q0, True), slab(k0, not shared_kv), slab(v0, not shared_kv),
                  pl.BlockSpec((None, 2, 2, BLK, 2 * BLK), lambda p: (bias0 + p, 0, 0, 0, 0)),
                  pair, pair, pair],
        out_specs=[pair, kv_out, kv_out,
                   pl.BlockSpec((None, 2, BLK, 2 * BLK), lambda p: (p, 0, 0, 0)),
                   pl.BlockSpec((None, 8, LANES), lambda p: (p, 0, 0))],
        out_shape=[jax.ShapeDtypeStruct((npairs, S, LANES), BF16),
                   jax.ShapeDtypeStruct((nkv, S, LANES), BF16),
                   jax.ShapeDtypeStruct((nkv, S, LANES), BF16),
                   jax.ShapeDtypeStruct((npairs, 2, BLK, 2 * BLK), F32),
                   jax.ShapeDtypeStruct((npairs, 8, LANES), F32)],
        scratch_shapes=[pltpu.VMEM((S, LANES), BF16)] * 6 + [pltpu.VMEM((S, LANES), F32)] * 11,
        compiler_params=_cp("arbitrary"), name=name)(sinks, proj, proj, proj, bias, o, do, lse)


KC = 512
NSUB = KC // BLK
QB = 512
QPG = KC // QB


def _split2(x):
    hi = x.astype(BF16)
    lo = (x - hi.astype(F32)).astype(BF16)
    return hi, lo


def _tri_ones(cmp):
    jj = lax.broadcasted_iota(jnp.int32, (2 * BLK, BLK), 0) % BLK
    ss = lax.broadcasted_iota(jnp.int32, (2 * BLK, BLK), 1)
    return jnp.concatenate([cmp(jj, ss).astype(BF16), jnp.ones((2 * BLK, BLK), BF16)], axis=1)


def _sub_sums(x, tri1):
    n = x.shape[0]
    st = jnp.concatenate([x[:, s * BLK:(s + 1) * BLK] for s in range(NSUB)], axis=0)
    hi, lo = _split2(st)
    r = _dot(jnp.concatenate([hi, lo], axis=1), tri1, 1, 0)
    return ([r[s * n:(s + 1) * n, :BLK] for s in range(NSUB)], [r[s * n:(s + 1) * n, BLK:] for s in range(NSUB)])


def _log_sig_pair(z):
    lb = jnp.minimum(z, 0.0) - jnp.log1p(jnp.exp(-jnp.abs(z)))
    return lb, lb - z


QGROUPS = NB // NSUB


def _stick_fwd(proj, *, q0, k0, v0, name):
    def body(q_ref, k_ref, v_ref, o_ref, t_ref, qs, ks, vs):
        qs[...] = (q_ref[...] * SCALE).astype(BF16)
        ks[...] = k_ref[...].astype(BF16)
        vs[...] = v_ref[...].astype(BF16)
        tri1 = _tri_ones(lambda j, s: j > s)
        col = lax.broadcasted_iota(jnp.int32, (QB, KC), 1)
        rowi = lax.broadcasted_iota(jnp.int32, (QB, KC), 0)

        for qg in range(QGROUPS):
            def qblock(ii, carry0, qg=qg):
                t0 = pl.multiple_of((qg * QPG + ii) * QB, QB)
                qb = qs[pl.ds(t0, QB), :]
                accs = [jnp.zeros((QB, HD), F32)] * 2
                runs = [jnp.zeros((QB, BLK), F32)] * 2
                for c in reversed(range(qg + 1)):
                    s0 = c * KC
                    diag = c == qg
                    before = (s0 + col) < (t0 + rowi) if diag else None
                    for hh in range(2):
                        kh = ks[s0:s0 + KC, hh * HD:(hh + 1) * HD]
                        vh = vs[s0:s0 + KC, hh * HD:(hh + 1) * HD]
                        lb, lk = _log_sig_pair(_dot(qb[:, hh * HD:(hh + 1) * HD], kh, 1, 1))
                        if diag:
                            lk = jnp.where(before, lk, 0.0)
                        suf, tot = _sub_sums(lk, tri1)
                        ws, run = [], runs[hh]
                        for s in reversed(range(NSUB)):
                            ws.append(jnp.exp(lb[:, s * BLK:(s + 1) * BLK] + suf[s] + run))
                            run = run + tot[s]
                        w = jnp.concatenate(ws[::-1], axis=1)
                        if diag:
                            w = jnp.where(before, w, 0.0)
                        accs[hh] = accs[hh] + _dot(w.astype(BF16), vh, 1, 0)
                        runs[hh] = run
                o_ref[pl.ds(t0, QB), :] = jnp.concatenate(accs, axis=1)
                t_ref[pl.ds(t0, QB), :] = _lane_halves(runs[0], runs[1])
                return carry0

            lax.fori_loop(0, QPG, qblock, 0)

    slab = lambda off: pl.BlockSpec((None, S, LANES), lambda p: (off + p, 0, 0))
    out = pl.BlockSpec((None, S, LANES), lambda p: (p, 0, 0))
    return pl.pallas_call(
        body, grid=(2,), in_specs=[slab(q0), slab(k0), slab(v0)], out_specs=[out, out],
        out_shape=[jax.ShapeDtypeStruct((2, S, LANES), F32)] * 2,
        scratch_shapes=[pltpu.VMEM((S, LANES), BF16)] * 3,
        compiler_params=_cp("arbitrary"), name=name)(proj, proj, proj)


def _stick_bwd(proj, do, tot, *, q0, k0, v0, name):
    def body(q_ref, k_ref, v_ref, do_ref, t_ref, dq_ref, dk_ref, dv_ref, qs, ks, vs, dos, dk_acc, dv_acc):
        qs[...] = (q_ref[...] * SCALE).astype(BF16)
        ks[...] = k_ref[...].astype(BF16)
        vs[...] = v_ref[...].astype(BF16)
        dos[...] = do_ref[...].astype(BF16)
        dk_acc[...] = jnp.zeros((2, S, HD), F32)
        dv_acc[...] = jnp.zeros((2, S, HD), F32)
        tri_inc = _tri_ones(lambda j, s: j <= s)
        tri_exc = _tri_ones(lambda j, s: j < s)
        col = lax.broadcasted_iota(jnp.int32, (QB, KC), 1)
        rowi = lax.broadcasted_iota(jnp.int32, (QB, KC), 0)

        for qg in range(QGROUPS):
            def qblock(ii, carry0, qg=qg):
                t0 = pl.multiple_of((qg * QPG + ii) * QB, QB)
                qb = qs[pl.ds(t0, QB), :]
                dob = dos[pl.ds(t0, QB), :]
                tb = t_ref[pl.ds(t0, QB), :]
                dqs = [jnp.zeros((QB, HD), F32)] * 2
                pruns = [jnp.zeros((QB, BLK), F32)] * 2
                eruns = [jnp.zeros((QB, BLK), F32)] * 2
                for c in range(qg + 1):
                    s0 = c * KC
                    diag = c == qg
                    before = (s0 + col) < (t0 + rowi) if diag else None
                    for hh in range(2):
                        qh = qb[:, hh * HD:(hh + 1) * HD]
                        doh = dob[:, hh * HD:(hh + 1) * HD]
                        tt = tb[:, 64 * hh:64 * hh + 1]
                        kh = ks[s0:s0 + KC, hh * HD:(hh + 1) * HD]
                        vh = vs[s0:s0 + KC, hh * HD:(hh + 1) * HD]
                        lb, lk = _log_sig_pair(_dot(qh, kh, 1, 1))
                        if diag:
                            lk = jnp.where(before, lk, 0.0)
                        pin, ptot = _sub_sums(lk, tri_inc)
                        ws, prun = [], pruns[hh]
                        for s in range(NSUB):
                            ws.append(jnp.exp(lb[:, s * BLK:(s + 1) * BLK] + (tt - (pin[s] + prun))))
                            prun = prun + ptot[s]
                        w = jnp.concatenate(ws, axis=1)
                        if diag:
                            w = jnp.where(before, w, 0.0)
                        e = w * _dot(doh, vh, 1, 1)
                        pex, etot = _sub_sums(e, tri_exc)
                        cs, erun = [], eruns[hh]
                        for s in range(NSUB):
                            cs.append(pex[s] + erun)
                            erun = erun + etot[s]
                        sig = jnp.exp(lb)
                        dz = e * (1.0 - sig) - jnp.concatenate(cs, axis=1) * sig
                        if diag:
                            dz = jnp.where(before, dz, 0.0)
                        dz = dz.astype(BF16)
                        dqs[hh] = dqs[hh] + _dot(dz, kh, 1, 0)
                        dk_acc[hh, s0:s0 + KC, :] += _dot(dz, qh, 0, 0)
                        dv_acc[hh, s0:s0 + KC, :] += _dot(w.astype(BF16), doh, 0, 0)
                        pruns[hh], eruns[hh] = prun, erun
                dq_ref[pl.ds(t0, QB), :] = (jnp.concatenate(dqs, axis=1) * SCALE).astype(BF16)
                return carry0

            lax.fori_loop(0, QPG, qblock, 0)
        dk_ref[...] = jnp.concatenate([dk_acc[0], dk_acc[1]], axis=1).astype(BF16)
        dv_ref[...] = jnp.concatenate([dv_acc[0], dv_acc[1]], axis=1).astype(BF16)

    slab = lambda off: pl.BlockSpec((None, S, LANES), lambda p: (off + p, 0, 0))
    pair = pl.BlockSpec((None, S, LANES), lambda p: (p, 0, 0))
    return pl.pallas_call(
        body, grid=(2,), in_specs=[slab(q0), slab(k0), slab(v0), pair, pair], out_specs=[pair] * 3,
        out_shape=[jax.ShapeDtypeStruct((2, S, LANES), BF16)] * 3,
        scratch_shapes=[pltpu.VMEM((S, LANES), BF16)] * 4 + [pltpu.VMEM((2, S, HD), F32)] * 2,
        compiler_params=_cp("arbitrary"), name=name)(proj, proj, proj, do, tot)


def _cat_slabs(ref):
    return jnp.concatenate([ref[s] for s in range(ref.shape[0])], axis=1)


def _merge_fwd(o_a, o_b, o_c, gates, b_gate, wa, wb, wc, w_out, name):
    tm = ROW_TILE

    def body(oa_ref, ob_ref, oc_ref, g_ref, bg_ref, wa_ref, wb_ref, wc_ref, wo_ref, mg_ref, mo_ref):
        acc = jnp.zeros((tm, D), F32)
        for i, (o_ref, w_ref) in enumerate(((oa_ref, wa_ref), (ob_ref, wb_ref), (oc_ref, wc_ref))):
            pr = _dot(_cat_slabs(o_ref).astype(BF16), w_ref[...], 1, 0)
            sg = jax.nn.sigmoid(g_ref[:, i * D:(i + 1) * D] + bg_ref[i:i + 1, :])
            acc = acc + sg * pr
        mg = acc.astype(BF16)
        mg_ref[...] = mg
        mo_ref[...] = _dot(mg, wo_ref[...], 1, 0)

    slabs = lambda n: pl.BlockSpec((n, tm, LANES), lambda i: (0, i, 0))
    full = lambda r, c: pl.BlockSpec((r, c), lambda i: (0, 0))
    row = pl.BlockSpec((tm, D), lambda i: (i, 0))
    return pl.pallas_call(
        body, grid=(S // tm,),
        in_specs=[slabs(2), slabs(4), slabs(2), pl.BlockSpec((tm, GATE_COLS), lambda i: (i, 0)), full(3, D),
                  full(256, D), full(512, D), full(256, D), full(D, D)],
        out_specs=[row, row],
        out_shape=[jax.ShapeDtypeStruct((S, D), BF16), jax.ShapeDtypeStruct((S, D), F32)],
        compiler_params=_cp("parallel"), name=name)(o_a, o_b, o_c, gates, b_gate, wa, wb, wc, w_out)


def _merge_bwd(d_mo, o_a, o_b, o_c, gates, b_gate, wa, wb, wc, w_out, name):
    tm = ROW_TILE

    def body(dmo_ref, oa_ref, ob_ref, oc_ref, g_ref, bg_ref, wa_ref, wb_ref, wc_ref, wo_ref,
             doa_ref, dob_ref, doc_ref, dg_ref, dwa_ref, dwb_ref, dwc_ref, dbg_ref):
        @pl.when(pl.program_id(0) == 0)
        def _():
            dwa_ref[...] = jnp.zeros(dwa_ref.shape, F32)
            dwb_ref[...] = jnp.zeros(dwb_ref.shape, F32)
            dwc_ref[...] = jnp.zeros(dwc_ref.shape, F32)
            dbg_ref[...] = jnp.zeros(dbg_ref.shape, F32)

        dmg = _dot(dmo_ref[...], wo_ref[...], 1, 1)
        trip = ((oa_ref, wa_ref, doa_ref, dwa_ref), (ob_ref, wb_ref, dob_ref, dwb_ref), (oc_ref, wc_ref, doc_ref, dwc_ref))
        for i, (o_ref, w_ref, do_ref, dw_ref) in enumerate(trip):
            ob = _cat_slabs(o_ref).astype(BF16)
            pr = _dot(ob, w_ref[...], 1, 0)
            sg = jax.nn.sigmoid(g_ref[:, i * D:(i + 1) * D] + bg_ref[i:i + 1, :])
            dgate = dmg * pr * sg * (1.0 - sg)
            dg_ref[:, i * D:(i + 1) * D] = dgate.astype(BF16)
            dbg_ref[i:i + 1, :] += jnp.sum(dgate, axis=0, keepdims=True)
            dpr = (dmg * sg).astype(BF16)
            do = _dot(dpr, w_ref[...], 1, 1)
            for s in range(do_ref.shape[0]):
                do_ref[s] = do[:, s * LANES:(s + 1) * LANES]
            dw_ref[...] += _dot(ob, dpr, 0, 0)

    slabs = lambda n: pl.BlockSpec((n, tm, LANES), lambda i: (0, i, 0))
    full = lambda r, c: pl.BlockSpec((r, c), lambda i: (0, 0))
    row = pl.BlockSpec((tm, D), lambda i: (i, 0))
    return pl.pallas_call(
        body, grid=(S // tm,),
        in_specs=[row, slabs(2), slabs(4), slabs(2), pl.BlockSpec((tm, GATE_COLS), lambda i: (i, 0)), full(3, D),
                  full(256, D), full(512, D), full(256, D), full(D, D)],
        out_specs=[slabs(2), slabs(4), slabs(2), pl.BlockSpec((tm, GATE_COLS), lambda i: (i, 0)),
                   full(256, D), full(512, D), full(256, D), full(3, D)],
        out_shape=[jax.ShapeDtypeStruct((2, S, LANES), F32), jax.ShapeDtypeStruct((4, S, LANES), F32),
                   jax.ShapeDtypeStruct((2, S, LANES), F32), jax.ShapeDtypeStruct((S, GATE_COLS), BF16),
                   jax.ShapeDtypeStruct((256, D), F32), jax.ShapeDtypeStruct((512, D), F32),
                   jax.ShapeDtypeStruct((256, D), F32), jax.ShapeDtypeStruct((3, D), F32)],
        compiler_params=_cp("arbitrary"), name=name)(d_mo, o_a, o_b, o_c, gates, b_gate, wa, wb, wc, w_out)


FC = 256
GELU_K = math.sqrt(2.0 / math.pi)
GELU_C = 0.044715


RC = 64
NRC = S // RC


def _down(tail, cur, n):
    row = lax.broadcasted_iota(jnp.int32, tail.shape, 0)
    rolled = pltpu.roll(cur, n, 0)
    first = jnp.where(row < n, pltpu.roll(tail, n, 0), rolled[0:8])
    return jnp.concatenate([first, rolled[8:]], axis=0)


def _up(cur, head, n):
    row = lax.broadcasted_iota(jnp.int32, head.shape, 0)
    rolled = pltpu.roll(cur, RC - n, 0)
    last = jnp.where(row >= 8 - n, pltpu.roll(head, 8 - n, 0), rolled[RC - 8:])
    return jnp.concatenate([rolled[:RC - 8], last], axis=0)


def _conv_chunk(load, j, w_ref, b_ref, half):
    r0 = pl.multiple_of(j * RC, RC)
    cur = load(r0, RC).astype(F32)
    tail = load(pl.multiple_of(jnp.maximum(r0 - 16, 0), 16), 16).astype(F32)[8:16]
    tail = jnp.where(j > 0, tail, 0.0)
    d1 = _down(tail, cur, 1)
    d2 = _down(tail, cur, 2)
    y = w_ref[0:1, half, :] * d2 + w_ref[1:2, half, :] * d1 + w_ref[2:3, half, :] * cur + b_ref[half:half + 1, :]
    return y, cur, d1, d2


def _chunk(j):
    return pl.ds(pl.multiple_of(j * RC, RC), RC)


def _fold8(x):
    return jnp.sum(x.reshape(RC // 8, 8, x.shape[-1]), axis=0)


def _ffn_act(u, conv_w, conv_b, name):
    def body(u_ref, w_ref, b_ref, a_ref, y_ref):
        def step(j, carry):
            yg = _conv_chunk(lambda r, n: u_ref[0, pl.ds(r, n), :], j, w_ref, b_ref, 0)[0]
            yv = _conv_chunk(lambda r, n: u_ref[1, pl.ds(r, n), :], j, w_ref, b_ref, 1)[0]
            th = jnp.tanh(GELU_K * (yg + GELU_C * yg * yg * yg))
            a_ref[_chunk(j), :] = (0.5 * yg * (1.0 + th) * yv).astype(BF16)
            y_ref[0, _chunk(j), :] = yg.astype(BF16)
            y_ref[1, _chunk(j), :] = yv.astype(BF16)
            return carry

        lax.fori_loop(0, NRC, step, 0)

    return pl.pallas_call(
        body, grid=(D_FF // FC,),
        in_specs=[pl.BlockSpec((2, S, FC), lambda j: (0, 0, j)), pl.BlockSpec((3, 2, FC), lambda j: (0, 0, j)),
                  pl.BlockSpec((2, FC), lambda j: (0, j))],
        out_specs=[pl.BlockSpec((S, FC), lambda j: (0, j)), pl.BlockSpec((2, S, FC), lambda j: (0, 0, j))],
        out_shape=[jax.ShapeDtypeStruct((S, D_FF), BF16), jax.ShapeDtypeStruct((2, S, D_FF), BF16)],
        compiler_params=_cp("parallel"), name=name)(u, conv_w, conv_b)


def _ffn_act_bwd(u, y, d_a, conv_w, name):
    def body(u_ref, y_ref, da_ref, w_ref, du_ref, dw_ref, db_ref, dy_s):
        def first(j, acc):
            yg = y_ref[0, _chunk(j), :].astype(F32)
            yv = y_ref[1, _chunk(j), :].astype(F32)
            th = jnp.tanh(GELU_K * (yg + GELU_C * yg * yg * yg))
            gelu = 0.5 * yg * (1.0 + th)
            dgelu = 0.5 * (1.0 + th) + 0.5 * yg * (1.0 - th * th) * GELU_K * (1.0 + 3.0 * GELU_C * yg * yg)
            da = da_ref[_chunk(j), :].astype(F32)
            dyg = da * yv * dgelu
            dyv = da * gelu
            dy_s[0, _chunk(j), :] = dyg
            dy_s[1, _chunk(j), :] = dyv
            return acc[0] + _fold8(dyg), acc[1] + _fold8(dyv)

        zero = jnp.zeros((8, FC), F32)
        accb = lax.fori_loop(0, NRC, first, (zero, zero))
        for half in range(2):
            db_ref[half:half + 1, :] = jnp.sum(accb[half], axis=0, keepdims=True)

        def second(j, acc):
            new = []
            for half in range(2):
                cur = dy_s[half, _chunk(j), :]
                h0 = pl.multiple_of(jnp.minimum((j + 1) * RC, S - 8), 8)
                head = jnp.where(j < NRC - 1, dy_s[half, pl.ds(h0, 8), :], 0.0)
                up1 = _up(cur, head, 1)
                up2 = _up(cur, head, 2)
                du = w_ref[2:3, half, :] * cur + w_ref[1:2, half, :] * up1 + w_ref[0:1, half, :] * up2
                du_ref[half, _chunk(j), :] = du.astype(BF16)
                uu = u_ref[half, _chunk(j), :].astype(F32)
                new += [_fold8(up2 * uu), _fold8(up1 * uu), _fold8(cur * uu)]
            return tuple(a + n for a, n in zip(acc, new))

        accw = lax.fori_loop(0, NRC, second, tuple(zero for _ in range(6)))
        for half in range(2):
            for k in range(3):
                dw_ref[k:k + 1, half, :] = jnp.sum(accw[3 * half + k], axis=0, keepdims=True)

    return pl.pallas_call(
        body, grid=(D_FF // FC,),
        in_specs=[pl.BlockSpec((2, S, FC), lambda j: (0, 0, j)), pl.BlockSpec((2, S, FC), lambda j: (0, 0, j)),
                  pl.BlockSpec((S, FC), lambda j: (0, j)), pl.BlockSpec((3, 2, FC), lambda j: (0, 0, j))],
        out_specs=[pl.BlockSpec((2, S, FC), lambda j: (0, 0, j)), pl.BlockSpec((3, 2, FC), lambda j: (0, 0, j)),
                   pl.BlockSpec((2, FC), lambda j: (0, j))],
        out_shape=[jax.ShapeDtypeStruct((2, S, D_FF), BF16), jax.ShapeDtypeStruct((3, 2, D_FF), F32),
                   jax.ShapeDtypeStruct((2, D_FF), F32)],
        scratch_shapes=[pltpu.VMEM((2, S, FC), F32)],
        compiler_params=_cp("parallel"), name=name)(u, y, d_a, conv_w)


def _layer_fwd(x, h1, w, bias, lname):
    n = lambda s: f"{lname}_{s}"
    w.need("in", h1)
    tn = 768
    proj = _mm(h1, w["w_in"], grid=(S // 1024, QKV_COLS // tn, 1),
               a_spec=pl.BlockSpec((1024, D), lambda i, j, k: (i, 0)),
               b_spec=pl.BlockSpec((tn, D), lambda i, j, k: (j, 0)),
               out_shape=jax.ShapeDtypeStruct((QKV_SLABS, S, LANES), F32),
               out_spec=pl.BlockSpec((tn // LANES, 1024, LANES), lambda i, j, k: (j, i, 0)),
               ca=1, cb=1, acc_shape=(1024, tn), out_slab=True, name=n("proj_qkv"))
    gates = _mm(h1, w["w_in"], grid=(S // 1024, GATE_COLS // tn, 1),
                a_spec=pl.BlockSpec((1024, D), lambda i, j, k: (i, 0)),
                b_spec=pl.BlockSpec((tn, D), lambda i, j, k: (j + QKV_COLS // tn, 0)),
                out_shape=jax.ShapeDtypeStruct((S, GATE_COLS), BF16),
                out_spec=pl.BlockSpec((1024, tn), lambda i, j, k: (i, j)),
                ca=1, cb=1, acc_shape=(1024, tn), name=n("proj_gate"))
    nums, stats = [], []
    for g, (_, d) in enumerate(A_GROUPS):
        nm, st = _band_fwd(proj, bias, w["sinks"], d=d, q0=2 * g, k0=6 + 2 * g, v0=12 + 2 * g, npairs=2, bias0=2 * g,
                           shared_kv=False, name=n(f"attn_a{g}_fwd"))
        nums.append(nm)
        stats.append(st)
    o_a, lse_a = _combine_a(nums, stats, n("attn_a_combine"))
    o_b, lse_b = _band_fwd(proj, bias, w["sinks"], d=1, q0=18, k0=22, v0=23, npairs=4, bias0=6, shared_kv=True,
                           name=n("attn_b_fwd"))
    o_c, tot_c = _stick_fwd(proj, q0=24, k0=26, v0=28, name=n("attn_c_fwd"))
    w.need("mix", tot_c)
    merged, mo = _merge_fwd(o_a, o_b, o_c, gates, w["b_gate"], w["w_br_a"], w["w_br_b"], w["w_br_c"], w["w_out"], n("merge_fwd"))
    x2, h2 = _postnorm_res(x, mo, w["attn_post_norm"], w["ffn_pre_norm"], n("attn_post"))
    w.need("ffn", h2)
    u = _mm(h2, w["w_up"], grid=(S // 1024, 2 * D_FF // 1024, 1),
            a_spec=pl.BlockSpec((1024, D), lambda i, j, k: (i, 0)),
            b_spec=pl.BlockSpec((D, 1024), lambda i, j, k: (0, j)),
            out_shape=jax.ShapeDtypeStruct((2, S, D_FF), BF16),
            out_spec=pl.BlockSpec((None, 1024, 1024), lambda i, j, k: (j // 4, i, j % 4)),
            ca=1, cb=0, acc_shape=(1024, 1024), name=n("ffn_up"))
    a, y = _ffn_act(u, w["conv_w"], w["conv_b"], n("ffn_act"))
    fo = _mm_nn(a, w["w_down"], F32, 1024, 1024, 2048, n("ffn_down"))
    saved = dict(x=x, h1=h1, proj=proj, gates=gates, o_a=o_a, lse_a=lse_a, o_b=o_b, lse_b=lse_b, o_c=o_c, tot_c=tot_c,
                 merged=merged, mo=mo, x2=x2, h2=h2, u=u, y=y, a=a, fo=fo)
    return saved


def _layer_bwd(dx3, sv, w, bias, lname, tok=None, on_part=None, d_fo=None, below=None):
    n = lambda s: f"{lname}_{s}"
    g = {}

    def part(group, vec):
        t = on_part(group, g) if on_part is not None else None
        return vec if t is None else vec + t

    if d_fo is None:
        gain = w["ffn_post_norm"] if tok is None else w["ffn_post_norm"] + tok
        d_fo, g["ffn_post_norm"] = _norm_bwd(sv["fo"], gain, [dx3], None, BF16, n("ffn_post_bwd"))
    else:
        d_fo, g["ffn_post_norm"] = d_fo
    d_a = _mm_nt(d_fo, w["w_down"], BF16, 1024, 1024, 1024, n("ffn_down_bwd_x"))
    g["w_down"] = _mm_tn(sv["a"], d_fo, BF16, 1024, 1024, S, n("ffn_down_bwd_w"))
    d_u, dcw, dcb = _ffn_act_bwd(sv["u"], sv["y"], d_a, w["conv_w"], n("ffn_act_bwd"))
    g["conv_w"] = dcw.reshape(3, 2 * D_FF)
    g["conv_b"] = dcb.reshape(1, 2 * D_FF)
    g["w_up"] = _mm(sv["h2"], d_u, grid=(1, 2 * D_FF // 1024, 1),
                    a_spec=pl.BlockSpec((S, D), lambda i, j, k: (k, 0)),
                    b_spec=pl.BlockSpec((None, S, 1024), lambda i, j, k: (j // 4, k, j % 4)),
                    out_shape=jax.ShapeDtypeStruct((D, 2 * D_FF), BF16),
                    out_spec=pl.BlockSpec((D, 1024), lambda i, j, k: (0, j)),
                    ca=0, cb=0, acc_shape=(D, 1024), name=n("ffn_up_bwd_w"))
    tok_ffn = on_part("ffn", g) if on_part is not None else None
    d_h2 = _mm(d_u, w["w_up"], grid=(S // 1024, 1, 2),
               a_spec=pl.BlockSpec((None, 1024, D_FF), lambda i, j, k: (k, i, 0)),
               b_spec=pl.BlockSpec((D, D_FF), lambda i, j, k: (0, k)),
               out_shape=jax.ShapeDtypeStruct((S, D), F32),
               out_spec=pl.BlockSpec((1024, D), lambda i, j, k: (i, 0)),
               ca=1, cb=1, acc_shape=(1024, D), after=tok_ffn, name=n("ffn_up_bwd_x"))
    dx2, d_mo, g["ffn_pre_norm"], g["attn_post_norm"] = _norm_bwd_chain(
        sv["x2"], w["ffn_pre_norm"], [d_h2], dx3, sv["mo"], w["attn_post_norm"], n("ffn_pre_attn_post_bwd"))
    g["w_out"] = _mm_tn(sv["merged"], d_mo, BF16, 1024, 1024, S, n("out_bwd_w"))
    do_a, do_b, do_c, d_gates, dwa, dwb, dwc, g["b_gate"] = _merge_bwd(
        d_mo, sv["o_a"], sv["o_b"], sv["o_c"], sv["gates"], w["b_gate"], w["w_br_a"], w["w_br_b"], w["w_br_c"],
        w["w_out"], n("merge_bwd"))
    g["w_br_a"], g["w_br_b"], g["w_br_c"] = dwa, dwb, dwc
    sinks = part("mix", w["sinks"])
    proj = sv["proj"]
    dqa, dka, dva, gbias = [], [], [], []
    for gi, (_, d) in enumerate(A_GROUPS):
        dq, dk, dv, gg, _ = _band_bwd(proj, bias, sv["o_a"], do_a, sv["lse_a"], sinks, d=d, q0=2 * gi, k0=6 + 2 * gi,
                                      v0=12 + 2 * gi, npairs=2, bias0=2 * gi, shared_kv=False, name=n(f"attn_a{gi}_bwd"))
        dqa.append(dq), dka.append(dk), dva.append(dv), gbias.append(gg)
    dqb, dkb, dvb, ggb, dsink = _band_bwd(proj, bias, sv["o_b"], do_b, sv["lse_b"], sinks, d=1, q0=18, k0=22, v0=23,
                                          npairs=4, bias0=6, shared_kv=True, name=n("attn_b_bwd"))
    gbias.append(ggb)
    g["bias_g"] = jnp.concatenate(gbias, axis=0).reshape(N_BIAS_HEADS, BLK, 2 * BLK)
    g["sinks"] = dsink[:, 0, :2].reshape(1, 8)
    dqc, dkc, dvc = _stick_bwd(proj, do_c, sv["tot_c"], q0=24, k0=26, v0=28, name=n("attn_c_bwd"))
    dqkv = jnp.concatenate(dqa + dka + dva + [dqb, dkb, dvb, dqc, dkc, dvc], axis=0)
    ts = 6
    tsx = QKV_SLABS
    dw_in = _mm(dqkv, sv["h1"], grid=(QKV_SLABS // ts, 1, 1),
                a_spec=pl.BlockSpec((ts, S, LANES), lambda i, j, k: (i, k, 0)),
                b_spec=pl.BlockSpec((S, D), lambda i, j, k: (k, 0)),
                out_shape=jax.ShapeDtypeStruct((IN_COLS, D), BF16),
                out_spec=pl.BlockSpec((ts * LANES, D), lambda i, j, k: (i, 0)),
                ca=0, cb=0, acc_shape=(ts * LANES, D), a_slab=True, name=n("in_bwd_w_qkv"))
    g["w_in"] = _mm(d_gates, sv["h1"], grid=(GATE_COLS // 768, 1, 1),
                    a_spec=pl.BlockSpec((S, 768), lambda i, j, k: (k, i)),
                    b_spec=pl.BlockSpec((S, D), lambda i, j, k: (k, 0)),
                    out_shape=jax.ShapeDtypeStruct((IN_COLS, D), BF16),
                    out_spec=pl.BlockSpec((768, D), lambda i, j, k: (i + QKV_COLS // 768, 0)),
                    ca=0, cb=0, acc_shape=(768, D), alias_out=dw_in, name=n("in_bwd_w_gate"))
    tok_in = on_part("in", g) if on_part is not None else None
    d_h1a = _mm(dqkv, w["w_in"], grid=(S // 1024, 1, QKV_SLABS // tsx),
                a_spec=pl.BlockSpec((tsx, 1024, LANES), lambda i, j, k: (k, i, 0)),
                b_spec=pl.BlockSpec((tsx * LANES, D), lambda i, j, k: (k, 0)),
                out_shape=jax.ShapeDtypeStruct((S, D), F32),
                out_spec=pl.BlockSpec((1024, D), lambda i, j, k: (i, 0)),
                ca=1, cb=0, acc_shape=(1024, D), a_slab=True, after=tok_in, name=n("in_bwd_x_qkv"))
    d_h1b = _mm(d_gates, w["w_in"], grid=(S // 1024, 1, GATE_COLS // 768),
                a_spec=pl.BlockSpec((1024, 768), lambda i, j, k: (i, k)),
                b_spec=pl.BlockSpec((768, D), lambda i, j, k: (k + QKV_COLS // 768, 0)),
                out_shape=jax.ShapeDtypeStruct((S, D), F32),
                out_spec=pl.BlockSpec((1024, D), lambda i, j, k: (i, 0)),
                ca=1, cb=0, acc_shape=(1024, D), after=tok_in, name=n("in_bwd_x_gate"))
    if below is None:
        dx, g["attn_pre_norm"] = _norm_bwd(sv["x"], w["attn_pre_norm"], [d_h1a, d_h1b], dx2, F32, n("attn_pre_bwd"))
        return dx, g, tok_in, None
    dx, d_fo_below, g["attn_pre_norm"], dg_below = _norm_bwd_chain(
        sv["x"], w["attn_pre_norm"], [d_h1a, d_h1b], dx2, below[0], below[1], n("attn_pre_ffn_post_bwd"))
    return dx, g, tok_in, (d_fo_below, dg_below)


def _local_step(x, target, ws, rel_bias, tok=None, on_grads=None):
    buckets = jnp.asarray(_bucket_tiles())
    bias = _bias_tiles(rel_bias, buckets, "bias_tiles").reshape(N_BIAS_HEADS // 2, 2, 2, BLK, 2 * BLK)
    saved = []
    gain0 = ws[0]["attn_pre_norm"] if tok is None else ws[0]["attn_pre_norm"] + tok
    h1 = _prenorm(x, gain0, "l0_attn_pre")
    for l in range(DEPTH):
        sv = _layer_fwd(x, h1, ws[l], bias, f"l{l}")
        saved.append(sv)
        g_next = ws[l + 1]["attn_pre_norm"] if l + 1 < DEPTH else ws[l]["attn_pre_norm"]
        x, h1 = _postnorm_res(sv["x2"], sv["fo"], ws[l]["ffn_post_norm"], g_next, f"l{l}_ffn_post")
    dy, loss_tile = _loss_head(x, target, "loss_head")
    grads = [None] * DEPTH
    tok, d_fo = None, None
    for l in reversed(range(DEPTH)):
        on_part = None if on_grads is None else functools.partial(on_grads, l)
        below = (saved[l - 1]["fo"], ws[l - 1]["ffn_post_norm"]) if l > 0 else None
        dy, grads[l], tok, d_fo = _layer_bwd(dy, saved[l], ws[l], bias, f"l{l}", tok, on_part, d_fo, below)
    g_rel = _bias_grad([grads[l]["bias_g"] for l in range(DEPTH)], buckets, "bias_grad")[:, :N_BIAS_HEADS]
    return loss_tile[0, 0], dy, grads, g_rel


def _coords():
    return lax.axis_index("x"), lax.axis_index("y"), lax.axis_index("c")


def _peer(rel):
    x, y, c = _coords()
    return (1 - x if rel & 4 else x, 1 - y if rel & 2 else y, 1 - c if rel & 1 else c)


def _exchange(srcs, dst_shapes, src_win, dst_win, name, after=None):
    nt = len(srcs)
    extra = [] if after is None else [after]

    def body(*refs):
        src_refs, dst_refs = refs[:nt], refs[nt + len(extra):2 * nt + len(extra)]
        send_sems, recv_sems, local_sems = refs[2 * nt + len(extra):]
        x, y, c = _coords()
        me = 4 * x + 2 * y + c
        locals_ = []
        for t in range(nt):
            cp = pltpu.make_async_copy(src_win(t, src_refs[t], me), dst_win(t, dst_refs[t], me), local_sems.at[t])
            cp.start()
            locals_.append(cp)
        sends = []
        for rel in range(1, NDEV):
            px, py, pc = _peer(rel)
            q = 4 * px + 2 * py + pc
            for t in range(nt):
                cp = pltpu.make_async_remote_copy(
                    src_ref=src_win(t, src_refs[t], q), dst_ref=dst_win(t, dst_refs[t], me),
                    send_sem=send_sems.at[rel - 1, t], recv_sem=recv_sems.at[rel - 1, t],
                    device_id=(px, py, pc), device_id_type=MESH)
                cp.start()
                sends.append(cp)
        for rel in range(1, NDEV):
            px, py, pc = _peer(rel)
            q = 4 * px + 2 * py + pc
            for t in range(nt):
                pltpu.make_async_remote_copy(
                    src_ref=src_win(t, src_refs[t], me), dst_ref=dst_win(t, dst_refs[t], q),
                    send_sem=send_sems.at[rel - 1, t], recv_sem=recv_sems.at[rel - 1, t],
                    device_id=(px, py, pc), device_id_type=MESH).wait_recv()
        for cp in sends:
            cp.wait_send()
        for cp in locals_:
            cp.wait()

    return pl.pallas_call(
        body, in_specs=[ANY] * (nt + len(extra)), out_specs=[ANY] * nt, out_shape=dst_shapes,
        scratch_shapes=[pltpu.SemaphoreType.DMA((NDEV - 1, nt)), pltpu.SemaphoreType.DMA((NDEV - 1, nt)),
                        pltpu.SemaphoreType.DMA((nt,))],
        name=name)(*srcs, *extra)


BIG = (("w_in", 0, 864), ("w_br_a", 1, 128), ("w_br_b", 1, 128), ("w_br_c", 1, 128), ("w_out", 0, 128),
       ("w_up", 1, 1024), ("w_down", 0, 512))


NBIG = len(BIG)
BIG_FULL = {"w_in": (IN_COLS, D), "w_br_a": (256, D), "w_br_b": (512, D), "w_br_c": (256, D), "w_out": (D, D),
            "w_up": (D, 2 * D_FF), "w_down": (D_FF, D)}
SHARD_ROWS = {"w_in": 288, "w_up": 256, "w_down": 256}
LAYER_GROUPS = (("in", (0,)), ("mix", (1, 2, 3, 4)), ("ffn", (5, 6)))

HBM_SPEC = pl.BlockSpec(memory_space=pltpu.HBM)
SEM_SPEC = pl.BlockSpec(memory_space=pltpu.SEMAPHORE)


def _hbm(a):
    return pltpu.with_memory_space_constraint(a, pltpu.HBM)


def _shard_window(t, ref, k):
    nm, ax, ext = BIG[t % NBIG]
    off = pl.multiple_of(k * ext, ext)
    if ax == 0:
        return ref.at[pl.ds(off, ext), :]
    return ref.at[:, pl.ds(off, ext)]


def _whole(t, ref, k):
    return ref


def _slot(t, ref, k):
    return ref.at[k]


def _own_block_spec(t, rows, me_of):
    nm, ax, ext = BIG[t % NBIG]
    r, c = BIG_FULL[nm]
    if ax == 0:
        return pl.BlockSpec((rows, c), lambda i, m: (me_of(m) * (ext // rows) + i, 0))
    return pl.BlockSpec((rows, ext), lambda i, m: (i, me_of(m)))


def _cast_own(t, shards, me_arr, name):
    nm, ax, ext = BIG[t % NBIG]
    layer = t // NBIG
    _, nr, nc = shards.shape
    rows = SHARD_ROWS.get(nm, nr)
    shape = BIG_FULL[nm]

    def body(m_ref, s_ref, o_ref):
        o_ref[...] = s_ref[...].astype(BF16)

    return pl.pallas_call(
        body, grid_spec=pltpu.PrefetchScalarGridSpec(
            num_scalar_prefetch=1, grid=(nr // rows,),
            in_specs=[pl.BlockSpec((None, rows, nc), lambda i, m: (layer, i, 0))],
            out_specs=_own_block_spec(t, rows, lambda m: m[0])),
        out_shape=jax.ShapeDtypeStruct(shape, BF16), compiler_params=_cp("arbitrary"), name=name)(me_arr, shards)


ALL_RELS = tuple(range(1, NDEV))
NEAR_RELS = (1, 2, 4, 6)
FAR_RELS = (2, 4, 6)


def _xchg_start(srcs, lands, groups, src_win, dst_win, after, name, rels=ALL_RELS, tids=None):
    ns = 0 if srcs is None else len(srcs)
    nt, ng = len(lands), len(groups)
    ins = ([] if srcs is None else list(srcs)) + list(lands)

    def body(*refs):
        src_refs, land_refs = refs[:ns], refs[ns:ns + nt]
        sems = refs[ns + nt + 1:ns + nt + 1 + 2 * ng]
        token = refs[-1]
        x, y, c = _coords()
        me = 4 * x + 2 * y + c
        for gi, grp in enumerate(groups):
            for j, t in enumerate(grp):
                tid = t if tids is None else tids[t]
                for ri, rel in enumerate(rels):
                    px, py, pc = _peer(rel)
                    q = 4 * px + 2 * py + pc
                    src = dst_win(tid, land_refs[t], me) if srcs is None else src_win(tid, src_refs[t], q)
                    pltpu.make_async_remote_copy(
                        src_ref=src, dst_ref=dst_win(tid, land_refs[t], me),
                        send_sem=sems[2 * gi].at[ri * len(grp) + j],
                        recv_sem=sems[2 * gi + 1].at[ri * len(grp) + j],
                        device_id=(px, py, pc), device_id_type=MESH).start()
        token[...] = jnp.zeros((8, LANES), F32)

    out_shape = []
    for grp in groups:
        out_shape += [pltpu.SemaphoreType.DMA((len(rels) * len(grp),))] * 2
    out_shape += [pltpu.HBM(a.shape, a.dtype) for a in ins]
    out_shape.append(jax.ShapeDtypeStruct((8, LANES), F32))
    outs = pl.pallas_call(
        body, in_specs=[HBM_SPEC] * len(ins) + [ANY],
        out_specs=[SEM_SPEC] * (2 * ng) + [HBM_SPEC] * len(ins) + [pl.BlockSpec(memory_space=pltpu.VMEM)],
        out_shape=out_shape, input_output_aliases={i: 2 * ng + i for i in range(len(ins))},
        compiler_params=pltpu.CompilerParams(has_side_effects=pltpu.SideEffectType.DATAFLOW_SIDE_EFFECTING),
        name=name)(*[_hbm(a) for a in ins], after)
    sems = [(outs[2 * gi], outs[2 * gi + 1]) for gi in range(ng)]
    thru = list(outs[2 * ng:2 * ng + len(ins)])
    return sems, (None if srcs is None else thru[:ns]), thru[ns:], outs[-1]


def _xchg_wait(sems, srcs, lands, tids, after, src_win, dst_win, name, rels=ALL_RELS):
    ns = 0 if srcs is None else len(srcs)
    n = len(lands)
    send_sem, recv_sem = sems
    ins = ([] if srcs is None else list(srcs)) + list(lands)

    def body(*refs):
        src_refs, land_refs = refs[:ns], refs[ns:ns + n]
        ssem, rsem = refs[ns + n], refs[ns + n + 1]
        x, y, c = _coords()
        me = 4 * x + 2 * y + c
        for j, t in enumerate(tids):
            for ri, rel in enumerate(rels):
                px, py, pc = _peer(rel)
                q = 4 * px + 2 * py + pc
                src = dst_win(t, land_refs[j], me) if srcs is None else src_win(t, src_refs[j], q)
                cp = pltpu.make_async_remote_copy(
                    src_ref=src, dst_ref=dst_win(t, land_refs[j], q),
                    send_sem=ssem.at[ri * n + j], recv_sem=rsem.at[ri * n + j],
                    device_id=(px, py, pc), device_id_type=MESH)
                cp.wait_send()
                cp.wait_recv()

    outs = pl.pallas_call(
        body, in_specs=[HBM_SPEC] * len(ins) + [SEM_SPEC, SEM_SPEC, ANY], out_specs=[HBM_SPEC] * len(ins),
        out_shape=[pltpu.HBM(a.shape, a.dtype) for a in ins],
        input_output_aliases={i: i for i in range(len(ins))},
        compiler_params=pltpu.CompilerParams(has_side_effects=pltpu.SideEffectType.DATAFLOW_SIDE_EFFECTING),
        name=name)(*ins, send_sem, recv_sem, after)
    return (None if srcs is None else list(outs[:ns])), list(outs[ns:])


def _gather_forward(sems_in, lands, groups, tids, after, dst_win, name):
    nt, ng = len(lands), len(groups)

    def body(*refs):
        land_refs = refs[:nt]
        in_sems = refs[nt:nt + 2 * ng]
        out_sems = refs[nt + 2 * ng + 1:nt + 4 * ng + 1]
        token = refs[-1]
        x, y, c = _coords()
        me = 4 * x + 2 * y + c
        sib = (x, y, 1 - c)
        for gi, grp in enumerate(groups):
            n = len(grp)
            for j, pos in enumerate(grp):
                t = tids[pos]
                for ri, rel in enumerate(NEAR_RELS):
                    px, py, pc = _peer(rel)
                    q = 4 * px + 2 * py + pc
                    cp = pltpu.make_async_remote_copy(
                        src_ref=dst_win(t, land_refs[pos], me), dst_ref=dst_win(t, land_refs[pos], q),
                        send_sem=in_sems[2 * gi].at[ri * n + j], recv_sem=in_sems[2 * gi + 1].at[ri * n + j],
                        device_id=(px, py, pc), device_id_type=MESH)
                    cp.wait_send()
                    cp.wait_recv()
            for j, pos in enumerate(grp):
                t = tids[pos]
                for fi, rel in enumerate(FAR_RELS):
                    px, py, pc = _peer(rel)
                    q = 4 * px + 2 * py + pc
                    win = dst_win(t, land_refs[pos], q)
                    pltpu.make_async_remote_copy(
                        src_ref=win, dst_ref=win,
                        send_sem=out_sems[2 * gi].at[fi * n + j], recv_sem=out_sems[2 * gi + 1].at[fi * n + j],
                        device_id=sib, device_id_type=MESH).start()
        token[...] = jnp.zeros((8, LANES), F32)

    out_shape = []
    for grp in groups:
        out_shape += [pltpu.SemaphoreType.DMA((len(FAR_RELS) * len(grp),))] * 2
    out_shape += [pltpu.HBM(a.shape, a.dtype) for a in lands]
    out_shape.append(jax.ShapeDtypeStruct((8, LANES), F32))
    flat_sems = [s for pair in sems_in for s in pair]
    outs = pl.pallas_call(
        body, in_specs=[HBM_SPEC] * nt + [SEM_SPEC] * (2 * ng) + [ANY],
        out_specs=[SEM_SPEC] * (2 * ng) + [HBM_SPEC] * nt + [pl.BlockSpec(memory_space=pltpu.VMEM)],
        out_shape=out_shape, input_output_aliases={i: 2 * ng + i for i in range(nt)},
        compiler_params=pltpu.CompilerParams(has_side_effects=pltpu.SideEffectType.DATAFLOW_SIDE_EFFECTING),
        name=name)(*[_hbm(a) for a in lands], *flat_sems, after)
    sems = [(outs[2 * gi], outs[2 * gi + 1]) for gi in range(ng)]
    return sems, list(outs[2 * ng:2 * ng + nt]), outs[-1]


class _Weights:
    def __init__(self, ready, pending=None):
        self.ready = dict(ready)
        self.pending = dict(pending or {})

    def __getitem__(self, k):
        return self.ready[k]

    def need(self, group, after):
        fn = self.pending.pop(group, None)
        if fn is not None:
            self.ready.update(fn(after))


def _adamw_math(w, g, m, v):
    m2 = ADAM_B1 * m + (1.0 - ADAM_B1) * g
    v2 = ADAM_B2 * v + (1.0 - ADAM_B2) * (g * g)
    m_hat = m2 / (1.0 - ADAM_B1 ** ADAM_STEP)
    v_hat = v2 / (1.0 - ADAM_B2 ** ADAM_STEP)
    delta = -ADAM_LR * (m_hat / (jnp.sqrt(v_hat) + ADAM_EPS) + ADAM_WD * w)
    return delta, m2, v2


def _adamw(t, parts, own, me_arr, w, m, v, layer, prev, rows, name):
    nl, nr, nc = w.shape

    def body(me_ref, p_ref, own_ref, w_ref, m_ref, v_ref, *rest):
        g_ref, d_ref, m2_ref, v2_ref = rest[-4:]
        me = me_ref[0]
        g = None
        for k in range(NDEV):
            term = jnp.where(me == k, own_ref[...], p_ref[k]).astype(F32)
            g = term if g is None else g + term
        delta, m2, v2 = _adamw_math(w_ref[...], g, m_ref[...], v_ref[...])
        g_ref[...] = g
        d_ref[...] = delta
        m2_ref[...] = m2
        v2_ref[...] = v2

    blk = pl.BlockSpec((None, rows, nc), lambda i, mm: (layer, i, 0))
    pblk = pl.BlockSpec((NDEV, rows, nc), lambda i, mm: (0, i, 0))
    extra = [] if prev is None else list(prev)
    return pl.pallas_call(
        body, grid_spec=pltpu.PrefetchScalarGridSpec(
            num_scalar_prefetch=1, grid=(nr // rows,),
            in_specs=[pblk, _own_block_spec(t, rows, lambda mm: mm[0]), blk, blk, blk] + [ANY] * len(extra),
            out_specs=[blk] * 4),
        out_shape=[jax.ShapeDtypeStruct(w.shape, F32)] * 4,
        input_output_aliases={6 + k: k for k in range(len(extra))},
        compiler_params=_cp("arbitrary"), name=name)(me_arr, parts, own, w, m, v, *extra)


def _pack(vecs):
    flat = jnp.concatenate([v.reshape(-1).astype(F32) for v in vecs])
    n = flat.shape[0]
    rows = -(-n // (8 * LANES)) * 8
    return jnp.pad(flat, (0, rows * LANES - n)).reshape(rows, LANES)


ROWPACK = (("rel_bias", 32, 32, (NUM_BUCKETS, N_BIAS_HEADS)), ("sinks", 8, 8, (DEPTH, 8)),
           ("attn_pre_norm", 16, 16, (DEPTH, D)), ("attn_post_norm", 16, 16, (DEPTH, D)),
           ("ffn_pre_norm", 16, 16, (DEPTH, D)), ("ffn_post_norm", 16, 16, (DEPTH, D)),
           ("conv_b", 128, 128, (DEPTH, 2 * D_FF)), ("b_gate", 48, 8, (DEPTH, 3, 128)),
           ("conv_w", 384, 48, (DEPTH, 3, 1024)))
ROWS_OWN = sum(r for _, _, r, _ in ROWPACK)
N_REPL = 7
ROWS_REPL = sum(r for _, _, r, _ in ROWPACK[:N_REPL])
ROWS_SHARD = ROWS_OWN - ROWS_REPL


def _as_rows(a, rows):
    a = a.astype(F32)
    if a.shape[-1] < LANES:
        a = jnp.pad(a.reshape(-1, a.shape[-1]), ((0, 0), (0, LANES - a.shape[-1])))
    a = a.reshape(-1, LANES)
    return jnp.pad(a, ((0, rows - a.shape[0]), (0, 0)))


def _rowpack(arrs, entries=ROWPACK):
    return jnp.concatenate([_as_rows(arrs[nm], ro) for nm, _, ro, _ in entries], axis=0)


def _shard_rows(g):
    bg = jnp.transpose(g["b_gate"].astype(F32).reshape(DEPTH * 3, NDEV, LANES), (1, 0, 2))
    bg = jnp.pad(bg, ((0, 0), (0, 8 - DEPTH * 3), (0, 0)))
    cw = jnp.transpose(g["conv_w"].astype(F32).reshape(DEPTH * 3, NDEV, 8, LANES), (1, 0, 2, 3))
    return jnp.concatenate([bg, cw.reshape(NDEV, DEPTH * 3 * 8, LANES)], axis=1)


def _small_update(parts_repl, parts_shard, w, m, v, name):
    nsm = len(ROWPACK)

    def body(pr_ref, ps_ref, w_ref, m_ref, v_ref, *rest):
        outs = rest[:4 * nsm]
        g_s, d_s, m_s, v_s = rest[4 * nsm:]
        gr, gs = pr_ref[0], ps_ref[0]
        for k in range(1, NDEV):
            gr = gr + pr_ref[k]
            gs = gs + ps_ref[k]
        g_s[0:ROWS_REPL, :] = gr
        g_s[ROWS_REPL:ROWS_OWN, :] = gs
        delta, m2, v2 = _adamw_math(w_ref[...], g_s[...], m_ref[...], v_ref[...])
        d_s[...] = delta
        m_s[...] = m2
        v_s[...] = v2
        for kind, src in enumerate((g_s, d_s, m_s, v_s)):
            oo = 0
            for idx, (nm, rf, ro, shp) in enumerate(ROWPACK):
                o_ref = outs[kind * nsm + idx]
                if nm in ("rel_bias", "sinks"):
                    o_ref[...] = src[oo:oo + shp[0], 0:shp[1]]
                elif nm == "b_gate":
                    for l in range(DEPTH):
                        o_ref[l] = src[oo + 3 * l:oo + 3 * l + 3, :]
                elif nm == "conv_w":
                    for l in range(DEPTH):
                        for k in range(8):
                            o_ref[l, :, k * LANES:(k + 1) * LANES] = src[pl.ds(oo + 24 * l + k, 3, stride=8), :]
                else:
                    per = shp[1] // LANES
                    for k in range(per):
                        o_ref[:, k * LANES:(k + 1) * LANES] = src[pl.ds(oo + k, DEPTH, stride=per), :]
                oo += ro

    vm = pl.BlockSpec(memory_space=pltpu.VMEM)
    shapes = [jax.ShapeDtypeStruct(shp, F32) for _ in range(4) for _, _, _, shp in ROWPACK]
    outs = pl.pallas_call(
        body, in_specs=[vm] * 5, out_specs=[vm] * (4 * nsm), out_shape=shapes,
        scratch_shapes=[pltpu.VMEM((ROWS_OWN, LANES), F32)] * 4,
        name=name)(parts_repl, parts_shard, w, m, v)
    names = [nm for nm, _, _, _ in ROWPACK]
    return [dict(zip(names, outs[kind * nsm:(kind + 1) * nsm])) for kind in range(4)]


def kernel(x, rel_bias, attn_pre_norm, w_in, b_gate, sinks, w_br_a, w_br_b, w_br_c, w_out, attn_post_norm, ffn_pre_norm, w_up, conv_w, conv_b, w_down, ffn_post_norm, loss_target, m_rel_bias, m_attn_pre_norm, m_w_in, m_b_gate, m_sinks, m_w_br_a, m_w_br_b, m_w_br_c, m_w_out, m_attn_post_norm, m_ffn_pre_norm, m_w_up, m_conv_w, m_conv_b, m_w_down, m_ffn_post_norm, v_rel_bias, v_attn_pre_norm, v_w_in, v_b_gate, v_sinks, v_w_br_a, v_w_br_b, v_w_br_c, v_w_out, v_attn_post_norm, v_ffn_pre_norm, v_w_up, v_conv_w, v_conv_b, v_w_down, v_ffn_post_norm):
    P = dict(rel_bias=rel_bias, attn_pre_norm=attn_pre_norm, w_in=w_in, b_gate=b_gate, sinks=sinks, w_br_a=w_br_a,
             w_br_b=w_br_b, w_br_c=w_br_c, w_out=w_out, attn_post_norm=attn_post_norm, ffn_pre_norm=ffn_pre_norm,
             w_up=w_up, conv_w=conv_w, conv_b=conv_b, w_down=w_down, ffn_post_norm=ffn_post_norm)
    M = dict(rel_bias=m_rel_bias, attn_pre_norm=m_attn_pre_norm, w_in=m_w_in, b_gate=m_b_gate, sinks=m_sinks,
             w_br_a=m_w_br_a, w_br_b=m_w_br_b, w_br_c=m_w_br_c, w_out=m_w_out, attn_post_norm=m_attn_post_norm,
             ffn_pre_norm=m_ffn_pre_norm, w_up=m_w_up, conv_w=m_conv_w, conv_b=m_conv_b, w_down=m_w_down,
             ffn_post_norm=m_ffn_post_norm)
    V = dict(rel_bias=v_rel_bias, attn_pre_norm=v_attn_pre_norm, w_in=v_w_in, b_gate=v_b_gate, sinks=v_sinks,
             w_br_a=v_w_br_a, w_br_b=v_w_br_b, w_br_c=v_w_br_c, w_out=v_w_out, attn_post_norm=v_attn_post_norm,
             ffn_pre_norm=v_ffn_pre_norm, w_up=v_w_up, conv_w=v_conv_w, conv_b=v_conv_b, w_down=v_w_down,
             ffn_post_norm=v_ffn_post_norm)
    tr = lambda a: jnp.swapaxes(a, 1, 2)
    PB = {nm: (tr(P[nm]) if nm == "w_in" else P[nm]) for nm, _, _ in BIG}
    MB = {nm: (tr(M[nm]) if nm == "w_in" else M[nm]) for nm, _, _ in BIG}
    VB = {nm: (tr(V[nm]) if nm == "w_in" else V[nm]) for nm, _, _ in BIG}
    xi, yi, ci = _coords()
    me = 4 * xi + 2 * yi + ci

    me_arr = me.astype(jnp.int32).reshape(1)

    small_w = _pack([b_gate.reshape(-1), conv_w.reshape(-1)])
    (small_w_all,) = _exchange([small_w], [jax.ShapeDtypeStruct((NDEV,) + small_w.shape, F32)],
                               _whole, _slot, "gather_small_weights")

    groups = [tuple(l * NBIG + t for t in tids) for l in range(DEPTH) for _, tids in LAYER_GROUPS]
    cast = lambda i, m=me_arr: _cast_own(i, PB[BIG[i % NBIG][0]], m, f"gather_own_l{i // NBIG}_{BIG[i % NBIG][0]}")
    first = list(groups[0])
    rest = [i for grp in groups[1:] for i in grp]
    sems0, _, lands0, tok_first = _xchg_start(None, [cast(i) for i in first], [tuple(range(len(first)))], None,
                                              _shard_window, small_w_all, "gather_start_first", rels=NEAR_RELS, tids=first)
    where_rest = {tid: k for k, tid in enumerate(rest)}
    me_rest = me_arr + tok_first[0, 0:1].astype(jnp.int32)
    sems1, _, lands1, g_tok = _xchg_start(None, [cast(i, me_rest) for i in rest],
                                          [tuple(where_rest[i] for i in grp) for grp in groups[1:]], None,
                                          _shard_window, lands0[0], "gather_start_rest", rels=NEAR_RELS, tids=rest)
    g_sems = list(sems0) + list(sems1)
    tok0 = g_tok[0:1, 0:1]
    lands_now = [None] * (DEPTH * NBIG)
    for i, a in zip(first + rest, list(lands0) + list(lands1)):
        lands_now[i] = a
    fwd_sems = {}
    fwd_plan = {0: (0,), 1: (1,), 2: (2,), 3: (3, 4, 5)}

    def gather_waiter(gi, l, gname, tids):
        def wait(after):
            if gi in fwd_plan:
                gis = fwd_plan[gi]
                flat = [i for g2 in gis for i in groups[g2]]
                where = {tid: k for k, tid in enumerate(flat)}
                fs, new_lands, ftok = _gather_forward(
                    [g_sems[g2] for g2 in gis], [lands_now[i] for i in flat],
                    [[where[i] for i in groups[g2]] for g2 in gis], flat, after, _shard_window, f"gather_forward_{gi}")
                for g2, s in zip(gis, fs):
                    fwd_sems[g2] = s
                for i, a in zip(flat, new_lands):
                    lands_now[i] = a
                after = ftok
            ids = [l * NBIG + t for t in tids]
            _, got = _xchg_wait(fwd_sems[gi], None, [lands_now[i] for i in ids], ids, after,
                                None, _shard_window, f"gather_wait_l{l}_{gname}", rels=FAR_RELS)
            out = {}
            for t, arr in zip(tids, got):
                nm = BIG[t][0]
                out[nm] = arr
            return out
        return wait

    pending = [{gname: gather_waiter(l * len(LAYER_GROUPS) + k, l, gname, tids)
                for k, (gname, tids) in enumerate(LAYER_GROUPS)} for l in range(DEPTH)]
    nbg = DEPTH * 3 * 128
    ncw = DEPTH * 3 * 1024
    flat_all = small_w_all.reshape(NDEV, -1)
    b_gate_full = jnp.transpose(flat_all[:, :nbg].reshape(NDEV, DEPTH, 3, 128), (1, 2, 0, 3)).reshape(DEPTH, 3, D)
    conv_w_full = jnp.transpose(flat_all[:, nbg:nbg + ncw].reshape(NDEV, DEPTH, 3, 1024), (1, 2, 0, 3)).reshape(DEPTH, 3, 2 * D_FF)

    ws = []
    for l in range(DEPTH):
        ws.append(_Weights(dict(
            b_gate=b_gate_full[l], conv_w=conv_w_full[l].reshape(3, 2, D_FF), conv_b=conv_b[l].reshape(2, D_FF),
            sinks=sinks[l].reshape(1, 8),
            attn_pre_norm=attn_pre_norm[l].reshape(1, D), attn_post_norm=attn_post_norm[l].reshape(1, D),
            ffn_pre_norm=ffn_pre_norm[l].reshape(1, D), ffn_post_norm=ffn_post_norm[l].reshape(1, D)), pending[l]))

    rs = {}

    group_tids = dict(LAYER_GROUPS)

    def start_scatter(l, gname, grads_l):
        tids = group_tids[gname]
        blocks, lands_rs = [], []
        for t in tids:
            nm, ax, ext = BIG[t]
            gfull = grads_l[nm].astype(BF16)
            shp = (NDEV, ext, gfull.shape[1]) if ax == 0 else (NDEV, gfull.shape[0], ext)
            blocks.append(gfull)
            lands_rs.append(lax.empty(shp, BF16))
        local = list(range(len(tids)))
        win = lambda j, ref, k: _shard_window(tids[j], ref, k)
        sems, s_thru, l_thru, tok = _xchg_start(blocks, lands_rs, [tuple(local)], win, _slot, me_arr,
                                                f"scatter_start_l{l}_{gname}")
        rs[(l, gname)] = (sems[0], s_thru, l_thru, win, local)
        return tok[0:1, 0:1]

    loss_local, grad_x, grads, g_rel = _local_step(x[0], loss_target[0], ws, rel_bias, tok0, start_scatter)
    loss = lax.psum(loss_local, ("x", "y", "c"))

    stack = lambda nm: jnp.stack([grads[l][nm] for l in range(DEPTH)], axis=0)
    small_g = {nm: (g_rel if nm == "rel_bias" else stack(nm)) for nm, _, _, _ in ROWPACK}
    small_repl = _rowpack(small_g, ROWPACK[:N_REPL])
    small_shard = _shard_rows(small_g)

    out_g, out_d, out_m, out_v = {}, {}, {}, {}
    prev = {nm: None for nm, _, _ in BIG}
    todo = [(l, gname) for l in reversed(range(DEPTH)) for gname in ("ffn", "mix", "in")]
    after, small_parts = grad_x, None
    for l, gname in todo:
        if (l, gname) == todo[-1]:
            small_parts = _exchange(
                [small_repl, small_shard],
                [jax.ShapeDtypeStruct((NDEV, ROWS_REPL, LANES), F32), jax.ShapeDtypeStruct((NDEV, ROWS_SHARD, LANES), F32)],
                lambda t, ref, q: ref if t == 0 else ref.at[q], _slot, "exchange_small_grads", after=after)
            after = small_parts[0]
        sems, s_thru, l_thru, win, local = rs[(l, gname)]
        owns, parts = _xchg_wait(sems, s_thru, l_thru, local, after, win, _slot, f"scatter_wait_l{l}_{gname}")
        for t, own, prt in zip(group_tids[gname], owns, parts):
            nm = BIG[t][0]
            rows = SHARD_ROWS.get(nm, PB[nm].shape[1])
            prev[nm] = _adamw(t, prt, own, me_arr, PB[nm], MB[nm], VB[nm], l, prev[nm], rows, f"adamw_{nm}_l{l}")
            after = prev[nm][1]
    for nm, _, _ in BIG:
        out_g[nm], out_d[nm], out_m[nm], out_v[nm] = [tr(a) if nm == "w_in" else a for a in prev[nm]]
    sm_g, sm_d, sm_m, sm_v = _small_update(small_parts[0], small_parts[1], _rowpack(P), _rowpack(M), _rowpack(V),
                                           "small_update")
    for dst, src in ((out_g, sm_g), (out_d, sm_d), (out_m, sm_m), (out_v, sm_v)):
        dst.update(src)

    order = ["rel_bias", "attn_pre_norm", "w_in", "b_gate", "sinks", "w_br_a", "w_br_b", "w_br_c", "w_out",
             "attn_post_norm", "ffn_pre_norm", "w_up", "conv_w", "conv_b", "w_down", "ffn_post_norm"]
    return (loss, grad_x[None], *[out_g[k] for k in order], *[out_d[k] for k in order],
            *[out_m[k] for k in order], *[out_v[k] for k in order])
```

```python
import functools
import math

import numpy as np
import jax
import jax.numpy as jnp
from jax import lax
from jax.experimental import pallas as pl
from jax.experimental.pallas import tpu as pltpu

F32 = jnp.float32
BF16 = jnp.bfloat16

S = 2048
D = 1024
DEPTH = 2
NDEV = 8
HD = 64
BLK = 128
NB = S // BLK
A_GROUPS = ((128, 1), (512, 4), (2048, 16))
NUM_BUCKETS = 32
MAX_DISTANCE = 2048
N_BIAS_HEADS = 20
D_FF = 4096
IN_COLS = 6912
QKV_COLS = 3840
QKV_SLABS = QKV_COLS // 128
GATE_COLS = 3072
EPS = 1e-6
SCALE = HD ** -0.5
NEG = -1e30
LANES = 128

ADAM_LR = 0.001
ADAM_B1 = 0.9
ADAM_B2 = 0.999
ADAM_EPS = 1e-08
ADAM_WD = 0.01
ADAM_STEP = 10

VMEM_LIMIT = 56 * 1024 * 1024
MESH = pl.DeviceIdType.MESH
ANY = pl.BlockSpec(memory_space=pl.ANY)
SMEM = pl.BlockSpec(memory_space=pltpu.SMEM)


def _cp(*sem):
    return pltpu.CompilerParams(dimension_semantics=sem if sem else None, vmem_limit_bytes=VMEM_LIMIT)


def _dot(a, b, ca, cb):
    return lax.dot_general(a, b, (((ca,), (cb,)), ((), ())), preferred_element_type=F32)


def _mm(a, b, *, grid, a_spec, b_spec, out_shape, out_spec, ca, cb, acc_shape, name,
        a_slab=False, b_slab=False, out_slab=False, alias_out=None, after=None):
    nk = grid[2]

    def body(*refs):
        a_ref, b_ref = refs[0], refs[1]
        o_ref, acc_ref = refs[-2], refs[-1]
        k = pl.program_id(2)

        def load(ref, slab):
            if slab:
                return jnp.concatenate([ref[s] for s in range(ref.shape[0])], axis=1).astype(BF16)
            return ref[...].astype(BF16)

        def write(val):
            if out_slab:
                for s in range(o_ref.shape[0]):
                    o_ref[s] = val[:, s * LANES:(s + 1) * LANES].astype(o_ref.dtype)
            else:
                o_ref[...] = val.astype(o_ref.dtype)

        d = _dot(load(a_ref, a_slab), load(b_ref, b_slab), ca, cb)
        if nk == 1:
            write(d)
        elif direct:
            @pl.when(k == 0)
            def _():
                o_ref[...] = d

            @pl.when(k > 0)
            def _():
                o_ref[...] += d
        else:
            @pl.when(k == 0)
            def _():
                acc_ref[...] = d

            if nk > 2:
                @pl.when((k > 0) & (k < nk - 1))
                def _():
                    acc_ref[...] += d

            @pl.when(k == nk - 1)
            def _():
                write(acc_ref[...] + d)

    direct = (not out_slab) and out_shape.dtype == F32
    if nk == 1 or direct:
        acc_shape = (8, LANES)
    in_specs = [a_spec, b_spec]
    args = [a, b]
    aliases = {}
    if alias_out is not None:
        in_specs.append(ANY)
        args.append(alias_out)
        aliases = {2: 0}
    if after is not None:
        in_specs.append(ANY)
        args.append(after)
    return pl.pallas_call(
        body, grid=grid, in_specs=in_specs, out_specs=out_spec, out_shape=out_shape,
        scratch_shapes=[pltpu.VMEM(acc_shape, F32)], input_output_aliases=aliases,
        compiler_params=_cp("parallel", "parallel", "arbitrary"), name=name)(*args)


def _mm_nn(a, b, out_dtype, tm, tn, tk, name):
    m, kk = a.shape
    n = b.shape[1]
    return _mm(a, b, grid=(m // tm, n // tn, kk // tk),
               a_spec=pl.BlockSpec((tm, tk), lambda i, j, k: (i, k)),
               b_spec=pl.BlockSpec((tk, tn), lambda i, j, k: (k, j)),
               out_shape=jax.ShapeDtypeStruct((m, n), out_dtype),
               out_spec=pl.BlockSpec((tm, tn), lambda i, j, k: (i, j)),
               ca=1, cb=0, acc_shape=(tm, tn), name=name)


def _mm_nt(a, b, out_dtype, tm, tn, tk, name):
    m, kk = a.shape
    n = b.shape[0]
    return _mm(a, b, grid=(m // tm, n // tn, kk // tk),
               a_spec=pl.BlockSpec((tm, tk), lambda i, j, k: (i, k)),
               b_spec=pl.BlockSpec((tn, tk), lambda i, j, k: (j, k)),
               out_shape=jax.ShapeDtypeStruct((m, n), out_dtype),
               out_spec=pl.BlockSpec((tm, tn), lambda i, j, k: (i, j)),
               ca=1, cb=1, acc_shape=(tm, tn), name=name)


def _mm_tn(a, b, out_dtype, tm, tn, tk, name):
    kk, m = a.shape
    n = b.shape[1]
    return _mm(a, b, grid=(m // tm, n // tn, kk // tk),
               a_spec=pl.BlockSpec((tk, tm), lambda i, j, k: (k, i)),
               b_spec=pl.BlockSpec((tk, tn), lambda i, j, k: (k, j)),
               out_shape=jax.ShapeDtypeStruct((m, n), out_dtype),
               out_spec=pl.BlockSpec((tm, tn), lambda i, j, k: (i, j)),
               ca=0, cb=0, acc_shape=(tm, tn), name=name)


ROW_TILE = 256


def _rms(x, g):
    r = lax.rsqrt(jnp.mean(x * x, axis=-1, keepdims=True) + EPS)
    return x * r * g


def _prenorm(x, g, name):
    def body(x_ref, g_ref, o_ref):
        o_ref[...] = _rms(x_ref[...], g_ref[...]).astype(BF16)

    return pl.pallas_call(
        body, grid=(S // ROW_TILE,),
        in_specs=[pl.BlockSpec((ROW_TILE, D), lambda i: (i, 0)), pl.BlockSpec((1, D), lambda i: (0, 0))],
        out_specs=pl.BlockSpec((ROW_TILE, D), lambda i: (i, 0)),
        out_shape=jax.ShapeDtypeStruct((S, D), BF16), compiler_params=_cp("parallel"), name=name)(x, g)


def _postnorm_res(x, f, g_post, g_next, name):
    def body(x_ref, f_ref, gp_ref, gn_ref, xo_ref, ho_ref):
        xn = x_ref[...] + _rms(f_ref[...], gp_ref[...])
        xo_ref[...] = xn
        ho_ref[...] = _rms(xn, gn_ref[...]).astype(BF16)

    row = pl.BlockSpec((ROW_TILE, D), lambda i: (i, 0))
    vec = pl.BlockSpec((1, D), lambda i: (0, 0))
    return pl.pallas_call(
        body, grid=(S // ROW_TILE,), in_specs=[row, row, vec, vec], out_specs=[row, row],
        out_shape=[jax.ShapeDtypeStruct((S, D), F32), jax.ShapeDtypeStruct((S, D), BF16)],
        compiler_params=_cp("parallel"), name=name)(x, f, g_post, g_next)


def _norm_bwd(f, g, dys, res, out_dtype, name):
    ndy = len(dys)
    has_res = res is not None

    def body(*refs):
        f_ref, g_ref = refs[0], refs[1]
        dy_refs = refs[2:2 + ndy]
        res_ref = refs[2 + ndy] if has_res else None
        o_ref, dg_ref = refs[-2], refs[-1]
        fv = f_ref[...]
        dy = dy_refs[0][...].astype(F32)
        for r in dy_refs[1:]:
            dy = dy + r[...].astype(F32)
        r = lax.rsqrt(jnp.mean(fv * fv, axis=-1, keepdims=True) + EPS)
        n = fv * r
        dn = dy * g_ref[...]
        df = r * (dn - n * jnp.mean(dn * n, axis=-1, keepdims=True))
        if has_res:
            df = df + res_ref[...]
        o_ref[...] = df.astype(out_dtype)

        @pl.when(pl.program_id(0) == 0)
        def _():
            dg_ref[...] = jnp.zeros((1, D), F32)

        dg_ref[...] += jnp.sum(dy * n, axis=0, keepdims=True)

    row = pl.BlockSpec((ROW_TILE, D), lambda i: (i, 0))
    vec = pl.BlockSpec((1, D), lambda i: (0, 0))
    in_specs = [row, vec] + [row] * ndy + ([row] if has_res else [])
    args = [f, g] + list(dys) + ([res] if has_res else [])
    return pl.pallas_call(
        body, grid=(S // ROW_TILE,), in_specs=in_specs, out_specs=[row, vec],
        out_shape=[jax.ShapeDtypeStruct((S, D), out_dtype), jax.ShapeDtypeStruct((1, D), F32)],
        compiler_params=_cp("arbitrary"), name=name)(*args)


def _rms_bwd_rows(fv, g, dy):
    r = lax.rsqrt(jnp.mean(fv * fv, axis=-1, keepdims=True) + EPS)
    n = fv * r
    dn = dy * g
    return r * (dn - n * jnp.mean(dn * n, axis=-1, keepdims=True)), dy * n


def _norm_bwd_chain(f1, g1, dys, res, f2, g2, name):
    ndy = len(dys)

    def body(*refs):
        f1_ref, g1_ref = refs[0], refs[1]
        dy_refs = refs[2:2 + ndy]
        res_ref, f2_ref, g2_ref = refs[2 + ndy:5 + ndy]
        o1_ref, o2_ref, dg1_ref, dg2_ref = refs[-4:]
        dy = dy_refs[0][...].astype(F32)
        for r in dy_refs[1:]:
            dy = dy + r[...].astype(F32)
        df1, c1 = _rms_bwd_rows(f1_ref[...], g1_ref[...], dy)
        out1 = df1 + res_ref[...]
        o1_ref[...] = out1
        df2, c2 = _rms_bwd_rows(f2_ref[...], g2_ref[...], out1)
        o2_ref[...] = df2.astype(BF16)

        @pl.when(pl.program_id(0) == 0)
        def _():
            dg1_ref[...] = jnp.zeros((1, D), F32)
            dg2_ref[...] = jnp.zeros((1, D), F32)

        dg1_ref[...] += jnp.sum(c1, axis=0, keepdims=True)
        dg2_ref[...] += jnp.sum(c2, axis=0, keepdims=True)

    row = pl.BlockSpec((ROW_TILE, D), lambda i: (i, 0))
    vec = pl.BlockSpec((1, D), lambda i: (0, 0))
    return pl.pallas_call(
        body, grid=(S // ROW_TILE,), in_specs=[row, vec] + [row] * ndy + [row, row, vec],
        out_specs=[row, row, vec, vec],
        out_shape=[jax.ShapeDtypeStruct((S, D), F32), jax.ShapeDtypeStruct((S, D), BF16),
                   jax.ShapeDtypeStruct((1, D), F32), jax.ShapeDtypeStruct((1, D), F32)],
        compiler_params=_cp("arbitrary"), name=name)(f1, g1, *dys, res, f2, g2)


def _loss_head(x, f, g, target, name):
    def body(x_ref, f_ref, g_ref, t_ref, dy_ref, l_ref, df_ref, dg_ref):
        fv = f_ref[...]
        e = x_ref[...] + _rms(fv, g_ref[...]) - t_ref[...]
        dy = e * (1.0 / D)
        dy_ref[...] = dy
        df, c = _rms_bwd_rows(fv, g_ref[...], dy)
        df_ref[...] = df.astype(BF16)

        @pl.when(pl.program_id(0) == 0)
        def _():
            l_ref[...] = jnp.zeros((8, LANES), F32)
            dg_ref[...] = jnp.zeros((1, D), F32)

        l_ref[...] += jnp.sum(e * e) * (0.5 / D)
        dg_ref[...] += jnp.sum(c, axis=0, keepdims=True)

    row = pl.BlockSpec((ROW_TILE, D), lambda i: (i, 0))
    vec = pl.BlockSpec((1, D), lambda i: (0, 0))
    return pl.pallas_call(
        body, grid=(S // ROW_TILE,), in_specs=[row, row, vec, row],
        out_specs=[row, pl.BlockSpec((8, LANES), lambda i: (0, 0)), row, vec],
        out_shape=[jax.ShapeDtypeStruct((S, D), F32), jax.ShapeDtypeStruct((8, LANES), F32),
                   jax.ShapeDtypeStruct((S, D), BF16), jax.ShapeDtypeStruct((1, D), F32)],
        compiler_params=_cp("arbitrary"), name=name)(x, f, g, target)


def _bucket_tiles():
    a = np.arange(BLK)[:, None]
    b = np.arange(2 * BLK)[None, :]
    dist = a + BLK - b
    out = np.zeros((4, 2, BLK, 2 * BLK), np.int32)
    cfg = [(w // d, d) for w, d in A_GROUPS] + [(BLK - 1, 1)]
    for gi, (max_dist, d) in enumerate(cfg):
        band = (dist >= 0) & (dist <= max_dist)
        tok = np.maximum(dist, 0) * d
        nf = np.maximum(tok, 1).astype(np.float32)
        max_exact = NUM_BUCKETS // 2
        large = max_exact + (np.log(nf / np.float32(max_exact)) / np.float32(math.log(MAX_DISTANCE / max_exact))
                             * np.float32(NUM_BUCKETS - max_exact)).astype(np.int32)
        large = np.minimum(large, NUM_BUCKETS - 1)
        bkt = np.where(tok < max_exact, tok, large).astype(np.int32)
        full = np.where(band, bkt, -1)
        out[gi, 1] = full
        out[gi, 0] = np.where(b >= BLK, full, -1)
    return out


def _bias_tiles(rel_bias, buckets, name):
    def body(tab_ref, bkt_ref, o_ref):
        h = pl.program_id(0)
        bkt = bkt_ref[...]
        acc = jnp.zeros(bkt.shape, F32)
        for bb in range(NUM_BUCKETS):
            acc = jnp.where(bkt == bb, tab_ref[bb, h], acc)
        o_ref[...] = jnp.where(bkt < 0, NEG, acc)

    return pl.pallas_call(
        body, grid=(N_BIAS_HEADS,),
        in_specs=[SMEM, pl.BlockSpec((None, 2, BLK, 2 * BLK), lambda h: (jnp.minimum(h // 4, 3), 0, 0, 0))],
        out_specs=pl.BlockSpec((None, 2, BLK, 2 * BLK), lambda h: (h, 0, 0, 0)),
        out_shape=jax.ShapeDtypeStruct((N_BIAS_HEADS, 2, BLK, 2 * BLK), F32),
        compiler_params=_cp("arbitrary"), name=name)(rel_bias, buckets)


def _bias_grad(gs, buckets, name):
    ng = len(gs)

    def body(*refs):
        g_refs = refs[:ng]
        bkt_ref, o_ref = refs[ng], refs[ng + 1]
        h = pl.program_id(0)
        g = g_refs[0][...]
        for r in g_refs[1:]:
            g = g + r[...]
        bkt = bkt_ref[...]
        row = lax.broadcasted_iota(jnp.int32, (NUM_BUCKETS, LANES), 0)
        lane = lax.broadcasted_iota(jnp.int32, (NUM_BUCKETS, LANES), 1)

        @pl.when(h == 0)
        def _():
            o_ref[...] = jnp.zeros((NUM_BUCKETS, LANES), F32)

        acc = o_ref[...]
        for bb in range(NUM_BUCKETS):
            s = jnp.sum(jnp.where(bkt == bb, g, 0.0))
            acc = jnp.where((row == bb) & (lane == h), s, acc)
        o_ref[...] = acc

    g_spec = pl.BlockSpec((None, BLK, 2 * BLK), lambda h: (h, 0, 0))
    return pl.pallas_call(
        body, grid=(N_BIAS_HEADS,),
        in_specs=[g_spec] * ng + [pl.BlockSpec((None, None, BLK, 2 * BLK), lambda h: (jnp.minimum(h // 4, 3), 1, 0, 0))],
        out_specs=pl.BlockSpec((NUM_BUCKETS, LANES), lambda h: (0, 0)),
        out_shape=jax.ShapeDtypeStruct((NUM_BUCKETS, LANES), F32),
        compiler_params=_cp("arbitrary"), name=name)(*gs, buckets)


def _to_class_major(src_ref, dst_refs, d, fn=None):
    ln = S // d
    for r in range(d):
        v = src_ref[pl.ds(r, ln, stride=d), :] if d > 1 else src_ref[...]
        outs = fn(v) if fn is not None else (v,) * len(dst_refs)
        for dst, o in zip(dst_refs, outs):
            dst[pl.ds(r * ln, ln), :] = o.astype(dst.dtype)


def _head_masks(rows):
    lane = lax.broadcasted_iota(jnp.int32, (rows, LANES), 1)
    return lane < HD, lane >= HD


def _split_heads(v):
    m0, m1 = _head_masks(v.shape[0])
    return jnp.where(m0, v, 0.0), jnp.where(m1, v, 0.0)


def _dup_head(v, hi):
    m0, _ = _head_masks(v.shape[0])
    r = pltpu.roll(v, HD, 1)
    return jnp.where(m0, jnp.where(hi, r, v), jnp.where(hi, v, r))


def _block_rows(b, d):
    nbc = NB // d
    i = b % nbc
    r = b // nbc
    has_prev = (i > 0).astype(jnp.int32)
    prev = pl.multiple_of(jnp.maximum(b - 1, 0) * BLK, BLK)
    nat = i * (BLK * d) + r
    return has_prev, prev, nat


def _lane_halves(v0, v1):
    lane = lax.broadcasted_iota(jnp.int32, (v0.shape[0], LANES), 1)
    return jnp.where(lane < HD, v0, v1)


def _band_fwd(proj, bias, sinks, *, d, q0, k0, v0, npairs, bias0, shared_kv, name):
    def body(sink_ref, q_ref, k_ref, v_ref, b_ref, num_ref, st_ref, qz0, qz1, ks, vs):
        p = pl.program_id(0)
        kv = (lambda v: (_dup_head(v, p >= 2),)) if shared_kv else None
        _to_class_major(q_ref, (qz0, qz1), d, lambda v: _split_heads(v * SCALE))
        _to_class_major(k_ref, (ks,), d, kv)
        _to_class_major(v_ref, (vs,), d, kv)
        lane = lax.broadcasted_iota(jnp.int32, (BLK, LANES), 1)

        def blk(b, carry):
            has_prev, prev, nat = _block_rows(b, d)
            cur = pl.multiple_of(b * BLK, BLK)
            k2 = jnp.concatenate([ks[pl.ds(prev, BLK), :], ks[pl.ds(cur, BLK), :]], axis=0)
            v2 = jnp.concatenate([vs[pl.ds(prev, BLK), :], vs[pl.ds(cur, BLK), :]], axis=0)
            nums, ms, ls = [], [], []
            for hh, qz in enumerate((qz0, qz1)):
                z = _dot(qz[pl.ds(cur, BLK), :], k2, 1, 1) + b_ref[hh, has_prev]
                m = jnp.max(z, axis=1, keepdims=True)
                e = jnp.exp(z - m)
                l = jnp.sum(e, axis=1, keepdims=True)
                num = _dot(e.astype(BF16), v2, 1, 0)
                if shared_kv:
                    sink = sink_ref[0, 2 * p + hh]
                    mx = jnp.maximum(m, sink)
                    c = jnp.exp(m - mx)
                    zden = l * c + jnp.exp(sink - mx)
                    num = num * (c / zden)
                    m = mx + jnp.log(zden)
                ls.append(l)
                ms.append(m)
                nums.append(num)
            num_t = jnp.where(lane < HD, nums[0], nums[1])
            if shared_kv:
                st_t = jnp.where(lane < HD, ms[0], ms[1])
            else:
                st_t = jnp.where(lane < 32, ms[0], jnp.where(lane < 64, ls[0], jnp.where(lane < 96, ms[1], ls[1])))
            if d > 1:
                num_ref[pl.ds(nat, BLK, stride=d), :] = num_t
                st_ref[pl.ds(nat, BLK, stride=d), :] = st_t
            else:
                num_ref[pl.ds(cur, BLK), :] = num_t
                st_ref[pl.ds(cur, BLK), :] = st_t
            return carry

        lax.fori_loop(0, NB, blk, 0, unroll=8)

    slab = lambda off, per_pair: pl.BlockSpec((None, S, LANES), (lambda p: (off + p, 0, 0)) if per_pair else (lambda p: (off, 0, 0)))
    out = pl.BlockSpec((None, S, LANES), lambda p: (p, 0, 0))
    return pl.pallas_call(
        body, grid=(npairs,),
        in_specs=[SMEM, slab(q0, True), slab(k0, not shared_kv), slab(v0, not shared_kv),
                  pl.BlockSpec((None, 2, 2, BLK, 2 * BLK), lambda p: (bias0 + p, 0, 0, 0, 0))],
        out_specs=[out, out],
        out_shape=[jax.ShapeDtypeStruct((npairs, S, LANES), F32)] * 2,
        scratch_shapes=[pltpu.VMEM((S, LANES), BF16)] * 4,
        compiler_params=_cp("arbitrary"), name=name)(sinks, proj, proj, proj, bias)


def _combine_a(nums, stats, name):
    rt = 512

    def body(n0, n1, n2, s0, s1, s2, o_ref, l_ref):
        n_refs, s_refs = (n0, n1, n2), (s0, s1, s2)
        outs, lses = [], []
        for hh in range(2):
            ms = [s[:, 64 * hh:64 * hh + 1] for s in s_refs]
            ls = [s[:, 64 * hh + 32:64 * hh + 33] for s in s_refs]
            mx = jnp.maximum(jnp.maximum(ms[0], ms[1]), ms[2])
            cs = [jnp.exp(m - mx) for m in ms]
            z = cs[0] * ls[0] + cs[1] * ls[1] + cs[2] * ls[2]
            acc = cs[0] * n_refs[0][:, hh * HD:(hh + 1) * HD]
            acc = acc + cs[1] * n_refs[1][:, hh * HD:(hh + 1) * HD]
            acc = acc + cs[2] * n_refs[2][:, hh * HD:(hh + 1) * HD]
            outs.append(acc / z)
            lses.append(mx + jnp.log(z))
        o_ref[...] = jnp.concatenate(outs, axis=1)
        l_ref[...] = _lane_halves(lses[0], lses[1])

    spec = pl.BlockSpec((None, rt, LANES), lambda p, i: (p, i, 0))
    return pl.pallas_call(
        body, grid=(2, S // rt), in_specs=[spec] * 6, out_specs=[spec, spec],
        out_shape=[jax.ShapeDtypeStruct((2, S, LANES), F32)] * 2,
        compiler_params=_cp("parallel", "parallel"), name=name)(*nums, *stats)


def _band_bwd(proj, bias, o, do, lse, sinks, *, d, q0, k0, v0, npairs, bias0, shared_kv, name):
    nkv = 1 if shared_kv else npairs

    def body(sink_ref, q_ref, k_ref, v_ref, b_ref, o_ref, do_ref, lse_ref,
             dq_ref, dk_ref, dv_ref, g_ref, ds_ref,
             qz0, qz1, ks, vs, doz0, doz1, ls0, ls1, dls0, dls1, stage, dq_nat, dk_cm, dv_cm, kv_nat, dk_acc, dv_acc):
        p = pl.program_id(0)
        m0, m1 = _head_masks(S)
        kk = lax.broadcasted_iota(jnp.int32, (2 * LANES, LANES), 0) % LANES
        ll = lax.broadcasted_iota(jnp.int32, (2 * LANES, LANES), 1)
        hi, lo = _split2(do_ref[...] * o_ref[...])
        dl = _dot(jnp.concatenate([hi, lo], axis=1), ((kk < HD) == (ll < HD)).astype(BF16), 1, 0)
        if shared_kv:
            row8 = lax.broadcasted_iota(jnp.int32, (8, LANES), 0)
            lane8 = lax.broadcasted_iota(jnp.int32, (8, LANES), 1)
            sinkv = jnp.where(m0, sink_ref[0, 2 * p], sink_ref[0, 2 * p + 1])
            contrib = jnp.exp(sinkv - lse_ref[...]) * dl
            t = jnp.zeros((8, LANES), F32)
            for hh, mh in enumerate((m0, m1)):
                dsink = -jnp.sum(jnp.where(mh, contrib, 0.0)) * (1.0 / HD)
                t = jnp.where((row8 == 0) & (lane8 == hh), dsink, t)
            ds_ref[...] = t
        else:
            ds_ref[...] = jnp.zeros((8, LANES), F32)
        kv = (lambda v: (_dup_head(v, p >= 2),)) if shared_kv else None
        _to_class_major(q_ref, (qz0, qz1), d, lambda v: _split_heads(v * SCALE))
        _to_class_major(k_ref, (ks,), d, kv)
        _to_class_major(v_ref, (vs,), d, kv)
        _to_class_major(do_ref, (doz0, doz1), d, _split_heads)
        def spread(v):
            a0, a1 = _head_masks(v.shape[0])
            r = pltpu.roll(v, HD, 1)
            return jnp.where(a0, v, r), jnp.where(a1, v, r)

        _to_class_major(lse_ref, (ls0, ls1), d, spread)
        stage[...] = dl
        _to_class_major(stage, (dls0, dls1), d, spread)

        dk_cm[...] = jnp.zeros((S, LANES), F32)
        dv_cm[...] = jnp.zeros((S, LANES), F32)
        g_ref[...] = jnp.zeros((2, BLK, 2 * BLK), F32)
        lane = lax.broadcasted_iota(jnp.int32, (BLK, LANES), 1)

        def blk(b, carry):
            has_prev, prev, nat = _block_rows(b, d)
            cur = pl.multiple_of(b * BLK, BLK)
            k2 = jnp.concatenate([ks[pl.ds(prev, BLK), :], ks[pl.ds(cur, BLK), :]], axis=0)
            v2 = jnp.concatenate([vs[pl.ds(prev, BLK), :], vs[pl.ds(cur, BLK), :]], axis=0)
            dqs, dks, dvs = [], [], []
            for hh, (qz, doz, lsr, dlr) in enumerate(((qz0, doz0, ls0, dls0), (qz1, doz1, ls1, dls1))):
                qb = qz[pl.ds(cur, BLK), :]
                dob = doz[pl.ds(cur, BLK), :]
                lb = lsr[pl.ds(cur, BLK), :]
                dlb = dlr[pl.ds(cur, BLK), :]
                z = _dot(qb, k2, 1, 1) + b_ref[hh, has_prev]
                pr = jnp.exp(z - jnp.concatenate([lb, lb], axis=1))
                dp = _dot(dob, v2, 1, 1)
                dz = pr * (dp - jnp.concatenate([dlb, dlb], axis=1))
                g_ref[hh] += dz
                dzb = dz.astype(BF16)
                dqs.append(_dot(dzb, k2, 1, 0))
                dks.append(_dot(dzb, qb, 0, 0))
                dvs.append(_dot(pr.astype(BF16), dob, 0, 0))
            dq_t = jnp.where(lane < HD, dqs[0], dqs[1]) * SCALE
            dk_t = dks[0] + dks[1]
            dv_t = dvs[0] + dvs[1]
            dk_cm[pl.ds(prev, BLK), :] += dk_t[:BLK]
            dk_cm[pl.ds(cur, BLK), :] += dk_t[BLK:]
            dv_cm[pl.ds(prev, BLK), :] += dv_t[:BLK]
            dv_cm[pl.ds(cur, BLK), :] += dv_t[BLK:]
            if d > 1:
                dq_nat[pl.ds(nat, BLK, stride=d), :] = dq_t
            else:
                dq_nat[pl.ds(cur, BLK), :] = dq_t
            return carry

        lax.fori_loop(0, NB, blk, 0, unroll=8)
        dq_ref[...] = dq_nat[...].astype(BF16)

        def from_class_major(src, dst_ref):
            if d == 1:
                dst_ref[...] = src[...].astype(BF16)
            else:
                ln = S // d
                for r in range(d):
                    kv_nat[pl.ds(r, ln, stride=d), :] = src[pl.ds(r * ln, ln), :]
                dst_ref[...] = kv_nat[...].astype(BF16)

        if not shared_kv:
            from_class_major(dk_cm, dk_ref)
            from_class_major(dv_cm, dv_ref)
        else:
            @pl.when(p == 0)
            def _():
                dk_acc[...] = jnp.zeros((S, LANES), F32)
                dv_acc[...] = jnp.zeros((S, LANES), F32)

            mine = m1 == (p >= 2)
            for cm, acc in ((dk_cm, dk_acc), (dv_cm, dv_acc)):
                val = cm[...]
                acc[...] += jnp.where(mine, val + pltpu.roll(val, HD, 1), 0.0)

            @pl.when(p == npairs - 1)
            def _():
                from_class_major(dk_acc, dk_ref)
                from_class_major(dv_acc, dv_ref)

    slab = lambda off, per_pair: pl.BlockSpec((None, S, LANES), (lambda p: (off + p, 0, 0)) if per_pair else (lambda p: (off, 0, 0)))
    pair = pl.BlockSpec((None, S, LANES), lambda p: (p, 0, 0))
    kv_out = pair if not shared_kv else pl.BlockSpec((None, S, LANES), lambda p: (0, 0, 0))
    return pl.pallas_call(
        body, grid=(npairs,),
        in_specs=[SMEM, slab(q0, True), slab(k0, not shared_kv), slab(v0, not shared_kv),
                  pl.BlockSpec((None, 2, 2, BLK, 2 * BLK), lambda p: (bias0 + p, 0, 0, 0, 0)),
                  pair, pair, pair],
        out_specs=[pair, kv_out, kv_out,
                   pl.BlockSpec((None, 2, BLK, 2 * BLK), lambda p: (p, 0, 0, 0)),
                   pl.BlockSpec((None, 8, LANES), lambda p: (p, 0, 0))],
        out_shape=[jax.ShapeDtypeStruct((npairs, S, LANES), BF16),
                   jax.ShapeDtypeStruct((nkv, S, LANES), BF16),
                   jax.ShapeDtypeStruct((nkv, S, LANES), BF16),
                   jax.ShapeDtypeStruct((npairs, 2, BLK, 2 * BLK), F32),
                   jax.ShapeDtypeStruct((npairs, 8, LANES), F32)],
        scratch_shapes=[pltpu.VMEM((S, LANES), BF16)] * 6 + [pltpu.VMEM((S, LANES), F32)] * 11,
        compiler_params=_cp("arbitrary"), name=name)(sinks, proj, proj, proj, bias, o, do, lse)


KC = 512
NSUB = KC // BLK
QB = 512
QPG = KC // QB


def _split2(x):
    hi = x.astype(BF16)
    lo = (x - hi.astype(F32)).astype(BF16)
    return hi, lo


def _tri_ones(cmp):
    jj = lax.broadcasted_iota(jnp.int32, (2 * BLK, BLK), 0) % BLK
    ss = lax.broadcasted_iota(jnp.int32, (2 * BLK, BLK), 1)
    return jnp.concatenate([cmp(jj, ss).astype(BF16), jnp.ones((2 * BLK, BLK), BF16)], axis=1)


def _sub_sums(x, tri1):
    n = x.shape[0]
    st = jnp.concatenate([x[:, s * BLK:(s + 1) * BLK] for s in range(NSUB)], axis=0)
    hi, lo = _split2(st)
    r = _dot(jnp.concatenate([hi, lo], axis=1), tri1, 1, 0)
    return ([r[s * n:(s + 1) * n, :BLK] for s in range(NSUB)], [r[s * n:(s + 1) * n, BLK:] for s in range(NSUB)])


def _log_sig_pair(z):
    lb = jnp.minimum(z, 0.0) - jnp.log1p(jnp.exp(-jnp.abs(z)))
    return lb, lb - z


QGROUPS = NB // NSUB


def _stick_fwd(proj, *, q0, k0, v0, name):
    def body(q_ref, k_ref, v_ref, o_ref, t_ref, qs, ks, vs):
        qs[...] = (q_ref[...] * SCALE).astype(BF16)
        ks[...] = k_ref[...].astype(BF16)
        vs[...] = v_ref[...].astype(BF16)
        tri1 = _tri_ones(lambda j, s: j > s)
        col = lax.broadcasted_iota(jnp.int32, (QB, KC), 1)
        rowi = lax.broadcasted_iota(jnp.int32, (QB, KC), 0)

        for qg in range(QGROUPS):
            def qblock(ii, carry0, qg=qg):
                t0 = pl.multiple_of((qg * QPG + ii) * QB, QB)
                qb = qs[pl.ds(t0, QB), :]
                accs = [jnp.zeros((QB, HD), F32)] * 2
                runs = [jnp.zeros((QB, BLK), F32)] * 2
                for c in reversed(range(qg + 1)):
                    s0 = c * KC
                    diag = c == qg
                    before = (s0 + col) < (t0 + rowi) if diag else None
                    for hh in range(2):
                        kh = ks[s0:s0 + KC, hh * HD:(hh + 1) * HD]
                        vh = vs[s0:s0 + KC, hh * HD:(hh + 1) * HD]
                        lb, lk = _log_sig_pair(_dot(qb[:, hh * HD:(hh + 1) * HD], kh, 1, 1))
                        if diag:
                            lk = jnp.where(before, lk, 0.0)
                        suf, tot = _sub_sums(lk, tri1)
                        ws, run = [], runs[hh]
                        for s in reversed(range(NSUB)):
                            ws.append(jnp.exp(lb[:, s * BLK:(s + 1) * BLK] + suf[s] + run))
                            run = run + tot[s]
                        w = jnp.concatenate(ws[::-1], axis=1)
                        if diag:
                            w = jnp.where(before, w, 0.0)
                        accs[hh] = accs[hh] + _dot(w.astype(BF16), vh, 1, 0)
                        runs[hh] = run
                o_ref[pl.ds(t0, QB), :] = jnp.concatenate(accs, axis=1)
                t_ref[pl.ds(t0, QB), :] = _lane_halves(runs[0], runs[1])
                return carry0

            lax.fori_loop(0, QPG, qblock, 0)

    slab = lambda off: pl.BlockSpec((None, S, LANES), lambda p: (off + p, 0, 0))
    out = pl.BlockSpec((None, S, LANES), lambda p: (p, 0, 0))
    return pl.pallas_call(
        body, grid=(2,), in_specs=[slab(q0), slab(k0), slab(v0)], out_specs=[out, out],
        out_shape=[jax.ShapeDtypeStruct((2, S, LANES), F32)] * 2,
        scratch_shapes=[pltpu.VMEM((S, LANES), BF16)] * 3,
        compiler_params=_cp("arbitrary"), name=name)(proj, proj, proj)


def _stick_bwd(proj, do, tot, *, q0, k0, v0, name):
    def body(q_ref, k_ref, v_ref, do_ref, t_ref, dq_ref, dk_ref, dv_ref, qs, ks, vs, dos, dk_acc, dv_acc):
        qs[...] = (q_ref[...] * SCALE).astype(BF16)
        ks[...] = k_ref[...].astype(BF16)
        vs[...] = v_ref[...].astype(BF16)
        dos[...] = do_ref[...].astype(BF16)
        dk_acc[...] = jnp.zeros((2, S, HD), F32)
        dv_acc[...] = jnp.zeros((2, S, HD), F32)
        tri_inc = _tri_ones(lambda j, s: j <= s)
        tri_exc = _tri_ones(lambda j, s: j < s)
        col = lax.broadcasted_iota(jnp.int32, (QB, KC), 1)
        rowi = lax.broadcasted_iota(jnp.int32, (QB, KC), 0)

        for qg in range(QGROUPS):
            def qblock(ii, carry0, qg=qg):
                t0 = pl.multiple_of((qg * QPG + ii) * QB, QB)
                qb = qs[pl.ds(t0, QB), :]
                dob = dos[pl.ds(t0, QB), :]
                tb = t_ref[pl.ds(t0, QB), :]
                dqs = [jnp.zeros((QB, HD), F32)] * 2
                pruns = [jnp.zeros((QB, BLK), F32)] * 2
                eruns = [jnp.zeros((QB, BLK), F32)] * 2
                for c in range(qg + 1):
                    s0 = c * KC
                    diag = c == qg
                    before = (s0 + col) < (t0 + rowi) if diag else None
                    for hh in range(2):
                        qh = qb[:, hh * HD:(hh + 1) * HD]
                        doh = dob[:, hh * HD:(hh + 1) * HD]
                        tt = tb[:, 64 * hh:64 * hh + 1]
                        kh = ks[s0:s0 + KC, hh * HD:(hh + 1) * HD]
                        vh = vs[s0:s0 + KC, hh * HD:(hh + 1) * HD]
                        lb, lk = _log_sig_pair(_dot(qh, kh, 1, 1))
                        if diag:
                            lk = jnp.where(before, lk, 0.0)
                        pin, ptot = _sub_sums(lk, tri_inc)
                        ws, prun = [], pruns[hh]
                        for s in range(NSUB):
                            ws.append(jnp.exp(lb[:, s * BLK:(s + 1) * BLK] + (tt - (pin[s] + prun))))
                            prun = prun + ptot[s]
                        w = jnp.concatenate(ws, axis=1)
                        if diag:
                            w = jnp.where(before, w, 0.0)
                        e = w * _dot(doh, vh, 1, 1)
                        pex, etot = _sub_sums(e, tri_exc)
                        cs, erun = [], eruns[hh]
                        for s in range(NSUB):
                            cs.append(pex[s] + erun)
                            erun = erun + etot[s]
                        sig = jnp.exp(lb)
                        dz = e * (1.0 - sig) - jnp.concatenate(cs, axis=1) * sig
                        if diag:
                            dz = jnp.where(before, dz, 0.0)
                        dz = dz.astype(BF16)
                        dqs[hh] = dqs[hh] + _dot(dz, kh, 1, 0)
                        dk_acc[hh, s0:s0 + KC, :] += _dot(dz, qh, 0, 0)
                        dv_acc[hh, s0:s0 + KC, :] += _dot(w.astype(BF16), doh, 0, 0)
                        pruns[hh], eruns[hh] = prun, erun
                dq_ref[pl.ds(t0, QB), :] = (jnp.concatenate(dqs, axis=1) * SCALE).astype(BF16)
                return carry0

            lax.fori_loop(0, QPG, qblock, 0)
        dk_ref[...] = jnp.concatenate([dk_acc[0], dk_acc[1]], axis=1).astype(BF16)
        dv_ref[...] = jnp.concatenate([dv_acc[0], dv_acc[1]], axis=1).astype(BF16)

    slab = lambda off: pl.BlockSpec((None, S, LANES), lambda p: (off + p, 0, 0))
    pair = pl.BlockSpec((None, S, LANES), lambda p: (p, 0, 0))
    return pl.pallas_call(
        body, grid=(2,), in_specs=[slab(q0), slab(k0), slab(v0), pair, pair], out_specs=[pair] * 3,
        out_shape=[jax.ShapeDtypeStruct((2, S, LANES), BF16)] * 3,
        scratch_shapes=[pltpu.VMEM((S, LANES), BF16)] * 4 + [pltpu.VMEM((2, S, HD), F32)] * 2,
        compiler_params=_cp("arbitrary"), name=name)(proj, proj, proj, do, tot)


def _cat_slabs(ref):
    return jnp.concatenate([ref[s] for s in range(ref.shape[0])], axis=1)


def _merge_fwd(o_a, o_b, o_c, gates, b_gate, wa, wb, wc, w_out, name):
    tm = ROW_TILE

    def body(oa_ref, ob_ref, oc_ref, g_ref, bg_ref, wa_ref, wb_ref, wc_ref, wo_ref, mg_ref, mo_ref):
        acc = jnp.zeros((tm, D), F32)
        for i, (o_ref, w_ref) in enumerate(((oa_ref, wa_ref), (ob_ref, wb_ref), (oc_ref, wc_ref))):
            pr = _dot(_cat_slabs(o_ref).astype(BF16), w_ref[...], 1, 0)
            sg = jax.nn.sigmoid(g_ref[:, i * D:(i + 1) * D] + bg_ref[i:i + 1, :])
            acc = acc + sg * pr
        mg = acc.astype(BF16)
        mg_ref[...] = mg
        mo_ref[...] = _dot(mg, wo_ref[...], 1, 0)

    slabs = lambda n: pl.BlockSpec((n, tm, LANES), lambda i: (0, i, 0))
    full = lambda r, c: pl.BlockSpec((r, c), lambda i: (0, 0))
    row = pl.BlockSpec((tm, D), lambda i: (i, 0))
    return pl.pallas_call(
        body, grid=(S // tm,),
        in_specs=[slabs(2), slabs(4), slabs(2), pl.BlockSpec((tm, GATE_COLS), lambda i: (i, 0)), full(3, D),
                  full(256, D), full(512, D), full(256, D), full(D, D)],
        out_specs=[row, row],
        out_shape=[jax.ShapeDtypeStruct((S, D), BF16), jax.ShapeDtypeStruct((S, D), F32)],
        compiler_params=_cp("parallel"), name=name)(o_a, o_b, o_c, gates, b_gate, wa, wb, wc, w_out)


def _merge_bwd(d_mo, o_a, o_b, o_c, gates, b_gate, wa, wb, wc, w_out, name):
    tm = ROW_TILE

    def body(dmo_ref, oa_ref, ob_ref, oc_ref, g_ref, bg_ref, wa_ref, wb_ref, wc_ref, wo_ref,
             doa_ref, dob_ref, doc_ref, dg_ref, dwa_ref, dwb_ref, dwc_ref, dbg_ref):
        @pl.when(pl.program_id(0) == 0)
        def _():
            dwa_ref[...] = jnp.zeros(dwa_ref.shape, F32)
            dwb_ref[...] = jnp.zeros(dwb_ref.shape, F32)
            dwc_ref[...] = jnp.zeros(dwc_ref.shape, F32)
            dbg_ref[...] = jnp.zeros(dbg_ref.shape, F32)

        dmg = _dot(dmo_ref[...], wo_ref[...], 1, 1)
        trip = ((oa_ref, wa_ref, doa_ref, dwa_ref), (ob_ref, wb_ref, dob_ref, dwb_ref), (oc_ref, wc_ref, doc_ref, dwc_ref))
        for i, (o_ref, w_ref, do_ref, dw_ref) in enumerate(trip):
            ob = _cat_slabs(o_ref).astype(BF16)
            pr = _dot(ob, w_ref[...], 1, 0)
            sg = jax.nn.sigmoid(g_ref[:, i * D:(i + 1) * D] + bg_ref[i:i + 1, :])
            dgate = dmg * pr * sg * (1.0 - sg)
            dg_ref[:, i * D:(i + 1) * D] = dgate.astype(BF16)
            dbg_ref[i:i + 1, :] += jnp.sum(dgate, axis=0, keepdims=True)
            dpr = (dmg * sg).astype(BF16)
            do = _dot(dpr, w_ref[...], 1, 1)
            for s in range(do_ref.shape[0]):
                do_ref[s] = do[:, s * LANES:(s + 1) * LANES]
            dw_ref[...] += _dot(ob, dpr, 0, 0)

    slabs = lambda n: pl.BlockSpec((n, tm, LANES), lambda i: (0, i, 0))
    full = lambda r, c: pl.BlockSpec((r, c), lambda i: (0, 0))
    row = pl.BlockSpec((tm, D), lambda i: (i, 0))
    return pl.pallas_call(
        body, grid=(S // tm,),
        in_specs=[row, slabs(2), slabs(4), slabs(2), pl.BlockSpec((tm, GATE_COLS), lambda i: (i, 0)), full(3, D),
                  full(256, D), full(512, D), full(256, D), full(D, D)],
        out_specs=[slabs(2), slabs(4), slabs(2), pl.BlockSpec((tm, GATE_COLS), lambda i: (i, 0)),
                   full(256, D), full(512, D), full(256, D), full(3, D)],
        out_shape=[jax.ShapeDtypeStruct((2, S, LANES), F32), jax.ShapeDtypeStruct((4, S, LANES), F32),
                   jax.ShapeDtypeStruct((2, S, LANES), F32), jax.ShapeDtypeStruct((S, GATE_COLS), BF16),
                   jax.ShapeDtypeStruct((256, D), F32), jax.ShapeDtypeStruct((512, D), F32),
                   jax.ShapeDtypeStruct((256, D), F32), jax.ShapeDtypeStruct((3, D), F32)],
        compiler_params=_cp("arbitrary"), name=name)(d_mo, o_a, o_b, o_c, gates, b_gate, wa, wb, wc, w_out)


FC = 256
GELU_K = math.sqrt(2.0 / math.pi)
GELU_C = 0.044715


RC = 64
NRC = S // RC


def _down(tail, cur, n):
    row = lax.broadcasted_iota(jnp.int32, tail.shape, 0)
    rolled = pltpu.roll(cur, n, 0)
    first = jnp.where(row < n, pltpu.roll(tail, n, 0), rolled[0:8])
    return jnp.concatenate([first, rolled[8:]], axis=0)


def _up(cur, head, n):
    row = lax.broadcasted_iota(jnp.int32, head.shape, 0)
    rolled = pltpu.roll(cur, RC - n, 0)
    last = jnp.where(row >= 8 - n, pltpu.roll(head, 8 - n, 0), rolled[RC - 8:])
    return jnp.concatenate([rolled[:RC - 8], last], axis=0)


def _conv_chunk(load, j, w_ref, b_ref, half):
    r0 = pl.multiple_of(j * RC, RC)
    cur = load(r0, RC).astype(F32)
    tail = load(pl.multiple_of(jnp.maximum(r0 - 16, 0), 16), 16).astype(F32)[8:16]
    tail = jnp.where(j > 0, tail, 0.0)
    d1 = _down(tail, cur, 1)
    d2 = _down(tail, cur, 2)
    y = w_ref[0:1, half, :] * d2 + w_ref[1:2, half, :] * d1 + w_ref[2:3, half, :] * cur + b_ref[half:half + 1, :]
    return y, cur, d1, d2


def _chunk(j):
    return pl.ds(pl.multiple_of(j * RC, RC), RC)


def _fold8(x):
    return jnp.sum(x.reshape(RC // 8, 8, x.shape[-1]), axis=0)


def _ffn_act(u, conv_w, conv_b, name):
    def body(u_ref, w_ref, b_ref, a_ref, y_ref):
        def step(j, carry):
            yg = _conv_chunk(lambda r, n: u_ref[0, pl.ds(r, n), :], j, w_ref, b_ref, 0)[0]
            yv = _conv_chunk(lambda r, n: u_ref[1, pl.ds(r, n), :], j, w_ref, b_ref, 1)[0]
            th = jnp.tanh(GELU_K * (yg + GELU_C * yg * yg * yg))
            a_ref[_chunk(j), :] = (0.5 * yg * (1.0 + th) * yv).astype(BF16)
            y_ref[0, _chunk(j), :] = yg.astype(BF16)
            y_ref[1, _chunk(j), :] = yv.astype(BF16)
            return carry

        lax.fori_loop(0, NRC, step, 0)

    return pl.pallas_call(
        body, grid=(D_FF // FC,),
        in_specs=[pl.BlockSpec((2, S, FC), lambda j: (0, 0, j)), pl.BlockSpec((3, 2, FC), lambda j: (0, 0, j)),
                  pl.BlockSpec((2, FC), lambda j: (0, j))],
        out_specs=[pl.BlockSpec((S, FC), lambda j: (0, j)), pl.BlockSpec((2, S, FC), lambda j: (0, 0, j))],
        out_shape=[jax.ShapeDtypeStruct((S, D_FF), BF16), jax.ShapeDtypeStruct((2, S, D_FF), BF16)],
        compiler_params=_cp("parallel"), name=name)(u, conv_w, conv_b)


def _ffn_act_bwd(u, y, d_a, conv_w, name):
    def body(u_ref, y_ref, da_ref, w_ref, du_ref, dw_ref, db_ref, dy_s):
        def first(j, acc):
            yg = y_ref[0, _chunk(j), :].astype(F32)
            yv = y_ref[1, _chunk(j), :].astype(F32)
            th = jnp.tanh(GELU_K * (yg + GELU_C * yg * yg * yg))
            gelu = 0.5 * yg * (1.0 + th)
            dgelu = 0.5 * (1.0 + th) + 0.5 * yg * (1.0 - th * th) * GELU_K * (1.0 + 3.0 * GELU_C * yg * yg)
            da = da_ref[_chunk(j), :].astype(F32)
            dyg = da * yv * dgelu
            dyv = da * gelu
            dy_s[0, _chunk(j), :] = dyg
            dy_s[1, _chunk(j), :] = dyv
            return acc[0] + _fold8(dyg), acc[1] + _fold8(dyv)

        zero = jnp.zeros((8, FC), F32)
        accb = lax.fori_loop(0, NRC, first, (zero, zero))
        for half in range(2):
            db_ref[half:half + 1, :] = jnp.sum(accb[half], axis=0, keepdims=True)

        def second(j, acc):
            new = []
            for half in range(2):
                cur = dy_s[half, _chunk(j), :]
                h0 = pl.multiple_of(jnp.minimum((j + 1) * RC, S - 8), 8)
                head = jnp.where(j < NRC - 1, dy_s[half, pl.ds(h0, 8), :], 0.0)
                up1 = _up(cur, head, 1)
                up2 = _up(cur, head, 2)
                du = w_ref[2:3, half, :] * cur + w_ref[1:2, half, :] * up1 + w_ref[0:1, half, :] * up2
                du_ref[half, _chunk(j), :] = du.astype(BF16)
                uu = u_ref[half, _chunk(j), :].astype(F32)
                new += [_fold8(up2 * uu), _fold8(up1 * uu), _fold8(cur * uu)]
            return tuple(a + n for a, n in zip(acc, new))

        accw = lax.fori_loop(0, NRC, second, tuple(zero for _ in range(6)))
        for half in range(2):
            for k in range(3):
                dw_ref[k:k + 1, half, :] = jnp.sum(accw[3 * half + k], axis=0, keepdims=True)

    return pl.pallas_call(
        body, grid=(D_FF // FC,),
        in_specs=[pl.BlockSpec((2, S, FC), lambda j: (0, 0, j)), pl.BlockSpec((2, S, FC), lambda j: (0, 0, j)),
                  pl.BlockSpec((S, FC), lambda j: (0, j)), pl.BlockSpec((3, 2, FC), lambda j: (0, 0, j))],
        out_specs=[pl.BlockSpec((2, S, FC), lambda j: (0, 0, j)), pl.BlockSpec((3, 2, FC), lambda j: (0, 0, j)),
                   pl.BlockSpec((2, FC), lambda j: (0, j))],
        out_shape=[jax.ShapeDtypeStruct((2, S, D_FF), BF16), jax.ShapeDtypeStruct((3, 2, D_FF), F32),
                   jax.ShapeDtypeStruct((2, D_FF), F32)],
        scratch_shapes=[pltpu.VMEM((2, S, FC), F32)],
        compiler_params=_cp("parallel"), name=name)(u, y, d_a, conv_w)


def _layer_fwd(x, h1, w, bias, lname):
    n = lambda s: f"{lname}_{s}"
    w.need("in", h1)
    tn = 768
    proj = _mm(h1, w["w_in"], grid=(S // 1024, QKV_COLS // tn, 1),
               a_spec=pl.BlockSpec((1024, D), lambda i, j, k: (i, 0)),
               b_spec=pl.BlockSpec((tn, D), lambda i, j, k: (j, 0)),
               out_shape=jax.ShapeDtypeStruct((QKV_SLABS, S, LANES), F32),
               out_spec=pl.BlockSpec((tn // LANES, 1024, LANES), lambda i, j, k: (j, i, 0)),
               ca=1, cb=1, acc_shape=(1024, tn), out_slab=True, name=n("proj_qkv"))
    gates = _mm(h1, w["w_in"], grid=(S // 1024, GATE_COLS // tn, 1),
                a_spec=pl.BlockSpec((1024, D), lambda i, j, k: (i, 0)),
                b_spec=pl.BlockSpec((tn, D), lambda i, j, k: (j + QKV_COLS // tn, 0)),
                out_shape=jax.ShapeDtypeStruct((S, GATE_COLS), BF16),
                out_spec=pl.BlockSpec((1024, tn), lambda i, j, k: (i, j)),
                ca=1, cb=1, acc_shape=(1024, tn), name=n("proj_gate"))
    nums, stats = [], []
    for g, (_, d) in enumerate(A_GROUPS):
        nm, st = _band_fwd(proj, bias, w["sinks"], d=d, q0=2 * g, k0=6 + 2 * g, v0=12 + 2 * g, npairs=2, bias0=2 * g,
                           shared_kv=False, name=n(f"attn_a{g}_fwd"))
        nums.append(nm)
        stats.append(st)
    o_a, lse_a = _combine_a(nums, stats, n("attn_a_combine"))
    o_b, lse_b = _band_fwd(proj, bias, w["sinks"], d=1, q0=18, k0=22, v0=23, npairs=4, bias0=6, shared_kv=True,
                           name=n("attn_b_fwd"))
    o_c, tot_c = _stick_fwd(proj, q0=24, k0=26, v0=28, name=n("attn_c_fwd"))
    w.need("mix", tot_c)
    merged, mo = _merge_fwd(o_a, o_b, o_c, gates, w["b_gate"], w["w_br_a"], w["w_br_b"], w["w_br_c"], w["w_out"], n("merge_fwd"))
    x2, h2 = _postnorm_res(x, mo, w["attn_post_norm"], w["ffn_pre_norm"], n("attn_post"))
    w.need("ffn", h2)
    u = _mm(h2, w["w_up"], grid=(S // 1024, 2 * D_FF // 1024, 1),
            a_spec=pl.BlockSpec((1024, D), lambda i, j, k: (i, 0)),
            b_spec=pl.BlockSpec((D, 1024), lambda i, j, k: (0, j)),
            out_shape=jax.ShapeDtypeStruct((2, S, D_FF), BF16),
            out_spec=pl.BlockSpec((None, 1024, 1024), lambda i, j, k: (j // 4, i, j % 4)),
            ca=1, cb=0, acc_shape=(1024, 1024), name=n("ffn_up"))
    a, y = _ffn_act(u, w["conv_w"], w["conv_b"], n("ffn_act"))
    fo = _mm_nn(a, w["w_down"], F32, 1024, 1024, 2048, n("ffn_down"))
    saved = dict(x=x, h1=h1, proj=proj, gates=gates, o_a=o_a, lse_a=lse_a, o_b=o_b, lse_b=lse_b, o_c=o_c, tot_c=tot_c,
                 merged=merged, mo=mo, x2=x2, h2=h2, u=u, y=y, a=a, fo=fo)
    return saved


def _layer_bwd(dx3, sv, w, bias, lname, tok=None, on_part=None, d_fo=None, below=None):
    n = lambda s: f"{lname}_{s}"
    g = {}

    def part(group, vec):
        t = on_part(group, g) if on_part is not None else None
        return vec if t is None else vec + t

    if d_fo is None:
        gain = w["ffn_post_norm"] if tok is None else w["ffn_post_norm"] + tok
        d_fo, g["ffn_post_norm"] = _norm_bwd(sv["fo"], gain, [dx3], None, BF16, n("ffn_post_bwd"))
    else:
        d_fo, g["ffn_post_norm"] = d_fo
    d_a = _mm_nt(d_fo, w["w_down"], BF16, 1024, 1024, 1024, n("ffn_down_bwd_x"))
    g["w_down"] = _mm_tn(sv["a"], d_fo, BF16, 1024, 1024, S, n("ffn_down_bwd_w"))
    d_u, dcw, dcb = _ffn_act_bwd(sv["u"], sv["y"], d_a, w["conv_w"], n("ffn_act_bwd"))
    g["conv_w"] = dcw.reshape(3, 2 * D_FF)
    g["conv_b"] = dcb.reshape(1, 2 * D_FF)
    g["w_up"] = _mm(sv["h2"], d_u, grid=(1, 2 * D_FF // 1024, 1),
                    a_spec=pl.BlockSpec((S, D), lambda i, j, k: (k, 0)),
                    b_spec=pl.BlockSpec((None, S, 1024), lambda i, j, k: (j // 4, k, j % 4)),
                    out_shape=jax.ShapeDtypeStruct((D, 2 * D_FF), BF16),
                    out_spec=pl.BlockSpec((D, 1024), lambda i, j, k: (0, j)),
                    ca=0, cb=0, acc_shape=(D, 1024), name=n("ffn_up_bwd_w"))
    tok_ffn = on_part("ffn", g) if on_part is not None else None
    d_h2 = _mm(d_u, w["w_up"], grid=(S // 1024, 1, 2),
               a_spec=pl.BlockSpec((None, 1024, D_FF), lambda i, j, k: (k, i, 0)),
               b_spec=pl.BlockSpec((D, D_FF), lambda i, j, k: (0, k)),
               out_shape=jax.ShapeDtypeStruct((S, D), F32),
               out_spec=pl.BlockSpec((1024, D), lambda i, j, k: (i, 0)),
               ca=1, cb=1, acc_shape=(1024, D), after=tok_ffn, name=n("ffn_up_bwd_x"))
    dx2, d_mo, g["ffn_pre_norm"], g["attn_post_norm"] = _norm_bwd_chain(
        sv["x2"], w["ffn_pre_norm"], [d_h2], dx3, sv["mo"], w["attn_post_norm"], n("ffn_pre_attn_post_bwd"))
    g["w_out"] = _mm_tn(sv["merged"], d_mo, BF16, 1024, 1024, S, n("out_bwd_w"))
    do_a, do_b, do_c, d_gates, dwa, dwb, dwc, g["b_gate"] = _merge_bwd(
        d_mo, sv["o_a"], sv["o_b"], sv["o_c"], sv["gates"], w["b_gate"], w["w_br_a"], w["w_br_b"], w["w_br_c"],
        w["w_out"], n("merge_bwd"))
    g["w_br_a"], g["w_br_b"], g["w_br_c"] = dwa, dwb, dwc
    sinks = part("mix", w["sinks"])
    proj = sv["proj"]
    dqa, dka, dva, gbias = [], [], [], []
    for gi, (_, d) in enumerate(A_GROUPS):
        dq, dk, dv, gg, _ = _band_bwd(proj, bias, sv["o_a"], do_a, sv["lse_a"], sinks, d=d, q0=2 * gi, k0=6 + 2 * gi,
                                      v0=12 + 2 * gi, npairs=2, bias0=2 * gi, shared_kv=False, name=n(f"attn_a{gi}_bwd"))
        dqa.append(dq), dka.append(dk), dva.append(dv), gbias.append(gg)
    dqb, dkb, dvb, ggb, dsink = _band_bwd(proj, bias, sv["o_b"], do_b, sv["lse_b"], sinks, d=1, q0=18, k0=22, v0=23,
                                          npairs=4, bias0=6, shared_kv=True, name=n("attn_b_bwd"))
    gbias.append(ggb)
    g["bias_g"] = jnp.concatenate(gbias, axis=0).reshape(N_BIAS_HEADS, BLK, 2 * BLK)
    g["sinks"] = dsink[:, 0, :2].reshape(1, 8)
    dqc, dkc, dvc = _stick_bwd(proj, do_c, sv["tot_c"], q0=24, k0=26, v0=28, name=n("attn_c_bwd"))
    dqkv = jnp.concatenate(dqa + dka + dva + [dqb, dkb, dvb, dqc, dkc, dvc], axis=0)
    ts = 6
    tsx = QKV_SLABS
    dw_in = _mm(dqkv, sv["h1"], grid=(QKV_SLABS // ts, 1, 1),
                a_spec=pl.BlockSpec((ts, S, LANES), lambda i, j, k: (i, k, 0)),
                b_spec=pl.BlockSpec((S, D), lambda i, j, k: (k, 0)),
                out_shape=jax.ShapeDtypeStruct((IN_COLS, D), BF16),
                out_spec=pl.BlockSpec((ts * LANES, D), lambda i, j, k: (i, 0)),
                ca=0, cb=0, acc_shape=(ts * LANES, D), a_slab=True, name=n("in_bwd_w_qkv"))
    g["w_in"] = _mm(d_gates, sv["h1"], grid=(GATE_COLS // 768, 1, 1),
                    a_spec=pl.BlockSpec((S, 768), lambda i, j, k: (k, i)),
                    b_spec=pl.BlockSpec((S, D), lambda i, j, k: (k, 0)),
                    out_shape=jax.ShapeDtypeStruct((IN_COLS, D), BF16),
                    out_spec=pl.BlockSpec((768, D), lambda i, j, k: (i + QKV_COLS // 768, 0)),
                    ca=0, cb=0, acc_shape=(768, D), alias_out=dw_in, name=n("in_bwd_w_gate"))
    tok_in = on_part("in", g) if on_part is not None else None
    d_h1a = _mm(dqkv, w["w_in"], grid=(S // 1024, 1, QKV_SLABS // tsx),
                a_spec=pl.BlockSpec((tsx, 1024, LANES), lambda i, j, k: (k, i, 0)),
                b_spec=pl.BlockSpec((tsx * LANES, D), lambda i, j, k: (k, 0)),
                out_shape=jax.ShapeDtypeStruct((S, D), F32),
                out_spec=pl.BlockSpec((1024, D), lambda i, j, k: (i, 0)),
                ca=1, cb=0, acc_shape=(1024, D), a_slab=True, after=tok_in, name=n("in_bwd_x_qkv"))
    d_h1b = _mm(d_gates, w["w_in"], grid=(S // 1024, 1, GATE_COLS // 768),
                a_spec=pl.BlockSpec((1024, 768), lambda i, j, k: (i, k)),
                b_spec=pl.BlockSpec((768, D), lambda i, j, k: (k + QKV_COLS // 768, 0)),
                out_shape=jax.ShapeDtypeStruct((S, D), F32),
                out_spec=pl.BlockSpec((1024, D), lambda i, j, k: (i, 0)),
                ca=1, cb=0, acc_shape=(1024, D), after=tok_in, name=n("in_bwd_x_gate"))
    if below is None:
        dx, g["attn_pre_norm"] = _norm_bwd(sv["x"], w["attn_pre_norm"], [d_h1a, d_h1b], dx2, F32, n("attn_pre_bwd"))
        return dx, g, tok_in, None
    dx, d_fo_below, g["attn_pre_norm"], dg_below = _norm_bwd_chain(
        sv["x"], w["attn_pre_norm"], [d_h1a, d_h1b], dx2, below[0], below[1], n("attn_pre_ffn_post_bwd"))
    return dx, g, tok_in, (d_fo_below, dg_below)


def _local_step(x, target, ws, rel_bias, tok=None, on_grads=None):
    buckets = jnp.asarray(_bucket_tiles())
    bias = _bias_tiles(rel_bias, buckets, "bias_tiles").reshape(N_BIAS_HEADS // 2, 2, 2, BLK, 2 * BLK)
    saved = []
    gain0 = ws[0]["attn_pre_norm"] if tok is None else ws[0]["attn_pre_norm"] + tok
    h1 = _prenorm(x, gain0, "l0_attn_pre")
    for l in range(DEPTH):
        sv = _layer_fwd(x, h1, ws[l], bias, f"l{l}")
        saved.append(sv)
        if l + 1 < DEPTH:
            x, h1 = _postnorm_res(sv["x2"], sv["fo"], ws[l]["ffn_post_norm"], ws[l + 1]["attn_pre_norm"], f"l{l}_ffn_post")
    top = saved[-1]
    dy, loss_tile, d_fo_top, dg_top = _loss_head(top["x2"], top["fo"], ws[-1]["ffn_post_norm"], target, "loss_head")
    grads = [None] * DEPTH
    tok, d_fo = None, (d_fo_top, dg_top)
    for l in reversed(range(DEPTH)):
        on_part = None if on_grads is None else functools.partial(on_grads, l)
        below = (saved[l - 1]["fo"], ws[l - 1]["ffn_post_norm"]) if l > 0 else None
        dy, grads[l], tok, d_fo = _layer_bwd(dy, saved[l], ws[l], bias, f"l{l}", tok, on_part, d_fo, below)
    g_rel = _bias_grad([grads[l]["bias_g"] for l in range(DEPTH)], buckets, "bias_grad")[:, :N_BIAS_HEADS]
    return loss_tile[0, 0], dy, grads, g_rel


def _coords():
    return lax.axis_index("x"), lax.axis_index("y"), lax.axis_index("c")


def _peer(rel):
    x, y, c = _coords()
    return (1 - x if rel & 4 else x, 1 - y if rel & 2 else y, 1 - c if rel & 1 else c)


def _exchange(srcs, dst_shapes, src_win, dst_win, name, after=None):
    nt = len(srcs)
    extra = [] if after is None else [after]

    def body(*refs):
        src_refs, dst_refs = refs[:nt], refs[nt + len(extra):2 * nt + len(extra)]
        send_sems, recv_sems, local_sems = refs[2 * nt + len(extra):]
        x, y, c = _coords()
        me = 4 * x + 2 * y + c
        locals_ = []
        for t in range(nt):
            cp = pltpu.make_async_copy(src_win(t, src_refs[t], me), dst_win(t, dst_refs[t], me), local_sems.at[t])
            cp.start()
            locals_.append(cp)
        sends = []
        for rel in range(1, NDEV):
            px, py, pc = _peer(rel)
            q = 4 * px + 2 * py + pc
            for t in range(nt):
                cp = pltpu.make_async_remote_copy(
                    src_ref=src_win(t, src_refs[t], q), dst_ref=dst_win(t, dst_refs[t], me),
                    send_sem=send_sems.at[rel - 1, t], recv_sem=recv_sems.at[rel - 1, t],
                    device_id=(px, py, pc), device_id_type=MESH)
                cp.start()
                sends.append(cp)
        for rel in range(1, NDEV):
            px, py, pc = _peer(rel)
            q = 4 * px + 2 * py + pc
            for t in range(nt):
                pltpu.make_async_remote_copy(
                    src_ref=src_win(t, src_refs[t], me), dst_ref=dst_win(t, dst_refs[t], q),
                    send_sem=send_sems.at[rel - 1, t], recv_sem=recv_sems.at[rel - 1, t],
                    device_id=(px, py, pc), device_id_type=MESH).wait_recv()
        for cp in sends:
            cp.wait_send()
        for cp in locals_:
            cp.wait()

    return pl.pallas_call(
        body, in_specs=[ANY] * (nt + len(extra)), out_specs=[ANY] * nt, out_shape=dst_shapes,
        scratch_shapes=[pltpu.SemaphoreType.DMA((NDEV - 1, nt)), pltpu.SemaphoreType.DMA((NDEV - 1, nt)),
                        pltpu.SemaphoreType.DMA((nt,))],
        name=name)(*srcs, *extra)


BIG = (("w_in", 0, 864), ("w_br_a", 1, 128), ("w_br_b", 1, 128), ("w_br_c", 1, 128), ("w_out", 0, 128),
       ("w_up", 1, 1024), ("w_down", 0, 512))


NBIG = len(BIG)
BIG_FULL = {"w_in": (IN_COLS, D), "w_br_a": (256, D), "w_br_b": (512, D), "w_br_c": (256, D), "w_out": (D, D),
            "w_up": (D, 2 * D_FF), "w_down": (D_FF, D)}
SHARD_ROWS = {"w_in": 288, "w_up": 256, "w_down": 256}
LAYER_GROUPS = (("in", (0,)), ("mix", (1, 2, 3, 4)), ("ffn", (5, 6)))

HBM_SPEC = pl.BlockSpec(memory_space=pltpu.HBM)
SEM_SPEC = pl.BlockSpec(memory_space=pltpu.SEMAPHORE)


def _hbm(a):
    return pltpu.with_memory_space_constraint(a, pltpu.HBM)


def _shard_window(t, ref, k):
    nm, ax, ext = BIG[t % NBIG]
    off = pl.multiple_of(k * ext, ext)
    if ax == 0:
        return ref.at[pl.ds(off, ext), :]
    return ref.at[:, pl.ds(off, ext)]


def _whole(t, ref, k):
    return ref


def _slot(t, ref, k):
    return ref.at[k]


def _own_block_spec(t, rows, me_of):
    nm, ax, ext = BIG[t % NBIG]
    r, c = BIG_FULL[nm]
    if ax == 0:
        return pl.BlockSpec((rows, c), lambda i, m: (me_of(m) * (ext // rows) + i, 0))
    return pl.BlockSpec((rows, ext), lambda i, m: (i, me_of(m)))


def _cast_own(t, shards, me_arr, name):
    nm, ax, ext = BIG[t % NBIG]
    layer = t // NBIG
    _, nr, nc = shards.shape
    rows = SHARD_ROWS.get(nm, nr)
    shape = BIG_FULL[nm]

    def body(m_ref, s_ref, o_ref):
        o_ref[...] = s_ref[...].astype(BF16)

    return pl.pallas_call(
        body, grid_spec=pltpu.PrefetchScalarGridSpec(
            num_scalar_prefetch=1, grid=(nr // rows,),
            in_specs=[pl.BlockSpec((None, rows, nc), lambda i, m: (layer, i, 0))],
            out_specs=_own_block_spec(t, rows, lambda m: m[0])),
        out_shape=jax.ShapeDtypeStruct(shape, BF16), compiler_params=_cp("arbitrary"), name=name)(me_arr, shards)


ALL_RELS = tuple(range(1, NDEV))
NEAR_RELS = (1, 2, 4, 6)
FAR_RELS = (2, 4, 6)


def _xchg_start(srcs, lands, groups, src_win, dst_win, after, name, rels=ALL_RELS, tids=None):
    ns = 0 if srcs is None else len(srcs)
    nt, ng = len(lands), len(groups)
    ins = ([] if srcs is None else list(srcs)) + list(lands)

    def body(*refs):
        src_refs, land_refs = refs[:ns], refs[ns:ns + nt]
        sems = refs[ns + nt + 1:ns + nt + 1 + 2 * ng]
        token = refs[-1]
        x, y, c = _coords()
        me = 4 * x + 2 * y + c
        for gi, grp in enumerate(groups):
            for j, t in enumerate(grp):
                tid = t if tids is None else tids[t]
                for ri, rel in enumerate(rels):
                    px, py, pc = _peer(rel)
                    q = 4 * px + 2 * py + pc
                    src = dst_win(tid, land_refs[t], me) if srcs is None else src_win(tid, src_refs[t], q)
                    pltpu.make_async_remote_copy(
                        src_ref=src, dst_ref=dst_win(tid, land_refs[t], me),
                        send_sem=sems[2 * gi].at[ri * len(grp) + j],
                        recv_sem=sems[2 * gi + 1].at[ri * len(grp) + j],
                        device_id=(px, py, pc), device_id_type=MESH).start()
        token[...] = jnp.zeros((8, LANES), F32)

    out_shape = []
    for grp in groups:
        out_shape += [pltpu.SemaphoreType.DMA((len(rels) * len(grp),))] * 2
    out_shape += [pltpu.HBM(a.shape, a.dtype) for a in ins]
    out_shape.append(jax.ShapeDtypeStruct((8, LANES), F32))
    outs = pl.pallas_call(
        body, in_specs=[HBM_SPEC] * len(ins) + [ANY],
        out_specs=[SEM_SPEC] * (2 * ng) + [HBM_SPEC] * len(ins) + [pl.BlockSpec(memory_space=pltpu.VMEM)],
        out_shape=out_shape, input_output_aliases={i: 2 * ng + i for i in range(len(ins))},
        compiler_params=pltpu.CompilerParams(has_side_effects=pltpu.SideEffectType.DATAFLOW_SIDE_EFFECTING),
        name=name)(*[_hbm(a) for a in ins], after)
    sems = [(outs[2 * gi], outs[2 * gi + 1]) for gi in range(ng)]
    thru = list(outs[2 * ng:2 * ng + len(ins)])
    return sems, (None if srcs is None else thru[:ns]), thru[ns:], outs[-1]


def _xchg_wait(sems, srcs, lands, tids, after, src_win, dst_win, name, rels=ALL_RELS):
    ns = 0 if srcs is None else len(srcs)
    n = len(lands)
    send_sem, recv_sem = sems
    ins = ([] if srcs is None else list(srcs)) + list(lands)

    def body(*refs):
        src_refs, land_refs = refs[:ns], refs[ns:ns + n]
        ssem, rsem = refs[ns + n], refs[ns + n + 1]
        x, y, c = _coords()
        me = 4 * x + 2 * y + c
        for j, t in enumerate(tids):
            for ri, rel in enumerate(rels):
                px, py, pc = _peer(rel)
                q = 4 * px + 2 * py + pc
                src = dst_win(t, land_refs[j], me) if srcs is None else src_win(t, src_refs[j], q)
                cp = pltpu.make_async_remote_copy(
                    src_ref=src, dst_ref=dst_win(t, land_refs[j], q),
                    send_sem=ssem.at[ri * n + j], recv_sem=rsem.at[ri * n + j],
                    device_id=(px, py, pc), device_id_type=MESH)
                cp.wait_send()
                cp.wait_recv()

    outs = pl.pallas_call(
        body, in_specs=[HBM_SPEC] * len(ins) + [SEM_SPEC, SEM_SPEC, ANY], out_specs=[HBM_SPEC] * len(ins),
        out_shape=[pltpu.HBM(a.shape, a.dtype) for a in ins],
        input_output_aliases={i: i for i in range(len(ins))},
        compiler_params=pltpu.CompilerParams(has_side_effects=pltpu.SideEffectType.DATAFLOW_SIDE_EFFECTING),
        name=name)(*ins, send_sem, recv_sem, after)
    return (None if srcs is None else list(outs[:ns])), list(outs[ns:])


def _gather_forward(sems_in, lands, groups, tids, after, dst_win, name):
    nt, ng = len(lands), len(groups)

    def body(*refs):
        land_refs = refs[:nt]
        in_sems = refs[nt:nt + 2 * ng]
        out_sems = refs[nt + 2 * ng + 1:nt + 4 * ng + 1]
        token = refs[-1]
        x, y, c = _coords()
        me = 4 * x + 2 * y + c
        sib = (x, y, 1 - c)
        for gi, grp in enumerate(groups):
            n = len(grp)
            for j, pos in enumerate(grp):
                t = tids[pos]
                for ri, rel in enumerate(NEAR_RELS):
                    px, py, pc = _peer(rel)
                    q = 4 * px + 2 * py + pc
                    cp = pltpu.make_async_remote_copy(
                        src_ref=dst_win(t, land_refs[pos], me), dst_ref=dst_win(t, land_refs[pos], q),
                        send_sem=in_sems[2 * gi].at[ri * n + j], recv_sem=in_sems[2 * gi + 1].at[ri * n + j],
                        device_id=(px, py, pc), device_id_type=MESH)
                    cp.wait_send()
                    cp.wait_recv()
            for j, pos in enumerate(grp):
                t = tids[pos]
                for fi, rel in enumerate(FAR_RELS):
                    px, py, pc = _peer(rel)
                    q = 4 * px + 2 * py + pc
                    win = dst_win(t, land_refs[pos], q)
                    pltpu.make_async_remote_copy(
                        src_ref=win, dst_ref=win,
                        send_sem=out_sems[2 * gi].at[fi * n + j], recv_sem=out_sems[2 * gi + 1].at[fi * n + j],
                        device_id=sib, device_id_type=MESH).start()
        token[...] = jnp.zeros((8, LANES), F32)

    out_shape = []
    for grp in groups:
        out_shape += [pltpu.SemaphoreType.DMA((len(FAR_RELS) * len(grp),))] * 2
    out_shape += [pltpu.HBM(a.shape, a.dtype) for a in lands]
    out_shape.append(jax.ShapeDtypeStruct((8, LANES), F32))
    flat_sems = [s for pair in sems_in for s in pair]
    outs = pl.pallas_call(
        body, in_specs=[HBM_SPEC] * nt + [SEM_SPEC] * (2 * ng) + [ANY],
        out_specs=[SEM_SPEC] * (2 * ng) + [HBM_SPEC] * nt + [pl.BlockSpec(memory_space=pltpu.VMEM)],
        out_shape=out_shape, input_output_aliases={i: 2 * ng + i for i in range(nt)},
        compiler_params=pltpu.CompilerParams(has_side_effects=pltpu.SideEffectType.DATAFLOW_SIDE_EFFECTING),
        name=name)(*[_hbm(a) for a in lands], *flat_sems, after)
    sems = [(outs[2 * gi], outs[2 * gi + 1]) for gi in range(ng)]
    return sems, list(outs[2 * ng:2 * ng + nt]), outs[-1]


class _Weights:
    def __init__(self, ready, pending=None):
        self.ready = dict(ready)
        self.pending = dict(pending or {})

    def __getitem__(self, k):
        return self.ready[k]

    def need(self, group, after):
        fn = self.pending.pop(group, None)
        if fn is not None:
            self.ready.update(fn(after))


def _adamw_math(w, g, m, v):
    m2 = ADAM_B1 * m + (1.0 - ADAM_B1) * g
    v2 = ADAM_B2 * v + (1.0 - ADAM_B2) * (g * g)
    m_hat = m2 / (1.0 - ADAM_B1 ** ADAM_STEP)
    v_hat = v2 / (1.0 - ADAM_B2 ** ADAM_STEP)
    delta = -ADAM_LR * (m_hat / (jnp.sqrt(v_hat) + ADAM_EPS) + ADAM_WD * w)
    return delta, m2, v2


def _adamw(t, parts, own, me_arr, w, m, v, layer, prev, rows, name):
    nl, nr, nc = w.shape

    def body(me_ref, p_ref, own_ref, w_ref, m_ref, v_ref, *rest):
        g_ref, d_ref, m2_ref, v2_ref = rest[-4:]
        me = me_ref[0]
        g = None
        for k in range(NDEV):
            term = jnp.where(me == k, own_ref[...], p_ref[k]).astype(F32)
            g = term if g is None else g + term
        delta, m2, v2 = _adamw_math(w_ref[...], g, m_ref[...], v_ref[...])
        g_ref[...] = g
        d_ref[...] = delta
        m2_ref[...] = m2
        v2_ref[...] = v2

    blk = pl.BlockSpec((None, rows, nc), lambda i, mm: (layer, i, 0))
    pblk = pl.BlockSpec((NDEV, rows, nc), lambda i, mm: (0, i, 0))
    extra = [] if prev is None else list(prev)
    return pl.pallas_call(
        body, grid_spec=pltpu.PrefetchScalarGridSpec(
            num_scalar_prefetch=1, grid=(nr // rows,),
            in_specs=[pblk, _own_block_spec(t, rows, lambda mm: mm[0]), blk, blk, blk] + [ANY] * len(extra),
            out_specs=[blk] * 4),
        out_shape=[jax.ShapeDtypeStruct(w.shape, F32)] * 4,
        input_output_aliases={6 + k: k for k in range(len(extra))},
        compiler_params=_cp("arbitrary"), name=name)(me_arr, parts, own, w, m, v, *extra)


def _pack(vecs):
    flat = jnp.concatenate([v.reshape(-1).astype(F32) for v in vecs])
    n = flat.shape[0]
    rows = -(-n // (8 * LANES)) * 8
    return jnp.pad(flat, (0, rows * LANES - n)).reshape(rows, LANES)


ROWPACK = (("rel_bias", 32, 32, (NUM_BUCKETS, N_BIAS_HEADS)), ("sinks", 8, 8, (DEPTH, 8)),
           ("attn_pre_norm", 16, 16, (DEPTH, D)), ("attn_post_norm", 16, 16, (DEPTH, D)),
           ("ffn_pre_norm", 16, 16, (DEPTH, D)), ("ffn_post_norm", 16, 16, (DEPTH, D)),
           ("conv_b", 128, 128, (DEPTH, 2 * D_FF)), ("b_gate", 48, 8, (DEPTH, 3, 128)),
           ("conv_w", 384, 48, (DEPTH, 3, 1024)))
ROWS_OWN = sum(r for _, _, r, _ in ROWPACK)
N_REPL = 7
ROWS_REPL = sum(r for _, _, r, _ in ROWPACK[:N_REPL])
ROWS_SHARD = ROWS_OWN - ROWS_REPL


def _as_rows(a, rows):
    a = a.astype(F32)
    if a.shape[-1] < LANES:
        a = jnp.pad(a.reshape(-1, a.shape[-1]), ((0, 0), (0, LANES - a.shape[-1])))
    a = a.reshape(-1, LANES)
    return jnp.pad(a, ((0, rows - a.shape[0]), (0, 0)))


def _rowpack(arrs, entries=ROWPACK):
    return jnp.concatenate([_as_rows(arrs[nm], ro) for nm, _, ro, _ in entries], axis=0)


def _shard_rows(g):
    bg = jnp.transpose(g["b_gate"].astype(F32).reshape(DEPTH * 3, NDEV, LANES), (1, 0, 2))
    bg = jnp.pad(bg, ((0, 0), (0, 8 - DEPTH * 3), (0, 0)))
    cw = jnp.transpose(g["conv_w"].astype(F32).reshape(DEPTH * 3, NDEV, 8, LANES), (1, 0, 2, 3))
    return jnp.concatenate([bg, cw.reshape(NDEV, DEPTH * 3 * 8, LANES)], axis=1)


def _small_update(parts_repl, parts_shard, w, m, v, name):
    nsm = len(ROWPACK)

    def body(pr_ref, ps_ref, w_ref, m_ref, v_ref, *rest):
        outs = rest[:4 * nsm]
        g_s, d_s, m_s, v_s = rest[4 * nsm:]
        gr, gs = pr_ref[0], ps_ref[0]
        for k in range(1, NDEV):
            gr = gr + pr_ref[k]
            gs = gs + ps_ref[k]
        g_s[0:ROWS_REPL, :] = gr
        g_s[ROWS_REPL:ROWS_OWN, :] = gs
        delta, m2, v2 = _adamw_math(w_ref[...], g_s[...], m_ref[...], v_ref[...])
        d_s[...] = delta
        m_s[...] = m2
        v_s[...] = v2
        for kind, src in enumerate((g_s, d_s, m_s, v_s)):
            oo = 0
            for idx, (nm, rf, ro, shp) in enumerate(ROWPACK):
                o_ref = outs[kind * nsm + idx]
                if nm in ("rel_bias", "sinks"):
                    o_ref[...] = src[oo:oo + shp[0], 0:shp[1]]
                elif nm == "b_gate":
                    for l in range(DEPTH):
                        o_ref[l] = src[oo + 3 * l:oo + 3 * l + 3, :]
                elif nm == "conv_w":
                    for l in range(DEPTH):
                        for k in range(8):
                            o_ref[l, :, k * LANES:(k + 1) * LANES] = src[pl.ds(oo + 24 * l + k, 3, stride=8), :]
                else:
                    per = shp[1] // LANES
                    for k in range(per):
                        o_ref[:, k * LANES:(k + 1) * LANES] = src[pl.ds(oo + k, DEPTH, stride=per), :]
                oo += ro

    vm = pl.BlockSpec(memory_space=pltpu.VMEM)
    shapes = [jax.ShapeDtypeStruct(shp, F32) for _ in range(4) for _, _, _, shp in ROWPACK]
    outs = pl.pallas_call(
        body, in_specs=[vm] * 5, out_specs=[vm] * (4 * nsm), out_shape=shapes,
        scratch_shapes=[pltpu.VMEM((ROWS_OWN, LANES), F32)] * 4,
        name=name)(parts_repl, parts_shard, w, m, v)
    names = [nm for nm, _, _, _ in ROWPACK]
    return [dict(zip(names, outs[kind * nsm:(kind + 1) * nsm])) for kind in range(4)]


def kernel(x, rel_bias, attn_pre_norm, w_in, b_gate, sinks, w_br_a, w_br_b, w_br_c, w_out, attn_post_norm, ffn_pre_norm, w_up, conv_w, conv_b, w_down, ffn_post_norm, loss_target, m_rel_bias, m_attn_pre_norm, m_w_in, m_b_gate, m_sinks, m_w_br_a, m_w_br_b, m_w_br_c, m_w_out, m_attn_post_norm, m_ffn_pre_norm, m_w_up, m_conv_w, m_conv_b, m_w_down, m_ffn_post_norm, v_rel_bias, v_attn_pre_norm, v_w_in, v_b_gate, v_sinks, v_w_br_a, v_w_br_b, v_w_br_c, v_w_out, v_attn_post_norm, v_ffn_pre_norm, v_w_up, v_conv_w, v_conv_b, v_w_down, v_ffn_post_norm):
    P = dict(rel_bias=rel_bias, attn_pre_norm=attn_pre_norm, w_in=w_in, b_gate=b_gate, sinks=sinks, w_br_a=w_br_a,
             w_br_b=w_br_b, w_br_c=w_br_c, w_out=w_out, attn_post_norm=attn_post_norm, ffn_pre_norm=ffn_pre_norm,
             w_up=w_up, conv_w=conv_w, conv_b=conv_b, w_down=w_down, ffn_post_norm=ffn_post_norm)
    M = dict(rel_bias=m_rel_bias, attn_pre_norm=m_attn_pre_norm, w_in=m_w_in, b_gate=m_b_gate, sinks=m_sinks,
             w_br_a=m_w_br_a, w_br_b=m_w_br_b, w_br_c=m_w_br_c, w_out=m_w_out, attn_post_norm=m_attn_post_norm,
             ffn_pre_norm=m_ffn_pre_norm, w_up=m_w_up, conv_w=m_conv_w, conv_b=m_conv_b, w_down=m_w_down,
             ffn_post_norm=m_ffn_post_norm)
    V = dict(rel_bias=v_rel_bias, attn_pre_norm=v_attn_pre_norm, w_in=v_w_in, b_gate=v_b_gate, sinks=v_sinks,
             w_br_a=v_w_br_a, w_br_b=v_w_br_b, w_br_c=v_w_br_c, w_out=v_w_out, attn_post_norm=v_attn_post_norm,
             ffn_pre_norm=v_ffn_pre_norm, w_up=v_w_up, conv_w=v_conv_w, conv_b=v_conv_b, w_down=v_w_down,
             ffn_post_norm=v_ffn_post_norm)
    tr = lambda a: jnp.swapaxes(a, 1, 2)
    PB = {nm: (tr(P[nm]) if nm == "w_in" else P[nm]) for nm, _, _ in BIG}
    MB = {nm: (tr(M[nm]) if nm == "w_in" else M[nm]) for nm, _, _ in BIG}
    VB = {nm: (tr(V[nm]) if nm == "w_in" else V[nm]) for nm, _, _ in BIG}
    xi, yi, ci = _coords()
    me = 4 * xi + 2 * yi + ci

    me_arr = me.astype(jnp.int32).reshape(1)

    small_w = _pack([b_gate.reshape(-1), conv_w.reshape(-1)])
    (small_w_all,) = _exchange([small_w], [jax.ShapeDtypeStruct((NDEV,) + small_w.shape, F32)],
                               _whole, _slot, "gather_small_weights")

    groups = [tuple(l * NBIG + t for t in tids) for l in range(DEPTH) for _, tids in LAYER_GROUPS]
    cast = lambda i, m=me_arr: _cast_own(i, PB[BIG[i % NBIG][0]], m, f"gather_own_l{i // NBIG}_{BIG[i % NBIG][0]}")
    first = list(groups[0])
    rest = [i for grp in groups[1:] for i in grp]
    sems0, _, lands0, tok_first = _xchg_start(None, [cast(i) for i in first], [tuple(range(len(first)))], None,
                                              _shard_window, small_w_all, "gather_start_first", rels=NEAR_RELS, tids=first)
    where_rest = {tid: k for k, tid in enumerate(rest)}
    me_rest = me_arr + tok_first[0, 0:1].astype(jnp.int32)
    sems1, _, lands1, g_tok = _xchg_start(None, [cast(i, me_rest) for i in rest],
                                          [tuple(where_rest[i] for i in grp) for grp in groups[1:]], None,
                                          _shard_window, lands0[0], "gather_start_rest", rels=NEAR_RELS, tids=rest)
    g_sems = list(sems0) + list(sems1)
    tok0 = g_tok[0:1, 0:1]
    lands_now = [None] * (DEPTH * NBIG)
    for i, a in zip(first + rest, list(lands0) + list(lands1)):
        lands_now[i] = a
    fwd_sems = {}
    fwd_plan = {0: (0,), 1: (1,), 2: (2,), 3: (3, 4, 5)}

    def gather_waiter(gi, l, gname, tids):
        def wait(after):
            if gi in fwd_plan:
                gis = fwd_plan[gi]
                flat = [i for g2 in gis for i in groups[g2]]
                where = {tid: k for k, tid in enumerate(flat)}
                fs, new_lands, ftok = _gather_forward(
                    [g_sems[g2] for g2 in gis], [lands_now[i] for i in flat],
                    [[where[i] for i in groups[g2]] for g2 in gis], flat, after, _shard_window, f"gather_forward_{gi}")
                for g2, s in zip(gis, fs):
                    fwd_sems[g2] = s
                for i, a in zip(flat, new_lands):
                    lands_now[i] = a
                after = ftok
            ids = [l * NBIG + t for t in tids]
            _, got = _xchg_wait(fwd_sems[gi], None, [lands_now[i] for i in ids], ids, after,
                                None, _shard_window, f"gather_wait_l{l}_{gname}", rels=FAR_RELS)
            out = {}
            for t, arr in zip(tids, got):
                nm = BIG[t][0]
                out[nm] = arr
            return out
        return wait

    pending = [{gname: gather_waiter(l * len(LAYER_GROUPS) + k, l, gname, tids)
                for k, (gname, tids) in enumerate(LAYER_GROUPS)} for l in range(DEPTH)]
    nbg = DEPTH * 3 * 128
    ncw = DEPTH * 3 * 1024
    flat_all = small_w_all.reshape(NDEV, -1)
    b_gate_full = jnp.transpose(flat_all[:, :nbg].reshape(NDEV, DEPTH, 3, 128), (1, 2, 0, 3)).reshape(DEPTH, 3, D)
    conv_w_full = jnp.transpose(flat_all[:, nbg:nbg + ncw].reshape(NDEV, DEPTH, 3, 1024), (1, 2, 0, 3)).reshape(DEPTH, 3, 2 * D_FF)

    ws = []
    for l in range(DEPTH):
        ws.append(_Weights(dict(
            b_gate=b_gate_full[l], conv_w=conv_w_full[l].reshape(3, 2, D_FF), conv_b=conv_b[l].reshape(2, D_FF),
            sinks=sinks[l].reshape(1, 8),
            attn_pre_norm=attn_pre_norm[l].reshape(1, D), attn_post_norm=attn_post_norm[l].reshape(1, D),
            ffn_pre_norm=ffn_pre_norm[l].reshape(1, D), ffn_post_norm=ffn_post_norm[l].reshape(1, D)), pending[l]))

    rs = {}

    group_tids = dict(LAYER_GROUPS)

    def start_scatter(l, gname, grads_l):
        tids = group_tids[gname]
        blocks, lands_rs = [], []
        for t in tids:
            nm, ax, ext = BIG[t]
            gfull = grads_l[nm].astype(BF16)
            shp = (NDEV, ext, gfull.shape[1]) if ax == 0 else (NDEV, gfull.shape[0], ext)
            blocks.append(gfull)
            lands_rs.append(lax.empty(shp, BF16))
        local = list(range(len(tids)))
        win = lambda j, ref, k: _shard_window(tids[j], ref, k)
        sems, s_thru, l_thru, tok = _xchg_start(blocks, lands_rs, [tuple(local)], win, _slot, me_arr,
                                                f"scatter_start_l{l}_{gname}")
        rs[(l, gname)] = (sems[0], s_thru, l_thru, win, local)
        return tok[0:1, 0:1]

    loss_local, grad_x, grads, g_rel = _local_step(x[0], loss_target[0], ws, rel_bias, tok0, start_scatter)
    loss = lax.psum(loss_local, ("x", "y", "c"))

    stack = lambda nm: jnp.stack([grads[l][nm] for l in range(DEPTH)], axis=0)
    small_g = {nm: (g_rel if nm == "rel_bias" else stack(nm)) for nm, _, _, _ in ROWPACK}
    small_repl = _rowpack(small_g, ROWPACK[:N_REPL])
    small_shard = _shard_rows(small_g)

    out_g, out_d, out_m, out_v = {}, {}, {}, {}
    prev = {nm: None for nm, _, _ in BIG}
    todo = [(l, gname) for l in reversed(range(DEPTH)) for gname in ("ffn", "mix", "in")]
    after, small_parts = grad_x, None
    for l, gname in todo:
        if (l, gname) == todo[-1]:
            small_parts = _exchange(
                [small_repl, small_shard],
                [jax.ShapeDtypeStruct((NDEV, ROWS_REPL, LANES), F32), jax.ShapeDtypeStruct((NDEV, ROWS_SHARD, LANES), F32)],
                lambda t, ref, q: ref if t == 0 else ref.at[q], _slot, "exchange_small_grads", after=after)
            after = small_parts[0]
        sems, s_thru, l_thru, win, local = rs[(l, gname)]
        owns, parts = _xchg_wait(sems, s_thru, l_thru, local, after, win, _slot, f"scatter_wait_l{l}_{gname}")
        for t, own, prt in zip(group_tids[gname], owns, parts):
            nm = BIG[t][0]
            rows = SHARD_ROWS.get(nm, PB[nm].shape[1])
            prev[nm] = _adamw(t, prt, own, me_arr, PB[nm], MB[nm], VB[nm], l, prev[nm], rows, f"adamw_{nm}_l{l}")
            after = prev[nm][1]
    for nm, _, _ in BIG:
        out_g[nm], out_d[nm], out_m[nm], out_v[nm] = [tr(a) if nm == "w_in" else a for a in prev[nm]]
    sm_g, sm_d, sm_m, sm_v = _small_update(small_parts[0], small_parts[1], _rowpack(P), _rowpack(M), _rowpack(V),
                                           "small_update")
    for dst, src in ((out_g, sm_g), (out_d, sm_d), (out_m, sm_m), (out_v, sm_v)):
        dst.update(src)

    order = ["rel_bias", "attn_pre_norm", "w_in", "b_gate", "sinks", "w_br_a", "w_br_b", "w_br_c", "w_out",
             "attn_post_norm", "ffn_pre_norm", "w_up", "conv_w", "conv_b", "w_down", "ffn_post_norm"]
    return (loss, grad_x[None], *[out_g[k] for k in order], *[out_d[k] for k in order],
            *[out_m[k] for k in order], *[out_v[k] for k in order])
```

```python
import functools
import math

import numpy as np
import jax
import jax.numpy as jnp
from jax import lax
from jax.experimental import pallas as pl
from jax.experimental.pallas import tpu as pltpu

F32 = jnp.float32
BF16 = jnp.bfloat16

S = 2048
D = 1024
DEPTH = 2
NDEV = 8
HD = 64
BLK = 128
NB = S // BLK
A_GROUPS = ((128, 1), (512, 4), (2048, 16))
NUM_BUCKETS = 32
MAX_DISTANCE = 2048
N_BIAS_HEADS = 20
D_FF = 4096
IN_COLS = 6912
QKV_COLS = 3840
QKV_SLABS = QKV_COLS // 128
GATE_COLS = 3072
EPS = 1e-6
SCALE = HD ** -0.5
NEG = -1e30
LANES = 128

ADAM_LR = 0.001
ADAM_B1 = 0.9
ADAM_B2 = 0.999
ADAM_EPS = 1e-08
ADAM_WD = 0.01
ADAM_STEP = 10

VMEM_LIMIT = 56 * 1024 * 1024
MESH = pl.DeviceIdType.MESH
ANY = pl.BlockSpec(memory_space=pl.ANY)
SMEM = pl.BlockSpec(memory_space=pltpu.SMEM)


def _cp(*sem):
    return pltpu.CompilerParams(dimension_semantics=sem if sem else None, vmem_limit_bytes=VMEM_LIMIT)


def _dot(a, b, ca, cb):
    return lax.dot_general(a, b, (((ca,), (cb,)), ((), ())), preferred_element_type=F32)


def _mm(a, b, *, grid, a_spec, b_spec, out_shape, out_spec, ca, cb, acc_shape, name,
        a_slab=False, b_slab=False, out_slab=False, alias_out=None, after=None):
    nk = grid[2]

    def body(*refs):
        a_ref, b_ref = refs[0], refs[1]
        o_ref, acc_ref = refs[-2], refs[-1]
        k = pl.program_id(2)

        def load(ref, slab):
            if slab:
                return jnp.concatenate([ref[s] for s in range(ref.shape[0])], axis=1).astype(BF16)
            return ref[...].astype(BF16)

        def write(val):
            if out_slab:
                for s in range(o_ref.shape[0]):
                    o_ref[s] = val[:, s * LANES:(s + 1) * LANES].astype(o_ref.dtype)
            else:
                o_ref[...] = val.astype(o_ref.dtype)

        d = _dot(load(a_ref, a_slab), load(b_ref, b_slab), ca, cb)
        if nk == 1:
            write(d)
        elif direct:
            @pl.when(k == 0)
            def _():
                o_ref[...] = d

            @pl.when(k > 0)
            def _():
                o_ref[...] += d
        else:
            @pl.when(k == 0)
            def _():
                acc_ref[...] = d

            if nk > 2:
                @pl.when((k > 0) & (k < nk - 1))
                def _():
                    acc_ref[...] += d

            @pl.when(k == nk - 1)
            def _():
                write(acc_ref[...] + d)

    direct = (not out_slab) and out_shape.dtype == F32
    if nk == 1 or direct:
        acc_shape = (8, LANES)
    in_specs = [a_spec, b_spec]
    args = [a, b]
    aliases = {}
    if alias_out is not None:
        in_specs.append(ANY)
        args.append(alias_out)
        aliases = {2: 0}
    if after is not None:
        in_specs.append(ANY)
        args.append(after)
    return pl.pallas_call(
        body, grid=grid, in_specs=in_specs, out_specs=out_spec, out_shape=out_shape,
        scratch_shapes=[pltpu.VMEM(acc_shape, F32)], input_output_aliases=aliases,
        compiler_params=_cp("parallel", "parallel", "arbitrary"), name=name)(*args)


def _mm_nn(a, b, out_dtype, tm, tn, tk, name):
    m, kk = a.shape
    n = b.shape[1]
    return _mm(a, b, grid=(m // tm, n // tn, kk // tk),
               a_spec=pl.BlockSpec((tm, tk), lambda i, j, k: (i, k)),
               b_spec=pl.BlockSpec((tk, tn), lambda i, j, k: (k, j)),
               out_shape=jax.ShapeDtypeStruct((m, n), out_dtype),
               out_spec=pl.BlockSpec((tm, tn), lambda i, j, k: (i, j)),
               ca=1, cb=0, acc_shape=(tm, tn), name=name)


def _mm_nt(a, b, out_dtype, tm, tn, tk, name):
    m, kk = a.shape
    n = b.shape[0]
    return _mm(a, b, grid=(m // tm, n // tn, kk // tk),
               a_spec=pl.BlockSpec((tm, tk), lambda i, j, k: (i, k)),
               b_spec=pl.BlockSpec((tn, tk), lambda i, j, k: (j, k)),
               out_shape=jax.ShapeDtypeStruct((m, n), out_dtype),
               out_spec=pl.BlockSpec((tm, tn), lambda i, j, k: (i, j)),
               ca=1, cb=1, acc_shape=(tm, tn), name=name)


def _mm_tn(a, b, out_dtype, tm, tn, tk, name):
    kk, m = a.shape
    n = b.shape[1]
    return _mm(a, b, grid=(m // tm, n // tn, kk // tk),
               a_spec=pl.BlockSpec((tk, tm), lambda i, j, k: (k, i)),
               b_spec=pl.BlockSpec((tk, tn), lambda i, j, k: (k, j)),
               out_shape=jax.ShapeDtypeStruct((m, n), out_dtype),
               out_spec=pl.BlockSpec((tm, tn), lambda i, j, k: (i, j)),
               ca=0, cb=0, acc_shape=(tm, tn), name=name)


ROW_TILE = 256


def _rms(x, g):
    r = lax.rsqrt(jnp.mean(x * x, axis=-1, keepdims=True) + EPS)
    return x * r * g


def _prenorm(x, g, name):
    def body(x_ref, g_ref, o_ref):
        o_ref[...] = _rms(x_ref[...], g_ref[...]).astype(BF16)

    return pl.pallas_call(
        body, grid=(S // ROW_TILE,),
        in_specs=[pl.BlockSpec((ROW_TILE, D), lambda i: (i, 0)), pl.BlockSpec((1, D), lambda i: (0, 0))],
        out_specs=pl.BlockSpec((ROW_TILE, D), lambda i: (i, 0)),
        out_shape=jax.ShapeDtypeStruct((S, D), BF16), compiler_params=_cp("parallel"), name=name)(x, g)


def _postnorm_res(x, f, g_post, g_next, name):
    def body(x_ref, f_ref, gp_ref, gn_ref, xo_ref, ho_ref):
        xn = x_ref[...] + _rms(f_ref[...], gp_ref[...])
        xo_ref[...] = xn
        ho_ref[...] = _rms(xn, gn_ref[...]).astype(BF16)

    row = pl.BlockSpec((ROW_TILE, D), lambda i: (i, 0))
    vec = pl.BlockSpec((1, D), lambda i: (0, 0))
    return pl.pallas_call(
        body, grid=(S // ROW_TILE,), in_specs=[row, row, vec, vec], out_specs=[row, row],
        out_shape=[jax.ShapeDtypeStruct((S, D), F32), jax.ShapeDtypeStruct((S, D), BF16)],
        compiler_params=_cp("parallel"), name=name)(x, f, g_post, g_next)


def _norm_bwd(f, g, dys, res, out_dtype, name):
    ndy = len(dys)
    has_res = res is not None

    def body(*refs):
        f_ref, g_ref = refs[0], refs[1]
        dy_refs = refs[2:2 + ndy]
        res_ref = refs[2 + ndy] if has_res else None
        o_ref, dg_ref = refs[-2], refs[-1]
        fv = f_ref[...]
        dy = dy_refs[0][...].astype(F32)
        for r in dy_refs[1:]:
            dy = dy + r[...].astype(F32)
        r = lax.rsqrt(jnp.mean(fv * fv, axis=-1, keepdims=True) + EPS)
        n = fv * r
        dn = dy * g_ref[...]
        df = r * (dn - n * jnp.mean(dn * n, axis=-1, keepdims=True))
        if has_res:
            df = df + res_ref[...]
        o_ref[...] = df.astype(out_dtype)

        @pl.when(pl.program_id(0) == 0)
        def _():
            dg_ref[...] = jnp.zeros((1, D), F32)

        dg_ref[...] += jnp.sum(dy * n, axis=0, keepdims=True)

    row = pl.BlockSpec((ROW_TILE, D), lambda i: (i, 0))
    vec = pl.BlockSpec((1, D), lambda i: (0, 0))
    in_specs = [row, vec] + [row] * ndy + ([row] if has_res else [])
    args = [f, g] + list(dys) + ([res] if has_res else [])
    return pl.pallas_call(
        body, grid=(S // ROW_TILE,), in_specs=in_specs, out_specs=[row, vec],
        out_shape=[jax.ShapeDtypeStruct((S, D), out_dtype), jax.ShapeDtypeStruct((1, D), F32)],
        compiler_params=_cp("arbitrary"), name=name)(*args)


def _rms_bwd_rows(fv, g, dy):
    r = lax.rsqrt(jnp.mean(fv * fv, axis=-1, keepdims=True) + EPS)
    n = fv * r
    dn = dy * g
    return r * (dn - n * jnp.mean(dn * n, axis=-1, keepdims=True)), dy * n


def _norm_bwd_chain(f1, g1, dys, res, f2, g2, name):
    ndy = len(dys)

    def body(*refs):
        f1_ref, g1_ref = refs[0], refs[1]
        dy_refs = refs[2:2 + ndy]
        res_ref, f2_ref, g2_ref = refs[2 + ndy:5 + ndy]
        o1_ref, o2_ref, dg1_ref, dg2_ref = refs[-4:]
        dy = dy_refs[0][...].astype(F32)
        for r in dy_refs[1:]:
            dy = dy + r[...].astype(F32)
        df1, c1 = _rms_bwd_rows(f1_ref[...], g1_ref[...], dy)
        out1 = df1 + res_ref[...]
        o1_ref[...] = out1
        df2, c2 = _rms_bwd_rows(f2_ref[...], g2_ref[...], out1)
        o2_ref[...] = df2.astype(BF16)

        @pl.when(pl.program_id(0) == 0)
        def _():
            dg1_ref[...] = jnp.zeros((1, D), F32)
            dg2_ref[...] = jnp.zeros((1, D), F32)

        dg1_ref[...] += jnp.sum(c1, axis=0, keepdims=True)
        dg2_ref[...] += jnp.sum(c2, axis=0, keepdims=True)

    row = pl.BlockSpec((ROW_TILE, D), lambda i: (i, 0))
    vec = pl.BlockSpec((1, D), lambda i: (0, 0))
    return pl.pallas_call(
        body, grid=(S // ROW_TILE,), in_specs=[row, vec] + [row] * ndy + [row, row, vec],
        out_specs=[row, row, vec, vec],
        out_shape=[jax.ShapeDtypeStruct((S, D), F32), jax.ShapeDtypeStruct((S, D), BF16),
                   jax.ShapeDtypeStruct((1, D), F32), jax.ShapeDtypeStruct((1, D), F32)],
        compiler_params=_cp("arbitrary"), name=name)(f1, g1, *dys, res, f2, g2)


def _loss_head(x, f, g, target, name):
    def body(x_ref, f_ref, g_ref, t_ref, dy_ref, l_ref, df_ref, dg_ref):
        fv = f_ref[...]
        e = x_ref[...] + _rms(fv, g_ref[...]) - t_ref[...]
        dy = e * (1.0 / D)
        dy_ref[...] = dy
        df, c = _rms_bwd_rows(fv, g_ref[...], dy)
        df_ref[...] = df.astype(BF16)

        @pl.when(pl.program_id(0) == 0)
        def _():
            l_ref[...] = jnp.zeros((8, LANES), F32)
            dg_ref[...] = jnp.zeros((1, D), F32)

        l_ref[...] += jnp.sum(e * e) * (0.5 / D)
        dg_ref[...] += jnp.sum(c, axis=0, keepdims=True)

    row = pl.BlockSpec((ROW_TILE, D), lambda i: (i, 0))
    vec = pl.BlockSpec((1, D), lambda i: (0, 0))
    return pl.pallas_call(
        body, grid=(S // ROW_TILE,), in_specs=[row, row, vec, row],
        out_specs=[row, pl.BlockSpec((8, LANES), lambda i: (0, 0)), row, vec],
        out_shape=[jax.ShapeDtypeStruct((S, D), F32), jax.ShapeDtypeStruct((8, LANES), F32),
                   jax.ShapeDtypeStruct((S, D), BF16), jax.ShapeDtypeStruct((1, D), F32)],
        compiler_params=_cp("arbitrary"), name=name)(x, f, g, target)


def _bucket_tiles():
    a = np.arange(BLK)[:, None]
    b = np.arange(2 * BLK)[None, :]
    dist = a + BLK - b
    out = np.zeros((4, 2, BLK, 2 * BLK), np.int32)
    cfg = [(w // d, d) for w, d in A_GROUPS] + [(BLK - 1, 1)]
    for gi, (max_dist, d) in enumerate(cfg):
        band = (dist >= 0) & (dist <= max_dist)
        tok = np.maximum(dist, 0) * d
        nf = np.maximum(tok, 1).astype(np.float32)
        max_exact = NUM_BUCKETS // 2
        large = max_exact + (np.log(nf / np.float32(max_exact)) / np.float32(math.log(MAX_DISTANCE / max_exact))
                             * np.float32(NUM_BUCKETS - max_exact)).astype(np.int32)
        large = np.minimum(large, NUM_BUCKETS - 1)
        bkt = np.where(tok < max_exact, tok, large).astype(np.int32)
        full = np.where(band, bkt, -1)
        out[gi, 1] = full
        out[gi, 0] = np.where(b >= BLK, full, -1)
    return out


def _bias_tiles(rel_bias, buckets, name):
    def body(tab_ref, bkt_ref, o_ref):
        h = pl.program_id(0)
        bkt = bkt_ref[...]
        acc = jnp.zeros(bkt.shape, F32)
        for bb in range(NUM_BUCKETS):
            acc = jnp.where(bkt == bb, tab_ref[bb, h], acc)
        o_ref[...] = jnp.where(bkt < 0, NEG, acc)

    return pl.pallas_call(
        body, grid=(N_BIAS_HEADS,),
        in_specs=[SMEM, pl.BlockSpec((None, 2, BLK, 2 * BLK), lambda h: (jnp.minimum(h // 4, 3), 0, 0, 0))],
        out_specs=pl.BlockSpec((None, 2, BLK, 2 * BLK), lambda h: (h, 0, 0, 0)),
        out_shape=jax.ShapeDtypeStruct((N_BIAS_HEADS, 2, BLK, 2 * BLK), F32),
        compiler_params=_cp("arbitrary"), name=name)(rel_bias, buckets)


def _bias_grad(gs, buckets, name):
    ng = len(gs)

    def body(*refs):
        g_refs = refs[:ng]
        bkt_ref, o_ref = refs[ng], refs[ng + 1]
        h = pl.program_id(0)
        g = g_refs[0][...]
        for r in g_refs[1:]:
            g = g + r[...]
        bkt = bkt_ref[...]
        row = lax.broadcasted_iota(jnp.int32, (NUM_BUCKETS, LANES), 0)
        lane = lax.broadcasted_iota(jnp.int32, (NUM_BUCKETS, LANES), 1)

        @pl.when(h == 0)
        def _():
            o_ref[...] = jnp.zeros((NUM_BUCKETS, LANES), F32)

        acc = o_ref[...]
        for bb in range(NUM_BUCKETS):
            s = jnp.sum(jnp.where(bkt == bb, g, 0.0))
            acc = jnp.where((row == bb) & (lane == h), s, acc)
        o_ref[...] = acc

    g_spec = pl.BlockSpec((None, BLK, 2 * BLK), lambda h: (h, 0, 0))
    return pl.pallas_call(
        body, grid=(N_BIAS_HEADS,),
        in_specs=[g_spec] * ng + [pl.BlockSpec((None, None, BLK, 2 * BLK), lambda h: (jnp.minimum(h // 4, 3), 1, 0, 0))],
        out_specs=pl.BlockSpec((NUM_BUCKETS, LANES), lambda h: (0, 0)),
        out_shape=jax.ShapeDtypeStruct((NUM_BUCKETS, LANES), F32),
        compiler_params=_cp("arbitrary"), name=name)(*gs, buckets)


def _to_class_major(src_ref, dst_refs, d, fn=None):
    ln = S // d
    for r in range(d):
        v = src_ref[pl.ds(r, ln, stride=d), :] if d > 1 else src_ref[...]
        outs = fn(v) if fn is not None else (v,) * len(dst_refs)
        for dst, o in zip(dst_refs, outs):
            dst[pl.ds(r * ln, ln), :] = o.astype(dst.dtype)


def _head_masks(rows):
    lane = lax.broadcasted_iota(jnp.int32, (rows, LANES), 1)
    return lane < HD, lane >= HD


def _split_heads(v):
    m0, m1 = _head_masks(v.shape[0])
    return jnp.where(m0, v, 0.0), jnp.where(m1, v, 0.0)


def _dup_head(v, hi):
    m0, _ = _head_masks(v.shape[0])
    r = pltpu.roll(v, HD, 1)
    return jnp.where(m0, jnp.where(hi, r, v), jnp.where(hi, v, r))


def _block_rows(b, d):
    nbc = NB // d
    i = b % nbc
    r = b // nbc
    has_prev = (i > 0).astype(jnp.int32)
    prev = pl.multiple_of(jnp.maximum(b - 1, 0) * BLK, BLK)
    nat = i * (BLK * d) + r
    return has_prev, prev, nat


def _lane_halves(v0, v1):
    lane = lax.broadcasted_iota(jnp.int32, (v0.shape[0], LANES), 1)
    return jnp.where(lane < HD, v0, v1)


def _band_fwd(proj, bias, sinks, *, d, q0, k0, v0, npairs, bias0, shared_kv, name):
    def body(sink_ref, q_ref, k_ref, v_ref, b_ref, num_ref, st_ref, qz0, qz1, ks, vs):
        p = pl.program_id(0)
        kv = (lambda v: (_dup_head(v, p >= 2),)) if shared_kv else None
        _to_class_major(q_ref, (qz0, qz1), d, lambda v: _split_heads(v * SCALE))
        _to_class_major(k_ref, (ks,), d, kv)
        _to_class_major(v_ref, (vs,), d, kv)
        lane = lax.broadcasted_iota(jnp.int32, (BLK, LANES), 1)

        def blk(b, carry):
            has_prev, prev, nat = _block_rows(b, d)
            cur = pl.multiple_of(b * BLK, BLK)
            k2 = jnp.concatenate([ks[pl.ds(prev, BLK), :], ks[pl.ds(cur, BLK), :]], axis=0)
            v2 = jnp.concatenate([vs[pl.ds(prev, BLK), :], vs[pl.ds(cur, BLK), :]], axis=0)
            nums, ms, ls = [], [], []
            for hh, qz in enumerate((qz0, qz1)):
                z = _dot(qz[pl.ds(cur, BLK), :], k2, 1, 1) + b_ref[hh, has_prev]
                m = jnp.max(z, axis=1, keepdims=True)
                e = jnp.exp(z - m)
                l = jnp.sum(e, axis=1, keepdims=True)
                num = _dot(e.astype(BF16), v2, 1, 0)
                if shared_kv:
                    sink = sink_ref[0, 2 * p + hh]
                    mx = jnp.maximum(m, sink)
                    c = jnp.exp(m - mx)
                    zden = l * c + jnp.exp(sink - mx)
                    num = num * (c / zden)
                    m = mx + jnp.log(zden)
                ls.append(l)
                ms.append(m)
                nums.append(num)
            num_t = jnp.where(lane < HD, nums[0], nums[1])
            if shared_kv:
                st_t = jnp.where(lane < HD, ms[0], ms[1])
            else:
                st_t = jnp.where(lane < 32, ms[0], jnp.where(lane < 64, ls[0], jnp.where(lane < 96, ms[1], ls[1])))
            if d > 1:
                num_ref[pl.ds(nat, BLK, stride=d), :] = num_t
                st_ref[pl.ds(nat, BLK, stride=d), :] = st_t
            else:
                num_ref[pl.ds(cur, BLK), :] = num_t
                st_ref[pl.ds(cur, BLK), :] = st_t
            return carry

        lax.fori_loop(0, NB, blk, 0, unroll=8)

    slab = lambda off, per_pair: pl.BlockSpec((None, S, LANES), (lambda p: (off + p, 0, 0)) if per_pair else (lambda p: (off, 0, 0)))
    out = pl.BlockSpec((None, S, LANES), lambda p: (p, 0, 0))
    return pl.pallas_call(
        body, grid=(npairs,),
        in_specs=[SMEM, slab(q0, True), slab(k0, not shared_kv), slab(v0, not shared_kv),
                  pl.BlockSpec((None, 2, 2, BLK, 2 * BLK), lambda p: (bias0 + p, 0, 0, 0, 0))],
        out_specs=[out, out],
        out_shape=[jax.ShapeDtypeStruct((npairs, S, LANES), F32)] * 2,
        scratch_shapes=[pltpu.VMEM((S, LANES), BF16)] * 4,
        compiler_params=_cp("arbitrary"), name=name)(sinks, proj, proj, proj, bias)


def _combine_a(nums, stats, name):
    rt = 512

    def body(n0, n1, n2, s0, s1, s2, o_ref, l_ref):
        n_refs, s_refs = (n0, n1, n2), (s0, s1, s2)
        outs, lses = [], []
        for hh in range(2):
            ms = [s[:, 64 * hh:64 * hh + 1] for s in s_refs]
            ls = [s[:, 64 * hh + 32:64 * hh + 33] for s in s_refs]
            mx = jnp.maximum(jnp.maximum(ms[0], ms[1]), ms[2])
            cs = [jnp.exp(m - mx) for m in ms]
            z = cs[0] * ls[0] + cs[1] * ls[1] + cs[2] * ls[2]
            acc = cs[0] * n_refs[0][:, hh * HD:(hh + 1) * HD]
            acc = acc + cs[1] * n_refs[1][:, hh * HD:(hh + 1) * HD]
            acc = acc + cs[2] * n_refs[2][:, hh * HD:(hh + 1) * HD]
            outs.append(acc / z)
            lses.append(mx + jnp.log(z))
        o_ref[...] = jnp.concatenate(outs, axis=1)
        l_ref[...] = _lane_halves(lses[0], lses[1])

    spec = pl.BlockSpec((None, rt, LANES), lambda p, i: (p, i, 0))
    return pl.pallas_call(
        body, grid=(2, S // rt), in_specs=[spec] * 6, out_specs=[spec, spec],
        out_shape=[jax.ShapeDtypeStruct((2, S, LANES), F32)] * 2,
        compiler_params=_cp("parallel", "parallel"), name=name)(*nums, *stats)


def _band_bwd(proj, bias, o, do, lse, sinks, *, d, q0, k0, v0, npairs, bias0, shared_kv, name):
    nkv = 1 if shared_kv else npairs

    def body(sink_ref, q_ref, k_ref, v_ref, b_ref, o_ref, do_ref, lse_ref,
             dq_ref, dk_ref, dv_ref, g_ref, ds_ref,
             qz0, qz1, ks, vs, doz0, doz1, ls0, ls1, dls0, dls1, stage, dq_nat, dk_cm, dv_cm, kv_nat, dk_acc, dv_acc):
        p = pl.program_id(0)
        m0, m1 = _head_masks(S)
        kk = lax.broadcasted_iota(jnp.int32, (2 * LANES, LANES), 0) % LANES
        ll = lax.broadcasted_iota(jnp.int32, (2 * LANES, LANES), 1)
        hi, lo = _split2(do_ref[...] * o_ref[...])
        dl = _dot(jnp.concatenate([hi, lo], axis=1), ((kk < HD) == (ll < HD)).astype(BF16), 1, 0)
        if shared_kv:
            row8 = lax.broadcasted_iota(jnp.int32, (8, LANES), 0)
            lane8 = lax.broadcasted_iota(jnp.int32, (8, LANES), 1)
            sinkv = jnp.where(m0, sink_ref[0, 2 * p], sink_ref[0, 2 * p + 1])
            contrib = jnp.exp(sinkv - lse_ref[...]) * dl
            t = jnp.zeros((8, LANES), F32)
            for hh, mh in enumerate((m0, m1)):
                dsink = -jnp.sum(jnp.where(mh, contrib, 0.0)) * (1.0 / HD)
                t = jnp.where((row8 == 0) & (lane8 == hh), dsink, t)
            ds_ref[...] = t
        else:
            ds_ref[...] = jnp.zeros((8, LANES), F32)
        kv = (lambda v: (_dup_head(v, p >= 2),)) if shared_kv else None
        _to_class_major(q_ref, (qz0, qz1), d, lambda v: _split_heads(v * SCALE))
        _to_class_major(k_ref, (ks,), d, kv)
        _to_class_major(v_ref, (vs,), d, kv)
        _to_class_major(do_ref, (doz0, doz1), d, _split_heads)
        def spread(v):
            a0, a1 = _head_masks(v.shape[0])
            r = pltpu.roll(v, HD, 1)
            return jnp.where(a0, v, r), jnp.where(a1, v, r)

        _to_class_major(lse_ref, (ls0, ls1), d, spread)
        stage[...] = dl
        _to_class_major(stage, (dls0, dls1), d, spread)

        dk_cm[...] = jnp.zeros((S, LANES), F32)
        dv_cm[...] = jnp.zeros((S, LANES), F32)
        g_ref[...] = jnp.zeros((2, BLK, 2 * BLK), F32)
        lane = lax.broadcasted_iota(jnp.int32, (BLK, LANES), 1)

        def blk(b, carry):
            has_prev, prev, nat = _block_rows(b, d)
            cur = pl.multiple_of(b * BLK, BLK)
            k2 = jnp.concatenate([ks[pl.ds(prev, BLK), :], ks[pl.ds(cur, BLK), :]], axis=0)
            v2 = jnp.concatenate([vs[pl.ds(prev, BLK), :], vs[pl.ds(cur, BLK), :]], axis=0)
            dqs, dks, dvs = [], [], []
            for hh, (qz, doz, lsr, dlr) in enumerate(((qz0, doz0, ls0, dls0), (qz1, doz1, ls1, dls1))):
                qb = qz[pl.ds(cur, BLK), :]
                dob = doz[pl.ds(cur, BLK), :]
                lb = lsr[pl.ds(cur, BLK), :]
                dlb = dlr[pl.ds(cur, BLK), :]
                z = _dot(qb, k2, 1, 1) + b_ref[hh, has_prev]
                pr = jnp.exp(z - jnp.concatenate([lb, lb], axis=1))
                dp = _dot(dob, v2, 1, 1)
                dz = pr * (dp - jnp.concatenate([dlb, dlb], axis=1))
                g_ref[hh] += dz
                dzb = dz.astype(BF16)
                dqs.append(_dot(dzb, k2, 1, 0))
                dks.append(_dot(dzb, qb, 0, 0))
                dvs.append(_dot(pr.astype(BF16), dob, 0, 0))
            dq_t = jnp.where(lane < HD, dqs[0], dqs[1]) * SCALE
            dk_t = dks[0] + dks[1]
            dv_t = dvs[0] + dvs[1]
            dk_cm[pl.ds(prev, BLK), :] += dk_t[:BLK]
            dk_cm[pl.ds(cur, BLK), :] += dk_t[BLK:]
            dv_cm[pl.ds(prev, BLK), :] += dv_t[:BLK]
            dv_cm[pl.ds(cur, BLK), :] += dv_t[BLK:]
            if d > 1:
                dq_nat[pl.ds(nat, BLK, stride=d), :] = dq_t
            else:
                dq_nat[pl.ds(cur, BLK), :] = dq_t
            return carry

        lax.fori_loop(0, NB, blk, 0, unroll=8)
        dq_ref[...] = dq_nat[...].astype(BF16)

        def from_class_major(src, dst_ref):
            if d == 1:
                dst_ref[...] = src[...].astype(BF16)
            else:
                ln = S // d
                for r in range(d):
                    kv_nat[pl.ds(r, ln, stride=d), :] = src[pl.ds(r * ln, ln), :]
                dst_ref[...] = kv_nat[...].astype(BF16)

        if not shared_kv:
            from_class_major(dk_cm, dk_ref)
            from_class_major(dv_cm, dv_ref)
        else:
            @pl.when(p == 0)
            def _():
                dk_acc[...] = jnp.zeros((S, LANES), F32)
                dv_acc[...] = jnp.zeros((S, LANES), F32)

            mine = m1 == (p >= 2)
            for cm, acc in ((dk_cm, dk_acc), (dv_cm, dv_acc)):
                val = cm[...]
                acc[...] += jnp.where(mine, val + pltpu.roll(val, HD, 1), 0.0)

            @pl.when(p == npairs - 1)
            def _():
                from_class_major(dk_acc, dk_ref)
                from_class_major(dv_acc, dv_ref)

    slab = lambda off, per_pair: pl.BlockSpec((None, S, LANES), (lambda p: (off + p, 0, 0)) if per_pair else (lambda p: (off, 0, 0)))
    pair = pl.BlockSpec((None, S, LANES), lambda p: (p, 0, 0))
    kv_out = pair if not shared_kv else pl.BlockSpec((None, S, LANES), lambda p: (0, 0, 0))
    return pl.pallas_call(
        body, grid=(npairs,),
        in_specs=[SMEM, slab(q0, True), slab(k0, not shared_kv), slab(v0, not shared_kv),
                  pl.BlockSpec((None, 2, 2, BLK, 2 * BLK), lambda p: (bias0 + p, 0, 0, 0, 0)),
                  pair, pair, pair],
        out_specs=[pair, kv_out, kv_out,
                   pl.BlockSpec((None, 2, BLK, 2 * BLK), lambda p: (p, 0, 0, 0)),
                   pl.BlockSpec((None, 8, LANES), lambda p: (p, 0, 0))],
        out_shape=[jax.ShapeDtypeStruct((npairs, S, LANES), BF16),
                   jax.ShapeDtypeStruct((nkv, S, LANES), BF16),
                   jax.ShapeDtypeStruct((nkv, S, LANES), BF16),
                   jax.ShapeDtypeStruct((npairs, 2, BLK, 2 * BLK), F32),
                   jax.ShapeDtypeStruct((npairs, 8, LANES), F32)],
        scratch_shapes=[pltpu.VMEM((S, LANES), BF16)] * 6 + [pltpu.VMEM((S, LANES), F32)] * 11,
        compiler_params=_cp("arbitrary"), name=name)(sinks, proj, proj, proj, bias, o, do, lse)


KC = 512
NSUB = KC // BLK
QB = 512
QPG = KC // QB


def _split2(x):
    hi = x.astype(BF16)
    lo = (x - hi.astype(F32)).astype(BF16)
    return hi, lo


def _tri_ones(cmp):
    jj = lax.broadcasted_iota(jnp.int32, (2 * BLK, BLK), 0) % BLK
    ss = lax.broadcasted_iota(jnp.int32, (2 * BLK, BLK), 1)
    return jnp.concatenate([cmp(jj, ss).astype(BF16), jnp.ones((2 * BLK, BLK), BF16)], axis=1)


def _sub_sums(x, tri1):
    n = x.shape[0]
    st = jnp.concatenate([x[:, s * BLK:(s + 1) * BLK] for s in range(NSUB)], axis=0)
    hi, lo = _split2(st)
    r = _dot(jnp.concatenate([hi, lo], axis=1), tri1, 1, 0)
    return ([r[s * n:(s + 1) * n, :BLK] for s in range(NSUB)], [r[s * n:(s + 1) * n, BLK:] for s in range(NSUB)])


def _log_sig_pair(z):
    lb = jnp.minimum(z, 0.0) - jnp.log1p(jnp.exp(-jnp.abs(z)))
    return lb, lb - z


QGROUPS = NB // NSUB


def _stick_fwd(proj, *, q0, k0, v0, name):
    def body(q_ref, k_ref, v_ref, o_ref, t_ref, qs, ks, vs):
        qs[...] = (q_ref[...] * SCALE).astype(BF16)
        ks[...] = k_ref[...].astype(BF16)
        vs[...] = v_ref[...].astype(BF16)
        tri1 = _tri_ones(lambda j, s: j > s)
        col = lax.broadcasted_iota(jnp.int32, (QB, KC), 1)
        rowi = lax.broadcasted_iota(jnp.int32, (QB, KC), 0)

        for qg in range(QGROUPS):
            def qblock(ii, carry0, qg=qg):
                t0 = pl.multiple_of((qg * QPG + ii) * QB, QB)
                qb = qs[pl.ds(t0, QB), :]
                accs = [jnp.zeros((QB, HD), F32)] * 2
                runs = [jnp.zeros((QB, BLK), F32)] * 2
                for c in reversed(range(qg + 1)):
                    s0 = c * KC
                    diag = c == qg
                    before = (s0 + col) < (t0 + rowi) if diag else None
                    for hh in range(2):
                        kh = ks[s0:s0 + KC, hh * HD:(hh + 1) * HD]
                        vh = vs[s0:s0 + KC, hh * HD:(hh + 1) * HD]
                        lb, lk = _log_sig_pair(_dot(qb[:, hh * HD:(hh + 1) * HD], kh, 1, 1))
                        if diag:
                            lk = jnp.where(before, lk, 0.0)
                        suf, tot = _sub_sums(lk, tri1)
                        ws, run = [], runs[hh]
                        for s in reversed(range(NSUB)):
                            ws.append(jnp.exp(lb[:, s * BLK:(s + 1) * BLK] + suf[s] + run))
                            run = run + tot[s]
                        w = jnp.concatenate(ws[::-1], axis=1)
                        if diag:
                            w = jnp.where(before, w, 0.0)
                        accs[hh] = accs[hh] + _dot(w.astype(BF16), vh, 1, 0)
                        runs[hh] = run
                o_ref[pl.ds(t0, QB), :] = jnp.concatenate(accs, axis=1)
                t_ref[pl.ds(t0, QB), :] = _lane_halves(runs[0], runs[1])
                return carry0

            lax.fori_loop(0, QPG, qblock, 0)

    slab = lambda off: pl.BlockSpec((None, S, LANES), lambda p: (off + p, 0, 0))
    out = pl.BlockSpec((None, S, LANES), lambda p: (p, 0, 0))
    return pl.pallas_call(
        body, grid=(2,), in_specs=[slab(q0), slab(k0), slab(v0)], out_specs=[out, out],
        out_shape=[jax.ShapeDtypeStruct((2, S, LANES), F32)] * 2,
        scratch_shapes=[pltpu.VMEM((S, LANES), BF16)] * 3,
        compiler_params=_cp("arbitrary"), name=name)(proj, proj, proj)


def _stick_bwd(proj, do, tot, *, q0, k0, v0, name):
    def body(q_ref, k_ref, v_ref, do_ref, t_ref, dq_ref, dk_ref, dv_ref, qs, ks, vs, dos, dk_acc, dv_acc):
        qs[...] = (q_ref[...] * SCALE).astype(BF16)
        ks[...] = k_ref[...].astype(BF16)
        vs[...] = v_ref[...].astype(BF16)
        dos[...] = do_ref[...].astype(BF16)
        dk_acc[...] = jnp.zeros((2, S, HD), F32)
        dv_acc[...] = jnp.zeros((2, S, HD), F32)
        tri_inc = _tri_ones(lambda j, s: j <= s)
        tri_exc = _tri_ones(lambda j, s: j < s)
        col = lax.broadcasted_iota(jnp.int32, (QB, KC), 1)
        rowi = lax.broadcasted_iota(jnp.int32, (QB, KC), 0)

        for qg in range(QGROUPS):
            def qblock(ii, carry0, qg=qg):
                t0 = pl.multiple_of((qg * QPG + ii) * QB, QB)
                qb = qs[pl.ds(t0, QB), :]
                dob = dos[pl.ds(t0, QB), :]
                tb = t_ref[pl.ds(t0, QB), :]
                dqs = [jnp.zeros((QB, HD), F32)] * 2
                pruns = [jnp.zeros((QB, BLK), F32)] * 2
                eruns = [jnp.zeros((QB, BLK), F32)] * 2
                for c in range(qg + 1):
                    s0 = c * KC
                    diag = c == qg
                    before = (s0 + col) < (t0 + rowi) if diag else None
                    for hh in range(2):
                        qh = qb[:, hh * HD:(hh + 1) * HD]
                        doh = dob[:, hh * HD:(hh + 1) * HD]
                        tt = tb[:, 64 * hh:64 * hh + 1]
                        kh = ks[s0:s0 + KC, hh * HD:(hh + 1) * HD]
                        vh = vs[s0:s0 + KC, hh * HD:(hh + 1) * HD]
                        lb, lk = _log_sig_pair(_dot(qh, kh, 1, 1))
                        if diag:
                            lk = jnp.where(before, lk, 0.0)
                        pin, ptot = _sub_sums(lk, tri_inc)
                        ws, prun = [], pruns[hh]
                        for s in range(NSUB):
                            ws.append(jnp.exp(lb[:, s * BLK:(s + 1) * BLK] + (tt - (pin[s] + prun))))
                            prun = prun + ptot[s]
                        w = jnp.concatenate(ws, axis=1)
                        if diag:
                            w = jnp.where(before, w, 0.0)
                        e = w * _dot(doh, vh, 1, 1)
                        pex, etot = _sub_sums(e, tri_exc)
                        cs, erun = [], eruns[hh]
                        for s in range(NSUB):
                            cs.append(pex[s] + erun)
                            erun = erun + etot[s]
                        sig = jnp.exp(lb)
                        dz = e * (1.0 - sig) - jnp.concatenate(cs, axis=1) * sig
                        if diag:
                            dz = jnp.where(before, dz, 0.0)
                        dz = dz.astype(BF16)
                        dqs[hh] = dqs[hh] + _dot(dz, kh, 1, 0)
                        dk_acc[hh, s0:s0 + KC, :] += _dot(dz, qh, 0, 0)
                        dv_acc[hh, s0:s0 + KC, :] += _dot(w.astype(BF16), doh, 0, 0)
                        pruns[hh], eruns[hh] = prun, erun
                dq_ref[pl.ds(t0, QB), :] = (jnp.concatenate(dqs, axis=1) * SCALE).astype(BF16)
                return carry0

            lax.fori_loop(0, QPG, qblock, 0)
        dk_ref[...] = jnp.concatenate([dk_acc[0], dk_acc[1]], axis=1).astype(BF16)
        dv_ref[...] = jnp.concatenate([dv_acc[0], dv_acc[1]], axis=1).astype(BF16)

    slab = lambda off: pl.BlockSpec((None, S, LANES), lambda p: (off + p, 0, 0))
    pair = pl.BlockSpec((None, S, LANES), lambda p: (p, 0, 0))
    return pl.pallas_call(
        body, grid=(2,), in_specs=[slab(q0), slab(k0), slab(v0), pair, pair], out_specs=[pair] * 3,
        out_shape=[jax.ShapeDtypeStruct((2, S, LANES), BF16)] * 3,
        scratch_shapes=[pltpu.VMEM((S, LANES), BF16)] * 4 + [pltpu.VMEM((2, S, HD), F32)] * 2,
        compiler_params=_cp("arbitrary"), name=name)(proj, proj, proj, do, tot)


def _cat_slabs(ref):
    return jnp.concatenate([ref[s] for s in range(ref.shape[0])], axis=1)


def _merge_fwd(o_a, o_b, o_c, gates, b_gate, wa, wb, wc, w_out, name):
    tm = ROW_TILE

    def body(oa_ref, ob_ref, oc_ref, g_ref, bg_ref, wa_ref, wb_ref, wc_ref, wo_ref, mg_ref, mo_ref):
        acc = jnp.zeros((tm, D), F32)
        for i, (o_ref, w_ref) in enumerate(((oa_ref, wa_ref), (ob_ref, wb_ref), (oc_ref, wc_ref))):
            pr = _dot(_cat_slabs(o_ref).astype(BF16), w_ref[...], 1, 0)
            sg = jax.nn.sigmoid(g_ref[:, i * D:(i + 1) * D] + bg_ref[i:i + 1, :])
            acc = acc + sg * pr
        mg = acc.astype(BF16)
        mg_ref[...] = mg
        mo_ref[...] = _dot(mg, wo_ref[...], 1, 0)

    slabs = lambda n: pl.BlockSpec((n, tm, LANES), lambda i: (0, i, 0))
    full = lambda r, c: pl.BlockSpec((r, c), lambda i: (0, 0))
    row = pl.BlockSpec((tm, D), lambda i: (i, 0))
    return pl.pallas_call(
        body, grid=(S // tm,),
        in_specs=[slabs(2), slabs(4), slabs(2), pl.BlockSpec((tm, GATE_COLS), lambda i: (i, 0)), full(3, D),
                  full(256, D), full(512, D), full(256, D), full(D, D)],
        out_specs=[row, row],
        out_shape=[jax.ShapeDtypeStruct((S, D), BF16), jax.ShapeDtypeStruct((S, D), F32)],
        compiler_params=_cp("parallel"), name=name)(o_a, o_b, o_c, gates, b_gate, wa, wb, wc, w_out)


def _merge_bwd(d_mo, o_a, o_b, o_c, gates, b_gate, wa, wb, wc, w_out, name):
    tm = ROW_TILE

    def body(dmo_ref, oa_ref, ob_ref, oc_ref, g_ref, bg_ref, wa_ref, wb_ref, wc_ref, wo_ref,
             doa_ref, dob_ref, doc_ref, dg_ref, dwa_ref, dwb_ref, dwc_ref, dbg_ref):
        @pl.when(pl.program_id(0) == 0)
        def _():
            dwa_ref[...] = jnp.zeros(dwa_ref.shape, F32)
            dwb_ref[...] = jnp.zeros(dwb_ref.shape, F32)
            dwc_ref[...] = jnp.zeros(dwc_ref.shape, F32)
            dbg_ref[...] = jnp.zeros(dbg_ref.shape, F32)

        dmg = _dot(dmo_ref[...], wo_ref[...], 1, 1)
        trip = ((oa_ref, wa_ref, doa_ref, dwa_ref), (ob_ref, wb_ref, dob_ref, dwb_ref), (oc_ref, wc_ref, doc_ref, dwc_ref))
        for i, (o_ref, w_ref, do_ref, dw_ref) in enumerate(trip):
            ob = _cat_slabs(o_ref).astype(BF16)
            pr = _dot(ob, w_ref[...], 1, 0)
            sg = jax.nn.sigmoid(g_ref[:, i * D:(i + 1) * D] + bg_ref[i:i + 1, :])
            dgate = dmg * pr * sg * (1.0 - sg)
            dg_ref[:, i * D:(i + 1) * D] = dgate.astype(BF16)
            dbg_ref[i:i + 1, :] += jnp.sum(dgate, axis=0, keepdims=True)
            dpr = (dmg * sg).astype(BF16)
            do = _dot(dpr, w_ref[...], 1, 1)
            for s in range(do_ref.shape[0]):
                do_ref[s] = do[:, s * LANES:(s + 1) * LANES]
            dw_ref[...] += _dot(ob, dpr, 0, 0)

    slabs = lambda n: pl.BlockSpec((n, tm, LANES), lambda i: (0, i, 0))
    full = lambda r, c: pl.BlockSpec((r, c), lambda i: (0, 0))
    row = pl.BlockSpec((tm, D), lambda i: (i, 0))
    return pl.pallas_call(
        body, grid=(S // tm,),
        in_specs=[row, slabs(2), slabs(4), slabs(2), pl.BlockSpec((tm, GATE_COLS), lambda i: (i, 0)), full(3, D),
                  full(256, D), full(512, D), full(256, D), full(D, D)],
        out_specs=[slabs(2), slabs(4), slabs(2), pl.BlockSpec((tm, GATE_COLS), lambda i: (i, 0)),
                   full(256, D), full(512, D), full(256, D), full(3, D)],
        out_shape=[jax.ShapeDtypeStruct((2, S, LANES), F32), jax.ShapeDtypeStruct((4, S, LANES), F32),
                   jax.ShapeDtypeStruct((2, S, LANES), F32), jax.ShapeDtypeStruct((S, GATE_COLS), BF16),
                   jax.ShapeDtypeStruct((256, D), F32), jax.ShapeDtypeStruct((512, D), F32),
                   jax.ShapeDtypeStruct((256, D), F32), jax.ShapeDtypeStruct((3, D), F32)],
        compiler_params=_cp("arbitrary"), name=name)(d_mo, o_a, o_b, o_c, gates, b_gate, wa, wb, wc, w_out)


FC = 256
GELU_K = math.sqrt(2.0 / math.pi)
GELU_C = 0.044715


RC = 64
NRC = S // RC


def _down(tail, cur, n):
    row = lax.broadcasted_iota(jnp.int32, tail.shape, 0)
    rolled = pltpu.roll(cur, n, 0)
    first = jnp.where(row < n, pltpu.roll(tail, n, 0), rolled[0:8])
    return jnp.concatenate([first, rolled[8:]], axis=0)


def _up(cur, head, n):
    row = lax.broadcasted_iota(jnp.int32, head.shape, 0)
    rolled = pltpu.roll(cur, RC - n, 0)
    last = jnp.where(row >= 8 - n, pltpu.roll(head, 8 - n, 0), rolled[RC - 8:])
    return jnp.concatenate([rolled[:RC - 8], last], axis=0)


def _conv_chunk(load, j, w_ref, b_ref, half):
    r0 = pl.multiple_of(j * RC, RC)
    cur = load(r0, RC).astype(F32)
    tail = load(pl.multiple_of(jnp.maximum(r0 - 16, 0), 16), 16).astype(F32)[8:16]
    tail = jnp.where(j > 0, tail, 0.0)
    d1 = _down(tail, cur, 1)
    d2 = _down(tail, cur, 2)
    y = w_ref[0:1, half, :] * d2 + w_ref[1:2, half, :] * d1 + w_ref[2:3, half, :] * cur + b_ref[half:half + 1, :]
    return y, cur, d1, d2


def _chunk(j):
    return pl.ds(pl.multiple_of(j * RC, RC), RC)


def _fold8(x):
    return jnp.sum(x.reshape(RC // 8, 8, x.shape[-1]), axis=0)


def _ffn_act(u, conv_w, conv_b, name):
    def body(u_ref, w_ref, b_ref, a_ref, y_ref):
        def step(j, carry):
            yg = _conv_chunk(lambda r, n: u_ref[0, pl.ds(r, n), :], j, w_ref, b_ref, 0)[0]
            yv = _conv_chunk(lambda r, n: u_ref[1, pl.ds(r, n), :], j, w_ref, b_ref, 1)[0]
            th = jnp.tanh(GELU_K * (yg + GELU_C * yg * yg * yg))
            a_ref[_chunk(j), :] = (0.5 * yg * (1.0 + th) * yv).astype(BF16)
            y_ref[0, _chunk(j), :] = yg.astype(BF16)
            y_ref[1, _chunk(j), :] = yv.astype(BF16)
            return carry

        lax.fori_loop(0, NRC, step, 0)

    return pl.pallas_call(
        body, grid=(D_FF // FC,),
        in_specs=[pl.BlockSpec((2, S, FC), lambda j: (0, 0, j)), pl.BlockSpec((3, 2, FC), lambda j: (0, 0, j)),
                  pl.BlockSpec((2, FC), lambda j: (0, j))],
        out_specs=[pl.BlockSpec((S, FC), lambda j: (0, j)), pl.BlockSpec((2, S, FC), lambda j: (0, 0, j))],
        out_shape=[jax.ShapeDtypeStruct((S, D_FF), BF16), jax.ShapeDtypeStruct((2, S, D_FF), BF16)],
        compiler_params=_cp("parallel"), name=name)(u, conv_w, conv_b)


def _ffn_act_bwd(u, y, d_a, conv_w, name):
    def body(u_ref, y_ref, da_ref, w_ref, du_ref, dw_ref, db_ref, dy_s):
        def first(j, acc):
            yg = y_ref[0, _chunk(j), :].astype(F32)
            yv = y_ref[1, _chunk(j), :].astype(F32)
            th = jnp.tanh(GELU_K * (yg + GELU_C * yg * yg * yg))
            gelu = 0.5 * yg * (1.0 + th)
            dgelu = 0.5 * (1.0 + th) + 0.5 * yg * (1.0 - th * th) * GELU_K * (1.0 + 3.0 * GELU_C * yg * yg)
            da = da_ref[_chunk(j), :].astype(F32)
            dyg = da * yv * dgelu
            dyv = da * gelu
            dy_s[0, _chunk(j), :] = dyg
            dy_s[1, _chunk(j), :] = dyv
            return acc[0] + _fold8(dyg), acc[1] + _fold8(dyv)

        zero = jnp.zeros((8, FC), F32)
        accb = lax.fori_loop(0, NRC, first, (zero, zero))
        for half in range(2):
            db_ref[half:half + 1, :] = jnp.sum(accb[half], axis=0, keepdims=True)

        def second(j, acc):
            new = []
            for half in range(2):
                cur = dy_s[half, _chunk(j), :]
                h0 = pl.multiple_of(jnp.minimum((j + 1) * RC, S - 8), 8)
                head = jnp.where(j < NRC - 1, dy_s[half, pl.ds(h0, 8), :], 0.0)
                up1 = _up(cur, head, 1)
                up2 = _up(cur, head, 2)
                du = w_ref[2:3, half, :] * cur + w_ref[1:2, half, :] * up1 + w_ref[0:1, half, :] * up2
                du_ref[half, _chunk(j), :] = du.astype(BF16)
                uu = u_ref[half, _chunk(j), :].astype(F32)
                new += [_fold8(up2 * uu), _fold8(up1 * uu), _fold8(cur * uu)]
            return tuple(a + n for a, n in zip(acc, new))

        accw = lax.fori_loop(0, NRC, second, tuple(zero for _ in range(6)))
        for half in range(2):
            for k in range(3):
                dw_ref[k:k + 1, half, :] = jnp.sum(accw[3 * half + k], axis=0, keepdims=True)

    return pl.pallas_call(
        body, grid=(D_FF // FC,),
        in_specs=[pl.BlockSpec((2, S, FC), lambda j: (0, 0, j)), pl.BlockSpec((2, S, FC), lambda j: (0, 0, j)),
                  pl.BlockSpec((S, FC), lambda j: (0, j)), pl.BlockSpec((3, 2, FC), lambda j: (0, 0, j))],
        out_specs=[pl.BlockSpec((2, S, FC), lambda j: (0, 0, j)), pl.BlockSpec((3, 2, FC), lambda j: (0, 0, j)),
                   pl.BlockSpec((2, FC), lambda j: (0, j))],
        out_shape=[jax.ShapeDtypeStruct((2, S, D_FF), BF16), jax.ShapeDtypeStruct((3, 2, D_FF), F32),
                   jax.ShapeDtypeStruct((2, D_FF), F32)],
        scratch_shapes=[pltpu.VMEM((2, S, FC), F32)],
        compiler_params=_cp("parallel"), name=name)(u, y, d_a, conv_w)


def _layer_fwd(x, h1, w, bias, lname):
    n = lambda s: f"{lname}_{s}"
    w.need("in", h1)
    tn = 768
    proj = _mm(h1, w["w_in"], grid=(1, QKV_COLS // tn, 1),
               a_spec=pl.BlockSpec((S, D), lambda i, j, k: (i, 0)),
               b_spec=pl.BlockSpec((tn, D), lambda i, j, k: (j, 0)),
               out_shape=jax.ShapeDtypeStruct((QKV_SLABS, S, LANES), F32),
               out_spec=pl.BlockSpec((tn // LANES, S, LANES), lambda i, j, k: (j, i, 0)),
               ca=1, cb=1, acc_shape=(S, tn), out_slab=True, name=n("proj_qkv"))
    gates = _mm(h1, w["w_in"], grid=(1, GATE_COLS // tn, 1),
                a_spec=pl.BlockSpec((S, D), lambda i, j, k: (i, 0)),
                b_spec=pl.BlockSpec((tn, D), lambda i, j, k: (j + QKV_COLS // tn, 0)),
                out_shape=jax.ShapeDtypeStruct((S, GATE_COLS), BF16),
                out_spec=pl.BlockSpec((S, tn), lambda i, j, k: (i, j)),
                ca=1, cb=1, acc_shape=(S, tn), name=n("proj_gate"))
    nums, stats = [], []
    for g, (_, d) in enumerate(A_GROUPS):
        nm, st = _band_fwd(proj, bias, w["sinks"], d=d, q0=2 * g, k0=6 + 2 * g, v0=12 + 2 * g, npairs=2, bias0=2 * g,
                           shared_kv=False, name=n(f"attn_a{g}_fwd"))
        nums.append(nm)
        stats.append(st)
    o_a, lse_a = _combine_a(nums, stats, n("attn_a_combine"))
    o_b, lse_b = _band_fwd(proj, bias, w["sinks"], d=1, q0=18, k0=22, v0=23, npairs=4, bias0=6, shared_kv=True,
                           name=n("attn_b_fwd"))
    o_c, tot_c = _stick_fwd(proj, q0=24, k0=26, v0=28, name=n("attn_c_fwd"))
    w.need("mix", tot_c)
    merged, mo = _merge_fwd(o_a, o_b, o_c, gates, w["b_gate"], w["w_br_a"], w["w_br_b"], w["w_br_c"], w["w_out"], n("merge_fwd"))
    x2, h2 = _postnorm_res(x, mo, w["attn_post_norm"], w["ffn_pre_norm"], n("attn_post"))
    w.need("ffn", h2)
    u = _mm(h2, w["w_up"], grid=(1, 2 * D_FF // 1024, 1),
            a_spec=pl.BlockSpec((S, D), lambda i, j, k: (i, 0)),
            b_spec=pl.BlockSpec((D, 1024), lambda i, j, k: (0, j)),
            out_shape=jax.ShapeDtypeStruct((2, S, D_FF), BF16),
            out_spec=pl.BlockSpec((None, S, 1024), lambda i, j, k: (j // 4, i, j % 4)),
            ca=1, cb=0, acc_shape=(S, 1024), name=n("ffn_up"))
    a, y = _ffn_act(u, w["conv_w"], w["conv_b"], n("ffn_act"))
    fo = _mm_nn(a, w["w_down"], F32, 1024, 1024, 2048, n("ffn_down"))
    saved = dict(x=x, h1=h1, proj=proj, gates=gates, o_a=o_a, lse_a=lse_a, o_b=o_b, lse_b=lse_b, o_c=o_c, tot_c=tot_c,
                 merged=merged, mo=mo, x2=x2, h2=h2, u=u, y=y, a=a, fo=fo)
    return saved


def _layer_bwd(dx3, sv, w, bias, lname, tok=None, on_part=None, d_fo=None, below=None):
    n = lambda s: f"{lname}_{s}"
    g = {}

    def part(group, vec):
        t = on_part(group, g) if on_part is not None else None
        return vec if t is None else vec + t

    if d_fo is None:
        gain = w["ffn_post_norm"] if tok is None else w["ffn_post_norm"] + tok
        d_fo, g["ffn_post_norm"] = _norm_bwd(sv["fo"], gain, [dx3], None, BF16, n("ffn_post_bwd"))
    else:
        d_fo, g["ffn_post_norm"] = d_fo
    d_a = _mm_nt(d_fo, w["w_down"], BF16, S, 1024, 1024, n("ffn_down_bwd_x"))
    g["w_down"] = _mm_tn(sv["a"], d_fo, BF16, 1024, 1024, S, n("ffn_down_bwd_w"))
    d_u, dcw, dcb = _ffn_act_bwd(sv["u"], sv["y"], d_a, w["conv_w"], n("ffn_act_bwd"))
    g["conv_w"] = dcw.reshape(3, 2 * D_FF)
    g["conv_b"] = dcb.reshape(1, 2 * D_FF)
    g["w_up"] = _mm(sv["h2"], d_u, grid=(1, 2 * D_FF // 1024, 1),
                    a_spec=pl.BlockSpec((S, D), lambda i, j, k: (k, 0)),
                    b_spec=pl.BlockSpec((None, S, 1024), lambda i, j, k: (j // 4, k, j % 4)),
                    out_shape=jax.ShapeDtypeStruct((D, 2 * D_FF), BF16),
                    out_spec=pl.BlockSpec((D, 1024), lambda i, j, k: (0, j)),
                    ca=0, cb=0, acc_shape=(D, 1024), name=n("ffn_up_bwd_w"))
    tok_ffn = on_part("ffn", g) if on_part is not None else None
    d_h2 = _mm(d_u, w["w_up"], grid=(S // 1024, 1, 2),
               a_spec=pl.BlockSpec((None, 1024, D_FF), lambda i, j, k: (k, i, 0)),
               b_spec=pl.BlockSpec((D, D_FF), lambda i, j, k: (0, k)),
               out_shape=jax.ShapeDtypeStruct((S, D), F32),
               out_spec=pl.BlockSpec((1024, D), lambda i, j, k: (i, 0)),
               ca=1, cb=1, acc_shape=(1024, D), after=tok_ffn, name=n("ffn_up_bwd_x"))
    dx2, d_mo, g["ffn_pre_norm"], g["attn_post_norm"] = _norm_bwd_chain(
        sv["x2"], w["ffn_pre_norm"], [d_h2], dx3, sv["mo"], w["attn_post_norm"], n("ffn_pre_attn_post_bwd"))
    g["w_out"] = _mm_tn(sv["merged"], d_mo, BF16, 1024, 1024, S, n("out_bwd_w"))
    do_a, do_b, do_c, d_gates, dwa, dwb, dwc, g["b_gate"] = _merge_bwd(
        d_mo, sv["o_a"], sv["o_b"], sv["o_c"], sv["gates"], w["b_gate"], w["w_br_a"], w["w_br_b"], w["w_br_c"],
        w["w_out"], n("merge_bwd"))
    g["w_br_a"], g["w_br_b"], g["w_br_c"] = dwa, dwb, dwc
    sinks = part("mix", w["sinks"])
    proj = sv["proj"]
    dqa, dka, dva, gbias = [], [], [], []
    for gi, (_, d) in enumerate(A_GROUPS):
        dq, dk, dv, gg, _ = _band_bwd(proj, bias, sv["o_a"], do_a, sv["lse_a"], sinks, d=d, q0=2 * gi, k0=6 + 2 * gi,
                                      v0=12 + 2 * gi, npairs=2, bias0=2 * gi, shared_kv=False, name=n(f"attn_a{gi}_bwd"))
        dqa.append(dq), dka.append(dk), dva.append(dv), gbias.append(gg)
    dqb, dkb, dvb, ggb, dsink = _band_bwd(proj, bias, sv["o_b"], do_b, sv["lse_b"], sinks, d=1, q0=18, k0=22, v0=23,
                                          npairs=4, bias0=6, shared_kv=True, name=n("attn_b_bwd"))
    gbias.append(ggb)
    g["bias_g"] = jnp.concatenate(gbias, axis=0).reshape(N_BIAS_HEADS, BLK, 2 * BLK)
    g["sinks"] = dsink[:, 0, :2].reshape(1, 8)
    dqc, dkc, dvc = _stick_bwd(proj, do_c, sv["tot_c"], q0=24, k0=26, v0=28, name=n("attn_c_bwd"))
    dqkv = jnp.concatenate(dqa + dka + dva + [dqb, dkb, dvb, dqc, dkc, dvc], axis=0)
    ts = 6
    tsx = QKV_SLABS
    dw_in = _mm(dqkv, sv["h1"], grid=(QKV_SLABS // ts, 1, 1),
                a_spec=pl.BlockSpec((ts, S, LANES), lambda i, j, k: (i, k, 0)),
                b_spec=pl.BlockSpec((S, D), lambda i, j, k: (k, 0)),
                out_shape=jax.ShapeDtypeStruct((IN_COLS, D), BF16),
                out_spec=pl.BlockSpec((ts * LANES, D), lambda i, j, k: (i, 0)),
                ca=0, cb=0, acc_shape=(ts * LANES, D), a_slab=True, name=n("in_bwd_w_qkv"))
    g["w_in"] = _mm(d_gates, sv["h1"], grid=(GATE_COLS // 768, 1, 1),
                    a_spec=pl.BlockSpec((S, 768), lambda i, j, k: (k, i)),
                    b_spec=pl.BlockSpec((S, D), lambda i, j, k: (k, 0)),
                    out_shape=jax.ShapeDtypeStruct((IN_COLS, D), BF16),
                    out_spec=pl.BlockSpec((768, D), lambda i, j, k: (i + QKV_COLS // 768, 0)),
                    ca=0, cb=0, acc_shape=(768, D), alias_out=dw_in, name=n("in_bwd_w_gate"))
    tok_in = on_part("in", g) if on_part is not None else None
    d_h1a = _mm(dqkv, w["w_in"], grid=(S // 1024, 1, QKV_SLABS // tsx),
                a_spec=pl.BlockSpec((tsx, 1024, LANES), lambda i, j, k: (k, i, 0)),
                b_spec=pl.BlockSpec((tsx * LANES, D), lambda i, j, k: (k, 0)),
                out_shape=jax.ShapeDtypeStruct((S, D), F32),
                out_spec=pl.BlockSpec((1024, D), lambda i, j, k: (i, 0)),
                ca=1, cb=0, acc_shape=(1024, D), a_slab=True, after=tok_in, name=n("in_bwd_x_qkv"))
    d_h1b = _mm(d_gates, w["w_in"], grid=(S // 1024, 1, GATE_COLS // 768),
                a_spec=pl.BlockSpec((1024, 768), lambda i, j, k: (i, k)),
                b_spec=pl.BlockSpec((768, D), lambda i, j, k: (k + QKV_COLS // 768, 0)),
                out_shape=jax.ShapeDtypeStruct((S, D), F32),
                out_spec=pl.BlockSpec((1024, D), lambda i, j, k: (i, 0)),
                ca=1, cb=0, acc_shape=(1024, D), after=tok_in, name=n("in_bwd_x_gate"))
    if below is None:
        dx, g["attn_pre_norm"] = _norm_bwd(sv["x"], w["attn_pre_norm"], [d_h1a, d_h1b], dx2, F32, n("attn_pre_bwd"))
        return dx, g, tok_in, None
    dx, d_fo_below, g["attn_pre_norm"], dg_below = _norm_bwd_chain(
        sv["x"], w["attn_pre_norm"], [d_h1a, d_h1b], dx2, below[0], below[1], n("attn_pre_ffn_post_bwd"))
    return dx, g, tok_in, (d_fo_below, dg_below)


def _local_step(x, target, ws, rel_bias, tok=None, on_grads=None):
    buckets = jnp.asarray(_bucket_tiles())
    bias = _bias_tiles(rel_bias, buckets, "bias_tiles").reshape(N_BIAS_HEADS // 2, 2, 2, BLK, 2 * BLK)
    saved = []
    gain0 = ws[0]["attn_pre_norm"] if tok is None else ws[0]["attn_pre_norm"] + tok
    h1 = _prenorm(x, gain0, "l0_attn_pre")
    for l in range(DEPTH):
        sv = _layer_fwd(x, h1, ws[l], bias, f"l{l}")
        saved.append(sv)
        if l + 1 < DEPTH:
            x, h1 = _postnorm_res(sv["x2"], sv["fo"], ws[l]["ffn_post_norm"], ws[l + 1]["attn_pre_norm"], f"l{l}_ffn_post")
    top = saved[-1]
    dy, loss_tile, d_fo_top, dg_top = _loss_head(top["x2"], top["fo"], ws[-1]["ffn_post_norm"], target, "loss_head")
    grads = [None] * DEPTH
    tok, d_fo = None, (d_fo_top, dg_top)
    for l in reversed(range(DEPTH)):
        on_part = None if on_grads is None else functools.partial(on_grads, l)
        below = (saved[l - 1]["fo"], ws[l - 1]["ffn_post_norm"]) if l > 0 else None
        dy, grads[l], tok, d_fo = _layer_bwd(dy, saved[l], ws[l], bias, f"l{l}", tok, on_part, d_fo, below)
    g_rel = _bias_grad([grads[l]["bias_g"] for l in range(DEPTH)], buckets, "bias_grad")[:, :N_BIAS_HEADS]
    return loss_tile, dy, grads, g_rel


def _coords():
    return lax.axis_index("x"), lax.axis_index("y"), lax.axis_index("c")


def _peer(rel):
    x, y, c = _coords()
    return (1 - x if rel & 4 else x, 1 - y if rel & 2 else y, 1 - c if rel & 1 else c)


def _exchange(srcs, dst_shapes, src_win, dst_win, name, after=None):
    nt = len(srcs)
    extra = [] if after is None else [after]

    def body(*refs):
        src_refs, dst_refs = refs[:nt], refs[nt + len(extra):2 * nt + len(extra)]
        send_sems, recv_sems, local_sems = refs[2 * nt + len(extra):]
        x, y, c = _coords()
        me = 4 * x + 2 * y + c
        locals_ = []
        for t in range(nt):
            cp = pltpu.make_async_copy(src_win(t, src_refs[t], me), dst_win(t, dst_refs[t], me), local_sems.at[t])
            cp.start()
            locals_.append(cp)
        sends = []
        for rel in range(1, NDEV):
            px, py, pc = _peer(rel)
            q = 4 * px + 2 * py + pc
            for t in range(nt):
                cp = pltpu.make_async_remote_copy(
                    src_ref=src_win(t, src_refs[t], q), dst_ref=dst_win(t, dst_refs[t], me),
                    send_sem=send_sems.at[rel - 1, t], recv_sem=recv_sems.at[rel - 1, t],
                    device_id=(px, py, pc), device_id_type=MESH)
                cp.start()
                sends.append(cp)
        for rel in range(1, NDEV):
            px, py, pc = _peer(rel)
            q = 4 * px + 2 * py + pc
            for t in range(nt):
                pltpu.make_async_remote_copy(
                    src_ref=src_win(t, src_refs[t], me), dst_ref=dst_win(t, dst_refs[t], q),
                    send_sem=send_sems.at[rel - 1, t], recv_sem=recv_sems.at[rel - 1, t],
                    device_id=(px, py, pc), device_id_type=MESH).wait_recv()
        for cp in sends:
            cp.wait_send()
        for cp in locals_:
            cp.wait()

    return pl.pallas_call(
        body, in_specs=[ANY] * (nt + len(extra)), out_specs=[ANY] * nt, out_shape=dst_shapes,
        scratch_shapes=[pltpu.SemaphoreType.DMA((NDEV - 1, nt)), pltpu.SemaphoreType.DMA((NDEV - 1, nt)),
                        pltpu.SemaphoreType.DMA((nt,))],
        name=name)(*srcs, *extra)


BIG = (("w_in", 0, 864), ("w_br_a", 1, 128), ("w_br_b", 1, 128), ("w_br_c", 1, 128), ("w_out", 0, 128),
       ("w_up", 1, 1024), ("w_down", 0, 512))


NBIG = len(BIG)
BIG_FULL = {"w_in": (IN_COLS, D), "w_br_a": (256, D), "w_br_b": (512, D), "w_br_c": (256, D), "w_out": (D, D),
            "w_up": (D, 2 * D_FF), "w_down": (D_FF, D)}
SHARD_ROWS = {"w_in": 288, "w_up": 256, "w_down": 256}
LAYER_GROUPS = (("in", (0,)), ("mix", (1, 2, 3, 4)), ("ffn", (5, 6)))

HBM_SPEC = pl.BlockSpec(memory_space=pltpu.HBM)
SEM_SPEC = pl.BlockSpec(memory_space=pltpu.SEMAPHORE)


def _hbm(a):
    return pltpu.with_memory_space_constraint(a, pltpu.HBM)


def _shard_window(t, ref, k):
    nm, ax, ext = BIG[t % NBIG]
    off = pl.multiple_of(k * ext, ext)
    if ax == 0:
        return ref.at[pl.ds(off, ext), :]
    return ref.at[:, pl.ds(off, ext)]


def _whole(t, ref, k):
    return ref


def _slot(t, ref, k):
    return ref.at[k]


def _own_block_spec(t, rows, me_of):
    nm, ax, ext = BIG[t % NBIG]
    r, c = BIG_FULL[nm]
    if ax == 0:
        return pl.BlockSpec((rows, c), lambda i, m: (me_of(m) * (ext // rows) + i, 0))
    return pl.BlockSpec((rows, ext), lambda i, m: (i, me_of(m)))


def _cast_own(t, shards, me_arr, name):
    nm, ax, ext = BIG[t % NBIG]
    layer = t // NBIG
    _, nr, nc = shards.shape
    rows = SHARD_ROWS.get(nm, nr)
    shape = BIG_FULL[nm]

    def body(m_ref, s_ref, o_ref):
        o_ref[...] = s_ref[...].astype(BF16)

    return pl.pallas_call(
        body, grid_spec=pltpu.PrefetchScalarGridSpec(
            num_scalar_prefetch=1, grid=(nr // rows,),
            in_specs=[pl.BlockSpec((None, rows, nc), lambda i, m: (layer, i, 0))],
            out_specs=_own_block_spec(t, rows, lambda m: m[0])),
        out_shape=jax.ShapeDtypeStruct(shape, BF16), compiler_params=_cp("arbitrary"), name=name)(me_arr, shards)


ALL_RELS = tuple(range(1, NDEV))
NEAR_RELS = (1, 2, 4, 6)
FAR_RELS = (2, 4, 6)


def _xchg_start(srcs, lands, groups, src_win, dst_win, after, name, rels=ALL_RELS, tids=None):
    ns = 0 if srcs is None else len(srcs)
    nt, ng = len(lands), len(groups)
    ins = ([] if srcs is None else list(srcs)) + list(lands)

    def body(*refs):
        src_refs, land_refs = refs[:ns], refs[ns:ns + nt]
        sems = refs[ns + nt + 1:ns + nt + 1 + 2 * ng]
        token = refs[-1]
        x, y, c = _coords()
        me = 4 * x + 2 * y + c
        for gi, grp in enumerate(groups):
            for j, t in enumerate(grp):
                tid = t if tids is None else tids[t]
                for ri, rel in enumerate(rels):
                    px, py, pc = _peer(rel)
                    q = 4 * px + 2 * py + pc
                    src = dst_win(tid, land_refs[t], me) if srcs is None else src_win(tid, src_refs[t], q)
                    pltpu.make_async_remote_copy(
                        src_ref=src, dst_ref=dst_win(tid, land_refs[t], me),
                        send_sem=sems[2 * gi].at[ri * len(grp) + j],
                        recv_sem=sems[2 * gi + 1].at[ri * len(grp) + j],
                        device_id=(px, py, pc), device_id_type=MESH).start()
        token[...] = jnp.zeros((8, LANES), F32)

    out_shape = []
    for grp in groups:
        out_shape += [pltpu.SemaphoreType.DMA((len(rels) * len(grp),))] * 2
    out_shape += [pltpu.HBM(a.shape, a.dtype) for a in ins]
    out_shape.append(jax.ShapeDtypeStruct((8, LANES), F32))
    outs = pl.pallas_call(
        body, in_specs=[HBM_SPEC] * len(ins) + [ANY],
        out_specs=[SEM_SPEC] * (2 * ng) + [HBM_SPEC] * len(ins) + [pl.BlockSpec(memory_space=pltpu.VMEM)],
        out_shape=out_shape, input_output_aliases={i: 2 * ng + i for i in range(len(ins))},
        compiler_params=pltpu.CompilerParams(has_side_effects=pltpu.SideEffectType.DATAFLOW_SIDE_EFFECTING),
        name=name)(*[_hbm(a) for a in ins], after)
    sems = [(outs[2 * gi], outs[2 * gi + 1]) for gi in range(ng)]
    thru = list(outs[2 * ng:2 * ng + len(ins)])
    return sems, (None if srcs is None else thru[:ns]), thru[ns:], outs[-1]


def _xchg_wait(sems, srcs, lands, tids, after, src_win, dst_win, name, rels=ALL_RELS):
    ns = 0 if srcs is None else len(srcs)
    n = len(lands)
    send_sem, recv_sem = sems
    ins = ([] if srcs is None else list(srcs)) + list(lands)

    def body(*refs):
        src_refs, land_refs = refs[:ns], refs[ns:ns + n]
        ssem, rsem = refs[ns + n], refs[ns + n + 1]
        x, y, c = _coords()
        me = 4 * x + 2 * y + c
        for j, t in enumerate(tids):
            for ri, rel in enumerate(rels):
                px, py, pc = _peer(rel)
                q = 4 * px + 2 * py + pc
                src = dst_win(t, land_refs[j], me) if srcs is None else src_win(t, src_refs[j], q)
                cp = pltpu.make_async_remote_copy(
                    src_ref=src, dst_ref=dst_win(t, land_refs[j], q),
                    send_sem=ssem.at[ri * n + j], recv_sem=rsem.at[ri * n + j],
                    device_id=(px, py, pc), device_id_type=MESH)
                cp.wait_send()
                cp.wait_recv()

    outs = pl.pallas_call(
        body, in_specs=[HBM_SPEC] * len(ins) + [SEM_SPEC, SEM_SPEC, ANY], out_specs=[HBM_SPEC] * len(ins),
        out_shape=[pltpu.HBM(a.shape, a.dtype) for a in ins],
        input_output_aliases={i: i for i in range(len(ins))},
        compiler_params=pltpu.CompilerParams(has_side_effects=pltpu.SideEffectType.DATAFLOW_SIDE_EFFECTING),
        name=name)(*ins, send_sem, recv_sem, after)
    return (None if srcs is None else list(outs[:ns])), list(outs[ns:])


def _gather_forward(sems_in, lands, groups, tids, after, dst_win, name):
    nt, ng = len(lands), len(groups)

    def body(*refs):
        land_refs = refs[:nt]
        in_sems = refs[nt:nt + 2 * ng]
        out_sems = refs[nt + 2 * ng + 1:nt + 4 * ng + 1]
        token = refs[-1]
        x, y, c = _coords()
        me = 4 * x + 2 * y + c
        sib = (x, y, 1 - c)
        for gi, grp in enumerate(groups):
            n = len(grp)
            for j, pos in enumerate(grp):
                t = tids[pos]
                for ri, rel in enumerate(NEAR_RELS):
                    px, py, pc = _peer(rel)
                    q = 4 * px + 2 * py + pc
                    cp = pltpu.make_async_remote_copy(
                        src_ref=dst_win(t, land_refs[pos], me), dst_ref=dst_win(t, land_refs[pos], q),
                        send_sem=in_sems[2 * gi].at[ri * n + j], recv_sem=in_sems[2 * gi + 1].at[ri * n + j],
                        device_id=(px, py, pc), device_id_type=MESH)
                    cp.wait_send()
                    cp.wait_recv()
            for j, pos in enumerate(grp):
                t = tids[pos]
                for fi, rel in enumerate(FAR_RELS):
                    px, py, pc = _peer(rel)
                    q = 4 * px + 2 * py + pc
                    win = dst_win(t, land_refs[pos], q)
                    pltpu.make_async_remote_copy(
                        src_ref=win, dst_ref=win,
                        send_sem=out_sems[2 * gi].at[fi * n + j], recv_sem=out_sems[2 * gi + 1].at[fi * n + j],
                        device_id=sib, device_id_type=MESH).start()
        token[...] = jnp.zeros((8, LANES), F32)

    out_shape = []
    for grp in groups:
        out_shape += [pltpu.SemaphoreType.DMA((len(FAR_RELS) * len(grp),))] * 2
    out_shape += [pltpu.HBM(a.shape, a.dtype) for a in lands]
    out_shape.append(jax.ShapeDtypeStruct((8, LANES), F32))
    flat_sems = [s for pair in sems_in for s in pair]
    outs = pl.pallas_call(
        body, in_specs=[HBM_SPEC] * nt + [SEM_SPEC] * (2 * ng) + [ANY],
        out_specs=[SEM_SPEC] * (2 * ng) + [HBM_SPEC] * nt + [pl.BlockSpec(memory_space=pltpu.VMEM)],
        out_shape=out_shape, input_output_aliases={i: 2 * ng + i for i in range(nt)},
        compiler_params=pltpu.CompilerParams(has_side_effects=pltpu.SideEffectType.DATAFLOW_SIDE_EFFECTING),
        name=name)(*[_hbm(a) for a in lands], *flat_sems, after)
    sems = [(outs[2 * gi], outs[2 * gi + 1]) for gi in range(ng)]
    return sems, list(outs[2 * ng:2 * ng + nt]), outs[-1]


class _Weights:
    def __init__(self, ready, pending=None):
        self.ready = dict(ready)
        self.pending = dict(pending or {})

    def __getitem__(self, k):
        return self.ready[k]

    def need(self, group, after):
        fn = self.pending.pop(group, None)
        if fn is not None:
            self.ready.update(fn(after))


def _adamw_math(w, g, m, v):
    m2 = ADAM_B1 * m + (1.0 - ADAM_B1) * g
    v2 = ADAM_B2 * v + (1.0 - ADAM_B2) * (g * g)
    m_hat = m2 / (1.0 - ADAM_B1 ** ADAM_STEP)
    v_hat = v2 / (1.0 - ADAM_B2 ** ADAM_STEP)
    delta = -ADAM_LR * (m_hat / (jnp.sqrt(v_hat) + ADAM_EPS) + ADAM_WD * w)
    return delta, m2, v2


def _adamw(t, parts, own, me_arr, w, m, v, layer, prev, rows, name):
    nl, nr, nc = w.shape

    def body(me_ref, p_ref, own_ref, w_ref, m_ref, v_ref, *rest):
        g_ref, d_ref, m2_ref, v2_ref = rest[-4:]
        me = me_ref[0]
        g = None
        for k in range(NDEV):
            term = jnp.where(me == k, own_ref[...], p_ref[k]).astype(F32)
            g = term if g is None else g + term
        delta, m2, v2 = _adamw_math(w_ref[...], g, m_ref[...], v_ref[...])
        g_ref[...] = g
        d_ref[...] = delta
        m2_ref[...] = m2
        v2_ref[...] = v2

    blk = pl.BlockSpec((None, rows, nc), lambda i, mm: (layer, i, 0))
    pblk = pl.BlockSpec((NDEV, rows, nc), lambda i, mm: (0, i, 0))
    extra = [] if prev is None else list(prev)
    return pl.pallas_call(
        body, grid_spec=pltpu.PrefetchScalarGridSpec(
            num_scalar_prefetch=1, grid=(nr // rows,),
            in_specs=[pblk, _own_block_spec(t, rows, lambda mm: mm[0]), blk, blk, blk] + [ANY] * len(extra),
            out_specs=[blk] * 4),
        out_shape=[jax.ShapeDtypeStruct(w.shape, F32)] * 4,
        input_output_aliases={6 + k: k for k in range(len(extra))},
        compiler_params=_cp("arbitrary"), name=name)(me_arr, parts, own, w, m, v, *extra)


def _pack(vecs):
    flat = jnp.concatenate([v.reshape(-1).astype(F32) for v in vecs])
    n = flat.shape[0]
    rows = -(-n // (8 * LANES)) * 8
    return jnp.pad(flat, (0, rows * LANES - n)).reshape(rows, LANES)


ROWPACK = (("rel_bias", 32, 32, (NUM_BUCKETS, N_BIAS_HEADS)), ("sinks", 8, 8, (DEPTH, 8)),
           ("attn_pre_norm", 16, 16, (DEPTH, D)), ("attn_post_norm", 16, 16, (DEPTH, D)),
           ("ffn_pre_norm", 16, 16, (DEPTH, D)), ("ffn_post_norm", 16, 16, (DEPTH, D)),
           ("conv_b", 128, 128, (DEPTH, 2 * D_FF)), ("b_gate", 48, 8, (DEPTH, 3, 128)),
           ("conv_w", 384, 48, (DEPTH, 3, 1024)))
ROWS_OWN = sum(r for _, _, r, _ in ROWPACK)
N_REPL = 7
ROWS_REPL = sum(r for _, _, r, _ in ROWPACK[:N_REPL])
ROWS_SHARD = ROWS_OWN - ROWS_REPL


def _as_rows(a, rows):
    a = a.astype(F32)
    if a.shape[-1] < LANES:
        a = jnp.pad(a.reshape(-1, a.shape[-1]), ((0, 0), (0, LANES - a.shape[-1])))
    a = a.reshape(-1, LANES)
    return jnp.pad(a, ((0, rows - a.shape[0]), (0, 0)))


def _rowpack(arrs, entries=ROWPACK):
    return jnp.concatenate([_as_rows(arrs[nm], ro) for nm, _, ro, _ in entries], axis=0)


def _shard_rows(g):
    bg = jnp.transpose(g["b_gate"].astype(F32).reshape(DEPTH * 3, NDEV, LANES), (1, 0, 2))
    bg = jnp.pad(bg, ((0, 0), (0, 8 - DEPTH * 3), (0, 0)))
    cw = jnp.transpose(g["conv_w"].astype(F32).reshape(DEPTH * 3, NDEV, 8, LANES), (1, 0, 2, 3))
    return jnp.concatenate([bg, cw.reshape(NDEV, DEPTH * 3 * 8, LANES)], axis=1)


def _small_update(parts_repl, parts_shard, w, m, v, name):
    nsm = len(ROWPACK)

    def body(pr_ref, ps_ref, w_ref, m_ref, v_ref, *rest):
        outs = rest[:4 * nsm]
        loss_ref = rest[4 * nsm]
        g_s, d_s, m_s, v_s = rest[4 * nsm + 1:]
        gr, gs = pr_ref[0], ps_ref[0]
        for k in range(1, NDEV):
            gr = gr + pr_ref[k]
            gs = gs + ps_ref[k]
        g_s[0:ROWS_REPL, :] = gr[:ROWS_REPL]
        g_s[ROWS_REPL:ROWS_OWN, :] = gs
        loss_ref[...] = gr[ROWS_REPL:]
        delta, m2, v2 = _adamw_math(w_ref[...], g_s[...], m_ref[...], v_ref[...])
        d_s[...] = delta
        m_s[...] = m2
        v_s[...] = v2
        for kind, src in enumerate((g_s, d_s, m_s, v_s)):
            oo = 0
            for idx, (nm, rf, ro, shp) in enumerate(ROWPACK):
                o_ref = outs[kind * nsm + idx]
                if nm in ("rel_bias", "sinks"):
                    o_ref[...] = src[oo:oo + shp[0], 0:shp[1]]
                elif nm == "b_gate":
                    for l in range(DEPTH):
                        o_ref[l] = src[oo + 3 * l:oo + 3 * l + 3, :]
                elif nm == "conv_w":
                    for l in range(DEPTH):
                        for k in range(8):
                            o_ref[l, :, k * LANES:(k + 1) * LANES] = src[pl.ds(oo + 24 * l + k, 3, stride=8), :]
                else:
                    per = shp[1] // LANES
                    for k in range(per):
                        o_ref[:, k * LANES:(k + 1) * LANES] = src[pl.ds(oo + k, DEPTH, stride=per), :]
                oo += ro

    vm = pl.BlockSpec(memory_space=pltpu.VMEM)
    shapes = [jax.ShapeDtypeStruct(shp, F32) for _ in range(4) for _, _, _, shp in ROWPACK]
    shapes.append(jax.ShapeDtypeStruct((8, LANES), F32))
    outs = pl.pallas_call(
        body, in_specs=[vm] * 5, out_specs=[vm] * (4 * nsm + 1), out_shape=shapes,
        scratch_shapes=[pltpu.VMEM((ROWS_OWN, LANES), F32)] * 4,
        name=name)(parts_repl, parts_shard, w, m, v)
    names = [nm for nm, _, _, _ in ROWPACK]
    return [dict(zip(names, outs[kind * nsm:(kind + 1) * nsm])) for kind in range(4)] + [outs[-1]]


def kernel(x, rel_bias, attn_pre_norm, w_in, b_gate, sinks, w_br_a, w_br_b, w_br_c, w_out, attn_post_norm, ffn_pre_norm, w_up, conv_w, conv_b, w_down, ffn_post_norm, loss_target, m_rel_bias, m_attn_pre_norm, m_w_in, m_b_gate, m_sinks, m_w_br_a, m_w_br_b, m_w_br_c, m_w_out, m_attn_post_norm, m_ffn_pre_norm, m_w_up, m_conv_w, m_conv_b, m_w_down, m_ffn_post_norm, v_rel_bias, v_attn_pre_norm, v_w_in, v_b_gate, v_sinks, v_w_br_a, v_w_br_b, v_w_br_c, v_w_out, v_attn_post_norm, v_ffn_pre_norm, v_w_up, v_conv_w, v_conv_b, v_w_down, v_ffn_post_norm):
    P = dict(rel_bias=rel_bias, attn_pre_norm=attn_pre_norm, w_in=w_in, b_gate=b_gate, sinks=sinks, w_br_a=w_br_a,
             w_br_b=w_br_b, w_br_c=w_br_c, w_out=w_out, attn_post_norm=attn_post_norm, ffn_pre_norm=ffn_pre_norm,
             w_up=w_up, conv_w=conv_w, conv_b=conv_b, w_down=w_down, ffn_post_norm=ffn_post_norm)
    M = dict(rel_bias=m_rel_bias, attn_pre_norm=m_attn_pre_norm, w_in=m_w_in, b_gate=m_b_gate, sinks=m_sinks,
             w_br_a=m_w_br_a, w_br_b=m_w_br_b, w_br_c=m_w_br_c, w_out=m_w_out, attn_post_norm=m_attn_post_norm,
             ffn_pre_norm=m_ffn_pre_norm, w_up=m_w_up, conv_w=m_conv_w, conv_b=m_conv_b, w_down=m_w_down,
             ffn_post_norm=m_ffn_post_norm)
    V = dict(rel_bias=v_rel_bias, attn_pre_norm=v_attn_pre_norm, w_in=v_w_in, b_gate=v_b_gate, sinks=v_sinks,
             w_br_a=v_w_br_a, w_br_b=v_w_br_b, w_br_c=v_w_br_c, w_out=v_w_out, attn_post_norm=v_attn_post_norm,
             ffn_pre_norm=v_ffn_pre_norm, w_up=v_w_up, conv_w=v_conv_w, conv_b=v_conv_b, w_down=v_w_down,
             ffn_post_norm=v_ffn_post_norm)
    tr = lambda a: jnp.swapaxes(a, 1, 2)
    PB = {nm: (tr(P[nm]) if nm == "w_in" else P[nm]) for nm, _, _ in BIG}
    MB = {nm: (tr(M[nm]) if nm == "w_in" else M[nm]) for nm, _, _ in BIG}
    VB = {nm: (tr(V[nm]) if nm == "w_in" else V[nm]) for nm, _, _ in BIG}
    xi, yi, ci = _coords()
    me = 4 * xi + 2 * yi + ci

    me_arr = me.astype(jnp.int32).reshape(1)

    small_w = _pack([b_gate.reshape(-1), conv_w.reshape(-1)])
    (small_w_all,) = _exchange([small_w], [jax.ShapeDtypeStruct((NDEV,) + small_w.shape, F32)],
                               _whole, _slot, "gather_small_weights")

    groups = [tuple(l * NBIG + t for t in tids) for l in range(DEPTH) for _, tids in LAYER_GROUPS]
    cast = lambda i, m=me_arr: _cast_own(i, PB[BIG[i % NBIG][0]], m, f"gather_own_l{i // NBIG}_{BIG[i % NBIG][0]}")
    first = list(groups[0])
    rest = [i for grp in groups[1:] for i in grp]
    sems0, _, lands0, tok_first = _xchg_start(None, [cast(i) for i in first], [tuple(range(len(first)))], None,
                                              _shard_window, small_w_all, "gather_start_first", rels=NEAR_RELS, tids=first)
    where_rest = {tid: k for k, tid in enumerate(rest)}
    me_rest = me_arr + tok_first[0, 0:1].astype(jnp.int32)
    sems1, _, lands1, g_tok = _xchg_start(None, [cast(i, me_rest) for i in rest],
                                          [tuple(where_rest[i] for i in grp) for grp in groups[1:]], None,
                                          _shard_window, lands0[0], "gather_start_rest", rels=NEAR_RELS, tids=rest)
    g_sems = list(sems0) + list(sems1)
    tok0 = g_tok[0:1, 0:1]
    lands_now = [None] * (DEPTH * NBIG)
    for i, a in zip(first + rest, list(lands0) + list(lands1)):
        lands_now[i] = a
    fwd_sems = {}
    fwd_plan = {0: (0,), 1: (1,), 2: (2,), 3: (3, 4, 5)}

    def gather_waiter(gi, l, gname, tids):
        def wait(after):
            if gi in fwd_plan:
                gis = fwd_plan[gi]
                flat = [i for g2 in gis for i in groups[g2]]
                where = {tid: k for k, tid in enumerate(flat)}
                fs, new_lands, ftok = _gather_forward(
                    [g_sems[g2] for g2 in gis], [lands_now[i] for i in flat],
                    [[where[i] for i in groups[g2]] for g2 in gis], flat, after, _shard_window, f"gather_forward_{gi}")
                for g2, s in zip(gis, fs):
                    fwd_sems[g2] = s
                for i, a in zip(flat, new_lands):
                    lands_now[i] = a
                after = ftok
            ids = [l * NBIG + t for t in tids]
            _, got = _xchg_wait(fwd_sems[gi], None, [lands_now[i] for i in ids], ids, after,
                                None, _shard_window, f"gather_wait_l{l}_{gname}", rels=FAR_RELS)
            out = {}
            for t, arr in zip(tids, got):
                nm = BIG[t][0]
                out[nm] = arr
            return out
        return wait

    pending = [{gname: gather_waiter(l * len(LAYER_GROUPS) + k, l, gname, tids)
                for k, (gname, tids) in enumerate(LAYER_GROUPS)} for l in range(DEPTH)]
    nbg = DEPTH * 3 * 128
    ncw = DEPTH * 3 * 1024
    flat_all = small_w_all.reshape(NDEV, -1)
    b_gate_full = jnp.transpose(flat_all[:, :nbg].reshape(NDEV, DEPTH, 3, 128), (1, 2, 0, 3)).reshape(DEPTH, 3, D)
    conv_w_full = jnp.transpose(flat_all[:, nbg:nbg + ncw].reshape(NDEV, DEPTH, 3, 1024), (1, 2, 0, 3)).reshape(DEPTH, 3, 2 * D_FF)

    ws = []
    for l in range(DEPTH):
        ws.append(_Weights(dict(
            b_gate=b_gate_full[l], conv_w=conv_w_full[l].reshape(3, 2, D_FF), conv_b=conv_b[l].reshape(2, D_FF),
            sinks=sinks[l].reshape(1, 8),
            attn_pre_norm=attn_pre_norm[l].reshape(1, D), attn_post_norm=attn_post_norm[l].reshape(1, D),
            ffn_pre_norm=ffn_pre_norm[l].reshape(1, D), ffn_post_norm=ffn_post_norm[l].reshape(1, D)), pending[l]))

    rs = {}

    group_tids = dict(LAYER_GROUPS)

    def start_scatter(l, gname, grads_l):
        tids = group_tids[gname]
        blocks, lands_rs = [], []
        for t in tids:
            nm, ax, ext = BIG[t]
            gfull = grads_l[nm].astype(BF16)
            shp = (NDEV, ext, gfull.shape[1]) if ax == 0 else (NDEV, gfull.shape[0], ext)
            blocks.append(gfull)
            lands_rs.append(lax.empty(shp, BF16))
        local = list(range(len(tids)))
        win = lambda j, ref, k: _shard_window(tids[j], ref, k)
        sems, s_thru, l_thru, tok = _xchg_start(blocks, lands_rs, [tuple(local)], win, _slot, me_arr,
                                                f"scatter_start_l{l}_{gname}")
        rs[(l, gname)] = (sems[0], s_thru, l_thru, win, local)
        return tok[0:1, 0:1]

    loss_tile, grad_x, grads, g_rel = _local_step(x[0], loss_target[0], ws, rel_bias, tok0, start_scatter)

    stack = lambda nm: jnp.stack([grads[l][nm] for l in range(DEPTH)], axis=0)
    small_g = {nm: (g_rel if nm == "rel_bias" else stack(nm)) for nm, _, _, _ in ROWPACK}
    small_repl = jnp.concatenate([_rowpack(small_g, ROWPACK[:N_REPL]), loss_tile], axis=0)
    small_shard = _shard_rows(small_g)

    out_g, out_d, out_m, out_v = {}, {}, {}, {}
    prev = {nm: None for nm, _, _ in BIG}
    todo = [(l, gname) for l in reversed(range(DEPTH)) for gname in ("ffn", "mix", "in")]
    after, small_parts = grad_x, None
    for l, gname in todo:
        if (l, gname) == todo[-1]:
            small_parts = _exchange(
                [small_repl, small_shard],
                [jax.ShapeDtypeStruct((NDEV, ROWS_REPL + 8, LANES), F32), jax.ShapeDtypeStruct((NDEV, ROWS_SHARD, LANES), F32)],
                lambda t, ref, q: ref if t == 0 else ref.at[q], _slot, "exchange_small_grads", after=after)
            after = small_parts[0]
        sems, s_thru, l_thru, win, local = rs[(l, gname)]
        owns, parts = _xchg_wait(sems, s_thru, l_thru, local, after, win, _slot, f"scatter_wait_l{l}_{gname}")
        for t, own, prt in zip(group_tids[gname], owns, parts):
            nm = BIG[t][0]
            rows = SHARD_ROWS.get(nm, PB[nm].shape[1])
            prev[nm] = _adamw(t, prt, own, me_arr, PB[nm], MB[nm], VB[nm], l, prev[nm], rows, f"adamw_{nm}_l{l}")
            after = prev[nm][1]
    for nm, _, _ in BIG:
        out_g[nm], out_d[nm], out_m[nm], out_v[nm] = [tr(a) if nm == "w_in" else a for a in prev[nm]]
    sm_g, sm_d, sm_m, sm_v, loss_all = _small_update(small_parts[0], small_parts[1], _rowpack(P), _rowpack(M),
                                                     _rowpack(V), "small_update")
    loss = loss_all[0, 0]
    for dst, src in ((out_g, sm_g), (out_d, sm_d), (out_m, sm_m), (out_v, sm_v)):
        dst.update(src)

    order = ["rel_bias", "attn_pre_norm", "w_in", "b_gate", "sinks", "w_br_a", "w_br_b", "w_br_c", "w_out",
             "attn_post_norm", "ffn_pre_norm", "w_up", "conv_w", "conv_b", "w_down", "ffn_post_norm"]
    return (loss, grad_x[None], *[out_g[k] for k in order], *[out_d[k] for k in order],
            *[out_m[k] for k in order], *[out_v[k] for k in order])
```

```python
import functools
import math

import numpy as np
import jax
import jax.numpy as jnp
from jax import lax
from jax.experimental import pallas as pl
from jax.experimental.pallas import tpu as pltpu

F32 = jnp.float32
BF16 = jnp.bfloat16

S = 2048
D = 1024
DEPTH = 2
NDEV = 8
HD = 64
BLK = 128
NB = S // BLK
A_GROUPS = ((128, 1), (512, 4), (2048, 16))
NUM_BUCKETS = 32
MAX_DISTANCE = 2048
N_BIAS_HEADS = 20
D_FF = 4096
IN_COLS = 6912
QKV_COLS = 3840
QKV_SLABS = QKV_COLS // 128
GATE_COLS = 3072
EPS = 1e-6
SCALE = HD ** -0.5
NEG = -1e30
LANES = 128

ADAM_LR = 0.001
ADAM_B1 = 0.9
ADAM_B2 = 0.999
ADAM_EPS = 1e-08
ADAM_WD = 0.01
ADAM_STEP = 10

VMEM_LIMIT = 56 * 1024 * 1024
MESH = pl.DeviceIdType.MESH
ANY = pl.BlockSpec(memory_space=pl.ANY)
SMEM = pl.BlockSpec(memory_space=pltpu.SMEM)


def _cp(*sem):
    return pltpu.CompilerParams(dimension_semantics=sem if sem else None, vmem_limit_bytes=VMEM_LIMIT)


def _dot(a, b, ca, cb):
    return lax.dot_general(a, b, (((ca,), (cb,)), ((), ())), preferred_element_type=F32)


def _mm(a, b, *, grid, a_spec, b_spec, out_shape, out_spec, ca, cb, acc_shape, name,
        a_slab=False, b_slab=False, out_slab=False, alias_out=None, after=None):
    nk = grid[2]

    def body(*refs):
        a_ref, b_ref = refs[0], refs[1]
        o_ref, acc_ref = refs[-2], refs[-1]
        k = pl.program_id(2)

        def load(ref, slab):
            if slab:
                return jnp.concatenate([ref[s] for s in range(ref.shape[0])], axis=1).astype(BF16)
            return ref[...].astype(BF16)

        def write(val):
            if out_slab:
                for s in range(o_ref.shape[0]):
                    o_ref[s] = val[:, s * LANES:(s + 1) * LANES].astype(o_ref.dtype)
            else:
                o_ref[...] = val.astype(o_ref.dtype)

        d = _dot(load(a_ref, a_slab), load(b_ref, b_slab), ca, cb)
        if nk == 1:
            write(d)
        elif direct:
            @pl.when(k == 0)
            def _():
                o_ref[...] = d

            @pl.when(k > 0)
            def _():
                o_ref[...] += d
        else:
            @pl.when(k == 0)
            def _():
                acc_ref[...] = d

            if nk > 2:
                @pl.when((k > 0) & (k < nk - 1))
                def _():
                    acc_ref[...] += d

            @pl.when(k == nk - 1)
            def _():
                write(acc_ref[...] + d)

    direct = (not out_slab) and out_shape.dtype == F32
    if nk == 1 or direct:
        acc_shape = (8, LANES)
    in_specs = [a_spec, b_spec]
    args = [a, b]
    aliases = {}
    if alias_out is not None:
        in_specs.append(ANY)
        args.append(alias_out)
        aliases = {2: 0}
    if after is not None:
        in_specs.append(ANY)
        args.append(after)
    return pl.pallas_call(
        body, grid=grid, in_specs=in_specs, out_specs=out_spec, out_shape=out_shape,
        scratch_shapes=[pltpu.VMEM(acc_shape, F32)], input_output_aliases=aliases,
        compiler_params=_cp("parallel", "parallel", "arbitrary"), name=name)(*args)


def _mm_nn(a, b, out_dtype, tm, tn, tk, name):
    m, kk = a.shape
    n = b.shape[1]
    return _mm(a, b, grid=(m // tm, n // tn, kk // tk),
               a_spec=pl.BlockSpec((tm, tk), lambda i, j, k: (i, k)),
               b_spec=pl.BlockSpec((tk, tn), lambda i, j, k: (k, j)),
               out_shape=jax.ShapeDtypeStruct((m, n), out_dtype),
               out_spec=pl.BlockSpec((tm, tn), lambda i, j, k: (i, j)),
               ca=1, cb=0, acc_shape=(tm, tn), name=name)


def _mm_nt(a, b, out_dtype, tm, tn, tk, name):
    m, kk = a.shape
    n = b.shape[0]
    return _mm(a, b, grid=(m // tm, n // tn, kk // tk),
               a_spec=pl.BlockSpec((tm, tk), lambda i, j, k: (i, k)),
               b_spec=pl.BlockSpec((tn, tk), lambda i, j, k: (j, k)),
               out_shape=jax.ShapeDtypeStruct((m, n), out_dtype),
               out_spec=pl.BlockSpec((tm, tn), lambda i, j, k: (i, j)),
               ca=1, cb=1, acc_shape=(tm, tn), name=name)


def _mm_tn(a, b, out_dtype, tm, tn, tk, name):
    kk, m = a.shape
    n = b.shape[1]
    return _mm(a, b, grid=(m // tm, n // tn, kk // tk),
               a_spec=pl.BlockSpec((tk, tm), lambda i, j, k: (k, i)),
               b_spec=pl.BlockSpec((tk, tn), lambda i, j, k: (k, j)),
               out_shape=jax.ShapeDtypeStruct((m, n), out_dtype),
               out_spec=pl.BlockSpec((tm, tn), lambda i, j, k: (i, j)),
               ca=0, cb=0, acc_shape=(tm, tn), name=name)


ROW_TILE = 256


def _rms(x, g):
    r = lax.rsqrt(jnp.mean(x * x, axis=-1, keepdims=True) + EPS)
    return x * r * g


def _prenorm(x, g, name):
    def body(x_ref, g_ref, o_ref):
        o_ref[...] = _rms(x_ref[...], g_ref[...]).astype(BF16)

    return pl.pallas_call(
        body, grid=(S // ROW_TILE,),
        in_specs=[pl.BlockSpec((ROW_TILE, D), lambda i: (i, 0)), pl.BlockSpec((1, D), lambda i: (0, 0))],
        out_specs=pl.BlockSpec((ROW_TILE, D), lambda i: (i, 0)),
        out_shape=jax.ShapeDtypeStruct((S, D), BF16), compiler_params=_cp("parallel"), name=name)(x, g)


def _postnorm_res(x, f, g_post, g_next, name):
    def body(x_ref, f_ref, gp_ref, gn_ref, xo_ref, ho_ref):
        xn = x_ref[...] + _rms(f_ref[...], gp_ref[...])
        xo_ref[...] = xn
        ho_ref[...] = _rms(xn, gn_ref[...]).astype(BF16)

    row = pl.BlockSpec((ROW_TILE, D), lambda i: (i, 0))
    vec = pl.BlockSpec((1, D), lambda i: (0, 0))
    return pl.pallas_call(
        body, grid=(S // ROW_TILE,), in_specs=[row, row, vec, vec], out_specs=[row, row],
        out_shape=[jax.ShapeDtypeStruct((S, D), F32), jax.ShapeDtypeStruct((S, D), BF16)],
        compiler_params=_cp("parallel"), name=name)(x, f, g_post, g_next)


def _norm_bwd(f, g, dys, res, out_dtype, name):
    ndy = len(dys)
    has_res = res is not None

    def body(*refs):
        f_ref, g_ref = refs[0], refs[1]
        dy_refs = refs[2:2 + ndy]
        res_ref = refs[2 + ndy] if has_res else None
        o_ref, dg_ref = refs[-2], refs[-1]
        fv = f_ref[...]
        dy = dy_refs[0][...].astype(F32)
        for r in dy_refs[1:]:
            dy = dy + r[...].astype(F32)
        r = lax.rsqrt(jnp.mean(fv * fv, axis=-1, keepdims=True) + EPS)
        n = fv * r
        dn = dy * g_ref[...]
        df = r * (dn - n * jnp.mean(dn * n, axis=-1, keepdims=True))
        if has_res:
            df = df + res_ref[...]
        o_ref[...] = df.astype(out_dtype)

        @pl.when(pl.program_id(0) == 0)
        def _():
            dg_ref[...] = jnp.zeros((1, D), F32)

        dg_ref[...] += jnp.sum(dy * n, axis=0, keepdims=True)

    row = pl.BlockSpec((ROW_TILE, D), lambda i: (i, 0))
    vec = pl.BlockSpec((1, D), lambda i: (0, 0))
    in_specs = [row, vec] + [row] * ndy + ([row] if has_res else [])
    args = [f, g] + list(dys) + ([res] if has_res else [])
    return pl.pallas_call(
        body, grid=(S // ROW_TILE,), in_specs=in_specs, out_specs=[row, vec],
        out_shape=[jax.ShapeDtypeStruct((S, D), out_dtype), jax.ShapeDtypeStruct((1, D), F32)],
        compiler_params=_cp("arbitrary"), name=name)(*args)


def _rms_bwd_rows(fv, g, dy):
    r = lax.rsqrt(jnp.mean(fv * fv, axis=-1, keepdims=True) + EPS)
    n = fv * r
    dn = dy * g
    return r * (dn - n * jnp.mean(dn * n, axis=-1, keepdims=True)), dy * n


def _norm_bwd_chain(f1, g1, dys, res, f2, g2, name):
    ndy = len(dys)

    def body(*refs):
        f1_ref, g1_ref = refs[0], refs[1]
        dy_refs = refs[2:2 + ndy]
        res_ref, f2_ref, g2_ref = refs[2 + ndy:5 + ndy]
        o1_ref, o2_ref, dg1_ref, dg2_ref = refs[-4:]
        dy = dy_refs[0][...].astype(F32)
        for r in dy_refs[1:]:
            dy = dy + r[...].astype(F32)
        df1, c1 = _rms_bwd_rows(f1_ref[...], g1_ref[...], dy)
        out1 = df1 + res_ref[...]
        o1_ref[...] = out1
        df2, c2 = _rms_bwd_rows(f2_ref[...], g2_ref[...], out1)
        o2_ref[...] = df2.astype(BF16)

        @pl.when(pl.program_id(0) == 0)
        def _():
            dg1_ref[...] = jnp.zeros((1, D), F32)
            dg2_ref[...] = jnp.zeros((1, D), F32)

        dg1_ref[...] += jnp.sum(c1, axis=0, keepdims=True)
        dg2_ref[...] += jnp.sum(c2, axis=0, keepdims=True)

    row = pl.BlockSpec((ROW_TILE, D), lambda i: (i, 0))
    vec = pl.BlockSpec((1, D), lambda i: (0, 0))
    return pl.pallas_call(
        body, grid=(S // ROW_TILE,), in_specs=[row, vec] + [row] * ndy + [row, row, vec],
        out_specs=[row, row, vec, vec],
        out_shape=[jax.ShapeDtypeStruct((S, D), F32), jax.ShapeDtypeStruct((S, D), BF16),
                   jax.ShapeDtypeStruct((1, D), F32), jax.ShapeDtypeStruct((1, D), F32)],
        compiler_params=_cp("arbitrary"), name=name)(f1, g1, *dys, res, f2, g2)


def _loss_head(x, f, g, target, name):
    def body(x_ref, f_ref, g_ref, t_ref, dy_ref, l_ref, df_ref, dg_ref):
        fv = f_ref[...]
        e = x_ref[...] + _rms(fv, g_ref[...]) - t_ref[...]
        dy = e * (1.0 / D)
        dy_ref[...] = dy
        df, c = _rms_bwd_rows(fv, g_ref[...], dy)
        df_ref[...] = df.astype(BF16)

        @pl.when(pl.program_id(0) == 0)
        def _():
            l_ref[...] = jnp.zeros((8, LANES), F32)
            dg_ref[...] = jnp.zeros((1, D), F32)

        l_ref[...] += jnp.sum(e * e) * (0.5 / D)
        dg_ref[...] += jnp.sum(c, axis=0, keepdims=True)

    row = pl.BlockSpec((ROW_TILE, D), lambda i: (i, 0))
    vec = pl.BlockSpec((1, D), lambda i: (0, 0))
    return pl.pallas_call(
        body, grid=(S // ROW_TILE,), in_specs=[row, row, vec, row],
        out_specs=[row, pl.BlockSpec((8, LANES), lambda i: (0, 0)), row, vec],
        out_shape=[jax.ShapeDtypeStruct((S, D), F32), jax.ShapeDtypeStruct((8, LANES), F32),
                   jax.ShapeDtypeStruct((S, D), BF16), jax.ShapeDtypeStruct((1, D), F32)],
        compiler_params=_cp("arbitrary"), name=name)(x, f, g, target)


def _bucket_tiles():
    a = np.arange(BLK)[:, None]
    b = np.arange(2 * BLK)[None, :]
    dist = a + BLK - b
    out = np.zeros((4, 2, BLK, 2 * BLK), np.int32)
    cfg = [(w // d, d) for w, d in A_GROUPS] + [(BLK - 1, 1)]
    for gi, (max_dist, d) in enumerate(cfg):
        band = (dist >= 0) & (dist <= max_dist)
        tok = np.maximum(dist, 0) * d
        nf = np.maximum(tok, 1).astype(np.float32)
        max_exact = NUM_BUCKETS // 2
        large = max_exact + (np.log(nf / np.float32(max_exact)) / np.float32(math.log(MAX_DISTANCE / max_exact))
                             * np.float32(NUM_BUCKETS - max_exact)).astype(np.int32)
        large = np.minimum(large, NUM_BUCKETS - 1)
        bkt = np.where(tok < max_exact, tok, large).astype(np.int32)
        full = np.where(band, bkt, -1)
        out[gi, 1] = full
        out[gi, 0] = np.where(b >= BLK, full, -1)
    return out


def _bias_tiles(rel_bias, buckets, name):
    def body(tab_ref, bkt_ref, o_ref):
        h = pl.program_id(0)
        bkt = bkt_ref[...]
        acc = jnp.zeros(bkt.shape, F32)
        for bb in range(NUM_BUCKETS):
            acc = jnp.where(bkt == bb, tab_ref[bb, h], acc)
        o_ref[...] = jnp.where(bkt < 0, NEG, acc)

    return pl.pallas_call(
        body, grid=(N_BIAS_HEADS,),
        in_specs=[SMEM, pl.BlockSpec((None, 2, BLK, 2 * BLK), lambda h: (jnp.minimum(h // 4, 3), 0, 0, 0))],
        out_specs=pl.BlockSpec((None, 2, BLK, 2 * BLK), lambda h: (h, 0, 0, 0)),
        out_shape=jax.ShapeDtypeStruct((N_BIAS_HEADS, 2, BLK, 2 * BLK), F32),
        compiler_params=_cp("arbitrary"), name=name)(rel_bias, buckets)


def _bias_grad(gs, buckets, name):
    ng = len(gs)

    def body(*refs):
        g_refs = refs[:ng]
        bkt_ref, o_ref = refs[ng], refs[ng + 1]
        h = pl.program_id(0)
        g = g_refs[0][...]
        for r in g_refs[1:]:
            g = g + r[...]
        bkt = bkt_ref[...]
        row = lax.broadcasted_iota(jnp.int32, (NUM_BUCKETS, LANES), 0)
        lane = lax.broadcasted_iota(jnp.int32, (NUM_BUCKETS, LANES), 1)

        @pl.when(h == 0)
        def _():
            o_ref[...] = jnp.zeros((NUM_BUCKETS, LANES), F32)

        acc = o_ref[...]
        for bb in range(NUM_BUCKETS):
            s = jnp.sum(jnp.where(bkt == bb, g, 0.0))
            acc = jnp.where((row == bb) & (lane == h), s, acc)
        o_ref[...] = acc

    g_spec = pl.BlockSpec((None, BLK, 2 * BLK), lambda h: (h, 0, 0))
    return pl.pallas_call(
        body, grid=(N_BIAS_HEADS,),
        in_specs=[g_spec] * ng + [pl.BlockSpec((None, None, BLK, 2 * BLK), lambda h: (jnp.minimum(h // 4, 3), 1, 0, 0))],
        out_specs=pl.BlockSpec((NUM_BUCKETS, LANES), lambda h: (0, 0)),
        out_shape=jax.ShapeDtypeStruct((NUM_BUCKETS, LANES), F32),
        compiler_params=_cp("arbitrary"), name=name)(*gs, buckets)


def _to_class_major(src_ref, dst_refs, d, fn=None):
    ln = S // d
    for r in range(d):
        v = src_ref[pl.ds(r, ln, stride=d), :] if d > 1 else src_ref[...]
        outs = fn(v) if fn is not None else (v,) * len(dst_refs)
        for dst, o in zip(dst_refs, outs):
            dst[pl.ds(r * ln, ln), :] = o.astype(dst.dtype)


def _head_masks(rows):
    lane = lax.broadcasted_iota(jnp.int32, (rows, LANES), 1)
    return lane < HD, lane >= HD


def _split_heads(v):
    m0, m1 = _head_masks(v.shape[0])
    return jnp.where(m0, v, 0.0), jnp.where(m1, v, 0.0)


def _dup_head(v, hi):
    m0, _ = _head_masks(v.shape[0])
    r = pltpu.roll(v, HD, 1)
    return jnp.where(m0, jnp.where(hi, r, v), jnp.where(hi, v, r))


def _block_rows(b, d):
    nbc = NB // d
    i = b % nbc
    r = b // nbc
    has_prev = (i > 0).astype(jnp.int32)
    prev = pl.multiple_of(jnp.maximum(b - 1, 0) * BLK, BLK)
    nat = i * (BLK * d) + r
    return has_prev, prev, nat


def _lane_halves(v0, v1):
    lane = lax.broadcasted_iota(jnp.int32, (v0.shape[0], LANES), 1)
    return jnp.where(lane < HD, v0, v1)


def _band_fwd(proj, bias, sinks, *, d, q0, k0, v0, npairs, bias0, shared_kv, name):
    def body(sink_ref, q_ref, k_ref, v_ref, b_ref, num_ref, st_ref, qz0, qz1, ks, vs):
        p = pl.program_id(0)
        kv = (lambda v: (_dup_head(v, p >= 2),)) if shared_kv else None
        _to_class_major(q_ref, (qz0, qz1), d, lambda v: _split_heads(v * SCALE))
        _to_class_major(k_ref, (ks,), d, kv)
        _to_class_major(v_ref, (vs,), d, kv)
        lane = lax.broadcasted_iota(jnp.int32, (BLK, LANES), 1)

        def blk(b, carry):
            has_prev, prev, nat = _block_rows(b, d)
            cur = pl.multiple_of(b * BLK, BLK)
            k2 = jnp.concatenate([ks[pl.ds(prev, BLK), :], ks[pl.ds(cur, BLK), :]], axis=0)
            v2 = jnp.concatenate([vs[pl.ds(prev, BLK), :], vs[pl.ds(cur, BLK), :]], axis=0)
            nums, ms, ls = [], [], []
            for hh, qz in enumerate((qz0, qz1)):
                z = _dot(qz[pl.ds(cur, BLK), :], k2, 1, 1) + b_ref[hh, has_prev]
                m = jnp.max(z, axis=1, keepdims=True)
                e = jnp.exp(z - m)
                l = jnp.sum(e, axis=1, keepdims=True)
                num = _dot(e.astype(BF16), v2, 1, 0)
                if shared_kv:
                    sink = sink_ref[0, 2 * p + hh]
                    mx = jnp.maximum(m, sink)
                    c = jnp.exp(m - mx)
                    zden = l * c + jnp.exp(sink - mx)
                    num = num * (c / zden)
                    m = mx + jnp.log(zden)
                ls.append(l)
                ms.append(m)
                nums.append(num)
            num_t = jnp.where(lane < HD, nums[0], nums[1])
            if shared_kv:
                st_t = jnp.where(lane < HD, ms[0], ms[1])
            else:
                st_t = jnp.where(lane < 32, ms[0], jnp.where(lane < 64, ls[0], jnp.where(lane < 96, ms[1], ls[1])))
            if d > 1:
                num_ref[pl.ds(nat, BLK, stride=d), :] = num_t
                st_ref[pl.ds(nat, BLK, stride=d), :] = st_t
            else:
                num_ref[pl.ds(cur, BLK), :] = num_t
                st_ref[pl.ds(cur, BLK), :] = st_t
            return carry

        lax.fori_loop(0, NB, blk, 0, unroll=8)

    slab = lambda off, per_pair: pl.BlockSpec((None, S, LANES), (lambda p: (off + p, 0, 0)) if per_pair else (lambda p: (off, 0, 0)))
    out = pl.BlockSpec((None, S, LANES), lambda p: (p, 0, 0))
    return pl.pallas_call(
        body, grid=(npairs,),
        in_specs=[SMEM, slab(q0, True), slab(k0, not shared_kv), slab(v0, not shared_kv),
                  pl.BlockSpec((None, 2, 2, BLK, 2 * BLK), lambda p: (bias0 + p, 0, 0, 0, 0))],
        out_specs=[out, out],
        out_shape=[jax.ShapeDtypeStruct((npairs, S, LANES), F32)] * 2,
        scratch_shapes=[pltpu.VMEM((S, LANES), BF16)] * 4,
        compiler_params=_cp("arbitrary"), name=name)(sinks, proj, proj, proj, bias)


def _combine_a(nums, stats, name):
    rt = 512

    def body(n0, n1, n2, s0, s1, s2, o_ref, l_ref):
        n_refs, s_refs = (n0, n1, n2), (s0, s1, s2)
        outs, lses = [], []
        for hh in range(2):
            ms = [s[:, 64 * hh:64 * hh + 1] for s in s_refs]
            ls = [s[:, 64 * hh + 32:64 * hh + 33] for s in s_refs]
            mx = jnp.maximum(jnp.maximum(ms[0], ms[1]), ms[2])
            cs = [jnp.exp(m - mx) for m in ms]
            z = cs[0] * ls[0] + cs[1] * ls[1] + cs[2] * ls[2]
            acc = cs[0] * n_refs[0][:, hh * HD:(hh + 1) * HD]
            acc = acc + cs[1] * n_refs[1][:, hh * HD:(hh + 1) * HD]
            acc = acc + cs[2] * n_refs[2][:, hh * HD:(hh + 1) * HD]
            outs.append(acc / z)
            lses.append(mx + jnp.log(z))
        o_ref[...] = jnp.concatenate(outs, axis=1)
        l_ref[...] = _lane_halves(lses[0], lses[1])

    spec = pl.BlockSpec((None, rt, LANES), lambda p, i: (p, i, 0))
    return pl.pallas_call(
        body, grid=(2, S // rt), in_specs=[spec] * 6, out_specs=[spec, spec],
        out_shape=[jax.ShapeDtypeStruct((2, S, LANES), F32)] * 2,
        compiler_params=_cp("parallel", "parallel"), name=name)(*nums, *stats)


def _band_bwd(proj, bias, o, do, lse, sinks, dqkv, *, d, q0, k0, v0, npairs, bias0, shared_kv, name):
    def body(sink_ref, q_ref, k_ref, v_ref, b_ref, o_ref, do_ref, lse_ref, dqkv_in, dqkv_ref, g_ref, ds_ref,
             qz0, qz1, ks, vs, doz0, doz1, ls0, ls1, dls0, dls1, stage, dq_nat, dk_cm, dv_cm, kv_nat, dk_acc, dv_acc,
             obuf, osem):
        p = pl.program_id(0)
        dq_ref, dk_ref, dv_ref = obuf.at[0], obuf.at[1], obuf.at[2]
        m0, m1 = _head_masks(S)
        kk = lax.broadcasted_iota(jnp.int32, (2 * LANES, LANES), 0) % LANES
        ll = lax.broadcasted_iota(jnp.int32, (2 * LANES, LANES), 1)
        hi, lo = _split2(do_ref[...] * o_ref[...])
        dl = _dot(jnp.concatenate([hi, lo], axis=1), ((kk < HD) == (ll < HD)).astype(BF16), 1, 0)
        if shared_kv:
            row8 = lax.broadcasted_iota(jnp.int32, (8, LANES), 0)
            lane8 = lax.broadcasted_iota(jnp.int32, (8, LANES), 1)
            sinkv = jnp.where(m0, sink_ref[0, 2 * p], sink_ref[0, 2 * p + 1])
            contrib = jnp.exp(sinkv - lse_ref[...]) * dl
            t = jnp.zeros((8, LANES), F32)
            for hh, mh in enumerate((m0, m1)):
                dsink = -jnp.sum(jnp.where(mh, contrib, 0.0)) * (1.0 / HD)
                t = jnp.where((row8 == 0) & (lane8 == hh), dsink, t)
            ds_ref[...] = t
        else:
            ds_ref[...] = jnp.zeros((8, LANES), F32)
        kv = (lambda v: (_dup_head(v, p >= 2),)) if shared_kv else None
        _to_class_major(q_ref, (qz0, qz1), d, lambda v: _split_heads(v * SCALE))
        _to_class_major(k_ref, (ks,), d, kv)
        _to_class_major(v_ref, (vs,), d, kv)
        _to_class_major(do_ref, (doz0, doz1), d, _split_heads)
        def spread(v):
            a0, a1 = _head_masks(v.shape[0])
            r = pltpu.roll(v, HD, 1)
            return jnp.where(a0, v, r), jnp.where(a1, v, r)

        _to_class_major(lse_ref, (ls0, ls1), d, spread)
        stage[...] = dl
        _to_class_major(stage, (dls0, dls1), d, spread)

        dk_cm[...] = jnp.zeros((S, LANES), F32)
        dv_cm[...] = jnp.zeros((S, LANES), F32)
        g_ref[...] = jnp.zeros((2, BLK, 2 * BLK), F32)
        lane = lax.broadcasted_iota(jnp.int32, (BLK, LANES), 1)

        def blk(b, carry):
            has_prev, prev, nat = _block_rows(b, d)
            cur = pl.multiple_of(b * BLK, BLK)
            k2 = jnp.concatenate([ks[pl.ds(prev, BLK), :], ks[pl.ds(cur, BLK), :]], axis=0)
            v2 = jnp.concatenate([vs[pl.ds(prev, BLK), :], vs[pl.ds(cur, BLK), :]], axis=0)
            dqs, dks, dvs = [], [], []
            for hh, (qz, doz, lsr, dlr) in enumerate(((qz0, doz0, ls0, dls0), (qz1, doz1, ls1, dls1))):
                qb = qz[pl.ds(cur, BLK), :]
                dob = doz[pl.ds(cur, BLK), :]
                lb = lsr[pl.ds(cur, BLK), :]
                dlb = dlr[pl.ds(cur, BLK), :]
                z = _dot(qb, k2, 1, 1) + b_ref[hh, has_prev]
                pr = jnp.exp(z - jnp.concatenate([lb, lb], axis=1))
                dp = _dot(dob, v2, 1, 1)
                dz = pr * (dp - jnp.concatenate([dlb, dlb], axis=1))
                g_ref[hh] += dz
                dzb = dz.astype(BF16)
                dqs.append(_dot(dzb, k2, 1, 0))
                dks.append(_dot(dzb, qb, 0, 0))
                dvs.append(_dot(pr.astype(BF16), dob, 0, 0))
            dq_t = jnp.where(lane < HD, dqs[0], dqs[1]) * SCALE
            dk_t = dks[0] + dks[1]
            dv_t = dvs[0] + dvs[1]
            dk_cm[pl.ds(prev, BLK), :] += dk_t[:BLK]
            dk_cm[pl.ds(cur, BLK), :] += dk_t[BLK:]
            dv_cm[pl.ds(prev, BLK), :] += dv_t[:BLK]
            dv_cm[pl.ds(cur, BLK), :] += dv_t[BLK:]
            if d > 1:
                dq_nat[pl.ds(nat, BLK, stride=d), :] = dq_t
            else:
                dq_nat[pl.ds(cur, BLK), :] = dq_t
            return carry

        lax.fori_loop(0, NB, blk, 0, unroll=8)
        dq_ref[...] = dq_nat[...].astype(BF16)

        def from_class_major(src, dst_ref):
            if d == 1:
                dst_ref[...] = src[...].astype(BF16)
            else:
                ln = S // d
                for r in range(d):
                    kv_nat[pl.ds(r, ln, stride=d), :] = src[pl.ds(r * ln, ln), :]
                dst_ref[...] = kv_nat[...].astype(BF16)

        def put(i, slab):
            return pltpu.make_async_copy(obuf.at[i], dqkv_ref.at[slab], osem.at[i])

        put(0, q0 + p).start()
        if not shared_kv:
            from_class_major(dk_cm, dk_ref)
            from_class_major(dv_cm, dv_ref)
            put(1, k0 + p).start()
            put(2, v0 + p).start()
            put(1, k0 + p).wait()
            put(2, v0 + p).wait()
        else:
            @pl.when(p == 0)
            def _():
                dk_acc[...] = jnp.zeros((S, LANES), F32)
                dv_acc[...] = jnp.zeros((S, LANES), F32)

            mine = m1 == (p >= 2)
            for cm, acc in ((dk_cm, dk_acc), (dv_cm, dv_acc)):
                val = cm[...]
                acc[...] += jnp.where(mine, val + pltpu.roll(val, HD, 1), 0.0)

            @pl.when(p == npairs - 1)
            def _():
                from_class_major(dk_acc, dk_ref)
                from_class_major(dv_acc, dv_ref)
                put(1, k0).start()
                put(2, v0).start()
                put(1, k0).wait()
                put(2, v0).wait()

        put(0, q0 + p).wait()

    slab = lambda off, per_pair: pl.BlockSpec((None, S, LANES), (lambda p: (off + p, 0, 0)) if per_pair else (lambda p: (off, 0, 0)))
    pair = pl.BlockSpec((None, S, LANES), lambda p: (p, 0, 0))
    return pl.pallas_call(
        body, grid=(npairs,),
        in_specs=[SMEM, slab(q0, True), slab(k0, not shared_kv), slab(v0, not shared_kv),
                  pl.BlockSpec((None, 2, 2, BLK, 2 * BLK), lambda p: (bias0 + p, 0, 0, 0, 0)),
                  pair, pair, pair, ANY],
        out_specs=[ANY,
                   pl.BlockSpec((None, 2, BLK, 2 * BLK), lambda p: (p, 0, 0, 0)),
                   pl.BlockSpec((None, 8, LANES), lambda p: (p, 0, 0))],
        out_shape=[jax.ShapeDtypeStruct(dqkv.shape, BF16),
                   jax.ShapeDtypeStruct((npairs, 2, BLK, 2 * BLK), F32),
                   jax.ShapeDtypeStruct((npairs, 8, LANES), F32)],
        scratch_shapes=[pltpu.VMEM((S, LANES), BF16)] * 6 + [pltpu.VMEM((S, LANES), F32)] * 11
        + [pltpu.VMEM((3, S, LANES), BF16), pltpu.SemaphoreType.DMA((3,))],
        input_output_aliases={8: 0},
        compiler_params=_cp("arbitrary"), name=name)(sinks, proj, proj, proj, bias, o, do, lse, dqkv)


KC = 512
NSUB = KC // BLK
QB = 512
QPG = KC // QB


def _split2(x):
    hi = x.astype(BF16)
    lo = (x - hi.astype(F32)).astype(BF16)
    return hi, lo


def _tri_ones(cmp):
    jj = lax.broadcasted_iota(jnp.int32, (2 * BLK, BLK), 0) % BLK
    ss = lax.broadcasted_iota(jnp.int32, (2 * BLK, BLK), 1)
    return jnp.concatenate([cmp(jj, ss).astype(BF16), jnp.ones((2 * BLK, BLK), BF16)], axis=1)


def _sub_sums(x, tri1):
    n = x.shape[0]
    st = jnp.concatenate([x[:, s * BLK:(s + 1) * BLK] for s in range(NSUB)], axis=0)
    hi, lo = _split2(st)
    r = _dot(jnp.concatenate([hi, lo], axis=1), tri1, 1, 0)
    return ([r[s * n:(s + 1) * n, :BLK] for s in range(NSUB)], [r[s * n:(s + 1) * n, BLK:] for s in range(NSUB)])


def _log_sig_pair(z):
    lb = jnp.minimum(z, 0.0) - jnp.log1p(jnp.exp(-jnp.abs(z)))
    return lb, lb - z


QGROUPS = NB // NSUB


def _stick_fwd(proj, *, q0, k0, v0, name):
    def body(q_ref, k_ref, v_ref, o_ref, t_ref, qs, ks, vs):
        qs[...] = (q_ref[...] * SCALE).astype(BF16)
        ks[...] = k_ref[...].astype(BF16)
        vs[...] = v_ref[...].astype(BF16)
        tri1 = _tri_ones(lambda j, s: j > s)
        col = lax.broadcasted_iota(jnp.int32, (QB, KC), 1)
        rowi = lax.broadcasted_iota(jnp.int32, (QB, KC), 0)

        for qg in range(QGROUPS):
            def qblock(ii, carry0, qg=qg):
                t0 = pl.multiple_of((qg * QPG + ii) * QB, QB)
                qb = qs[pl.ds(t0, QB), :]
                accs = [jnp.zeros((QB, HD), F32)] * 2
                runs = [jnp.zeros((QB, BLK), F32)] * 2
                for c in reversed(range(qg + 1)):
                    s0 = c * KC
                    diag = c == qg
                    before = (s0 + col) < (t0 + rowi) if diag else None
                    for hh in range(2):
                        kh = ks[s0:s0 + KC, hh * HD:(hh + 1) * HD]
                        vh = vs[s0:s0 + KC, hh * HD:(hh + 1) * HD]
                        lb, lk = _log_sig_pair(_dot(qb[:, hh * HD:(hh + 1) * HD], kh, 1, 1))
                        if diag:
                            lk = jnp.where(before, lk, 0.0)
                        suf, tot = _sub_sums(lk, tri1)
                        ws, run = [], runs[hh]
                        for s in reversed(range(NSUB)):
                            ws.append(jnp.exp(lb[:, s * BLK:(s + 1) * BLK] + suf[s] + run))
                            run = run + tot[s]
                        w = jnp.concatenate(ws[::-1], axis=1)
                        if diag:
                            w = jnp.where(before, w, 0.0)
                        accs[hh] = accs[hh] + _dot(w.astype(BF16), vh, 1, 0)
                        runs[hh] = run
                o_ref[pl.ds(t0, QB), :] = jnp.concatenate(accs, axis=1)
                t_ref[pl.ds(t0, QB), :] = _lane_halves(runs[0], runs[1])
                return carry0

            lax.fori_loop(0, QPG, qblock, 0)

    slab = lambda off: pl.BlockSpec((None, S, LANES), lambda p: (off + p, 0, 0))
    out = pl.BlockSpec((None, S, LANES), lambda p: (p, 0, 0))
    return pl.pallas_call(
        body, grid=(2,), in_specs=[slab(q0), slab(k0), slab(v0)], out_specs=[out, out],
        out_shape=[jax.ShapeDtypeStruct((2, S, LANES), F32)] * 2,
        scratch_shapes=[pltpu.VMEM((S, LANES), BF16)] * 3,
        compiler_params=_cp("arbitrary"), name=name)(proj, proj, proj)


def _stick_bwd(proj, do, tot, dqkv, *, q0, k0, v0, name):
    def body(q_ref, k_ref, v_ref, do_ref, t_ref, dqkv_in, dqkv_ref, qs, ks, vs, dos, dk_acc, dv_acc, obuf, osem):
        p = pl.program_id(0)
        dq_ref, dk_ref, dv_ref = obuf.at[0], obuf.at[1], obuf.at[2]
        qs[...] = (q_ref[...] * SCALE).astype(BF16)
        ks[...] = k_ref[...].astype(BF16)
        vs[...] = v_ref[...].astype(BF16)
        dos[...] = do_ref[...].astype(BF16)
        dk_acc[...] = jnp.zeros((2, S, HD), F32)
        dv_acc[...] = jnp.zeros((2, S, HD), F32)
        tri_inc = _tri_ones(lambda j, s: j <= s)
        tri_exc = _tri_ones(lambda j, s: j < s)
        col = lax.broadcasted_iota(jnp.int32, (QB, KC), 1)
        rowi = lax.broadcasted_iota(jnp.int32, (QB, KC), 0)

        for qg in range(QGROUPS):
            def qblock(ii, carry0, qg=qg):
                t0 = pl.multiple_of((qg * QPG + ii) * QB, QB)
                qb = qs[pl.ds(t0, QB), :]
                dob = dos[pl.ds(t0, QB), :]
                tb = t_ref[pl.ds(t0, QB), :]
                dqs = [jnp.zeros((QB, HD), F32)] * 2
                pruns = [jnp.zeros((QB, BLK), F32)] * 2
                eruns = [jnp.zeros((QB, BLK), F32)] * 2
                for c in range(qg + 1):
                    s0 = c * KC
                    diag = c == qg
                    before = (s0 + col) < (t0 + rowi) if diag else None
                    for hh in range(2):
                        qh = qb[:, hh * HD:(hh + 1) * HD]
                        doh = dob[:, hh * HD:(hh + 1) * HD]
                        tt = tb[:, 64 * hh:64 * hh + 1]
                        kh = ks[s0:s0 + KC, hh * HD:(hh + 1) * HD]
                        vh = vs[s0:s0 + KC, hh * HD:(hh + 1) * HD]
                        lb, lk = _log_sig_pair(_dot(qh, kh, 1, 1))
                        if diag:
                            lk = jnp.where(before, lk, 0.0)
                        pin, ptot = _sub_sums(lk, tri_inc)
                        ws, prun = [], pruns[hh]
                        for s in range(NSUB):
                            ws.append(jnp.exp(lb[:, s * BLK:(s + 1) * BLK] + (tt - (pin[s] + prun))))
                            prun = prun + ptot[s]
                        w = jnp.concatenate(ws, axis=1)
                        if diag:
                            w = jnp.where(before, w, 0.0)
                        e = w * _dot(doh, vh, 1, 1)
                        pex, etot = _sub_sums(e, tri_exc)
                        cs, erun = [], eruns[hh]
                        for s in range(NSUB):
                            cs.append(pex[s] + erun)
                            erun = erun + etot[s]
                        sig = jnp.exp(lb)
                        dz = e * (1.0 - sig) - jnp.concatenate(cs, axis=1) * sig
                        if diag:
                            dz = jnp.where(before, dz, 0.0)
                        dz = dz.astype(BF16)
                        dqs[hh] = dqs[hh] + _dot(dz, kh, 1, 0)
                        dk_acc[hh, s0:s0 + KC, :] += _dot(dz, qh, 0, 0)
                        dv_acc[hh, s0:s0 + KC, :] += _dot(w.astype(BF16), doh, 0, 0)
                        pruns[hh], eruns[hh] = prun, erun
                dq_ref[pl.ds(t0, QB), :] = (jnp.concatenate(dqs, axis=1) * SCALE).astype(BF16)
                return carry0

            lax.fori_loop(0, QPG, qblock, 0)
        dk_ref[...] = jnp.concatenate([dk_acc[0], dk_acc[1]], axis=1).astype(BF16)
        dv_ref[...] = jnp.concatenate([dv_acc[0], dv_acc[1]], axis=1).astype(BF16)
        puts = [pltpu.make_async_copy(obuf.at[i], dqkv_ref.at[off + p], osem.at[i]) for i, off in enumerate((q0, k0, v0))]
        for cp in puts:
            cp.start()
        for cp in puts:
            cp.wait()

    slab = lambda off: pl.BlockSpec((None, S, LANES), lambda p: (off + p, 0, 0))
    pair = pl.BlockSpec((None, S, LANES), lambda p: (p, 0, 0))
    return pl.pallas_call(
        body, grid=(2,), in_specs=[slab(q0), slab(k0), slab(v0), pair, pair, ANY], out_specs=ANY,
        out_shape=jax.ShapeDtypeStruct(dqkv.shape, BF16),
        scratch_shapes=[pltpu.VMEM((S, LANES), BF16)] * 4 + [pltpu.VMEM((2, S, HD), F32)] * 2
        + [pltpu.VMEM((3, S, LANES), BF16), pltpu.SemaphoreType.DMA((3,))],
        input_output_aliases={5: 0},
        compiler_params=_cp("arbitrary"), name=name)(proj, proj, proj, do, tot, dqkv)


def _cat_slabs(ref):
    return jnp.concatenate([ref[s] for s in range(ref.shape[0])], axis=1)


def _merge_fwd(o_a, o_b, o_c, gates, b_gate, wa, wb, wc, w_out, name):
    tm = ROW_TILE

    def body(oa_ref, ob_ref, oc_ref, g_ref, bg_ref, wa_ref, wb_ref, wc_ref, wo_ref, mg_ref, mo_ref):
        acc = jnp.zeros((tm, D), F32)
        for i, (o_ref, w_ref) in enumerate(((oa_ref, wa_ref), (ob_ref, wb_ref), (oc_ref, wc_ref))):
            pr = _dot(_cat_slabs(o_ref).astype(BF16), w_ref[...], 1, 0)
            sg = jax.nn.sigmoid(g_ref[:, i * D:(i + 1) * D] + bg_ref[i:i + 1, :])
            acc = acc + sg * pr
        mg = acc.astype(BF16)
        mg_ref[...] = mg
        mo_ref[...] = _dot(mg, wo_ref[...], 1, 0)

    slabs = lambda n: pl.BlockSpec((n, tm, LANES), lambda i: (0, i, 0))
    full = lambda r, c: pl.BlockSpec((r, c), lambda i: (0, 0))
    row = pl.BlockSpec((tm, D), lambda i: (i, 0))
    return pl.pallas_call(
        body, grid=(S // tm,),
        in_specs=[slabs(2), slabs(4), slabs(2), pl.BlockSpec((tm, GATE_COLS), lambda i: (i, 0)), full(3, D),
                  full(256, D), full(512, D), full(256, D), full(D, D)],
        out_specs=[row, row],
        out_shape=[jax.ShapeDtypeStruct((S, D), BF16), jax.ShapeDtypeStruct((S, D), F32)],
        compiler_params=_cp("parallel"), name=name)(o_a, o_b, o_c, gates, b_gate, wa, wb, wc, w_out)


def _merge_bwd(d_mo, o_a, o_b, o_c, gates, b_gate, wa, wb, wc, w_out, name):
    tm = ROW_TILE

    def body(dmo_ref, oa_ref, ob_ref, oc_ref, g_ref, bg_ref, wa_ref, wb_ref, wc_ref, wo_ref,
             doa_ref, dob_ref, doc_ref, dg_ref, dwa_ref, dwb_ref, dwc_ref, dbg_ref):
        @pl.when(pl.program_id(0) == 0)
        def _():
            dwa_ref[...] = jnp.zeros(dwa_ref.shape, F32)
            dwb_ref[...] = jnp.zeros(dwb_ref.shape, F32)
            dwc_ref[...] = jnp.zeros(dwc_ref.shape, F32)
            dbg_ref[...] = jnp.zeros(dbg_ref.shape, F32)

        dmg = _dot(dmo_ref[...], wo_ref[...], 1, 1)
        trip = ((oa_ref, wa_ref, doa_ref, dwa_ref), (ob_ref, wb_ref, dob_ref, dwb_ref), (oc_ref, wc_ref, doc_ref, dwc_ref))
        for i, (o_ref, w_ref, do_ref, dw_ref) in enumerate(trip):
            ob = _cat_slabs(o_ref).astype(BF16)
            pr = _dot(ob, w_ref[...], 1, 0)
            sg = jax.nn.sigmoid(g_ref[:, i * D:(i + 1) * D] + bg_ref[i:i + 1, :])
            dgate = dmg * pr * sg * (1.0 - sg)
            dg_ref[:, i * D:(i + 1) * D] = dgate.astype(BF16)
            dbg_ref[i:i + 1, :] += jnp.sum(dgate, axis=0, keepdims=True)
            dpr = (dmg * sg).astype(BF16)
            do = _dot(dpr, w_ref[...], 1, 1)
            for s in range(do_ref.shape[0]):
                do_ref[s] = do[:, s * LANES:(s + 1) * LANES]
            dw_ref[...] += _dot(ob, dpr, 0, 0)

    slabs = lambda n: pl.BlockSpec((n, tm, LANES), lambda i: (0, i, 0))
    full = lambda r, c: pl.BlockSpec((r, c), lambda i: (0, 0))
    row = pl.BlockSpec((tm, D), lambda i: (i, 0))
    return pl.pallas_call(
        body, grid=(S // tm,),
        in_specs=[row, slabs(2), slabs(4), slabs(2), pl.BlockSpec((tm, GATE_COLS), lambda i: (i, 0)), full(3, D),
                  full(256, D), full(512, D), full(256, D), full(D, D)],
        out_specs=[slabs(2), slabs(4), slabs(2), pl.BlockSpec((tm, GATE_COLS), lambda i: (i, 0)),
                   full(256, D), full(512, D), full(256, D), full(3, D)],
        out_shape=[jax.ShapeDtypeStruct((2, S, LANES), F32), jax.ShapeDtypeStruct((4, S, LANES), F32),
                   jax.ShapeDtypeStruct((2, S, LANES), F32), jax.ShapeDtypeStruct((S, GATE_COLS), BF16),
                   jax.ShapeDtypeStruct((256, D), F32), jax.ShapeDtypeStruct((512, D), F32),
                   jax.ShapeDtypeStruct((256, D), F32), jax.ShapeDtypeStruct((3, D), F32)],
        compiler_params=_cp("arbitrary"), name=name)(d_mo, o_a, o_b, o_c, gates, b_gate, wa, wb, wc, w_out)


FC = 256
GELU_K = math.sqrt(2.0 / math.pi)
GELU_C = 0.044715


RC = 64
NRC = S // RC


def _down(tail, cur, n):
    row = lax.broadcasted_iota(jnp.int32, tail.shape, 0)
    rolled = pltpu.roll(cur, n, 0)
    first = jnp.where(row < n, pltpu.roll(tail, n, 0), rolled[0:8])
    return jnp.concatenate([first, rolled[8:]], axis=0)


def _up(cur, head, n):
    row = lax.broadcasted_iota(jnp.int32, head.shape, 0)
    rolled = pltpu.roll(cur, RC - n, 0)
    last = jnp.where(row >= 8 - n, pltpu.roll(head, 8 - n, 0), rolled[RC - 8:])
    return jnp.concatenate([rolled[:RC - 8], last], axis=0)


def _conv_chunk(load, j, w_ref, b_ref, half):
    r0 = pl.multiple_of(j * RC, RC)
    cur = load(r0, RC).astype(F32)
    tail = load(pl.multiple_of(jnp.maximum(r0 - 16, 0), 16), 16).astype(F32)[8:16]
    tail = jnp.where(j > 0, tail, 0.0)
    d1 = _down(tail, cur, 1)
    d2 = _down(tail, cur, 2)
    y = w_ref[0:1, half, :] * d2 + w_ref[1:2, half, :] * d1 + w_ref[2:3, half, :] * cur + b_ref[half:half + 1, :]
    return y, cur, d1, d2


def _chunk(j):
    return pl.ds(pl.multiple_of(j * RC, RC), RC)


def _fold8(x):
    return jnp.sum(x.reshape(RC // 8, 8, x.shape[-1]), axis=0)


def _ffn_act(u, conv_w, conv_b, name):
    def body(u_ref, w_ref, b_ref, a_ref, y_ref):
        def step(j, carry):
            yg = _conv_chunk(lambda r, n: u_ref[0, pl.ds(r, n), :], j, w_ref, b_ref, 0)[0]
            yv = _conv_chunk(lambda r, n: u_ref[1, pl.ds(r, n), :], j, w_ref, b_ref, 1)[0]
            th = jnp.tanh(GELU_K * (yg + GELU_C * yg * yg * yg))
            a_ref[_chunk(j), :] = (0.5 * yg * (1.0 + th) * yv).astype(BF16)
            y_ref[0, _chunk(j), :] = yg.astype(BF16)
            y_ref[1, _chunk(j), :] = yv.astype(BF16)
            return carry

        lax.fori_loop(0, NRC, step, 0)

    return pl.pallas_call(
        body, grid=(D_FF // FC,),
        in_specs=[pl.BlockSpec((2, S, FC), lambda j: (0, 0, j)), pl.BlockSpec((3, 2, FC), lambda j: (0, 0, j)),
                  pl.BlockSpec((2, FC), lambda j: (0, j))],
        out_specs=[pl.BlockSpec((S, FC), lambda j: (0, j)), pl.BlockSpec((2, S, FC), lambda j: (0, 0, j))],
        out_shape=[jax.ShapeDtypeStruct((S, D_FF), BF16), jax.ShapeDtypeStruct((2, S, D_FF), BF16)],
        compiler_params=_cp("parallel"), name=name)(u, conv_w, conv_b)


def _ffn_act_bwd(u, y, d_a, conv_w, name):
    def body(u_ref, y_ref, da_ref, w_ref, du_ref, dw_ref, db_ref, dy_s):
        def first(j, acc):
            yg = y_ref[0, _chunk(j), :].astype(F32)
            yv = y_ref[1, _chunk(j), :].astype(F32)
            th = jnp.tanh(GELU_K * (yg + GELU_C * yg * yg * yg))
            gelu = 0.5 * yg * (1.0 + th)
            dgelu = 0.5 * (1.0 + th) + 0.5 * yg * (1.0 - th * th) * GELU_K * (1.0 + 3.0 * GELU_C * yg * yg)
            da = da_ref[_chunk(j), :].astype(F32)
            dyg = da * yv * dgelu
            dyv = da * gelu
            dy_s[0, _chunk(j), :] = dyg
            dy_s[1, _chunk(j), :] = dyv
            return acc[0] + _fold8(dyg), acc[1] + _fold8(dyv)

        zero = jnp.zeros((8, FC), F32)
        accb = lax.fori_loop(0, NRC, first, (zero, zero))
        for half in range(2):
            db_ref[half:half + 1, :] = jnp.sum(accb[half], axis=0, keepdims=True)

        def second(j, acc):
            new = []
            for half in range(2):
                cur = dy_s[half, _chunk(j), :]
                h0 = pl.multiple_of(jnp.minimum((j + 1) * RC, S - 8), 8)
                head = jnp.where(j < NRC - 1, dy_s[half, pl.ds(h0, 8), :], 0.0)
                up1 = _up(cur, head, 1)
                up2 = _up(cur, head, 2)
                du = w_ref[2:3, half, :] * cur + w_ref[1:2, half, :] * up1 + w_ref[0:1, half, :] * up2
                du_ref[half, _chunk(j), :] = du.astype(BF16)
                uu = u_ref[half, _chunk(j), :].astype(F32)
                new += [_fold8(up2 * uu), _fold8(up1 * uu), _fold8(cur * uu)]
            return tuple(a + n for a, n in zip(acc, new))

        accw = lax.fori_loop(0, NRC, second, tuple(zero for _ in range(6)))
        for half in range(2):
            for k in range(3):
                dw_ref[k:k + 1, half, :] = jnp.sum(accw[3 * half + k], axis=0, keepdims=True)

    return pl.pallas_call(
        body, grid=(D_FF // FC,),
        in_specs=[pl.BlockSpec((2, S, FC), lambda j: (0, 0, j)), pl.BlockSpec((2, S, FC), lambda j: (0, 0, j)),
                  pl.BlockSpec((S, FC), lambda j: (0, j)), pl.BlockSpec((3, 2, FC), lambda j: (0, 0, j))],
        out_specs=[pl.BlockSpec((2, S, FC), lambda j: (0, 0, j)), pl.BlockSpec((3, 2, FC), lambda j: (0, 0, j)),
                   pl.BlockSpec((2, FC), lambda j: (0, j))],
        out_shape=[jax.ShapeDtypeStruct((2, S, D_FF), BF16), jax.ShapeDtypeStruct((3, 2, D_FF), F32),
                   jax.ShapeDtypeStruct((2, D_FF), F32)],
        scratch_shapes=[pltpu.VMEM((2, S, FC), F32)],
        compiler_params=_cp("parallel"), name=name)(u, y, d_a, conv_w)


def _layer_fwd(x, h1, w, bias, lname):
    n = lambda s: f"{lname}_{s}"
    w.need("in", h1)
    tn = 768
    proj = _mm(h1, w["w_in"], grid=(1, QKV_COLS // tn, 1),
               a_spec=pl.BlockSpec((S, D), lambda i, j, k: (i, 0)),
               b_spec=pl.BlockSpec((tn, D), lambda i, j, k: (j, 0)),
               out_shape=jax.ShapeDtypeStruct((QKV_SLABS, S, LANES), F32),
               out_spec=pl.BlockSpec((tn // LANES, S, LANES), lambda i, j, k: (j, i, 0)),
               ca=1, cb=1, acc_shape=(S, tn), out_slab=True, name=n("proj_qkv"))
    gates = _mm(h1, w["w_in"], grid=(1, GATE_COLS // tn, 1),
                a_spec=pl.BlockSpec((S, D), lambda i, j, k: (i, 0)),
                b_spec=pl.BlockSpec((tn, D), lambda i, j, k: (j + QKV_COLS // tn, 0)),
                out_shape=jax.ShapeDtypeStruct((S, GATE_COLS), BF16),
                out_spec=pl.BlockSpec((S, tn), lambda i, j, k: (i, j)),
                ca=1, cb=1, acc_shape=(S, tn), name=n("proj_gate"))
    nums, stats = [], []
    for g, (_, d) in enumerate(A_GROUPS):
        nm, st = _band_fwd(proj, bias, w["sinks"], d=d, q0=2 * g, k0=6 + 2 * g, v0=12 + 2 * g, npairs=2, bias0=2 * g,
                           shared_kv=False, name=n(f"attn_a{g}_fwd"))
        nums.append(nm)
        stats.append(st)
    o_a, lse_a = _combine_a(nums, stats, n("attn_a_combine"))
    o_b, lse_b = _band_fwd(proj, bias, w["sinks"], d=1, q0=18, k0=22, v0=23, npairs=4, bias0=6, shared_kv=True,
                           name=n("attn_b_fwd"))
    o_c, tot_c = _stick_fwd(proj, q0=24, k0=26, v0=28, name=n("attn_c_fwd"))
    w.need("mix", tot_c)
    merged, mo = _merge_fwd(o_a, o_b, o_c, gates, w["b_gate"], w["w_br_a"], w["w_br_b"], w["w_br_c"], w["w_out"], n("merge_fwd"))
    x2, h2 = _postnorm_res(x, mo, w["attn_post_norm"], w["ffn_pre_norm"], n("attn_post"))
    w.need("ffn", h2)
    u = _mm(h2, w["w_up"], grid=(1, 2 * D_FF // 1024, 1),
            a_spec=pl.BlockSpec((S, D), lambda i, j, k: (i, 0)),
            b_spec=pl.BlockSpec((D, 1024), lambda i, j, k: (0, j)),
            out_shape=jax.ShapeDtypeStruct((2, S, D_FF), BF16),
            out_spec=pl.BlockSpec((None, S, 1024), lambda i, j, k: (j // 4, i, j % 4)),
            ca=1, cb=0, acc_shape=(S, 1024), name=n("ffn_up"))
    a, y = _ffn_act(u, w["conv_w"], w["conv_b"], n("ffn_act"))
    fo = _mm_nn(a, w["w_down"], F32, 1024, 1024, 2048, n("ffn_down"))
    saved = dict(x=x, h1=h1, proj=proj, gates=gates, o_a=o_a, lse_a=lse_a, o_b=o_b, lse_b=lse_b, o_c=o_c, tot_c=tot_c,
                 merged=merged, mo=mo, x2=x2, h2=h2, u=u, y=y, a=a, fo=fo)
    return saved


def _layer_bwd(dx3, sv, w, bias, lname, tok=None, on_part=None, d_fo=None, below=None):
    n = lambda s: f"{lname}_{s}"
    g = {}

    def part(group, vec):
        t = on_part(group, g) if on_part is not None else None
        return vec if t is None else vec + t

    if d_fo is None:
        gain = w["ffn_post_norm"] if tok is None else w["ffn_post_norm"] + tok
        d_fo, g["ffn_post_norm"] = _norm_bwd(sv["fo"], gain, [dx3], None, BF16, n("ffn_post_bwd"))
    else:
        d_fo, g["ffn_post_norm"] = d_fo
    d_a = _mm_nt(d_fo, w["w_down"], BF16, S, 1024, 1024, n("ffn_down_bwd_x"))
    g["w_down"] = _mm_tn(sv["a"], d_fo, BF16, 1024, 1024, S, n("ffn_down_bwd_w"))
    d_u, dcw, dcb = _ffn_act_bwd(sv["u"], sv["y"], d_a, w["conv_w"], n("ffn_act_bwd"))
    g["conv_w"] = dcw.reshape(3, 2 * D_FF)
    g["conv_b"] = dcb.reshape(1, 2 * D_FF)
    g["w_up"] = _mm(sv["h2"], d_u, grid=(1, 2 * D_FF // 1024, 1),
                    a_spec=pl.BlockSpec((S, D), lambda i, j, k: (k, 0)),
                    b_spec=pl.BlockSpec((None, S, 1024), lambda i, j, k: (j // 4, k, j % 4)),
                    out_shape=jax.ShapeDtypeStruct((D, 2 * D_FF), BF16),
                    out_spec=pl.BlockSpec((D, 1024), lambda i, j, k: (0, j)),
                    ca=0, cb=0, acc_shape=(D, 1024), name=n("ffn_up_bwd_w"))
    tok_ffn = on_part("ffn", g) if on_part is not None else None
    d_h2 = _mm(d_u, w["w_up"], grid=(S // 1024, 1, 2),
               a_spec=pl.BlockSpec((None, 1024, D_FF), lambda i, j, k: (k, i, 0)),
               b_spec=pl.BlockSpec((D, D_FF), lambda i, j, k: (0, k)),
               out_shape=jax.ShapeDtypeStruct((S, D), F32),
               out_spec=pl.BlockSpec((1024, D), lambda i, j, k: (i, 0)),
               ca=1, cb=1, acc_shape=(1024, D), after=tok_ffn, name=n("ffn_up_bwd_x"))
    dx2, d_mo, g["ffn_pre_norm"], g["attn_post_norm"] = _norm_bwd_chain(
        sv["x2"], w["ffn_pre_norm"], [d_h2], dx3, sv["mo"], w["attn_post_norm"], n("ffn_pre_attn_post_bwd"))
    g["w_out"] = _mm_tn(sv["merged"], d_mo, BF16, 1024, 1024, S, n("out_bwd_w"))
    do_a, do_b, do_c, d_gates, dwa, dwb, dwc, g["b_gate"] = _merge_bwd(
        d_mo, sv["o_a"], sv["o_b"], sv["o_c"], sv["gates"], w["b_gate"], w["w_br_a"], w["w_br_b"], w["w_br_c"],
        w["w_out"], n("merge_bwd"))
    g["w_br_a"], g["w_br_b"], g["w_br_c"] = dwa, dwb, dwc
    sinks = part("mix", w["sinks"])
    proj = sv["proj"]
    dqkv = lax.empty((QKV_SLABS, S, LANES), BF16)
    gbias = []
    for gi, (_, d) in enumerate(A_GROUPS):
        dqkv, gg, _ = _band_bwd(proj, bias, sv["o_a"], do_a, sv["lse_a"], sinks, dqkv, d=d, q0=2 * gi, k0=6 + 2 * gi,
                                v0=12 + 2 * gi, npairs=2, bias0=2 * gi, shared_kv=False, name=n(f"attn_a{gi}_bwd"))
        gbias.append(gg)
    dqkv, ggb, dsink = _band_bwd(proj, bias, sv["o_b"], do_b, sv["lse_b"], sinks, dqkv, d=1, q0=18, k0=22, v0=23,
                                 npairs=4, bias0=6, shared_kv=True, name=n("attn_b_bwd"))
    gbias.append(ggb)
    g["bias_g"] = jnp.concatenate(gbias, axis=0).reshape(N_BIAS_HEADS, BLK, 2 * BLK)
    g["sinks"] = dsink[:, 0, :2].reshape(1, 8)
    dqkv = _stick_bwd(proj, do_c, sv["tot_c"], dqkv, q0=24, k0=26, v0=28, name=n("attn_c_bwd"))
    ts = 6
    tsx = QKV_SLABS
    dw_in = _mm(dqkv, sv["h1"], grid=(QKV_SLABS // ts, 1, 1),
                a_spec=pl.BlockSpec((ts, S, LANES), lambda i, j, k: (i, k, 0)),
                b_spec=pl.BlockSpec((S, D), lambda i, j, k: (k, 0)),
                out_shape=jax.ShapeDtypeStruct((IN_COLS, D), BF16),
                out_spec=pl.BlockSpec((ts * LANES, D), lambda i, j, k: (i, 0)),
                ca=0, cb=0, acc_shape=(ts * LANES, D), a_slab=True, name=n("in_bwd_w_qkv"))
    g["w_in"] = _mm(d_gates, sv["h1"], grid=(GATE_COLS // 768, 1, 1),
                    a_spec=pl.BlockSpec((S, 768), lambda i, j, k: (k, i)),
                    b_spec=pl.BlockSpec((S, D), lambda i, j, k: (k, 0)),
                    out_shape=jax.ShapeDtypeStruct((IN_COLS, D), BF16),
                    out_spec=pl.BlockSpec((768, D), lambda i, j, k: (i + QKV_COLS // 768, 0)),
                    ca=0, cb=0, acc_shape=(768, D), alias_out=dw_in, name=n("in_bwd_w_gate"))
    tok_in = on_part("in", g) if on_part is not None else None
    d_h1a = _mm(dqkv, w["w_in"], grid=(S // 1024, 1, QKV_SLABS // tsx),
                a_spec=pl.BlockSpec((tsx, 1024, LANES), lambda i, j, k: (k, i, 0)),
                b_spec=pl.BlockSpec((tsx * LANES, D), lambda i, j, k: (k, 0)),
                out_shape=jax.ShapeDtypeStruct((S, D), F32),
                out_spec=pl.BlockSpec((1024, D), lambda i, j, k: (i, 0)),
                ca=1, cb=0, acc_shape=(1024, D), a_slab=True, after=tok_in, name=n("in_bwd_x_qkv"))
    d_h1b = _mm(d_gates, w["w_in"], grid=(S // 1024, 1, GATE_COLS // 768),
                a_spec=pl.BlockSpec((1024, 768), lambda i, j, k: (i, k)),
                b_spec=pl.BlockSpec((768, D), lambda i, j, k: (k + QKV_COLS // 768, 0)),
                out_shape=jax.ShapeDtypeStruct((S, D), F32),
                out_spec=pl.BlockSpec((1024, D), lambda i, j, k: (i, 0)),
                ca=1, cb=0, acc_shape=(1024, D), after=tok_in, name=n("in_bwd_x_gate"))
    if below is None:
        dx, g["attn_pre_norm"] = _norm_bwd(sv["x"], w["attn_pre_norm"], [d_h1a, d_h1b], dx2, F32, n("attn_pre_bwd"))
        return dx, g, tok_in, None
    dx, d_fo_below, g["attn_pre_norm"], dg_below = _norm_bwd_chain(
        sv["x"], w["attn_pre_norm"], [d_h1a, d_h1b], dx2, below[0], below[1], n("attn_pre_ffn_post_bwd"))
    return dx, g, tok_in, (d_fo_below, dg_below)


def _local_step(x, target, ws, rel_bias, tok=None, on_grads=None):
    buckets = jnp.asarray(_bucket_tiles())
    bias = _bias_tiles(rel_bias, buckets, "bias_tiles").reshape(N_BIAS_HEADS // 2, 2, 2, BLK, 2 * BLK)
    saved = []
    gain0 = ws[0]["attn_pre_norm"] if tok is None else ws[0]["attn_pre_norm"] + tok
    h1 = _prenorm(x, gain0, "l0_attn_pre")
    for l in range(DEPTH):
        sv = _layer_fwd(x, h1, ws[l], bias, f"l{l}")
        saved.append(sv)
        if l + 1 < DEPTH:
            x, h1 = _postnorm_res(sv["x2"], sv["fo"], ws[l]["ffn_post_norm"], ws[l + 1]["attn_pre_norm"], f"l{l}_ffn_post")
    top = saved[-1]
    dy, loss_tile, d_fo_top, dg_top = _loss_head(top["x2"], top["fo"], ws[-1]["ffn_post_norm"], target, "loss_head")
    grads = [None] * DEPTH
    tok, d_fo = None, (d_fo_top, dg_top)
    for l in reversed(range(DEPTH)):
        on_part = None if on_grads is None else functools.partial(on_grads, l)
        below = (saved[l - 1]["fo"], ws[l - 1]["ffn_post_norm"]) if l > 0 else None
        dy, grads[l], tok, d_fo = _layer_bwd(dy, saved[l], ws[l], bias, f"l{l}", tok, on_part, d_fo, below)
    g_rel = _bias_grad([grads[l]["bias_g"] for l in range(DEPTH)], buckets, "bias_grad")[:, :N_BIAS_HEADS]
    return loss_tile, dy, grads, g_rel


def _coords():
    return lax.axis_index("x"), lax.axis_index("y"), lax.axis_index("c")


def _peer(rel):
    x, y, c = _coords()
    return (1 - x if rel & 4 else x, 1 - y if rel & 2 else y, 1 - c if rel & 1 else c)


def _exchange(srcs, dst_shapes, src_win, dst_win, name, after=None):
    nt = len(srcs)
    extra = [] if after is None else [after]

    def body(*refs):
        src_refs, dst_refs = refs[:nt], refs[nt + len(extra):2 * nt + len(extra)]
        send_sems, recv_sems, local_sems = refs[2 * nt + len(extra):]
        x, y, c = _coords()
        me = 4 * x + 2 * y + c
        locals_ = []
        for t in range(nt):
            cp = pltpu.make_async_copy(src_win(t, src_refs[t], me), dst_win(t, dst_refs[t], me), local_sems.at[t])
            cp.start()
            locals_.append(cp)
        sends = []
        for rel in range(1, NDEV):
            px, py, pc = _peer(rel)
            q = 4 * px + 2 * py + pc
            for t in range(nt):
                cp = pltpu.make_async_remote_copy(
                    src_ref=src_win(t, src_refs[t], q), dst_ref=dst_win(t, dst_refs[t], me),
                    send_sem=send_sems.at[rel - 1, t], recv_sem=recv_sems.at[rel - 1, t],
                    device_id=(px, py, pc), device_id_type=MESH)
                cp.start()
                sends.append(cp)
        for rel in range(1, NDEV):
            px, py, pc = _peer(rel)
            q = 4 * px + 2 * py + pc
            for t in range(nt):
                pltpu.make_async_remote_copy(
                    src_ref=src_win(t, src_refs[t], me), dst_ref=dst_win(t, dst_refs[t], q),
                    send_sem=send_sems.at[rel - 1, t], recv_sem=recv_sems.at[rel - 1, t],
                    device_id=(px, py, pc), device_id_type=MESH).wait_recv()
        for cp in sends:
            cp.wait_send()
        for cp in locals_:
            cp.wait()

    return pl.pallas_call(
        body, in_specs=[ANY] * (nt + len(extra)), out_specs=[ANY] * nt, out_shape=dst_shapes,
        scratch_shapes=[pltpu.SemaphoreType.DMA((NDEV - 1, nt)), pltpu.SemaphoreType.DMA((NDEV - 1, nt)),
                        pltpu.SemaphoreType.DMA((nt,))],
        name=name)(*srcs, *extra)


BIG = (("w_in", 0, 864), ("w_br_a", 1, 128), ("w_br_b", 1, 128), ("w_br_c", 1, 128), ("w_out", 0, 128),
       ("w_up", 1, 1024), ("w_down", 0, 512))


NBIG = len(BIG)
BIG_FULL = {"w_in": (IN_COLS, D), "w_br_a": (256, D), "w_br_b": (512, D), "w_br_c": (256, D), "w_out": (D, D),
            "w_up": (D, 2 * D_FF), "w_down": (D_FF, D)}
SHARD_ROWS = {"w_in": 288, "w_up": 256, "w_down": 256}
LAYER_GROUPS = (("in", (0,)), ("mix", (1, 2, 3, 4)), ("ffn", (5, 6)))

HBM_SPEC = pl.BlockSpec(memory_space=pltpu.HBM)
SEM_SPEC = pl.BlockSpec(memory_space=pltpu.SEMAPHORE)


def _hbm(a):
    return pltpu.with_memory_space_constraint(a, pltpu.HBM)


def _shard_window(t, ref, k):
    nm, ax, ext = BIG[t % NBIG]
    off = pl.multiple_of(k * ext, ext)
    if ax == 0:
        return ref.at[pl.ds(off, ext), :]
    return ref.at[:, pl.ds(off, ext)]


def _whole(t, ref, k):
    return ref


def _slot(t, ref, k):
    return ref.at[k]


def _own_block_spec(t, rows, me_of):
    nm, ax, ext = BIG[t % NBIG]
    r, c = BIG_FULL[nm]
    if ax == 0:
        return pl.BlockSpec((rows, c), lambda i, m: (me_of(m) * (ext // rows) + i, 0))
    return pl.BlockSpec((rows, ext), lambda i, m: (i, me_of(m)))


def _cast_own(t, shards, me_arr, name):
    nm, ax, ext = BIG[t % NBIG]
    layer = t // NBIG
    _, nr, nc = shards.shape
    rows = SHARD_ROWS.get(nm, nr)
    shape = BIG_FULL[nm]

    def body(m_ref, s_ref, o_ref):
        o_ref[...] = s_ref[...].astype(BF16)

    return pl.pallas_call(
        body, grid_spec=pltpu.PrefetchScalarGridSpec(
            num_scalar_prefetch=1, grid=(nr // rows,),
            in_specs=[pl.BlockSpec((None, rows, nc), lambda i, m: (layer, i, 0))],
            out_specs=_own_block_spec(t, rows, lambda m: m[0])),
        out_shape=jax.ShapeDtypeStruct(shape, BF16), compiler_params=_cp("arbitrary"), name=name)(me_arr, shards)


ALL_RELS = tuple(range(1, NDEV))
NEAR_RELS = (1, 2, 4, 6)
FAR_RELS = (2, 4, 6)


def _xchg_start(srcs, lands, groups, src_win, dst_win, after, name, rels=ALL_RELS, tids=None):
    ns = 0 if srcs is None else len(srcs)
    nt, ng = len(lands), len(groups)
    ins = ([] if srcs is None else list(srcs)) + list(lands)

    def body(*refs):
        src_refs, land_refs = refs[:ns], refs[ns:ns + nt]
        sems = refs[ns + nt + 1:ns + nt + 1 + 2 * ng]
        token = refs[-1]
        x, y, c = _coords()
        me = 4 * x + 2 * y + c
        for gi, grp in enumerate(groups):
            for j, t in enumerate(grp):
                tid = t if tids is None else tids[t]
                for ri, rel in enumerate(rels):
                    px, py, pc = _peer(rel)
                    q = 4 * px + 2 * py + pc
                    src = dst_win(tid, land_refs[t], me) if srcs is None else src_win(tid, src_refs[t], q)
                    pltpu.make_async_remote_copy(
                        src_ref=src, dst_ref=dst_win(tid, land_refs[t], me),
                        send_sem=sems[2 * gi].at[ri * len(grp) + j],
                        recv_sem=sems[2 * gi + 1].at[ri * len(grp) + j],
                        device_id=(px, py, pc), device_id_type=MESH).start()
        token[...] = jnp.zeros((8, LANES), F32)

    out_shape = []
    for grp in groups:
        out_shape += [pltpu.SemaphoreType.DMA((len(rels) * len(grp),))] * 2
    out_shape += [pltpu.HBM(a.shape, a.dtype) for a in ins]
    out_shape.append(jax.ShapeDtypeStruct((8, LANES), F32))
    outs = pl.pallas_call(
        body, in_specs=[HBM_SPEC] * len(ins) + [ANY],
        out_specs=[SEM_SPEC] * (2 * ng) + [HBM_SPEC] * len(ins) + [pl.BlockSpec(memory_space=pltpu.VMEM)],
        out_shape=out_shape, input_output_aliases={i: 2 * ng + i for i in range(len(ins))},
        compiler_params=pltpu.CompilerParams(has_side_effects=pltpu.SideEffectType.DATAFLOW_SIDE_EFFECTING),
        name=name)(*[_hbm(a) for a in ins], after)
    sems = [(outs[2 * gi], outs[2 * gi + 1]) for gi in range(ng)]
    thru = list(outs[2 * ng:2 * ng + len(ins)])
    return sems, (None if srcs is None else thru[:ns]), thru[ns:], outs[-1]


def _xchg_wait(sems, srcs, lands, tids, after, src_win, dst_win, name, rels=ALL_RELS):
    ns = 0 if srcs is None else len(srcs)
    n = len(lands)
    send_sem, recv_sem = sems
    ins = ([] if srcs is None else list(srcs)) + list(lands)

    def body(*refs):
        src_refs, land_refs = refs[:ns], refs[ns:ns + n]
        ssem, rsem = refs[ns + n], refs[ns + n + 1]
        x, y, c = _coords()
        me = 4 * x + 2 * y + c
        for j, t in enumerate(tids):
            for ri, rel in enumerate(rels):
                px, py, pc = _peer(rel)
                q = 4 * px + 2 * py + pc
                src = dst_win(t, land_refs[j], me) if srcs is None else src_win(t, src_refs[j], q)
                cp = pltpu.make_async_remote_copy(
                    src_ref=src, dst_ref=dst_win(t, land_refs[j], q),
                    send_sem=ssem.at[ri * n + j], recv_sem=rsem.at[ri * n + j],
                    device_id=(px, py, pc), device_id_type=MESH)
                cp.wait_send()
                cp.wait_recv()

    outs = pl.pallas_call(
        body, in_specs=[HBM_SPEC] * len(ins) + [SEM_SPEC, SEM_SPEC, ANY], out_specs=[HBM_SPEC] * len(ins),
        out_shape=[pltpu.HBM(a.shape, a.dtype) for a in ins],
        input_output_aliases={i: i for i in range(len(ins))},
        compiler_params=pltpu.CompilerParams(has_side_effects=pltpu.SideEffectType.DATAFLOW_SIDE_EFFECTING),
        name=name)(*ins, send_sem, recv_sem, after)
    return (None if srcs is None else list(outs[:ns])), list(outs[ns:])


def _gather_forward(sems_in, lands, groups, tids, after, dst_win, name):
    nt, ng = len(lands), len(groups)

    def body(*refs):
        land_refs = refs[:nt]
        in_sems = refs[nt:nt + 2 * ng]
        out_sems = refs[nt + 2 * ng + 1:nt + 4 * ng + 1]
        token = refs[-1]
        x, y, c = _coords()
        me = 4 * x + 2 * y + c
        sib = (x, y, 1 - c)
        for gi, grp in enumerate(groups):
            n = len(grp)
            for j, pos in enumerate(grp):
                t = tids[pos]
                for ri, rel in enumerate(NEAR_RELS):
                    px, py, pc = _peer(rel)
                    q = 4 * px + 2 * py + pc
                    cp = pltpu.make_async_remote_copy(
                        src_ref=dst_win(t, land_refs[pos], me), dst_ref=dst_win(t, land_refs[pos], q),
                        send_sem=in_sems[2 * gi].at[ri * n + j], recv_sem=in_sems[2 * gi + 1].at[ri * n + j],
                        device_id=(px, py, pc), device_id_type=MESH)
                    cp.wait_send()
                    cp.wait_recv()
            for j, pos in enumerate(grp):
                t = tids[pos]
                for fi, rel in enumerate(FAR_RELS):
                    px, py, pc = _peer(rel)
                    q = 4 * px + 2 * py + pc
                    win = dst_win(t, land_refs[pos], q)
                    pltpu.make_async_remote_copy(
                        src_ref=win, dst_ref=win,
                        send_sem=out_sems[2 * gi].at[fi * n + j], recv_sem=out_sems[2 * gi + 1].at[fi * n + j],
                        device_id=sib, device_id_type=MESH).start()
        token[...] = jnp.zeros((8, LANES), F32)

    out_shape = []
    for grp in groups:
        out_shape += [pltpu.SemaphoreType.DMA((len(FAR_RELS) * len(grp),))] * 2
    out_shape += [pltpu.HBM(a.shape, a.dtype) for a in lands]
    out_shape.append(jax.ShapeDtypeStruct((8, LANES), F32))
    flat_sems = [s for pair in sems_in for s in pair]
    outs = pl.pallas_call(
        body, in_specs=[HBM_SPEC] * nt + [SEM_SPEC] * (2 * ng) + [ANY],
        out_specs=[SEM_SPEC] * (2 * ng) + [HBM_SPEC] * nt + [pl.BlockSpec(memory_space=pltpu.VMEM)],
        out_shape=out_shape, input_output_aliases={i: 2 * ng + i for i in range(nt)},
        compiler_params=pltpu.CompilerParams(has_side_effects=pltpu.SideEffectType.DATAFLOW_SIDE_EFFECTING),
        name=name)(*[_hbm(a) for a in lands], *flat_sems, after)
    sems = [(outs[2 * gi], outs[2 * gi + 1]) for gi in range(ng)]
    return sems, list(outs[2 * ng:2 * ng + nt]), outs[-1]


class _Weights:
    def __init__(self, ready, pending=None):
        self.ready = dict(ready)
        self.pending = dict(pending or {})

    def __getitem__(self, k):
        return self.ready[k]

    def need(self, group, after):
        fn = self.pending.pop(group, None)
        if fn is not None:
            self.ready.update(fn(after))


def _adamw_math(w, g, m, v):
    m2 = ADAM_B1 * m + (1.0 - ADAM_B1) * g
    v2 = ADAM_B2 * v + (1.0 - ADAM_B2) * (g * g)
    m_hat = m2 / (1.0 - ADAM_B1 ** ADAM_STEP)
    v_hat = v2 / (1.0 - ADAM_B2 ** ADAM_STEP)
    delta = -ADAM_LR * (m_hat / (jnp.sqrt(v_hat) + ADAM_EPS) + ADAM_WD * w)
    return delta, m2, v2


def _adamw(t, parts, own, me_arr, w, m, v, layer, prev, rows, name):
    nl, nr, nc = w.shape

    def body(me_ref, p_ref, own_ref, w_ref, m_ref, v_ref, *rest):
        g_ref, d_ref, m2_ref, v2_ref = rest[-4:]
        me = me_ref[0]
        g = None
        for k in range(NDEV):
            term = jnp.where(me == k, own_ref[...], p_ref[k]).astype(F32)
            g = term if g is None else g + term
        delta, m2, v2 = _adamw_math(w_ref[...], g, m_ref[...], v_ref[...])
        g_ref[...] = g
        d_ref[...] = delta
        m2_ref[...] = m2
        v2_ref[...] = v2

    blk = pl.BlockSpec((None, rows, nc), lambda i, mm: (layer, i, 0))
    pblk = pl.BlockSpec((NDEV, rows, nc), lambda i, mm: (0, i, 0))
    extra = [] if prev is None else list(prev)
    return pl.pallas_call(
        body, grid_spec=pltpu.PrefetchScalarGridSpec(
            num_scalar_prefetch=1, grid=(nr // rows,),
            in_specs=[pblk, _own_block_spec(t, rows, lambda mm: mm[0]), blk, blk, blk] + [ANY] * len(extra),
            out_specs=[blk] * 4),
        out_shape=[jax.ShapeDtypeStruct(w.shape, F32)] * 4,
        input_output_aliases={6 + k: k for k in range(len(extra))},
        compiler_params=_cp("arbitrary"), name=name)(me_arr, parts, own, w, m, v, *extra)


def _pack(vecs):
    flat = jnp.concatenate([v.reshape(-1).astype(F32) for v in vecs])
    n = flat.shape[0]
    rows = -(-n // (8 * LANES)) * 8
    return jnp.pad(flat, (0, rows * LANES - n)).reshape(rows, LANES)


ROWPACK = (("rel_bias", 32, 32, (NUM_BUCKETS, N_BIAS_HEADS)), ("sinks", 8, 8, (DEPTH, 8)),
           ("attn_pre_norm", 16, 16, (DEPTH, D)), ("attn_post_norm", 16, 16, (DEPTH, D)),
           ("ffn_pre_norm", 16, 16, (DEPTH, D)), ("ffn_post_norm", 16, 16, (DEPTH, D)),
           ("conv_b", 128, 128, (DEPTH, 2 * D_FF)), ("b_gate", 48, 8, (DEPTH, 3, 128)),
           ("conv_w", 384, 48, (DEPTH, 3, 1024)))
ROWS_OWN = sum(r for _, _, r, _ in ROWPACK)
N_REPL = 7
ROWS_REPL = sum(r for _, _, r, _ in ROWPACK[:N_REPL])
ROWS_SHARD = ROWS_OWN - ROWS_REPL


def _as_rows(a, rows):
    a = a.astype(F32)
    if a.shape[-1] < LANES:
        a = jnp.pad(a.reshape(-1, a.shape[-1]), ((0, 0), (0, LANES - a.shape[-1])))
    a = a.reshape(-1, LANES)
    return jnp.pad(a, ((0, rows - a.shape[0]), (0, 0)))


def _rowpack(arrs, entries=ROWPACK):
    return jnp.concatenate([_as_rows(arrs[nm], ro) for nm, _, ro, _ in entries], axis=0)


def _shard_rows(g):
    bg = jnp.transpose(g["b_gate"].astype(F32).reshape(DEPTH * 3, NDEV, LANES), (1, 0, 2))
    bg = jnp.pad(bg, ((0, 0), (0, 8 - DEPTH * 3), (0, 0)))
    cw = jnp.transpose(g["conv_w"].astype(F32).reshape(DEPTH * 3, NDEV, 8, LANES), (1, 0, 2, 3))
    return jnp.concatenate([bg, cw.reshape(NDEV, DEPTH * 3 * 8, LANES)], axis=1)


def _small_update(parts_repl, parts_shard, w, m, v, name):
    nsm = len(ROWPACK)

    def body(pr_ref, ps_ref, w_ref, m_ref, v_ref, *rest):
        outs = rest[:4 * nsm]
        loss_ref = rest[4 * nsm]
        g_s, d_s, m_s, v_s = rest[4 * nsm + 1:]
        gr, gs = pr_ref[0], ps_ref[0]
        for k in range(1, NDEV):
            gr = gr + pr_ref[k]
            gs = gs + ps_ref[k]
        g_s[0:ROWS_REPL, :] = gr[:ROWS_REPL]
        g_s[ROWS_REPL:ROWS_OWN, :] = gs
        loss_ref[...] = gr[ROWS_REPL:]
        delta, m2, v2 = _adamw_math(w_ref[...], g_s[...], m_ref[...], v_ref[...])
        d_s[...] = delta
        m_s[...] = m2
        v_s[...] = v2
        for kind, src in enumerate((g_s, d_s, m_s, v_s)):
            oo = 0
            for idx, (nm, rf, ro, shp) in enumerate(ROWPACK):
                o_ref = outs[kind * nsm + idx]
                if nm in ("rel_bias", "sinks"):
                    o_ref[...] = src[oo:oo + shp[0], 0:shp[1]]
                elif nm == "b_gate":
                    for l in range(DEPTH):
                        o_ref[l] = src[oo + 3 * l:oo + 3 * l + 3, :]
                elif nm == "conv_w":
                    for l in range(DEPTH):
                        for k in range(8):
                            o_ref[l, :, k * LANES:(k + 1) * LANES] = src[pl.ds(oo + 24 * l + k, 3, stride=8), :]
                else:
                    per = shp[1] // LANES
                    for k in range(per):
                        o_ref[:, k * LANES:(k + 1) * LANES] = src[pl.ds(oo + k, DEPTH, stride=per), :]
                oo += ro

    vm = pl.BlockSpec(memory_space=pltpu.VMEM)
    shapes = [jax.ShapeDtypeStruct(shp, F32) for _ in range(4) for _, _, _, shp in ROWPACK]
    shapes.append(jax.ShapeDtypeStruct((8, LANES), F32))
    outs = pl.pallas_call(
        body, in_specs=[vm] * 5, out_specs=[vm] * (4 * nsm + 1), out_shape=shapes,
        scratch_shapes=[pltpu.VMEM((ROWS_OWN, LANES), F32)] * 4,
        name=name)(parts_repl, parts_shard, w, m, v)
    names = [nm for nm, _, _, _ in ROWPACK]
    return [dict(zip(names, outs[kind * nsm:(kind + 1) * nsm])) for kind in range(4)] + [outs[-1]]


def kernel(x, rel_bias, attn_pre_norm, w_in, b_gate, sinks, w_br_a, w_br_b, w_br_c, w_out, attn_post_norm, ffn_pre_norm, w_up, conv_w, conv_b, w_down, ffn_post_norm, loss_target, m_rel_bias, m_attn_pre_norm, m_w_in, m_b_gate, m_sinks, m_w_br_a, m_w_br_b, m_w_br_c, m_w_out, m_attn_post_norm, m_ffn_pre_norm, m_w_up, m_conv_w, m_conv_b, m_w_down, m_ffn_post_norm, v_rel_bias, v_attn_pre_norm, v_w_in, v_b_gate, v_sinks, v_w_br_a, v_w_br_b, v_w_br_c, v_w_out, v_attn_post_norm, v_ffn_pre_norm, v_w_up, v_conv_w, v_conv_b, v_w_down, v_ffn_post_norm):
    P = dict(rel_bias=rel_bias, attn_pre_norm=attn_pre_norm, w_in=w_in, b_gate=b_gate, sinks=sinks, w_br_a=w_br_a,
             w_br_b=w_br_b, w_br_c=w_br_c, w_out=w_out, attn_post_norm=attn_post_norm, ffn_pre_norm=ffn_pre_norm,
             w_up=w_up, conv_w=conv_w, conv_b=conv_b, w_down=w_down, ffn_post_norm=ffn_post_norm)
    M = dict(rel_bias=m_rel_bias, attn_pre_norm=m_attn_pre_norm, w_in=m_w_in, b_gate=m_b_gate, sinks=m_sinks,
             w_br_a=m_w_br_a, w_br_b=m_w_br_b, w_br_c=m_w_br_c, w_out=m_w_out, attn_post_norm=m_attn_post_norm,
             ffn_pre_norm=m_ffn_pre_norm, w_up=m_w_up, conv_w=m_conv_w, conv_b=m_conv_b, w_down=m_w_down,
             ffn_post_norm=m_ffn_post_norm)
    V = dict(rel_bias=v_rel_bias, attn_pre_norm=v_attn_pre_norm, w_in=v_w_in, b_gate=v_b_gate, sinks=v_sinks,
             w_br_a=v_w_br_a, w_br_b=v_w_br_b, w_br_c=v_w_br_c, w_out=v_w_out, attn_post_norm=v_attn_post_norm,
             ffn_pre_norm=v_ffn_pre_norm, w_up=v_w_up, conv_w=v_conv_w, conv_b=v_conv_b, w_down=v_w_down,
             ffn_post_norm=v_ffn_post_norm)
    tr = lambda a: jnp.swapaxes(a, 1, 2)
    PB = {nm: (tr(P[nm]) if nm == "w_in" else P[nm]) for nm, _, _ in BIG}
    MB = {nm: (tr(M[nm]) if nm == "w_in" else M[nm]) for nm, _, _ in BIG}
    VB = {nm: (tr(V[nm]) if nm == "w_in" else V[nm]) for nm, _, _ in BIG}
    xi, yi, ci = _coords()
    me = 4 * xi + 2 * yi + ci

    me_arr = me.astype(jnp.int32).reshape(1)

    small_w = _pack([b_gate.reshape(-1), conv_w.reshape(-1)])
    (small_w_all,) = _exchange([small_w], [jax.ShapeDtypeStruct((NDEV,) + small_w.shape, F32)],
                               _whole, _slot, "gather_small_weights")

    groups = [tuple(l * NBIG + t for t in tids) for l in range(DEPTH) for _, tids in LAYER_GROUPS]
    cast = lambda i, m=me_arr: _cast_own(i, PB[BIG[i % NBIG][0]], m, f"gather_own_l{i // NBIG}_{BIG[i % NBIG][0]}")
    first = list(groups[0])
    rest = [i for grp in groups[1:] for i in grp]
    sems0, _, lands0, tok_first = _xchg_start(None, [cast(i) for i in first], [tuple(range(len(first)))], None,
                                              _shard_window, small_w_all, "gather_start_first", rels=NEAR_RELS, tids=first)
    where_rest = {tid: k for k, tid in enumerate(rest)}
    me_rest = me_arr + tok_first[0, 0:1].astype(jnp.int32)
    sems1, _, lands1, g_tok = _xchg_start(None, [cast(i, me_rest) for i in rest],
                                          [tuple(where_rest[i] for i in grp) for grp in groups[1:]], None,
                                          _shard_window, lands0[0], "gather_start_rest", rels=NEAR_RELS, tids=rest)
    g_sems = list(sems0) + list(sems1)
    tok0 = g_tok[0:1, 0:1]
    lands_now = [None] * (DEPTH * NBIG)
    for i, a in zip(first + rest, list(lands0) + list(lands1)):
        lands_now[i] = a
    fwd_sems = {}
    fwd_plan = {0: (0,), 1: (1,), 2: (2,), 3: (3, 4, 5)}

    def gather_waiter(gi, l, gname, tids):
        def wait(after):
            if gi in fwd_plan:
                gis = fwd_plan[gi]
                flat = [i for g2 in gis for i in groups[g2]]
                where = {tid: k for k, tid in enumerate(flat)}
                fs, new_lands, ftok = _gather_forward(
                    [g_sems[g2] for g2 in gis], [lands_now[i] for i in flat],
                    [[where[i] for i in groups[g2]] for g2 in gis], flat, after, _shard_window, f"gather_forward_{gi}")
                for g2, s in zip(gis, fs):
                    fwd_sems[g2] = s
                for i, a in zip(flat, new_lands):
                    lands_now[i] = a
                after = ftok
            ids = [l * NBIG + t for t in tids]
            _, got = _xchg_wait(fwd_sems[gi], None, [lands_now[i] for i in ids], ids, after,
                                None, _shard_window, f"gather_wait_l{l}_{gname}", rels=FAR_RELS)
            out = {}
            for t, arr in zip(tids, got):
                nm = BIG[t][0]
                out[nm] = arr
            return out
        return wait

    pending = [{gname: gather_waiter(l * len(LAYER_GROUPS) + k, l, gname, tids)
                for k, (gname, tids) in enumerate(LAYER_GROUPS)} for l in range(DEPTH)]
    nbg = DEPTH * 3 * 128
    ncw = DEPTH * 3 * 1024
    flat_all = small_w_all.reshape(NDEV, -1)
    b_gate_full = jnp.transpose(flat_all[:, :nbg].reshape(NDEV, DEPTH, 3, 128), (1, 2, 0, 3)).reshape(DEPTH, 3, D)
    conv_w_full = jnp.transpose(flat_all[:, nbg:nbg + ncw].reshape(NDEV, DEPTH, 3, 1024), (1, 2, 0, 3)).reshape(DEPTH, 3, 2 * D_FF)

    ws = []
    for l in range(DEPTH):
        ws.append(_Weights(dict(
            b_gate=b_gate_full[l], conv_w=conv_w_full[l].reshape(3, 2, D_FF), conv_b=conv_b[l].reshape(2, D_FF),
            sinks=sinks[l].reshape(1, 8),
            attn_pre_norm=attn_pre_norm[l].reshape(1, D), attn_post_norm=attn_post_norm[l].reshape(1, D),
            ffn_pre_norm=ffn_pre_norm[l].reshape(1, D), ffn_post_norm=ffn_post_norm[l].reshape(1, D)), pending[l]))

    rs = {}

    group_tids = dict(LAYER_GROUPS)

    def start_scatter(l, gname, grads_l):
        tids = group_tids[gname]
        blocks, lands_rs = [], []
        for t in tids:
            nm, ax, ext = BIG[t]
            gfull = grads_l[nm].astype(BF16)
            shp = (NDEV, ext, gfull.shape[1]) if ax == 0 else (NDEV, gfull.shape[0], ext)
            blocks.append(gfull)
            lands_rs.append(lax.empty(shp, BF16))
        local = list(range(len(tids)))
        win = lambda j, ref, k: _shard_window(tids[j], ref, k)
        sems, s_thru, l_thru, tok = _xchg_start(blocks, lands_rs, [tuple(local)], win, _slot, me_arr,
                                                f"scatter_start_l{l}_{gname}")
        rs[(l, gname)] = (sems[0], s_thru, l_thru, win, local)
        return tok[0:1, 0:1]

    loss_tile, grad_x, grads, g_rel = _local_step(x[0], loss_target[0], ws, rel_bias, tok0, start_scatter)

    stack = lambda nm: jnp.stack([grads[l][nm] for l in range(DEPTH)], axis=0)
    small_g = {nm: (g_rel if nm == "rel_bias" else stack(nm)) for nm, _, _, _ in ROWPACK}
    small_repl = jnp.concatenate([_rowpack(small_g, ROWPACK[:N_REPL]), loss_tile], axis=0)
    small_shard = _shard_rows(small_g)

    out_g, out_d, out_m, out_v = {}, {}, {}, {}
    prev = {nm: None for nm, _, _ in BIG}
    todo = [(l, gname) for l in reversed(range(DEPTH)) for gname in ("ffn", "mix", "in")]
    after, small_parts = grad_x, None
    for l, gname in todo:
        if (l, gname) == todo[-1]:
            small_parts = _exchange(
                [small_repl, small_shard],
                [jax.ShapeDtypeStruct((NDEV, ROWS_REPL + 8, LANES), F32), jax.ShapeDtypeStruct((NDEV, ROWS_SHARD, LANES), F32)],
                lambda t, ref, q: ref if t == 0 else ref.at[q], _slot, "exchange_small_grads", after=after)
            after = small_parts[0]
        sems, s_thru, l_thru, win, local = rs[(l, gname)]
        owns, parts = _xchg_wait(sems, s_thru, l_thru, local, after, win, _slot, f"scatter_wait_l{l}_{gname}")
        for t, own, prt in zip(group_tids[gname], owns, parts):
            nm = BIG[t][0]
            rows = SHARD_ROWS.get(nm, PB[nm].shape[1])
            prev[nm] = _adamw(t, prt, own, me_arr, PB[nm], MB[nm], VB[nm], l, prev[nm], rows, f"adamw_{nm}_l{l}")
            after = prev[nm][1]
    for nm, _, _ in BIG:
        out_g[nm], out_d[nm], out_m[nm], out_v[nm] = [tr(a) if nm == "w_in" else a for a in prev[nm]]
    sm_g, sm_d, sm_m, sm_v, loss_all = _small_update(small_parts[0], small_parts[1], _rowpack(P), _rowpack(M),
                                                     _rowpack(V), "small_update")
    loss = loss_all[0, 0]
    for dst, src in ((out_g, sm_g), (out_d, sm_d), (out_m, sm_m), (out_v, sm_v)):
        dst.update(src)

    order = ["rel_bias", "attn_pre_norm", "w_in", "b_gate", "sinks", "w_br_a", "w_br_b", "w_br_c", "w_out",
             "attn_post_norm", "ffn_pre_norm", "w_up", "conv_w", "conv_b", "w_down", "ffn_post_norm"]
    return (loss, grad_x[None], *[out_g[k] for k in order], *[out_d[k] for k in order],
            *[out_m[k] for k in order], *[out_v[k] for k in order])
```

```python
import functools
import math

import numpy as np
import jax
import jax.numpy as jnp
from jax import lax
from jax.experimental import pallas as pl
from jax.experimental.pallas import tpu as pltpu

F32 = jnp.float32
BF16 = jnp.bfloat16

S = 2048
D = 1024
DEPTH = 2
NDEV = 8
HD = 64
BLK = 128
NB = S // BLK
A_GROUPS = ((128, 1), (512, 4), (2048, 16))
NUM_BUCKETS = 32
MAX_DISTANCE = 2048
N_BIAS_HEADS = 20
D_FF = 4096
IN_COLS = 6912
QKV_COLS = 3840
QKV_SLABS = QKV_COLS // 128
GATE_COLS = 3072
EPS = 1e-6
SCALE = HD ** -0.5
NEG = -1e30
LANES = 128

ADAM_LR = 0.001
ADAM_B1 = 0.9
ADAM_B2 = 0.999
ADAM_EPS = 1e-08
ADAM_WD = 0.01
ADAM_STEP = 10

VMEM_LIMIT = 56 * 1024 * 1024
MESH = pl.DeviceIdType.MESH
ANY = pl.BlockSpec(memory_space=pl.ANY)
SMEM = pl.BlockSpec(memory_space=pltpu.SMEM)


def _cp(*sem):
    return pltpu.CompilerParams(dimension_semantics=sem if sem else None, vmem_limit_bytes=VMEM_LIMIT)


def _dot(a, b, ca, cb):
    return lax.dot_general(a, b, (((ca,), (cb,)), ((), ())), preferred_element_type=F32)


def _mm(a, b, *, grid, a_spec, b_spec, out_shape, out_spec, ca, cb, acc_shape, name,
        a_slab=False, b_slab=False, out_slab=False, alias_out=None, after=None):
    nk = grid[2]

    def body(*refs):
        a_ref, b_ref = refs[0], refs[1]
        o_ref, acc_ref = refs[-2], refs[-1]
        k = pl.program_id(2)

        def load(ref, slab):
            if slab:
                return jnp.concatenate([ref[s] for s in range(ref.shape[0])], axis=1).astype(BF16)
            return ref[...].astype(BF16)

        def write(val):
            if out_slab:
                for s in range(o_ref.shape[0]):
                    o_ref[s] = val[:, s * LANES:(s + 1) * LANES].astype(o_ref.dtype)
            else:
                o_ref[...] = val.astype(o_ref.dtype)

        d = _dot(load(a_ref, a_slab), load(b_ref, b_slab), ca, cb)
        if nk == 1:
            write(d)
        elif direct:
            @pl.when(k == 0)
            def _():
                o_ref[...] = d

            @pl.when(k > 0)
            def _():
                o_ref[...] += d
        else:
            @pl.when(k == 0)
            def _():
                acc_ref[...] = d

            if nk > 2:
                @pl.when((k > 0) & (k < nk - 1))
                def _():
                    acc_ref[...] += d

            @pl.when(k == nk - 1)
            def _():
                write(acc_ref[...] + d)

    direct = (not out_slab) and out_shape.dtype == F32
    if nk == 1 or direct:
        acc_shape = (8, LANES)
    in_specs = [a_spec, b_spec]
    args = [a, b]
    aliases = {}
    if alias_out is not None:
        in_specs.append(ANY)
        args.append(alias_out)
        aliases = {2: 0}
    if after is not None:
        in_specs.append(ANY)
        args.append(after)
    return pl.pallas_call(
        body, grid=grid, in_specs=in_specs, out_specs=out_spec, out_shape=out_shape,
        scratch_shapes=[pltpu.VMEM(acc_shape, F32)], input_output_aliases=aliases,
        compiler_params=_cp("parallel", "parallel", "arbitrary"), name=name)(*args)


def _mm_nn(a, b, out_dtype, tm, tn, tk, name):
    m, kk = a.shape
    n = b.shape[1]
    return _mm(a, b, grid=(m // tm, n // tn, kk // tk),
               a_spec=pl.BlockSpec((tm, tk), lambda i, j, k: (i, k)),
               b_spec=pl.BlockSpec((tk, tn), lambda i, j, k: (k, j)),
               out_shape=jax.ShapeDtypeStruct((m, n), out_dtype),
               out_spec=pl.BlockSpec((tm, tn), lambda i, j, k: (i, j)),
               ca=1, cb=0, acc_shape=(tm, tn), name=name)


def _mm_nt(a, b, out_dtype, tm, tn, tk, name):
    m, kk = a.shape
    n = b.shape[0]
    return _mm(a, b, grid=(m // tm, n // tn, kk // tk),
               a_spec=pl.BlockSpec((tm, tk), lambda i, j, k: (i, k)),
               b_spec=pl.BlockSpec((tn, tk), lambda i, j, k: (j, k)),
               out_shape=jax.ShapeDtypeStruct((m, n), out_dtype),
               out_spec=pl.BlockSpec((tm, tn), lambda i, j, k: (i, j)),
               ca=1, cb=1, acc_shape=(tm, tn), name=name)


def _mm_tn(a, b, out_dtype, tm, tn, tk, name):
    kk, m = a.shape
    n = b.shape[1]
    return _mm(a, b, grid=(m // tm, n // tn, kk // tk),
               a_spec=pl.BlockSpec((tk, tm), lambda i, j, k: (k, i)),
               b_spec=pl.BlockSpec((tk, tn), lambda i, j, k: (k, j)),
               out_shape=jax.ShapeDtypeStruct((m, n), out_dtype),
               out_spec=pl.BlockSpec((tm, tn), lambda i, j, k: (i, j)),
               ca=0, cb=0, acc_shape=(tm, tn), name=name)


ROW_TILE = 256


def _rms(x, g):
    r = lax.rsqrt(jnp.mean(x * x, axis=-1, keepdims=True) + EPS)
    return x * r * g


def _prenorm(x, g, name):
    def body(x_ref, g_ref, o_ref):
        o_ref[...] = _rms(x_ref[...], g_ref[...]).astype(BF16)

    return pl.pallas_call(
        body, grid=(S // ROW_TILE,),
        in_specs=[pl.BlockSpec((ROW_TILE, D), lambda i: (i, 0)), pl.BlockSpec((1, D), lambda i: (0, 0))],
        out_specs=pl.BlockSpec((ROW_TILE, D), lambda i: (i, 0)),
        out_shape=jax.ShapeDtypeStruct((S, D), BF16), compiler_params=_cp("parallel"), name=name)(x, g)


def _postnorm_res(x, f, g_post, g_next, name):
    def body(x_ref, f_ref, gp_ref, gn_ref, xo_ref, ho_ref):
        xn = x_ref[...] + _rms(f_ref[...], gp_ref[...])
        xo_ref[...] = xn
        ho_ref[...] = _rms(xn, gn_ref[...]).astype(BF16)

    row = pl.BlockSpec((ROW_TILE, D), lambda i: (i, 0))
    vec = pl.BlockSpec((1, D), lambda i: (0, 0))
    return pl.pallas_call(
        body, grid=(S // ROW_TILE,), in_specs=[row, row, vec, vec], out_specs=[row, row],
        out_shape=[jax.ShapeDtypeStruct((S, D), F32), jax.ShapeDtypeStruct((S, D), BF16)],
        compiler_params=_cp("parallel"), name=name)(x, f, g_post, g_next)


def _norm_bwd(f, g, dys, res, out_dtype, name):
    ndy = len(dys)
    has_res = res is not None

    def body(*refs):
        f_ref, g_ref = refs[0], refs[1]
        dy_refs = refs[2:2 + ndy]
        res_ref = refs[2 + ndy] if has_res else None
        o_ref, dg_ref = refs[-2], refs[-1]
        fv = f_ref[...]
        dy = dy_refs[0][...].astype(F32)
        for r in dy_refs[1:]:
            dy = dy + r[...].astype(F32)
        r = lax.rsqrt(jnp.mean(fv * fv, axis=-1, keepdims=True) + EPS)
        n = fv * r
        dn = dy * g_ref[...]
        df = r * (dn - n * jnp.mean(dn * n, axis=-1, keepdims=True))
        if has_res:
            df = df + res_ref[...]
        o_ref[...] = df.astype(out_dtype)

        @pl.when(pl.program_id(0) == 0)
        def _():
            dg_ref[...] = jnp.zeros((1, D), F32)

        dg_ref[...] += jnp.sum(dy * n, axis=0, keepdims=True)

    row = pl.BlockSpec((ROW_TILE, D), lambda i: (i, 0))
    vec = pl.BlockSpec((1, D), lambda i: (0, 0))
    in_specs = [row, vec] + [row] * ndy + ([row] if has_res else [])
    args = [f, g] + list(dys) + ([res] if has_res else [])
    return pl.pallas_call(
        body, grid=(S // ROW_TILE,), in_specs=in_specs, out_specs=[row, vec],
        out_shape=[jax.ShapeDtypeStruct((S, D), out_dtype), jax.ShapeDtypeStruct((1, D), F32)],
        compiler_params=_cp("arbitrary"), name=name)(*args)


def _rms_bwd_rows(fv, g, dy):
    r = lax.rsqrt(jnp.mean(fv * fv, axis=-1, keepdims=True) + EPS)
    n = fv * r
    dn = dy * g
    return r * (dn - n * jnp.mean(dn * n, axis=-1, keepdims=True)), dy * n


def _norm_bwd_chain(f1, g1, dys, res, f2, g2, name):
    ndy = len(dys)

    def body(*refs):
        f1_ref, g1_ref = refs[0], refs[1]
        dy_refs = refs[2:2 + ndy]
        res_ref, f2_ref, g2_ref = refs[2 + ndy:5 + ndy]
        o1_ref, o2_ref, dg1_ref, dg2_ref = refs[-4:]
        dy = dy_refs[0][...].astype(F32)
        for r in dy_refs[1:]:
            dy = dy + r[...].astype(F32)
        df1, c1 = _rms_bwd_rows(f1_ref[...], g1_ref[...], dy)
        out1 = df1 + res_ref[...]
        o1_ref[...] = out1
        df2, c2 = _rms_bwd_rows(f2_ref[...], g2_ref[...], out1)
        o2_ref[...] = df2.astype(BF16)

        @pl.when(pl.program_id(0) == 0)
        def _():
            dg1_ref[...] = jnp.zeros((1, D), F32)
            dg2_ref[...] = jnp.zeros((1, D), F32)

        dg1_ref[...] += jnp.sum(c1, axis=0, keepdims=True)
        dg2_ref[...] += jnp.sum(c2, axis=0, keepdims=True)

    row = pl.BlockSpec((ROW_TILE, D), lambda i: (i, 0))
    vec = pl.BlockSpec((1, D), lambda i: (0, 0))
    return pl.pallas_call(
        body, grid=(S // ROW_TILE,), in_specs=[row, vec] + [row] * ndy + [row, row, vec],
        out_specs=[row, row, vec, vec],
        out_shape=[jax.ShapeDtypeStruct((S, D), F32), jax.ShapeDtypeStruct((S, D), BF16),
                   jax.ShapeDtypeStruct((1, D), F32), jax.ShapeDtypeStruct((1, D), F32)],
        compiler_params=_cp("arbitrary"), name=name)(f1, g1, *dys, res, f2, g2)


def _loss_head(x, f, g, target, name):
    def body(x_ref, f_ref, g_ref, t_ref, dy_ref, l_ref, df_ref, dg_ref):
        fv = f_ref[...]
        e = x_ref[...] + _rms(fv, g_ref[...]) - t_ref[...]
        dy = e * (1.0 / D)
        dy_ref[...] = dy
        df, c = _rms_bwd_rows(fv, g_ref[...], dy)
        df_ref[...] = df.astype(BF16)

        @pl.when(pl.program_id(0) == 0)
        def _():
            l_ref[...] = jnp.zeros((8, LANES), F32)
            dg_ref[...] = jnp.zeros((1, D), F32)

        l_ref[...] += jnp.sum(e * e) * (0.5 / D)
        dg_ref[...] += jnp.sum(c, axis=0, keepdims=True)

    row = pl.BlockSpec((ROW_TILE, D), lambda i: (i, 0))
    vec = pl.BlockSpec((1, D), lambda i: (0, 0))
    return pl.pallas_call(
        body, grid=(S // ROW_TILE,), in_specs=[row, row, vec, row],
        out_specs=[row, pl.BlockSpec((8, LANES), lambda i: (0, 0)), row, vec],
        out_shape=[jax.ShapeDtypeStruct((S, D), F32), jax.ShapeDtypeStruct((8, LANES), F32),
                   jax.ShapeDtypeStruct((S, D), BF16), jax.ShapeDtypeStruct((1, D), F32)],
        compiler_params=_cp("arbitrary"), name=name)(x, f, g, target)


def _bucket_tiles():
    a = np.arange(BLK)[:, None]
    b = np.arange(2 * BLK)[None, :]
    dist = a + BLK - b
    out = np.zeros((4, 2, BLK, 2 * BLK), np.int32)
    cfg = [(w // d, d) for w, d in A_GROUPS] + [(BLK - 1, 1)]
    for gi, (max_dist, d) in enumerate(cfg):
        band = (dist >= 0) & (dist <= max_dist)
        tok = np.maximum(dist, 0) * d
        nf = np.maximum(tok, 1).astype(np.float32)
        max_exact = NUM_BUCKETS // 2
        large = max_exact + (np.log(nf / np.float32(max_exact)) / np.float32(math.log(MAX_DISTANCE / max_exact))
                             * np.float32(NUM_BUCKETS - max_exact)).astype(np.int32)
        large = np.minimum(large, NUM_BUCKETS - 1)
        bkt = np.where(tok < max_exact, tok, large).astype(np.int32)
        full = np.where(band, bkt, -1)
        out[gi, 1] = full
        out[gi, 0] = np.where(b >= BLK, full, -1)
    return out


def _bias_tiles(rel_bias, buckets, name):
    def body(tab_ref, bkt_ref, o_ref):
        h = pl.program_id(0)
        bkt = bkt_ref[...]
        acc = jnp.zeros(bkt.shape, F32)
        for bb in range(NUM_BUCKETS):
            acc = jnp.where(bkt == bb, tab_ref[bb, h], acc)
        o_ref[...] = jnp.where(bkt < 0, NEG, acc)

    return pl.pallas_call(
        body, grid=(N_BIAS_HEADS,),
        in_specs=[SMEM, pl.BlockSpec((None, 2, BLK, 2 * BLK), lambda h: (jnp.minimum(h // 4, 3), 0, 0, 0))],
        out_specs=pl.BlockSpec((None, 2, BLK, 2 * BLK), lambda h: (h, 0, 0, 0)),
        out_shape=jax.ShapeDtypeStruct((N_BIAS_HEADS, 2, BLK, 2 * BLK), F32),
        compiler_params=_cp("arbitrary"), name=name)(rel_bias, buckets)


def _bias_grad(gs, buckets, name):
    ng = len(gs)

    def body(*refs):
        g_refs = refs[:ng]
        bkt_ref, o_ref = refs[ng], refs[ng + 1]
        h = pl.program_id(0)
        g = g_refs[0][...]
        for r in g_refs[1:]:
            g = g + r[...]
        bkt = bkt_ref[...]
        row = lax.broadcasted_iota(jnp.int32, (NUM_BUCKETS, LANES), 0)
        lane = lax.broadcasted_iota(jnp.int32, (NUM_BUCKETS, LANES), 1)

        @pl.when(h == 0)
        def _():
            o_ref[...] = jnp.zeros((NUM_BUCKETS, LANES), F32)

        acc = o_ref[...]
        for bb in range(NUM_BUCKETS):
            s = jnp.sum(jnp.where(bkt == bb, g, 0.0))
            acc = jnp.where((row == bb) & (lane == h), s, acc)
        o_ref[...] = acc

    g_spec = pl.BlockSpec((None, BLK, 2 * BLK), lambda h: (h, 0, 0))
    return pl.pallas_call(
        body, grid=(N_BIAS_HEADS,),
        in_specs=[g_spec] * ng + [pl.BlockSpec((None, None, BLK, 2 * BLK), lambda h: (jnp.minimum(h // 4, 3), 1, 0, 0))],
        out_specs=pl.BlockSpec((NUM_BUCKETS, LANES), lambda h: (0, 0)),
        out_shape=jax.ShapeDtypeStruct((NUM_BUCKETS, LANES), F32),
        compiler_params=_cp("arbitrary"), name=name)(*gs, buckets)


def _to_class_major(src_ref, dst_refs, d, fn=None):
    ln = S // d
    for r in range(d):
        v = src_ref[pl.ds(r, ln, stride=d), :] if d > 1 else src_ref[...]
        outs = fn(v) if fn is not None else (v,) * len(dst_refs)
        for dst, o in zip(dst_refs, outs):
            dst[pl.ds(r * ln, ln), :] = o.astype(dst.dtype)


def _head_masks(rows):
    lane = lax.broadcasted_iota(jnp.int32, (rows, LANES), 1)
    return lane < HD, lane >= HD


def _split_heads(v):
    m0, m1 = _head_masks(v.shape[0])
    return jnp.where(m0, v, 0.0), jnp.where(m1, v, 0.0)


def _dup_head(v, hi):
    m0, _ = _head_masks(v.shape[0])
    r = pltpu.roll(v, HD, 1)
    return jnp.where(m0, jnp.where(hi, r, v), jnp.where(hi, v, r))


def _block_rows(b, d):
    nbc = NB // d
    i = b % nbc
    r = b // nbc
    has_prev = (i > 0).astype(jnp.int32)
    prev = pl.multiple_of(jnp.maximum(b - 1, 0) * BLK, BLK)
    nat = i * (BLK * d) + r
    return has_prev, prev, nat


def _lane_halves(v0, v1):
    lane = lax.broadcasted_iota(jnp.int32, (v0.shape[0], LANES), 1)
    return jnp.where(lane < HD, v0, v1)


def _band_fwd(proj, bias, sinks, *, d, q0, k0, v0, npairs, bias0, shared_kv, name):
    def body(sink_ref, q_ref, k_ref, v_ref, b_ref, num_ref, st_ref, qz0, qz1, ks, vs):
        p = pl.program_id(0)
        kv = (lambda v: (_dup_head(v, p >= 2),)) if shared_kv else None
        _to_class_major(q_ref, (qz0, qz1), d, lambda v: _split_heads(v * SCALE))
        _to_class_major(k_ref, (ks,), d, kv)
        _to_class_major(v_ref, (vs,), d, kv)
        lane = lax.broadcasted_iota(jnp.int32, (BLK, LANES), 1)

        def blk(b, carry):
            has_prev, prev, nat = _block_rows(b, d)
            cur = pl.multiple_of(b * BLK, BLK)
            k2 = jnp.concatenate([ks[pl.ds(prev, BLK), :], ks[pl.ds(cur, BLK), :]], axis=0)
            v2 = jnp.concatenate([vs[pl.ds(prev, BLK), :], vs[pl.ds(cur, BLK), :]], axis=0)
            nums, ms, ls = [], [], []
            for hh, qz in enumerate((qz0, qz1)):
                z = _dot(qz[pl.ds(cur, BLK), :], k2, 1, 1) + b_ref[hh, has_prev]
                m = jnp.max(z, axis=1, keepdims=True)
                e = jnp.exp(z - m)
                l = jnp.sum(e, axis=1, keepdims=True)
                num = _dot(e.astype(BF16), v2, 1, 0)
                if shared_kv:
                    sink = sink_ref[0, 2 * p + hh]
                    mx = jnp.maximum(m, sink)
                    c = jnp.exp(m - mx)
                    zden = l * c + jnp.exp(sink - mx)
                    num = num * (c / zden)
                    m = mx + jnp.log(zden)
                ls.append(l)
                ms.append(m)
                nums.append(num)
            num_t = jnp.where(lane < HD, nums[0], nums[1])
            if shared_kv:
                st_t = jnp.where(lane < HD, ms[0], ms[1])
            else:
                st_t = jnp.where(lane < 32, ms[0], jnp.where(lane < 64, ls[0], jnp.where(lane < 96, ms[1], ls[1])))
            if d > 1:
                num_ref[pl.ds(nat, BLK, stride=d), :] = num_t
                st_ref[pl.ds(nat, BLK, stride=d), :] = st_t
            else:
                num_ref[pl.ds(cur, BLK), :] = num_t
                st_ref[pl.ds(cur, BLK), :] = st_t
            return carry

        lax.fori_loop(0, NB, blk, 0, unroll=8)

    slab = lambda off, per_pair: pl.BlockSpec((None, S, LANES), (lambda p: (off + p, 0, 0)) if per_pair else (lambda p: (off, 0, 0)))
    out = pl.BlockSpec((None, S, LANES), lambda p: (p, 0, 0))
    return pl.pallas_call(
        body, grid=(npairs,),
        in_specs=[SMEM, slab(q0, True), slab(k0, not shared_kv), slab(v0, not shared_kv),
                  pl.BlockSpec((None, 2, 2, BLK, 2 * BLK), lambda p: (bias0 + p, 0, 0, 0, 0))],
        out_specs=[out, out],
        out_shape=[jax.ShapeDtypeStruct((npairs, S, LANES), F32)] * 2,
        scratch_shapes=[pltpu.VMEM((S, LANES), BF16)] * 4,
        compiler_params=_cp("arbitrary"), name=name)(sinks, proj, proj, proj, bias)


def _combine_a(nums, stats, name):
    rt = 512

    def body(n0, n1, n2, s0, s1, s2, o_ref, l_ref):
        n_refs, s_refs = (n0, n1, n2), (s0, s1, s2)
        outs, lses = [], []
        for hh in range(2):
            ms = [s[:, 64 * hh:64 * hh + 1] for s in s_refs]
            ls = [s[:, 64 * hh + 32:64 * hh + 33] for s in s_refs]
            mx = jnp.maximum(jnp.maximum(ms[0], ms[1]), ms[2])
            cs = [jnp.exp(m - mx) for m in ms]
            z = cs[0] * ls[0] + cs[1] * ls[1] + cs[2] * ls[2]
            acc = cs[0] * n_refs[0][:, hh * HD:(hh + 1) * HD]
            acc = acc + cs[1] * n_refs[1][:, hh * HD:(hh + 1) * HD]
            acc = acc + cs[2] * n_refs[2][:, hh * HD:(hh + 1) * HD]
            outs.append(acc / z)
            lses.append(mx + jnp.log(z))
        o_ref[...] = jnp.concatenate(outs, axis=1)
        l_ref[...] = _lane_halves(lses[0], lses[1])

    spec = pl.BlockSpec((None, rt, LANES), lambda p, i: (p, i, 0))
    return pl.pallas_call(
        body, grid=(2, S // rt), in_specs=[spec] * 6, out_specs=[spec, spec],
        out_shape=[jax.ShapeDtypeStruct((2, S, LANES), F32)] * 2,
        compiler_params=_cp("parallel", "parallel"), name=name)(*nums, *stats)


def _band_bwd(proj, bias, o, do, lse, sinks, dqkv, *, d, q0, k0, v0, npairs, bias0, shared_kv, name):
    def body(sink_ref, q_ref, k_ref, v_ref, b_ref, o_ref, do_ref, lse_ref, dqkv_in, dqkv_ref, g_ref, ds_ref,
             qz0, qz1, ks, vs, doz0, doz1, ls0, ls1, dls0, dls1, stage, dq_nat, dk_cm, dv_cm, kv_nat, dk_acc, dv_acc,
             obuf, osem):
        p = pl.program_id(0)
        dq_ref, dk_ref, dv_ref = obuf.at[0], obuf.at[1], obuf.at[2]
        m0, m1 = _head_masks(S)
        kk = lax.broadcasted_iota(jnp.int32, (2 * LANES, LANES), 0) % LANES
        ll = lax.broadcasted_iota(jnp.int32, (2 * LANES, LANES), 1)
        hi, lo = _split2(do_ref[...] * o_ref[...])
        dl = _dot(jnp.concatenate([hi, lo], axis=1), ((kk < HD) == (ll < HD)).astype(BF16), 1, 0)
        if shared_kv:
            row8 = lax.broadcasted_iota(jnp.int32, (8, LANES), 0)
            lane8 = lax.broadcasted_iota(jnp.int32, (8, LANES), 1)
            sinkv = jnp.where(m0, sink_ref[0, 2 * p], sink_ref[0, 2 * p + 1])
            contrib = jnp.exp(sinkv - lse_ref[...]) * dl
            t = jnp.zeros((8, LANES), F32)
            for hh, mh in enumerate((m0, m1)):
                dsink = -jnp.sum(jnp.where(mh, contrib, 0.0)) * (1.0 / HD)
                t = jnp.where((row8 == 0) & (lane8 == hh), dsink, t)
            ds_ref[...] = t
        else:
            ds_ref[...] = jnp.zeros((8, LANES), F32)
        kv = (lambda v: (_dup_head(v, p >= 2),)) if shared_kv else None
        _to_class_major(q_ref, (qz0, qz1), d, lambda v: _split_heads(v * SCALE))
        _to_class_major(k_ref, (ks,), d, kv)
        _to_class_major(v_ref, (vs,), d, kv)
        _to_class_major(do_ref, (doz0, doz1), d, _split_heads)
        def spread(v):
            a0, a1 = _head_masks(v.shape[0])
            r = pltpu.roll(v, HD, 1)
            return jnp.where(a0, v, r), jnp.where(a1, v, r)

        _to_class_major(lse_ref, (ls0, ls1), d, spread)
        stage[...] = dl
        _to_class_major(stage, (dls0, dls1), d, spread)

        dk_cm[...] = jnp.zeros((S, LANES), F32)
        dv_cm[...] = jnp.zeros((S, LANES), F32)
        g_ref[...] = jnp.zeros((2, BLK, 2 * BLK), F32)
        lane = lax.broadcasted_iota(jnp.int32, (BLK, LANES), 1)

        def blk(b, carry):
            has_prev, prev, nat = _block_rows(b, d)
            cur = pl.multiple_of(b * BLK, BLK)
            k2 = jnp.concatenate([ks[pl.ds(prev, BLK), :], ks[pl.ds(cur, BLK), :]], axis=0)
            v2 = jnp.concatenate([vs[pl.ds(prev, BLK), :], vs[pl.ds(cur, BLK), :]], axis=0)
            dqs, dks, dvs = [], [], []
            for hh, (qz, doz, lsr, dlr) in enumerate(((qz0, doz0, ls0, dls0), (qz1, doz1, ls1, dls1))):
                qb = qz[pl.ds(cur, BLK), :]
                dob = doz[pl.ds(cur, BLK), :]
                lb = lsr[pl.ds(cur, BLK), :]
                dlb = dlr[pl.ds(cur, BLK), :]
                z = _dot(qb, k2, 1, 1) + b_ref[hh, has_prev]
                pr = jnp.exp(z - jnp.concatenate([lb, lb], axis=1))
                dp = _dot(dob, v2, 1, 1)
                dz = pr * (dp - jnp.concatenate([dlb, dlb], axis=1))
                g_ref[hh] += dz
                dzb = dz.astype(BF16)
                dqs.append(_dot(dzb, k2, 1, 0))
                dks.append(_dot(dzb, qb, 0, 0))
                dvs.append(_dot(pr.astype(BF16), dob, 0, 0))
            dq_t = jnp.where(lane < HD, dqs[0], dqs[1]) * SCALE
            dk_t = dks[0] + dks[1]
            dv_t = dvs[0] + dvs[1]
            dk_cm[pl.ds(prev, BLK), :] += dk_t[:BLK]
            dk_cm[pl.ds(cur, BLK), :] += dk_t[BLK:]
            dv_cm[pl.ds(prev, BLK), :] += dv_t[:BLK]
            dv_cm[pl.ds(cur, BLK), :] += dv_t[BLK:]
            if d > 1:
                dq_nat[pl.ds(nat, BLK, stride=d), :] = dq_t
            else:
                dq_nat[pl.ds(cur, BLK), :] = dq_t
            return carry

        lax.fori_loop(0, NB, blk, 0, unroll=8)
        dq_ref[...] = dq_nat[...].astype(BF16)

        def from_class_major(src, dst_ref):
            if d == 1:
                dst_ref[...] = src[...].astype(BF16)
            else:
                ln = S // d
                for r in range(d):
                    kv_nat[pl.ds(r, ln, stride=d), :] = src[pl.ds(r * ln, ln), :]
                dst_ref[...] = kv_nat[...].astype(BF16)

        def put(i, slab):
            return pltpu.make_async_copy(obuf.at[i], dqkv_ref.at[slab], osem.at[i])

        put(0, q0 + p).start()
        if not shared_kv:
            from_class_major(dk_cm, dk_ref)
            from_class_major(dv_cm, dv_ref)
            put(1, k0 + p).start()
            put(2, v0 + p).start()
            put(1, k0 + p).wait()
            put(2, v0 + p).wait()
        else:
            @pl.when(p == 0)
            def _():
                dk_acc[...] = jnp.zeros((S, LANES), F32)
                dv_acc[...] = jnp.zeros((S, LANES), F32)

            mine = m1 == (p >= 2)
            for cm, acc in ((dk_cm, dk_acc), (dv_cm, dv_acc)):
                val = cm[...]
                acc[...] += jnp.where(mine, val + pltpu.roll(val, HD, 1), 0.0)

            @pl.when(p == npairs - 1)
            def _():
                from_class_major(dk_acc, dk_ref)
                from_class_major(dv_acc, dv_ref)
                put(1, k0).start()
                put(2, v0).start()
                put(1, k0).wait()
                put(2, v0).wait()

        put(0, q0 + p).wait()

    slab = lambda off, per_pair: pl.BlockSpec((None, S, LANES), (lambda p: (off + p, 0, 0)) if per_pair else (lambda p: (off, 0, 0)))
    pair = pl.BlockSpec((None, S, LANES), lambda p: (p, 0, 0))
    return pl.pallas_call(
        body, grid=(npairs,),
        in_specs=[SMEM, slab(q0, True), slab(k0, not shared_kv), slab(v0, not shared_kv),
                  pl.BlockSpec((None, 2, 2, BLK, 2 * BLK), lambda p: (bias0 + p, 0, 0, 0, 0)),
                  pair, pair, pair, ANY],
        out_specs=[ANY,
                   pl.BlockSpec((None, 2, BLK, 2 * BLK), lambda p: (p, 0, 0, 0)),
                   pl.BlockSpec((None, 8, LANES), lambda p: (p, 0, 0))],
        out_shape=[jax.ShapeDtypeStruct(dqkv.shape, BF16),
                   jax.ShapeDtypeStruct((npairs, 2, BLK, 2 * BLK), F32),
                   jax.ShapeDtypeStruct((npairs, 8, LANES), F32)],
        scratch_shapes=[pltpu.VMEM((S, LANES), BF16)] * 6 + [pltpu.VMEM((S, LANES), F32)] * 11
        + [pltpu.VMEM((3, S, LANES), BF16), pltpu.SemaphoreType.DMA((3,))],
        input_output_aliases={8: 0},
        compiler_params=_cp("arbitrary"), name=name)(sinks, proj, proj, proj, bias, o, do, lse, dqkv)


KC = 512
NSUB = KC // BLK
QB = 512
QPG = KC // QB


def _split2(x):
    hi = x.astype(BF16)
    lo = (x - hi.astype(F32)).astype(BF16)
    return hi, lo


def _tri_ones(cmp):
    jj = lax.broadcasted_iota(jnp.int32, (2 * BLK, BLK), 0) % BLK
    ss = lax.broadcasted_iota(jnp.int32, (2 * BLK, BLK), 1)
    return jnp.concatenate([cmp(jj, ss).astype(BF16), jnp.ones((2 * BLK, BLK), BF16)], axis=1)


def _sub_sums(x, tri1):
    n = x.shape[0]
    st = jnp.concatenate([x[:, s * BLK:(s + 1) * BLK] for s in range(NSUB)], axis=0)
    hi, lo = _split2(st)
    r = _dot(jnp.concatenate([hi, lo], axis=1), tri1, 1, 0)
    return ([r[s * n:(s + 1) * n, :BLK] for s in range(NSUB)], [r[s * n:(s + 1) * n, BLK:] for s in range(NSUB)])


def _log_sig_pair(z):
    lb = jnp.minimum(z, 0.0) - jnp.log1p(jnp.exp(-jnp.abs(z)))
    return lb, lb - z


QGROUPS = NB // NSUB


def _stick_fwd(proj, *, q0, k0, v0, name):
    def body(q_ref, k_ref, v_ref, o_ref, t_ref, qs, ks, vs):
        qs[...] = (q_ref[...] * SCALE).astype(BF16)
        ks[...] = k_ref[...].astype(BF16)
        vs[...] = v_ref[...].astype(BF16)
        tri1 = _tri_ones(lambda j, s: j > s)
        col = lax.broadcasted_iota(jnp.int32, (QB, KC), 1)
        rowi = lax.broadcasted_iota(jnp.int32, (QB, KC), 0)

        for qg in range(QGROUPS):
            def qblock(ii, carry0, qg=qg):
                t0 = pl.multiple_of((qg * QPG + ii) * QB, QB)
                qb = qs[pl.ds(t0, QB), :]
                accs = [jnp.zeros((QB, HD), F32)] * 2
                runs = [jnp.zeros((QB, BLK), F32)] * 2
                for c in reversed(range(qg + 1)):
                    s0 = c * KC
                    diag = c == qg
                    before = (s0 + col) < (t0 + rowi) if diag else None
                    for hh in range(2):
                        kh = ks[s0:s0 + KC, hh * HD:(hh + 1) * HD]
                        vh = vs[s0:s0 + KC, hh * HD:(hh + 1) * HD]
                        lb, lk = _log_sig_pair(_dot(qb[:, hh * HD:(hh + 1) * HD], kh, 1, 1))
                        if diag:
                            lk = jnp.where(before, lk, 0.0)
                        suf, tot = _sub_sums(lk, tri1)
                        ws, run = [], runs[hh]
                        for s in reversed(range(NSUB)):
                            ws.append(jnp.exp(lb[:, s * BLK:(s + 1) * BLK] + suf[s] + run))
                            run = run + tot[s]
                        w = jnp.concatenate(ws[::-1], axis=1)
                        if diag:
                            w = jnp.where(before, w, 0.0)
                        accs[hh] = accs[hh] + _dot(w.astype(BF16), vh, 1, 0)
                        runs[hh] = run
                o_ref[pl.ds(t0, QB), :] = jnp.concatenate(accs, axis=1)
                t_ref[pl.ds(t0, QB), :] = _lane_halves(runs[0], runs[1])
                return carry0

            lax.fori_loop(0, QPG, qblock, 0)

    slab = lambda off: pl.BlockSpec((None, S, LANES), lambda p: (off + p, 0, 0))
    out = pl.BlockSpec((None, S, LANES), lambda p: (p, 0, 0))
    return pl.pallas_call(
        body, grid=(2,), in_specs=[slab(q0), slab(k0), slab(v0)], out_specs=[out, out],
        out_shape=[jax.ShapeDtypeStruct((2, S, LANES), F32)] * 2,
        scratch_shapes=[pltpu.VMEM((S, LANES), BF16)] * 3,
        compiler_params=_cp("arbitrary"), name=name)(proj, proj, proj)


def _stick_bwd(proj, do, tot, dqkv, *, q0, k0, v0, name):
    def body(q_ref, k_ref, v_ref, do_ref, t_ref, dqkv_in, dqkv_ref, qs, ks, vs, dos, dk_acc, dv_acc, obuf, osem):
        p = pl.program_id(0)
        dq_ref, dk_ref, dv_ref = obuf.at[0], obuf.at[1], obuf.at[2]
        qs[...] = (q_ref[...] * SCALE).astype(BF16)
        ks[...] = k_ref[...].astype(BF16)
        vs[...] = v_ref[...].astype(BF16)
        dos[...] = do_ref[...].astype(BF16)
        dk_acc[...] = jnp.zeros((2, S, HD), F32)
        dv_acc[...] = jnp.zeros((2, S, HD), F32)
        tri_inc = _tri_ones(lambda j, s: j <= s)
        tri_exc = _tri_ones(lambda j, s: j < s)
        col = lax.broadcasted_iota(jnp.int32, (QB, KC), 1)
        rowi = lax.broadcasted_iota(jnp.int32, (QB, KC), 0)

        for qg in range(QGROUPS):
            def qblock(ii, carry0, qg=qg):
                t0 = pl.multiple_of((qg * QPG + ii) * QB, QB)
                qb = qs[pl.ds(t0, QB), :]
                dob = dos[pl.ds(t0, QB), :]
                tb = t_ref[pl.ds(t0, QB), :]
                dqs = [jnp.zeros((QB, HD), F32)] * 2
                pruns = [jnp.zeros((QB, BLK), F32)] * 2
                eruns = [jnp.zeros((QB, BLK), F32)] * 2
                for c in range(qg + 1):
                    s0 = c * KC
                    diag = c == qg
                    before = (s0 + col) < (t0 + rowi) if diag else None
                    for hh in range(2):
                        qh = qb[:, hh * HD:(hh + 1) * HD]
                        doh = dob[:, hh * HD:(hh + 1) * HD]
                        tt = tb[:, 64 * hh:64 * hh + 1]
                        kh = ks[s0:s0 + KC, hh * HD:(hh + 1) * HD]
                        vh = vs[s0:s0 + KC, hh * HD:(hh + 1) * HD]
                        lb, lk = _log_sig_pair(_dot(qh, kh, 1, 1))
                        if diag:
                            lk = jnp.where(before, lk, 0.0)
                        pin, ptot = _sub_sums(lk, tri_inc)
                        ws, prun = [], pruns[hh]
                        for s in range(NSUB):
                            ws.append(jnp.exp(lb[:, s * BLK:(s + 1) * BLK] + (tt - (pin[s] + prun))))
                            prun = prun + ptot[s]
                        w = jnp.concatenate(ws, axis=1)
                        if diag:
                            w = jnp.where(before, w, 0.0)
                        e = w * _dot(doh, vh, 1, 1)
                        pex, etot = _sub_sums(e, tri_exc)
                        cs, erun = [], eruns[hh]
                        for s in range(NSUB):
                            cs.append(pex[s] + erun)
                            erun = erun + etot[s]
                        sig = jnp.exp(lb)
                        dz = e * (1.0 - sig) - jnp.concatenate(cs, axis=1) * sig
                        if diag:
                            dz = jnp.where(before, dz, 0.0)
                        dz = dz.astype(BF16)
                        dqs[hh] = dqs[hh] + _dot(dz, kh, 1, 0)
                        dk_acc[hh, s0:s0 + KC, :] += _dot(dz, qh, 0, 0)
                        dv_acc[hh, s0:s0 + KC, :] += _dot(w.astype(BF16), doh, 0, 0)
                        pruns[hh], eruns[hh] = prun, erun
                dq_ref[pl.ds(t0, QB), :] = (jnp.concatenate(dqs, axis=1) * SCALE).astype(BF16)
                return carry0

            lax.fori_loop(0, QPG, qblock, 0)
        dk_ref[...] = jnp.concatenate([dk_acc[0], dk_acc[1]], axis=1).astype(BF16)
        dv_ref[...] = jnp.concatenate([dv_acc[0], dv_acc[1]], axis=1).astype(BF16)
        puts = [pltpu.make_async_copy(obuf.at[i], dqkv_ref.at[off + p], osem.at[i]) for i, off in enumerate((q0, k0, v0))]
        for cp in puts:
            cp.start()
        for cp in puts:
            cp.wait()

    slab = lambda off: pl.BlockSpec((None, S, LANES), lambda p: (off + p, 0, 0))
    pair = pl.BlockSpec((None, S, LANES), lambda p: (p, 0, 0))
    return pl.pallas_call(
        body, grid=(2,), in_specs=[slab(q0), slab(k0), slab(v0), pair, pair, ANY], out_specs=ANY,
        out_shape=jax.ShapeDtypeStruct(dqkv.shape, BF16),
        scratch_shapes=[pltpu.VMEM((S, LANES), BF16)] * 4 + [pltpu.VMEM((2, S, HD), F32)] * 2
        + [pltpu.VMEM((3, S, LANES), BF16), pltpu.SemaphoreType.DMA((3,))],
        input_output_aliases={5: 0},
        compiler_params=_cp("arbitrary"), name=name)(proj, proj, proj, do, tot, dqkv)


def _cat_slabs(ref):
    return jnp.concatenate([ref[s] for s in range(ref.shape[0])], axis=1)


def _merge_fwd(o_a, o_b, o_c, gates, b_gate, wa, wb, wc, w_out, name):
    tm = ROW_TILE

    def body(oa_ref, ob_ref, oc_ref, g_ref, bg_ref, wa_ref, wb_ref, wc_ref, wo_ref, mg_ref, mo_ref):
        acc = jnp.zeros((tm, D), F32)
        for i, (o_ref, w_ref) in enumerate(((oa_ref, wa_ref), (ob_ref, wb_ref), (oc_ref, wc_ref))):
            pr = _dot(_cat_slabs(o_ref).astype(BF16), w_ref[...], 1, 0)
            sg = jax.nn.sigmoid(g_ref[:, i * D:(i + 1) * D] + bg_ref[i:i + 1, :])
            acc = acc + sg * pr
        mg = acc.astype(BF16)
        mg_ref[...] = mg
        mo_ref[...] = _dot(mg, wo_ref[...], 1, 0)

    slabs = lambda n: pl.BlockSpec((n, tm, LANES), lambda i: (0, i, 0))
    full = lambda r, c: pl.BlockSpec((r, c), lambda i: (0, 0))
    row = pl.BlockSpec((tm, D), lambda i: (i, 0))
    return pl.pallas_call(
        body, grid=(S // tm,),
        in_specs=[slabs(2), slabs(4), slabs(2), pl.BlockSpec((tm, GATE_COLS), lambda i: (i, 0)), full(3, D),
                  full(256, D), full(512, D), full(256, D), full(D, D)],
        out_specs=[row, row],
        out_shape=[jax.ShapeDtypeStruct((S, D), BF16), jax.ShapeDtypeStruct((S, D), F32)],
        compiler_params=_cp("parallel"), name=name)(o_a, o_b, o_c, gates, b_gate, wa, wb, wc, w_out)


def _merge_bwd(d_mo, o_a, o_b, o_c, gates, b_gate, wa, wb, wc, w_out, name):
    tm = ROW_TILE
    nsteps = S // tm

    def body(dmo_ref, oa_ref, ob_ref, oc_ref, g_ref, bg_ref, wa_ref, wb_ref, wc_ref, wo_ref,
             doa_ref, dob_ref, doc_ref, dg_ref, dwa_ref, dwb_ref, dwc_ref, dbg_ref, acc_a, acc_b, acc_c):
        @pl.when(pl.program_id(0) == 0)
        def _():
            acc_a[...] = jnp.zeros(acc_a.shape, F32)
            acc_b[...] = jnp.zeros(acc_b.shape, F32)
            acc_c[...] = jnp.zeros(acc_c.shape, F32)
            dbg_ref[...] = jnp.zeros(dbg_ref.shape, F32)

        dmg = _dot(dmo_ref[...], wo_ref[...], 1, 1)
        trip = ((oa_ref, wa_ref, doa_ref, acc_a), (ob_ref, wb_ref, dob_ref, acc_b), (oc_ref, wc_ref, doc_ref, acc_c))
        for i, (o_ref, w_ref, do_ref, dw_ref) in enumerate(trip):
            ob = _cat_slabs(o_ref).astype(BF16)
            pr = _dot(ob, w_ref[...], 1, 0)
            sg = jax.nn.sigmoid(g_ref[:, i * D:(i + 1) * D] + bg_ref[i:i + 1, :])
            dgate = dmg * pr * sg * (1.0 - sg)
            dg_ref[:, i * D:(i + 1) * D] = dgate.astype(BF16)
            dbg_ref[i:i + 1, :] += jnp.sum(dgate, axis=0, keepdims=True)
            dpr = (dmg * sg).astype(BF16)
            do = _dot(dpr, w_ref[...], 1, 1)
            for s in range(do_ref.shape[0]):
                do_ref[s] = do[:, s * LANES:(s + 1) * LANES]
            dw_ref[...] += _dot(ob, dpr, 0, 0)

        @pl.when(pl.program_id(0) == nsteps - 1)
        def _():
            dwa_ref[...] = acc_a[...].astype(BF16)
            dwb_ref[...] = acc_b[...].astype(BF16)
            dwc_ref[...] = acc_c[...].astype(BF16)

    slabs = lambda n: pl.BlockSpec((n, tm, LANES), lambda i: (0, i, 0))
    full = lambda r, c: pl.BlockSpec((r, c), lambda i: (0, 0))
    row = pl.BlockSpec((tm, D), lambda i: (i, 0))
    return pl.pallas_call(
        body, grid=(S // tm,),
        in_specs=[row, slabs(2), slabs(4), slabs(2), pl.BlockSpec((tm, GATE_COLS), lambda i: (i, 0)), full(3, D),
                  full(256, D), full(512, D), full(256, D), full(D, D)],
        out_specs=[slabs(2), slabs(4), slabs(2), pl.BlockSpec((tm, GATE_COLS), lambda i: (i, 0)),
                   full(256, D), full(512, D), full(256, D), full(3, D)],
        out_shape=[jax.ShapeDtypeStruct((2, S, LANES), F32), jax.ShapeDtypeStruct((4, S, LANES), F32),
                   jax.ShapeDtypeStruct((2, S, LANES), F32), jax.ShapeDtypeStruct((S, GATE_COLS), BF16),
                   jax.ShapeDtypeStruct((256, D), BF16), jax.ShapeDtypeStruct((512, D), BF16),
                   jax.ShapeDtypeStruct((256, D), BF16), jax.ShapeDtypeStruct((3, D), F32)],
        scratch_shapes=[pltpu.VMEM((256, D), F32), pltpu.VMEM((512, D), F32), pltpu.VMEM((256, D), F32)],
        compiler_params=_cp("arbitrary"), name=name)(d_mo, o_a, o_b, o_c, gates, b_gate, wa, wb, wc, w_out)


FC = 256
GELU_K = math.sqrt(2.0 / math.pi)
GELU_C = 0.044715


RC = 64
NRC = S // RC


def _down(tail, cur, n):
    row = lax.broadcasted_iota(jnp.int32, tail.shape, 0)
    rolled = pltpu.roll(cur, n, 0)
    first = jnp.where(row < n, pltpu.roll(tail, n, 0), rolled[0:8])
    return jnp.concatenate([first, rolled[8:]], axis=0)


def _up(cur, head, n):
    row = lax.broadcasted_iota(jnp.int32, head.shape, 0)
    rolled = pltpu.roll(cur, RC - n, 0)
    last = jnp.where(row >= 8 - n, pltpu.roll(head, 8 - n, 0), rolled[RC - 8:])
    return jnp.concatenate([rolled[:RC - 8], last], axis=0)


def _conv_chunk(load, j, w_ref, b_ref, half):
    r0 = pl.multiple_of(j * RC, RC)
    cur = load(r0, RC).astype(F32)
    tail = load(pl.multiple_of(jnp.maximum(r0 - 16, 0), 16), 16).astype(F32)[8:16]
    tail = jnp.where(j > 0, tail, 0.0)
    d1 = _down(tail, cur, 1)
    d2 = _down(tail, cur, 2)
    y = w_ref[0:1, half, :] * d2 + w_ref[1:2, half, :] * d1 + w_ref[2:3, half, :] * cur + b_ref[half:half + 1, :]
    return y, cur, d1, d2


def _chunk(j):
    return pl.ds(pl.multiple_of(j * RC, RC), RC)


def _fold8(x):
    return jnp.sum(x.reshape(RC // 8, 8, x.shape[-1]), axis=0)


def _ffn_act(u, conv_w, conv_b, name):
    def body(u_ref, w_ref, b_ref, a_ref, y_ref):
        def step(j, carry):
            yg = _conv_chunk(lambda r, n: u_ref[0, pl.ds(r, n), :], j, w_ref, b_ref, 0)[0]
            yv = _conv_chunk(lambda r, n: u_ref[1, pl.ds(r, n), :], j, w_ref, b_ref, 1)[0]
            th = jnp.tanh(GELU_K * (yg + GELU_C * yg * yg * yg))
            a_ref[_chunk(j), :] = (0.5 * yg * (1.0 + th) * yv).astype(BF16)
            y_ref[0, _chunk(j), :] = yg.astype(BF16)
            y_ref[1, _chunk(j), :] = yv.astype(BF16)
            return carry

        lax.fori_loop(0, NRC, step, 0)

    return pl.pallas_call(
        body, grid=(D_FF // FC,),
        in_specs=[pl.BlockSpec((2, S, FC), lambda j: (0, 0, j)), pl.BlockSpec((3, 2, FC), lambda j: (0, 0, j)),
                  pl.BlockSpec((2, FC), lambda j: (0, j))],
        out_specs=[pl.BlockSpec((S, FC), lambda j: (0, j)), pl.BlockSpec((2, S, FC), lambda j: (0, 0, j))],
        out_shape=[jax.ShapeDtypeStruct((S, D_FF), BF16), jax.ShapeDtypeStruct((2, S, D_FF), BF16)],
        compiler_params=_cp("parallel"), name=name)(u, conv_w, conv_b)


def _ffn_act_bwd(u, y, d_a, conv_w, name):
    def body(u_ref, y_ref, da_ref, w_ref, du_ref, dw_ref, db_ref, dy_s):
        def first(j, acc):
            yg = y_ref[0, _chunk(j), :].astype(F32)
            yv = y_ref[1, _chunk(j), :].astype(F32)
            th = jnp.tanh(GELU_K * (yg + GELU_C * yg * yg * yg))
            gelu = 0.5 * yg * (1.0 + th)
            dgelu = 0.5 * (1.0 + th) + 0.5 * yg * (1.0 - th * th) * GELU_K * (1.0 + 3.0 * GELU_C * yg * yg)
            da = da_ref[_chunk(j), :].astype(F32)
            dyg = da * yv * dgelu
            dyv = da * gelu
            dy_s[0, _chunk(j), :] = dyg
            dy_s[1, _chunk(j), :] = dyv
            return acc[0] + _fold8(dyg), acc[1] + _fold8(dyv)

        zero = jnp.zeros((8, FC), F32)
        accb = lax.fori_loop(0, NRC, first, (zero, zero))
        for half in range(2):
            db_ref[half:half + 1, :] = jnp.sum(accb[half], axis=0, keepdims=True)

        def second(j, acc):
            new = []
            for half in range(2):
                cur = dy_s[half, _chunk(j), :]
                h0 = pl.multiple_of(jnp.minimum((j + 1) * RC, S - 8), 8)
                head = jnp.where(j < NRC - 1, dy_s[half, pl.ds(h0, 8), :], 0.0)
                up1 = _up(cur, head, 1)
                up2 = _up(cur, head, 2)
                du = w_ref[2:3, half, :] * cur + w_ref[1:2, half, :] * up1 + w_ref[0:1, half, :] * up2
                du_ref[half, _chunk(j), :] = du.astype(BF16)
                uu = u_ref[half, _chunk(j), :].astype(F32)
                new += [_fold8(up2 * uu), _fold8(up1 * uu), _fold8(cur * uu)]
            return tuple(a + n for a, n in zip(acc, new))

        accw = lax.fori_loop(0, NRC, second, tuple(zero for _ in range(6)))
        for half in range(2):
            for k in range(3):
                dw_ref[k:k + 1, half, :] = jnp.sum(accw[3 * half + k], axis=0, keepdims=True)

    return pl.pallas_call(
        body, grid=(D_FF // FC,),
        in_specs=[pl.BlockSpec((2, S, FC), lambda j: (0, 0, j)), pl.BlockSpec((2, S, FC), lambda j: (0, 0, j)),
                  pl.BlockSpec((S, FC), lambda j: (0, j)), pl.BlockSpec((3, 2, FC), lambda j: (0, 0, j))],
        out_specs=[pl.BlockSpec((2, S, FC), lambda j: (0, 0, j)), pl.BlockSpec((3, 2, FC), lambda j: (0, 0, j)),
                   pl.BlockSpec((2, FC), lambda j: (0, j))],
        out_shape=[jax.ShapeDtypeStruct((2, S, D_FF), BF16), jax.ShapeDtypeStruct((3, 2, D_FF), F32),
                   jax.ShapeDtypeStruct((2, D_FF), F32)],
        scratch_shapes=[pltpu.VMEM((2, S, FC), F32)],
        compiler_params=_cp("parallel"), name=name)(u, y, d_a, conv_w)


def _layer_fwd(x, h1, w, bias, lname):
    n = lambda s: f"{lname}_{s}"
    w.need("in", h1)
    tn = 768
    proj = _mm(h1, w["w_in"], grid=(1, QKV_COLS // tn, 1),
               a_spec=pl.BlockSpec((S, D), lambda i, j, k: (i, 0)),
               b_spec=pl.BlockSpec((tn, D), lambda i, j, k: (j, 0)),
               out_shape=jax.ShapeDtypeStruct((QKV_SLABS, S, LANES), F32),
               out_spec=pl.BlockSpec((tn // LANES, S, LANES), lambda i, j, k: (j, i, 0)),
               ca=1, cb=1, acc_shape=(S, tn), out_slab=True, name=n("proj_qkv"))
    gates = _mm(h1, w["w_in"], grid=(1, GATE_COLS // tn, 1),
                a_spec=pl.BlockSpec((S, D), lambda i, j, k: (i, 0)),
                b_spec=pl.BlockSpec((tn, D), lambda i, j, k: (j + QKV_COLS // tn, 0)),
                out_shape=jax.ShapeDtypeStruct((S, GATE_COLS), BF16),
                out_spec=pl.BlockSpec((S, tn), lambda i, j, k: (i, j)),
                ca=1, cb=1, acc_shape=(S, tn), name=n("proj_gate"))
    nums, stats = [], []
    for g, (_, d) in enumerate(A_GROUPS):
        nm, st = _band_fwd(proj, bias, w["sinks"], d=d, q0=2 * g, k0=6 + 2 * g, v0=12 + 2 * g, npairs=2, bias0=2 * g,
                           shared_kv=False, name=n(f"attn_a{g}_fwd"))
        nums.append(nm)
        stats.append(st)
    o_a, lse_a = _combine_a(nums, stats, n("attn_a_combine"))
    o_b, lse_b = _band_fwd(proj, bias, w["sinks"], d=1, q0=18, k0=22, v0=23, npairs=4, bias0=6, shared_kv=True,
                           name=n("attn_b_fwd"))
    o_c, tot_c = _stick_fwd(proj, q0=24, k0=26, v0=28, name=n("attn_c_fwd"))
    w.need("mix", tot_c)
    merged, mo = _merge_fwd(o_a, o_b, o_c, gates, w["b_gate"], w["w_br_a"], w["w_br_b"], w["w_br_c"], w["w_out"], n("merge_fwd"))
    x2, h2 = _postnorm_res(x, mo, w["attn_post_norm"], w["ffn_pre_norm"], n("attn_post"))
    w.need("ffn", h2)
    u = _mm(h2, w["w_up"], grid=(1, 2 * D_FF // 1024, 1),
            a_spec=pl.BlockSpec((S, D), lambda i, j, k: (i, 0)),
            b_spec=pl.BlockSpec((D, 1024), lambda i, j, k: (0, j)),
            out_shape=jax.ShapeDtypeStruct((2, S, D_FF), BF16),
            out_spec=pl.BlockSpec((None, S, 1024), lambda i, j, k: (j // 4, i, j % 4)),
            ca=1, cb=0, acc_shape=(S, 1024), name=n("ffn_up"))
    a, y = _ffn_act(u, w["conv_w"], w["conv_b"], n("ffn_act"))
    fo = _mm_nn(a, w["w_down"], F32, 1024, 1024, 2048, n("ffn_down"))
    saved = dict(x=x, h1=h1, proj=proj, gates=gates, o_a=o_a, lse_a=lse_a, o_b=o_b, lse_b=lse_b, o_c=o_c, tot_c=tot_c,
                 merged=merged, mo=mo, x2=x2, h2=h2, u=u, y=y, a=a, fo=fo)
    return saved


def _layer_bwd(dx3, sv, w, bias, lname, tok=None, on_part=None, d_fo=None, below=None):
    n = lambda s: f"{lname}_{s}"
    g = {}

    def part(group, vec):
        t = on_part(group, g) if on_part is not None else None
        return vec if t is None else vec + t

    if d_fo is None:
        gain = w["ffn_post_norm"] if tok is None else w["ffn_post_norm"] + tok
        d_fo, g["ffn_post_norm"] = _norm_bwd(sv["fo"], gain, [dx3], None, BF16, n("ffn_post_bwd"))
    else:
        d_fo, g["ffn_post_norm"] = d_fo
    d_a = _mm_nt(d_fo, w["w_down"], BF16, S, 1024, 1024, n("ffn_down_bwd_x"))
    g["w_down"] = _mm_tn(sv["a"], d_fo, BF16, 1024, 1024, S, n("ffn_down_bwd_w"))
    d_u, dcw, dcb = _ffn_act_bwd(sv["u"], sv["y"], d_a, w["conv_w"], n("ffn_act_bwd"))
    g["conv_w"] = dcw.reshape(3, 2 * D_FF)
    g["conv_b"] = dcb.reshape(1, 2 * D_FF)
    g["w_up"] = _mm(sv["h2"], d_u, grid=(1, 2 * D_FF // 1024, 1),
                    a_spec=pl.BlockSpec((S, D), lambda i, j, k: (k, 0)),
                    b_spec=pl.BlockSpec((None, S, 1024), lambda i, j, k: (j // 4, k, j % 4)),
                    out_shape=jax.ShapeDtypeStruct((D, 2 * D_FF), BF16),
                    out_spec=pl.BlockSpec((D, 1024), lambda i, j, k: (0, j)),
                    ca=0, cb=0, acc_shape=(D, 1024), name=n("ffn_up_bwd_w"))
    tok_ffn = on_part("ffn", g) if on_part is not None else None
    d_h2 = _mm(d_u, w["w_up"], grid=(S // 1024, 1, 2),
               a_spec=pl.BlockSpec((None, 1024, D_FF), lambda i, j, k: (k, i, 0)),
               b_spec=pl.BlockSpec((D, D_FF), lambda i, j, k: (0, k)),
               out_shape=jax.ShapeDtypeStruct((S, D), F32),
               out_spec=pl.BlockSpec((1024, D), lambda i, j, k: (i, 0)),
               ca=1, cb=1, acc_shape=(1024, D), after=tok_ffn, name=n("ffn_up_bwd_x"))
    dx2, d_mo, g["ffn_pre_norm"], g["attn_post_norm"] = _norm_bwd_chain(
        sv["x2"], w["ffn_pre_norm"], [d_h2], dx3, sv["mo"], w["attn_post_norm"], n("ffn_pre_attn_post_bwd"))
    g["w_out"] = _mm_tn(sv["merged"], d_mo, BF16, 1024, 1024, S, n("out_bwd_w"))
    do_a, do_b, do_c, d_gates, dwa, dwb, dwc, g["b_gate"] = _merge_bwd(
        d_mo, sv["o_a"], sv["o_b"], sv["o_c"], sv["gates"], w["b_gate"], w["w_br_a"], w["w_br_b"], w["w_br_c"],
        w["w_out"], n("merge_bwd"))
    g["w_br_a"], g["w_br_b"], g["w_br_c"] = dwa, dwb, dwc
    sinks = part("mix", w["sinks"])
    proj = sv["proj"]
    dqkv = lax.empty((QKV_SLABS, S, LANES), BF16)
    gbias = []
    for gi, (_, d) in enumerate(A_GROUPS):
        dqkv, gg, _ = _band_bwd(proj, bias, sv["o_a"], do_a, sv["lse_a"], sinks, dqkv, d=d, q0=2 * gi, k0=6 + 2 * gi,
                                v0=12 + 2 * gi, npairs=2, bias0=2 * gi, shared_kv=False, name=n(f"attn_a{gi}_bwd"))
        gbias.append(gg)
    dqkv, ggb, dsink = _band_bwd(proj, bias, sv["o_b"], do_b, sv["lse_b"], sinks, dqkv, d=1, q0=18, k0=22, v0=23,
                                 npairs=4, bias0=6, shared_kv=True, name=n("attn_b_bwd"))
    gbias.append(ggb)
    g["bias_g"] = jnp.concatenate(gbias, axis=0).reshape(N_BIAS_HEADS, BLK, 2 * BLK)
    g["sinks"] = dsink[:, 0, :2].reshape(1, 8)
    dqkv = _stick_bwd(proj, do_c, sv["tot_c"], dqkv, q0=24, k0=26, v0=28, name=n("attn_c_bwd"))
    ts = 6
    tsx = QKV_SLABS
    dw_in = _mm(dqkv, sv["h1"], grid=(QKV_SLABS // ts, 1, 1),
                a_spec=pl.BlockSpec((ts, S, LANES), lambda i, j, k: (i, k, 0)),
                b_spec=pl.BlockSpec((S, D), lambda i, j, k: (k, 0)),
                out_shape=jax.ShapeDtypeStruct((IN_COLS, D), BF16),
                out_spec=pl.BlockSpec((ts * LANES, D), lambda i, j, k: (i, 0)),
                ca=0, cb=0, acc_shape=(ts * LANES, D), a_slab=True, name=n("in_bwd_w_qkv"))
    g["w_in"] = _mm(d_gates, sv["h1"], grid=(GATE_COLS // 768, 1, 1),
                    a_spec=pl.BlockSpec((S, 768), lambda i, j, k: (k, i)),
                    b_spec=pl.BlockSpec((S, D), lambda i, j, k: (k, 0)),
                    out_shape=jax.ShapeDtypeStruct((IN_COLS, D), BF16),
                    out_spec=pl.BlockSpec((768, D), lambda i, j, k: (i + QKV_COLS // 768, 0)),
                    ca=0, cb=0, acc_shape=(768, D), alias_out=dw_in, name=n("in_bwd_w_gate"))
    tok_in = on_part("in", g) if on_part is not None else None
    d_h1a = _mm(dqkv, w["w_in"], grid=(S // 1024, 1, QKV_SLABS // tsx),
                a_spec=pl.BlockSpec((tsx, 1024, LANES), lambda i, j, k: (k, i, 0)),
                b_spec=pl.BlockSpec((tsx * LANES, D), lambda i, j, k: (k, 0)),
                out_shape=jax.ShapeDtypeStruct((S, D), F32),
                out_spec=pl.BlockSpec((1024, D), lambda i, j, k: (i, 0)),
                ca=1, cb=0, acc_shape=(1024, D), a_slab=True, after=tok_in, name=n("in_bwd_x_qkv"))
    d_h1b = _mm(d_gates, w["w_in"], grid=(S // 1024, 1, GATE_COLS // 768),
                a_spec=pl.BlockSpec((1024, 768), lambda i, j, k: (i, k)),
                b_spec=pl.BlockSpec((768, D), lambda i, j, k: (k + QKV_COLS // 768, 0)),
                out_shape=jax.ShapeDtypeStruct((S, D), F32),
                out_spec=pl.BlockSpec((1024, D), lambda i, j, k: (i, 0)),
                ca=1, cb=0, acc_shape=(1024, D), after=tok_in, name=n("in_bwd_x_gate"))
    if below is None:
        dx, g["attn_pre_norm"] = _norm_bwd(sv["x"], w["attn_pre_norm"], [d_h1a, d_h1b], dx2, F32, n("attn_pre_bwd"))
        return dx, g, tok_in, None
    dx, d_fo_below, g["attn_pre_norm"], dg_below = _norm_bwd_chain(
        sv["x"], w["attn_pre_norm"], [d_h1a, d_h1b], dx2, below[0], below[1], n("attn_pre_ffn_post_bwd"))
    return dx, g, tok_in, (d_fo_below, dg_below)


def _local_step(x, target, ws, rel_bias, tok=None, on_grads=None):
    buckets = jnp.asarray(_bucket_tiles())
    bias = _bias_tiles(rel_bias, buckets, "bias_tiles").reshape(N_BIAS_HEADS // 2, 2, 2, BLK, 2 * BLK)
    saved = []
    gain0 = ws[0]["attn_pre_norm"] if tok is None else ws[0]["attn_pre_norm"] + tok
    h1 = _prenorm(x, gain0, "l0_attn_pre")
    for l in range(DEPTH):
        sv = _layer_fwd(x, h1, ws[l], bias, f"l{l}")
        saved.append(sv)
        if l + 1 < DEPTH:
            x, h1 = _postnorm_res(sv["x2"], sv["fo"], ws[l]["ffn_post_norm"], ws[l + 1]["attn_pre_norm"], f"l{l}_ffn_post")
    top = saved[-1]
    dy, loss_tile, d_fo_top, dg_top = _loss_head(top["x2"], top["fo"], ws[-1]["ffn_post_norm"], target, "loss_head")
    grads = [None] * DEPTH
    tok, d_fo = None, (d_fo_top, dg_top)
    for l in reversed(range(DEPTH)):
        on_part = None if on_grads is None else functools.partial(on_grads, l)
        below = (saved[l - 1]["fo"], ws[l - 1]["ffn_post_norm"]) if l > 0 else None
        dy, grads[l], tok, d_fo = _layer_bwd(dy, saved[l], ws[l], bias, f"l{l}", tok, on_part, d_fo, below)
    g_rel = _bias_grad([grads[l]["bias_g"] for l in range(DEPTH)], buckets, "bias_grad")[:, :N_BIAS_HEADS]
    return loss_tile, dy, grads, g_rel


def _coords():
    return lax.axis_index("x"), lax.axis_index("y"), lax.axis_index("c")


def _peer(rel):
    x, y, c = _coords()
    return (1 - x if rel & 4 else x, 1 - y if rel & 2 else y, 1 - c if rel & 1 else c)


def _exchange(srcs, dst_shapes, src_win, dst_win, name, after=None):
    nt = len(srcs)
    extra = [] if after is None else [after]

    def body(*refs):
        src_refs, dst_refs = refs[:nt], refs[nt + len(extra):2 * nt + len(extra)]
        send_sems, recv_sems, local_sems = refs[2 * nt + len(extra):]
        x, y, c = _coords()
        me = 4 * x + 2 * y + c
        locals_ = []
        for t in range(nt):
            cp = pltpu.make_async_copy(src_win(t, src_refs[t], me), dst_win(t, dst_refs[t], me), local_sems.at[t])
            cp.start()
            locals_.append(cp)
        sends = []
        for rel in range(1, NDEV):
            px, py, pc = _peer(rel)
            q = 4 * px + 2 * py + pc
            for t in range(nt):
                cp = pltpu.make_async_remote_copy(
                    src_ref=src_win(t, src_refs[t], q), dst_ref=dst_win(t, dst_refs[t], me),
                    send_sem=send_sems.at[rel - 1, t], recv_sem=recv_sems.at[rel - 1, t],
                    device_id=(px, py, pc), device_id_type=MESH)
                cp.start()
                sends.append(cp)
        for rel in range(1, NDEV):
            px, py, pc = _peer(rel)
            q = 4 * px + 2 * py + pc
            for t in range(nt):
                pltpu.make_async_remote_copy(
                    src_ref=src_win(t, src_refs[t], me), dst_ref=dst_win(t, dst_refs[t], q),
                    send_sem=send_sems.at[rel - 1, t], recv_sem=recv_sems.at[rel - 1, t],
                    device_id=(px, py, pc), device_id_type=MESH).wait_recv()
        for cp in sends:
            cp.wait_send()
        for cp in locals_:
            cp.wait()

    return pl.pallas_call(
        body, in_specs=[ANY] * (nt + len(extra)), out_specs=[ANY] * nt, out_shape=dst_shapes,
        scratch_shapes=[pltpu.SemaphoreType.DMA((NDEV - 1, nt)), pltpu.SemaphoreType.DMA((NDEV - 1, nt)),
                        pltpu.SemaphoreType.DMA((nt,))],
        name=name)(*srcs, *extra)


BIG = (("w_in", 0, 864), ("w_br_a", 1, 128), ("w_br_b", 1, 128), ("w_br_c", 1, 128), ("w_out", 0, 128),
       ("w_up", 1, 1024), ("w_down", 0, 512))


NBIG = len(BIG)
BIG_FULL = {"w_in": (IN_COLS, D), "w_br_a": (256, D), "w_br_b": (512, D), "w_br_c": (256, D), "w_out": (D, D),
            "w_up": (D, 2 * D_FF), "w_down": (D_FF, D)}
SHARD_ROWS = {"w_in": 288, "w_up": 256, "w_down": 256}
LAYER_GROUPS = (("in", (0,)), ("mix", (1, 2, 3, 4)), ("ffn", (5, 6)))

HBM_SPEC = pl.BlockSpec(memory_space=pltpu.HBM)
SEM_SPEC = pl.BlockSpec(memory_space=pltpu.SEMAPHORE)


def _hbm(a):
    return pltpu.with_memory_space_constraint(a, pltpu.HBM)


def _shard_window(t, ref, k):
    nm, ax, ext = BIG[t % NBIG]
    off = pl.multiple_of(k * ext, ext)
    if ax == 0:
        return ref.at[pl.ds(off, ext), :]
    return ref.at[:, pl.ds(off, ext)]


def _whole(t, ref, k):
    return ref


def _slot(t, ref, k):
    return ref.at[k]


def _own_block_spec(t, rows, me_of):
    nm, ax, ext = BIG[t % NBIG]
    r, c = BIG_FULL[nm]
    if ax == 0:
        return pl.BlockSpec((rows, c), lambda i, m: (me_of(m) * (ext // rows) + i, 0))
    return pl.BlockSpec((rows, ext), lambda i, m: (i, me_of(m)))


def _cast_own(t, shards, me_arr, name):
    nm, ax, ext = BIG[t % NBIG]
    layer = t // NBIG
    _, nr, nc = shards.shape
    rows = SHARD_ROWS.get(nm, nr)
    shape = BIG_FULL[nm]

    def body(m_ref, s_ref, o_ref):
        o_ref[...] = s_ref[...].astype(BF16)

    return pl.pallas_call(
        body, grid_spec=pltpu.PrefetchScalarGridSpec(
            num_scalar_prefetch=1, grid=(nr // rows,),
            in_specs=[pl.BlockSpec((None, rows, nc), lambda i, m: (layer, i, 0))],
            out_specs=_own_block_spec(t, rows, lambda m: m[0])),
        out_shape=jax.ShapeDtypeStruct(shape, BF16), compiler_params=_cp("arbitrary"), name=name)(me_arr, shards)


ALL_RELS = tuple(range(1, NDEV))
NEAR_RELS = (1, 2, 4, 6)
FAR_RELS = (2, 4, 6)


def _xchg_start(srcs, lands, groups, src_win, dst_win, after, name, rels=ALL_RELS, tids=None):
    ns = 0 if srcs is None else len(srcs)
    nt, ng = len(lands), len(groups)
    ins = ([] if srcs is None else list(srcs)) + list(lands)

    def body(*refs):
        src_refs, land_refs = refs[:ns], refs[ns:ns + nt]
        sems = refs[ns + nt + 1:ns + nt + 1 + 2 * ng]
        token = refs[-1]
        x, y, c = _coords()
        me = 4 * x + 2 * y + c
        for gi, grp in enumerate(groups):
            for j, t in enumerate(grp):
                tid = t if tids is None else tids[t]
                for ri, rel in enumerate(rels):
                    px, py, pc = _peer(rel)
                    q = 4 * px + 2 * py + pc
                    src = dst_win(tid, land_refs[t], me) if srcs is None else src_win(tid, src_refs[t], q)
                    pltpu.make_async_remote_copy(
                        src_ref=src, dst_ref=dst_win(tid, land_refs[t], me),
                        send_sem=sems[2 * gi].at[ri * len(grp) + j],
                        recv_sem=sems[2 * gi + 1].at[ri * len(grp) + j],
                        device_id=(px, py, pc), device_id_type=MESH).start()
        token[...] = jnp.zeros((8, LANES), F32)

    out_shape = []
    for grp in groups:
        out_shape += [pltpu.SemaphoreType.DMA((len(rels) * len(grp),))] * 2
    out_shape += [pltpu.HBM(a.shape, a.dtype) for a in ins]
    out_shape.append(jax.ShapeDtypeStruct((8, LANES), F32))
    outs = pl.pallas_call(
        body, in_specs=[HBM_SPEC] * len(ins) + [ANY],
        out_specs=[SEM_SPEC] * (2 * ng) + [HBM_SPEC] * len(ins) + [pl.BlockSpec(memory_space=pltpu.VMEM)],
        out_shape=out_shape, input_output_aliases={i: 2 * ng + i for i in range(len(ins))},
        compiler_params=pltpu.CompilerParams(has_side_effects=pltpu.SideEffectType.DATAFLOW_SIDE_EFFECTING),
        name=name)(*[_hbm(a) for a in ins], after)
    sems = [(outs[2 * gi], outs[2 * gi + 1]) for gi in range(ng)]
    thru = list(outs[2 * ng:2 * ng + len(ins)])
    return sems, (None if srcs is None else thru[:ns]), thru[ns:], outs[-1]


def _xchg_wait(sems, srcs, lands, tids, after, src_win, dst_win, name, rels=ALL_RELS):
    ns = 0 if srcs is None else len(srcs)
    n = len(lands)
    send_sem, recv_sem = sems
    ins = ([] if srcs is None else list(srcs)) + list(lands)

    def body(*refs):
        src_refs, land_refs = refs[:ns], refs[ns:ns + n]
        ssem, rsem = refs[ns + n], refs[ns + n + 1]
        x, y, c = _coords()
        me = 4 * x + 2 * y + c
        for j, t in enumerate(tids):
            for ri, rel in enumerate(rels):
                px, py, pc = _peer(rel)
                q = 4 * px + 2 * py + pc
                src = dst_win(t, land_refs[j], me) if srcs is None else src_win(t, src_refs[j], q)
                cp = pltpu.make_async_remote_copy(
                    src_ref=src, dst_ref=dst_win(t, land_refs[j], q),
                    send_sem=ssem.at[ri * n + j], recv_sem=rsem.at[ri * n + j],
                    device_id=(px, py, pc), device_id_type=MESH)
                cp.wait_send()
                cp.wait_recv()

    outs = pl.pallas_call(
        body, in_specs=[HBM_SPEC] * len(ins) + [SEM_SPEC, SEM_SPEC, ANY], out_specs=[HBM_SPEC] * len(ins),
        out_shape=[pltpu.HBM(a.shape, a.dtype) for a in ins],
        input_output_aliases={i: i for i in range(len(ins))},
        compiler_params=pltpu.CompilerParams(has_side_effects=pltpu.SideEffectType.DATAFLOW_SIDE_EFFECTING),
        name=name)(*ins, send_sem, recv_sem, after)
    return (None if srcs is None else list(outs[:ns])), list(outs[ns:])


def _gather_forward(sems_in, lands, groups, tids, after, dst_win, name):
    nt, ng = len(lands), len(groups)

    def body(*refs):
        land_refs = refs[:nt]
        in_sems = refs[nt:nt + 2 * ng]
        out_sems = refs[nt + 2 * ng + 1:nt + 4 * ng + 1]
        token = refs[-1]
        x, y, c = _coords()
        me = 4 * x + 2 * y + c
        sib = (x, y, 1 - c)
        for gi, grp in enumerate(groups):
            n = len(grp)
            for j, pos in enumerate(grp):
                t = tids[pos]
                for ri, rel in enumerate(NEAR_RELS):
                    px, py, pc = _peer(rel)
                    q = 4 * px + 2 * py + pc
                    cp = pltpu.make_async_remote_copy(
                        src_ref=dst_win(t, land_refs[pos], me), dst_ref=dst_win(t, land_refs[pos], q),
                        send_sem=in_sems[2 * gi].at[ri * n + j], recv_sem=in_sems[2 * gi + 1].at[ri * n + j],
                        device_id=(px, py, pc), device_id_type=MESH)
                    cp.wait_send()
                    cp.wait_recv()
            for j, pos in enumerate(grp):
                t = tids[pos]
                for fi, rel in enumerate(FAR_RELS):
                    px, py, pc = _peer(rel)
                    q = 4 * px + 2 * py + pc
                    win = dst_win(t, land_refs[pos], q)
                    pltpu.make_async_remote_copy(
                        src_ref=win, dst_ref=win,
                        send_sem=out_sems[2 * gi].at[fi * n + j], recv_sem=out_sems[2 * gi + 1].at[fi * n + j],
                        device_id=sib, device_id_type=MESH).start()
        token[...] = jnp.zeros((8, LANES), F32)

    out_shape = []
    for grp in groups:
        out_shape += [pltpu.SemaphoreType.DMA((len(FAR_RELS) * len(grp),))] * 2
    out_shape += [pltpu.HBM(a.shape, a.dtype) for a in lands]
    out_shape.append(jax.ShapeDtypeStruct((8, LANES), F32))
    flat_sems = [s for pair in sems_in for s in pair]
    outs = pl.pallas_call(
        body, in_specs=[HBM_SPEC] * nt + [SEM_SPEC] * (2 * ng) + [ANY],
        out_specs=[SEM_SPEC] * (2 * ng) + [HBM_SPEC] * nt + [pl.BlockSpec(memory_space=pltpu.VMEM)],
        out_shape=out_shape, input_output_aliases={i: 2 * ng + i for i in range(nt)},
        compiler_params=pltpu.CompilerParams(has_side_effects=pltpu.SideEffectType.DATAFLOW_SIDE_EFFECTING),
        name=name)(*[_hbm(a) for a in lands], *flat_sems, after)
    sems = [(outs[2 * gi], outs[2 * gi + 1]) for gi in range(ng)]
    return sems, list(outs[2 * ng:2 * ng + nt]), outs[-1]


class _Weights:
    def __init__(self, ready, pending=None):
        self.ready = dict(ready)
        self.pending = dict(pending or {})

    def __getitem__(self, k):
        return self.ready[k]

    def need(self, group, after):
        fn = self.pending.pop(group, None)
        if fn is not None:
            self.ready.update(fn(after))


def _adamw_math(w, g, m, v):
    m2 = ADAM_B1 * m + (1.0 - ADAM_B1) * g
    v2 = ADAM_B2 * v + (1.0 - ADAM_B2) * (g * g)
    m_hat = m2 / (1.0 - ADAM_B1 ** ADAM_STEP)
    v_hat = v2 / (1.0 - ADAM_B2 ** ADAM_STEP)
    delta = -ADAM_LR * (m_hat / (jnp.sqrt(v_hat) + ADAM_EPS) + ADAM_WD * w)
    return delta, m2, v2


def _adamw(t, parts, own, me_arr, w, m, v, layer, prev, rows, name):
    nl, nr, nc = w.shape

    def body(me_ref, p_ref, own_ref, w_ref, m_ref, v_ref, *rest):
        g_ref, d_ref, m2_ref, v2_ref = rest[-4:]
        me = me_ref[0]
        g = None
        for k in range(NDEV):
            term = jnp.where(me == k, own_ref[...], p_ref[k]).astype(F32)
            g = term if g is None else g + term
        delta, m2, v2 = _adamw_math(w_ref[...], g, m_ref[...], v_ref[...])
        g_ref[...] = g
        d_ref[...] = delta
        m2_ref[...] = m2
        v2_ref[...] = v2

    blk = pl.BlockSpec((None, rows, nc), lambda i, mm: (layer, i, 0))
    pblk = pl.BlockSpec((NDEV, rows, nc), lambda i, mm: (0, i, 0))
    extra = [] if prev is None else list(prev)
    return pl.pallas_call(
        body, grid_spec=pltpu.PrefetchScalarGridSpec(
            num_scalar_prefetch=1, grid=(nr // rows,),
            in_specs=[pblk, _own_block_spec(t, rows, lambda mm: mm[0]), blk, blk, blk] + [ANY] * len(extra),
            out_specs=[blk] * 4),
        out_shape=[jax.ShapeDtypeStruct(w.shape, F32)] * 4,
        input_output_aliases={6 + k: k for k in range(len(extra))},
        compiler_params=_cp("arbitrary"), name=name)(me_arr, parts, own, w, m, v, *extra)


def _pack(vecs):
    flat = jnp.concatenate([v.reshape(-1).astype(F32) for v in vecs])
    n = flat.shape[0]
    rows = -(-n // (8 * LANES)) * 8
    return jnp.pad(flat, (0, rows * LANES - n)).reshape(rows, LANES)


ROWPACK = (("rel_bias", 32, 32, (NUM_BUCKETS, N_BIAS_HEADS)), ("sinks", 8, 8, (DEPTH, 8)),
           ("attn_pre_norm", 16, 16, (DEPTH, D)), ("attn_post_norm", 16, 16, (DEPTH, D)),
           ("ffn_pre_norm", 16, 16, (DEPTH, D)), ("ffn_post_norm", 16, 16, (DEPTH, D)),
           ("conv_b", 128, 128, (DEPTH, 2 * D_FF)), ("b_gate", 48, 8, (DEPTH, 3, 128)),
           ("conv_w", 384, 48, (DEPTH, 3, 1024)))
ROWS_OWN = sum(r for _, _, r, _ in ROWPACK)
N_REPL = 7
ROWS_REPL = sum(r for _, _, r, _ in ROWPACK[:N_REPL])
ROWS_SHARD = ROWS_OWN - ROWS_REPL


def _as_rows(a, rows):
    a = a.astype(F32)
    if a.shape[-1] < LANES:
        a = jnp.pad(a.reshape(-1, a.shape[-1]), ((0, 0), (0, LANES - a.shape[-1])))
    a = a.reshape(-1, LANES)
    return jnp.pad(a, ((0, rows - a.shape[0]), (0, 0)))


def _rowpack(arrs, entries=ROWPACK):
    return jnp.concatenate([_as_rows(arrs[nm], ro) for nm, _, ro, _ in entries], axis=0)


def _shard_rows(g):
    bg = jnp.transpose(g["b_gate"].astype(F32).reshape(DEPTH * 3, NDEV, LANES), (1, 0, 2))
    bg = jnp.pad(bg, ((0, 0), (0, 8 - DEPTH * 3), (0, 0)))
    cw = jnp.transpose(g["conv_w"].astype(F32).reshape(DEPTH * 3, NDEV, 8, LANES), (1, 0, 2, 3))
    return jnp.concatenate([bg, cw.reshape(NDEV, DEPTH * 3 * 8, LANES)], axis=1)


def _small_update(parts_repl, parts_shard, w, m, v, name):
    nsm = len(ROWPACK)

    def body(pr_ref, ps_ref, w_ref, m_ref, v_ref, *rest):
        outs = rest[:4 * nsm]
        loss_ref = rest[4 * nsm]
        g_s, d_s, m_s, v_s = rest[4 * nsm + 1:]
        gr, gs = pr_ref[0], ps_ref[0]
        for k in range(1, NDEV):
            gr = gr + pr_ref[k]
            gs = gs + ps_ref[k]
        g_s[0:ROWS_REPL, :] = gr[:ROWS_REPL]
        g_s[ROWS_REPL:ROWS_OWN, :] = gs
        loss_ref[...] = gr[ROWS_REPL:]
        delta, m2, v2 = _adamw_math(w_ref[...], g_s[...], m_ref[...], v_ref[...])
        d_s[...] = delta
        m_s[...] = m2
        v_s[...] = v2
        for kind, src in enumerate((g_s, d_s, m_s, v_s)):
            oo = 0
            for idx, (nm, rf, ro, shp) in enumerate(ROWPACK):
                o_ref = outs[kind * nsm + idx]
                if nm in ("rel_bias", "sinks"):
                    o_ref[...] = src[oo:oo + shp[0], 0:shp[1]]
                elif nm == "b_gate":
                    for l in range(DEPTH):
                        o_ref[l] = src[oo + 3 * l:oo + 3 * l + 3, :]
                elif nm == "conv_w":
                    for l in range(DEPTH):
                        for k in range(8):
                            o_ref[l, :, k * LANES:(k + 1) * LANES] = src[pl.ds(oo + 24 * l + k, 3, stride=8), :]
                else:
                    per = shp[1] // LANES
                    for k in range(per):
                        o_ref[:, k * LANES:(k + 1) * LANES] = src[pl.ds(oo + k, DEPTH, stride=per), :]
                oo += ro

    vm = pl.BlockSpec(memory_space=pltpu.VMEM)
    shapes = [jax.ShapeDtypeStruct(shp, F32) for _ in range(4) for _, _, _, shp in ROWPACK]
    shapes.append(jax.ShapeDtypeStruct((8, LANES), F32))
    outs = pl.pallas_call(
        body, in_specs=[vm] * 5, out_specs=[vm] * (4 * nsm + 1), out_shape=shapes,
        scratch_shapes=[pltpu.VMEM((ROWS_OWN, LANES), F32)] * 4,
        name=name)(parts_repl, parts_shard, w, m, v)
    names = [nm for nm, _, _, _ in ROWPACK]
    return [dict(zip(names, outs[kind * nsm:(kind + 1) * nsm])) for kind in range(4)] + [outs[-1]]


def kernel(x, rel_bias, attn_pre_norm, w_in, b_gate, sinks, w_br_a, w_br_b, w_br_c, w_out, attn_post_norm, ffn_pre_norm, w_up, conv_w, conv_b, w_down, ffn_post_norm, loss_target, m_rel_bias, m_attn_pre_norm, m_w_in, m_b_gate, m_sinks, m_w_br_a, m_w_br_b, m_w_br_c, m_w_out, m_attn_post_norm, m_ffn_pre_norm, m_w_up, m_conv_w, m_conv_b, m_w_down, m_ffn_post_norm, v_rel_bias, v_attn_pre_norm, v_w_in, v_b_gate, v_sinks, v_w_br_a, v_w_br_b, v_w_br_c, v_w_out, v_attn_post_norm, v_ffn_pre_norm, v_w_up, v_conv_w, v_conv_b, v_w_down, v_ffn_post_norm):
    P = dict(rel_bias=rel_bias, attn_pre_norm=attn_pre_norm, w_in=w_in, b_gate=b_gate, sinks=sinks, w_br_a=w_br_a,
             w_br_b=w_br_b, w_br_c=w_br_c, w_out=w_out, attn_post_norm=attn_post_norm, ffn_pre_norm=ffn_pre_norm,
             w_up=w_up, conv_w=conv_w, conv_b=conv_b, w_down=w_down, ffn_post_norm=ffn_post_norm)
    M = dict(rel_bias=m_rel_bias, attn_pre_norm=m_attn_pre_norm, w_in=m_w_in, b_gate=m_b_gate, sinks=m_sinks,
             w_br_a=m_w_br_a, w_br_b=m_w_br_b, w_br_c=m_w_br_c, w_out=m_w_out, attn_post_norm=m_attn_post_norm,
             ffn_pre_norm=m_ffn_pre_norm, w_up=m_w_up, conv_w=m_conv_w, conv_b=m_conv_b, w_down=m_w_down,
             ffn_post_norm=m_ffn_post_norm)
    V = dict(rel_bias=v_rel_bias, attn_pre_norm=v_attn_pre_norm, w_in=v_w_in, b_gate=v_b_gate, sinks=v_sinks,
             w_br_a=v_w_br_a, w_br_b=v_w_br_b, w_br_c=v_w_br_c, w_out=v_w_out, attn_post_norm=v_attn_post_norm,
             ffn_pre_norm=v_ffn_pre_norm, w_up=v_w_up, conv_w=v_conv_w, conv_b=v_conv_b, w_down=v_w_down,
             ffn_post_norm=v_ffn_post_norm)
    tr = lambda a: jnp.swapaxes(a, 1, 2)
    PB = {nm: (tr(P[nm]) if nm == "w_in" else P[nm]) for nm, _, _ in BIG}
    MB = {nm: (tr(M[nm]) if nm == "w_in" else M[nm]) for nm, _, _ in BIG}
    VB = {nm: (tr(V[nm]) if nm == "w_in" else V[nm]) for nm, _, _ in BIG}
    xi, yi, ci = _coords()
    me = 4 * xi + 2 * yi + ci

    me_arr = me.astype(jnp.int32).reshape(1)

    groups = [tuple(l * NBIG + t for t in tids) for l in range(DEPTH) for _, tids in LAYER_GROUPS]
    cast = lambda i, m=me_arr: _cast_own(i, PB[BIG[i % NBIG][0]], m, f"gather_own_l{i // NBIG}_{BIG[i % NBIG][0]}")
    first = list(groups[0])
    rest = [i for grp in groups[1:] for i in grp]
    sems0, _, lands0, tok_first = _xchg_start(None, [cast(i) for i in first], [tuple(range(len(first)))], None,
                                              _shard_window, me_arr, "gather_start_first", rels=NEAR_RELS, tids=first)
    small_w = _pack([b_gate.reshape(-1), conv_w.reshape(-1)])
    sw_sems, sw_src, sw_land, _ = _xchg_start([small_w], [lax.empty((NDEV,) + small_w.shape, F32)], [(0,)], _whole,
                                              _slot, tok_first, "gather_small_weights_start")
    small_full = {}

    def small_weights(after):
        if not small_full:
            srcs, got = _xchg_wait(sw_sems[0], sw_src, sw_land, [0], after, _whole, _slot, "gather_small_weights_wait")
            flat_all = lax.dynamic_update_slice(got[0], srcs[0][None], (me, 0, 0)).reshape(NDEV, -1)
            nbg, ncw = DEPTH * 3 * 128, DEPTH * 3 * 1024
            small_full["b_gate"] = jnp.transpose(
                flat_all[:, :nbg].reshape(NDEV, DEPTH, 3, 128), (1, 2, 0, 3)).reshape(DEPTH, 3, D)
            small_full["conv_w"] = jnp.transpose(
                flat_all[:, nbg:nbg + ncw].reshape(NDEV, DEPTH, 3, 1024), (1, 2, 0, 3)).reshape(DEPTH, 3, 2 * D_FF)
        return small_full

    where_rest = {tid: k for k, tid in enumerate(rest)}
    me_rest = me_arr + tok_first[0, 0:1].astype(jnp.int32)
    sems1, _, lands1, g_tok = _xchg_start(None, [cast(i, me_rest) for i in rest],
                                          [tuple(where_rest[i] for i in grp) for grp in groups[1:]], None,
                                          _shard_window, lands0[0], "gather_start_rest", rels=NEAR_RELS, tids=rest)
    g_sems = list(sems0) + list(sems1)
    tok0 = g_tok[0:1, 0:1]
    lands_now = [None] * (DEPTH * NBIG)
    for i, a in zip(first + rest, list(lands0) + list(lands1)):
        lands_now[i] = a
    fwd_sems = {}
    fwd_plan = {0: (0,), 1: (1,), 2: (2,), 3: (3, 4, 5)}

    def gather_waiter(gi, l, gname, tids):
        def wait(after):
            if gi in fwd_plan:
                gis = fwd_plan[gi]
                flat = [i for g2 in gis for i in groups[g2]]
                where = {tid: k for k, tid in enumerate(flat)}
                fs, new_lands, ftok = _gather_forward(
                    [g_sems[g2] for g2 in gis], [lands_now[i] for i in flat],
                    [[where[i] for i in groups[g2]] for g2 in gis], flat, after, _shard_window, f"gather_forward_{gi}")
                for g2, s in zip(gis, fs):
                    fwd_sems[g2] = s
                for i, a in zip(flat, new_lands):
                    lands_now[i] = a
                after = ftok
            ids = [l * NBIG + t for t in tids]
            _, got = _xchg_wait(fwd_sems[gi], None, [lands_now[i] for i in ids], ids, after,
                                None, _shard_window, f"gather_wait_l{l}_{gname}", rels=FAR_RELS)
            out = {}
            for t, arr in zip(tids, got):
                nm = BIG[t][0]
                out[nm] = arr
            if gname == "mix":
                sw = small_weights(got[0])
                out["b_gate"] = sw["b_gate"][l]
                out["conv_w"] = sw["conv_w"][l].reshape(3, 2, D_FF)
            return out
        return wait

    pending = [{gname: gather_waiter(l * len(LAYER_GROUPS) + k, l, gname, tids)
                for k, (gname, tids) in enumerate(LAYER_GROUPS)} for l in range(DEPTH)]
    ws = []
    for l in range(DEPTH):
        ws.append(_Weights(dict(
            conv_b=conv_b[l].reshape(2, D_FF), sinks=sinks[l].reshape(1, 8),
            attn_pre_norm=attn_pre_norm[l].reshape(1, D), attn_post_norm=attn_post_norm[l].reshape(1, D),
            ffn_pre_norm=ffn_pre_norm[l].reshape(1, D), ffn_post_norm=ffn_post_norm[l].reshape(1, D)), pending[l]))

    rs = {}

    group_tids = dict(LAYER_GROUPS)

    def start_scatter(l, gname, grads_l):
        tids = group_tids[gname]
        blocks, lands_rs = [], []
        for t in tids:
            nm, ax, ext = BIG[t]
            gfull = grads_l[nm].astype(BF16)
            shp = (NDEV, ext, gfull.shape[1]) if ax == 0 else (NDEV, gfull.shape[0], ext)
            blocks.append(gfull)
            lands_rs.append(lax.empty(shp, BF16))
        local = list(range(len(tids)))
        win = lambda j, ref, k: _shard_window(tids[j], ref, k)
        sems, s_thru, l_thru, tok = _xchg_start(blocks, lands_rs, [tuple(local)], win, _slot, me_arr,
                                                f"scatter_start_l{l}_{gname}")
        rs[(l, gname)] = (sems[0], s_thru, l_thru, win, local)
        return tok[0:1, 0:1]

    loss_tile, grad_x, grads, g_rel = _local_step(x[0], loss_target[0], ws, rel_bias, tok0, start_scatter)

    stack = lambda nm: jnp.stack([grads[l][nm] for l in range(DEPTH)], axis=0)
    small_g = {nm: (g_rel if nm == "rel_bias" else stack(nm)) for nm, _, _, _ in ROWPACK}
    small_repl = jnp.concatenate([_rowpack(small_g, ROWPACK[:N_REPL]), loss_tile], axis=0)
    small_shard = _shard_rows(small_g)

    out_g, out_d, out_m, out_v = {}, {}, {}, {}
    prev = {nm: None for nm, _, _ in BIG}
    todo = [(l, gname) for l in reversed(range(DEPTH)) for gname in ("ffn", "mix", "in")]
    after, small_parts = grad_x, None
    for l, gname in todo:
        if (l, gname) == todo[-1]:
            small_parts = _exchange(
                [small_repl, small_shard],
                [jax.ShapeDtypeStruct((NDEV, ROWS_REPL + 8, LANES), F32), jax.ShapeDtypeStruct((NDEV, ROWS_SHARD, LANES), F32)],
                lambda t, ref, q: ref if t == 0 else ref.at[q], _slot, "exchange_small_grads", after=after)
            after = small_parts[0]
        sems, s_thru, l_thru, win, local = rs[(l, gname)]
        owns, parts = _xchg_wait(sems, s_thru, l_thru, local, after, win, _slot, f"scatter_wait_l{l}_{gname}")
        for t, own, prt in zip(group_tids[gname], owns, parts):
            nm = BIG[t][0]
            rows = SHARD_ROWS.get(nm, PB[nm].shape[1])
            prev[nm] = _adamw(t, prt, own, me_arr, PB[nm], MB[nm], VB[nm], l, prev[nm], rows, f"adamw_{nm}_l{l}")
            after = prev[nm][1]
    for nm, _, _ in BIG:
        out_g[nm], out_d[nm], out_m[nm], out_v[nm] = [tr(a) if nm == "w_in" else a for a in prev[nm]]
    sm_g, sm_d, sm_m, sm_v, loss_all = _small_update(small_parts[0], small_parts[1], _rowpack(P), _rowpack(M),
                                                     _rowpack(V), "small_update")
    loss = loss_all[0, 0]
    for dst, src in ((out_g, sm_g), (out_d, sm_d), (out_m, sm_m), (out_v, sm_v)):
        dst.update(src)

    order = ["rel_bias", "attn_pre_norm", "w_in", "b_gate", "sinks", "w_br_a", "w_br_b", "w_br_c", "w_out",
             "attn_post_norm", "ffn_pre_norm", "w_up", "conv_w", "conv_b", "w_down", "ffn_post_norm"]
    return (loss, grad_x[None], *[out_g[k] for k in order], *[out_d[k] for k in order],
            *[out_m[k] for k in order], *[out_v[k] for k in order])
```

```python
import functools
import math

import numpy as np
import jax
import jax.numpy as jnp
from jax import lax
from jax.experimental import pallas as pl
from jax.experimental.pallas import tpu as pltpu

F32 = jnp.float32
BF16 = jnp.bfloat16

S = 2048
D = 1024
DEPTH = 2
NDEV = 8
HD = 64
BLK = 128
NB = S // BLK
A_GROUPS = ((128, 1), (512, 4), (2048, 16))
NUM_BUCKETS = 32
MAX_DISTANCE = 2048
N_BIAS_HEADS = 20
D_FF = 4096
IN_COLS = 6912
QKV_COLS = 3840
QKV_SLABS = QKV_COLS // 128
GATE_COLS = 3072
EPS = 1e-6
SCALE = HD ** -0.5
NEG = -1e30
LANES = 128

ADAM_LR = 0.001
ADAM_B1 = 0.9
ADAM_B2 = 0.999
ADAM_EPS = 1e-08
ADAM_WD = 0.01
ADAM_STEP = 10

VMEM_LIMIT = 56 * 1024 * 1024
MESH = pl.DeviceIdType.MESH
ANY = pl.BlockSpec(memory_space=pl.ANY)
SMEM = pl.BlockSpec(memory_space=pltpu.SMEM)


def _cp(*sem):
    return pltpu.CompilerParams(dimension_semantics=sem if sem else None, vmem_limit_bytes=VMEM_LIMIT)


def _dot(a, b, ca, cb):
    return lax.dot_general(a, b, (((ca,), (cb,)), ((), ())), preferred_element_type=F32)


def _mm(a, b, *, grid, a_spec, b_spec, out_shape, out_spec, ca, cb, acc_shape, name,
        a_slab=False, b_slab=False, out_slab=False, alias_out=None, after=None):
    nk = grid[2]

    def body(*refs):
        a_ref, b_ref = refs[0], refs[1]
        o_ref, acc_ref = refs[-2], refs[-1]
        k = pl.program_id(2)

        def load(ref, slab):
            if slab:
                return jnp.concatenate([ref[s] for s in range(ref.shape[0])], axis=1).astype(BF16)
            return ref[...].astype(BF16)

        def write(val):
            if out_slab:
                for s in range(o_ref.shape[0]):
                    o_ref[s] = val[:, s * LANES:(s + 1) * LANES].astype(o_ref.dtype)
            else:
                o_ref[...] = val.astype(o_ref.dtype)

        d = _dot(load(a_ref, a_slab), load(b_ref, b_slab), ca, cb)
        if nk == 1:
            write(d)
        elif direct:
            @pl.when(k == 0)
            def _():
                o_ref[...] = d

            @pl.when(k > 0)
            def _():
                o_ref[...] += d
        else:
            @pl.when(k == 0)
            def _():
                acc_ref[...] = d

            if nk > 2:
                @pl.when((k > 0) & (k < nk - 1))
                def _():
                    acc_ref[...] += d

            @pl.when(k == nk - 1)
            def _():
                write(acc_ref[...] + d)

    direct = (not out_slab) and out_shape.dtype == F32
    if nk == 1 or direct:
        acc_shape = (8, LANES)
    in_specs = [a_spec, b_spec]
    args = [a, b]
    aliases = {}
    if alias_out is not None:
        in_specs.append(ANY)
        args.append(alias_out)
        aliases = {2: 0}
    if after is not None:
        in_specs.append(ANY)
        args.append(after)
    return pl.pallas_call(
        body, grid=grid, in_specs=in_specs, out_specs=out_spec, out_shape=out_shape,
        scratch_shapes=[pltpu.VMEM(acc_shape, F32)], input_output_aliases=aliases,
        compiler_params=_cp("parallel", "parallel", "arbitrary"), name=name)(*args)


def _mm_nn(a, b, out_dtype, tm, tn, tk, name):
    m, kk = a.shape
    n = b.shape[1]
    return _mm(a, b, grid=(m // tm, n // tn, kk // tk),
               a_spec=pl.BlockSpec((tm, tk), lambda i, j, k: (i, k)),
               b_spec=pl.BlockSpec((tk, tn), lambda i, j, k: (k, j)),
               out_shape=jax.ShapeDtypeStruct((m, n), out_dtype),
               out_spec=pl.BlockSpec((tm, tn), lambda i, j, k: (i, j)),
               ca=1, cb=0, acc_shape=(tm, tn), name=name)


def _mm_nt(a, b, out_dtype, tm, tn, tk, name):
    m, kk = a.shape
    n = b.shape[0]
    return _mm(a, b, grid=(m // tm, n // tn, kk // tk),
               a_spec=pl.BlockSpec((tm, tk), lambda i, j, k: (i, k)),
               b_spec=pl.BlockSpec((tn, tk), lambda i, j, k: (j, k)),
               out_shape=jax.ShapeDtypeStruct((m, n), out_dtype),
               out_spec=pl.BlockSpec((tm, tn), lambda i, j, k: (i, j)),
               ca=1, cb=1, acc_shape=(tm, tn), name=name)


def _mm_tn(a, b, out_dtype, tm, tn, tk, name):
    kk, m = a.shape
    n = b.shape[1]
    return _mm(a, b, grid=(m // tm, n // tn, kk // tk),
               a_spec=pl.BlockSpec((tk, tm), lambda i, j, k: (k, i)),
               b_spec=pl.BlockSpec((tk, tn), lambda i, j, k: (k, j)),
               out_shape=jax.ShapeDtypeStruct((m, n), out_dtype),
               out_spec=pl.BlockSpec((tm, tn), lambda i, j, k: (i, j)),
               ca=0, cb=0, acc_shape=(tm, tn), name=name)


ROW_TILE = 256


def _rms(x, g):
    r = lax.rsqrt(jnp.mean(x * x, axis=-1, keepdims=True) + EPS)
    return x * r * g


def _prenorm(x, g, name):
    def body(x_ref, g_ref, o_ref):
        o_ref[...] = _rms(x_ref[...], g_ref[...]).astype(BF16)

    return pl.pallas_call(
        body, grid=(S // ROW_TILE,),
        in_specs=[pl.BlockSpec((ROW_TILE, D), lambda i: (i, 0)), pl.BlockSpec((1, D), lambda i: (0, 0))],
        out_specs=pl.BlockSpec((ROW_TILE, D), lambda i: (i, 0)),
        out_shape=jax.ShapeDtypeStruct((S, D), BF16), compiler_params=_cp("parallel"), name=name)(x, g)


def _postnorm_res(x, f, g_post, g_next, name):
    def body(x_ref, f_ref, gp_ref, gn_ref, xo_ref, ho_ref):
        xn = x_ref[...] + _rms(f_ref[...], gp_ref[...])
        xo_ref[...] = xn
        ho_ref[...] = _rms(xn, gn_ref[...]).astype(BF16)

    row = pl.BlockSpec((ROW_TILE, D), lambda i: (i, 0))
    vec = pl.BlockSpec((1, D), lambda i: (0, 0))
    return pl.pallas_call(
        body, grid=(S // ROW_TILE,), in_specs=[row, row, vec, vec], out_specs=[row, row],
        out_shape=[jax.ShapeDtypeStruct((S, D), F32), jax.ShapeDtypeStruct((S, D), BF16)],
        compiler_params=_cp("parallel"), name=name)(x, f, g_post, g_next)


def _norm_bwd(f, g, dys, res, out_dtype, name):
    ndy = len(dys)
    has_res = res is not None

    def body(*refs):
        f_ref, g_ref = refs[0], refs[1]
        dy_refs = refs[2:2 + ndy]
        res_ref = refs[2 + ndy] if has_res else None
        o_ref, dg_ref = refs[-2], refs[-1]
        fv = f_ref[...]
        dy = dy_refs[0][...].astype(F32)
        for r in dy_refs[1:]:
            dy = dy + r[...].astype(F32)
        r = lax.rsqrt(jnp.mean(fv * fv, axis=-1, keepdims=True) + EPS)
        n = fv * r
        dn = dy * g_ref[...]
        df = r * (dn - n * jnp.mean(dn * n, axis=-1, keepdims=True))
        if has_res:
            df = df + res_ref[...]
        o_ref[...] = df.astype(out_dtype)

        @pl.when(pl.program_id(0) == 0)
        def _():
            dg_ref[...] = jnp.zeros((1, D), F32)

        dg_ref[...] += jnp.sum(dy * n, axis=0, keepdims=True)

    row = pl.BlockSpec((ROW_TILE, D), lambda i: (i, 0))
    vec = pl.BlockSpec((1, D), lambda i: (0, 0))
    in_specs = [row, vec] + [row] * ndy + ([row] if has_res else [])
    args = [f, g] + list(dys) + ([res] if has_res else [])
    return pl.pallas_call(
        body, grid=(S // ROW_TILE,), in_specs=in_specs, out_specs=[row, vec],
        out_shape=[jax.ShapeDtypeStruct((S, D), out_dtype), jax.ShapeDtypeStruct((1, D), F32)],
        compiler_params=_cp("arbitrary"), name=name)(*args)


def _rms_bwd_rows(fv, g, dy):
    r = lax.rsqrt(jnp.mean(fv * fv, axis=-1, keepdims=True) + EPS)
    n = fv * r
    dn = dy * g
    return r * (dn - n * jnp.mean(dn * n, axis=-1, keepdims=True)), dy * n


def _norm_bwd_chain(f1, g1, dys, res, f2, g2, name):
    ndy = len(dys)

    def body(*refs):
        f1_ref, g1_ref = refs[0], refs[1]
        dy_refs = refs[2:2 + ndy]
        res_ref, f2_ref, g2_ref = refs[2 + ndy:5 + ndy]
        o1_ref, o2_ref, dg1_ref, dg2_ref = refs[-4:]
        dy = dy_refs[0][...].astype(F32)
        for r in dy_refs[1:]:
            dy = dy + r[...].astype(F32)
        df1, c1 = _rms_bwd_rows(f1_ref[...], g1_ref[...], dy)
        out1 = df1 + res_ref[...]
        o1_ref[...] = out1
        df2, c2 = _rms_bwd_rows(f2_ref[...], g2_ref[...], out1)
        o2_ref[...] = df2.astype(BF16)

        @pl.when(pl.program_id(0) == 0)
        def _():
            dg1_ref[...] = jnp.zeros((1, D), F32)
            dg2_ref[...] = jnp.zeros((1, D), F32)

        dg1_ref[...] += jnp.sum(c1, axis=0, keepdims=True)
        dg2_ref[...] += jnp.sum(c2, axis=0, keepdims=True)

    row = pl.BlockSpec((ROW_TILE, D), lambda i: (i, 0))
    vec = pl.BlockSpec((1, D), lambda i: (0, 0))
    return pl.pallas_call(
        body, grid=(S // ROW_TILE,), in_specs=[row, vec] + [row] * ndy + [row, row, vec],
        out_specs=[row, row, vec, vec],
        out_shape=[jax.ShapeDtypeStruct((S, D), F32), jax.ShapeDtypeStruct((S, D), BF16),
                   jax.ShapeDtypeStruct((1, D), F32), jax.ShapeDtypeStruct((1, D), F32)],
        compiler_params=_cp("arbitrary"), name=name)(f1, g1, *dys, res, f2, g2)


def _loss_head(x, f, g, target, name):
    def body(x_ref, f_ref, g_ref, t_ref, dy_ref, l_ref, df_ref, dg_ref):
        fv = f_ref[...]
        e = x_ref[...] + _rms(fv, g_ref[...]) - t_ref[...]
        dy = e * (1.0 / D)
        dy_ref[...] = dy
        df, c = _rms_bwd_rows(fv, g_ref[...], dy)
        df_ref[...] = df.astype(BF16)

        @pl.when(pl.program_id(0) == 0)
        def _():
            l_ref[...] = jnp.zeros((8, LANES), F32)
            dg_ref[...] = jnp.zeros((1, D), F32)

        l_ref[...] += jnp.sum(e * e) * (0.5 / D)
        dg_ref[...] += jnp.sum(c, axis=0, keepdims=True)

    row = pl.BlockSpec((ROW_TILE, D), lambda i: (i, 0))
    vec = pl.BlockSpec((1, D), lambda i: (0, 0))
    return pl.pallas_call(
        body, grid=(S // ROW_TILE,), in_specs=[row, row, vec, row],
        out_specs=[row, pl.BlockSpec((8, LANES), lambda i: (0, 0)), row, vec],
        out_shape=[jax.ShapeDtypeStruct((S, D), F32), jax.ShapeDtypeStruct((8, LANES), F32),
                   jax.ShapeDtypeStruct((S, D), BF16), jax.ShapeDtypeStruct((1, D), F32)],
        compiler_params=_cp("arbitrary"), name=name)(x, f, g, target)


def _bucket_tiles():
    a = np.arange(BLK)[:, None]
    b = np.arange(2 * BLK)[None, :]
    dist = a + BLK - b
    out = np.zeros((4, 2, BLK, 2 * BLK), np.int32)
    cfg = [(w // d, d) for w, d in A_GROUPS] + [(BLK - 1, 1)]
    for gi, (max_dist, d) in enumerate(cfg):
        band = (dist >= 0) & (dist <= max_dist)
        tok = np.maximum(dist, 0) * d
        nf = np.maximum(tok, 1).astype(np.float32)
        max_exact = NUM_BUCKETS // 2
        large = max_exact + (np.log(nf / np.float32(max_exact)) / np.float32(math.log(MAX_DISTANCE / max_exact))
                             * np.float32(NUM_BUCKETS - max_exact)).astype(np.int32)
        large = np.minimum(large, NUM_BUCKETS - 1)
        bkt = np.where(tok < max_exact, tok, large).astype(np.int32)
        full = np.where(band, bkt, -1)
        out[gi, 1] = full
        out[gi, 0] = np.where(b >= BLK, full, -1)
    return out


def _bias_tiles(rel_bias, buckets, name):
    def body(tab_ref, bkt_ref, o_ref):
        h = pl.program_id(0)
        bkt = bkt_ref[...]
        acc = jnp.zeros(bkt.shape, F32)
        for bb in range(NUM_BUCKETS):
            acc = jnp.where(bkt == bb, tab_ref[bb, h], acc)
        o_ref[...] = jnp.where(bkt < 0, NEG, acc)

    return pl.pallas_call(
        body, grid=(N_BIAS_HEADS,),
        in_specs=[SMEM, pl.BlockSpec((None, 2, BLK, 2 * BLK), lambda h: (jnp.minimum(h // 4, 3), 0, 0, 0))],
        out_specs=pl.BlockSpec((None, 2, BLK, 2 * BLK), lambda h: (h, 0, 0, 0)),
        out_shape=jax.ShapeDtypeStruct((N_BIAS_HEADS, 2, BLK, 2 * BLK), F32),
        compiler_params=_cp("arbitrary"), name=name)(rel_bias, buckets)


def _bias_grad(gs, buckets, name):
    ng = len(gs)

    def body(*refs):
        g_refs = refs[:ng]
        bkt_ref, o_ref = refs[ng], refs[ng + 1]
        h = pl.program_id(0)
        g = g_refs[0][...]
        for r in g_refs[1:]:
            g = g + r[...]
        bkt = bkt_ref[...]
        row = lax.broadcasted_iota(jnp.int32, (NUM_BUCKETS, LANES), 0)
        lane = lax.broadcasted_iota(jnp.int32, (NUM_BUCKETS, LANES), 1)

        @pl.when(h == 0)
        def _():
            o_ref[...] = jnp.zeros((NUM_BUCKETS, LANES), F32)

        acc = o_ref[...]
        for bb in range(NUM_BUCKETS):
            s = jnp.sum(jnp.where(bkt == bb, g, 0.0))
            acc = jnp.where((row == bb) & (lane == h), s, acc)
        o_ref[...] = acc

    g_spec = pl.BlockSpec((None, BLK, 2 * BLK), lambda h: (h, 0, 0))
    return pl.pallas_call(
        body, grid=(N_BIAS_HEADS,),
        in_specs=[g_spec] * ng + [pl.BlockSpec((None, None, BLK, 2 * BLK), lambda h: (jnp.minimum(h // 4, 3), 1, 0, 0))],
        out_specs=pl.BlockSpec((NUM_BUCKETS, LANES), lambda h: (0, 0)),
        out_shape=jax.ShapeDtypeStruct((NUM_BUCKETS, LANES), F32),
        compiler_params=_cp("arbitrary"), name=name)(*gs, buckets)


def _to_class_major(src_ref, dst_refs, d, fn=None):
    ln = S // d
    for r in range(d):
        v = src_ref[pl.ds(r, ln, stride=d), :] if d > 1 else src_ref[...]
        outs = fn(v) if fn is not None else (v,) * len(dst_refs)
        for dst, o in zip(dst_refs, outs):
            dst[pl.ds(r * ln, ln), :] = o.astype(dst.dtype)


def _head_masks(rows):
    lane = lax.broadcasted_iota(jnp.int32, (rows, LANES), 1)
    return lane < HD, lane >= HD


def _split_heads(v):
    m0, m1 = _head_masks(v.shape[0])
    return jnp.where(m0, v, 0.0), jnp.where(m1, v, 0.0)


def _dup_head(v, hi):
    m0, _ = _head_masks(v.shape[0])
    r = pltpu.roll(v, HD, 1)
    return jnp.where(m0, jnp.where(hi, r, v), jnp.where(hi, v, r))


def _block_rows(b, d):
    nbc = NB // d
    i = b % nbc
    r = b // nbc
    has_prev = (i > 0).astype(jnp.int32)
    prev = pl.multiple_of(jnp.maximum(b - 1, 0) * BLK, BLK)
    nat = i * (BLK * d) + r
    return has_prev, prev, nat


def _lane_halves(v0, v1):
    lane = lax.broadcasted_iota(jnp.int32, (v0.shape[0], LANES), 1)
    return jnp.where(lane < HD, v0, v1)


def _band_fwd(proj, bias, sinks, *, d, q0, k0, v0, npairs, bias0, shared_kv, name):
    def body(sink_ref, q_ref, k_ref, v_ref, b_ref, num_ref, st_ref, qz0, qz1, ks, vs):
        p = pl.program_id(0)
        kv = (lambda v: (_dup_head(v, p >= 2),)) if shared_kv else None
        _to_class_major(q_ref, (qz0, qz1), d, lambda v: _split_heads(v * SCALE))
        _to_class_major(k_ref, (ks,), d, kv)
        _to_class_major(v_ref, (vs,), d, kv)
        lane = lax.broadcasted_iota(jnp.int32, (BLK, LANES), 1)

        def blk(b, carry):
            has_prev, prev, nat = _block_rows(b, d)
            cur = pl.multiple_of(b * BLK, BLK)
            k2 = jnp.concatenate([ks[pl.ds(prev, BLK), :], ks[pl.ds(cur, BLK), :]], axis=0)
            v2 = jnp.concatenate([vs[pl.ds(prev, BLK), :], vs[pl.ds(cur, BLK), :]], axis=0)
            nums, ms, ls = [], [], []
            for hh, qz in enumerate((qz0, qz1)):
                z = _dot(qz[pl.ds(cur, BLK), :], k2, 1, 1) + b_ref[hh, has_prev]
                m = jnp.max(z, axis=1, keepdims=True)
                e = jnp.exp(z - m)
                l = jnp.sum(e, axis=1, keepdims=True)
                num = _dot(e.astype(BF16), v2, 1, 0)
                if shared_kv:
                    sink = sink_ref[0, 2 * p + hh]
                    mx = jnp.maximum(m, sink)
                    c = jnp.exp(m - mx)
                    zden = l * c + jnp.exp(sink - mx)
                    num = num * (c / zden)
                    m = mx + jnp.log(zden)
                ls.append(l)
                ms.append(m)
                nums.append(num)
            num_t = jnp.where(lane < HD, nums[0], nums[1])
            if shared_kv:
                st_t = jnp.where(lane < HD, ms[0], ms[1])
            else:
                st_t = jnp.where(lane < 32, ms[0], jnp.where(lane < 64, ls[0], jnp.where(lane < 96, ms[1], ls[1])))
            if d > 1:
                num_ref[pl.ds(nat, BLK, stride=d), :] = num_t
                st_ref[pl.ds(nat, BLK, stride=d), :] = st_t
            else:
                num_ref[pl.ds(cur, BLK), :] = num_t
                st_ref[pl.ds(cur, BLK), :] = st_t
            return carry

        lax.fori_loop(0, NB, blk, 0, unroll=8)

    slab = lambda off, per_pair: pl.BlockSpec((None, S, LANES), (lambda p: (off + p, 0, 0)) if per_pair else (lambda p: (off, 0, 0)))
    out = pl.BlockSpec((None, S, LANES), lambda p: (p, 0, 0))
    return pl.pallas_call(
        body, grid=(npairs,),
        in_specs=[SMEM, slab(q0, True), slab(k0, not shared_kv), slab(v0, not shared_kv),
                  pl.BlockSpec((None, 2, 2, BLK, 2 * BLK), lambda p: (bias0 + p, 0, 0, 0, 0))],
        out_specs=[out, out],
        out_shape=[jax.ShapeDtypeStruct((npairs, S, LANES), F32)] * 2,
        scratch_shapes=[pltpu.VMEM((S, LANES), BF16)] * 4,
        compiler_params=_cp("arbitrary"), name=name)(sinks, proj, proj, proj, bias)


def _combine_a(nums, stats, name):
    rt = 512

    def body(n0, n1, n2, s0, s1, s2, o_ref, l_ref):
        n_refs, s_refs = (n0, n1, n2), (s0, s1, s2)
        outs, lses = [], []
        for hh in range(2):
            ms = [s[:, 64 * hh:64 * hh + 1] for s in s_refs]
            ls = [s[:, 64 * hh + 32:64 * hh + 33] for s in s_refs]
            mx = jnp.maximum(jnp.maximum(ms[0], ms[1]), ms[2])
            cs = [jnp.exp(m - mx) for m in ms]
            z = cs[0] * ls[0] + cs[1] * ls[1] + cs[2] * ls[2]
            acc = cs[0] * n_refs[0][:, hh * HD:(hh + 1) * HD]
            acc = acc + cs[1] * n_refs[1][:, hh * HD:(hh + 1) * HD]
            acc = acc + cs[2] * n_refs[2][:, hh * HD:(hh + 1) * HD]
            outs.append(acc / z)
            lses.append(mx + jnp.log(z))
        o_ref[...] = jnp.concatenate(outs, axis=1)
        l_ref[...] = _lane_halves(lses[0], lses[1])

    spec = pl.BlockSpec((None, rt, LANES), lambda p, i: (p, i, 0))
    return pl.pallas_call(
        body, grid=(2, S // rt), in_specs=[spec] * 6, out_specs=[spec, spec],
        out_shape=[jax.ShapeDtypeStruct((2, S, LANES), F32)] * 2,
        compiler_params=_cp("parallel", "parallel"), name=name)(*nums, *stats)


def _band_bwd(proj, bias, o, do, lse, sinks, dqkv, *, d, q0, k0, v0, npairs, bias0, shared_kv, name):
    def body(sink_ref, q_ref, k_ref, v_ref, b_ref, o_ref, do_ref, lse_ref, dqkv_in, dqkv_ref, g_ref, ds_ref,
             qz0, qz1, ks, vs, doz0, doz1, ls0, ls1, dls0, dls1, stage, dq_nat, dk_cm, dv_cm, kv_nat, dk_acc, dv_acc,
             obuf, osem):
        p = pl.program_id(0)
        dq_ref, dk_ref, dv_ref = obuf.at[0], obuf.at[1], obuf.at[2]
        m0, m1 = _head_masks(S)
        kk = lax.broadcasted_iota(jnp.int32, (2 * LANES, LANES), 0) % LANES
        ll = lax.broadcasted_iota(jnp.int32, (2 * LANES, LANES), 1)
        hi, lo = _split2(do_ref[...] * o_ref[...])
        dl = _dot(jnp.concatenate([hi, lo], axis=1), ((kk < HD) == (ll < HD)).astype(BF16), 1, 0)
        if shared_kv:
            row8 = lax.broadcasted_iota(jnp.int32, (8, LANES), 0)
            lane8 = lax.broadcasted_iota(jnp.int32, (8, LANES), 1)
            sinkv = jnp.where(m0, sink_ref[0, 2 * p], sink_ref[0, 2 * p + 1])
            contrib = jnp.exp(sinkv - lse_ref[...]) * dl
            t = jnp.zeros((8, LANES), F32)
            for hh, mh in enumerate((m0, m1)):
                dsink = -jnp.sum(jnp.where(mh, contrib, 0.0)) * (1.0 / HD)
                t = jnp.where((row8 == 0) & (lane8 == hh), dsink, t)
            ds_ref[...] = t
        else:
            ds_ref[...] = jnp.zeros((8, LANES), F32)
        kv = (lambda v: (_dup_head(v, p >= 2),)) if shared_kv else None
        _to_class_major(q_ref, (qz0, qz1), d, lambda v: _split_heads(v * SCALE))
        _to_class_major(k_ref, (ks,), d, kv)
        _to_class_major(v_ref, (vs,), d, kv)
        _to_class_major(do_ref, (doz0, doz1), d, _split_heads)
        def spread(v):
            a0, a1 = _head_masks(v.shape[0])
            r = pltpu.roll(v, HD, 1)
            return jnp.where(a0, v, r), jnp.where(a1, v, r)

        _to_class_major(lse_ref, (ls0, ls1), d, spread)
        stage[...] = dl
        _to_class_major(stage, (dls0, dls1), d, spread)

        dk_cm[...] = jnp.zeros((S, LANES), F32)
        dv_cm[...] = jnp.zeros((S, LANES), F32)
        g_ref[...] = jnp.zeros((2, BLK, 2 * BLK), F32)
        lane = lax.broadcasted_iota(jnp.int32, (BLK, LANES), 1)

        def blk(b, carry):
            has_prev, prev, nat = _block_rows(b, d)
            cur = pl.multiple_of(b * BLK, BLK)
            k2 = jnp.concatenate([ks[pl.ds(prev, BLK), :], ks[pl.ds(cur, BLK), :]], axis=0)
            v2 = jnp.concatenate([vs[pl.ds(prev, BLK), :], vs[pl.ds(cur, BLK), :]], axis=0)
            dqs, dks, dvs = [], [], []
            for hh, (qz, doz, lsr, dlr) in enumerate(((qz0, doz0, ls0, dls0), (qz1, doz1, ls1, dls1))):
                qb = qz[pl.ds(cur, BLK), :]
                dob = doz[pl.ds(cur, BLK), :]
                lb = lsr[pl.ds(cur, BLK), :]
                dlb = dlr[pl.ds(cur, BLK), :]
                z = _dot(qb, k2, 1, 1) + b_ref[hh, has_prev]
                pr = jnp.exp(z - jnp.concatenate([lb, lb], axis=1))
                dp = _dot(dob, v2, 1, 1)
                dz = pr * (dp - jnp.concatenate([dlb, dlb], axis=1))
                g_ref[hh] += dz
                dzb = dz.astype(BF16)
                dqs.append(_dot(dzb, k2, 1, 0))
                dks.append(_dot(dzb, qb, 0, 0))
                dvs.append(_dot(pr.astype(BF16), dob, 0, 0))
            dq_t = jnp.where(lane < HD, dqs[0], dqs[1]) * SCALE
            dk_t = dks[0] + dks[1]
            dv_t = dvs[0] + dvs[1]
            dk_cm[pl.ds(prev, BLK), :] += dk_t[:BLK]
            dk_cm[pl.ds(cur, BLK), :] += dk_t[BLK:]
            dv_cm[pl.ds(prev, BLK), :] += dv_t[:BLK]
            dv_cm[pl.ds(cur, BLK), :] += dv_t[BLK:]
            if d > 1:
                dq_nat[pl.ds(nat, BLK, stride=d), :] = dq_t
            else:
                dq_nat[pl.ds(cur, BLK), :] = dq_t
            return carry

        lax.fori_loop(0, NB, blk, 0, unroll=8)
        dq_ref[...] = dq_nat[...].astype(BF16)

        def from_class_major(src, dst_ref):
            if d == 1:
                dst_ref[...] = src[...].astype(BF16)
            else:
                ln = S // d
                for r in range(d):
                    kv_nat[pl.ds(r, ln, stride=d), :] = src[pl.ds(r * ln, ln), :]
                dst_ref[...] = kv_nat[...].astype(BF16)

        def put(i, slab):
            return pltpu.make_async_copy(obuf.at[i], dqkv_ref.at[slab], osem.at[i])

        put(0, q0 + p).start()
        if not shared_kv:
            from_class_major(dk_cm, dk_ref)
            from_class_major(dv_cm, dv_ref)
            put(1, k0 + p).start()
            put(2, v0 + p).start()
            put(1, k0 + p).wait()
            put(2, v0 + p).wait()
        else:
            @pl.when(p == 0)
            def _():
                dk_acc[...] = jnp.zeros((S, LANES), F32)
                dv_acc[...] = jnp.zeros((S, LANES), F32)

            mine = m1 == (p >= 2)
            for cm, acc in ((dk_cm, dk_acc), (dv_cm, dv_acc)):
                val = cm[...]
                acc[...] += jnp.where(mine, val + pltpu.roll(val, HD, 1), 0.0)

            @pl.when(p == npairs - 1)
            def _():
                from_class_major(dk_acc, dk_ref)
                from_class_major(dv_acc, dv_ref)
                put(1, k0).start()
                put(2, v0).start()
                put(1, k0).wait()
                put(2, v0).wait()

        put(0, q0 + p).wait()

    slab = lambda off, per_pair: pl.BlockSpec((None, S, LANES), (lambda p: (off + p, 0, 0)) if per_pair else (lambda p: (off, 0, 0)))
    pair = pl.BlockSpec((None, S, LANES), lambda p: (p, 0, 0))
    return pl.pallas_call(
        body, grid=(npairs,),
        in_specs=[SMEM, slab(q0, True), slab(k0, not shared_kv), slab(v0, not shared_kv),
                  pl.BlockSpec((None, 2, 2, BLK, 2 * BLK), lambda p: (bias0 + p, 0, 0, 0, 0)),
                  pair, pair, pair, ANY],
        out_specs=[ANY,
                   pl.BlockSpec((None, 2, BLK, 2 * BLK), lambda p: (p, 0, 0, 0)),
                   pl.BlockSpec((None, 8, LANES), lambda p: (p, 0, 0))],
        out_shape=[jax.ShapeDtypeStruct(dqkv.shape, BF16),
                   jax.ShapeDtypeStruct((npairs, 2, BLK, 2 * BLK), F32),
                   jax.ShapeDtypeStruct((npairs, 8, LANES), F32)],
        scratch_shapes=[pltpu.VMEM((S, LANES), BF16)] * 6 + [pltpu.VMEM((S, LANES), F32)] * 11
        + [pltpu.VMEM((3, S, LANES), BF16), pltpu.SemaphoreType.DMA((3,))],
        input_output_aliases={8: 0},
        compiler_params=_cp("arbitrary"), name=name)(sinks, proj, proj, proj, bias, o, do, lse, dqkv)


KC = 512
NSUB = KC // BLK
QB = 512
QPG = KC // QB


def _split2(x):
    hi = x.astype(BF16)
    lo = (x - hi.astype(F32)).astype(BF16)
    return hi, lo


def _tri_ones(cmp):
    jj = lax.broadcasted_iota(jnp.int32, (2 * BLK, BLK), 0) % BLK
    ss = lax.broadcasted_iota(jnp.int32, (2 * BLK, BLK), 1)
    return jnp.concatenate([cmp(jj, ss).astype(BF16), jnp.ones((2 * BLK, BLK), BF16)], axis=1)


def _sub_sums(x, tri1):
    n = x.shape[0]
    st = jnp.concatenate([x[:, s * BLK:(s + 1) * BLK] for s in range(NSUB)], axis=0)
    hi, lo = _split2(st)
    r = _dot(jnp.concatenate([hi, lo], axis=1), tri1, 1, 0)
    return ([r[s * n:(s + 1) * n, :BLK] for s in range(NSUB)], [r[s * n:(s + 1) * n, BLK:] for s in range(NSUB)])


def _log_sig_pair(z):
    lb = jnp.minimum(z, 0.0) - jnp.log1p(jnp.exp(-jnp.abs(z)))
    return lb, lb - z


QGROUPS = NB // NSUB


def _stick_fwd(proj, *, q0, k0, v0, name):
    def body(q_ref, k_ref, v_ref, o_ref, t_ref, qs, ks, vs):
        qs[...] = (q_ref[...] * SCALE).astype(BF16)
        ks[...] = k_ref[...].astype(BF16)
        vs[...] = v_ref[...].astype(BF16)
        tri1 = _tri_ones(lambda j, s: j > s)
        col = lax.broadcasted_iota(jnp.int32, (QB, KC), 1)
        rowi = lax.broadcasted_iota(jnp.int32, (QB, KC), 0)

        for qg in range(QGROUPS):
            def qblock(ii, carry0, qg=qg):
                t0 = pl.multiple_of((qg * QPG + ii) * QB, QB)
                qb = qs[pl.ds(t0, QB), :]
                accs = [jnp.zeros((QB, HD), F32)] * 2
                runs = [jnp.zeros((QB, BLK), F32)] * 2
                for c in reversed(range(qg + 1)):
                    s0 = c * KC
                    diag = c == qg
                    before = (s0 + col) < (t0 + rowi) if diag else None
                    for hh in range(2):
                        kh = ks[s0:s0 + KC, hh * HD:(hh + 1) * HD]
                        vh = vs[s0:s0 + KC, hh * HD:(hh + 1) * HD]
                        lb, lk = _log_sig_pair(_dot(qb[:, hh * HD:(hh + 1) * HD], kh, 1, 1))
                        if diag:
                            lk = jnp.where(before, lk, 0.0)
                        suf, tot = _sub_sums(lk, tri1)
                        ws, run = [], runs[hh]
                        for s in reversed(range(NSUB)):
                            ws.append(jnp.exp(lb[:, s * BLK:(s + 1) * BLK] + suf[s] + run))
                            run = run + tot[s]
                        w = jnp.concatenate(ws[::-1], axis=1)
                        if diag:
                            w = jnp.where(before, w, 0.0)
                        accs[hh] = accs[hh] + _dot(w.astype(BF16), vh, 1, 0)
                        runs[hh] = run
                o_ref[pl.ds(t0, QB), :] = jnp.concatenate(accs, axis=1)
                t_ref[pl.ds(t0, QB), :] = _lane_halves(runs[0], runs[1])
                return carry0

            lax.fori_loop(0, QPG, qblock, 0)

    slab = lambda off: pl.BlockSpec((None, S, LANES), lambda p: (off + p, 0, 0))
    out = pl.BlockSpec((None, S, LANES), lambda p: (p, 0, 0))
    return pl.pallas_call(
        body, grid=(2,), in_specs=[slab(q0), slab(k0), slab(v0)], out_specs=[out, out],
        out_shape=[jax.ShapeDtypeStruct((2, S, LANES), F32)] * 2,
        scratch_shapes=[pltpu.VMEM((S, LANES), BF16)] * 3,
        compiler_params=_cp("arbitrary"), name=name)(proj, proj, proj)


def _stick_bwd(proj, do, tot, dqkv, *, q0, k0, v0, name):
    def body(q_ref, k_ref, v_ref, do_ref, t_ref, dqkv_in, dqkv_ref, qs, ks, vs, dos, dk_acc, dv_acc, obuf, osem):
        p = pl.program_id(0)
        dq_ref, dk_ref, dv_ref = obuf.at[0], obuf.at[1], obuf.at[2]
        qs[...] = (q_ref[...] * SCALE).astype(BF16)
        ks[...] = k_ref[...].astype(BF16)
        vs[...] = v_ref[...].astype(BF16)
        dos[...] = do_ref[...].astype(BF16)
        dk_acc[...] = jnp.zeros((2, S, HD), F32)
        dv_acc[...] = jnp.zeros((2, S, HD), F32)
        tri_inc = _tri_ones(lambda j, s: j <= s)
        tri_exc = _tri_ones(lambda j, s: j < s)
        col = lax.broadcasted_iota(jnp.int32, (QB, KC), 1)
        rowi = lax.broadcasted_iota(jnp.int32, (QB, KC), 0)

        for qg in range(QGROUPS):
            def qblock(ii, carry0, qg=qg):
                t0 = pl.multiple_of((qg * QPG + ii) * QB, QB)
                qb = qs[pl.ds(t0, QB), :]
                dob = dos[pl.ds(t0, QB), :]
                tb = t_ref[pl.ds(t0, QB), :]
                dqs = [jnp.zeros((QB, HD), F32)] * 2
                pruns = [jnp.zeros((QB, BLK), F32)] * 2
                eruns = [jnp.zeros((QB, BLK), F32)] * 2
                for c in range(qg + 1):
                    s0 = c * KC
                    diag = c == qg
                    before = (s0 + col) < (t0 + rowi) if diag else None
                    for hh in range(2):
                        qh = qb[:, hh * HD:(hh + 1) * HD]
                        doh = dob[:, hh * HD:(hh + 1) * HD]
                        tt = tb[:, 64 * hh:64 * hh + 1]
                        kh = ks[s0:s0 + KC, hh * HD:(hh + 1) * HD]
                        vh = vs[s0:s0 + KC, hh * HD:(hh + 1) * HD]
                        lb, lk = _log_sig_pair(_dot(qh, kh, 1, 1))
                        if diag:
                            lk = jnp.where(before, lk, 0.0)
                        pin, ptot = _sub_sums(lk, tri_inc)
                        ws, prun = [], pruns[hh]
                        for s in range(NSUB):
                            ws.append(jnp.exp(lb[:, s * BLK:(s + 1) * BLK] + (tt - (pin[s] + prun))))
                            prun = prun + ptot[s]
                        w = jnp.concatenate(ws, axis=1)
                        if diag:
                            w = jnp.where(before, w, 0.0)
                        e = w * _dot(doh, vh, 1, 1)
                        pex, etot = _sub_sums(e, tri_exc)
                        cs, erun = [], eruns[hh]
                        for s in range(NSUB):
                            cs.append(pex[s] + erun)
                            erun = erun + etot[s]
                        sig = jnp.exp(lb)
                        dz = e * (1.0 - sig) - jnp.concatenate(cs, axis=1) * sig
                        if diag:
                            dz = jnp.where(before, dz, 0.0)
                        dz = dz.astype(BF16)
                        dqs[hh] = dqs[hh] + _dot(dz, kh, 1, 0)
                        dk_acc[hh, s0:s0 + KC, :] += _dot(dz, qh, 0, 0)
                        dv_acc[hh, s0:s0 + KC, :] += _dot(w.astype(BF16), doh, 0, 0)
                        pruns[hh], eruns[hh] = prun, erun
                dq_ref[pl.ds(t0, QB), :] = (jnp.concatenate(dqs, axis=1) * SCALE).astype(BF16)
                return carry0

            lax.fori_loop(0, QPG, qblock, 0)
        dk_ref[...] = jnp.concatenate([dk_acc[0], dk_acc[1]], axis=1).astype(BF16)
        dv_ref[...] = jnp.concatenate([dv_acc[0], dv_acc[1]], axis=1).astype(BF16)
        puts = [pltpu.make_async_copy(obuf.at[i], dqkv_ref.at[off + p], osem.at[i]) for i, off in enumerate((q0, k0, v0))]
        for cp in puts:
            cp.start()
        for cp in puts:
            cp.wait()

    slab = lambda off: pl.BlockSpec((None, S, LANES), lambda p: (off + p, 0, 0))
    pair = pl.BlockSpec((None, S, LANES), lambda p: (p, 0, 0))
    return pl.pallas_call(
        body, grid=(2,), in_specs=[slab(q0), slab(k0), slab(v0), pair, pair, ANY], out_specs=ANY,
        out_shape=jax.ShapeDtypeStruct(dqkv.shape, BF16),
        scratch_shapes=[pltpu.VMEM((S, LANES), BF16)] * 4 + [pltpu.VMEM((2, S, HD), F32)] * 2
        + [pltpu.VMEM((3, S, LANES), BF16), pltpu.SemaphoreType.DMA((3,))],
        input_output_aliases={5: 0},
        compiler_params=_cp("arbitrary"), name=name)(proj, proj, proj, do, tot, dqkv)


def _cat_slabs(ref):
    return jnp.concatenate([ref[s] for s in range(ref.shape[0])], axis=1)


def _merge_fwd(o_a, o_b, o_c, gates, b_gate, wa, wb, wc, w_out, name):
    tm = ROW_TILE

    def body(oa_ref, ob_ref, oc_ref, g_ref, bg_ref, wa_ref, wb_ref, wc_ref, wo_ref, mg_ref, mo_ref):
        acc = jnp.zeros((tm, D), F32)
        for i, (o_ref, w_ref) in enumerate(((oa_ref, wa_ref), (ob_ref, wb_ref), (oc_ref, wc_ref))):
            pr = _dot(_cat_slabs(o_ref).astype(BF16), w_ref[...], 1, 0)
            sg = jax.nn.sigmoid(g_ref[:, i * D:(i + 1) * D] + bg_ref[i:i + 1, :])
            acc = acc + sg * pr
        mg = acc.astype(BF16)
        mg_ref[...] = mg
        mo_ref[...] = _dot(mg, wo_ref[...], 1, 0)

    slabs = lambda n: pl.BlockSpec((n, tm, LANES), lambda i: (0, i, 0))
    full = lambda r, c: pl.BlockSpec((r, c), lambda i: (0, 0))
    row = pl.BlockSpec((tm, D), lambda i: (i, 0))
    return pl.pallas_call(
        body, grid=(S // tm,),
        in_specs=[slabs(2), slabs(4), slabs(2), pl.BlockSpec((tm, GATE_COLS), lambda i: (i, 0)), full(3, D),
                  full(256, D), full(512, D), full(256, D), full(D, D)],
        out_specs=[row, row],
        out_shape=[jax.ShapeDtypeStruct((S, D), BF16), jax.ShapeDtypeStruct((S, D), F32)],
        compiler_params=_cp("parallel"), name=name)(o_a, o_b, o_c, gates, b_gate, wa, wb, wc, w_out)


def _merge_bwd(d_mo, o_a, o_b, o_c, gates, b_gate, wa, wb, wc, w_out, name):
    tm = ROW_TILE
    nsteps = S // tm

    def body(dmo_ref, oa_ref, ob_ref, oc_ref, g_ref, bg_ref, wa_ref, wb_ref, wc_ref, wo_ref,
             doa_ref, dob_ref, doc_ref, dg_ref, dwa_ref, dwb_ref, dwc_ref, dbg_ref, acc_a, acc_b, acc_c):
        @pl.when(pl.program_id(0) == 0)
        def _():
            acc_a[...] = jnp.zeros(acc_a.shape, F32)
            acc_b[...] = jnp.zeros(acc_b.shape, F32)
            acc_c[...] = jnp.zeros(acc_c.shape, F32)
            dbg_ref[...] = jnp.zeros(dbg_ref.shape, F32)

        dmg = _dot(dmo_ref[...], wo_ref[...], 1, 1)
        trip = ((oa_ref, wa_ref, doa_ref, acc_a), (ob_ref, wb_ref, dob_ref, acc_b), (oc_ref, wc_ref, doc_ref, acc_c))
        for i, (o_ref, w_ref, do_ref, dw_ref) in enumerate(trip):
            ob = _cat_slabs(o_ref).astype(BF16)
            pr = _dot(ob, w_ref[...], 1, 0)
            sg = jax.nn.sigmoid(g_ref[:, i * D:(i + 1) * D] + bg_ref[i:i + 1, :])
            dgate = dmg * pr * sg * (1.0 - sg)
            dg_ref[:, i * D:(i + 1) * D] = dgate.astype(BF16)
            dbg_ref[i:i + 1, :] += jnp.sum(dgate, axis=0, keepdims=True)
            dpr = (dmg * sg).astype(BF16)
            do = _dot(dpr, w_ref[...], 1, 1)
            for s in range(do_ref.shape[0]):
                do_ref[s] = do[:, s * LANES:(s + 1) * LANES]
            dw_ref[...] += _dot(ob, dpr, 0, 0)

        @pl.when(pl.program_id(0) == nsteps - 1)
        def _():
            dwa_ref[...] = acc_a[...].astype(BF16)
            dwb_ref[...] = acc_b[...].astype(BF16)
            dwc_ref[...] = acc_c[...].astype(BF16)

    slabs = lambda n: pl.BlockSpec((n, tm, LANES), lambda i: (0, i, 0))
    full = lambda r, c: pl.BlockSpec((r, c), lambda i: (0, 0))
    row = pl.BlockSpec((tm, D), lambda i: (i, 0))
    return pl.pallas_call(
        body, grid=(S // tm,),
        in_specs=[row, slabs(2), slabs(4), slabs(2), pl.BlockSpec((tm, GATE_COLS), lambda i: (i, 0)), full(3, D),
                  full(256, D), full(512, D), full(256, D), full(D, D)],
        out_specs=[slabs(2), slabs(4), slabs(2), pl.BlockSpec((tm, GATE_COLS), lambda i: (i, 0)),
                   full(256, D), full(512, D), full(256, D), full(3, D)],
        out_shape=[jax.ShapeDtypeStruct((2, S, LANES), F32), jax.ShapeDtypeStruct((4, S, LANES), F32),
                   jax.ShapeDtypeStruct((2, S, LANES), F32), jax.ShapeDtypeStruct((S, GATE_COLS), BF16),
                   jax.ShapeDtypeStruct((256, D), BF16), jax.ShapeDtypeStruct((512, D), BF16),
                   jax.ShapeDtypeStruct((256, D), BF16), jax.ShapeDtypeStruct((3, D), F32)],
        scratch_shapes=[pltpu.VMEM((256, D), F32), pltpu.VMEM((512, D), F32), pltpu.VMEM((256, D), F32)],
        compiler_params=_cp("arbitrary"), name=name)(d_mo, o_a, o_b, o_c, gates, b_gate, wa, wb, wc, w_out)


FC = 256
GELU_K = math.sqrt(2.0 / math.pi)
GELU_C = 0.044715


RC = 64
NRC = S // RC


def _down(tail, cur, n):
    row = lax.broadcasted_iota(jnp.int32, tail.shape, 0)
    rolled = pltpu.roll(cur, n, 0)
    first = jnp.where(row < n, pltpu.roll(tail, n, 0), rolled[0:8])
    return jnp.concatenate([first, rolled[8:]], axis=0)


def _up(cur, head, n):
    row = lax.broadcasted_iota(jnp.int32, head.shape, 0)
    rolled = pltpu.roll(cur, RC - n, 0)
    last = jnp.where(row >= 8 - n, pltpu.roll(head, 8 - n, 0), rolled[RC - 8:])
    return jnp.concatenate([rolled[:RC - 8], last], axis=0)


def _conv_chunk(load, j, w_ref, b_ref, half):
    r0 = pl.multiple_of(j * RC, RC)
    cur = load(r0, RC).astype(F32)
    tail = load(pl.multiple_of(jnp.maximum(r0 - 16, 0), 16), 16).astype(F32)[8:16]
    tail = jnp.where(j > 0, tail, 0.0)
    d1 = _down(tail, cur, 1)
    d2 = _down(tail, cur, 2)
    y = w_ref[0:1, half, :] * d2 + w_ref[1:2, half, :] * d1 + w_ref[2:3, half, :] * cur + b_ref[half:half + 1, :]
    return y, cur, d1, d2


def _chunk(j):
    return pl.ds(pl.multiple_of(j * RC, RC), RC)


def _fold8(x):
    return jnp.sum(x.reshape(RC // 8, 8, x.shape[-1]), axis=0)


def _ffn_act(u, conv_w, conv_b, name):
    def body(u_ref, w_ref, b_ref, a_ref, y_ref):
        def step(j, carry):
            yg = _conv_chunk(lambda r, n: u_ref[0, pl.ds(r, n), :], j, w_ref, b_ref, 0)[0]
            yv = _conv_chunk(lambda r, n: u_ref[1, pl.ds(r, n), :], j, w_ref, b_ref, 1)[0]
            th = jnp.tanh(GELU_K * (yg + GELU_C * yg * yg * yg))
            a_ref[_chunk(j), :] = (0.5 * yg * (1.0 + th) * yv).astype(BF16)
            y_ref[0, _chunk(j), :] = yg.astype(BF16)
            y_ref[1, _chunk(j), :] = yv.astype(BF16)
            return carry

        lax.fori_loop(0, NRC, step, 0)

    return pl.pallas_call(
        body, grid=(D_FF // FC,),
        in_specs=[pl.BlockSpec((2, S, FC), lambda j: (0, 0, j)), pl.BlockSpec((3, 2, FC), lambda j: (0, 0, j)),
                  pl.BlockSpec((2, FC), lambda j: (0, j))],
        out_specs=[pl.BlockSpec((S, FC), lambda j: (0, j)), pl.BlockSpec((2, S, FC), lambda j: (0, 0, j))],
        out_shape=[jax.ShapeDtypeStruct((S, D_FF), BF16), jax.ShapeDtypeStruct((2, S, D_FF), BF16)],
        compiler_params=_cp("parallel"), name=name)(u, conv_w, conv_b)


def _ffn_act_bwd(u, y, d_a, conv_w, name):
    def body(u_ref, y_ref, da_ref, w_ref, du_ref, dw_ref, db_ref, dy_s):
        def first(j, acc):
            yg = y_ref[0, _chunk(j), :].astype(F32)
            yv = y_ref[1, _chunk(j), :].astype(F32)
            th = jnp.tanh(GELU_K * (yg + GELU_C * yg * yg * yg))
            gelu = 0.5 * yg * (1.0 + th)
            dgelu = 0.5 * (1.0 + th) + 0.5 * yg * (1.0 - th * th) * GELU_K * (1.0 + 3.0 * GELU_C * yg * yg)
            da = da_ref[_chunk(j), :].astype(F32)
            dyg = da * yv * dgelu
            dyv = da * gelu
            dy_s[0, _chunk(j), :] = dyg
            dy_s[1, _chunk(j), :] = dyv
            return acc[0] + _fold8(dyg), acc[1] + _fold8(dyv)

        zero = jnp.zeros((8, FC), F32)
        accb = lax.fori_loop(0, NRC, first, (zero, zero))
        for half in range(2):
            db_ref[half:half + 1, :] = jnp.sum(accb[half], axis=0, keepdims=True)

        def second(j, acc):
            new = []
            for half in range(2):
                cur = dy_s[half, _chunk(j), :]
                h0 = pl.multiple_of(jnp.minimum((j + 1) * RC, S - 8), 8)
                head = jnp.where(j < NRC - 1, dy_s[half, pl.ds(h0, 8), :], 0.0)
                up1 = _up(cur, head, 1)
                up2 = _up(cur, head, 2)
                du = w_ref[2:3, half, :] * cur + w_ref[1:2, half, :] * up1 + w_ref[0:1, half, :] * up2
                du_ref[half, _chunk(j), :] = du.astype(BF16)
                uu = u_ref[half, _chunk(j), :].astype(F32)
                new += [_fold8(up2 * uu), _fold8(up1 * uu), _fold8(cur * uu)]
            return tuple(a + n for a, n in zip(acc, new))

        accw = lax.fori_loop(0, NRC, second, tuple(zero for _ in range(6)))
        for half in range(2):
            for k in range(3):
                dw_ref[k:k + 1, half, :] = jnp.sum(accw[3 * half + k], axis=0, keepdims=True)

    return pl.pallas_call(
        body, grid=(D_FF // FC,),
        in_specs=[pl.BlockSpec((2, S, FC), lambda j: (0, 0, j)), pl.BlockSpec((2, S, FC), lambda j: (0, 0, j)),
                  pl.BlockSpec((S, FC), lambda j: (0, j)), pl.BlockSpec((3, 2, FC), lambda j: (0, 0, j))],
        out_specs=[pl.BlockSpec((2, S, FC), lambda j: (0, 0, j)), pl.BlockSpec((3, 2, FC), lambda j: (0, 0, j)),
                   pl.BlockSpec((2, FC), lambda j: (0, j))],
        out_shape=[jax.ShapeDtypeStruct((2, S, D_FF), BF16), jax.ShapeDtypeStruct((3, 2, D_FF), F32),
                   jax.ShapeDtypeStruct((2, D_FF), F32)],
        scratch_shapes=[pltpu.VMEM((2, S, FC), F32)],
        compiler_params=_cp("parallel"), name=name)(u, y, d_a, conv_w)


def _layer_fwd(x, h1, w, bias, lname):
    n = lambda s: f"{lname}_{s}"
    w.need("in", h1)
    tn = 768
    proj = _mm(h1, w["w_in"], grid=(1, QKV_COLS // tn, 1),
               a_spec=pl.BlockSpec((S, D), lambda i, j, k: (i, 0)),
               b_spec=pl.BlockSpec((tn, D), lambda i, j, k: (j, 0)),
               out_shape=jax.ShapeDtypeStruct((QKV_SLABS, S, LANES), F32),
               out_spec=pl.BlockSpec((tn // LANES, S, LANES), lambda i, j, k: (j, i, 0)),
               ca=1, cb=1, acc_shape=(S, tn), out_slab=True, name=n("proj_qkv"))
    gates = _mm(h1, w["w_in"], grid=(1, GATE_COLS // tn, 1),
                a_spec=pl.BlockSpec((S, D), lambda i, j, k: (i, 0)),
                b_spec=pl.BlockSpec((tn, D), lambda i, j, k: (j + QKV_COLS // tn, 0)),
                out_shape=jax.ShapeDtypeStruct((S, GATE_COLS), BF16),
                out_spec=pl.BlockSpec((S, tn), lambda i, j, k: (i, j)),
                ca=1, cb=1, acc_shape=(S, tn), name=n("proj_gate"))
    nums, stats = [], []
    for g, (_, d) in enumerate(A_GROUPS):
        nm, st = _band_fwd(proj, bias, w["sinks"], d=d, q0=2 * g, k0=6 + 2 * g, v0=12 + 2 * g, npairs=2, bias0=2 * g,
                           shared_kv=False, name=n(f"attn_a{g}_fwd"))
        nums.append(nm)
        stats.append(st)
    o_a, lse_a = _combine_a(nums, stats, n("attn_a_combine"))
    o_b, lse_b = _band_fwd(proj, bias, w["sinks"], d=1, q0=18, k0=22, v0=23, npairs=4, bias0=6, shared_kv=True,
                           name=n("attn_b_fwd"))
    o_c, tot_c = _stick_fwd(proj, q0=24, k0=26, v0=28, name=n("attn_c_fwd"))
    w.need("mix", tot_c)
    merged, mo = _merge_fwd(o_a, o_b, o_c, gates, w["b_gate"], w["w_br_a"], w["w_br_b"], w["w_br_c"], w["w_out"], n("merge_fwd"))
    x2, h2 = _postnorm_res(x, mo, w["attn_post_norm"], w["ffn_pre_norm"], n("attn_post"))
    w.need("ffn", h2)
    u = _mm(h2, w["w_up"], grid=(1, 2 * D_FF // 1024, 1),
            a_spec=pl.BlockSpec((S, D), lambda i, j, k: (i, 0)),
            b_spec=pl.BlockSpec((D, 1024), lambda i, j, k: (0, j)),
            out_shape=jax.ShapeDtypeStruct((2, S, D_FF), BF16),
            out_spec=pl.BlockSpec((None, S, 1024), lambda i, j, k: (j // 4, i, j % 4)),
            ca=1, cb=0, acc_shape=(S, 1024), name=n("ffn_up"))
    a, y = _ffn_act(u, w["conv_w"], w["conv_b"], n("ffn_act"))
    fo = _mm_nn(a, w["w_down"], F32, 1024, 1024, 2048, n("ffn_down"))
    saved = dict(x=x, h1=h1, proj=proj, gates=gates, o_a=o_a, lse_a=lse_a, o_b=o_b, lse_b=lse_b, o_c=o_c, tot_c=tot_c,
                 merged=merged, mo=mo, x2=x2, h2=h2, u=u, y=y, a=a, fo=fo)
    return saved


def _layer_bwd(dx3, sv, w, bias, lname, tok=None, on_part=None, d_fo=None, below=None):
    n = lambda s: f"{lname}_{s}"
    g = {}

    def part(group, vec):
        t = on_part(group, g) if on_part is not None else None
        return vec if t is None else vec + t

    if d_fo is None:
        gain = w["ffn_post_norm"] if tok is None else w["ffn_post_norm"] + tok
        d_fo, g["ffn_post_norm"] = _norm_bwd(sv["fo"], gain, [dx3], None, BF16, n("ffn_post_bwd"))
    else:
        d_fo, g["ffn_post_norm"] = d_fo
    d_a = _mm_nt(d_fo, w["w_down"], BF16, S, 1024, 1024, n("ffn_down_bwd_x"))
    g["w_down"] = _mm_tn(sv["a"], d_fo, BF16, 1024, 1024, S, n("ffn_down_bwd_w"))
    d_u, dcw, dcb = _ffn_act_bwd(sv["u"], sv["y"], d_a, w["conv_w"], n("ffn_act_bwd"))
    g["conv_w"] = dcw.reshape(3, 2 * D_FF)
    g["conv_b"] = dcb.reshape(1, 2 * D_FF)
    g["w_up"] = _mm(sv["h2"], d_u, grid=(1, 2 * D_FF // 1024, 1),
                    a_spec=pl.BlockSpec((S, D), lambda i, j, k: (k, 0)),
                    b_spec=pl.BlockSpec((None, S, 1024), lambda i, j, k: (j // 4, k, j % 4)),
                    out_shape=jax.ShapeDtypeStruct((D, 2 * D_FF), BF16),
                    out_spec=pl.BlockSpec((D, 1024), lambda i, j, k: (0, j)),
                    ca=0, cb=0, acc_shape=(D, 1024), name=n("ffn_up_bwd_w"))
    tok_ffn = on_part("ffn", g) if on_part is not None else None
    d_h2 = _mm(d_u, w["w_up"], grid=(S // 1024, 1, 2),
               a_spec=pl.BlockSpec((None, 1024, D_FF), lambda i, j, k: (k, i, 0)),
               b_spec=pl.BlockSpec((D, D_FF), lambda i, j, k: (0, k)),
               out_shape=jax.ShapeDtypeStruct((S, D), F32),
               out_spec=pl.BlockSpec((1024, D), lambda i, j, k: (i, 0)),
               ca=1, cb=1, acc_shape=(1024, D), after=tok_ffn, name=n("ffn_up_bwd_x"))
    dx2, d_mo, g["ffn_pre_norm"], g["attn_post_norm"] = _norm_bwd_chain(
        sv["x2"], w["ffn_pre_norm"], [d_h2], dx3, sv["mo"], w["attn_post_norm"], n("ffn_pre_attn_post_bwd"))
    g["w_out"] = _mm_tn(sv["merged"], d_mo, BF16, 1024, 1024, S, n("out_bwd_w"))
    do_a, do_b, do_c, d_gates, dwa, dwb, dwc, g["b_gate"] = _merge_bwd(
        d_mo, sv["o_a"], sv["o_b"], sv["o_c"], sv["gates"], w["b_gate"], w["w_br_a"], w["w_br_b"], w["w_br_c"],
        w["w_out"], n("merge_bwd"))
    g["w_br_a"], g["w_br_b"], g["w_br_c"] = dwa, dwb, dwc
    sinks = part("mix", w["sinks"])
    proj = sv["proj"]
    dqkv = lax.empty((QKV_SLABS, S, LANES), BF16)
    gbias = []
    for gi, (_, d) in enumerate(A_GROUPS):
        dqkv, gg, _ = _band_bwd(proj, bias, sv["o_a"], do_a, sv["lse_a"], sinks, dqkv, d=d, q0=2 * gi, k0=6 + 2 * gi,
                                v0=12 + 2 * gi, npairs=2, bias0=2 * gi, shared_kv=False, name=n(f"attn_a{gi}_bwd"))
        gbias.append(gg)
    dqkv, ggb, dsink = _band_bwd(proj, bias, sv["o_b"], do_b, sv["lse_b"], sinks, dqkv, d=1, q0=18, k0=22, v0=23,
                                 npairs=4, bias0=6, shared_kv=True, name=n("attn_b_bwd"))
    gbias.append(ggb)
    g["bias_g"] = jnp.concatenate(gbias, axis=0).reshape(N_BIAS_HEADS, BLK, 2 * BLK)
    g["sinks"] = dsink[:, 0, :2].reshape(1, 8)
    dqkv = _stick_bwd(proj, do_c, sv["tot_c"], dqkv, q0=24, k0=26, v0=28, name=n("attn_c_bwd"))
    ts = 6
    tsx = QKV_SLABS
    dw_in = _mm(dqkv, sv["h1"], grid=(QKV_SLABS // ts, 1, 1),
                a_spec=pl.BlockSpec((ts, S, LANES), lambda i, j, k: (i, k, 0)),
                b_spec=pl.BlockSpec((S, D), lambda i, j, k: (k, 0)),
                out_shape=jax.ShapeDtypeStruct((IN_COLS, D), BF16),
                out_spec=pl.BlockSpec((ts * LANES, D), lambda i, j, k: (i, 0)),
                ca=0, cb=0, acc_shape=(ts * LANES, D), a_slab=True, name=n("in_bwd_w_qkv"))
    g["w_in"] = _mm(d_gates, sv["h1"], grid=(GATE_COLS // 768, 1, 1),
                    a_spec=pl.BlockSpec((S, 768), lambda i, j, k: (k, i)),
                    b_spec=pl.BlockSpec((S, D), lambda i, j, k: (k, 0)),
                    out_shape=jax.ShapeDtypeStruct((IN_COLS, D), BF16),
                    out_spec=pl.BlockSpec((768, D), lambda i, j, k: (i + QKV_COLS // 768, 0)),
                    ca=0, cb=0, acc_shape=(768, D), alias_out=dw_in, name=n("in_bwd_w_gate"))
    tok_in = on_part("in", g) if on_part is not None else None
    d_h1a = _mm(dqkv, w["w_in"], grid=(S // 1024, 1, QKV_SLABS // tsx),
                a_spec=pl.BlockSpec((tsx, 1024, LANES), lambda i, j, k: (k, i, 0)),
                b_spec=pl.BlockSpec((tsx * LANES, D), lambda i, j, k: (k, 0)),
                out_shape=jax.ShapeDtypeStruct((S, D), F32),
                out_spec=pl.BlockSpec((1024, D), lambda i, j, k: (i, 0)),
                ca=1, cb=0, acc_shape=(1024, D), a_slab=True, after=tok_in, name=n("in_bwd_x_qkv"))
    d_h1b = _mm(d_gates, w["w_in"], grid=(S // 1024, 1, GATE_COLS // 768),
                a_spec=pl.BlockSpec((1024, 768), lambda i, j, k: (i, k)),
                b_spec=pl.BlockSpec((768, D), lambda i, j, k: (k + QKV_COLS // 768, 0)),
                out_shape=jax.ShapeDtypeStruct((S, D), F32),
                out_spec=pl.BlockSpec((1024, D), lambda i, j, k: (i, 0)),
                ca=1, cb=0, acc_shape=(1024, D), after=tok_in, name=n("in_bwd_x_gate"))
    if below is None:
        dx, g["attn_pre_norm"] = _norm_bwd(sv["x"], w["attn_pre_norm"], [d_h1a, d_h1b], dx2, F32, n("attn_pre_bwd"))
        return dx, g, tok_in, None
    dx, d_fo_below, g["attn_pre_norm"], dg_below = _norm_bwd_chain(
        sv["x"], w["attn_pre_norm"], [d_h1a, d_h1b], dx2, below[0], below[1], n("attn_pre_ffn_post_bwd"))
    return dx, g, tok_in, (d_fo_below, dg_below)


def _local_step(x, target, ws, rel_bias, tok=None, on_grads=None):
    buckets = jnp.asarray(_bucket_tiles())
    bias = _bias_tiles(rel_bias, buckets, "bias_tiles").reshape(N_BIAS_HEADS // 2, 2, 2, BLK, 2 * BLK)
    saved = []
    gain0 = ws[0]["attn_pre_norm"] if tok is None else ws[0]["attn_pre_norm"] + tok
    h1 = _prenorm(x, gain0, "l0_attn_pre")
    for l in range(DEPTH):
        sv = _layer_fwd(x, h1, ws[l], bias, f"l{l}")
        saved.append(sv)
        if l + 1 < DEPTH:
            x, h1 = _postnorm_res(sv["x2"], sv["fo"], ws[l]["ffn_post_norm"], ws[l + 1]["attn_pre_norm"], f"l{l}_ffn_post")
    top = saved[-1]
    dy, loss_tile, d_fo_top, dg_top = _loss_head(top["x2"], top["fo"], ws[-1]["ffn_post_norm"], target, "loss_head")
    grads = [None] * DEPTH
    tok, d_fo = None, (d_fo_top, dg_top)
    for l in reversed(range(DEPTH)):
        on_part = None if on_grads is None else functools.partial(on_grads, l)
        below = (saved[l - 1]["fo"], ws[l - 1]["ffn_post_norm"]) if l > 0 else None
        dy, grads[l], tok, d_fo = _layer_bwd(dy, saved[l], ws[l], bias, f"l{l}", tok, on_part, d_fo, below)
    g_rel = _bias_grad([grads[l]["bias_g"] for l in range(DEPTH)], buckets, "bias_grad")[:, :N_BIAS_HEADS]
    return loss_tile, dy, grads, g_rel


def _coords():
    return lax.axis_index("x"), lax.axis_index("y"), lax.axis_index("c")


def _peer(rel):
    x, y, c = _coords()
    return (1 - x if rel & 4 else x, 1 - y if rel & 2 else y, 1 - c if rel & 1 else c)


def _exchange(srcs, dst_shapes, src_win, dst_win, name, after=None):
    nt = len(srcs)
    extra = [] if after is None else [after]

    def body(*refs):
        src_refs, dst_refs = refs[:nt], refs[nt + len(extra):2 * nt + len(extra)]
        send_sems, recv_sems, local_sems = refs[2 * nt + len(extra):]
        x, y, c = _coords()
        me = 4 * x + 2 * y + c
        locals_ = []
        for t in range(nt):
            cp = pltpu.make_async_copy(src_win(t, src_refs[t], me), dst_win(t, dst_refs[t], me), local_sems.at[t])
            cp.start()
            locals_.append(cp)
        sends = []
        for rel in range(1, NDEV):
            px, py, pc = _peer(rel)
            q = 4 * px + 2 * py + pc
            for t in range(nt):
                cp = pltpu.make_async_remote_copy(
                    src_ref=src_win(t, src_refs[t], q), dst_ref=dst_win(t, dst_refs[t], me),
                    send_sem=send_sems.at[rel - 1, t], recv_sem=recv_sems.at[rel - 1, t],
                    device_id=(px, py, pc), device_id_type=MESH)
                cp.start()
                sends.append(cp)
        for rel in range(1, NDEV):
            px, py, pc = _peer(rel)
            q = 4 * px + 2 * py + pc
            for t in range(nt):
                pltpu.make_async_remote_copy(
                    src_ref=src_win(t, src_refs[t], me), dst_ref=dst_win(t, dst_refs[t], q),
                    send_sem=send_sems.at[rel - 1, t], recv_sem=recv_sems.at[rel - 1, t],
                    device_id=(px, py, pc), device_id_type=MESH).wait_recv()
        for cp in sends:
            cp.wait_send()
        for cp in locals_:
            cp.wait()

    return pl.pallas_call(
        body, in_specs=[ANY] * (nt + len(extra)), out_specs=[ANY] * nt, out_shape=dst_shapes,
        scratch_shapes=[pltpu.SemaphoreType.DMA((NDEV - 1, nt)), pltpu.SemaphoreType.DMA((NDEV - 1, nt)),
                        pltpu.SemaphoreType.DMA((nt,))],
        name=name)(*srcs, *extra)


BIG = (("w_in", 0, 864), ("w_br_a", 1, 128), ("w_br_b", 1, 128), ("w_br_c", 1, 128), ("w_out", 0, 128),
       ("w_up", 1, 1024), ("w_down", 0, 512))


NBIG = len(BIG)
BIG_FULL = {"w_in": (IN_COLS, D), "w_br_a": (256, D), "w_br_b": (512, D), "w_br_c": (256, D), "w_out": (D, D),
            "w_up": (D, 2 * D_FF), "w_down": (D_FF, D)}
SHARD_ROWS = {"w_in": 288, "w_up": 256, "w_down": 256}
LAYER_GROUPS = (("in", (0,)), ("mix", (1, 2, 3, 4)), ("ffn", (5, 6)))

HBM_SPEC = pl.BlockSpec(memory_space=pltpu.HBM)
SEM_SPEC = pl.BlockSpec(memory_space=pltpu.SEMAPHORE)


def _hbm(a):
    return pltpu.with_memory_space_constraint(a, pltpu.HBM)


def _shard_window(t, ref, k):
    nm, ax, ext = BIG[t % NBIG]
    off = pl.multiple_of(k * ext, ext)
    if ax == 0:
        return ref.at[pl.ds(off, ext), :]
    return ref.at[:, pl.ds(off, ext)]


def _whole(t, ref, k):
    return ref


def _slot(t, ref, k):
    return ref.at[k]


def _own_block_spec(t, rows, me_of):
    nm, ax, ext = BIG[t % NBIG]
    r, c = BIG_FULL[nm]
    if ax == 0:
        return pl.BlockSpec((rows, c), lambda i, m: (me_of(m) * (ext // rows) + i, 0))
    return pl.BlockSpec((rows, ext), lambda i, m: (i, me_of(m)))


def _cast_own(t, shards, me_arr, name):
    nm, ax, ext = BIG[t % NBIG]
    layer = t // NBIG
    _, nr, nc = shards.shape
    rows = SHARD_ROWS.get(nm, nr)
    shape = BIG_FULL[nm]

    def body(m_ref, s_ref, o_ref):
        o_ref[...] = s_ref[...].astype(BF16)

    return pl.pallas_call(
        body, grid_spec=pltpu.PrefetchScalarGridSpec(
            num_scalar_prefetch=1, grid=(nr // rows,),
            in_specs=[pl.BlockSpec((None, rows, nc), lambda i, m: (layer, i, 0))],
            out_specs=_own_block_spec(t, rows, lambda m: m[0])),
        out_shape=jax.ShapeDtypeStruct(shape, BF16), compiler_params=_cp("arbitrary"), name=name)(me_arr, shards)


ALL_RELS = tuple(range(1, NDEV))
NEAR_RELS = (1, 2, 4, 6)
FAR_RELS = (2, 4, 6)


def _xchg_start(srcs, lands, groups, src_win, dst_win, after, name, rels=ALL_RELS, tids=None):
    ns = 0 if srcs is None else len(srcs)
    nt, ng = len(lands), len(groups)
    ins = ([] if srcs is None else list(srcs)) + list(lands)

    def body(*refs):
        src_refs, land_refs = refs[:ns], refs[ns:ns + nt]
        sems = refs[ns + nt + 1:ns + nt + 1 + 2 * ng]
        token = refs[-1]
        x, y, c = _coords()
        me = 4 * x + 2 * y + c
        for gi, grp in enumerate(groups):
            for j, t in enumerate(grp):
                tid = t if tids is None else tids[t]
                for ri, rel in enumerate(rels):
                    px, py, pc = _peer(rel)
                    q = 4 * px + 2 * py + pc
                    src = dst_win(tid, land_refs[t], me) if srcs is None else src_win(tid, src_refs[t], q)
                    pltpu.make_async_remote_copy(
                        src_ref=src, dst_ref=dst_win(tid, land_refs[t], me),
                        send_sem=sems[2 * gi].at[ri * len(grp) + j],
                        recv_sem=sems[2 * gi + 1].at[ri * len(grp) + j],
                        device_id=(px, py, pc), device_id_type=MESH).start()
        token[...] = jnp.zeros((8, LANES), F32)

    out_shape = []
    for grp in groups:
        out_shape += [pltpu.SemaphoreType.DMA((len(rels) * len(grp),))] * 2
    out_shape += [pltpu.HBM(a.shape, a.dtype) for a in ins]
    out_shape.append(jax.ShapeDtypeStruct((8, LANES), F32))
    outs = pl.pallas_call(
        body, in_specs=[HBM_SPEC] * len(ins) + [ANY],
        out_specs=[SEM_SPEC] * (2 * ng) + [HBM_SPEC] * len(ins) + [pl.BlockSpec(memory_space=pltpu.VMEM)],
        out_shape=out_shape, input_output_aliases={i: 2 * ng + i for i in range(len(ins))},
        compiler_params=pltpu.CompilerParams(has_side_effects=pltpu.SideEffectType.DATAFLOW_SIDE_EFFECTING),
        name=name)(*[_hbm(a) for a in ins], after)
    sems = [(outs[2 * gi], outs[2 * gi + 1]) for gi in range(ng)]
    thru = list(outs[2 * ng:2 * ng + len(ins)])
    return sems, (None if srcs is None else thru[:ns]), thru[ns:], outs[-1]


def _xchg_wait(sems, srcs, lands, tids, after, src_win, dst_win, name, rels=ALL_RELS):
    ns = 0 if srcs is None else len(srcs)
    n = len(lands)
    send_sem, recv_sem = sems
    ins = ([] if srcs is None else list(srcs)) + list(lands)

    def body(*refs):
        src_refs, land_refs = refs[:ns], refs[ns:ns + n]
        ssem, rsem = refs[ns + n], refs[ns + n + 1]
        x, y, c = _coords()
        me = 4 * x + 2 * y + c
        for j, t in enumerate(tids):
            for ri, rel in enumerate(rels):
                px, py, pc = _peer(rel)
                q = 4 * px + 2 * py + pc
                src = dst_win(t, land_refs[j], me) if srcs is None else src_win(t, src_refs[j], q)
                cp = pltpu.make_async_remote_copy(
                    src_ref=src, dst_ref=dst_win(t, land_refs[j], q),
                    send_sem=ssem.at[ri * n + j], recv_sem=rsem.at[ri * n + j],
                    device_id=(px, py, pc), device_id_type=MESH)
                cp.wait_send()
                cp.wait_recv()

    outs = pl.pallas_call(
        body, in_specs=[HBM_SPEC] * len(ins) + [SEM_SPEC, SEM_SPEC, ANY], out_specs=[HBM_SPEC] * len(ins),
        out_shape=[pltpu.HBM(a.shape, a.dtype) for a in ins],
        input_output_aliases={i: i for i in range(len(ins))},
        compiler_params=pltpu.CompilerParams(has_side_effects=pltpu.SideEffectType.DATAFLOW_SIDE_EFFECTING),
        name=name)(*ins, send_sem, recv_sem, after)
    return (None if srcs is None else list(outs[:ns])), list(outs[ns:])


def _gather_forward(sems_in, lands, groups, tids, after, dst_win, name):
    nt, ng = len(lands), len(groups)

    def body(*refs):
        land_refs = refs[:nt]
        in_sems = refs[nt:nt + 2 * ng]
        out_sems = refs[nt + 2 * ng + 1:nt + 4 * ng + 1]
        token = refs[-1]
        x, y, c = _coords()
        me = 4 * x + 2 * y + c
        sib = (x, y, 1 - c)
        for gi, grp in enumerate(groups):
            n = len(grp)
            for j, pos in enumerate(grp):
                t = tids[pos]
                for ri, rel in enumerate(NEAR_RELS):
                    px, py, pc = _peer(rel)
                    q = 4 * px + 2 * py + pc
                    cp = pltpu.make_async_remote_copy(
                        src_ref=dst_win(t, land_refs[pos], me), dst_ref=dst_win(t, land_refs[pos], q),
                        send_sem=in_sems[2 * gi].at[ri * n + j], recv_sem=in_sems[2 * gi + 1].at[ri * n + j],
                        device_id=(px, py, pc), device_id_type=MESH)
                    cp.wait_send()
                    cp.wait_recv()
            for j, pos in enumerate(grp):
                t = tids[pos]
                for fi, rel in enumerate(FAR_RELS):
                    px, py, pc = _peer(rel)
                    q = 4 * px + 2 * py + pc
                    win = dst_win(t, land_refs[pos], q)
                    pltpu.make_async_remote_copy(
                        src_ref=win, dst_ref=win,
                        send_sem=out_sems[2 * gi].at[fi * n + j], recv_sem=out_sems[2 * gi + 1].at[fi * n + j],
                        device_id=sib, device_id_type=MESH).start()
        token[...] = jnp.zeros((8, LANES), F32)

    out_shape = []
    for grp in groups:
        out_shape += [pltpu.SemaphoreType.DMA((len(FAR_RELS) * len(grp),))] * 2
    out_shape += [pltpu.HBM(a.shape, a.dtype) for a in lands]
    out_shape.append(jax.ShapeDtypeStruct((8, LANES), F32))
    flat_sems = [s for pair in sems_in for s in pair]
    outs = pl.pallas_call(
        body, in_specs=[HBM_SPEC] * nt + [SEM_SPEC] * (2 * ng) + [ANY],
        out_specs=[SEM_SPEC] * (2 * ng) + [HBM_SPEC] * nt + [pl.BlockSpec(memory_space=pltpu.VMEM)],
        out_shape=out_shape, input_output_aliases={i: 2 * ng + i for i in range(nt)},
        compiler_params=pltpu.CompilerParams(has_side_effects=pltpu.SideEffectType.DATAFLOW_SIDE_EFFECTING),
        name=name)(*[_hbm(a) for a in lands], *flat_sems, after)
    sems = [(outs[2 * gi], outs[2 * gi + 1]) for gi in range(ng)]
    return sems, list(outs[2 * ng:2 * ng + nt]), outs[-1]


class _Weights:
    def __init__(self, ready, pending=None):
        self.ready = dict(ready)
        self.pending = dict(pending or {})

    def __getitem__(self, k):
        return self.ready[k]

    def need(self, group, after):
        fn = self.pending.pop(group, None)
        if fn is not None:
            self.ready.update(fn(after))


def _adamw_math(w, g, m, v):
    m2 = ADAM_B1 * m + (1.0 - ADAM_B1) * g
    v2 = ADAM_B2 * v + (1.0 - ADAM_B2) * (g * g)
    m_hat = m2 / (1.0 - ADAM_B1 ** ADAM_STEP)
    v_hat = v2 / (1.0 - ADAM_B2 ** ADAM_STEP)
    delta = -ADAM_LR * (m_hat / (jnp.sqrt(v_hat) + ADAM_EPS) + ADAM_WD * w)
    return delta, m2, v2


def _adamw(t, parts, own, me_arr, w, m, v, layer, prev, rows, name):
    nl, nr, nc = w.shape

    def body(me_ref, p_ref, own_ref, w_ref, m_ref, v_ref, *rest):
        g_ref, d_ref, m2_ref, v2_ref = rest[-4:]
        me = me_ref[0]
        g = None
        for k in range(NDEV):
            term = jnp.where(me == k, own_ref[...], p_ref[k]).astype(F32)
            g = term if g is None else g + term
        delta, m2, v2 = _adamw_math(w_ref[...], g, m_ref[...], v_ref[...])
        g_ref[...] = g
        d_ref[...] = delta
        m2_ref[...] = m2
        v2_ref[...] = v2

    blk = pl.BlockSpec((None, rows, nc), lambda i, mm: (layer, i, 0))
    pblk = pl.BlockSpec((NDEV, rows, nc), lambda i, mm: (0, i, 0))
    extra = [] if prev is None else list(prev)
    return pl.pallas_call(
        body, grid_spec=pltpu.PrefetchScalarGridSpec(
            num_scalar_prefetch=1, grid=(nr // rows,),
            in_specs=[pblk, _own_block_spec(t, rows, lambda mm: mm[0]), blk, blk, blk] + [ANY] * len(extra),
            out_specs=[blk] * 4),
        out_shape=[jax.ShapeDtypeStruct(w.shape, F32)] * 4,
        input_output_aliases={6 + k: k for k in range(len(extra))},
        compiler_params=_cp("arbitrary"), name=name)(me_arr, parts, own, w, m, v, *extra)


def _pack(vecs):
    flat = jnp.concatenate([v.reshape(-1).astype(F32) for v in vecs])
    n = flat.shape[0]
    rows = -(-n // (8 * LANES)) * 8
    return jnp.pad(flat, (0, rows * LANES - n)).reshape(rows, LANES)


ROWPACK = (("rel_bias", 32, 32, (NUM_BUCKETS, N_BIAS_HEADS)), ("sinks", 8, 8, (DEPTH, 8)),
           ("attn_pre_norm", 16, 16, (DEPTH, D)), ("attn_post_norm", 16, 16, (DEPTH, D)),
           ("ffn_pre_norm", 16, 16, (DEPTH, D)), ("ffn_post_norm", 16, 16, (DEPTH, D)),
           ("conv_b", 128, 128, (DEPTH, 2 * D_FF)), ("b_gate", 48, 8, (DEPTH, 3, 128)),
           ("conv_w", 384, 48, (DEPTH, 3, 1024)))
ROWS_OWN = sum(r for _, _, r, _ in ROWPACK)
N_REPL = 7
ROWS_REPL = sum(r for _, _, r, _ in ROWPACK[:N_REPL])
ROWS_SHARD = ROWS_OWN - ROWS_REPL


def _as_rows(a, rows):
    a = a.astype(F32)
    if a.shape[-1] < LANES:
        a = jnp.pad(a.reshape(-1, a.shape[-1]), ((0, 0), (0, LANES - a.shape[-1])))
    a = a.reshape(-1, LANES)
    return jnp.pad(a, ((0, rows - a.shape[0]), (0, 0)))


def _rowpack(arrs, entries=ROWPACK):
    return jnp.concatenate([_as_rows(arrs[nm], ro) for nm, _, ro, _ in entries], axis=0)


def _shard_rows(g):
    bg = jnp.transpose(g["b_gate"].astype(F32).reshape(DEPTH * 3, NDEV, LANES), (1, 0, 2))
    bg = jnp.pad(bg, ((0, 0), (0, 8 - DEPTH * 3), (0, 0)))
    cw = jnp.transpose(g["conv_w"].astype(F32).reshape(DEPTH * 3, NDEV, 8, LANES), (1, 0, 2, 3))
    return jnp.concatenate([bg, cw.reshape(NDEV, DEPTH * 3 * 8, LANES)], axis=1)


def _small_update(parts_repl, parts_shard, w, m, v, name):
    nsm = len(ROWPACK)

    def body(pr_ref, ps_ref, w_ref, m_ref, v_ref, *rest):
        outs = rest[:4 * nsm]
        loss_ref = rest[4 * nsm]
        g_s, d_s, m_s, v_s = rest[4 * nsm + 1:]
        gr, gs = pr_ref[0], ps_ref[0]
        for k in range(1, NDEV):
            gr = gr + pr_ref[k]
            gs = gs + ps_ref[k]
        g_s[0:ROWS_REPL, :] = gr[:ROWS_REPL]
        g_s[ROWS_REPL:ROWS_OWN, :] = gs
        loss_ref[...] = gr[ROWS_REPL:]
        delta, m2, v2 = _adamw_math(w_ref[...], g_s[...], m_ref[...], v_ref[...])
        d_s[...] = delta
        m_s[...] = m2
        v_s[...] = v2
        for kind, src in enumerate((g_s, d_s, m_s, v_s)):
            oo = 0
            for idx, (nm, rf, ro, shp) in enumerate(ROWPACK):
                o_ref = outs[kind * nsm + idx]
                if nm in ("rel_bias", "sinks"):
                    o_ref[...] = src[oo:oo + shp[0], 0:shp[1]]
                elif nm == "b_gate":
                    for l in range(DEPTH):
                        o_ref[l] = src[oo + 3 * l:oo + 3 * l + 3, :]
                elif nm == "conv_w":
                    for l in range(DEPTH):
                        for k in range(8):
                            o_ref[l, :, k * LANES:(k + 1) * LANES] = src[pl.ds(oo + 24 * l + k, 3, stride=8), :]
                else:
                    per = shp[1] // LANES
                    for k in range(per):
                        o_ref[:, k * LANES:(k + 1) * LANES] = src[pl.ds(oo + k, DEPTH, stride=per), :]
                oo += ro

    vm = pl.BlockSpec(memory_space=pltpu.VMEM)
    shapes = [jax.ShapeDtypeStruct(shp, F32) for _ in range(4) for _, _, _, shp in ROWPACK]
    shapes.append(jax.ShapeDtypeStruct((8, LANES), F32))
    outs = pl.pallas_call(
        body, in_specs=[vm] * 5, out_specs=[vm] * (4 * nsm + 1), out_shape=shapes,
        scratch_shapes=[pltpu.VMEM((ROWS_OWN, LANES), F32)] * 4,
        name=name)(parts_repl, parts_shard, w, m, v)
    names = [nm for nm, _, _, _ in ROWPACK]
    return [dict(zip(names, outs[kind * nsm:(kind + 1) * nsm])) for kind in range(4)] + [outs[-1]]


def kernel(x, rel_bias, attn_pre_norm, w_in, b_gate, sinks, w_br_a, w_br_b, w_br_c, w_out, attn_post_norm, ffn_pre_norm, w_up, conv_w, conv_b, w_down, ffn_post_norm, loss_target, m_rel_bias, m_attn_pre_norm, m_w_in, m_b_gate, m_sinks, m_w_br_a, m_w_br_b, m_w_br_c, m_w_out, m_attn_post_norm, m_ffn_pre_norm, m_w_up, m_conv_w, m_conv_b, m_w_down, m_ffn_post_norm, v_rel_bias, v_attn_pre_norm, v_w_in, v_b_gate, v_sinks, v_w_br_a, v_w_br_b, v_w_br_c, v_w_out, v_attn_post_norm, v_ffn_pre_norm, v_w_up, v_conv_w, v_conv_b, v_w_down, v_ffn_post_norm):
    P = dict(rel_bias=rel_bias, attn_pre_norm=attn_pre_norm, w_in=w_in, b_gate=b_gate, sinks=sinks, w_br_a=w_br_a,
             w_br_b=w_br_b, w_br_c=w_br_c, w_out=w_out, attn_post_norm=attn_post_norm, ffn_pre_norm=ffn_pre_norm,
             w_up=w_up, conv_w=conv_w, conv_b=conv_b, w_down=w_down, ffn_post_norm=ffn_post_norm)
    M = dict(rel_bias=m_rel_bias, attn_pre_norm=m_attn_pre_norm, w_in=m_w_in, b_gate=m_b_gate, sinks=m_sinks,
             w_br_a=m_w_br_a, w_br_b=m_w_br_b, w_br_c=m_w_br_c, w_out=m_w_out, attn_post_norm=m_attn_post_norm,
             ffn_pre_norm=m_ffn_pre_norm, w_up=m_w_up, conv_w=m_conv_w, conv_b=m_conv_b, w_down=m_w_down,
             ffn_post_norm=m_ffn_post_norm)
    V = dict(rel_bias=v_rel_bias, attn_pre_norm=v_attn_pre_norm, w_in=v_w_in, b_gate=v_b_gate, sinks=v_sinks,
             w_br_a=v_w_br_a, w_br_b=v_w_br_b, w_br_c=v_w_br_c, w_out=v_w_out, attn_post_norm=v_attn_post_norm,
             ffn_pre_norm=v_ffn_pre_norm, w_up=v_w_up, conv_w=v_conv_w, conv_b=v_conv_b, w_down=v_w_down,
             ffn_post_norm=v_ffn_post_norm)
    tr = lambda a: jnp.swapaxes(a, 1, 2)
    PB = {nm: (tr(P[nm]) if nm == "w_in" else P[nm]) for nm, _, _ in BIG}
    MB = {nm: (tr(M[nm]) if nm == "w_in" else M[nm]) for nm, _, _ in BIG}
    VB = {nm: (tr(V[nm]) if nm == "w_in" else V[nm]) for nm, _, _ in BIG}
    xi, yi, ci = _coords()
    me = 4 * xi + 2 * yi + ci

    me_arr = me.astype(jnp.int32).reshape(1)

    small_w = _pack([b_gate.reshape(-1), conv_w.reshape(-1)])
    (small_w_all,) = _exchange([small_w], [jax.ShapeDtypeStruct((NDEV,) + small_w.shape, F32)],
                               _whole, _slot, "gather_small_weights")
    nbg, ncw = DEPTH * 3 * 128, DEPTH * 3 * 1024
    flat_all = small_w_all.reshape(NDEV, -1)
    b_gate_full = jnp.transpose(flat_all[:, :nbg].reshape(NDEV, DEPTH, 3, 128), (1, 2, 0, 3)).reshape(DEPTH, 3, D)
    conv_w_full = jnp.transpose(flat_all[:, nbg:nbg + ncw].reshape(NDEV, DEPTH, 3, 1024), (1, 2, 0, 3)).reshape(DEPTH, 3, 2 * D_FF)

    groups = [tuple(l * NBIG + t for t in tids) for l in range(DEPTH) for _, tids in LAYER_GROUPS]
    cast = lambda i, m=me_arr: _cast_own(i, PB[BIG[i % NBIG][0]], m, f"gather_own_l{i // NBIG}_{BIG[i % NBIG][0]}")
    first = list(groups[0])
    rest = [i for grp in groups[1:] for i in grp]
    sems0, _, lands0, tok_first = _xchg_start(None, [cast(i) for i in first], [tuple(range(len(first)))], None,
                                              _shard_window, small_w_all, "gather_start_first", rels=NEAR_RELS, tids=first)
    where_rest = {tid: k for k, tid in enumerate(rest)}
    me_rest = me_arr + tok_first[0, 0:1].astype(jnp.int32)
    sems1, _, lands1, g_tok = _xchg_start(None, [cast(i, me_rest) for i in rest],
                                          [tuple(where_rest[i] for i in grp) for grp in groups[1:]], None,
                                          _shard_window, lands0[0], "gather_start_rest", rels=NEAR_RELS, tids=rest)
    g_sems = list(sems0) + list(sems1)
    tok0 = g_tok[0:1, 0:1]
    lands_now = [None] * (DEPTH * NBIG)
    for i, a in zip(first + rest, list(lands0) + list(lands1)):
        lands_now[i] = a
    fwd_sems = {}
    fwd_plan = {0: (0,), 1: (1,), 2: (2,), 3: (3, 4, 5)}

    def gather_waiter(gi, l, gname, tids):
        def wait(after):
            if gi in fwd_plan:
                gis = fwd_plan[gi]
                flat = [i for g2 in gis for i in groups[g2]]
                where = {tid: k for k, tid in enumerate(flat)}
                fs, new_lands, ftok = _gather_forward(
                    [g_sems[g2] for g2 in gis], [lands_now[i] for i in flat],
                    [[where[i] for i in groups[g2]] for g2 in gis], flat, after, _shard_window, f"gather_forward_{gi}")
                for g2, s in zip(gis, fs):
                    fwd_sems[g2] = s
                for i, a in zip(flat, new_lands):
                    lands_now[i] = a
                after = ftok
            ids = [l * NBIG + t for t in tids]
            _, got = _xchg_wait(fwd_sems[gi], None, [lands_now[i] for i in ids], ids, after,
                                None, _shard_window, f"gather_wait_l{l}_{gname}", rels=FAR_RELS)
            out = {}
            for t, arr in zip(tids, got):
                nm = BIG[t][0]
                out[nm] = arr
            return out
        return wait

    pending = [{gname: gather_waiter(l * len(LAYER_GROUPS) + k, l, gname, tids)
                for k, (gname, tids) in enumerate(LAYER_GROUPS)} for l in range(DEPTH)]
    ws = []
    for l in range(DEPTH):
        ws.append(_Weights(dict(
            b_gate=b_gate_full[l], conv_w=conv_w_full[l].reshape(3, 2, D_FF), conv_b=conv_b[l].reshape(2, D_FF),
            sinks=sinks[l].reshape(1, 8),
            attn_pre_norm=attn_pre_norm[l].reshape(1, D), attn_post_norm=attn_post_norm[l].reshape(1, D),
            ffn_pre_norm=ffn_pre_norm[l].reshape(1, D), ffn_post_norm=ffn_post_norm[l].reshape(1, D)), pending[l]))

    rs = {}

    group_tids = dict(LAYER_GROUPS)

    def start_scatter(l, gname, grads_l):
        tids = group_tids[gname]
        blocks, lands_rs = [], []
        for t in tids:
            nm, ax, ext = BIG[t]
            gfull = grads_l[nm].astype(BF16)
            shp = (NDEV, ext, gfull.shape[1]) if ax == 0 else (NDEV, gfull.shape[0], ext)
            blocks.append(gfull)
            lands_rs.append(lax.empty(shp, BF16))
        local = list(range(len(tids)))
        win = lambda j, ref, k: _shard_window(tids[j], ref, k)
        sems, s_thru, l_thru, tok = _xchg_start(blocks, lands_rs, [tuple(local)], win, _slot, me_arr,
                                                f"scatter_start_l{l}_{gname}")
        rs[(l, gname)] = (sems[0], s_thru, l_thru, win, local)
        return tok[0:1, 0:1]

    loss_tile, grad_x, grads, g_rel = _local_step(x[0], loss_target[0], ws, rel_bias, tok0, start_scatter)

    stack = lambda nm: jnp.stack([grads[l][nm] for l in range(DEPTH)], axis=0)
    small_g = {nm: (g_rel if nm == "rel_bias" else stack(nm)) for nm, _, _, _ in ROWPACK}
    small_repl = jnp.concatenate([_rowpack(small_g, ROWPACK[:N_REPL]), loss_tile], axis=0)
    small_shard = _shard_rows(small_g)

    out_g, out_d, out_m, out_v = {}, {}, {}, {}
    prev = {nm: None for nm, _, _ in BIG}
    todo = [(l, gname) for l in reversed(range(DEPTH)) for gname in ("ffn", "mix", "in")]
    after, small_parts = grad_x, None
    for l, gname in todo:
        if (l, gname) == todo[-1]:
            small_parts = _exchange(
                [small_repl, small_shard],
                [jax.ShapeDtypeStruct((NDEV, ROWS_REPL + 8, LANES), F32), jax.ShapeDtypeStruct((NDEV, ROWS_SHARD, LANES), F32)],
                lambda t, ref, q: ref if t == 0 else ref.at[q], _slot, "exchange_small_grads", after=after)
            after = small_parts[0]
        sems, s_thru, l_thru, win, local = rs[(l, gname)]
        owns, parts = _xchg_wait(sems, s_thru, l_thru, local, after, win, _slot, f"scatter_wait_l{l}_{gname}")
        for t, own, prt in zip(group_tids[gname], owns, parts):
            nm = BIG[t][0]
            rows = SHARD_ROWS.get(nm, PB[nm].shape[1])
            prev[nm] = _adamw(t, prt, own, me_arr, PB[nm], MB[nm], VB[nm], l, prev[nm], rows, f"adamw_{nm}_l{l}")
            after = prev[nm][1]
    for nm, _, _ in BIG:
        out_g[nm], out_d[nm], out_m[nm], out_v[nm] = [tr(a) if nm == "w_in" else a for a in prev[nm]]
    sm_g, sm_d, sm_m, sm_v, loss_all = _small_update(small_parts[0], small_parts[1], _rowpack(P), _rowpack(M),
                                                     _rowpack(V), "small_update")
    loss = loss_all[0, 0]
    for dst, src in ((out_g, sm_g), (out_d, sm_d), (out_m, sm_m), (out_v, sm_v)):
        dst.update(src)

    order = ["rel_bias", "attn_pre_norm", "w_in", "b_gate", "sinks", "w_br_a", "w_br_b", "w_br_c", "w_out",
             "attn_post_norm", "ffn_pre_norm", "w_up", "conv_w", "conv_b", "w_down", "ffn_post_norm"]
    return (loss, grad_x[None], *[out_g[k] for k in order], *[out_d[k] for k in order],
            *[out_m[k] for k in order], *[out_v[k] for k in order])
```

```python
import functools
import math

import numpy as np
import jax
import jax.numpy as jnp
from jax import lax
from jax.experimental import pallas as pl
from jax.experimental.pallas import tpu as pltpu

F32 = jnp.float32
BF16 = jnp.bfloat16

S = 2048
D = 1024
DEPTH = 2
NDEV = 8
HD = 64
BLK = 128
NB = S // BLK
A_GROUPS = ((128, 1), (512, 4), (2048, 16))
NUM_BUCKETS = 32
MAX_DISTANCE = 2048
N_BIAS_HEADS = 20
D_FF = 4096
IN_COLS = 6912
QKV_COLS = 3840
QKV_SLABS = QKV_COLS // 128
GATE_COLS = 3072
EPS = 1e-6
SCALE = HD ** -0.5
NEG = -1e30
LANES = 128

ADAM_LR = 0.001
ADAM_B1 = 0.9
ADAM_B2 = 0.999
ADAM_EPS = 1e-08
ADAM_WD = 0.01
ADAM_STEP = 10

VMEM_LIMIT = 56 * 1024 * 1024
MESH = pl.DeviceIdType.MESH
ANY = pl.BlockSpec(memory_space=pl.ANY)
SMEM = pl.BlockSpec(memory_space=pltpu.SMEM)


def _cp(*sem):
    return pltpu.CompilerParams(dimension_semantics=sem if sem else None, vmem_limit_bytes=VMEM_LIMIT)


def _dot(a, b, ca, cb):
    return lax.dot_general(a, b, (((ca,), (cb,)), ((), ())), preferred_element_type=F32)


def _mm(a, b, *, grid, a_spec, b_spec, out_shape, out_spec, ca, cb, acc_shape, name,
        a_slab=False, b_slab=False, out_slab=False, alias_out=None, after=None):
    nk = grid[2]

    def body(*refs):
        a_ref, b_ref = refs[0], refs[1]
        o_ref, acc_ref = refs[-2], refs[-1]
        k = pl.program_id(2)

        def load(ref, slab):
            if slab:
                return jnp.concatenate([ref[s] for s in range(ref.shape[0])], axis=1).astype(BF16)
            return ref[...].astype(BF16)

        def write(val):
            if out_slab:
                for s in range(o_ref.shape[0]):
                    o_ref[s] = val[:, s * LANES:(s + 1) * LANES].astype(o_ref.dtype)
            else:
                o_ref[...] = val.astype(o_ref.dtype)

        d = _dot(load(a_ref, a_slab), load(b_ref, b_slab), ca, cb)
        if nk == 1:
            write(d)
        elif direct:
            @pl.when(k == 0)
            def _():
                o_ref[...] = d

            @pl.when(k > 0)
            def _():
                o_ref[...] += d
        else:
            @pl.when(k == 0)
            def _():
                acc_ref[...] = d

            if nk > 2:
                @pl.when((k > 0) & (k < nk - 1))
                def _():
                    acc_ref[...] += d

            @pl.when(k == nk - 1)
            def _():
                write(acc_ref[...] + d)

    direct = (not out_slab) and out_shape.dtype == F32
    if nk == 1 or direct:
        acc_shape = (8, LANES)
    in_specs = [a_spec, b_spec]
    args = [a, b]
    aliases = {}
    if alias_out is not None:
        in_specs.append(ANY)
        args.append(alias_out)
        aliases = {2: 0}
    if after is not None:
        in_specs.append(ANY)
        args.append(after)
    return pl.pallas_call(
        body, grid=grid, in_specs=in_specs, out_specs=out_spec, out_shape=out_shape,
        scratch_shapes=[pltpu.VMEM(acc_shape, F32)], input_output_aliases=aliases,
        compiler_params=_cp("parallel", "parallel", "arbitrary"), name=name)(*args)


def _mm_nn(a, b, out_dtype, tm, tn, tk, name):
    m, kk = a.shape
    n = b.shape[1]
    return _mm(a, b, grid=(m // tm, n // tn, kk // tk),
               a_spec=pl.BlockSpec((tm, tk), lambda i, j, k: (i, k)),
               b_spec=pl.BlockSpec((tk, tn), lambda i, j, k: (k, j)),
               out_shape=jax.ShapeDtypeStruct((m, n), out_dtype),
               out_spec=pl.BlockSpec((tm, tn), lambda i, j, k: (i, j)),
               ca=1, cb=0, acc_shape=(tm, tn), name=name)


def _mm_nt(a, b, out_dtype, tm, tn, tk, name):
    m, kk = a.shape
    n = b.shape[0]
    return _mm(a, b, grid=(m // tm, n // tn, kk // tk),
               a_spec=pl.BlockSpec((tm, tk), lambda i, j, k: (i, k)),
               b_spec=pl.BlockSpec((tn, tk), lambda i, j, k: (j, k)),
               out_shape=jax.ShapeDtypeStruct((m, n), out_dtype),
               out_spec=pl.BlockSpec((tm, tn), lambda i, j, k: (i, j)),
               ca=1, cb=1, acc_shape=(tm, tn), name=name)


def _mm_tn(a, b, out_dtype, tm, tn, tk, name):
    kk, m = a.shape
    n = b.shape[1]
    return _mm(a, b, grid=(m // tm, n // tn, kk // tk),
               a_spec=pl.BlockSpec((tk, tm), lambda i, j, k: (k, i)),
               b_spec=pl.BlockSpec((tk, tn), lambda i, j, k: (k, j)),
               out_shape=jax.ShapeDtypeStruct((m, n), out_dtype),
               out_spec=pl.BlockSpec((tm, tn), lambda i, j, k: (i, j)),
               ca=0, cb=0, acc_shape=(tm, tn), name=name)


ROW_TILE = 256


def _rms(x, g):
    r = lax.rsqrt(jnp.mean(x * x, axis=-1, keepdims=True) + EPS)
    return x * r * g


def _prenorm(x, g, name):
    def body(x_ref, g_ref, o_ref):
        o_ref[...] = _rms(x_ref[...], g_ref[...]).astype(BF16)

    return pl.pallas_call(
        body, grid=(S // ROW_TILE,),
        in_specs=[pl.BlockSpec((ROW_TILE, D), lambda i: (i, 0)), pl.BlockSpec((1, D), lambda i: (0, 0))],
        out_specs=pl.BlockSpec((ROW_TILE, D), lambda i: (i, 0)),
        out_shape=jax.ShapeDtypeStruct((S, D), BF16), compiler_params=_cp("parallel"), name=name)(x, g)


def _postnorm_res(x, f, g_post, g_next, name):
    def body(x_ref, f_ref, gp_ref, gn_ref, xo_ref, ho_ref):
        xn = x_ref[...] + _rms(f_ref[...], gp_ref[...])
        xo_ref[...] = xn
        ho_ref[...] = _rms(xn, gn_ref[...]).astype(BF16)

    row = pl.BlockSpec((ROW_TILE, D), lambda i: (i, 0))
    vec = pl.BlockSpec((1, D), lambda i: (0, 0))
    return pl.pallas_call(
        body, grid=(S // ROW_TILE,), in_specs=[row, row, vec, vec], out_specs=[row, row],
        out_shape=[jax.ShapeDtypeStruct((S, D), F32), jax.ShapeDtypeStruct((S, D), BF16)],
        compiler_params=_cp("parallel"), name=name)(x, f, g_post, g_next)


def _norm_bwd(f, g, dys, res, out_dtype, name):
    ndy = len(dys)
    has_res = res is not None

    def body(*refs):
        f_ref, g_ref = refs[0], refs[1]
        dy_refs = refs[2:2 + ndy]
        res_ref = refs[2 + ndy] if has_res else None
        o_ref, dg_ref = refs[-2], refs[-1]
        fv = f_ref[...]
        dy = dy_refs[0][...].astype(F32)
        for r in dy_refs[1:]:
            dy = dy + r[...].astype(F32)
        r = lax.rsqrt(jnp.mean(fv * fv, axis=-1, keepdims=True) + EPS)
        n = fv * r
        dn = dy * g_ref[...]
        df = r * (dn - n * jnp.mean(dn * n, axis=-1, keepdims=True))
        if has_res:
            df = df + res_ref[...]
        o_ref[...] = df.astype(out_dtype)

        @pl.when(pl.program_id(0) == 0)
        def _():
            dg_ref[...] = jnp.zeros((1, D), F32)

        dg_ref[...] += jnp.sum(dy * n, axis=0, keepdims=True)

    row = pl.BlockSpec((ROW_TILE, D), lambda i: (i, 0))
    vec = pl.BlockSpec((1, D), lambda i: (0, 0))
    in_specs = [row, vec] + [row] * ndy + ([row] if has_res else [])
    args = [f, g] + list(dys) + ([res] if has_res else [])
    return pl.pallas_call(
        body, grid=(S // ROW_TILE,), in_specs=in_specs, out_specs=[row, vec],
        out_shape=[jax.ShapeDtypeStruct((S, D), out_dtype), jax.ShapeDtypeStruct((1, D), F32)],
        compiler_params=_cp("arbitrary"), name=name)(*args)


def _rms_bwd_rows(fv, g, dy):
    r = lax.rsqrt(jnp.mean(fv * fv, axis=-1, keepdims=True) + EPS)
    n = fv * r
    dn = dy * g
    return r * (dn - n * jnp.mean(dn * n, axis=-1, keepdims=True)), dy * n


def _norm_bwd_chain(f1, g1, dys, res, f2, g2, name):
    ndy = len(dys)

    def body(*refs):
        f1_ref, g1_ref = refs[0], refs[1]
        dy_refs = refs[2:2 + ndy]
        res_ref, f2_ref, g2_ref = refs[2 + ndy:5 + ndy]
        o1_ref, o2_ref, dg1_ref, dg2_ref = refs[-4:]
        dy = dy_refs[0][...].astype(F32)
        for r in dy_refs[1:]:
            dy = dy + r[...].astype(F32)
        df1, c1 = _rms_bwd_rows(f1_ref[...], g1_ref[...], dy)
        out1 = df1 + res_ref[...]
        o1_ref[...] = out1
        df2, c2 = _rms_bwd_rows(f2_ref[...], g2_ref[...], out1)
        o2_ref[...] = df2.astype(BF16)

        @pl.when(pl.program_id(0) == 0)
        def _():
            dg1_ref[...] = jnp.zeros((1, D), F32)
            dg2_ref[...] = jnp.zeros((1, D), F32)

        dg1_ref[...] += jnp.sum(c1, axis=0, keepdims=True)
        dg2_ref[...] += jnp.sum(c2, axis=0, keepdims=True)

    row = pl.BlockSpec((ROW_TILE, D), lambda i: (i, 0))
    vec = pl.BlockSpec((1, D), lambda i: (0, 0))
    return pl.pallas_call(
        body, grid=(S // ROW_TILE,), in_specs=[row, vec] + [row] * ndy + [row, row, vec],
        out_specs=[row, row, vec, vec],
        out_shape=[jax.ShapeDtypeStruct((S, D), F32), jax.ShapeDtypeStruct((S, D), BF16),
                   jax.ShapeDtypeStruct((1, D), F32), jax.ShapeDtypeStruct((1, D), F32)],
        compiler_params=_cp("arbitrary"), name=name)(f1, g1, *dys, res, f2, g2)


def _loss_head(x, f, g, target, name):
    def body(x_ref, f_ref, g_ref, t_ref, dy_ref, l_ref, df_ref, dg_ref):
        fv = f_ref[...]
        e = x_ref[...] + _rms(fv, g_ref[...]) - t_ref[...]
        dy = e * (1.0 / D)
        dy_ref[...] = dy
        df, c = _rms_bwd_rows(fv, g_ref[...], dy)
        df_ref[...] = df.astype(BF16)

        @pl.when(pl.program_id(0) == 0)
        def _():
            l_ref[...] = jnp.zeros((8, LANES), F32)
            dg_ref[...] = jnp.zeros((1, D), F32)

        l_ref[...] += jnp.sum(e * e) * (0.5 / D)
        dg_ref[...] += jnp.sum(c, axis=0, keepdims=True)

    row = pl.BlockSpec((ROW_TILE, D), lambda i: (i, 0))
    vec = pl.BlockSpec((1, D), lambda i: (0, 0))
    return pl.pallas_call(
        body, grid=(S // ROW_TILE,), in_specs=[row, row, vec, row],
        out_specs=[row, pl.BlockSpec((8, LANES), lambda i: (0, 0)), row, vec],
        out_shape=[jax.ShapeDtypeStruct((S, D), F32), jax.ShapeDtypeStruct((8, LANES), F32),
                   jax.ShapeDtypeStruct((S, D), BF16), jax.ShapeDtypeStruct((1, D), F32)],
        compiler_params=_cp("arbitrary"), name=name)(x, f, g, target)


def _bucket_tiles():
    a = np.arange(BLK)[:, None]
    b = np.arange(2 * BLK)[None, :]
    dist = a + BLK - b
    out = np.zeros((4, 2, BLK, 2 * BLK), np.int32)
    cfg = [(w // d, d) for w, d in A_GROUPS] + [(BLK - 1, 1)]
    for gi, (max_dist, d) in enumerate(cfg):
        band = (dist >= 0) & (dist <= max_dist)
        tok = np.maximum(dist, 0) * d
        nf = np.maximum(tok, 1).astype(np.float32)
        max_exact = NUM_BUCKETS // 2
        large = max_exact + (np.log(nf / np.float32(max_exact)) / np.float32(math.log(MAX_DISTANCE / max_exact))
                             * np.float32(NUM_BUCKETS - max_exact)).astype(np.int32)
        large = np.minimum(large, NUM_BUCKETS - 1)
        bkt = np.where(tok < max_exact, tok, large).astype(np.int32)
        full = np.where(band, bkt, -1)
        out[gi, 1] = full
        out[gi, 0] = np.where(b >= BLK, full, -1)
    return out


def _bias_tiles(rel_bias, buckets, name):
    def body(tab_ref, bkt_ref, o_ref):
        h = pl.program_id(0)
        bkt = bkt_ref[...]
        acc = jnp.zeros(bkt.shape, F32)
        for bb in range(NUM_BUCKETS):
            acc = jnp.where(bkt == bb, tab_ref[bb, h], acc)
        o_ref[...] = jnp.where(bkt < 0, NEG, acc)

    return pl.pallas_call(
        body, grid=(N_BIAS_HEADS,),
        in_specs=[SMEM, pl.BlockSpec((None, 2, BLK, 2 * BLK), lambda h: (jnp.minimum(h // 4, 3), 0, 0, 0))],
        out_specs=pl.BlockSpec((None, 2, BLK, 2 * BLK), lambda h: (h, 0, 0, 0)),
        out_shape=jax.ShapeDtypeStruct((N_BIAS_HEADS, 2, BLK, 2 * BLK), F32),
        compiler_params=_cp("arbitrary"), name=name)(rel_bias, buckets)


def _bias_grad(gs, buckets, name):
    ng = len(gs)

    def body(*refs):
        g_refs = refs[:ng]
        bkt_ref, o_ref = refs[ng], refs[ng + 1]
        h = pl.program_id(0)
        g = g_refs[0][...]
        for r in g_refs[1:]:
            g = g + r[...]
        bkt = bkt_ref[...]
        row = lax.broadcasted_iota(jnp.int32, (NUM_BUCKETS, LANES), 0)
        lane = lax.broadcasted_iota(jnp.int32, (NUM_BUCKETS, LANES), 1)

        @pl.when(h == 0)
        def _():
            o_ref[...] = jnp.zeros((NUM_BUCKETS, LANES), F32)

        acc = o_ref[...]
        for bb in range(NUM_BUCKETS):
            s = jnp.sum(jnp.where(bkt == bb, g, 0.0))
            acc = jnp.where((row == bb) & (lane == h), s, acc)
        o_ref[...] = acc

    g_spec = pl.BlockSpec((None, BLK, 2 * BLK), lambda h: (h, 0, 0))
    return pl.pallas_call(
        body, grid=(N_BIAS_HEADS,),
        in_specs=[g_spec] * ng + [pl.BlockSpec((None, None, BLK, 2 * BLK), lambda h: (jnp.minimum(h // 4, 3), 1, 0, 0))],
        out_specs=pl.BlockSpec((NUM_BUCKETS, LANES), lambda h: (0, 0)),
        out_shape=jax.ShapeDtypeStruct((NUM_BUCKETS, LANES), F32),
        compiler_params=_cp("arbitrary"), name=name)(*gs, buckets)


def _to_class_major(src_ref, dst_refs, d, fn=None):
    ln = S // d
    for r in range(d):
        v = src_ref[pl.ds(r, ln, stride=d), :] if d > 1 else src_ref[...]
        outs = fn(v) if fn is not None else (v,) * len(dst_refs)
        for dst, o in zip(dst_refs, outs):
            dst[pl.ds(r * ln, ln), :] = o.astype(dst.dtype)


def _head_masks(rows):
    lane = lax.broadcasted_iota(jnp.int32, (rows, LANES), 1)
    return lane < HD, lane >= HD


def _split_heads(v):
    m0, m1 = _head_masks(v.shape[0])
    return jnp.where(m0, v, 0.0), jnp.where(m1, v, 0.0)


def _dup_head(v, hi):
    m0, _ = _head_masks(v.shape[0])
    r = pltpu.roll(v, HD, 1)
    return jnp.where(m0, jnp.where(hi, r, v), jnp.where(hi, v, r))


def _block_rows(b, d):
    nbc = NB // d
    i = b % nbc
    r = b // nbc
    has_prev = (i > 0).astype(jnp.int32)
    prev = pl.multiple_of(jnp.maximum(b - 1, 0) * BLK, BLK)
    nat = i * (BLK * d) + r
    return has_prev, prev, nat


def _lane_halves(v0, v1):
    lane = lax.broadcasted_iota(jnp.int32, (v0.shape[0], LANES), 1)
    return jnp.where(lane < HD, v0, v1)


def _band_fwd(proj, bias, sinks, *, d, q0, k0, v0, npairs, bias0, shared_kv, name):
    def body(sink_ref, q_ref, k_ref, v_ref, b_ref, num_ref, st_ref, qz0, qz1, ks, vs):
        p = pl.program_id(0)
        kv = (lambda v: (_dup_head(v, p >= 2),)) if shared_kv else None
        _to_class_major(q_ref, (qz0, qz1), d, lambda v: _split_heads(v * SCALE))
        _to_class_major(k_ref, (ks,), d, kv)
        _to_class_major(v_ref, (vs,), d, kv)
        lane = lax.broadcasted_iota(jnp.int32, (BLK, LANES), 1)

        def blk(b, carry):
            has_prev, prev, nat = _block_rows(b, d)
            cur = pl.multiple_of(b * BLK, BLK)
            k2 = jnp.concatenate([ks[pl.ds(prev, BLK), :], ks[pl.ds(cur, BLK), :]], axis=0)
            v2 = jnp.concatenate([vs[pl.ds(prev, BLK), :], vs[pl.ds(cur, BLK), :]], axis=0)
            nums, ms, ls = [], [], []
            for hh, qz in enumerate((qz0, qz1)):
                z = _dot(qz[pl.ds(cur, BLK), :], k2, 1, 1) + b_ref[hh, has_prev]
                m = jnp.max(z, axis=1, keepdims=True)
                e = jnp.exp(z - m)
                l = jnp.sum(e, axis=1, keepdims=True)
                num = _dot(e.astype(BF16), v2, 1, 0)
                if shared_kv:
                    sink = sink_ref[0, 2 * p + hh]
                    mx = jnp.maximum(m, sink)
                    c = jnp.exp(m - mx)
                    zden = l * c + jnp.exp(sink - mx)
                    num = num * (c / zden)
                    m = mx + jnp.log(zden)
                ls.append(l)
                ms.append(m)
                nums.append(num)
            num_t = jnp.where(lane < HD, nums[0], nums[1])
            if shared_kv:
                st_t = jnp.where(lane < HD, ms[0], ms[1])
            else:
                st_t = jnp.where(lane < 32, ms[0], jnp.where(lane < 64, ls[0], jnp.where(lane < 96, ms[1], ls[1])))
            if d > 1:
                num_ref[pl.ds(nat, BLK, stride=d), :] = num_t
                st_ref[pl.ds(nat, BLK, stride=d), :] = st_t
            else:
                num_ref[pl.ds(cur, BLK), :] = num_t
                st_ref[pl.ds(cur, BLK), :] = st_t
            return carry

        lax.fori_loop(0, NB, blk, 0, unroll=True)

    slab = lambda off, per_pair: pl.BlockSpec((None, S, LANES), (lambda p: (off + p, 0, 0)) if per_pair else (lambda p: (off, 0, 0)))
    out = pl.BlockSpec((None, S, LANES), lambda p: (p, 0, 0))
    return pl.pallas_call(
        body, grid=(npairs,),
        in_specs=[SMEM, slab(q0, True), slab(k0, not shared_kv), slab(v0, not shared_kv),
                  pl.BlockSpec((None, 2, 2, BLK, 2 * BLK), lambda p: (bias0 + p, 0, 0, 0, 0))],
        out_specs=[out, out],
        out_shape=[jax.ShapeDtypeStruct((npairs, S, LANES), F32)] * 2,
        scratch_shapes=[pltpu.VMEM((S, LANES), BF16)] * 4,
        compiler_params=_cp("arbitrary"), name=name)(sinks, proj, proj, proj, bias)


def _combine_a(nums, stats, name):
    rt = 512

    def body(n0, n1, n2, s0, s1, s2, o_ref, l_ref):
        n_refs, s_refs = (n0, n1, n2), (s0, s1, s2)
        outs, lses = [], []
        for hh in range(2):
            ms = [s[:, 64 * hh:64 * hh + 1] for s in s_refs]
            ls = [s[:, 64 * hh + 32:64 * hh + 33] for s in s_refs]
            mx = jnp.maximum(jnp.maximum(ms[0], ms[1]), ms[2])
            cs = [jnp.exp(m - mx) for m in ms]
            z = cs[0] * ls[0] + cs[1] * ls[1] + cs[2] * ls[2]
            acc = cs[0] * n_refs[0][:, hh * HD:(hh + 1) * HD]
            acc = acc + cs[1] * n_refs[1][:, hh * HD:(hh + 1) * HD]
            acc = acc + cs[2] * n_refs[2][:, hh * HD:(hh + 1) * HD]
            outs.append(acc / z)
            lses.append(mx + jnp.log(z))
        o_ref[...] = jnp.concatenate(outs, axis=1)
        l_ref[...] = _lane_halves(lses[0], lses[1])

    spec = pl.BlockSpec((None, rt, LANES), lambda p, i: (p, i, 0))
    return pl.pallas_call(
        body, grid=(2, S // rt), in_specs=[spec] * 6, out_specs=[spec, spec],
        out_shape=[jax.ShapeDtypeStruct((2, S, LANES), F32)] * 2,
        compiler_params=_cp("parallel", "parallel"), name=name)(*nums, *stats)


def _band_bwd(proj, bias, o, do, lse, sinks, dqkv, *, d, q0, k0, v0, npairs, bias0, shared_kv, name):
    def body(sink_ref, q_ref, k_ref, v_ref, b_ref, o_ref, do_ref, lse_ref, dqkv_in, dqkv_ref, g_ref, ds_ref,
             qz0, qz1, ks, vs, doz0, doz1, ls0, ls1, dls0, dls1, stage, dq_nat, dk_cm, dv_cm, kv_nat, dk_acc, dv_acc,
             obuf, osem):
        p = pl.program_id(0)
        dq_ref, dk_ref, dv_ref = obuf.at[0], obuf.at[1], obuf.at[2]
        m0, m1 = _head_masks(S)
        kk = lax.broadcasted_iota(jnp.int32, (2 * LANES, LANES), 0) % LANES
        ll = lax.broadcasted_iota(jnp.int32, (2 * LANES, LANES), 1)
        hi, lo = _split2(do_ref[...] * o_ref[...])
        dl = _dot(jnp.concatenate([hi, lo], axis=1), ((kk < HD) == (ll < HD)).astype(BF16), 1, 0)
        if shared_kv:
            row8 = lax.broadcasted_iota(jnp.int32, (8, LANES), 0)
            lane8 = lax.broadcasted_iota(jnp.int32, (8, LANES), 1)
            sinkv = jnp.where(m0, sink_ref[0, 2 * p], sink_ref[0, 2 * p + 1])
            contrib = jnp.exp(sinkv - lse_ref[...]) * dl
            t = jnp.zeros((8, LANES), F32)
            for hh, mh in enumerate((m0, m1)):
                dsink = -jnp.sum(jnp.where(mh, contrib, 0.0)) * (1.0 / HD)
                t = jnp.where((row8 == 0) & (lane8 == hh), dsink, t)
            ds_ref[...] = t
        else:
            ds_ref[...] = jnp.zeros((8, LANES), F32)
        kv = (lambda v: (_dup_head(v, p >= 2),)) if shared_kv else None
        _to_class_major(q_ref, (qz0, qz1), d, lambda v: _split_heads(v * SCALE))
        _to_class_major(k_ref, (ks,), d, kv)
        _to_class_major(v_ref, (vs,), d, kv)
        _to_class_major(do_ref, (doz0, doz1), d, _split_heads)
        def spread(v):
            a0, a1 = _head_masks(v.shape[0])
            r = pltpu.roll(v, HD, 1)
            return jnp.where(a0, v, r), jnp.where(a1, v, r)

        _to_class_major(lse_ref, (ls0, ls1), d, spread)
        stage[...] = dl
        _to_class_major(stage, (dls0, dls1), d, spread)

        dk_cm[...] = jnp.zeros((S, LANES), F32)
        dv_cm[...] = jnp.zeros((S, LANES), F32)
        g_ref[...] = jnp.zeros((2, BLK, 2 * BLK), F32)
        lane = lax.broadcasted_iota(jnp.int32, (BLK, LANES), 1)

        def blk(b, carry):
            has_prev, prev, nat = _block_rows(b, d)
            cur = pl.multiple_of(b * BLK, BLK)
            k2 = jnp.concatenate([ks[pl.ds(prev, BLK), :], ks[pl.ds(cur, BLK), :]], axis=0)
            v2 = jnp.concatenate([vs[pl.ds(prev, BLK), :], vs[pl.ds(cur, BLK), :]], axis=0)
            dqs, dks, dvs = [], [], []
            for hh, (qz, doz, lsr, dlr) in enumerate(((qz0, doz0, ls0, dls0), (qz1, doz1, ls1, dls1))):
                qb = qz[pl.ds(cur, BLK), :]
                dob = doz[pl.ds(cur, BLK), :]
                lb = lsr[pl.ds(cur, BLK), :]
                dlb = dlr[pl.ds(cur, BLK), :]
                z = _dot(qb, k2, 1, 1) + b_ref[hh, has_prev]
                pr = jnp.exp(z - jnp.concatenate([lb, lb], axis=1))
                dp = _dot(dob, v2, 1, 1)
                dz = pr * (dp - jnp.concatenate([dlb, dlb], axis=1))
                g_ref[hh] += dz
                dzb = dz.astype(BF16)
                dqs.append(_dot(dzb, k2, 1, 0))
                dks.append(_dot(dzb, qb, 0, 0))
                dvs.append(_dot(pr.astype(BF16), dob, 0, 0))
            dq_t = jnp.where(lane < HD, dqs[0], dqs[1]) * SCALE
            dk_t = dks[0] + dks[1]
            dv_t = dvs[0] + dvs[1]
            dk_cm[pl.ds(prev, BLK), :] += dk_t[:BLK]
            dk_cm[pl.ds(cur, BLK), :] += dk_t[BLK:]
            dv_cm[pl.ds(prev, BLK), :] += dv_t[:BLK]
            dv_cm[pl.ds(cur, BLK), :] += dv_t[BLK:]
            if d > 1:
                dq_nat[pl.ds(nat, BLK, stride=d), :] = dq_t
            else:
                dq_nat[pl.ds(cur, BLK), :] = dq_t
            return carry

        lax.fori_loop(0, NB, blk, 0, unroll=True)
        dq_ref[...] = dq_nat[...].astype(BF16)

        def from_class_major(src, dst_ref):
            if d == 1:
                dst_ref[...] = src[...].astype(BF16)
            else:
                ln = S // d
                for r in range(d):
                    kv_nat[pl.ds(r, ln, stride=d), :] = src[pl.ds(r * ln, ln), :]
                dst_ref[...] = kv_nat[...].astype(BF16)

        def put(i, slab):
            return pltpu.make_async_copy(obuf.at[i], dqkv_ref.at[slab], osem.at[i])

        put(0, q0 + p).start()
        if not shared_kv:
            from_class_major(dk_cm, dk_ref)
            from_class_major(dv_cm, dv_ref)
            put(1, k0 + p).start()
            put(2, v0 + p).start()
            put(1, k0 + p).wait()
            put(2, v0 + p).wait()
        else:
            @pl.when(p == 0)
            def _():
                dk_acc[...] = jnp.zeros((S, LANES), F32)
                dv_acc[...] = jnp.zeros((S, LANES), F32)

            mine = m1 == (p >= 2)
            for cm, acc in ((dk_cm, dk_acc), (dv_cm, dv_acc)):
                val = cm[...]
                acc[...] += jnp.where(mine, val + pltpu.roll(val, HD, 1), 0.0)

            @pl.when(p == npairs - 1)
            def _():
                from_class_major(dk_acc, dk_ref)
                from_class_major(dv_acc, dv_ref)
                put(1, k0).start()
                put(2, v0).start()
                put(1, k0).wait()
                put(2, v0).wait()

        put(0, q0 + p).wait()

    slab = lambda off, per_pair: pl.BlockSpec((None, S, LANES), (lambda p: (off + p, 0, 0)) if per_pair else (lambda p: (off, 0, 0)))
    pair = pl.BlockSpec((None, S, LANES), lambda p: (p, 0, 0))
    return pl.pallas_call(
        body, grid=(npairs,),
        in_specs=[SMEM, slab(q0, True), slab(k0, not shared_kv), slab(v0, not shared_kv),
                  pl.BlockSpec((None, 2, 2, BLK, 2 * BLK), lambda p: (bias0 + p, 0, 0, 0, 0)),
                  pair, pair, pair, ANY],
        out_specs=[ANY,
                   pl.BlockSpec((None, 2, BLK, 2 * BLK), lambda p: (p, 0, 0, 0)),
                   pl.BlockSpec((None, 8, LANES), lambda p: (p, 0, 0))],
        out_shape=[jax.ShapeDtypeStruct(dqkv.shape, BF16),
                   jax.ShapeDtypeStruct((npairs, 2, BLK, 2 * BLK), F32),
                   jax.ShapeDtypeStruct((npairs, 8, LANES), F32)],
        scratch_shapes=[pltpu.VMEM((S, LANES), BF16)] * 6 + [pltpu.VMEM((S, LANES), F32)] * 11
        + [pltpu.VMEM((3, S, LANES), BF16), pltpu.SemaphoreType.DMA((3,))],
        input_output_aliases={8: 0},
        compiler_params=_cp("arbitrary"), name=name)(sinks, proj, proj, proj, bias, o, do, lse, dqkv)


KC = 512
NSUB = KC // BLK
QB = 512
QPG = KC // QB


def _split2(x):
    hi = x.astype(BF16)
    lo = (x - hi.astype(F32)).astype(BF16)
    return hi, lo


def _tri_ones(cmp):
    jj = lax.broadcasted_iota(jnp.int32, (2 * BLK, BLK), 0) % BLK
    ss = lax.broadcasted_iota(jnp.int32, (2 * BLK, BLK), 1)
    return jnp.concatenate([cmp(jj, ss).astype(BF16), jnp.ones((2 * BLK, BLK), BF16)], axis=1)


def _sub_sums(x, tri1):
    n = x.shape[0]
    st = jnp.concatenate([x[:, s * BLK:(s + 1) * BLK] for s in range(NSUB)], axis=0)
    hi, lo = _split2(st)
    r = _dot(jnp.concatenate([hi, lo], axis=1), tri1, 1, 0)
    return ([r[s * n:(s + 1) * n, :BLK] for s in range(NSUB)], [r[s * n:(s + 1) * n, BLK:] for s in range(NSUB)])


def _log_sig_pair(z):
    lb = jnp.minimum(z, 0.0) - jnp.log1p(jnp.exp(-jnp.abs(z)))
    return lb, lb - z


QGROUPS = NB // NSUB


def _stick_fwd(proj, *, q0, k0, v0, name):
    def body(q_ref, k_ref, v_ref, o_ref, t_ref, qs, ks, vs):
        qs[...] = (q_ref[...] * SCALE).astype(BF16)
        ks[...] = k_ref[...].astype(BF16)
        vs[...] = v_ref[...].astype(BF16)
        tri1 = _tri_ones(lambda j, s: j > s)
        col = lax.broadcasted_iota(jnp.int32, (QB, KC), 1)
        rowi = lax.broadcasted_iota(jnp.int32, (QB, KC), 0)

        for qg in range(QGROUPS):
            def qblock(ii, carry0, qg=qg):
                t0 = pl.multiple_of((qg * QPG + ii) * QB, QB)
                qb = qs[pl.ds(t0, QB), :]
                accs = [jnp.zeros((QB, HD), F32)] * 2
                runs = [jnp.zeros((QB, BLK), F32)] * 2
                for c in reversed(range(qg + 1)):
                    s0 = c * KC
                    diag = c == qg
                    before = (s0 + col) < (t0 + rowi) if diag else None
                    for hh in range(2):
                        kh = ks[s0:s0 + KC, hh * HD:(hh + 1) * HD]
                        vh = vs[s0:s0 + KC, hh * HD:(hh + 1) * HD]
                        lb, lk = _log_sig_pair(_dot(qb[:, hh * HD:(hh + 1) * HD], kh, 1, 1))
                        if diag:
                            lk = jnp.where(before, lk, 0.0)
                        suf, tot = _sub_sums(lk, tri1)
                        ws, run = [], runs[hh]
                        for s in reversed(range(NSUB)):
                            ws.append(jnp.exp(lb[:, s * BLK:(s + 1) * BLK] + suf[s] + run))
                            run = run + tot[s]
                        w = jnp.concatenate(ws[::-1], axis=1)
                        if diag:
                            w = jnp.where(before, w, 0.0)
                        accs[hh] = accs[hh] + _dot(w.astype(BF16), vh, 1, 0)
                        runs[hh] = run
                o_ref[pl.ds(t0, QB), :] = jnp.concatenate(accs, axis=1)
                t_ref[pl.ds(t0, QB), :] = _lane_halves(runs[0], runs[1])
                return carry0

            lax.fori_loop(0, QPG, qblock, 0)

    slab = lambda off: pl.BlockSpec((None, S, LANES), lambda p: (off + p, 0, 0))
    out = pl.BlockSpec((None, S, LANES), lambda p: (p, 0, 0))
    return pl.pallas_call(
        body, grid=(2,), in_specs=[slab(q0), slab(k0), slab(v0)], out_specs=[out, out],
        out_shape=[jax.ShapeDtypeStruct((2, S, LANES), F32)] * 2,
        scratch_shapes=[pltpu.VMEM((S, LANES), BF16)] * 3,
        compiler_params=_cp("arbitrary"), name=name)(proj, proj, proj)


def _stick_bwd(proj, do, tot, dqkv, *, q0, k0, v0, name):
    def body(q_ref, k_ref, v_ref, do_ref, t_ref, dqkv_in, dqkv_ref, qs, ks, vs, dos, dk_acc, dv_acc, obuf, osem):
        p = pl.program_id(0)
        dq_ref, dk_ref, dv_ref = obuf.at[0], obuf.at[1], obuf.at[2]
        qs[...] = (q_ref[...] * SCALE).astype(BF16)
        ks[...] = k_ref[...].astype(BF16)
        vs[...] = v_ref[...].astype(BF16)
        dos[...] = do_ref[...].astype(BF16)
        dk_acc[...] = jnp.zeros((2, S, HD), F32)
        dv_acc[...] = jnp.zeros((2, S, HD), F32)
        tri_inc = _tri_ones(lambda j, s: j <= s)
        tri_exc = _tri_ones(lambda j, s: j < s)
        col = lax.broadcasted_iota(jnp.int32, (QB, KC), 1)
        rowi = lax.broadcasted_iota(jnp.int32, (QB, KC), 0)

        for qg in range(QGROUPS):
            def qblock(ii, carry0, qg=qg):
                t0 = pl.multiple_of((qg * QPG + ii) * QB, QB)
                qb = qs[pl.ds(t0, QB), :]
                dob = dos[pl.ds(t0, QB), :]
                tb = t_ref[pl.ds(t0, QB), :]
                dqs = [jnp.zeros((QB, HD), F32)] * 2
                pruns = [jnp.zeros((QB, BLK), F32)] * 2
                eruns = [jnp.zeros((QB, BLK), F32)] * 2
                for c in range(qg + 1):
                    s0 = c * KC
                    diag = c == qg
                    before = (s0 + col) < (t0 + rowi) if diag else None
                    for hh in range(2):
                        qh = qb[:, hh * HD:(hh + 1) * HD]
                        doh = dob[:, hh * HD:(hh + 1) * HD]
                        tt = tb[:, 64 * hh:64 * hh + 1]
                        kh = ks[s0:s0 + KC, hh * HD:(hh + 1) * HD]
                        vh = vs[s0:s0 + KC, hh * HD:(hh + 1) * HD]
                        lb, lk = _log_sig_pair(_dot(qh, kh, 1, 1))
                        if diag:
                            lk = jnp.where(before, lk, 0.0)
                        pin, ptot = _sub_sums(lk, tri_inc)
                        ws, prun = [], pruns[hh]
                        for s in range(NSUB):
                            ws.append(jnp.exp(lb[:, s * BLK:(s + 1) * BLK] + (tt - (pin[s] + prun))))
                            prun = prun + ptot[s]
                        w = jnp.concatenate(ws, axis=1)
                        if diag:
                            w = jnp.where(before, w, 0.0)
                        e = w * _dot(doh, vh, 1, 1)
                        pex, etot = _sub_sums(e, tri_exc)
                        cs, erun = [], eruns[hh]
                        for s in range(NSUB):
                            cs.append(pex[s] + erun)
                            erun = erun + etot[s]
                        sig = jnp.exp(lb)
                        dz = e * (1.0 - sig) - jnp.concatenate(cs, axis=1) * sig
                        if diag:
                            dz = jnp.where(before, dz, 0.0)
                        dz = dz.astype(BF16)
                        dqs[hh] = dqs[hh] + _dot(dz, kh, 1, 0)
                        dk_acc[hh, s0:s0 + KC, :] += _dot(dz, qh, 0, 0)
                        dv_acc[hh, s0:s0 + KC, :] += _dot(w.astype(BF16), doh, 0, 0)
                        pruns[hh], eruns[hh] = prun, erun
                dq_ref[pl.ds(t0, QB), :] = (jnp.concatenate(dqs, axis=1) * SCALE).astype(BF16)
                return carry0

            lax.fori_loop(0, QPG, qblock, 0)
        dk_ref[...] = jnp.concatenate([dk_acc[0], dk_acc[1]], axis=1).astype(BF16)
        dv_ref[...] = jnp.concatenate([dv_acc[0], dv_acc[1]], axis=1).astype(BF16)
        puts = [pltpu.make_async_copy(obuf.at[i], dqkv_ref.at[off + p], osem.at[i]) for i, off in enumerate((q0, k0, v0))]
        for cp in puts:
            cp.start()
        for cp in puts:
            cp.wait()

    slab = lambda off: pl.BlockSpec((None, S, LANES), lambda p: (off + p, 0, 0))
    pair = pl.BlockSpec((None, S, LANES), lambda p: (p, 0, 0))
    return pl.pallas_call(
        body, grid=(2,), in_specs=[slab(q0), slab(k0), slab(v0), pair, pair, ANY], out_specs=ANY,
        out_shape=jax.ShapeDtypeStruct(dqkv.shape, BF16),
        scratch_shapes=[pltpu.VMEM((S, LANES), BF16)] * 4 + [pltpu.VMEM((2, S, HD), F32)] * 2
        + [pltpu.VMEM((3, S, LANES), BF16), pltpu.SemaphoreType.DMA((3,))],
        input_output_aliases={5: 0},
        compiler_params=_cp("arbitrary"), name=name)(proj, proj, proj, do, tot, dqkv)


def _cat_slabs(ref):
    return jnp.concatenate([ref[s] for s in range(ref.shape[0])], axis=1)


def _merge_fwd(o_a, o_b, o_c, gates, b_gate, wa, wb, wc, w_out, name):
    tm = ROW_TILE

    def body(oa_ref, ob_ref, oc_ref, g_ref, bg_ref, wa_ref, wb_ref, wc_ref, wo_ref, mg_ref, mo_ref):
        acc = jnp.zeros((tm, D), F32)
        for i, (o_ref, w_ref) in enumerate(((oa_ref, wa_ref), (ob_ref, wb_ref), (oc_ref, wc_ref))):
            pr = _dot(_cat_slabs(o_ref).astype(BF16), w_ref[...], 1, 0)
            sg = jax.nn.sigmoid(g_ref[:, i * D:(i + 1) * D] + bg_ref[i:i + 1, :])
            acc = acc + sg * pr
        mg = acc.astype(BF16)
        mg_ref[...] = mg
        mo_ref[...] = _dot(mg, wo_ref[...], 1, 0)

    slabs = lambda n: pl.BlockSpec((n, tm, LANES), lambda i: (0, i, 0))
    full = lambda r, c: pl.BlockSpec((r, c), lambda i: (0, 0))
    row = pl.BlockSpec((tm, D), lambda i: (i, 0))
    return pl.pallas_call(
        body, grid=(S // tm,),
        in_specs=[slabs(2), slabs(4), slabs(2), pl.BlockSpec((tm, GATE_COLS), lambda i: (i, 0)), full(3, D),
                  full(256, D), full(512, D), full(256, D), full(D, D)],
        out_specs=[row, row],
        out_shape=[jax.ShapeDtypeStruct((S, D), BF16), jax.ShapeDtypeStruct((S, D), F32)],
        compiler_params=_cp("parallel"), name=name)(o_a, o_b, o_c, gates, b_gate, wa, wb, wc, w_out)


def _merge_bwd(d_mo, o_a, o_b, o_c, gates, b_gate, wa, wb, wc, w_out, name):
    tm = ROW_TILE
    nsteps = S // tm

    def body(dmo_ref, oa_ref, ob_ref, oc_ref, g_ref, bg_ref, wa_ref, wb_ref, wc_ref, wo_ref,
             doa_ref, dob_ref, doc_ref, dg_ref, dwa_ref, dwb_ref, dwc_ref, dbg_ref, acc_a, acc_b, acc_c):
        @pl.when(pl.program_id(0) == 0)
        def _():
            acc_a[...] = jnp.zeros(acc_a.shape, F32)
            acc_b[...] = jnp.zeros(acc_b.shape, F32)
            acc_c[...] = jnp.zeros(acc_c.shape, F32)
            dbg_ref[...] = jnp.zeros(dbg_ref.shape, F32)

        dmg = _dot(dmo_ref[...], wo_ref[...], 1, 1)
        trip = ((oa_ref, wa_ref, doa_ref, acc_a), (ob_ref, wb_ref, dob_ref, acc_b), (oc_ref, wc_ref, doc_ref, acc_c))
        for i, (o_ref, w_ref, do_ref, dw_ref) in enumerate(trip):
            ob = _cat_slabs(o_ref).astype(BF16)
            pr = _dot(ob, w_ref[...], 1, 0)
            sg = jax.nn.sigmoid(g_ref[:, i * D:(i + 1) * D] + bg_ref[i:i + 1, :])
            dgate = dmg * pr * sg * (1.0 - sg)
            dg_ref[:, i * D:(i + 1) * D] = dgate.astype(BF16)
            dbg_ref[i:i + 1, :] += jnp.sum(dgate, axis=0, keepdims=True)
            dpr = (dmg * sg).astype(BF16)
            do = _dot(dpr, w_ref[...], 1, 1)
            for s in range(do_ref.shape[0]):
                do_ref[s] = do[:, s * LANES:(s + 1) * LANES]
            dw_ref[...] += _dot(ob, dpr, 0, 0)

        @pl.when(pl.program_id(0) == nsteps - 1)
        def _():
            dwa_ref[...] = acc_a[...].astype(BF16)
            dwb_ref[...] = acc_b[...].astype(BF16)
            dwc_ref[...] = acc_c[...].astype(BF16)

    slabs = lambda n: pl.BlockSpec((n, tm, LANES), lambda i: (0, i, 0))
    full = lambda r, c: pl.BlockSpec((r, c), lambda i: (0, 0))
    row = pl.BlockSpec((tm, D), lambda i: (i, 0))
    return pl.pallas_call(
        body, grid=(S // tm,),
        in_specs=[row, slabs(2), slabs(4), slabs(2), pl.BlockSpec((tm, GATE_COLS), lambda i: (i, 0)), full(3, D),
                  full(256, D), full(512, D), full(256, D), full(D, D)],
        out_specs=[slabs(2), slabs(4), slabs(2), pl.BlockSpec((tm, GATE_COLS), lambda i: (i, 0)),
                   full(256, D), full(512, D), full(256, D), full(3, D)],
        out_shape=[jax.ShapeDtypeStruct((2, S, LANES), F32), jax.ShapeDtypeStruct((4, S, LANES), F32),
                   jax.ShapeDtypeStruct((2, S, LANES), F32), jax.ShapeDtypeStruct((S, GATE_COLS), BF16),
                   jax.ShapeDtypeStruct((256, D), BF16), jax.ShapeDtypeStruct((512, D), BF16),
                   jax.ShapeDtypeStruct((256, D), BF16), jax.ShapeDtypeStruct((3, D), F32)],
        scratch_shapes=[pltpu.VMEM((256, D), F32), pltpu.VMEM((512, D), F32), pltpu.VMEM((256, D), F32)],
        compiler_params=_cp("arbitrary"), name=name)(d_mo, o_a, o_b, o_c, gates, b_gate, wa, wb, wc, w_out)


FC = 256
GELU_K = math.sqrt(2.0 / math.pi)
GELU_C = 0.044715


RC = 64
NRC = S // RC


def _down(tail, cur, n):
    row = lax.broadcasted_iota(jnp.int32, tail.shape, 0)
    rolled = pltpu.roll(cur, n, 0)
    first = jnp.where(row < n, pltpu.roll(tail, n, 0), rolled[0:8])
    return jnp.concatenate([first, rolled[8:]], axis=0)


def _up(cur, head, n):
    row = lax.broadcasted_iota(jnp.int32, head.shape, 0)
    rolled = pltpu.roll(cur, RC - n, 0)
    last = jnp.where(row >= 8 - n, pltpu.roll(head, 8 - n, 0), rolled[RC - 8:])
    return jnp.concatenate([rolled[:RC - 8], last], axis=0)


def _conv_chunk(load, j, w_ref, b_ref, half):
    r0 = pl.multiple_of(j * RC, RC)
    cur = load(r0, RC).astype(F32)
    tail = load(pl.multiple_of(jnp.maximum(r0 - 16, 0), 16), 16).astype(F32)[8:16]
    tail = jnp.where(j > 0, tail, 0.0)
    d1 = _down(tail, cur, 1)
    d2 = _down(tail, cur, 2)
    y = w_ref[0:1, half, :] * d2 + w_ref[1:2, half, :] * d1 + w_ref[2:3, half, :] * cur + b_ref[half:half + 1, :]
    return y, cur, d1, d2


def _chunk(j):
    return pl.ds(pl.multiple_of(j * RC, RC), RC)


def _fold8(x):
    return jnp.sum(x.reshape(RC // 8, 8, x.shape[-1]), axis=0)


def _ffn_act(u, conv_w, conv_b, name):
    def body(u_ref, w_ref, b_ref, a_ref, y_ref):
        def step(j, carry):
            yg = _conv_chunk(lambda r, n: u_ref[0, pl.ds(r, n), :], j, w_ref, b_ref, 0)[0]
            yv = _conv_chunk(lambda r, n: u_ref[1, pl.ds(r, n), :], j, w_ref, b_ref, 1)[0]
            th = jnp.tanh(GELU_K * (yg + GELU_C * yg * yg * yg))
            a_ref[_chunk(j), :] = (0.5 * yg * (1.0 + th) * yv).astype(BF16)
            y_ref[0, _chunk(j), :] = yg.astype(BF16)
            y_ref[1, _chunk(j), :] = yv.astype(BF16)
            return carry

        lax.fori_loop(0, NRC, step, 0)

    return pl.pallas_call(
        body, grid=(D_FF // FC,),
        in_specs=[pl.BlockSpec((2, S, FC), lambda j: (0, 0, j)), pl.BlockSpec((3, 2, FC), lambda j: (0, 0, j)),
                  pl.BlockSpec((2, FC), lambda j: (0, j))],
        out_specs=[pl.BlockSpec((S, FC), lambda j: (0, j)), pl.BlockSpec((2, S, FC), lambda j: (0, 0, j))],
        out_shape=[jax.ShapeDtypeStruct((S, D_FF), BF16), jax.ShapeDtypeStruct((2, S, D_FF), BF16)],
        compiler_params=_cp("parallel"), name=name)(u, conv_w, conv_b)


def _ffn_act_bwd(u, y, d_a, conv_w, name):
    def body(u_ref, y_ref, da_ref, w_ref, du_ref, dw_ref, db_ref, dy_s):
        def first(j, acc):
            yg = y_ref[0, _chunk(j), :].astype(F32)
            yv = y_ref[1, _chunk(j), :].astype(F32)
            th = jnp.tanh(GELU_K * (yg + GELU_C * yg * yg * yg))
            gelu = 0.5 * yg * (1.0 + th)
            dgelu = 0.5 * (1.0 + th) + 0.5 * yg * (1.0 - th * th) * GELU_K * (1.0 + 3.0 * GELU_C * yg * yg)
            da = da_ref[_chunk(j), :].astype(F32)
            dyg = da * yv * dgelu
            dyv = da * gelu
            dy_s[0, _chunk(j), :] = dyg
            dy_s[1, _chunk(j), :] = dyv
            return acc[0] + _fold8(dyg), acc[1] + _fold8(dyv)

        zero = jnp.zeros((8, FC), F32)
        accb = lax.fori_loop(0, NRC, first, (zero, zero))
        for half in range(2):
            db_ref[half:half + 1, :] = jnp.sum(accb[half], axis=0, keepdims=True)

        def second(j, acc):
            new = []
            for half in range(2):
                cur = dy_s[half, _chunk(j), :]
                h0 = pl.multiple_of(jnp.minimum((j + 1) * RC, S - 8), 8)
                head = jnp.where(j < NRC - 1, dy_s[half, pl.ds(h0, 8), :], 0.0)
                up1 = _up(cur, head, 1)
                up2 = _up(cur, head, 2)
                du = w_ref[2:3, half, :] * cur + w_ref[1:2, half, :] * up1 + w_ref[0:1, half, :] * up2
                du_ref[half, _chunk(j), :] = du.astype(BF16)
                uu = u_ref[half, _chunk(j), :].astype(F32)
                new += [_fold8(up2 * uu), _fold8(up1 * uu), _fold8(cur * uu)]
            return tuple(a + n for a, n in zip(acc, new))

        accw = lax.fori_loop(0, NRC, second, tuple(zero for _ in range(6)))
        for half in range(2):
            for k in range(3):
                dw_ref[k:k + 1, half, :] = jnp.sum(accw[3 * half + k], axis=0, keepdims=True)

    return pl.pallas_call(
        body, grid=(D_FF // FC,),
        in_specs=[pl.BlockSpec((2, S, FC), lambda j: (0, 0, j)), pl.BlockSpec((2, S, FC), lambda j: (0, 0, j)),
                  pl.BlockSpec((S, FC), lambda j: (0, j)), pl.BlockSpec((3, 2, FC), lambda j: (0, 0, j))],
        out_specs=[pl.BlockSpec((2, S, FC), lambda j: (0, 0, j)), pl.BlockSpec((3, 2, FC), lambda j: (0, 0, j)),
                   pl.BlockSpec((2, FC), lambda j: (0, j))],
        out_shape=[jax.ShapeDtypeStruct((2, S, D_FF), BF16), jax.ShapeDtypeStruct((3, 2, D_FF), F32),
                   jax.ShapeDtypeStruct((2, D_FF), F32)],
        scratch_shapes=[pltpu.VMEM((2, S, FC), F32)],
        compiler_params=_cp("parallel"), name=name)(u, y, d_a, conv_w)


def _layer_fwd(x, h1, w, bias, lname):
    n = lambda s: f"{lname}_{s}"
    w.need("in", h1)
    tn = 768
    proj = _mm(h1, w["w_in"], grid=(1, QKV_COLS // tn, 1),
               a_spec=pl.BlockSpec((S, D), lambda i, j, k: (i, 0)),
               b_spec=pl.BlockSpec((tn, D), lambda i, j, k: (j, 0)),
               out_shape=jax.ShapeDtypeStruct((QKV_SLABS, S, LANES), F32),
               out_spec=pl.BlockSpec((tn // LANES, S, LANES), lambda i, j, k: (j, i, 0)),
               ca=1, cb=1, acc_shape=(S, tn), out_slab=True, name=n("proj_qkv"))
    gates = _mm(h1, w["w_in"], grid=(1, GATE_COLS // tn, 1),
                a_spec=pl.BlockSpec((S, D), lambda i, j, k: (i, 0)),
                b_spec=pl.BlockSpec((tn, D), lambda i, j, k: (j + QKV_COLS // tn, 0)),
                out_shape=jax.ShapeDtypeStruct((S, GATE_COLS), BF16),
                out_spec=pl.BlockSpec((S, tn), lambda i, j, k: (i, j)),
                ca=1, cb=1, acc_shape=(S, tn), name=n("proj_gate"))
    nums, stats = [], []
    for g, (_, d) in enumerate(A_GROUPS):
        nm, st = _band_fwd(proj, bias, w["sinks"], d=d, q0=2 * g, k0=6 + 2 * g, v0=12 + 2 * g, npairs=2, bias0=2 * g,
                           shared_kv=False, name=n(f"attn_a{g}_fwd"))
        nums.append(nm)
        stats.append(st)
    o_a, lse_a = _combine_a(nums, stats, n("attn_a_combine"))
    o_b, lse_b = _band_fwd(proj, bias, w["sinks"], d=1, q0=18, k0=22, v0=23, npairs=4, bias0=6, shared_kv=True,
                           name=n("attn_b_fwd"))
    o_c, tot_c = _stick_fwd(proj, q0=24, k0=26, v0=28, name=n("attn_c_fwd"))
    w.need("mix", tot_c)
    merged, mo = _merge_fwd(o_a, o_b, o_c, gates, w["b_gate"], w["w_br_a"], w["w_br_b"], w["w_br_c"], w["w_out"], n("merge_fwd"))
    x2, h2 = _postnorm_res(x, mo, w["attn_post_norm"], w["ffn_pre_norm"], n("attn_post"))
    w.need("ffn", h2)
    u = _mm(h2, w["w_up"], grid=(1, 2 * D_FF // 1024, 1),
            a_spec=pl.BlockSpec((S, D), lambda i, j, k: (i, 0)),
            b_spec=pl.BlockSpec((D, 1024), lambda i, j, k: (0, j)),
            out_shape=jax.ShapeDtypeStruct((2, S, D_FF), BF16),
            out_spec=pl.BlockSpec((None, S, 1024), lambda i, j, k: (j // 4, i, j % 4)),
            ca=1, cb=0, acc_shape=(S, 1024), name=n("ffn_up"))
    a, y = _ffn_act(u, w["conv_w"], w["conv_b"], n("ffn_act"))
    fo = _mm_nn(a, w["w_down"], F32, 1024, 1024, 2048, n("ffn_down"))
    saved = dict(x=x, h1=h1, proj=proj, gates=gates, o_a=o_a, lse_a=lse_a, o_b=o_b, lse_b=lse_b, o_c=o_c, tot_c=tot_c,
                 merged=merged, mo=mo, x2=x2, h2=h2, u=u, y=y, a=a, fo=fo)
    return saved


def _layer_bwd(dx3, sv, w, bias, lname, tok=None, on_part=None, d_fo=None, below=None):
    n = lambda s: f"{lname}_{s}"
    g = {}

    def part(group, vec):
        t = on_part(group, g) if on_part is not None else None
        return vec if t is None else vec + t

    if d_fo is None:
        gain = w["ffn_post_norm"] if tok is None else w["ffn_post_norm"] + tok
        d_fo, g["ffn_post_norm"] = _norm_bwd(sv["fo"], gain, [dx3], None, BF16, n("ffn_post_bwd"))
    else:
        d_fo, g["ffn_post_norm"] = d_fo
    d_a = _mm_nt(d_fo, w["w_down"], BF16, S, 1024, 1024, n("ffn_down_bwd_x"))
    g["w_down"] = _mm_tn(sv["a"], d_fo, BF16, 1024, 1024, S, n("ffn_down_bwd_w"))
    d_u, dcw, dcb = _ffn_act_bwd(sv["u"], sv["y"], d_a, w["conv_w"], n("ffn_act_bwd"))
    g["conv_w"] = dcw.reshape(3, 2 * D_FF)
    g["conv_b"] = dcb.reshape(1, 2 * D_FF)
    g["w_up"] = _mm(sv["h2"], d_u, grid=(1, 2 * D_FF // 1024, 1),
                    a_spec=pl.BlockSpec((S, D), lambda i, j, k: (k, 0)),
                    b_spec=pl.BlockSpec((None, S, 1024), lambda i, j, k: (j // 4, k, j % 4)),
                    out_shape=jax.ShapeDtypeStruct((D, 2 * D_FF), BF16),
                    out_spec=pl.BlockSpec((D, 1024), lambda i, j, k: (0, j)),
                    ca=0, cb=0, acc_shape=(D, 1024), name=n("ffn_up_bwd_w"))
    tok_ffn = on_part("ffn", g) if on_part is not None else None
    d_h2 = _mm(d_u, w["w_up"], grid=(S // 1024, 1, 2),
               a_spec=pl.BlockSpec((None, 1024, D_FF), lambda i, j, k: (k, i, 0)),
               b_spec=pl.BlockSpec((D, D_FF), lambda i, j, k: (0, k)),
               out_shape=jax.ShapeDtypeStruct((S, D), F32),
               out_spec=pl.BlockSpec((1024, D), lambda i, j, k: (i, 0)),
               ca=1, cb=1, acc_shape=(1024, D), after=tok_ffn, name=n("ffn_up_bwd_x"))
    dx2, d_mo, g["ffn_pre_norm"], g["attn_post_norm"] = _norm_bwd_chain(
        sv["x2"], w["ffn_pre_norm"], [d_h2], dx3, sv["mo"], w["attn_post_norm"], n("ffn_pre_attn_post_bwd"))
    g["w_out"] = _mm_tn(sv["merged"], d_mo, BF16, 1024, 1024, S, n("out_bwd_w"))
    do_a, do_b, do_c, d_gates, dwa, dwb, dwc, g["b_gate"] = _merge_bwd(
        d_mo, sv["o_a"], sv["o_b"], sv["o_c"], sv["gates"], w["b_gate"], w["w_br_a"], w["w_br_b"], w["w_br_c"],
        w["w_out"], n("merge_bwd"))
    g["w_br_a"], g["w_br_b"], g["w_br_c"] = dwa, dwb, dwc
    sinks = part("mix", w["sinks"])
    proj = sv["proj"]
    dqkv = lax.empty((QKV_SLABS, S, LANES), BF16)
    gbias = []
    for gi, (_, d) in enumerate(A_GROUPS):
        dqkv, gg, _ = _band_bwd(proj, bias, sv["o_a"], do_a, sv["lse_a"], sinks, dqkv, d=d, q0=2 * gi, k0=6 + 2 * gi,
                                v0=12 + 2 * gi, npairs=2, bias0=2 * gi, shared_kv=False, name=n(f"attn_a{gi}_bwd"))
        gbias.append(gg)
    dqkv, ggb, dsink = _band_bwd(proj, bias, sv["o_b"], do_b, sv["lse_b"], sinks, dqkv, d=1, q0=18, k0=22, v0=23,
                                 npairs=4, bias0=6, shared_kv=True, name=n("attn_b_bwd"))
    gbias.append(ggb)
    g["bias_g"] = jnp.concatenate(gbias, axis=0).reshape(N_BIAS_HEADS, BLK, 2 * BLK)
    g["sinks"] = dsink[:, 0, :2].reshape(1, 8)
    dqkv = _stick_bwd(proj, do_c, sv["tot_c"], dqkv, q0=24, k0=26, v0=28, name=n("attn_c_bwd"))
    ts = 6
    tsx = QKV_SLABS
    dw_in = _mm(dqkv, sv["h1"], grid=(QKV_SLABS // ts, 1, 1),
                a_spec=pl.BlockSpec((ts, S, LANES), lambda i, j, k: (i, k, 0)),
                b_spec=pl.BlockSpec((S, D), lambda i, j, k: (k, 0)),
                out_shape=jax.ShapeDtypeStruct((IN_COLS, D), BF16),
                out_spec=pl.BlockSpec((ts * LANES, D), lambda i, j, k: (i, 0)),
                ca=0, cb=0, acc_shape=(ts * LANES, D), a_slab=True, name=n("in_bwd_w_qkv"))
    g["w_in"] = _mm(d_gates, sv["h1"], grid=(GATE_COLS // 768, 1, 1),
                    a_spec=pl.BlockSpec((S, 768), lambda i, j, k: (k, i)),
                    b_spec=pl.BlockSpec((S, D), lambda i, j, k: (k, 0)),
                    out_shape=jax.ShapeDtypeStruct((IN_COLS, D), BF16),
                    out_spec=pl.BlockSpec((768, D), lambda i, j, k: (i + QKV_COLS // 768, 0)),
                    ca=0, cb=0, acc_shape=(768, D), alias_out=dw_in, name=n("in_bwd_w_gate"))
    tok_in = on_part("in", g) if on_part is not None else None
    d_h1a = _mm(dqkv, w["w_in"], grid=(S // 1024, 1, QKV_SLABS // tsx),
                a_spec=pl.BlockSpec((tsx, 1024, LANES), lambda i, j, k: (k, i, 0)),
                b_spec=pl.BlockSpec((tsx * LANES, D), lambda i, j, k: (k, 0)),
                out_shape=jax.ShapeDtypeStruct((S, D), F32),
                out_spec=pl.BlockSpec((1024, D), lambda i, j, k: (i, 0)),
                ca=1, cb=0, acc_shape=(1024, D), a_slab=True, after=tok_in, name=n("in_bwd_x_qkv"))
    d_h1b = _mm(d_gates, w["w_in"], grid=(S // 1024, 1, GATE_COLS // 768),
                a_spec=pl.BlockSpec((1024, 768), lambda i, j, k: (i, k)),
                b_spec=pl.BlockSpec((768, D), lambda i, j, k: (k + QKV_COLS // 768, 0)),
                out_shape=jax.ShapeDtypeStruct((S, D), F32),
                out_spec=pl.BlockSpec((1024, D), lambda i, j, k: (i, 0)),
                ca=1, cb=0, acc_shape=(1024, D), after=tok_in, name=n("in_bwd_x_gate"))
    if below is None:
        dx, g["attn_pre_norm"] = _norm_bwd(sv["x"], w["attn_pre_norm"], [d_h1a, d_h1b], dx2, F32, n("attn_pre_bwd"))
        return dx, g, tok_in, None
    dx, d_fo_below, g["attn_pre_norm"], dg_below = _norm_bwd_chain(
        sv["x"], w["attn_pre_norm"], [d_h1a, d_h1b], dx2, below[0], below[1], n("attn_pre_ffn_post_bwd"))
    return dx, g, tok_in, (d_fo_below, dg_below)


def _local_step(x, target, ws, rel_bias, tok=None, on_grads=None):
    buckets = jnp.asarray(_bucket_tiles())
    bias = _bias_tiles(rel_bias, buckets, "bias_tiles").reshape(N_BIAS_HEADS // 2, 2, 2, BLK, 2 * BLK)
    saved = []
    gain0 = ws[0]["attn_pre_norm"] if tok is None else ws[0]["attn_pre_norm"] + tok
    h1 = _prenorm(x, gain0, "l0_attn_pre")
    for l in range(DEPTH):
        sv = _layer_fwd(x, h1, ws[l], bias, f"l{l}")
        saved.append(sv)
        if l + 1 < DEPTH:
            x, h1 = _postnorm_res(sv["x2"], sv["fo"], ws[l]["ffn_post_norm"], ws[l + 1]["attn_pre_norm"], f"l{l}_ffn_post")
    top = saved[-1]
    dy, loss_tile, d_fo_top, dg_top = _loss_head(top["x2"], top["fo"], ws[-1]["ffn_post_norm"], target, "loss_head")
    grads = [None] * DEPTH
    tok, d_fo = None, (d_fo_top, dg_top)
    for l in reversed(range(DEPTH)):
        on_part = None if on_grads is None else functools.partial(on_grads, l)
        below = (saved[l - 1]["fo"], ws[l - 1]["ffn_post_norm"]) if l > 0 else None
        dy, grads[l], tok, d_fo = _layer_bwd(dy, saved[l], ws[l], bias, f"l{l}", tok, on_part, d_fo, below)
    g_rel = _bias_grad([grads[l]["bias_g"] for l in range(DEPTH)], buckets, "bias_grad")[:, :N_BIAS_HEADS]
    return loss_tile, dy, grads, g_rel


def _coords():
    return lax.axis_index("x"), lax.axis_index("y"), lax.axis_index("c")


def _peer(rel):
    x, y, c = _coords()
    return (1 - x if rel & 4 else x, 1 - y if rel & 2 else y, 1 - c if rel & 1 else c)


def _exchange(srcs, dst_shapes, src_win, dst_win, name, after=None):
    nt = len(srcs)
    extra = [] if after is None else [after]

    def body(*refs):
        src_refs, dst_refs = refs[:nt], refs[nt + len(extra):2 * nt + len(extra)]
        send_sems, recv_sems, local_sems = refs[2 * nt + len(extra):]
        x, y, c = _coords()
        me = 4 * x + 2 * y + c
        locals_ = []
        for t in range(nt):
            cp = pltpu.make_async_copy(src_win(t, src_refs[t], me), dst_win(t, dst_refs[t], me), local_sems.at[t])
            cp.start()
            locals_.append(cp)
        sends = []
        for rel in range(1, NDEV):
            px, py, pc = _peer(rel)
            q = 4 * px + 2 * py + pc
            for t in range(nt):
                cp = pltpu.make_async_remote_copy(
                    src_ref=src_win(t, src_refs[t], q), dst_ref=dst_win(t, dst_refs[t], me),
                    send_sem=send_sems.at[rel - 1, t], recv_sem=recv_sems.at[rel - 1, t],
                    device_id=(px, py, pc), device_id_type=MESH)
                cp.start()
                sends.append(cp)
        for rel in range(1, NDEV):
            px, py, pc = _peer(rel)
            q = 4 * px + 2 * py + pc
            for t in range(nt):
                pltpu.make_async_remote_copy(
                    src_ref=src_win(t, src_refs[t], me), dst_ref=dst_win(t, dst_refs[t], q),
                    send_sem=send_sems.at[rel - 1, t], recv_sem=recv_sems.at[rel - 1, t],
                    device_id=(px, py, pc), device_id_type=MESH).wait_recv()
        for cp in sends:
            cp.wait_send()
        for cp in locals_:
            cp.wait()

    return pl.pallas_call(
        body, in_specs=[ANY] * (nt + len(extra)), out_specs=[ANY] * nt, out_shape=dst_shapes,
        scratch_shapes=[pltpu.SemaphoreType.DMA((NDEV - 1, nt)), pltpu.SemaphoreType.DMA((NDEV - 1, nt)),
                        pltpu.SemaphoreType.DMA((nt,))],
        name=name)(*srcs, *extra)


BIG = (("w_in", 0, 864), ("w_br_a", 1, 128), ("w_br_b", 1, 128), ("w_br_c", 1, 128), ("w_out", 0, 128),
       ("w_up", 1, 1024), ("w_down", 0, 512))


NBIG = len(BIG)
BIG_FULL = {"w_in": (IN_COLS, D), "w_br_a": (256, D), "w_br_b": (512, D), "w_br_c": (256, D), "w_out": (D, D),
            "w_up": (D, 2 * D_FF), "w_down": (D_FF, D)}
SHARD_ROWS = {"w_in": 288, "w_up": 256, "w_down": 256}
LAYER_GROUPS = (("in", (0,)), ("mix", (1, 2, 3, 4)), ("ffn", (5, 6)))

HBM_SPEC = pl.BlockSpec(memory_space=pltpu.HBM)
SEM_SPEC = pl.BlockSpec(memory_space=pltpu.SEMAPHORE)


def _hbm(a):
    return pltpu.with_memory_space_constraint(a, pltpu.HBM)


def _shard_window(t, ref, k):
    nm, ax, ext = BIG[t % NBIG]
    off = pl.multiple_of(k * ext, ext)
    if ax == 0:
        return ref.at[pl.ds(off, ext), :]
    return ref.at[:, pl.ds(off, ext)]


def _whole(t, ref, k):
    return ref


def _slot(t, ref, k):
    return ref.at[k]


def _own_block_spec(t, rows, me_of):
    nm, ax, ext = BIG[t % NBIG]
    r, c = BIG_FULL[nm]
    if ax == 0:
        return pl.BlockSpec((rows, c), lambda i, m: (me_of(m) * (ext // rows) + i, 0))
    return pl.BlockSpec((rows, ext), lambda i, m: (i, me_of(m)))


def _cast_own(t, shards, me_arr, name):
    nm, ax, ext = BIG[t % NBIG]
    layer = t // NBIG
    _, nr, nc = shards.shape
    rows = SHARD_ROWS.get(nm, nr)
    shape = BIG_FULL[nm]

    def body(m_ref, s_ref, o_ref):
        o_ref[...] = s_ref[...].astype(BF16)

    return pl.pallas_call(
        body, grid_spec=pltpu.PrefetchScalarGridSpec(
            num_scalar_prefetch=1, grid=(nr // rows,),
            in_specs=[pl.BlockSpec((None, rows, nc), lambda i, m: (layer, i, 0))],
            out_specs=_own_block_spec(t, rows, lambda m: m[0])),
        out_shape=jax.ShapeDtypeStruct(shape, BF16), compiler_params=_cp("arbitrary"), name=name)(me_arr, shards)


ALL_RELS = tuple(range(1, NDEV))
NEAR_RELS = (1, 2, 4, 6)
FAR_RELS = (2, 4, 6)


def _xchg_start(srcs, lands, groups, src_win, dst_win, after, name, rels=ALL_RELS, tids=None):
    ns = 0 if srcs is None else len(srcs)
    nt, ng = len(lands), len(groups)
    ins = ([] if srcs is None else list(srcs)) + list(lands)

    def body(*refs):
        src_refs, land_refs = refs[:ns], refs[ns:ns + nt]
        sems = refs[ns + nt + 1:ns + nt + 1 + 2 * ng]
        token = refs[-1]
        x, y, c = _coords()
        me = 4 * x + 2 * y + c
        for gi, grp in enumerate(groups):
            for j, t in enumerate(grp):
                tid = t if tids is None else tids[t]
                for ri, rel in enumerate(rels):
                    px, py, pc = _peer(rel)
                    q = 4 * px + 2 * py + pc
                    src = dst_win(tid, land_refs[t], me) if srcs is None else src_win(tid, src_refs[t], q)
                    pltpu.make_async_remote_copy(
                        src_ref=src, dst_ref=dst_win(tid, land_refs[t], me),
                        send_sem=sems[2 * gi].at[ri * len(grp) + j],
                        recv_sem=sems[2 * gi + 1].at[ri * len(grp) + j],
                        device_id=(px, py, pc), device_id_type=MESH).start()
        token[...] = jnp.zeros((8, LANES), F32)

    out_shape = []
    for grp in groups:
        out_shape += [pltpu.SemaphoreType.DMA((len(rels) * len(grp),))] * 2
    out_shape += [pltpu.HBM(a.shape, a.dtype) for a in ins]
    out_shape.append(jax.ShapeDtypeStruct((8, LANES), F32))
    outs = pl.pallas_call(
        body, in_specs=[HBM_SPEC] * len(ins) + [ANY],
        out_specs=[SEM_SPEC] * (2 * ng) + [HBM_SPEC] * len(ins) + [pl.BlockSpec(memory_space=pltpu.VMEM)],
        out_shape=out_shape, input_output_aliases={i: 2 * ng + i for i in range(len(ins))},
        compiler_params=pltpu.CompilerParams(has_side_effects=pltpu.SideEffectType.DATAFLOW_SIDE_EFFECTING),
        name=name)(*[_hbm(a) for a in ins], after)
    sems = [(outs[2 * gi], outs[2 * gi + 1]) for gi in range(ng)]
    thru = list(outs[2 * ng:2 * ng + len(ins)])
    return sems, (None if srcs is None else thru[:ns]), thru[ns:], outs[-1]


def _xchg_wait(sems, srcs, lands, tids, after, src_win, dst_win, name, rels=ALL_RELS):
    ns = 0 if srcs is None else len(srcs)
    n = len(lands)
    send_sem, recv_sem = sems
    ins = ([] if srcs is None else list(srcs)) + list(lands)

    def body(*refs):
        src_refs, land_refs = refs[:ns], refs[ns:ns + n]
        ssem, rsem = refs[ns + n], refs[ns + n + 1]
        x, y, c = _coords()
        me = 4 * x + 2 * y + c
        for j, t in enumerate(tids):
            for ri, rel in enumerate(rels):
                px, py, pc = _peer(rel)
                q = 4 * px + 2 * py + pc
                src = dst_win(t, land_refs[j], me) if srcs is None else src_win(t, src_refs[j], q)
                cp = pltpu.make_async_remote_copy(
                    src_ref=src, dst_ref=dst_win(t, land_refs[j], q),
                    send_sem=ssem.at[ri * n + j], recv_sem=rsem.at[ri * n + j],
                    device_id=(px, py, pc), device_id_type=MESH)
                cp.wait_send()
                cp.wait_recv()

    outs = pl.pallas_call(
        body, in_specs=[HBM_SPEC] * len(ins) + [SEM_SPEC, SEM_SPEC, ANY], out_specs=[HBM_SPEC] * len(ins),
        out_shape=[pltpu.HBM(a.shape, a.dtype) for a in ins],
        input_output_aliases={i: i for i in range(len(ins))},
        compiler_params=pltpu.CompilerParams(has_side_effects=pltpu.SideEffectType.DATAFLOW_SIDE_EFFECTING),
        name=name)(*ins, send_sem, recv_sem, after)
    return (None if srcs is None else list(outs[:ns])), list(outs[ns:])


def _gather_forward(sems_in, lands, groups, tids, after, dst_win, name):
    nt, ng = len(lands), len(groups)

    def body(*refs):
        land_refs = refs[:nt]
        in_sems = refs[nt:nt + 2 * ng]
        out_sems = refs[nt + 2 * ng + 1:nt + 4 * ng + 1]
        token = refs[-1]
        x, y, c = _coords()
        me = 4 * x + 2 * y + c
        sib = (x, y, 1 - c)
        for gi, grp in enumerate(groups):
            n = len(grp)
            for j, pos in enumerate(grp):
                t = tids[pos]
                for ri, rel in enumerate(NEAR_RELS):
                    px, py, pc = _peer(rel)
                    q = 4 * px + 2 * py + pc
                    cp = pltpu.make_async_remote_copy(
                        src_ref=dst_win(t, land_refs[pos], me), dst_ref=dst_win(t, land_refs[pos], q),
                        send_sem=in_sems[2 * gi].at[ri * n + j], recv_sem=in_sems[2 * gi + 1].at[ri * n + j],
                        device_id=(px, py, pc), device_id_type=MESH)
                    cp.wait_send()
                    cp.wait_recv()
            for j, pos in enumerate(grp):
                t = tids[pos]
                for fi, rel in enumerate(FAR_RELS):
                    px, py, pc = _peer(rel)
                    q = 4 * px + 2 * py + pc
                    win = dst_win(t, land_refs[pos], q)
                    pltpu.make_async_remote_copy(
                        src_ref=win, dst_ref=win,
                        send_sem=out_sems[2 * gi].at[fi * n + j], recv_sem=out_sems[2 * gi + 1].at[fi * n + j],
                        device_id=sib, device_id_type=MESH).start()
        token[...] = jnp.zeros((8, LANES), F32)

    out_shape = []
    for grp in groups:
        out_shape += [pltpu.SemaphoreType.DMA((len(FAR_RELS) * len(grp),))] * 2
    out_shape += [pltpu.HBM(a.shape, a.dtype) for a in lands]
    out_shape.append(jax.ShapeDtypeStruct((8, LANES), F32))
    flat_sems = [s for pair in sems_in for s in pair]
    outs = pl.pallas_call(
        body, in_specs=[HBM_SPEC] * nt + [SEM_SPEC] * (2 * ng) + [ANY],
        out_specs=[SEM_SPEC] * (2 * ng) + [HBM_SPEC] * nt + [pl.BlockSpec(memory_space=pltpu.VMEM)],
        out_shape=out_shape, input_output_aliases={i: 2 * ng + i for i in range(nt)},
        compiler_params=pltpu.CompilerParams(has_side_effects=pltpu.SideEffectType.DATAFLOW_SIDE_EFFECTING),
        name=name)(*[_hbm(a) for a in lands], *flat_sems, after)
    sems = [(outs[2 * gi], outs[2 * gi + 1]) for gi in range(ng)]
    return sems, list(outs[2 * ng:2 * ng + nt]), outs[-1]


class _Weights:
    def __init__(self, ready, pending=None):
        self.ready = dict(ready)
        self.pending = dict(pending or {})

    def __getitem__(self, k):
        return self.ready[k]

    def need(self, group, after):
        fn = self.pending.pop(group, None)
        if fn is not None:
            self.ready.update(fn(after))


def _adamw_math(w, g, m, v):
    m2 = ADAM_B1 * m + (1.0 - ADAM_B1) * g
    v2 = ADAM_B2 * v + (1.0 - ADAM_B2) * (g * g)
    m_hat = m2 / (1.0 - ADAM_B1 ** ADAM_STEP)
    v_hat = v2 / (1.0 - ADAM_B2 ** ADAM_STEP)
    delta = -ADAM_LR * (m_hat / (jnp.sqrt(v_hat) + ADAM_EPS) + ADAM_WD * w)
    return delta, m2, v2


def _adamw(t, parts, own, me_arr, w, m, v, layer, prev, rows, name):
    nl, nr, nc = w.shape

    def body(me_ref, p_ref, own_ref, w_ref, m_ref, v_ref, *rest):
        g_ref, d_ref, m2_ref, v2_ref = rest[-4:]
        me = me_ref[0]
        g = None
        for k in range(NDEV):
            term = jnp.where(me == k, own_ref[...], p_ref[k]).astype(F32)
            g = term if g is None else g + term
        delta, m2, v2 = _adamw_math(w_ref[...], g, m_ref[...], v_ref[...])
        g_ref[...] = g
        d_ref[...] = delta
        m2_ref[...] = m2
        v2_ref[...] = v2

    blk = pl.BlockSpec((None, rows, nc), lambda i, mm: (layer, i, 0))
    pblk = pl.BlockSpec((NDEV, rows, nc), lambda i, mm: (0, i, 0))
    extra = [] if prev is None else list(prev)
    return pl.pallas_call(
        body, grid_spec=pltpu.PrefetchScalarGridSpec(
            num_scalar_prefetch=1, grid=(nr // rows,),
            in_specs=[pblk, _own_block_spec(t, rows, lambda mm: mm[0]), blk, blk, blk] + [ANY] * len(extra),
            out_specs=[blk] * 4),
        out_shape=[jax.ShapeDtypeStruct(w.shape, F32)] * 4,
        input_output_aliases={6 + k: k for k in range(len(extra))},
        compiler_params=_cp("arbitrary"), name=name)(me_arr, parts, own, w, m, v, *extra)


def _pack(vecs):
    flat = jnp.concatenate([v.reshape(-1).astype(F32) for v in vecs])
    n = flat.shape[0]
    rows = -(-n // (8 * LANES)) * 8
    return jnp.pad(flat, (0, rows * LANES - n)).reshape(rows, LANES)


ROWPACK = (("rel_bias", 32, 32, (NUM_BUCKETS, N_BIAS_HEADS)), ("sinks", 8, 8, (DEPTH, 8)),
           ("attn_pre_norm", 16, 16, (DEPTH, D)), ("attn_post_norm", 16, 16, (DEPTH, D)),
           ("ffn_pre_norm", 16, 16, (DEPTH, D)), ("ffn_post_norm", 16, 16, (DEPTH, D)),
           ("conv_b", 128, 128, (DEPTH, 2 * D_FF)), ("b_gate", 48, 8, (DEPTH, 3, 128)),
           ("conv_w", 384, 48, (DEPTH, 3, 1024)))
ROWS_OWN = sum(r for _, _, r, _ in ROWPACK)
N_REPL = 7
ROWS_REPL = sum(r for _, _, r, _ in ROWPACK[:N_REPL])
ROWS_SHARD = ROWS_OWN - ROWS_REPL


def _as_rows(a, rows):
    a = a.astype(F32)
    if a.shape[-1] < LANES:
        a = jnp.pad(a.reshape(-1, a.shape[-1]), ((0, 0), (0, LANES - a.shape[-1])))
    a = a.reshape(-1, LANES)
    return jnp.pad(a, ((0, rows - a.shape[0]), (0, 0)))


def _rowpack(arrs, entries=ROWPACK):
    return jnp.concatenate([_as_rows(arrs[nm], ro) for nm, _, ro, _ in entries], axis=0)


def _shard_rows(g):
    bg = jnp.transpose(g["b_gate"].astype(F32).reshape(DEPTH * 3, NDEV, LANES), (1, 0, 2))
    bg = jnp.pad(bg, ((0, 0), (0, 8 - DEPTH * 3), (0, 0)))
    cw = jnp.transpose(g["conv_w"].astype(F32).reshape(DEPTH * 3, NDEV, 8, LANES), (1, 0, 2, 3))
    return jnp.concatenate([bg, cw.reshape(NDEV, DEPTH * 3 * 8, LANES)], axis=1)


def _small_update(parts_repl, parts_shard, w, m, v, name):
    nsm = len(ROWPACK)

    def body(pr_ref, ps_ref, w_ref, m_ref, v_ref, *rest):
        outs = rest[:4 * nsm]
        loss_ref = rest[4 * nsm]
        g_s, d_s, m_s, v_s = rest[4 * nsm + 1:]
        gr, gs = pr_ref[0], ps_ref[0]
        for k in range(1, NDEV):
            gr = gr + pr_ref[k]
            gs = gs + ps_ref[k]
        g_s[0:ROWS_REPL, :] = gr[:ROWS_REPL]
        g_s[ROWS_REPL:ROWS_OWN, :] = gs
        loss_ref[...] = gr[ROWS_REPL:]
        delta, m2, v2 = _adamw_math(w_ref[...], g_s[...], m_ref[...], v_ref[...])
        d_s[...] = delta
        m_s[...] = m2
        v_s[...] = v2
        for kind, src in enumerate((g_s, d_s, m_s, v_s)):
            oo = 0
            for idx, (nm, rf, ro, shp) in enumerate(ROWPACK):
                o_ref = outs[kind * nsm + idx]
                if nm in ("rel_bias", "sinks"):
                    o_ref[...] = src[oo:oo + shp[0], 0:shp[1]]
                elif nm == "b_gate":
                    for l in range(DEPTH):
                        o_ref[l] = src[oo + 3 * l:oo + 3 * l + 3, :]
                elif nm == "conv_w":
                    for l in range(DEPTH):
                        for k in range(8):
                            o_ref[l, :, k * LANES:(k + 1) * LANES] = src[pl.ds(oo + 24 * l + k, 3, stride=8), :]
                else:
                    per = shp[1] // LANES
                    for k in range(per):
                        o_ref[:, k * LANES:(k + 1) * LANES] = src[pl.ds(oo + k, DEPTH, stride=per), :]
                oo += ro

    vm = pl.BlockSpec(memory_space=pltpu.VMEM)
    shapes = [jax.ShapeDtypeStruct(shp, F32) for _ in range(4) for _, _, _, shp in ROWPACK]
    shapes.append(jax.ShapeDtypeStruct((8, LANES), F32))
    outs = pl.pallas_call(
        body, in_specs=[vm] * 5, out_specs=[vm] * (4 * nsm + 1), out_shape=shapes,
        scratch_shapes=[pltpu.VMEM((ROWS_OWN, LANES), F32)] * 4,
        name=name)(parts_repl, parts_shard, w, m, v)
    names = [nm for nm, _, _, _ in ROWPACK]
    return [dict(zip(names, outs[kind * nsm:(kind + 1) * nsm])) for kind in range(4)] + [outs[-1]]


def kernel(x, rel_bias, attn_pre_norm, w_in, b_gate, sinks, w_br_a, w_br_b, w_br_c, w_out, attn_post_norm, ffn_pre_norm, w_up, conv_w, conv_b, w_down, ffn_post_norm, loss_target, m_rel_bias, m_attn_pre_norm, m_w_in, m_b_gate, m_sinks, m_w_br_a, m_w_br_b, m_w_br_c, m_w_out, m_attn_post_norm, m_ffn_pre_norm, m_w_up, m_conv_w, m_conv_b, m_w_down, m_ffn_post_norm, v_rel_bias, v_attn_pre_norm, v_w_in, v_b_gate, v_sinks, v_w_br_a, v_w_br_b, v_w_br_c, v_w_out, v_attn_post_norm, v_ffn_pre_norm, v_w_up, v_conv_w, v_conv_b, v_w_down, v_ffn_post_norm):
    P = dict(rel_bias=rel_bias, attn_pre_norm=attn_pre_norm, w_in=w_in, b_gate=b_gate, sinks=sinks, w_br_a=w_br_a,
             w_br_b=w_br_b, w_br_c=w_br_c, w_out=w_out, attn_post_norm=attn_post_norm, ffn_pre_norm=ffn_pre_norm,
             w_up=w_up, conv_w=conv_w, conv_b=conv_b, w_down=w_down, ffn_post_norm=ffn_post_norm)
    M = dict(rel_bias=m_rel_bias, attn_pre_norm=m_attn_pre_norm, w_in=m_w_in, b_gate=m_b_gate, sinks=m_sinks,
             w_br_a=m_w_br_a, w_br_b=m_w_br_b, w_br_c=m_w_br_c, w_out=m_w_out, attn_post_norm=m_attn_post_norm,
             ffn_pre_norm=m_ffn_pre_norm, w_up=m_w_up, conv_w=m_conv_w, conv_b=m_conv_b, w_down=m_w_down,
             ffn_post_norm=m_ffn_post_norm)
    V = dict(rel_bias=v_rel_bias, attn_pre_norm=v_attn_pre_norm, w_in=v_w_in, b_gate=v_b_gate, sinks=v_sinks,
             w_br_a=v_w_br_a, w_br_b=v_w_br_b, w_br_c=v_w_br_c, w_out=v_w_out, attn_post_norm=v_attn_post_norm,
             ffn_pre_norm=v_ffn_pre_norm, w_up=v_w_up, conv_w=v_conv_w, conv_b=v_conv_b, w_down=v_w_down,
             ffn_post_norm=v_ffn_post_norm)
    tr = lambda a: jnp.swapaxes(a, 1, 2)
    PB = {nm: (tr(P[nm]) if nm == "w_in" else P[nm]) for nm, _, _ in BIG}
    MB = {nm: (tr(M[nm]) if nm == "w_in" else M[nm]) for nm, _, _ in BIG}
    VB = {nm: (tr(V[nm]) if nm == "w_in" else V[nm]) for nm, _, _ in BIG}
    xi, yi, ci = _coords()
    me = 4 * xi + 2 * yi + ci

    me_arr = me.astype(jnp.int32).reshape(1)

    small_w = _pack([b_gate.reshape(-1), conv_w.reshape(-1)])
    (small_w_all,) = _exchange([small_w], [jax.ShapeDtypeStruct((NDEV,) + small_w.shape, F32)],
                               _whole, _slot, "gather_small_weights")
    nbg, ncw = DEPTH * 3 * 128, DEPTH * 3 * 1024
    flat_all = small_w_all.reshape(NDEV, -1)
    b_gate_full = jnp.transpose(flat_all[:, :nbg].reshape(NDEV, DEPTH, 3, 128), (1, 2, 0, 3)).reshape(DEPTH, 3, D)
    conv_w_full = jnp.transpose(flat_all[:, nbg:nbg + ncw].reshape(NDEV, DEPTH, 3, 1024), (1, 2, 0, 3)).reshape(DEPTH, 3, 2 * D_FF)

    groups = [tuple(l * NBIG + t for t in tids) for l in range(DEPTH) for _, tids in LAYER_GROUPS]
    cast = lambda i, m=me_arr: _cast_own(i, PB[BIG[i % NBIG][0]], m, f"gather_own_l{i // NBIG}_{BIG[i % NBIG][0]}")
    first = list(groups[0])
    rest = [i for grp in groups[1:] for i in grp]
    sems0, _, lands0, tok_first = _xchg_start(None, [cast(i) for i in first], [tuple(range(len(first)))], None,
                                              _shard_window, small_w_all, "gather_start_first", rels=NEAR_RELS, tids=first)
    where_rest = {tid: k for k, tid in enumerate(rest)}
    me_rest = me_arr + tok_first[0, 0:1].astype(jnp.int32)
    sems1, _, lands1, g_tok = _xchg_start(None, [cast(i, me_rest) for i in rest],
                                          [tuple(where_rest[i] for i in grp) for grp in groups[1:]], None,
                                          _shard_window, lands0[0], "gather_start_rest", rels=NEAR_RELS, tids=rest)
    g_sems = list(sems0) + list(sems1)
    tok0 = g_tok[0:1, 0:1]
    lands_now = [None] * (DEPTH * NBIG)
    for i, a in zip(first + rest, list(lands0) + list(lands1)):
        lands_now[i] = a
    fwd_sems = {}
    fwd_plan = {0: (0,), 1: (1,), 2: (2,), 3: (3, 4, 5)}

    def gather_waiter(gi, l, gname, tids):
        def wait(after):
            if gi in fwd_plan:
                gis = fwd_plan[gi]
                flat = [i for g2 in gis for i in groups[g2]]
                where = {tid: k for k, tid in enumerate(flat)}
                fs, new_lands, ftok = _gather_forward(
                    [g_sems[g2] for g2 in gis], [lands_now[i] for i in flat],
                    [[where[i] for i in groups[g2]] for g2 in gis], flat, after, _shard_window, f"gather_forward_{gi}")
                for g2, s in zip(gis, fs):
                    fwd_sems[g2] = s
                for i, a in zip(flat, new_lands):
                    lands_now[i] = a
                after = ftok
            ids = [l * NBIG + t for t in tids]
            _, got = _xchg_wait(fwd_sems[gi], None, [lands_now[i] for i in ids], ids, after,
                                None, _shard_window, f"gather_wait_l{l}_{gname}", rels=FAR_RELS)
            out = {}
            for t, arr in zip(tids, got):
                nm = BIG[t][0]
                out[nm] = arr
            return out
        return wait

    pending = [{gname: gather_waiter(l * len(LAYER_GROUPS) + k, l, gname, tids)
                for k, (gname, tids) in enumerate(LAYER_GROUPS)} for l in range(DEPTH)]
    ws = []
    for l in range(DEPTH):
        ws.append(_Weights(dict(
            b_gate=b_gate_full[l], conv_w=conv_w_full[l].reshape(3, 2, D_FF), conv_b=conv_b[l].reshape(2, D_FF),
            sinks=sinks[l].reshape(1, 8),
            attn_pre_norm=attn_pre_norm[l].reshape(1, D), attn_post_norm=attn_post_norm[l].reshape(1, D),
            ffn_pre_norm=ffn_pre_norm[l].reshape(1, D), ffn_post_norm=ffn_post_norm[l].reshape(1, D)), pending[l]))

    rs = {}

    group_tids = dict(LAYER_GROUPS)

    def start_scatter(l, gname, grads_l):
        tids = group_tids[gname]
        blocks, lands_rs = [], []
        for t in tids:
            nm, ax, ext = BIG[t]
            gfull = grads_l[nm].astype(BF16)
            shp = (NDEV, ext, gfull.shape[1]) if ax == 0 else (NDEV, gfull.shape[0], ext)
            blocks.append(gfull)
            lands_rs.append(lax.empty(shp, BF16))
        local = list(range(len(tids)))
        win = lambda j, ref, k: _shard_window(tids[j], ref, k)
        sems, s_thru, l_thru, tok = _xchg_start(blocks, lands_rs, [tuple(local)], win, _slot, me_arr,
                                                f"scatter_start_l{l}_{gname}")
        rs[(l, gname)] = (sems[0], s_thru, l_thru, win, local)
        return tok[0:1, 0:1]

    loss_tile, grad_x, grads, g_rel = _local_step(x[0], loss_target[0], ws, rel_bias, tok0, start_scatter)

    stack = lambda nm: jnp.stack([grads[l][nm] for l in range(DEPTH)], axis=0)
    small_g = {nm: (g_rel if nm == "rel_bias" else stack(nm)) for nm, _, _, _ in ROWPACK}
    small_repl = jnp.concatenate([_rowpack(small_g, ROWPACK[:N_REPL]), loss_tile], axis=0)
    small_shard = _shard_rows(small_g)

    out_g, out_d, out_m, out_v = {}, {}, {}, {}
    prev = {nm: None for nm, _, _ in BIG}
    todo = [(l, gname) for l in reversed(range(DEPTH)) for gname in ("ffn", "mix", "in")]
    after, small_parts = grad_x, None
    for l, gname in todo:
        if (l, gname) == todo[-1]:
            small_parts = _exchange(
                [small_repl, small_shard],
                [jax.ShapeDtypeStruct((NDEV, ROWS_REPL + 8, LANES), F32), jax.ShapeDtypeStruct((NDEV, ROWS_SHARD, LANES), F32)],
                lambda t, ref, q: ref if t == 0 else ref.at[q], _slot, "exchange_small_grads", after=after)
            after = small_parts[0]
        sems, s_thru, l_thru, win, local = rs[(l, gname)]
        owns, parts = _xchg_wait(sems, s_thru, l_thru, local, after, win, _slot, f"scatter_wait_l{l}_{gname}")
        for t, own, prt in zip(group_tids[gname], owns, parts):
            nm = BIG[t][0]
            rows = SHARD_ROWS.get(nm, PB[nm].shape[1])
            prev[nm] = _adamw(t, prt, own, me_arr, PB[nm], MB[nm], VB[nm], l, prev[nm], rows, f"adamw_{nm}_l{l}")
            after = prev[nm][1]
    for nm, _, _ in BIG:
        out_g[nm], out_d[nm], out_m[nm], out_v[nm] = [tr(a) if nm == "w_in" else a for a in prev[nm]]
    sm_g, sm_d, sm_m, sm_v, loss_all = _small_update(small_parts[0], small_parts[1], _rowpack(P), _rowpack(M),
                                                     _rowpack(V), "small_update")
    loss = loss_all[0, 0]
    for dst, src in ((out_g, sm_g), (out_d, sm_d), (out_m, sm_m), (out_v, sm_v)):
        dst.update(src)

    order = ["rel_bias", "attn_pre_norm", "w_in", "b_gate", "sinks", "w_br_a", "w_br_b", "w_br_c", "w_out",
             "attn_post_norm", "ffn_pre_norm", "w_up", "conv_w", "conv_b", "w_down", "ffn_post_norm"]
    return (loss, grad_x[None], *[out_g[k] for k in order], *[out_d[k] for k in order],
            *[out_m[k] for k in order], *[out_v[k] for k in order])
```

```python
import functools
import math

import numpy as np
import jax
import jax.numpy as jnp
from jax import lax
from jax.experimental import pallas as pl
from jax.experimental.pallas import tpu as pltpu

F32 = jnp.float32
BF16 = jnp.bfloat16

S = 2048
D = 1024
DEPTH = 2
NDEV = 8
HD = 64
BLK = 128
NB = S // BLK
A_GROUPS = ((128, 1), (512, 4), (2048, 16))
NUM_BUCKETS = 32
MAX_DISTANCE = 2048
N_BIAS_HEADS = 20
D_FF = 4096
IN_COLS = 6912
QKV_COLS = 3840
QKV_SLABS = QKV_COLS // 128
GATE_COLS = 3072
EPS = 1e-6
SCALE = HD ** -0.5
NEG = -1e30
LANES = 128

ADAM_LR = 0.001
ADAM_B1 = 0.9
ADAM_B2 = 0.999
ADAM_EPS = 1e-08
ADAM_WD = 0.01
ADAM_STEP = 10

VMEM_LIMIT = 56 * 1024 * 1024
MESH = pl.DeviceIdType.MESH
ANY = pl.BlockSpec(memory_space=pl.ANY)
SMEM = pl.BlockSpec(memory_space=pltpu.SMEM)


def _cp(*sem):
    return pltpu.CompilerParams(dimension_semantics=sem if sem else None, vmem_limit_bytes=VMEM_LIMIT)


def _dot(a, b, ca, cb):
    return lax.dot_general(a, b, (((ca,), (cb,)), ((), ())), preferred_element_type=F32)


def _mm(a, b, *, grid, a_spec, b_spec, out_shape, out_spec, ca, cb, acc_shape, name,
        a_slab=False, b_slab=False, out_slab=False, alias_out=None, after=None):
    nk = grid[2]

    def body(*refs):
        a_ref, b_ref = refs[0], refs[1]
        o_ref, acc_ref = refs[-2], refs[-1]
        k = pl.program_id(2)

        def load(ref, slab):
            if slab:
                return jnp.concatenate([ref[s] for s in range(ref.shape[0])], axis=1).astype(BF16)
            return ref[...].astype(BF16)

        def write(val):
            if out_slab:
                for s in range(o_ref.shape[0]):
                    o_ref[s] = val[:, s * LANES:(s + 1) * LANES].astype(o_ref.dtype)
            else:
                o_ref[...] = val.astype(o_ref.dtype)

        d = _dot(load(a_ref, a_slab), load(b_ref, b_slab), ca, cb)
        if nk == 1:
            write(d)
        elif direct:
            @pl.when(k == 0)
            def _():
                o_ref[...] = d

            @pl.when(k > 0)
            def _():
                o_ref[...] += d
        else:
            @pl.when(k == 0)
            def _():
                acc_ref[...] = d

            if nk > 2:
                @pl.when((k > 0) & (k < nk - 1))
                def _():
                    acc_ref[...] += d

            @pl.when(k == nk - 1)
            def _():
                write(acc_ref[...] + d)

    direct = (not out_slab) and out_shape.dtype == F32
    if nk == 1 or direct:
        acc_shape = (8, LANES)
    in_specs = [a_spec, b_spec]
    args = [a, b]
    aliases = {}
    if alias_out is not None:
        in_specs.append(ANY)
        args.append(alias_out)
        aliases = {2: 0}
    if after is not None:
        in_specs.append(ANY)
        args.append(after)
    return pl.pallas_call(
        body, grid=grid, in_specs=in_specs, out_specs=out_spec, out_shape=out_shape,
        scratch_shapes=[pltpu.VMEM(acc_shape, F32)], input_output_aliases=aliases,
        compiler_params=_cp("parallel", "parallel", "arbitrary"), name=name)(*args)


def _mm_nn(a, b, out_dtype, tm, tn, tk, name):
    m, kk = a.shape
    n = b.shape[1]
    return _mm(a, b, grid=(m // tm, n // tn, kk // tk),
               a_spec=pl.BlockSpec((tm, tk), lambda i, j, k: (i, k)),
               b_spec=pl.BlockSpec((tk, tn), lambda i, j, k: (k, j)),
               out_shape=jax.ShapeDtypeStruct((m, n), out_dtype),
               out_spec=pl.BlockSpec((tm, tn), lambda i, j, k: (i, j)),
               ca=1, cb=0, acc_shape=(tm, tn), name=name)


def _mm_nt(a, b, out_dtype, tm, tn, tk, name):
    m, kk = a.shape
    n = b.shape[0]
    return _mm(a, b, grid=(m // tm, n // tn, kk // tk),
               a_spec=pl.BlockSpec((tm, tk), lambda i, j, k: (i, k)),
               b_spec=pl.BlockSpec((tn, tk), lambda i, j, k: (j, k)),
               out_shape=jax.ShapeDtypeStruct((m, n), out_dtype),
               out_spec=pl.BlockSpec((tm, tn), lambda i, j, k: (i, j)),
               ca=1, cb=1, acc_shape=(tm, tn), name=name)


def _mm_tn(a, b, out_dtype, tm, tn, tk, name):
    kk, m = a.shape
    n = b.shape[1]
    return _mm(a, b, grid=(m // tm, n // tn, kk // tk),
               a_spec=pl.BlockSpec((tk, tm), lambda i, j, k: (k, i)),
               b_spec=pl.BlockSpec((tk, tn), lambda i, j, k: (k, j)),
               out_shape=jax.ShapeDtypeStruct((m, n), out_dtype),
               out_spec=pl.BlockSpec((tm, tn), lambda i, j, k: (i, j)),
               ca=0, cb=0, acc_shape=(tm, tn), name=name)


ROW_TILE = 512
MERGE_TILE = 256


def _rms(x, g):
    r = lax.rsqrt(jnp.mean(x * x, axis=-1, keepdims=True) + EPS)
    return x * r * g


def _prenorm(x, g, name):
    def body(x_ref, g_ref, o_ref):
        o_ref[...] = _rms(x_ref[...], g_ref[...]).astype(BF16)

    return pl.pallas_call(
        body, grid=(S // ROW_TILE,),
        in_specs=[pl.BlockSpec((ROW_TILE, D), lambda i: (i, 0)), pl.BlockSpec((1, D), lambda i: (0, 0))],
        out_specs=pl.BlockSpec((ROW_TILE, D), lambda i: (i, 0)),
        out_shape=jax.ShapeDtypeStruct((S, D), BF16), compiler_params=_cp("parallel"), name=name)(x, g)


def _postnorm_res(x, f, g_post, g_next, name):
    def body(x_ref, f_ref, gp_ref, gn_ref, xo_ref, ho_ref):
        xn = x_ref[...] + _rms(f_ref[...], gp_ref[...])
        xo_ref[...] = xn
        ho_ref[...] = _rms(xn, gn_ref[...]).astype(BF16)

    row = pl.BlockSpec((ROW_TILE, D), lambda i: (i, 0))
    vec = pl.BlockSpec((1, D), lambda i: (0, 0))
    return pl.pallas_call(
        body, grid=(S // ROW_TILE,), in_specs=[row, row, vec, vec], out_specs=[row, row],
        out_shape=[jax.ShapeDtypeStruct((S, D), F32), jax.ShapeDtypeStruct((S, D), BF16)],
        compiler_params=_cp("parallel"), name=name)(x, f, g_post, g_next)


def _norm_bwd(f, g, dys, res, out_dtype, name):
    ndy = len(dys)
    has_res = res is not None

    def body(*refs):
        f_ref, g_ref = refs[0], refs[1]
        dy_refs = refs[2:2 + ndy]
        res_ref = refs[2 + ndy] if has_res else None
        o_ref, dg_ref = refs[-2], refs[-1]
        fv = f_ref[...]
        dy = dy_refs[0][...].astype(F32)
        for r in dy_refs[1:]:
            dy = dy + r[...].astype(F32)
        r = lax.rsqrt(jnp.mean(fv * fv, axis=-1, keepdims=True) + EPS)
        n = fv * r
        dn = dy * g_ref[...]
        df = r * (dn - n * jnp.mean(dn * n, axis=-1, keepdims=True))
        if has_res:
            df = df + res_ref[...]
        o_ref[...] = df.astype(out_dtype)

        @pl.when(pl.program_id(0) == 0)
        def _():
            dg_ref[...] = jnp.zeros((1, D), F32)

        dg_ref[...] += jnp.sum(dy * n, axis=0, keepdims=True)

    row = pl.BlockSpec((ROW_TILE, D), lambda i: (i, 0))
    vec = pl.BlockSpec((1, D), lambda i: (0, 0))
    in_specs = [row, vec] + [row] * ndy + ([row] if has_res else [])
    args = [f, g] + list(dys) + ([res] if has_res else [])
    return pl.pallas_call(
        body, grid=(S // ROW_TILE,), in_specs=in_specs, out_specs=[row, vec],
        out_shape=[jax.ShapeDtypeStruct((S, D), out_dtype), jax.ShapeDtypeStruct((1, D), F32)],
        compiler_params=_cp("arbitrary"), name=name)(*args)


def _rms_bwd_rows(fv, g, dy):
    r = lax.rsqrt(jnp.mean(fv * fv, axis=-1, keepdims=True) + EPS)
    n = fv * r
    dn = dy * g
    return r * (dn - n * jnp.mean(dn * n, axis=-1, keepdims=True)), dy * n


def _norm_bwd_chain(f1, g1, dys, res, f2, g2, name):
    ndy = len(dys)

    def body(*refs):
        f1_ref, g1_ref = refs[0], refs[1]
        dy_refs = refs[2:2 + ndy]
        res_ref, f2_ref, g2_ref = refs[2 + ndy:5 + ndy]
        o1_ref, o2_ref, dg1_ref, dg2_ref = refs[-4:]
        dy = dy_refs[0][...].astype(F32)
        for r in dy_refs[1:]:
            dy = dy + r[...].astype(F32)
        df1, c1 = _rms_bwd_rows(f1_ref[...], g1_ref[...], dy)
        out1 = df1 + res_ref[...]
        o1_ref[...] = out1
        df2, c2 = _rms_bwd_rows(f2_ref[...], g2_ref[...], out1)
        o2_ref[...] = df2.astype(BF16)

        @pl.when(pl.program_id(0) == 0)
        def _():
            dg1_ref[...] = jnp.zeros((1, D), F32)
            dg2_ref[...] = jnp.zeros((1, D), F32)

        dg1_ref[...] += jnp.sum(c1, axis=0, keepdims=True)
        dg2_ref[...] += jnp.sum(c2, axis=0, keepdims=True)

    row = pl.BlockSpec((ROW_TILE, D), lambda i: (i, 0))
    vec = pl.BlockSpec((1, D), lambda i: (0, 0))
    return pl.pallas_call(
        body, grid=(S // ROW_TILE,), in_specs=[row, vec] + [row] * ndy + [row, row, vec],
        out_specs=[row, row, vec, vec],
        out_shape=[jax.ShapeDtypeStruct((S, D), F32), jax.ShapeDtypeStruct((S, D), BF16),
                   jax.ShapeDtypeStruct((1, D), F32), jax.ShapeDtypeStruct((1, D), F32)],
        compiler_params=_cp("arbitrary"), name=name)(f1, g1, *dys, res, f2, g2)


def _loss_head(x, f, g, target, name):
    def body(x_ref, f_ref, g_ref, t_ref, dy_ref, l_ref, df_ref, dg_ref):
        fv = f_ref[...]
        e = x_ref[...] + _rms(fv, g_ref[...]) - t_ref[...]
        dy = e * (1.0 / D)
        dy_ref[...] = dy
        df, c = _rms_bwd_rows(fv, g_ref[...], dy)
        df_ref[...] = df.astype(BF16)

        @pl.when(pl.program_id(0) == 0)
        def _():
            l_ref[...] = jnp.zeros((8, LANES), F32)
            dg_ref[...] = jnp.zeros((1, D), F32)

        l_ref[...] += jnp.sum(e * e) * (0.5 / D)
        dg_ref[...] += jnp.sum(c, axis=0, keepdims=True)

    row = pl.BlockSpec((ROW_TILE, D), lambda i: (i, 0))
    vec = pl.BlockSpec((1, D), lambda i: (0, 0))
    return pl.pallas_call(
        body, grid=(S // ROW_TILE,), in_specs=[row, row, vec, row],
        out_specs=[row, pl.BlockSpec((8, LANES), lambda i: (0, 0)), row, vec],
        out_shape=[jax.ShapeDtypeStruct((S, D), F32), jax.ShapeDtypeStruct((8, LANES), F32),
                   jax.ShapeDtypeStruct((S, D), BF16), jax.ShapeDtypeStruct((1, D), F32)],
        compiler_params=_cp("arbitrary"), name=name)(x, f, g, target)


def _bucket_tiles():
    a = np.arange(BLK)[:, None]
    b = np.arange(2 * BLK)[None, :]
    dist = a + BLK - b
    out = np.zeros((4, 2, BLK, 2 * BLK), np.int32)
    cfg = [(w // d, d) for w, d in A_GROUPS] + [(BLK - 1, 1)]
    for gi, (max_dist, d) in enumerate(cfg):
        band = (dist >= 0) & (dist <= max_dist)
        tok = np.maximum(dist, 0) * d
        nf = np.maximum(tok, 1).astype(np.float32)
        max_exact = NUM_BUCKETS // 2
        large = max_exact + (np.log(nf / np.float32(max_exact)) / np.float32(math.log(MAX_DISTANCE / max_exact))
                             * np.float32(NUM_BUCKETS - max_exact)).astype(np.int32)
        large = np.minimum(large, NUM_BUCKETS - 1)
        bkt = np.where(tok < max_exact, tok, large).astype(np.int32)
        full = np.where(band, bkt, -1)
        out[gi, 1] = full
        out[gi, 0] = np.where(b >= BLK, full, -1)
    return out


def _bias_tiles(rel_bias, buckets, name):
    def body(tab_ref, bkt_ref, o_ref):
        h = pl.program_id(0)
        bkt = bkt_ref[...]
        acc = jnp.zeros(bkt.shape, F32)
        for bb in range(NUM_BUCKETS):
            acc = jnp.where(bkt == bb, tab_ref[bb, h], acc)
        o_ref[...] = jnp.where(bkt < 0, NEG, acc)

    return pl.pallas_call(
        body, grid=(N_BIAS_HEADS,),
        in_specs=[SMEM, pl.BlockSpec((None, 2, BLK, 2 * BLK), lambda h: (jnp.minimum(h // 4, 3), 0, 0, 0))],
        out_specs=pl.BlockSpec((None, 2, BLK, 2 * BLK), lambda h: (h, 0, 0, 0)),
        out_shape=jax.ShapeDtypeStruct((N_BIAS_HEADS, 2, BLK, 2 * BLK), F32),
        compiler_params=_cp("arbitrary"), name=name)(rel_bias, buckets)


def _bias_grad(gs, buckets, name):
    ng = len(gs)

    def body(*refs):
        g_refs = refs[:ng]
        bkt_ref, o_ref = refs[ng], refs[ng + 1]
        h = pl.program_id(0)
        g = g_refs[0][...]
        for r in g_refs[1:]:
            g = g + r[...]
        bkt = bkt_ref[...]
        row = lax.broadcasted_iota(jnp.int32, (NUM_BUCKETS, LANES), 0)
        lane = lax.broadcasted_iota(jnp.int32, (NUM_BUCKETS, LANES), 1)

        @pl.when(h == 0)
        def _():
            o_ref[...] = jnp.zeros((NUM_BUCKETS, LANES), F32)

        acc = o_ref[...]
        for bb in range(NUM_BUCKETS):
            s = jnp.sum(jnp.where(bkt == bb, g, 0.0))
            acc = jnp.where((row == bb) & (lane == h), s, acc)
        o_ref[...] = acc

    g_spec = pl.BlockSpec((None, BLK, 2 * BLK), lambda h: (h, 0, 0))
    return pl.pallas_call(
        body, grid=(N_BIAS_HEADS,),
        in_specs=[g_spec] * ng + [pl.BlockSpec((None, None, BLK, 2 * BLK), lambda h: (jnp.minimum(h // 4, 3), 1, 0, 0))],
        out_specs=pl.BlockSpec((NUM_BUCKETS, LANES), lambda h: (0, 0)),
        out_shape=jax.ShapeDtypeStruct((NUM_BUCKETS, LANES), F32),
        compiler_params=_cp("arbitrary"), name=name)(*gs, buckets)


def _to_class_major(src_ref, dst_refs, d, fn=None):
    ln = S // d
    for r in range(d):
        v = src_ref[pl.ds(r, ln, stride=d), :] if d > 1 else src_ref[...]
        outs = fn(v) if fn is not None else (v,) * len(dst_refs)
        for dst, o in zip(dst_refs, outs):
            dst[pl.ds(r * ln, ln), :] = o.astype(dst.dtype)


def _head_masks(rows):
    lane = lax.broadcasted_iota(jnp.int32, (rows, LANES), 1)
    return lane < HD, lane >= HD


def _split_heads(v):
    m0, m1 = _head_masks(v.shape[0])
    return jnp.where(m0, v, 0.0), jnp.where(m1, v, 0.0)


def _dup_head(v, hi):
    m0, _ = _head_masks(v.shape[0])
    r = pltpu.roll(v, HD, 1)
    return jnp.where(m0, jnp.where(hi, r, v), jnp.where(hi, v, r))


def _block_rows(b, d):
    nbc = NB // d
    i = b % nbc
    r = b // nbc
    has_prev = (i > 0).astype(jnp.int32)
    prev = pl.multiple_of(jnp.maximum(b - 1, 0) * BLK, BLK)
    nat = i * (BLK * d) + r
    return has_prev, prev, nat


def _lane_halves(v0, v1):
    lane = lax.broadcasted_iota(jnp.int32, (v0.shape[0], LANES), 1)
    return jnp.where(lane < HD, v0, v1)


def _band_fwd(proj, bias, sinks, *, d, q0, k0, v0, npairs, bias0, shared_kv, name):
    def body(sink_ref, q_ref, k_ref, v_ref, b_ref, num_ref, st_ref, qz0, qz1, ks, vs):
        p = pl.program_id(0)
        kv = (lambda v: (_dup_head(v, p >= 2),)) if shared_kv else None
        _to_class_major(q_ref, (qz0, qz1), d, lambda v: _split_heads(v * SCALE))
        _to_class_major(k_ref, (ks,), d, kv)
        _to_class_major(v_ref, (vs,), d, kv)
        lane = lax.broadcasted_iota(jnp.int32, (BLK, LANES), 1)

        def blk(b, carry):
            has_prev, prev, nat = _block_rows(b, d)
            cur = pl.multiple_of(b * BLK, BLK)
            k2 = jnp.concatenate([ks[pl.ds(prev, BLK), :], ks[pl.ds(cur, BLK), :]], axis=0)
            v2 = jnp.concatenate([vs[pl.ds(prev, BLK), :], vs[pl.ds(cur, BLK), :]], axis=0)
            nums, ms, ls = [], [], []
            for hh, qz in enumerate((qz0, qz1)):
                z = _dot(qz[pl.ds(cur, BLK), :], k2, 1, 1) + b_ref[hh, has_prev]
                m = jnp.max(z, axis=1, keepdims=True)
                e = jnp.exp(z - m)
                l = jnp.sum(e, axis=1, keepdims=True)
                num = _dot(e.astype(BF16), v2, 1, 0)
                if shared_kv:
                    sink = sink_ref[0, 2 * p + hh]
                    mx = jnp.maximum(m, sink)
                    c = jnp.exp(m - mx)
                    zden = l * c + jnp.exp(sink - mx)
                    num = num * (c / zden)
                    m = mx + jnp.log(zden)
                ls.append(l)
                ms.append(m)
                nums.append(num)
            num_t = jnp.where(lane < HD, nums[0], nums[1])
            if shared_kv:
                st_t = jnp.where(lane < HD, ms[0], ms[1])
            else:
                st_t = jnp.where(lane < 32, ms[0], jnp.where(lane < 64, ls[0], jnp.where(lane < 96, ms[1], ls[1])))
            if d > 1:
                num_ref[pl.ds(nat, BLK, stride=d), :] = num_t
                st_ref[pl.ds(nat, BLK, stride=d), :] = st_t
            else:
                num_ref[pl.ds(cur, BLK), :] = num_t
                st_ref[pl.ds(cur, BLK), :] = st_t
            return carry

        lax.fori_loop(0, NB, blk, 0, unroll=True)

    slab = lambda off, per_pair: pl.BlockSpec((None, S, LANES), (lambda p: (off + p, 0, 0)) if per_pair else (lambda p: (off, 0, 0)))
    out = pl.BlockSpec((None, S, LANES), lambda p: (p, 0, 0))
    return pl.pallas_call(
        body, grid=(npairs,),
        in_specs=[SMEM, slab(q0, True), slab(k0, not shared_kv), slab(v0, not shared_kv),
                  pl.BlockSpec((None, 2, 2, BLK, 2 * BLK), lambda p: (bias0 + p, 0, 0, 0, 0))],
        out_specs=[out, out],
        out_shape=[jax.ShapeDtypeStruct((npairs, S, LANES), F32)] * 2,
        scratch_shapes=[pltpu.VMEM((S, LANES), BF16)] * 4,
        compiler_params=_cp("arbitrary"), name=name)(sinks, proj, proj, proj, bias)


def _combine_a(nums, stats, name):
    rt = 512

    def body(n0, n1, n2, s0, s1, s2, o_ref, l_ref):
        n_refs, s_refs = (n0, n1, n2), (s0, s1, s2)
        outs, lses = [], []
        for hh in range(2):
            ms = [s[:, 64 * hh:64 * hh + 1] for s in s_refs]
            ls = [s[:, 64 * hh + 32:64 * hh + 33] for s in s_refs]
            mx = jnp.maximum(jnp.maximum(ms[0], ms[1]), ms[2])
            cs = [jnp.exp(m - mx) for m in ms]
            z = cs[0] * ls[0] + cs[1] * ls[1] + cs[2] * ls[2]
            acc = cs[0] * n_refs[0][:, hh * HD:(hh + 1) * HD]
            acc = acc + cs[1] * n_refs[1][:, hh * HD:(hh + 1) * HD]
            acc = acc + cs[2] * n_refs[2][:, hh * HD:(hh + 1) * HD]
            outs.append(acc / z)
            lses.append(mx + jnp.log(z))
        o_ref[...] = jnp.concatenate(outs, axis=1)
        l_ref[...] = _lane_halves(lses[0], lses[1])

    spec = pl.BlockSpec((None, rt, LANES), lambda p, i: (p, i, 0))
    return pl.pallas_call(
        body, grid=(2, S // rt), in_specs=[spec] * 6, out_specs=[spec, spec],
        out_shape=[jax.ShapeDtypeStruct((2, S, LANES), F32)] * 2,
        compiler_params=_cp("parallel", "parallel"), name=name)(*nums, *stats)


def _band_bwd(proj, bias, o, do, lse, sinks, dqkv, *, d, q0, k0, v0, npairs, bias0, shared_kv, name):
    def body(sink_ref, q_ref, k_ref, v_ref, b_ref, o_ref, do_ref, lse_ref, dqkv_in, dqkv_ref, g_ref, ds_ref,
             qz0, qz1, ks, vs, doz0, doz1, ls0, ls1, dls0, dls1, stage, dq_nat, dk_cm, dv_cm, kv_nat, dk_acc, dv_acc,
             obuf, osem):
        p = pl.program_id(0)
        dq_ref, dk_ref, dv_ref = obuf.at[0], obuf.at[1], obuf.at[2]
        m0, m1 = _head_masks(S)
        kk = lax.broadcasted_iota(jnp.int32, (2 * LANES, LANES), 0) % LANES
        ll = lax.broadcasted_iota(jnp.int32, (2 * LANES, LANES), 1)
        hi, lo = _split2(do_ref[...] * o_ref[...])
        dl = _dot(jnp.concatenate([hi, lo], axis=1), ((kk < HD) == (ll < HD)).astype(BF16), 1, 0)
        if shared_kv:
            row8 = lax.broadcasted_iota(jnp.int32, (8, LANES), 0)
            lane8 = lax.broadcasted_iota(jnp.int32, (8, LANES), 1)
            sinkv = jnp.where(m0, sink_ref[0, 2 * p], sink_ref[0, 2 * p + 1])
            contrib = jnp.exp(sinkv - lse_ref[...]) * dl
            t = jnp.zeros((8, LANES), F32)
            for hh, mh in enumerate((m0, m1)):
                dsink = -jnp.sum(jnp.where(mh, contrib, 0.0)) * (1.0 / HD)
                t = jnp.where((row8 == 0) & (lane8 == hh), dsink, t)
            ds_ref[...] = t
        else:
            ds_ref[...] = jnp.zeros((8, LANES), F32)
        kv = (lambda v: (_dup_head(v, p >= 2),)) if shared_kv else None
        _to_class_major(q_ref, (qz0, qz1), d, lambda v: _split_heads(v * SCALE))
        _to_class_major(k_ref, (ks,), d, kv)
        _to_class_major(v_ref, (vs,), d, kv)
        _to_class_major(do_ref, (doz0, doz1), d, _split_heads)
        def spread(v):
            a0, a1 = _head_masks(v.shape[0])
            r = pltpu.roll(v, HD, 1)
            return jnp.where(a0, v, r), jnp.where(a1, v, r)

        _to_class_major(lse_ref, (ls0, ls1), d, spread)
        stage[...] = dl
        _to_class_major(stage, (dls0, dls1), d, spread)

        dk_cm[...] = jnp.zeros((S, LANES), F32)
        dv_cm[...] = jnp.zeros((S, LANES), F32)
        g_ref[...] = jnp.zeros((2, BLK, 2 * BLK), F32)
        lane = lax.broadcasted_iota(jnp.int32, (BLK, LANES), 1)

        def blk(b, carry):
            has_prev, prev, nat = _block_rows(b, d)
            cur = pl.multiple_of(b * BLK, BLK)
            k2 = jnp.concatenate([ks[pl.ds(prev, BLK), :], ks[pl.ds(cur, BLK), :]], axis=0)
            v2 = jnp.concatenate([vs[pl.ds(prev, BLK), :], vs[pl.ds(cur, BLK), :]], axis=0)
            dqs, dks, dvs = [], [], []
            for hh, (qz, doz, lsr, dlr) in enumerate(((qz0, doz0, ls0, dls0), (qz1, doz1, ls1, dls1))):
                qb = qz[pl.ds(cur, BLK), :]
                dob = doz[pl.ds(cur, BLK), :]
                lb = lsr[pl.ds(cur, BLK), :]
                dlb = dlr[pl.ds(cur, BLK), :]
                z = _dot(qb, k2, 1, 1) + b_ref[hh, has_prev]
                pr = jnp.exp(z - jnp.concatenate([lb, lb], axis=1))
                dp = _dot(dob, v2, 1, 1)
                dz = pr * (dp - jnp.concatenate([dlb, dlb], axis=1))
                g_ref[hh] += dz
                dzb = dz.astype(BF16)
                dqs.append(_dot(dzb, k2, 1, 0))
                dks.append(_dot(dzb, qb, 0, 0))
                dvs.append(_dot(pr.astype(BF16), dob, 0, 0))
            dq_t = jnp.where(lane < HD, dqs[0], dqs[1]) * SCALE
            dk_t = dks[0] + dks[1]
            dv_t = dvs[0] + dvs[1]
            dk_cm[pl.ds(prev, BLK), :] += dk_t[:BLK]
            dk_cm[pl.ds(cur, BLK), :] += dk_t[BLK:]
            dv_cm[pl.ds(prev, BLK), :] += dv_t[:BLK]
            dv_cm[pl.ds(cur, BLK), :] += dv_t[BLK:]
            if d > 1:
                dq_nat[pl.ds(nat, BLK, stride=d), :] = dq_t
            else:
                dq_nat[pl.ds(cur, BLK), :] = dq_t
            return carry

        lax.fori_loop(0, NB, blk, 0, unroll=True)
        dq_ref[...] = dq_nat[...].astype(BF16)

        def from_class_major(src, dst_ref):
            if d == 1:
                dst_ref[...] = src[...].astype(BF16)
            else:
                ln = S // d
                for r in range(d):
                    kv_nat[pl.ds(r, ln, stride=d), :] = src[pl.ds(r * ln, ln), :]
                dst_ref[...] = kv_nat[...].astype(BF16)

        def put(i, slab):
            return pltpu.make_async_copy(obuf.at[i], dqkv_ref.at[slab], osem.at[i])

        put(0, q0 + p).start()
        if not shared_kv:
            from_class_major(dk_cm, dk_ref)
            from_class_major(dv_cm, dv_ref)
            put(1, k0 + p).start()
            put(2, v0 + p).start()
            put(1, k0 + p).wait()
            put(2, v0 + p).wait()
        else:
            @pl.when(p == 0)
            def _():
                dk_acc[...] = jnp.zeros((S, LANES), F32)
                dv_acc[...] = jnp.zeros((S, LANES), F32)

            mine = m1 == (p >= 2)
            for cm, acc in ((dk_cm, dk_acc), (dv_cm, dv_acc)):
                val = cm[...]
                acc[...] += jnp.where(mine, val + pltpu.roll(val, HD, 1), 0.0)

            @pl.when(p == npairs - 1)
            def _():
                from_class_major(dk_acc, dk_ref)
                from_class_major(dv_acc, dv_ref)
                put(1, k0).start()
                put(2, v0).start()
                put(1, k0).wait()
                put(2, v0).wait()

        put(0, q0 + p).wait()

    slab = lambda off, per_pair: pl.BlockSpec((None, S, LANES), (lambda p: (off + p, 0, 0)) if per_pair else (lambda p: (off, 0, 0)))
    pair = pl.BlockSpec((None, S, LANES), lambda p: (p, 0, 0))
    return pl.pallas_call(
        body, grid=(npairs,),
        in_specs=[SMEM, slab(q0, True), slab(k0, not shared_kv), slab(v0, not shared_kv),
                  pl.BlockSpec((None, 2, 2, BLK, 2 * BLK), lambda p: (bias0 + p, 0, 0, 0, 0)),
                  pair, pair, pair, ANY],
        out_specs=[ANY,
                   pl.BlockSpec((None, 2, BLK, 2 * BLK), lambda p: (p, 0, 0, 0)),
                   pl.BlockSpec((None, 8, LANES), lambda p: (p, 0, 0))],
        out_shape=[jax.ShapeDtypeStruct(dqkv.shape, BF16),
                   jax.ShapeDtypeStruct((npairs, 2, BLK, 2 * BLK), F32),
                   jax.ShapeDtypeStruct((npairs, 8, LANES), F32)],
        scratch_shapes=[pltpu.VMEM((S, LANES), BF16)] * 6 + [pltpu.VMEM((S, LANES), F32)] * 11
        + [pltpu.VMEM((3, S, LANES), BF16), pltpu.SemaphoreType.DMA((3,))],
        input_output_aliases={8: 0},
        compiler_params=_cp("arbitrary"), name=name)(sinks, proj, proj, proj, bias, o, do, lse, dqkv)


KC = 512
NSUB = KC // BLK
QB = 512
QPG = KC // QB


def _split2(x):
    hi = x.astype(BF16)
    lo = (x - hi.astype(F32)).astype(BF16)
    return hi, lo


def _tri_ones(cmp):
    jj = lax.broadcasted_iota(jnp.int32, (2 * BLK, BLK), 0) % BLK
    ss = lax.broadcasted_iota(jnp.int32, (2 * BLK, BLK), 1)
    return jnp.concatenate([cmp(jj, ss).astype(BF16), jnp.ones((2 * BLK, BLK), BF16)], axis=1)


def _sub_sums(x, tri1):
    n = x.shape[0]
    st = jnp.concatenate([x[:, s * BLK:(s + 1) * BLK] for s in range(NSUB)], axis=0)
    hi, lo = _split2(st)
    r = _dot(jnp.concatenate([hi, lo], axis=1), tri1, 1, 0)
    return ([r[s * n:(s + 1) * n, :BLK] for s in range(NSUB)], [r[s * n:(s + 1) * n, BLK:] for s in range(NSUB)])


def _log_sig_pair(z):
    lb = jnp.minimum(z, 0.0) - jnp.log1p(jnp.exp(-jnp.abs(z)))
    return lb, lb - z


QGROUPS = NB // NSUB


def _stick_fwd(proj, *, q0, k0, v0, name):
    def body(q_ref, k_ref, v_ref, o_ref, t_ref, qs, ks, vs):
        qs[...] = (q_ref[...] * SCALE).astype(BF16)
        ks[...] = k_ref[...].astype(BF16)
        vs[...] = v_ref[...].astype(BF16)
        tri1 = _tri_ones(lambda j, s: j > s)
        col = lax.broadcasted_iota(jnp.int32, (QB, KC), 1)
        rowi = lax.broadcasted_iota(jnp.int32, (QB, KC), 0)

        for qg in range(QGROUPS):
            def qblock(ii, carry0, qg=qg):
                t0 = pl.multiple_of((qg * QPG + ii) * QB, QB)
                qb = qs[pl.ds(t0, QB), :]
                accs = [jnp.zeros((QB, HD), F32)] * 2
                runs = [jnp.zeros((QB, BLK), F32)] * 2
                for c in reversed(range(qg + 1)):
                    s0 = c * KC
                    diag = c == qg
                    before = (s0 + col) < (t0 + rowi) if diag else None
                    for hh in range(2):
                        kh = ks[s0:s0 + KC, hh * HD:(hh + 1) * HD]
                        vh = vs[s0:s0 + KC, hh * HD:(hh + 1) * HD]
                        lb, lk = _log_sig_pair(_dot(qb[:, hh * HD:(hh + 1) * HD], kh, 1, 1))
                        if diag:
                            lk = jnp.where(before, lk, 0.0)
                        suf, tot = _sub_sums(lk, tri1)
                        ws, run = [], runs[hh]
                        for s in reversed(range(NSUB)):
                            ws.append(jnp.exp(lb[:, s * BLK:(s + 1) * BLK] + suf[s] + run))
                            run = run + tot[s]
                        w = jnp.concatenate(ws[::-1], axis=1)
                        if diag:
                            w = jnp.where(before, w, 0.0)
                        accs[hh] = accs[hh] + _dot(w.astype(BF16), vh, 1, 0)
                        runs[hh] = run
                o_ref[pl.ds(t0, QB), :] = jnp.concatenate(accs, axis=1)
                t_ref[pl.ds(t0, QB), :] = _lane_halves(runs[0], runs[1])
                return carry0

            lax.fori_loop(0, QPG, qblock, 0)

    slab = lambda off: pl.BlockSpec((None, S, LANES), lambda p: (off + p, 0, 0))
    out = pl.BlockSpec((None, S, LANES), lambda p: (p, 0, 0))
    return pl.pallas_call(
        body, grid=(2,), in_specs=[slab(q0), slab(k0), slab(v0)], out_specs=[out, out],
        out_shape=[jax.ShapeDtypeStruct((2, S, LANES), F32)] * 2,
        scratch_shapes=[pltpu.VMEM((S, LANES), BF16)] * 3,
        compiler_params=_cp("arbitrary"), name=name)(proj, proj, proj)


def _stick_bwd(proj, do, tot, dqkv, *, q0, k0, v0, name):
    def body(q_ref, k_ref, v_ref, do_ref, t_ref, dqkv_in, dqkv_ref, qs, ks, vs, dos, dk_acc, dv_acc, obuf, osem):
        p = pl.program_id(0)
        dq_ref, dk_ref, dv_ref = obuf.at[0], obuf.at[1], obuf.at[2]
        qs[...] = (q_ref[...] * SCALE).astype(BF16)
        ks[...] = k_ref[...].astype(BF16)
        vs[...] = v_ref[...].astype(BF16)
        dos[...] = do_ref[...].astype(BF16)
        dk_acc[...] = jnp.zeros((2, S, HD), F32)
        dv_acc[...] = jnp.zeros((2, S, HD), F32)
        tri_inc = _tri_ones(lambda j, s: j <= s)
        tri_exc = _tri_ones(lambda j, s: j < s)
        col = lax.broadcasted_iota(jnp.int32, (QB, KC), 1)
        rowi = lax.broadcasted_iota(jnp.int32, (QB, KC), 0)

        for qg in range(QGROUPS):
            def qblock(ii, carry0, qg=qg):
                t0 = pl.multiple_of((qg * QPG + ii) * QB, QB)
                qb = qs[pl.ds(t0, QB), :]
                dob = dos[pl.ds(t0, QB), :]
                tb = t_ref[pl.ds(t0, QB), :]
                dqs = [jnp.zeros((QB, HD), F32)] * 2
                pruns = [jnp.zeros((QB, BLK), F32)] * 2
                eruns = [jnp.zeros((QB, BLK), F32)] * 2
                for c in range(qg + 1):
                    s0 = c * KC
                    diag = c == qg
                    before = (s0 + col) < (t0 + rowi) if diag else None
                    for hh in range(2):
                        qh = qb[:, hh * HD:(hh + 1) * HD]
                        doh = dob[:, hh * HD:(hh + 1) * HD]
                        tt = tb[:, 64 * hh:64 * hh + 1]
                        kh = ks[s0:s0 + KC, hh * HD:(hh + 1) * HD]
                        vh = vs[s0:s0 + KC, hh * HD:(hh + 1) * HD]
                        lb, lk = _log_sig_pair(_dot(qh, kh, 1, 1))
                        if diag:
                            lk = jnp.where(before, lk, 0.0)
                        pin, ptot = _sub_sums(lk, tri_inc)
                        ws, prun = [], pruns[hh]
                        for s in range(NSUB):
                            ws.append(jnp.exp(lb[:, s * BLK:(s + 1) * BLK] + (tt - (pin[s] + prun))))
                            prun = prun + ptot[s]
                        w = jnp.concatenate(ws, axis=1)
                        if diag:
                            w = jnp.where(before, w, 0.0)
                        e = w * _dot(doh, vh, 1, 1)
                        pex, etot = _sub_sums(e, tri_exc)
                        cs, erun = [], eruns[hh]
                        for s in range(NSUB):
                            cs.append(pex[s] + erun)
                            erun = erun + etot[s]
                        sig = jnp.exp(lb)
                        dz = e * (1.0 - sig) - jnp.concatenate(cs, axis=1) * sig
                        if diag:
                            dz = jnp.where(before, dz, 0.0)
                        dz = dz.astype(BF16)
                        dqs[hh] = dqs[hh] + _dot(dz, kh, 1, 0)
                        dk_acc[hh, s0:s0 + KC, :] += _dot(dz, qh, 0, 0)
                        dv_acc[hh, s0:s0 + KC, :] += _dot(w.astype(BF16), doh, 0, 0)
                        pruns[hh], eruns[hh] = prun, erun
                dq_ref[pl.ds(t0, QB), :] = (jnp.concatenate(dqs, axis=1) * SCALE).astype(BF16)
                return carry0

            lax.fori_loop(0, QPG, qblock, 0)
        dk_ref[...] = jnp.concatenate([dk_acc[0], dk_acc[1]], axis=1).astype(BF16)
        dv_ref[...] = jnp.concatenate([dv_acc[0], dv_acc[1]], axis=1).astype(BF16)
        puts = [pltpu.make_async_copy(obuf.at[i], dqkv_ref.at[off + p], osem.at[i]) for i, off in enumerate((q0, k0, v0))]
        for cp in puts:
            cp.start()
        for cp in puts:
            cp.wait()

    slab = lambda off: pl.BlockSpec((None, S, LANES), lambda p: (off + p, 0, 0))
    pair = pl.BlockSpec((None, S, LANES), lambda p: (p, 0, 0))
    return pl.pallas_call(
        body, grid=(2,), in_specs=[slab(q0), slab(k0), slab(v0), pair, pair, ANY], out_specs=ANY,
        out_shape=jax.ShapeDtypeStruct(dqkv.shape, BF16),
        scratch_shapes=[pltpu.VMEM((S, LANES), BF16)] * 4 + [pltpu.VMEM((2, S, HD), F32)] * 2
        + [pltpu.VMEM((3, S, LANES), BF16), pltpu.SemaphoreType.DMA((3,))],
        input_output_aliases={5: 0},
        compiler_params=_cp("arbitrary"), name=name)(proj, proj, proj, do, tot, dqkv)


def _cat_slabs(ref):
    return jnp.concatenate([ref[s] for s in range(ref.shape[0])], axis=1)


def _merge_fwd(o_a, o_b, o_c, gates, b_gate, wa, wb, wc, w_out, name):
    tm = MERGE_TILE

    def body(oa_ref, ob_ref, oc_ref, g_ref, bg_ref, wa_ref, wb_ref, wc_ref, wo_ref, mg_ref, mo_ref):
        acc = jnp.zeros((tm, D), F32)
        for i, (o_ref, w_ref) in enumerate(((oa_ref, wa_ref), (ob_ref, wb_ref), (oc_ref, wc_ref))):
            pr = _dot(_cat_slabs(o_ref).astype(BF16), w_ref[...], 1, 0)
            sg = jax.nn.sigmoid(g_ref[:, i * D:(i + 1) * D] + bg_ref[i:i + 1, :])
            acc = acc + sg * pr
        mg = acc.astype(BF16)
        mg_ref[...] = mg
        mo_ref[...] = _dot(mg, wo_ref[...], 1, 0)

    slabs = lambda n: pl.BlockSpec((n, tm, LANES), lambda i: (0, i, 0))
    full = lambda r, c: pl.BlockSpec((r, c), lambda i: (0, 0))
    row = pl.BlockSpec((tm, D), lambda i: (i, 0))
    return pl.pallas_call(
        body, grid=(S // tm,),
        in_specs=[slabs(2), slabs(4), slabs(2), pl.BlockSpec((tm, GATE_COLS), lambda i: (i, 0)), full(3, D),
                  full(256, D), full(512, D), full(256, D), full(D, D)],
        out_specs=[row, row],
        out_shape=[jax.ShapeDtypeStruct((S, D), BF16), jax.ShapeDtypeStruct((S, D), F32)],
        compiler_params=_cp("parallel"), name=name)(o_a, o_b, o_c, gates, b_gate, wa, wb, wc, w_out)


def _merge_bwd(d_mo, o_a, o_b, o_c, gates, b_gate, wa, wb, wc, w_out, name):
    tm = MERGE_TILE
    nsteps = S // tm

    def body(dmo_ref, oa_ref, ob_ref, oc_ref, g_ref, bg_ref, wa_ref, wb_ref, wc_ref, wo_ref,
             doa_ref, dob_ref, doc_ref, dg_ref, dwa_ref, dwb_ref, dwc_ref, dbg_ref, acc_a, acc_b, acc_c):
        @pl.when(pl.program_id(0) == 0)
        def _():
            acc_a[...] = jnp.zeros(acc_a.shape, F32)
            acc_b[...] = jnp.zeros(acc_b.shape, F32)
            acc_c[...] = jnp.zeros(acc_c.shape, F32)
            dbg_ref[...] = jnp.zeros(dbg_ref.shape, F32)

        dmg = _dot(dmo_ref[...], wo_ref[...], 1, 1)
        trip = ((oa_ref, wa_ref, doa_ref, acc_a), (ob_ref, wb_ref, dob_ref, acc_b), (oc_ref, wc_ref, doc_ref, acc_c))
        for i, (o_ref, w_ref, do_ref, dw_ref) in enumerate(trip):
            ob = _cat_slabs(o_ref).astype(BF16)
            pr = _dot(ob, w_ref[...], 1, 0)
            sg = jax.nn.sigmoid(g_ref[:, i * D:(i + 1) * D] + bg_ref[i:i + 1, :])
            dgate = dmg * pr * sg * (1.0 - sg)
            dg_ref[:, i * D:(i + 1) * D] = dgate.astype(BF16)
            dbg_ref[i:i + 1, :] += jnp.sum(dgate, axis=0, keepdims=True)
            dpr = (dmg * sg).astype(BF16)
            do = _dot(dpr, w_ref[...], 1, 1)
            for s in range(do_ref.shape[0]):
                do_ref[s] = do[:, s * LANES:(s + 1) * LANES]
            dw_ref[...] += _dot(ob, dpr, 0, 0)

        @pl.when(pl.program_id(0) == nsteps - 1)
        def _():
            dwa_ref[...] = acc_a[...].astype(BF16)
            dwb_ref[...] = acc_b[...].astype(BF16)
            dwc_ref[...] = acc_c[...].astype(BF16)

    slabs = lambda n: pl.BlockSpec((n, tm, LANES), lambda i: (0, i, 0))
    full = lambda r, c: pl.BlockSpec((r, c), lambda i: (0, 0))
    row = pl.BlockSpec((tm, D), lambda i: (i, 0))
    return pl.pallas_call(
        body, grid=(S // tm,),
        in_specs=[row, slabs(2), slabs(4), slabs(2), pl.BlockSpec((tm, GATE_COLS), lambda i: (i, 0)), full(3, D),
                  full(256, D), full(512, D), full(256, D), full(D, D)],
        out_specs=[slabs(2), slabs(4), slabs(2), pl.BlockSpec((tm, GATE_COLS), lambda i: (i, 0)),
                   full(256, D), full(512, D), full(256, D), full(3, D)],
        out_shape=[jax.ShapeDtypeStruct((2, S, LANES), F32), jax.ShapeDtypeStruct((4, S, LANES), F32),
                   jax.ShapeDtypeStruct((2, S, LANES), F32), jax.ShapeDtypeStruct((S, GATE_COLS), BF16),
                   jax.ShapeDtypeStruct((256, D), BF16), jax.ShapeDtypeStruct((512, D), BF16),
                   jax.ShapeDtypeStruct((256, D), BF16), jax.ShapeDtypeStruct((3, D), F32)],
        scratch_shapes=[pltpu.VMEM((256, D), F32), pltpu.VMEM((512, D), F32), pltpu.VMEM((256, D), F32)],
        compiler_params=_cp("arbitrary"), name=name)(d_mo, o_a, o_b, o_c, gates, b_gate, wa, wb, wc, w_out)


FC = 256
GELU_K = math.sqrt(2.0 / math.pi)
GELU_C = 0.044715


RC = 64
NRC = S // RC


def _down(tail, cur, n):
    row = lax.broadcasted_iota(jnp.int32, tail.shape, 0)
    rolled = pltpu.roll(cur, n, 0)
    first = jnp.where(row < n, pltpu.roll(tail, n, 0), rolled[0:8])
    return jnp.concatenate([first, rolled[8:]], axis=0)


def _up(cur, head, n):
    row = lax.broadcasted_iota(jnp.int32, head.shape, 0)
    rolled = pltpu.roll(cur, RC - n, 0)
    last = jnp.where(row >= 8 - n, pltpu.roll(head, 8 - n, 0), rolled[RC - 8:])
    return jnp.concatenate([rolled[:RC - 8], last], axis=0)


def _conv_chunk(load, j, w_ref, b_ref, half):
    r0 = pl.multiple_of(j * RC, RC)
    cur = load(r0, RC).astype(F32)
    tail = load(pl.multiple_of(jnp.maximum(r0 - 16, 0), 16), 16).astype(F32)[8:16]
    tail = jnp.where(j > 0, tail, 0.0)
    d1 = _down(tail, cur, 1)
    d2 = _down(tail, cur, 2)
    y = w_ref[0:1, half, :] * d2 + w_ref[1:2, half, :] * d1 + w_ref[2:3, half, :] * cur + b_ref[half:half + 1, :]
    return y, cur, d1, d2


def _chunk(j):
    return pl.ds(pl.multiple_of(j * RC, RC), RC)


def _fold8(x):
    return jnp.sum(x.reshape(RC // 8, 8, x.shape[-1]), axis=0)


def _ffn_act(u, conv_w, conv_b, name):
    def body(u_ref, w_ref, b_ref, a_ref, y_ref):
        def step(j, carry):
            yg = _conv_chunk(lambda r, n: u_ref[0, pl.ds(r, n), :], j, w_ref, b_ref, 0)[0]
            yv = _conv_chunk(lambda r, n: u_ref[1, pl.ds(r, n), :], j, w_ref, b_ref, 1)[0]
            th = jnp.tanh(GELU_K * (yg + GELU_C * yg * yg * yg))
            a_ref[_chunk(j), :] = (0.5 * yg * (1.0 + th) * yv).astype(BF16)
            y_ref[0, _chunk(j), :] = yg.astype(BF16)
            y_ref[1, _chunk(j), :] = yv.astype(BF16)
            return carry

        lax.fori_loop(0, NRC, step, 0)

    return pl.pallas_call(
        body, grid=(D_FF // FC,),
        in_specs=[pl.BlockSpec((2, S, FC), lambda j: (0, 0, j)), pl.BlockSpec((3, 2, FC), lambda j: (0, 0, j)),
                  pl.BlockSpec((2, FC), lambda j: (0, j))],
        out_specs=[pl.BlockSpec((S, FC), lambda j: (0, j)), pl.BlockSpec((2, S, FC), lambda j: (0, 0, j))],
        out_shape=[jax.ShapeDtypeStruct((S, D_FF), BF16), jax.ShapeDtypeStruct((2, S, D_FF), BF16)],
        compiler_params=_cp("parallel"), name=name)(u, conv_w, conv_b)


def _ffn_act_bwd(u, y, d_a, conv_w, name):
    def body(u_ref, y_ref, da_ref, w_ref, du_ref, dw_ref, db_ref, dy_s):
        def first(j, acc):
            yg = y_ref[0, _chunk(j), :].astype(F32)
            yv = y_ref[1, _chunk(j), :].astype(F32)
            th = jnp.tanh(GELU_K * (yg + GELU_C * yg * yg * yg))
            gelu = 0.5 * yg * (1.0 + th)
            dgelu = 0.5 * (1.0 + th) + 0.5 * yg * (1.0 - th * th) * GELU_K * (1.0 + 3.0 * GELU_C * yg * yg)
            da = da_ref[_chunk(j), :].astype(F32)
            dyg = da * yv * dgelu
            dyv = da * gelu
            dy_s[0, _chunk(j), :] = dyg
            dy_s[1, _chunk(j), :] = dyv
            return acc[0] + _fold8(dyg), acc[1] + _fold8(dyv)

        zero = jnp.zeros((8, FC), F32)
        accb = lax.fori_loop(0, NRC, first, (zero, zero))
        for half in range(2):
            db_ref[half:half + 1, :] = jnp.sum(accb[half], axis=0, keepdims=True)

        def second(j, acc):
            new = []
            for half in range(2):
                cur = dy_s[half, _chunk(j), :]
                h0 = pl.multiple_of(jnp.minimum((j + 1) * RC, S - 8), 8)
                head = jnp.where(j < NRC - 1, dy_s[half, pl.ds(h0, 8), :], 0.0)
                up1 = _up(cur, head, 1)
                up2 = _up(cur, head, 2)
                du = w_ref[2:3, half, :] * cur + w_ref[1:2, half, :] * up1 + w_ref[0:1, half, :] * up2
                du_ref[half, _chunk(j), :] = du.astype(BF16)
                uu = u_ref[half, _chunk(j), :].astype(F32)
                new += [_fold8(up2 * uu), _fold8(up1 * uu), _fold8(cur * uu)]
            return tuple(a + n for a, n in zip(acc, new))

        accw = lax.fori_loop(0, NRC, second, tuple(zero for _ in range(6)))
        for half in range(2):
            for k in range(3):
                dw_ref[k:k + 1, half, :] = jnp.sum(accw[3 * half + k], axis=0, keepdims=True)

    return pl.pallas_call(
        body, grid=(D_FF // FC,),
        in_specs=[pl.BlockSpec((2, S, FC), lambda j: (0, 0, j)), pl.BlockSpec((2, S, FC), lambda j: (0, 0, j)),
                  pl.BlockSpec((S, FC), lambda j: (0, j)), pl.BlockSpec((3, 2, FC), lambda j: (0, 0, j))],
        out_specs=[pl.BlockSpec((2, S, FC), lambda j: (0, 0, j)), pl.BlockSpec((3, 2, FC), lambda j: (0, 0, j)),
                   pl.BlockSpec((2, FC), lambda j: (0, j))],
        out_shape=[jax.ShapeDtypeStruct((2, S, D_FF), BF16), jax.ShapeDtypeStruct((3, 2, D_FF), F32),
                   jax.ShapeDtypeStruct((2, D_FF), F32)],
        scratch_shapes=[pltpu.VMEM((2, S, FC), F32)],
        compiler_params=_cp("parallel"), name=name)(u, y, d_a, conv_w)


def _layer_fwd(x, h1, w, bias, lname):
    n = lambda s: f"{lname}_{s}"
    w.need("in", h1)
    tn = 768
    proj = _mm(h1, w["w_in"], grid=(1, QKV_COLS // tn, 1),
               a_spec=pl.BlockSpec((S, D), lambda i, j, k: (i, 0)),
               b_spec=pl.BlockSpec((tn, D), lambda i, j, k: (j, 0)),
               out_shape=jax.ShapeDtypeStruct((QKV_SLABS, S, LANES), F32),
               out_spec=pl.BlockSpec((tn // LANES, S, LANES), lambda i, j, k: (j, i, 0)),
               ca=1, cb=1, acc_shape=(S, tn), out_slab=True, name=n("proj_qkv"))
    gates = _mm(h1, w["w_in"], grid=(1, GATE_COLS // tn, 1),
                a_spec=pl.BlockSpec((S, D), lambda i, j, k: (i, 0)),
                b_spec=pl.BlockSpec((tn, D), lambda i, j, k: (j + QKV_COLS // tn, 0)),
                out_shape=jax.ShapeDtypeStruct((S, GATE_COLS), BF16),
                out_spec=pl.BlockSpec((S, tn), lambda i, j, k: (i, j)),
                ca=1, cb=1, acc_shape=(S, tn), name=n("proj_gate"))
    nums, stats = [], []
    for g, (_, d) in enumerate(A_GROUPS):
        nm, st = _band_fwd(proj, bias, w["sinks"], d=d, q0=2 * g, k0=6 + 2 * g, v0=12 + 2 * g, npairs=2, bias0=2 * g,
                           shared_kv=False, name=n(f"attn_a{g}_fwd"))
        nums.append(nm)
        stats.append(st)
    o_a, lse_a = _combine_a(nums, stats, n("attn_a_combine"))
    o_b, lse_b = _band_fwd(proj, bias, w["sinks"], d=1, q0=18, k0=22, v0=23, npairs=4, bias0=6, shared_kv=True,
                           name=n("attn_b_fwd"))
    o_c, tot_c = _stick_fwd(proj, q0=24, k0=26, v0=28, name=n("attn_c_fwd"))
    w.need("mix", tot_c)
    merged, mo = _merge_fwd(o_a, o_b, o_c, gates, w["b_gate"], w["w_br_a"], w["w_br_b"], w["w_br_c"], w["w_out"], n("merge_fwd"))
    x2, h2 = _postnorm_res(x, mo, w["attn_post_norm"], w["ffn_pre_norm"], n("attn_post"))
    w.need("ffn", h2)
    u = _mm(h2, w["w_up"], grid=(1, 2 * D_FF // 1024, 1),
            a_spec=pl.BlockSpec((S, D), lambda i, j, k: (i, 0)),
            b_spec=pl.BlockSpec((D, 1024), lambda i, j, k: (0, j)),
            out_shape=jax.ShapeDtypeStruct((2, S, D_FF), BF16),
            out_spec=pl.BlockSpec((None, S, 1024), lambda i, j, k: (j // 4, i, j % 4)),
            ca=1, cb=0, acc_shape=(S, 1024), name=n("ffn_up"))
    a, y = _ffn_act(u, w["conv_w"], w["conv_b"], n("ffn_act"))
    fo = _mm_nn(a, w["w_down"], F32, 1024, 1024, 2048, n("ffn_down"))
    saved = dict(x=x, h1=h1, proj=proj, gates=gates, o_a=o_a, lse_a=lse_a, o_b=o_b, lse_b=lse_b, o_c=o_c, tot_c=tot_c,
                 merged=merged, mo=mo, x2=x2, h2=h2, u=u, y=y, a=a, fo=fo)
    return saved


def _layer_bwd(dx3, sv, w, bias, lname, tok=None, on_part=None, d_fo=None, below=None):
    n = lambda s: f"{lname}_{s}"
    g = {}

    def part(group, vec):
        t = on_part(group, g) if on_part is not None else None
        return vec if t is None else vec + t

    if d_fo is None:
        gain = w["ffn_post_norm"] if tok is None else w["ffn_post_norm"] + tok
        d_fo, g["ffn_post_norm"] = _norm_bwd(sv["fo"], gain, [dx3], None, BF16, n("ffn_post_bwd"))
    else:
        d_fo, g["ffn_post_norm"] = d_fo
    d_a = _mm_nt(d_fo, w["w_down"], BF16, S, 1024, 1024, n("ffn_down_bwd_x"))
    g["w_down"] = _mm_tn(sv["a"], d_fo, BF16, 1024, 1024, S, n("ffn_down_bwd_w"))
    d_u, dcw, dcb = _ffn_act_bwd(sv["u"], sv["y"], d_a, w["conv_w"], n("ffn_act_bwd"))
    g["conv_w"] = dcw.reshape(3, 2 * D_FF)
    g["conv_b"] = dcb.reshape(1, 2 * D_FF)
    g["w_up"] = _mm(sv["h2"], d_u, grid=(1, 2 * D_FF // 1024, 1),
                    a_spec=pl.BlockSpec((S, D), lambda i, j, k: (k, 0)),
                    b_spec=pl.BlockSpec((None, S, 1024), lambda i, j, k: (j // 4, k, j % 4)),
                    out_shape=jax.ShapeDtypeStruct((D, 2 * D_FF), BF16),
                    out_spec=pl.BlockSpec((D, 1024), lambda i, j, k: (0, j)),
                    ca=0, cb=0, acc_shape=(D, 1024), name=n("ffn_up_bwd_w"))
    tok_ffn = on_part("ffn", g) if on_part is not None else None
    d_h2 = _mm(d_u, w["w_up"], grid=(S // 1024, 1, 2),
               a_spec=pl.BlockSpec((None, 1024, D_FF), lambda i, j, k: (k, i, 0)),
               b_spec=pl.BlockSpec((D, D_FF), lambda i, j, k: (0, k)),
               out_shape=jax.ShapeDtypeStruct((S, D), F32),
               out_spec=pl.BlockSpec((1024, D), lambda i, j, k: (i, 0)),
               ca=1, cb=1, acc_shape=(1024, D), after=tok_ffn, name=n("ffn_up_bwd_x"))
    dx2, d_mo, g["ffn_pre_norm"], g["attn_post_norm"] = _norm_bwd_chain(
        sv["x2"], w["ffn_pre_norm"], [d_h2], dx3, sv["mo"], w["attn_post_norm"], n("ffn_pre_attn_post_bwd"))
    g["w_out"] = _mm_tn(sv["merged"], d_mo, BF16, 1024, 1024, S, n("out_bwd_w"))
    do_a, do_b, do_c, d_gates, dwa, dwb, dwc, g["b_gate"] = _merge_bwd(
        d_mo, sv["o_a"], sv["o_b"], sv["o_c"], sv["gates"], w["b_gate"], w["w_br_a"], w["w_br_b"], w["w_br_c"],
        w["w_out"], n("merge_bwd"))
    g["w_br_a"], g["w_br_b"], g["w_br_c"] = dwa, dwb, dwc
    sinks = part("mix", w["sinks"])
    proj = sv["proj"]
    dqkv = lax.empty((QKV_SLABS, S, LANES), BF16)
    gbias = []
    for gi, (_, d) in enumerate(A_GROUPS):
        dqkv, gg, _ = _band_bwd(proj, bias, sv["o_a"], do_a, sv["lse_a"], sinks, dqkv, d=d, q0=2 * gi, k0=6 + 2 * gi,
                                v0=12 + 2 * gi, npairs=2, bias0=2 * gi, shared_kv=False, name=n(f"attn_a{gi}_bwd"))
        gbias.append(gg)
    dqkv, ggb, dsink = _band_bwd(proj, bias, sv["o_b"], do_b, sv["lse_b"], sinks, dqkv, d=1, q0=18, k0=22, v0=23,
                                 npairs=4, bias0=6, shared_kv=True, name=n("attn_b_bwd"))
    gbias.append(ggb)
    g["bias_g"] = jnp.concatenate(gbias, axis=0).reshape(N_BIAS_HEADS, BLK, 2 * BLK)
    g["sinks"] = dsink[:, 0, :2].reshape(1, 8)
    dqkv = _stick_bwd(proj, do_c, sv["tot_c"], dqkv, q0=24, k0=26, v0=28, name=n("attn_c_bwd"))
    ts = 6
    tsx = QKV_SLABS
    dw_in = _mm(dqkv, sv["h1"], grid=(QKV_SLABS // ts, 1, 1),
                a_spec=pl.BlockSpec((ts, S, LANES), lambda i, j, k: (i, k, 0)),
                b_spec=pl.BlockSpec((S, D), lambda i, j, k: (k, 0)),
                out_shape=jax.ShapeDtypeStruct((IN_COLS, D), BF16),
                out_spec=pl.BlockSpec((ts * LANES, D), lambda i, j, k: (i, 0)),
                ca=0, cb=0, acc_shape=(ts * LANES, D), a_slab=True, name=n("in_bwd_w_qkv"))
    g["w_in"] = _mm(d_gates, sv["h1"], grid=(GATE_COLS // 768, 1, 1),
                    a_spec=pl.BlockSpec((S, 768), lambda i, j, k: (k, i)),
                    b_spec=pl.BlockSpec((S, D), lambda i, j, k: (k, 0)),
                    out_shape=jax.ShapeDtypeStruct((IN_COLS, D), BF16),
                    out_spec=pl.BlockSpec((768, D), lambda i, j, k: (i + QKV_COLS // 768, 0)),
                    ca=0, cb=0, acc_shape=(768, D), alias_out=dw_in, name=n("in_bwd_w_gate"))
    tok_in = on_part("in", g) if on_part is not None else None
    d_h1a = _mm(dqkv, w["w_in"], grid=(S // 1024, 1, QKV_SLABS // tsx),
                a_spec=pl.BlockSpec((tsx, 1024, LANES), lambda i, j, k: (k, i, 0)),
                b_spec=pl.BlockSpec((tsx * LANES, D), lambda i, j, k: (k, 0)),
                out_shape=jax.ShapeDtypeStruct((S, D), F32),
                out_spec=pl.BlockSpec((1024, D), lambda i, j, k: (i, 0)),
                ca=1, cb=0, acc_shape=(1024, D), a_slab=True, after=tok_in, name=n("in_bwd_x_qkv"))
    d_h1b = _mm(d_gates, w["w_in"], grid=(S // 1024, 1, GATE_COLS // 768),
                a_spec=pl.BlockSpec((1024, 768), lambda i, j, k: (i, k)),
                b_spec=pl.BlockSpec((768, D), lambda i, j, k: (k + QKV_COLS // 768, 0)),
                out_shape=jax.ShapeDtypeStruct((S, D), F32),
                out_spec=pl.BlockSpec((1024, D), lambda i, j, k: (i, 0)),
                ca=1, cb=0, acc_shape=(1024, D), after=tok_in, name=n("in_bwd_x_gate"))
    if below is None:
        dx, g["attn_pre_norm"] = _norm_bwd(sv["x"], w["attn_pre_norm"], [d_h1a, d_h1b], dx2, F32, n("attn_pre_bwd"))
        return dx, g, tok_in, None
    dx, d_fo_below, g["attn_pre_norm"], dg_below = _norm_bwd_chain(
        sv["x"], w["attn_pre_norm"], [d_h1a, d_h1b], dx2, below[0], below[1], n("attn_pre_ffn_post_bwd"))
    return dx, g, tok_in, (d_fo_below, dg_below)


def _local_step(x, target, ws, rel_bias, tok=None, on_grads=None):
    buckets = jnp.asarray(_bucket_tiles())
    bias = _bias_tiles(rel_bias, buckets, "bias_tiles").reshape(N_BIAS_HEADS // 2, 2, 2, BLK, 2 * BLK)
    saved = []
    gain0 = ws[0]["attn_pre_norm"] if tok is None else ws[0]["attn_pre_norm"] + tok
    h1 = _prenorm(x, gain0, "l0_attn_pre")
    for l in range(DEPTH):
        sv = _layer_fwd(x, h1, ws[l], bias, f"l{l}")
        saved.append(sv)
        if l + 1 < DEPTH:
            x, h1 = _postnorm_res(sv["x2"], sv["fo"], ws[l]["ffn_post_norm"], ws[l + 1]["attn_pre_norm"], f"l{l}_ffn_post")
    top = saved[-1]
    dy, loss_tile, d_fo_top, dg_top = _loss_head(top["x2"], top["fo"], ws[-1]["ffn_post_norm"], target, "loss_head")
    grads = [None] * DEPTH
    tok, d_fo = None, (d_fo_top, dg_top)
    for l in reversed(range(DEPTH)):
        on_part = None if on_grads is None else functools.partial(on_grads, l)
        below = (saved[l - 1]["fo"], ws[l - 1]["ffn_post_norm"]) if l > 0 else None
        dy, grads[l], tok, d_fo = _layer_bwd(dy, saved[l], ws[l], bias, f"l{l}", tok, on_part, d_fo, below)
    g_rel = _bias_grad([grads[l]["bias_g"] for l in range(DEPTH)], buckets, "bias_grad")[:, :N_BIAS_HEADS]
    return loss_tile, dy, grads, g_rel


def _coords():
    return lax.axis_index("x"), lax.axis_index("y"), lax.axis_index("c")


def _peer(rel):
    x, y, c = _coords()
    return (1 - x if rel & 4 else x, 1 - y if rel & 2 else y, 1 - c if rel & 1 else c)


def _exchange(srcs, dst_shapes, src_win, dst_win, name, after=None):
    nt = len(srcs)
    extra = [] if after is None else [after]

    def body(*refs):
        src_refs, dst_refs = refs[:nt], refs[nt + len(extra):2 * nt + len(extra)]
        send_sems, recv_sems, local_sems = refs[2 * nt + len(extra):]
        x, y, c = _coords()
        me = 4 * x + 2 * y + c
        locals_ = []
        for t in range(nt):
            cp = pltpu.make_async_copy(src_win(t, src_refs[t], me), dst_win(t, dst_refs[t], me), local_sems.at[t])
            cp.start()
            locals_.append(cp)
        sends = []
        for rel in range(1, NDEV):
            px, py, pc = _peer(rel)
            q = 4 * px + 2 * py + pc
            for t in range(nt):
                cp = pltpu.make_async_remote_copy(
                    src_ref=src_win(t, src_refs[t], q), dst_ref=dst_win(t, dst_refs[t], me),
                    send_sem=send_sems.at[rel - 1, t], recv_sem=recv_sems.at[rel - 1, t],
                    device_id=(px, py, pc), device_id_type=MESH)
                cp.start()
                sends.append(cp)
        for rel in range(1, NDEV):
            px, py, pc = _peer(rel)
            q = 4 * px + 2 * py + pc
            for t in range(nt):
                pltpu.make_async_remote_copy(
                    src_ref=src_win(t, src_refs[t], me), dst_ref=dst_win(t, dst_refs[t], q),
                    send_sem=send_sems.at[rel - 1, t], recv_sem=recv_sems.at[rel - 1, t],
                    device_id=(px, py, pc), device_id_type=MESH).wait_recv()
        for cp in sends:
            cp.wait_send()
        for cp in locals_:
            cp.wait()

    return pl.pallas_call(
        body, in_specs=[ANY] * (nt + len(extra)), out_specs=[ANY] * nt, out_shape=dst_shapes,
        scratch_shapes=[pltpu.SemaphoreType.DMA((NDEV - 1, nt)), pltpu.SemaphoreType.DMA((NDEV - 1, nt)),
                        pltpu.SemaphoreType.DMA((nt,))],
        name=name)(*srcs, *extra)


BIG = (("w_in", 0, 864), ("w_br_a", 1, 128), ("w_br_b", 1, 128), ("w_br_c", 1, 128), ("w_out", 0, 128),
       ("w_up", 1, 1024), ("w_down", 0, 512))


NBIG = len(BIG)
BIG_FULL = {"w_in": (IN_COLS, D), "w_br_a": (256, D), "w_br_b": (512, D), "w_br_c": (256, D), "w_out": (D, D),
            "w_up": (D, 2 * D_FF), "w_down": (D_FF, D)}
SHARD_ROWS = {"w_in": 288, "w_up": 256, "w_down": 256}
LAYER_GROUPS = (("in", (0,)), ("mix", (1, 2, 3, 4)), ("ffn", (5, 6)))

HBM_SPEC = pl.BlockSpec(memory_space=pltpu.HBM)
SEM_SPEC = pl.BlockSpec(memory_space=pltpu.SEMAPHORE)


def _hbm(a):
    return pltpu.with_memory_space_constraint(a, pltpu.HBM)


def _shard_window(t, ref, k):
    nm, ax, ext = BIG[t % NBIG]
    off = pl.multiple_of(k * ext, ext)
    if ax == 0:
        return ref.at[pl.ds(off, ext), :]
    return ref.at[:, pl.ds(off, ext)]


def _whole(t, ref, k):
    return ref


def _slot(t, ref, k):
    return ref.at[k]


def _own_block_spec(t, rows, me_of):
    nm, ax, ext = BIG[t % NBIG]
    r, c = BIG_FULL[nm]
    if ax == 0:
        return pl.BlockSpec((rows, c), lambda i, m: (me_of(m) * (ext // rows) + i, 0))
    return pl.BlockSpec((rows, ext), lambda i, m: (i, me_of(m)))


def _cast_own(t, shards, me_arr, name):
    nm, ax, ext = BIG[t % NBIG]
    layer = t // NBIG
    _, nr, nc = shards.shape
    rows = SHARD_ROWS.get(nm, nr)
    shape = BIG_FULL[nm]

    def body(m_ref, s_ref, o_ref):
        o_ref[...] = s_ref[...].astype(BF16)

    return pl.pallas_call(
        body, grid_spec=pltpu.PrefetchScalarGridSpec(
            num_scalar_prefetch=1, grid=(nr // rows,),
            in_specs=[pl.BlockSpec((None, rows, nc), lambda i, m: (layer, i, 0))],
            out_specs=_own_block_spec(t, rows, lambda m: m[0])),
        out_shape=jax.ShapeDtypeStruct(shape, BF16), compiler_params=_cp("arbitrary"), name=name)(me_arr, shards)


ALL_RELS = tuple(range(1, NDEV))
NEAR_RELS = (1, 2, 4, 6)
FAR_RELS = (2, 4, 6)


def _xchg_start(srcs, lands, groups, src_win, dst_win, after, name, rels=ALL_RELS, tids=None):
    ns = 0 if srcs is None else len(srcs)
    nt, ng = len(lands), len(groups)
    ins = ([] if srcs is None else list(srcs)) + list(lands)

    def body(*refs):
        src_refs, land_refs = refs[:ns], refs[ns:ns + nt]
        sems = refs[ns + nt + 1:ns + nt + 1 + 2 * ng]
        token = refs[-1]
        x, y, c = _coords()
        me = 4 * x + 2 * y + c
        for gi, grp in enumerate(groups):
            for j, t in enumerate(grp):
                tid = t if tids is None else tids[t]
                for ri, rel in enumerate(rels):
                    px, py, pc = _peer(rel)
                    q = 4 * px + 2 * py + pc
                    src = dst_win(tid, land_refs[t], me) if srcs is None else src_win(tid, src_refs[t], q)
                    pltpu.make_async_remote_copy(
                        src_ref=src, dst_ref=dst_win(tid, land_refs[t], me),
                        send_sem=sems[2 * gi].at[ri * len(grp) + j],
                        recv_sem=sems[2 * gi + 1].at[ri * len(grp) + j],
                        device_id=(px, py, pc), device_id_type=MESH).start()
        token[...] = jnp.zeros((8, LANES), F32)

    out_shape = []
    for grp in groups:
        out_shape += [pltpu.SemaphoreType.DMA((len(rels) * len(grp),))] * 2
    out_shape += [pltpu.HBM(a.shape, a.dtype) for a in ins]
    out_shape.append(jax.ShapeDtypeStruct((8, LANES), F32))
    outs = pl.pallas_call(
        body, in_specs=[HBM_SPEC] * len(ins) + [ANY],
        out_specs=[SEM_SPEC] * (2 * ng) + [HBM_SPEC] * len(ins) + [pl.BlockSpec(memory_space=pltpu.VMEM)],
        out_shape=out_shape, input_output_aliases={i: 2 * ng + i for i in range(len(ins))},
        compiler_params=pltpu.CompilerParams(has_side_effects=pltpu.SideEffectType.DATAFLOW_SIDE_EFFECTING),
        name=name)(*[_hbm(a) for a in ins], after)
    sems = [(outs[2 * gi], outs[2 * gi + 1]) for gi in range(ng)]
    thru = list(outs[2 * ng:2 * ng + len(ins)])
    return sems, (None if srcs is None else thru[:ns]), thru[ns:], outs[-1]


def _xchg_wait(sems, srcs, lands, tids, after, src_win, dst_win, name, rels=ALL_RELS):
    ns = 0 if srcs is None else len(srcs)
    n = len(lands)
    send_sem, recv_sem = sems
    ins = ([] if srcs is None else list(srcs)) + list(lands)

    def body(*refs):
        src_refs, land_refs = refs[:ns], refs[ns:ns + n]
        ssem, rsem = refs[ns + n], refs[ns + n + 1]
        x, y, c = _coords()
        me = 4 * x + 2 * y + c
        for j, t in enumerate(tids):
            for ri, rel in enumerate(rels):
                px, py, pc = _peer(rel)
                q = 4 * px + 2 * py + pc
                src = dst_win(t, land_refs[j], me) if srcs is None else src_win(t, src_refs[j], q)
                cp = pltpu.make_async_remote_copy(
                    src_ref=src, dst_ref=dst_win(t, land_refs[j], q),
                    send_sem=ssem.at[ri * n + j], recv_sem=rsem.at[ri * n + j],
                    device_id=(px, py, pc), device_id_type=MESH)
                cp.wait_send()
                cp.wait_recv()

    outs = pl.pallas_call(
        body, in_specs=[HBM_SPEC] * len(ins) + [SEM_SPEC, SEM_SPEC, ANY], out_specs=[HBM_SPEC] * len(ins),
        out_shape=[pltpu.HBM(a.shape, a.dtype) for a in ins],
        input_output_aliases={i: i for i in range(len(ins))},
        compiler_params=pltpu.CompilerParams(has_side_effects=pltpu.SideEffectType.DATAFLOW_SIDE_EFFECTING),
        name=name)(*ins, send_sem, recv_sem, after)
    return (None if srcs is None else list(outs[:ns])), list(outs[ns:])


def _gather_forward(sems_in, lands, groups, tids, after, dst_win, name):
    nt, ng = len(lands), len(groups)

    def body(*refs):
        land_refs = refs[:nt]
        in_sems = refs[nt:nt + 2 * ng]
        out_sems = refs[nt + 2 * ng + 1:nt + 4 * ng + 1]
        token = refs[-1]
        x, y, c = _coords()
        me = 4 * x + 2 * y + c
        sib = (x, y, 1 - c)
        for gi, grp in enumerate(groups):
            n = len(grp)
            for j, pos in enumerate(grp):
                t = tids[pos]
                for ri, rel in enumerate(NEAR_RELS):
                    px, py, pc = _peer(rel)
                    q = 4 * px + 2 * py + pc
                    cp = pltpu.make_async_remote_copy(
                        src_ref=dst_win(t, land_refs[pos], me), dst_ref=dst_win(t, land_refs[pos], q),
                        send_sem=in_sems[2 * gi].at[ri * n + j], recv_sem=in_sems[2 * gi + 1].at[ri * n + j],
                        device_id=(px, py, pc), device_id_type=MESH)
                    cp.wait_send()
                    cp.wait_recv()
            for j, pos in enumerate(grp):
                t = tids[pos]
                for fi, rel in enumerate(FAR_RELS):
                    px, py, pc = _peer(rel)
                    q = 4 * px + 2 * py + pc
                    win = dst_win(t, land_refs[pos], q)
                    pltpu.make_async_remote_copy(
                        src_ref=win, dst_ref=win,
                        send_sem=out_sems[2 * gi].at[fi * n + j], recv_sem=out_sems[2 * gi + 1].at[fi * n + j],
                        device_id=sib, device_id_type=MESH).start()
        token[...] = jnp.zeros((8, LANES), F32)

    out_shape = []
    for grp in groups:
        out_shape += [pltpu.SemaphoreType.DMA((len(FAR_RELS) * len(grp),))] * 2
    out_shape += [pltpu.HBM(a.shape, a.dtype) for a in lands]
    out_shape.append(jax.ShapeDtypeStruct((8, LANES), F32))
    flat_sems = [s for pair in sems_in for s in pair]
    outs = pl.pallas_call(
        body, in_specs=[HBM_SPEC] * nt + [SEM_SPEC] * (2 * ng) + [ANY],
        out_specs=[SEM_SPEC] * (2 * ng) + [HBM_SPEC] * nt + [pl.BlockSpec(memory_space=pltpu.VMEM)],
        out_shape=out_shape, input_output_aliases={i: 2 * ng + i for i in range(nt)},
        compiler_params=pltpu.CompilerParams(has_side_effects=pltpu.SideEffectType.DATAFLOW_SIDE_EFFECTING),
        name=name)(*[_hbm(a) for a in lands], *flat_sems, after)
    sems = [(outs[2 * gi], outs[2 * gi + 1]) for gi in range(ng)]
    return sems, list(outs[2 * ng:2 * ng + nt]), outs[-1]


class _Weights:
    def __init__(self, ready, pending=None):
        self.ready = dict(ready)
        self.pending = dict(pending or {})

    def __getitem__(self, k):
        return self.ready[k]

    def need(self, group, after):
        fn = self.pending.pop(group, None)
        if fn is not None:
            self.ready.update(fn(after))


def _adamw_math(w, g, m, v):
    m2 = ADAM_B1 * m + (1.0 - ADAM_B1) * g
    v2 = ADAM_B2 * v + (1.0 - ADAM_B2) * (g * g)
    m_hat = m2 / (1.0 - ADAM_B1 ** ADAM_STEP)
    v_hat = v2 / (1.0 - ADAM_B2 ** ADAM_STEP)
    delta = -ADAM_LR * (m_hat / (jnp.sqrt(v_hat) + ADAM_EPS) + ADAM_WD * w)
    return delta, m2, v2


def _adamw(t, parts, own, me_arr, w, m, v, layer, prev, rows, name):
    nl, nr, nc = w.shape

    def body(me_ref, p_ref, own_ref, w_ref, m_ref, v_ref, *rest):
        g_ref, d_ref, m2_ref, v2_ref = rest[-4:]
        me = me_ref[0]
        g = None
        for k in range(NDEV):
            term = jnp.where(me == k, own_ref[...], p_ref[k]).astype(F32)
            g = term if g is None else g + term
        delta, m2, v2 = _adamw_math(w_ref[...], g, m_ref[...], v_ref[...])
        g_ref[...] = g
        d_ref[...] = delta
        m2_ref[...] = m2
        v2_ref[...] = v2

    blk = pl.BlockSpec((None, rows, nc), lambda i, mm: (layer, i, 0))
    pblk = pl.BlockSpec((NDEV, rows, nc), lambda i, mm: (0, i, 0))
    extra = [] if prev is None else list(prev)
    return pl.pallas_call(
        body, grid_spec=pltpu.PrefetchScalarGridSpec(
            num_scalar_prefetch=1, grid=(nr // rows,),
            in_specs=[pblk, _own_block_spec(t, rows, lambda mm: mm[0]), blk, blk, blk] + [ANY] * len(extra),
            out_specs=[blk] * 4),
        out_shape=[jax.ShapeDtypeStruct(w.shape, F32)] * 4,
        input_output_aliases={6 + k: k for k in range(len(extra))},
        compiler_params=_cp("arbitrary"), name=name)(me_arr, parts, own, w, m, v, *extra)


def _pack(vecs):
    flat = jnp.concatenate([v.reshape(-1).astype(F32) for v in vecs])
    n = flat.shape[0]
    rows = -(-n // (8 * LANES)) * 8
    return jnp.pad(flat, (0, rows * LANES - n)).reshape(rows, LANES)


ROWPACK = (("rel_bias", 32, 32, (NUM_BUCKETS, N_BIAS_HEADS)), ("sinks", 8, 8, (DEPTH, 8)),
           ("attn_pre_norm", 16, 16, (DEPTH, D)), ("attn_post_norm", 16, 16, (DEPTH, D)),
           ("ffn_pre_norm", 16, 16, (DEPTH, D)), ("ffn_post_norm", 16, 16, (DEPTH, D)),
           ("conv_b", 128, 128, (DEPTH, 2 * D_FF)), ("b_gate", 48, 8, (DEPTH, 3, 128)),
           ("conv_w", 384, 48, (DEPTH, 3, 1024)))
ROWS_OWN = sum(r for _, _, r, _ in ROWPACK)
N_REPL = 7
ROWS_REPL = sum(r for _, _, r, _ in ROWPACK[:N_REPL])
ROWS_SHARD = ROWS_OWN - ROWS_REPL


def _as_rows(a, rows):
    a = a.astype(F32)
    if a.shape[-1] < LANES:
        a = jnp.pad(a.reshape(-1, a.shape[-1]), ((0, 0), (0, LANES - a.shape[-1])))
    a = a.reshape(-1, LANES)
    return jnp.pad(a, ((0, rows - a.shape[0]), (0, 0)))


def _rowpack(arrs, entries=ROWPACK):
    return jnp.concatenate([_as_rows(arrs[nm], ro) for nm, _, ro, _ in entries], axis=0)


def _shard_rows(g):
    bg = jnp.transpose(g["b_gate"].astype(F32).reshape(DEPTH * 3, NDEV, LANES), (1, 0, 2))
    bg = jnp.pad(bg, ((0, 0), (0, 8 - DEPTH * 3), (0, 0)))
    cw = jnp.transpose(g["conv_w"].astype(F32).reshape(DEPTH * 3, NDEV, 8, LANES), (1, 0, 2, 3))
    return jnp.concatenate([bg, cw.reshape(NDEV, DEPTH * 3 * 8, LANES)], axis=1)


def _small_update(parts_repl, parts_shard, w, m, v, name):
    nsm = len(ROWPACK)

    def body(pr_ref, ps_ref, w_ref, m_ref, v_ref, *rest):
        outs = rest[:4 * nsm]
        loss_ref = rest[4 * nsm]
        g_s, d_s, m_s, v_s = rest[4 * nsm + 1:]
        gr, gs = pr_ref[0], ps_ref[0]
        for k in range(1, NDEV):
            gr = gr + pr_ref[k]
            gs = gs + ps_ref[k]
        g_s[0:ROWS_REPL, :] = gr[:ROWS_REPL]
        g_s[ROWS_REPL:ROWS_OWN, :] = gs
        loss_ref[...] = gr[ROWS_REPL:]
        delta, m2, v2 = _adamw_math(w_ref[...], g_s[...], m_ref[...], v_ref[...])
        d_s[...] = delta
        m_s[...] = m2
        v_s[...] = v2
        for kind, src in enumerate((g_s, d_s, m_s, v_s)):
            oo = 0
            for idx, (nm, rf, ro, shp) in enumerate(ROWPACK):
                o_ref = outs[kind * nsm + idx]
                if nm in ("rel_bias", "sinks"):
                    o_ref[...] = src[oo:oo + shp[0], 0:shp[1]]
                elif nm == "b_gate":
                    for l in range(DEPTH):
                        o_ref[l] = src[oo + 3 * l:oo + 3 * l + 3, :]
                elif nm == "conv_w":
                    for l in range(DEPTH):
                        for k in range(8):
                            o_ref[l, :, k * LANES:(k + 1) * LANES] = src[pl.ds(oo + 24 * l + k, 3, stride=8), :]
                else:
                    per = shp[1] // LANES
                    for k in range(per):
                        o_ref[:, k * LANES:(k + 1) * LANES] = src[pl.ds(oo + k, DEPTH, stride=per), :]
                oo += ro

    vm = pl.BlockSpec(memory_space=pltpu.VMEM)
    shapes = [jax.ShapeDtypeStruct(shp, F32) for _ in range(4) for _, _, _, shp in ROWPACK]
    shapes.append(jax.ShapeDtypeStruct((8, LANES), F32))
    outs = pl.pallas_call(
        body, in_specs=[vm] * 5, out_specs=[vm] * (4 * nsm + 1), out_shape=shapes,
        scratch_shapes=[pltpu.VMEM((ROWS_OWN, LANES), F32)] * 4,
        name=name)(parts_repl, parts_shard, w, m, v)
    names = [nm for nm, _, _, _ in ROWPACK]
    return [dict(zip(names, outs[kind * nsm:(kind + 1) * nsm])) for kind in range(4)] + [outs[-1]]


def kernel(x, rel_bias, attn_pre_norm, w_in, b_gate, sinks, w_br_a, w_br_b, w_br_c, w_out, attn_post_norm, ffn_pre_norm, w_up, conv_w, conv_b, w_down, ffn_post_norm, loss_target, m_rel_bias, m_attn_pre_norm, m_w_in, m_b_gate, m_sinks, m_w_br_a, m_w_br_b, m_w_br_c, m_w_out, m_attn_post_norm, m_ffn_pre_norm, m_w_up, m_conv_w, m_conv_b, m_w_down, m_ffn_post_norm, v_rel_bias, v_attn_pre_norm, v_w_in, v_b_gate, v_sinks, v_w_br_a, v_w_br_b, v_w_br_c, v_w_out, v_attn_post_norm, v_ffn_pre_norm, v_w_up, v_conv_w, v_conv_b, v_w_down, v_ffn_post_norm):
    P = dict(rel_bias=rel_bias, attn_pre_norm=attn_pre_norm, w_in=w_in, b_gate=b_gate, sinks=sinks, w_br_a=w_br_a,
             w_br_b=w_br_b, w_br_c=w_br_c, w_out=w_out, attn_post_norm=attn_post_norm, ffn_pre_norm=ffn_pre_norm,
             w_up=w_up, conv_w=conv_w, conv_b=conv_b, w_down=w_down, ffn_post_norm=ffn_post_norm)
    M = dict(rel_bias=m_rel_bias, attn_pre_norm=m_attn_pre_norm, w_in=m_w_in, b_gate=m_b_gate, sinks=m_sinks,
             w_br_a=m_w_br_a, w_br_b=m_w_br_b, w_br_c=m_w_br_c, w_out=m_w_out, attn_post_norm=m_attn_post_norm,
             ffn_pre_norm=m_ffn_pre_norm, w_up=m_w_up, conv_w=m_conv_w, conv_b=m_conv_b, w_down=m_w_down,
             ffn_post_norm=m_ffn_post_norm)
    V = dict(rel_bias=v_rel_bias, attn_pre_norm=v_attn_pre_norm, w_in=v_w_in, b_gate=v_b_gate, sinks=v_sinks,
             w_br_a=v_w_br_a, w_br_b=v_w_br_b, w_br_c=v_w_br_c, w_out=v_w_out, attn_post_norm=v_attn_post_norm,
             ffn_pre_norm=v_ffn_pre_norm, w_up=v_w_up, conv_w=v_conv_w, conv_b=v_conv_b, w_down=v_w_down,
             ffn_post_norm=v_ffn_post_norm)
    tr = lambda a: jnp.swapaxes(a, 1, 2)
    PB = {nm: (tr(P[nm]) if nm == "w_in" else P[nm]) for nm, _, _ in BIG}
    MB = {nm: (tr(M[nm]) if nm == "w_in" else M[nm]) for nm, _, _ in BIG}
    VB = {nm: (tr(V[nm]) if nm == "w_in" else V[nm]) for nm, _, _ in BIG}
    xi, yi, ci = _coords()
    me = 4 * xi + 2 * yi + ci

    me_arr = me.astype(jnp.int32).reshape(1)

    small_w = _pack([b_gate.reshape(-1), conv_w.reshape(-1)])
    (small_w_all,) = _exchange([small_w], [jax.ShapeDtypeStruct((NDEV,) + small_w.shape, F32)],
                               _whole, _slot, "gather_small_weights")
    nbg, ncw = DEPTH * 3 * 128, DEPTH * 3 * 1024
    flat_all = small_w_all.reshape(NDEV, -1)
    b_gate_full = jnp.transpose(flat_all[:, :nbg].reshape(NDEV, DEPTH, 3, 128), (1, 2, 0, 3)).reshape(DEPTH, 3, D)
    conv_w_full = jnp.transpose(flat_all[:, nbg:nbg + ncw].reshape(NDEV, DEPTH, 3, 1024), (1, 2, 0, 3)).reshape(DEPTH, 3, 2 * D_FF)

    groups = [tuple(l * NBIG + t for t in tids) for l in range(DEPTH) for _, tids in LAYER_GROUPS]
    cast = lambda i, m=me_arr: _cast_own(i, PB[BIG[i % NBIG][0]], m, f"gather_own_l{i // NBIG}_{BIG[i % NBIG][0]}")
    first = list(groups[0])
    rest = [i for grp in groups[1:] for i in grp]
    sems0, _, lands0, tok_first = _xchg_start(None, [cast(i) for i in first], [tuple(range(len(first)))], None,
                                              _shard_window, small_w_all, "gather_start_first", rels=NEAR_RELS, tids=first)
    where_rest = {tid: k for k, tid in enumerate(rest)}
    me_rest = me_arr + tok_first[0, 0:1].astype(jnp.int32)
    sems1, _, lands1, g_tok = _xchg_start(None, [cast(i, me_rest) for i in rest],
                                          [tuple(where_rest[i] for i in grp) for grp in groups[1:]], None,
                                          _shard_window, lands0[0], "gather_start_rest", rels=NEAR_RELS, tids=rest)
    g_sems = list(sems0) + list(sems1)
    tok0 = g_tok[0:1, 0:1]
    lands_now = [None] * (DEPTH * NBIG)
    for i, a in zip(first + rest, list(lands0) + list(lands1)):
        lands_now[i] = a
    fwd_sems = {}
    fwd_plan = {0: (0,), 1: (1,), 2: (2,), 3: (3, 4, 5)}

    def gather_waiter(gi, l, gname, tids):
        def wait(after):
            if gi in fwd_plan:
                gis = fwd_plan[gi]
                flat = [i for g2 in gis for i in groups[g2]]
                where = {tid: k for k, tid in enumerate(flat)}
                fs, new_lands, ftok = _gather_forward(
                    [g_sems[g2] for g2 in gis], [lands_now[i] for i in flat],
                    [[where[i] for i in groups[g2]] for g2 in gis], flat, after, _shard_window, f"gather_forward_{gi}")
                for g2, s in zip(gis, fs):
                    fwd_sems[g2] = s
                for i, a in zip(flat, new_lands):
                    lands_now[i] = a
                after = ftok
            ids = [l * NBIG + t for t in tids]
            _, got = _xchg_wait(fwd_sems[gi], None, [lands_now[i] for i in ids], ids, after,
                                None, _shard_window, f"gather_wait_l{l}_{gname}", rels=FAR_RELS)
            out = {}
            for t, arr in zip(tids, got):
                nm = BIG[t][0]
                out[nm] = arr
            return out
        return wait

    pending = [{gname: gather_waiter(l * len(LAYER_GROUPS) + k, l, gname, tids)
                for k, (gname, tids) in enumerate(LAYER_GROUPS)} for l in range(DEPTH)]
    ws = []
    for l in range(DEPTH):
        ws.append(_Weights(dict(
            b_gate=b_gate_full[l], conv_w=conv_w_full[l].reshape(3, 2, D_FF), conv_b=conv_b[l].reshape(2, D_FF),
            sinks=sinks[l].reshape(1, 8),
            attn_pre_norm=attn_pre_norm[l].reshape(1, D), attn_post_norm=attn_post_norm[l].reshape(1, D),
            ffn_pre_norm=ffn_pre_norm[l].reshape(1, D), ffn_post_norm=ffn_post_norm[l].reshape(1, D)), pending[l]))

    rs = {}

    group_tids = dict(LAYER_GROUPS)

    def start_scatter(l, gname, grads_l):
        tids = group_tids[gname]
        blocks, lands_rs = [], []
        for t in tids:
            nm, ax, ext = BIG[t]
            gfull = grads_l[nm].astype(BF16)
            shp = (NDEV, ext, gfull.shape[1]) if ax == 0 else (NDEV, gfull.shape[0], ext)
            blocks.append(gfull)
            lands_rs.append(lax.empty(shp, BF16))
        local = list(range(len(tids)))
        win = lambda j, ref, k: _shard_window(tids[j], ref, k)
        sems, s_thru, l_thru, tok = _xchg_start(blocks, lands_rs, [tuple(local)], win, _slot, me_arr,
                                                f"scatter_start_l{l}_{gname}")
        rs[(l, gname)] = (sems[0], s_thru, l_thru, win, local)
        return tok[0:1, 0:1]

    loss_tile, grad_x, grads, g_rel = _local_step(x[0], loss_target[0], ws, rel_bias, tok0, start_scatter)

    stack = lambda nm: jnp.stack([grads[l][nm] for l in range(DEPTH)], axis=0)
    small_g = {nm: (g_rel if nm == "rel_bias" else stack(nm)) for nm, _, _, _ in ROWPACK}
    small_repl = jnp.concatenate([_rowpack(small_g, ROWPACK[:N_REPL]), loss_tile], axis=0)
    small_shard = _shard_rows(small_g)

    out_g, out_d, out_m, out_v = {}, {}, {}, {}
    prev = {nm: None for nm, _, _ in BIG}
    todo = [(l, gname) for l in reversed(range(DEPTH)) for gname in ("ffn", "mix", "in")]
    after, small_parts = grad_x, None
    for l, gname in todo:
        if (l, gname) == todo[-1]:
            small_parts = _exchange(
                [small_repl, small_shard],
                [jax.ShapeDtypeStruct((NDEV, ROWS_REPL + 8, LANES), F32), jax.ShapeDtypeStruct((NDEV, ROWS_SHARD, LANES), F32)],
                lambda t, ref, q: ref if t == 0 else ref.at[q], _slot, "exchange_small_grads", after=after)
            after = small_parts[0]
        sems, s_thru, l_thru, win, local = rs[(l, gname)]
        owns, parts = _xchg_wait(sems, s_thru, l_thru, local, after, win, _slot, f"scatter_wait_l{l}_{gname}")
        for t, own, prt in zip(group_tids[gname], owns, parts):
            nm = BIG[t][0]
            rows = SHARD_ROWS.get(nm, PB[nm].shape[1])
            prev[nm] = _adamw(t, prt, own, me_arr, PB[nm], MB[nm], VB[nm], l, prev[nm], rows, f"adamw_{nm}_l{l}")
            after = prev[nm][1]
    for nm, _, _ in BIG:
        out_g[nm], out_d[nm], out_m[nm], out_v[nm] = [tr(a) if nm == "w_in" else a for a in prev[nm]]
    sm_g, sm_d, sm_m, sm_v, loss_all = _small_update(small_parts[0], small_parts[1], _rowpack(P), _rowpack(M),
                                                     _rowpack(V), "small_update")
    loss = loss_all[0, 0]
    for dst, src in ((out_g, sm_g), (out_d, sm_d), (out_m, sm_m), (out_v, sm_v)):
        dst.update(src)

    order = ["rel_bias", "attn_pre_norm", "w_in", "b_gate", "sinks", "w_br_a", "w_br_b", "w_br_c", "w_out",
             "attn_post_norm", "ffn_pre_norm", "w_up", "conv_w", "conv_b", "w_down", "ffn_post_norm"]
    return (loss, grad_x[None], *[out_g[k] for k in order], *[out_d[k] for k in order],
            *[out_m[k] for k in order], *[out_v[k] for k in order])
```

```python
import functools
import math

import numpy as np
import jax
import jax.numpy as jnp
from jax import lax
from jax.experimental import pallas as pl
from jax.experimental.pallas import tpu as pltpu

F32 = jnp.float32
BF16 = jnp.bfloat16

S = 2048
D = 1024
DEPTH = 2
NDEV = 8
HD = 64
BLK = 128
NB = S // BLK
A_GROUPS = ((128, 1), (512, 4), (2048, 16))
NUM_BUCKETS = 32
MAX_DISTANCE = 2048
N_BIAS_HEADS = 20
D_FF = 4096
IN_COLS = 6912
QKV_COLS = 3840
QKV_SLABS = QKV_COLS // 128
GATE_COLS = 3072
EPS = 1e-6
SCALE = HD ** -0.5
NEG = -1e30
LANES = 128

ADAM_LR = 0.001
ADAM_B1 = 0.9
ADAM_B2 = 0.999
ADAM_EPS = 1e-08
ADAM_WD = 0.01
ADAM_STEP = 10

VMEM_LIMIT = 56 * 1024 * 1024
MESH = pl.DeviceIdType.MESH
ANY = pl.BlockSpec(memory_space=pl.ANY)
SMEM = pl.BlockSpec(memory_space=pltpu.SMEM)


def _cp(*sem):
    return pltpu.CompilerParams(dimension_semantics=sem if sem else None, vmem_limit_bytes=VMEM_LIMIT)


def _dot(a, b, ca, cb):
    return lax.dot_general(a, b, (((ca,), (cb,)), ((), ())), preferred_element_type=F32)


def _mm(a, b, *, grid, a_spec, b_spec, out_shape, out_spec, ca, cb, acc_shape, name,
        a_slab=False, b_slab=False, out_slab=False, alias_out=None, after=None):
    nk = grid[2]

    def body(*refs):
        a_ref, b_ref = refs[0], refs[1]
        o_ref, acc_ref = refs[-2], refs[-1]
        k = pl.program_id(2)

        def load(ref, slab):
            if slab:
                return jnp.concatenate([ref[s] for s in range(ref.shape[0])], axis=1).astype(BF16)
            return ref[...].astype(BF16)

        def write(val):
            if out_slab:
                for s in range(o_ref.shape[0]):
                    o_ref[s] = val[:, s * LANES:(s + 1) * LANES].astype(o_ref.dtype)
            else:
                o_ref[...] = val.astype(o_ref.dtype)

        d = _dot(load(a_ref, a_slab), load(b_ref, b_slab), ca, cb)
        if nk == 1:
            write(d)
        elif direct:
            @pl.when(k == 0)
            def _():
                o_ref[...] = d

            @pl.when(k > 0)
            def _():
                o_ref[...] += d
        else:
            @pl.when(k == 0)
            def _():
                acc_ref[...] = d

            if nk > 2:
                @pl.when((k > 0) & (k < nk - 1))
                def _():
                    acc_ref[...] += d

            @pl.when(k == nk - 1)
            def _():
                write(acc_ref[...] + d)

    direct = (not out_slab) and out_shape.dtype == F32
    if nk == 1 or direct:
        acc_shape = (8, LANES)
    in_specs = [a_spec, b_spec]
    args = [a, b]
    aliases = {}
    if alias_out is not None:
        in_specs.append(ANY)
        args.append(alias_out)
        aliases = {2: 0}
    if after is not None:
        in_specs.append(ANY)
        args.append(after)
    return pl.pallas_call(
        body, grid=grid, in_specs=in_specs, out_specs=out_spec, out_shape=out_shape,
        scratch_shapes=[pltpu.VMEM(acc_shape, F32)], input_output_aliases=aliases,
        compiler_params=_cp("parallel", "parallel", "arbitrary"), name=name)(*args)


def _mm_nn(a, b, out_dtype, tm, tn, tk, name):
    m, kk = a.shape
    n = b.shape[1]
    return _mm(a, b, grid=(m // tm, n // tn, kk // tk),
               a_spec=pl.BlockSpec((tm, tk), lambda i, j, k: (i, k)),
               b_spec=pl.BlockSpec((tk, tn), lambda i, j, k: (k, j)),
               out_shape=jax.ShapeDtypeStruct((m, n), out_dtype),
               out_spec=pl.BlockSpec((tm, tn), lambda i, j, k: (i, j)),
               ca=1, cb=0, acc_shape=(tm, tn), name=name)


def _mm_nt(a, b, out_dtype, tm, tn, tk, name):
    m, kk = a.shape
    n = b.shape[0]
    return _mm(a, b, grid=(m // tm, n // tn, kk // tk),
               a_spec=pl.BlockSpec((tm, tk), lambda i, j, k: (i, k)),
               b_spec=pl.BlockSpec((tn, tk), lambda i, j, k: (j, k)),
               out_shape=jax.ShapeDtypeStruct((m, n), out_dtype),
               out_spec=pl.BlockSpec((tm, tn), lambda i, j, k: (i, j)),
               ca=1, cb=1, acc_shape=(tm, tn), name=name)


def _mm_tn(a, b, out_dtype, tm, tn, tk, name):
    kk, m = a.shape
    n = b.shape[1]
    return _mm(a, b, grid=(m // tm, n // tn, kk // tk),
               a_spec=pl.BlockSpec((tk, tm), lambda i, j, k: (k, i)),
               b_spec=pl.BlockSpec((tk, tn), lambda i, j, k: (k, j)),
               out_shape=jax.ShapeDtypeStruct((m, n), out_dtype),
               out_spec=pl.BlockSpec((tm, tn), lambda i, j, k: (i, j)),
               ca=0, cb=0, acc_shape=(tm, tn), name=name)


ROW_TILE = 512
MERGE_TILE = 256


def _rms(x, g):
    r = lax.rsqrt(jnp.mean(x * x, axis=-1, keepdims=True) + EPS)
    return x * r * g


def _prenorm(x, g, name):
    def body(x_ref, g_ref, o_ref):
        o_ref[...] = _rms(x_ref[...], g_ref[...]).astype(BF16)

    return pl.pallas_call(
        body, grid=(S // ROW_TILE,),
        in_specs=[pl.BlockSpec((ROW_TILE, D), lambda i: (i, 0)), pl.BlockSpec((1, D), lambda i: (0, 0))],
        out_specs=pl.BlockSpec((ROW_TILE, D), lambda i: (i, 0)),
        out_shape=jax.ShapeDtypeStruct((S, D), BF16), compiler_params=_cp("parallel"), name=name)(x, g)


def _postnorm_res(x, f, g_post, g_next, name):
    def body(x_ref, f_ref, gp_ref, gn_ref, xo_ref, ho_ref):
        xn = x_ref[...] + _rms(f_ref[...], gp_ref[...])
        xo_ref[...] = xn
        ho_ref[...] = _rms(xn, gn_ref[...]).astype(BF16)

    row = pl.BlockSpec((ROW_TILE, D), lambda i: (i, 0))
    vec = pl.BlockSpec((1, D), lambda i: (0, 0))
    return pl.pallas_call(
        body, grid=(S // ROW_TILE,), in_specs=[row, row, vec, vec], out_specs=[row, row],
        out_shape=[jax.ShapeDtypeStruct((S, D), F32), jax.ShapeDtypeStruct((S, D), BF16)],
        compiler_params=_cp("parallel"), name=name)(x, f, g_post, g_next)


def _norm_bwd(f, g, dys, res, out_dtype, name):
    ndy = len(dys)
    has_res = res is not None

    def body(*refs):
        f_ref, g_ref = refs[0], refs[1]
        dy_refs = refs[2:2 + ndy]
        res_ref = refs[2 + ndy] if has_res else None
        o_ref, dg_ref = refs[-2], refs[-1]
        fv = f_ref[...]
        dy = dy_refs[0][...].astype(F32)
        for r in dy_refs[1:]:
            dy = dy + r[...].astype(F32)
        r = lax.rsqrt(jnp.mean(fv * fv, axis=-1, keepdims=True) + EPS)
        n = fv * r
        dn = dy * g_ref[...]
        df = r * (dn - n * jnp.mean(dn * n, axis=-1, keepdims=True))
        if has_res:
            df = df + res_ref[...]
        o_ref[...] = df.astype(out_dtype)

        @pl.when(pl.program_id(0) == 0)
        def _():
            dg_ref[...] = jnp.zeros((1, D), F32)

        dg_ref[...] += jnp.sum(dy * n, axis=0, keepdims=True)

    row = pl.BlockSpec((ROW_TILE, D), lambda i: (i, 0))
    vec = pl.BlockSpec((1, D), lambda i: (0, 0))
    in_specs = [row, vec] + [row] * ndy + ([row] if has_res else [])
    args = [f, g] + list(dys) + ([res] if has_res else [])
    return pl.pallas_call(
        body, grid=(S // ROW_TILE,), in_specs=in_specs, out_specs=[row, vec],
        out_shape=[jax.ShapeDtypeStruct((S, D), out_dtype), jax.ShapeDtypeStruct((1, D), F32)],
        compiler_params=_cp("arbitrary"), name=name)(*args)


def _rms_bwd_rows(fv, g, dy):
    r = lax.rsqrt(jnp.mean(fv * fv, axis=-1, keepdims=True) + EPS)
    n = fv * r
    dn = dy * g
    return r * (dn - n * jnp.mean(dn * n, axis=-1, keepdims=True)), dy * n


def _norm_bwd_chain(f1, g1, dys, res, f2, g2, name):
    ndy = len(dys)

    def body(*refs):
        f1_ref, g1_ref = refs[0], refs[1]
        dy_refs = refs[2:2 + ndy]
        res_ref, f2_ref, g2_ref = refs[2 + ndy:5 + ndy]
        o1_ref, o2_ref, dg1_ref, dg2_ref = refs[-4:]
        dy = dy_refs[0][...].astype(F32)
        for r in dy_refs[1:]:
            dy = dy + r[...].astype(F32)
        df1, c1 = _rms_bwd_rows(f1_ref[...], g1_ref[...], dy)
        out1 = df1 + res_ref[...]
        o1_ref[...] = out1
        df2, c2 = _rms_bwd_rows(f2_ref[...], g2_ref[...], out1)
        o2_ref[...] = df2.astype(BF16)

        @pl.when(pl.program_id(0) == 0)
        def _():
            dg1_ref[...] = jnp.zeros((1, D), F32)
            dg2_ref[...] = jnp.zeros((1, D), F32)

        dg1_ref[...] += jnp.sum(c1, axis=0, keepdims=True)
        dg2_ref[...] += jnp.sum(c2, axis=0, keepdims=True)

    row = pl.BlockSpec((ROW_TILE, D), lambda i: (i, 0))
    vec = pl.BlockSpec((1, D), lambda i: (0, 0))
    return pl.pallas_call(
        body, grid=(S // ROW_TILE,), in_specs=[row, vec] + [row] * ndy + [row, row, vec],
        out_specs=[row, row, vec, vec],
        out_shape=[jax.ShapeDtypeStruct((S, D), F32), jax.ShapeDtypeStruct((S, D), BF16),
                   jax.ShapeDtypeStruct((1, D), F32), jax.ShapeDtypeStruct((1, D), F32)],
        compiler_params=_cp("arbitrary"), name=name)(f1, g1, *dys, res, f2, g2)


def _loss_head(x, f, g, target, name):
    def body(x_ref, f_ref, g_ref, t_ref, dy_ref, l_ref, df_ref, dg_ref):
        fv = f_ref[...]
        e = x_ref[...] + _rms(fv, g_ref[...]) - t_ref[...]
        dy = e * (1.0 / D)
        dy_ref[...] = dy
        df, c = _rms_bwd_rows(fv, g_ref[...], dy)
        df_ref[...] = df.astype(BF16)

        @pl.when(pl.program_id(0) == 0)
        def _():
            l_ref[...] = jnp.zeros((8, LANES), F32)
            dg_ref[...] = jnp.zeros((1, D), F32)

        l_ref[...] += jnp.sum(e * e) * (0.5 / D)
        dg_ref[...] += jnp.sum(c, axis=0, keepdims=True)

    row = pl.BlockSpec((ROW_TILE, D), lambda i: (i, 0))
    vec = pl.BlockSpec((1, D), lambda i: (0, 0))
    return pl.pallas_call(
        body, grid=(S // ROW_TILE,), in_specs=[row, row, vec, row],
        out_specs=[row, pl.BlockSpec((8, LANES), lambda i: (0, 0)), row, vec],
        out_shape=[jax.ShapeDtypeStruct((S, D), F32), jax.ShapeDtypeStruct((8, LANES), F32),
                   jax.ShapeDtypeStruct((S, D), BF16), jax.ShapeDtypeStruct((1, D), F32)],
        compiler_params=_cp("arbitrary"), name=name)(x, f, g, target)


def _bucket_tiles():
    a = np.arange(BLK)[:, None]
    b = np.arange(2 * BLK)[None, :]
    dist = a + BLK - b
    out = np.zeros((4, 2, BLK, 2 * BLK), np.int32)
    cfg = [(w // d, d) for w, d in A_GROUPS] + [(BLK - 1, 1)]
    for gi, (max_dist, d) in enumerate(cfg):
        band = (dist >= 0) & (dist <= max_dist)
        tok = np.maximum(dist, 0) * d
        nf = np.maximum(tok, 1).astype(np.float32)
        max_exact = NUM_BUCKETS // 2
        large = max_exact + (np.log(nf / np.float32(max_exact)) / np.float32(math.log(MAX_DISTANCE / max_exact))
                             * np.float32(NUM_BUCKETS - max_exact)).astype(np.int32)
        large = np.minimum(large, NUM_BUCKETS - 1)
        bkt = np.where(tok < max_exact, tok, large).astype(np.int32)
        full = np.where(band, bkt, -1)
        out[gi, 1] = full
        out[gi, 0] = np.where(b >= BLK, full, -1)
    return out


def _bias_tiles(rel_bias, buckets, name):
    hps = 4

    def body(tab_ref, bkt_ref, o_ref):
        s = pl.program_id(0)
        bkt = bkt_ref[...]
        masks = [bkt == bb for bb in range(NUM_BUCKETS)]
        for hh in range(hps):
            acc = jnp.zeros(bkt.shape, F32)
            for bb in range(NUM_BUCKETS):
                acc = jnp.where(masks[bb], tab_ref[bb, hps * s + hh], acc)
            o_ref[hh] = jnp.where(bkt < 0, NEG, acc)

    return pl.pallas_call(
        body, grid=(N_BIAS_HEADS // hps,),
        in_specs=[SMEM, pl.BlockSpec((None, 2, BLK, 2 * BLK), lambda s: (jnp.minimum(s, 3), 0, 0, 0))],
        out_specs=pl.BlockSpec((hps, 2, BLK, 2 * BLK), lambda s: (s, 0, 0, 0)),
        out_shape=jax.ShapeDtypeStruct((N_BIAS_HEADS, 2, BLK, 2 * BLK), F32),
        compiler_params=_cp("arbitrary"), name=name)(rel_bias, buckets)


def _bias_grad(gs, buckets, name):
    ng = len(gs)

    def body(*refs):
        g_refs = refs[:ng]
        bkt_ref, o_ref = refs[ng], refs[ng + 1]
        h = pl.program_id(0)
        g = g_refs[0][...]
        for r in g_refs[1:]:
            g = g + r[...]
        bkt = bkt_ref[...]
        row = lax.broadcasted_iota(jnp.int32, (NUM_BUCKETS, LANES), 0)
        lane = lax.broadcasted_iota(jnp.int32, (NUM_BUCKETS, LANES), 1)

        @pl.when(h == 0)
        def _():
            o_ref[...] = jnp.zeros((NUM_BUCKETS, LANES), F32)

        acc = o_ref[...]
        for bb in range(NUM_BUCKETS):
            s = jnp.sum(jnp.where(bkt == bb, g, 0.0))
            acc = jnp.where((row == bb) & (lane == h), s, acc)
        o_ref[...] = acc

    g_spec = pl.BlockSpec((None, BLK, 2 * BLK), lambda h: (h, 0, 0))
    return pl.pallas_call(
        body, grid=(N_BIAS_HEADS,),
        in_specs=[g_spec] * ng + [pl.BlockSpec((None, None, BLK, 2 * BLK), lambda h: (jnp.minimum(h // 4, 3), 1, 0, 0))],
        out_specs=pl.BlockSpec((NUM_BUCKETS, LANES), lambda h: (0, 0)),
        out_shape=jax.ShapeDtypeStruct((NUM_BUCKETS, LANES), F32),
        compiler_params=_cp("arbitrary"), name=name)(*gs, buckets)


def _to_class_major(src_ref, dst_refs, d, fn=None):
    ln = S // d
    for r in range(d):
        v = src_ref[pl.ds(r, ln, stride=d), :] if d > 1 else src_ref[...]
        outs = fn(v) if fn is not None else (v,) * len(dst_refs)
        for dst, o in zip(dst_refs, outs):
            dst[pl.ds(r * ln, ln), :] = o.astype(dst.dtype)


def _head_masks(rows):
    lane = lax.broadcasted_iota(jnp.int32, (rows, LANES), 1)
    return lane < HD, lane >= HD


def _split_heads(v):
    m0, m1 = _head_masks(v.shape[0])
    return jnp.where(m0, v, 0.0), jnp.where(m1, v, 0.0)


def _dup_head(v, hi):
    m0, _ = _head_masks(v.shape[0])
    r = pltpu.roll(v, HD, 1)
    return jnp.where(m0, jnp.where(hi, r, v), jnp.where(hi, v, r))


def _block_rows(b, d):
    nbc = NB // d
    i = b % nbc
    r = b // nbc
    has_prev = (i > 0).astype(jnp.int32)
    prev = pl.multiple_of(jnp.maximum(b - 1, 0) * BLK, BLK)
    nat = i * (BLK * d) + r
    return has_prev, prev, nat


def _lane_halves(v0, v1):
    lane = lax.broadcasted_iota(jnp.int32, (v0.shape[0], LANES), 1)
    return jnp.where(lane < HD, v0, v1)


def _band_fwd(proj, bias, sinks, *, d, q0, k0, v0, npairs, bias0, shared_kv, name):
    def body(sink_ref, q_ref, k_ref, v_ref, b_ref, num_ref, st_ref, qz0, qz1, ks, vs):
        p = pl.program_id(0)
        kv = (lambda v: (_dup_head(v, p >= 2),)) if shared_kv else None
        _to_class_major(q_ref, (qz0, qz1), d, lambda v: _split_heads(v * SCALE))
        _to_class_major(k_ref, (ks,), d, kv)
        _to_class_major(v_ref, (vs,), d, kv)
        lane = lax.broadcasted_iota(jnp.int32, (BLK, LANES), 1)

        def blk(b, carry):
            has_prev, prev, nat = _block_rows(b, d)
            cur = pl.multiple_of(b * BLK, BLK)
            k2 = jnp.concatenate([ks[pl.ds(prev, BLK), :], ks[pl.ds(cur, BLK), :]], axis=0)
            v2 = jnp.concatenate([vs[pl.ds(prev, BLK), :], vs[pl.ds(cur, BLK), :]], axis=0)
            nums, ms, ls = [], [], []
            for hh, qz in enumerate((qz0, qz1)):
                z = _dot(qz[pl.ds(cur, BLK), :], k2, 1, 1) + b_ref[hh, has_prev]
                m = jnp.max(z, axis=1, keepdims=True)
                e = jnp.exp(z - m)
                l = jnp.sum(e, axis=1, keepdims=True)
                num = _dot(e.astype(BF16), v2, 1, 0)
                if shared_kv:
                    sink = sink_ref[0, 2 * p + hh]
                    mx = jnp.maximum(m, sink)
                    c = jnp.exp(m - mx)
                    zden = l * c + jnp.exp(sink - mx)
                    num = num * (c / zden)
                    m = mx + jnp.log(zden)
                ls.append(l)
                ms.append(m)
                nums.append(num)
            num_t = jnp.where(lane < HD, nums[0], nums[1])
            if shared_kv:
                st_t = jnp.where(lane < HD, ms[0], ms[1])
            else:
                st_t = jnp.where(lane < 32, ms[0], jnp.where(lane < 64, ls[0], jnp.where(lane < 96, ms[1], ls[1])))
            if d > 1:
                num_ref[pl.ds(nat, BLK, stride=d), :] = num_t
                st_ref[pl.ds(nat, BLK, stride=d), :] = st_t
            else:
                num_ref[pl.ds(cur, BLK), :] = num_t
                st_ref[pl.ds(cur, BLK), :] = st_t
            return carry

        lax.fori_loop(0, NB, blk, 0, unroll=True)

    slab = lambda off, per_pair: pl.BlockSpec((None, S, LANES), (lambda p: (off + p, 0, 0)) if per_pair else (lambda p: (off, 0, 0)))
    out = pl.BlockSpec((None, S, LANES), lambda p: (p, 0, 0))
    return pl.pallas_call(
        body, grid=(npairs,),
        in_specs=[SMEM, slab(q0, True), slab(k0, not shared_kv), slab(v0, not shared_kv),
                  pl.BlockSpec((None, 2, 2, BLK, 2 * BLK), lambda p: (bias0 + p, 0, 0, 0, 0))],
        out_specs=[out, out],
        out_shape=[jax.ShapeDtypeStruct((npairs, S, LANES), F32)] * 2,
        scratch_shapes=[pltpu.VMEM((S, LANES), BF16)] * 4,
        compiler_params=_cp("arbitrary"), name=name)(sinks, proj, proj, proj, bias)


def _combine_a(nums, stats, name):
    rt = 512

    def body(n0, n1, n2, s0, s1, s2, o_ref, l_ref):
        n_refs, s_refs = (n0, n1, n2), (s0, s1, s2)
        outs, lses = [], []
        for hh in range(2):
            ms = [s[:, 64 * hh:64 * hh + 1] for s in s_refs]
            ls = [s[:, 64 * hh + 32:64 * hh + 33] for s in s_refs]
            mx = jnp.maximum(jnp.maximum(ms[0], ms[1]), ms[2])
            cs = [jnp.exp(m - mx) for m in ms]
            z = cs[0] * ls[0] + cs[1] * ls[1] + cs[2] * ls[2]
            acc = cs[0] * n_refs[0][:, hh * HD:(hh + 1) * HD]
            acc = acc + cs[1] * n_refs[1][:, hh * HD:(hh + 1) * HD]
            acc = acc + cs[2] * n_refs[2][:, hh * HD:(hh + 1) * HD]
            outs.append(acc / z)
            lses.append(mx + jnp.log(z))
        o_ref[...] = jnp.concatenate(outs, axis=1)
        l_ref[...] = _lane_halves(lses[0], lses[1])

    spec = pl.BlockSpec((None, rt, LANES), lambda p, i: (p, i, 0))
    return pl.pallas_call(
        body, grid=(2, S // rt), in_specs=[spec] * 6, out_specs=[spec, spec],
        out_shape=[jax.ShapeDtypeStruct((2, S, LANES), F32)] * 2,
        compiler_params=_cp("parallel", "parallel"), name=name)(*nums, *stats)


def _band_bwd(proj, bias, o, do, lse, sinks, dqkv, *, d, q0, k0, v0, npairs, bias0, shared_kv, name):
    def body(sink_ref, q_ref, k_ref, v_ref, b_ref, o_ref, do_ref, lse_ref, dqkv_in, dqkv_ref, g_ref, ds_ref,
             qz0, qz1, ks, vs, doz0, doz1, ls0, ls1, dls0, dls1, stage, dq_nat, dk_cm, dv_cm, kv_nat, dk_acc, dv_acc,
             obuf, osem):
        p = pl.program_id(0)
        dq_ref, dk_ref, dv_ref = obuf.at[0], obuf.at[1], obuf.at[2]
        m0, m1 = _head_masks(S)
        kk = lax.broadcasted_iota(jnp.int32, (2 * LANES, LANES), 0) % LANES
        ll = lax.broadcasted_iota(jnp.int32, (2 * LANES, LANES), 1)
        hi, lo = _split2(do_ref[...] * o_ref[...])
        dl = _dot(jnp.concatenate([hi, lo], axis=1), ((kk < HD) == (ll < HD)).astype(BF16), 1, 0)
        if shared_kv:
            row8 = lax.broadcasted_iota(jnp.int32, (8, LANES), 0)
            lane8 = lax.broadcasted_iota(jnp.int32, (8, LANES), 1)
            sinkv = jnp.where(m0, sink_ref[0, 2 * p], sink_ref[0, 2 * p + 1])
            contrib = jnp.exp(sinkv - lse_ref[...]) * dl
            t = jnp.zeros((8, LANES), F32)
            for hh, mh in enumerate((m0, m1)):
                dsink = -jnp.sum(jnp.where(mh, contrib, 0.0)) * (1.0 / HD)
                t = jnp.where((row8 == 0) & (lane8 == hh), dsink, t)
            ds_ref[...] = t
        else:
            ds_ref[...] = jnp.zeros((8, LANES), F32)
        kv = (lambda v: (_dup_head(v, p >= 2),)) if shared_kv else None
        _to_class_major(q_ref, (qz0, qz1), d, lambda v: _split_heads(v * SCALE))
        _to_class_major(k_ref, (ks,), d, kv)
        _to_class_major(v_ref, (vs,), d, kv)
        _to_class_major(do_ref, (doz0, doz1), d, _split_heads)
        def spread(v):
            a0, a1 = _head_masks(v.shape[0])
            r = pltpu.roll(v, HD, 1)
            return jnp.where(a0, v, r), jnp.where(a1, v, r)

        _to_class_major(lse_ref, (ls0, ls1), d, spread)
        stage[...] = dl
        _to_class_major(stage, (dls0, dls1), d, spread)

        dk_cm[...] = jnp.zeros((S, LANES), F32)
        dv_cm[...] = jnp.zeros((S, LANES), F32)
        g_ref[...] = jnp.zeros((2, BLK, 2 * BLK), F32)
        lane = lax.broadcasted_iota(jnp.int32, (BLK, LANES), 1)

        def blk(b, carry):
            has_prev, prev, nat = _block_rows(b, d)
            cur = pl.multiple_of(b * BLK, BLK)
            k2 = jnp.concatenate([ks[pl.ds(prev, BLK), :], ks[pl.ds(cur, BLK), :]], axis=0)
            v2 = jnp.concatenate([vs[pl.ds(prev, BLK), :], vs[pl.ds(cur, BLK), :]], axis=0)
            dqs, dks, dvs = [], [], []
            for hh, (qz, doz, lsr, dlr) in enumerate(((qz0, doz0, ls0, dls0), (qz1, doz1, ls1, dls1))):
                qb = qz[pl.ds(cur, BLK), :]
                dob = doz[pl.ds(cur, BLK), :]
                lb = lsr[pl.ds(cur, BLK), :]
                dlb = dlr[pl.ds(cur, BLK), :]
                z = _dot(qb, k2, 1, 1) + b_ref[hh, has_prev]
                pr = jnp.exp(z - jnp.concatenate([lb, lb], axis=1))
                dp = _dot(dob, v2, 1, 1)
                dz = pr * (dp - jnp.concatenate([dlb, dlb], axis=1))
                g_ref[hh] += dz
                dzb = dz.astype(BF16)
                dqs.append(_dot(dzb, k2, 1, 0))
                dks.append(_dot(dzb, qb, 0, 0))
                dvs.append(_dot(pr.astype(BF16), dob, 0, 0))
            dq_t = jnp.where(lane < HD, dqs[0], dqs[1]) * SCALE
            dk_t = dks[0] + dks[1]
            dv_t = dvs[0] + dvs[1]
            dk_cm[pl.ds(prev, BLK), :] += dk_t[:BLK]
            dk_cm[pl.ds(cur, BLK), :] += dk_t[BLK:]
            dv_cm[pl.ds(prev, BLK), :] += dv_t[:BLK]
            dv_cm[pl.ds(cur, BLK), :] += dv_t[BLK:]
            if d > 1:
                dq_nat[pl.ds(nat, BLK, stride=d), :] = dq_t
            else:
                dq_nat[pl.ds(cur, BLK), :] = dq_t
            return carry

        lax.fori_loop(0, NB, blk, 0, unroll=True)
        dq_ref[...] = dq_nat[...].astype(BF16)

        def from_class_major(src, dst_ref):
            if d == 1:
                dst_ref[...] = src[...].astype(BF16)
            else:
                ln = S // d
                for r in range(d):
                    kv_nat[pl.ds(r, ln, stride=d), :] = src[pl.ds(r * ln, ln), :]
                dst_ref[...] = kv_nat[...].astype(BF16)

        def put(i, slab):
            return pltpu.make_async_copy(obuf.at[i], dqkv_ref.at[slab], osem.at[i])

        put(0, q0 + p).start()
        if not shared_kv:
            from_class_major(dk_cm, dk_ref)
            from_class_major(dv_cm, dv_ref)
            put(1, k0 + p).start()
            put(2, v0 + p).start()
            put(1, k0 + p).wait()
            put(2, v0 + p).wait()
        else:
            @pl.when(p == 0)
            def _():
                dk_acc[...] = jnp.zeros((S, LANES), F32)
                dv_acc[...] = jnp.zeros((S, LANES), F32)

            mine = m1 == (p >= 2)
            for cm, acc in ((dk_cm, dk_acc), (dv_cm, dv_acc)):
                val = cm[...]
                acc[...] += jnp.where(mine, val + pltpu.roll(val, HD, 1), 0.0)

            @pl.when(p == npairs - 1)
            def _():
                from_class_major(dk_acc, dk_ref)
                from_class_major(dv_acc, dv_ref)
                put(1, k0).start()
                put(2, v0).start()
                put(1, k0).wait()
                put(2, v0).wait()

        put(0, q0 + p).wait()

    slab = lambda off, per_pair: pl.BlockSpec((None, S, LANES), (lambda p: (off + p, 0, 0)) if per_pair else (lambda p: (off, 0, 0)))
    pair = pl.BlockSpec((None, S, LANES), lambda p: (p, 0, 0))
    return pl.pallas_call(
        body, grid=(npairs,),
        in_specs=[SMEM, slab(q0, True), slab(k0, not shared_kv), slab(v0, not shared_kv),
                  pl.BlockSpec((None, 2, 2, BLK, 2 * BLK), lambda p: (bias0 + p, 0, 0, 0, 0)),
                  pair, pair, pair, ANY],
        out_specs=[ANY,
                   pl.BlockSpec((None, 2, BLK, 2 * BLK), lambda p: (p, 0, 0, 0)),
                   pl.BlockSpec((None, 8, LANES), lambda p: (p, 0, 0))],
        out_shape=[jax.ShapeDtypeStruct(dqkv.shape, BF16),
                   jax.ShapeDtypeStruct((npairs, 2, BLK, 2 * BLK), F32),
                   jax.ShapeDtypeStruct((npairs, 8, LANES), F32)],
        scratch_shapes=[pltpu.VMEM((S, LANES), BF16)] * 6 + [pltpu.VMEM((S, LANES), F32)] * 11
        + [pltpu.VMEM((3, S, LANES), BF16), pltpu.SemaphoreType.DMA((3,))],
        input_output_aliases={8: 0},
        compiler_params=_cp("arbitrary"), name=name)(sinks, proj, proj, proj, bias, o, do, lse, dqkv)


KC = 512
NSUB = KC // BLK
QB = 512
QPG = KC // QB


def _split2(x):
    hi = x.astype(BF16)
    lo = (x - hi.astype(F32)).astype(BF16)
    return hi, lo


def _tri_ones(cmp):
    jj = lax.broadcasted_iota(jnp.int32, (2 * BLK, BLK), 0) % BLK
    ss = lax.broadcasted_iota(jnp.int32, (2 * BLK, BLK), 1)
    return jnp.concatenate([cmp(jj, ss).astype(BF16), jnp.ones((2 * BLK, BLK), BF16)], axis=1)


def _sub_sums(x, tri1):
    n = x.shape[0]
    st = jnp.concatenate([x[:, s * BLK:(s + 1) * BLK] for s in range(NSUB)], axis=0)
    hi, lo = _split2(st)
    r = _dot(jnp.concatenate([hi, lo], axis=1), tri1, 1, 0)
    return ([r[s * n:(s + 1) * n, :BLK] for s in range(NSUB)], [r[s * n:(s + 1) * n, BLK:] for s in range(NSUB)])


def _log_sig_pair(z):
    lb = jnp.minimum(z, 0.0) - jnp.log1p(jnp.exp(-jnp.abs(z)))
    return lb, lb - z


QGROUPS = NB // NSUB


def _stick_fwd(proj, *, q0, k0, v0, name):
    def body(q_ref, k_ref, v_ref, o_ref, t_ref, qs, ks, vs):
        qs[...] = (q_ref[...] * SCALE).astype(BF16)
        ks[...] = k_ref[...].astype(BF16)
        vs[...] = v_ref[...].astype(BF16)
        tri1 = _tri_ones(lambda j, s: j > s)
        col = lax.broadcasted_iota(jnp.int32, (QB, KC), 1)
        rowi = lax.broadcasted_iota(jnp.int32, (QB, KC), 0)

        for qg in range(QGROUPS):
            def qblock(ii, carry0, qg=qg):
                t0 = pl.multiple_of((qg * QPG + ii) * QB, QB)
                qb = qs[pl.ds(t0, QB), :]
                accs = [jnp.zeros((QB, HD), F32)] * 2
                runs = [jnp.zeros((QB, BLK), F32)] * 2
                for c in reversed(range(qg + 1)):
                    s0 = c * KC
                    diag = c == qg
                    before = (s0 + col) < (t0 + rowi) if diag else None
                    for hh in range(2):
                        kh = ks[s0:s0 + KC, hh * HD:(hh + 1) * HD]
                        vh = vs[s0:s0 + KC, hh * HD:(hh + 1) * HD]
                        lb, lk = _log_sig_pair(_dot(qb[:, hh * HD:(hh + 1) * HD], kh, 1, 1))
                        if diag:
                            lk = jnp.where(before, lk, 0.0)
                        suf, tot = _sub_sums(lk, tri1)
                        ws, run = [], runs[hh]
                        for s in reversed(range(NSUB)):
                            ws.append(jnp.exp(lb[:, s * BLK:(s + 1) * BLK] + suf[s] + run))
                            run = run + tot[s]
                        w = jnp.concatenate(ws[::-1], axis=1)
                        if diag:
                            w = jnp.where(before, w, 0.0)
                        accs[hh] = accs[hh] + _dot(w.astype(BF16), vh, 1, 0)
                        runs[hh] = run
                o_ref[pl.ds(t0, QB), :] = jnp.concatenate(accs, axis=1)
                t_ref[pl.ds(t0, QB), :] = _lane_halves(runs[0], runs[1])
                return carry0

            lax.fori_loop(0, QPG, qblock, 0)

    slab = lambda off: pl.BlockSpec((None, S, LANES), lambda p: (off + p, 0, 0))
    out = pl.BlockSpec((None, S, LANES), lambda p: (p, 0, 0))
    return pl.pallas_call(
        body, grid=(2,), in_specs=[slab(q0), slab(k0), slab(v0)], out_specs=[out, out],
        out_shape=[jax.ShapeDtypeStruct((2, S, LANES), F32)] * 2,
        scratch_shapes=[pltpu.VMEM((S, LANES), BF16)] * 3,
        compiler_params=_cp("arbitrary"), name=name)(proj, proj, proj)


def _stick_bwd(proj, do, tot, dqkv, *, q0, k0, v0, name):
    def body(q_ref, k_ref, v_ref, do_ref, t_ref, dqkv_in, dqkv_ref, qs, ks, vs, dos, dk_acc, dv_acc, obuf, osem):
        p = pl.program_id(0)
        dq_ref, dk_ref, dv_ref = obuf.at[0], obuf.at[1], obuf.at[2]
        qs[...] = (q_ref[...] * SCALE).astype(BF16)
        ks[...] = k_ref[...].astype(BF16)
        vs[...] = v_ref[...].astype(BF16)
        dos[...] = do_ref[...].astype(BF16)
        dk_acc[...] = jnp.zeros((2, S, HD), F32)
        dv_acc[...] = jnp.zeros((2, S, HD), F32)
        tri_inc = _tri_ones(lambda j, s: j <= s)
        tri_exc = _tri_ones(lambda j, s: j < s)
        col = lax.broadcasted_iota(jnp.int32, (QB, KC), 1)
        rowi = lax.broadcasted_iota(jnp.int32, (QB, KC), 0)

        for qg in range(QGROUPS):
            def qblock(ii, carry0, qg=qg):
                t0 = pl.multiple_of((qg * QPG + ii) * QB, QB)
                qb = qs[pl.ds(t0, QB), :]
                dob = dos[pl.ds(t0, QB), :]
                tb = t_ref[pl.ds(t0, QB), :]
                dqs = [jnp.zeros((QB, HD), F32)] * 2
                pruns = [jnp.zeros((QB, BLK), F32)] * 2
                eruns = [jnp.zeros((QB, BLK), F32)] * 2
                for c in range(qg + 1):
                    s0 = c * KC
                    diag = c == qg
                    before = (s0 + col) < (t0 + rowi) if diag else None
                    for hh in range(2):
                        qh = qb[:, hh * HD:(hh + 1) * HD]
                        doh = dob[:, hh * HD:(hh + 1) * HD]
                        tt = tb[:, 64 * hh:64 * hh + 1]
                        kh = ks[s0:s0 + KC, hh * HD:(hh + 1) * HD]
                        vh = vs[s0:s0 + KC, hh * HD:(hh + 1) * HD]
                        lb, lk = _log_sig_pair(_dot(qh, kh, 1, 1))
                        if diag:
                            lk = jnp.where(before, lk, 0.0)
                        pin, ptot = _sub_sums(lk, tri_inc)
                        ws, prun = [], pruns[hh]
                        for s in range(NSUB):
                            ws.append(jnp.exp(lb[:, s * BLK:(s + 1) * BLK] + (tt - (pin[s] + prun))))
                            prun = prun + ptot[s]
                        w = jnp.concatenate(ws, axis=1)
                        if diag:
                            w = jnp.where(before, w, 0.0)
                        e = w * _dot(doh, vh, 1, 1)
                        pex, etot = _sub_sums(e, tri_exc)
                        cs, erun = [], eruns[hh]
                        for s in range(NSUB):
                            cs.append(pex[s] + erun)
                            erun = erun + etot[s]
                        sig = jnp.exp(lb)
                        dz = e * (1.0 - sig) - jnp.concatenate(cs, axis=1) * sig
                        if diag:
                            dz = jnp.where(before, dz, 0.0)
                        dz = dz.astype(BF16)
                        dqs[hh] = dqs[hh] + _dot(dz, kh, 1, 0)
                        dk_acc[hh, s0:s0 + KC, :] += _dot(dz, qh, 0, 0)
                        dv_acc[hh, s0:s0 + KC, :] += _dot(w.astype(BF16), doh, 0, 0)
                        pruns[hh], eruns[hh] = prun, erun
                dq_ref[pl.ds(t0, QB), :] = (jnp.concatenate(dqs, axis=1) * SCALE).astype(BF16)
                return carry0

            lax.fori_loop(0, QPG, qblock, 0)
        dk_ref[...] = jnp.concatenate([dk_acc[0], dk_acc[1]], axis=1).astype(BF16)
        dv_ref[...] = jnp.concatenate([dv_acc[0], dv_acc[1]], axis=1).astype(BF16)
        puts = [pltpu.make_async_copy(obuf.at[i], dqkv_ref.at[off + p], osem.at[i]) for i, off in enumerate((q0, k0, v0))]
        for cp in puts:
            cp.start()
        for cp in puts:
            cp.wait()

    slab = lambda off: pl.BlockSpec((None, S, LANES), lambda p: (off + p, 0, 0))
    pair = pl.BlockSpec((None, S, LANES), lambda p: (p, 0, 0))
    return pl.pallas_call(
        body, grid=(2,), in_specs=[slab(q0), slab(k0), slab(v0), pair, pair, ANY], out_specs=ANY,
        out_shape=jax.ShapeDtypeStruct(dqkv.shape, BF16),
        scratch_shapes=[pltpu.VMEM((S, LANES), BF16)] * 4 + [pltpu.VMEM((2, S, HD), F32)] * 2
        + [pltpu.VMEM((3, S, LANES), BF16), pltpu.SemaphoreType.DMA((3,))],
        input_output_aliases={5: 0},
        compiler_params=_cp("arbitrary"), name=name)(proj, proj, proj, do, tot, dqkv)


def _cat_slabs(ref):
    return jnp.concatenate([ref[s] for s in range(ref.shape[0])], axis=1)


def _merge_fwd(o_a, o_b, o_c, gates, b_gate, wa, wb, wc, w_out, name):
    tm = MERGE_TILE

    def body(oa_ref, ob_ref, oc_ref, g_ref, bg_ref, wa_ref, wb_ref, wc_ref, wo_ref, mg_ref, mo_ref):
        acc = jnp.zeros((tm, D), F32)
        for i, (o_ref, w_ref) in enumerate(((oa_ref, wa_ref), (ob_ref, wb_ref), (oc_ref, wc_ref))):
            pr = _dot(_cat_slabs(o_ref).astype(BF16), w_ref[...], 1, 0)
            sg = jax.nn.sigmoid(g_ref[:, i * D:(i + 1) * D] + bg_ref[i:i + 1, :])
            acc = acc + sg * pr
        mg = acc.astype(BF16)
        mg_ref[...] = mg
        mo_ref[...] = _dot(mg, wo_ref[...], 1, 0)

    slabs = lambda n: pl.BlockSpec((n, tm, LANES), lambda i: (0, i, 0))
    full = lambda r, c: pl.BlockSpec((r, c), lambda i: (0, 0))
    row = pl.BlockSpec((tm, D), lambda i: (i, 0))
    return pl.pallas_call(
        body, grid=(S // tm,),
        in_specs=[slabs(2), slabs(4), slabs(2), pl.BlockSpec((tm, GATE_COLS), lambda i: (i, 0)), full(3, D),
                  full(256, D), full(512, D), full(256, D), full(D, D)],
        out_specs=[row, row],
        out_shape=[jax.ShapeDtypeStruct((S, D), BF16), jax.ShapeDtypeStruct((S, D), F32)],
        compiler_params=_cp("parallel"), name=name)(o_a, o_b, o_c, gates, b_gate, wa, wb, wc, w_out)


def _merge_bwd(d_mo, o_a, o_b, o_c, gates, b_gate, wa, wb, wc, w_out, name):
    tm = MERGE_TILE
    nsteps = S // tm

    def body(dmo_ref, oa_ref, ob_ref, oc_ref, g_ref, bg_ref, wa_ref, wb_ref, wc_ref, wo_ref,
             doa_ref, dob_ref, doc_ref, dg_ref, dwa_ref, dwb_ref, dwc_ref, dbg_ref, acc_a, acc_b, acc_c):
        @pl.when(pl.program_id(0) == 0)
        def _():
            acc_a[...] = jnp.zeros(acc_a.shape, F32)
            acc_b[...] = jnp.zeros(acc_b.shape, F32)
            acc_c[...] = jnp.zeros(acc_c.shape, F32)
            dbg_ref[...] = jnp.zeros(dbg_ref.shape, F32)

        dmg = _dot(dmo_ref[...], wo_ref[...], 1, 1)
        trip = ((oa_ref, wa_ref, doa_ref, acc_a), (ob_ref, wb_ref, dob_ref, acc_b), (oc_ref, wc_ref, doc_ref, acc_c))
        for i, (o_ref, w_ref, do_ref, dw_ref) in enumerate(trip):
            ob = _cat_slabs(o_ref).astype(BF16)
            pr = _dot(ob, w_ref[...], 1, 0)
            sg = jax.nn.sigmoid(g_ref[:, i * D:(i + 1) * D] + bg_ref[i:i + 1, :])
            dgate = dmg * pr * sg * (1.0 - sg)
            dg_ref[:, i * D:(i + 1) * D] = dgate.astype(BF16)
            dbg_ref[i:i + 1, :] += jnp.sum(dgate, axis=0, keepdims=True)
            dpr = (dmg * sg).astype(BF16)
            do = _dot(dpr, w_ref[...], 1, 1)
            for s in range(do_ref.shape[0]):
                do_ref[s] = do[:, s * LANES:(s + 1) * LANES]
            dw_ref[...] += _dot(ob, dpr, 0, 0)

        @pl.when(pl.program_id(0) == nsteps - 1)
        def _():
            dwa_ref[...] = acc_a[...].astype(BF16)
            dwb_ref[...] = acc_b[...].astype(BF16)
            dwc_ref[...] = acc_c[...].astype(BF16)

    slabs = lambda n: pl.BlockSpec((n, tm, LANES), lambda i: (0, i, 0))
    full = lambda r, c: pl.BlockSpec((r, c), lambda i: (0, 0))
    row = pl.BlockSpec((tm, D), lambda i: (i, 0))
    return pl.pallas_call(
        body, grid=(S // tm,),
        in_specs=[row, slabs(2), slabs(4), slabs(2), pl.BlockSpec((tm, GATE_COLS), lambda i: (i, 0)), full(3, D),
                  full(256, D), full(512, D), full(256, D), full(D, D)],
        out_specs=[slabs(2), slabs(4), slabs(2), pl.BlockSpec((tm, GATE_COLS), lambda i: (i, 0)),
                   full(256, D), full(512, D), full(256, D), full(3, D)],
        out_shape=[jax.ShapeDtypeStruct((2, S, LANES), F32), jax.ShapeDtypeStruct((4, S, LANES), F32),
                   jax.ShapeDtypeStruct((2, S, LANES), F32), jax.ShapeDtypeStruct((S, GATE_COLS), BF16),
                   jax.ShapeDtypeStruct((256, D), BF16), jax.ShapeDtypeStruct((512, D), BF16),
                   jax.ShapeDtypeStruct((256, D), BF16), jax.ShapeDtypeStruct((3, D), F32)],
        scratch_shapes=[pltpu.VMEM((256, D), F32), pltpu.VMEM((512, D), F32), pltpu.VMEM((256, D), F32)],
        compiler_params=_cp("arbitrary"), name=name)(d_mo, o_a, o_b, o_c, gates, b_gate, wa, wb, wc, w_out)


FC = 256
GELU_K = math.sqrt(2.0 / math.pi)
GELU_C = 0.044715


RC = 64
NRC = S // RC


def _down(tail, cur, n):
    row = lax.broadcasted_iota(jnp.int32, tail.shape, 0)
    rolled = pltpu.roll(cur, n, 0)
    first = jnp.where(row < n, pltpu.roll(tail, n, 0), rolled[0:8])
    return jnp.concatenate([first, rolled[8:]], axis=0)


def _up(cur, head, n):
    row = lax.broadcasted_iota(jnp.int32, head.shape, 0)
    rolled = pltpu.roll(cur, RC - n, 0)
    last = jnp.where(row >= 8 - n, pltpu.roll(head, 8 - n, 0), rolled[RC - 8:])
    return jnp.concatenate([rolled[:RC - 8], last], axis=0)


def _conv_chunk(load, j, w_ref, b_ref, half):
    r0 = pl.multiple_of(j * RC, RC)
    cur = load(r0, RC).astype(F32)
    tail = load(pl.multiple_of(jnp.maximum(r0 - 16, 0), 16), 16).astype(F32)[8:16]
    tail = jnp.where(j > 0, tail, 0.0)
    d1 = _down(tail, cur, 1)
    d2 = _down(tail, cur, 2)
    y = w_ref[0:1, half, :] * d2 + w_ref[1:2, half, :] * d1 + w_ref[2:3, half, :] * cur + b_ref[half:half + 1, :]
    return y, cur, d1, d2


def _chunk(j):
    return pl.ds(pl.multiple_of(j * RC, RC), RC)


def _fold8(x):
    return jnp.sum(x.reshape(RC // 8, 8, x.shape[-1]), axis=0)


def _ffn_act(u, conv_w, conv_b, name):
    def body(u_ref, w_ref, b_ref, a_ref, y_ref):
        def step(j, carry):
            yg = _conv_chunk(lambda r, n: u_ref[0, pl.ds(r, n), :], j, w_ref, b_ref, 0)[0]
            yv = _conv_chunk(lambda r, n: u_ref[1, pl.ds(r, n), :], j, w_ref, b_ref, 1)[0]
            th = jnp.tanh(GELU_K * (yg + GELU_C * yg * yg * yg))
            a_ref[_chunk(j), :] = (0.5 * yg * (1.0 + th) * yv).astype(BF16)
            y_ref[0, _chunk(j), :] = yg.astype(BF16)
            y_ref[1, _chunk(j), :] = yv.astype(BF16)
            return carry

        lax.fori_loop(0, NRC, step, 0)

    return pl.pallas_call(
        body, grid=(D_FF // FC,),
        in_specs=[pl.BlockSpec((2, S, FC), lambda j: (0, 0, j)), pl.BlockSpec((3, 2, FC), lambda j: (0, 0, j)),
                  pl.BlockSpec((2, FC), lambda j: (0, j))],
        out_specs=[pl.BlockSpec((S, FC), lambda j: (0, j)), pl.BlockSpec((2, S, FC), lambda j: (0, 0, j))],
        out_shape=[jax.ShapeDtypeStruct((S, D_FF), BF16), jax.ShapeDtypeStruct((2, S, D_FF), BF16)],
        compiler_params=_cp("parallel"), name=name)(u, conv_w, conv_b)


def _ffn_act_bwd(u, y, d_a, conv_w, name):
    def body(u_ref, y_ref, da_ref, w_ref, du_ref, dw_ref, db_ref, dy_s):
        def first(j, acc):
            yg = y_ref[0, _chunk(j), :].astype(F32)
            yv = y_ref[1, _chunk(j), :].astype(F32)
            th = jnp.tanh(GELU_K * (yg + GELU_C * yg * yg * yg))
            gelu = 0.5 * yg * (1.0 + th)
            dgelu = 0.5 * (1.0 + th) + 0.5 * yg * (1.0 - th * th) * GELU_K * (1.0 + 3.0 * GELU_C * yg * yg)
            da = da_ref[_chunk(j), :].astype(F32)
            dyg = da * yv * dgelu
            dyv = da * gelu
            dy_s[0, _chunk(j), :] = dyg
            dy_s[1, _chunk(j), :] = dyv
            return acc[0] + _fold8(dyg), acc[1] + _fold8(dyv)

        zero = jnp.zeros((8, FC), F32)
        accb = lax.fori_loop(0, NRC, first, (zero, zero))
        for half in range(2):
            db_ref[half:half + 1, :] = jnp.sum(accb[half], axis=0, keepdims=True)

        def second(j, acc):
            new = []
            for half in range(2):
                cur = dy_s[half, _chunk(j), :]
                h0 = pl.multiple_of(jnp.minimum((j + 1) * RC, S - 8), 8)
                head = jnp.where(j < NRC - 1, dy_s[half, pl.ds(h0, 8), :], 0.0)
                up1 = _up(cur, head, 1)
                up2 = _up(cur, head, 2)
                du = w_ref[2:3, half, :] * cur + w_ref[1:2, half, :] * up1 + w_ref[0:1, half, :] * up2
                du_ref[half, _chunk(j), :] = du.astype(BF16)
                uu = u_ref[half, _chunk(j), :].astype(F32)
                new += [_fold8(up2 * uu), _fold8(up1 * uu), _fold8(cur * uu)]
            return tuple(a + n for a, n in zip(acc, new))

        accw = lax.fori_loop(0, NRC, second, tuple(zero for _ in range(6)))
        for half in range(2):
            for k in range(3):
                dw_ref[k:k + 1, half, :] = jnp.sum(accw[3 * half + k], axis=0, keepdims=True)

    return pl.pallas_call(
        body, grid=(D_FF // FC,),
        in_specs=[pl.BlockSpec((2, S, FC), lambda j: (0, 0, j)), pl.BlockSpec((2, S, FC), lambda j: (0, 0, j)),
                  pl.BlockSpec((S, FC), lambda j: (0, j)), pl.BlockSpec((3, 2, FC), lambda j: (0, 0, j))],
        out_specs=[pl.BlockSpec((2, S, FC), lambda j: (0, 0, j)), pl.BlockSpec((3, 2, FC), lambda j: (0, 0, j)),
                   pl.BlockSpec((2, FC), lambda j: (0, j))],
        out_shape=[jax.ShapeDtypeStruct((2, S, D_FF), BF16), jax.ShapeDtypeStruct((3, 2, D_FF), F32),
                   jax.ShapeDtypeStruct((2, D_FF), F32)],
        scratch_shapes=[pltpu.VMEM((2, S, FC), F32)],
        compiler_params=_cp("parallel"), name=name)(u, y, d_a, conv_w)


def _layer_fwd(x, h1, w, bias, lname):
    n = lambda s: f"{lname}_{s}"
    w.need("in", h1)
    tn = 768
    proj = _mm(h1, w["w_in"], grid=(1, QKV_COLS // tn, 1),
               a_spec=pl.BlockSpec((S, D), lambda i, j, k: (i, 0)),
               b_spec=pl.BlockSpec((tn, D), lambda i, j, k: (j, 0)),
               out_shape=jax.ShapeDtypeStruct((QKV_SLABS, S, LANES), F32),
               out_spec=pl.BlockSpec((tn // LANES, S, LANES), lambda i, j, k: (j, i, 0)),
               ca=1, cb=1, acc_shape=(S, tn), out_slab=True, name=n("proj_qkv"))
    gates = _mm(h1, w["w_in"], grid=(1, GATE_COLS // tn, 1),
                a_spec=pl.BlockSpec((S, D), lambda i, j, k: (i, 0)),
                b_spec=pl.BlockSpec((tn, D), lambda i, j, k: (j + QKV_COLS // tn, 0)),
                out_shape=jax.ShapeDtypeStruct((S, GATE_COLS), BF16),
                out_spec=pl.BlockSpec((S, tn), lambda i, j, k: (i, j)),
                ca=1, cb=1, acc_shape=(S, tn), name=n("proj_gate"))
    nums, stats = [], []
    for g, (_, d) in enumerate(A_GROUPS):
        nm, st = _band_fwd(proj, bias, w["sinks"], d=d, q0=2 * g, k0=6 + 2 * g, v0=12 + 2 * g, npairs=2, bias0=2 * g,
                           shared_kv=False, name=n(f"attn_a{g}_fwd"))
        nums.append(nm)
        stats.append(st)
    o_a, lse_a = _combine_a(nums, stats, n("attn_a_combine"))
    o_b, lse_b = _band_fwd(proj, bias, w["sinks"], d=1, q0=18, k0=22, v0=23, npairs=4, bias0=6, shared_kv=True,
                           name=n("attn_b_fwd"))
    o_c, tot_c = _stick_fwd(proj, q0=24, k0=26, v0=28, name=n("attn_c_fwd"))
    w.need("mix", tot_c)
    merged, mo = _merge_fwd(o_a, o_b, o_c, gates, w["b_gate"], w["w_br_a"], w["w_br_b"], w["w_br_c"], w["w_out"], n("merge_fwd"))
    x2, h2 = _postnorm_res(x, mo, w["attn_post_norm"], w["ffn_pre_norm"], n("attn_post"))
    w.need("ffn", h2)
    u = _mm(h2, w["w_up"], grid=(1, 2 * D_FF // 1024, 1),
            a_spec=pl.BlockSpec((S, D), lambda i, j, k: (i, 0)),
            b_spec=pl.BlockSpec((D, 1024), lambda i, j, k: (0, j)),
            out_shape=jax.ShapeDtypeStruct((2, S, D_FF), BF16),
            out_spec=pl.BlockSpec((None, S, 1024), lambda i, j, k: (j // 4, i, j % 4)),
            ca=1, cb=0, acc_shape=(S, 1024), name=n("ffn_up"))
    a, y = _ffn_act(u, w["conv_w"], w["conv_b"], n("ffn_act"))
    fo = _mm_nn(a, w["w_down"], F32, 1024, 1024, 2048, n("ffn_down"))
    saved = dict(x=x, h1=h1, proj=proj, gates=gates, o_a=o_a, lse_a=lse_a, o_b=o_b, lse_b=lse_b, o_c=o_c, tot_c=tot_c,
                 merged=merged, mo=mo, x2=x2, h2=h2, u=u, y=y, a=a, fo=fo)
    return saved


def _layer_bwd(dx3, sv, w, bias, lname, tok=None, on_part=None, d_fo=None, below=None):
    n = lambda s: f"{lname}_{s}"
    g = {}

    def part(group, vec):
        t = on_part(group, g) if on_part is not None else None
        return vec if t is None else vec + t

    if d_fo is None:
        gain = w["ffn_post_norm"] if tok is None else w["ffn_post_norm"] + tok
        d_fo, g["ffn_post_norm"] = _norm_bwd(sv["fo"], gain, [dx3], None, BF16, n("ffn_post_bwd"))
    else:
        d_fo, g["ffn_post_norm"] = d_fo
    d_a = _mm_nt(d_fo, w["w_down"], BF16, S, 1024, 1024, n("ffn_down_bwd_x"))
    g["w_down"] = _mm_tn(sv["a"], d_fo, BF16, 1024, 1024, S, n("ffn_down_bwd_w"))
    d_u, dcw, dcb = _ffn_act_bwd(sv["u"], sv["y"], d_a, w["conv_w"], n("ffn_act_bwd"))
    g["conv_w"] = dcw.reshape(3, 2 * D_FF)
    g["conv_b"] = dcb.reshape(1, 2 * D_FF)
    g["w_up"] = _mm(sv["h2"], d_u, grid=(1, 2 * D_FF // 1024, 1),
                    a_spec=pl.BlockSpec((S, D), lambda i, j, k: (k, 0)),
                    b_spec=pl.BlockSpec((None, S, 1024), lambda i, j, k: (j // 4, k, j % 4)),
                    out_shape=jax.ShapeDtypeStruct((D, 2 * D_FF), BF16),
                    out_spec=pl.BlockSpec((D, 1024), lambda i, j, k: (0, j)),
                    ca=0, cb=0, acc_shape=(D, 1024), name=n("ffn_up_bwd_w"))
    tok_ffn = on_part("ffn", g) if on_part is not None else None
    d_h2 = _mm(d_u, w["w_up"], grid=(S // 1024, 1, 2),
               a_spec=pl.BlockSpec((None, 1024, D_FF), lambda i, j, k: (k, i, 0)),
               b_spec=pl.BlockSpec((D, D_FF), lambda i, j, k: (0, k)),
               out_shape=jax.ShapeDtypeStruct((S, D), F32),
               out_spec=pl.BlockSpec((1024, D), lambda i, j, k: (i, 0)),
               ca=1, cb=1, acc_shape=(1024, D), after=tok_ffn, name=n("ffn_up_bwd_x"))
    dx2, d_mo, g["ffn_pre_norm"], g["attn_post_norm"] = _norm_bwd_chain(
        sv["x2"], w["ffn_pre_norm"], [d_h2], dx3, sv["mo"], w["attn_post_norm"], n("ffn_pre_attn_post_bwd"))
    g["w_out"] = _mm_tn(sv["merged"], d_mo, BF16, 1024, 1024, S, n("out_bwd_w"))
    do_a, do_b, do_c, d_gates, dwa, dwb, dwc, g["b_gate"] = _merge_bwd(
        d_mo, sv["o_a"], sv["o_b"], sv["o_c"], sv["gates"], w["b_gate"], w["w_br_a"], w["w_br_b"], w["w_br_c"],
        w["w_out"], n("merge_bwd"))
    g["w_br_a"], g["w_br_b"], g["w_br_c"] = dwa, dwb, dwc
    sinks = part("mix", w["sinks"])
    proj = sv["proj"]
    dqkv = lax.empty((QKV_SLABS, S, LANES), BF16)
    gbias = []
    for gi, (_, d) in enumerate(A_GROUPS):
        dqkv, gg, _ = _band_bwd(proj, bias, sv["o_a"], do_a, sv["lse_a"], sinks, dqkv, d=d, q0=2 * gi, k0=6 + 2 * gi,
                                v0=12 + 2 * gi, npairs=2, bias0=2 * gi, shared_kv=False, name=n(f"attn_a{gi}_bwd"))
        gbias.append(gg)
    dqkv, ggb, dsink = _band_bwd(proj, bias, sv["o_b"], do_b, sv["lse_b"], sinks, dqkv, d=1, q0=18, k0=22, v0=23,
                                 npairs=4, bias0=6, shared_kv=True, name=n("attn_b_bwd"))
    gbias.append(ggb)
    g["bias_g"] = jnp.concatenate(gbias, axis=0).reshape(N_BIAS_HEADS, BLK, 2 * BLK)
    g["sinks"] = dsink[:, 0, :2].reshape(1, 8)
    dqkv = _stick_bwd(proj, do_c, sv["tot_c"], dqkv, q0=24, k0=26, v0=28, name=n("attn_c_bwd"))
    ts = 6
    tsx = QKV_SLABS
    dw_in = _mm(dqkv, sv["h1"], grid=(QKV_SLABS // ts, 1, 1),
                a_spec=pl.BlockSpec((ts, S, LANES), lambda i, j, k: (i, k, 0)),
                b_spec=pl.BlockSpec((S, D), lambda i, j, k: (k, 0)),
                out_shape=jax.ShapeDtypeStruct((IN_COLS, D), BF16),
                out_spec=pl.BlockSpec((ts * LANES, D), lambda i, j, k: (i, 0)),
                ca=0, cb=0, acc_shape=(ts * LANES, D), a_slab=True, name=n("in_bwd_w_qkv"))
    g["w_in"] = _mm(d_gates, sv["h1"], grid=(GATE_COLS // 768, 1, 1),
                    a_spec=pl.BlockSpec((S, 768), lambda i, j, k: (k, i)),
                    b_spec=pl.BlockSpec((S, D), lambda i, j, k: (k, 0)),
                    out_shape=jax.ShapeDtypeStruct((IN_COLS, D), BF16),
                    out_spec=pl.BlockSpec((768, D), lambda i, j, k: (i + QKV_COLS // 768, 0)),
                    ca=0, cb=0, acc_shape=(768, D), alias_out=dw_in, name=n("in_bwd_w_gate"))
    tok_in = on_part("in", g) if on_part is not None else None
    d_h1a = _mm(dqkv, w["w_in"], grid=(S // 1024, 1, QKV_SLABS // tsx),
                a_spec=pl.BlockSpec((tsx, 1024, LANES), lambda i, j, k: (k, i, 0)),
                b_spec=pl.BlockSpec((tsx * LANES, D), lambda i, j, k: (k, 0)),
                out_shape=jax.ShapeDtypeStruct((S, D), F32),
                out_spec=pl.BlockSpec((1024, D), lambda i, j, k: (i, 0)),
                ca=1, cb=0, acc_shape=(1024, D), a_slab=True, after=tok_in, name=n("in_bwd_x_qkv"))
    d_h1b = _mm(d_gates, w["w_in"], grid=(S // 1024, 1, GATE_COLS // 768),
                a_spec=pl.BlockSpec((1024, 768), lambda i, j, k: (i, k)),
                b_spec=pl.BlockSpec((768, D), lambda i, j, k: (k + QKV_COLS // 768, 0)),
                out_shape=jax.ShapeDtypeStruct((S, D), F32),
                out_spec=pl.BlockSpec((1024, D), lambda i, j, k: (i, 0)),
                ca=1, cb=0, acc_shape=(1024, D), after=tok_in, name=n("in_bwd_x_gate"))
    if below is None:
        dx, g["attn_pre_norm"] = _norm_bwd(sv["x"], w["attn_pre_norm"], [d_h1a, d_h1b], dx2, F32, n("attn_pre_bwd"))
        return dx, g, tok_in, None
    dx, d_fo_below, g["attn_pre_norm"], dg_below = _norm_bwd_chain(
        sv["x"], w["attn_pre_norm"], [d_h1a, d_h1b], dx2, below[0], below[1], n("attn_pre_ffn_post_bwd"))
    return dx, g, tok_in, (d_fo_below, dg_below)


def _local_step(x, target, ws, rel_bias, tok=None, on_grads=None):
    buckets = jnp.asarray(_bucket_tiles())
    bias = _bias_tiles(rel_bias, buckets, "bias_tiles").reshape(N_BIAS_HEADS // 2, 2, 2, BLK, 2 * BLK)
    saved = []
    gain0 = ws[0]["attn_pre_norm"] if tok is None else ws[0]["attn_pre_norm"] + tok
    h1 = _prenorm(x, gain0, "l0_attn_pre")
    for l in range(DEPTH):
        sv = _layer_fwd(x, h1, ws[l], bias, f"l{l}")
        saved.append(sv)
        if l + 1 < DEPTH:
            x, h1 = _postnorm_res(sv["x2"], sv["fo"], ws[l]["ffn_post_norm"], ws[l + 1]["attn_pre_norm"], f"l{l}_ffn_post")
    top = saved[-1]
    dy, loss_tile, d_fo_top, dg_top = _loss_head(top["x2"], top["fo"], ws[-1]["ffn_post_norm"], target, "loss_head")
    grads = [None] * DEPTH
    tok, d_fo = None, (d_fo_top, dg_top)
    for l in reversed(range(DEPTH)):
        on_part = None if on_grads is None else functools.partial(on_grads, l)
        below = (saved[l - 1]["fo"], ws[l - 1]["ffn_post_norm"]) if l > 0 else None
        dy, grads[l], tok, d_fo = _layer_bwd(dy, saved[l], ws[l], bias, f"l{l}", tok, on_part, d_fo, below)
    g_rel = _bias_grad([grads[l]["bias_g"] for l in range(DEPTH)], buckets, "bias_grad")[:, :N_BIAS_HEADS]
    return loss_tile, dy, grads, g_rel


def _coords():
    return lax.axis_index("x"), lax.axis_index("y"), lax.axis_index("c")


def _peer(rel):
    x, y, c = _coords()
    return (1 - x if rel & 4 else x, 1 - y if rel & 2 else y, 1 - c if rel & 1 else c)


def _exchange(srcs, dst_shapes, src_win, dst_win, name, after=None):
    nt = len(srcs)
    extra = [] if after is None else [after]

    def body(*refs):
        src_refs, dst_refs = refs[:nt], refs[nt + len(extra):2 * nt + len(extra)]
        send_sems, recv_sems, local_sems = refs[2 * nt + len(extra):]
        x, y, c = _coords()
        me = 4 * x + 2 * y + c
        locals_ = []
        for t in range(nt):
            cp = pltpu.make_async_copy(src_win(t, src_refs[t], me), dst_win(t, dst_refs[t], me), local_sems.at[t])
            cp.start()
            locals_.append(cp)
        sends = []
        for rel in range(1, NDEV):
            px, py, pc = _peer(rel)
            q = 4 * px + 2 * py + pc
            for t in range(nt):
                cp = pltpu.make_async_remote_copy(
                    src_ref=src_win(t, src_refs[t], q), dst_ref=dst_win(t, dst_refs[t], me),
                    send_sem=send_sems.at[rel - 1, t], recv_sem=recv_sems.at[rel - 1, t],
                    device_id=(px, py, pc), device_id_type=MESH)
                cp.start()
                sends.append(cp)
        for rel in range(1, NDEV):
            px, py, pc = _peer(rel)
            q = 4 * px + 2 * py + pc
            for t in range(nt):
                pltpu.make_async_remote_copy(
                    src_ref=src_win(t, src_refs[t], me), dst_ref=dst_win(t, dst_refs[t], q),
                    send_sem=send_sems.at[rel - 1, t], recv_sem=recv_sems.at[rel - 1, t],
                    device_id=(px, py, pc), device_id_type=MESH).wait_recv()
        for cp in sends:
            cp.wait_send()
        for cp in locals_:
            cp.wait()

    return pl.pallas_call(
        body, in_specs=[ANY] * (nt + len(extra)), out_specs=[ANY] * nt, out_shape=dst_shapes,
        scratch_shapes=[pltpu.SemaphoreType.DMA((NDEV - 1, nt)), pltpu.SemaphoreType.DMA((NDEV - 1, nt)),
                        pltpu.SemaphoreType.DMA((nt,))],
        name=name)(*srcs, *extra)


BIG = (("w_in", 0, 864), ("w_br_a", 1, 128), ("w_br_b", 1, 128), ("w_br_c", 1, 128), ("w_out", 0, 128),
       ("w_up", 1, 1024), ("w_down", 0, 512))


NBIG = len(BIG)
BIG_FULL = {"w_in": (IN_COLS, D), "w_br_a": (256, D), "w_br_b": (512, D), "w_br_c": (256, D), "w_out": (D, D),
            "w_up": (D, 2 * D_FF), "w_down": (D_FF, D)}
SHARD_ROWS = {"w_in": 288, "w_up": 256, "w_down": 256}
LAYER_GROUPS = (("in", (0,)), ("mix", (1, 2, 3, 4)), ("ffn", (5, 6)))

HBM_SPEC = pl.BlockSpec(memory_space=pltpu.HBM)
SEM_SPEC = pl.BlockSpec(memory_space=pltpu.SEMAPHORE)


def _hbm(a):
    return pltpu.with_memory_space_constraint(a, pltpu.HBM)


def _shard_window(t, ref, k):
    nm, ax, ext = BIG[t % NBIG]
    off = pl.multiple_of(k * ext, ext)
    if ax == 0:
        return ref.at[pl.ds(off, ext), :]
    return ref.at[:, pl.ds(off, ext)]


def _whole(t, ref, k):
    return ref


def _slot(t, ref, k):
    return ref.at[k]


def _own_block_spec(t, rows, me_of):
    nm, ax, ext = BIG[t % NBIG]
    r, c = BIG_FULL[nm]
    if ax == 0:
        return pl.BlockSpec((rows, c), lambda i, m: (me_of(m) * (ext // rows) + i, 0))
    return pl.BlockSpec((rows, ext), lambda i, m: (i, me_of(m)))


def _cast_own(t, shards, me_arr, name):
    nm, ax, ext = BIG[t % NBIG]
    layer = t // NBIG
    _, nr, nc = shards.shape
    rows = SHARD_ROWS.get(nm, nr)
    shape = BIG_FULL[nm]

    def body(m_ref, s_ref, o_ref):
        o_ref[...] = s_ref[...].astype(BF16)

    return pl.pallas_call(
        body, grid_spec=pltpu.PrefetchScalarGridSpec(
            num_scalar_prefetch=1, grid=(nr // rows,),
            in_specs=[pl.BlockSpec((None, rows, nc), lambda i, m: (layer, i, 0))],
            out_specs=_own_block_spec(t, rows, lambda m: m[0])),
        out_shape=jax.ShapeDtypeStruct(shape, BF16), compiler_params=_cp("arbitrary"), name=name)(me_arr, shards)


ALL_RELS = tuple(range(1, NDEV))
NEAR_RELS = (1, 2, 4, 6)
FAR_RELS = (2, 4, 6)


def _xchg_start(srcs, lands, groups, src_win, dst_win, after, name, rels=ALL_RELS, tids=None):
    ns = 0 if srcs is None else len(srcs)
    nt, ng = len(lands), len(groups)
    ins = ([] if srcs is None else list(srcs)) + list(lands)

    def body(*refs):
        src_refs, land_refs = refs[:ns], refs[ns:ns + nt]
        sems = refs[ns + nt + 1:ns + nt + 1 + 2 * ng]
        token = refs[-1]
        x, y, c = _coords()
        me = 4 * x + 2 * y + c
        for gi, grp in enumerate(groups):
            for j, t in enumerate(grp):
                tid = t if tids is None else tids[t]
                for ri, rel in enumerate(rels):
                    px, py, pc = _peer(rel)
                    q = 4 * px + 2 * py + pc
                    src = dst_win(tid, land_refs[t], me) if srcs is None else src_win(tid, src_refs[t], q)
                    pltpu.make_async_remote_copy(
                        src_ref=src, dst_ref=dst_win(tid, land_refs[t], me),
                        send_sem=sems[2 * gi].at[ri * len(grp) + j],
                        recv_sem=sems[2 * gi + 1].at[ri * len(grp) + j],
                        device_id=(px, py, pc), device_id_type=MESH).start()
        token[...] = jnp.zeros((8, LANES), F32)

    out_shape = []
    for grp in groups:
        out_shape += [pltpu.SemaphoreType.DMA((len(rels) * len(grp),))] * 2
    out_shape += [pltpu.HBM(a.shape, a.dtype) for a in ins]
    out_shape.append(jax.ShapeDtypeStruct((8, LANES), F32))
    outs = pl.pallas_call(
        body, in_specs=[HBM_SPEC] * len(ins) + [ANY],
        out_specs=[SEM_SPEC] * (2 * ng) + [HBM_SPEC] * len(ins) + [pl.BlockSpec(memory_space=pltpu.VMEM)],
        out_shape=out_shape, input_output_aliases={i: 2 * ng + i for i in range(len(ins))},
        compiler_params=pltpu.CompilerParams(has_side_effects=pltpu.SideEffectType.DATAFLOW_SIDE_EFFECTING),
        name=name)(*[_hbm(a) for a in ins], after)
    sems = [(outs[2 * gi], outs[2 * gi + 1]) for gi in range(ng)]
    thru = list(outs[2 * ng:2 * ng + len(ins)])
    return sems, (None if srcs is None else thru[:ns]), thru[ns:], outs[-1]


def _xchg_wait(sems, srcs, lands, tids, after, src_win, dst_win, name, rels=ALL_RELS):
    ns = 0 if srcs is None else len(srcs)
    n = len(lands)
    send_sem, recv_sem = sems
    ins = ([] if srcs is None else list(srcs)) + list(lands)

    def body(*refs):
        src_refs, land_refs = refs[:ns], refs[ns:ns + n]
        ssem, rsem = refs[ns + n], refs[ns + n + 1]
        x, y, c = _coords()
        me = 4 * x + 2 * y + c
        for j, t in enumerate(tids):
            for ri, rel in enumerate(rels):
                px, py, pc = _peer(rel)
                q = 4 * px + 2 * py + pc
                src = dst_win(t, land_refs[j], me) if srcs is None else src_win(t, src_refs[j], q)
                cp = pltpu.make_async_remote_copy(
                    src_ref=src, dst_ref=dst_win(t, land_refs[j], q),
                    send_sem=ssem.at[ri * n + j], recv_sem=rsem.at[ri * n + j],
                    device_id=(px, py, pc), device_id_type=MESH)
                cp.wait_send()
                cp.wait_recv()

    outs = pl.pallas_call(
        body, in_specs=[HBM_SPEC] * len(ins) + [SEM_SPEC, SEM_SPEC, ANY], out_specs=[HBM_SPEC] * len(ins),
        out_shape=[pltpu.HBM(a.shape, a.dtype) for a in ins],
        input_output_aliases={i: i for i in range(len(ins))},
        compiler_params=pltpu.CompilerParams(has_side_effects=pltpu.SideEffectType.DATAFLOW_SIDE_EFFECTING),
        name=name)(*ins, send_sem, recv_sem, after)
    return (None if srcs is None else list(outs[:ns])), list(outs[ns:])


def _gather_forward(sems_in, lands, groups, tids, after, dst_win, name):
    nt, ng = len(lands), len(groups)

    def body(*refs):
        land_refs = refs[:nt]
        in_sems = refs[nt:nt + 2 * ng]
        out_sems = refs[nt + 2 * ng + 1:nt + 4 * ng + 1]
        token = refs[-1]
        x, y, c = _coords()
        me = 4 * x + 2 * y + c
        sib = (x, y, 1 - c)
        for gi, grp in enumerate(groups):
            n = len(grp)
            for j, pos in enumerate(grp):
                t = tids[pos]
                for ri, rel in enumerate(NEAR_RELS):
                    px, py, pc = _peer(rel)
                    q = 4 * px + 2 * py + pc
                    cp = pltpu.make_async_remote_copy(
                        src_ref=dst_win(t, land_refs[pos], me), dst_ref=dst_win(t, land_refs[pos], q),
                        send_sem=in_sems[2 * gi].at[ri * n + j], recv_sem=in_sems[2 * gi + 1].at[ri * n + j],
                        device_id=(px, py, pc), device_id_type=MESH)
                    cp.wait_send()
                    cp.wait_recv()
            for j, pos in enumerate(grp):
                t = tids[pos]
                for fi, rel in enumerate(FAR_RELS):
                    px, py, pc = _peer(rel)
                    q = 4 * px + 2 * py + pc
                    win = dst_win(t, land_refs[pos], q)
                    pltpu.make_async_remote_copy(
                        src_ref=win, dst_ref=win,
                        send_sem=out_sems[2 * gi].at[fi * n + j], recv_sem=out_sems[2 * gi + 1].at[fi * n + j],
                        device_id=sib, device_id_type=MESH).start()
        token[...] = jnp.zeros((8, LANES), F32)

    out_shape = []
    for grp in groups:
        out_shape += [pltpu.SemaphoreType.DMA((len(FAR_RELS) * len(grp),))] * 2
    out_shape += [pltpu.HBM(a.shape, a.dtype) for a in lands]
    out_shape.append(jax.ShapeDtypeStruct((8, LANES), F32))
    flat_sems = [s for pair in sems_in for s in pair]
    outs = pl.pallas_call(
        body, in_specs=[HBM_SPEC] * nt + [SEM_SPEC] * (2 * ng) + [ANY],
        out_specs=[SEM_SPEC] * (2 * ng) + [HBM_SPEC] * nt + [pl.BlockSpec(memory_space=pltpu.VMEM)],
        out_shape=out_shape, input_output_aliases={i: 2 * ng + i for i in range(nt)},
        compiler_params=pltpu.CompilerParams(has_side_effects=pltpu.SideEffectType.DATAFLOW_SIDE_EFFECTING),
        name=name)(*[_hbm(a) for a in lands], *flat_sems, after)
    sems = [(outs[2 * gi], outs[2 * gi + 1]) for gi in range(ng)]
    return sems, list(outs[2 * ng:2 * ng + nt]), outs[-1]


class _Weights:
    def __init__(self, ready, pending=None):
        self.ready = dict(ready)
        self.pending = dict(pending or {})

    def __getitem__(self, k):
        return self.ready[k]

    def need(self, group, after):
        fn = self.pending.pop(group, None)
        if fn is not None:
            self.ready.update(fn(after))


def _adamw_math(w, g, m, v):
    m2 = ADAM_B1 * m + (1.0 - ADAM_B1) * g
    v2 = ADAM_B2 * v + (1.0 - ADAM_B2) * (g * g)
    m_hat = m2 / (1.0 - ADAM_B1 ** ADAM_STEP)
    v_hat = v2 / (1.0 - ADAM_B2 ** ADAM_STEP)
    delta = -ADAM_LR * (m_hat / (jnp.sqrt(v_hat) + ADAM_EPS) + ADAM_WD * w)
    return delta, m2, v2


def _adamw(t, parts, own, me_arr, w, m, v, layer, prev, rows, name):
    nl, nr, nc = w.shape

    def body(me_ref, p_ref, own_ref, w_ref, m_ref, v_ref, *rest):
        g_ref, d_ref, m2_ref, v2_ref = rest[-4:]
        me = me_ref[0]
        g = None
        for k in range(NDEV):
            term = jnp.where(me == k, own_ref[...], p_ref[k]).astype(F32)
            g = term if g is None else g + term
        delta, m2, v2 = _adamw_math(w_ref[...], g, m_ref[...], v_ref[...])
        g_ref[...] = g
        d_ref[...] = delta
        m2_ref[...] = m2
        v2_ref[...] = v2

    blk = pl.BlockSpec((None, rows, nc), lambda i, mm: (layer, i, 0))
    pblk = pl.BlockSpec((NDEV, rows, nc), lambda i, mm: (0, i, 0))
    extra = [] if prev is None else list(prev)
    return pl.pallas_call(
        body, grid_spec=pltpu.PrefetchScalarGridSpec(
            num_scalar_prefetch=1, grid=(nr // rows,),
            in_specs=[pblk, _own_block_spec(t, rows, lambda mm: mm[0]), blk, blk, blk] + [ANY] * len(extra),
            out_specs=[blk] * 4),
        out_shape=[jax.ShapeDtypeStruct(w.shape, F32)] * 4,
        input_output_aliases={6 + k: k for k in range(len(extra))},
        compiler_params=_cp("arbitrary"), name=name)(me_arr, parts, own, w, m, v, *extra)


def _pack(vecs):
    flat = jnp.concatenate([v.reshape(-1).astype(F32) for v in vecs])
    n = flat.shape[0]
    rows = -(-n // (8 * LANES)) * 8
    return jnp.pad(flat, (0, rows * LANES - n)).reshape(rows, LANES)


ROWPACK = (("rel_bias", 32, 32, (NUM_BUCKETS, N_BIAS_HEADS)), ("sinks", 8, 8, (DEPTH, 8)),
           ("attn_pre_norm", 16, 16, (DEPTH, D)), ("attn_post_norm", 16, 16, (DEPTH, D)),
           ("ffn_pre_norm", 16, 16, (DEPTH, D)), ("ffn_post_norm", 16, 16, (DEPTH, D)),
           ("conv_b", 128, 128, (DEPTH, 2 * D_FF)), ("b_gate", 48, 8, (DEPTH, 3, 128)),
           ("conv_w", 384, 48, (DEPTH, 3, 1024)))
ROWS_OWN = sum(r for _, _, r, _ in ROWPACK)
N_REPL = 7
ROWS_REPL = sum(r for _, _, r, _ in ROWPACK[:N_REPL])
ROWS_SHARD = ROWS_OWN - ROWS_REPL


def _as_rows(a, rows):
    a = a.astype(F32)
    if a.shape[-1] < LANES:
        a = jnp.pad(a.reshape(-1, a.shape[-1]), ((0, 0), (0, LANES - a.shape[-1])))
    a = a.reshape(-1, LANES)
    return jnp.pad(a, ((0, rows - a.shape[0]), (0, 0)))


def _rowpack(arrs, entries=ROWPACK):
    return jnp.concatenate([_as_rows(arrs[nm], ro) for nm, _, ro, _ in entries], axis=0)


def _shard_rows(g):
    bg = jnp.transpose(g["b_gate"].astype(F32).reshape(DEPTH * 3, NDEV, LANES), (1, 0, 2))
    bg = jnp.pad(bg, ((0, 0), (0, 8 - DEPTH * 3), (0, 0)))
    cw = jnp.transpose(g["conv_w"].astype(F32).reshape(DEPTH * 3, NDEV, 8, LANES), (1, 0, 2, 3))
    return jnp.concatenate([bg, cw.reshape(NDEV, DEPTH * 3 * 8, LANES)], axis=1)


def _small_update(parts_repl, parts_shard, w, m, v, name):
    nsm = len(ROWPACK)

    def body(pr_ref, ps_ref, w_ref, m_ref, v_ref, *rest):
        outs = rest[:4 * nsm]
        loss_ref = rest[4 * nsm]
        g_s, d_s, m_s, v_s = rest[4 * nsm + 1:]
        gr, gs = pr_ref[0], ps_ref[0]
        for k in range(1, NDEV):
            gr = gr + pr_ref[k]
            gs = gs + ps_ref[k]
        g_s[0:ROWS_REPL, :] = gr[:ROWS_REPL]
        g_s[ROWS_REPL:ROWS_OWN, :] = gs
        loss_ref[...] = gr[ROWS_REPL:]
        delta, m2, v2 = _adamw_math(w_ref[...], g_s[...], m_ref[...], v_ref[...])
        d_s[...] = delta
        m_s[...] = m2
        v_s[...] = v2
        for kind, src in enumerate((g_s, d_s, m_s, v_s)):
            oo = 0
            for idx, (nm, rf, ro, shp) in enumerate(ROWPACK):
                o_ref = outs[kind * nsm + idx]
                if nm in ("rel_bias", "sinks"):
                    o_ref[...] = src[oo:oo + shp[0], 0:shp[1]]
                elif nm == "b_gate":
                    for l in range(DEPTH):
                        o_ref[l] = src[oo + 3 * l:oo + 3 * l + 3, :]
                elif nm == "conv_w":
                    for l in range(DEPTH):
                        for k in range(8):
                            o_ref[l, :, k * LANES:(k + 1) * LANES] = src[pl.ds(oo + 24 * l + k, 3, stride=8), :]
                else:
                    per = shp[1] // LANES
                    for k in range(per):
                        o_ref[:, k * LANES:(k + 1) * LANES] = src[pl.ds(oo + k, DEPTH, stride=per), :]
                oo += ro

    vm = pl.BlockSpec(memory_space=pltpu.VMEM)
    shapes = [jax.ShapeDtypeStruct(shp, F32) for _ in range(4) for _, _, _, shp in ROWPACK]
    shapes.append(jax.ShapeDtypeStruct((8, LANES), F32))
    outs = pl.pallas_call(
        body, in_specs=[vm] * 5, out_specs=[vm] * (4 * nsm + 1), out_shape=shapes,
        scratch_shapes=[pltpu.VMEM((ROWS_OWN, LANES), F32)] * 4,
        name=name)(parts_repl, parts_shard, w, m, v)
    names = [nm for nm, _, _, _ in ROWPACK]
    return [dict(zip(names, outs[kind * nsm:(kind + 1) * nsm])) for kind in range(4)] + [outs[-1]]


def kernel(x, rel_bias, attn_pre_norm, w_in, b_gate, sinks, w_br_a, w_br_b, w_br_c, w_out, attn_post_norm, ffn_pre_norm, w_up, conv_w, conv_b, w_down, ffn_post_norm, loss_target, m_rel_bias, m_attn_pre_norm, m_w_in, m_b_gate, m_sinks, m_w_br_a, m_w_br_b, m_w_br_c, m_w_out, m_attn_post_norm, m_ffn_pre_norm, m_w_up, m_conv_w, m_conv_b, m_w_down, m_ffn_post_norm, v_rel_bias, v_attn_pre_norm, v_w_in, v_b_gate, v_sinks, v_w_br_a, v_w_br_b, v_w_br_c, v_w_out, v_attn_post_norm, v_ffn_pre_norm, v_w_up, v_conv_w, v_conv_b, v_w_down, v_ffn_post_norm):
    P = dict(rel_bias=rel_bias, attn_pre_norm=attn_pre_norm, w_in=w_in, b_gate=b_gate, sinks=sinks, w_br_a=w_br_a,
             w_br_b=w_br_b, w_br_c=w_br_c, w_out=w_out, attn_post_norm=attn_post_norm, ffn_pre_norm=ffn_pre_norm,
             w_up=w_up, conv_w=conv_w, conv_b=conv_b, w_down=w_down, ffn_post_norm=ffn_post_norm)
    M = dict(rel_bias=m_rel_bias, attn_pre_norm=m_attn_pre_norm, w_in=m_w_in, b_gate=m_b_gate, sinks=m_sinks,
             w_br_a=m_w_br_a, w_br_b=m_w_br_b, w_br_c=m_w_br_c, w_out=m_w_out, attn_post_norm=m_attn_post_norm,
             ffn_pre_norm=m_ffn_pre_norm, w_up=m_w_up, conv_w=m_conv_w, conv_b=m_conv_b, w_down=m_w_down,
             ffn_post_norm=m_ffn_post_norm)
    V = dict(rel_bias=v_rel_bias, attn_pre_norm=v_attn_pre_norm, w_in=v_w_in, b_gate=v_b_gate, sinks=v_sinks,
             w_br_a=v_w_br_a, w_br_b=v_w_br_b, w_br_c=v_w_br_c, w_out=v_w_out, attn_post_norm=v_attn_post_norm,
             ffn_pre_norm=v_ffn_pre_norm, w_up=v_w_up, conv_w=v_conv_w, conv_b=v_conv_b, w_down=v_w_down,
             ffn_post_norm=v_ffn_post_norm)
    tr = lambda a: jnp.swapaxes(a, 1, 2)
    PB = {nm: (tr(P[nm]) if nm == "w_in" else P[nm]) for nm, _, _ in BIG}
    MB = {nm: (tr(M[nm]) if nm == "w_in" else M[nm]) for nm, _, _ in BIG}
    VB = {nm: (tr(V[nm]) if nm == "w_in" else V[nm]) for nm, _, _ in BIG}
    xi, yi, ci = _coords()
    me = 4 * xi + 2 * yi + ci

    me_arr = me.astype(jnp.int32).reshape(1)

    small_w = _pack([b_gate.reshape(-1), conv_w.reshape(-1)])
    (small_w_all,) = _exchange([small_w], [jax.ShapeDtypeStruct((NDEV,) + small_w.shape, F32)],
                               _whole, _slot, "gather_small_weights")
    nbg, ncw = DEPTH * 3 * 128, DEPTH * 3 * 1024
    flat_all = small_w_all.reshape(NDEV, -1)
    b_gate_full = jnp.transpose(flat_all[:, :nbg].reshape(NDEV, DEPTH, 3, 128), (1, 2, 0, 3)).reshape(DEPTH, 3, D)
    conv_w_full = jnp.transpose(flat_all[:, nbg:nbg + ncw].reshape(NDEV, DEPTH, 3, 1024), (1, 2, 0, 3)).reshape(DEPTH, 3, 2 * D_FF)

    groups = [tuple(l * NBIG + t for t in tids) for l in range(DEPTH) for _, tids in LAYER_GROUPS]
    cast = lambda i, m=me_arr: _cast_own(i, PB[BIG[i % NBIG][0]], m, f"gather_own_l{i // NBIG}_{BIG[i % NBIG][0]}")
    first = list(groups[0])
    rest = [i for grp in groups[1:] for i in grp]
    sems0, _, lands0, tok_first = _xchg_start(None, [cast(i) for i in first], [tuple(range(len(first)))], None,
                                              _shard_window, small_w_all, "gather_start_first", rels=NEAR_RELS, tids=first)
    where_rest = {tid: k for k, tid in enumerate(rest)}
    me_rest = me_arr + tok_first[0, 0:1].astype(jnp.int32)
    sems1, _, lands1, g_tok = _xchg_start(None, [cast(i, me_rest) for i in rest],
                                          [tuple(where_rest[i] for i in grp) for grp in groups[1:]], None,
                                          _shard_window, lands0[0], "gather_start_rest", rels=NEAR_RELS, tids=rest)
    g_sems = list(sems0) + list(sems1)
    tok0 = g_tok[0:1, 0:1]
    lands_now = [None] * (DEPTH * NBIG)
    for i, a in zip(first + rest, list(lands0) + list(lands1)):
        lands_now[i] = a
    fwd_sems = {}
    fwd_plan = {0: (0,), 1: (1,), 2: (2,), 3: (3, 4, 5)}

    def gather_waiter(gi, l, gname, tids):
        def wait(after):
            if gi in fwd_plan:
                gis = fwd_plan[gi]
                flat = [i for g2 in gis for i in groups[g2]]
                where = {tid: k for k, tid in enumerate(flat)}
                fs, new_lands, ftok = _gather_forward(
                    [g_sems[g2] for g2 in gis], [lands_now[i] for i in flat],
                    [[where[i] for i in groups[g2]] for g2 in gis], flat, after, _shard_window, f"gather_forward_{gi}")
                for g2, s in zip(gis, fs):
                    fwd_sems[g2] = s
                for i, a in zip(flat, new_lands):
                    lands_now[i] = a
                after = ftok
            ids = [l * NBIG + t for t in tids]
            _, got = _xchg_wait(fwd_sems[gi], None, [lands_now[i] for i in ids], ids, after,
                                None, _shard_window, f"gather_wait_l{l}_{gname}", rels=FAR_RELS)
            out = {}
            for t, arr in zip(tids, got):
                nm = BIG[t][0]
                out[nm] = arr
            return out
        return wait

    pending = [{gname: gather_waiter(l * len(LAYER_GROUPS) + k, l, gname, tids)
                for k, (gname, tids) in enumerate(LAYER_GROUPS)} for l in range(DEPTH)]
    ws = []
    for l in range(DEPTH):
        ws.append(_Weights(dict(
            b_gate=b_gate_full[l], conv_w=conv_w_full[l].reshape(3, 2, D_FF), conv_b=conv_b[l].reshape(2, D_FF),
            sinks=sinks[l].reshape(1, 8),
            attn_pre_norm=attn_pre_norm[l].reshape(1, D), attn_post_norm=attn_post_norm[l].reshape(1, D),
            ffn_pre_norm=ffn_pre_norm[l].reshape(1, D), ffn_post_norm=ffn_post_norm[l].reshape(1, D)), pending[l]))

    rs = {}

    group_tids = dict(LAYER_GROUPS)

    def start_scatter(l, gname, grads_l):
        tids = group_tids[gname]
        blocks, lands_rs = [], []
        for t in tids:
            nm, ax, ext = BIG[t]
            gfull = grads_l[nm].astype(BF16)
            shp = (NDEV, ext, gfull.shape[1]) if ax == 0 else (NDEV, gfull.shape[0], ext)
            blocks.append(gfull)
            lands_rs.append(lax.empty(shp, BF16))
        local = list(range(len(tids)))
        win = lambda j, ref, k: _shard_window(tids[j], ref, k)
        sems, s_thru, l_thru, tok = _xchg_start(blocks, lands_rs, [tuple(local)], win, _slot, me_arr,
                                                f"scatter_start_l{l}_{gname}")
        rs[(l, gname)] = (sems[0], s_thru, l_thru, win, local)
        return tok[0:1, 0:1]

    loss_tile, grad_x, grads, g_rel = _local_step(x[0], loss_target[0], ws, rel_bias, tok0, start_scatter)

    stack = lambda nm: jnp.stack([grads[l][nm] for l in range(DEPTH)], axis=0)
    small_g = {nm: (g_rel if nm == "rel_bias" else stack(nm)) for nm, _, _, _ in ROWPACK}
    small_repl = jnp.concatenate([_rowpack(small_g, ROWPACK[:N_REPL]), loss_tile], axis=0)
    small_shard = _shard_rows(small_g)

    out_g, out_d, out_m, out_v = {}, {}, {}, {}
    prev = {nm: None for nm, _, _ in BIG}
    todo = [(l, gname) for l in reversed(range(DEPTH)) for gname in ("ffn", "mix", "in")]
    after, small_parts = grad_x, None
    for l, gname in todo:
        if (l, gname) == todo[-1]:
            small_parts = _exchange(
                [small_repl, small_shard],
                [jax.ShapeDtypeStruct((NDEV, ROWS_REPL + 8, LANES), F32), jax.ShapeDtypeStruct((NDEV, ROWS_SHARD, LANES), F32)],
                lambda t, ref, q: ref if t == 0 else ref.at[q], _slot, "exchange_small_grads", after=after)
            after = small_parts[0]
        sems, s_thru, l_thru, win, local = rs[(l, gname)]
        owns, parts = _xchg_wait(sems, s_thru, l_thru, local, after, win, _slot, f"scatter_wait_l{l}_{gname}")
        for t, own, prt in zip(group_tids[gname], owns, parts):
            nm = BIG[t][0]
            rows = SHARD_ROWS.get(nm, PB[nm].shape[1])
            prev[nm] = _adamw(t, prt, own, me_arr, PB[nm], MB[nm], VB[nm], l, prev[nm], rows, f"adamw_{nm}_l{l}")
            after = prev[nm][1]
    for nm, _, _ in BIG:
        out_g[nm], out_d[nm], out_m[nm], out_v[nm] = [tr(a) if nm == "w_in" else a for a in prev[nm]]
    sm_g, sm_d, sm_m, sm_v, loss_all = _small_update(small_parts[0], small_parts[1], _rowpack(P), _rowpack(M),
                                                     _rowpack(V), "small_update")
    loss = loss_all[0, 0]
    for dst, src in ((out_g, sm_g), (out_d, sm_d), (out_m, sm_m), (out_v, sm_v)):
        dst.update(src)

    order = ["rel_bias", "attn_pre_norm", "w_in", "b_gate", "sinks", "w_br_a", "w_br_b", "w_br_c", "w_out",
             "attn_post_norm", "ffn_pre_norm", "w_up", "conv_w", "conv_b", "w_down", "ffn_post_norm"]
    return (loss, grad_x[None], *[out_g[k] for k in order], *[out_d[k] for k in order],
            *[out_m[k] for k in order], *[out_v[k] for k in order])
```

```python
import functools
import math

import numpy as np
import jax
import jax.numpy as jnp
from jax import lax
from jax.experimental import pallas as pl
from jax.experimental.pallas import tpu as pltpu

F32 = jnp.float32
BF16 = jnp.bfloat16

S = 2048
D = 1024
DEPTH = 2
NDEV = 8
HD = 64
BLK = 128
NB = S // BLK
A_GROUPS = ((128, 1), (512, 4), (2048, 16))
NUM_BUCKETS = 32
MAX_DISTANCE = 2048
N_BIAS_HEADS = 20
D_FF = 4096
IN_COLS = 6912
QKV_COLS = 3840
QKV_SLABS = QKV_COLS // 128
GATE_COLS = 3072
EPS = 1e-6
SCALE = HD ** -0.5
NEG = -1e30
LANES = 128

ADAM_LR = 0.001
ADAM_B1 = 0.9
ADAM_B2 = 0.999
ADAM_EPS = 1e-08
ADAM_WD = 0.01
ADAM_STEP = 10

VMEM_LIMIT = 56 * 1024 * 1024
MESH = pl.DeviceIdType.MESH
ANY = pl.BlockSpec(memory_space=pl.ANY)
SMEM = pl.BlockSpec(memory_space=pltpu.SMEM)


def _cp(*sem):
    return pltpu.CompilerParams(dimension_semantics=sem if sem else None, vmem_limit_bytes=VMEM_LIMIT)


def _dot(a, b, ca, cb):
    return lax.dot_general(a, b, (((ca,), (cb,)), ((), ())), preferred_element_type=F32)


def _mm(a, b, *, grid, a_spec, b_spec, out_shape, out_spec, ca, cb, acc_shape, name,
        a_slab=False, b_slab=False, out_slab=False, alias_out=None, after=None):
    nk = grid[2]

    def body(*refs):
        a_ref, b_ref = refs[0], refs[1]
        o_ref, acc_ref = refs[-2], refs[-1]
        k = pl.program_id(2)

        def load(ref, slab):
            if slab:
                return jnp.concatenate([ref[s] for s in range(ref.shape[0])], axis=1).astype(BF16)
            return ref[...].astype(BF16)

        def write(val):
            if out_slab:
                for s in range(o_ref.shape[0]):
                    o_ref[s] = val[:, s * LANES:(s + 1) * LANES].astype(o_ref.dtype)
            else:
                o_ref[...] = val.astype(o_ref.dtype)

        d = _dot(load(a_ref, a_slab), load(b_ref, b_slab), ca, cb)
        if nk == 1:
            write(d)
        elif direct:
            @pl.when(k == 0)
            def _():
                o_ref[...] = d

            @pl.when(k > 0)
            def _():
                o_ref[...] += d
        else:
            @pl.when(k == 0)
            def _():
                acc_ref[...] = d

            if nk > 2:
                @pl.when((k > 0) & (k < nk - 1))
                def _():
                    acc_ref[...] += d

            @pl.when(k == nk - 1)
            def _():
                write(acc_ref[...] + d)

    direct = (not out_slab) and out_shape.dtype == F32
    if nk == 1 or direct:
        acc_shape = (8, LANES)
    in_specs = [a_spec, b_spec]
    args = [a, b]
    aliases = {}
    if alias_out is not None:
        in_specs.append(ANY)
        args.append(alias_out)
        aliases = {2: 0}
    if after is not None:
        in_specs.append(ANY)
        args.append(after)
    return pl.pallas_call(
        body, grid=grid, in_specs=in_specs, out_specs=out_spec, out_shape=out_shape,
        scratch_shapes=[pltpu.VMEM(acc_shape, F32)], input_output_aliases=aliases,
        compiler_params=_cp("parallel", "parallel", "arbitrary"), name=name)(*args)


def _mm_nn(a, b, out_dtype, tm, tn, tk, name):
    m, kk = a.shape
    n = b.shape[1]
    return _mm(a, b, grid=(m // tm, n // tn, kk // tk),
               a_spec=pl.BlockSpec((tm, tk), lambda i, j, k: (i, k)),
               b_spec=pl.BlockSpec((tk, tn), lambda i, j, k: (k, j)),
               out_shape=jax.ShapeDtypeStruct((m, n), out_dtype),
               out_spec=pl.BlockSpec((tm, tn), lambda i, j, k: (i, j)),
               ca=1, cb=0, acc_shape=(tm, tn), name=name)


def _mm_nt(a, b, out_dtype, tm, tn, tk, name):
    m, kk = a.shape
    n = b.shape[0]
    return _mm(a, b, grid=(m // tm, n // tn, kk // tk),
               a_spec=pl.BlockSpec((tm, tk), lambda i, j, k: (i, k)),
               b_spec=pl.BlockSpec((tn, tk), lambda i, j, k: (j, k)),
               out_shape=jax.ShapeDtypeStruct((m, n), out_dtype),
               out_spec=pl.BlockSpec((tm, tn), lambda i, j, k: (i, j)),
               ca=1, cb=1, acc_shape=(tm, tn), name=name)


def _mm_tn(a, b, out_dtype, tm, tn, tk, name):
    kk, m = a.shape
    n = b.shape[1]
    return _mm(a, b, grid=(m // tm, n // tn, kk // tk),
               a_spec=pl.BlockSpec((tk, tm), lambda i, j, k: (k, i)),
               b_spec=pl.BlockSpec((tk, tn), lambda i, j, k: (k, j)),
               out_shape=jax.ShapeDtypeStruct((m, n), out_dtype),
               out_spec=pl.BlockSpec((tm, tn), lambda i, j, k: (i, j)),
               ca=0, cb=0, acc_shape=(tm, tn), name=name)


ROW_TILE = 512
MERGE_TILE = 256


def _rms(x, g):
    r = lax.rsqrt(jnp.mean(x * x, axis=-1, keepdims=True) + EPS)
    return x * r * g


def _prenorm(x, g, name):
    def body(x_ref, g_ref, o_ref):
        o_ref[...] = _rms(x_ref[...], g_ref[...]).astype(BF16)

    return pl.pallas_call(
        body, grid=(S // ROW_TILE,),
        in_specs=[pl.BlockSpec((ROW_TILE, D), lambda i: (i, 0)), pl.BlockSpec((1, D), lambda i: (0, 0))],
        out_specs=pl.BlockSpec((ROW_TILE, D), lambda i: (i, 0)),
        out_shape=jax.ShapeDtypeStruct((S, D), BF16), compiler_params=_cp("parallel"), name=name)(x, g)


def _postnorm_res(x, f, g_post, g_next, name):
    def body(x_ref, f_ref, gp_ref, gn_ref, xo_ref, ho_ref):
        xn = x_ref[...] + _rms(f_ref[...], gp_ref[...])
        xo_ref[...] = xn
        ho_ref[...] = _rms(xn, gn_ref[...]).astype(BF16)

    row = pl.BlockSpec((ROW_TILE, D), lambda i: (i, 0))
    vec = pl.BlockSpec((1, D), lambda i: (0, 0))
    return pl.pallas_call(
        body, grid=(S // ROW_TILE,), in_specs=[row, row, vec, vec], out_specs=[row, row],
        out_shape=[jax.ShapeDtypeStruct((S, D), F32), jax.ShapeDtypeStruct((S, D), BF16)],
        compiler_params=_cp("parallel"), name=name)(x, f, g_post, g_next)


def _norm_bwd(f, g, dys, res, out_dtype, name):
    ndy = len(dys)
    has_res = res is not None

    def body(*refs):
        f_ref, g_ref = refs[0], refs[1]
        dy_refs = refs[2:2 + ndy]
        res_ref = refs[2 + ndy] if has_res else None
        o_ref, dg_ref = refs[-2], refs[-1]
        fv = f_ref[...]
        dy = dy_refs[0][...].astype(F32)
        for r in dy_refs[1:]:
            dy = dy + r[...].astype(F32)
        r = lax.rsqrt(jnp.mean(fv * fv, axis=-1, keepdims=True) + EPS)
        n = fv * r
        dn = dy * g_ref[...]
        df = r * (dn - n * jnp.mean(dn * n, axis=-1, keepdims=True))
        if has_res:
            df = df + res_ref[...]
        o_ref[...] = df.astype(out_dtype)

        @pl.when(pl.program_id(0) == 0)
        def _():
            dg_ref[...] = jnp.zeros((1, D), F32)

        dg_ref[...] += jnp.sum(dy * n, axis=0, keepdims=True)

    row = pl.BlockSpec((ROW_TILE, D), lambda i: (i, 0))
    vec = pl.BlockSpec((1, D), lambda i: (0, 0))
    in_specs = [row, vec] + [row] * ndy + ([row] if has_res else [])
    args = [f, g] + list(dys) + ([res] if has_res else [])
    return pl.pallas_call(
        body, grid=(S // ROW_TILE,), in_specs=in_specs, out_specs=[row, vec],
        out_shape=[jax.ShapeDtypeStruct((S, D), out_dtype), jax.ShapeDtypeStruct((1, D), F32)],
        compiler_params=_cp("arbitrary"), name=name)(*args)


def _rms_bwd_rows(fv, g, dy):
    r = lax.rsqrt(jnp.mean(fv * fv, axis=-1, keepdims=True) + EPS)
    n = fv * r
    dn = dy * g
    return r * (dn - n * jnp.mean(dn * n, axis=-1, keepdims=True)), dy * n


def _norm_bwd_chain(f1, g1, dys, res, f2, g2, name):
    ndy = len(dys)

    def body(*refs):
        f1_ref, g1_ref = refs[0], refs[1]
        dy_refs = refs[2:2 + ndy]
        res_ref, f2_ref, g2_ref = refs[2 + ndy:5 + ndy]
        o1_ref, o2_ref, dg1_ref, dg2_ref = refs[-4:]
        dy = dy_refs[0][...].astype(F32)
        for r in dy_refs[1:]:
            dy = dy + r[...].astype(F32)
        df1, c1 = _rms_bwd_rows(f1_ref[...], g1_ref[...], dy)
        out1 = df1 + res_ref[...]
        o1_ref[...] = out1
        df2, c2 = _rms_bwd_rows(f2_ref[...], g2_ref[...], out1)
        o2_ref[...] = df2.astype(BF16)

        @pl.when(pl.program_id(0) == 0)
        def _():
            dg1_ref[...] = jnp.zeros((1, D), F32)
            dg2_ref[...] = jnp.zeros((1, D), F32)

        dg1_ref[...] += jnp.sum(c1, axis=0, keepdims=True)
        dg2_ref[...] += jnp.sum(c2, axis=0, keepdims=True)

    row = pl.BlockSpec((ROW_TILE, D), lambda i: (i, 0))
    vec = pl.BlockSpec((1, D), lambda i: (0, 0))
    return pl.pallas_call(
        body, grid=(S // ROW_TILE,), in_specs=[row, vec] + [row] * ndy + [row, row, vec],
        out_specs=[row, row, vec, vec],
        out_shape=[jax.ShapeDtypeStruct((S, D), F32), jax.ShapeDtypeStruct((S, D), BF16),
                   jax.ShapeDtypeStruct((1, D), F32), jax.ShapeDtypeStruct((1, D), F32)],
        compiler_params=_cp("arbitrary"), name=name)(f1, g1, *dys, res, f2, g2)


def _loss_head(x, f, g, target, name):
    def body(x_ref, f_ref, g_ref, t_ref, dy_ref, l_ref, df_ref, dg_ref):
        fv = f_ref[...]
        e = x_ref[...] + _rms(fv, g_ref[...]) - t_ref[...]
        dy = e * (1.0 / D)
        dy_ref[...] = dy
        df, c = _rms_bwd_rows(fv, g_ref[...], dy)
        df_ref[...] = df.astype(BF16)

        @pl.when(pl.program_id(0) == 0)
        def _():
            l_ref[...] = jnp.zeros((8, LANES), F32)
            dg_ref[...] = jnp.zeros((1, D), F32)

        l_ref[...] += jnp.sum(e * e) * (0.5 / D)
        dg_ref[...] += jnp.sum(c, axis=0, keepdims=True)

    row = pl.BlockSpec((ROW_TILE, D), lambda i: (i, 0))
    vec = pl.BlockSpec((1, D), lambda i: (0, 0))
    return pl.pallas_call(
        body, grid=(S // ROW_TILE,), in_specs=[row, row, vec, row],
        out_specs=[row, pl.BlockSpec((8, LANES), lambda i: (0, 0)), row, vec],
        out_shape=[jax.ShapeDtypeStruct((S, D), F32), jax.ShapeDtypeStruct((8, LANES), F32),
                   jax.ShapeDtypeStruct((S, D), BF16), jax.ShapeDtypeStruct((1, D), F32)],
        compiler_params=_cp("arbitrary"), name=name)(x, f, g, target)


def _bucket_tiles():
    a = np.arange(BLK)[:, None]
    b = np.arange(2 * BLK)[None, :]
    dist = a + BLK - b
    out = np.zeros((4, 2, BLK, 2 * BLK), np.int32)
    cfg = [(w // d, d) for w, d in A_GROUPS] + [(BLK - 1, 1)]
    for gi, (max_dist, d) in enumerate(cfg):
        band = (dist >= 0) & (dist <= max_dist)
        tok = np.maximum(dist, 0) * d
        nf = np.maximum(tok, 1).astype(np.float32)
        max_exact = NUM_BUCKETS // 2
        large = max_exact + (np.log(nf / np.float32(max_exact)) / np.float32(math.log(MAX_DISTANCE / max_exact))
                             * np.float32(NUM_BUCKETS - max_exact)).astype(np.int32)
        large = np.minimum(large, NUM_BUCKETS - 1)
        bkt = np.where(tok < max_exact, tok, large).astype(np.int32)
        full = np.where(band, bkt, -1)
        out[gi, 1] = full
        out[gi, 0] = np.where(b >= BLK, full, -1)
    return out


def _bias_tiles(rel_bias, buckets, name):
    def body(tab_ref, bkt_ref, o_ref):
        h = pl.program_id(0)
        bkt = bkt_ref[...]
        acc = jnp.zeros(bkt.shape, F32)
        for bb in range(NUM_BUCKETS):
            acc = jnp.where(bkt == bb, tab_ref[bb, h], acc)
        o_ref[...] = jnp.where(bkt < 0, NEG, acc)

    return pl.pallas_call(
        body, grid=(N_BIAS_HEADS,),
        in_specs=[SMEM, pl.BlockSpec((None, 2, BLK, 2 * BLK), lambda h: (jnp.minimum(h // 4, 3), 0, 0, 0))],
        out_specs=pl.BlockSpec((None, 2, BLK, 2 * BLK), lambda h: (h, 0, 0, 0)),
        out_shape=jax.ShapeDtypeStruct((N_BIAS_HEADS, 2, BLK, 2 * BLK), F32),
        compiler_params=_cp("arbitrary"), name=name)(rel_bias, buckets)


def _bias_grad(gs, buckets, name):
    ng = len(gs)

    def body(*refs):
        g_refs = refs[:ng]
        bkt_ref, o_ref = refs[ng], refs[ng + 1]
        h = pl.program_id(0)
        g = g_refs[0][...]
        for r in g_refs[1:]:
            g = g + r[...]
        bkt = bkt_ref[...]
        row = lax.broadcasted_iota(jnp.int32, (NUM_BUCKETS, LANES), 0)
        lane = lax.broadcasted_iota(jnp.int32, (NUM_BUCKETS, LANES), 1)

        @pl.when(h == 0)
        def _():
            o_ref[...] = jnp.zeros((NUM_BUCKETS, LANES), F32)

        acc = o_ref[...]
        for bb in range(NUM_BUCKETS):
            s = jnp.sum(jnp.where(bkt == bb, g, 0.0))
            acc = jnp.where((row == bb) & (lane == h), s, acc)
        o_ref[...] = acc

    g_spec = pl.BlockSpec((None, BLK, 2 * BLK), lambda h: (h, 0, 0))
    return pl.pallas_call(
        body, grid=(N_BIAS_HEADS,),
        in_specs=[g_spec] * ng + [pl.BlockSpec((None, None, BLK, 2 * BLK), lambda h: (jnp.minimum(h // 4, 3), 1, 0, 0))],
        out_specs=pl.BlockSpec((NUM_BUCKETS, LANES), lambda h: (0, 0)),
        out_shape=jax.ShapeDtypeStruct((NUM_BUCKETS, LANES), F32),
        compiler_params=_cp("arbitrary"), name=name)(*gs, buckets)


def _to_class_major(src_ref, dst_refs, d, fn=None):
    ln = S // d
    for r in range(d):
        v = src_ref[pl.ds(r, ln, stride=d), :] if d > 1 else src_ref[...]
        outs = fn(v) if fn is not None else (v,) * len(dst_refs)
        for dst, o in zip(dst_refs, outs):
            dst[pl.ds(r * ln, ln), :] = o.astype(dst.dtype)


def _head_masks(rows):
    lane = lax.broadcasted_iota(jnp.int32, (rows, LANES), 1)
    return lane < HD, lane >= HD


def _split_heads(v):
    m0, m1 = _head_masks(v.shape[0])
    return jnp.where(m0, v, 0.0), jnp.where(m1, v, 0.0)


def _dup_head(v, hi):
    m0, _ = _head_masks(v.shape[0])
    r = pltpu.roll(v, HD, 1)
    return jnp.where(m0, jnp.where(hi, r, v), jnp.where(hi, v, r))


def _block_rows(b, d):
    nbc = NB // d
    i = b % nbc
    r = b // nbc
    has_prev = (i > 0).astype(jnp.int32)
    prev = pl.multiple_of(jnp.maximum(b - 1, 0) * BLK, BLK)
    nat = i * (BLK * d) + r
    return has_prev, prev, nat


def _lane_halves(v0, v1):
    lane = lax.broadcasted_iota(jnp.int32, (v0.shape[0], LANES), 1)
    return jnp.where(lane < HD, v0, v1)


def _band_fwd(proj, bias, sinks, *, d, q0, k0, v0, npairs, bias0, shared_kv, name):
    def body(sink_ref, q_ref, k_ref, v_ref, b_ref, num_ref, st_ref, qz0, qz1, ks, vs):
        p = pl.program_id(0)
        kv = (lambda v: (_dup_head(v, p >= 2),)) if shared_kv else None
        _to_class_major(q_ref, (qz0, qz1), d, lambda v: _split_heads(v * SCALE))
        _to_class_major(k_ref, (ks,), d, kv)
        _to_class_major(v_ref, (vs,), d, kv)
        lane = lax.broadcasted_iota(jnp.int32, (BLK, LANES), 1)

        def blk(b, carry):
            has_prev, prev, nat = _block_rows(b, d)
            cur = pl.multiple_of(b * BLK, BLK)
            k2 = jnp.concatenate([ks[pl.ds(prev, BLK), :], ks[pl.ds(cur, BLK), :]], axis=0)
            v2 = jnp.concatenate([vs[pl.ds(prev, BLK), :], vs[pl.ds(cur, BLK), :]], axis=0)
            nums, ms, ls = [], [], []
            for hh, qz in enumerate((qz0, qz1)):
                z = _dot(qz[pl.ds(cur, BLK), :], k2, 1, 1) + b_ref[hh, has_prev]
                m = jnp.max(z, axis=1, keepdims=True)
                e = jnp.exp(z - m)
                l = jnp.sum(e, axis=1, keepdims=True)
                num = _dot(e.astype(BF16), v2, 1, 0)
                if shared_kv:
                    sink = sink_ref[0, 2 * p + hh]
                    mx = jnp.maximum(m, sink)
                    c = jnp.exp(m - mx)
                    zden = l * c + jnp.exp(sink - mx)
                    num = num * (c / zden)
                    m = mx + jnp.log(zden)
                ls.append(l)
                ms.append(m)
                nums.append(num)
            num_t = jnp.where(lane < HD, nums[0], nums[1])
            if shared_kv:
                st_t = jnp.where(lane < HD, ms[0], ms[1])
            else:
                st_t = jnp.where(lane < 32, ms[0], jnp.where(lane < 64, ls[0], jnp.where(lane < 96, ms[1], ls[1])))
            if d > 1:
                num_ref[pl.ds(nat, BLK, stride=d), :] = num_t
                st_ref[pl.ds(nat, BLK, stride=d), :] = st_t
            else:
                num_ref[pl.ds(cur, BLK), :] = num_t
                st_ref[pl.ds(cur, BLK), :] = st_t
            return carry

        lax.fori_loop(0, NB, blk, 0, unroll=True)

    slab = lambda off, per_pair: pl.BlockSpec((None, S, LANES), (lambda p: (off + p, 0, 0)) if per_pair else (lambda p: (off, 0, 0)))
    out = pl.BlockSpec((None, S, LANES), lambda p: (p, 0, 0))
    return pl.pallas_call(
        body, grid=(npairs,),
        in_specs=[SMEM, slab(q0, True), slab(k0, not shared_kv), slab(v0, not shared_kv),
                  pl.BlockSpec((None, 2, 2, BLK, 2 * BLK), lambda p: (bias0 + p, 0, 0, 0, 0))],
        out_specs=[out, out],
        out_shape=[jax.ShapeDtypeStruct((npairs, S, LANES), F32)] * 2,
        scratch_shapes=[pltpu.VMEM((S, LANES), BF16)] * 4,
        compiler_params=_cp("arbitrary"), name=name)(sinks, proj, proj, proj, bias)


def _combine_a(nums, stats, name):
    rt = 512

    def body(n0, n1, n2, s0, s1, s2, o_ref, l_ref):
        n_refs, s_refs = (n0, n1, n2), (s0, s1, s2)
        outs, lses = [], []
        for hh in range(2):
            ms = [s[:, 64 * hh:64 * hh + 1] for s in s_refs]
            ls = [s[:, 64 * hh + 32:64 * hh + 33] for s in s_refs]
            mx = jnp.maximum(jnp.maximum(ms[0], ms[1]), ms[2])
            cs = [jnp.exp(m - mx) for m in ms]
            z = cs[0] * ls[0] + cs[1] * ls[1] + cs[2] * ls[2]
            acc = cs[0] * n_refs[0][:, hh * HD:(hh + 1) * HD]
            acc = acc + cs[1] * n_refs[1][:, hh * HD:(hh + 1) * HD]
            acc = acc + cs[2] * n_refs[2][:, hh * HD:(hh + 1) * HD]
            outs.append(acc / z)
            lses.append(mx + jnp.log(z))
        o_ref[...] = jnp.concatenate(outs, axis=1)
        l_ref[...] = _lane_halves(lses[0], lses[1])

    spec = pl.BlockSpec((None, rt, LANES), lambda p, i: (p, i, 0))
    return pl.pallas_call(
        body, grid=(2, S // rt), in_specs=[spec] * 6, out_specs=[spec, spec],
        out_shape=[jax.ShapeDtypeStruct((2, S, LANES), F32)] * 2,
        compiler_params=_cp("parallel", "parallel"), name=name)(*nums, *stats)


def _band_bwd(proj, bias, o, do, lse, sinks, dqkv, *, d, q0, k0, v0, npairs, bias0, shared_kv, name):
    def body(sink_ref, q_ref, k_ref, v_ref, b_ref, o_ref, do_ref, lse_ref, dqkv_in, dqkv_ref, g_ref, ds_ref,
             qz0, qz1, ks, vs, doz0, doz1, ls0, ls1, dls0, dls1, stage, dq_nat, dk_cm, dv_cm, kv_nat, dk_acc, dv_acc,
             obuf, osem):
        p = pl.program_id(0)
        dq_ref, dk_ref, dv_ref = obuf.at[0], obuf.at[1], obuf.at[2]
        m0, m1 = _head_masks(S)
        kk = lax.broadcasted_iota(jnp.int32, (2 * LANES, LANES), 0) % LANES
        ll = lax.broadcasted_iota(jnp.int32, (2 * LANES, LANES), 1)
        hi, lo = _split2(do_ref[...] * o_ref[...])
        dl = _dot(jnp.concatenate([hi, lo], axis=1), ((kk < HD) == (ll < HD)).astype(BF16), 1, 0)
        if shared_kv:
            row8 = lax.broadcasted_iota(jnp.int32, (8, LANES), 0)
            lane8 = lax.broadcasted_iota(jnp.int32, (8, LANES), 1)
            sinkv = jnp.where(m0, sink_ref[0, 2 * p], sink_ref[0, 2 * p + 1])
            contrib = jnp.exp(sinkv - lse_ref[...]) * dl
            t = jnp.zeros((8, LANES), F32)
            for hh, mh in enumerate((m0, m1)):
                dsink = -jnp.sum(jnp.where(mh, contrib, 0.0)) * (1.0 / HD)
                t = jnp.where((row8 == 0) & (lane8 == hh), dsink, t)
            ds_ref[...] = t
        else:
            ds_ref[...] = jnp.zeros((8, LANES), F32)
        kv = (lambda v: (_dup_head(v, p >= 2),)) if shared_kv else None
        _to_class_major(q_ref, (qz0, qz1), d, lambda v: _split_heads(v * SCALE))
        _to_class_major(k_ref, (ks,), d, kv)
        _to_class_major(v_ref, (vs,), d, kv)
        _to_class_major(do_ref, (doz0, doz1), d, _split_heads)
        def spread(v):
            a0, a1 = _head_masks(v.shape[0])
            r = pltpu.roll(v, HD, 1)
            return jnp.where(a0, v, r), jnp.where(a1, v, r)

        _to_class_major(lse_ref, (ls0, ls1), d, spread)
        stage[...] = dl
        _to_class_major(stage, (dls0, dls1), d, spread)

        dk_cm[...] = jnp.zeros((S, LANES), F32)
        dv_cm[...] = jnp.zeros((S, LANES), F32)
        g_ref[...] = jnp.zeros((2, BLK, 2 * BLK), F32)
        lane = lax.broadcasted_iota(jnp.int32, (BLK, LANES), 1)

        def blk(b, carry):
            has_prev, prev, nat = _block_rows(b, d)
            cur = pl.multiple_of(b * BLK, BLK)
            k2 = jnp.concatenate([ks[pl.ds(prev, BLK), :], ks[pl.ds(cur, BLK), :]], axis=0)
            v2 = jnp.concatenate([vs[pl.ds(prev, BLK), :], vs[pl.ds(cur, BLK), :]], axis=0)
            dqs, dks, dvs = [], [], []
            for hh, (qz, doz, lsr, dlr) in enumerate(((qz0, doz0, ls0, dls0), (qz1, doz1, ls1, dls1))):
                qb = qz[pl.ds(cur, BLK), :]
                dob = doz[pl.ds(cur, BLK), :]
                lb = lsr[pl.ds(cur, BLK), :]
                dlb = dlr[pl.ds(cur, BLK), :]
                z = _dot(qb, k2, 1, 1) + b_ref[hh, has_prev]
                pr = jnp.exp(z - jnp.concatenate([lb, lb], axis=1))
                dp = _dot(dob, v2, 1, 1)
                dz = pr * (dp - jnp.concatenate([dlb, dlb], axis=1))
                g_ref[hh] += dz
                dzb = dz.astype(BF16)
                dqs.append(_dot(dzb, k2, 1, 0))
                dks.append(_dot(dzb, qb, 0, 0))
                dvs.append(_dot(pr.astype(BF16), dob, 0, 0))
            dq_t = jnp.where(lane < HD, dqs[0], dqs[1]) * SCALE
            dk_t = dks[0] + dks[1]
            dv_t = dvs[0] + dvs[1]
            dk_cm[pl.ds(prev, BLK), :] += dk_t[:BLK]
            dk_cm[pl.ds(cur, BLK), :] += dk_t[BLK:]
            dv_cm[pl.ds(prev, BLK), :] += dv_t[:BLK]
            dv_cm[pl.ds(cur, BLK), :] += dv_t[BLK:]
            if d > 1:
                dq_nat[pl.ds(nat, BLK, stride=d), :] = dq_t
            else:
                dq_nat[pl.ds(cur, BLK), :] = dq_t
            return carry

        lax.fori_loop(0, NB, blk, 0, unroll=True)
        dq_ref[...] = dq_nat[...].astype(BF16)

        def from_class_major(src, dst_ref):
            if d == 1:
                dst_ref[...] = src[...].astype(BF16)
            else:
                ln = S // d
                for r in range(d):
                    kv_nat[pl.ds(r, ln, stride=d), :] = src[pl.ds(r * ln, ln), :]
                dst_ref[...] = kv_nat[...].astype(BF16)

        def put(i, slab):
            return pltpu.make_async_copy(obuf.at[i], dqkv_ref.at[slab], osem.at[i])

        put(0, q0 + p).start()
        if not shared_kv:
            from_class_major(dk_cm, dk_ref)
            from_class_major(dv_cm, dv_ref)
            put(1, k0 + p).start()
            put(2, v0 + p).start()
            put(1, k0 + p).wait()
            put(2, v0 + p).wait()
        else:
            @pl.when(p == 0)
            def _():
                dk_acc[...] = jnp.zeros((S, LANES), F32)
                dv_acc[...] = jnp.zeros((S, LANES), F32)

            mine = m1 == (p >= 2)
            for cm, acc in ((dk_cm, dk_acc), (dv_cm, dv_acc)):
                val = cm[...]
                acc[...] += jnp.where(mine, val + pltpu.roll(val, HD, 1), 0.0)

            @pl.when(p == npairs - 1)
            def _():
                from_class_major(dk_acc, dk_ref)
                from_class_major(dv_acc, dv_ref)
                put(1, k0).start()
                put(2, v0).start()
                put(1, k0).wait()
                put(2, v0).wait()

        put(0, q0 + p).wait()

    slab = lambda off, per_pair: pl.BlockSpec((None, S, LANES), (lambda p: (off + p, 0, 0)) if per_pair else (lambda p: (off, 0, 0)))
    pair = pl.BlockSpec((None, S, LANES), lambda p: (p, 0, 0))
    return pl.pallas_call(
        body, grid=(npairs,),
        in_specs=[SMEM, slab(q0, True), slab(k0, not shared_kv), slab(v0, not shared_kv),
                  pl.BlockSpec((None, 2, 2, BLK, 2 * BLK), lambda p: (bias0 + p, 0, 0, 0, 0)),
                  pair, pair, pair, ANY],
        out_specs=[ANY,
                   pl.BlockSpec((None, 2, BLK, 2 * BLK), lambda p: (p, 0, 0, 0)),
                   pl.BlockSpec((None, 8, LANES), lambda p: (p, 0, 0))],
        out_shape=[jax.ShapeDtypeStruct(dqkv.shape, BF16),
                   jax.ShapeDtypeStruct((npairs, 2, BLK, 2 * BLK), F32),
                   jax.ShapeDtypeStruct((npairs, 8, LANES), F32)],
        scratch_shapes=[pltpu.VMEM((S, LANES), BF16)] * 6 + [pltpu.VMEM((S, LANES), F32)] * 11
        + [pltpu.VMEM((3, S, LANES), BF16), pltpu.SemaphoreType.DMA((3,))],
        input_output_aliases={8: 0},
        compiler_params=_cp("arbitrary"), name=name)(sinks, proj, proj, proj, bias, o, do, lse, dqkv)


KC = 512
NSUB = KC // BLK
QB = 512
QPG = KC // QB


def _split2(x):
    hi = x.astype(BF16)
    lo = (x - hi.astype(F32)).astype(BF16)
    return hi, lo


def _tri_ones(cmp):
    jj = lax.broadcasted_iota(jnp.int32, (2 * BLK, BLK), 0) % BLK
    ss = lax.broadcasted_iota(jnp.int32, (2 * BLK, BLK), 1)
    return jnp.concatenate([cmp(jj, ss).astype(BF16), jnp.ones((2 * BLK, BLK), BF16)], axis=1)


def _sub_sums(x, tri1):
    n = x.shape[0]
    st = jnp.concatenate([x[:, s * BLK:(s + 1) * BLK] for s in range(NSUB)], axis=0)
    hi, lo = _split2(st)
    r = _dot(jnp.concatenate([hi, lo], axis=1), tri1, 1, 0)
    return ([r[s * n:(s + 1) * n, :BLK] for s in range(NSUB)], [r[s * n:(s + 1) * n, BLK:] for s in range(NSUB)])


def _log_sig_pair(z):
    lb = jnp.minimum(z, 0.0) - jnp.log1p(jnp.exp(-jnp.abs(z)))
    return lb, lb - z


QGROUPS = NB // NSUB


def _stick_fwd(proj, *, q0, k0, v0, name):
    def body(q_ref, k_ref, v_ref, o_ref, t_ref, qs, ks, vs):
        qs[...] = (q_ref[...] * SCALE).astype(BF16)
        ks[...] = k_ref[...].astype(BF16)
        vs[...] = v_ref[...].astype(BF16)
        tri1 = _tri_ones(lambda j, s: j > s)
        col = lax.broadcasted_iota(jnp.int32, (QB, KC), 1)
        rowi = lax.broadcasted_iota(jnp.int32, (QB, KC), 0)

        for qg in range(QGROUPS):
            def qblock(ii, carry0, qg=qg):
                t0 = pl.multiple_of((qg * QPG + ii) * QB, QB)
                qb = qs[pl.ds(t0, QB), :]
                accs = [jnp.zeros((QB, HD), F32)] * 2
                runs = [jnp.zeros((QB, BLK), F32)] * 2
                for c in reversed(range(qg + 1)):
                    s0 = c * KC
                    diag = c == qg
                    before = (s0 + col) < (t0 + rowi) if diag else None
                    for hh in range(2):
                        kh = ks[s0:s0 + KC, hh * HD:(hh + 1) * HD]
                        vh = vs[s0:s0 + KC, hh * HD:(hh + 1) * HD]
                        lb, lk = _log_sig_pair(_dot(qb[:, hh * HD:(hh + 1) * HD], kh, 1, 1))
                        if diag:
                            lk = jnp.where(before, lk, 0.0)
                        suf, tot = _sub_sums(lk, tri1)
                        ws, run = [], runs[hh]
                        for s in reversed(range(NSUB)):
                            ws.append(jnp.exp(lb[:, s * BLK:(s + 1) * BLK] + suf[s] + run))
                            run = run + tot[s]
                        w = jnp.concatenate(ws[::-1], axis=1)
                        if diag:
                            w = jnp.where(before, w, 0.0)
                        accs[hh] = accs[hh] + _dot(w.astype(BF16), vh, 1, 0)
                        runs[hh] = run
                o_ref[pl.ds(t0, QB), :] = jnp.concatenate(accs, axis=1)
                t_ref[pl.ds(t0, QB), :] = _lane_halves(runs[0], runs[1])
                return carry0

            lax.fori_loop(0, QPG, qblock, 0)

    slab = lambda off: pl.BlockSpec((None, S, LANES), lambda p: (off + p, 0, 0))
    out = pl.BlockSpec((None, S, LANES), lambda p: (p, 0, 0))
    return pl.pallas_call(
        body, grid=(2,), in_specs=[slab(q0), slab(k0), slab(v0)], out_specs=[out, out],
        out_shape=[jax.ShapeDtypeStruct((2, S, LANES), F32)] * 2,
        scratch_shapes=[pltpu.VMEM((S, LANES), BF16)] * 3,
        compiler_params=_cp("arbitrary"), name=name)(proj, proj, proj)


def _stick_bwd(proj, do, tot, dqkv, *, q0, k0, v0, name):
    def body(q_ref, k_ref, v_ref, do_ref, t_ref, dqkv_in, dqkv_ref, qs, ks, vs, dos, dk_acc, dv_acc, obuf, osem):
        p = pl.program_id(0)
        dq_ref, dk_ref, dv_ref = obuf.at[0], obuf.at[1], obuf.at[2]
        qs[...] = (q_ref[...] * SCALE).astype(BF16)
        ks[...] = k_ref[...].astype(BF16)
        vs[...] = v_ref[...].astype(BF16)
        dos[...] = do_ref[...].astype(BF16)
        dk_acc[...] = jnp.zeros((2, S, HD), F32)
        dv_acc[...] = jnp.zeros((2, S, HD), F32)
        tri_inc = _tri_ones(lambda j, s: j <= s)
        tri_exc = _tri_ones(lambda j, s: j < s)
        col = lax.broadcasted_iota(jnp.int32, (QB, KC), 1)
        rowi = lax.broadcasted_iota(jnp.int32, (QB, KC), 0)

        for qg in range(QGROUPS):
            def qblock(ii, carry0, qg=qg):
                t0 = pl.multiple_of((qg * QPG + ii) * QB, QB)
                qb = qs[pl.ds(t0, QB), :]
                dob = dos[pl.ds(t0, QB), :]
                tb = t_ref[pl.ds(t0, QB), :]
                dqs = [jnp.zeros((QB, HD), F32)] * 2
                pruns = [jnp.zeros((QB, BLK), F32)] * 2
                eruns = [jnp.zeros((QB, BLK), F32)] * 2
                for c in range(qg + 1):
                    s0 = c * KC
                    diag = c == qg
                    before = (s0 + col) < (t0 + rowi) if diag else None
                    for hh in range(2):
                        qh = qb[:, hh * HD:(hh + 1) * HD]
                        doh = dob[:, hh * HD:(hh + 1) * HD]
                        tt = tb[:, 64 * hh:64 * hh + 1]
                        kh = ks[s0:s0 + KC, hh * HD:(hh + 1) * HD]
                        vh = vs[s0:s0 + KC, hh * HD:(hh + 1) * HD]
                        lb, lk = _log_sig_pair(_dot(qh, kh, 1, 1))
                        if diag:
                            lk = jnp.where(before, lk, 0.0)
                        pin, ptot = _sub_sums(lk, tri_inc)
                        ws, prun = [], pruns[hh]
                        for s in range(NSUB):
                            ws.append(jnp.exp(lb[:, s * BLK:(s + 1) * BLK] + (tt - (pin[s] + prun))))
                            prun = prun + ptot[s]
                        w = jnp.concatenate(ws, axis=1)
                        if diag:
                            w = jnp.where(before, w, 0.0)
                        e = w * _dot(doh, vh, 1, 1)
                        pex, etot = _sub_sums(e, tri_exc)
                        cs, erun = [], eruns[hh]
                        for s in range(NSUB):
                            cs.append(pex[s] + erun)
                            erun = erun + etot[s]
                        sig = jnp.exp(lb)
                        dz = e * (1.0 - sig) - jnp.concatenate(cs, axis=1) * sig
                        if diag:
                            dz = jnp.where(before, dz, 0.0)
                        dz = dz.astype(BF16)
                        dqs[hh] = dqs[hh] + _dot(dz, kh, 1, 0)
                        dk_acc[hh, s0:s0 + KC, :] += _dot(dz, qh, 0, 0)
                        dv_acc[hh, s0:s0 + KC, :] += _dot(w.astype(BF16), doh, 0, 0)
                        pruns[hh], eruns[hh] = prun, erun
                dq_ref[pl.ds(t0, QB), :] = (jnp.concatenate(dqs, axis=1) * SCALE).astype(BF16)
                return carry0

            lax.fori_loop(0, QPG, qblock, 0)
        dk_ref[...] = jnp.concatenate([dk_acc[0], dk_acc[1]], axis=1).astype(BF16)
        dv_ref[...] = jnp.concatenate([dv_acc[0], dv_acc[1]], axis=1).astype(BF16)
        puts = [pltpu.make_async_copy(obuf.at[i], dqkv_ref.at[off + p], osem.at[i]) for i, off in enumerate((q0, k0, v0))]
        for cp in puts:
            cp.start()
        for cp in puts:
            cp.wait()

    slab = lambda off: pl.BlockSpec((None, S, LANES), lambda p: (off + p, 0, 0))
    pair = pl.BlockSpec((None, S, LANES), lambda p: (p, 0, 0))
    return pl.pallas_call(
        body, grid=(2,), in_specs=[slab(q0), slab(k0), slab(v0), pair, pair, ANY], out_specs=ANY,
        out_shape=jax.ShapeDtypeStruct(dqkv.shape, BF16),
        scratch_shapes=[pltpu.VMEM((S, LANES), BF16)] * 4 + [pltpu.VMEM((2, S, HD), F32)] * 2
        + [pltpu.VMEM((3, S, LANES), BF16), pltpu.SemaphoreType.DMA((3,))],
        input_output_aliases={5: 0},
        compiler_params=_cp("arbitrary"), name=name)(proj, proj, proj, do, tot, dqkv)


def _cat_slabs(ref):
    return jnp.concatenate([ref[s] for s in range(ref.shape[0])], axis=1)


def _merge_fwd(o_a, o_b, o_c, gates, b_gate, wa, wb, wc, w_out, name):
    tm = MERGE_TILE

    def body(oa_ref, ob_ref, oc_ref, g_ref, bg_ref, wa_ref, wb_ref, wc_ref, wo_ref, mg_ref, mo_ref):
        acc = jnp.zeros((tm, D), F32)
        for i, (o_ref, w_ref) in enumerate(((oa_ref, wa_ref), (ob_ref, wb_ref), (oc_ref, wc_ref))):
            pr = _dot(_cat_slabs(o_ref).astype(BF16), w_ref[...], 1, 0)
            sg = jax.nn.sigmoid(g_ref[:, i * D:(i + 1) * D] + bg_ref[i:i + 1, :])
            acc = acc + sg * pr
        mg = acc.astype(BF16)
        mg_ref[...] = mg
        mo_ref[...] = _dot(mg, wo_ref[...], 1, 0)

    slabs = lambda n: pl.BlockSpec((n, tm, LANES), lambda i: (0, i, 0))
    full = lambda r, c: pl.BlockSpec((r, c), lambda i: (0, 0))
    row = pl.BlockSpec((tm, D), lambda i: (i, 0))
    return pl.pallas_call(
        body, grid=(S // tm,),
        in_specs=[slabs(2), slabs(4), slabs(2), pl.BlockSpec((tm, GATE_COLS), lambda i: (i, 0)), full(3, D),
                  full(256, D), full(512, D), full(256, D), full(D, D)],
        out_specs=[row, row],
        out_shape=[jax.ShapeDtypeStruct((S, D), BF16), jax.ShapeDtypeStruct((S, D), F32)],
        compiler_params=_cp("parallel"), name=name)(o_a, o_b, o_c, gates, b_gate, wa, wb, wc, w_out)


def _merge_bwd(d_mo, o_a, o_b, o_c, gates, b_gate, wa, wb, wc, w_out, name):
    tm = MERGE_TILE
    nsteps = S // tm

    def body(dmo_ref, oa_ref, ob_ref, oc_ref, g_ref, bg_ref, wa_ref, wb_ref, wc_ref, wo_ref,
             doa_ref, dob_ref, doc_ref, dg_ref, dwa_ref, dwb_ref, dwc_ref, dbg_ref, acc_a, acc_b, acc_c):
        @pl.when(pl.program_id(0) == 0)
        def _():
            acc_a[...] = jnp.zeros(acc_a.shape, F32)
            acc_b[...] = jnp.zeros(acc_b.shape, F32)
            acc_c[...] = jnp.zeros(acc_c.shape, F32)
            dbg_ref[...] = jnp.zeros(dbg_ref.shape, F32)

        dmg = _dot(dmo_ref[...], wo_ref[...], 1, 1)
        trip = ((oa_ref, wa_ref, doa_ref, acc_a), (ob_ref, wb_ref, dob_ref, acc_b), (oc_ref, wc_ref, doc_ref, acc_c))
        for i, (o_ref, w_ref, do_ref, dw_ref) in enumerate(trip):
            ob = _cat_slabs(o_ref).astype(BF16)
            pr = _dot(ob, w_ref[...], 1, 0)
            sg = jax.nn.sigmoid(g_ref[:, i * D:(i + 1) * D] + bg_ref[i:i + 1, :])
            dgate = dmg * pr * sg * (1.0 - sg)
            dg_ref[:, i * D:(i + 1) * D] = dgate.astype(BF16)
            dbg_ref[i:i + 1, :] += jnp.sum(dgate, axis=0, keepdims=True)
            dpr = (dmg * sg).astype(BF16)
            do = _dot(dpr, w_ref[...], 1, 1)
            for s in range(do_ref.shape[0]):
                do_ref[s] = do[:, s * LANES:(s + 1) * LANES]
            dw_ref[...] += _dot(ob, dpr, 0, 0)

        @pl.when(pl.program_id(0) == nsteps - 1)
        def _():
            dwa_ref[...] = acc_a[...].astype(BF16)
            dwb_ref[...] = acc_b[...].astype(BF16)
            dwc_ref[...] = acc_c[...].astype(BF16)

    slabs = lambda n: pl.BlockSpec((n, tm, LANES), lambda i: (0, i, 0))
    full = lambda r, c: pl.BlockSpec((r, c), lambda i: (0, 0))
    row = pl.BlockSpec((tm, D), lambda i: (i, 0))
    return pl.pallas_call(
        body, grid=(S // tm,),
        in_specs=[row, slabs(2), slabs(4), slabs(2), pl.BlockSpec((tm, GATE_COLS), lambda i: (i, 0)), full(3, D),
                  full(256, D), full(512, D), full(256, D), full(D, D)],
        out_specs=[slabs(2), slabs(4), slabs(2), pl.BlockSpec((tm, GATE_COLS), lambda i: (i, 0)),
                   full(256, D), full(512, D), full(256, D), full(3, D)],
        out_shape=[jax.ShapeDtypeStruct((2, S, LANES), F32), jax.ShapeDtypeStruct((4, S, LANES), F32),
                   jax.ShapeDtypeStruct((2, S, LANES), F32), jax.ShapeDtypeStruct((S, GATE_COLS), BF16),
                   jax.ShapeDtypeStruct((256, D), BF16), jax.ShapeDtypeStruct((512, D), BF16),
                   jax.ShapeDtypeStruct((256, D), BF16), jax.ShapeDtypeStruct((3, D), F32)],
        scratch_shapes=[pltpu.VMEM((256, D), F32), pltpu.VMEM((512, D), F32), pltpu.VMEM((256, D), F32)],
        compiler_params=_cp("arbitrary"), name=name)(d_mo, o_a, o_b, o_c, gates, b_gate, wa, wb, wc, w_out)


FC = 256
GELU_K = math.sqrt(2.0 / math.pi)
GELU_C = 0.044715


RC = 64
NRC = S // RC


def _down(tail, cur, n):
    row = lax.broadcasted_iota(jnp.int32, tail.shape, 0)
    rolled = pltpu.roll(cur, n, 0)
    first = jnp.where(row < n, pltpu.roll(tail, n, 0), rolled[0:8])
    return jnp.concatenate([first, rolled[8:]], axis=0)


def _up(cur, head, n):
    row = lax.broadcasted_iota(jnp.int32, head.shape, 0)
    rolled = pltpu.roll(cur, RC - n, 0)
    last = jnp.where(row >= 8 - n, pltpu.roll(head, 8 - n, 0), rolled[RC - 8:])
    return jnp.concatenate([rolled[:RC - 8], last], axis=0)


def _conv_chunk(load, j, w_ref, b_ref, half):
    r0 = pl.multiple_of(j * RC, RC)
    cur = load(r0, RC).astype(F32)
    tail = load(pl.multiple_of(jnp.maximum(r0 - 16, 0), 16), 16).astype(F32)[8:16]
    tail = jnp.where(j > 0, tail, 0.0)
    d1 = _down(tail, cur, 1)
    d2 = _down(tail, cur, 2)
    y = w_ref[0:1, half, :] * d2 + w_ref[1:2, half, :] * d1 + w_ref[2:3, half, :] * cur + b_ref[half:half + 1, :]
    return y, cur, d1, d2


def _chunk(j):
    return pl.ds(pl.multiple_of(j * RC, RC), RC)


def _fold8(x):
    return jnp.sum(x.reshape(RC // 8, 8, x.shape[-1]), axis=0)


def _ffn_act(u, conv_w, conv_b, name):
    def body(u_ref, w_ref, b_ref, a_ref, y_ref):
        def step(j, carry):
            yg = _conv_chunk(lambda r, n: u_ref[0, pl.ds(r, n), :], j, w_ref, b_ref, 0)[0]
            yv = _conv_chunk(lambda r, n: u_ref[1, pl.ds(r, n), :], j, w_ref, b_ref, 1)[0]
            th = jnp.tanh(GELU_K * (yg + GELU_C * yg * yg * yg))
            a_ref[_chunk(j), :] = (0.5 * yg * (1.0 + th) * yv).astype(BF16)
            y_ref[0, _chunk(j), :] = yg.astype(BF16)
            y_ref[1, _chunk(j), :] = yv.astype(BF16)
            return carry

        lax.fori_loop(0, NRC, step, 0)

    return pl.pallas_call(
        body, grid=(D_FF // FC,),
        in_specs=[pl.BlockSpec((2, S, FC), lambda j: (0, 0, j)), pl.BlockSpec((3, 2, FC), lambda j: (0, 0, j)),
                  pl.BlockSpec((2, FC), lambda j: (0, j))],
        out_specs=[pl.BlockSpec((S, FC), lambda j: (0, j)), pl.BlockSpec((2, S, FC), lambda j: (0, 0, j))],
        out_shape=[jax.ShapeDtypeStruct((S, D_FF), BF16), jax.ShapeDtypeStruct((2, S, D_FF), BF16)],
        compiler_params=_cp("parallel"), name=name)(u, conv_w, conv_b)


def _ffn_act_bwd(u, y, d_a, conv_w, name):
    def body(u_ref, y_ref, da_ref, w_ref, du_ref, dw_ref, db_ref, dy_s):
        def first(j, acc):
            yg = y_ref[0, _chunk(j), :].astype(F32)
            yv = y_ref[1, _chunk(j), :].astype(F32)
            th = jnp.tanh(GELU_K * (yg + GELU_C * yg * yg * yg))
            gelu = 0.5 * yg * (1.0 + th)
            dgelu = 0.5 * (1.0 + th) + 0.5 * yg * (1.0 - th * th) * GELU_K * (1.0 + 3.0 * GELU_C * yg * yg)
            da = da_ref[_chunk(j), :].astype(F32)
            dyg = da * yv * dgelu
            dyv = da * gelu
            dy_s[0, _chunk(j), :] = dyg
            dy_s[1, _chunk(j), :] = dyv
            return acc[0] + _fold8(dyg), acc[1] + _fold8(dyv)

        zero = jnp.zeros((8, FC), F32)
        accb = lax.fori_loop(0, NRC, first, (zero, zero))
        for half in range(2):
            db_ref[half:half + 1, :] = jnp.sum(accb[half], axis=0, keepdims=True)

        def second(j, acc):
            new = []
            for half in range(2):
                cur = dy_s[half, _chunk(j), :]
                h0 = pl.multiple_of(jnp.minimum((j + 1) * RC, S - 8), 8)
                head = jnp.where(j < NRC - 1, dy_s[half, pl.ds(h0, 8), :], 0.0)
                up1 = _up(cur, head, 1)
                up2 = _up(cur, head, 2)
                du = w_ref[2:3, half, :] * cur + w_ref[1:2, half, :] * up1 + w_ref[0:1, half, :] * up2
                du_ref[half, _chunk(j), :] = du.astype(BF16)
                uu = u_ref[half, _chunk(j), :].astype(F32)
                new += [_fold8(up2 * uu), _fold8(up1 * uu), _fold8(cur * uu)]
            return tuple(a + n for a, n in zip(acc, new))

        accw = lax.fori_loop(0, NRC, second, tuple(zero for _ in range(6)))
        for half in range(2):
            for k in range(3):
                dw_ref[k:k + 1, half, :] = jnp.sum(accw[3 * half + k], axis=0, keepdims=True)

    return pl.pallas_call(
        body, grid=(D_FF // FC,),
        in_specs=[pl.BlockSpec((2, S, FC), lambda j: (0, 0, j)), pl.BlockSpec((2, S, FC), lambda j: (0, 0, j)),
                  pl.BlockSpec((S, FC), lambda j: (0, j)), pl.BlockSpec((3, 2, FC), lambda j: (0, 0, j))],
        out_specs=[pl.BlockSpec((2, S, FC), lambda j: (0, 0, j)), pl.BlockSpec((3, 2, FC), lambda j: (0, 0, j)),
                   pl.BlockSpec((2, FC), lambda j: (0, j))],
        out_shape=[jax.ShapeDtypeStruct((2, S, D_FF), BF16), jax.ShapeDtypeStruct((3, 2, D_FF), F32),
                   jax.ShapeDtypeStruct((2, D_FF), F32)],
        scratch_shapes=[pltpu.VMEM((2, S, FC), F32)],
        compiler_params=_cp("parallel"), name=name)(u, y, d_a, conv_w)


def _layer_fwd(x, h1, w, bias, lname):
    n = lambda s: f"{lname}_{s}"
    w.need("in", h1)
    tn = 768
    proj = _mm(h1, w["w_in"], grid=(1, QKV_COLS // tn, 1),
               a_spec=pl.BlockSpec((S, D), lambda i, j, k: (i, 0)),
               b_spec=pl.BlockSpec((tn, D), lambda i, j, k: (j, 0)),
               out_shape=jax.ShapeDtypeStruct((QKV_SLABS, S, LANES), F32),
               out_spec=pl.BlockSpec((tn // LANES, S, LANES), lambda i, j, k: (j, i, 0)),
               ca=1, cb=1, acc_shape=(S, tn), out_slab=True, name=n("proj_qkv"))
    gates = _mm(h1, w["w_in"], grid=(1, GATE_COLS // tn, 1),
                a_spec=pl.BlockSpec((S, D), lambda i, j, k: (i, 0)),
                b_spec=pl.BlockSpec((tn, D), lambda i, j, k: (j + QKV_COLS // tn, 0)),
                out_shape=jax.ShapeDtypeStruct((S, GATE_COLS), BF16),
                out_spec=pl.BlockSpec((S, tn), lambda i, j, k: (i, j)),
                ca=1, cb=1, acc_shape=(S, tn), name=n("proj_gate"))
    nums, stats = [], []
    for g, (_, d) in enumerate(A_GROUPS):
        nm, st = _band_fwd(proj, bias, w["sinks"], d=d, q0=2 * g, k0=6 + 2 * g, v0=12 + 2 * g, npairs=2, bias0=2 * g,
                           shared_kv=False, name=n(f"attn_a{g}_fwd"))
        nums.append(nm)
        stats.append(st)
    o_a, lse_a = _combine_a(nums, stats, n("attn_a_combine"))
    o_b, lse_b = _band_fwd(proj, bias, w["sinks"], d=1, q0=18, k0=22, v0=23, npairs=4, bias0=6, shared_kv=True,
                           name=n("attn_b_fwd"))
    o_c, tot_c = _stick_fwd(proj, q0=24, k0=26, v0=28, name=n("attn_c_fwd"))
    w.need("mix", tot_c)
    merged, mo = _merge_fwd(o_a, o_b, o_c, gates, w["b_gate"], w["w_br_a"], w["w_br_b"], w["w_br_c"], w["w_out"], n("merge_fwd"))
    w.need("pre_ffn", mo)
    x2, h2 = _postnorm_res(x, mo, w["attn_post_norm"], w["ffn_pre_norm"], n("attn_post"))
    w.need("ffn", h2)
    u = _mm(h2, w["w_up"], grid=(1, 2 * D_FF // 1024, 1),
            a_spec=pl.BlockSpec((S, D), lambda i, j, k: (i, 0)),
            b_spec=pl.BlockSpec((D, 1024), lambda i, j, k: (0, j)),
            out_shape=jax.ShapeDtypeStruct((2, S, D_FF), BF16),
            out_spec=pl.BlockSpec((None, S, 1024), lambda i, j, k: (j // 4, i, j % 4)),
            ca=1, cb=0, acc_shape=(S, 1024), name=n("ffn_up"))
    a, y = _ffn_act(u, w["conv_w"], w["conv_b"], n("ffn_act"))
    w.need("pre_next", a)
    fo = _mm_nn(a, w["w_down"], F32, 1024, 1024, 2048, n("ffn_down"))
    saved = dict(x=x, h1=h1, proj=proj, gates=gates, o_a=o_a, lse_a=lse_a, o_b=o_b, lse_b=lse_b, o_c=o_c, tot_c=tot_c,
                 merged=merged, mo=mo, x2=x2, h2=h2, u=u, y=y, a=a, fo=fo)
    return saved


def _layer_bwd(dx3, sv, w, bias, lname, tok=None, on_part=None, d_fo=None, below=None):
    n = lambda s: f"{lname}_{s}"
    g = {}

    def part(group, vec):
        t = on_part(group, g) if on_part is not None else None
        return vec if t is None else vec + t

    if d_fo is None:
        gain = w["ffn_post_norm"] if tok is None else w["ffn_post_norm"] + tok
        d_fo, g["ffn_post_norm"] = _norm_bwd(sv["fo"], gain, [dx3], None, BF16, n("ffn_post_bwd"))
    else:
        d_fo, g["ffn_post_norm"] = d_fo
    d_a = _mm_nt(d_fo, w["w_down"], BF16, S, 1024, 1024, n("ffn_down_bwd_x"))
    g["w_down"] = _mm_tn(sv["a"], d_fo, BF16, 1024, 1024, S, n("ffn_down_bwd_w"))
    d_u, dcw, dcb = _ffn_act_bwd(sv["u"], sv["y"], d_a, w["conv_w"], n("ffn_act_bwd"))
    g["conv_w"] = dcw.reshape(3, 2 * D_FF)
    g["conv_b"] = dcb.reshape(1, 2 * D_FF)
    g["w_up"] = _mm(sv["h2"], d_u, grid=(1, 2 * D_FF // 1024, 1),
                    a_spec=pl.BlockSpec((S, D), lambda i, j, k: (k, 0)),
                    b_spec=pl.BlockSpec((None, S, 1024), lambda i, j, k: (j // 4, k, j % 4)),
                    out_shape=jax.ShapeDtypeStruct((D, 2 * D_FF), BF16),
                    out_spec=pl.BlockSpec((D, 1024), lambda i, j, k: (0, j)),
                    ca=0, cb=0, acc_shape=(D, 1024), name=n("ffn_up_bwd_w"))
    tok_ffn = on_part("ffn", g) if on_part is not None else None
    d_h2 = _mm(d_u, w["w_up"], grid=(S // 1024, 1, 2),
               a_spec=pl.BlockSpec((None, 1024, D_FF), lambda i, j, k: (k, i, 0)),
               b_spec=pl.BlockSpec((D, D_FF), lambda i, j, k: (0, k)),
               out_shape=jax.ShapeDtypeStruct((S, D), F32),
               out_spec=pl.BlockSpec((1024, D), lambda i, j, k: (i, 0)),
               ca=1, cb=1, acc_shape=(1024, D), after=tok_ffn, name=n("ffn_up_bwd_x"))
    dx2, d_mo, g["ffn_pre_norm"], g["attn_post_norm"] = _norm_bwd_chain(
        sv["x2"], w["ffn_pre_norm"], [d_h2], dx3, sv["mo"], w["attn_post_norm"], n("ffn_pre_attn_post_bwd"))
    g["w_out"] = _mm_tn(sv["merged"], d_mo, BF16, 1024, 1024, S, n("out_bwd_w"))
    do_a, do_b, do_c, d_gates, dwa, dwb, dwc, g["b_gate"] = _merge_bwd(
        d_mo, sv["o_a"], sv["o_b"], sv["o_c"], sv["gates"], w["b_gate"], w["w_br_a"], w["w_br_b"], w["w_br_c"],
        w["w_out"], n("merge_bwd"))
    g["w_br_a"], g["w_br_b"], g["w_br_c"] = dwa, dwb, dwc
    sinks = part("mix", w["sinks"])
    proj = sv["proj"]
    dqkv = lax.empty((QKV_SLABS, S, LANES), BF16)
    gbias = []
    for gi, (_, d) in enumerate(A_GROUPS):
        dqkv, gg, _ = _band_bwd(proj, bias, sv["o_a"], do_a, sv["lse_a"], sinks, dqkv, d=d, q0=2 * gi, k0=6 + 2 * gi,
                                v0=12 + 2 * gi, npairs=2, bias0=2 * gi, shared_kv=False, name=n(f"attn_a{gi}_bwd"))
        gbias.append(gg)
    dqkv, ggb, dsink = _band_bwd(proj, bias, sv["o_b"], do_b, sv["lse_b"], sinks, dqkv, d=1, q0=18, k0=22, v0=23,
                                 npairs=4, bias0=6, shared_kv=True, name=n("attn_b_bwd"))
    gbias.append(ggb)
    g["bias_g"] = jnp.concatenate(gbias, axis=0).reshape(N_BIAS_HEADS, BLK, 2 * BLK)
    g["sinks"] = dsink[:, 0, :2].reshape(1, 8)
    dqkv = _stick_bwd(proj, do_c, sv["tot_c"], dqkv, q0=24, k0=26, v0=28, name=n("attn_c_bwd"))
    ts = 6
    tsx = QKV_SLABS
    dw_in = _mm(dqkv, sv["h1"], grid=(QKV_SLABS // ts, 1, 1),
                a_spec=pl.BlockSpec((ts, S, LANES), lambda i, j, k: (i, k, 0)),
                b_spec=pl.BlockSpec((S, D), lambda i, j, k: (k, 0)),
                out_shape=jax.ShapeDtypeStruct((IN_COLS, D), BF16),
                out_spec=pl.BlockSpec((ts * LANES, D), lambda i, j, k: (i, 0)),
                ca=0, cb=0, acc_shape=(ts * LANES, D), a_slab=True, name=n("in_bwd_w_qkv"))
    g["w_in"] = _mm(d_gates, sv["h1"], grid=(GATE_COLS // 768, 1, 1),
                    a_spec=pl.BlockSpec((S, 768), lambda i, j, k: (k, i)),
                    b_spec=pl.BlockSpec((S, D), lambda i, j, k: (k, 0)),
                    out_shape=jax.ShapeDtypeStruct((IN_COLS, D), BF16),
                    out_spec=pl.BlockSpec((768, D), lambda i, j, k: (i + QKV_COLS // 768, 0)),
                    ca=0, cb=0, acc_shape=(768, D), alias_out=dw_in, name=n("in_bwd_w_gate"))
    tok_in = on_part("in", g) if on_part is not None else None
    d_h1a = _mm(dqkv, w["w_in"], grid=(S // 1024, 1, QKV_SLABS // tsx),
                a_spec=pl.BlockSpec((tsx, 1024, LANES), lambda i, j, k: (k, i, 0)),
                b_spec=pl.BlockSpec((tsx * LANES, D), lambda i, j, k: (k, 0)),
                out_shape=jax.ShapeDtypeStruct((S, D), F32),
                out_spec=pl.BlockSpec((1024, D), lambda i, j, k: (i, 0)),
                ca=1, cb=0, acc_shape=(1024, D), a_slab=True, after=tok_in, name=n("in_bwd_x_qkv"))
    d_h1b = _mm(d_gates, w["w_in"], grid=(S // 1024, 1, GATE_COLS // 768),
                a_spec=pl.BlockSpec((1024, 768), lambda i, j, k: (i, k)),
                b_spec=pl.BlockSpec((768, D), lambda i, j, k: (k + QKV_COLS // 768, 0)),
                out_shape=jax.ShapeDtypeStruct((S, D), F32),
                out_spec=pl.BlockSpec((1024, D), lambda i, j, k: (i, 0)),
                ca=1, cb=0, acc_shape=(1024, D), after=tok_in, name=n("in_bwd_x_gate"))
    if below is None:
        dx, g["attn_pre_norm"] = _norm_bwd(sv["x"], w["attn_pre_norm"], [d_h1a, d_h1b], dx2, F32, n("attn_pre_bwd"))
        return dx, g, tok_in, None
    dx, d_fo_below, g["attn_pre_norm"], dg_below = _norm_bwd_chain(
        sv["x"], w["attn_pre_norm"], [d_h1a, d_h1b], dx2, below[0], below[1], n("attn_pre_ffn_post_bwd"))
    return dx, g, tok_in, (d_fo_below, dg_below)


def _local_step(x, target, ws, rel_bias, tok=None, on_grads=None):
    buckets = jnp.asarray(_bucket_tiles())
    bias = _bias_tiles(rel_bias, buckets, "bias_tiles").reshape(N_BIAS_HEADS // 2, 2, 2, BLK, 2 * BLK)
    saved = []
    gain0 = ws[0]["attn_pre_norm"] if tok is None else ws[0]["attn_pre_norm"] + tok
    h1 = _prenorm(x, gain0, "l0_attn_pre")
    for l in range(DEPTH):
        sv = _layer_fwd(x, h1, ws[l], bias, f"l{l}")
        saved.append(sv)
        if l + 1 < DEPTH:
            x, h1 = _postnorm_res(sv["x2"], sv["fo"], ws[l]["ffn_post_norm"], ws[l + 1]["attn_pre_norm"], f"l{l}_ffn_post")
    top = saved[-1]
    dy, loss_tile, d_fo_top, dg_top = _loss_head(top["x2"], top["fo"], ws[-1]["ffn_post_norm"], target, "loss_head")
    grads = [None] * DEPTH
    tok, d_fo = None, (d_fo_top, dg_top)
    for l in reversed(range(DEPTH)):
        on_part = None if on_grads is None else functools.partial(on_grads, l)
        below = (saved[l - 1]["fo"], ws[l - 1]["ffn_post_norm"]) if l > 0 else None
        dy, grads[l], tok, d_fo = _layer_bwd(dy, saved[l], ws[l], bias, f"l{l}", tok, on_part, d_fo, below)
    g_rel = _bias_grad([grads[l]["bias_g"] for l in range(DEPTH)], buckets, "bias_grad")[:, :N_BIAS_HEADS]
    return loss_tile, dy, grads, g_rel


def _coords():
    return lax.axis_index("x"), lax.axis_index("y"), lax.axis_index("c")


def _peer(rel):
    x, y, c = _coords()
    return (1 - x if rel & 4 else x, 1 - y if rel & 2 else y, 1 - c if rel & 1 else c)


def _exchange(srcs, dst_shapes, src_win, dst_win, name, after=None):
    nt = len(srcs)
    extra = [] if after is None else [after]

    def body(*refs):
        src_refs, dst_refs = refs[:nt], refs[nt + len(extra):2 * nt + len(extra)]
        send_sems, recv_sems, local_sems = refs[2 * nt + len(extra):]
        x, y, c = _coords()
        me = 4 * x + 2 * y + c
        locals_ = []
        for t in range(nt):
            cp = pltpu.make_async_copy(src_win(t, src_refs[t], me), dst_win(t, dst_refs[t], me), local_sems.at[t])
            cp.start()
            locals_.append(cp)
        sends = []
        for rel in range(1, NDEV):
            px, py, pc = _peer(rel)
            q = 4 * px + 2 * py + pc
            for t in range(nt):
                cp = pltpu.make_async_remote_copy(
                    src_ref=src_win(t, src_refs[t], q), dst_ref=dst_win(t, dst_refs[t], me),
                    send_sem=send_sems.at[rel - 1, t], recv_sem=recv_sems.at[rel - 1, t],
                    device_id=(px, py, pc), device_id_type=MESH)
                cp.start()
                sends.append(cp)
        for rel in range(1, NDEV):
            px, py, pc = _peer(rel)
            q = 4 * px + 2 * py + pc
            for t in range(nt):
                pltpu.make_async_remote_copy(
                    src_ref=src_win(t, src_refs[t], me), dst_ref=dst_win(t, dst_refs[t], q),
                    send_sem=send_sems.at[rel - 1, t], recv_sem=recv_sems.at[rel - 1, t],
                    device_id=(px, py, pc), device_id_type=MESH).wait_recv()
        for cp in sends:
            cp.wait_send()
        for cp in locals_:
            cp.wait()

    return pl.pallas_call(
        body, in_specs=[ANY] * (nt + len(extra)), out_specs=[ANY] * nt, out_shape=dst_shapes,
        scratch_shapes=[pltpu.SemaphoreType.DMA((NDEV - 1, nt)), pltpu.SemaphoreType.DMA((NDEV - 1, nt)),
                        pltpu.SemaphoreType.DMA((nt,))],
        name=name)(*srcs, *extra)


BIG = (("w_in", 0, 864), ("w_br_a", 1, 128), ("w_br_b", 1, 128), ("w_br_c", 1, 128), ("w_out", 0, 128),
       ("w_up", 1, 1024), ("w_down", 0, 512))


NBIG = len(BIG)
BIG_FULL = {"w_in": (IN_COLS, D), "w_br_a": (256, D), "w_br_b": (512, D), "w_br_c": (256, D), "w_out": (D, D),
            "w_up": (D, 2 * D_FF), "w_down": (D_FF, D)}
SHARD_ROWS = {"w_in": 288, "w_up": 256, "w_down": 256}
LAYER_GROUPS = (("in", (0,)), ("mix", (1, 2, 3, 4)), ("ffn", (5, 6)))

HBM_SPEC = pl.BlockSpec(memory_space=pltpu.HBM)
SEM_SPEC = pl.BlockSpec(memory_space=pltpu.SEMAPHORE)


def _hbm(a):
    return pltpu.with_memory_space_constraint(a, pltpu.HBM)


def _shard_window(t, ref, k):
    nm, ax, ext = BIG[t % NBIG]
    off = pl.multiple_of(k * ext, ext)
    if ax == 0:
        return ref.at[pl.ds(off, ext), :]
    return ref.at[:, pl.ds(off, ext)]


def _whole(t, ref, k):
    return ref


def _slot(t, ref, k):
    return ref.at[k]


def _own_block_spec(t, rows, me_of):
    nm, ax, ext = BIG[t % NBIG]
    r, c = BIG_FULL[nm]
    if ax == 0:
        return pl.BlockSpec((rows, c), lambda i, m: (me_of(m) * (ext // rows) + i, 0))
    return pl.BlockSpec((rows, ext), lambda i, m: (i, me_of(m)))


def _cast_own(t, shards, me_arr, name):
    nm, ax, ext = BIG[t % NBIG]
    layer = t // NBIG
    _, nr, nc = shards.shape
    rows = SHARD_ROWS.get(nm, nr)
    shape = BIG_FULL[nm]

    def body(m_ref, s_ref, o_ref):
        o_ref[...] = s_ref[...].astype(BF16)

    return pl.pallas_call(
        body, grid_spec=pltpu.PrefetchScalarGridSpec(
            num_scalar_prefetch=1, grid=(nr // rows,),
            in_specs=[pl.BlockSpec((None, rows, nc), lambda i, m: (layer, i, 0))],
            out_specs=_own_block_spec(t, rows, lambda m: m[0])),
        out_shape=jax.ShapeDtypeStruct(shape, BF16), compiler_params=_cp("arbitrary"), name=name)(me_arr, shards)


ALL_RELS = tuple(range(1, NDEV))
NEAR_RELS = (1, 2, 4, 6)
FAR_RELS = (2, 4, 6)


def _xchg_start(srcs, lands, groups, src_win, dst_win, after, name, rels=ALL_RELS, tids=None):
    ns = 0 if srcs is None else len(srcs)
    nt, ng = len(lands), len(groups)
    ins = ([] if srcs is None else list(srcs)) + list(lands)

    def body(*refs):
        src_refs, land_refs = refs[:ns], refs[ns:ns + nt]
        sems = refs[ns + nt + 1:ns + nt + 1 + 2 * ng]
        token = refs[-1]
        x, y, c = _coords()
        me = 4 * x + 2 * y + c
        for gi, grp in enumerate(groups):
            for j, t in enumerate(grp):
                tid = t if tids is None else tids[t]
                for ri, rel in enumerate(rels):
                    px, py, pc = _peer(rel)
                    q = 4 * px + 2 * py + pc
                    src = dst_win(tid, land_refs[t], me) if srcs is None else src_win(tid, src_refs[t], q)
                    pltpu.make_async_remote_copy(
                        src_ref=src, dst_ref=dst_win(tid, land_refs[t], me),
                        send_sem=sems[2 * gi].at[ri * len(grp) + j],
                        recv_sem=sems[2 * gi + 1].at[ri * len(grp) + j],
                        device_id=(px, py, pc), device_id_type=MESH).start()
        token[...] = jnp.zeros((8, LANES), F32)

    out_shape = []
    for grp in groups:
        out_shape += [pltpu.SemaphoreType.DMA((len(rels) * len(grp),))] * 2
    out_shape += [pltpu.HBM(a.shape, a.dtype) for a in ins]
    out_shape.append(jax.ShapeDtypeStruct((8, LANES), F32))
    outs = pl.pallas_call(
        body, in_specs=[HBM_SPEC] * len(ins) + [ANY],
        out_specs=[SEM_SPEC] * (2 * ng) + [HBM_SPEC] * len(ins) + [pl.BlockSpec(memory_space=pltpu.VMEM)],
        out_shape=out_shape, input_output_aliases={i: 2 * ng + i for i in range(len(ins))},
        compiler_params=pltpu.CompilerParams(has_side_effects=pltpu.SideEffectType.DATAFLOW_SIDE_EFFECTING),
        name=name)(*[_hbm(a) for a in ins], after)
    sems = [(outs[2 * gi], outs[2 * gi + 1]) for gi in range(ng)]
    thru = list(outs[2 * ng:2 * ng + len(ins)])
    return sems, (None if srcs is None else thru[:ns]), thru[ns:], outs[-1]


def _xchg_wait(sems, srcs, lands, tids, after, src_win, dst_win, name, rels=ALL_RELS):
    ns = 0 if srcs is None else len(srcs)
    n = len(lands)
    send_sem, recv_sem = sems
    ins = ([] if srcs is None else list(srcs)) + list(lands)

    def body(*refs):
        src_refs, land_refs = refs[:ns], refs[ns:ns + n]
        ssem, rsem = refs[ns + n], refs[ns + n + 1]
        x, y, c = _coords()
        me = 4 * x + 2 * y + c
        for j, t in enumerate(tids):
            for ri, rel in enumerate(rels):
                px, py, pc = _peer(rel)
                q = 4 * px + 2 * py + pc
                src = dst_win(t, land_refs[j], me) if srcs is None else src_win(t, src_refs[j], q)
                cp = pltpu.make_async_remote_copy(
                    src_ref=src, dst_ref=dst_win(t, land_refs[j], q),
                    send_sem=ssem.at[ri * n + j], recv_sem=rsem.at[ri * n + j],
                    device_id=(px, py, pc), device_id_type=MESH)
                cp.wait_send()
                cp.wait_recv()

    outs = pl.pallas_call(
        body, in_specs=[HBM_SPEC] * len(ins) + [SEM_SPEC, SEM_SPEC, ANY], out_specs=[HBM_SPEC] * len(ins),
        out_shape=[pltpu.HBM(a.shape, a.dtype) for a in ins],
        input_output_aliases={i: i for i in range(len(ins))},
        compiler_params=pltpu.CompilerParams(has_side_effects=pltpu.SideEffectType.DATAFLOW_SIDE_EFFECTING),
        name=name)(*ins, send_sem, recv_sem, after)
    return (None if srcs is None else list(outs[:ns])), list(outs[ns:])


def _gather_forward(sems_in, lands, groups, tids, after, dst_win, name):
    nt, ng = len(lands), len(groups)

    def body(*refs):
        land_refs = refs[:nt]
        in_sems = refs[nt:nt + 2 * ng]
        out_sems = refs[nt + 2 * ng + 1:nt + 4 * ng + 1]
        token = refs[-1]
        x, y, c = _coords()
        me = 4 * x + 2 * y + c
        sib = (x, y, 1 - c)
        for gi, grp in enumerate(groups):
            n = len(grp)
            for j, pos in enumerate(grp):
                t = tids[pos]
                for ri, rel in enumerate(NEAR_RELS):
                    px, py, pc = _peer(rel)
                    q = 4 * px + 2 * py + pc
                    cp = pltpu.make_async_remote_copy(
                        src_ref=dst_win(t, land_refs[pos], me), dst_ref=dst_win(t, land_refs[pos], q),
                        send_sem=in_sems[2 * gi].at[ri * n + j], recv_sem=in_sems[2 * gi + 1].at[ri * n + j],
                        device_id=(px, py, pc), device_id_type=MESH)
                    cp.wait_send()
                    cp.wait_recv()
            for j, pos in enumerate(grp):
                t = tids[pos]
                for fi, rel in enumerate(FAR_RELS):
                    px, py, pc = _peer(rel)
                    q = 4 * px + 2 * py + pc
                    win = dst_win(t, land_refs[pos], q)
                    pltpu.make_async_remote_copy(
                        src_ref=win, dst_ref=win,
                        send_sem=out_sems[2 * gi].at[fi * n + j], recv_sem=out_sems[2 * gi + 1].at[fi * n + j],
                        device_id=sib, device_id_type=MESH).start()
        token[...] = jnp.zeros((8, LANES), F32)

    out_shape = []
    for grp in groups:
        out_shape += [pltpu.SemaphoreType.DMA((len(FAR_RELS) * len(grp),))] * 2
    out_shape += [pltpu.HBM(a.shape, a.dtype) for a in lands]
    out_shape.append(jax.ShapeDtypeStruct((8, LANES), F32))
    flat_sems = [s for pair in sems_in for s in pair]
    outs = pl.pallas_call(
        body, in_specs=[HBM_SPEC] * nt + [SEM_SPEC] * (2 * ng) + [ANY],
        out_specs=[SEM_SPEC] * (2 * ng) + [HBM_SPEC] * nt + [pl.BlockSpec(memory_space=pltpu.VMEM)],
        out_shape=out_shape, input_output_aliases={i: 2 * ng + i for i in range(nt)},
        compiler_params=pltpu.CompilerParams(has_side_effects=pltpu.SideEffectType.DATAFLOW_SIDE_EFFECTING),
        name=name)(*[_hbm(a) for a in lands], *flat_sems, after)
    sems = [(outs[2 * gi], outs[2 * gi + 1]) for gi in range(ng)]
    return sems, list(outs[2 * ng:2 * ng + nt]), outs[-1]


class _Weights:
    def __init__(self, ready, pending=None):
        self.ready = dict(ready)
        self.pending = dict(pending or {})

    def __getitem__(self, k):
        return self.ready[k]

    def need(self, group, after):
        fn = self.pending.pop(group, None)
        if fn is not None:
            self.ready.update(fn(after))


def _adamw_math(w, g, m, v):
    m2 = ADAM_B1 * m + (1.0 - ADAM_B1) * g
    v2 = ADAM_B2 * v + (1.0 - ADAM_B2) * (g * g)
    m_hat = m2 / (1.0 - ADAM_B1 ** ADAM_STEP)
    v_hat = v2 / (1.0 - ADAM_B2 ** ADAM_STEP)
    delta = -ADAM_LR * (m_hat / (jnp.sqrt(v_hat) + ADAM_EPS) + ADAM_WD * w)
    return delta, m2, v2


def _adamw(t, parts, own, me_arr, w, m, v, layer, prev, rows, name):
    nl, nr, nc = w.shape

    def body(me_ref, p_ref, own_ref, w_ref, m_ref, v_ref, *rest):
        g_ref, d_ref, m2_ref, v2_ref = rest[-4:]
        me = me_ref[0]
        g = None
        for k in range(NDEV):
            term = jnp.where(me == k, own_ref[...], p_ref[k]).astype(F32)
            g = term if g is None else g + term
        delta, m2, v2 = _adamw_math(w_ref[...], g, m_ref[...], v_ref[...])
        g_ref[...] = g
        d_ref[...] = delta
        m2_ref[...] = m2
        v2_ref[...] = v2

    blk = pl.BlockSpec((None, rows, nc), lambda i, mm: (layer, i, 0))
    pblk = pl.BlockSpec((NDEV, rows, nc), lambda i, mm: (0, i, 0))
    extra = [] if prev is None else list(prev)
    return pl.pallas_call(
        body, grid_spec=pltpu.PrefetchScalarGridSpec(
            num_scalar_prefetch=1, grid=(nr // rows,),
            in_specs=[pblk, _own_block_spec(t, rows, lambda mm: mm[0]), blk, blk, blk] + [ANY] * len(extra),
            out_specs=[blk] * 4),
        out_shape=[jax.ShapeDtypeStruct(w.shape, F32)] * 4,
        input_output_aliases={6 + k: k for k in range(len(extra))},
        compiler_params=_cp("arbitrary"), name=name)(me_arr, parts, own, w, m, v, *extra)


def _pack(vecs):
    flat = jnp.concatenate([v.reshape(-1).astype(F32) for v in vecs])
    n = flat.shape[0]
    rows = -(-n // (8 * LANES)) * 8
    return jnp.pad(flat, (0, rows * LANES - n)).reshape(rows, LANES)


ROWPACK = (("rel_bias", 32, 32, (NUM_BUCKETS, N_BIAS_HEADS)), ("sinks", 8, 8, (DEPTH, 8)),
           ("attn_pre_norm", 16, 16, (DEPTH, D)), ("attn_post_norm", 16, 16, (DEPTH, D)),
           ("ffn_pre_norm", 16, 16, (DEPTH, D)), ("ffn_post_norm", 16, 16, (DEPTH, D)),
           ("conv_b", 128, 128, (DEPTH, 2 * D_FF)), ("b_gate", 48, 8, (DEPTH, 3, 128)),
           ("conv_w", 384, 48, (DEPTH, 3, 1024)))
ROWS_OWN = sum(r for _, _, r, _ in ROWPACK)
N_REPL = 7
ROWS_REPL = sum(r for _, _, r, _ in ROWPACK[:N_REPL])
ROWS_SHARD = ROWS_OWN - ROWS_REPL


def _as_rows(a, rows):
    a = a.astype(F32)
    if a.shape[-1] < LANES:
        a = jnp.pad(a.reshape(-1, a.shape[-1]), ((0, 0), (0, LANES - a.shape[-1])))
    a = a.reshape(-1, LANES)
    return jnp.pad(a, ((0, rows - a.shape[0]), (0, 0)))


def _rowpack(arrs, entries=ROWPACK):
    return jnp.concatenate([_as_rows(arrs[nm], ro) for nm, _, ro, _ in entries], axis=0)


def _shard_rows(g):
    bg = jnp.transpose(g["b_gate"].astype(F32).reshape(DEPTH * 3, NDEV, LANES), (1, 0, 2))
    bg = jnp.pad(bg, ((0, 0), (0, 8 - DEPTH * 3), (0, 0)))
    cw = jnp.transpose(g["conv_w"].astype(F32).reshape(DEPTH * 3, NDEV, 8, LANES), (1, 0, 2, 3))
    return jnp.concatenate([bg, cw.reshape(NDEV, DEPTH * 3 * 8, LANES)], axis=1)


def _small_update(parts_repl, parts_shard, w, m, v, name):
    nsm = len(ROWPACK)

    def body(pr_ref, ps_ref, w_ref, m_ref, v_ref, *rest):
        outs = rest[:4 * nsm]
        loss_ref = rest[4 * nsm]
        g_s, d_s, m_s, v_s = rest[4 * nsm + 1:]
        gr, gs = pr_ref[0], ps_ref[0]
        for k in range(1, NDEV):
            gr = gr + pr_ref[k]
            gs = gs + ps_ref[k]
        g_s[0:ROWS_REPL, :] = gr[:ROWS_REPL]
        g_s[ROWS_REPL:ROWS_OWN, :] = gs
        loss_ref[...] = gr[ROWS_REPL:]
        delta, m2, v2 = _adamw_math(w_ref[...], g_s[...], m_ref[...], v_ref[...])
        d_s[...] = delta
        m_s[...] = m2
        v_s[...] = v2
        for kind, src in enumerate((g_s, d_s, m_s, v_s)):
            oo = 0
            for idx, (nm, rf, ro, shp) in enumerate(ROWPACK):
                o_ref = outs[kind * nsm + idx]
                if nm in ("rel_bias", "sinks"):
                    o_ref[...] = src[oo:oo + shp[0], 0:shp[1]]
                elif nm == "b_gate":
                    for l in range(DEPTH):
                        o_ref[l] = src[oo + 3 * l:oo + 3 * l + 3, :]
                elif nm == "conv_w":
                    for l in range(DEPTH):
                        for k in range(8):
                            o_ref[l, :, k * LANES:(k + 1) * LANES] = src[pl.ds(oo + 24 * l + k, 3, stride=8), :]
                else:
                    per = shp[1] // LANES
                    for k in range(per):
                        o_ref[:, k * LANES:(k + 1) * LANES] = src[pl.ds(oo + k, DEPTH, stride=per), :]
                oo += ro

    vm = pl.BlockSpec(memory_space=pltpu.VMEM)
    shapes = [jax.ShapeDtypeStruct(shp, F32) for _ in range(4) for _, _, _, shp in ROWPACK]
    shapes.append(jax.ShapeDtypeStruct((8, LANES), F32))
    outs = pl.pallas_call(
        body, in_specs=[vm] * 5, out_specs=[vm] * (4 * nsm + 1), out_shape=shapes,
        scratch_shapes=[pltpu.VMEM((ROWS_OWN, LANES), F32)] * 4,
        name=name)(parts_repl, parts_shard, w, m, v)
    names = [nm for nm, _, _, _ in ROWPACK]
    return [dict(zip(names, outs[kind * nsm:(kind + 1) * nsm])) for kind in range(4)] + [outs[-1]]


def kernel(x, rel_bias, attn_pre_norm, w_in, b_gate, sinks, w_br_a, w_br_b, w_br_c, w_out, attn_post_norm, ffn_pre_norm, w_up, conv_w, conv_b, w_down, ffn_post_norm, loss_target, m_rel_bias, m_attn_pre_norm, m_w_in, m_b_gate, m_sinks, m_w_br_a, m_w_br_b, m_w_br_c, m_w_out, m_attn_post_norm, m_ffn_pre_norm, m_w_up, m_conv_w, m_conv_b, m_w_down, m_ffn_post_norm, v_rel_bias, v_attn_pre_norm, v_w_in, v_b_gate, v_sinks, v_w_br_a, v_w_br_b, v_w_br_c, v_w_out, v_attn_post_norm, v_ffn_pre_norm, v_w_up, v_conv_w, v_conv_b, v_w_down, v_ffn_post_norm):
    P = dict(rel_bias=rel_bias, attn_pre_norm=attn_pre_norm, w_in=w_in, b_gate=b_gate, sinks=sinks, w_br_a=w_br_a,
             w_br_b=w_br_b, w_br_c=w_br_c, w_out=w_out, attn_post_norm=attn_post_norm, ffn_pre_norm=ffn_pre_norm,
             w_up=w_up, conv_w=conv_w, conv_b=conv_b, w_down=w_down, ffn_post_norm=ffn_post_norm)
    M = dict(rel_bias=m_rel_bias, attn_pre_norm=m_attn_pre_norm, w_in=m_w_in, b_gate=m_b_gate, sinks=m_sinks,
             w_br_a=m_w_br_a, w_br_b=m_w_br_b, w_br_c=m_w_br_c, w_out=m_w_out, attn_post_norm=m_attn_post_norm,
             ffn_pre_norm=m_ffn_pre_norm, w_up=m_w_up, conv_w=m_conv_w, conv_b=m_conv_b, w_down=m_w_down,
             ffn_post_norm=m_ffn_post_norm)
    V = dict(rel_bias=v_rel_bias, attn_pre_norm=v_attn_pre_norm, w_in=v_w_in, b_gate=v_b_gate, sinks=v_sinks,
             w_br_a=v_w_br_a, w_br_b=v_w_br_b, w_br_c=v_w_br_c, w_out=v_w_out, attn_post_norm=v_attn_post_norm,
             ffn_pre_norm=v_ffn_pre_norm, w_up=v_w_up, conv_w=v_conv_w, conv_b=v_conv_b, w_down=v_w_down,
             ffn_post_norm=v_ffn_post_norm)
    tr = lambda a: jnp.swapaxes(a, 1, 2)
    PB = {nm: (tr(P[nm]) if nm == "w_in" else P[nm]) for nm, _, _ in BIG}
    MB = {nm: (tr(M[nm]) if nm == "w_in" else M[nm]) for nm, _, _ in BIG}
    VB = {nm: (tr(V[nm]) if nm == "w_in" else V[nm]) for nm, _, _ in BIG}
    xi, yi, ci = _coords()
    me = 4 * xi + 2 * yi + ci

    me_arr = me.astype(jnp.int32).reshape(1)

    small_w = _pack([b_gate.reshape(-1), conv_w.reshape(-1)])
    (small_w_all,) = _exchange([small_w], [jax.ShapeDtypeStruct((NDEV,) + small_w.shape, F32)],
                               _whole, _slot, "gather_small_weights")
    nbg, ncw = DEPTH * 3 * 128, DEPTH * 3 * 1024
    flat_all = small_w_all.reshape(NDEV, -1)
    b_gate_full = jnp.transpose(flat_all[:, :nbg].reshape(NDEV, DEPTH, 3, 128), (1, 2, 0, 3)).reshape(DEPTH, 3, D)
    conv_w_full = jnp.transpose(flat_all[:, nbg:nbg + ncw].reshape(NDEV, DEPTH, 3, 1024), (1, 2, 0, 3)).reshape(DEPTH, 3, 2 * D_FF)

    groups = [tuple(l * NBIG + t for t in tids) for l in range(DEPTH) for _, tids in LAYER_GROUPS]
    cast = lambda i, m=me_arr: _cast_own(i, PB[BIG[i % NBIG][0]], m, f"gather_own_l{i // NBIG}_{BIG[i % NBIG][0]}")
    first = list(groups[0])
    rest = [i for grp in groups[1:] for i in grp]
    sems0, _, lands0, tok_first = _xchg_start(None, [cast(i) for i in first], [tuple(range(len(first)))], None,
                                              _shard_window, small_w_all, "gather_start_first", rels=NEAR_RELS, tids=first)
    where_rest = {tid: k for k, tid in enumerate(rest)}
    me_rest = me_arr + tok_first[0, 0:1].astype(jnp.int32)
    sems1, _, lands1, g_tok = _xchg_start(None, [cast(i, me_rest) for i in rest],
                                          [tuple(where_rest[i] for i in grp) for grp in groups[1:]], None,
                                          _shard_window, lands0[0], "gather_start_rest", rels=NEAR_RELS, tids=rest)
    g_sems = list(sems0) + list(sems1)
    tok0 = g_tok[0:1, 0:1]
    lands_now = [None] * (DEPTH * NBIG)
    for i, a in zip(first + rest, list(lands0) + list(lands1)):
        lands_now[i] = a
    fwd_sems = {}
    fwd_plan = {0: (0,), 1: (1,), 2: (2,), 3: (3, 4, 5)}

    def forward(gis, after):
        gis = tuple(g2 for g2 in gis if g2 not in fwd_sems)
        if not gis:
            return after
        flat = [i for g2 in gis for i in groups[g2]]
        where = {tid: k for k, tid in enumerate(flat)}
        fs, new_lands, ftok = _gather_forward(
            [g_sems[g2] for g2 in gis], [lands_now[i] for i in flat],
            [[where[i] for i in groups[g2]] for g2 in gis], flat, after, _shard_window, f"gather_forward_{gis[0]}")
        for g2, s in zip(gis, fs):
            fwd_sems[g2] = s
        for i, a in zip(flat, new_lands):
            lands_now[i] = a
        return ftok

    def prefetch(gis):
        def run(after):
            forward(gis, after)
            return {}
        return run

    def gather_waiter(gi, l, gname, tids):
        def wait(after):
            after = forward(fwd_plan.get(gi, ()), after)
            ids = [l * NBIG + t for t in tids]
            _, got = _xchg_wait(fwd_sems[gi], None, [lands_now[i] for i in ids], ids, after,
                                None, _shard_window, f"gather_wait_l{l}_{gname}", rels=FAR_RELS)
            out = {}
            for t, arr in zip(tids, got):
                nm = BIG[t][0]
                out[nm] = arr
            return out
        return wait

    pending = [{gname: gather_waiter(l * len(LAYER_GROUPS) + k, l, gname, tids)
                for k, (gname, tids) in enumerate(LAYER_GROUPS)} for l in range(DEPTH)]
    ng = len(LAYER_GROUPS)
    for l in range(DEPTH):
        pending[l]["pre_ffn"] = prefetch((l * ng + 2,))
        if l + 1 < DEPTH:
            pending[l]["pre_next"] = prefetch(tuple(range((l + 1) * ng, (l + 2) * ng)))
    ws = []
    for l in range(DEPTH):
        ws.append(_Weights(dict(
            b_gate=b_gate_full[l], conv_w=conv_w_full[l].reshape(3, 2, D_FF), conv_b=conv_b[l].reshape(2, D_FF),
            sinks=sinks[l].reshape(1, 8),
            attn_pre_norm=attn_pre_norm[l].reshape(1, D), attn_post_norm=attn_post_norm[l].reshape(1, D),
            ffn_pre_norm=ffn_pre_norm[l].reshape(1, D), ffn_post_norm=ffn_post_norm[l].reshape(1, D)), pending[l]))

    rs = {}

    group_tids = dict(LAYER_GROUPS)

    def start_scatter(l, gname, grads_l):
        tids = group_tids[gname]
        blocks, lands_rs = [], []
        for t in tids:
            nm, ax, ext = BIG[t]
            gfull = grads_l[nm].astype(BF16)
            shp = (NDEV, ext, gfull.shape[1]) if ax == 0 else (NDEV, gfull.shape[0], ext)
            blocks.append(gfull)
            lands_rs.append(lax.empty(shp, BF16))
        local = list(range(len(tids)))
        win = lambda j, ref, k: _shard_window(tids[j], ref, k)
        sems, s_thru, l_thru, tok = _xchg_start(blocks, lands_rs, [tuple(local)], win, _slot, me_arr,
                                                f"scatter_start_l{l}_{gname}")
        rs[(l, gname)] = (sems[0], s_thru, l_thru, win, local)
        return tok[0:1, 0:1]

    loss_tile, grad_x, grads, g_rel = _local_step(x[0], loss_target[0], ws, rel_bias, tok0, start_scatter)

    stack = lambda nm: jnp.stack([grads[l][nm] for l in range(DEPTH)], axis=0)
    small_g = {nm: (g_rel if nm == "rel_bias" else stack(nm)) for nm, _, _, _ in ROWPACK}
    small_repl = jnp.concatenate([_rowpack(small_g, ROWPACK[:N_REPL]), loss_tile], axis=0)
    small_shard = _shard_rows(small_g)

    out_g, out_d, out_m, out_v = {}, {}, {}, {}
    prev = {nm: None for nm, _, _ in BIG}
    todo = [(l, gname) for l in reversed(range(DEPTH)) for gname in ("ffn", "mix", "in")]
    after, small_parts = grad_x, None
    for l, gname in todo:
        if (l, gname) == todo[-1]:
            small_parts = _exchange(
                [small_repl, small_shard],
                [jax.ShapeDtypeStruct((NDEV, ROWS_REPL + 8, LANES), F32), jax.ShapeDtypeStruct((NDEV, ROWS_SHARD, LANES), F32)],
                lambda t, ref, q: ref if t == 0 else ref.at[q], _slot, "exchange_small_grads", after=after)
            after = small_parts[0]
        sems, s_thru, l_thru, win, local = rs[(l, gname)]
        owns, parts = _xchg_wait(sems, s_thru, l_thru, local, after, win, _slot, f"scatter_wait_l{l}_{gname}")
        for t, own, prt in zip(group_tids[gname], owns, parts):
            nm = BIG[t][0]
            rows = SHARD_ROWS.get(nm, PB[nm].shape[1])
            prev[nm] = _adamw(t, prt, own, me_arr, PB[nm], MB[nm], VB[nm], l, prev[nm], rows, f"adamw_{nm}_l{l}")
            after = prev[nm][1]
    for nm, _, _ in BIG:
        out_g[nm], out_d[nm], out_m[nm], out_v[nm] = [tr(a) if nm == "w_in" else a for a in prev[nm]]
    sm_g, sm_d, sm_m, sm_v, loss_all = _small_update(small_parts[0], small_parts[1], _rowpack(P), _rowpack(M),
                                                     _rowpack(V), "small_update")
    loss = loss_all[0, 0]
    for dst, src in ((out_g, sm_g), (out_d, sm_d), (out_m, sm_m), (out_v, sm_v)):
        dst.update(src)

    order = ["rel_bias", "attn_pre_norm", "w_in", "b_gate", "sinks", "w_br_a", "w_br_b", "w_br_c", "w_out",
             "attn_post_norm", "ffn_pre_norm", "w_up", "conv_w", "conv_b", "w_down", "ffn_post_norm"]
    return (loss, grad_x[None], *[out_g[k] for k in order], *[out_d[k] for k in order],
            *[out_m[k] for k in order], *[out_v[k] for k in order])
```

```python
import functools
import math

import numpy as np
import jax
import jax.numpy as jnp
from jax import lax
from jax.experimental import pallas as pl
from jax.experimental.pallas import tpu as pltpu

F32 = jnp.float32
BF16 = jnp.bfloat16

S = 2048
D = 1024
DEPTH = 2
NDEV = 8
HD = 64
BLK = 128
NB = S // BLK
A_GROUPS = ((128, 1), (512, 4), (2048, 16))
NUM_BUCKETS = 32
MAX_DISTANCE = 2048
N_BIAS_HEADS = 20
D_FF = 4096
IN_COLS = 6912
QKV_COLS = 3840
QKV_SLABS = QKV_COLS // 128
GATE_COLS = 3072
EPS = 1e-6
SCALE = HD ** -0.5
NEG = -1e30
LANES = 128

ADAM_LR = 0.001
ADAM_B1 = 0.9
ADAM_B2 = 0.999
ADAM_EPS = 1e-08
ADAM_WD = 0.01
ADAM_STEP = 10

VMEM_LIMIT = 56 * 1024 * 1024
MESH = pl.DeviceIdType.MESH
ANY = pl.BlockSpec(memory_space=pl.ANY)
SMEM = pl.BlockSpec(memory_space=pltpu.SMEM)


def _cp(*sem):
    return pltpu.CompilerParams(dimension_semantics=sem if sem else None, vmem_limit_bytes=VMEM_LIMIT)


def _dot(a, b, ca, cb):
    return lax.dot_general(a, b, (((ca,), (cb,)), ((), ())), preferred_element_type=F32)


def _mm(a, b, *, grid, a_spec, b_spec, out_shape, out_spec, ca, cb, acc_shape, name,
        a_slab=False, b_slab=False, out_slab=False, alias_out=None, after=None):
    nk = grid[2]

    def body(*refs):
        a_ref, b_ref = refs[0], refs[1]
        o_ref, acc_ref = refs[-2], refs[-1]
        k = pl.program_id(2)

        def load(ref, slab):
            if slab:
                return jnp.concatenate([ref[s] for s in range(ref.shape[0])], axis=1).astype(BF16)
            return ref[...].astype(BF16)

        def write(val):
            if out_slab:
                for s in range(o_ref.shape[0]):
                    o_ref[s] = val[:, s * LANES:(s + 1) * LANES].astype(o_ref.dtype)
            else:
                o_ref[...] = val.astype(o_ref.dtype)

        d = _dot(load(a_ref, a_slab), load(b_ref, b_slab), ca, cb)
        if nk == 1:
            write(d)
        elif direct:
            @pl.when(k == 0)
            def _():
                o_ref[...] = d

            @pl.when(k > 0)
            def _():
                o_ref[...] += d
        else:
            @pl.when(k == 0)
            def _():
                acc_ref[...] = d

            if nk > 2:
                @pl.when((k > 0) & (k < nk - 1))
                def _():
                    acc_ref[...] += d

            @pl.when(k == nk - 1)
            def _():
                write(acc_ref[...] + d)

    direct = (not out_slab) and out_shape.dtype == F32
    if nk == 1 or direct:
        acc_shape = (8, LANES)
    in_specs = [a_spec, b_spec]
    args = [a, b]
    aliases = {}
    if alias_out is not None:
        in_specs.append(ANY)
        args.append(alias_out)
        aliases = {2: 0}
    if after is not None:
        in_specs.append(ANY)
        args.append(after)
    return pl.pallas_call(
        body, grid=grid, in_specs=in_specs, out_specs=out_spec, out_shape=out_shape,
        scratch_shapes=[pltpu.VMEM(acc_shape, F32)], input_output_aliases=aliases,
        compiler_params=_cp("parallel", "parallel", "arbitrary"), name=name)(*args)


def _mm_nn(a, b, out_dtype, tm, tn, tk, name):
    m, kk = a.shape
    n = b.shape[1]
    return _mm(a, b, grid=(m // tm, n // tn, kk // tk),
               a_spec=pl.BlockSpec((tm, tk), lambda i, j, k: (i, k)),
               b_spec=pl.BlockSpec((tk, tn), lambda i, j, k: (k, j)),
               out_shape=jax.ShapeDtypeStruct((m, n), out_dtype),
               out_spec=pl.BlockSpec((tm, tn), lambda i, j, k: (i, j)),
               ca=1, cb=0, acc_shape=(tm, tn), name=name)


def _mm_nt(a, b, out_dtype, tm, tn, tk, name):
    m, kk = a.shape
    n = b.shape[0]
    return _mm(a, b, grid=(m // tm, n // tn, kk // tk),
               a_spec=pl.BlockSpec((tm, tk), lambda i, j, k: (i, k)),
               b_spec=pl.BlockSpec((tn, tk), lambda i, j, k: (j, k)),
               out_shape=jax.ShapeDtypeStruct((m, n), out_dtype),
               out_spec=pl.BlockSpec((tm, tn), lambda i, j, k: (i, j)),
               ca=1, cb=1, acc_shape=(tm, tn), name=name)


def _mm_tn(a, b, out_dtype, tm, tn, tk, name):
    kk, m = a.shape
    n = b.shape[1]
    return _mm(a, b, grid=(m // tm, n // tn, kk // tk),
               a_spec=pl.BlockSpec((tk, tm), lambda i, j, k: (k, i)),
               b_spec=pl.BlockSpec((tk, tn), lambda i, j, k: (k, j)),
               out_shape=jax.ShapeDtypeStruct((m, n), out_dtype),
               out_spec=pl.BlockSpec((tm, tn), lambda i, j, k: (i, j)),
               ca=0, cb=0, acc_shape=(tm, tn), name=name)


ROW_TILE = 512
MERGE_TILE = 256


def _rms(x, g):
    r = lax.rsqrt(jnp.mean(x * x, axis=-1, keepdims=True) + EPS)
    return x * r * g


def _prenorm(x, g, name):
    def body(x_ref, g_ref, o_ref):
        o_ref[...] = _rms(x_ref[...], g_ref[...]).astype(BF16)

    return pl.pallas_call(
        body, grid=(S // ROW_TILE,),
        in_specs=[pl.BlockSpec((ROW_TILE, D), lambda i: (i, 0)), pl.BlockSpec((1, D), lambda i: (0, 0))],
        out_specs=pl.BlockSpec((ROW_TILE, D), lambda i: (i, 0)),
        out_shape=jax.ShapeDtypeStruct((S, D), BF16), compiler_params=_cp("parallel"), name=name)(x, g)


def _postnorm_res(x, f, g_post, g_next, name):
    def body(x_ref, f_ref, gp_ref, gn_ref, xo_ref, ho_ref):
        xn = x_ref[...] + _rms(f_ref[...], gp_ref[...])
        xo_ref[...] = xn
        ho_ref[...] = _rms(xn, gn_ref[...]).astype(BF16)

    row = pl.BlockSpec((ROW_TILE, D), lambda i: (i, 0))
    vec = pl.BlockSpec((1, D), lambda i: (0, 0))
    return pl.pallas_call(
        body, grid=(S // ROW_TILE,), in_specs=[row, row, vec, vec], out_specs=[row, row],
        out_shape=[jax.ShapeDtypeStruct((S, D), F32), jax.ShapeDtypeStruct((S, D), BF16)],
        compiler_params=_cp("parallel"), name=name)(x, f, g_post, g_next)


def _norm_bwd(f, g, dys, res, out_dtype, name):
    ndy = len(dys)
    has_res = res is not None

    def body(*refs):
        f_ref, g_ref = refs[0], refs[1]
        dy_refs = refs[2:2 + ndy]
        res_ref = refs[2 + ndy] if has_res else None
        o_ref, dg_ref = refs[-2], refs[-1]
        fv = f_ref[...]
        dy = dy_refs[0][...].astype(F32)
        for r in dy_refs[1:]:
            dy = dy + r[...].astype(F32)
        r = lax.rsqrt(jnp.mean(fv * fv, axis=-1, keepdims=True) + EPS)
        n = fv * r
        dn = dy * g_ref[...]
        df = r * (dn - n * jnp.mean(dn * n, axis=-1, keepdims=True))
        if has_res:
            df = df + res_ref[...]
        o_ref[...] = df.astype(out_dtype)

        @pl.when(pl.program_id(0) == 0)
        def _():
            dg_ref[...] = jnp.zeros((1, D), F32)

        dg_ref[...] += jnp.sum(dy * n, axis=0, keepdims=True)

    row = pl.BlockSpec((ROW_TILE, D), lambda i: (i, 0))
    vec = pl.BlockSpec((1, D), lambda i: (0, 0))
    in_specs = [row, vec] + [row] * ndy + ([row] if has_res else [])
    args = [f, g] + list(dys) + ([res] if has_res else [])
    return pl.pallas_call(
        body, grid=(S // ROW_TILE,), in_specs=in_specs, out_specs=[row, vec],
        out_shape=[jax.ShapeDtypeStruct((S, D), out_dtype), jax.ShapeDtypeStruct((1, D), F32)],
        compiler_params=_cp("arbitrary"), name=name)(*args)


def _rms_bwd_rows(fv, g, dy):
    r = lax.rsqrt(jnp.mean(fv * fv, axis=-1, keepdims=True) + EPS)
    n = fv * r
    dn = dy * g
    return r * (dn - n * jnp.mean(dn * n, axis=-1, keepdims=True)), dy * n


def _norm_bwd_chain(f1, g1, dys, res, f2, g2, name):
    ndy = len(dys)

    def body(*refs):
        f1_ref, g1_ref = refs[0], refs[1]
        dy_refs = refs[2:2 + ndy]
        res_ref, f2_ref, g2_ref = refs[2 + ndy:5 + ndy]
        o1_ref, o2_ref, dg1_ref, dg2_ref = refs[-4:]
        dy = dy_refs[0][...].astype(F32)
        for r in dy_refs[1:]:
            dy = dy + r[...].astype(F32)
        df1, c1 = _rms_bwd_rows(f1_ref[...], g1_ref[...], dy)
        out1 = df1 + res_ref[...]
        o1_ref[...] = out1
        df2, c2 = _rms_bwd_rows(f2_ref[...], g2_ref[...], out1)
        o2_ref[...] = df2.astype(BF16)

        @pl.when(pl.program_id(0) == 0)
        def _():
            dg1_ref[...] = jnp.zeros((1, D), F32)
            dg2_ref[...] = jnp.zeros((1, D), F32)

        dg1_ref[...] += jnp.sum(c1, axis=0, keepdims=True)
        dg2_ref[...] += jnp.sum(c2, axis=0, keepdims=True)

    row = pl.BlockSpec((ROW_TILE, D), lambda i: (i, 0))
    vec = pl.BlockSpec((1, D), lambda i: (0, 0))
    return pl.pallas_call(
        body, grid=(S // ROW_TILE,), in_specs=[row, vec] + [row] * ndy + [row, row, vec],
        out_specs=[row, row, vec, vec],
        out_shape=[jax.ShapeDtypeStruct((S, D), F32), jax.ShapeDtypeStruct((S, D), BF16),
                   jax.ShapeDtypeStruct((1, D), F32), jax.ShapeDtypeStruct((1, D), F32)],
        compiler_params=_cp("arbitrary"), name=name)(f1, g1, *dys, res, f2, g2)


def _loss_head(x, f, g, target, name):
    def body(x_ref, f_ref, g_ref, t_ref, dy_ref, l_ref, df_ref, dg_ref):
        fv = f_ref[...]
        e = x_ref[...] + _rms(fv, g_ref[...]) - t_ref[...]
        dy = e * (1.0 / D)
        dy_ref[...] = dy
        df, c = _rms_bwd_rows(fv, g_ref[...], dy)
        df_ref[...] = df.astype(BF16)

        @pl.when(pl.program_id(0) == 0)
        def _():
            l_ref[...] = jnp.zeros((8, LANES), F32)
            dg_ref[...] = jnp.zeros((1, D), F32)

        l_ref[...] += jnp.sum(e * e) * (0.5 / D)
        dg_ref[...] += jnp.sum(c, axis=0, keepdims=True)

    row = pl.BlockSpec((ROW_TILE, D), lambda i: (i, 0))
    vec = pl.BlockSpec((1, D), lambda i: (0, 0))
    return pl.pallas_call(
        body, grid=(S // ROW_TILE,), in_specs=[row, row, vec, row],
        out_specs=[row, pl.BlockSpec((8, LANES), lambda i: (0, 0)), row, vec],
        out_shape=[jax.ShapeDtypeStruct((S, D), F32), jax.ShapeDtypeStruct((8, LANES), F32),
                   jax.ShapeDtypeStruct((S, D), BF16), jax.ShapeDtypeStruct((1, D), F32)],
        compiler_params=_cp("arbitrary"), name=name)(x, f, g, target)


def _bucket_tiles():
    a = np.arange(BLK)[:, None]
    b = np.arange(2 * BLK)[None, :]
    dist = a + BLK - b
    out = np.zeros((4, 2, BLK, 2 * BLK), np.int32)
    cfg = [(w // d, d) for w, d in A_GROUPS] + [(BLK - 1, 1)]
    for gi, (max_dist, d) in enumerate(cfg):
        band = (dist >= 0) & (dist <= max_dist)
        tok = np.maximum(dist, 0) * d
        nf = np.maximum(tok, 1).astype(np.float32)
        max_exact = NUM_BUCKETS // 2
        large = max_exact + (np.log(nf / np.float32(max_exact)) / np.float32(math.log(MAX_DISTANCE / max_exact))
                             * np.float32(NUM_BUCKETS - max_exact)).astype(np.int32)
        large = np.minimum(large, NUM_BUCKETS - 1)
        bkt = np.where(tok < max_exact, tok, large).astype(np.int32)
        full = np.where(band, bkt, -1)
        out[gi, 1] = full
        out[gi, 0] = np.where(b >= BLK, full, -1)
    return out


def _bias_tiles(rel_bias, buckets, name):
    def body(tab_ref, bkt_ref, o_ref):
        h = pl.program_id(0)
        bkt = bkt_ref[...]
        acc = jnp.zeros(bkt.shape, F32)
        for bb in range(NUM_BUCKETS):
            acc = jnp.where(bkt == bb, tab_ref[bb, h], acc)
        o_ref[...] = jnp.where(bkt < 0, NEG, acc)

    return pl.pallas_call(
        body, grid=(N_BIAS_HEADS,),
        in_specs=[SMEM, pl.BlockSpec((None, 2, BLK, 2 * BLK), lambda h: (jnp.minimum(h // 4, 3), 0, 0, 0))],
        out_specs=pl.BlockSpec((None, 2, BLK, 2 * BLK), lambda h: (h, 0, 0, 0)),
        out_shape=jax.ShapeDtypeStruct((N_BIAS_HEADS, 2, BLK, 2 * BLK), F32),
        compiler_params=_cp("arbitrary"), name=name)(rel_bias, buckets)


def _bias_grad(gs, buckets, name):
    ng = len(gs)

    def body(*refs):
        g_refs = refs[:ng]
        bkt_ref, o_ref = refs[ng], refs[ng + 1]
        h = pl.program_id(0)
        g = g_refs[0][...]
        for r in g_refs[1:]:
            g = g + r[...]
        bkt = bkt_ref[...]
        row = lax.broadcasted_iota(jnp.int32, (NUM_BUCKETS, LANES), 0)
        lane = lax.broadcasted_iota(jnp.int32, (NUM_BUCKETS, LANES), 1)

        @pl.when(h == 0)
        def _():
            o_ref[...] = jnp.zeros((NUM_BUCKETS, LANES), F32)

        acc = o_ref[...]
        for bb in range(NUM_BUCKETS):
            s = jnp.sum(jnp.where(bkt == bb, g, 0.0))
            acc = jnp.where((row == bb) & (lane == h), s, acc)
        o_ref[...] = acc

    g_spec = pl.BlockSpec((None, BLK, 2 * BLK), lambda h: (h, 0, 0))
    return pl.pallas_call(
        body, grid=(N_BIAS_HEADS,),
        in_specs=[g_spec] * ng + [pl.BlockSpec((None, None, BLK, 2 * BLK), lambda h: (jnp.minimum(h // 4, 3), 1, 0, 0))],
        out_specs=pl.BlockSpec((NUM_BUCKETS, LANES), lambda h: (0, 0)),
        out_shape=jax.ShapeDtypeStruct((NUM_BUCKETS, LANES), F32),
        compiler_params=_cp("arbitrary"), name=name)(*gs, buckets)


def _to_class_major(src_ref, dst_refs, d, fn=None):
    ln = S // d
    for r in range(d):
        v = src_ref[pl.ds(r, ln, stride=d), :] if d > 1 else src_ref[...]
        outs = fn(v) if fn is not None else (v,) * len(dst_refs)
        for dst, o in zip(dst_refs, outs):
            dst[pl.ds(r * ln, ln), :] = o.astype(dst.dtype)


def _head_masks(rows):
    lane = lax.broadcasted_iota(jnp.int32, (rows, LANES), 1)
    return lane < HD, lane >= HD


def _split_heads(v):
    m0, m1 = _head_masks(v.shape[0])
    return jnp.where(m0, v, 0.0), jnp.where(m1, v, 0.0)


def _dup_head(v, hi):
    m0, _ = _head_masks(v.shape[0])
    r = pltpu.roll(v, HD, 1)
    return jnp.where(m0, jnp.where(hi, r, v), jnp.where(hi, v, r))


def _block_rows(b, d):
    nbc = NB // d
    i = b % nbc
    r = b // nbc
    has_prev = (i > 0).astype(jnp.int32)
    prev = pl.multiple_of(jnp.maximum(b - 1, 0) * BLK, BLK)
    nat = i * (BLK * d) + r
    return has_prev, prev, nat


def _lane_halves(v0, v1):
    lane = lax.broadcasted_iota(jnp.int32, (v0.shape[0], LANES), 1)
    return jnp.where(lane < HD, v0, v1)


def _band_fwd(proj, bias, sinks, *, d, q0, k0, v0, npairs, bias0, shared_kv, name):
    def body(sink_ref, q_ref, k_ref, v_ref, b_ref, num_ref, st_ref, qz0, qz1, ks, vs):
        p = pl.program_id(0)
        kv = (lambda v: (_dup_head(v, p >= 2),)) if shared_kv else None
        _to_class_major(q_ref, (qz0, qz1), d, lambda v: _split_heads(v * SCALE))
        _to_class_major(k_ref, (ks,), d, kv)
        _to_class_major(v_ref, (vs,), d, kv)
        lane = lax.broadcasted_iota(jnp.int32, (BLK, LANES), 1)

        def blk(b, carry):
            has_prev, prev, nat = _block_rows(b, d)
            cur = pl.multiple_of(b * BLK, BLK)
            k2 = jnp.concatenate([ks[pl.ds(prev, BLK), :], ks[pl.ds(cur, BLK), :]], axis=0)
            v2 = jnp.concatenate([vs[pl.ds(prev, BLK), :], vs[pl.ds(cur, BLK), :]], axis=0)
            nums, ms, ls = [], [], []
            for hh, qz in enumerate((qz0, qz1)):
                z = _dot(qz[pl.ds(cur, BLK), :], k2, 1, 1) + b_ref[hh, has_prev]
                m = jnp.max(z, axis=1, keepdims=True)
                e = jnp.exp(z - m)
                l = jnp.sum(e, axis=1, keepdims=True)
                num = _dot(e.astype(BF16), v2, 1, 0)
                if shared_kv:
                    sink = sink_ref[0, 2 * p + hh]
                    mx = jnp.maximum(m, sink)
                    c = jnp.exp(m - mx)
                    zden = l * c + jnp.exp(sink - mx)
                    num = num * (c / zden)
                    m = mx + jnp.log(zden)
                ls.append(l)
                ms.append(m)
                nums.append(num)
            num_t = jnp.where(lane < HD, nums[0], nums[1])
            if shared_kv:
                st_t = jnp.where(lane < HD, ms[0], ms[1])
            else:
                st_t = jnp.where(lane < 32, ms[0], jnp.where(lane < 64, ls[0], jnp.where(lane < 96, ms[1], ls[1])))
            if d > 1:
                num_ref[pl.ds(nat, BLK, stride=d), :] = num_t
                st_ref[pl.ds(nat, BLK, stride=d), :] = st_t
            else:
                num_ref[pl.ds(cur, BLK), :] = num_t
                st_ref[pl.ds(cur, BLK), :] = st_t
            return carry

        lax.fori_loop(0, NB, blk, 0, unroll=True)

    slab = lambda off, per_pair: pl.BlockSpec((None, S, LANES), (lambda p: (off + p, 0, 0)) if per_pair else (lambda p: (off, 0, 0)))
    out = pl.BlockSpec((None, S, LANES), lambda p: (p, 0, 0))
    return pl.pallas_call(
        body, grid=(npairs,),
        in_specs=[SMEM, slab(q0, True), slab(k0, not shared_kv), slab(v0, not shared_kv),
                  pl.BlockSpec((None, 2, 2, BLK, 2 * BLK), lambda p: (bias0 + p, 0, 0, 0, 0))],
        out_specs=[out, out],
        out_shape=[jax.ShapeDtypeStruct((npairs, S, LANES), F32)] * 2,
        scratch_shapes=[pltpu.VMEM((S, LANES), BF16)] * 4,
        compiler_params=_cp("arbitrary"), name=name)(sinks, proj, proj, proj, bias)


def _combine_a(nums, stats, name):
    rt = 512

    def body(n0, n1, n2, s0, s1, s2, o_ref, l_ref):
        n_refs, s_refs = (n0, n1, n2), (s0, s1, s2)
        outs, lses = [], []
        for hh in range(2):
            ms = [s[:, 64 * hh:64 * hh + 1] for s in s_refs]
            ls = [s[:, 64 * hh + 32:64 * hh + 33] for s in s_refs]
            mx = jnp.maximum(jnp.maximum(ms[0], ms[1]), ms[2])
            cs = [jnp.exp(m - mx) for m in ms]
            z = cs[0] * ls[0] + cs[1] * ls[1] + cs[2] * ls[2]
            acc = cs[0] * n_refs[0][:, hh * HD:(hh + 1) * HD]
            acc = acc + cs[1] * n_refs[1][:, hh * HD:(hh + 1) * HD]
            acc = acc + cs[2] * n_refs[2][:, hh * HD:(hh + 1) * HD]
            outs.append(acc / z)
            lses.append(mx + jnp.log(z))
        o_ref[...] = jnp.concatenate(outs, axis=1)
        l_ref[...] = _lane_halves(lses[0], lses[1])

    spec = pl.BlockSpec((None, rt, LANES), lambda p, i: (p, i, 0))
    return pl.pallas_call(
        body, grid=(2, S // rt), in_specs=[spec] * 6, out_specs=[spec, spec],
        out_shape=[jax.ShapeDtypeStruct((2, S, LANES), F32)] * 2,
        compiler_params=_cp("parallel", "parallel"), name=name)(*nums, *stats)


def _band_bwd(proj, bias, o, do, lse, sinks, dqkv, *, d, q0, k0, v0, npairs, bias0, shared_kv, name):
    def body(sink_ref, q_ref, k_ref, v_ref, b_ref, o_ref, do_ref, lse_ref, dqkv_in, dqkv_ref, g_ref, ds_ref,
             qz0, qz1, ks, vs, doz0, doz1, ls0, ls1, dls0, dls1, stage, dq_nat, dk_cm, dv_cm, kv_nat, dk_acc, dv_acc,
             obuf, osem):
        p = pl.program_id(0)
        dq_ref, dk_ref, dv_ref = obuf.at[0], obuf.at[1], obuf.at[2]
        m0, m1 = _head_masks(S)
        kk = lax.broadcasted_iota(jnp.int32, (2 * LANES, LANES), 0) % LANES
        ll = lax.broadcasted_iota(jnp.int32, (2 * LANES, LANES), 1)
        hi, lo = _split2(do_ref[...] * o_ref[...])
        dl = _dot(jnp.concatenate([hi, lo], axis=1), ((kk < HD) == (ll < HD)).astype(BF16), 1, 0)
        if shared_kv:
            row8 = lax.broadcasted_iota(jnp.int32, (8, LANES), 0)
            lane8 = lax.broadcasted_iota(jnp.int32, (8, LANES), 1)
            sinkv = jnp.where(m0, sink_ref[0, 2 * p], sink_ref[0, 2 * p + 1])
            contrib = jnp.exp(sinkv - lse_ref[...]) * dl
            t = jnp.zeros((8, LANES), F32)
            for hh, mh in enumerate((m0, m1)):
                dsink = -jnp.sum(jnp.where(mh, contrib, 0.0)) * (1.0 / HD)
                t = jnp.where((row8 == 0) & (lane8 == hh), dsink, t)
            ds_ref[...] = t
        else:
            ds_ref[...] = jnp.zeros((8, LANES), F32)
        kv = (lambda v: (_dup_head(v, p >= 2),)) if shared_kv else None
        _to_class_major(q_ref, (qz0, qz1), d, lambda v: _split_heads(v * SCALE))
        _to_class_major(k_ref, (ks,), d, kv)
        _to_class_major(v_ref, (vs,), d, kv)
        _to_class_major(do_ref, (doz0, doz1), d, _split_heads)
        def spread(v):
            a0, a1 = _head_masks(v.shape[0])
            r = pltpu.roll(v, HD, 1)
            return jnp.where(a0, v, r), jnp.where(a1, v, r)

        _to_class_major(lse_ref, (ls0, ls1), d, spread)
        stage[...] = dl
        _to_class_major(stage, (dls0, dls1), d, spread)

        dk_cm[...] = jnp.zeros((S, LANES), F32)
        dv_cm[...] = jnp.zeros((S, LANES), F32)
        g_ref[...] = jnp.zeros((2, BLK, 2 * BLK), F32)
        lane = lax.broadcasted_iota(jnp.int32, (BLK, LANES), 1)

        def blk(b, carry):
            has_prev, prev, nat = _block_rows(b, d)
            cur = pl.multiple_of(b * BLK, BLK)
            k2 = jnp.concatenate([ks[pl.ds(prev, BLK), :], ks[pl.ds(cur, BLK), :]], axis=0)
            v2 = jnp.concatenate([vs[pl.ds(prev, BLK), :], vs[pl.ds(cur, BLK), :]], axis=0)
            dqs, dks, dvs = [], [], []
            for hh, (qz, doz, lsr, dlr) in enumerate(((qz0, doz0, ls0, dls0), (qz1, doz1, ls1, dls1))):
                qb = qz[pl.ds(cur, BLK), :]
                dob = doz[pl.ds(cur, BLK), :]
                lb = lsr[pl.ds(cur, BLK), :]
                dlb = dlr[pl.ds(cur, BLK), :]
                z = _dot(qb, k2, 1, 1) + b_ref[hh, has_prev]
                pr = jnp.exp(z - jnp.concatenate([lb, lb], axis=1))
                dp = _dot(dob, v2, 1, 1)
                dz = pr * (dp - jnp.concatenate([dlb, dlb], axis=1))
                g_ref[hh] += dz
                dzb = dz.astype(BF16)
                dqs.append(_dot(dzb, k2, 1, 0))
                dks.append(_dot(dzb, qb, 0, 0))
                dvs.append(_dot(pr.astype(BF16), dob, 0, 0))
            dq_t = jnp.where(lane < HD, dqs[0], dqs[1]) * SCALE
            dk_t = dks[0] + dks[1]
            dv_t = dvs[0] + dvs[1]
            dk_cm[pl.ds(prev, BLK), :] += dk_t[:BLK]
            dk_cm[pl.ds(cur, BLK), :] += dk_t[BLK:]
            dv_cm[pl.ds(prev, BLK), :] += dv_t[:BLK]
            dv_cm[pl.ds(cur, BLK), :] += dv_t[BLK:]
            if d > 1:
                dq_nat[pl.ds(nat, BLK, stride=d), :] = dq_t
            else:
                dq_nat[pl.ds(cur, BLK), :] = dq_t
            return carry

        lax.fori_loop(0, NB, blk, 0, unroll=True)
        dq_ref[...] = dq_nat[...].astype(BF16)

        def from_class_major(src, dst_ref):
            if d == 1:
                dst_ref[...] = src[...].astype(BF16)
            else:
                ln = S // d
                for r in range(d):
                    kv_nat[pl.ds(r, ln, stride=d), :] = src[pl.ds(r * ln, ln), :]
                dst_ref[...] = kv_nat[...].astype(BF16)

        def put(i, slab):
            return pltpu.make_async_copy(obuf.at[i], dqkv_ref.at[slab], osem.at[i])

        put(0, q0 + p).start()
        if not shared_kv:
            from_class_major(dk_cm, dk_ref)
            from_class_major(dv_cm, dv_ref)
            put(1, k0 + p).start()
            put(2, v0 + p).start()
            put(1, k0 + p).wait()
            put(2, v0 + p).wait()
        else:
            @pl.when(p == 0)
            def _():
                dk_acc[...] = jnp.zeros((S, LANES), F32)
                dv_acc[...] = jnp.zeros((S, LANES), F32)

            mine = m1 == (p >= 2)
            for cm, acc in ((dk_cm, dk_acc), (dv_cm, dv_acc)):
                val = cm[...]
                acc[...] += jnp.where(mine, val + pltpu.roll(val, HD, 1), 0.0)

            @pl.when(p == npairs - 1)
            def _():
                from_class_major(dk_acc, dk_ref)
                from_class_major(dv_acc, dv_ref)
                put(1, k0).start()
                put(2, v0).start()
                put(1, k0).wait()
                put(2, v0).wait()

        put(0, q0 + p).wait()

    slab = lambda off, per_pair: pl.BlockSpec((None, S, LANES), (lambda p: (off + p, 0, 0)) if per_pair else (lambda p: (off, 0, 0)))
    pair = pl.BlockSpec((None, S, LANES), lambda p: (p, 0, 0))
    return pl.pallas_call(
        body, grid=(npairs,),
        in_specs=[SMEM, slab(q0, True), slab(k0, not shared_kv), slab(v0, not shared_kv),
                  pl.BlockSpec((None, 2, 2, BLK, 2 * BLK), lambda p: (bias0 + p, 0, 0, 0, 0)),
                  pair, pair, pair, ANY],
        out_specs=[ANY,
                   pl.BlockSpec((None, 2, BLK, 2 * BLK), lambda p: (p, 0, 0, 0)),
                   pl.BlockSpec((None, 8, LANES), lambda p: (p, 0, 0))],
        out_shape=[jax.ShapeDtypeStruct(dqkv.shape, BF16),
                   jax.ShapeDtypeStruct((npairs, 2, BLK, 2 * BLK), F32),
                   jax.ShapeDtypeStruct((npairs, 8, LANES), F32)],
        scratch_shapes=[pltpu.VMEM((S, LANES), BF16)] * 6 + [pltpu.VMEM((S, LANES), F32)] * 11
        + [pltpu.VMEM((3, S, LANES), BF16), pltpu.SemaphoreType.DMA((3,))],
        input_output_aliases={8: 0},
        compiler_params=_cp("arbitrary"), name=name)(sinks, proj, proj, proj, bias, o, do, lse, dqkv)


KC = 512
NSUB = KC // BLK
QB = 512
QPG = KC // QB


def _split2(x):
    hi = x.astype(BF16)
    lo = (x - hi.astype(F32)).astype(BF16)
    return hi, lo


def _tri_ones(cmp):
    jj = lax.broadcasted_iota(jnp.int32, (2 * BLK, BLK), 0) % BLK
    ss = lax.broadcasted_iota(jnp.int32, (2 * BLK, BLK), 1)
    return jnp.concatenate([cmp(jj, ss).astype(BF16), jnp.ones((2 * BLK, BLK), BF16)], axis=1)


def _sub_sums(x, tri1):
    n = x.shape[0]
    st = jnp.concatenate([x[:, s * BLK:(s + 1) * BLK] for s in range(NSUB)], axis=0)
    hi, lo = _split2(st)
    r = _dot(jnp.concatenate([hi, lo], axis=1), tri1, 1, 0)
    return ([r[s * n:(s + 1) * n, :BLK] for s in range(NSUB)], [r[s * n:(s + 1) * n, BLK:] for s in range(NSUB)])


def _log_sig_pair(z):
    lb = jnp.minimum(z, 0.0) - jnp.log1p(jnp.exp(-jnp.abs(z)))
    return lb, lb - z


QGROUPS = NB // NSUB


def _stick_fwd(proj, *, q0, k0, v0, name):
    def body(q_ref, k_ref, v_ref, o_ref, t_ref, qs, ks, vs):
        qs[...] = (q_ref[...] * SCALE).astype(BF16)
        ks[...] = k_ref[...].astype(BF16)
        vs[...] = v_ref[...].astype(BF16)
        tri1 = _tri_ones(lambda j, s: j > s)
        col = lax.broadcasted_iota(jnp.int32, (QB, KC), 1)
        rowi = lax.broadcasted_iota(jnp.int32, (QB, KC), 0)

        for qg in range(QGROUPS):
            def qblock(ii, carry0, qg=qg):
                t0 = pl.multiple_of((qg * QPG + ii) * QB, QB)
                qb = qs[pl.ds(t0, QB), :]
                accs = [jnp.zeros((QB, HD), F32)] * 2
                runs = [jnp.zeros((QB, BLK), F32)] * 2
                for c in reversed(range(qg + 1)):
                    s0 = c * KC
                    diag = c == qg
                    before = (s0 + col) < (t0 + rowi) if diag else None
                    for hh in range(2):
                        kh = ks[s0:s0 + KC, hh * HD:(hh + 1) * HD]
                        vh = vs[s0:s0 + KC, hh * HD:(hh + 1) * HD]
                        lb, lk = _log_sig_pair(_dot(qb[:, hh * HD:(hh + 1) * HD], kh, 1, 1))
                        if diag:
                            lk = jnp.where(before, lk, 0.0)
                        suf, tot = _sub_sums(lk, tri1)
                        ws, run = [], runs[hh]
                        for s in reversed(range(NSUB)):
                            ws.append(jnp.exp(lb[:, s * BLK:(s + 1) * BLK] + suf[s] + run))
                            run = run + tot[s]
                        w = jnp.concatenate(ws[::-1], axis=1)
                        if diag:
                            w = jnp.where(before, w, 0.0)
                        accs[hh] = accs[hh] + _dot(w.astype(BF16), vh, 1, 0)
                        runs[hh] = run
                o_ref[pl.ds(t0, QB), :] = jnp.concatenate(accs, axis=1)
                t_ref[pl.ds(t0, QB), :] = _lane_halves(runs[0], runs[1])
                return carry0

            lax.fori_loop(0, QPG, qblock, 0)

    slab = lambda off: pl.BlockSpec((None, S, LANES), lambda p: (off + p, 0, 0))
    out = pl.BlockSpec((None, S, LANES), lambda p: (p, 0, 0))
    return pl.pallas_call(
        body, grid=(2,), in_specs=[slab(q0), slab(k0), slab(v0)], out_specs=[out, out],
        out_shape=[jax.ShapeDtypeStruct((2, S, LANES), F32)] * 2,
        scratch_shapes=[pltpu.VMEM((S, LANES), BF16)] * 3,
        compiler_params=_cp("arbitrary"), name=name)(proj, proj, proj)


def _stick_bwd(proj, do, tot, dqkv, *, q0, k0, v0, name):
    def body(q_ref, k_ref, v_ref, do_ref, t_ref, dqkv_in, dqkv_ref, qs, ks, vs, dos, dk_acc, dv_acc, obuf, osem):
        p = pl.program_id(0)
        dq_ref, dk_ref, dv_ref = obuf.at[0], obuf.at[1], obuf.at[2]
        qs[...] = (q_ref[...] * SCALE).astype(BF16)
        ks[...] = k_ref[...].astype(BF16)
        vs[...] = v_ref[...].astype(BF16)
        dos[...] = do_ref[...].astype(BF16)
        dk_acc[...] = jnp.zeros((2, S, HD), F32)
        dv_acc[...] = jnp.zeros((2, S, HD), F32)
        tri_inc = _tri_ones(lambda j, s: j <= s)
        tri_exc = _tri_ones(lambda j, s: j < s)
        col = lax.broadcasted_iota(jnp.int32, (QB, KC), 1)
        rowi = lax.broadcasted_iota(jnp.int32, (QB, KC), 0)

        for qg in range(QGROUPS):
            def qblock(ii, carry0, qg=qg):
                t0 = pl.multiple_of((qg * QPG + ii) * QB, QB)
                qb = qs[pl.ds(t0, QB), :]
                dob = dos[pl.ds(t0, QB), :]
                tb = t_ref[pl.ds(t0, QB), :]
                dqs = [jnp.zeros((QB, HD), F32)] * 2
                pruns = [jnp.zeros((QB, BLK), F32)] * 2
                eruns = [jnp.zeros((QB, BLK), F32)] * 2
                for c in range(qg + 1):
                    s0 = c * KC
                    diag = c == qg
                    before = (s0 + col) < (t0 + rowi) if diag else None
                    for hh in range(2):
                        qh = qb[:, hh * HD:(hh + 1) * HD]
                        doh = dob[:, hh * HD:(hh + 1) * HD]
                        tt = tb[:, 64 * hh:64 * hh + 1]
                        kh = ks[s0:s0 + KC, hh * HD:(hh + 1) * HD]
                        vh = vs[s0:s0 + KC, hh * HD:(hh + 1) * HD]
                        lb, lk = _log_sig_pair(_dot(qh, kh, 1, 1))
                        if diag:
                            lk = jnp.where(before, lk, 0.0)
                        pin, ptot = _sub_sums(lk, tri_inc)
                        ws, prun = [], pruns[hh]
                        for s in range(NSUB):
                            ws.append(jnp.exp(lb[:, s * BLK:(s + 1) * BLK] + (tt - (pin[s] + prun))))
                            prun = prun + ptot[s]
                        w = jnp.concatenate(ws, axis=1)
                        if diag:
                            w = jnp.where(before, w, 0.0)
                        e = w * _dot(doh, vh, 1, 1)
                        pex, etot = _sub_sums(e, tri_exc)
                        cs, erun = [], eruns[hh]
                        for s in range(NSUB):
                            cs.append(pex[s] + erun)
                            erun = erun + etot[s]
                        sig = jnp.exp(lb)
                        dz = e * (1.0 - sig) - jnp.concatenate(cs, axis=1) * sig
                        if diag:
                            dz = jnp.where(before, dz, 0.0)
                        dz = dz.astype(BF16)
                        dqs[hh] = dqs[hh] + _dot(dz, kh, 1, 0)
                        dk_acc[hh, s0:s0 + KC, :] += _dot(dz, qh, 0, 0)
                        dv_acc[hh, s0:s0 + KC, :] += _dot(w.astype(BF16), doh, 0, 0)
                        pruns[hh], eruns[hh] = prun, erun
                dq_ref[pl.ds(t0, QB), :] = (jnp.concatenate(dqs, axis=1) * SCALE).astype(BF16)
                return carry0

            lax.fori_loop(0, QPG, qblock, 0)
        dk_ref[...] = jnp.concatenate([dk_acc[0], dk_acc[1]], axis=1).astype(BF16)
        dv_ref[...] = jnp.concatenate([dv_acc[0], dv_acc[1]], axis=1).astype(BF16)
        puts = [pltpu.make_async_copy(obuf.at[i], dqkv_ref.at[off + p], osem.at[i]) for i, off in enumerate((q0, k0, v0))]
        for cp in puts:
            cp.start()
        for cp in puts:
            cp.wait()

    slab = lambda off: pl.BlockSpec((None, S, LANES), lambda p: (off + p, 0, 0))
    pair = pl.BlockSpec((None, S, LANES), lambda p: (p, 0, 0))
    return pl.pallas_call(
        body, grid=(2,), in_specs=[slab(q0), slab(k0), slab(v0), pair, pair, ANY], out_specs=ANY,
        out_shape=jax.ShapeDtypeStruct(dqkv.shape, BF16),
        scratch_shapes=[pltpu.VMEM((S, LANES), BF16)] * 4 + [pltpu.VMEM((2, S, HD), F32)] * 2
        + [pltpu.VMEM((3, S, LANES), BF16), pltpu.SemaphoreType.DMA((3,))],
        input_output_aliases={5: 0},
        compiler_params=_cp("arbitrary"), name=name)(proj, proj, proj, do, tot, dqkv)


def _cat_slabs(ref):
    return jnp.concatenate([ref[s] for s in range(ref.shape[0])], axis=1)


def _merge_fwd(o_a, o_b, o_c, gates, b_gate, wa, wb, wc, w_out, name):
    tm = MERGE_TILE

    def body(oa_ref, ob_ref, oc_ref, g_ref, bg_ref, wa_ref, wb_ref, wc_ref, wo_ref, mg_ref, mo_ref):
        acc = jnp.zeros((tm, D), F32)
        for i, (o_ref, w_ref) in enumerate(((oa_ref, wa_ref), (ob_ref, wb_ref), (oc_ref, wc_ref))):
            pr = _dot(_cat_slabs(o_ref).astype(BF16), w_ref[...], 1, 0)
            sg = jax.nn.sigmoid(g_ref[:, i * D:(i + 1) * D] + bg_ref[i:i + 1, :])
            acc = acc + sg * pr
        mg = acc.astype(BF16)
        mg_ref[...] = mg
        mo_ref[...] = _dot(mg, wo_ref[...], 1, 0)

    slabs = lambda n: pl.BlockSpec((n, tm, LANES), lambda i: (0, i, 0))
    full = lambda r, c: pl.BlockSpec((r, c), lambda i: (0, 0))
    row = pl.BlockSpec((tm, D), lambda i: (i, 0))
    return pl.pallas_call(
        body, grid=(S // tm,),
        in_specs=[slabs(2), slabs(4), slabs(2), pl.BlockSpec((tm, GATE_COLS), lambda i: (i, 0)), full(3, D),
                  full(256, D), full(512, D), full(256, D), full(D, D)],
        out_specs=[row, row],
        out_shape=[jax.ShapeDtypeStruct((S, D), BF16), jax.ShapeDtypeStruct((S, D), F32)],
        compiler_params=_cp("parallel"), name=name)(o_a, o_b, o_c, gates, b_gate, wa, wb, wc, w_out)


def _merge_bwd(d_mo, o_a, o_b, o_c, gates, b_gate, wa, wb, wc, w_out, name):
    tm = MERGE_TILE
    nsteps = S // tm

    def body(dmo_ref, oa_ref, ob_ref, oc_ref, g_ref, bg_ref, wa_ref, wb_ref, wc_ref, wo_ref,
             doa_ref, dob_ref, doc_ref, dg_ref, dwa_ref, dwb_ref, dwc_ref, dbg_ref, acc_a, acc_b, acc_c):
        @pl.when(pl.program_id(0) == 0)
        def _():
            acc_a[...] = jnp.zeros(acc_a.shape, F32)
            acc_b[...] = jnp.zeros(acc_b.shape, F32)
            acc_c[...] = jnp.zeros(acc_c.shape, F32)
            dbg_ref[...] = jnp.zeros(dbg_ref.shape, F32)

        dmg = _dot(dmo_ref[...], wo_ref[...], 1, 1)
        trip = ((oa_ref, wa_ref, doa_ref, acc_a), (ob_ref, wb_ref, dob_ref, acc_b), (oc_ref, wc_ref, doc_ref, acc_c))
        for i, (o_ref, w_ref, do_ref, dw_ref) in enumerate(trip):
            ob = _cat_slabs(o_ref).astype(BF16)
            pr = _dot(ob, w_ref[...], 1, 0)
            sg = jax.nn.sigmoid(g_ref[:, i * D:(i + 1) * D] + bg_ref[i:i + 1, :])
            dgate = dmg * pr * sg * (1.0 - sg)
            dg_ref[:, i * D:(i + 1) * D] = dgate.astype(BF16)
            dbg_ref[i:i + 1, :] += jnp.sum(dgate, axis=0, keepdims=True)
            dpr = (dmg * sg).astype(BF16)
            do = _dot(dpr, w_ref[...], 1, 1)
            for s in range(do_ref.shape[0]):
                do_ref[s] = do[:, s * LANES:(s + 1) * LANES]
            dw_ref[...] += _dot(ob, dpr, 0, 0)

        @pl.when(pl.program_id(0) == nsteps - 1)
        def _():
            dwa_ref[...] = acc_a[...].astype(BF16)
            dwb_ref[...] = acc_b[...].astype(BF16)
            dwc_ref[...] = acc_c[...].astype(BF16)

    slabs = lambda n: pl.BlockSpec((n, tm, LANES), lambda i: (0, i, 0))
    full = lambda r, c: pl.BlockSpec((r, c), lambda i: (0, 0))
    row = pl.BlockSpec((tm, D), lambda i: (i, 0))
    return pl.pallas_call(
        body, grid=(S // tm,),
        in_specs=[row, slabs(2), slabs(4), slabs(2), pl.BlockSpec((tm, GATE_COLS), lambda i: (i, 0)), full(3, D),
                  full(256, D), full(512, D), full(256, D), full(D, D)],
        out_specs=[slabs(2), slabs(4), slabs(2), pl.BlockSpec((tm, GATE_COLS), lambda i: (i, 0)),
                   full(256, D), full(512, D), full(256, D), full(3, D)],
        out_shape=[jax.ShapeDtypeStruct((2, S, LANES), F32), jax.ShapeDtypeStruct((4, S, LANES), F32),
                   jax.ShapeDtypeStruct((2, S, LANES), F32), jax.ShapeDtypeStruct((S, GATE_COLS), BF16),
                   jax.ShapeDtypeStruct((256, D), BF16), jax.ShapeDtypeStruct((512, D), BF16),
                   jax.ShapeDtypeStruct((256, D), BF16), jax.ShapeDtypeStruct((3, D), F32)],
        scratch_shapes=[pltpu.VMEM((256, D), F32), pltpu.VMEM((512, D), F32), pltpu.VMEM((256, D), F32)],
        compiler_params=_cp("arbitrary"), name=name)(d_mo, o_a, o_b, o_c, gates, b_gate, wa, wb, wc, w_out)


FC = 256
GELU_K = math.sqrt(2.0 / math.pi)
GELU_C = 0.044715


RC = 64
NRC = S // RC


def _down(tail, cur, n):
    row = lax.broadcasted_iota(jnp.int32, tail.shape, 0)
    rolled = pltpu.roll(cur, n, 0)
    first = jnp.where(row < n, pltpu.roll(tail, n, 0), rolled[0:8])
    return jnp.concatenate([first, rolled[8:]], axis=0)


def _up(cur, head, n):
    row = lax.broadcasted_iota(jnp.int32, head.shape, 0)
    rolled = pltpu.roll(cur, RC - n, 0)
    last = jnp.where(row >= 8 - n, pltpu.roll(head, 8 - n, 0), rolled[RC - 8:])
    return jnp.concatenate([rolled[:RC - 8], last], axis=0)


def _conv_chunk(load, j, w_ref, b_ref, half):
    r0 = pl.multiple_of(j * RC, RC)
    cur = load(r0, RC).astype(F32)
    tail = load(pl.multiple_of(jnp.maximum(r0 - 16, 0), 16), 16).astype(F32)[8:16]
    tail = jnp.where(j > 0, tail, 0.0)
    d1 = _down(tail, cur, 1)
    d2 = _down(tail, cur, 2)
    y = w_ref[0:1, half, :] * d2 + w_ref[1:2, half, :] * d1 + w_ref[2:3, half, :] * cur + b_ref[half:half + 1, :]
    return y, cur, d1, d2


def _chunk(j):
    return pl.ds(pl.multiple_of(j * RC, RC), RC)


def _fold8(x):
    return jnp.sum(x.reshape(RC // 8, 8, x.shape[-1]), axis=0)


def _ffn_act(u, conv_w, conv_b, name):
    def body(u_ref, w_ref, b_ref, a_ref, y_ref):
        def step(j, carry):
            yg = _conv_chunk(lambda r, n: u_ref[0, pl.ds(r, n), :], j, w_ref, b_ref, 0)[0]
            yv = _conv_chunk(lambda r, n: u_ref[1, pl.ds(r, n), :], j, w_ref, b_ref, 1)[0]
            th = jnp.tanh(GELU_K * (yg + GELU_C * yg * yg * yg))
            a_ref[_chunk(j), :] = (0.5 * yg * (1.0 + th) * yv).astype(BF16)
            y_ref[0, _chunk(j), :] = yg.astype(BF16)
            y_ref[1, _chunk(j), :] = yv.astype(BF16)
            return carry

        lax.fori_loop(0, NRC, step, 0)

    return pl.pallas_call(
        body, grid=(D_FF // FC,),
        in_specs=[pl.BlockSpec((2, S, FC), lambda j: (0, 0, j)), pl.BlockSpec((3, 2, FC), lambda j: (0, 0, j)),
                  pl.BlockSpec((2, FC), lambda j: (0, j))],
        out_specs=[pl.BlockSpec((S, FC), lambda j: (0, j)), pl.BlockSpec((2, S, FC), lambda j: (0, 0, j))],
        out_shape=[jax.ShapeDtypeStruct((S, D_FF), BF16), jax.ShapeDtypeStruct((2, S, D_FF), BF16)],
        compiler_params=_cp("parallel"), name=name)(u, conv_w, conv_b)


def _ffn_act_bwd(u, y, d_a, conv_w, name):
    def body(u_ref, y_ref, da_ref, w_ref, du_ref, dw_ref, db_ref, dy_s):
        def first(j, acc):
            yg = y_ref[0, _chunk(j), :].astype(F32)
            yv = y_ref[1, _chunk(j), :].astype(F32)
            th = jnp.tanh(GELU_K * (yg + GELU_C * yg * yg * yg))
            gelu = 0.5 * yg * (1.0 + th)
            dgelu = 0.5 * (1.0 + th) + 0.5 * yg * (1.0 - th * th) * GELU_K * (1.0 + 3.0 * GELU_C * yg * yg)
            da = da_ref[_chunk(j), :].astype(F32)
            dyg = da * yv * dgelu
            dyv = da * gelu
            dy_s[0, _chunk(j), :] = dyg
            dy_s[1, _chunk(j), :] = dyv
            return acc[0] + _fold8(dyg), acc[1] + _fold8(dyv)

        zero = jnp.zeros((8, FC), F32)
        accb = lax.fori_loop(0, NRC, first, (zero, zero))
        for half in range(2):
            db_ref[half:half + 1, :] = jnp.sum(accb[half], axis=0, keepdims=True)

        def second(j, acc):
            new = []
            for half in range(2):
                cur = dy_s[half, _chunk(j), :]
                h0 = pl.multiple_of(jnp.minimum((j + 1) * RC, S - 8), 8)
                head = jnp.where(j < NRC - 1, dy_s[half, pl.ds(h0, 8), :], 0.0)
                up1 = _up(cur, head, 1)
                up2 = _up(cur, head, 2)
                du = w_ref[2:3, half, :] * cur + w_ref[1:2, half, :] * up1 + w_ref[0:1, half, :] * up2
                du_ref[half, _chunk(j), :] = du.astype(BF16)
                uu = u_ref[half, _chunk(j), :].astype(F32)
                new += [_fold8(up2 * uu), _fold8(up1 * uu), _fold8(cur * uu)]
            return tuple(a + n for a, n in zip(acc, new))

        accw = lax.fori_loop(0, NRC, second, tuple(zero for _ in range(6)))
        for half in range(2):
            for k in range(3):
                dw_ref[k:k + 1, half, :] = jnp.sum(accw[3 * half + k], axis=0, keepdims=True)

    return pl.pallas_call(
        body, grid=(D_FF // FC,),
        in_specs=[pl.BlockSpec((2, S, FC), lambda j: (0, 0, j)), pl.BlockSpec((2, S, FC), lambda j: (0, 0, j)),
                  pl.BlockSpec((S, FC), lambda j: (0, j)), pl.BlockSpec((3, 2, FC), lambda j: (0, 0, j))],
        out_specs=[pl.BlockSpec((2, S, FC), lambda j: (0, 0, j)), pl.BlockSpec((3, 2, FC), lambda j: (0, 0, j)),
                   pl.BlockSpec((2, FC), lambda j: (0, j))],
        out_shape=[jax.ShapeDtypeStruct((2, S, D_FF), BF16), jax.ShapeDtypeStruct((3, 2, D_FF), F32),
                   jax.ShapeDtypeStruct((2, D_FF), F32)],
        scratch_shapes=[pltpu.VMEM((2, S, FC), F32)],
        compiler_params=_cp("parallel"), name=name)(u, y, d_a, conv_w)


def _layer_fwd(x, h1, w, bias, lname):
    n = lambda s: f"{lname}_{s}"
    w.need("in", h1)
    tn = 768
    proj = _mm(h1, w["w_in"], grid=(1, QKV_COLS // tn, 1),
               a_spec=pl.BlockSpec((S, D), lambda i, j, k: (i, 0)),
               b_spec=pl.BlockSpec((tn, D), lambda i, j, k: (j, 0)),
               out_shape=jax.ShapeDtypeStruct((QKV_SLABS, S, LANES), F32),
               out_spec=pl.BlockSpec((tn // LANES, S, LANES), lambda i, j, k: (j, i, 0)),
               ca=1, cb=1, acc_shape=(S, tn), out_slab=True, name=n("proj_qkv"))
    gates = _mm(h1, w["w_in"], grid=(1, GATE_COLS // tn, 1),
                a_spec=pl.BlockSpec((S, D), lambda i, j, k: (i, 0)),
                b_spec=pl.BlockSpec((tn, D), lambda i, j, k: (j + QKV_COLS // tn, 0)),
                out_shape=jax.ShapeDtypeStruct((S, GATE_COLS), BF16),
                out_spec=pl.BlockSpec((S, tn), lambda i, j, k: (i, j)),
                ca=1, cb=1, acc_shape=(S, tn), name=n("proj_gate"))
    nums, stats = [], []
    for g, (_, d) in enumerate(A_GROUPS):
        nm, st = _band_fwd(proj, bias, w["sinks"], d=d, q0=2 * g, k0=6 + 2 * g, v0=12 + 2 * g, npairs=2, bias0=2 * g,
                           shared_kv=False, name=n(f"attn_a{g}_fwd"))
        nums.append(nm)
        stats.append(st)
    o_a, lse_a = _combine_a(nums, stats, n("attn_a_combine"))
    o_b, lse_b = _band_fwd(proj, bias, w["sinks"], d=1, q0=18, k0=22, v0=23, npairs=4, bias0=6, shared_kv=True,
                           name=n("attn_b_fwd"))
    o_c, tot_c = _stick_fwd(proj, q0=24, k0=26, v0=28, name=n("attn_c_fwd"))
    w.need("mix", tot_c)
    merged, mo = _merge_fwd(o_a, o_b, o_c, gates, w["b_gate"], w["w_br_a"], w["w_br_b"], w["w_br_c"], w["w_out"], n("merge_fwd"))
    w.need("pre_ffn", mo)
    x2, h2 = _postnorm_res(x, mo, w["attn_post_norm"], w["ffn_pre_norm"], n("attn_post"))
    w.need("ffn", h2)
    u = _mm(h2, w["w_up"], grid=(1, 2 * D_FF // 1024, 1),
            a_spec=pl.BlockSpec((S, D), lambda i, j, k: (i, 0)),
            b_spec=pl.BlockSpec((D, 1024), lambda i, j, k: (0, j)),
            out_shape=jax.ShapeDtypeStruct((2, S, D_FF), BF16),
            out_spec=pl.BlockSpec((None, S, 1024), lambda i, j, k: (j // 4, i, j % 4)),
            ca=1, cb=0, acc_shape=(S, 1024), name=n("ffn_up"))
    a, y = _ffn_act(u, w["conv_w"], w["conv_b"], n("ffn_act"))
    fo = _mm_nn(a, w["w_down"], F32, 1024, 1024, 2048, n("ffn_down"))
    saved = dict(x=x, h1=h1, proj=proj, gates=gates, o_a=o_a, lse_a=lse_a, o_b=o_b, lse_b=lse_b, o_c=o_c, tot_c=tot_c,
                 merged=merged, mo=mo, x2=x2, h2=h2, u=u, y=y, a=a, fo=fo)
    return saved


def _layer_bwd(dx3, sv, w, bias, lname, tok=None, on_part=None, d_fo=None, below=None):
    n = lambda s: f"{lname}_{s}"
    g = {}

    def part(group, vec):
        t = on_part(group, g) if on_part is not None else None
        return vec if t is None else vec + t

    if d_fo is None:
        gain = w["ffn_post_norm"] if tok is None else w["ffn_post_norm"] + tok
        d_fo, g["ffn_post_norm"] = _norm_bwd(sv["fo"], gain, [dx3], None, BF16, n("ffn_post_bwd"))
    else:
        d_fo, g["ffn_post_norm"] = d_fo
    d_a = _mm_nt(d_fo, w["w_down"], BF16, S, 1024, 1024, n("ffn_down_bwd_x"))
    g["w_down"] = _mm_tn(sv["a"], d_fo, BF16, 1024, 1024, S, n("ffn_down_bwd_w"))
    d_u, dcw, dcb = _ffn_act_bwd(sv["u"], sv["y"], d_a, w["conv_w"], n("ffn_act_bwd"))
    g["conv_w"] = dcw.reshape(3, 2 * D_FF)
    g["conv_b"] = dcb.reshape(1, 2 * D_FF)
    g["w_up"] = _mm(sv["h2"], d_u, grid=(1, 2 * D_FF // 1024, 1),
                    a_spec=pl.BlockSpec((S, D), lambda i, j, k: (k, 0)),
                    b_spec=pl.BlockSpec((None, S, 1024), lambda i, j, k: (j // 4, k, j % 4)),
                    out_shape=jax.ShapeDtypeStruct((D, 2 * D_FF), BF16),
                    out_spec=pl.BlockSpec((D, 1024), lambda i, j, k: (0, j)),
                    ca=0, cb=0, acc_shape=(D, 1024), name=n("ffn_up_bwd_w"))
    tok_ffn = on_part("ffn", g) if on_part is not None else None
    d_h2 = _mm(d_u, w["w_up"], grid=(S // 1024, 1, 2),
               a_spec=pl.BlockSpec((None, 1024, D_FF), lambda i, j, k: (k, i, 0)),
               b_spec=pl.BlockSpec((D, D_FF), lambda i, j, k: (0, k)),
               out_shape=jax.ShapeDtypeStruct((S, D), F32),
               out_spec=pl.BlockSpec((1024, D), lambda i, j, k: (i, 0)),
               ca=1, cb=1, acc_shape=(1024, D), after=tok_ffn, name=n("ffn_up_bwd_x"))
    dx2, d_mo, g["ffn_pre_norm"], g["attn_post_norm"] = _norm_bwd_chain(
        sv["x2"], w["ffn_pre_norm"], [d_h2], dx3, sv["mo"], w["attn_post_norm"], n("ffn_pre_attn_post_bwd"))
    g["w_out"] = _mm_tn(sv["merged"], d_mo, BF16, 1024, 1024, S, n("out_bwd_w"))
    do_a, do_b, do_c, d_gates, dwa, dwb, dwc, g["b_gate"] = _merge_bwd(
        d_mo, sv["o_a"], sv["o_b"], sv["o_c"], sv["gates"], w["b_gate"], w["w_br_a"], w["w_br_b"], w["w_br_c"],
        w["w_out"], n("merge_bwd"))
    g["w_br_a"], g["w_br_b"], g["w_br_c"] = dwa, dwb, dwc
    sinks = part("mix", w["sinks"])
    proj = sv["proj"]
    dqkv = lax.empty((QKV_SLABS, S, LANES), BF16)
    gbias = []
    for gi, (_, d) in enumerate(A_GROUPS):
        dqkv, gg, _ = _band_bwd(proj, bias, sv["o_a"], do_a, sv["lse_a"], sinks, dqkv, d=d, q0=2 * gi, k0=6 + 2 * gi,
                                v0=12 + 2 * gi, npairs=2, bias0=2 * gi, shared_kv=False, name=n(f"attn_a{gi}_bwd"))
        gbias.append(gg)
    dqkv, ggb, dsink = _band_bwd(proj, bias, sv["o_b"], do_b, sv["lse_b"], sinks, dqkv, d=1, q0=18, k0=22, v0=23,
                                 npairs=4, bias0=6, shared_kv=True, name=n("attn_b_bwd"))
    gbias.append(ggb)
    g["bias_g"] = jnp.concatenate(gbias, axis=0).reshape(N_BIAS_HEADS, BLK, 2 * BLK)
    g["sinks"] = dsink[:, 0, :2].reshape(1, 8)
    dqkv = _stick_bwd(proj, do_c, sv["tot_c"], dqkv, q0=24, k0=26, v0=28, name=n("attn_c_bwd"))
    ts = 6
    tsx = QKV_SLABS
    dw_in = _mm(dqkv, sv["h1"], grid=(QKV_SLABS // ts, 1, 1),
                a_spec=pl.BlockSpec((ts, S, LANES), lambda i, j, k: (i, k, 0)),
                b_spec=pl.BlockSpec((S, D), lambda i, j, k: (k, 0)),
                out_shape=jax.ShapeDtypeStruct((IN_COLS, D), BF16),
                out_spec=pl.BlockSpec((ts * LANES, D), lambda i, j, k: (i, 0)),
                ca=0, cb=0, acc_shape=(ts * LANES, D), a_slab=True, name=n("in_bwd_w_qkv"))
    g["w_in"] = _mm(d_gates, sv["h1"], grid=(GATE_COLS // 768, 1, 1),
                    a_spec=pl.BlockSpec((S, 768), lambda i, j, k: (k, i)),
                    b_spec=pl.BlockSpec((S, D), lambda i, j, k: (k, 0)),
                    out_shape=jax.ShapeDtypeStruct((IN_COLS, D), BF16),
                    out_spec=pl.BlockSpec((768, D), lambda i, j, k: (i + QKV_COLS // 768, 0)),
                    ca=0, cb=0, acc_shape=(768, D), alias_out=dw_in, name=n("in_bwd_w_gate"))
    tok_in = on_part("in", g) if on_part is not None else None
    d_h1a = _mm(dqkv, w["w_in"], grid=(S // 1024, 1, QKV_SLABS // tsx),
                a_spec=pl.BlockSpec((tsx, 1024, LANES), lambda i, j, k: (k, i, 0)),
                b_spec=pl.BlockSpec((tsx * LANES, D), lambda i, j, k: (k, 0)),
                out_shape=jax.ShapeDtypeStruct((S, D), F32),
                out_spec=pl.BlockSpec((1024, D), lambda i, j, k: (i, 0)),
                ca=1, cb=0, acc_shape=(1024, D), a_slab=True, after=tok_in, name=n("in_bwd_x_qkv"))
    d_h1b = _mm(d_gates, w["w_in"], grid=(S // 1024, 1, GATE_COLS // 768),
                a_spec=pl.BlockSpec((1024, 768), lambda i, j, k: (i, k)),
                b_spec=pl.BlockSpec((768, D), lambda i, j, k: (k + QKV_COLS // 768, 0)),
                out_shape=jax.ShapeDtypeStruct((S, D), F32),
                out_spec=pl.BlockSpec((1024, D), lambda i, j, k: (i, 0)),
                ca=1, cb=0, acc_shape=(1024, D), after=tok_in, name=n("in_bwd_x_gate"))
    if below is None:
        dx, g["attn_pre_norm"] = _norm_bwd(sv["x"], w["attn_pre_norm"], [d_h1a, d_h1b], dx2, F32, n("attn_pre_bwd"))
        return dx, g, tok_in, None
    dx, d_fo_below, g["attn_pre_norm"], dg_below = _norm_bwd_chain(
        sv["x"], w["attn_pre_norm"], [d_h1a, d_h1b], dx2, below[0], below[1], n("attn_pre_ffn_post_bwd"))
    return dx, g, tok_in, (d_fo_below, dg_below)


def _local_step(x, target, ws, rel_bias, tok=None, on_grads=None):
    buckets = jnp.asarray(_bucket_tiles())
    bias = _bias_tiles(rel_bias, buckets, "bias_tiles").reshape(N_BIAS_HEADS // 2, 2, 2, BLK, 2 * BLK)
    saved = []
    gain0 = ws[0]["attn_pre_norm"] if tok is None else ws[0]["attn_pre_norm"] + tok
    h1 = _prenorm(x, gain0, "l0_attn_pre")
    for l in range(DEPTH):
        sv = _layer_fwd(x, h1, ws[l], bias, f"l{l}")
        saved.append(sv)
        if l + 1 < DEPTH:
            x, h1 = _postnorm_res(sv["x2"], sv["fo"], ws[l]["ffn_post_norm"], ws[l + 1]["attn_pre_norm"], f"l{l}_ffn_post")
    top = saved[-1]
    dy, loss_tile, d_fo_top, dg_top = _loss_head(top["x2"], top["fo"], ws[-1]["ffn_post_norm"], target, "loss_head")
    grads = [None] * DEPTH
    tok, d_fo = None, (d_fo_top, dg_top)
    for l in reversed(range(DEPTH)):
        on_part = None if on_grads is None else functools.partial(on_grads, l)
        below = (saved[l - 1]["fo"], ws[l - 1]["ffn_post_norm"]) if l > 0 else None
        dy, grads[l], tok, d_fo = _layer_bwd(dy, saved[l], ws[l], bias, f"l{l}", tok, on_part, d_fo, below)
    g_rel = _bias_grad([grads[l]["bias_g"] for l in range(DEPTH)], buckets, "bias_grad")[:, :N_BIAS_HEADS]
    return loss_tile, dy, grads, g_rel


def _coords():
    return lax.axis_index("x"), lax.axis_index("y"), lax.axis_index("c")


def _peer(rel):
    x, y, c = _coords()
    return (1 - x if rel & 4 else x, 1 - y if rel & 2 else y, 1 - c if rel & 1 else c)


def _exchange(srcs, dst_shapes, src_win, dst_win, name, after=None):
    nt = len(srcs)
    extra = [] if after is None else [after]

    def body(*refs):
        src_refs, dst_refs = refs[:nt], refs[nt + len(extra):2 * nt + len(extra)]
        send_sems, recv_sems, local_sems = refs[2 * nt + len(extra):]
        x, y, c = _coords()
        me = 4 * x + 2 * y + c
        locals_ = []
        for t in range(nt):
            cp = pltpu.make_async_copy(src_win(t, src_refs[t], me), dst_win(t, dst_refs[t], me), local_sems.at[t])
            cp.start()
            locals_.append(cp)
        sends = []
        for rel in range(1, NDEV):
            px, py, pc = _peer(rel)
            q = 4 * px + 2 * py + pc
            for t in range(nt):
                cp = pltpu.make_async_remote_copy(
                    src_ref=src_win(t, src_refs[t], q), dst_ref=dst_win(t, dst_refs[t], me),
                    send_sem=send_sems.at[rel - 1, t], recv_sem=recv_sems.at[rel - 1, t],
                    device_id=(px, py, pc), device_id_type=MESH)
                cp.start()
                sends.append(cp)
        for rel in range(1, NDEV):
            px, py, pc = _peer(rel)
            q = 4 * px + 2 * py + pc
            for t in range(nt):
                pltpu.make_async_remote_copy(
                    src_ref=src_win(t, src_refs[t], me), dst_ref=dst_win(t, dst_refs[t], q),
                    send_sem=send_sems.at[rel - 1, t], recv_sem=recv_sems.at[rel - 1, t],
                    device_id=(px, py, pc), device_id_type=MESH).wait_recv()
        for cp in sends:
            cp.wait_send()
        for cp in locals_:
            cp.wait()

    return pl.pallas_call(
        body, in_specs=[ANY] * (nt + len(extra)), out_specs=[ANY] * nt, out_shape=dst_shapes,
        scratch_shapes=[pltpu.SemaphoreType.DMA((NDEV - 1, nt)), pltpu.SemaphoreType.DMA((NDEV - 1, nt)),
                        pltpu.SemaphoreType.DMA((nt,))],
        name=name)(*srcs, *extra)


BIG = (("w_in", 0, 864), ("w_br_a", 1, 128), ("w_br_b", 1, 128), ("w_br_c", 1, 128), ("w_out", 0, 128),
       ("w_up", 1, 1024), ("w_down", 0, 512))


NBIG = len(BIG)
BIG_FULL = {"w_in": (IN_COLS, D), "w_br_a": (256, D), "w_br_b": (512, D), "w_br_c": (256, D), "w_out": (D, D),
            "w_up": (D, 2 * D_FF), "w_down": (D_FF, D)}
SHARD_ROWS = {"w_in": 288, "w_up": 256, "w_down": 256}
LAYER_GROUPS = (("in", (0,)), ("mix", (1, 2, 3, 4)), ("ffn", (5, 6)))

HBM_SPEC = pl.BlockSpec(memory_space=pltpu.HBM)
SEM_SPEC = pl.BlockSpec(memory_space=pltpu.SEMAPHORE)


def _hbm(a):
    return pltpu.with_memory_space_constraint(a, pltpu.HBM)


def _shard_window(t, ref, k):
    nm, ax, ext = BIG[t % NBIG]
    off = pl.multiple_of(k * ext, ext)
    if ax == 0:
        return ref.at[pl.ds(off, ext), :]
    return ref.at[:, pl.ds(off, ext)]


def _whole(t, ref, k):
    return ref


def _slot(t, ref, k):
    return ref.at[k]


def _own_block_spec(t, rows, me_of):
    nm, ax, ext = BIG[t % NBIG]
    r, c = BIG_FULL[nm]
    if ax == 0:
        return pl.BlockSpec((rows, c), lambda i, m: (me_of(m) * (ext // rows) + i, 0))
    return pl.BlockSpec((rows, ext), lambda i, m: (i, me_of(m)))


def _cast_own(t, shards, me_arr, name):
    nm, ax, ext = BIG[t % NBIG]
    layer = t // NBIG
    _, nr, nc = shards.shape
    rows = SHARD_ROWS.get(nm, nr)
    shape = BIG_FULL[nm]

    def body(m_ref, s_ref, o_ref):
        o_ref[...] = s_ref[...].astype(BF16)

    return pl.pallas_call(
        body, grid_spec=pltpu.PrefetchScalarGridSpec(
            num_scalar_prefetch=1, grid=(nr // rows,),
            in_specs=[pl.BlockSpec((None, rows, nc), lambda i, m: (layer, i, 0))],
            out_specs=_own_block_spec(t, rows, lambda m: m[0])),
        out_shape=jax.ShapeDtypeStruct(shape, BF16), compiler_params=_cp("arbitrary"), name=name)(me_arr, shards)


ALL_RELS = tuple(range(1, NDEV))
NEAR_RELS = (1, 2, 4, 6)
FAR_RELS = (2, 4, 6)


def _xchg_start(srcs, lands, groups, src_win, dst_win, after, name, rels=ALL_RELS, tids=None):
    ns = 0 if srcs is None else len(srcs)
    nt, ng = len(lands), len(groups)
    ins = ([] if srcs is None else list(srcs)) + list(lands)

    def body(*refs):
        src_refs, land_refs = refs[:ns], refs[ns:ns + nt]
        sems = refs[ns + nt + 1:ns + nt + 1 + 2 * ng]
        token = refs[-1]
        x, y, c = _coords()
        me = 4 * x + 2 * y + c
        for gi, grp in enumerate(groups):
            for j, t in enumerate(grp):
                tid = t if tids is None else tids[t]
                for ri, rel in enumerate(rels):
                    px, py, pc = _peer(rel)
                    q = 4 * px + 2 * py + pc
                    src = dst_win(tid, land_refs[t], me) if srcs is None else src_win(tid, src_refs[t], q)
                    pltpu.make_async_remote_copy(
                        src_ref=src, dst_ref=dst_win(tid, land_refs[t], me),
                        send_sem=sems[2 * gi].at[ri * len(grp) + j],
                        recv_sem=sems[2 * gi + 1].at[ri * len(grp) + j],
                        device_id=(px, py, pc), device_id_type=MESH).start()
        token[...] = jnp.zeros((8, LANES), F32)

    out_shape = []
    for grp in groups:
        out_shape += [pltpu.SemaphoreType.DMA((len(rels) * len(grp),))] * 2
    out_shape += [pltpu.HBM(a.shape, a.dtype) for a in ins]
    out_shape.append(jax.ShapeDtypeStruct((8, LANES), F32))
    outs = pl.pallas_call(
        body, in_specs=[HBM_SPEC] * len(ins) + [ANY],
        out_specs=[SEM_SPEC] * (2 * ng) + [HBM_SPEC] * len(ins) + [pl.BlockSpec(memory_space=pltpu.VMEM)],
        out_shape=out_shape, input_output_aliases={i: 2 * ng + i for i in range(len(ins))},
        compiler_params=pltpu.CompilerParams(has_side_effects=pltpu.SideEffectType.DATAFLOW_SIDE_EFFECTING),
        name=name)(*[_hbm(a) for a in ins], after)
    sems = [(outs[2 * gi], outs[2 * gi + 1]) for gi in range(ng)]
    thru = list(outs[2 * ng:2 * ng + len(ins)])
    return sems, (None if srcs is None else thru[:ns]), thru[ns:], outs[-1]


def _xchg_wait(sems, srcs, lands, tids, after, src_win, dst_win, name, rels=ALL_RELS):
    ns = 0 if srcs is None else len(srcs)
    n = len(lands)
    send_sem, recv_sem = sems
    ins = ([] if srcs is None else list(srcs)) + list(lands)

    def body(*refs):
        src_refs, land_refs = refs[:ns], refs[ns:ns + n]
        ssem, rsem = refs[ns + n], refs[ns + n + 1]
        x, y, c = _coords()
        me = 4 * x + 2 * y + c
        for j, t in enumerate(tids):
            for ri, rel in enumerate(rels):
                px, py, pc = _peer(rel)
                q = 4 * px + 2 * py + pc
                src = dst_win(t, land_refs[j], me) if srcs is None else src_win(t, src_refs[j], q)
                cp = pltpu.make_async_remote_copy(
                    src_ref=src, dst_ref=dst_win(t, land_refs[j], q),
                    send_sem=ssem.at[ri * n + j], recv_sem=rsem.at[ri * n + j],
                    device_id=(px, py, pc), device_id_type=MESH)
                cp.wait_send()
                cp.wait_recv()

    outs = pl.pallas_call(
        body, in_specs=[HBM_SPEC] * len(ins) + [SEM_SPEC, SEM_SPEC, ANY], out_specs=[HBM_SPEC] * len(ins),
        out_shape=[pltpu.HBM(a.shape, a.dtype) for a in ins],
        input_output_aliases={i: i for i in range(len(ins))},
        compiler_params=pltpu.CompilerParams(has_side_effects=pltpu.SideEffectType.DATAFLOW_SIDE_EFFECTING),
        name=name)(*ins, send_sem, recv_sem, after)
    return (None if srcs is None else list(outs[:ns])), list(outs[ns:])


def _gather_forward(sems_in, lands, groups, tids, after, dst_win, name):
    nt, ng = len(lands), len(groups)

    def body(*refs):
        land_refs = refs[:nt]
        in_sems = refs[nt:nt + 2 * ng]
        out_sems = refs[nt + 2 * ng + 1:nt + 4 * ng + 1]
        token = refs[-1]
        x, y, c = _coords()
        me = 4 * x + 2 * y + c
        sib = (x, y, 1 - c)
        for gi, grp in enumerate(groups):
            n = len(grp)
            for j, pos in enumerate(grp):
                t = tids[pos]
                for ri, rel in enumerate(NEAR_RELS):
                    px, py, pc = _peer(rel)
                    q = 4 * px + 2 * py + pc
                    cp = pltpu.make_async_remote_copy(
                        src_ref=dst_win(t, land_refs[pos], me), dst_ref=dst_win(t, land_refs[pos], q),
                        send_sem=in_sems[2 * gi].at[ri * n + j], recv_sem=in_sems[2 * gi + 1].at[ri * n + j],
                        device_id=(px, py, pc), device_id_type=MESH)
                    cp.wait_send()
                    cp.wait_recv()
            for j, pos in enumerate(grp):
                t = tids[pos]
                for fi, rel in enumerate(FAR_RELS):
                    px, py, pc = _peer(rel)
                    q = 4 * px + 2 * py + pc
                    win = dst_win(t, land_refs[pos], q)
                    pltpu.make_async_remote_copy(
                        src_ref=win, dst_ref=win,
                        send_sem=out_sems[2 * gi].at[fi * n + j], recv_sem=out_sems[2 * gi + 1].at[fi * n + j],
                        device_id=sib, device_id_type=MESH).start()
        token[...] = jnp.zeros((8, LANES), F32)

    out_shape = []
    for grp in groups:
        out_shape += [pltpu.SemaphoreType.DMA((len(FAR_RELS) * len(grp),))] * 2
    out_shape += [pltpu.HBM(a.shape, a.dtype) for a in lands]
    out_shape.append(jax.ShapeDtypeStruct((8, LANES), F32))
    flat_sems = [s for pair in sems_in for s in pair]
    outs = pl.pallas_call(
        body, in_specs=[HBM_SPEC] * nt + [SEM_SPEC] * (2 * ng) + [ANY],
        out_specs=[SEM_SPEC] * (2 * ng) + [HBM_SPEC] * nt + [pl.BlockSpec(memory_space=pltpu.VMEM)],
        out_shape=out_shape, input_output_aliases={i: 2 * ng + i for i in range(nt)},
        compiler_params=pltpu.CompilerParams(has_side_effects=pltpu.SideEffectType.DATAFLOW_SIDE_EFFECTING),
        name=name)(*[_hbm(a) for a in lands], *flat_sems, after)
    sems = [(outs[2 * gi], outs[2 * gi + 1]) for gi in range(ng)]
    return sems, list(outs[2 * ng:2 * ng + nt]), outs[-1]


class _Weights:
    def __init__(self, ready, pending=None):
        self.ready = dict(ready)
        self.pending = dict(pending or {})

    def __getitem__(self, k):
        return self.ready[k]

    def need(self, group, after):
        fn = self.pending.pop(group, None)
        if fn is not None:
            self.ready.update(fn(after))


def _adamw_math(w, g, m, v):
    m2 = ADAM_B1 * m + (1.0 - ADAM_B1) * g
    v2 = ADAM_B2 * v + (1.0 - ADAM_B2) * (g * g)
    m_hat = m2 / (1.0 - ADAM_B1 ** ADAM_STEP)
    v_hat = v2 / (1.0 - ADAM_B2 ** ADAM_STEP)
    delta = -ADAM_LR * (m_hat / (jnp.sqrt(v_hat) + ADAM_EPS) + ADAM_WD * w)
    return delta, m2, v2


def _adamw(t, parts, own, me_arr, w, m, v, layer, prev, rows, name):
    nl, nr, nc = w.shape

    def body(me_ref, p_ref, own_ref, w_ref, m_ref, v_ref, *rest):
        g_ref, d_ref, m2_ref, v2_ref = rest[-4:]
        me = me_ref[0]
        g = None
        for k in range(NDEV):
            term = jnp.where(me == k, own_ref[...], p_ref[k]).astype(F32)
            g = term if g is None else g + term
        delta, m2, v2 = _adamw_math(w_ref[...], g, m_ref[...], v_ref[...])
        g_ref[...] = g
        d_ref[...] = delta
        m2_ref[...] = m2
        v2_ref[...] = v2

    blk = pl.BlockSpec((None, rows, nc), lambda i, mm: (layer, i, 0))
    pblk = pl.BlockSpec((NDEV, rows, nc), lambda i, mm: (0, i, 0))
    extra = [] if prev is None else list(prev)
    return pl.pallas_call(
        body, grid_spec=pltpu.PrefetchScalarGridSpec(
            num_scalar_prefetch=1, grid=(nr // rows,),
            in_specs=[pblk, _own_block_spec(t, rows, lambda mm: mm[0]), blk, blk, blk] + [ANY] * len(extra),
            out_specs=[blk] * 4),
        out_shape=[jax.ShapeDtypeStruct(w.shape, F32)] * 4,
        input_output_aliases={6 + k: k for k in range(len(extra))},
        compiler_params=_cp("arbitrary"), name=name)(me_arr, parts, own, w, m, v, *extra)


def _pack(vecs):
    flat = jnp.concatenate([v.reshape(-1).astype(F32) for v in vecs])
    n = flat.shape[0]
    rows = -(-n // (8 * LANES)) * 8
    return jnp.pad(flat, (0, rows * LANES - n)).reshape(rows, LANES)


ROWPACK = (("rel_bias", 32, 32, (NUM_BUCKETS, N_BIAS_HEADS)), ("sinks", 8, 8, (DEPTH, 8)),
           ("attn_pre_norm", 16, 16, (DEPTH, D)), ("attn_post_norm", 16, 16, (DEPTH, D)),
           ("ffn_pre_norm", 16, 16, (DEPTH, D)), ("ffn_post_norm", 16, 16, (DEPTH, D)),
           ("conv_b", 128, 128, (DEPTH, 2 * D_FF)), ("b_gate", 48, 8, (DEPTH, 3, 128)),
           ("conv_w", 384, 48, (DEPTH, 3, 1024)))
ROWS_OWN = sum(r for _, _, r, _ in ROWPACK)
N_REPL = 7
ROWS_REPL = sum(r for _, _, r, _ in ROWPACK[:N_REPL])
ROWS_SHARD = ROWS_OWN - ROWS_REPL


def _as_rows(a, rows):
    a = a.astype(F32)
    if a.shape[-1] < LANES:
        a = jnp.pad(a.reshape(-1, a.shape[-1]), ((0, 0), (0, LANES - a.shape[-1])))
    a = a.reshape(-1, LANES)
    return jnp.pad(a, ((0, rows - a.shape[0]), (0, 0)))


def _rowpack(arrs, entries=ROWPACK):
    return jnp.concatenate([_as_rows(arrs[nm], ro) for nm, _, ro, _ in entries], axis=0)


def _shard_rows(g):
    bg = jnp.transpose(g["b_gate"].astype(F32).reshape(DEPTH * 3, NDEV, LANES), (1, 0, 2))
    bg = jnp.pad(bg, ((0, 0), (0, 8 - DEPTH * 3), (0, 0)))
    cw = jnp.transpose(g["conv_w"].astype(F32).reshape(DEPTH * 3, NDEV, 8, LANES), (1, 0, 2, 3))
    return jnp.concatenate([bg, cw.reshape(NDEV, DEPTH * 3 * 8, LANES)], axis=1)


def _small_update(parts_repl, parts_shard, w, m, v, name):
    nsm = len(ROWPACK)

    def body(pr_ref, ps_ref, w_ref, m_ref, v_ref, *rest):
        outs = rest[:4 * nsm]
        loss_ref = rest[4 * nsm]
        g_s, d_s, m_s, v_s = rest[4 * nsm + 1:]
        gr, gs = pr_ref[0], ps_ref[0]
        for k in range(1, NDEV):
            gr = gr + pr_ref[k]
            gs = gs + ps_ref[k]
        g_s[0:ROWS_REPL, :] = gr[:ROWS_REPL]
        g_s[ROWS_REPL:ROWS_OWN, :] = gs
        loss_ref[...] = gr[ROWS_REPL:]
        delta, m2, v2 = _adamw_math(w_ref[...], g_s[...], m_ref[...], v_ref[...])
        d_s[...] = delta
        m_s[...] = m2
        v_s[...] = v2
        for kind, src in enumerate((g_s, d_s, m_s, v_s)):
            oo = 0
            for idx, (nm, rf, ro, shp) in enumerate(ROWPACK):
                o_ref = outs[kind * nsm + idx]
                if nm in ("rel_bias", "sinks"):
                    o_ref[...] = src[oo:oo + shp[0], 0:shp[1]]
                elif nm == "b_gate":
                    for l in range(DEPTH):
                        o_ref[l] = src[oo + 3 * l:oo + 3 * l + 3, :]
                elif nm == "conv_w":
                    for l in range(DEPTH):
                        for k in range(8):
                            o_ref[l, :, k * LANES:(k + 1) * LANES] = src[pl.ds(oo + 24 * l + k, 3, stride=8), :]
                else:
                    per = shp[1] // LANES
                    for k in range(per):
                        o_ref[:, k * LANES:(k + 1) * LANES] = src[pl.ds(oo + k, DEPTH, stride=per), :]
                oo += ro

    vm = pl.BlockSpec(memory_space=pltpu.VMEM)
    shapes = [jax.ShapeDtypeStruct(shp, F32) for _ in range(4) for _, _, _, shp in ROWPACK]
    shapes.append(jax.ShapeDtypeStruct((8, LANES), F32))
    outs = pl.pallas_call(
        body, in_specs=[vm] * 5, out_specs=[vm] * (4 * nsm + 1), out_shape=shapes,
        scratch_shapes=[pltpu.VMEM((ROWS_OWN, LANES), F32)] * 4,
        name=name)(parts_repl, parts_shard, w, m, v)
    names = [nm for nm, _, _, _ in ROWPACK]
    return [dict(zip(names, outs[kind * nsm:(kind + 1) * nsm])) for kind in range(4)] + [outs[-1]]


def kernel(x, rel_bias, attn_pre_norm, w_in, b_gate, sinks, w_br_a, w_br_b, w_br_c, w_out, attn_post_norm, ffn_pre_norm, w_up, conv_w, conv_b, w_down, ffn_post_norm, loss_target, m_rel_bias, m_attn_pre_norm, m_w_in, m_b_gate, m_sinks, m_w_br_a, m_w_br_b, m_w_br_c, m_w_out, m_attn_post_norm, m_ffn_pre_norm, m_w_up, m_conv_w, m_conv_b, m_w_down, m_ffn_post_norm, v_rel_bias, v_attn_pre_norm, v_w_in, v_b_gate, v_sinks, v_w_br_a, v_w_br_b, v_w_br_c, v_w_out, v_attn_post_norm, v_ffn_pre_norm, v_w_up, v_conv_w, v_conv_b, v_w_down, v_ffn_post_norm):
    P = dict(rel_bias=rel_bias, attn_pre_norm=attn_pre_norm, w_in=w_in, b_gate=b_gate, sinks=sinks, w_br_a=w_br_a,
             w_br_b=w_br_b, w_br_c=w_br_c, w_out=w_out, attn_post_norm=attn_post_norm, ffn_pre_norm=ffn_pre_norm,
             w_up=w_up, conv_w=conv_w, conv_b=conv_b, w_down=w_down, ffn_post_norm=ffn_post_norm)
    M = dict(rel_bias=m_rel_bias, attn_pre_norm=m_attn_pre_norm, w_in=m_w_in, b_gate=m_b_gate, sinks=m_sinks,
             w_br_a=m_w_br_a, w_br_b=m_w_br_b, w_br_c=m_w_br_c, w_out=m_w_out, attn_post_norm=m_attn_post_norm,
             ffn_pre_norm=m_ffn_pre_norm, w_up=m_w_up, conv_w=m_conv_w, conv_b=m_conv_b, w_down=m_w_down,
             ffn_post_norm=m_ffn_post_norm)
    V = dict(rel_bias=v_rel_bias, attn_pre_norm=v_attn_pre_norm, w_in=v_w_in, b_gate=v_b_gate, sinks=v_sinks,
             w_br_a=v_w_br_a, w_br_b=v_w_br_b, w_br_c=v_w_br_c, w_out=v_w_out, attn_post_norm=v_attn_post_norm,
             ffn_pre_norm=v_ffn_pre_norm, w_up=v_w_up, conv_w=v_conv_w, conv_b=v_conv_b, w_down=v_w_down,
             ffn_post_norm=v_ffn_post_norm)
    tr = lambda a: jnp.swapaxes(a, 1, 2)
    PB = {nm: (tr(P[nm]) if nm == "w_in" else P[nm]) for nm, _, _ in BIG}
    MB = {nm: (tr(M[nm]) if nm == "w_in" else M[nm]) for nm, _, _ in BIG}
    VB = {nm: (tr(V[nm]) if nm == "w_in" else V[nm]) for nm, _, _ in BIG}
    xi, yi, ci = _coords()
    me = 4 * xi + 2 * yi + ci

    me_arr = me.astype(jnp.int32).reshape(1)

    small_w = _pack([b_gate.reshape(-1), conv_w.reshape(-1)])
    (small_w_all,) = _exchange([small_w], [jax.ShapeDtypeStruct((NDEV,) + small_w.shape, F32)],
                               _whole, _slot, "gather_small_weights")
    nbg, ncw = DEPTH * 3 * 128, DEPTH * 3 * 1024
    flat_all = small_w_all.reshape(NDEV, -1)
    b_gate_full = jnp.transpose(flat_all[:, :nbg].reshape(NDEV, DEPTH, 3, 128), (1, 2, 0, 3)).reshape(DEPTH, 3, D)
    conv_w_full = jnp.transpose(flat_all[:, nbg:nbg + ncw].reshape(NDEV, DEPTH, 3, 1024), (1, 2, 0, 3)).reshape(DEPTH, 3, 2 * D_FF)

    groups = [tuple(l * NBIG + t for t in tids) for l in range(DEPTH) for _, tids in LAYER_GROUPS]
    cast = lambda i, m=me_arr: _cast_own(i, PB[BIG[i % NBIG][0]], m, f"gather_own_l{i // NBIG}_{BIG[i % NBIG][0]}")
    first = list(groups[0])
    rest = [i for grp in groups[1:] for i in grp]
    sems0, _, lands0, tok_first = _xchg_start(None, [cast(i) for i in first], [tuple(range(len(first)))], None,
                                              _shard_window, small_w_all, "gather_start_first", rels=NEAR_RELS, tids=first)
    where_rest = {tid: k for k, tid in enumerate(rest)}
    me_rest = me_arr + tok_first[0, 0:1].astype(jnp.int32)
    sems1, _, lands1, g_tok = _xchg_start(None, [cast(i, me_rest) for i in rest],
                                          [tuple(where_rest[i] for i in grp) for grp in groups[1:]], None,
                                          _shard_window, lands0[0], "gather_start_rest", rels=NEAR_RELS, tids=rest)
    g_sems = list(sems0) + list(sems1)
    tok0 = g_tok[0:1, 0:1]
    lands_now = [None] * (DEPTH * NBIG)
    for i, a in zip(first + rest, list(lands0) + list(lands1)):
        lands_now[i] = a
    fwd_sems = {}
    fwd_plan = {0: (0,), 1: (1,), 2: (2,), 3: (3, 4, 5)}

    def forward(gis, after):
        gis = tuple(g2 for g2 in gis if g2 not in fwd_sems)
        if not gis:
            return after
        flat = [i for g2 in gis for i in groups[g2]]
        where = {tid: k for k, tid in enumerate(flat)}
        fs, new_lands, ftok = _gather_forward(
            [g_sems[g2] for g2 in gis], [lands_now[i] for i in flat],
            [[where[i] for i in groups[g2]] for g2 in gis], flat, after, _shard_window, f"gather_forward_{gis[0]}")
        for g2, s in zip(gis, fs):
            fwd_sems[g2] = s
        for i, a in zip(flat, new_lands):
            lands_now[i] = a
        return ftok

    def prefetch(gis):
        def run(after):
            forward(gis, after)
            return {}
        return run

    def gather_waiter(gi, l, gname, tids):
        def wait(after):
            after = forward(fwd_plan.get(gi, ()), after)
            ids = [l * NBIG + t for t in tids]
            _, got = _xchg_wait(fwd_sems[gi], None, [lands_now[i] for i in ids], ids, after,
                                None, _shard_window, f"gather_wait_l{l}_{gname}", rels=FAR_RELS)
            out = {}
            for t, arr in zip(tids, got):
                nm = BIG[t][0]
                out[nm] = arr
            return out
        return wait

    pending = [{gname: gather_waiter(l * len(LAYER_GROUPS) + k, l, gname, tids)
                for k, (gname, tids) in enumerate(LAYER_GROUPS)} for l in range(DEPTH)]
    pending[0]["pre_ffn"] = prefetch((2,))
    ws = []
    for l in range(DEPTH):
        ws.append(_Weights(dict(
            b_gate=b_gate_full[l], conv_w=conv_w_full[l].reshape(3, 2, D_FF), conv_b=conv_b[l].reshape(2, D_FF),
            sinks=sinks[l].reshape(1, 8),
            attn_pre_norm=attn_pre_norm[l].reshape(1, D), attn_post_norm=attn_post_norm[l].reshape(1, D),
            ffn_pre_norm=ffn_pre_norm[l].reshape(1, D), ffn_post_norm=ffn_post_norm[l].reshape(1, D)), pending[l]))

    rs = {}

    group_tids = dict(LAYER_GROUPS)

    def start_scatter(l, gname, grads_l):
        tids = group_tids[gname]
        blocks, lands_rs = [], []
        for t in tids:
            nm, ax, ext = BIG[t]
            gfull = grads_l[nm].astype(BF16)
            shp = (NDEV, ext, gfull.shape[1]) if ax == 0 else (NDEV, gfull.shape[0], ext)
            blocks.append(gfull)
            lands_rs.append(lax.empty(shp, BF16))
        local = list(range(len(tids)))
        win = lambda j, ref, k: _shard_window(tids[j], ref, k)
        sems, s_thru, l_thru, tok = _xchg_start(blocks, lands_rs, [tuple(local)], win, _slot, me_arr,
                                                f"scatter_start_l{l}_{gname}")
        rs[(l, gname)] = (sems[0], s_thru, l_thru, win, local)
        return tok[0:1, 0:1]

    loss_tile, grad_x, grads, g_rel = _local_step(x[0], loss_target[0], ws, rel_bias, tok0, start_scatter)

    stack = lambda nm: jnp.stack([grads[l][nm] for l in range(DEPTH)], axis=0)
    small_g = {nm: (g_rel if nm == "rel_bias" else stack(nm)) for nm, _, _, _ in ROWPACK}
    small_repl = jnp.concatenate([_rowpack(small_g, ROWPACK[:N_REPL]), loss_tile], axis=0)
    small_shard = _shard_rows(small_g)

    out_g, out_d, out_m, out_v = {}, {}, {}, {}
    prev = {nm: None for nm, _, _ in BIG}
    todo = [(l, gname) for l in reversed(range(DEPTH)) for gname in ("ffn", "mix", "in")]
    after, small_parts = grad_x, None
    for l, gname in todo:
        if (l, gname) == todo[-1]:
            small_parts = _exchange(
                [small_repl, small_shard],
                [jax.ShapeDtypeStruct((NDEV, ROWS_REPL + 8, LANES), F32), jax.ShapeDtypeStruct((NDEV, ROWS_SHARD, LANES), F32)],
                lambda t, ref, q: ref if t == 0 else ref.at[q], _slot, "exchange_small_grads", after=after)
            after = small_parts[0]
        sems, s_thru, l_thru, win, local = rs[(l, gname)]
        owns, parts = _xchg_wait(sems, s_thru, l_thru, local, after, win, _slot, f"scatter_wait_l{l}_{gname}")
        for t, own, prt in zip(group_tids[gname], owns, parts):
            nm = BIG[t][0]
            rows = SHARD_ROWS.get(nm, PB[nm].shape[1])
            prev[nm] = _adamw(t, prt, own, me_arr, PB[nm], MB[nm], VB[nm], l, prev[nm], rows, f"adamw_{nm}_l{l}")
            after = prev[nm][1]
    for nm, _, _ in BIG:
        out_g[nm], out_d[nm], out_m[nm], out_v[nm] = [tr(a) if nm == "w_in" else a for a in prev[nm]]
    sm_g, sm_d, sm_m, sm_v, loss_all = _small_update(small_parts[0], small_parts[1], _rowpack(P), _rowpack(M),
                                                     _rowpack(V), "small_update")
    loss = loss_all[0, 0]
    for dst, src in ((out_g, sm_g), (out_d, sm_d), (out_m, sm_m), (out_v, sm_v)):
        dst.update(src)

    order = ["rel_bias", "attn_pre_norm", "w_in", "b_gate", "sinks", "w_br_a", "w_br_b", "w_br_c", "w_out",
             "attn_post_norm", "ffn_pre_norm", "w_up", "conv_w", "conv_b", "w_down", "ffn_post_norm"]
    return (loss, grad_x[None], *[out_g[k] for k in order], *[out_d[k] for k in order],
            *[out_m[k] for k in order], *[out_v[k] for k in order])
```

```python
import functools
import math

import numpy as np
import jax
import jax.numpy as jnp
from jax import lax
from jax.experimental import pallas as pl
from jax.experimental.pallas import tpu as pltpu

F32 = jnp.float32
BF16 = jnp.bfloat16

S = 2048
D = 1024
DEPTH = 2
NDEV = 8
HD = 64
BLK = 128
NB = S // BLK
A_GROUPS = ((128, 1), (512, 4), (2048, 16))
NUM_BUCKETS = 32
MAX_DISTANCE = 2048
N_BIAS_HEADS = 20
D_FF = 4096
IN_COLS = 6912
QKV_COLS = 3840
QKV_SLABS = QKV_COLS // 128
GATE_COLS = 3072
EPS = 1e-6
SCALE = HD ** -0.5
NEG = -1e30
LANES = 128

ADAM_LR = 0.001
ADAM_B1 = 0.9
ADAM_B2 = 0.999
ADAM_EPS = 1e-08
ADAM_WD = 0.01
ADAM_STEP = 10

VMEM_LIMIT = 56 * 1024 * 1024
MESH = pl.DeviceIdType.MESH
ANY = pl.BlockSpec(memory_space=pl.ANY)
SMEM = pl.BlockSpec(memory_space=pltpu.SMEM)


def _cp(*sem):
    return pltpu.CompilerParams(dimension_semantics=sem if sem else None, vmem_limit_bytes=VMEM_LIMIT)


def _dot(a, b, ca, cb):
    return lax.dot_general(a, b, (((ca,), (cb,)), ((), ())), preferred_element_type=F32)


def _mm(a, b, *, grid, a_spec, b_spec, out_shape, out_spec, ca, cb, acc_shape, name,
        a_slab=False, b_slab=False, out_slab=False, alias_out=None, after=None):
    nk = grid[2]

    def body(*refs):
        a_ref, b_ref = refs[0], refs[1]
        o_ref, acc_ref = refs[-2], refs[-1]
        k = pl.program_id(2)

        def load(ref, slab):
            if slab:
                return jnp.concatenate([ref[s] for s in range(ref.shape[0])], axis=1).astype(BF16)
            return ref[...].astype(BF16)

        def write(val):
            if out_slab:
                for s in range(o_ref.shape[0]):
                    o_ref[s] = val[:, s * LANES:(s + 1) * LANES].astype(o_ref.dtype)
            else:
                o_ref[...] = val.astype(o_ref.dtype)

        d = _dot(load(a_ref, a_slab), load(b_ref, b_slab), ca, cb)
        if nk == 1:
            write(d)
        elif direct:
            @pl.when(k == 0)
            def _():
                o_ref[...] = d

            @pl.when(k > 0)
            def _():
                o_ref[...] += d
        else:
            @pl.when(k == 0)
            def _():
                acc_ref[...] = d

            if nk > 2:
                @pl.when((k > 0) & (k < nk - 1))
                def _():
                    acc_ref[...] += d

            @pl.when(k == nk - 1)
            def _():
                write(acc_ref[...] + d)

    direct = (not out_slab) and out_shape.dtype == F32
    if nk == 1 or direct:
        acc_shape = (8, LANES)
    in_specs = [a_spec, b_spec]
    args = [a, b]
    aliases = {}
    if alias_out is not None:
        in_specs.append(ANY)
        args.append(alias_out)
        aliases = {2: 0}
    if after is not None:
        in_specs.append(ANY)
        args.append(after)
    return pl.pallas_call(
        body, grid=grid, in_specs=in_specs, out_specs=out_spec, out_shape=out_shape,
        scratch_shapes=[pltpu.VMEM(acc_shape, F32)], input_output_aliases=aliases,
        compiler_params=_cp("parallel", "parallel", "arbitrary"), name=name)(*args)


def _mm_nn(a, b, out_dtype, tm, tn, tk, name):
    m, kk = a.shape
    n = b.shape[1]
    return _mm(a, b, grid=(m // tm, n // tn, kk // tk),
               a_spec=pl.BlockSpec((tm, tk), lambda i, j, k: (i, k)),
               b_spec=pl.BlockSpec((tk, tn), lambda i, j, k: (k, j)),
               out_shape=jax.ShapeDtypeStruct((m, n), out_dtype),
               out_spec=pl.BlockSpec((tm, tn), lambda i, j, k: (i, j)),
               ca=1, cb=0, acc_shape=(tm, tn), name=name)


def _mm_nt(a, b, out_dtype, tm, tn, tk, name):
    m, kk = a.shape
    n = b.shape[0]
    return _mm(a, b, grid=(m // tm, n // tn, kk // tk),
               a_spec=pl.BlockSpec((tm, tk), lambda i, j, k: (i, k)),
               b_spec=pl.BlockSpec((tn, tk), lambda i, j, k: (j, k)),
               out_shape=jax.ShapeDtypeStruct((m, n), out_dtype),
               out_spec=pl.BlockSpec((tm, tn), lambda i, j, k: (i, j)),
               ca=1, cb=1, acc_shape=(tm, tn), name=name)


def _mm_tn(a, b, out_dtype, tm, tn, tk, name):
    kk, m = a.shape
    n = b.shape[1]
    return _mm(a, b, grid=(m // tm, n // tn, kk // tk),
               a_spec=pl.BlockSpec((tk, tm), lambda i, j, k: (k, i)),
               b_spec=pl.BlockSpec((tk, tn), lambda i, j, k: (k, j)),
               out_shape=jax.ShapeDtypeStruct((m, n), out_dtype),
               out_spec=pl.BlockSpec((tm, tn), lambda i, j, k: (i, j)),
               ca=0, cb=0, acc_shape=(tm, tn), name=name)


ROW_TILE = 512
MERGE_TILE = 256


def _rms(x, g):
    r = lax.rsqrt(jnp.mean(x * x, axis=-1, keepdims=True) + EPS)
    return x * r * g


def _prenorm(x, g, name):
    def body(x_ref, g_ref, o_ref):
        o_ref[...] = _rms(x_ref[...], g_ref[...]).astype(BF16)

    return pl.pallas_call(
        body, grid=(S // ROW_TILE,),
        in_specs=[pl.BlockSpec((ROW_TILE, D), lambda i: (i, 0)), pl.BlockSpec((1, D), lambda i: (0, 0))],
        out_specs=pl.BlockSpec((ROW_TILE, D), lambda i: (i, 0)),
        out_shape=jax.ShapeDtypeStruct((S, D), BF16), compiler_params=_cp("parallel"), name=name)(x, g)


def _postnorm_res(x, f, g_post, g_next, name):
    def body(x_ref, f_ref, gp_ref, gn_ref, xo_ref, ho_ref):
        xn = x_ref[...] + _rms(f_ref[...], gp_ref[...])
        xo_ref[...] = xn
        ho_ref[...] = _rms(xn, gn_ref[...]).astype(BF16)

    row = pl.BlockSpec((ROW_TILE, D), lambda i: (i, 0))
    vec = pl.BlockSpec((1, D), lambda i: (0, 0))
    return pl.pallas_call(
        body, grid=(S // ROW_TILE,), in_specs=[row, row, vec, vec], out_specs=[row, row],
        out_shape=[jax.ShapeDtypeStruct((S, D), F32), jax.ShapeDtypeStruct((S, D), BF16)],
        compiler_params=_cp("parallel"), name=name)(x, f, g_post, g_next)


def _norm_bwd(f, g, dys, res, out_dtype, name):
    ndy = len(dys)
    has_res = res is not None

    def body(*refs):
        f_ref, g_ref = refs[0], refs[1]
        dy_refs = refs[2:2 + ndy]
        res_ref = refs[2 + ndy] if has_res else None
        o_ref, dg_ref = refs[-2], refs[-1]
        fv = f_ref[...]
        dy = dy_refs[0][...].astype(F32)
        for r in dy_refs[1:]:
            dy = dy + r[...].astype(F32)
        r = lax.rsqrt(jnp.mean(fv * fv, axis=-1, keepdims=True) + EPS)
        n = fv * r
        dn = dy * g_ref[...]
        df = r * (dn - n * jnp.mean(dn * n, axis=-1, keepdims=True))
        if has_res:
            df = df + res_ref[...]
        o_ref[...] = df.astype(out_dtype)

        @pl.when(pl.program_id(0) == 0)
        def _():
            dg_ref[...] = jnp.zeros((1, D), F32)

        dg_ref[...] += jnp.sum(dy * n, axis=0, keepdims=True)

    row = pl.BlockSpec((ROW_TILE, D), lambda i: (i, 0))
    vec = pl.BlockSpec((1, D), lambda i: (0, 0))
    in_specs = [row, vec] + [row] * ndy + ([row] if has_res else [])
    args = [f, g] + list(dys) + ([res] if has_res else [])
    return pl.pallas_call(
        body, grid=(S // ROW_TILE,), in_specs=in_specs, out_specs=[row, vec],
        out_shape=[jax.ShapeDtypeStruct((S, D), out_dtype), jax.ShapeDtypeStruct((1, D), F32)],
        compiler_params=_cp("arbitrary"), name=name)(*args)


def _rms_bwd_rows(fv, g, dy):
    r = lax.rsqrt(jnp.mean(fv * fv, axis=-1, keepdims=True) + EPS)
    n = fv * r
    dn = dy * g
    return r * (dn - n * jnp.mean(dn * n, axis=-1, keepdims=True)), dy * n


def _norm_bwd_chain(f1, g1, dys, res, f2, g2, name):
    ndy = len(dys)

    def body(*refs):
        f1_ref, g1_ref = refs[0], refs[1]
        dy_refs = refs[2:2 + ndy]
        res_ref, f2_ref, g2_ref = refs[2 + ndy:5 + ndy]
        o1_ref, o2_ref, dg1_ref, dg2_ref = refs[-4:]
        dy = dy_refs[0][...].astype(F32)
        for r in dy_refs[1:]:
            dy = dy + r[...].astype(F32)
        df1, c1 = _rms_bwd_rows(f1_ref[...], g1_ref[...], dy)
        out1 = df1 + res_ref[...]
        o1_ref[...] = out1
        df2, c2 = _rms_bwd_rows(f2_ref[...], g2_ref[...], out1)
        o2_ref[...] = df2.astype(BF16)

        @pl.when(pl.program_id(0) == 0)
        def _():
            dg1_ref[...] = jnp.zeros((1, D), F32)
            dg2_ref[...] = jnp.zeros((1, D), F32)

        dg1_ref[...] += jnp.sum(c1, axis=0, keepdims=True)
        dg2_ref[...] += jnp.sum(c2, axis=0, keepdims=True)

    row = pl.BlockSpec((ROW_TILE, D), lambda i: (i, 0))
    vec = pl.BlockSpec((1, D), lambda i: (0, 0))
    return pl.pallas_call(
        body, grid=(S // ROW_TILE,), in_specs=[row, vec] + [row] * ndy + [row, row, vec],
        out_specs=[row, row, vec, vec],
        out_shape=[jax.ShapeDtypeStruct((S, D), F32), jax.ShapeDtypeStruct((S, D), BF16),
                   jax.ShapeDtypeStruct((1, D), F32), jax.ShapeDtypeStruct((1, D), F32)],
        compiler_params=_cp("arbitrary"), name=name)(f1, g1, *dys, res, f2, g2)


def _loss_head(x, f, g, target, name):
    def body(x_ref, f_ref, g_ref, t_ref, dy_ref, l_ref, df_ref, dg_ref):
        fv = f_ref[...]
        e = x_ref[...] + _rms(fv, g_ref[...]) - t_ref[...]
        dy = e * (1.0 / D)
        dy_ref[...] = dy
        df, c = _rms_bwd_rows(fv, g_ref[...], dy)
        df_ref[...] = df.astype(BF16)

        @pl.when(pl.program_id(0) == 0)
        def _():
            l_ref[...] = jnp.zeros((8, LANES), F32)
            dg_ref[...] = jnp.zeros((1, D), F32)

        l_ref[...] += jnp.sum(e * e) * (0.5 / D)
        dg_ref[...] += jnp.sum(c, axis=0, keepdims=True)

    row = pl.BlockSpec((ROW_TILE, D), lambda i: (i, 0))
    vec = pl.BlockSpec((1, D), lambda i: (0, 0))
    return pl.pallas_call(
        body, grid=(S // ROW_TILE,), in_specs=[row, row, vec, row],
        out_specs=[row, pl.BlockSpec((8, LANES), lambda i: (0, 0)), row, vec],
        out_shape=[jax.ShapeDtypeStruct((S, D), F32), jax.ShapeDtypeStruct((8, LANES), F32),
                   jax.ShapeDtypeStruct((S, D), BF16), jax.ShapeDtypeStruct((1, D), F32)],
        compiler_params=_cp("arbitrary"), name=name)(x, f, g, target)


def _bucket_tiles():
    a = np.arange(BLK)[:, None]
    b = np.arange(2 * BLK)[None, :]
    dist = a + BLK - b
    out = np.zeros((4, 2, BLK, 2 * BLK), np.int32)
    cfg = [(w // d, d) for w, d in A_GROUPS] + [(BLK - 1, 1)]
    for gi, (max_dist, d) in enumerate(cfg):
        band = (dist >= 0) & (dist <= max_dist)
        tok = np.maximum(dist, 0) * d
        nf = np.maximum(tok, 1).astype(np.float32)
        max_exact = NUM_BUCKETS // 2
        large = max_exact + (np.log(nf / np.float32(max_exact)) / np.float32(math.log(MAX_DISTANCE / max_exact))
                             * np.float32(NUM_BUCKETS - max_exact)).astype(np.int32)
        large = np.minimum(large, NUM_BUCKETS - 1)
        bkt = np.where(tok < max_exact, tok, large).astype(np.int32)
        full = np.where(band, bkt, -1)
        out[gi, 1] = full
        out[gi, 0] = np.where(b >= BLK, full, -1)
    return out


def _bias_tiles(rel_bias, buckets, name):
    def body(tab_ref, bkt_ref, o_ref):
        h = pl.program_id(0)
        bkt = bkt_ref[...]
        acc = jnp.zeros(bkt.shape, F32)
        for bb in range(NUM_BUCKETS):
            acc = jnp.where(bkt == bb, tab_ref[bb, h], acc)
        o_ref[...] = jnp.where(bkt < 0, NEG, acc)

    return pl.pallas_call(
        body, grid=(N_BIAS_HEADS,),
        in_specs=[SMEM, pl.BlockSpec((None, 2, BLK, 2 * BLK), lambda h: (jnp.minimum(h // 4, 3), 0, 0, 0))],
        out_specs=pl.BlockSpec((None, 2, BLK, 2 * BLK), lambda h: (h, 0, 0, 0)),
        out_shape=jax.ShapeDtypeStruct((N_BIAS_HEADS, 2, BLK, 2 * BLK), F32),
        compiler_params=_cp("arbitrary"), name=name)(rel_bias, buckets)


def _bias_grad(gs, buckets, name):
    ng = len(gs)

    def body(*refs):
        g_refs = refs[:ng]
        bkt_ref, o_ref = refs[ng], refs[ng + 1]
        h = pl.program_id(0)
        g = g_refs[0][...]
        for r in g_refs[1:]:
            g = g + r[...]
        bkt = bkt_ref[...]
        row = lax.broadcasted_iota(jnp.int32, (NUM_BUCKETS, LANES), 0)
        lane = lax.broadcasted_iota(jnp.int32, (NUM_BUCKETS, LANES), 1)

        @pl.when(h == 0)
        def _():
            o_ref[...] = jnp.zeros((NUM_BUCKETS, LANES), F32)

        acc = o_ref[...]
        for bb in range(NUM_BUCKETS):
            s = jnp.sum(jnp.where(bkt == bb, g, 0.0))
            acc = jnp.where((row == bb) & (lane == h), s, acc)
        o_ref[...] = acc

    g_spec = pl.BlockSpec((None, BLK, 2 * BLK), lambda h: (h, 0, 0))
    return pl.pallas_call(
        body, grid=(N_BIAS_HEADS,),
        in_specs=[g_spec] * ng + [pl.BlockSpec((None, None, BLK, 2 * BLK), lambda h: (jnp.minimum(h // 4, 3), 1, 0, 0))],
        out_specs=pl.BlockSpec((NUM_BUCKETS, LANES), lambda h: (0, 0)),
        out_shape=jax.ShapeDtypeStruct((NUM_BUCKETS, LANES), F32),
        compiler_params=_cp("arbitrary"), name=name)(*gs, buckets)


def _to_class_major(src_ref, dst_refs, d, fn=None):
    ln = S // d
    for r in range(d):
        v = src_ref[pl.ds(r, ln, stride=d), :] if d > 1 else src_ref[...]
        outs = fn(v) if fn is not None else (v,) * len(dst_refs)
        for dst, o in zip(dst_refs, outs):
            dst[pl.ds(r * ln, ln), :] = o.astype(dst.dtype)


def _head_masks(rows):
    lane = lax.broadcasted_iota(jnp.int32, (rows, LANES), 1)
    return lane < HD, lane >= HD


def _split_heads(v):
    m0, m1 = _head_masks(v.shape[0])
    return jnp.where(m0, v, 0.0), jnp.where(m1, v, 0.0)


def _dup_head(v, hi):
    m0, _ = _head_masks(v.shape[0])
    r = pltpu.roll(v, HD, 1)
    return jnp.where(m0, jnp.where(hi, r, v), jnp.where(hi, v, r))


def _block_rows(b, d):
    nbc = NB // d
    i = b % nbc
    r = b // nbc
    has_prev = (i > 0).astype(jnp.int32)
    prev = pl.multiple_of(jnp.maximum(b - 1, 0) * BLK, BLK)
    nat = i * (BLK * d) + r
    return has_prev, prev, nat


def _lane_halves(v0, v1):
    lane = lax.broadcasted_iota(jnp.int32, (v0.shape[0], LANES), 1)
    return jnp.where(lane < HD, v0, v1)


def _band_fwd(proj, bias, sinks, *, d, q0, k0, v0, npairs, bias0, shared_kv, name):
    def body(sink_ref, q_ref, k_ref, v_ref, b_ref, num_ref, st_ref, qz0, qz1, ks, vs):
        p = pl.program_id(0)
        kv = (lambda v: (_dup_head(v, p >= 2),)) if shared_kv else None
        _to_class_major(q_ref, (qz0, qz1), d, lambda v: _split_heads(v * SCALE))
        _to_class_major(k_ref, (ks,), d, kv)
        _to_class_major(v_ref, (vs,), d, kv)
        lane = lax.broadcasted_iota(jnp.int32, (BLK, LANES), 1)

        def blk(b, carry):
            has_prev, prev, nat = _block_rows(b, d)
            cur = pl.multiple_of(b * BLK, BLK)
            k2 = jnp.concatenate([ks[pl.ds(prev, BLK), :], ks[pl.ds(cur, BLK), :]], axis=0)
            v2 = jnp.concatenate([vs[pl.ds(prev, BLK), :], vs[pl.ds(cur, BLK), :]], axis=0)
            nums, ms, ls = [], [], []
            for hh, qz in enumerate((qz0, qz1)):
                z = _dot(qz[pl.ds(cur, BLK), :], k2, 1, 1) + b_ref[hh, has_prev]
                m = jnp.max(z, axis=1, keepdims=True)
                e = jnp.exp(z - m)
                l = jnp.sum(e, axis=1, keepdims=True)
                num = _dot(e.astype(BF16), v2, 1, 0)
                if shared_kv:
                    sink = sink_ref[0, 2 * p + hh]
                    mx = jnp.maximum(m, sink)
                    c = jnp.exp(m - mx)
                    zden = l * c + jnp.exp(sink - mx)
                    num = num * (c / zden)
                    m = mx + jnp.log(zden)
                ls.append(l)
                ms.append(m)
                nums.append(num)
            num_t = jnp.where(lane < HD, nums[0], nums[1])
            if shared_kv:
                st_t = jnp.where(lane < HD, ms[0], ms[1])
            else:
                st_t = jnp.where(lane < 32, ms[0], jnp.where(lane < 64, ls[0], jnp.where(lane < 96, ms[1], ls[1])))
            if d > 1:
                num_ref[pl.ds(nat, BLK, stride=d), :] = num_t
                st_ref[pl.ds(nat, BLK, stride=d), :] = st_t
            else:
                num_ref[pl.ds(cur, BLK), :] = num_t
                st_ref[pl.ds(cur, BLK), :] = st_t
            return carry

        lax.fori_loop(0, NB, blk, 0, unroll=True)

    slab = lambda off, per_pair: pl.BlockSpec((None, S, LANES), (lambda p: (off + p, 0, 0)) if per_pair else (lambda p: (off, 0, 0)))
    out = pl.BlockSpec((None, S, LANES), lambda p: (p, 0, 0))
    return pl.pallas_call(
        body, grid=(npairs,),
        in_specs=[SMEM, slab(q0, True), slab(k0, not shared_kv), slab(v0, not shared_kv),
                  pl.BlockSpec((None, 2, 2, BLK, 2 * BLK), lambda p: (bias0 + p, 0, 0, 0, 0))],
        out_specs=[out, out],
        out_shape=[jax.ShapeDtypeStruct((npairs, S, LANES), F32)] * 2,
        scratch_shapes=[pltpu.VMEM((S, LANES), BF16)] * 4,
        compiler_params=_cp("arbitrary"), name=name)(sinks, proj, proj, proj, bias)


def _combine_a(nums, stats, name):
    rt = 512

    def body(n0, n1, n2, s0, s1, s2, o_ref, l_ref):
        n_refs, s_refs = (n0, n1, n2), (s0, s1, s2)
        outs, lses = [], []
        for hh in range(2):
            ms = [s[:, 64 * hh:64 * hh + 1] for s in s_refs]
            ls = [s[:, 64 * hh + 32:64 * hh + 33] for s in s_refs]
            mx = jnp.maximum(jnp.maximum(ms[0], ms[1]), ms[2])
            cs = [jnp.exp(m - mx) for m in ms]
            z = cs[0] * ls[0] + cs[1] * ls[1] + cs[2] * ls[2]
            acc = cs[0] * n_refs[0][:, hh * HD:(hh + 1) * HD]
            acc = acc + cs[1] * n_refs[1][:, hh * HD:(hh + 1) * HD]
            acc = acc + cs[2] * n_refs[2][:, hh * HD:(hh + 1) * HD]
            outs.append(acc / z)
            lses.append(mx + jnp.log(z))
        o_ref[...] = jnp.concatenate(outs, axis=1)
        l_ref[...] = _lane_halves(lses[0], lses[1])

    spec = pl.BlockSpec((None, rt, LANES), lambda p, i: (p, i, 0))
    return pl.pallas_call(
        body, grid=(2, S // rt), in_specs=[spec] * 6, out_specs=[spec, spec],
        out_shape=[jax.ShapeDtypeStruct((2, S, LANES), F32)] * 2,
        compiler_params=_cp("parallel", "parallel"), name=name)(*nums, *stats)


def _band_bwd(proj, bias, o, do, lse, sinks, dqkv, *, d, q0, k0, v0, npairs, bias0, shared_kv, name):
    def body(sink_ref, q_ref, k_ref, v_ref, b_ref, o_ref, do_ref, lse_ref, dqkv_in, dqkv_ref, g_ref, ds_ref,
             qz0, qz1, ks, vs, doz0, doz1, ls0, ls1, dls0, dls1, stage, dq_nat, dk_cm, dv_cm, kv_nat, dk_acc, dv_acc,
             obuf, osem):
        p = pl.program_id(0)
        dq_ref, dk_ref, dv_ref = obuf.at[0], obuf.at[1], obuf.at[2]
        m0, m1 = _head_masks(S)
        kk = lax.broadcasted_iota(jnp.int32, (2 * LANES, LANES), 0) % LANES
        ll = lax.broadcasted_iota(jnp.int32, (2 * LANES, LANES), 1)
        hi, lo = _split2(do_ref[...] * o_ref[...])
        dl = _dot(jnp.concatenate([hi, lo], axis=1), ((kk < HD) == (ll < HD)).astype(BF16), 1, 0)
        if shared_kv:
            row8 = lax.broadcasted_iota(jnp.int32, (8, LANES), 0)
            lane8 = lax.broadcasted_iota(jnp.int32, (8, LANES), 1)
            sinkv = jnp.where(m0, sink_ref[0, 2 * p], sink_ref[0, 2 * p + 1])
            contrib = jnp.exp(sinkv - lse_ref[...]) * dl
            t = jnp.zeros((8, LANES), F32)
            for hh, mh in enumerate((m0, m1)):
                dsink = -jnp.sum(jnp.where(mh, contrib, 0.0)) * (1.0 / HD)
                t = jnp.where((row8 == 0) & (lane8 == hh), dsink, t)
            ds_ref[...] = t
        else:
            ds_ref[...] = jnp.zeros((8, LANES), F32)
        kv = (lambda v: (_dup_head(v, p >= 2),)) if shared_kv else None
        _to_class_major(q_ref, (qz0, qz1), d, lambda v: _split_heads(v * SCALE))
        _to_class_major(k_ref, (ks,), d, kv)
        _to_class_major(v_ref, (vs,), d, kv)
        _to_class_major(do_ref, (doz0, doz1), d, _split_heads)
        def spread(v):
            a0, a1 = _head_masks(v.shape[0])
            r = pltpu.roll(v, HD, 1)
            return jnp.where(a0, v, r), jnp.where(a1, v, r)

        _to_class_major(lse_ref, (ls0, ls1), d, spread)
        stage[...] = dl
        _to_class_major(stage, (dls0, dls1), d, spread)

        dk_cm[...] = jnp.zeros((S, LANES), F32)
        dv_cm[...] = jnp.zeros((S, LANES), F32)
        g_ref[...] = jnp.zeros((2, BLK, 2 * BLK), F32)
        lane = lax.broadcasted_iota(jnp.int32, (BLK, LANES), 1)

        def blk(b, carry):
            has_prev, prev, nat = _block_rows(b, d)
            cur = pl.multiple_of(b * BLK, BLK)
            k2 = jnp.concatenate([ks[pl.ds(prev, BLK), :], ks[pl.ds(cur, BLK), :]], axis=0)
            v2 = jnp.concatenate([vs[pl.ds(prev, BLK), :], vs[pl.ds(cur, BLK), :]], axis=0)
            dqs, dks, dvs = [], [], []
            for hh, (qz, doz, lsr, dlr) in enumerate(((qz0, doz0, ls0, dls0), (qz1, doz1, ls1, dls1))):
                qb = qz[pl.ds(cur, BLK), :]
                dob = doz[pl.ds(cur, BLK), :]
                lb = lsr[pl.ds(cur, BLK), :]
                dlb = dlr[pl.ds(cur, BLK), :]
                z = _dot(qb, k2, 1, 1) + b_ref[hh, has_prev]
                pr = jnp.exp(z - jnp.concatenate([lb, lb], axis=1))
                dp = _dot(dob, v2, 1, 1)
                dz = pr * (dp - jnp.concatenate([dlb, dlb], axis=1))
                g_ref[hh] += dz
                dzb = dz.astype(BF16)
                dqs.append(_dot(dzb, k2, 1, 0))
                dks.append(_dot(dzb, qb, 0, 0))
                dvs.append(_dot(pr.astype(BF16), dob, 0, 0))
            dq_t = jnp.where(lane < HD, dqs[0], dqs[1]) * SCALE
            dk_t = dks[0] + dks[1]
            dv_t = dvs[0] + dvs[1]
            dk_cm[pl.ds(prev, BLK), :] += dk_t[:BLK]
            dk_cm[pl.ds(cur, BLK), :] += dk_t[BLK:]
            dv_cm[pl.ds(prev, BLK), :] += dv_t[:BLK]
            dv_cm[pl.ds(cur, BLK), :] += dv_t[BLK:]
            if d > 1:
                dq_nat[pl.ds(nat, BLK, stride=d), :] = dq_t
            else:
                dq_nat[pl.ds(cur, BLK), :] = dq_t
            return carry

        lax.fori_loop(0, NB, blk, 0, unroll=True)
        dq_ref[...] = dq_nat[...].astype(BF16)

        def from_class_major(src, dst_ref):
            if d == 1:
                dst_ref[...] = src[...].astype(BF16)
            else:
                ln = S // d
                for r in range(d):
                    kv_nat[pl.ds(r, ln, stride=d), :] = src[pl.ds(r * ln, ln), :]
                dst_ref[...] = kv_nat[...].astype(BF16)

        def put(i, slab):
            return pltpu.make_async_copy(obuf.at[i], dqkv_ref.at[slab], osem.at[i])

        put(0, q0 + p).start()
        if not shared_kv:
            from_class_major(dk_cm, dk_ref)
            from_class_major(dv_cm, dv_ref)
            put(1, k0 + p).start()
            put(2, v0 + p).start()
            put(1, k0 + p).wait()
            put(2, v0 + p).wait()
        else:
            @pl.when(p == 0)
            def _():
                dk_acc[...] = jnp.zeros((S, LANES), F32)
                dv_acc[...] = jnp.zeros((S, LANES), F32)

            mine = m1 == (p >= 2)
            for cm, acc in ((dk_cm, dk_acc), (dv_cm, dv_acc)):
                val = cm[...]
                acc[...] += jnp.where(mine, val + pltpu.roll(val, HD, 1), 0.0)

            @pl.when(p == npairs - 1)
            def _():
                from_class_major(dk_acc, dk_ref)
                from_class_major(dv_acc, dv_ref)
                put(1, k0).start()
                put(2, v0).start()
                put(1, k0).wait()
                put(2, v0).wait()

        put(0, q0 + p).wait()

    slab = lambda off, per_pair: pl.BlockSpec((None, S, LANES), (lambda p: (off + p, 0, 0)) if per_pair else (lambda p: (off, 0, 0)))
    pair = pl.BlockSpec((None, S, LANES), lambda p: (p, 0, 0))
    return pl.pallas_call(
        body, grid=(npairs,),
        in_specs=[SMEM, slab(q0, True), slab(k0, not shared_kv), slab(v0, not shared_kv),
                  pl.BlockSpec((None, 2, 2, BLK, 2 * BLK), lambda p: (bias0 + p, 0, 0, 0, 0)),
                  pair, pair, pair, ANY],
        out_specs=[ANY,
                   pl.BlockSpec((None, 2, BLK, 2 * BLK), lambda p: (p, 0, 0, 0)),
                   pl.BlockSpec((None, 8, LANES), lambda p: (p, 0, 0))],
        out_shape=[jax.ShapeDtypeStruct(dqkv.shape, BF16),
                   jax.ShapeDtypeStruct((npairs, 2, BLK, 2 * BLK), F32),
                   jax.ShapeDtypeStruct((npairs, 8, LANES), F32)],
        scratch_shapes=[pltpu.VMEM((S, LANES), BF16)] * 6 + [pltpu.VMEM((S, LANES), F32)] * 11
        + [pltpu.VMEM((3, S, LANES), BF16), pltpu.SemaphoreType.DMA((3,))],
        input_output_aliases={8: 0},
        compiler_params=_cp("arbitrary"), name=name)(sinks, proj, proj, proj, bias, o, do, lse, dqkv)


KC = 512
NSUB = KC // BLK
QB = 512
QPG = KC // QB


def _split2(x):
    hi = x.astype(BF16)
    lo = (x - hi.astype(F32)).astype(BF16)
    return hi, lo


def _tri_ones(cmp):
    jj = lax.broadcasted_iota(jnp.int32, (2 * BLK, BLK), 0) % BLK
    ss = lax.broadcasted_iota(jnp.int32, (2 * BLK, BLK), 1)
    return jnp.concatenate([cmp(jj, ss).astype(BF16), jnp.ones((2 * BLK, BLK), BF16)], axis=1)


def _sub_sums(x, tri1):
    n = x.shape[0]
    st = jnp.concatenate([x[:, s * BLK:(s + 1) * BLK] for s in range(NSUB)], axis=0)
    hi, lo = _split2(st)
    r = _dot(jnp.concatenate([hi, lo], axis=1), tri1, 1, 0)
    return ([r[s * n:(s + 1) * n, :BLK] for s in range(NSUB)], [r[s * n:(s + 1) * n, BLK:] for s in range(NSUB)])


def _log_sig_pair(z):
    lb = jnp.minimum(z, 0.0) - jnp.log1p(jnp.exp(-jnp.abs(z)))
    return lb, lb - z


QGROUPS = NB // NSUB


def _stick_fwd(proj, *, q0, k0, v0, name):
    def body(q_ref, k_ref, v_ref, o_ref, t_ref, qs, ks, vs):
        qs[...] = (q_ref[...] * SCALE).astype(BF16)
        ks[...] = k_ref[...].astype(BF16)
        vs[...] = v_ref[...].astype(BF16)
        tri1 = _tri_ones(lambda j, s: j > s)
        col = lax.broadcasted_iota(jnp.int32, (QB, KC), 1)
        rowi = lax.broadcasted_iota(jnp.int32, (QB, KC), 0)

        for qg in range(QGROUPS):
            def qblock(ii, carry0, qg=qg):
                t0 = pl.multiple_of((qg * QPG + ii) * QB, QB)
                qb = qs[pl.ds(t0, QB), :]
                accs = [jnp.zeros((QB, HD), F32)] * 2
                runs = [jnp.zeros((QB, BLK), F32)] * 2
                for c in reversed(range(qg + 1)):
                    s0 = c * KC
                    diag = c == qg
                    before = (s0 + col) < (t0 + rowi) if diag else None
                    for hh in range(2):
                        kh = ks[s0:s0 + KC, hh * HD:(hh + 1) * HD]
                        vh = vs[s0:s0 + KC, hh * HD:(hh + 1) * HD]
                        lb, lk = _log_sig_pair(_dot(qb[:, hh * HD:(hh + 1) * HD], kh, 1, 1))
                        if diag:
                            lk = jnp.where(before, lk, 0.0)
                        suf, tot = _sub_sums(lk, tri1)
                        ws, run = [], runs[hh]
                        for s in reversed(range(NSUB)):
                            ws.append(jnp.exp(lb[:, s * BLK:(s + 1) * BLK] + suf[s] + run))
                            run = run + tot[s]
                        w = jnp.concatenate(ws[::-1], axis=1)
                        if diag:
                            w = jnp.where(before, w, 0.0)
                        accs[hh] = accs[hh] + _dot(w.astype(BF16), vh, 1, 0)
                        runs[hh] = run
                o_ref[pl.ds(t0, QB), :] = jnp.concatenate(accs, axis=1)
                t_ref[pl.ds(t0, QB), :] = _lane_halves(runs[0], runs[1])
                return carry0

            lax.fori_loop(0, QPG, qblock, 0)

    slab = lambda off: pl.BlockSpec((None, S, LANES), lambda p: (off + p, 0, 0))
    out = pl.BlockSpec((None, S, LANES), lambda p: (p, 0, 0))
    return pl.pallas_call(
        body, grid=(2,), in_specs=[slab(q0), slab(k0), slab(v0)], out_specs=[out, out],
        out_shape=[jax.ShapeDtypeStruct((2, S, LANES), F32)] * 2,
        scratch_shapes=[pltpu.VMEM((S, LANES), BF16)] * 3,
        compiler_params=_cp("arbitrary"), name=name)(proj, proj, proj)


def _stick_bwd(proj, do, tot, dqkv, *, q0, k0, v0, name):
    def body(q_ref, k_ref, v_ref, do_ref, t_ref, dqkv_in, dqkv_ref, qs, ks, vs, dos, dk_acc, dv_acc, obuf, osem):
        p = pl.program_id(0)
        dq_ref, dk_ref, dv_ref = obuf.at[0], obuf.at[1], obuf.at[2]
        qs[...] = (q_ref[...] * SCALE).astype(BF16)
        ks[...] = k_ref[...].astype(BF16)
        vs[...] = v_ref[...].astype(BF16)
        dos[...] = do_ref[...].astype(BF16)
        dk_acc[...] = jnp.zeros((2, S, HD), F32)
        dv_acc[...] = jnp.zeros((2, S, HD), F32)
        tri_inc = _tri_ones(lambda j, s: j <= s)
        tri_exc = _tri_ones(lambda j, s: j < s)
        col = lax.broadcasted_iota(jnp.int32, (QB, KC), 1)
        rowi = lax.broadcasted_iota(jnp.int32, (QB, KC), 0)

        for qg in range(QGROUPS):
            def qblock(ii, carry0, qg=qg):
                t0 = pl.multiple_of((qg * QPG + ii) * QB, QB)
                qb = qs[pl.ds(t0, QB), :]
                dob = dos[pl.ds(t0, QB), :]
                tb = t_ref[pl.ds(t0, QB), :]
                dqs = [jnp.zeros((QB, HD), F32)] * 2
                pruns = [jnp.zeros((QB, BLK), F32)] * 2
                eruns = [jnp.zeros((QB, BLK), F32)] * 2
                for c in range(qg + 1):
                    s0 = c * KC
                    diag = c == qg
                    before = (s0 + col) < (t0 + rowi) if diag else None
                    for hh in range(2):
                        qh = qb[:, hh * HD:(hh + 1) * HD]
                        doh = dob[:, hh * HD:(hh + 1) * HD]
                        tt = tb[:, 64 * hh:64 * hh + 1]
                        kh = ks[s0:s0 + KC, hh * HD:(hh + 1) * HD]
                        vh = vs[s0:s0 + KC, hh * HD:(hh + 1) * HD]
                        lb, lk = _log_sig_pair(_dot(qh, kh, 1, 1))
                        if diag:
                            lk = jnp.where(before, lk, 0.0)
                        pin, ptot = _sub_sums(lk, tri_inc)
                        ws, prun = [], pruns[hh]
                        for s in range(NSUB):
                            ws.append(jnp.exp(lb[:, s * BLK:(s + 1) * BLK] + (tt - (pin[s] + prun))))
                            prun = prun + ptot[s]
                        w = jnp.concatenate(ws, axis=1)
                        if diag:
                            w = jnp.where(before, w, 0.0)
                        e = w * _dot(doh, vh, 1, 1)
                        pex, etot = _sub_sums(e, tri_exc)
                        cs, erun = [], eruns[hh]
                        for s in range(NSUB):
                            cs.append(pex[s] + erun)
                            erun = erun + etot[s]
                        sig = jnp.exp(lb)
                        dz = e * (1.0 - sig) - jnp.concatenate(cs, axis=1) * sig
                        if diag:
                            dz = jnp.where(before, dz, 0.0)
                        dz = dz.astype(BF16)
                        dqs[hh] = dqs[hh] + _dot(dz, kh, 1, 0)
                        dk_acc[hh, s0:s0 + KC, :] += _dot(dz, qh, 0, 0)
                        dv_acc[hh, s0:s0 + KC, :] += _dot(w.astype(BF16), doh, 0, 0)
                        pruns[hh], eruns[hh] = prun, erun
                dq_ref[pl.ds(t0, QB), :] = (jnp.concatenate(dqs, axis=1) * SCALE).astype(BF16)
                return carry0

            lax.fori_loop(0, QPG, qblock, 0)
        dk_ref[...] = jnp.concatenate([dk_acc[0], dk_acc[1]], axis=1).astype(BF16)
        dv_ref[...] = jnp.concatenate([dv_acc[0], dv_acc[1]], axis=1).astype(BF16)
        puts = [pltpu.make_async_copy(obuf.at[i], dqkv_ref.at[off + p], osem.at[i]) for i, off in enumerate((q0, k0, v0))]
        for cp in puts:
            cp.start()
        for cp in puts:
            cp.wait()

    slab = lambda off: pl.BlockSpec((None, S, LANES), lambda p: (off + p, 0, 0))
    pair = pl.BlockSpec((None, S, LANES), lambda p: (p, 0, 0))
    return pl.pallas_call(
        body, grid=(2,), in_specs=[slab(q0), slab(k0), slab(v0), pair, pair, ANY], out_specs=ANY,
        out_shape=jax.ShapeDtypeStruct(dqkv.shape, BF16),
        scratch_shapes=[pltpu.VMEM((S, LANES), BF16)] * 4 + [pltpu.VMEM((2, S, HD), F32)] * 2
        + [pltpu.VMEM((3, S, LANES), BF16), pltpu.SemaphoreType.DMA((3,))],
        input_output_aliases={5: 0},
        compiler_params=_cp("arbitrary"), name=name)(proj, proj, proj, do, tot, dqkv)


def _cat_slabs(ref):
    return jnp.concatenate([ref[s] for s in range(ref.shape[0])], axis=1)


def _merge_fwd(o_a, o_b, o_c, gates, b_gate, wa, wb, wc, w_out, name):
    tm = MERGE_TILE

    def body(oa_ref, ob_ref, oc_ref, g_ref, bg_ref, wa_ref, wb_ref, wc_ref, wo_ref, mg_ref, mo_ref):
        acc = jnp.zeros((tm, D), F32)
        for i, (o_ref, w_ref) in enumerate(((oa_ref, wa_ref), (ob_ref, wb_ref), (oc_ref, wc_ref))):
            pr = _dot(_cat_slabs(o_ref).astype(BF16), w_ref[...], 1, 0)
            sg = jax.nn.sigmoid(g_ref[:, i * D:(i + 1) * D] + bg_ref[i:i + 1, :])
            acc = acc + sg * pr
        mg = acc.astype(BF16)
        mg_ref[...] = mg
        mo_ref[...] = _dot(mg, wo_ref[...], 1, 0)

    slabs = lambda n: pl.BlockSpec((n, tm, LANES), lambda i: (0, i, 0))
    full = lambda r, c: pl.BlockSpec((r, c), lambda i: (0, 0))
    row = pl.BlockSpec((tm, D), lambda i: (i, 0))
    return pl.pallas_call(
        body, grid=(S // tm,),
        in_specs=[slabs(2), slabs(4), slabs(2), pl.BlockSpec((tm, GATE_COLS), lambda i: (i, 0)), full(3, D),
                  full(256, D), full(512, D), full(256, D), full(D, D)],
        out_specs=[row, row],
        out_shape=[jax.ShapeDtypeStruct((S, D), BF16), jax.ShapeDtypeStruct((S, D), F32)],
        compiler_params=_cp("parallel"), name=name)(o_a, o_b, o_c, gates, b_gate, wa, wb, wc, w_out)


def _merge_bwd(d_mo, o_a, o_b, o_c, gates, b_gate, wa, wb, wc, w_out, name):
    tm = MERGE_TILE
    nsteps = S // tm

    def body(dmo_ref, oa_ref, ob_ref, oc_ref, g_ref, bg_ref, wa_ref, wb_ref, wc_ref, wo_ref,
             doa_ref, dob_ref, doc_ref, dg_ref, dwa_ref, dwb_ref, dwc_ref, dbg_ref, acc_a, acc_b, acc_c):
        @pl.when(pl.program_id(0) == 0)
        def _():
            acc_a[...] = jnp.zeros(acc_a.shape, F32)
            acc_b[...] = jnp.zeros(acc_b.shape, F32)
            acc_c[...] = jnp.zeros(acc_c.shape, F32)
            dbg_ref[...] = jnp.zeros(dbg_ref.shape, F32)

        dmg = _dot(dmo_ref[...], wo_ref[...], 1, 1)
        trip = ((oa_ref, wa_ref, doa_ref, acc_a), (ob_ref, wb_ref, dob_ref, acc_b), (oc_ref, wc_ref, doc_ref, acc_c))
        for i, (o_ref, w_ref, do_ref, dw_ref) in enumerate(trip):
            ob = _cat_slabs(o_ref).astype(BF16)
            pr = _dot(ob, w_ref[...], 1, 0)
            sg = jax.nn.sigmoid(g_ref[:, i * D:(i + 1) * D] + bg_ref[i:i + 1, :])
            dgate = dmg * pr * sg * (1.0 - sg)
            dg_ref[:, i * D:(i + 1) * D] = dgate.astype(BF16)
            dbg_ref[i:i + 1, :] += jnp.sum(dgate, axis=0, keepdims=True)
            dpr = (dmg * sg).astype(BF16)
            do = _dot(dpr, w_ref[...], 1, 1)
            for s in range(do_ref.shape[0]):
                do_ref[s] = do[:, s * LANES:(s + 1) * LANES]
            dw_ref[...] += _dot(ob, dpr, 0, 0)

        @pl.when(pl.program_id(0) == nsteps - 1)
        def _():
            dwa_ref[...] = acc_a[...].astype(BF16)
            dwb_ref[...] = acc_b[...].astype(BF16)
            dwc_ref[...] = acc_c[...].astype(BF16)

    slabs = lambda n: pl.BlockSpec((n, tm, LANES), lambda i: (0, i, 0))
    full = lambda r, c: pl.BlockSpec((r, c), lambda i: (0, 0))
    row = pl.BlockSpec((tm, D), lambda i: (i, 0))
    return pl.pallas_call(
        body, grid=(S // tm,),
        in_specs=[row, slabs(2), slabs(4), slabs(2), pl.BlockSpec((tm, GATE_COLS), lambda i: (i, 0)), full(3, D),
                  full(256, D), full(512, D), full(256, D), full(D, D)],
        out_specs=[slabs(2), slabs(4), slabs(2), pl.BlockSpec((tm, GATE_COLS), lambda i: (i, 0)),
                   full(256, D), full(512, D), full(256, D), full(3, D)],
        out_shape=[jax.ShapeDtypeStruct((2, S, LANES), F32), jax.ShapeDtypeStruct((4, S, LANES), F32),
                   jax.ShapeDtypeStruct((2, S, LANES), F32), jax.ShapeDtypeStruct((S, GATE_COLS), BF16),
                   jax.ShapeDtypeStruct((256, D), BF16), jax.ShapeDtypeStruct((512, D), BF16),
                   jax.ShapeDtypeStruct((256, D), BF16), jax.ShapeDtypeStruct((3, D), F32)],
        scratch_shapes=[pltpu.VMEM((256, D), F32), pltpu.VMEM((512, D), F32), pltpu.VMEM((256, D), F32)],
        compiler_params=_cp("arbitrary"), name=name)(d_mo, o_a, o_b, o_c, gates, b_gate, wa, wb, wc, w_out)


FC = 256
GELU_K = math.sqrt(2.0 / math.pi)
GELU_C = 0.044715


RC = 64
NRC = S // RC


def _down(tail, cur, n):
    row = lax.broadcasted_iota(jnp.int32, tail.shape, 0)
    rolled = pltpu.roll(cur, n, 0)
    first = jnp.where(row < n, pltpu.roll(tail, n, 0), rolled[0:8])
    return jnp.concatenate([first, rolled[8:]], axis=0)


def _up(cur, head, n):
    row = lax.broadcasted_iota(jnp.int32, head.shape, 0)
    rolled = pltpu.roll(cur, RC - n, 0)
    last = jnp.where(row >= 8 - n, pltpu.roll(head, 8 - n, 0), rolled[RC - 8:])
    return jnp.concatenate([rolled[:RC - 8], last], axis=0)


def _conv_chunk(load, j, w_ref, b_ref, half):
    r0 = pl.multiple_of(j * RC, RC)
    cur = load(r0, RC).astype(F32)
    tail = load(pl.multiple_of(jnp.maximum(r0 - 16, 0), 16), 16).astype(F32)[8:16]
    tail = jnp.where(j > 0, tail, 0.0)
    d1 = _down(tail, cur, 1)
    d2 = _down(tail, cur, 2)
    y = w_ref[0:1, half, :] * d2 + w_ref[1:2, half, :] * d1 + w_ref[2:3, half, :] * cur + b_ref[half:half + 1, :]
    return y, cur, d1, d2


def _chunk(j):
    return pl.ds(pl.multiple_of(j * RC, RC), RC)


def _fold8(x):
    return jnp.sum(x.reshape(RC // 8, 8, x.shape[-1]), axis=0)


def _ffn_act(u, conv_w, conv_b, name):
    def body(u_ref, w_ref, b_ref, a_ref, y_ref):
        def step(j, carry):
            yg = _conv_chunk(lambda r, n: u_ref[0, pl.ds(r, n), :], j, w_ref, b_ref, 0)[0]
            yv = _conv_chunk(lambda r, n: u_ref[1, pl.ds(r, n), :], j, w_ref, b_ref, 1)[0]
            th = jnp.tanh(GELU_K * (yg + GELU_C * yg * yg * yg))
            a_ref[_chunk(j), :] = (0.5 * yg * (1.0 + th) * yv).astype(BF16)
            y_ref[0, _chunk(j), :] = yg.astype(BF16)
            y_ref[1, _chunk(j), :] = yv.astype(BF16)
            return carry

        lax.fori_loop(0, NRC, step, 0)

    return pl.pallas_call(
        body, grid=(D_FF // FC,),
        in_specs=[pl.BlockSpec((2, S, FC), lambda j: (0, 0, j)), pl.BlockSpec((3, 2, FC), lambda j: (0, 0, j)),
                  pl.BlockSpec((2, FC), lambda j: (0, j))],
        out_specs=[pl.BlockSpec((S, FC), lambda j: (0, j)), pl.BlockSpec((2, S, FC), lambda j: (0, 0, j))],
        out_shape=[jax.ShapeDtypeStruct((S, D_FF), BF16), jax.ShapeDtypeStruct((2, S, D_FF), BF16)],
        compiler_params=_cp("parallel"), name=name)(u, conv_w, conv_b)


def _ffn_act_bwd(u, y, d_a, conv_w, name):
    def body(u_ref, y_ref, da_ref, w_ref, du_ref, dw_ref, db_ref, dy_s):
        def first(j, acc):
            yg = y_ref[0, _chunk(j), :].astype(F32)
            yv = y_ref[1, _chunk(j), :].astype(F32)
            th = jnp.tanh(GELU_K * (yg + GELU_C * yg * yg * yg))
            gelu = 0.5 * yg * (1.0 + th)
            dgelu = 0.5 * (1.0 + th) + 0.5 * yg * (1.0 - th * th) * GELU_K * (1.0 + 3.0 * GELU_C * yg * yg)
            da = da_ref[_chunk(j), :].astype(F32)
            dyg = da * yv * dgelu
            dyv = da * gelu
            dy_s[0, _chunk(j), :] = dyg
            dy_s[1, _chunk(j), :] = dyv
            return acc[0] + _fold8(dyg), acc[1] + _fold8(dyv)

        zero = jnp.zeros((8, FC), F32)
        accb = lax.fori_loop(0, NRC, first, (zero, zero))
        for half in range(2):
            db_ref[half:half + 1, :] = jnp.sum(accb[half], axis=0, keepdims=True)

        def second(j, acc):
            new = []
            for half in range(2):
                cur = dy_s[half, _chunk(j), :]
                h0 = pl.multiple_of(jnp.minimum((j + 1) * RC, S - 8), 8)
                head = jnp.where(j < NRC - 1, dy_s[half, pl.ds(h0, 8), :], 0.0)
                up1 = _up(cur, head, 1)
                up2 = _up(cur, head, 2)
                du = w_ref[2:3, half, :] * cur + w_ref[1:2, half, :] * up1 + w_ref[0:1, half, :] * up2
                du_ref[half, _chunk(j), :] = du.astype(BF16)
                uu = u_ref[half, _chunk(j), :].astype(F32)
                new += [_fold8(up2 * uu), _fold8(up1 * uu), _fold8(cur * uu)]
            return tuple(a + n for a, n in zip(acc, new))

        accw = lax.fori_loop(0, NRC, second, tuple(zero for _ in range(6)))
        for half in range(2):
            for k in range(3):
                dw_ref[k:k + 1, half, :] = jnp.sum(accw[3 * half + k], axis=0, keepdims=True)

    return pl.pallas_call(
        body, grid=(D_FF // FC,),
        in_specs=[pl.BlockSpec((2, S, FC), lambda j: (0, 0, j)), pl.BlockSpec((2, S, FC), lambda j: (0, 0, j)),
                  pl.BlockSpec((S, FC), lambda j: (0, j)), pl.BlockSpec((3, 2, FC), lambda j: (0, 0, j))],
        out_specs=[pl.BlockSpec((2, S, FC), lambda j: (0, 0, j)), pl.BlockSpec((3, 2, FC), lambda j: (0, 0, j)),
                   pl.BlockSpec((2, FC), lambda j: (0, j))],
        out_shape=[jax.ShapeDtypeStruct((2, S, D_FF), BF16), jax.ShapeDtypeStruct((3, 2, D_FF), F32),
                   jax.ShapeDtypeStruct((2, D_FF), F32)],
        scratch_shapes=[pltpu.VMEM((2, S, FC), F32)],
        compiler_params=_cp("parallel"), name=name)(u, y, d_a, conv_w)


def _layer_fwd(x, h1, w, bias, lname):
    n = lambda s: f"{lname}_{s}"
    w.need("in", h1)
    tn = 768
    proj = _mm(h1, w["w_in"], grid=(1, QKV_COLS // tn, 1),
               a_spec=pl.BlockSpec((S, D), lambda i, j, k: (i, 0)),
               b_spec=pl.BlockSpec((tn, D), lambda i, j, k: (j, 0)),
               out_shape=jax.ShapeDtypeStruct((QKV_SLABS, S, LANES), F32),
               out_spec=pl.BlockSpec((tn // LANES, S, LANES), lambda i, j, k: (j, i, 0)),
               ca=1, cb=1, acc_shape=(S, tn), out_slab=True, name=n("proj_qkv"))
    gates = _mm(h1, w["w_in"], grid=(1, GATE_COLS // tn, 1),
                a_spec=pl.BlockSpec((S, D), lambda i, j, k: (i, 0)),
                b_spec=pl.BlockSpec((tn, D), lambda i, j, k: (j + QKV_COLS // tn, 0)),
                out_shape=jax.ShapeDtypeStruct((S, GATE_COLS), BF16),
                out_spec=pl.BlockSpec((S, tn), lambda i, j, k: (i, j)),
                ca=1, cb=1, acc_shape=(S, tn), name=n("proj_gate"))
    nums, stats = [], []
    for g, (_, d) in enumerate(A_GROUPS):
        nm, st = _band_fwd(proj, bias, w["sinks"], d=d, q0=2 * g, k0=6 + 2 * g, v0=12 + 2 * g, npairs=2, bias0=2 * g,
                           shared_kv=False, name=n(f"attn_a{g}_fwd"))
        nums.append(nm)
        stats.append(st)
    o_a, lse_a = _combine_a(nums, stats, n("attn_a_combine"))
    o_b, lse_b = _band_fwd(proj, bias, w["sinks"], d=1, q0=18, k0=22, v0=23, npairs=4, bias0=6, shared_kv=True,
                           name=n("attn_b_fwd"))
    o_c, tot_c = _stick_fwd(proj, q0=24, k0=26, v0=28, name=n("attn_c_fwd"))
    w.need("mix", tot_c)
    merged, mo = _merge_fwd(o_a, o_b, o_c, gates, w["b_gate"], w["w_br_a"], w["w_br_b"], w["w_br_c"], w["w_out"], n("merge_fwd"))
    w.need("pre_ffn", mo)
    x2, h2 = _postnorm_res(x, mo, w["attn_post_norm"], w["ffn_pre_norm"], n("attn_post"))
    w.need("ffn", h2)
    u = _mm(h2, w["w_up"], grid=(1, 2 * D_FF // 1024, 1),
            a_spec=pl.BlockSpec((S, D), lambda i, j, k: (i, 0)),
            b_spec=pl.BlockSpec((D, 1024), lambda i, j, k: (0, j)),
            out_shape=jax.ShapeDtypeStruct((2, S, D_FF), BF16),
            out_spec=pl.BlockSpec((None, S, 1024), lambda i, j, k: (j // 4, i, j % 4)),
            ca=1, cb=0, acc_shape=(S, 1024), name=n("ffn_up"))
    a, y = _ffn_act(u, w["conv_w"], w["conv_b"], n("ffn_act"))
    fo = _mm_nn(a, w["w_down"], F32, 1024, 1024, 2048, n("ffn_down"))
    saved = dict(x=x, h1=h1, proj=proj, gates=gates, o_a=o_a, lse_a=lse_a, o_b=o_b, lse_b=lse_b, o_c=o_c, tot_c=tot_c,
                 merged=merged, mo=mo, x2=x2, h2=h2, u=u, y=y, a=a, fo=fo)
    return saved


def _layer_bwd(dx3, sv, w, bias, lname, tok=None, on_part=None, d_fo=None, below=None):
    n = lambda s: f"{lname}_{s}"
    g = {}

    def part(group, vec):
        t = on_part(group, g) if on_part is not None else None
        return vec if t is None else vec + t

    if d_fo is None:
        gain = w["ffn_post_norm"] if tok is None else w["ffn_post_norm"] + tok
        d_fo, g["ffn_post_norm"] = _norm_bwd(sv["fo"], gain, [dx3], None, BF16, n("ffn_post_bwd"))
    else:
        d_fo, g["ffn_post_norm"] = d_fo
    d_a = _mm_nt(d_fo, w["w_down"], BF16, S, 1024, 1024, n("ffn_down_bwd_x"))
    g["w_down"] = _mm_tn(sv["a"], d_fo, BF16, 1024, 1024, S, n("ffn_down_bwd_w"))
    d_u, dcw, dcb = _ffn_act_bwd(sv["u"], sv["y"], d_a, w["conv_w"], n("ffn_act_bwd"))
    g["conv_w"] = dcw.reshape(3, 2 * D_FF)
    g["conv_b"] = dcb.reshape(1, 2 * D_FF)
    g["w_up"] = _mm(sv["h2"], d_u, grid=(1, 2 * D_FF // 1024, 1),
                    a_spec=pl.BlockSpec((S, D), lambda i, j, k: (k, 0)),
                    b_spec=pl.BlockSpec((None, S, 1024), lambda i, j, k: (j // 4, k, j % 4)),
                    out_shape=jax.ShapeDtypeStruct((D, 2 * D_FF), BF16),
                    out_spec=pl.BlockSpec((D, 1024), lambda i, j, k: (0, j)),
                    ca=0, cb=0, acc_shape=(D, 1024), name=n("ffn_up_bwd_w"))
    tok_ffn = on_part("ffn", g) if on_part is not None else None
    d_h2 = _mm(d_u, w["w_up"], grid=(S // 1024, 1, 2),
               a_spec=pl.BlockSpec((None, 1024, D_FF), lambda i, j, k: (k, i, 0)),
               b_spec=pl.BlockSpec((D, D_FF), lambda i, j, k: (0, k)),
               out_shape=jax.ShapeDtypeStruct((S, D), F32),
               out_spec=pl.BlockSpec((1024, D), lambda i, j, k: (i, 0)),
               ca=1, cb=1, acc_shape=(1024, D), after=tok_ffn, name=n("ffn_up_bwd_x"))
    dx2, d_mo, g["ffn_pre_norm"], g["attn_post_norm"] = _norm_bwd_chain(
        sv["x2"], w["ffn_pre_norm"], [d_h2], dx3, sv["mo"], w["attn_post_norm"], n("ffn_pre_attn_post_bwd"))
    g["w_out"] = _mm_tn(sv["merged"], d_mo, BF16, 1024, 1024, S, n("out_bwd_w"))
    do_a, do_b, do_c, d_gates, dwa, dwb, dwc, g["b_gate"] = _merge_bwd(
        d_mo, sv["o_a"], sv["o_b"], sv["o_c"], sv["gates"], w["b_gate"], w["w_br_a"], w["w_br_b"], w["w_br_c"],
        w["w_out"], n("merge_bwd"))
    g["w_br_a"], g["w_br_b"], g["w_br_c"] = dwa, dwb, dwc
    sinks = part("mix", w["sinks"])
    proj = sv["proj"]
    dqkv = lax.empty((QKV_SLABS, S, LANES), BF16)
    gbias = []
    for gi, (_, d) in enumerate(A_GROUPS):
        dqkv, gg, _ = _band_bwd(proj, bias, sv["o_a"], do_a, sv["lse_a"], sinks, dqkv, d=d, q0=2 * gi, k0=6 + 2 * gi,
                                v0=12 + 2 * gi, npairs=2, bias0=2 * gi, shared_kv=False, name=n(f"attn_a{gi}_bwd"))
        gbias.append(gg)
    dqkv, ggb, dsink = _band_bwd(proj, bias, sv["o_b"], do_b, sv["lse_b"], sinks, dqkv, d=1, q0=18, k0=22, v0=23,
                                 npairs=4, bias0=6, shared_kv=True, name=n("attn_b_bwd"))
    gbias.append(ggb)
    g["bias_g"] = jnp.concatenate(gbias, axis=0).reshape(N_BIAS_HEADS, BLK, 2 * BLK)
    g["sinks"] = dsink[:, 0, :2].reshape(1, 8)
    dqkv = _stick_bwd(proj, do_c, sv["tot_c"], dqkv, q0=24, k0=26, v0=28, name=n("attn_c_bwd"))
    ts = 6
    tsx = QKV_SLABS
    dw_in = _mm(dqkv, sv["h1"], grid=(QKV_SLABS // ts, 1, 1),
                a_spec=pl.BlockSpec((ts, S, LANES), lambda i, j, k: (i, k, 0)),
                b_spec=pl.BlockSpec((S, D), lambda i, j, k: (k, 0)),
                out_shape=jax.ShapeDtypeStruct((IN_COLS, D), BF16),
                out_spec=pl.BlockSpec((ts * LANES, D), lambda i, j, k: (i, 0)),
                ca=0, cb=0, acc_shape=(ts * LANES, D), a_slab=True, name=n("in_bwd_w_qkv"))
    g["w_in"] = _mm(d_gates, sv["h1"], grid=(GATE_COLS // 768, 1, 1),
                    a_spec=pl.BlockSpec((S, 768), lambda i, j, k: (k, i)),
                    b_spec=pl.BlockSpec((S, D), lambda i, j, k: (k, 0)),
                    out_shape=jax.ShapeDtypeStruct((IN_COLS, D), BF16),
                    out_spec=pl.BlockSpec((768, D), lambda i, j, k: (i + QKV_COLS // 768, 0)),
                    ca=0, cb=0, acc_shape=(768, D), alias_out=dw_in, name=n("in_bwd_w_gate"))
    tok_in = on_part("in", g) if on_part is not None else None
    d_h1a = _mm(dqkv, w["w_in"], grid=(S // 1024, 1, QKV_SLABS // tsx),
                a_spec=pl.BlockSpec((tsx, 1024, LANES), lambda i, j, k: (k, i, 0)),
                b_spec=pl.BlockSpec((tsx * LANES, D), lambda i, j, k: (k, 0)),
                out_shape=jax.ShapeDtypeStruct((S, D), F32),
                out_spec=pl.BlockSpec((1024, D), lambda i, j, k: (i, 0)),
                ca=1, cb=0, acc_shape=(1024, D), a_slab=True, after=tok_in, name=n("in_bwd_x_qkv"))
    d_h1b = _mm(d_gates, w["w_in"], grid=(S // 1024, 1, GATE_COLS // 768),
                a_spec=pl.BlockSpec((1024, 768), lambda i, j, k: (i, k)),
                b_spec=pl.BlockSpec((768, D), lambda i, j, k: (k + QKV_COLS // 768, 0)),
                out_shape=jax.ShapeDtypeStruct((S, D), F32),
                out_spec=pl.BlockSpec((1024, D), lambda i, j, k: (i, 0)),
                ca=1, cb=0, acc_shape=(1024, D), after=tok_in, name=n("in_bwd_x_gate"))
    if below is None:
        dx, g["attn_pre_norm"] = _norm_bwd(sv["x"], w["attn_pre_norm"], [d_h1a, d_h1b], dx2, F32, n("attn_pre_bwd"))
        return dx, g, tok_in, None
    dx, d_fo_below, g["attn_pre_norm"], dg_below = _norm_bwd_chain(
        sv["x"], w["attn_pre_norm"], [d_h1a, d_h1b], dx2, below[0], below[1], n("attn_pre_ffn_post_bwd"))
    return dx, g, tok_in, (d_fo_below, dg_below)


def _local_step(x, target, ws, rel_bias, tok=None, on_grads=None):
    buckets = jnp.asarray(_bucket_tiles())
    bias = _bias_tiles(rel_bias, buckets, "bias_tiles").reshape(N_BIAS_HEADS // 2, 2, 2, BLK, 2 * BLK)
    saved = []
    gain0 = ws[0]["attn_pre_norm"] if tok is None else ws[0]["attn_pre_norm"] + tok
    h1 = _prenorm(x, gain0, "l0_attn_pre")
    for l in range(DEPTH):
        sv = _layer_fwd(x, h1, ws[l], bias, f"l{l}")
        saved.append(sv)
        if l + 1 < DEPTH:
            x, h1 = _postnorm_res(sv["x2"], sv["fo"], ws[l]["ffn_post_norm"], ws[l + 1]["attn_pre_norm"], f"l{l}_ffn_post")
    top = saved[-1]
    dy, loss_tile, d_fo_top, dg_top = _loss_head(top["x2"], top["fo"], ws[-1]["ffn_post_norm"], target, "loss_head")
    grads = [None] * DEPTH
    tok, d_fo = None, (d_fo_top, dg_top)
    for l in reversed(range(DEPTH)):
        on_part = None if on_grads is None else functools.partial(on_grads, l)
        below = (saved[l - 1]["fo"], ws[l - 1]["ffn_post_norm"]) if l > 0 else None
        dy, grads[l], tok, d_fo = _layer_bwd(dy, saved[l], ws[l], bias, f"l{l}", tok, on_part, d_fo, below)
    g_rel = _bias_grad([grads[l]["bias_g"] for l in range(DEPTH)], buckets, "bias_grad")[:, :N_BIAS_HEADS]
    return loss_tile, dy, grads, g_rel


def _coords():
    return lax.axis_index("x"), lax.axis_index("y"), lax.axis_index("c")


def _peer(rel):
    x, y, c = _coords()
    return (1 - x if rel & 4 else x, 1 - y if rel & 2 else y, 1 - c if rel & 1 else c)


def _exchange(srcs, dst_shapes, src_win, dst_win, name, after=None):
    nt = len(srcs)
    extra = [] if after is None else [after]

    def body(*refs):
        src_refs, dst_refs = refs[:nt], refs[nt + len(extra):2 * nt + len(extra)]
        send_sems, recv_sems, local_sems = refs[2 * nt + len(extra):]
        x, y, c = _coords()
        me = 4 * x + 2 * y + c
        locals_ = []
        for t in range(nt):
            cp = pltpu.make_async_copy(src_win(t, src_refs[t], me), dst_win(t, dst_refs[t], me), local_sems.at[t])
            cp.start()
            locals_.append(cp)
        sends = []
        for rel in range(1, NDEV):
            px, py, pc = _peer(rel)
            q = 4 * px + 2 * py + pc
            for t in range(nt):
                cp = pltpu.make_async_remote_copy(
                    src_ref=src_win(t, src_refs[t], q), dst_ref=dst_win(t, dst_refs[t], me),
                    send_sem=send_sems.at[rel - 1, t], recv_sem=recv_sems.at[rel - 1, t],
                    device_id=(px, py, pc), device_id_type=MESH)
                cp.start()
                sends.append(cp)
        for rel in range(1, NDEV):
            px, py, pc = _peer(rel)
            q = 4 * px + 2 * py + pc
            for t in range(nt):
                pltpu.make_async_remote_copy(
                    src_ref=src_win(t, src_refs[t], me), dst_ref=dst_win(t, dst_refs[t], q),
                    send_sem=send_sems.at[rel - 1, t], recv_sem=recv_sems.at[rel - 1, t],
                    device_id=(px, py, pc), device_id_type=MESH).wait_recv()
        for cp in sends:
            cp.wait_send()
        for cp in locals_:
            cp.wait()

    return pl.pallas_call(
        body, in_specs=[ANY] * (nt + len(extra)), out_specs=[ANY] * nt, out_shape=dst_shapes,
        scratch_shapes=[pltpu.SemaphoreType.DMA((NDEV - 1, nt)), pltpu.SemaphoreType.DMA((NDEV - 1, nt)),
                        pltpu.SemaphoreType.DMA((nt,))],
        name=name)(*srcs, *extra)


BIG = (("w_in", 0, 864), ("w_br_a", 1, 128), ("w_br_b", 1, 128), ("w_br_c", 1, 128), ("w_out", 0, 128),
       ("w_up", 1, 1024), ("w_down", 0, 512))


NBIG = len(BIG)
BIG_FULL = {"w_in": (IN_COLS, D), "w_br_a": (256, D), "w_br_b": (512, D), "w_br_c": (256, D), "w_out": (D, D),
            "w_up": (D, 2 * D_FF), "w_down": (D_FF, D)}
SHARD_ROWS = {"w_in": 288, "w_up": 256, "w_down": 256}
LAYER_GROUPS = (("in", (0,)), ("mix", (1, 2, 3, 4)), ("ffn", (5, 6)))

HBM_SPEC = pl.BlockSpec(memory_space=pltpu.HBM)
SEM_SPEC = pl.BlockSpec(memory_space=pltpu.SEMAPHORE)


def _hbm(a):
    return pltpu.with_memory_space_constraint(a, pltpu.HBM)


def _shard_window(t, ref, k):
    nm, ax, ext = BIG[t % NBIG]
    off = pl.multiple_of(k * ext, ext)
    if ax == 0:
        return ref.at[pl.ds(off, ext), :]
    return ref.at[:, pl.ds(off, ext)]


def _whole(t, ref, k):
    return ref


def _slot(t, ref, k):
    return ref.at[k]


def _own_block_spec(t, rows, me_of):
    nm, ax, ext = BIG[t % NBIG]
    r, c = BIG_FULL[nm]
    if ax == 0:
        return pl.BlockSpec((rows, c), lambda i, m: (me_of(m) * (ext // rows) + i, 0))
    return pl.BlockSpec((rows, ext), lambda i, m: (i, me_of(m)))


def _cast_own(t, shards, me_arr, name):
    nm, ax, ext = BIG[t % NBIG]
    layer = t // NBIG
    _, nr, nc = shards.shape
    rows = SHARD_ROWS.get(nm, nr)
    shape = BIG_FULL[nm]

    def body(m_ref, s_ref, o_ref):
        o_ref[...] = s_ref[...].astype(BF16)

    return pl.pallas_call(
        body, grid_spec=pltpu.PrefetchScalarGridSpec(
            num_scalar_prefetch=1, grid=(nr // rows,),
            in_specs=[pl.BlockSpec((None, rows, nc), lambda i, m: (layer, i, 0))],
            out_specs=_own_block_spec(t, rows, lambda m: m[0])),
        out_shape=jax.ShapeDtypeStruct(shape, BF16), compiler_params=_cp("arbitrary"), name=name)(me_arr, shards)


ALL_RELS = tuple(range(1, NDEV))
NEAR_RELS = (1, 2, 4, 6)
FAR_RELS = (2, 4, 6)


def _xchg_start(srcs, lands, groups, src_win, dst_win, after, name, rels=ALL_RELS, tids=None):
    ns = 0 if srcs is None else len(srcs)
    nt, ng = len(lands), len(groups)
    ins = ([] if srcs is None else list(srcs)) + list(lands)

    def body(*refs):
        src_refs, land_refs = refs[:ns], refs[ns:ns + nt]
        sems = refs[ns + nt + 1:ns + nt + 1 + 2 * ng]
        token = refs[-1]
        x, y, c = _coords()
        me = 4 * x + 2 * y + c
        for gi, grp in enumerate(groups):
            for j, t in enumerate(grp):
                tid = t if tids is None else tids[t]
                for ri, rel in enumerate(rels):
                    px, py, pc = _peer(rel)
                    q = 4 * px + 2 * py + pc
                    src = dst_win(tid, land_refs[t], me) if srcs is None else src_win(tid, src_refs[t], q)
                    pltpu.make_async_remote_copy(
                        src_ref=src, dst_ref=dst_win(tid, land_refs[t], me),
                        send_sem=sems[2 * gi].at[ri * len(grp) + j],
                        recv_sem=sems[2 * gi + 1].at[ri * len(grp) + j],
                        device_id=(px, py, pc), device_id_type=MESH).start()
        token[...] = jnp.zeros((8, LANES), F32)

    out_shape = []
    for grp in groups:
        out_shape += [pltpu.SemaphoreType.DMA((len(rels) * len(grp),))] * 2
    out_shape += [pltpu.HBM(a.shape, a.dtype) for a in ins]
    out_shape.append(jax.ShapeDtypeStruct((8, LANES), F32))
    outs = pl.pallas_call(
        body, in_specs=[HBM_SPEC] * len(ins) + [ANY],
        out_specs=[SEM_SPEC] * (2 * ng) + [HBM_SPEC] * len(ins) + [pl.BlockSpec(memory_space=pltpu.VMEM)],
        out_shape=out_shape, input_output_aliases={i: 2 * ng + i for i in range(len(ins))},
        compiler_params=pltpu.CompilerParams(has_side_effects=pltpu.SideEffectType.DATAFLOW_SIDE_EFFECTING),
        name=name)(*[_hbm(a) for a in ins], after)
    sems = [(outs[2 * gi], outs[2 * gi + 1]) for gi in range(ng)]
    thru = list(outs[2 * ng:2 * ng + len(ins)])
    return sems, (None if srcs is None else thru[:ns]), thru[ns:], outs[-1]


def _xchg_wait(sems, srcs, lands, tids, after, src_win, dst_win, name, rels=ALL_RELS):
    ns = 0 if srcs is None else len(srcs)
    n = len(lands)
    send_sem, recv_sem = sems
    ins = ([] if srcs is None else list(srcs)) + list(lands)

    def body(*refs):
        src_refs, land_refs = refs[:ns], refs[ns:ns + n]
        ssem, rsem = refs[ns + n], refs[ns + n + 1]
        x, y, c = _coords()
        me = 4 * x + 2 * y + c
        for j, t in enumerate(tids):
            for ri, rel in enumerate(rels):
                px, py, pc = _peer(rel)
                q = 4 * px + 2 * py + pc
                src = dst_win(t, land_refs[j], me) if srcs is None else src_win(t, src_refs[j], q)
                cp = pltpu.make_async_remote_copy(
                    src_ref=src, dst_ref=dst_win(t, land_refs[j], q),
                    send_sem=ssem.at[ri * n + j], recv_sem=rsem.at[ri * n + j],
                    device_id=(px, py, pc), device_id_type=MESH)
                cp.wait_send()
                cp.wait_recv()

    outs = pl.pallas_call(
        body, in_specs=[HBM_SPEC] * len(ins) + [SEM_SPEC, SEM_SPEC, ANY], out_specs=[HBM_SPEC] * len(ins),
        out_shape=[pltpu.HBM(a.shape, a.dtype) for a in ins],
        input_output_aliases={i: i for i in range(len(ins))},
        compiler_params=pltpu.CompilerParams(has_side_effects=pltpu.SideEffectType.DATAFLOW_SIDE_EFFECTING),
        name=name)(*ins, send_sem, recv_sem, after)
    return (None if srcs is None else list(outs[:ns])), list(outs[ns:])


def _gather_forward(sems_in, lands, groups, tids, after, dst_win, name):
    nt, ng = len(lands), len(groups)

    def body(*refs):
        land_refs = refs[:nt]
        in_sems = refs[nt:nt + 2 * ng]
        out_sems = refs[nt + 2 * ng + 1:nt + 4 * ng + 1]
        token = refs[-1]
        x, y, c = _coords()
        me = 4 * x + 2 * y + c
        sib = (x, y, 1 - c)
        for gi, grp in enumerate(groups):
            n = len(grp)
            for j, pos in enumerate(grp):
                t = tids[pos]
                for ri, rel in enumerate(NEAR_RELS):
                    px, py, pc = _peer(rel)
                    q = 4 * px + 2 * py + pc
                    cp = pltpu.make_async_remote_copy(
                        src_ref=dst_win(t, land_refs[pos], me), dst_ref=dst_win(t, land_refs[pos], q),
                        send_sem=in_sems[2 * gi].at[ri * n + j], recv_sem=in_sems[2 * gi + 1].at[ri * n + j],
                        device_id=(px, py, pc), device_id_type=MESH)
                    cp.wait_send()
                    cp.wait_recv()
            for j, pos in enumerate(grp):
                t = tids[pos]
                for fi, rel in enumerate(FAR_RELS):
                    px, py, pc = _peer(rel)
                    q = 4 * px + 2 * py + pc
                    win = dst_win(t, land_refs[pos], q)
                    pltpu.make_async_remote_copy(
                        src_ref=win, dst_ref=win,
                        send_sem=out_sems[2 * gi].at[fi * n + j], recv_sem=out_sems[2 * gi + 1].at[fi * n + j],
                        device_id=sib, device_id_type=MESH).start()
        token[...] = jnp.zeros((8, LANES), F32)

    out_shape = []
    for grp in groups:
        out_shape += [pltpu.SemaphoreType.DMA((len(FAR_RELS) * len(grp),))] * 2
    out_shape += [pltpu.HBM(a.shape, a.dtype) for a in lands]
    out_shape.append(jax.ShapeDtypeStruct((8, LANES), F32))
    flat_sems = [s for pair in sems_in for s in pair]
    outs = pl.pallas_call(
        body, in_specs=[HBM_SPEC] * nt + [SEM_SPEC] * (2 * ng) + [ANY],
        out_specs=[SEM_SPEC] * (2 * ng) + [HBM_SPEC] * nt + [pl.BlockSpec(memory_space=pltpu.VMEM)],
        out_shape=out_shape, input_output_aliases={i: 2 * ng + i for i in range(nt)},
        compiler_params=pltpu.CompilerParams(has_side_effects=pltpu.SideEffectType.DATAFLOW_SIDE_EFFECTING),
        name=name)(*[_hbm(a) for a in lands], *flat_sems, after)
    sems = [(outs[2 * gi], outs[2 * gi + 1]) for gi in range(ng)]
    return sems, list(outs[2 * ng:2 * ng + nt]), outs[-1]


class _Weights:
    def __init__(self, ready, pending=None):
        self.ready = dict(ready)
        self.pending = dict(pending or {})

    def __getitem__(self, k):
        return self.ready[k]

    def need(self, group, after):
        fn = self.pending.pop(group, None)
        if fn is not None:
            self.ready.update(fn(after))


def _adamw_math(w, g, m, v):
    m2 = ADAM_B1 * m + (1.0 - ADAM_B1) * g
    v2 = ADAM_B2 * v + (1.0 - ADAM_B2) * (g * g)
    m_hat = m2 / (1.0 - ADAM_B1 ** ADAM_STEP)
    v_hat = v2 / (1.0 - ADAM_B2 ** ADAM_STEP)
    delta = -ADAM_LR * (m_hat / (jnp.sqrt(v_hat) + ADAM_EPS) + ADAM_WD * w)
    return delta, m2, v2


def _adamw(t, parts, own, me_arr, w, m, v, layer, prev, rows, name):
    nl, nr, nc = w.shape

    def body(me_ref, p_ref, own_ref, w_ref, m_ref, v_ref, *rest):
        g_ref, d_ref, m2_ref, v2_ref = rest[-4:]
        me = me_ref[0]
        g = None
        for k in range(NDEV):
            term = jnp.where(me == k, own_ref[...], p_ref[k]).astype(F32)
            g = term if g is None else g + term
        delta, m2, v2 = _adamw_math(w_ref[...], g, m_ref[...], v_ref[...])
        g_ref[...] = g
        d_ref[...] = delta
        m2_ref[...] = m2
        v2_ref[...] = v2

    blk = pl.BlockSpec((None, rows, nc), lambda i, mm: (layer, i, 0))
    pblk = pl.BlockSpec((NDEV, rows, nc), lambda i, mm: (0, i, 0))
    extra = [] if prev is None else list(prev)
    return pl.pallas_call(
        body, grid_spec=pltpu.PrefetchScalarGridSpec(
            num_scalar_prefetch=1, grid=(nr // rows,),
            in_specs=[pblk, _own_block_spec(t, rows, lambda mm: mm[0]), blk, blk, blk] + [ANY] * len(extra),
            out_specs=[blk] * 4),
        out_shape=[jax.ShapeDtypeStruct(w.shape, F32)] * 4,
        input_output_aliases={6 + k: k for k in range(len(extra))},
        compiler_params=_cp("arbitrary"), name=name)(me_arr, parts, own, w, m, v, *extra)


def _pack(vecs):
    flat = jnp.concatenate([v.reshape(-1).astype(F32) for v in vecs])
    n = flat.shape[0]
    rows = -(-n // (8 * LANES)) * 8
    return jnp.pad(flat, (0, rows * LANES - n)).reshape(rows, LANES)


ROWPACK = (("rel_bias", 32, 32, (NUM_BUCKETS, N_BIAS_HEADS)), ("sinks", 8, 8, (DEPTH, 8)),
           ("attn_pre_norm", 16, 16, (DEPTH, D)), ("attn_post_norm", 16, 16, (DEPTH, D)),
           ("ffn_pre_norm", 16, 16, (DEPTH, D)), ("ffn_post_norm", 16, 16, (DEPTH, D)),
           ("conv_b", 128, 128, (DEPTH, 2 * D_FF)), ("b_gate", 48, 8, (DEPTH, 3, 128)),
           ("conv_w", 384, 48, (DEPTH, 3, 1024)))
ROWS_OWN = sum(r for _, _, r, _ in ROWPACK)
N_REPL = 7
ROWS_REPL = sum(r for _, _, r, _ in ROWPACK[:N_REPL])
ROWS_SHARD = ROWS_OWN - ROWS_REPL


def _as_rows(a, rows):
    a = a.astype(F32)
    if a.shape[-1] < LANES:
        a = jnp.pad(a.reshape(-1, a.shape[-1]), ((0, 0), (0, LANES - a.shape[-1])))
    a = a.reshape(-1, LANES)
    return jnp.pad(a, ((0, rows - a.shape[0]), (0, 0)))


def _rowpack(arrs, entries=ROWPACK):
    return jnp.concatenate([_as_rows(arrs[nm], ro) for nm, _, ro, _ in entries], axis=0)


def _shard_rows(g):
    bg = jnp.transpose(g["b_gate"].astype(F32).reshape(DEPTH * 3, NDEV, LANES), (1, 0, 2))
    bg = jnp.pad(bg, ((0, 0), (0, 8 - DEPTH * 3), (0, 0)))
    cw = jnp.transpose(g["conv_w"].astype(F32).reshape(DEPTH * 3, NDEV, 8, LANES), (1, 0, 2, 3))
    return jnp.concatenate([bg, cw.reshape(NDEV, DEPTH * 3 * 8, LANES)], axis=1)


def _small_update(parts_repl, parts_shard, w, m, v, name):
    nsm = len(ROWPACK)

    def body(pr_ref, ps_ref, w_ref, m_ref, v_ref, *rest):
        outs = rest[:4 * nsm]
        loss_ref = rest[4 * nsm]
        g_s, d_s, m_s, v_s = rest[4 * nsm + 1:]
        gr, gs = pr_ref[0], ps_ref[0]
        for k in range(1, NDEV):
            gr = gr + pr_ref[k]
            gs = gs + ps_ref[k]
        g_s[0:ROWS_REPL, :] = gr[:ROWS_REPL]
        g_s[ROWS_REPL:ROWS_OWN, :] = gs
        loss_ref[...] = gr[ROWS_REPL:]
        delta, m2, v2 = _adamw_math(w_ref[...], g_s[...], m_ref[...], v_ref[...])
        d_s[...] = delta
        m_s[...] = m2
        v_s[...] = v2
        for kind, src in enumerate((g_s, d_s, m_s, v_s)):
            oo = 0
            for idx, (nm, rf, ro, shp) in enumerate(ROWPACK):
                o_ref = outs[kind * nsm + idx]
                if nm in ("rel_bias", "sinks"):
                    o_ref[...] = src[oo:oo + shp[0], 0:shp[1]]
                elif nm == "b_gate":
                    for l in range(DEPTH):
                        o_ref[l] = src[oo + 3 * l:oo + 3 * l + 3, :]
                elif nm == "conv_w":
                    for l in range(DEPTH):
                        for k in range(8):
                            o_ref[l, :, k * LANES:(k + 1) * LANES] = src[pl.ds(oo + 24 * l + k, 3, stride=8), :]
                else:
                    per = shp[1] // LANES
                    for k in range(per):
                        o_ref[:, k * LANES:(k + 1) * LANES] = src[pl.ds(oo + k, DEPTH, stride=per), :]
                oo += ro

    vm = pl.BlockSpec(memory_space=pltpu.VMEM)
    shapes = [jax.ShapeDtypeStruct(shp, F32) for _ in range(4) for _, _, _, shp in ROWPACK]
    shapes.append(jax.ShapeDtypeStruct((8, LANES), F32))
    outs = pl.pallas_call(
        body, in_specs=[vm] * 5, out_specs=[vm] * (4 * nsm + 1), out_shape=shapes,
        scratch_shapes=[pltpu.VMEM((ROWS_OWN, LANES), F32)] * 4,
        name=name)(parts_repl, parts_shard, w, m, v)
    names = [nm for nm, _, _, _ in ROWPACK]
    return [dict(zip(names, outs[kind * nsm:(kind + 1) * nsm])) for kind in range(4)] + [outs[-1]]


def kernel(x, rel_bias, attn_pre_norm, w_in, b_gate, sinks, w_br_a, w_br_b, w_br_c, w_out, attn_post_norm, ffn_pre_norm, w_up, conv_w, conv_b, w_down, ffn_post_norm, loss_target, m_rel_bias, m_attn_pre_norm, m_w_in, m_b_gate, m_sinks, m_w_br_a, m_w_br_b, m_w_br_c, m_w_out, m_attn_post_norm, m_ffn_pre_norm, m_w_up, m_conv_w, m_conv_b, m_w_down, m_ffn_post_norm, v_rel_bias, v_attn_pre_norm, v_w_in, v_b_gate, v_sinks, v_w_br_a, v_w_br_b, v_w_br_c, v_w_out, v_attn_post_norm, v_ffn_pre_norm, v_w_up, v_conv_w, v_conv_b, v_w_down, v_ffn_post_norm):
    P = dict(rel_bias=rel_bias, attn_pre_norm=attn_pre_norm, w_in=w_in, b_gate=b_gate, sinks=sinks, w_br_a=w_br_a,
             w_br_b=w_br_b, w_br_c=w_br_c, w_out=w_out, attn_post_norm=attn_post_norm, ffn_pre_norm=ffn_pre_norm,
             w_up=w_up, conv_w=conv_w, conv_b=conv_b, w_down=w_down, ffn_post_norm=ffn_post_norm)
    M = dict(rel_bias=m_rel_bias, attn_pre_norm=m_attn_pre_norm, w_in=m_w_in, b_gate=m_b_gate, sinks=m_sinks,
             w_br_a=m_w_br_a, w_br_b=m_w_br_b, w_br_c=m_w_br_c, w_out=m_w_out, attn_post_norm=m_attn_post_norm,
             ffn_pre_norm=m_ffn_pre_norm, w_up=m_w_up, conv_w=m_conv_w, conv_b=m_conv_b, w_down=m_w_down,
             ffn_post_norm=m_ffn_post_norm)
    V = dict(rel_bias=v_rel_bias, attn_pre_norm=v_attn_pre_norm, w_in=v_w_in, b_gate=v_b_gate, sinks=v_sinks,
             w_br_a=v_w_br_a, w_br_b=v_w_br_b, w_br_c=v_w_br_c, w_out=v_w_out, attn_post_norm=v_attn_post_norm,
             ffn_pre_norm=v_ffn_pre_norm, w_up=v_w_up, conv_w=v_conv_w, conv_b=v_conv_b, w_down=v_w_down,
             ffn_post_norm=v_ffn_post_norm)
    tr = lambda a: jnp.swapaxes(a, 1, 2)
    PB = {nm: (tr(P[nm]) if nm == "w_in" else P[nm]) for nm, _, _ in BIG}
    MB = {nm: (tr(M[nm]) if nm == "w_in" else M[nm]) for nm, _, _ in BIG}
    VB = {nm: (tr(V[nm]) if nm == "w_in" else V[nm]) for nm, _, _ in BIG}
    xi, yi, ci = _coords()
    me = 4 * xi + 2 * yi + ci

    me_arr = me.astype(jnp.int32).reshape(1)

    small_w = _pack([b_gate.reshape(-1), conv_w.reshape(-1)])
    (small_w_all,) = _exchange([small_w], [jax.ShapeDtypeStruct((NDEV,) + small_w.shape, F32)],
                               _whole, _slot, "gather_small_weights")
    nbg, ncw = DEPTH * 3 * 128, DEPTH * 3 * 1024
    flat_all = small_w_all.reshape(NDEV, -1)
    b_gate_full = jnp.transpose(flat_all[:, :nbg].reshape(NDEV, DEPTH, 3, 128), (1, 2, 0, 3)).reshape(DEPTH, 3, D)
    conv_w_full = jnp.transpose(flat_all[:, nbg:nbg + ncw].reshape(NDEV, DEPTH, 3, 1024), (1, 2, 0, 3)).reshape(DEPTH, 3, 2 * D_FF)

    groups = [tuple(l * NBIG + t for t in tids) for l in range(DEPTH) for _, tids in LAYER_GROUPS]
    cast = lambda i, m=me_arr: _cast_own(i, PB[BIG[i % NBIG][0]], m, f"gather_own_l{i // NBIG}_{BIG[i % NBIG][0]}")
    first = list(groups[0])
    rest = [i for grp in groups[1:] for i in grp]
    sems0, _, lands0, tok_first = _xchg_start(None, [cast(i) for i in first], [tuple(range(len(first)))], None,
                                              _shard_window, small_w_all, "gather_start_first", rels=NEAR_RELS, tids=first)
    where_rest = {tid: k for k, tid in enumerate(rest)}
    me_rest = me_arr + tok_first[0, 0:1].astype(jnp.int32)
    sems1, _, lands1, g_tok = _xchg_start(None, [cast(i, me_rest) for i in rest],
                                          [tuple(where_rest[i] for i in grp) for grp in groups[1:]], None,
                                          _shard_window, lands0[0], "gather_start_rest", rels=NEAR_RELS, tids=rest)
    g_sems = list(sems0) + list(sems1)
    tok0 = g_tok[0:1, 0:1]
    lands_now = [None] * (DEPTH * NBIG)
    for i, a in zip(first + rest, list(lands0) + list(lands1)):
        lands_now[i] = a
    fwd_sems = {}
    fwd_plan = {0: (0,), 1: (1,), 2: (2,), 3: (3, 4), 5: (5,)}

    def forward(gis, after):
        gis = tuple(g2 for g2 in gis if g2 not in fwd_sems)
        if not gis:
            return after
        flat = [i for g2 in gis for i in groups[g2]]
        where = {tid: k for k, tid in enumerate(flat)}
        fs, new_lands, ftok = _gather_forward(
            [g_sems[g2] for g2 in gis], [lands_now[i] for i in flat],
            [[where[i] for i in groups[g2]] for g2 in gis], flat, after, _shard_window, f"gather_forward_{gis[0]}")
        for g2, s in zip(gis, fs):
            fwd_sems[g2] = s
        for i, a in zip(flat, new_lands):
            lands_now[i] = a
        return ftok

    def prefetch(gis):
        def run(after):
            forward(gis, after)
            return {}
        return run

    def gather_waiter(gi, l, gname, tids):
        def wait(after):
            after = forward(fwd_plan.get(gi, ()), after)
            ids = [l * NBIG + t for t in tids]
            _, got = _xchg_wait(fwd_sems[gi], None, [lands_now[i] for i in ids], ids, after,
                                None, _shard_window, f"gather_wait_l{l}_{gname}", rels=FAR_RELS)
            out = {}
            for t, arr in zip(tids, got):
                nm = BIG[t][0]
                out[nm] = arr
            return out
        return wait

    pending = [{gname: gather_waiter(l * len(LAYER_GROUPS) + k, l, gname, tids)
                for k, (gname, tids) in enumerate(LAYER_GROUPS)} for l in range(DEPTH)]
    for l in range(DEPTH):
        pending[l]["pre_ffn"] = prefetch((l * len(LAYER_GROUPS) + 2,))
    ws = []
    for l in range(DEPTH):
        ws.append(_Weights(dict(
            b_gate=b_gate_full[l], conv_w=conv_w_full[l].reshape(3, 2, D_FF), conv_b=conv_b[l].reshape(2, D_FF),
            sinks=sinks[l].reshape(1, 8),
            attn_pre_norm=attn_pre_norm[l].reshape(1, D), attn_post_norm=attn_post_norm[l].reshape(1, D),
            ffn_pre_norm=ffn_pre_norm[l].reshape(1, D), ffn_post_norm=ffn_post_norm[l].reshape(1, D)), pending[l]))

    rs = {}

    group_tids = dict(LAYER_GROUPS)

    def start_scatter(l, gname, grads_l):
        tids = group_tids[gname]
        blocks, lands_rs = [], []
        for t in tids:
            nm, ax, ext = BIG[t]
            gfull = grads_l[nm].astype(BF16)
            shp = (NDEV, ext, gfull.shape[1]) if ax == 0 else (NDEV, gfull.shape[0], ext)
            blocks.append(gfull)
            lands_rs.append(lax.empty(shp, BF16))
        local = list(range(len(tids)))
        win = lambda j, ref, k: _shard_window(tids[j], ref, k)
        sems, s_thru, l_thru, tok = _xchg_start(blocks, lands_rs, [tuple(local)], win, _slot, me_arr,
                                                f"scatter_start_l{l}_{gname}")
        rs[(l, gname)] = (sems[0], s_thru, l_thru, win, local)
        return tok[0:1, 0:1]

    loss_tile, grad_x, grads, g_rel = _local_step(x[0], loss_target[0], ws, rel_bias, tok0, start_scatter)

    stack = lambda nm: jnp.stack([grads[l][nm] for l in range(DEPTH)], axis=0)
    small_g = {nm: (g_rel if nm == "rel_bias" else stack(nm)) for nm, _, _, _ in ROWPACK}
    small_repl = jnp.concatenate([_rowpack(small_g, ROWPACK[:N_REPL]), loss_tile], axis=0)
    small_shard = _shard_rows(small_g)

    out_g, out_d, out_m, out_v = {}, {}, {}, {}
    prev = {nm: None for nm, _, _ in BIG}
    todo = [(l, gname) for l in reversed(range(DEPTH)) for gname in ("ffn", "mix", "in")]
    after, small_parts = grad_x, None
    for l, gname in todo:
        if (l, gname) == todo[-1]:
            small_parts = _exchange(
                [small_repl, small_shard],
                [jax.ShapeDtypeStruct((NDEV, ROWS_REPL + 8, LANES), F32), jax.ShapeDtypeStruct((NDEV, ROWS_SHARD, LANES), F32)],
                lambda t, ref, q: ref if t == 0 else ref.at[q], _slot, "exchange_small_grads", after=after)
            after = small_parts[0]
        sems, s_thru, l_thru, win, local = rs[(l, gname)]
        owns, parts = _xchg_wait(sems, s_thru, l_thru, local, after, win, _slot, f"scatter_wait_l{l}_{gname}")
        for t, own, prt in zip(group_tids[gname], owns, parts):
            nm = BIG[t][0]
            rows = SHARD_ROWS.get(nm, PB[nm].shape[1])
            prev[nm] = _adamw(t, prt, own, me_arr, PB[nm], MB[nm], VB[nm], l, prev[nm], rows, f"adamw_{nm}_l{l}")
            after = prev[nm][1]
    for nm, _, _ in BIG:
        out_g[nm], out_d[nm], out_m[nm], out_v[nm] = [tr(a) if nm == "w_in" else a for a in prev[nm]]
    sm_g, sm_d, sm_m, sm_v, loss_all = _small_update(small_parts[0], small_parts[1], _rowpack(P), _rowpack(M),
                                                     _rowpack(V), "small_update")
    loss = loss_all[0, 0]
    for dst, src in ((out_g, sm_g), (out_d, sm_d), (out_m, sm_m), (out_v, sm_v)):
        dst.update(src)

    order = ["rel_bias", "attn_pre_norm", "w_in", "b_gate", "sinks", "w_br_a", "w_br_b", "w_br_c", "w_out",
             "attn_post_norm", "ffn_pre_norm", "w_up", "conv_w", "conv_b", "w_down", "ffn_post_norm"]
    return (loss, grad_x[None], *[out_g[k] for k in order], *[out_d[k] for k in order],
            *[out_m[k] for k in order], *[out_v[k] for k in order])
```
